```python
import jax, jax.numpy as jnp
from jax import lax
import numpy as np

D_MODEL = 1024
BATCH = 8
SEQ = 8192
DEPTH = 2

CHUNK = 64
N_BRANCH = 4
BRANCH = D_MODEL // N_BRANCH
D_MIX = N_BRANCH * BRANCH
EPS = 1e-6

CONV_A_WIDTH = 3
CONV_A_GROUPS = 4
GLA_HEADS = 4
GLA_DK = BRANCH // 2
GLA_DV = BRANCH
GLA_GATE_RANK = 16
GLA_TAU = 16.0
POOL_WINDOWS = (2, 4, 8, 16)
POOL_GROUP = BRANCH // len(POOL_WINDOWS)
SSD_HEAD_DIM = 64
SSD_HEADS = BRANCH // SSD_HEAD_DIM
SSD_GROUPS = 2
SSD_STATE = 128
SSD_CONV = 4
SSD_XBC = BRANCH + 2 * SSD_GROUPS * SSD_STATE

PROJ_SPLITS = (BRANCH, BRANCH, BRANCH, BRANCH,
               GLA_DK, GLA_DK, GLA_DV, GLA_GATE_RANK, GLA_DV,
               BRANCH, BRANCH,
               BRANCH, SSD_XBC, SSD_HEADS)
D_PROJ = sum(PROJ_SPLITS)

kernel_name = 'hybrid_parallel_conv_gla_pool_ssd'


def _rmsnorm(x, w):
    xf = x.astype(jnp.float32)
    y = xf * lax.rsqrt(jnp.mean(xf * xf, axis=-1, keepdims=True) + EPS)
    return (y * w.astype(jnp.float32)).astype(x.dtype)


def _causal_depthwise_conv(u, w):
    k = w.shape[0]
    return lax.conv_general_dilated(
        u, w[:, None, :].astype(u.dtype), window_strides=(1,), padding=[(k - 1, 0)],
        dimension_numbers=('NWC', 'WIO', 'NWC'), feature_group_count=u.shape[-1])


def _chunk_states(decay, update):
    def step(s, inp):
        a_c, u_c = inp
        s = a_c * s + u_c
        return s, s
    init = jnp.zeros_like(update[:, 0])
    _, states = lax.scan(step, init, (jnp.moveaxis(decay, 1, 0), jnp.moveaxis(update, 1, 0)))
    return jnp.moveaxis(states, 0, 1)


def _conv_mixer(h, bg, cg, z, conv_w):
    y = bg * _causal_depthwise_conv(cg * h, conv_w)
    return y * jax.nn.silu(z)


def _gla_mixer(q, k, v, g_lr, z, gate_w, gate_b, norm_w):
    b, l, _ = q.shape
    nc = l // CHUNK
    dk = GLA_DK // GLA_HEADS
    dv = GLA_DV // GLA_HEADS
    f32 = jnp.float32
    qc = q.astype(f32).reshape(b, nc, CHUNK, GLA_HEADS, dk) * (dk ** -0.5)
    kc = k.astype(f32).reshape(b, nc, CHUNK, GLA_HEADS, dk)
    vc = v.astype(f32).reshape(b, nc, CHUNK, GLA_HEADS, dv)
    log_a = jax.nn.log_sigmoid((g_lr @ gate_w + gate_b).astype(f32)) / GLA_TAU
    log_a = log_a.reshape(b, nc, CHUNK, GLA_HEADS, dk)
    cum = jnp.cumsum(log_a, axis=2)
    total = cum[:, :, -1]
    k_dec = kc * jnp.exp(total[:, :, None] - cum)
    upd = jnp.einsum('bnchk,bnchv->bnhkv', k_dec, vc)
    states = _chunk_states(jnp.exp(total)[..., None], upd)
    o = jnp.einsum('bnchk,bnhkv->bnchv', qc, states)
    o = _rmsnorm(o, norm_w).reshape(b, l, GLA_DV)
    return (o * jax.nn.silu(z.astype(f32))).astype(q.dtype)


def _pool_mixer(u, z, pool_w, pool_scale):
    b, l, _ = u.shape
    f32 = jnp.float32
    uf = u.astype(f32)
    cs = jnp.pad(jnp.cumsum(uf, axis=1), ((0, 0), (1, 0), (0, 0)))
    pos = jnp.arange(l)
    outs = []
    for gi, win in enumerate(POOL_WINDOWS):
        sl = slice(gi * POOL_GROUP, (gi + 1) * POOL_GROUP)
        cg = cs[:, :, sl]
        prev = jnp.pad(cg, ((0, 0), (win - 1, 0), (0, 0)))[:, :l]
        cnt = jnp.minimum(pos + 1, win).astype(f32)[None, :, None]
        outs.append((cg[:, 1:] - prev) / cnt - uf[:, :, sl])
    pooled = jnp.stack(outs, axis=2)
    mixed = jnp.einsum('blgc,gcd->blgd', pooled, pool_w.astype(f32)).reshape(b, l, BRANCH)
    return (pool_scale.astype(f32) * mixed * jax.nn.silu(z.astype(f32))).astype(u.dtype)


def _ssd_mixer(xbc, dt, z, conv_w, conv_b, dt_bias, a_log, d_skip, norm_w):
    b, l, _ = xbc.shape
    nc = l // CHUNK
    hpg = SSD_HEADS // SSD_GROUPS
    f32 = jnp.float32
    xbc = jax.nn.silu(_causal_depthwise_conv(xbc, conv_w) + conv_b).astype(f32)
    xs, bm, cm = jnp.split(xbc, [BRANCH, BRANCH + SSD_GROUPS * SSD_STATE], axis=-1)
    xs = xs.reshape(b, nc, CHUNK, SSD_GROUPS, hpg, SSD_HEAD_DIM)
    bm = bm.reshape(b, nc, CHUNK, SSD_GROUPS, SSD_STATE)
    cm = cm.reshape(b, nc, CHUNK, SSD_GROUPS, SSD_STATE)
    dt = jax.nn.softplus(dt.astype(f32) + dt_bias.astype(f32)).reshape(b, nc, CHUNK, SSD_GROUPS, hpg)
    a = -jnp.exp(a_log.astype(f32)).reshape(SSD_GROUPS, hpg)
    cum = jnp.cumsum(dt * a, axis=2)
    total = cum[:, :, -1]
    w = jnp.exp(total[:, :, None] - cum) * dt
    upd = jnp.einsum('bncgs,bncgh,bncghp->bnghps', bm, w, xs)
    states = _chunk_states(jnp.exp(total)[..., None, None], upd)
    y = jnp.einsum('bncgs,bnghps->bncghp', cm, states)
    y = y + d_skip.astype(f32).reshape(SSD_GROUPS, hpg)[:, :, None] * xs
    y = y.reshape(b, l, BRANCH)
    y = _rmsnorm(y * jax.nn.silu(z.astype(f32)), norm_w)
    return y.astype(z.dtype)


def _fwd_setup_inputs(seed: int = 0) -> dict:
    key = jax.random.key(seed)
    ks = jax.random.split(key, 20)
    f32 = jnp.float32
    nrm = lambda k, shape, s: jax.random.normal(k, shape, f32) * s
    x = jax.random.normal(ks[0], (BATCH, SEQ, D_MODEL), f32)
    norm_w = 1.0 + nrm(ks[1], (DEPTH, D_MODEL), 0.02)
    w_in = nrm(ks[2], (DEPTH, D_MODEL, D_PROJ), D_MODEL ** -0.5)
    conv_a_w = nrm(ks[3], (DEPTH, CONV_A_WIDTH, BRANCH), CONV_A_WIDTH ** -0.5)
    gla_gate_w = nrm(ks[4], (DEPTH, GLA_GATE_RANK, GLA_DK), GLA_GATE_RANK ** -0.5)
    gla_gate_b = nrm(ks[5], (DEPTH, GLA_DK), 0.1)
    gla_norm_w = 1.0 + nrm(ks[6], (DEPTH, GLA_DV // GLA_HEADS), 0.02)
    pool_w = nrm(ks[7], (DEPTH, len(POOL_WINDOWS), POOL_GROUP, POOL_GROUP), POOL_GROUP ** -0.5)
    pool_scale = 1.0 + nrm(ks[8], (DEPTH, BRANCH), 0.1)
    ssd_conv_w = nrm(ks[9], (DEPTH, SSD_CONV, SSD_XBC), SSD_CONV ** -0.5)
    ssd_conv_b = nrm(ks[10], (DEPTH, SSD_XBC), 0.01)
    dt0 = jnp.exp(jax.random.uniform(ks[11], (DEPTH, SSD_HEADS), f32, np.log(1e-3), np.log(1e-1)))
    ssd_dt_bias = dt0 + jnp.log(-jnp.expm1(-dt0))
    ssd_a_log = jnp.log(jax.random.uniform(ks[12], (DEPTH, SSD_HEADS), f32, 1.0, 16.0))
    ssd_d = 1.0 + nrm(ks[13], (DEPTH, SSD_HEADS), 0.1)
    ssd_norm_w = 1.0 + nrm(ks[14], (DEPTH, BRANCH), 0.02)
    w_out = nrm(ks[15], (DEPTH, D_MIX, D_MODEL), D_MIX ** -0.5)
    final_norm_w = 1.0 + nrm(ks[16], (D_MODEL,), 0.02)
    return {'x': x, 'norm_w': norm_w, 'w_in': w_in, 'conv_a_w': conv_a_w,
            'gla_gate_w': gla_gate_w, 'gla_gate_b': gla_gate_b, 'gla_norm_w': gla_norm_w,
            'pool_w': pool_w, 'pool_scale': pool_scale,
            'ssd_conv_w': ssd_conv_w, 'ssd_conv_b': ssd_conv_b, 'ssd_dt_bias': ssd_dt_bias,
            'ssd_a_log': ssd_a_log, 'ssd_d': ssd_d, 'ssd_norm_w': ssd_norm_w,
            'w_out': w_out, 'final_norm_w': final_norm_w}


def _fwd_reference(x, norm_w, w_in, conv_a_w, gla_gate_w, gla_gate_b, gla_norm_w, pool_w, pool_scale,
              ssd_conv_w, ssd_conv_b, ssd_dt_bias, ssd_a_log, ssd_d, ssd_norm_w, w_out, final_norm_w):
    split_idx = [int(i) for i in np.cumsum(PROJ_SPLITS)[:-1]]
    for layer in range(DEPTH):
        h = _rmsnorm(x, norm_w[layer])
        proj = h @ w_in[layer]
        (a_h, a_b, a_c, a_z, g_q, g_k, g_v, g_lr, g_z, p_u, p_z, s_z, s_xbc, s_dt) = jnp.split(proj, split_idx, axis=-1)
        y_a = _conv_mixer(a_h, a_b, a_c, a_z, conv_a_w[layer])
        y_b = _gla_mixer(g_q, g_k, g_v, g_lr, g_z, gla_gate_w[layer], gla_gate_b[layer], gla_norm_w[layer])
        y_c = _pool_mixer(p_u, p_z, pool_w[layer], pool_scale[layer])
        y_d = _ssd_mixer(s_xbc, s_dt, s_z, ssd_conv_w[layer], ssd_conv_b[layer], ssd_dt_bias[layer],
                         ssd_a_log[layer], ssd_d[layer], ssd_norm_w[layer])
        mix = jnp.concatenate([y_a, y_b, y_c, y_d], axis=-1)
        x = x + mix @ w_out[layer]
    return _rmsnorm(x, final_norm_w)


import jax as _jax
import jax.numpy as _jnp

TWIN_FORMAT = 'train_step'
FWD_PARAMS = ['x', 'norm_w', 'w_in', 'conv_a_w', 'gla_gate_w', 'gla_gate_b', 'gla_norm_w', 'pool_w', 'pool_scale', 'ssd_conv_w', 'ssd_conv_b', 'ssd_dt_bias', 'ssd_a_log', 'ssd_d', 'ssd_norm_w', 'w_out', 'final_norm_w']
TWIN_WEIGHTS = ['norm_w', 'w_in', 'conv_a_w', 'gla_gate_w', 'gla_gate_b', 'gla_norm_w', 'pool_w', 'pool_scale', 'ssd_conv_w', 'ssd_conv_b', 'ssd_dt_bias', 'ssd_a_log', 'ssd_d', 'ssd_norm_w', 'w_out', 'final_norm_w']
TWIN_DIFF_INPUT = 'x'
TWIN_INPUTS = ['x', 'norm_w', 'w_in', 'conv_a_w', 'gla_gate_w', 'gla_gate_b', 'gla_norm_w', 'pool_w', 'pool_scale', 'ssd_conv_w', 'ssd_conv_b', 'ssd_dt_bias', 'ssd_a_log', 'ssd_d', 'ssd_norm_w', 'w_out', 'final_norm_w', 'loss_target', 'm_norm_w', 'm_w_in', 'm_conv_a_w', 'm_gla_gate_w', 'm_gla_gate_b', 'm_gla_norm_w', 'm_pool_w', 'm_pool_scale', 'm_ssd_conv_w', 'm_ssd_conv_b', 'm_ssd_dt_bias', 'm_ssd_a_log', 'm_ssd_d', 'm_ssd_norm_w', 'm_w_out', 'm_final_norm_w', 'v_norm_w', 'v_w_in', 'v_conv_a_w', 'v_gla_gate_w', 'v_gla_gate_b', 'v_gla_norm_w', 'v_pool_w', 'v_pool_scale', 'v_ssd_conv_w', 'v_ssd_conv_b', 'v_ssd_dt_bias', 'v_ssd_a_log', 'v_ssd_d', 'v_ssd_norm_w', 'v_w_out', 'v_final_norm_w']
TWIN_OUTPUTS = ['loss', 'grad_x', 'grad_norm_w', 'grad_w_in', 'grad_conv_a_w', 'grad_gla_gate_w', 'grad_gla_gate_b', 'grad_gla_norm_w', 'grad_pool_w', 'grad_pool_scale', 'grad_ssd_conv_w', 'grad_ssd_conv_b', 'grad_ssd_dt_bias', 'grad_ssd_a_log', 'grad_ssd_d', 'grad_ssd_norm_w', 'grad_w_out', 'grad_final_norm_w', 'delta_norm_w', 'delta_w_in', 'delta_conv_a_w', 'delta_gla_gate_w', 'delta_gla_gate_b', 'delta_gla_norm_w', 'delta_pool_w', 'delta_pool_scale', 'delta_ssd_conv_w', 'delta_ssd_conv_b', 'delta_ssd_dt_bias', 'delta_ssd_a_log', 'delta_ssd_d', 'delta_ssd_norm_w', 'delta_w_out', 'delta_final_norm_w', 'new_m_norm_w', 'new_m_w_in', 'new_m_conv_a_w', 'new_m_gla_gate_w', 'new_m_gla_gate_b', 'new_m_gla_norm_w', 'new_m_pool_w', 'new_m_pool_scale', 'new_m_ssd_conv_w', 'new_m_ssd_conv_b', 'new_m_ssd_dt_bias', 'new_m_ssd_a_log', 'new_m_ssd_d', 'new_m_ssd_norm_w', 'new_m_w_out', 'new_m_final_norm_w', 'new_v_norm_w', 'new_v_w_in', 'new_v_conv_a_w', 'new_v_gla_gate_w', 'new_v_gla_gate_b', 'new_v_gla_norm_w', 'new_v_pool_w', 'new_v_pool_scale', 'new_v_ssd_conv_w', 'new_v_ssd_conv_b', 'new_v_ssd_dt_bias', 'new_v_ssd_a_log', 'new_v_ssd_d', 'new_v_ssd_norm_w', 'new_v_w_out', 'new_v_final_norm_w']
TWIN_LEAF_KINDS = {'loss': 'loss', 'grad_x': 'grad_x', 'grad_norm_w': 'grad_w', 'grad_w_in': 'grad_w', 'grad_conv_a_w': 'grad_w', 'grad_gla_gate_w': 'grad_w', 'grad_gla_gate_b': 'grad_w', 'grad_gla_norm_w': 'grad_w', 'grad_pool_w': 'grad_w', 'grad_pool_scale': 'grad_w', 'grad_ssd_conv_w': 'grad_w', 'grad_ssd_conv_b': 'grad_w', 'grad_ssd_dt_bias': 'grad_w', 'grad_ssd_a_log': 'grad_w', 'grad_ssd_d': 'grad_w', 'grad_ssd_norm_w': 'grad_w', 'grad_w_out': 'grad_w', 'grad_final_norm_w': 'grad_w', 'delta_norm_w': 'delta_w', 'delta_w_in': 'delta_w', 'delta_conv_a_w': 'delta_w', 'delta_gla_gate_w': 'delta_w', 'delta_gla_gate_b': 'delta_w', 'delta_gla_norm_w': 'delta_w', 'delta_pool_w': 'delta_w', 'delta_pool_scale': 'delta_w', 'delta_ssd_conv_w': 'delta_w', 'delta_ssd_conv_b': 'delta_w', 'delta_ssd_dt_bias': 'delta_w', 'delta_ssd_a_log': 'delta_w', 'delta_ssd_d': 'delta_w', 'delta_ssd_norm_w': 'delta_w', 'delta_w_out': 'delta_w', 'delta_final_norm_w': 'delta_w', 'new_m_norm_w': 'new_m', 'new_m_w_in': 'new_m', 'new_m_conv_a_w': 'new_m', 'new_m_gla_gate_w': 'new_m', 'new_m_gla_gate_b': 'new_m', 'new_m_gla_norm_w': 'new_m', 'new_m_pool_w': 'new_m', 'new_m_pool_scale': 'new_m', 'new_m_ssd_conv_w': 'new_m', 'new_m_ssd_conv_b': 'new_m', 'new_m_ssd_dt_bias': 'new_m', 'new_m_ssd_a_log': 'new_m', 'new_m_ssd_d': 'new_m', 'new_m_ssd_norm_w': 'new_m', 'new_m_w_out': 'new_m', 'new_m_final_norm_w': 'new_m', 'new_v_norm_w': 'new_v', 'new_v_w_in': 'new_v', 'new_v_conv_a_w': 'new_v', 'new_v_gla_gate_w': 'new_v', 'new_v_gla_gate_b': 'new_v', 'new_v_gla_norm_w': 'new_v', 'new_v_pool_w': 'new_v', 'new_v_pool_scale': 'new_v', 'new_v_ssd_conv_w': 'new_v', 'new_v_ssd_conv_b': 'new_v', 'new_v_ssd_dt_bias': 'new_v', 'new_v_ssd_a_log': 'new_v', 'new_v_ssd_d': 'new_v', 'new_v_ssd_norm_w': 'new_v', 'new_v_w_out': 'new_v', 'new_v_final_norm_w': 'new_v'}


def _forward(args):
    return _fwd_reference(*[args[k] for k in FWD_PARAMS])


def _output_shape():
    def fwd():
        inp = _fwd_setup_inputs(0)
        return _fwd_reference(*[inp[k] for k in FWD_PARAMS])
    out = _jax.eval_shape(fwd)
    return out.shape, out.dtype

N_MICROBATCH = 1
ADAM_LR = 0.001
ADAM_B1 = 0.9
ADAM_B2 = 0.999
ADAM_EPS = 1e-08
ADAM_WD = 0.01
ADAM_STEP = 10
PER_EXAMPLE_BATCH_AXIS = {'x': 0, 'loss_target': 0}
SHARED_INPUTS = []
_WEIGHT_DTYPES = {'norm_w': _jnp.float32, 'w_in': _jnp.float32, 'conv_a_w': _jnp.float32, 'gla_gate_w': _jnp.float32, 'gla_gate_b': _jnp.float32, 'gla_norm_w': _jnp.float32, 'pool_w': _jnp.float32, 'pool_scale': _jnp.float32, 'ssd_conv_w': _jnp.float32, 'ssd_conv_b': _jnp.float32, 'ssd_dt_bias': _jnp.float32, 'ssd_a_log': _jnp.float32, 'ssd_d': _jnp.float32, 'ssd_norm_w': _jnp.float32, 'w_out': _jnp.float32, 'final_norm_w': _jnp.float32}
MOMENT_SCALE = {'norm_w': 2.629194e-01, 'w_in': 1.446266e-01, 'conv_a_w': 1.313119e-01, 'gla_gate_w': 2.064340e-02, 'gla_gate_b': 9.243263e-02, 'gla_norm_w': 2.689848e-01, 'pool_w': 1.129950e-01, 'pool_scale': 1.140072e-01, 'ssd_conv_w': 1.371279e-01, 'ssd_conv_b': 1.886339e-01, 'ssd_dt_bias': 9.110288e-01, 'ssd_a_log': 1.198825e+00, 'ssd_d': 1.330462e+00, 'ssd_norm_w': 2.178127e-01, 'w_out': 1.527632e-01, 'final_norm_w': 6.402367e+01}


def _to_microbatches(a, axis):
    t = _jnp.moveaxis(a, axis, 0)
    t = t.reshape((N_MICROBATCH, t.shape[0] // N_MICROBATCH) + t.shape[1:])
    return _jnp.moveaxis(t, 1, axis + 1)


def setup_inputs(seed: int = 0) -> dict:
    inp = _fwd_setup_inputs(seed)
    key = _jax.random.fold_in(_jax.random.key(seed), 7919)
    shape, _ = _output_shape()
    out = dict(inp)
    out["loss_target"] = _jax.random.normal(_jax.random.fold_in(key, 0), shape, _jnp.float32)
    for i, name in enumerate(TWIN_WEIGHTS):
        w = inp[name].astype(_jnp.float32)
        if MOMENT_SCALE is None:
            s = _jnp.sqrt(_jnp.mean(_jnp.square(w)) + 1e-30)
        else:
            s = MOMENT_SCALE[name]
        km, kv = _jax.random.split(_jax.random.fold_in(key, i + 1))
        out[name] = w
        out["m_" + name] = s * _jax.random.normal(km, w.shape, _jnp.float32)
        out["v_" + name] = (s * s) * _jax.random.uniform(kv, w.shape, _jnp.float32, 0.5, 1.5)
    if N_MICROBATCH > 1:
        for name, axis in PER_EXAMPLE_BATCH_AXIS.items():
            out[name] = _to_microbatches(out[name], axis)
    return {'x': out['x'], 'norm_w': out['norm_w'], 'w_in': out['w_in'], 'conv_a_w': out['conv_a_w'], 'gla_gate_w': out['gla_gate_w'], 'gla_gate_b': out['gla_gate_b'], 'gla_norm_w': out['gla_norm_w'], 'pool_w': out['pool_w'], 'pool_scale': out['pool_scale'], 'ssd_conv_w': out['ssd_conv_w'], 'ssd_conv_b': out['ssd_conv_b'], 'ssd_dt_bias': out['ssd_dt_bias'], 'ssd_a_log': out['ssd_a_log'], 'ssd_d': out['ssd_d'], 'ssd_norm_w': out['ssd_norm_w'], 'w_out': out['w_out'], 'final_norm_w': out['final_norm_w'], 'loss_target': out['loss_target'], 'm_norm_w': out['m_norm_w'], 'm_w_in': out['m_w_in'], 'm_conv_a_w': out['m_conv_a_w'], 'm_gla_gate_w': out['m_gla_gate_w'], 'm_gla_gate_b': out['m_gla_gate_b'], 'm_gla_norm_w': out['m_gla_norm_w'], 'm_pool_w': out['m_pool_w'], 'm_pool_scale': out['m_pool_scale'], 'm_ssd_conv_w': out['m_ssd_conv_w'], 'm_ssd_conv_b': out['m_ssd_conv_b'], 'm_ssd_dt_bias': out['m_ssd_dt_bias'], 'm_ssd_a_log': out['m_ssd_a_log'], 'm_ssd_d': out['m_ssd_d'], 'm_ssd_norm_w': out['m_ssd_norm_w'], 'm_w_out': out['m_w_out'], 'm_final_norm_w': out['m_final_norm_w'], 'v_norm_w': out['v_norm_w'], 'v_w_in': out['v_w_in'], 'v_conv_a_w': out['v_conv_a_w'], 'v_gla_gate_w': out['v_gla_gate_w'], 'v_gla_gate_b': out['v_gla_gate_b'], 'v_gla_norm_w': out['v_gla_norm_w'], 'v_pool_w': out['v_pool_w'], 'v_pool_scale': out['v_pool_scale'], 'v_ssd_conv_w': out['v_ssd_conv_w'], 'v_ssd_conv_b': out['v_ssd_conv_b'], 'v_ssd_dt_bias': out['v_ssd_dt_bias'], 'v_ssd_a_log': out['v_ssd_a_log'], 'v_ssd_d': out['v_ssd_d'], 'v_ssd_norm_w': out['v_ssd_norm_w'], 'v_w_out': out['v_w_out'], 'v_final_norm_w': out['v_final_norm_w']}


def _loss(weights, diff, rest, loss_target):
    with _jax.named_scope("forward"):
        args = {**rest, TWIN_DIFF_INPUT: diff, **{k: w.astype(_WEIGHT_DTYPES[k]) for k, w in weights.items()}}
        y = _forward(args)
    with _jax.named_scope("loss_head"):
        err = _jnp.square(y.astype(_jnp.float32) - loss_target)
        return 0.5 * _jnp.sum(_jnp.mean(err, axis=-1)) if err.ndim else 0.5 * err


def _adamw(w, g, m, v):
    m = ADAM_B1 * m + (1.0 - ADAM_B1) * g
    v = ADAM_B2 * v + (1.0 - ADAM_B2) * _jnp.square(g)
    m_hat = m / (1.0 - ADAM_B1 ** ADAM_STEP)
    v_hat = v / (1.0 - ADAM_B2 ** ADAM_STEP)
    delta = -ADAM_LR * (m_hat / (_jnp.sqrt(v_hat) + ADAM_EPS) + ADAM_WD * w)
    return delta, m, v


def reference(x, norm_w, w_in, conv_a_w, gla_gate_w, gla_gate_b, gla_norm_w, pool_w, pool_scale, ssd_conv_w, ssd_conv_b, ssd_dt_bias, ssd_a_log, ssd_d, ssd_norm_w, w_out, final_norm_w, loss_target, m_norm_w, m_w_in, m_conv_a_w, m_gla_gate_w, m_gla_gate_b, m_gla_norm_w, m_pool_w, m_pool_scale, m_ssd_conv_w, m_ssd_conv_b, m_ssd_dt_bias, m_ssd_a_log, m_ssd_d, m_ssd_norm_w, m_w_out, m_final_norm_w, v_norm_w, v_w_in, v_conv_a_w, v_gla_gate_w, v_gla_gate_b, v_gla_norm_w, v_pool_w, v_pool_scale, v_ssd_conv_w, v_ssd_conv_b, v_ssd_dt_bias, v_ssd_a_log, v_ssd_d, v_ssd_norm_w, v_w_out, v_final_norm_w):
    given = dict(x=x, norm_w=norm_w, w_in=w_in, conv_a_w=conv_a_w, gla_gate_w=gla_gate_w, gla_gate_b=gla_gate_b, gla_norm_w=gla_norm_w, pool_w=pool_w, pool_scale=pool_scale, ssd_conv_w=ssd_conv_w, ssd_conv_b=ssd_conv_b, ssd_dt_bias=ssd_dt_bias, ssd_a_log=ssd_a_log, ssd_d=ssd_d, ssd_norm_w=ssd_norm_w, w_out=w_out, final_norm_w=final_norm_w, loss_target=loss_target, m_norm_w=m_norm_w, m_w_in=m_w_in, m_conv_a_w=m_conv_a_w, m_gla_gate_w=m_gla_gate_w, m_gla_gate_b=m_gla_gate_b, m_gla_norm_w=m_gla_norm_w, m_pool_w=m_pool_w, m_pool_scale=m_pool_scale, m_ssd_conv_w=m_ssd_conv_w, m_ssd_conv_b=m_ssd_conv_b, m_ssd_dt_bias=m_ssd_dt_bias, m_ssd_a_log=m_ssd_a_log, m_ssd_d=m_ssd_d, m_ssd_norm_w=m_ssd_norm_w, m_w_out=m_w_out, m_final_norm_w=m_final_norm_w, v_norm_w=v_norm_w, v_w_in=v_w_in, v_conv_a_w=v_conv_a_w, v_gla_gate_w=v_gla_gate_w, v_gla_gate_b=v_gla_gate_b, v_gla_norm_w=v_gla_norm_w, v_pool_w=v_pool_w, v_pool_scale=v_pool_scale, v_ssd_conv_w=v_ssd_conv_w, v_ssd_conv_b=v_ssd_conv_b, v_ssd_dt_bias=v_ssd_dt_bias, v_ssd_a_log=v_ssd_a_log, v_ssd_d=v_ssd_d, v_ssd_norm_w=v_ssd_norm_w, v_w_out=v_w_out, v_final_norm_w=v_final_norm_w)
    weights = {n: given[n] for n in TWIN_WEIGHTS}
    shared = {n: given[n] for n in SHARED_INPUTS}
    per_example = {n: given[n] for n in ['x']}
    grad_fn = _jax.value_and_grad(_loss, argnums=(0, 1))

    def one_microbatch(ex, loss_target):
        ex = dict(ex)
        diff = ex.pop(TWIN_DIFF_INPUT)
        return grad_fn(weights, diff, {**shared, **ex}, loss_target)

    if N_MICROBATCH == 1:
        loss, (grad_w, grad_x) = one_microbatch(per_example, given["loss_target"])
    else:
        def body(carry, xs):
            loss_sum, grad_sum = carry
            l_k, (gw_k, gx_k) = one_microbatch(xs[0], xs[1])
            with _jax.named_scope("update"):
                return (loss_sum + l_k, _jax.tree.map(_jnp.add, grad_sum, gw_k)), gx_k

        init = (_jnp.zeros((), _jnp.float32), _jax.tree.map(_jnp.zeros_like, weights))
        (loss, grad_w), grad_x = _jax.lax.scan(body, init, (per_example, given["loss_target"]))
    with _jax.named_scope("update"):
        delta_w, new_m, new_v = {}, {}, {}
        for n in TWIN_WEIGHTS:
            delta_w[n], new_m[n], new_v[n] = _adamw(weights[n], grad_w[n], given["m_" + n], given["v_" + n])
    return (loss, grad_x, *[grad_w[n] for n in TWIN_WEIGHTS], *[delta_w[n] for n in TWIN_WEIGHTS],
            *[new_m[n] for n in TWIN_WEIGHTS], *[new_v[n] for n in TWIN_WEIGHTS])
```

```python
import functools

import jax
import jax.numpy as jnp
from jax import lax
from jax.experimental import pallas as pl
from jax.experimental.pallas import tpu as pltpu

F32 = jnp.float32
BF16 = jnp.bfloat16
MESH = pl.DeviceIdType.MESH

D = 1024
CH = 64
EPS = 1e-6
NP = 3456
NPROJ = 3348
GLA_SCALE = 32.0 ** -0.5
INV_TAU = 1.0 / 16.0
TB = 256
NCH = TB // CH
HALO_W = 1536

C_SX, C_AH, C_AC, C_PU, C_AB, C_AZ = 0, 768, 1024, 1280, 1536, 1792
C_GQ, C_GK, C_GV, C_GZ, C_PZ, C_SZ, C_TL = 2048, 2176, 2304, 2560, 2816, 3072, 3328
_PERM = ((2576, 768), (0, 256), (512, 256), (1808, 256), (256, 256), (768, 256), (1024, 128), (1152, 128),
         (1280, 256), (1552, 256), (2064, 256), (2320, 256), (1536, 16), (3344, 4))
_UNPERM = ((768, 256), (1536, 256), (1024, 256), (1792, 256), (2048, 128), (2176, 128), (2304, 256), (3328, 16),
           (2560, 256), (1280, 256), (2816, 256), (3072, 256), (0, 768), (3344, 4))

R_CAW, R_GB, R_GNW, R_PSC, R_SCB, R_DTB, R_AE, R_DE, R_SNW, R_SCW = 0, 3, 4, 5, 6, 7, 8, 9, 10, 12

ADAM_LR, ADAM_B1, ADAM_B2, ADAM_EPS, ADAM_WD, ADAM_STEP = 0.001, 0.9, 0.999, 1e-08, 0.01, 10

VMEM_LIMIT = 56 * 1024 * 1024


def _cparams(sem, limit=VMEM_LIMIT):
    return pltpu.CompilerParams(dimension_semantics=sem, vmem_limit_bytes=limit)


def _dot(a, b):
    return jnp.dot(a.astype(BF16), b.astype(BF16), preferred_element_type=F32)


def _dot_nt(a, b):
    return lax.dot_general(a.astype(BF16), b.astype(BF16), (((1,), (1,)), ((), ())), preferred_element_type=F32)


def _dot_tn(a, b):
    return lax.dot_general(a.astype(BF16), b.astype(BF16), (((0,), (0,)), ((), ())), preferred_element_type=F32)


def _split(a):
    hi = a.astype(BF16)
    lo = (a - hi.astype(F32)).astype(BF16)
    return hi, lo


def _dot2_l(a, b):
    hi, lo = _split(a)
    return _dot(hi, b) + _dot(lo, b)


def _dot2_r(a, b):
    hi, lo = _split(b)
    return _dot(a, hi) + _dot(a, lo)


def _dot3_l(a, b):
    hi, lo = _split(a)
    lo2 = ((a - hi.astype(F32)) - lo.astype(F32)).astype(BF16)
    return _dot(hi, b) + _dot(lo, b) + _dot(lo2, b)


def _dot2_nt(a, b):
    hi, lo = _split(a)
    return _dot_nt(hi, b) + _dot_nt(lo, b)


def _silu(z):
    return z * jax.nn.sigmoid(z)


def _dsilu(z):
    s = jax.nn.sigmoid(z)
    return s * (1.0 + z * (1.0 - s))


def _lse1(x):
    return jnp.log(1.0 + jnp.exp(-jnp.abs(x)))


def _cs(a):
    return jnp.sum(a, axis=0, keepdims=True)


def _iota(shape, dim):
    return lax.broadcasted_iota(jnp.int32, shape, dim)


def _strict_upper():
    return jnp.where(_iota((CH, CH), 1) > _iota((CH, CH), 0), 1.0, 0.0).astype(BF16)


def _strict_lower():
    return jnp.where(_iota((CH, CH), 1) < _iota((CH, CH), 0), 1.0, 0.0).astype(BF16)


def _expand_mat():
    return jnp.where(_iota((128, 256), 0) - 16 == (_iota((128, 256), 1) >> 6), 1.0, 0.0).astype(BF16)


def _group_mean_mat():
    return jnp.where((_iota((256, 256), 0) >> 6) == (_iota((256, 256), 1) >> 6), 1.0 / 64.0, 0.0).astype(BF16)


def _head_mask_t():
    return jnp.where((_iota((256, 128), 0) >> 6) == (_iota((256, 128), 1) >> 5), 1.0, 0.0).astype(F32)


def _dn(ext, k, n, h):
    return pltpu.roll(ext, k, axis=0)[h:h + n]


def _up(ext, k, n):
    return pltpu.roll(ext, ext.shape[0] - k, axis=0)[:n]


def _pool_lane_select(lane, s2, s4, s8, s16):
    return jnp.where(lane < 64, s2, jnp.where(lane < 128, s4, jnp.where(lane < 192, s8, s16)))


def _winsum_dn(ext, lane):
    s2 = ext + pltpu.roll(ext, 1, axis=0)
    s4 = s2 + pltpu.roll(s2, 2, axis=0)
    s8 = s4 + pltpu.roll(s4, 4, axis=0)
    s16 = s8 + pltpu.roll(s8, 8, axis=0)
    return _pool_lane_select(lane, s2, s4, s8, s16)


def _winsum_up(ext, lane):
    m = ext.shape[0]
    s2 = ext + pltpu.roll(ext, m - 1, axis=0)
    s4 = s2 + pltpu.roll(s2, m - 2, axis=0)
    s8 = s4 + pltpu.roll(s4, m - 4, axis=0)
    s16 = s8 + pltpu.roll(s8, m - 8, axis=0)
    return _pool_lane_select(lane, s2, s4, s8, s16)


def _pool_count(tile, n):
    lane = _iota((1, 256), 1)
    win = _pool_lane_select(lane, 2.0, 4.0, 8.0, 16.0).astype(F32)
    tpos = (tile * n + _iota((n, 1), 0) + 1).astype(F32)
    return jnp.minimum(tpos, win)


def _gla_chunk_fwd(q, k, v, tail, gw, gb, s_prev, su, mask_t):
    pre = _dot(tail, gw) + gb
    la = (jnp.minimum(pre, 0.0) - _lse1(pre)) * INV_TAU
    rev = _dot2_r(su, la)
    dec = jnp.exp(rev)
    kd = k * dec
    d_s = jnp.exp(_cs(la))
    s_new = s_prev * d_s + _dot_tn(v, kd) * mask_t
    qs = q * GLA_SCALE
    o = _dot_nt(qs, s_new)
    return pre, dec, kd, d_s, s_new, qs, o


def _ssd_chunk_fwd(xc, tail, dtb, a_e, s_prev, su, emat):
    xs, bm, cm = xc[:, 0:256], xc[:, 256:512], xc[:, 512:768]
    dtin = tail + dtb
    dtf = jnp.maximum(dtin, 0.0) + _lse1(dtin)
    dte = _dot2_l(dtf, emat)
    da = dte * a_e
    wdec = jnp.exp(_dot2_r(su, da))
    w = wdec * dte
    xw = xs * w
    et = jnp.exp(_cs(da))
    ut = jnp.concatenate([_dot_tn(bm[:, 0:128], xw[:, 0:128]), _dot_tn(bm[:, 128:256], xw[:, 128:256])], axis=1)
    s_new = s_prev * et + ut
    y = jnp.concatenate([_dot(cm[:, 0:128], s_new[:, 0:128]), _dot(cm[:, 128:256], s_new[:, 128:256])], axis=1)
    return xs, bm, cm, dtin, dte, wdec, w, xw, et, s_new, y


def _rmsproj(x, nw, wp, name, tm=256):
    t = x.shape[0]

    def body(x_ref, nw_ref, w_ref, o_ref):
        xv = x_ref[...]
        rs = lax.rsqrt(jnp.mean(xv * xv, axis=-1, keepdims=True) + EPS)
        h = (xv * rs * nw_ref[...]).astype(BF16)
        o_ref[...] = jnp.dot(h, w_ref[...], preferred_element_type=F32)

    return pl.pallas_call(
        body, grid=(t // tm,), name=name,
        in_specs=[pl.BlockSpec((tm, D), lambda i: (i, 0)), pl.BlockSpec((1, D), lambda i: (0, 0)),
                  pl.BlockSpec((D, NP), lambda i: (0, 0))],
        out_specs=pl.BlockSpec((tm, NP), lambda i: (i, 0)),
        out_shape=jax.ShapeDtypeStruct((t, NP), F32),
        compiler_params=_cparams(("parallel",)),
    )(x, nw, wp)


def _outproj(x, mix, wo, name, tm=512):
    t = x.shape[0]

    def body(x_ref, m_ref, w_ref, o_ref):
        o_ref[...] = x_ref[...] + jnp.dot(m_ref[...].astype(BF16), w_ref[...], preferred_element_type=F32)

    return pl.pallas_call(
        body, grid=(t // tm,), name=name,
        in_specs=[pl.BlockSpec((tm, D), lambda i: (i, 0)), pl.BlockSpec((tm, D), lambda i: (i, 0)),
                  pl.BlockSpec((D, D), lambda i: (0, 0))],
        out_specs=pl.BlockSpec((tm, D), lambda i: (i, 0)),
        out_shape=jax.ShapeDtypeStruct((t, D), F32),
        compiler_params=_cparams(("parallel",)),
    )(x, mix, wo)


def _matmul_rows(a, w, name, tm=512):
    t, kdim = a.shape
    n = w.shape[1]

    def body(a_ref, w_ref, o_ref):
        o_ref[...] = jnp.dot(a_ref[...].astype(BF16), w_ref[...], preferred_element_type=F32)

    return pl.pallas_call(
        body, grid=(t // tm,), name=name,
        in_specs=[pl.BlockSpec((tm, kdim), lambda i: (i, 0)), pl.BlockSpec((kdim, n), lambda i: (0, 0))],
        out_specs=pl.BlockSpec((tm, n), lambda i: (i, 0)),
        out_shape=jax.ShapeDtypeStruct((t, n), F32),
        compiler_params=_cparams(("parallel",)),
    )(a, w)


def _head(x, tgt, fw, name, tm=512):
    t = x.shape[0]

    def body(x_ref, t_ref, w_ref, dx_ref, acc_ref):
        @pl.when(pl.program_id(0) == 0)
        def _():
            acc_ref[...] = jnp.zeros_like(acc_ref)

        xv = x_ref[...]
        w = w_ref[...]
        rs = lax.rsqrt(jnp.mean(xv * xv, axis=-1, keepdims=True) + EPS)
        xh = xv * rs
        err = xh * w - t_ref[...]
        dy = err * (1.0 / D)
        dxh = dy * w
        dx_ref[...] = rs * (dxh - xh * jnp.mean(dxh * xh, axis=-1, keepdims=True))
        acc_ref[0:1, :] += _cs(dy * xh)
        acc_ref[1:2, :] += jnp.zeros((1, D), F32) + (0.5 / D) * jnp.sum(err * err)

    return pl.pallas_call(
        body, grid=(t // tm,), name=name,
        in_specs=[pl.BlockSpec((tm, D), lambda i: (i, 0)), pl.BlockSpec((tm, D), lambda i: (i, 0)),
                  pl.BlockSpec((1, D), lambda i: (0, 0))],
        out_specs=[pl.BlockSpec((tm, D), lambda i: (i, 0)), pl.BlockSpec((8, D), lambda i: (0, 0))],
        out_shape=[jax.ShapeDtypeStruct((t, D), F32), jax.ShapeDtypeStruct((8, D), F32)],
        compiler_params=_cparams(("arbitrary",)),
    )(x, tgt, fw)


def _dxin(dp, wpt, x, dxn, nw, name, tm=256):
    t = x.shape[0]

    def body(dp_ref, w_ref, x_ref, dxn_ref, nw_ref, dx_ref, dnw_ref):
        @pl.when(pl.program_id(0) == 0)
        def _():
            dnw_ref[...] = jnp.zeros_like(dnw_ref)

        dh = jnp.dot(dp_ref[...].astype(BF16), w_ref[...], preferred_element_type=F32)
        xv = x_ref[...]
        rs = lax.rsqrt(jnp.mean(xv * xv, axis=-1, keepdims=True) + EPS)
        xh = xv * rs
        dnw_ref[0:1, :] += _cs(dh * xh)
        dxh = dh * nw_ref[...]
        dx_ref[...] = dxn_ref[...] + rs * (dxh - xh * jnp.mean(dxh * xh, axis=-1, keepdims=True))

    return pl.pallas_call(
        body, grid=(t // tm,), name=name,
        in_specs=[pl.BlockSpec((tm, NP), lambda i: (i, 0)), pl.BlockSpec((NP, D), lambda i: (0, 0)),
                  pl.BlockSpec((tm, D), lambda i: (i, 0)), pl.BlockSpec((tm, D), lambda i: (i, 0)),
                  pl.BlockSpec((1, D), lambda i: (0, 0))],
        out_specs=[pl.BlockSpec((tm, D), lambda i: (i, 0)), pl.BlockSpec((8, D), lambda i: (0, 0))],
        out_shape=[jax.ShapeDtypeStruct((t, D), F32), jax.ShapeDtypeStruct((8, D), F32)],
        compiler_params=_cparams(("arbitrary",)),
    )(dp, wpt, x, dxn, nw)


def _dwin(x, nw, dp, name, tm=512, tn=1152):
    t = x.shape[0]

    def body(x_ref, nw_ref, dp_ref, o_ref):
        @pl.when(pl.program_id(1) == 0)
        def _():
            o_ref[...] = jnp.zeros_like(o_ref)

        xv = x_ref[...]
        rs = lax.rsqrt(jnp.mean(xv * xv, axis=-1, keepdims=True) + EPS)
        h = xv * rs * nw_ref[...]
        o_ref[...] += _dot_tn(h, dp_ref[...])

    return pl.pallas_call(
        body, grid=(NP // tn, t // tm), name=name,
        in_specs=[pl.BlockSpec((tm, D), lambda j, i: (i, 0)), pl.BlockSpec((1, D), lambda j, i: (0, 0)),
                  pl.BlockSpec((tm, tn), lambda j, i: (i, j))],
        out_specs=pl.BlockSpec((D, tn), lambda j, i: (0, j)),
        out_shape=jax.ShapeDtypeStruct((D, NP), F32),
        compiler_params=_cparams(("parallel", "arbitrary")),
    )(x, nw, dp)


def _dwout(mix, dxn, name, tm=512):
    t = mix.shape[0]

    def body(m_ref, g_ref, o_ref):
        @pl.when(pl.program_id(0) == 0)
        def _():
            o_ref[...] = jnp.zeros_like(o_ref)

        o_ref[...] += _dot_tn(m_ref[...], g_ref[...])

    return pl.pallas_call(
        body, grid=(t // tm,), name=name,
        in_specs=[pl.BlockSpec((tm, D), lambda i: (i, 0)), pl.BlockSpec((tm, D), lambda i: (i, 0))],
        out_specs=pl.BlockSpec((D, D), lambda i: (0, 0)),
        out_shape=jax.ShapeDtypeStruct((D, D), F32),
        compiler_params=_cparams(("arbitrary",)),
    )(mix, dxn)


def _mixer_fwd(proj, prm, gw, pw, name):
    t = proj.shape[0]
    nt, nc = t // TB, t // CH

    def body(p_ref, prm_ref, gw_ref, pw_ref, mix_ref, sg_ref, ss_ref, sg_s, ss_s, h_ua, h_pu, h_sx, xc_s):
        i = pl.program_id(0)

        @pl.when(i == 0)
        def _():
            sg_s[...] = jnp.zeros_like(sg_s)
            ss_s[...] = jnp.zeros_like(ss_s)
            h_ua[...] = jnp.zeros_like(h_ua)
            h_pu[...] = jnp.zeros_like(h_pu)
            h_sx[...] = jnp.zeros_like(h_sx)

        lane = _iota((1, 256), 1)
        u = p_ref[:, C_AC:C_AC + 256] * p_ref[:, C_AH:C_AH + 256]
        ext = jnp.concatenate([h_ua[...], u], axis=0)
        cv = (prm_ref[R_CAW + 2:R_CAW + 3, 0:256] * u + prm_ref[R_CAW + 1:R_CAW + 2, 0:256] * _dn(ext, 1, TB, 8)
              + prm_ref[R_CAW:R_CAW + 1, 0:256] * _dn(ext, 2, TB, 8))
        mix_ref[:, 0:256] = p_ref[:, C_AB:C_AB + 256] * cv * _silu(p_ref[:, C_AZ:C_AZ + 256])
        h_ua[...] = u[TB - 8:, :]
        pu = p_ref[:, C_PU:C_PU + 256]
        ext = jnp.concatenate([h_pu[...], pu], axis=0)
        pooled = _winsum_dn(ext, lane)[16:] / _pool_count(i, TB) - pu
        mixed = _dot(pooled, pw_ref[...])
        mix_ref[:, 512:768] = prm_ref[R_PSC:R_PSC + 1, 0:256] * mixed * _silu(p_ref[:, C_PZ:C_PZ + 256])
        h_pu[...] = pu[TB - 16:, :]
        sx = p_ref[:, C_SX:C_SX + 768]
        ext = jnp.concatenate([h_sx[...], sx], axis=0)
        pre = (prm_ref[R_SCW + 3:R_SCW + 4, :] * sx + prm_ref[R_SCW + 2:R_SCW + 3, :] * _dn(ext, 1, TB, 8)
               + prm_ref[R_SCW + 1:R_SCW + 2, :] * _dn(ext, 2, TB, 8) + prm_ref[R_SCW:R_SCW + 1, :] * _dn(ext, 3, TB, 8)
               + prm_ref[R_SCB:R_SCB + 1, :])
        xc_s[...] = _silu(pre)
        h_sx[...] = sx[TB - 8:, :]

        su = _strict_upper()
        mask_t = _head_mask_t()
        emat = _expand_mat()
        gmean = _group_mean_mat()
        gw_v = gw_ref[...]
        gb = prm_ref[R_GB:R_GB + 1, 0:128]
        gnw = prm_ref[R_GNW:R_GNW + 1, 0:256]
        dtb = prm_ref[R_DTB:R_DTB + 1, 0:128]
        a_e = prm_ref[R_AE:R_AE + 1, 0:256]
        d_e = prm_ref[R_DE:R_DE + 1, 0:256]
        snw = prm_ref[R_SNW:R_SNW + 1, 0:256]
        for c in range(NCH):
            rows = pl.ds(c * CH, CH)
            tail = p_ref[rows, C_TL:C_TL + 128]
            sg_prev = sg_s[...]
            sg_ref[c] = sg_prev
            _, _, _, _, sg_new, _, o = _gla_chunk_fwd(
                p_ref[rows, C_GQ:C_GQ + 128], p_ref[rows, C_GK:C_GK + 128], p_ref[rows, C_GV:C_GV + 256], tail,
                gw_v, gb, sg_prev, su, mask_t)
            sg_s[...] = sg_new
            on = o * lax.rsqrt(_dot2_l(o * o, gmean) + EPS)
            mix_ref[rows, 256:512] = on * gnw * _silu(p_ref[rows, C_GZ:C_GZ + 256])
            ss_prev = ss_s[...]
            ss_ref[c] = ss_prev
            xs, _, _, _, _, _, _, _, _, ss_new, y = _ssd_chunk_fwd(xc_s[rows, :], tail, dtb, a_e, ss_prev, su, emat)
            ss_s[...] = ss_new
            y2 = (y + d_e * xs) * _silu(p_ref[rows, C_SZ:C_SZ + 256])
            mix_ref[rows, 768:1024] = y2 * lax.rsqrt(jnp.mean(y2 * y2, axis=-1, keepdims=True) + EPS) * snw

    return pl.pallas_call(
        body, grid=(nt,), name=name,
        in_specs=[pl.BlockSpec((TB, NP), lambda i: (i, 0)), pl.BlockSpec((16, 768), lambda i: (0, 0)),
                  pl.BlockSpec((128, 128), lambda i: (0, 0)), pl.BlockSpec((256, 256), lambda i: (0, 0))],
        out_specs=[pl.BlockSpec((TB, D), lambda i: (i, 0)), pl.BlockSpec((NCH, 256, 128), lambda i: (i, 0, 0)),
                   pl.BlockSpec((NCH, 128, 256), lambda i: (i, 0, 0))],
        out_shape=[jax.ShapeDtypeStruct((t, D), F32), jax.ShapeDtypeStruct((nc, 256, 128), F32),
                   jax.ShapeDtypeStruct((nc, 128, 256), F32)],
        scratch_shapes=[pltpu.VMEM((256, 128), F32), pltpu.VMEM((128, 256), F32), pltpu.VMEM((8, 256), F32),
                        pltpu.VMEM((16, 256), F32), pltpu.VMEM((8, 768), F32), pltpu.VMEM((TB, 768), F32)],
        compiler_params=_cparams(("arbitrary",)),
    )(proj, prm, gw, pw)


def _mixer_bwd(proj, dmix, sg, ss, prm, gw, pw, name):
    t = proj.shape[0]
    nt = t // TB
    rev = lambda i: nt - 1 - i

    def body(p_ref, hp_ref, dm_ref, sg_ref, ss_ref, prm_ref, gw_ref, pw_ref, dp_ref, gsm_ref, dgw_ref, dpw_ref,
             gg_s, gs_s, h_dcv, h_dpl, h_dpre, xc_s, pre_s, dxc_s):
        i = pl.program_id(0)
        tile = nt - 1 - i

        @pl.when(i == 0)
        def _():
            for r in (gg_s, gs_s, h_dcv, h_dpl, h_dpre, gsm_ref, dgw_ref, dpw_ref):
                r[...] = jnp.zeros_like(r)

        lane = _iota((1, 256), 1)
        first = (tile > 0).astype(F32)
        ah, ac = p_ref[:, C_AH:C_AH + 256], p_ref[:, C_AC:C_AC + 256]
        ab, az = p_ref[:, C_AB:C_AB + 256], p_ref[:, C_AZ:C_AZ + 256]
        w0, w1, w2 = (prm_ref[R_CAW + j:R_CAW + j + 1, 0:256] for j in range(3))
        u = ac * ah
        ext = jnp.concatenate([hp_ref[8:16, C_AC:C_AC + 256] * hp_ref[8:16, C_AH:C_AH + 256] * first, u], axis=0)
        u1, u2 = _dn(ext, 1, TB, 8), _dn(ext, 2, TB, 8)
        cv = w2 * u + w1 * u1 + w0 * u2
        g = dm_ref[:, 0:256]
        sz = _silu(az)
        dp_ref[:, C_AB:C_AB + 256] = g * cv * sz
        dp_ref[:, C_AZ:C_AZ + 256] = g * ab * cv * _dsilu(az)
        dcv = g * ab * sz
        dext = jnp.concatenate([dcv, h_dcv[...]], axis=0)
        du = w2 * dcv + w1 * _up(dext, 1, TB) + w0 * _up(dext, 2, TB)
        dp_ref[:, C_AC:C_AC + 256] = du * ah
        dp_ref[:, C_AH:C_AH + 256] = du * ac
        gsm_ref[R_CAW:R_CAW + 1, 0:256] += _cs(dcv * u2)
        gsm_ref[R_CAW + 1:R_CAW + 2, 0:256] += _cs(dcv * u1)
        gsm_ref[R_CAW + 2:R_CAW + 3, 0:256] += _cs(dcv * u)
        h_dcv[...] = dcv[0:8, :]
        pu, pz = p_ref[:, C_PU:C_PU + 256], p_ref[:, C_PZ:C_PZ + 256]
        psc = prm_ref[R_PSC:R_PSC + 1, 0:256]
        cnt = _pool_count(tile, TB)
        ext = jnp.concatenate([hp_ref[:, C_PU:C_PU + 256] * first, pu], axis=0)
        pooled = _winsum_dn(ext, lane)[16:] / cnt - pu
        pw_v = pw_ref[...]
        mixed = _dot(pooled, pw_v)
        g = dm_ref[:, 512:768]
        sz = _silu(pz)
        gsm_ref[R_PSC:R_PSC + 1, 0:256] += _cs(g * mixed * sz)
        dp_ref[:, C_PZ:C_PZ + 256] = g * psc * mixed * _dsilu(pz)
        dmixed = g * psc * sz
        dpw_ref[...] += _dot_tn(pooled, dmixed)
        dpooled = _dot_nt(dmixed, pw_v)
        qd = dpooled / cnt
        dext = jnp.concatenate([qd, h_dpl[...]], axis=0)
        dp_ref[:, C_PU:C_PU + 256] = _winsum_up(dext, lane)[:TB] - dpooled
        h_dpl[...] = qd[0:16, :]
        sx = p_ref[:, C_SX:C_SX + 768]
        cw = [prm_ref[R_SCW + j:R_SCW + j + 1, :] for j in range(4)]
        ext = jnp.concatenate([hp_ref[8:16, C_SX:C_SX + 768] * first, sx], axis=0)
        sx1, sx2, sx3 = _dn(ext, 1, TB, 8), _dn(ext, 2, TB, 8), _dn(ext, 3, TB, 8)
        pre = cw[3] * sx + cw[2] * sx1 + cw[1] * sx2 + cw[0] * sx3 + prm_ref[R_SCB:R_SCB + 1, :]
        pre_s[...] = pre
        xc_s[...] = _silu(pre)

        su, sl = _strict_upper(), _strict_lower()
        mask_t = _head_mask_t()
        emat = _expand_mat()
        gmean = _group_mean_mat()
        gw_v = gw_ref[...]
        gb = prm_ref[R_GB:R_GB + 1, 0:128]
        gnw = prm_ref[R_GNW:R_GNW + 1, 0:256]
        dtb = prm_ref[R_DTB:R_DTB + 1, 0:128]
        a_e = prm_ref[R_AE:R_AE + 1, 0:256]
        d_e = prm_ref[R_DE:R_DE + 1, 0:256]
        snw = prm_ref[R_SNW:R_SNW + 1, 0:256]
        for c in reversed(range(NCH)):
            rows = pl.ds(c * CH, CH)
            tail = p_ref[rows, C_TL:C_TL + 128]
            k, v, gz = p_ref[rows, C_GK:C_GK + 128], p_ref[rows, C_GV:C_GV + 256], p_ref[rows, C_GZ:C_GZ + 256]
            sg_prev = sg_ref[c]
            pre_g, dec, kd, d_s, s_n, qs, o = _gla_chunk_fwd(p_ref[rows, C_GQ:C_GQ + 128], k, v, tail, gw_v, gb,
                                                            sg_prev, su, mask_t)
            r = lax.rsqrt(_dot2_l(o * o, gmean) + EPS)
            on = o * r
            dyb = dm_ref[rows, 256:512]
            dp_ref[rows, C_GZ:C_GZ + 256] = dyb * on * gnw * _dsilu(gz)
            tg = dyb * _silu(gz)
            gsm_ref[R_GNW:R_GNW + 1, 0:256] += _cs(tg * on)
            don = tg * gnw
            do = r * (don - on * _dot2_l(don * on, gmean))
            dp_ref[rows, C_GQ:C_GQ + 128] = _dot(do, s_n) * GLA_SCALE
            gmat = _dot_tn(do, qs) * mask_t + gg_s[...]
            dds = _cs(gmat * sg_prev)
            dkd = _dot(v, gmat)
            dp_ref[rows, C_GV:C_GV + 256] = _dot_nt(kd, gmat)
            dp_ref[rows, C_GK:C_GK + 128] = dkd * dec
            dla = _dot2_r(sl, dkd * kd) + dds * d_s
            dpre = dla * INV_TAU * jax.nn.sigmoid(-pre_g)
            dtail = _dot_nt(dpre, gw_v)
            dgw_ref[...] += _dot_tn(tail, dpre)
            gsm_ref[R_GB:R_GB + 1, 0:128] += _cs(dpre)
            gg_s[...] = gmat * d_s
            ssz = p_ref[rows, C_SZ:C_SZ + 256]
            ss_prev = ss_ref[c]
            xs, bm, cm, dtin, dte, wdec, w, xw, et, s_n, y = _ssd_chunk_fwd(xc_s[rows, :], tail, dtb, a_e, ss_prev,
                                                                         su, emat)
            y = y + d_e * xs
            sil = _silu(ssz)
            y2 = y * sil
            r = lax.rsqrt(jnp.mean(y2 * y2, axis=-1, keepdims=True) + EPS)
            yn = y2 * r
            dyd = dm_ref[rows, 768:1024]
            gsm_ref[R_SNW:R_SNW + 1, 0:256] += _cs(dyd * yn)
            dn = dyd * snw
            dy2 = r * (dn - yn * jnp.mean(dn * yn, axis=-1, keepdims=True))
            dp_ref[rows, C_SZ:C_SZ + 256] = dy2 * y * _dsilu(ssz)
            dy = dy2 * sil
            gsm_ref[R_DE:R_DE + 1, 0:256] += _cs(dy * xs)
            gcar = gs_s[...]
            dcm, dbm, dxw, gnew = [], [], [], []
            for h in (0, 1):
                hs = slice(128 * h, 128 * h + 128)
                dcm.append(_dot_nt(dy[:, hs], s_n[:, hs]))
                gh = _dot_tn(cm[:, hs], dy[:, hs]) + gcar[:, hs]
                dbm.append(_dot_nt(xw[:, hs], gh))
                dxw.append(_dot(bm[:, hs], gh))
                gnew.append(gh)
            gmat = jnp.concatenate(gnew, axis=1)
            dxw = jnp.concatenate(dxw, axis=1)
            det = _cs(gmat * ss_prev)
            dxs = dy * d_e + dxw * w
            dw = dxw * xs
            ddte = dw * wdec
            dda = _dot2_r(sl, dw * dte * wdec) + det * et
            ddte = ddte + dda * a_e
            gsm_ref[R_AE:R_AE + 1, 0:256] += _cs(dda * dte)
            dtail_s = _dot2_nt(ddte, emat) * jax.nn.sigmoid(dtin)
            gsm_ref[R_DTB:R_DTB + 1, 0:128] += _cs(dtail_s)
            dp_ref[rows, C_TL:C_TL + 128] = dtail + dtail_s
            dxc_s[rows, :] = jnp.concatenate([dxs, dbm[0], dbm[1], dcm[0], dcm[1]], axis=1)
            gs_s[...] = gmat * et
        dpre = dxc_s[...] * _dsilu(pre_s[...])
        dext = jnp.concatenate([dpre, h_dpre[...]], axis=0)
        dp_ref[:, C_SX:C_SX + 768] = (cw[3] * dpre + cw[2] * _up(dext, 1, TB) + cw[1] * _up(dext, 2, TB)
                                      + cw[0] * _up(dext, 3, TB))
        gsm_ref[R_SCW + 3:R_SCW + 4, :] += _cs(dpre * sx)
        gsm_ref[R_SCW + 2:R_SCW + 3, :] += _cs(dpre * sx1)
        gsm_ref[R_SCW + 1:R_SCW + 2, :] += _cs(dpre * sx2)
        gsm_ref[R_SCW:R_SCW + 1, :] += _cs(dpre * sx3)
        gsm_ref[R_SCB:R_SCB + 1, :] += _cs(dpre)
        h_dpre[...] = dpre[0:8, :]

        @pl.when(i == nt - 1)
        def _():
            ri, ci = _iota((256, 256), 0), _iota((256, 256), 1)
            same_head = jnp.where((ri >> 6) == (ci >> 6), 1.0, 0.0).astype(BF16)
            same_dv = jnp.where((ri & 63) == (ci & 63), 1.0, 0.0).astype(BF16)
            row = _iota((8, 256), 0)
            top = gsm_ref[0:8, 0:256]
            gsm_ref[0:8, 0:256] = jnp.where(row == R_GNW, _dot3_l(top, same_dv), top)
            bot = gsm_ref[8:16, 0:256]
            fold = _dot3_l(bot, same_head)
            gsm_ref[8:16, 0:256] = jnp.where(row == R_AE - 8, fold * a_e, jnp.where(row == R_DE - 8, fold, bot))

    return pl.pallas_call(
        body, grid=(nt,), name=name,
        in_specs=[pl.BlockSpec((TB, NP), lambda i: (rev(i), 0)),
                  pl.BlockSpec((16, HALO_W), lambda i: (jnp.maximum(rev(i) * (TB // 16) - 1, 0), 0)),
                  pl.BlockSpec((TB, D), lambda i: (rev(i), 0)),
                  pl.BlockSpec((NCH, 256, 128), lambda i: (rev(i), 0, 0)),
                  pl.BlockSpec((NCH, 128, 256), lambda i: (rev(i), 0, 0)),
                  pl.BlockSpec((16, 768), lambda i: (0, 0)), pl.BlockSpec((128, 128), lambda i: (0, 0)),
                  pl.BlockSpec((256, 256), lambda i: (0, 0))],
        out_specs=[pl.BlockSpec((TB, NP), lambda i: (rev(i), 0)), pl.BlockSpec((16, 768), lambda i: (0, 0)),
                   pl.BlockSpec((128, 128), lambda i: (0, 0)), pl.BlockSpec((256, 256), lambda i: (0, 0))],
        out_shape=[jax.ShapeDtypeStruct((t, NP), F32), jax.ShapeDtypeStruct((16, 768), F32),
                   jax.ShapeDtypeStruct((128, 128), F32), jax.ShapeDtypeStruct((256, 256), F32)],
        scratch_shapes=[pltpu.VMEM((256, 128), F32), pltpu.VMEM((128, 256), F32), pltpu.VMEM((8, 256), F32),
                        pltpu.VMEM((16, 256), F32), pltpu.VMEM((8, 768), F32), pltpu.VMEM((TB, 768), F32),
                        pltpu.VMEM((TB, 768), F32), pltpu.VMEM((TB, 768), F32)],
        compiler_params=_cparams(("arbitrary",)),
    )(proj, proj, dmix, sg, ss, prm, gw, pw)


def _add2(a, b, name, br=256):
    n, r, c = a.shape

    def body(a_ref, b_ref, o_ref):
        o_ref[...] = a_ref[...] + b_ref[...]

    spec = pl.BlockSpec((1, br, c), lambda i, j: (i, j, 0))
    return pl.pallas_call(body, grid=(n, r // br), name=name, in_specs=[spec, spec], out_specs=spec,
                          out_shape=jax.ShapeDtypeStruct(a.shape, F32),
                          compiler_params=_cparams(("parallel", "parallel")))(a, b)


def _sum4(a, name, br=256):
    _, r, c = a.shape

    def body(a_ref, o_ref):
        o_ref[...] = ((a_ref[0] + a_ref[1]) + a_ref[2]) + a_ref[3]

    return pl.pallas_call(body, grid=(r // br,), name=name,
                          in_specs=[pl.BlockSpec((4, br, c), lambda i: (0, i, 0))],
                          out_specs=pl.BlockSpec((br, c), lambda i: (i, 0)),
                          out_shape=jax.ShapeDtypeStruct((r, c), F32),
                          compiler_params=_cparams(("parallel",)))(a)


def _adamw(w, g, m, v, name, br):
    n, r, c = w.shape

    def body(w_ref, g_ref, m_ref, v_ref, d_ref, m2_ref, v2_ref):
        gv = g_ref[...]
        m2 = ADAM_B1 * m_ref[...] + (1.0 - ADAM_B1) * gv
        v2 = ADAM_B2 * v_ref[...] + (1.0 - ADAM_B2) * (gv * gv)
        m_hat = m2 / (1.0 - ADAM_B1 ** ADAM_STEP)
        v_hat = v2 / (1.0 - ADAM_B2 ** ADAM_STEP)
        d_ref[...] = -ADAM_LR * (m_hat / (jnp.sqrt(v_hat) + ADAM_EPS) + ADAM_WD * w_ref[...])
        m2_ref[...] = m2
        v2_ref[...] = v2

    spec = pl.BlockSpec((1, br, c), lambda i, j: (i, j, 0))
    shp = jax.ShapeDtypeStruct(w.shape, F32)
    return pl.pallas_call(body, grid=(n, r // br), name=name, in_specs=[spec] * 4, out_specs=[spec] * 3,
                          out_shape=[shp] * 3, compiler_params=_cparams(("parallel", "parallel")))(w, g, m, v)


def _place():
    return lax.axis_index("x"), lax.axis_index("y"), lax.axis_index("c")


_ANY = pl.BlockSpec(memory_space=pl.ANY)


def _gather_shards(a, b, s):
    srcs = (a, b, s)

    def body(a_ref, b_ref, s_ref, ao, bo, so, send_sems, recv_sems, local_sems):
        x, y, c = _place()
        me = 2 * x + y
        src, dst = (a_ref, b_ref, s_ref), (ao, bo, so)
        own = [pltpu.make_async_copy(src[k], dst[k].at[me], local_sems.at[k]) for k in range(3)]
        for cp in own:
            cp.start()
        chips = [(1 - x, y), (x, 1 - y), (1 - x, 1 - y)]
        sends = []
        for j, (px, py) in enumerate(chips):
            for k in range(3):
                cp = pltpu.make_async_remote_copy(src_ref=src[k], dst_ref=dst[k].at[me], send_sem=send_sems.at[3 * j + k],
                                                  recv_sem=recv_sems.at[3 * j + k], device_id=(px, py, c),
                                                  device_id_type=MESH)
                cp.start()
                sends.append(cp)
        for j, (px, py) in enumerate(chips):
            for k in range(3):
                pltpu.make_async_remote_copy(src_ref=src[k], dst_ref=dst[k].at[2 * px + py],
                                             send_sem=send_sems.at[3 * j + k], recv_sem=recv_sems.at[3 * j + k],
                                             device_id=(px, py, c), device_id_type=MESH).wait_recv()
        for cp in sends:
            cp.wait_send()
        for cp in own:
            cp.wait()

    return pl.pallas_call(
        body, name="gather_weights", in_specs=[_ANY] * 3, out_specs=[_ANY] * 3,
        out_shape=[jax.ShapeDtypeStruct((4,) + t.shape, t.dtype) for t in srcs],
        scratch_shapes=[pltpu.SemaphoreType.DMA((9,)), pltpu.SemaphoreType.DMA((9,)), pltpu.SemaphoreType.DMA((3,))],
    )(*srcs)


def _swap_with_sibling(gin, gout):
    def body(gi, go, ri, ro, send_sems, recv_sems):
        x, y, c = _place()
        cps = [pltpu.make_async_remote_copy(src_ref=s.at[1 - c], dst_ref=d, send_sem=send_sems.at[k], recv_sem=recv_sems.at[k],
                                            device_id=(x, y, 1 - c), device_id_type=MESH)
               for k, (s, d) in enumerate(((gi, ri), (go, ro)))]
        for cp in cps:
            cp.start()
        for cp in cps:
            cp.wait()

    return pl.pallas_call(
        body, name="reduce_swap_sibling", in_specs=[_ANY] * 2, out_specs=[_ANY] * 2,
        out_shape=[jax.ShapeDtypeStruct(gin.shape[1:], F32), jax.ShapeDtypeStruct(gout.shape[1:], F32)],
        scratch_shapes=[pltpu.SemaphoreType.DMA((2,)), pltpu.SemaphoreType.DMA((2,))],
    )(gin, gout)


def _scatter_to_chips(pin, pout):
    def body(pi, po, ri, ro, send_sems, recv_sems, local_sems):
        x, y, c = _place()
        me = 2 * x + y
        pairs = ((pi, ri), (po, ro))
        own = [pltpu.make_async_copy(s.at[me], d.at[me], local_sems.at[k]) for k, (s, d) in enumerate(pairs)]
        for cp in own:
            cp.start()
        chips = [(1 - x, y), (x, 1 - y), (1 - x, 1 - y)]
        sends = []
        for j, (px, py) in enumerate(chips):
            for k, (s, d) in enumerate(pairs):
                cp = pltpu.make_async_remote_copy(src_ref=s.at[2 * px + py], dst_ref=d.at[me], send_sem=send_sems.at[2 * j + k],
                                                  recv_sem=recv_sems.at[2 * j + k], device_id=(px, py, c),
                                                  device_id_type=MESH)
                cp.start()
                sends.append(cp)
        for j, (px, py) in enumerate(chips):
            for k, (s, d) in enumerate(pairs):
                pltpu.make_async_remote_copy(src_ref=s.at[me], dst_ref=d.at[2 * px + py], send_sem=send_sems.at[2 * j + k],
                                             recv_sem=recv_sems.at[2 * j + k], device_id=(px, py, c),
                                             device_id_type=MESH).wait_recv()
        for cp in sends:
            cp.wait_send()
        for cp in own:
            cp.wait()

    return pl.pallas_call(
        body, name="reduce_scatter_chips", in_specs=[_ANY] * 2, out_specs=[_ANY] * 2,
        out_shape=[jax.ShapeDtypeStruct(pin.shape, F32), jax.ShapeDtypeStruct(pout.shape, F32)],
        scratch_shapes=[pltpu.SemaphoreType.DMA((6,)), pltpu.SemaphoreType.DMA((6,)), pltpu.SemaphoreType.DMA((2,))],
    )(pin, pout)


def _share_with_sibling(fin, fout):
    def body(fi, fo, gi, go, send_sems, recv_sems, local_sems):
        x, y, c = _place()
        pairs = ((fi, gi), (fo, go))
        own = [pltpu.make_async_copy(s, d.at[c], local_sems.at[k]) for k, (s, d) in enumerate(pairs)]
        for cp in own:
            cp.start()
        sends = [pltpu.make_async_remote_copy(src_ref=s, dst_ref=d.at[c], send_sem=send_sems.at[k], recv_sem=recv_sems.at[k],
                                              device_id=(x, y, 1 - c), device_id_type=MESH)
                 for k, (s, d) in enumerate(pairs)]
        for cp in sends:
            cp.start()
        for k, (s, d) in enumerate(pairs):
            pltpu.make_async_remote_copy(src_ref=s, dst_ref=d.at[1 - c], send_sem=send_sems.at[k], recv_sem=recv_sems.at[k],
                                         device_id=(x, y, 1 - c), device_id_type=MESH).wait_recv()
        for cp in sends:
            cp.wait_send()
        for cp in own:
            cp.wait()

    return pl.pallas_call(
        body, name="reduce_share_sibling", in_specs=[_ANY] * 2, out_specs=[_ANY] * 2,
        out_shape=[jax.ShapeDtypeStruct((2,) + fin.shape, F32), jax.ShapeDtypeStruct((2,) + fout.shape, F32)],
        scratch_shapes=[pltpu.SemaphoreType.DMA((2,)), pltpu.SemaphoreType.DMA((2,)), pltpu.SemaphoreType.DMA((2,))],
    )(fin, fout)


def _allreduce_small(sm):
    r = sm.shape[0]

    def body(sm_ref, o_ref, rbuf, send_sems, recv_sems):
        x, y, c = _place()
        me = 4 * x + 2 * y + c
        sends = []
        for k in range(1, 8):
            kx, ky, kc = (k >> 2) & 1, (k >> 1) & 1, k & 1
            peer = (1 - x if kx else x, 1 - y if ky else y, 1 - c if kc else c)
            cp = pltpu.make_async_remote_copy(src_ref=sm_ref, dst_ref=rbuf.at[k], send_sem=send_sems.at[k - 1],
                                              recv_sem=recv_sems.at[k - 1], device_id=peer, device_id_type=MESH)
            cp.start()
            sends.append(cp)
        rbuf[0] = sm_ref[...]
        for cp in sends:
            cp.wait()
        acc = rbuf[me]
        for src in range(1, 8):
            acc = acc + rbuf[jnp.bitwise_xor(me, src)]
        o_ref[...] = acc

    return pl.pallas_call(
        body, name="allreduce_small", out_shape=jax.ShapeDtypeStruct((r, 128), F32),
        in_specs=[pl.BlockSpec(memory_space=pltpu.VMEM)], out_specs=pl.BlockSpec(memory_space=pltpu.VMEM),
        scratch_shapes=[pltpu.VMEM((8, r, 128), F32), pltpu.SemaphoreType.DMA((7,)), pltpu.SemaphoreType.DMA((7,))],
    )(sm)


_SMALL = (("norm_w", (2, 1024)), ("conv_a_w", (2, 3, 256)), ("gla_gate_w", (2, 16, 128)), ("gla_gate_b", (2, 128)),
          ("gla_norm_w", (2, 64)), ("pool_w", (2, 4, 64, 64)), ("pool_scale", (2, 256)), ("ssd_conv_w", (2, 4, 768)),
          ("ssd_conv_b", (2, 768)), ("ssd_dt_bias", (2, 4)), ("ssd_a_log", (2, 4)), ("ssd_d", (2, 4)),
          ("ssd_norm_w", (2, 256)), ("final_norm_w", (1024,)))
_SHARDED_SMALL = {"conv_a_w": (2, 3, 64), "ssd_conv_w": (2, 4, 192)}


def _rows_of(shape):
    n = 1
    for s in shape:
        n *= s
    return -(-n // 128)


def _pack(arrays):
    parts = []
    for a in arrays:
        flat = a.reshape(-1).astype(F32)
        parts.append(jnp.pad(flat, (0, _rows_of(a.shape) * 128 - flat.shape[0])).reshape(-1, 128))
    buf = jnp.concatenate(parts, axis=0)
    return jnp.pad(buf, ((0, -buf.shape[0] % 8), (0, 0)))


def _unpack(buf, shapes):
    out, r = [], 0
    for shape in shapes:
        n = 1
        for s in shape:
            n *= s
        rows = _rows_of(shape)
        out.append(buf[r:r + rows].reshape(-1)[:n].reshape(shape))
        r += rows
    return out


def _permute_cols(w):
    parts = [w[..., s:s + n] for s, n in _PERM]
    parts.append(jnp.zeros(w.shape[:-1] + (NP - NPROJ,), w.dtype))
    return jnp.concatenate(parts, axis=-1)


def _unpermute_cols(w):
    return jnp.concatenate([w[..., s:s + n] for s, n in _UNPERM], axis=-1)


def _mixer_consts(layer, conv_a_w, gla_gate_w, gla_gate_b, gla_norm_w, pool_w, pool_scale, ssd_conv_w, ssd_conv_b,
                  ssd_dt_bias, ssd_a_log, ssd_d, ssd_norm_w):
    def row(v):
        return jnp.pad(v.reshape(1, -1), ((0, 0), (0, 768 - v.size)))

    dtb = jnp.zeros((128,), F32).at[16:20].set(ssd_dt_bias[layer])
    rows = [jnp.pad(conv_a_w[layer], ((0, 0), (0, 512))), row(gla_gate_b[layer]), row(jnp.tile(gla_norm_w[layer], 4)),
            row(pool_scale[layer]), row(ssd_conv_b[layer]), row(dtb), row(jnp.repeat(-jnp.exp(ssd_a_log[layer]), 64)),
            row(jnp.repeat(ssd_d[layer], 64)), row(ssd_norm_w[layer]), jnp.zeros((1, 768), F32), ssd_conv_w[layer]]
    prm = jnp.concatenate(rows, axis=0)
    gw = jnp.zeros((128, 128), F32).at[0:16].set(gla_gate_w[layer]).astype(BF16)
    pw = jnp.zeros((256, 256), F32)
    for g in range(4):
        pw = pw.at[64 * g:64 * g + 64, 64 * g:64 * g + 64].set(pool_w[layer, g])
    return prm, gw, pw.astype(BF16)


def _mixer_grads(gsm, dgw, dpw):
    return {
        "conv_a_w": gsm[R_CAW:R_CAW + 3, 0:256], "gla_gate_b": gsm[R_GB, 0:128], "gla_norm_w": gsm[R_GNW, 0:64],
        "pool_scale": gsm[R_PSC, 0:256], "ssd_conv_b": gsm[R_SCB], "ssd_dt_bias": gsm[R_DTB, 16:20],
        "ssd_a_log": gsm[R_AE, 0:256:64], "ssd_d": gsm[R_DE, 0:256:64], "ssd_norm_w": gsm[R_SNW, 0:256],
        "ssd_conv_w": gsm[R_SCW:R_SCW + 4], "gla_gate_w": dgw[0:16],
        "pool_w": jnp.stack([dpw[64 * g:64 * g + 64, 64 * g:64 * g + 64] for g in range(4)]),
    }


def _local_step(x, tgt, norm_w, final_norm_w, wp, wpt, wo, wot, consts):
    xs, projs, mixes, sgs, sss = [x], [], [], [], []
    for l in range(2):
        proj = _rmsproj(xs[l], norm_w[l:l + 1], wp[l], name=f"rmsproj{l}")
        mix, sg, ss = _mixer_fwd(proj, *consts[l], name=f"mixer_fwd{l}")
        xs.append(_outproj(xs[l], mix, wo[l], name=f"outproj{l}"))
        projs.append(proj), mixes.append(mix), sgs.append(sg), sss.append(ss)
    dx, head = _head(xs[2], tgt, final_norm_w.reshape(1, D), name="loss_head")
    dwp, dwo, dnw, mgr = [None, None], [None, None], [None, None], [None, None]
    for l in (1, 0):
        dmix = _matmul_rows(dx, wot[l], name=f"dmix{l}")
        dwo[l] = _dwout(mixes[l], dx, name=f"dwout{l}")
        dproj, gsm, dgw, dpw = _mixer_bwd(projs[l], dmix, sgs[l], sss[l], *consts[l], name=f"mixer_bwd{l}")
        mgr[l] = _mixer_grads(gsm, dgw, dpw)
        dwp[l] = _dwin(xs[l], norm_w[l:l + 1], dproj, name=f"dwin{l}")
        dx, dnw_l = _dxin(dproj, wpt[l], xs[l], dx, norm_w[l:l + 1], name=f"dxin{l}")
        dnw[l] = dnw_l[0]
    return head, dx, jnp.stack(dwp), jnp.stack(dwo), jnp.stack(dnw), mgr


def kernel(x, norm_w, w_in, conv_a_w, gla_gate_w, gla_gate_b, gla_norm_w, pool_w, pool_scale, ssd_conv_w, ssd_conv_b, ssd_dt_bias, ssd_a_log, ssd_d, ssd_norm_w, w_out, final_norm_w, loss_target, m_norm_w, m_w_in, m_conv_a_w, m_gla_gate_w, m_gla_gate_b, m_gla_norm_w, m_pool_w, m_pool_scale, m_ssd_conv_w, m_ssd_conv_b, m_ssd_dt_bias, m_ssd_a_log, m_ssd_d, m_ssd_norm_w, m_w_out, m_final_norm_w, v_norm_w, v_w_in, v_conv_a_w, v_gla_gate_w, v_gla_gate_b, v_gla_norm_w, v_pool_w, v_pool_scale, v_ssd_conv_w, v_ssd_conv_b, v_ssd_dt_bias, v_ssd_a_log, v_ssd_d, v_ssd_norm_w, v_w_out, v_final_norm_w):
    weights = dict(norm_w=norm_w, w_in=w_in, conv_a_w=conv_a_w, gla_gate_w=gla_gate_w, gla_gate_b=gla_gate_b,
                   gla_norm_w=gla_norm_w, pool_w=pool_w, pool_scale=pool_scale, ssd_conv_w=ssd_conv_w,
                   ssd_conv_b=ssd_conv_b, ssd_dt_bias=ssd_dt_bias, ssd_a_log=ssd_a_log, ssd_d=ssd_d,
                   ssd_norm_w=ssd_norm_w, w_out=w_out, final_norm_w=final_norm_w)
    m_in = dict(norm_w=m_norm_w, w_in=m_w_in, conv_a_w=m_conv_a_w, gla_gate_w=m_gla_gate_w, gla_gate_b=m_gla_gate_b,
                gla_norm_w=m_gla_norm_w, pool_w=m_pool_w, pool_scale=m_pool_scale, ssd_conv_w=m_ssd_conv_w,
                ssd_conv_b=m_ssd_conv_b, ssd_dt_bias=m_ssd_dt_bias, ssd_a_log=m_ssd_a_log, ssd_d=m_ssd_d,
                ssd_norm_w=m_ssd_norm_w, w_out=m_w_out, final_norm_w=m_final_norm_w)
    v_in = dict(norm_w=v_norm_w, w_in=v_w_in, conv_a_w=v_conv_a_w, gla_gate_w=v_gla_gate_w, gla_gate_b=v_gla_gate_b,
                gla_norm_w=v_gla_norm_w, pool_w=v_pool_w, pool_scale=v_pool_scale, ssd_conv_w=v_ssd_conv_w,
                ssd_conv_b=v_ssd_conv_b, ssd_dt_bias=v_ssd_dt_bias, ssd_a_log=v_ssd_a_log, ssd_d=v_ssd_d,
                ssd_norm_w=v_ssd_norm_w, w_out=v_w_out, final_norm_w=v_final_norm_w)
    order = ("norm_w", "w_in", "conv_a_w", "gla_gate_w", "gla_gate_b", "gla_norm_w", "pool_w", "pool_scale",
             "ssd_conv_w", "ssd_conv_b", "ssd_dt_bias", "ssd_a_log", "ssd_d", "ssd_norm_w", "w_out", "final_norm_w")
    t = x.shape[1]
    chip = 2 * lax.axis_index("x") + lax.axis_index("y")
    core = lax.axis_index("c")

    cshard = jnp.zeros((16, 256), F32)
    for l in range(2):
        cshard = cshard.at[8 * l:8 * l + 3, 0:64].set(conv_a_w[l]).at[8 * l + 3:8 * l + 7, 0:192].set(ssd_conv_w[l])
    g_in, g_out, g_c = _gather_shards(w_in.astype(BF16), w_out.astype(BF16), cshard)
    w_in_full = jnp.transpose(g_in, (1, 2, 0, 3)).reshape(2, D, NPROJ)
    wp = _permute_cols(w_in_full)
    wpt = jnp.swapaxes(wp, 1, 2)
    wo = jnp.transpose(g_out, (1, 0, 2, 3)).reshape(2, D, D)
    wot = jnp.swapaxes(wo, 1, 2)
    conv_a_full = jnp.stack([jnp.concatenate([g_c[s, 8 * l:8 * l + 3, 0:64] for s in range(4)], axis=-1) for l in range(2)])
    ssd_conv_full = jnp.stack([jnp.concatenate([g_c[s, 8 * l + 3:8 * l + 7, 0:192] for s in range(4)], axis=-1)
                               for l in range(2)])
    consts = [_mixer_consts(l, conv_a_full, gla_gate_w, gla_gate_b, gla_norm_w, pool_w, pool_scale, ssd_conv_full,
                            ssd_conv_b, ssd_dt_bias, ssd_a_log, ssd_d, ssd_norm_w) for l in range(2)]

    head, dx, dwp, dwo, dnw, mgr = _local_step(x.reshape(t, D), loss_target.reshape(t, D), norm_w, final_norm_w,
                                               wp, wpt, wo, wot, consts)

    small = {k: jnp.stack([mgr[0][k], mgr[1][k]]) for k in mgr[0]}
    small["norm_w"] = dnw
    small["final_norm_w"] = head[0]
    red = _allreduce_small(_pack([small[k] for k, _ in _SMALL] + [head[1, 0:1]]))
    red = _unpack(red, [s for _, s in _SMALL] + [(1,)])
    grads = {k: g for (k, _), g in zip(_SMALL, red)}
    loss = red[-1].reshape(())
    grads["conv_a_w"] = lax.dynamic_slice_in_dim(grads["conv_a_w"], chip * 64, 64, axis=2)
    grads["ssd_conv_w"] = lax.dynamic_slice_in_dim(grads["ssd_conv_w"], chip * 192, 192, axis=2)

    gin = jnp.transpose(_unpermute_cols(dwp).reshape(2, D, 4, NPROJ // 4), (0, 2, 1, 3))
    gout = dwo.reshape(2, 4, D // 4, D)
    r_in, r_out = _swap_with_sibling(gin, gout)
    p_in = _add2(lax.dynamic_index_in_dim(gin, core, 0, keepdims=False), r_in, name="reduce_add_pair_in")
    p_out = _add2(lax.dynamic_index_in_dim(gout, core, 0, keepdims=False), r_out, name="reduce_add_pair_out")
    q_in, q_out = _scatter_to_chips(p_in, p_out)
    grads["w_in"], grads["w_out"] = _share_with_sibling(_sum4(q_in, name="reduce_sum_chips_in"),
                                                        _sum4(q_out, name="reduce_sum_chips_out"))

    delta, new_m, new_v = {}, {}, {}
    for k, br in (("w_in", 256), ("w_out", 256)):
        delta[k], new_m[k], new_v[k] = _adamw(weights[k], grads[k], m_in[k], v_in[k], name=f"adamw_{k}", br=br)
    names = [k for k, _ in _SMALL]
    shapes = [_SHARDED_SMALL.get(k, s) for k, s in _SMALL]
    packed = [_pack([d[k] for k in names]) for d in (weights, grads, m_in, v_in)]
    rows = packed[0].shape[0]
    outs = _adamw(*[p.reshape(1, rows, 128) for p in packed], name="adamw_small", br=rows)
    for d, o in zip((delta, new_m, new_v), outs):
        d.update(zip(names, _unpack(o.reshape(rows, 128), shapes)))

    return (loss, dx.reshape(1, t, D), *[grads[k] for k in order], *[delta[k] for k in order],
            *[new_m[k] for k in order], *[new_v[k] for k in order])
```

```python
import functools

import jax
import jax.numpy as jnp
from jax import lax
from jax.experimental import pallas as pl
from jax.experimental.pallas import tpu as pltpu

F32 = jnp.float32
BF16 = jnp.bfloat16
MESH = pl.DeviceIdType.MESH

D = 1024
CH = 64
EPS = 1e-6
NP = 3456
NPROJ = 3348
GLA_SCALE = 32.0 ** -0.5
INV_TAU = 1.0 / 16.0
TB = 256
NCH = TB // CH
HALO_W = 1536

C_SX, C_AH, C_AC, C_PU, C_AB, C_AZ = 0, 768, 1024, 1280, 1536, 1792
C_GQ, C_GK, C_GV, C_GZ, C_PZ, C_SZ, C_TL = 2048, 2176, 2304, 2560, 2816, 3072, 3328
_PERM = ((2576, 768), (0, 256), (512, 256), (1808, 256), (256, 256), (768, 256), (1024, 128), (1152, 128),
         (1280, 256), (1552, 256), (2064, 256), (2320, 256), (1536, 16), (3344, 4))
_UNPERM = ((768, 256), (1536, 256), (1024, 256), (1792, 256), (2048, 128), (2176, 128), (2304, 256), (3328, 16),
           (2560, 256), (1280, 256), (2816, 256), (3072, 256), (0, 768), (3344, 4))

R_CAW, R_GB, R_GNW, R_PSC, R_SCB, R_DTB, R_AE, R_DE, R_SNW, R_SCW = 0, 3, 4, 5, 6, 7, 8, 9, 10, 12

ADAM_LR, ADAM_B1, ADAM_B2, ADAM_EPS, ADAM_WD, ADAM_STEP = 0.001, 0.9, 0.999, 1e-08, 0.01, 10

VMEM_LIMIT = 56 * 1024 * 1024


def _cparams(sem, limit=VMEM_LIMIT):
    return pltpu.CompilerParams(dimension_semantics=sem, vmem_limit_bytes=limit)


def _dot(a, b):
    return jnp.dot(a.astype(BF16), b.astype(BF16), preferred_element_type=F32)


def _dot_nt(a, b):
    return lax.dot_general(a.astype(BF16), b.astype(BF16), (((1,), (1,)), ((), ())), preferred_element_type=F32)


def _dot_tn(a, b):
    return lax.dot_general(a.astype(BF16), b.astype(BF16), (((0,), (0,)), ((), ())), preferred_element_type=F32)


def _split(a):
    hi = a.astype(BF16)
    lo = (a - hi.astype(F32)).astype(BF16)
    return hi, lo


def _dot2_l(a, b):
    hi, lo = _split(a)
    return _dot(hi, b) + _dot(lo, b)


def _dot2_r(a, b):
    hi, lo = _split(b)
    return _dot(a, hi) + _dot(a, lo)


def _dot3_l(a, b):
    hi, lo = _split(a)
    lo2 = ((a - hi.astype(F32)) - lo.astype(F32)).astype(BF16)
    return _dot(hi, b) + _dot(lo, b) + _dot(lo2, b)


def _dot2_nt(a, b):
    hi, lo = _split(a)
    return _dot_nt(hi, b) + _dot_nt(lo, b)


def _silu(z):
    return z * jax.nn.sigmoid(z)


def _dsilu(z):
    s = jax.nn.sigmoid(z)
    return s * (1.0 + z * (1.0 - s))


def _lse1(x):
    return jnp.log(1.0 + jnp.exp(-jnp.abs(x)))


def _cs(a):
    return jnp.sum(a, axis=0, keepdims=True)


def _iota(shape, dim):
    return lax.broadcasted_iota(jnp.int32, shape, dim)


def _expand_mat():
    return jnp.where(_iota((128, 256), 0) - 16 == (_iota((128, 256), 1) >> 6), 1.0, 0.0).astype(BF16)


def _group_mean_mat():
    return jnp.where((_iota((256, 256), 0) >> 6) == (_iota((256, 256), 1) >> 6), 1.0 / 64.0, 0.0).astype(BF16)


def _head_mask_t():
    return jnp.where((_iota((256, 128), 0) >> 6) == (_iota((256, 128), 1) >> 5), 1.0, 0.0).astype(F32)


def _dn(ext, k, n, h):
    return pltpu.roll(ext, k, axis=0)[h:h + n]


def _up(ext, k, n):
    return pltpu.roll(ext, ext.shape[0] - k, axis=0)[:n]


def _pool_lane_select(lane, s2, s4, s8, s16):
    return jnp.where(lane < 64, s2, jnp.where(lane < 128, s4, jnp.where(lane < 192, s8, s16)))


def _winsum_dn(ext, lane):
    s2 = ext + pltpu.roll(ext, 1, axis=0)
    s4 = s2 + pltpu.roll(s2, 2, axis=0)
    s8 = s4 + pltpu.roll(s4, 4, axis=0)
    s16 = s8 + pltpu.roll(s8, 8, axis=0)
    return _pool_lane_select(lane, s2, s4, s8, s16)


def _winsum_up(ext, lane):
    m = ext.shape[0]
    s2 = ext + pltpu.roll(ext, m - 1, axis=0)
    s4 = s2 + pltpu.roll(s2, m - 2, axis=0)
    s8 = s4 + pltpu.roll(s4, m - 4, axis=0)
    s16 = s8 + pltpu.roll(s8, m - 8, axis=0)
    return _pool_lane_select(lane, s2, s4, s8, s16)


def _pool_count(tile, n):
    lane = _iota((1, 256), 1)
    win = _pool_lane_select(lane, 2.0, 4.0, 8.0, 16.0).astype(F32)
    tpos = (tile * n + _iota((n, 1), 0) + 1).astype(F32)
    return jnp.minimum(tpos, win)


def _chunk_tri(n, upper):
    r, c = _iota((n, n), 0), _iota((n, n), 1)
    tri = (c > r) if upper else (c < r)
    return jnp.where(tri & ((r >> 6) == (c >> 6)), 1.0, 0.0).astype(BF16)


def _chunks(a):
    return [a[c * CH:(c + 1) * CH] for c in range(a.shape[0] // CH)]


def _halves(fn, a, b):
    return jnp.concatenate([fn(a[:, 0:128], b[:, 0:128]), fn(a[:, 128:256], b[:, 128:256])], axis=1)


def _mixer_tile_prep(p_ref, xc, prm_ref, gw_v):
    tail = p_ref[:, C_TL:C_TL + 128]
    pre = _dot(tail, gw_v) + prm_ref[R_GB:R_GB + 1, 0:128]
    la = (jnp.minimum(pre, 0.0) - _lse1(pre)) * INV_TAU
    dtin = tail + prm_ref[R_DTB:R_DTB + 1, 0:128]
    dtf = jnp.maximum(dtin, 0.0) + _lse1(dtin)
    dte = _dot2_l(dtf, _expand_mat())
    da = dte * prm_ref[R_AE:R_AE + 1, 0:256]
    rev = _dot2_r(_chunk_tri(TB, True), jnp.concatenate([la, da], axis=1))
    dec = jnp.exp(rev[:, 0:128])
    kd = p_ref[:, C_GK:C_GK + 128] * dec
    wdec = jnp.exp(rev[:, 128:384])
    w = wdec * dte
    xw = xc[:, 0:256] * w
    d_s = [jnp.exp(_cs(a)) for a in _chunks(la)]
    et = [jnp.exp(_cs(a)) for a in _chunks(da)]
    mask_t = _head_mask_t()
    ut_g = [_dot_tn(v, k) * mask_t for v, k in zip(_chunks(p_ref[:, C_GV:C_GV + 256]), _chunks(kd))]
    ut_s = [_halves(_dot_tn, b, x) for b, x in zip(_chunks(xc[:, 256:512]), _chunks(xw))]
    return tail, pre, dtin, dte, dec, kd, wdec, w, xw, d_s, et, ut_g, ut_s


def _rmsproj(x, nw, wp, name, tm=256):
    t = x.shape[0]

    def body(x_ref, nw_ref, w_ref, o_ref):
        xv = x_ref[...]
        rs = lax.rsqrt(jnp.mean(xv * xv, axis=-1, keepdims=True) + EPS)
        h = (xv * rs * nw_ref[...]).astype(BF16)
        o_ref[...] = jnp.dot(h, w_ref[...], preferred_element_type=F32)

    return pl.pallas_call(
        body, grid=(t // tm,), name=name,
        in_specs=[pl.BlockSpec((tm, D), lambda i: (i, 0)), pl.BlockSpec((1, D), lambda i: (0, 0)),
                  pl.BlockSpec((D, NP), lambda i: (0, 0))],
        out_specs=pl.BlockSpec((tm, NP), lambda i: (i, 0)),
        out_shape=jax.ShapeDtypeStruct((t, NP), F32),
        compiler_params=_cparams(("parallel",)),
    )(x, nw, wp)


def _outproj(x, mix, wo, name, tm=512):
    t = x.shape[0]

    def body(x_ref, m_ref, w_ref, o_ref):
        o_ref[...] = x_ref[...] + jnp.dot(m_ref[...].astype(BF16), w_ref[...], preferred_element_type=F32)

    return pl.pallas_call(
        body, grid=(t // tm,), name=name,
        in_specs=[pl.BlockSpec((tm, D), lambda i: (i, 0)), pl.BlockSpec((tm, D), lambda i: (i, 0)),
                  pl.BlockSpec((D, D), lambda i: (0, 0))],
        out_specs=pl.BlockSpec((tm, D), lambda i: (i, 0)),
        out_shape=jax.ShapeDtypeStruct((t, D), F32),
        compiler_params=_cparams(("parallel",)),
    )(x, mix, wo)


def _matmul_rows(a, w, name, tm=512):
    t, kdim = a.shape
    n = w.shape[1]

    def body(a_ref, w_ref, o_ref):
        o_ref[...] = jnp.dot(a_ref[...].astype(BF16), w_ref[...], preferred_element_type=F32)

    return pl.pallas_call(
        body, grid=(t // tm,), name=name,
        in_specs=[pl.BlockSpec((tm, kdim), lambda i: (i, 0)), pl.BlockSpec((kdim, n), lambda i: (0, 0))],
        out_specs=pl.BlockSpec((tm, n), lambda i: (i, 0)),
        out_shape=jax.ShapeDtypeStruct((t, n), F32),
        compiler_params=_cparams(("parallel",)),
    )(a, w)


def _head(x, tgt, fw, name, tm=512):
    t = x.shape[0]

    def body(x_ref, t_ref, w_ref, dx_ref, acc_ref):
        @pl.when(pl.program_id(0) == 0)
        def _():
            acc_ref[...] = jnp.zeros_like(acc_ref)

        xv = x_ref[...]
        w = w_ref[...]
        rs = lax.rsqrt(jnp.mean(xv * xv, axis=-1, keepdims=True) + EPS)
        xh = xv * rs
        err = xh * w - t_ref[...]
        dy = err * (1.0 / D)
        dxh = dy * w
        dx_ref[...] = rs * (dxh - xh * jnp.mean(dxh * xh, axis=-1, keepdims=True))
        acc_ref[0:1, :] += _cs(dy * xh)
        acc_ref[1:2, :] += jnp.zeros((1, D), F32) + (0.5 / D) * jnp.sum(err * err)

    return pl.pallas_call(
        body, grid=(t // tm,), name=name,
        in_specs=[pl.BlockSpec((tm, D), lambda i: (i, 0)), pl.BlockSpec((tm, D), lambda i: (i, 0)),
                  pl.BlockSpec((1, D), lambda i: (0, 0))],
        out_specs=[pl.BlockSpec((tm, D), lambda i: (i, 0)), pl.BlockSpec((8, D), lambda i: (0, 0))],
        out_shape=[jax.ShapeDtypeStruct((t, D), F32), jax.ShapeDtypeStruct((8, D), F32)],
        compiler_params=_cparams(("arbitrary",)),
    )(x, tgt, fw)


def _dxin(dp, wpt, x, dxn, nw, name, tm=256):
    t = x.shape[0]

    def body(dp_ref, w_ref, x_ref, dxn_ref, nw_ref, dx_ref, dnw_ref):
        @pl.when(pl.program_id(0) == 0)
        def _():
            dnw_ref[...] = jnp.zeros_like(dnw_ref)

        dh = jnp.dot(dp_ref[...].astype(BF16), w_ref[...], preferred_element_type=F32)
        xv = x_ref[...]
        rs = lax.rsqrt(jnp.mean(xv * xv, axis=-1, keepdims=True) + EPS)
        xh = xv * rs
        dnw_ref[0:1, :] += _cs(dh * xh)
        dxh = dh * nw_ref[...]
        dx_ref[...] = dxn_ref[...] + rs * (dxh - xh * jnp.mean(dxh * xh, axis=-1, keepdims=True))

    return pl.pallas_call(
        body, grid=(t // tm,), name=name,
        in_specs=[pl.BlockSpec((tm, NP), lambda i: (i, 0)), pl.BlockSpec((NP, D), lambda i: (0, 0)),
                  pl.BlockSpec((tm, D), lambda i: (i, 0)), pl.BlockSpec((tm, D), lambda i: (i, 0)),
                  pl.BlockSpec((1, D), lambda i: (0, 0))],
        out_specs=[pl.BlockSpec((tm, D), lambda i: (i, 0)), pl.BlockSpec((8, D), lambda i: (0, 0))],
        out_shape=[jax.ShapeDtypeStruct((t, D), F32), jax.ShapeDtypeStruct((8, D), F32)],
        compiler_params=_cparams(("arbitrary",)),
    )(dp, wpt, x, dxn, nw)


def _dwin(x, nw, dp, name, tm=512, tn=1152):
    t = x.shape[0]

    def body(x_ref, nw_ref, dp_ref, o_ref):
        @pl.when(pl.program_id(1) == 0)
        def _():
            o_ref[...] = jnp.zeros_like(o_ref)

        xv = x_ref[...]
        rs = lax.rsqrt(jnp.mean(xv * xv, axis=-1, keepdims=True) + EPS)
        h = xv * rs * nw_ref[...]
        o_ref[...] += _dot_tn(h, dp_ref[...])

    return pl.pallas_call(
        body, grid=(NP // tn, t // tm), name=name,
        in_specs=[pl.BlockSpec((tm, D), lambda j, i: (i, 0)), pl.BlockSpec((1, D), lambda j, i: (0, 0)),
                  pl.BlockSpec((tm, tn), lambda j, i: (i, j))],
        out_specs=pl.BlockSpec((D, tn), lambda j, i: (0, j)),
        out_shape=jax.ShapeDtypeStruct((D, NP), F32),
        compiler_params=_cparams(("parallel", "arbitrary")),
    )(x, nw, dp)


def _dwout(mix, dxn, name, tm=512):
    t = mix.shape[0]

    def body(m_ref, g_ref, o_ref):
        @pl.when(pl.program_id(0) == 0)
        def _():
            o_ref[...] = jnp.zeros_like(o_ref)

        o_ref[...] += _dot_tn(m_ref[...], g_ref[...])

    return pl.pallas_call(
        body, grid=(t // tm,), name=name,
        in_specs=[pl.BlockSpec((tm, D), lambda i: (i, 0)), pl.BlockSpec((tm, D), lambda i: (i, 0))],
        out_specs=pl.BlockSpec((D, D), lambda i: (0, 0)),
        out_shape=jax.ShapeDtypeStruct((D, D), F32),
        compiler_params=_cparams(("arbitrary",)),
    )(mix, dxn)


def _mixer_fwd(proj, prm, gw, pw, name):
    t = proj.shape[0]
    nt, nc = t // TB, t // CH

    def body(p_ref, prm_ref, gw_ref, pw_ref, mix_ref, sg_ref, ss_ref, sg_s, ss_s, h_ua, h_pu, h_sx):
        i = pl.program_id(0)

        @pl.when(i == 0)
        def _():
            for r in (sg_s, ss_s, h_ua, h_pu, h_sx):
                r[...] = jnp.zeros_like(r)

        lane = _iota((1, 256), 1)
        u = p_ref[:, C_AC:C_AC + 256] * p_ref[:, C_AH:C_AH + 256]
        ext = jnp.concatenate([h_ua[...], u], axis=0)
        cv = (prm_ref[R_CAW + 2:R_CAW + 3, 0:256] * u + prm_ref[R_CAW + 1:R_CAW + 2, 0:256] * _dn(ext, 1, TB, 8)
              + prm_ref[R_CAW:R_CAW + 1, 0:256] * _dn(ext, 2, TB, 8))
        mix_ref[:, 0:256] = p_ref[:, C_AB:C_AB + 256] * cv * _silu(p_ref[:, C_AZ:C_AZ + 256])
        h_ua[...] = u[TB - 8:, :]
        pu = p_ref[:, C_PU:C_PU + 256]
        ext = jnp.concatenate([h_pu[...], pu], axis=0)
        pooled = _winsum_dn(ext, lane)[16:] / _pool_count(i, TB) - pu
        mixed = _dot(pooled, pw_ref[...])
        mix_ref[:, 512:768] = prm_ref[R_PSC:R_PSC + 1, 0:256] * mixed * _silu(p_ref[:, C_PZ:C_PZ + 256])
        h_pu[...] = pu[TB - 16:, :]
        sx = p_ref[:, C_SX:C_SX + 768]
        ext = jnp.concatenate([h_sx[...], sx], axis=0)
        xc = _silu(prm_ref[R_SCW + 3:R_SCW + 4, :] * sx + prm_ref[R_SCW + 2:R_SCW + 3, :] * _dn(ext, 1, TB, 8)
                   + prm_ref[R_SCW + 1:R_SCW + 2, :] * _dn(ext, 2, TB, 8) + prm_ref[R_SCW:R_SCW + 1, :] * _dn(ext, 3, TB, 8)
                   + prm_ref[R_SCB:R_SCB + 1, :])
        h_sx[...] = sx[TB - 8:, :]

        _, _, _, _, _, _, _, _, _, d_s, et, ut_g, ut_s = _mixer_tile_prep(p_ref, xc, prm_ref, gw_ref[...])
        s_g, s_s = sg_s[...], ss_s[...]
        o, y = [], []
        qs = _chunks(p_ref[:, C_GQ:C_GQ + 128] * GLA_SCALE)
        cm = _chunks(xc[:, 512:768])
        for c in range(NCH):
            sg_ref[c] = s_g
            ss_ref[c] = s_s
            s_g = s_g * d_s[c] + ut_g[c]
            s_s = s_s * et[c] + ut_s[c]
            o.append(_dot_nt(qs[c], s_g))
            y.append(_halves(_dot, cm[c], s_s))
        sg_s[...] = s_g
        ss_s[...] = s_s
        o = jnp.concatenate(o, axis=0)
        on = o * lax.rsqrt(_dot2_l(o * o, _group_mean_mat()) + EPS)
        mix_ref[:, 256:512] = on * prm_ref[R_GNW:R_GNW + 1, 0:256] * _silu(p_ref[:, C_GZ:C_GZ + 256])
        y2 = ((jnp.concatenate(y, axis=0) + prm_ref[R_DE:R_DE + 1, 0:256] * xc[:, 0:256])
              * _silu(p_ref[:, C_SZ:C_SZ + 256]))
        mix_ref[:, 768:1024] = (y2 * lax.rsqrt(jnp.mean(y2 * y2, axis=-1, keepdims=True) + EPS)
                                * prm_ref[R_SNW:R_SNW + 1, 0:256])

    return pl.pallas_call(
        body, grid=(nt,), name=name,
        in_specs=[pl.BlockSpec((TB, NP), lambda i: (i, 0)), pl.BlockSpec((16, 768), lambda i: (0, 0)),
                  pl.BlockSpec((128, 128), lambda i: (0, 0)), pl.BlockSpec((256, 256), lambda i: (0, 0))],
        out_specs=[pl.BlockSpec((TB, D), lambda i: (i, 0)), pl.BlockSpec((NCH, 256, 128), lambda i: (i, 0, 0)),
                   pl.BlockSpec((NCH, 128, 256), lambda i: (i, 0, 0))],
        out_shape=[jax.ShapeDtypeStruct((t, D), F32), jax.ShapeDtypeStruct((nc, 256, 128), F32),
                   jax.ShapeDtypeStruct((nc, 128, 256), F32)],
        scratch_shapes=[pltpu.VMEM((256, 128), F32), pltpu.VMEM((128, 256), F32), pltpu.VMEM((8, 256), F32),
                        pltpu.VMEM((16, 256), F32), pltpu.VMEM((8, 768), F32)],
        compiler_params=_cparams(("arbitrary",)),
    )(proj, prm, gw, pw)


def _mixer_bwd(proj, dmix, sg, ss, prm, gw, pw, name):
    t = proj.shape[0]
    nt = t // TB
    rev = lambda i: nt - 1 - i

    def body(p_ref, hp_ref, dm_ref, sg_ref, ss_ref, prm_ref, gw_ref, pw_ref, dp_ref, gsm_ref, dgw_ref, dpw_ref,
             gg_s, gs_s, h_dcv, h_dpl, h_dpre):
        i = pl.program_id(0)
        tile = nt - 1 - i

        @pl.when(i == 0)
        def _():
            for r in (gg_s, gs_s, h_dcv, h_dpl, h_dpre, gsm_ref, dgw_ref, dpw_ref):
                r[...] = jnp.zeros_like(r)

        lane = _iota((1, 256), 1)
        first = (tile > 0).astype(F32)
        ah, ac = p_ref[:, C_AH:C_AH + 256], p_ref[:, C_AC:C_AC + 256]
        ab, az = p_ref[:, C_AB:C_AB + 256], p_ref[:, C_AZ:C_AZ + 256]
        w0, w1, w2 = (prm_ref[R_CAW + j:R_CAW + j + 1, 0:256] for j in range(3))
        u = ac * ah
        ext = jnp.concatenate([hp_ref[8:16, C_AC:C_AC + 256] * hp_ref[8:16, C_AH:C_AH + 256] * first, u], axis=0)
        u1, u2 = _dn(ext, 1, TB, 8), _dn(ext, 2, TB, 8)
        cv = w2 * u + w1 * u1 + w0 * u2
        g = dm_ref[:, 0:256]
        sz = _silu(az)
        dp_ref[:, C_AB:C_AB + 256] = g * cv * sz
        dp_ref[:, C_AZ:C_AZ + 256] = g * ab * cv * _dsilu(az)
        dcv = g * ab * sz
        dext = jnp.concatenate([dcv, h_dcv[...]], axis=0)
        du = w2 * dcv + w1 * _up(dext, 1, TB) + w0 * _up(dext, 2, TB)
        dp_ref[:, C_AC:C_AC + 256] = du * ah
        dp_ref[:, C_AH:C_AH + 256] = du * ac
        gsm_ref[R_CAW:R_CAW + 1, 0:256] += _cs(dcv * u2)
        gsm_ref[R_CAW + 1:R_CAW + 2, 0:256] += _cs(dcv * u1)
        gsm_ref[R_CAW + 2:R_CAW + 3, 0:256] += _cs(dcv * u)
        h_dcv[...] = dcv[0:8, :]
        pu, pz = p_ref[:, C_PU:C_PU + 256], p_ref[:, C_PZ:C_PZ + 256]
        psc = prm_ref[R_PSC:R_PSC + 1, 0:256]
        cnt = _pool_count(tile, TB)
        ext = jnp.concatenate([hp_ref[:, C_PU:C_PU + 256] * first, pu], axis=0)
        pooled = _winsum_dn(ext, lane)[16:] / cnt - pu
        pw_v = pw_ref[...]
        mixed = _dot(pooled, pw_v)
        g = dm_ref[:, 512:768]
        sz = _silu(pz)
        gsm_ref[R_PSC:R_PSC + 1, 0:256] += _cs(g * mixed * sz)
        dp_ref[:, C_PZ:C_PZ + 256] = g * psc * mixed * _dsilu(pz)
        dmixed = g * psc * sz
        dpw_ref[...] += _dot_tn(pooled, dmixed)
        dpooled = _dot_nt(dmixed, pw_v)
        qd = dpooled / cnt
        dext = jnp.concatenate([qd, h_dpl[...]], axis=0)
        dp_ref[:, C_PU:C_PU + 256] = _winsum_up(dext, lane)[:TB] - dpooled
        h_dpl[...] = qd[0:16, :]
        sx = p_ref[:, C_SX:C_SX + 768]
        cw = [prm_ref[R_SCW + j:R_SCW + j + 1, :] for j in range(4)]
        ext = jnp.concatenate([hp_ref[8:16, C_SX:C_SX + 768] * first, sx], axis=0)
        sx1, sx2, sx3 = _dn(ext, 1, TB, 8), _dn(ext, 2, TB, 8), _dn(ext, 3, TB, 8)
        cpre = cw[3] * sx + cw[2] * sx1 + cw[1] * sx2 + cw[0] * sx3 + prm_ref[R_SCB:R_SCB + 1, :]
        xc = _silu(cpre)
        xs, bm, cm = xc[:, 0:256], xc[:, 256:512], xc[:, 512:768]

        gw_v = gw_ref[...]
        tail, pre, dtin, dte, dec, kd, wdec, w, xw, d_s, et, ut_g, ut_s = _mixer_tile_prep(p_ref, xc, prm_ref, gw_v)
        gmean = _group_mean_mat()
        mask_t = _head_mask_t()
        gnw = prm_ref[R_GNW:R_GNW + 1, 0:256]
        a_e = prm_ref[R_AE:R_AE + 1, 0:256]
        d_e = prm_ref[R_DE:R_DE + 1, 0:256]
        snw = prm_ref[R_SNW:R_SNW + 1, 0:256]
        sg_in = [sg_ref[c] for c in range(NCH)]
        ss_in = [ss_ref[c] for c in range(NCH)]
        sg_n = [sg_in[c] * d_s[c] + ut_g[c] for c in range(NCH)]
        ss_n = [ss_in[c] * et[c] + ut_s[c] for c in range(NCH)]
        qs = _chunks(p_ref[:, C_GQ:C_GQ + 128] * GLA_SCALE)
        cm_c, bm_c, xw_c, kd_c = _chunks(cm), _chunks(bm), _chunks(xw), _chunks(kd)
        v_c = _chunks(p_ref[:, C_GV:C_GV + 256])
        o = jnp.concatenate([_dot_nt(qs[c], sg_n[c]) for c in range(NCH)], axis=0)
        y = jnp.concatenate([_halves(_dot, cm_c[c], ss_n[c]) for c in range(NCH)], axis=0) + d_e * xs
        gz = p_ref[:, C_GZ:C_GZ + 256]
        r = lax.rsqrt(_dot2_l(o * o, gmean) + EPS)
        on = o * r
        dyb = dm_ref[:, 256:512]
        dp_ref[:, C_GZ:C_GZ + 256] = dyb * on * gnw * _dsilu(gz)
        tg = dyb * _silu(gz)
        gsm_ref[R_GNW:R_GNW + 1, 0:256] += _cs(tg * on)
        don = tg * gnw
        do_c = _chunks(r * (don - on * _dot2_l(don * on, gmean)))
        ssz = p_ref[:, C_SZ:C_SZ + 256]
        sil = _silu(ssz)
        y2 = y * sil
        r = lax.rsqrt(jnp.mean(y2 * y2, axis=-1, keepdims=True) + EPS)
        yn = y2 * r
        dyd = dm_ref[:, 768:1024]
        gsm_ref[R_SNW:R_SNW + 1, 0:256] += _cs(dyd * yn)
        dn = dyd * snw
        dy2 = r * (dn - yn * jnp.mean(dn * yn, axis=-1, keepdims=True))
        dp_ref[:, C_SZ:C_SZ + 256] = dy2 * y * _dsilu(ssz)
        dy = dy2 * sil
        gsm_ref[R_DE:R_DE + 1, 0:256] += _cs(dy * xs)
        dy_c = _chunks(dy)
        dp_ref[:, C_GQ:C_GQ + 128] = jnp.concatenate([_dot(do_c[c], sg_n[c]) for c in range(NCH)], axis=0) * GLA_SCALE
        dcm = jnp.concatenate([_halves(_dot_nt, dy_c[c], ss_n[c]) for c in range(NCH)], axis=0)
        gg = [_dot_tn(do_c[c], qs[c]) * mask_t for c in range(NCH)]
        gs = [_halves(_dot_tn, cm_c[c], dy_c[c]) for c in range(NCH)]
        car_g, car_s = gg_s[...], gs_s[...]
        for c in reversed(range(NCH)):
            gg[c] = gg[c] + car_g
            gs[c] = gs[c] + car_s
            car_g = gg[c] * d_s[c]
            car_s = gs[c] * et[c]
        gg_s[...] = car_g
        gs_s[...] = car_s
        dkd = jnp.concatenate([_dot(v_c[c], gg[c]) for c in range(NCH)], axis=0)
        dp_ref[:, C_GV:C_GV + 256] = jnp.concatenate([_dot_nt(kd_c[c], gg[c]) for c in range(NCH)], axis=0)
        dp_ref[:, C_GK:C_GK + 128] = dkd * dec
        dbm = jnp.concatenate([_halves(_dot_nt, xw_c[c], gs[c]) for c in range(NCH)], axis=0)
        dxw = jnp.concatenate([_halves(_dot, bm_c[c], gs[c]) for c in range(NCH)], axis=0)
        dxs = dy * d_e + dxw * w
        dw = dxw * xs
        dsuf = _dot2_r(_chunk_tri(TB, False), jnp.concatenate([dkd * kd, dw * dte * wdec], axis=1))
        tot_g = jnp.concatenate([jnp.broadcast_to(_cs(gg[c] * sg_in[c]) * d_s[c], (CH, 128)) for c in range(NCH)], axis=0)
        tot_s = jnp.concatenate([jnp.broadcast_to(_cs(gs[c] * ss_in[c]) * et[c], (CH, 256)) for c in range(NCH)], axis=0)
        dpre = (dsuf[:, 0:128] + tot_g) * INV_TAU * jax.nn.sigmoid(-pre)
        dgw_ref[...] += _dot_tn(tail, dpre)
        gsm_ref[R_GB:R_GB + 1, 0:128] += _cs(dpre)
        dda = dsuf[:, 128:384] + tot_s
        gsm_ref[R_AE:R_AE + 1, 0:256] += _cs(dda * dte)
        dtail_s = _dot2_nt(dw * wdec + dda * a_e, _expand_mat()) * jax.nn.sigmoid(dtin)
        gsm_ref[R_DTB:R_DTB + 1, 0:128] += _cs(dtail_s)
        dp_ref[:, C_TL:C_TL + 128] = _dot_nt(dpre, gw_v) + dtail_s
        dpre_c = jnp.concatenate([dxs, dbm, dcm], axis=1) * _dsilu(cpre)
        dext = jnp.concatenate([dpre_c, h_dpre[...]], axis=0)
        dp_ref[:, C_SX:C_SX + 768] = (cw[3] * dpre_c + cw[2] * _up(dext, 1, TB) + cw[1] * _up(dext, 2, TB)
                                      + cw[0] * _up(dext, 3, TB))
        gsm_ref[R_SCW + 3:R_SCW + 4, :] += _cs(dpre_c * sx)
        gsm_ref[R_SCW + 2:R_SCW + 3, :] += _cs(dpre_c * sx1)
        gsm_ref[R_SCW + 1:R_SCW + 2, :] += _cs(dpre_c * sx2)
        gsm_ref[R_SCW:R_SCW + 1, :] += _cs(dpre_c * sx3)
        gsm_ref[R_SCB:R_SCB + 1, :] += _cs(dpre_c)
        h_dpre[...] = dpre_c[0:8, :]

        @pl.when(i == nt - 1)
        def _():
            ri, ci = _iota((256, 256), 0), _iota((256, 256), 1)
            same_head = jnp.where((ri >> 6) == (ci >> 6), 1.0, 0.0).astype(BF16)
            same_dv = jnp.where((ri & 63) == (ci & 63), 1.0, 0.0).astype(BF16)
            row = _iota((8, 256), 0)
            top = gsm_ref[0:8, 0:256]
            gsm_ref[0:8, 0:256] = jnp.where(row == R_GNW, _dot3_l(top, same_dv), top)
            bot = gsm_ref[8:16, 0:256]
            fold = _dot3_l(bot, same_head)
            gsm_ref[8:16, 0:256] = jnp.where(row == R_AE - 8, fold * a_e, jnp.where(row == R_DE - 8, fold, bot))

    return pl.pallas_call(
        body, grid=(nt,), name=name,
        in_specs=[pl.BlockSpec((TB, NP), lambda i: (rev(i), 0)),
                  pl.BlockSpec((16, HALO_W), lambda i: (jnp.maximum(rev(i) * (TB // 16) - 1, 0), 0)),
                  pl.BlockSpec((TB, D), lambda i: (rev(i), 0)),
                  pl.BlockSpec((NCH, 256, 128), lambda i: (rev(i), 0, 0)),
                  pl.BlockSpec((NCH, 128, 256), lambda i: (rev(i), 0, 0)),
                  pl.BlockSpec((16, 768), lambda i: (0, 0)), pl.BlockSpec((128, 128), lambda i: (0, 0)),
                  pl.BlockSpec((256, 256), lambda i: (0, 0))],
        out_specs=[pl.BlockSpec((TB, NP), lambda i: (rev(i), 0)), pl.BlockSpec((16, 768), lambda i: (0, 0)),
                   pl.BlockSpec((128, 128), lambda i: (0, 0)), pl.BlockSpec((256, 256), lambda i: (0, 0))],
        out_shape=[jax.ShapeDtypeStruct((t, NP), F32), jax.ShapeDtypeStruct((16, 768), F32),
                   jax.ShapeDtypeStruct((128, 128), F32), jax.ShapeDtypeStruct((256, 256), F32)],
        scratch_shapes=[pltpu.VMEM((256, 128), F32), pltpu.VMEM((128, 256), F32), pltpu.VMEM((8, 256), F32),
                        pltpu.VMEM((16, 256), F32), pltpu.VMEM((8, 768), F32)],
        compiler_params=_cparams(("arbitrary",)),
    )(proj, proj, dmix, sg, ss, prm, gw, pw)


def _add2(a, b, name, br=256):
    n, r, c = a.shape

    def body(a_ref, b_ref, o_ref):
        o_ref[...] = a_ref[...] + b_ref[...]

    spec = pl.BlockSpec((1, br, c), lambda i, j: (i, j, 0))
    return pl.pallas_call(body, grid=(n, r // br), name=name, in_specs=[spec, spec], out_specs=spec,
                          out_shape=jax.ShapeDtypeStruct(a.shape, F32),
                          compiler_params=_cparams(("parallel", "parallel")))(a, b)


def _sum4(a, name, br=256):
    _, r, c = a.shape

    def body(a_ref, o_ref):
        o_ref[...] = ((a_ref[0] + a_ref[1]) + a_ref[2]) + a_ref[3]

    return pl.pallas_call(body, grid=(r // br,), name=name,
                          in_specs=[pl.BlockSpec((4, br, c), lambda i: (0, i, 0))],
                          out_specs=pl.BlockSpec((br, c), lambda i: (i, 0)),
                          out_shape=jax.ShapeDtypeStruct((r, c), F32),
                          compiler_params=_cparams(("parallel",)))(a)


def _adamw(w, g, m, v, name, br):
    n, r, c = w.shape

    def body(w_ref, g_ref, m_ref, v_ref, d_ref, m2_ref, v2_ref):
        gv = g_ref[...]
        m2 = ADAM_B1 * m_ref[...] + (1.0 - ADAM_B1) * gv
        v2 = ADAM_B2 * v_ref[...] + (1.0 - ADAM_B2) * (gv * gv)
        m_hat = m2 / (1.0 - ADAM_B1 ** ADAM_STEP)
        v_hat = v2 / (1.0 - ADAM_B2 ** ADAM_STEP)
        d_ref[...] = -ADAM_LR * (m_hat / (jnp.sqrt(v_hat) + ADAM_EPS) + ADAM_WD * w_ref[...])
        m2_ref[...] = m2
        v2_ref[...] = v2

    spec = pl.BlockSpec((1, br, c), lambda i, j: (i, j, 0))
    shp = jax.ShapeDtypeStruct(w.shape, F32)
    return pl.pallas_call(body, grid=(n, r // br), name=name, in_specs=[spec] * 4, out_specs=[spec] * 3,
                          out_shape=[shp] * 3, compiler_params=_cparams(("parallel", "parallel")))(w, g, m, v)


def _place():
    return lax.axis_index("x"), lax.axis_index("y"), lax.axis_index("c")


_ANY = pl.BlockSpec(memory_space=pl.ANY)


def _gather_shards(a, b, s):
    srcs = (a, b, s)

    def body(a_ref, b_ref, s_ref, ao, bo, so, send_sems, recv_sems, local_sems):
        x, y, c = _place()
        me = 2 * x + y
        src, dst = (a_ref, b_ref, s_ref), (ao, bo, so)
        own = [pltpu.make_async_copy(src[k], dst[k].at[me], local_sems.at[k]) for k in range(3)]
        for cp in own:
            cp.start()
        chips = [(1 - x, y), (x, 1 - y), (1 - x, 1 - y)]
        sends = []
        for j, (px, py) in enumerate(chips):
            for k in range(3):
                cp = pltpu.make_async_remote_copy(src_ref=src[k], dst_ref=dst[k].at[me], send_sem=send_sems.at[3 * j + k],
                                                  recv_sem=recv_sems.at[3 * j + k], device_id=(px, py, c),
                                                  device_id_type=MESH)
                cp.start()
                sends.append(cp)
        for j, (px, py) in enumerate(chips):
            for k in range(3):
                pltpu.make_async_remote_copy(src_ref=src[k], dst_ref=dst[k].at[2 * px + py],
                                             send_sem=send_sems.at[3 * j + k], recv_sem=recv_sems.at[3 * j + k],
                                             device_id=(px, py, c), device_id_type=MESH).wait_recv()
        for cp in sends:
            cp.wait_send()
        for cp in own:
            cp.wait()

    return pl.pallas_call(
        body, name="gather_weights", in_specs=[_ANY] * 3, out_specs=[_ANY] * 3,
        out_shape=[jax.ShapeDtypeStruct((4,) + t.shape, t.dtype) for t in srcs],
        scratch_shapes=[pltpu.SemaphoreType.DMA((9,)), pltpu.SemaphoreType.DMA((9,)), pltpu.SemaphoreType.DMA((3,))],
    )(*srcs)


def _swap_with_sibling(gin, gout):
    def body(gi, go, ri, ro, send_sems, recv_sems):
        x, y, c = _place()
        cps = [pltpu.make_async_remote_copy(src_ref=s.at[1 - c], dst_ref=d, send_sem=send_sems.at[k], recv_sem=recv_sems.at[k],
                                            device_id=(x, y, 1 - c), device_id_type=MESH)
               for k, (s, d) in enumerate(((gi, ri), (go, ro)))]
        for cp in cps:
            cp.start()
        for cp in cps:
            cp.wait()

    return pl.pallas_call(
        body, name="reduce_swap_sibling", in_specs=[_ANY] * 2, out_specs=[_ANY] * 2,
        out_shape=[jax.ShapeDtypeStruct(gin.shape[1:], F32), jax.ShapeDtypeStruct(gout.shape[1:], F32)],
        scratch_shapes=[pltpu.SemaphoreType.DMA((2,)), pltpu.SemaphoreType.DMA((2,))],
    )(gin, gout)


def _scatter_to_chips(pin, pout):
    def body(pi, po, ri, ro, send_sems, recv_sems, local_sems):
        x, y, c = _place()
        me = 2 * x + y
        pairs = ((pi, ri), (po, ro))
        own = [pltpu.make_async_copy(s.at[me], d.at[me], local_sems.at[k]) for k, (s, d) in enumerate(pairs)]
        for cp in own:
            cp.start()
        chips = [(1 - x, y), (x, 1 - y), (1 - x, 1 - y)]
        sends = []
        for j, (px, py) in enumerate(chips):
            for k, (s, d) in enumerate(pairs):
                cp = pltpu.make_async_remote_copy(src_ref=s.at[2 * px + py], dst_ref=d.at[me], send_sem=send_sems.at[2 * j + k],
                                                  recv_sem=recv_sems.at[2 * j + k], device_id=(px, py, c),
                                                  device_id_type=MESH)
                cp.start()
                sends.append(cp)
        for j, (px, py) in enumerate(chips):
            for k, (s, d) in enumerate(pairs):
                pltpu.make_async_remote_copy(src_ref=s.at[me], dst_ref=d.at[2 * px + py], send_sem=send_sems.at[2 * j + k],
                                             recv_sem=recv_sems.at[2 * j + k], device_id=(px, py, c),
                                             device_id_type=MESH).wait_recv()
        for cp in sends:
            cp.wait_send()
        for cp in own:
            cp.wait()

    return pl.pallas_call(
        body, name="reduce_scatter_chips", in_specs=[_ANY] * 2, out_specs=[_ANY] * 2,
        out_shape=[jax.ShapeDtypeStruct(pin.shape, F32), jax.ShapeDtypeStruct(pout.shape, F32)],
        scratch_shapes=[pltpu.SemaphoreType.DMA((6,)), pltpu.SemaphoreType.DMA((6,)), pltpu.SemaphoreType.DMA((2,))],
    )(pin, pout)


def _share_with_sibling(fin, fout):
    def body(fi, fo, gi, go, send_sems, recv_sems, local_sems):
        x, y, c = _place()
        pairs = ((fi, gi), (fo, go))
        own = [pltpu.make_async_copy(s, d.at[c], local_sems.at[k]) for k, (s, d) in enumerate(pairs)]
        for cp in own:
            cp.start()
        sends = [pltpu.make_async_remote_copy(src_ref=s, dst_ref=d.at[c], send_sem=send_sems.at[k], recv_sem=recv_sems.at[k],
                                              device_id=(x, y, 1 - c), device_id_type=MESH)
                 for k, (s, d) in enumerate(pairs)]
        for cp in sends:
            cp.start()
        for k, (s, d) in enumerate(pairs):
            pltpu.make_async_remote_copy(src_ref=s, dst_ref=d.at[1 - c], send_sem=send_sems.at[k], recv_sem=recv_sems.at[k],
                                         device_id=(x, y, 1 - c), device_id_type=MESH).wait_recv()
        for cp in sends:
            cp.wait_send()
        for cp in own:
            cp.wait()

    return pl.pallas_call(
        body, name="reduce_share_sibling", in_specs=[_ANY] * 2, out_specs=[_ANY] * 2,
        out_shape=[jax.ShapeDtypeStruct((2,) + fin.shape, F32), jax.ShapeDtypeStruct((2,) + fout.shape, F32)],
        scratch_shapes=[pltpu.SemaphoreType.DMA((2,)), pltpu.SemaphoreType.DMA((2,)), pltpu.SemaphoreType.DMA((2,))],
    )(fin, fout)


def _allreduce_small(sm):
    r = sm.shape[0]

    def body(sm_ref, o_ref, rbuf, send_sems, recv_sems):
        x, y, c = _place()
        me = 4 * x + 2 * y + c
        sends = []
        for k in range(1, 8):
            kx, ky, kc = (k >> 2) & 1, (k >> 1) & 1, k & 1
            peer = (1 - x if kx else x, 1 - y if ky else y, 1 - c if kc else c)
            cp = pltpu.make_async_remote_copy(src_ref=sm_ref, dst_ref=rbuf.at[k], send_sem=send_sems.at[k - 1],
                                              recv_sem=recv_sems.at[k - 1], device_id=peer, device_id_type=MESH)
            cp.start()
            sends.append(cp)
        rbuf[0] = sm_ref[...]
        for cp in sends:
            cp.wait()
        acc = rbuf[me]
        for src in range(1, 8):
            acc = acc + rbuf[jnp.bitwise_xor(me, src)]
        o_ref[...] = acc

    return pl.pallas_call(
        body, name="allreduce_small", out_shape=jax.ShapeDtypeStruct((r, 128), F32),
        in_specs=[pl.BlockSpec(memory_space=pltpu.VMEM)], out_specs=pl.BlockSpec(memory_space=pltpu.VMEM),
        scratch_shapes=[pltpu.VMEM((8, r, 128), F32), pltpu.SemaphoreType.DMA((7,)), pltpu.SemaphoreType.DMA((7,))],
    )(sm)


_SMALL = (("norm_w", (2, 1024)), ("conv_a_w", (2, 3, 256)), ("gla_gate_w", (2, 16, 128)), ("gla_gate_b", (2, 128)),
          ("gla_norm_w", (2, 64)), ("pool_w", (2, 4, 64, 64)), ("pool_scale", (2, 256)), ("ssd_conv_w", (2, 4, 768)),
          ("ssd_conv_b", (2, 768)), ("ssd_dt_bias", (2, 4)), ("ssd_a_log", (2, 4)), ("ssd_d", (2, 4)),
          ("ssd_norm_w", (2, 256)), ("final_norm_w", (1024,)))
_SHARDED_SMALL = {"conv_a_w": (2, 3, 64), "ssd_conv_w": (2, 4, 192)}


def _rows_of(shape):
    n = 1
    for s in shape:
        n *= s
    return -(-n // 128)


def _pack(arrays):
    parts = []
    for a in arrays:
        flat = a.reshape(-1).astype(F32)
        parts.append(jnp.pad(flat, (0, _rows_of(a.shape) * 128 - flat.shape[0])).reshape(-1, 128))
    buf = jnp.concatenate(parts, axis=0)
    return jnp.pad(buf, ((0, -buf.shape[0] % 8), (0, 0)))


def _unpack(buf, shapes):
    out, r = [], 0
    for shape in shapes:
        n = 1
        for s in shape:
            n *= s
        rows = _rows_of(shape)
        out.append(buf[r:r + rows].reshape(-1)[:n].reshape(shape))
        r += rows
    return out


def _permute_cols(w):
    parts = [w[..., s:s + n] for s, n in _PERM]
    parts.append(jnp.zeros(w.shape[:-1] + (NP - NPROJ,), w.dtype))
    return jnp.concatenate(parts, axis=-1)


def _unpermute_cols(w):
    return jnp.concatenate([w[..., s:s + n] for s, n in _UNPERM], axis=-1)


def _mixer_consts(layer, conv_a_w, gla_gate_w, gla_gate_b, gla_norm_w, pool_w, pool_scale, ssd_conv_w, ssd_conv_b,
                  ssd_dt_bias, ssd_a_log, ssd_d, ssd_norm_w):
    def row(v):
        return jnp.pad(v.reshape(1, -1), ((0, 0), (0, 768 - v.size)))

    dtb = jnp.zeros((128,), F32).at[16:20].set(ssd_dt_bias[layer])
    rows = [jnp.pad(conv_a_w[layer], ((0, 0), (0, 512))), row(gla_gate_b[layer]), row(jnp.tile(gla_norm_w[layer], 4)),
            row(pool_scale[layer]), row(ssd_conv_b[layer]), row(dtb), row(jnp.repeat(-jnp.exp(ssd_a_log[layer]), 64)),
            row(jnp.repeat(ssd_d[layer], 64)), row(ssd_norm_w[layer]), jnp.zeros((1, 768), F32), ssd_conv_w[layer]]
    prm = jnp.concatenate(rows, axis=0)
    gw = jnp.zeros((128, 128), F32).at[0:16].set(gla_gate_w[layer]).astype(BF16)
    pw = jnp.zeros((256, 256), F32)
    for g in range(4):
        pw = pw.at[64 * g:64 * g + 64, 64 * g:64 * g + 64].set(pool_w[layer, g])
    return prm, gw, pw.astype(BF16)


def _mixer_grads(gsm, dgw, dpw):
    return {
        "conv_a_w": gsm[R_CAW:R_CAW + 3, 0:256], "gla_gate_b": gsm[R_GB, 0:128], "gla_norm_w": gsm[R_GNW, 0:64],
        "pool_scale": gsm[R_PSC, 0:256], "ssd_conv_b": gsm[R_SCB], "ssd_dt_bias": gsm[R_DTB, 16:20],
        "ssd_a_log": gsm[R_AE, 0:256:64], "ssd_d": gsm[R_DE, 0:256:64], "ssd_norm_w": gsm[R_SNW, 0:256],
        "ssd_conv_w": gsm[R_SCW:R_SCW + 4], "gla_gate_w": dgw[0:16],
        "pool_w": jnp.stack([dpw[64 * g:64 * g + 64, 64 * g:64 * g + 64] for g in range(4)]),
    }


def _local_step(x, tgt, norm_w, final_norm_w, wp, wpt, wo, wot, consts):
    xs, projs, mixes, sgs, sss = [x], [], [], [], []
    for l in range(2):
        proj = _rmsproj(xs[l], norm_w[l:l + 1], wp[l], name=f"rmsproj{l}")
        mix, sg, ss = _mixer_fwd(proj, *consts[l], name=f"mixer_fwd{l}")
        xs.append(_outproj(xs[l], mix, wo[l], name=f"outproj{l}"))
        projs.append(proj), mixes.append(mix), sgs.append(sg), sss.append(ss)
    dx, head = _head(xs[2], tgt, final_norm_w.reshape(1, D), name="loss_head")
    dwp, dwo, dnw, mgr = [None, None], [None, None], [None, None], [None, None]
    for l in (1, 0):
        dmix = _matmul_rows(dx, wot[l], name=f"dmix{l}")
        dwo[l] = _dwout(mixes[l], dx, name=f"dwout{l}")
        dproj, gsm, dgw, dpw = _mixer_bwd(projs[l], dmix, sgs[l], sss[l], *consts[l], name=f"mixer_bwd{l}")
        mgr[l] = _mixer_grads(gsm, dgw, dpw)
        dwp[l] = _dwin(xs[l], norm_w[l:l + 1], dproj, name=f"dwin{l}")
        dx, dnw_l = _dxin(dproj, wpt[l], xs[l], dx, norm_w[l:l + 1], name=f"dxin{l}")
        dnw[l] = dnw_l[0]
    return head, dx, jnp.stack(dwp), jnp.stack(dwo), jnp.stack(dnw), mgr


def kernel(x, norm_w, w_in, conv_a_w, gla_gate_w, gla_gate_b, gla_norm_w, pool_w, pool_scale, ssd_conv_w, ssd_conv_b, ssd_dt_bias, ssd_a_log, ssd_d, ssd_norm_w, w_out, final_norm_w, loss_target, m_norm_w, m_w_in, m_conv_a_w, m_gla_gate_w, m_gla_gate_b, m_gla_norm_w, m_pool_w, m_pool_scale, m_ssd_conv_w, m_ssd_conv_b, m_ssd_dt_bias, m_ssd_a_log, m_ssd_d, m_ssd_norm_w, m_w_out, m_final_norm_w, v_norm_w, v_w_in, v_conv_a_w, v_gla_gate_w, v_gla_gate_b, v_gla_norm_w, v_pool_w, v_pool_scale, v_ssd_conv_w, v_ssd_conv_b, v_ssd_dt_bias, v_ssd_a_log, v_ssd_d, v_ssd_norm_w, v_w_out, v_final_norm_w):
    weights = dict(norm_w=norm_w, w_in=w_in, conv_a_w=conv_a_w, gla_gate_w=gla_gate_w, gla_gate_b=gla_gate_b,
                   gla_norm_w=gla_norm_w, pool_w=pool_w, pool_scale=pool_scale, ssd_conv_w=ssd_conv_w,
                   ssd_conv_b=ssd_conv_b, ssd_dt_bias=ssd_dt_bias, ssd_a_log=ssd_a_log, ssd_d=ssd_d,
                   ssd_norm_w=ssd_norm_w, w_out=w_out, final_norm_w=final_norm_w)
    m_in = dict(norm_w=m_norm_w, w_in=m_w_in, conv_a_w=m_conv_a_w, gla_gate_w=m_gla_gate_w, gla_gate_b=m_gla_gate_b,
                gla_norm_w=m_gla_norm_w, pool_w=m_pool_w, pool_scale=m_pool_scale, ssd_conv_w=m_ssd_conv_w,
                ssd_conv_b=m_ssd_conv_b, ssd_dt_bias=m_ssd_dt_bias, ssd_a_log=m_ssd_a_log, ssd_d=m_ssd_d,
                ssd_norm_w=m_ssd_norm_w, w_out=m_w_out, final_norm_w=m_final_norm_w)
    v_in = dict(norm_w=v_norm_w, w_in=v_w_in, conv_a_w=v_conv_a_w, gla_gate_w=v_gla_gate_w, gla_gate_b=v_gla_gate_b,
                gla_norm_w=v_gla_norm_w, pool_w=v_pool_w, pool_scale=v_pool_scale, ssd_conv_w=v_ssd_conv_w,
                ssd_conv_b=v_ssd_conv_b, ssd_dt_bias=v_ssd_dt_bias, ssd_a_log=v_ssd_a_log, ssd_d=v_ssd_d,
                ssd_norm_w=v_ssd_norm_w, w_out=v_w_out, final_norm_w=v_final_norm_w)
    order = ("norm_w", "w_in", "conv_a_w", "gla_gate_w", "gla_gate_b", "gla_norm_w", "pool_w", "pool_scale",
             "ssd_conv_w", "ssd_conv_b", "ssd_dt_bias", "ssd_a_log", "ssd_d", "ssd_norm_w", "w_out", "final_norm_w")
    t = x.shape[1]
    chip = 2 * lax.axis_index("x") + lax.axis_index("y")
    core = lax.axis_index("c")

    cshard = jnp.zeros((16, 256), F32)
    for l in range(2):
        cshard = cshard.at[8 * l:8 * l + 3, 0:64].set(conv_a_w[l]).at[8 * l + 3:8 * l + 7, 0:192].set(ssd_conv_w[l])
    g_in, g_out, g_c = _gather_shards(w_in.astype(BF16), w_out.astype(BF16), cshard)
    w_in_full = jnp.transpose(g_in, (1, 2, 0, 3)).reshape(2, D, NPROJ)
    wp = _permute_cols(w_in_full)
    wpt = jnp.swapaxes(wp, 1, 2)
    wo = jnp.transpose(g_out, (1, 0, 2, 3)).reshape(2, D, D)
    wot = jnp.swapaxes(wo, 1, 2)
    conv_a_full = jnp.stack([jnp.concatenate([g_c[s, 8 * l:8 * l + 3, 0:64] for s in range(4)], axis=-1) for l in range(2)])
    ssd_conv_full = jnp.stack([jnp.concatenate([g_c[s, 8 * l + 3:8 * l + 7, 0:192] for s in range(4)], axis=-1)
                               for l in range(2)])
    consts = [_mixer_consts(l, conv_a_full, gla_gate_w, gla_gate_b, gla_norm_w, pool_w, pool_scale, ssd_conv_full,
                            ssd_conv_b, ssd_dt_bias, ssd_a_log, ssd_d, ssd_norm_w) for l in range(2)]

    head, dx, dwp, dwo, dnw, mgr = _local_step(x.reshape(t, D), loss_target.reshape(t, D), norm_w, final_norm_w,
                                               wp, wpt, wo, wot, consts)

    small = {k: jnp.stack([mgr[0][k], mgr[1][k]]) for k in mgr[0]}
    small["norm_w"] = dnw
    small["final_norm_w"] = head[0]
    red = _allreduce_small(_pack([small[k] for k, _ in _SMALL] + [head[1, 0:1]]))
    red = _unpack(red, [s for _, s in _SMALL] + [(1,)])
    grads = {k: g for (k, _), g in zip(_SMALL, red)}
    loss = red[-1].reshape(())
    grads["conv_a_w"] = lax.dynamic_slice_in_dim(grads["conv_a_w"], chip * 64, 64, axis=2)
    grads["ssd_conv_w"] = lax.dynamic_slice_in_dim(grads["ssd_conv_w"], chip * 192, 192, axis=2)

    gin = jnp.transpose(_unpermute_cols(dwp).reshape(2, D, 4, NPROJ // 4), (0, 2, 1, 3))
    gout = dwo.reshape(2, 4, D // 4, D)
    r_in, r_out = _swap_with_sibling(gin, gout)
    p_in = _add2(lax.dynamic_index_in_dim(gin, core, 0, keepdims=False), r_in, name="reduce_add_pair_in")
    p_out = _add2(lax.dynamic_index_in_dim(gout, core, 0, keepdims=False), r_out, name="reduce_add_pair_out")
    q_in, q_out = _scatter_to_chips(p_in, p_out)
    grads["w_in"], grads["w_out"] = _share_with_sibling(_sum4(q_in, name="reduce_sum_chips_in"),
                                                        _sum4(q_out, name="reduce_sum_chips_out"))

    delta, new_m, new_v = {}, {}, {}
    for k, br in (("w_in", 256), ("w_out", 256)):
        delta[k], new_m[k], new_v[k] = _adamw(weights[k], grads[k], m_in[k], v_in[k], name=f"adamw_{k}", br=br)
    names = [k for k, _ in _SMALL]
    shapes = [_SHARDED_SMALL.get(k, s) for k, s in _SMALL]
    packed = [_pack([d[k] for k in names]) for d in (weights, grads, m_in, v_in)]
    rows = packed[0].shape[0]
    outs = _adamw(*[p.reshape(1, rows, 128) for p in packed], name="adamw_small", br=rows)
    for d, o in zip((delta, new_m, new_v), outs):
        d.update(zip(names, _unpack(o.reshape(rows, 128), shapes)))

    return (loss, dx.reshape(1, t, D), *[grads[k] for k in order], *[delta[k] for k in order],
            *[new_m[k] for k in order], *[new_v[k] for k in order])
```

```python
import functools

import jax
import jax.numpy as jnp
from jax import lax
from jax.experimental import pallas as pl
from jax.experimental.pallas import tpu as pltpu

F32 = jnp.float32
BF16 = jnp.bfloat16
MESH = pl.DeviceIdType.MESH

D = 1024
CH = 64
EPS = 1e-6
NP = 3456
NPROJ = 3348
GLA_SCALE = 32.0 ** -0.5
INV_TAU = 1.0 / 16.0
TB = 256
NCH = TB // CH
HALO_W = 1536

C_SX, C_AH, C_AC, C_PU, C_AB, C_AZ = 0, 768, 1024, 1280, 1536, 1792
C_GQ, C_GK, C_GV, C_GZ, C_PZ, C_SZ, C_TL = 2048, 2176, 2304, 2560, 2816, 3072, 3328
_PERM = ((2576, 768), (0, 256), (512, 256), (1808, 256), (256, 256), (768, 256), (1024, 128), (1152, 128),
         (1280, 256), (1552, 256), (2064, 256), (2320, 256), (1536, 16), (3344, 4))
_UNPERM = ((768, 256), (1536, 256), (1024, 256), (1792, 256), (2048, 128), (2176, 128), (2304, 256), (3328, 16),
           (2560, 256), (1280, 256), (2816, 256), (3072, 256), (0, 768), (3344, 4))

R_CAW, R_GB, R_GNW, R_PSC, R_SCB, R_DTB, R_AE, R_DE, R_SNW, R_SCW = 0, 3, 4, 5, 6, 7, 8, 9, 10, 12

ADAM_LR, ADAM_B1, ADAM_B2, ADAM_EPS, ADAM_WD, ADAM_STEP = 0.001, 0.9, 0.999, 1e-08, 0.01, 10

VMEM_LIMIT = 56 * 1024 * 1024


def _cparams(sem, limit=VMEM_LIMIT):
    return pltpu.CompilerParams(dimension_semantics=sem, vmem_limit_bytes=limit)


def _dot(a, b):
    return jnp.dot(a.astype(BF16), b.astype(BF16), preferred_element_type=F32)


def _dot_nt(a, b):
    return lax.dot_general(a.astype(BF16), b.astype(BF16), (((1,), (1,)), ((), ())), preferred_element_type=F32)


def _dot_tn(a, b):
    return lax.dot_general(a.astype(BF16), b.astype(BF16), (((0,), (0,)), ((), ())), preferred_element_type=F32)


def _split(a):
    hi = a.astype(BF16)
    lo = (a - hi.astype(F32)).astype(BF16)
    return hi, lo


def _dot2_l(a, b):
    hi, lo = _split(a)
    return _dot(hi, b) + _dot(lo, b)


def _dot2_r(a, b):
    hi, lo = _split(b)
    return _dot(a, hi) + _dot(a, lo)


def _dot3_l(a, b):
    hi, lo = _split(a)
    lo2 = ((a - hi.astype(F32)) - lo.astype(F32)).astype(BF16)
    return _dot(hi, b) + _dot(lo, b) + _dot(lo2, b)


def _dot2_nt(a, b):
    hi, lo = _split(a)
    return _dot_nt(hi, b) + _dot_nt(lo, b)


def _silu(z):
    return z * jax.nn.sigmoid(z)


def _dsilu(z):
    s = jax.nn.sigmoid(z)
    return s * (1.0 + z * (1.0 - s))


def _lse1(x):
    return jnp.log(1.0 + jnp.exp(-jnp.abs(x)))


def _cs(a):
    return jnp.sum(a, axis=0, keepdims=True)


def _iota(shape, dim):
    return lax.broadcasted_iota(jnp.int32, shape, dim)


def _expand_mat():
    return jnp.where(_iota((128, 256), 0) - 16 == (_iota((128, 256), 1) >> 6), 1.0, 0.0).astype(BF16)


def _group_mean_mat():
    return jnp.where((_iota((256, 256), 0) >> 6) == (_iota((256, 256), 1) >> 6), 1.0 / 64.0, 0.0).astype(BF16)


def _head_mask_t():
    return jnp.where((_iota((256, 128), 0) >> 6) == (_iota((256, 128), 1) >> 5), 1.0, 0.0).astype(F32)


def _dn(ext, k, n, h):
    return pltpu.roll(ext, k, axis=0)[h:h + n]


def _up(ext, k, n):
    return pltpu.roll(ext, ext.shape[0] - k, axis=0)[:n]


def _pool_lane_select(lane, s2, s4, s8, s16):
    return jnp.where(lane < 64, s2, jnp.where(lane < 128, s4, jnp.where(lane < 192, s8, s16)))


def _winsum_dn(ext, lane):
    s2 = ext + pltpu.roll(ext, 1, axis=0)
    s4 = s2 + pltpu.roll(s2, 2, axis=0)
    s8 = s4 + pltpu.roll(s4, 4, axis=0)
    s16 = s8 + pltpu.roll(s8, 8, axis=0)
    return _pool_lane_select(lane, s2, s4, s8, s16)


def _winsum_up(ext, lane):
    m = ext.shape[0]
    s2 = ext + pltpu.roll(ext, m - 1, axis=0)
    s4 = s2 + pltpu.roll(s2, m - 2, axis=0)
    s8 = s4 + pltpu.roll(s4, m - 4, axis=0)
    s16 = s8 + pltpu.roll(s8, m - 8, axis=0)
    return _pool_lane_select(lane, s2, s4, s8, s16)


def _pool_count(tile, n):
    lane = _iota((1, 256), 1)
    win = _pool_lane_select(lane, 2.0, 4.0, 8.0, 16.0).astype(F32)
    tpos = (tile * n + _iota((n, 1), 0) + 1).astype(F32)
    return jnp.minimum(tpos, win)


def _chunk_tri(n, upper):
    r, c = _iota((n, n), 0), _iota((n, n), 1)
    tri = (c > r) if upper else (c < r)
    return jnp.where(tri & ((r >> 6) == (c >> 6)), 1.0, 0.0).astype(BF16)


def _chunks(a):
    return [a[c * CH:(c + 1) * CH] for c in range(a.shape[0] // CH)]


def _halves(fn, a, b):
    return jnp.concatenate([fn(a[:, 0:128], b[:, 0:128]), fn(a[:, 128:256], b[:, 128:256])], axis=1)


def _mixer_tile_prep(p_ref, xc, prm_ref, gw_v):
    tail = p_ref[:, C_TL:C_TL + 128]
    pre = _dot(tail, gw_v) + prm_ref[R_GB:R_GB + 1, 0:128]
    la = (jnp.minimum(pre, 0.0) - _lse1(pre)) * INV_TAU
    dtin = tail + prm_ref[R_DTB:R_DTB + 1, 0:128]
    dtf = jnp.maximum(dtin, 0.0) + _lse1(dtin)
    dte = _dot2_l(dtf, _expand_mat())
    da = dte * prm_ref[R_AE:R_AE + 1, 0:256]
    rev = _dot2_r(_chunk_tri(TB, True), jnp.concatenate([la, da], axis=1))
    dec = jnp.exp(rev[:, 0:128])
    kd = p_ref[:, C_GK:C_GK + 128] * dec
    wdec = jnp.exp(rev[:, 128:384])
    w = wdec * dte
    xw = xc[:, 0:256] * w
    d_s = [jnp.exp(_cs(a)) for a in _chunks(la)]
    et = [jnp.exp(_cs(a)) for a in _chunks(da)]
    mask_t = _head_mask_t()
    ut_g = [_dot_tn(v, k) * mask_t for v, k in zip(_chunks(p_ref[:, C_GV:C_GV + 256]), _chunks(kd))]
    ut_s = [_halves(_dot_tn, b, x) for b, x in zip(_chunks(xc[:, 256:512]), _chunks(xw))]
    return tail, pre, dtin, dte, dec, kd, wdec, w, xw, d_s, et, ut_g, ut_s


def _rmsproj(x, nw, wp, name, tm=256):
    t = x.shape[0]

    def body(x_ref, nw_ref, w_ref, o_ref):
        xv = x_ref[...]
        rs = lax.rsqrt(jnp.mean(xv * xv, axis=-1, keepdims=True) + EPS)
        h = (xv * rs * nw_ref[...]).astype(BF16)
        o_ref[...] = jnp.dot(h, w_ref[...], preferred_element_type=F32)

    return pl.pallas_call(
        body, grid=(t // tm,), name=name,
        in_specs=[pl.BlockSpec((tm, D), lambda i: (i, 0)), pl.BlockSpec((1, D), lambda i: (0, 0)),
                  pl.BlockSpec((D, NP), lambda i: (0, 0))],
        out_specs=pl.BlockSpec((tm, NP), lambda i: (i, 0)),
        out_shape=jax.ShapeDtypeStruct((t, NP), F32),
        compiler_params=_cparams(("parallel",)),
    )(x, nw, wp)


def _outproj(x, mix, wo, name, tm=512):
    t = x.shape[0]

    def body(x_ref, m_ref, w_ref, o_ref):
        o_ref[...] = x_ref[...] + jnp.dot(m_ref[...].astype(BF16), w_ref[...], preferred_element_type=F32)

    return pl.pallas_call(
        body, grid=(t // tm,), name=name,
        in_specs=[pl.BlockSpec((tm, D), lambda i: (i, 0)), pl.BlockSpec((tm, D), lambda i: (i, 0)),
                  pl.BlockSpec((D, D), lambda i: (0, 0))],
        out_specs=pl.BlockSpec((tm, D), lambda i: (i, 0)),
        out_shape=jax.ShapeDtypeStruct((t, D), F32),
        compiler_params=_cparams(("parallel",)),
    )(x, mix, wo)


def _matmul_rows(a, w, name, tm=512):
    t, kdim = a.shape
    n = w.shape[1]

    def body(a_ref, w_ref, o_ref):
        o_ref[...] = jnp.dot(a_ref[...].astype(BF16), w_ref[...], preferred_element_type=F32)

    return pl.pallas_call(
        body, grid=(t // tm,), name=name,
        in_specs=[pl.BlockSpec((tm, kdim), lambda i: (i, 0)), pl.BlockSpec((kdim, n), lambda i: (0, 0))],
        out_specs=pl.BlockSpec((tm, n), lambda i: (i, 0)),
        out_shape=jax.ShapeDtypeStruct((t, n), F32),
        compiler_params=_cparams(("parallel",)),
    )(a, w)


def _head(x, tgt, fw, name, tm=512):
    t = x.shape[0]

    def body(x_ref, t_ref, w_ref, dx_ref, acc_ref):
        @pl.when(pl.program_id(0) == 0)
        def _():
            acc_ref[...] = jnp.zeros_like(acc_ref)

        xv = x_ref[...]
        w = w_ref[...]
        rs = lax.rsqrt(jnp.mean(xv * xv, axis=-1, keepdims=True) + EPS)
        xh = xv * rs
        err = xh * w - t_ref[...]
        dy = err * (1.0 / D)
        dxh = dy * w
        dx_ref[...] = rs * (dxh - xh * jnp.mean(dxh * xh, axis=-1, keepdims=True))
        acc_ref[0:1, :] += _cs(dy * xh)
        acc_ref[1:2, :] += jnp.zeros((1, D), F32) + (0.5 / D) * jnp.sum(err * err)

    return pl.pallas_call(
        body, grid=(t // tm,), name=name,
        in_specs=[pl.BlockSpec((tm, D), lambda i: (i, 0)), pl.BlockSpec((tm, D), lambda i: (i, 0)),
                  pl.BlockSpec((1, D), lambda i: (0, 0))],
        out_specs=[pl.BlockSpec((tm, D), lambda i: (i, 0)), pl.BlockSpec((8, D), lambda i: (0, 0))],
        out_shape=[jax.ShapeDtypeStruct((t, D), F32), jax.ShapeDtypeStruct((8, D), F32)],
        compiler_params=_cparams(("arbitrary",)),
    )(x, tgt, fw)


def _dxin(dp, wpt, x, dxn, nw, name, tm=256):
    t = x.shape[0]

    def body(dp_ref, w_ref, x_ref, dxn_ref, nw_ref, dx_ref, dnw_ref):
        @pl.when(pl.program_id(0) == 0)
        def _():
            dnw_ref[...] = jnp.zeros_like(dnw_ref)

        dh = jnp.dot(dp_ref[...].astype(BF16), w_ref[...], preferred_element_type=F32)
        xv = x_ref[...]
        rs = lax.rsqrt(jnp.mean(xv * xv, axis=-1, keepdims=True) + EPS)
        xh = xv * rs
        dnw_ref[0:1, :] += _cs(dh * xh)
        dxh = dh * nw_ref[...]
        dx_ref[...] = dxn_ref[...] + rs * (dxh - xh * jnp.mean(dxh * xh, axis=-1, keepdims=True))

    return pl.pallas_call(
        body, grid=(t // tm,), name=name,
        in_specs=[pl.BlockSpec((tm, NP), lambda i: (i, 0)), pl.BlockSpec((NP, D), lambda i: (0, 0)),
                  pl.BlockSpec((tm, D), lambda i: (i, 0)), pl.BlockSpec((tm, D), lambda i: (i, 0)),
                  pl.BlockSpec((1, D), lambda i: (0, 0))],
        out_specs=[pl.BlockSpec((tm, D), lambda i: (i, 0)), pl.BlockSpec((8, D), lambda i: (0, 0))],
        out_shape=[jax.ShapeDtypeStruct((t, D), F32), jax.ShapeDtypeStruct((8, D), F32)],
        compiler_params=_cparams(("arbitrary",)),
    )(dp, wpt, x, dxn, nw)


def _dwin(x, nw, dp, name, tm=512, tn=1152):
    t = x.shape[0]

    def body(x_ref, nw_ref, dp_ref, o_ref):
        @pl.when(pl.program_id(1) == 0)
        def _():
            o_ref[...] = jnp.zeros_like(o_ref)

        xv = x_ref[...]
        rs = lax.rsqrt(jnp.mean(xv * xv, axis=-1, keepdims=True) + EPS)
        h = xv * rs * nw_ref[...]
        o_ref[...] += _dot_tn(h, dp_ref[...])

    return pl.pallas_call(
        body, grid=(NP // tn, t // tm), name=name,
        in_specs=[pl.BlockSpec((tm, D), lambda j, i: (i, 0)), pl.BlockSpec((1, D), lambda j, i: (0, 0)),
                  pl.BlockSpec((tm, tn), lambda j, i: (i, j))],
        out_specs=pl.BlockSpec((D, tn), lambda j, i: (0, j)),
        out_shape=jax.ShapeDtypeStruct((D, NP), F32),
        compiler_params=_cparams(("parallel", "arbitrary")),
    )(x, nw, dp)


def _dwout(mix, dxn, name, tm=512):
    t = mix.shape[0]

    def body(m_ref, g_ref, o_ref):
        @pl.when(pl.program_id(0) == 0)
        def _():
            o_ref[...] = jnp.zeros_like(o_ref)

        o_ref[...] += _dot_tn(m_ref[...], g_ref[...])

    return pl.pallas_call(
        body, grid=(t // tm,), name=name,
        in_specs=[pl.BlockSpec((tm, D), lambda i: (i, 0)), pl.BlockSpec((tm, D), lambda i: (i, 0))],
        out_specs=pl.BlockSpec((D, D), lambda i: (0, 0)),
        out_shape=jax.ShapeDtypeStruct((D, D), F32),
        compiler_params=_cparams(("arbitrary",)),
    )(mix, dxn)


def _mixer_fwd(proj, prm, gw, pw, name):
    t = proj.shape[0]
    nt, nc = t // TB, t // CH

    def body(p_ref, prm_ref, gw_ref, pw_ref, mix_ref, sg_ref, ss_ref, sg_s, ss_s, h_ua, h_pu, h_sx):
        i = pl.program_id(0)

        @pl.when(i == 0)
        def _():
            for r in (sg_s, ss_s, h_ua, h_pu, h_sx):
                r[...] = jnp.zeros_like(r)

        lane = _iota((1, 256), 1)
        u = p_ref[:, C_AC:C_AC + 256] * p_ref[:, C_AH:C_AH + 256]
        ext = jnp.concatenate([h_ua[...], u], axis=0)
        cv = (prm_ref[R_CAW + 2:R_CAW + 3, 0:256] * u + prm_ref[R_CAW + 1:R_CAW + 2, 0:256] * _dn(ext, 1, TB, 8)
              + prm_ref[R_CAW:R_CAW + 1, 0:256] * _dn(ext, 2, TB, 8))
        mix_ref[:, 0:256] = p_ref[:, C_AB:C_AB + 256] * cv * _silu(p_ref[:, C_AZ:C_AZ + 256])
        h_ua[...] = u[TB - 8:, :]
        pu = p_ref[:, C_PU:C_PU + 256]
        ext = jnp.concatenate([h_pu[...], pu], axis=0)
        pooled = _winsum_dn(ext, lane)[16:] / _pool_count(i, TB) - pu
        mixed = _dot(pooled, pw_ref[...])
        mix_ref[:, 512:768] = prm_ref[R_PSC:R_PSC + 1, 0:256] * mixed * _silu(p_ref[:, C_PZ:C_PZ + 256])
        h_pu[...] = pu[TB - 16:, :]
        sx = p_ref[:, C_SX:C_SX + 768]
        ext = jnp.concatenate([h_sx[...], sx], axis=0)
        xc = _silu(prm_ref[R_SCW + 3:R_SCW + 4, :] * sx + prm_ref[R_SCW + 2:R_SCW + 3, :] * _dn(ext, 1, TB, 8)
                   + prm_ref[R_SCW + 1:R_SCW + 2, :] * _dn(ext, 2, TB, 8) + prm_ref[R_SCW:R_SCW + 1, :] * _dn(ext, 3, TB, 8)
                   + prm_ref[R_SCB:R_SCB + 1, :])
        h_sx[...] = sx[TB - 8:, :]

        _, _, _, _, _, _, _, _, _, d_s, et, ut_g, ut_s = _mixer_tile_prep(p_ref, xc, prm_ref, gw_ref[...])
        s_g, s_s = sg_s[...], ss_s[...]
        o, y = [], []
        qs = _chunks(p_ref[:, C_GQ:C_GQ + 128] * GLA_SCALE)
        cm = _chunks(xc[:, 512:768])
        for c in range(NCH):
            sg_ref[c] = s_g
            ss_ref[c] = s_s
            s_g = s_g * d_s[c] + ut_g[c]
            s_s = s_s * et[c] + ut_s[c]
            o.append(_dot_nt(qs[c], s_g))
            y.append(_halves(_dot, cm[c], s_s))
        sg_s[...] = s_g
        ss_s[...] = s_s
        o = jnp.concatenate(o, axis=0)
        on = o * lax.rsqrt(_dot2_l(o * o, _group_mean_mat()) + EPS)
        mix_ref[:, 256:512] = on * prm_ref[R_GNW:R_GNW + 1, 0:256] * _silu(p_ref[:, C_GZ:C_GZ + 256])
        y2 = ((jnp.concatenate(y, axis=0) + prm_ref[R_DE:R_DE + 1, 0:256] * xc[:, 0:256])
              * _silu(p_ref[:, C_SZ:C_SZ + 256]))
        mix_ref[:, 768:1024] = (y2 * lax.rsqrt(jnp.mean(y2 * y2, axis=-1, keepdims=True) + EPS)
                                * prm_ref[R_SNW:R_SNW + 1, 0:256])

    return pl.pallas_call(
        body, grid=(nt,), name=name,
        in_specs=[pl.BlockSpec((TB, NP), lambda i: (i, 0)), pl.BlockSpec((16, 768), lambda i: (0, 0)),
                  pl.BlockSpec((128, 128), lambda i: (0, 0)), pl.BlockSpec((256, 256), lambda i: (0, 0))],
        out_specs=[pl.BlockSpec((TB, D), lambda i: (i, 0)), pl.BlockSpec((NCH, 256, 128), lambda i: (i, 0, 0)),
                   pl.BlockSpec((NCH, 128, 256), lambda i: (i, 0, 0))],
        out_shape=[jax.ShapeDtypeStruct((t, D), F32), jax.ShapeDtypeStruct((nc, 256, 128), F32),
                   jax.ShapeDtypeStruct((nc, 128, 256), F32)],
        scratch_shapes=[pltpu.VMEM((256, 128), F32), pltpu.VMEM((128, 256), F32), pltpu.VMEM((8, 256), F32),
                        pltpu.VMEM((16, 256), F32), pltpu.VMEM((8, 768), F32)],
        compiler_params=_cparams(("arbitrary",)),
    )(proj, prm, gw, pw)


def _mixer_bwd(proj, dmix, sg, ss, prm, gw, pw, name):
    t = proj.shape[0]
    nt = t // TB
    rev = lambda i: nt - 1 - i

    def body(p_ref, hp_ref, dm_ref, sg_ref, ss_ref, prm_ref, gw_ref, pw_ref, dp_ref, sgc_ref,
             gg_s, gs_s, h_dcv, h_dpl, h_dpre, gsm_ref, dgw_ref, dpw_ref):
        i = pl.program_id(0)
        tile = nt - 1 - i

        @pl.when(i == 0)
        def _():
            for r in (gg_s, gs_s, h_dcv, h_dpl, h_dpre, gsm_ref, dgw_ref, dpw_ref):
                r[...] = jnp.zeros_like(r)

        lane = _iota((1, 256), 1)
        first = (tile > 0).astype(F32)
        ah, ac = p_ref[:, C_AH:C_AH + 256], p_ref[:, C_AC:C_AC + 256]
        ab, az = p_ref[:, C_AB:C_AB + 256], p_ref[:, C_AZ:C_AZ + 256]
        w0, w1, w2 = (prm_ref[R_CAW + j:R_CAW + j + 1, 0:256] for j in range(3))
        u = ac * ah
        ext = jnp.concatenate([hp_ref[8:16, C_AC:C_AC + 256] * hp_ref[8:16, C_AH:C_AH + 256] * first, u], axis=0)
        u1, u2 = _dn(ext, 1, TB, 8), _dn(ext, 2, TB, 8)
        cv = w2 * u + w1 * u1 + w0 * u2
        g = dm_ref[:, 0:256]
        sz = _silu(az)
        dp_ref[:, C_AB:C_AB + 256] = g * cv * sz
        dp_ref[:, C_AZ:C_AZ + 256] = g * ab * cv * _dsilu(az)
        dcv = g * ab * sz
        dext = jnp.concatenate([dcv, h_dcv[...]], axis=0)
        du = w2 * dcv + w1 * _up(dext, 1, TB) + w0 * _up(dext, 2, TB)
        dp_ref[:, C_AC:C_AC + 256] = du * ah
        dp_ref[:, C_AH:C_AH + 256] = du * ac
        gsm_ref[R_CAW:R_CAW + 1, 0:256] += _cs(dcv * u2)
        gsm_ref[R_CAW + 1:R_CAW + 2, 0:256] += _cs(dcv * u1)
        gsm_ref[R_CAW + 2:R_CAW + 3, 0:256] += _cs(dcv * u)
        h_dcv[...] = dcv[0:8, :]
        pu, pz = p_ref[:, C_PU:C_PU + 256], p_ref[:, C_PZ:C_PZ + 256]
        psc = prm_ref[R_PSC:R_PSC + 1, 0:256]
        cnt = _pool_count(tile, TB)
        ext = jnp.concatenate([hp_ref[:, C_PU:C_PU + 256] * first, pu], axis=0)
        pooled = _winsum_dn(ext, lane)[16:] / cnt - pu
        pw_v = pw_ref[...]
        mixed = _dot(pooled, pw_v)
        g = dm_ref[:, 512:768]
        sz = _silu(pz)
        gsm_ref[R_PSC:R_PSC + 1, 0:256] += _cs(g * mixed * sz)
        dp_ref[:, C_PZ:C_PZ + 256] = g * psc * mixed * _dsilu(pz)
        dmixed = g * psc * sz
        dpw_ref[...] += _dot_tn(pooled, dmixed)
        dpooled = _dot_nt(dmixed, pw_v)
        qd = dpooled / cnt
        dext = jnp.concatenate([qd, h_dpl[...]], axis=0)
        dp_ref[:, C_PU:C_PU + 256] = _winsum_up(dext, lane)[:TB] - dpooled
        h_dpl[...] = qd[0:16, :]
        sx = p_ref[:, C_SX:C_SX + 768]
        cw = [prm_ref[R_SCW + j:R_SCW + j + 1, :] for j in range(4)]
        ext = jnp.concatenate([hp_ref[8:16, C_SX:C_SX + 768] * first, sx], axis=0)
        sx1, sx2, sx3 = _dn(ext, 1, TB, 8), _dn(ext, 2, TB, 8), _dn(ext, 3, TB, 8)
        cpre = cw[3] * sx + cw[2] * sx1 + cw[1] * sx2 + cw[0] * sx3 + prm_ref[R_SCB:R_SCB + 1, :]
        xc = _silu(cpre)
        xs, bm, cm = xc[:, 0:256], xc[:, 256:512], xc[:, 512:768]

        gw_v = gw_ref[...]
        tail, pre, dtin, dte, dec, kd, wdec, w, xw, d_s, et, ut_g, ut_s = _mixer_tile_prep(p_ref, xc, prm_ref, gw_v)
        gmean = _group_mean_mat()
        mask_t = _head_mask_t()
        gnw = prm_ref[R_GNW:R_GNW + 1, 0:256]
        a_e = prm_ref[R_AE:R_AE + 1, 0:256]
        d_e = prm_ref[R_DE:R_DE + 1, 0:256]
        snw = prm_ref[R_SNW:R_SNW + 1, 0:256]
        sg_in = [sg_ref[c] for c in range(NCH)]
        ss_in = [ss_ref[c] for c in range(NCH)]
        sg_n = [sg_in[c] * d_s[c] + ut_g[c] for c in range(NCH)]
        ss_n = [ss_in[c] * et[c] + ut_s[c] for c in range(NCH)]
        qs = _chunks(p_ref[:, C_GQ:C_GQ + 128] * GLA_SCALE)
        cm_c, bm_c, xw_c, kd_c = _chunks(cm), _chunks(bm), _chunks(xw), _chunks(kd)
        v_c = _chunks(p_ref[:, C_GV:C_GV + 256])
        o = jnp.concatenate([_dot_nt(qs[c], sg_n[c]) for c in range(NCH)], axis=0)
        y = jnp.concatenate([_halves(_dot, cm_c[c], ss_n[c]) for c in range(NCH)], axis=0) + d_e * xs
        gz = p_ref[:, C_GZ:C_GZ + 256]
        r = lax.rsqrt(_dot2_l(o * o, gmean) + EPS)
        on = o * r
        dyb = dm_ref[:, 256:512]
        dp_ref[:, C_GZ:C_GZ + 256] = dyb * on * gnw * _dsilu(gz)
        tg = dyb * _silu(gz)
        gsm_ref[R_GNW:R_GNW + 1, 0:256] += _cs(tg * on)
        don = tg * gnw
        do_c = _chunks(r * (don - on * _dot2_l(don * on, gmean)))
        ssz = p_ref[:, C_SZ:C_SZ + 256]
        sil = _silu(ssz)
        y2 = y * sil
        r = lax.rsqrt(jnp.mean(y2 * y2, axis=-1, keepdims=True) + EPS)
        yn = y2 * r
        dyd = dm_ref[:, 768:1024]
        gsm_ref[R_SNW:R_SNW + 1, 0:256] += _cs(dyd * yn)
        dn = dyd * snw
        dy2 = r * (dn - yn * jnp.mean(dn * yn, axis=-1, keepdims=True))
        dp_ref[:, C_SZ:C_SZ + 256] = dy2 * y * _dsilu(ssz)
        dy = dy2 * sil
        gsm_ref[R_DE:R_DE + 1, 0:256] += _cs(dy * xs)
        dy_c = _chunks(dy)
        dp_ref[:, C_GQ:C_GQ + 128] = jnp.concatenate([_dot(do_c[c], sg_n[c]) for c in range(NCH)], axis=0) * GLA_SCALE
        dcm = jnp.concatenate([_halves(_dot_nt, dy_c[c], ss_n[c]) for c in range(NCH)], axis=0)
        gg = [_dot_tn(do_c[c], qs[c]) * mask_t for c in range(NCH)]
        gs = [_halves(_dot_tn, cm_c[c], dy_c[c]) for c in range(NCH)]
        car_g, car_s = gg_s[...], gs_s[...]
        for c in reversed(range(NCH)):
            gg[c] = gg[c] + car_g
            gs[c] = gs[c] + car_s
            car_g = gg[c] * d_s[c]
            car_s = gs[c] * et[c]
        gg_s[...] = car_g
        gs_s[...] = car_s
        dkd = jnp.concatenate([_dot(v_c[c], gg[c]) for c in range(NCH)], axis=0)
        dp_ref[:, C_GV:C_GV + 256] = jnp.concatenate([_dot_nt(kd_c[c], gg[c]) for c in range(NCH)], axis=0)
        dp_ref[:, C_GK:C_GK + 128] = dkd * dec
        dbm = jnp.concatenate([_halves(_dot_nt, xw_c[c], gs[c]) for c in range(NCH)], axis=0)
        dxw = jnp.concatenate([_halves(_dot, bm_c[c], gs[c]) for c in range(NCH)], axis=0)
        dxs = dy * d_e + dxw * w
        dw = dxw * xs
        dsuf = _dot2_r(_chunk_tri(TB, False), jnp.concatenate([dkd * kd, dw * dte * wdec], axis=1))
        tot_g = jnp.concatenate([jnp.broadcast_to(_cs(gg[c] * sg_in[c]) * d_s[c], (CH, 128)) for c in range(NCH)], axis=0)
        tot_s = jnp.concatenate([jnp.broadcast_to(_cs(gs[c] * ss_in[c]) * et[c], (CH, 256)) for c in range(NCH)], axis=0)
        dpre = (dsuf[:, 0:128] + tot_g) * INV_TAU * jax.nn.sigmoid(-pre)
        dgw_ref[...] += _dot_tn(tail, dpre)
        gsm_ref[R_GB:R_GB + 1, 0:128] += _cs(dpre)
        dda = dsuf[:, 128:384] + tot_s
        gsm_ref[R_AE:R_AE + 1, 0:256] += _cs(dda * dte)
        dtail_s = _dot2_nt(dw * wdec + dda * a_e, _expand_mat()) * jax.nn.sigmoid(dtin)
        gsm_ref[R_DTB:R_DTB + 1, 0:128] += _cs(dtail_s)
        dp_ref[:, C_TL:C_TL + 128] = _dot_nt(dpre, gw_v) + dtail_s
        dpre_c = jnp.concatenate([dxs, dbm, dcm], axis=1) * _dsilu(cpre)
        dext = jnp.concatenate([dpre_c, h_dpre[...]], axis=0)
        dp_ref[:, C_SX:C_SX + 768] = (cw[3] * dpre_c + cw[2] * _up(dext, 1, TB) + cw[1] * _up(dext, 2, TB)
                                      + cw[0] * _up(dext, 3, TB))
        gsm_ref[R_SCW + 3:R_SCW + 4, :] += _cs(dpre_c * sx)
        gsm_ref[R_SCW + 2:R_SCW + 3, :] += _cs(dpre_c * sx1)
        gsm_ref[R_SCW + 1:R_SCW + 2, :] += _cs(dpre_c * sx2)
        gsm_ref[R_SCW:R_SCW + 1, :] += _cs(dpre_c * sx3)
        gsm_ref[R_SCB:R_SCB + 1, :] += _cs(dpre_c)
        h_dpre[...] = dpre_c[0:8, :]

        @pl.when(i == nt - 1)
        def _():
            ri, ci = _iota((256, 256), 0), _iota((256, 256), 1)
            per_head = jnp.where((ri >> 6) == ci, 1.0, 0.0).astype(BF16)
            per_dv = jnp.where((ri & 63) == ci, 1.0, 0.0).astype(BF16)
            row = _iota((8, 256), 0)
            top = gsm_ref[0:8, 0:256]
            sgc_ref[0:8, 0:256] = jnp.where(row == R_GNW, _dot3_l(top, per_dv), top)
            bot = gsm_ref[8:16, 0:256]
            fold = _dot3_l(jnp.where(row == R_AE - 8, bot * a_e, bot), per_head)
            sgc_ref[8:16, 0:256] = jnp.where((row == R_AE - 8) | (row == R_DE - 8), fold, bot)
            sgc_ref[0:16, 256:768] = gsm_ref[:, 256:768]
            sgc_ref[0:16, 768:896] = dgw_ref[0:16, :]
            sgc_ref[0:16, 896:1024] = jnp.zeros((16, 128), F32)
            diag = _pool_lane_select(lane, dpw_ref[0:64, :], dpw_ref[64:128, :], dpw_ref[128:192, :], dpw_ref[192:256, :])
            for q in range(4):
                sgc_ref[16:32, 256 * q:256 * q + 256] = diag[16 * q:16 * q + 16, :]

    return pl.pallas_call(
        body, grid=(nt,), name=name,
        in_specs=[pl.BlockSpec((TB, NP), lambda i: (rev(i), 0)),
                  pl.BlockSpec((16, HALO_W), lambda i: (jnp.maximum(rev(i) * (TB // 16) - 1, 0), 0)),
                  pl.BlockSpec((TB, D), lambda i: (rev(i), 0)),
                  pl.BlockSpec((NCH, 256, 128), lambda i: (rev(i), 0, 0)),
                  pl.BlockSpec((NCH, 128, 256), lambda i: (rev(i), 0, 0)),
                  pl.BlockSpec((16, 768), lambda i: (0, 0)), pl.BlockSpec((128, 128), lambda i: (0, 0)),
                  pl.BlockSpec((256, 256), lambda i: (0, 0))],
        out_specs=[pl.BlockSpec((TB, NP), lambda i: (rev(i), 0)), pl.BlockSpec((32, 1024), lambda i: (0, 0))],
        out_shape=[jax.ShapeDtypeStruct((t, NP), F32), jax.ShapeDtypeStruct((32, 1024), F32)],
        scratch_shapes=[pltpu.VMEM((256, 128), F32), pltpu.VMEM((128, 256), F32), pltpu.VMEM((8, 256), F32),
                        pltpu.VMEM((16, 256), F32), pltpu.VMEM((8, 768), F32), pltpu.VMEM((16, 768), F32),
                        pltpu.VMEM((128, 128), F32), pltpu.VMEM((256, 256), F32)],
        compiler_params=_cparams(("arbitrary",)),
    )(proj, proj, dmix, sg, ss, prm, gw, pw)


def _add2(a, b, name, br=256):
    n, r, c = a.shape

    def body(a_ref, b_ref, o_ref):
        o_ref[...] = a_ref[...] + b_ref[...]

    spec = pl.BlockSpec((1, br, c), lambda i, j: (i, j, 0))
    return pl.pallas_call(body, grid=(n, r // br), name=name, in_specs=[spec, spec], out_specs=spec,
                          out_shape=jax.ShapeDtypeStruct(a.shape, F32),
                          compiler_params=_cparams(("parallel", "parallel")))(a, b)


def _sum4(a, name, br=256):
    _, r, c = a.shape

    def body(a_ref, o_ref):
        o_ref[...] = ((a_ref[0] + a_ref[1]) + a_ref[2]) + a_ref[3]

    return pl.pallas_call(body, grid=(r // br,), name=name,
                          in_specs=[pl.BlockSpec((4, br, c), lambda i: (0, i, 0))],
                          out_specs=pl.BlockSpec((br, c), lambda i: (i, 0)),
                          out_shape=jax.ShapeDtypeStruct((r, c), F32),
                          compiler_params=_cparams(("parallel",)))(a)


def _adamw(w, g, m, v, name, br):
    n, r, c = w.shape

    def body(w_ref, g_ref, m_ref, v_ref, d_ref, m2_ref, v2_ref):
        gv = g_ref[...]
        m2 = ADAM_B1 * m_ref[...] + (1.0 - ADAM_B1) * gv
        v2 = ADAM_B2 * v_ref[...] + (1.0 - ADAM_B2) * (gv * gv)
        m_hat = m2 / (1.0 - ADAM_B1 ** ADAM_STEP)
        v_hat = v2 / (1.0 - ADAM_B2 ** ADAM_STEP)
        d_ref[...] = -ADAM_LR * (m_hat / (jnp.sqrt(v_hat) + ADAM_EPS) + ADAM_WD * w_ref[...])
        m2_ref[...] = m2
        v2_ref[...] = v2

    spec = pl.BlockSpec((1, br, c), lambda i, j: (i, j, 0))
    shp = jax.ShapeDtypeStruct(w.shape, F32)
    return pl.pallas_call(body, grid=(n, r // br), name=name, in_specs=[spec] * 4, out_specs=[spec] * 3,
                          out_shape=[shp] * 3, compiler_params=_cparams(("parallel", "parallel")))(w, g, m, v)


def _place():
    return lax.axis_index("x"), lax.axis_index("y"), lax.axis_index("c")


_ANY = pl.BlockSpec(memory_space=pl.ANY)


def _gather_shards(a, b, s):
    srcs = (a, b, s)

    def body(a_ref, b_ref, s_ref, ao, bo, so, send_sems, recv_sems, local_sems):
        x, y, c = _place()
        me = 2 * x + y
        src, dst = (a_ref, b_ref, s_ref), (ao, bo, so)
        own = [pltpu.make_async_copy(src[k], dst[k].at[me], local_sems.at[k]) for k in range(3)]
        for cp in own:
            cp.start()
        chips = [(1 - x, y), (x, 1 - y), (1 - x, 1 - y)]
        sends = []
        for j, (px, py) in enumerate(chips):
            for k in range(3):
                cp = pltpu.make_async_remote_copy(src_ref=src[k], dst_ref=dst[k].at[me], send_sem=send_sems.at[3 * j + k],
                                                  recv_sem=recv_sems.at[3 * j + k], device_id=(px, py, c),
                                                  device_id_type=MESH)
                cp.start()
                sends.append(cp)
        for j, (px, py) in enumerate(chips):
            for k in range(3):
                pltpu.make_async_remote_copy(src_ref=src[k], dst_ref=dst[k].at[2 * px + py],
                                             send_sem=send_sems.at[3 * j + k], recv_sem=recv_sems.at[3 * j + k],
                                             device_id=(px, py, c), device_id_type=MESH).wait_recv()
        for cp in sends:
            cp.wait_send()
        for cp in own:
            cp.wait()

    return pl.pallas_call(
        body, name="gather_weights", in_specs=[_ANY] * 3, out_specs=[_ANY] * 3,
        out_shape=[jax.ShapeDtypeStruct((4,) + t.shape, t.dtype) for t in srcs],
        scratch_shapes=[pltpu.SemaphoreType.DMA((9,)), pltpu.SemaphoreType.DMA((9,)), pltpu.SemaphoreType.DMA((3,))],
    )(*srcs)


def _swap_with_sibling(gin, gout):
    def body(gi, go, ri, ro, send_sems, recv_sems):
        x, y, c = _place()
        cps = [pltpu.make_async_remote_copy(src_ref=s.at[1 - c], dst_ref=d, send_sem=send_sems.at[k], recv_sem=recv_sems.at[k],
                                            device_id=(x, y, 1 - c), device_id_type=MESH)
               for k, (s, d) in enumerate(((gi, ri), (go, ro)))]
        for cp in cps:
            cp.start()
        for cp in cps:
            cp.wait()

    return pl.pallas_call(
        body, name="reduce_swap_sibling", in_specs=[_ANY] * 2, out_specs=[_ANY] * 2,
        out_shape=[jax.ShapeDtypeStruct(gin.shape[1:], F32), jax.ShapeDtypeStruct(gout.shape[1:], F32)],
        scratch_shapes=[pltpu.SemaphoreType.DMA((2,)), pltpu.SemaphoreType.DMA((2,))],
    )(gin, gout)


def _scatter_to_chips(pin, pout):
    def body(pi, po, ri, ro, send_sems, recv_sems, local_sems):
        x, y, c = _place()
        me = 2 * x + y
        pairs = ((pi, ri), (po, ro))
        own = [pltpu.make_async_copy(s.at[me], d.at[me], local_sems.at[k]) for k, (s, d) in enumerate(pairs)]
        for cp in own:
            cp.start()
        chips = [(1 - x, y), (x, 1 - y), (1 - x, 1 - y)]
        sends = []
        for j, (px, py) in enumerate(chips):
            for k, (s, d) in enumerate(pairs):
                cp = pltpu.make_async_remote_copy(src_ref=s.at[2 * px + py], dst_ref=d.at[me], send_sem=send_sems.at[2 * j + k],
                                                  recv_sem=recv_sems.at[2 * j + k], device_id=(px, py, c),
                                                  device_id_type=MESH)
                cp.start()
                sends.append(cp)
        for j, (px, py) in enumerate(chips):
            for k, (s, d) in enumerate(pairs):
                pltpu.make_async_remote_copy(src_ref=s.at[me], dst_ref=d.at[2 * px + py], send_sem=send_sems.at[2 * j + k],
                                             recv_sem=recv_sems.at[2 * j + k], device_id=(px, py, c),
                                             device_id_type=MESH).wait_recv()
        for cp in sends:
            cp.wait_send()
        for cp in own:
            cp.wait()

    return pl.pallas_call(
        body, name="reduce_scatter_chips", in_specs=[_ANY] * 2, out_specs=[_ANY] * 2,
        out_shape=[jax.ShapeDtypeStruct(pin.shape, F32), jax.ShapeDtypeStruct(pout.shape, F32)],
        scratch_shapes=[pltpu.SemaphoreType.DMA((6,)), pltpu.SemaphoreType.DMA((6,)), pltpu.SemaphoreType.DMA((2,))],
    )(pin, pout)


def _share_with_sibling(fin, fout):
    def body(fi, fo, gi, go, send_sems, recv_sems, local_sems):
        x, y, c = _place()
        pairs = ((fi, gi), (fo, go))
        own = [pltpu.make_async_copy(s, d.at[c], local_sems.at[k]) for k, (s, d) in enumerate(pairs)]
        for cp in own:
            cp.start()
        sends = [pltpu.make_async_remote_copy(src_ref=s, dst_ref=d.at[c], send_sem=send_sems.at[k], recv_sem=recv_sems.at[k],
                                              device_id=(x, y, 1 - c), device_id_type=MESH)
                 for k, (s, d) in enumerate(pairs)]
        for cp in sends:
            cp.start()
        for k, (s, d) in enumerate(pairs):
            pltpu.make_async_remote_copy(src_ref=s, dst_ref=d.at[1 - c], send_sem=send_sems.at[k], recv_sem=recv_sems.at[k],
                                         device_id=(x, y, 1 - c), device_id_type=MESH).wait_recv()
        for cp in sends:
            cp.wait_send()
        for cp in own:
            cp.wait()

    return pl.pallas_call(
        body, name="reduce_share_sibling", in_specs=[_ANY] * 2, out_specs=[_ANY] * 2,
        out_shape=[jax.ShapeDtypeStruct((2,) + fin.shape, F32), jax.ShapeDtypeStruct((2,) + fout.shape, F32)],
        scratch_shapes=[pltpu.SemaphoreType.DMA((2,)), pltpu.SemaphoreType.DMA((2,)), pltpu.SemaphoreType.DMA((2,))],
    )(fin, fout)


_SMALL_NAMES = ("norm_w", "conv_a_w", "gla_gate_w", "gla_gate_b", "gla_norm_w", "pool_w", "pool_scale", "ssd_conv_w",
                "ssd_conv_b", "ssd_dt_bias", "ssd_a_log", "ssd_d", "ssd_norm_w", "final_norm_w")
SMALL_ROWS = 72


def _adam_math(w, g, m, v):
    m2 = ADAM_B1 * m + (1.0 - ADAM_B1) * g
    v2 = ADAM_B2 * v + (1.0 - ADAM_B2) * (g * g)
    m_hat = m2 / (1.0 - ADAM_B1 ** ADAM_STEP)
    v_hat = v2 / (1.0 - ADAM_B2 ** ADAM_STEP)
    return -ADAM_LR * (m_hat / (jnp.sqrt(v_hat) + ADAM_EPS) + ADAM_WD * w), m2, v2


def _small_slices(name, chip):
    if name == "conv_a_w":
        return [((), slice(R_CAW, R_CAW + 3), slice(64 * chip, 64 * chip + 64))]
    if name == "ssd_conv_w":
        return [((), slice(R_SCW, R_SCW + 4), slice(192 * chip, 192 * chip + 192))]
    if name == "gla_gate_w":
        return [((), slice(0, 16), slice(768, 896))]
    if name == "pool_w":
        return [((g, slice(16 * q, 16 * q + 16)), slice(16, 32), slice(256 * q + 64 * g, 256 * q + 64 * g + 64))
                for g in range(4) for q in range(4)]
    row, lanes = {"gla_gate_b": (R_GB, slice(0, 128)), "gla_norm_w": (R_GNW, slice(0, 64)),
                  "pool_scale": (R_PSC, slice(0, 256)), "ssd_conv_b": (R_SCB, slice(0, 768)),
                  "ssd_dt_bias": (R_DTB, slice(16, 20)), "ssd_a_log": (R_AE, slice(0, 4)), "ssd_d": (R_DE, slice(0, 4)),
                  "ssd_norm_w": (R_SNW, slice(0, 256))}[name]
    return [((), slice(row, row + 1), lanes)]


def _small_step(sg0, sg1, dnw0, dnw1, head, w, m, v):
    n = len(_SMALL_NAMES)

    def body(*refs):
        sg0_ref, sg1_ref, dnw0_ref, dnw1_ref, head_ref = refs[0:5]
        w_refs, m_refs, v_refs = refs[5:5 + n], refs[5 + n:5 + 2 * n], refs[5 + 2 * n:5 + 3 * n]
        o = 5 + 3 * n
        g_out, d_out, m_out, v_out = refs[o:o + n], refs[o + n:o + 2 * n], refs[o + 2 * n:o + 3 * n], refs[o + 3 * n:o + 4 * n]
        loss_ref = refs[o + 4 * n]
        stage, pair, rbuf, acc, send_sems, recv_sems = refs[o + 4 * n + 1:]
        x, y, c = _place()
        chip = 2 * x + y
        stage[0:32, :] = sg0_ref[...]
        stage[32:64, :] = sg1_ref[...]
        stage[64:65, :] = dnw0_ref[0:1, :]
        stage[65:66, :] = dnw1_ref[0:1, :]
        stage[66:68, :] = head_ref[0:2, :]
        stage[68:72, :] = jnp.zeros((4, D), F32)
        sib = pltpu.make_async_remote_copy(src_ref=stage, dst_ref=pair, send_sem=send_sems.at[0], recv_sem=recv_sems.at[0],
                                           device_id=(x, y, 1 - c), device_id_type=MESH)
        sib.start()
        sib.wait()
        rbuf[0] = stage[...] + pair[...]
        sends = []
        for k, (px, py) in enumerate(((1 - x, y), (x, 1 - y), (1 - x, 1 - y)), start=1):
            cp = pltpu.make_async_remote_copy(src_ref=rbuf.at[0], dst_ref=rbuf.at[k], send_sem=send_sems.at[k],
                                              recv_sem=recv_sems.at[k], device_id=(px, py, c), device_id_type=MESH)
            cp.start()
            sends.append(cp)
        for cp in sends:
            cp.wait()
        slab = lambda s: jnp.where(s == 0, 0, jnp.where(s == 2, 1, jnp.where(s == 1, 2, 3)))
        total = rbuf[slab(jnp.bitwise_xor(chip, 0))]
        for s in range(1, 4):
            total = total + rbuf[slab(jnp.bitwise_xor(chip, s))]
        acc[...] = total
        loss_ref[...] = acc[67:68, 0:1]

        def update(i, idx, g):
            wv, mv, vv = w_refs[i][idx], m_refs[i][idx], v_refs[i][idx]
            d, m2, v2 = _adam_math(wv, g, mv, vv)
            g_out[i][idx], d_out[i][idx], m_out[i][idx], v_out[i][idx] = g, d, m2, v2

        for i, name in enumerate(_SMALL_NAMES):
            if name == "final_norm_w":
                update(i, (slice(0, 1), slice(None)), acc[66:67, :])
            elif name == "norm_w":
                for l in range(2):
                    update(i, (slice(l, l + 1), slice(None)), acc[64 + l:65 + l, :])
            elif name in ("conv_a_w", "ssd_conv_w"):
                for s in range(4):
                    @pl.when(chip == s)
                    def _(i=i, name=name, s=s):
                        for l in range(2):
                            (_, rows, lanes), = _small_slices(name, s)
                            update(i, (l,), acc[rows.start + 32 * l:rows.stop + 32 * l, lanes])
            else:
                for l in range(2):
                    for idx, rows, lanes in _small_slices(name, 0):
                        g = acc[rows.start + 32 * l:rows.stop + 32 * l, lanes]
                        if w_refs[i].ndim == 2:
                            update(i, (slice(l, l + 1), slice(None)), g)
                        else:
                            update(i, (l,) + idx, g)

    args = [sg0, sg1, dnw0, dnw1, head] + [d[k] for d in (w, m, v) for k in _SMALL_NAMES]
    shapes = [jax.ShapeDtypeStruct(w[k].shape, F32) for k in _SMALL_NAMES]
    vmem = pl.BlockSpec(memory_space=pltpu.VMEM)
    outs = pl.pallas_call(
        body, name="small_allreduce_adamw", in_specs=[vmem] * len(args), out_specs=[vmem] * (4 * n + 1),
        out_shape=shapes * 4 + [jax.ShapeDtypeStruct((1, 1), F32)],
        scratch_shapes=[pltpu.VMEM((SMALL_ROWS, D), F32), pltpu.VMEM((SMALL_ROWS, D), F32),
                        pltpu.VMEM((4, SMALL_ROWS, D), F32), pltpu.VMEM((SMALL_ROWS, D), F32),
                        pltpu.SemaphoreType.DMA((4,)), pltpu.SemaphoreType.DMA((4,))],
    )(*args)
    return outs[0:n], outs[n:2 * n], outs[2 * n:3 * n], outs[3 * n:4 * n], outs[4 * n]


def _permute_cols(w):
    parts = [w[..., s:s + n] for s, n in _PERM]
    parts.append(jnp.zeros(w.shape[:-1] + (NP - NPROJ,), w.dtype))
    return jnp.concatenate(parts, axis=-1)


def _unpermute_cols(w):
    return jnp.concatenate([w[..., s:s + n] for s, n in _UNPERM], axis=-1)


def _mixer_consts(layer, conv_a_w, gla_gate_w, gla_gate_b, gla_norm_w, pool_w, pool_scale, ssd_conv_w, ssd_conv_b,
                  ssd_dt_bias, ssd_a_log, ssd_d, ssd_norm_w):
    def row(v):
        return jnp.pad(v.reshape(1, -1), ((0, 0), (0, 768 - v.size)))

    dtb = jnp.zeros((128,), F32).at[16:20].set(ssd_dt_bias[layer])
    rows = [jnp.pad(conv_a_w[layer], ((0, 0), (0, 512))), row(gla_gate_b[layer]), row(jnp.tile(gla_norm_w[layer], 4)),
            row(pool_scale[layer]), row(ssd_conv_b[layer]), row(dtb), row(jnp.repeat(-jnp.exp(ssd_a_log[layer]), 64)),
            row(jnp.repeat(ssd_d[layer], 64)), row(ssd_norm_w[layer]), jnp.zeros((1, 768), F32), ssd_conv_w[layer]]
    prm = jnp.concatenate(rows, axis=0)
    gw = jnp.zeros((128, 128), F32).at[0:16].set(gla_gate_w[layer]).astype(BF16)
    pw = jnp.zeros((256, 256), F32)
    for g in range(4):
        pw = pw.at[64 * g:64 * g + 64, 64 * g:64 * g + 64].set(pool_w[layer, g])
    return prm, gw, pw.astype(BF16)


def _local_step(x, tgt, norm_w, final_norm_w, wp, wpt, wo, wot, consts):
    xs, projs, mixes, sgs, sss = [x], [], [], [], []
    for l in range(2):
        proj = _rmsproj(xs[l], norm_w[l:l + 1], wp[l], name=f"rmsproj{l}")
        mix, sg, ss = _mixer_fwd(proj, *consts[l], name=f"mixer_fwd{l}")
        xs.append(_outproj(xs[l], mix, wo[l], name=f"outproj{l}"))
        projs.append(proj), mixes.append(mix), sgs.append(sg), sss.append(ss)
    dx, head = _head(xs[2], tgt, final_norm_w.reshape(1, D), name="loss_head")
    dwp, dwo, dnw, mgr = [None, None], [None, None], [None, None], [None, None]
    for l in (1, 0):
        dmix = _matmul_rows(dx, wot[l], name=f"dmix{l}")
        dwo[l] = _dwout(mixes[l], dx, name=f"dwout{l}")
        dproj, mgr[l] = _mixer_bwd(projs[l], dmix, sgs[l], sss[l], *consts[l], name=f"mixer_bwd{l}")
        dwp[l] = _dwin(xs[l], norm_w[l:l + 1], dproj, name=f"dwin{l}")
        dx, dnw[l] = _dxin(dproj, wpt[l], xs[l], dx, norm_w[l:l + 1], name=f"dxin{l}")
    return head, dx, jnp.stack(dwp), jnp.stack(dwo), dnw, mgr


def kernel(x, norm_w, w_in, conv_a_w, gla_gate_w, gla_gate_b, gla_norm_w, pool_w, pool_scale, ssd_conv_w, ssd_conv_b, ssd_dt_bias, ssd_a_log, ssd_d, ssd_norm_w, w_out, final_norm_w, loss_target, m_norm_w, m_w_in, m_conv_a_w, m_gla_gate_w, m_gla_gate_b, m_gla_norm_w, m_pool_w, m_pool_scale, m_ssd_conv_w, m_ssd_conv_b, m_ssd_dt_bias, m_ssd_a_log, m_ssd_d, m_ssd_norm_w, m_w_out, m_final_norm_w, v_norm_w, v_w_in, v_conv_a_w, v_gla_gate_w, v_gla_gate_b, v_gla_norm_w, v_pool_w, v_pool_scale, v_ssd_conv_w, v_ssd_conv_b, v_ssd_dt_bias, v_ssd_a_log, v_ssd_d, v_ssd_norm_w, v_w_out, v_final_norm_w):
    weights = dict(norm_w=norm_w, w_in=w_in, conv_a_w=conv_a_w, gla_gate_w=gla_gate_w, gla_gate_b=gla_gate_b,
                   gla_norm_w=gla_norm_w, pool_w=pool_w, pool_scale=pool_scale, ssd_conv_w=ssd_conv_w,
                   ssd_conv_b=ssd_conv_b, ssd_dt_bias=ssd_dt_bias, ssd_a_log=ssd_a_log, ssd_d=ssd_d,
                   ssd_norm_w=ssd_norm_w, w_out=w_out, final_norm_w=final_norm_w)
    m_in = dict(norm_w=m_norm_w, w_in=m_w_in, conv_a_w=m_conv_a_w, gla_gate_w=m_gla_gate_w, gla_gate_b=m_gla_gate_b,
                gla_norm_w=m_gla_norm_w, pool_w=m_pool_w, pool_scale=m_pool_scale, ssd_conv_w=m_ssd_conv_w,
                ssd_conv_b=m_ssd_conv_b, ssd_dt_bias=m_ssd_dt_bias, ssd_a_log=m_ssd_a_log, ssd_d=m_ssd_d,
                ssd_norm_w=m_ssd_norm_w, w_out=m_w_out, final_norm_w=m_final_norm_w)
    v_in = dict(norm_w=v_norm_w, w_in=v_w_in, conv_a_w=v_conv_a_w, gla_gate_w=v_gla_gate_w, gla_gate_b=v_gla_gate_b,
                gla_norm_w=v_gla_norm_w, pool_w=v_pool_w, pool_scale=v_pool_scale, ssd_conv_w=v_ssd_conv_w,
                ssd_conv_b=v_ssd_conv_b, ssd_dt_bias=v_ssd_dt_bias, ssd_a_log=v_ssd_a_log, ssd_d=v_ssd_d,
                ssd_norm_w=v_ssd_norm_w, w_out=v_w_out, final_norm_w=v_final_norm_w)
    order = ("norm_w", "w_in", "conv_a_w", "gla_gate_w", "gla_gate_b", "gla_norm_w", "pool_w", "pool_scale",
             "ssd_conv_w", "ssd_conv_b", "ssd_dt_bias", "ssd_a_log", "ssd_d", "ssd_norm_w", "w_out", "final_norm_w")
    t = x.shape[1]
    chip = 2 * lax.axis_index("x") + lax.axis_index("y")
    core = lax.axis_index("c")

    cshard = jnp.zeros((16, 256), F32)
    for l in range(2):
        cshard = cshard.at[8 * l:8 * l + 3, 0:64].set(conv_a_w[l]).at[8 * l + 3:8 * l + 7, 0:192].set(ssd_conv_w[l])
    g_in, g_out, g_c = _gather_shards(w_in.astype(BF16), w_out.astype(BF16), cshard)
    w_in_full = jnp.transpose(g_in, (1, 2, 0, 3)).reshape(2, D, NPROJ)
    wp = _permute_cols(w_in_full)
    wpt = jnp.swapaxes(wp, 1, 2)
    wo = jnp.transpose(g_out, (1, 0, 2, 3)).reshape(2, D, D)
    wot = jnp.swapaxes(wo, 1, 2)
    conv_a_full = jnp.stack([jnp.concatenate([g_c[s, 8 * l:8 * l + 3, 0:64] for s in range(4)], axis=-1) for l in range(2)])
    ssd_conv_full = jnp.stack([jnp.concatenate([g_c[s, 8 * l + 3:8 * l + 7, 0:192] for s in range(4)], axis=-1)
                               for l in range(2)])
    consts = [_mixer_consts(l, conv_a_full, gla_gate_w, gla_gate_b, gla_norm_w, pool_w, pool_scale, ssd_conv_full,
                            ssd_conv_b, ssd_dt_bias, ssd_a_log, ssd_d, ssd_norm_w) for l in range(2)]

    head, dx, dwp, dwo, dnw, mgr = _local_step(x.reshape(t, D), loss_target.reshape(t, D), norm_w, final_norm_w,
                                               wp, wpt, wo, wot, consts)

    as2d = lambda d: {k: (d[k].reshape(1, D) if k == "final_norm_w" else d[k]) for k in _SMALL_NAMES}
    small = _small_step(mgr[0], mgr[1], dnw[0], dnw[1], head, as2d(weights), as2d(m_in), as2d(v_in))
    grads, delta, new_m, new_v = ({k: (a.reshape(D) if k == "final_norm_w" else a) for k, a in zip(_SMALL_NAMES, part)}
                                  for part in small[0:4])
    loss = small[4].reshape(())

    gin = jnp.transpose(_unpermute_cols(dwp).reshape(2, D, 4, NPROJ // 4), (0, 2, 1, 3))
    gout = dwo.reshape(2, 4, D // 4, D)
    r_in, r_out = _swap_with_sibling(gin, gout)
    p_in = _add2(lax.dynamic_index_in_dim(gin, core, 0, keepdims=False), r_in, name="reduce_add_pair_in")
    p_out = _add2(lax.dynamic_index_in_dim(gout, core, 0, keepdims=False), r_out, name="reduce_add_pair_out")
    q_in, q_out = _scatter_to_chips(p_in, p_out)
    grads["w_in"], grads["w_out"] = _share_with_sibling(_sum4(q_in, name="reduce_sum_chips_in"),
                                                        _sum4(q_out, name="reduce_sum_chips_out"))

    for k in ("w_in", "w_out"):
        delta[k], new_m[k], new_v[k] = _adamw(weights[k], grads[k], m_in[k], v_in[k], name=f"adamw_{k}", br=256)

    return (loss, dx.reshape(1, t, D), *[grads[k] for k in order], *[delta[k] for k in order],
            *[new_m[k] for k in order], *[new_v[k] for k in order])
```

```python
import functools

import jax
import jax.numpy as jnp
from jax import lax
from jax.experimental import pallas as pl
from jax.experimental.pallas import tpu as pltpu

F32 = jnp.float32
BF16 = jnp.bfloat16
MESH = pl.DeviceIdType.MESH

D = 1024
CH = 64
EPS = 1e-6
NP = 3456
NPROJ = 3348
GLA_SCALE = 32.0 ** -0.5
INV_TAU = 1.0 / 16.0
TB = 256
NCH = TB // CH
HALO_W = 1536

C_SX, C_AH, C_AC, C_PU, C_AB, C_AZ = 0, 768, 1024, 1280, 1536, 1792
C_GQ, C_GK, C_GV, C_GZ, C_PZ, C_SZ, C_TL = 2048, 2176, 2304, 2560, 2816, 3072, 3328
_PERM = ((2576, 768), (0, 256), (512, 256), (1808, 256), (256, 256), (768, 256), (1024, 128), (1152, 128),
         (1280, 256), (1552, 256), (2064, 256), (2320, 256), (1536, 16), (3344, 4))
_UNPERM = ((768, 256), (1536, 256), (1024, 256), (1792, 256), (2048, 128), (2176, 128), (2304, 256), (3328, 16),
           (2560, 256), (1280, 256), (2816, 256), (3072, 256), (0, 768), (3344, 4))

R_CAW, R_GB, R_GNW, R_PSC, R_SCB, R_DTB, R_AE, R_DE, R_SNW, R_SCW = 0, 3, 4, 5, 6, 7, 8, 9, 10, 12

ADAM_LR, ADAM_B1, ADAM_B2, ADAM_EPS, ADAM_WD, ADAM_STEP = 0.001, 0.9, 0.999, 1e-08, 0.01, 10

VMEM_LIMIT = 56 * 1024 * 1024


def _cparams(sem, limit=VMEM_LIMIT):
    return pltpu.CompilerParams(dimension_semantics=sem, vmem_limit_bytes=limit)


_ANY = pl.BlockSpec(memory_space=pl.ANY)


def _place():
    return lax.axis_index("x"), lax.axis_index("y"), lax.axis_index("c")


class _Rider:
    def __init__(self, inputs, out_shapes, sems, start, finish, aliases=None):
        self.inputs, self.out_shapes, self.sems = tuple(inputs), tuple(out_shapes), tuple(sems)
        self.start, self.finish, self.aliases = start, finish, dict(aliases or {})


def _call(body, args, *, grid, in_specs, out_specs, out_shape, name, sem, scratch_shapes=(), rider=None):
    if rider is None:
        outs = pl.pallas_call(body, grid=grid, name=name, in_specs=list(in_specs), out_specs=list(out_specs),
                              out_shape=list(out_shape), scratch_shapes=list(scratch_shapes),
                              compiler_params=_cparams(sem))(*args)
        return list(outs), []
    ni, no, ns = len(args), len(out_shape), len(scratch_shapes)
    ri, ro = len(rider.inputs), len(rider.out_shapes)

    def full(*refs):
        ins, rins = refs[:ni], refs[ni:ni + ri]
        outs, routs = refs[ni + ri:ni + ri + no], refs[ni + ri + no:ni + ri + no + ro]
        scr, rsem = refs[ni + ri + no + ro:ni + ri + no + ro + ns], refs[ni + ri + no + ro + ns:]
        first = functools.reduce(jnp.logical_and, [pl.program_id(a) == 0 for a in range(len(grid))])
        last = functools.reduce(jnp.logical_and, [pl.program_id(a) == grid[a] - 1 for a in range(len(grid))])

        @pl.when(first)
        def _():
            rider.start(rins, routs, rsem)

        body(*ins, *outs, *scr)

        @pl.when(last)
        def _():
            rider.finish(rins, routs, rsem)

    outs = pl.pallas_call(
        full, grid=grid, name=name, in_specs=list(in_specs) + [_ANY] * ri, out_specs=list(out_specs) + [_ANY] * ro,
        out_shape=list(out_shape) + list(rider.out_shapes), scratch_shapes=list(scratch_shapes) + list(rider.sems),
        input_output_aliases={ni + k: no + v for k, v in rider.aliases.items()},
        compiler_params=_cparams(("arbitrary",) * len(grid)))(*args, *rider.inputs)
    return list(outs[:no]), list(outs[no:])


def _run_rider(rider, name):
    ri = len(rider.inputs)

    def body(*refs):
        rins, routs, rsem = refs[:ri], refs[ri:ri + len(rider.out_shapes)], refs[ri + len(rider.out_shapes):]
        rider.start(rins, routs, rsem)
        rider.finish(rins, routs, rsem)

    return list(pl.pallas_call(body, name=name, in_specs=[_ANY] * ri, out_specs=[_ANY] * len(rider.out_shapes),
                               out_shape=list(rider.out_shapes), scratch_shapes=list(rider.sems),
                               input_output_aliases=dict(rider.aliases))(*rider.inputs))


def _dot(a, b):
    return jnp.dot(a.astype(BF16), b.astype(BF16), preferred_element_type=F32)


def _dot_nt(a, b):
    return lax.dot_general(a.astype(BF16), b.astype(BF16), (((1,), (1,)), ((), ())), preferred_element_type=F32)


def _dot_tn(a, b):
    return lax.dot_general(a.astype(BF16), b.astype(BF16), (((0,), (0,)), ((), ())), preferred_element_type=F32)


def _split(a):
    hi = a.astype(BF16)
    lo = (a - hi.astype(F32)).astype(BF16)
    return hi, lo


def _dot2_l(a, b):
    hi, lo = _split(a)
    return _dot(hi, b) + _dot(lo, b)


def _dot2_r(a, b):
    hi, lo = _split(b)
    return _dot(a, hi) + _dot(a, lo)


def _dot3_l(a, b):
    hi, lo = _split(a)
    lo2 = ((a - hi.astype(F32)) - lo.astype(F32)).astype(BF16)
    return _dot(hi, b) + _dot(lo, b) + _dot(lo2, b)


def _dot2_nt(a, b):
    hi, lo = _split(a)
    return _dot_nt(hi, b) + _dot_nt(lo, b)


def _silu(z):
    return z * jax.nn.sigmoid(z)


def _dsilu(z):
    s = jax.nn.sigmoid(z)
    return s * (1.0 + z * (1.0 - s))


def _lse1(x):
    return jnp.log(1.0 + jnp.exp(-jnp.abs(x)))


def _cs(a):
    return jnp.sum(a, axis=0, keepdims=True)


def _iota(shape, dim):
    return lax.broadcasted_iota(jnp.int32, shape, dim)


def _expand_mat():
    return jnp.where(_iota((128, 256), 0) - 16 == (_iota((128, 256), 1) >> 6), 1.0, 0.0).astype(BF16)


def _group_mean_mat():
    return jnp.where((_iota((256, 256), 0) >> 6) == (_iota((256, 256), 1) >> 6), 1.0 / 64.0, 0.0).astype(BF16)


def _head_mask_t():
    return jnp.where((_iota((256, 128), 0) >> 6) == (_iota((256, 128), 1) >> 5), 1.0, 0.0).astype(F32)


def _dn(ext, k, n, h):
    return pltpu.roll(ext, k, axis=0)[h:h + n]


def _up(ext, k, n):
    return pltpu.roll(ext, ext.shape[0] - k, axis=0)[:n]


def _pool_lane_select(lane, s2, s4, s8, s16):
    return jnp.where(lane < 64, s2, jnp.where(lane < 128, s4, jnp.where(lane < 192, s8, s16)))


def _winsum_dn(ext, lane):
    s2 = ext + pltpu.roll(ext, 1, axis=0)
    s4 = s2 + pltpu.roll(s2, 2, axis=0)
    s8 = s4 + pltpu.roll(s4, 4, axis=0)
    s16 = s8 + pltpu.roll(s8, 8, axis=0)
    return _pool_lane_select(lane, s2, s4, s8, s16)


def _winsum_up(ext, lane):
    m = ext.shape[0]
    s2 = ext + pltpu.roll(ext, m - 1, axis=0)
    s4 = s2 + pltpu.roll(s2, m - 2, axis=0)
    s8 = s4 + pltpu.roll(s4, m - 4, axis=0)
    s16 = s8 + pltpu.roll(s8, m - 8, axis=0)
    return _pool_lane_select(lane, s2, s4, s8, s16)


def _pool_count(tile, n):
    lane = _iota((1, 256), 1)
    win = _pool_lane_select(lane, 2.0, 4.0, 8.0, 16.0).astype(F32)
    tpos = (tile * n + _iota((n, 1), 0) + 1).astype(F32)
    return jnp.minimum(tpos, win)


def _chunk_tri(n, upper):
    r, c = _iota((n, n), 0), _iota((n, n), 1)
    tri = (c > r) if upper else (c < r)
    return jnp.where(tri & ((r >> 6) == (c >> 6)), 1.0, 0.0).astype(BF16)


def _chunks(a):
    return [a[c * CH:(c + 1) * CH] for c in range(a.shape[0] // CH)]


def _halves(fn, a, b):
    return jnp.concatenate([fn(a[:, 0:128], b[:, 0:128]), fn(a[:, 128:256], b[:, 128:256])], axis=1)


def _mixer_tile_prep(p_ref, xc, prm_ref, gw_v):
    tail = p_ref[:, C_TL:C_TL + 128]
    pre = _dot(tail, gw_v) + prm_ref[R_GB:R_GB + 1, 0:128]
    la = (jnp.minimum(pre, 0.0) - _lse1(pre)) * INV_TAU
    dtin = tail + prm_ref[R_DTB:R_DTB + 1, 0:128]
    dtf = jnp.maximum(dtin, 0.0) + _lse1(dtin)
    dte = _dot2_l(dtf, _expand_mat())
    da = dte * prm_ref[R_AE:R_AE + 1, 0:256]
    rev = _dot2_r(_chunk_tri(TB, True), jnp.concatenate([la, da], axis=1))
    dec = jnp.exp(rev[:, 0:128])
    kd = p_ref[:, C_GK:C_GK + 128] * dec
    wdec = jnp.exp(rev[:, 128:384])
    w = wdec * dte
    xw = xc[:, 0:256] * w
    d_s = [jnp.exp(_cs(a)) for a in _chunks(la)]
    et = [jnp.exp(_cs(a)) for a in _chunks(da)]
    mask_t = _head_mask_t()
    ut_g = [_dot_tn(v, k) * mask_t for v, k in zip(_chunks(p_ref[:, C_GV:C_GV + 256]), _chunks(kd))]
    ut_s = [_halves(_dot_tn, b, x) for b, x in zip(_chunks(xc[:, 256:512]), _chunks(xw))]
    return tail, pre, dtin, dte, dec, kd, wdec, w, xw, d_s, et, ut_g, ut_s


def _rmsproj(x, nw, wp, name, tm=256, rider=None):
    t = x.shape[0]

    def body(x_ref, nw_ref, w_ref, o_ref):
        xv = x_ref[...]
        rs = lax.rsqrt(jnp.mean(xv * xv, axis=-1, keepdims=True) + EPS)
        h = (xv * rs * nw_ref[...]).astype(BF16)
        o_ref[...] = jnp.dot(h, w_ref[...], preferred_element_type=F32)

    (proj,), extra = _call(
        body, (x, nw, wp), grid=(t // tm,), name=name, sem=("parallel",), rider=rider,
        in_specs=[pl.BlockSpec((tm, D), lambda i: (i, 0)), pl.BlockSpec((1, D), lambda i: (0, 0)),
                  pl.BlockSpec((D, NP), lambda i: (0, 0))],
        out_specs=[pl.BlockSpec((tm, NP), lambda i: (i, 0))], out_shape=[jax.ShapeDtypeStruct((t, NP), F32)])
    return proj, extra


def _outproj(x, mix, wo, name, tm=512):
    t = x.shape[0]

    def body(x_ref, m_ref, w_ref, o_ref):
        o_ref[...] = x_ref[...] + jnp.dot(m_ref[...].astype(BF16), w_ref[...], preferred_element_type=F32)

    return pl.pallas_call(
        body, grid=(t // tm,), name=name,
        in_specs=[pl.BlockSpec((tm, D), lambda i: (i, 0)), pl.BlockSpec((tm, D), lambda i: (i, 0)),
                  pl.BlockSpec((D, D), lambda i: (0, 0))],
        out_specs=pl.BlockSpec((tm, D), lambda i: (i, 0)),
        out_shape=jax.ShapeDtypeStruct((t, D), F32),
        compiler_params=_cparams(("parallel",)),
    )(x, mix, wo)


def _matmul_rows(a, w, name, tm=512):
    t, kdim = a.shape
    n = w.shape[1]

    def body(a_ref, w_ref, o_ref):
        o_ref[...] = jnp.dot(a_ref[...].astype(BF16), w_ref[...], preferred_element_type=F32)

    return pl.pallas_call(
        body, grid=(t // tm,), name=name,
        in_specs=[pl.BlockSpec((tm, kdim), lambda i: (i, 0)), pl.BlockSpec((kdim, n), lambda i: (0, 0))],
        out_specs=pl.BlockSpec((tm, n), lambda i: (i, 0)),
        out_shape=jax.ShapeDtypeStruct((t, n), F32),
        compiler_params=_cparams(("parallel",)),
    )(a, w)


def _head(x, tgt, fw, name, tm=512):
    t = x.shape[0]

    def body(x_ref, t_ref, w_ref, dx_ref, acc_ref):
        @pl.when(pl.program_id(0) == 0)
        def _():
            acc_ref[...] = jnp.zeros_like(acc_ref)

        xv = x_ref[...]
        w = w_ref[...]
        rs = lax.rsqrt(jnp.mean(xv * xv, axis=-1, keepdims=True) + EPS)
        xh = xv * rs
        err = xh * w - t_ref[...]
        dy = err * (1.0 / D)
        dxh = dy * w
        dx_ref[...] = rs * (dxh - xh * jnp.mean(dxh * xh, axis=-1, keepdims=True))
        acc_ref[0:1, :] += _cs(dy * xh)
        acc_ref[1:2, :] += jnp.zeros((1, D), F32) + (0.5 / D) * jnp.sum(err * err)

    return pl.pallas_call(
        body, grid=(t // tm,), name=name,
        in_specs=[pl.BlockSpec((tm, D), lambda i: (i, 0)), pl.BlockSpec((tm, D), lambda i: (i, 0)),
                  pl.BlockSpec((1, D), lambda i: (0, 0))],
        out_specs=[pl.BlockSpec((tm, D), lambda i: (i, 0)), pl.BlockSpec((8, D), lambda i: (0, 0))],
        out_shape=[jax.ShapeDtypeStruct((t, D), F32), jax.ShapeDtypeStruct((8, D), F32)],
        compiler_params=_cparams(("arbitrary",)),
    )(x, tgt, fw)


def _dxin(dp, wpt, x, dxn, nw, name, tm=256, rider=None):
    t = x.shape[0]

    def body(dp_ref, w_ref, x_ref, dxn_ref, nw_ref, dx_ref, dnw_ref):
        @pl.when(pl.program_id(0) == 0)
        def _():
            dnw_ref[...] = jnp.zeros_like(dnw_ref)

        dh = jnp.dot(dp_ref[...].astype(BF16), w_ref[...], preferred_element_type=F32)
        xv = x_ref[...]
        rs = lax.rsqrt(jnp.mean(xv * xv, axis=-1, keepdims=True) + EPS)
        xh = xv * rs
        dnw_ref[0:1, :] += _cs(dh * xh)
        dxh = dh * nw_ref[...]
        dx_ref[...] = dxn_ref[...] + rs * (dxh - xh * jnp.mean(dxh * xh, axis=-1, keepdims=True))

    return _call(
        body, (dp, wpt, x, dxn, nw), grid=(t // tm,), name=name, sem=("arbitrary",), rider=rider,
        in_specs=[pl.BlockSpec((tm, NP), lambda i: (i, 0)), pl.BlockSpec((NP, D), lambda i: (0, 0)),
                  pl.BlockSpec((tm, D), lambda i: (i, 0)), pl.BlockSpec((tm, D), lambda i: (i, 0)),
                  pl.BlockSpec((1, D), lambda i: (0, 0))],
        out_specs=[pl.BlockSpec((tm, D), lambda i: (i, 0)), pl.BlockSpec((8, D), lambda i: (0, 0))],
        out_shape=[jax.ShapeDtypeStruct((t, D), F32), jax.ShapeDtypeStruct((8, D), F32)])


def _dwin(x, nw, dp, name, tm=512, tn=1152, rider=None):
    t = x.shape[0]

    def body(x_ref, nw_ref, dp_ref, o_ref):
        @pl.when(pl.program_id(1) == 0)
        def _():
            o_ref[...] = jnp.zeros_like(o_ref)

        xv = x_ref[...]
        rs = lax.rsqrt(jnp.mean(xv * xv, axis=-1, keepdims=True) + EPS)
        h = xv * rs * nw_ref[...]
        o_ref[...] += _dot_tn(h, dp_ref[...])

    (dwp,), extra = _call(
        body, (x, nw, dp), grid=(NP // tn, t // tm), name=name, sem=("parallel", "arbitrary"), rider=rider,
        in_specs=[pl.BlockSpec((tm, D), lambda j, i: (i, 0)), pl.BlockSpec((1, D), lambda j, i: (0, 0)),
                  pl.BlockSpec((tm, tn), lambda j, i: (i, j))],
        out_specs=[pl.BlockSpec((D, tn), lambda j, i: (0, j))], out_shape=[jax.ShapeDtypeStruct((D, NP), F32)])
    return dwp, extra


def _dwout(mix, dxn, name, tm=512):
    t = mix.shape[0]

    def body(m_ref, g_ref, o_ref):
        @pl.when(pl.program_id(0) == 0)
        def _():
            o_ref[...] = jnp.zeros_like(o_ref)

        o_ref[...] += _dot_tn(m_ref[...], g_ref[...])

    return pl.pallas_call(
        body, grid=(t // tm,), name=name,
        in_specs=[pl.BlockSpec((tm, D), lambda i: (i, 0)), pl.BlockSpec((tm, D), lambda i: (i, 0))],
        out_specs=pl.BlockSpec((D, D), lambda i: (0, 0)),
        out_shape=jax.ShapeDtypeStruct((D, D), F32),
        compiler_params=_cparams(("arbitrary",)),
    )(mix, dxn)


def _mixer_fwd(proj, prm, gw, pw, name, rider=None):
    t = proj.shape[0]
    nt, nc = t // TB, t // CH

    def body(p_ref, prm_ref, gw_ref, pw_ref, mix_ref, sg_ref, ss_ref, sg_s, ss_s, h_ua, h_pu, h_sx):
        i = pl.program_id(0)

        @pl.when(i == 0)
        def _():
            for r in (sg_s, ss_s, h_ua, h_pu, h_sx):
                r[...] = jnp.zeros_like(r)

        lane = _iota((1, 256), 1)
        u = p_ref[:, C_AC:C_AC + 256] * p_ref[:, C_AH:C_AH + 256]
        ext = jnp.concatenate([h_ua[...], u], axis=0)
        cv = (prm_ref[R_CAW + 2:R_CAW + 3, 0:256] * u + prm_ref[R_CAW + 1:R_CAW + 2, 0:256] * _dn(ext, 1, TB, 8)
              + prm_ref[R_CAW:R_CAW + 1, 0:256] * _dn(ext, 2, TB, 8))
        mix_ref[:, 0:256] = p_ref[:, C_AB:C_AB + 256] * cv * _silu(p_ref[:, C_AZ:C_AZ + 256])
        h_ua[...] = u[TB - 8:, :]
        pu = p_ref[:, C_PU:C_PU + 256]
        ext = jnp.concatenate([h_pu[...], pu], axis=0)
        pooled = _winsum_dn(ext, lane)[16:] / _pool_count(i, TB) - pu
        mixed = _dot(pooled, pw_ref[...])
        mix_ref[:, 512:768] = prm_ref[R_PSC:R_PSC + 1, 0:256] * mixed * _silu(p_ref[:, C_PZ:C_PZ + 256])
        h_pu[...] = pu[TB - 16:, :]
        sx = p_ref[:, C_SX:C_SX + 768]
        ext = jnp.concatenate([h_sx[...], sx], axis=0)
        xc = _silu(prm_ref[R_SCW + 3:R_SCW + 4, :] * sx + prm_ref[R_SCW + 2:R_SCW + 3, :] * _dn(ext, 1, TB, 8)
                   + prm_ref[R_SCW + 1:R_SCW + 2, :] * _dn(ext, 2, TB, 8) + prm_ref[R_SCW:R_SCW + 1, :] * _dn(ext, 3, TB, 8)
                   + prm_ref[R_SCB:R_SCB + 1, :])
        h_sx[...] = sx[TB - 8:, :]

        _, _, _, _, _, _, _, _, _, d_s, et, ut_g, ut_s = _mixer_tile_prep(p_ref, xc, prm_ref, gw_ref[...])
        s_g, s_s = sg_s[...], ss_s[...]
        o, y = [], []
        qs = _chunks(p_ref[:, C_GQ:C_GQ + 128] * GLA_SCALE)
        cm = _chunks(xc[:, 512:768])
        for c in range(NCH):
            sg_ref[c] = s_g
            ss_ref[c] = s_s
            s_g = s_g * d_s[c] + ut_g[c]
            s_s = s_s * et[c] + ut_s[c]
            o.append(_dot_nt(qs[c], s_g))
            y.append(_halves(_dot, cm[c], s_s))
        sg_s[...] = s_g
        ss_s[...] = s_s
        o = jnp.concatenate(o, axis=0)
        on = o * lax.rsqrt(_dot2_l(o * o, _group_mean_mat()) + EPS)
        mix_ref[:, 256:512] = on * prm_ref[R_GNW:R_GNW + 1, 0:256] * _silu(p_ref[:, C_GZ:C_GZ + 256])
        y2 = ((jnp.concatenate(y, axis=0) + prm_ref[R_DE:R_DE + 1, 0:256] * xc[:, 0:256])
              * _silu(p_ref[:, C_SZ:C_SZ + 256]))
        mix_ref[:, 768:1024] = (y2 * lax.rsqrt(jnp.mean(y2 * y2, axis=-1, keepdims=True) + EPS)
                                * prm_ref[R_SNW:R_SNW + 1, 0:256])

    return _call(
        body, (proj, prm, gw, pw), grid=(nt,), name=name, sem=("arbitrary",), rider=rider,
        in_specs=[pl.BlockSpec((TB, NP), lambda i: (i, 0)), pl.BlockSpec((16, 768), lambda i: (0, 0)),
                  pl.BlockSpec((128, 128), lambda i: (0, 0)), pl.BlockSpec((256, 256), lambda i: (0, 0))],
        out_specs=[pl.BlockSpec((TB, D), lambda i: (i, 0)), pl.BlockSpec((NCH, 256, 128), lambda i: (i, 0, 0)),
                   pl.BlockSpec((NCH, 128, 256), lambda i: (i, 0, 0))],
        out_shape=[jax.ShapeDtypeStruct((t, D), F32), jax.ShapeDtypeStruct((nc, 256, 128), F32),
                   jax.ShapeDtypeStruct((nc, 128, 256), F32)],
        scratch_shapes=[pltpu.VMEM((256, 128), F32), pltpu.VMEM((128, 256), F32), pltpu.VMEM((8, 256), F32),
                        pltpu.VMEM((16, 256), F32), pltpu.VMEM((8, 768), F32)])


def _mixer_bwd(proj, dmix, sg, ss, prm, gw, pw, name, rider=None):
    t = proj.shape[0]
    nt = t // TB
    rev = lambda i: nt - 1 - i

    def body(p_ref, hp_ref, dm_ref, sg_ref, ss_ref, prm_ref, gw_ref, pw_ref, dp_ref, sgc_ref,
             gg_s, gs_s, h_dcv, h_dpl, h_dpre, gsm_ref, dgw_ref, dpw_ref):
        i = pl.program_id(0)
        tile = nt - 1 - i

        @pl.when(i == 0)
        def _():
            for r in (gg_s, gs_s, h_dcv, h_dpl, h_dpre, gsm_ref, dgw_ref, dpw_ref):
                r[...] = jnp.zeros_like(r)

        lane = _iota((1, 256), 1)
        first = (tile > 0).astype(F32)
        ah, ac = p_ref[:, C_AH:C_AH + 256], p_ref[:, C_AC:C_AC + 256]
        ab, az = p_ref[:, C_AB:C_AB + 256], p_ref[:, C_AZ:C_AZ + 256]
        w0, w1, w2 = (prm_ref[R_CAW + j:R_CAW + j + 1, 0:256] for j in range(3))
        u = ac * ah
        ext = jnp.concatenate([hp_ref[8:16, C_AC:C_AC + 256] * hp_ref[8:16, C_AH:C_AH + 256] * first, u], axis=0)
        u1, u2 = _dn(ext, 1, TB, 8), _dn(ext, 2, TB, 8)
        cv = w2 * u + w1 * u1 + w0 * u2
        g = dm_ref[:, 0:256]
        sz = _silu(az)
        dp_ref[:, C_AB:C_AB + 256] = g * cv * sz
        dp_ref[:, C_AZ:C_AZ + 256] = g * ab * cv * _dsilu(az)
        dcv = g * ab * sz
        dext = jnp.concatenate([dcv, h_dcv[...]], axis=0)
        du = w2 * dcv + w1 * _up(dext, 1, TB) + w0 * _up(dext, 2, TB)
        dp_ref[:, C_AC:C_AC + 256] = du * ah
        dp_ref[:, C_AH:C_AH + 256] = du * ac
        gsm_ref[R_CAW:R_CAW + 1, 0:256] += _cs(dcv * u2)
        gsm_ref[R_CAW + 1:R_CAW + 2, 0:256] += _cs(dcv * u1)
        gsm_ref[R_CAW + 2:R_CAW + 3, 0:256] += _cs(dcv * u)
        h_dcv[...] = dcv[0:8, :]
        pu, pz = p_ref[:, C_PU:C_PU + 256], p_ref[:, C_PZ:C_PZ + 256]
        psc = prm_ref[R_PSC:R_PSC + 1, 0:256]
        cnt = _pool_count(tile, TB)
        ext = jnp.concatenate([hp_ref[:, C_PU:C_PU + 256] * first, pu], axis=0)
        pooled = _winsum_dn(ext, lane)[16:] / cnt - pu
        pw_v = pw_ref[...]
        mixed = _dot(pooled, pw_v)
        g = dm_ref[:, 512:768]
        sz = _silu(pz)
        gsm_ref[R_PSC:R_PSC + 1, 0:256] += _cs(g * mixed * sz)
        dp_ref[:, C_PZ:C_PZ + 256] = g * psc * mixed * _dsilu(pz)
        dmixed = g * psc * sz
        dpw_ref[...] += _dot_tn(pooled, dmixed)
        dpooled = _dot_nt(dmixed, pw_v)
        qd = dpooled / cnt
        dext = jnp.concatenate([qd, h_dpl[...]], axis=0)
        dp_ref[:, C_PU:C_PU + 256] = _winsum_up(dext, lane)[:TB] - dpooled
        h_dpl[...] = qd[0:16, :]
        sx = p_ref[:, C_SX:C_SX + 768]
        cw = [prm_ref[R_SCW + j:R_SCW + j + 1, :] for j in range(4)]
        ext = jnp.concatenate([hp_ref[8:16, C_SX:C_SX + 768] * first, sx], axis=0)
        sx1, sx2, sx3 = _dn(ext, 1, TB, 8), _dn(ext, 2, TB, 8), _dn(ext, 3, TB, 8)
        cpre = cw[3] * sx + cw[2] * sx1 + cw[1] * sx2 + cw[0] * sx3 + prm_ref[R_SCB:R_SCB + 1, :]
        xc = _silu(cpre)
        xs, bm, cm = xc[:, 0:256], xc[:, 256:512], xc[:, 512:768]

        gw_v = gw_ref[...]
        tail, pre, dtin, dte, dec, kd, wdec, w, xw, d_s, et, ut_g, ut_s = _mixer_tile_prep(p_ref, xc, prm_ref, gw_v)
        gmean = _group_mean_mat()
        mask_t = _head_mask_t()
        gnw = prm_ref[R_GNW:R_GNW + 1, 0:256]
        a_e = prm_ref[R_AE:R_AE + 1, 0:256]
        d_e = prm_ref[R_DE:R_DE + 1, 0:256]
        snw = prm_ref[R_SNW:R_SNW + 1, 0:256]
        sg_in = [sg_ref[c] for c in range(NCH)]
        ss_in = [ss_ref[c] for c in range(NCH)]
        sg_n = [sg_in[c] * d_s[c] + ut_g[c] for c in range(NCH)]
        ss_n = [ss_in[c] * et[c] + ut_s[c] for c in range(NCH)]
        qs = _chunks(p_ref[:, C_GQ:C_GQ + 128] * GLA_SCALE)
        cm_c, bm_c, xw_c, kd_c = _chunks(cm), _chunks(bm), _chunks(xw), _chunks(kd)
        v_c = _chunks(p_ref[:, C_GV:C_GV + 256])
        o = jnp.concatenate([_dot_nt(qs[c], sg_n[c]) for c in range(NCH)], axis=0)
        y = jnp.concatenate([_halves(_dot, cm_c[c], ss_n[c]) for c in range(NCH)], axis=0) + d_e * xs
        gz = p_ref[:, C_GZ:C_GZ + 256]
        r = lax.rsqrt(_dot2_l(o * o, gmean) + EPS)
        on = o * r
        dyb = dm_ref[:, 256:512]
        dp_ref[:, C_GZ:C_GZ + 256] = dyb * on * gnw * _dsilu(gz)
        tg = dyb * _silu(gz)
        gsm_ref[R_GNW:R_GNW + 1, 0:256] += _cs(tg * on)
        don = tg * gnw
        do_c = _chunks(r * (don - on * _dot2_l(don * on, gmean)))
        ssz = p_ref[:, C_SZ:C_SZ + 256]
        sil = _silu(ssz)
        y2 = y * sil
        r = lax.rsqrt(jnp.mean(y2 * y2, axis=-1, keepdims=True) + EPS)
        yn = y2 * r
        dyd = dm_ref[:, 768:1024]
        gsm_ref[R_SNW:R_SNW + 1, 0:256] += _cs(dyd * yn)
        dn = dyd * snw
        dy2 = r * (dn - yn * jnp.mean(dn * yn, axis=-1, keepdims=True))
        dp_ref[:, C_SZ:C_SZ + 256] = dy2 * y * _dsilu(ssz)
        dy = dy2 * sil
        gsm_ref[R_DE:R_DE + 1, 0:256] += _cs(dy * xs)
        dy_c = _chunks(dy)
        dp_ref[:, C_GQ:C_GQ + 128] = jnp.concatenate([_dot(do_c[c], sg_n[c]) for c in range(NCH)], axis=0) * GLA_SCALE
        dcm = jnp.concatenate([_halves(_dot_nt, dy_c[c], ss_n[c]) for c in range(NCH)], axis=0)
        gg = [_dot_tn(do_c[c], qs[c]) * mask_t for c in range(NCH)]
        gs = [_halves(_dot_tn, cm_c[c], dy_c[c]) for c in range(NCH)]
        car_g, car_s = gg_s[...], gs_s[...]
        for c in reversed(range(NCH)):
            gg[c] = gg[c] + car_g
            gs[c] = gs[c] + car_s
            car_g = gg[c] * d_s[c]
            car_s = gs[c] * et[c]
        gg_s[...] = car_g
        gs_s[...] = car_s
        dkd = jnp.concatenate([_dot(v_c[c], gg[c]) for c in range(NCH)], axis=0)
        dp_ref[:, C_GV:C_GV + 256] = jnp.concatenate([_dot_nt(kd_c[c], gg[c]) for c in range(NCH)], axis=0)
        dp_ref[:, C_GK:C_GK + 128] = dkd * dec
        dbm = jnp.concatenate([_halves(_dot_nt, xw_c[c], gs[c]) for c in range(NCH)], axis=0)
        dxw = jnp.concatenate([_halves(_dot, bm_c[c], gs[c]) for c in range(NCH)], axis=0)
        dxs = dy * d_e + dxw * w
        dw = dxw * xs
        dsuf = _dot2_r(_chunk_tri(TB, False), jnp.concatenate([dkd * kd, dw * dte * wdec], axis=1))
        tot_g = jnp.concatenate([jnp.broadcast_to(_cs(gg[c] * sg_in[c]) * d_s[c], (CH, 128)) for c in range(NCH)], axis=0)
        tot_s = jnp.concatenate([jnp.broadcast_to(_cs(gs[c] * ss_in[c]) * et[c], (CH, 256)) for c in range(NCH)], axis=0)
        dpre = (dsuf[:, 0:128] + tot_g) * INV_TAU * jax.nn.sigmoid(-pre)
        dgw_ref[...] += _dot_tn(tail, dpre)
        gsm_ref[R_GB:R_GB + 1, 0:128] += _cs(dpre)
        dda = dsuf[:, 128:384] + tot_s
        gsm_ref[R_AE:R_AE + 1, 0:256] += _cs(dda * dte)
        dtail_s = _dot2_nt(dw * wdec + dda * a_e, _expand_mat()) * jax.nn.sigmoid(dtin)
        gsm_ref[R_DTB:R_DTB + 1, 0:128] += _cs(dtail_s)
        dp_ref[:, C_TL:C_TL + 128] = _dot_nt(dpre, gw_v) + dtail_s
        dpre_c = jnp.concatenate([dxs, dbm, dcm], axis=1) * _dsilu(cpre)
        dext = jnp.concatenate([dpre_c, h_dpre[...]], axis=0)
        dp_ref[:, C_SX:C_SX + 768] = (cw[3] * dpre_c + cw[2] * _up(dext, 1, TB) + cw[1] * _up(dext, 2, TB)
                                      + cw[0] * _up(dext, 3, TB))
        gsm_ref[R_SCW + 3:R_SCW + 4, :] += _cs(dpre_c * sx)
        gsm_ref[R_SCW + 2:R_SCW + 3, :] += _cs(dpre_c * sx1)
        gsm_ref[R_SCW + 1:R_SCW + 2, :] += _cs(dpre_c * sx2)
        gsm_ref[R_SCW:R_SCW + 1, :] += _cs(dpre_c * sx3)
        gsm_ref[R_SCB:R_SCB + 1, :] += _cs(dpre_c)
        h_dpre[...] = dpre_c[0:8, :]

        @pl.when(i == nt - 1)
        def _():
            ri, ci = _iota((256, 256), 0), _iota((256, 256), 1)
            per_head = jnp.where((ri >> 6) == ci, 1.0, 0.0).astype(BF16)
            per_dv = jnp.where((ri & 63) == ci, 1.0, 0.0).astype(BF16)
            row = _iota((8, 256), 0)
            top = gsm_ref[0:8, 0:256]
            sgc_ref[0:8, 0:256] = jnp.where(row == R_GNW, _dot3_l(top, per_dv), top)
            bot = gsm_ref[8:16, 0:256]
            fold = _dot3_l(jnp.where(row == R_AE - 8, bot * a_e, bot), per_head)
            sgc_ref[8:16, 0:256] = jnp.where((row == R_AE - 8) | (row == R_DE - 8), fold, bot)
            sgc_ref[0:16, 256:768] = gsm_ref[:, 256:768]
            sgc_ref[0:16, 768:896] = dgw_ref[0:16, :]
            sgc_ref[0:16, 896:1024] = jnp.zeros((16, 128), F32)
            diag = _pool_lane_select(lane, dpw_ref[0:64, :], dpw_ref[64:128, :], dpw_ref[128:192, :], dpw_ref[192:256, :])
            for q in range(4):
                sgc_ref[16:32, 256 * q:256 * q + 256] = diag[16 * q:16 * q + 16, :]

    return _call(
        body, (proj, proj, dmix, sg, ss, prm, gw, pw), grid=(nt,), name=name, sem=("arbitrary",), rider=rider,
        in_specs=[pl.BlockSpec((TB, NP), lambda i: (rev(i), 0)),
                  pl.BlockSpec((16, HALO_W), lambda i: (jnp.maximum(rev(i) * (TB // 16) - 1, 0), 0)),
                  pl.BlockSpec((TB, D), lambda i: (rev(i), 0)),
                  pl.BlockSpec((NCH, 256, 128), lambda i: (rev(i), 0, 0)),
                  pl.BlockSpec((NCH, 128, 256), lambda i: (rev(i), 0, 0)),
                  pl.BlockSpec((16, 768), lambda i: (0, 0)), pl.BlockSpec((128, 128), lambda i: (0, 0)),
                  pl.BlockSpec((256, 256), lambda i: (0, 0))],
        out_specs=[pl.BlockSpec((TB, NP), lambda i: (rev(i), 0)), pl.BlockSpec((32, 1024), lambda i: (0, 0))],
        out_shape=[jax.ShapeDtypeStruct((t, NP), F32), jax.ShapeDtypeStruct((32, 1024), F32)],
        scratch_shapes=[pltpu.VMEM((256, 128), F32), pltpu.VMEM((128, 256), F32), pltpu.VMEM((8, 256), F32),
                        pltpu.VMEM((16, 256), F32), pltpu.VMEM((8, 768), F32), pltpu.VMEM((16, 768), F32),
                        pltpu.VMEM((128, 128), F32), pltpu.VMEM((256, 256), F32)])


def _half(c, n):
    return pl.ds(pl.multiple_of(c * (n // 2), n // 2), n // 2)


def _other_chips(x, y):
    return ((1 - x, y), (x, 1 - y), (1 - x, 1 - y))


def _remote(src, dst, send, recv, k, dev):
    return pltpu.make_async_remote_copy(src_ref=src, dst_ref=dst, send_sem=send.at[k], recv_sem=recv.at[k], device_id=dev,
                                        device_id_type=MESH)


def _sem(n):
    return pltpu.SemaphoreType.DMA((n,))


def _rider_gather_ici(shards, extra=None):
    shards = tuple(shards) + ((extra,) if extra is not None else ())
    n = len(shards)

    def copies(rins, routs, sems, arrivals=True):
        send, recv, loc = sems
        x, y, c = _place()
        me = 2 * x + y
        own = [pltpu.make_async_copy(rins[k], routs[k].at[me], loc.at[k]) for k in range(n)]
        out, inc = [], []
        for j, (px, py) in enumerate(_other_chips(x, y)):
            for k in range(n):
                whole = extra is not None and k == n - 1
                rows = pl.ds(0, shards[k].shape[0]) if whole else _half(c, shards[k].shape[0])
                out.append(_remote(rins[k].at[rows], routs[k].at[me, rows], send, recv, n * j + k, (px, py, c)))
                if arrivals:
                    inc.append(_remote(rins[k].at[rows], routs[k].at[2 * px + py, rows], send, recv, n * j + k, (px, py, c)))
        return own, out, inc

    def start(rins, routs, sems):
        own, out, _ = copies(rins, routs, sems, arrivals=False)
        for cp in own + out:
            cp.start()

    def finish(rins, routs, sems):
        own, out, inc = copies(rins, routs, sems)
        for cp in inc:
            cp.wait_recv()
        for cp in out:
            cp.wait_send()
        for cp in own:
            cp.wait()

    return _Rider(shards, [jax.ShapeDtypeStruct((4,) + a.shape, a.dtype) for a in shards],
                  [_sem(3 * n), _sem(3 * n), _sem(n)], start, finish)


def _rider_gather_d2d(slabs):
    slabs = tuple(slabs)
    n = len(slabs)

    def copies(routs, sems, arrivals=True):
        send, recv = sems
        x, y, c = _place()
        out, inc = [], []
        for j, (px, py) in enumerate(_other_chips(x, y)):
            for k in range(n):
                rows = slabs[k].shape[1]
                mine, theirs = routs[k].at[2 * px + py, _half(c, rows)], routs[k].at[2 * px + py, _half(1 - c, rows)]
                out.append(_remote(mine, mine, send, recv, n * j + k, (x, y, 1 - c)))
                if arrivals:
                    inc.append(_remote(theirs, theirs, send, recv, n * j + k, (x, y, 1 - c)))
        return out, inc

    def start(rins, routs, sems):
        for cp in copies(routs, sems, arrivals=False)[0]:
            cp.start()

    def finish(rins, routs, sems):
        out, inc = copies(routs, sems)
        for cp in inc:
            cp.wait_recv()
        for cp in out:
            cp.wait_send()

    return _Rider(slabs, [jax.ShapeDtypeStruct(a.shape, a.dtype) for a in slabs], [_sem(3 * n), _sem(3 * n)], start, finish,
                  aliases={k: k for k in range(n)})


def _rider_swap(parts):
    parts = tuple(parts)
    n = len(parts)

    def copies(rins, routs, sems):
        send, recv = sems
        x, y, c = _place()
        return [_remote(rins[k].at[:, _half(1 - c, parts[k].shape[1])], routs[k], send, recv, k, (x, y, 1 - c))
                for k in range(n)]

    def start(rins, routs, sems):
        for cp in copies(rins, routs, sems):
            cp.start()

    def finish(rins, routs, sems):
        for cp in copies(rins, routs, sems):
            cp.wait()

    return _Rider(parts, [jax.ShapeDtypeStruct((4, a.shape[1] // 2, a.shape[2]), a.dtype) for a in parts],
                  [_sem(n), _sem(n)], start, finish)


def _rider_scatter(parts):
    parts = tuple(parts)
    n = len(parts)

    def copies(rins, routs, sems, arrivals=True):
        send, recv, loc = sems
        x, y, c = _place()
        me = 2 * x + y
        own = [pltpu.make_async_copy(rins[k].at[me], routs[k].at[me], loc.at[k]) for k in range(n)]
        out, inc = [], []
        for j, (px, py) in enumerate(_other_chips(x, y)):
            for k in range(n):
                out.append(_remote(rins[k].at[2 * px + py], routs[k].at[me], send, recv, n * j + k, (px, py, c)))
                if arrivals:
                    inc.append(_remote(rins[k].at[me], routs[k].at[2 * px + py], send, recv, n * j + k, (px, py, c)))
        return own, out, inc

    def start(rins, routs, sems):
        own, out, _ = copies(rins, routs, sems, arrivals=False)
        for cp in own + out:
            cp.start()

    def finish(rins, routs, sems):
        own, out, inc = copies(rins, routs, sems)
        for cp in inc:
            cp.wait_recv()
        for cp in out:
            cp.wait_send()
        for cp in own:
            cp.wait()

    return _Rider(parts, [jax.ShapeDtypeStruct(a.shape, a.dtype) for a in parts], [_sem(3 * n), _sem(3 * n), _sem(n)],
                  start, finish)


def _rider_share(halves, layer, prev=None):
    halves = tuple(halves)
    n = len(halves)
    inputs = halves + (tuple(prev) if prev is not None else ())

    def copies(rins, routs, sems, arrivals=True):
        send, recv, loc = sems
        x, y, c = _place()
        own, out, inc = [], [], []
        for k in range(n):
            rows = 2 * halves[k].shape[0]
            own.append(pltpu.make_async_copy(rins[k], routs[k].at[layer, _half(c, rows)], loc.at[k]))
            out.append(_remote(rins[k], routs[k].at[layer, _half(c, rows)], send, recv, k, (x, y, 1 - c)))
            if arrivals:
                inc.append(_remote(rins[k], routs[k].at[layer, _half(1 - c, rows)], send, recv, k, (x, y, 1 - c)))
        return own, out, inc

    def start(rins, routs, sems):
        own, out, _ = copies(rins, routs, sems, arrivals=False)
        for cp in own + out:
            cp.start()

    def finish(rins, routs, sems):
        own, out, inc = copies(rins, routs, sems)
        for cp in inc:
            cp.wait_recv()
        for cp in out:
            cp.wait_send()
        for cp in own:
            cp.wait()

    return _Rider(inputs, [jax.ShapeDtypeStruct((2, 2 * a.shape[0], a.shape[1]), a.dtype) for a in halves],
                  [_sem(n), _sem(n), _sem(n)], start, finish,
                  aliases={n + k: k for k in range(n)} if prev is not None else None)


def _pair_sum(core, full, recv, name, br=128):
    n, rows, cols = recv.shape

    def body(c_ref, a_ref, b_ref, o_ref):
        o_ref[...] = (a_ref[...] + b_ref[...]).astype(BF16)

    nb = rows // br
    return pl.pallas_call(
        body, name=name, out_shape=jax.ShapeDtypeStruct(recv.shape, BF16),
        grid_spec=pltpu.PrefetchScalarGridSpec(
            num_scalar_prefetch=1, grid=(n, nb),
            in_specs=[pl.BlockSpec((1, br, cols), lambda i, j, c: (i, c[0] * nb + j, 0)),
                      pl.BlockSpec((1, br, cols), lambda i, j, c: (i, j, 0))],
            out_specs=pl.BlockSpec((1, br, cols), lambda i, j, c: (i, j, 0))),
        compiler_params=_cparams(("parallel", "parallel")))(core, full, recv)


def _sum4(a, name, br=128):
    _, r, c = a.shape

    def body(a_ref, o_ref):
        o_ref[...] = ((a_ref[0].astype(F32) + a_ref[1].astype(F32)) + a_ref[2].astype(F32)) + a_ref[3].astype(F32)

    return pl.pallas_call(body, grid=(r // br,), name=name,
                          in_specs=[pl.BlockSpec((4, br, c), lambda i: (0, i, 0))],
                          out_specs=pl.BlockSpec((br, c), lambda i: (i, 0)),
                          out_shape=jax.ShapeDtypeStruct((r, c), F32),
                          compiler_params=_cparams(("parallel",)))(a)


def _adamw(w, g, m, v, name, br):
    n, r, c = w.shape

    def body(w_ref, g_ref, m_ref, v_ref, d_ref, m2_ref, v2_ref):
        gv = g_ref[...]
        m2 = ADAM_B1 * m_ref[...] + (1.0 - ADAM_B1) * gv
        v2 = ADAM_B2 * v_ref[...] + (1.0 - ADAM_B2) * (gv * gv)
        m_hat = m2 / (1.0 - ADAM_B1 ** ADAM_STEP)
        v_hat = v2 / (1.0 - ADAM_B2 ** ADAM_STEP)
        d_ref[...] = -ADAM_LR * (m_hat / (jnp.sqrt(v_hat) + ADAM_EPS) + ADAM_WD * w_ref[...])
        m2_ref[...] = m2
        v2_ref[...] = v2

    spec = pl.BlockSpec((1, br, c), lambda i, j: (i, j, 0))
    shp = jax.ShapeDtypeStruct(w.shape, F32)
    return pl.pallas_call(body, grid=(n, r // br), name=name, in_specs=[spec] * 4, out_specs=[spec] * 3,
                          out_shape=[shp] * 3, compiler_params=_cparams(("parallel", "parallel")))(w, g, m, v)


_SMALL_NAMES = ("norm_w", "conv_a_w", "gla_gate_w", "gla_gate_b", "gla_norm_w", "pool_w", "pool_scale", "ssd_conv_w",
                "ssd_conv_b", "ssd_dt_bias", "ssd_a_log", "ssd_d", "ssd_norm_w", "final_norm_w")
SMALL_ROWS = 72


def _adam_math(w, g, m, v):
    m2 = ADAM_B1 * m + (1.0 - ADAM_B1) * g
    v2 = ADAM_B2 * v + (1.0 - ADAM_B2) * (g * g)
    m_hat = m2 / (1.0 - ADAM_B1 ** ADAM_STEP)
    v_hat = v2 / (1.0 - ADAM_B2 ** ADAM_STEP)
    return -ADAM_LR * (m_hat / (jnp.sqrt(v_hat) + ADAM_EPS) + ADAM_WD * w), m2, v2


def _small_slices(name, chip):
    if name == "conv_a_w":
        return [((), slice(R_CAW, R_CAW + 3), slice(64 * chip, 64 * chip + 64))]
    if name == "ssd_conv_w":
        return [((), slice(R_SCW, R_SCW + 4), slice(192 * chip, 192 * chip + 192))]
    if name == "gla_gate_w":
        return [((), slice(0, 16), slice(768, 896))]
    if name == "pool_w":
        return [((g, slice(16 * q, 16 * q + 16)), slice(16, 32), slice(256 * q + 64 * g, 256 * q + 64 * g + 64))
                for g in range(4) for q in range(4)]
    row, lanes = {"gla_gate_b": (R_GB, slice(0, 128)), "gla_norm_w": (R_GNW, slice(0, 64)),
                  "pool_scale": (R_PSC, slice(0, 256)), "ssd_conv_b": (R_SCB, slice(0, 768)),
                  "ssd_dt_bias": (R_DTB, slice(16, 20)), "ssd_a_log": (R_AE, slice(0, 4)), "ssd_d": (R_DE, slice(0, 4)),
                  "ssd_norm_w": (R_SNW, slice(0, 256))}[name]
    return [((), slice(row, row + 1), lanes)]


def _small_step(sg0, sg1, dnw0, dnw1, head, w, m, v):
    n = len(_SMALL_NAMES)

    def body(*refs):
        sg0_ref, sg1_ref, dnw0_ref, dnw1_ref, head_ref = refs[0:5]
        w_refs, m_refs, v_refs = refs[5:5 + n], refs[5 + n:5 + 2 * n], refs[5 + 2 * n:5 + 3 * n]
        o = 5 + 3 * n
        g_out, d_out, m_out, v_out = refs[o:o + n], refs[o + n:o + 2 * n], refs[o + 2 * n:o + 3 * n], refs[o + 3 * n:o + 4 * n]
        loss_ref = refs[o + 4 * n]
        stage, pair, rbuf, acc, send_sems, recv_sems = refs[o + 4 * n + 1:]
        x, y, c = _place()
        chip = 2 * x + y
        stage[0:32, :] = sg0_ref[...]
        stage[32:64, :] = sg1_ref[...]
        stage[64:65, :] = dnw0_ref[0:1, :]
        stage[65:66, :] = dnw1_ref[0:1, :]
        stage[66:68, :] = head_ref[0:2, :]
        stage[68:72, :] = jnp.zeros((4, D), F32)
        sib = pltpu.make_async_remote_copy(src_ref=stage, dst_ref=pair, send_sem=send_sems.at[0], recv_sem=recv_sems.at[0],
                                           device_id=(x, y, 1 - c), device_id_type=MESH)
        sib.start()
        sib.wait()
        rbuf[0] = stage[...] + pair[...]
        sends = []
        for k, (px, py) in enumerate(((1 - x, y), (x, 1 - y), (1 - x, 1 - y)), start=1):
            cp = pltpu.make_async_remote_copy(src_ref=rbuf.at[0], dst_ref=rbuf.at[k], send_sem=send_sems.at[k],
                                              recv_sem=recv_sems.at[k], device_id=(px, py, c), device_id_type=MESH)
            cp.start()
            sends.append(cp)
        for cp in sends:
            cp.wait()
        slab = lambda s: jnp.where(s == 0, 0, jnp.where(s == 2, 1, jnp.where(s == 1, 2, 3)))
        total = rbuf[slab(jnp.bitwise_xor(chip, 0))]
        for s in range(1, 4):
            total = total + rbuf[slab(jnp.bitwise_xor(chip, s))]
        acc[...] = total
        loss_ref[...] = acc[67:68, 0:1]

        def update(i, idx, g):
            wv, mv, vv = w_refs[i][idx], m_refs[i][idx], v_refs[i][idx]
            d, m2, v2 = _adam_math(wv, g, mv, vv)
            g_out[i][idx], d_out[i][idx], m_out[i][idx], v_out[i][idx] = g, d, m2, v2

        for i, name in enumerate(_SMALL_NAMES):
            if name == "final_norm_w":
                update(i, (slice(0, 1), slice(None)), acc[66:67, :])
            elif name == "norm_w":
                for l in range(2):
                    update(i, (slice(l, l + 1), slice(None)), acc[64 + l:65 + l, :])
            elif name in ("conv_a_w", "ssd_conv_w"):
                for s in range(4):
                    @pl.when(chip == s)
                    def _(i=i, name=name, s=s):
                        for l in range(2):
                            (_, rows, lanes), = _small_slices(name, s)
                            update(i, (l,), acc[rows.start + 32 * l:rows.stop + 32 * l, lanes])
            else:
                for l in range(2):
                    for idx, rows, lanes in _small_slices(name, 0):
                        g = acc[rows.start + 32 * l:rows.stop + 32 * l, lanes]
                        if w_refs[i].ndim == 2:
                            update(i, (slice(l, l + 1), slice(None)), g)
                        else:
                            update(i, (l,) + idx, g)

    args = [sg0, sg1, dnw0, dnw1, head] + [d[k] for d in (w, m, v) for k in _SMALL_NAMES]
    shapes = [jax.ShapeDtypeStruct(w[k].shape, F32) for k in _SMALL_NAMES]
    vmem = pl.BlockSpec(memory_space=pltpu.VMEM)
    outs = pl.pallas_call(
        body, name="small_allreduce_adamw", in_specs=[vmem] * len(args), out_specs=[vmem] * (4 * n + 1),
        out_shape=shapes * 4 + [jax.ShapeDtypeStruct((1, 1), F32)],
        scratch_shapes=[pltpu.VMEM((SMALL_ROWS, D), F32), pltpu.VMEM((SMALL_ROWS, D), F32),
                        pltpu.VMEM((4, SMALL_ROWS, D), F32), pltpu.VMEM((SMALL_ROWS, D), F32),
                        pltpu.SemaphoreType.DMA((4,)), pltpu.SemaphoreType.DMA((4,))],
    )(*args)
    return outs[0:n], outs[n:2 * n], outs[2 * n:3 * n], outs[3 * n:4 * n], outs[4 * n]


def _permute_cols(w):
    parts = [w[..., s:s + n] for s, n in _PERM]
    parts.append(jnp.zeros(w.shape[:-1] + (NP - NPROJ,), w.dtype))
    return jnp.concatenate(parts, axis=-1)


def _unpermute_cols(w):
    return jnp.concatenate([w[..., s:s + n] for s, n in _UNPERM], axis=-1)


def _mixer_consts(layer, conv_a_w, gla_gate_w, gla_gate_b, gla_norm_w, pool_w, pool_scale, ssd_conv_w, ssd_conv_b,
                  ssd_dt_bias, ssd_a_log, ssd_d, ssd_norm_w):
    def row(v):
        return jnp.pad(v.reshape(1, -1), ((0, 0), (0, 768 - v.size)))

    dtb = jnp.zeros((128,), F32).at[16:20].set(ssd_dt_bias[layer])
    rows = [jnp.pad(conv_a_w[layer], ((0, 0), (0, 512))), row(gla_gate_b[layer]), row(jnp.tile(gla_norm_w[layer], 4)),
            row(pool_scale[layer]), row(ssd_conv_b[layer]), row(dtb), row(jnp.repeat(-jnp.exp(ssd_a_log[layer]), 64)),
            row(jnp.repeat(ssd_d[layer], 64)), row(ssd_norm_w[layer]), jnp.zeros((1, 768), F32), ssd_conv_w[layer]]
    prm = jnp.concatenate(rows, axis=0)
    gw = jnp.zeros((128, 128), F32).at[0:16].set(gla_gate_w[layer]).astype(BF16)
    pw = jnp.zeros((256, 256), F32)
    for g in range(4):
        pw = pw.at[64 * g:64 * g + 64, 64 * g:64 * g + 64].set(pool_w[layer, g])
    return prm, gw, pw.astype(BF16)


def _layer_weights(s_in, s_out):
    wp = _permute_cols(jnp.transpose(s_in, (1, 0, 2)).reshape(D, NPROJ))
    wo = s_out.reshape(D, D)
    return wp, wp.T, wo, wo.T


def _grad_slabs(dwp, dwo):
    return jnp.transpose(_unpermute_cols(dwp).reshape(D, 4, NPROJ // 4), (1, 0, 2)), dwo.reshape(4, D // 4, D)


class _Comm:
    def __init__(self, w_in16, w_out16):
        self.w_in16, self.w_out16 = w_in16, w_out16
        self.core = lax.axis_index("c").astype(jnp.int32).reshape(1)

    def gather_ici(self, layer, extra=None):
        return _rider_gather_ici((self.w_in16[layer], self.w_out16[layer]), extra)

    def pair_sum(self, layer, slabs, received):
        return [_pair_sum(self.core, a, b, name=f"reduce_pair_sum{layer}_{k}") for k, (a, b) in enumerate(zip(slabs, received))]

    def chip_sum(self, layer, gathered):
        return [_sum4(a, name=f"reduce_chip_sum{layer}_{k}") for k, a in enumerate(gathered)]


def _local_step(x, tgt, norm_w, final_norm_w, consts, wts0, wts1=None, comm=None):
    nw = [norm_w[l:l + 1] for l in range(2)]
    proj0, slabs = _rmsproj(x, nw[0], wts0[0], name="rmsproj0", rider=comm and comm.gather_ici(1))
    (mix0, sg0, ss0), slabs = _mixer_fwd(proj0, *consts[0], name="mixer_fwd0", rider=comm and _rider_gather_d2d(slabs))
    if comm:
        wts1 = _layer_weights(*slabs)
    x1 = _outproj(x, mix0, wts0[2], name="outproj0")
    proj1, _ = _rmsproj(x1, nw[1], wts1[0], name="rmsproj1")
    (mix1, sg1, ss1), _ = _mixer_fwd(proj1, *consts[1], name="mixer_fwd1")
    x2 = _outproj(x1, mix1, wts1[2], name="outproj1")
    dx, head = _head(x2, tgt, final_norm_w.reshape(1, D), name="loss_head")
    dmix = _matmul_rows(dx, wts1[3], name="dmix1")
    dwo1 = _dwout(mix1, dx, name="dwout1")
    (dproj, mgr1), _ = _mixer_bwd(proj1, dmix, sg1, ss1, *consts[1], name="mixer_bwd1")
    dwp1, _ = _dwin(x1, nw[1], dproj, name="dwin1")
    slabs1 = _grad_slabs(dwp1, dwo1)
    (dx, dnw1), recv = _dxin(dproj, wts1[1], x1, dx, nw[1], name="dxin1", rider=comm and _rider_swap(slabs1))
    dmix = _matmul_rows(dx, wts0[3], name="dmix0")
    dwo0 = _dwout(mix0, dx, name="dwout0")
    scat = comm and _rider_scatter(comm.pair_sum(1, slabs1, recv))
    (dproj, mgr0), gathered = _mixer_bwd(proj0, dmix, sg0, ss0, *consts[0], name="mixer_bwd0", rider=scat)
    share = comm and _rider_share(comm.chip_sum(1, gathered), 1)
    dwp0, big = _dwin(x, nw[0], dproj, name="dwin0", rider=share)
    (dx, dnw0), _ = _dxin(dproj, wts0[1], x, dx, nw[0], name="dxin0")
    if comm:
        slabs0 = _grad_slabs(dwp0, dwo0)
        recv = _run_rider(_rider_swap(slabs0), "reduce_swap0")
        gathered = _run_rider(_rider_scatter(comm.pair_sum(0, slabs0, recv)), "reduce_scatter0")
        big = _run_rider(_rider_share(comm.chip_sum(0, gathered), 0, prev=big), "reduce_share0")
    else:
        big = (jnp.stack([dwp0, dwp1]), jnp.stack([dwo0, dwo1]))
    return head, dx, big, (dnw0, dnw1), (mgr0, mgr1)


def kernel(x, norm_w, w_in, conv_a_w, gla_gate_w, gla_gate_b, gla_norm_w, pool_w, pool_scale, ssd_conv_w, ssd_conv_b, ssd_dt_bias, ssd_a_log, ssd_d, ssd_norm_w, w_out, final_norm_w, loss_target, m_norm_w, m_w_in, m_conv_a_w, m_gla_gate_w, m_gla_gate_b, m_gla_norm_w, m_pool_w, m_pool_scale, m_ssd_conv_w, m_ssd_conv_b, m_ssd_dt_bias, m_ssd_a_log, m_ssd_d, m_ssd_norm_w, m_w_out, m_final_norm_w, v_norm_w, v_w_in, v_conv_a_w, v_gla_gate_w, v_gla_gate_b, v_gla_norm_w, v_pool_w, v_pool_scale, v_ssd_conv_w, v_ssd_conv_b, v_ssd_dt_bias, v_ssd_a_log, v_ssd_d, v_ssd_norm_w, v_w_out, v_final_norm_w):
    weights = dict(norm_w=norm_w, w_in=w_in, conv_a_w=conv_a_w, gla_gate_w=gla_gate_w, gla_gate_b=gla_gate_b,
                   gla_norm_w=gla_norm_w, pool_w=pool_w, pool_scale=pool_scale, ssd_conv_w=ssd_conv_w,
                   ssd_conv_b=ssd_conv_b, ssd_dt_bias=ssd_dt_bias, ssd_a_log=ssd_a_log, ssd_d=ssd_d,
                   ssd_norm_w=ssd_norm_w, w_out=w_out, final_norm_w=final_norm_w)
    m_in = dict(norm_w=m_norm_w, w_in=m_w_in, conv_a_w=m_conv_a_w, gla_gate_w=m_gla_gate_w, gla_gate_b=m_gla_gate_b,
                gla_norm_w=m_gla_norm_w, pool_w=m_pool_w, pool_scale=m_pool_scale, ssd_conv_w=m_ssd_conv_w,
                ssd_conv_b=m_ssd_conv_b, ssd_dt_bias=m_ssd_dt_bias, ssd_a_log=m_ssd_a_log, ssd_d=m_ssd_d,
                ssd_norm_w=m_ssd_norm_w, w_out=m_w_out, final_norm_w=m_final_norm_w)
    v_in = dict(norm_w=v_norm_w, w_in=v_w_in, conv_a_w=v_conv_a_w, gla_gate_w=v_gla_gate_w, gla_gate_b=v_gla_gate_b,
                gla_norm_w=v_gla_norm_w, pool_w=v_pool_w, pool_scale=v_pool_scale, ssd_conv_w=v_ssd_conv_w,
                ssd_conv_b=v_ssd_conv_b, ssd_dt_bias=v_ssd_dt_bias, ssd_a_log=v_ssd_a_log, ssd_d=v_ssd_d,
                ssd_norm_w=v_ssd_norm_w, w_out=v_w_out, final_norm_w=v_final_norm_w)
    order = ("norm_w", "w_in", "conv_a_w", "gla_gate_w", "gla_gate_b", "gla_norm_w", "pool_w", "pool_scale",
             "ssd_conv_w", "ssd_conv_b", "ssd_dt_bias", "ssd_a_log", "ssd_d", "ssd_norm_w", "w_out", "final_norm_w")
    t = x.shape[1]

    comm = _Comm(w_in.astype(BF16), w_out.astype(BF16))
    cshard = jnp.zeros((16, 256), F32)
    for l in range(2):
        cshard = cshard.at[8 * l:8 * l + 3, 0:64].set(conv_a_w[l]).at[8 * l + 3:8 * l + 7, 0:192].set(ssd_conv_w[l])
    s_in, s_out, g_c = _run_rider(comm.gather_ici(0, cshard), "gather_ici0")
    s_in, s_out = _run_rider(_rider_gather_d2d((s_in, s_out)), "gather_d2d0")
    conv_a_full = jnp.stack([jnp.concatenate([g_c[s, 8 * l:8 * l + 3, 0:64] for s in range(4)], axis=-1) for l in range(2)])
    ssd_conv_full = jnp.stack([jnp.concatenate([g_c[s, 8 * l + 3:8 * l + 7, 0:192] for s in range(4)], axis=-1)
                               for l in range(2)])
    consts = [_mixer_consts(l, conv_a_full, gla_gate_w, gla_gate_b, gla_norm_w, pool_w, pool_scale, ssd_conv_full,
                            ssd_conv_b, ssd_dt_bias, ssd_a_log, ssd_d, ssd_norm_w) for l in range(2)]

    head, dx, big, dnw, mgr = _local_step(x.reshape(t, D), loss_target.reshape(t, D), norm_w, final_norm_w, consts,
                                          _layer_weights(s_in, s_out), comm=comm)

    as2d = lambda d: {k: (d[k].reshape(1, D) if k == "final_norm_w" else d[k]) for k in _SMALL_NAMES}
    small = _small_step(mgr[0], mgr[1], dnw[0], dnw[1], head, as2d(weights), as2d(m_in), as2d(v_in))
    grads, delta, new_m, new_v = ({k: (a.reshape(D) if k == "final_norm_w" else a) for k, a in zip(_SMALL_NAMES, part)}
                                  for part in small[0:4])
    loss = small[4].reshape(())

    grads["w_in"], grads["w_out"] = big

    for k in ("w_in", "w_out"):
        delta[k], new_m[k], new_v[k] = _adamw(weights[k], grads[k], m_in[k], v_in[k], name=f"adamw_{k}", br=256)

    return (loss, dx.reshape(1, t, D), *[grads[k] for k in order], *[delta[k] for k in order],
            *[new_m[k] for k in order], *[new_v[k] for k in order])
```

```python
import functools

import jax
import jax.numpy as jnp
from jax import lax
from jax.experimental import pallas as pl
from jax.experimental.pallas import tpu as pltpu

F32 = jnp.float32
BF16 = jnp.bfloat16
MESH = pl.DeviceIdType.MESH

D = 1024
CH = 64
EPS = 1e-6
NP = 3456
NPROJ = 3348
GLA_SCALE = 32.0 ** -0.5
INV_TAU = 1.0 / 16.0
TB = 256
NCH = TB // CH
HALO_W = 1536

C_SX, C_AH, C_AC, C_PU, C_AB, C_AZ = 0, 768, 1024, 1280, 1536, 1792
C_GQ, C_GK, C_GV, C_GZ, C_PZ, C_SZ, C_TL = 2048, 2176, 2304, 2560, 2816, 3072, 3328
_PERM = ((2576, 768), (0, 256), (512, 256), (1808, 256), (256, 256), (768, 256), (1024, 128), (1152, 128),
         (1280, 256), (1552, 256), (2064, 256), (2320, 256), (1536, 16), (3344, 4))
_UNPERM = ((768, 256), (1536, 256), (1024, 256), (1792, 256), (2048, 128), (2176, 128), (2304, 256), (3328, 16),
           (2560, 256), (1280, 256), (2816, 256), (3072, 256), (0, 768), (3344, 4))

R_CAW, R_GB, R_GNW, R_PSC, R_SCB, R_DTB, R_AE, R_DE, R_SNW, R_SCW = 0, 3, 4, 5, 6, 7, 8, 9, 10, 12

ADAM_LR, ADAM_B1, ADAM_B2, ADAM_EPS, ADAM_WD, ADAM_STEP = 0.001, 0.9, 0.999, 1e-08, 0.01, 10

VMEM_LIMIT = 56 * 1024 * 1024


def _cparams(sem, limit=VMEM_LIMIT):
    return pltpu.CompilerParams(dimension_semantics=sem, vmem_limit_bytes=limit)


_ANY = pl.BlockSpec(memory_space=pl.ANY)


def _place():
    return lax.axis_index("x"), lax.axis_index("y"), lax.axis_index("c")


class _Rider:
    def __init__(self, inputs, out_shapes, sems, start, finish, aliases=None):
        self.inputs, self.out_shapes, self.sems = tuple(inputs), tuple(out_shapes), tuple(sems)
        self.start, self.finish, self.aliases = start, finish, dict(aliases or {})


def _call(body, args, *, grid, in_specs, out_specs, out_shape, name, sem, scratch_shapes=(), rider=None):
    if rider is None:
        outs = pl.pallas_call(body, grid=grid, name=name, in_specs=list(in_specs), out_specs=list(out_specs),
                              out_shape=list(out_shape), scratch_shapes=list(scratch_shapes),
                              compiler_params=_cparams(sem))(*args)
        return list(outs), []
    ni, no, ns = len(args), len(out_shape), len(scratch_shapes)
    ri, ro = len(rider.inputs), len(rider.out_shapes)

    def full(*refs):
        ins, rins = refs[:ni], refs[ni:ni + ri]
        outs, routs = refs[ni + ri:ni + ri + no], refs[ni + ri + no:ni + ri + no + ro]
        scr, rsem = refs[ni + ri + no + ro:ni + ri + no + ro + ns], refs[ni + ri + no + ro + ns:]
        first = functools.reduce(jnp.logical_and, [pl.program_id(a) == 0 for a in range(len(grid))])
        last = functools.reduce(jnp.logical_and, [pl.program_id(a) == grid[a] - 1 for a in range(len(grid))])

        @pl.when(first)
        def _():
            rider.start(rins, routs, rsem)

        body(*ins, *outs, *scr)

        @pl.when(last)
        def _():
            rider.finish(rins, routs, rsem)

    outs = pl.pallas_call(
        full, grid=grid, name=name, in_specs=list(in_specs) + [_ANY] * ri, out_specs=list(out_specs) + [_ANY] * ro,
        out_shape=list(out_shape) + list(rider.out_shapes), scratch_shapes=list(scratch_shapes) + list(rider.sems),
        input_output_aliases={ni + k: no + v for k, v in rider.aliases.items()},
        compiler_params=_cparams(("arbitrary",) * len(grid)))(*args, *rider.inputs)
    return list(outs[:no]), list(outs[no:])


def _run_rider(rider, name):
    ri = len(rider.inputs)

    def body(*refs):
        rins, routs, rsem = refs[:ri], refs[ri:ri + len(rider.out_shapes)], refs[ri + len(rider.out_shapes):]
        rider.start(rins, routs, rsem)
        rider.finish(rins, routs, rsem)

    return list(pl.pallas_call(body, name=name, in_specs=[_ANY] * ri, out_specs=[_ANY] * len(rider.out_shapes),
                               out_shape=list(rider.out_shapes), scratch_shapes=list(rider.sems),
                               input_output_aliases=dict(rider.aliases))(*rider.inputs))


def _dot(a, b):
    return jnp.dot(a.astype(BF16), b.astype(BF16), preferred_element_type=F32)


def _dot_nt(a, b):
    return lax.dot_general(a.astype(BF16), b.astype(BF16), (((1,), (1,)), ((), ())), preferred_element_type=F32)


def _dot_tn(a, b):
    return lax.dot_general(a.astype(BF16), b.astype(BF16), (((0,), (0,)), ((), ())), preferred_element_type=F32)


def _split(a):
    hi = a.astype(BF16)
    lo = (a - hi.astype(F32)).astype(BF16)
    return hi, lo


def _dot2_l(a, b):
    hi, lo = _split(a)
    return _dot(hi, b) + _dot(lo, b)


def _dot2_r(a, b):
    hi, lo = _split(b)
    return _dot(a, hi) + _dot(a, lo)


def _dot3_l(a, b):
    hi, lo = _split(a)
    lo2 = ((a - hi.astype(F32)) - lo.astype(F32)).astype(BF16)
    return _dot(hi, b) + _dot(lo, b) + _dot(lo2, b)


def _dot2_nt(a, b):
    hi, lo = _split(a)
    return _dot_nt(hi, b) + _dot_nt(lo, b)


def _silu(z):
    return z * jax.nn.sigmoid(z)


def _dsilu(z):
    s = jax.nn.sigmoid(z)
    return s * (1.0 + z * (1.0 - s))


def _lse1(x):
    return jnp.log(1.0 + jnp.exp(-jnp.abs(x)))


def _cs(a):
    return jnp.sum(a, axis=0, keepdims=True)


def _iota(shape, dim):
    return lax.broadcasted_iota(jnp.int32, shape, dim)


def _expand_mat():
    return jnp.where(_iota((128, 256), 0) - 16 == (_iota((128, 256), 1) >> 6), 1.0, 0.0).astype(BF16)


def _group_mean_mat():
    return jnp.where((_iota((256, 256), 0) >> 6) == (_iota((256, 256), 1) >> 6), 1.0 / 64.0, 0.0).astype(BF16)


def _head_mask_t():
    return jnp.where((_iota((256, 128), 0) >> 6) == (_iota((256, 128), 1) >> 5), 1.0, 0.0).astype(F32)


def _dn(ext, k, n, h):
    return pltpu.roll(ext, k, axis=0)[h:h + n]


def _up(ext, k, n):
    return pltpu.roll(ext, ext.shape[0] - k, axis=0)[:n]


def _pool_lane_select(lane, s2, s4, s8, s16):
    return jnp.where(lane < 64, s2, jnp.where(lane < 128, s4, jnp.where(lane < 192, s8, s16)))


def _winsum_dn(ext, lane):
    s2 = ext + pltpu.roll(ext, 1, axis=0)
    s4 = s2 + pltpu.roll(s2, 2, axis=0)
    s8 = s4 + pltpu.roll(s4, 4, axis=0)
    s16 = s8 + pltpu.roll(s8, 8, axis=0)
    return _pool_lane_select(lane, s2, s4, s8, s16)


def _winsum_up(ext, lane):
    m = ext.shape[0]
    s2 = ext + pltpu.roll(ext, m - 1, axis=0)
    s4 = s2 + pltpu.roll(s2, m - 2, axis=0)
    s8 = s4 + pltpu.roll(s4, m - 4, axis=0)
    s16 = s8 + pltpu.roll(s8, m - 8, axis=0)
    return _pool_lane_select(lane, s2, s4, s8, s16)


def _pool_count(tile, n):
    lane = _iota((1, 256), 1)
    win = _pool_lane_select(lane, 2.0, 4.0, 8.0, 16.0).astype(F32)
    tpos = (tile * n + _iota((n, 1), 0) + 1).astype(F32)
    return jnp.minimum(tpos, win)


def _chunk_tri(n, upper):
    r, c = _iota((n, n), 0), _iota((n, n), 1)
    tri = (c > r) if upper else (c < r)
    return jnp.where(tri & ((r >> 6) == (c >> 6)), 1.0, 0.0).astype(BF16)


def _chunks(a):
    return [a[c * CH:(c + 1) * CH] for c in range(a.shape[0] // CH)]


def _halves(fn, a, b):
    return jnp.concatenate([fn(a[:, 0:128], b[:, 0:128]), fn(a[:, 128:256], b[:, 128:256])], axis=1)


def _mixer_tile_prep(p_ref, xc, prm_ref, gw_v):
    tail = p_ref[:, C_TL:C_TL + 128]
    pre = _dot(tail, gw_v) + prm_ref[R_GB:R_GB + 1, 0:128]
    la = (jnp.minimum(pre, 0.0) - _lse1(pre)) * INV_TAU
    dtin = tail + prm_ref[R_DTB:R_DTB + 1, 0:128]
    dtf = jnp.maximum(dtin, 0.0) + _lse1(dtin)
    dte = _dot2_l(dtf, _expand_mat())
    da = dte * prm_ref[R_AE:R_AE + 1, 0:256]
    rev = _dot2_r(_chunk_tri(TB, True), jnp.concatenate([la, da], axis=1))
    dec = jnp.exp(rev[:, 0:128])
    kd = p_ref[:, C_GK:C_GK + 128] * dec
    wdec = jnp.exp(rev[:, 128:384])
    w = wdec * dte
    xw = xc[:, 0:256] * w
    d_s = [jnp.exp(_cs(a)) for a in _chunks(la)]
    et = [jnp.exp(_cs(a)) for a in _chunks(da)]
    mask_t = _head_mask_t()
    ut_g = [_dot_tn(v, k) * mask_t for v, k in zip(_chunks(p_ref[:, C_GV:C_GV + 256]), _chunks(kd))]
    ut_s = [_halves(_dot_tn, b, x) for b, x in zip(_chunks(xc[:, 256:512]), _chunks(xw))]
    return tail, pre, dtin, dte, dec, kd, wdec, w, xw, d_s, et, ut_g, ut_s


def _rmsproj(x, nw, wp, name, tm=512, rider=None):
    t = x.shape[0]

    def body(x_ref, nw_ref, w_ref, o_ref, h_ref):
        xv = x_ref[...]
        rs = lax.rsqrt(jnp.mean(xv * xv, axis=-1, keepdims=True) + EPS)
        h = (xv * rs * nw_ref[...]).astype(BF16)
        h_ref[...] = h
        o_ref[...] = jnp.dot(h, w_ref[...], preferred_element_type=F32)

    (proj, h), extra = _call(
        body, (x, nw, wp), grid=(t // tm,), name=name, sem=("parallel",), rider=rider,
        in_specs=[pl.BlockSpec((tm, D), lambda i: (i, 0)), pl.BlockSpec((1, D), lambda i: (0, 0)),
                  pl.BlockSpec((D, NP), lambda i: (0, 0))],
        out_specs=[pl.BlockSpec((tm, NP), lambda i: (i, 0)), pl.BlockSpec((tm, D), lambda i: (i, 0))],
        out_shape=[jax.ShapeDtypeStruct((t, NP), F32), jax.ShapeDtypeStruct((t, D), BF16)])
    return proj, h, extra


def _outproj(x, mix, wo, name, tm=512):
    t = x.shape[0]

    def body(x_ref, m_ref, w_ref, o_ref):
        o_ref[...] = x_ref[...] + jnp.dot(m_ref[...].astype(BF16), w_ref[...], preferred_element_type=F32)

    return pl.pallas_call(
        body, grid=(t // tm,), name=name,
        in_specs=[pl.BlockSpec((tm, D), lambda i: (i, 0)), pl.BlockSpec((tm, D), lambda i: (i, 0)),
                  pl.BlockSpec((D, D), lambda i: (0, 0))],
        out_specs=pl.BlockSpec((tm, D), lambda i: (i, 0)),
        out_shape=jax.ShapeDtypeStruct((t, D), F32),
        compiler_params=_cparams(("parallel",)),
    )(x, mix, wo)


def _matmul_rows(a, w, name, tm=512):
    t, kdim = a.shape
    n = w.shape[1]

    def body(a_ref, w_ref, o_ref):
        o_ref[...] = jnp.dot(a_ref[...].astype(BF16), w_ref[...], preferred_element_type=F32)

    return pl.pallas_call(
        body, grid=(t // tm,), name=name,
        in_specs=[pl.BlockSpec((tm, kdim), lambda i: (i, 0)), pl.BlockSpec((kdim, n), lambda i: (0, 0))],
        out_specs=pl.BlockSpec((tm, n), lambda i: (i, 0)),
        out_shape=jax.ShapeDtypeStruct((t, n), F32),
        compiler_params=_cparams(("parallel",)),
    )(a, w)


def _head(x, tgt, fw, name, tm=512):
    t = x.shape[0]

    def body(x_ref, t_ref, w_ref, dx_ref, acc_ref):
        @pl.when(pl.program_id(0) == 0)
        def _():
            acc_ref[...] = jnp.zeros_like(acc_ref)

        xv = x_ref[...]
        w = w_ref[...]
        rs = lax.rsqrt(jnp.mean(xv * xv, axis=-1, keepdims=True) + EPS)
        xh = xv * rs
        err = xh * w - t_ref[...]
        dy = err * (1.0 / D)
        dxh = dy * w
        dx_ref[...] = rs * (dxh - xh * jnp.mean(dxh * xh, axis=-1, keepdims=True))
        acc_ref[0:1, :] += _cs(dy * xh)
        acc_ref[1:2, :] += jnp.zeros((1, D), F32) + (0.5 / D) * jnp.sum(err * err)

    return pl.pallas_call(
        body, grid=(t // tm,), name=name,
        in_specs=[pl.BlockSpec((tm, D), lambda i: (i, 0)), pl.BlockSpec((tm, D), lambda i: (i, 0)),
                  pl.BlockSpec((1, D), lambda i: (0, 0))],
        out_specs=[pl.BlockSpec((tm, D), lambda i: (i, 0)), pl.BlockSpec((8, D), lambda i: (0, 0))],
        out_shape=[jax.ShapeDtypeStruct((t, D), F32), jax.ShapeDtypeStruct((8, D), F32)],
        compiler_params=_cparams(("arbitrary",)),
    )(x, tgt, fw)


def _dxin(dp, wpt, x, dxn, nw, name, tm=512, rider=None):
    t = x.shape[0]

    def body(dp_ref, w_ref, x_ref, dxn_ref, nw_ref, dx_ref, dnw_ref):
        @pl.when(pl.program_id(0) == 0)
        def _():
            dnw_ref[...] = jnp.zeros_like(dnw_ref)

        dh = jnp.dot(dp_ref[...].astype(BF16), w_ref[...], preferred_element_type=F32)
        xv = x_ref[...]
        rs = lax.rsqrt(jnp.mean(xv * xv, axis=-1, keepdims=True) + EPS)
        xh = xv * rs
        dnw_ref[0:1, :] += _cs(dh * xh)
        dxh = dh * nw_ref[...]
        dx_ref[...] = dxn_ref[...] + rs * (dxh - xh * jnp.mean(dxh * xh, axis=-1, keepdims=True))

    return _call(
        body, (dp, wpt, x, dxn, nw), grid=(t // tm,), name=name, sem=("arbitrary",), rider=rider,
        in_specs=[pl.BlockSpec((tm, NP), lambda i: (i, 0)), pl.BlockSpec((NP, D), lambda i: (0, 0)),
                  pl.BlockSpec((tm, D), lambda i: (i, 0)), pl.BlockSpec((tm, D), lambda i: (i, 0)),
                  pl.BlockSpec((1, D), lambda i: (0, 0))],
        out_specs=[pl.BlockSpec((tm, D), lambda i: (i, 0)), pl.BlockSpec((8, D), lambda i: (0, 0))],
        out_shape=[jax.ShapeDtypeStruct((t, D), F32), jax.ShapeDtypeStruct((8, D), F32)])


def _dwin(h, dp, name, tm=512, tn=NP, rider=None):
    t = h.shape[0]

    def body(h_ref, dp_ref, o_ref):
        @pl.when(pl.program_id(1) == 0)
        def _():
            o_ref[...] = jnp.zeros_like(o_ref)

        o_ref[...] += _dot_tn(h_ref[...], dp_ref[...])

    (dwp,), extra = _call(
        body, (h, dp), grid=(NP // tn, t // tm), name=name, sem=("parallel", "arbitrary"), rider=rider,
        in_specs=[pl.BlockSpec((tm, D), lambda j, i: (i, 0)), pl.BlockSpec((tm, tn), lambda j, i: (i, j))],
        out_specs=[pl.BlockSpec((D, tn), lambda j, i: (0, j))], out_shape=[jax.ShapeDtypeStruct((D, NP), F32)])
    return dwp, extra


def _dwout(mix, dxn, name, tm=512):
    t = mix.shape[0]

    def body(m_ref, g_ref, o_ref):
        @pl.when(pl.program_id(0) == 0)
        def _():
            o_ref[...] = jnp.zeros_like(o_ref)

        o_ref[...] += _dot_tn(m_ref[...], g_ref[...])

    return pl.pallas_call(
        body, grid=(t // tm,), name=name,
        in_specs=[pl.BlockSpec((tm, D), lambda i: (i, 0)), pl.BlockSpec((tm, D), lambda i: (i, 0))],
        out_specs=pl.BlockSpec((D, D), lambda i: (0, 0)),
        out_shape=jax.ShapeDtypeStruct((D, D), F32),
        compiler_params=_cparams(("arbitrary",)),
    )(mix, dxn)


def _mixer_fwd(proj, prm, gw, pw, name, rider=None):
    t = proj.shape[0]
    nt, nc = t // TB, t // CH

    def body(p_ref, prm_ref, gw_ref, pw_ref, mix_ref, sg_ref, ss_ref, sg_s, ss_s, h_ua, h_pu, h_sx):
        i = pl.program_id(0)

        @pl.when(i == 0)
        def _():
            for r in (sg_s, ss_s, h_ua, h_pu, h_sx):
                r[...] = jnp.zeros_like(r)

        lane = _iota((1, 256), 1)
        u = p_ref[:, C_AC:C_AC + 256] * p_ref[:, C_AH:C_AH + 256]
        ext = jnp.concatenate([h_ua[...], u], axis=0)
        cv = (prm_ref[R_CAW + 2:R_CAW + 3, 0:256] * u + prm_ref[R_CAW + 1:R_CAW + 2, 0:256] * _dn(ext, 1, TB, 8)
              + prm_ref[R_CAW:R_CAW + 1, 0:256] * _dn(ext, 2, TB, 8))
        mix_ref[:, 0:256] = (p_ref[:, C_AB:C_AB + 256] * cv * _silu(p_ref[:, C_AZ:C_AZ + 256])).astype(BF16)
        h_ua[...] = u[TB - 8:, :]
        pu = p_ref[:, C_PU:C_PU + 256]
        ext = jnp.concatenate([h_pu[...], pu], axis=0)
        pooled = _winsum_dn(ext, lane)[16:] / _pool_count(i, TB) - pu
        mixed = _dot(pooled, pw_ref[...])
        mix_ref[:, 512:768] = (prm_ref[R_PSC:R_PSC + 1, 0:256] * mixed * _silu(p_ref[:, C_PZ:C_PZ + 256])).astype(BF16)
        h_pu[...] = pu[TB - 16:, :]
        sx = p_ref[:, C_SX:C_SX + 768]
        ext = jnp.concatenate([h_sx[...], sx], axis=0)
        xc = _silu(prm_ref[R_SCW + 3:R_SCW + 4, :] * sx + prm_ref[R_SCW + 2:R_SCW + 3, :] * _dn(ext, 1, TB, 8)
                   + prm_ref[R_SCW + 1:R_SCW + 2, :] * _dn(ext, 2, TB, 8) + prm_ref[R_SCW:R_SCW + 1, :] * _dn(ext, 3, TB, 8)
                   + prm_ref[R_SCB:R_SCB + 1, :])
        h_sx[...] = sx[TB - 8:, :]

        _, _, _, _, _, _, _, _, _, d_s, et, ut_g, ut_s = _mixer_tile_prep(p_ref, xc, prm_ref, gw_ref[...])
        s_g, s_s = sg_s[...], ss_s[...]
        o, y = [], []
        qs = _chunks(p_ref[:, C_GQ:C_GQ + 128] * GLA_SCALE)
        cm = _chunks(xc[:, 512:768])
        for c in range(NCH):
            sg_ref[c] = s_g
            ss_ref[c] = s_s
            s_g = s_g * d_s[c] + ut_g[c]
            s_s = s_s * et[c] + ut_s[c]
            o.append(_dot_nt(qs[c], s_g))
            y.append(_halves(_dot, cm[c], s_s))
        sg_s[...] = s_g
        ss_s[...] = s_s
        o = jnp.concatenate(o, axis=0)
        on = o * lax.rsqrt(_dot2_l(o * o, _group_mean_mat()) + EPS)
        mix_ref[:, 256:512] = (on * prm_ref[R_GNW:R_GNW + 1, 0:256] * _silu(p_ref[:, C_GZ:C_GZ + 256])).astype(BF16)
        y2 = ((jnp.concatenate(y, axis=0) + prm_ref[R_DE:R_DE + 1, 0:256] * xc[:, 0:256])
              * _silu(p_ref[:, C_SZ:C_SZ + 256]))
        mix_ref[:, 768:1024] = (y2 * lax.rsqrt(jnp.mean(y2 * y2, axis=-1, keepdims=True) + EPS)
                                * prm_ref[R_SNW:R_SNW + 1, 0:256]).astype(BF16)

    return _call(
        body, (proj, prm, gw, pw), grid=(nt,), name=name, sem=("arbitrary",), rider=rider,
        in_specs=[pl.BlockSpec((TB, NP), lambda i: (i, 0)), pl.BlockSpec((16, 768), lambda i: (0, 0)),
                  pl.BlockSpec((128, 128), lambda i: (0, 0)), pl.BlockSpec((256, 256), lambda i: (0, 0))],
        out_specs=[pl.BlockSpec((TB, D), lambda i: (i, 0)), pl.BlockSpec((NCH, 256, 128), lambda i: (i, 0, 0)),
                   pl.BlockSpec((NCH, 128, 256), lambda i: (i, 0, 0))],
        out_shape=[jax.ShapeDtypeStruct((t, D), BF16), jax.ShapeDtypeStruct((nc, 256, 128), F32),
                   jax.ShapeDtypeStruct((nc, 128, 256), F32)],
        scratch_shapes=[pltpu.VMEM((256, 128), F32), pltpu.VMEM((128, 256), F32), pltpu.VMEM((8, 256), F32),
                        pltpu.VMEM((16, 256), F32), pltpu.VMEM((8, 768), F32)])


def _mixer_bwd(proj, dmix, sg, ss, prm, gw, pw, name, rider=None):
    t = proj.shape[0]
    nt = t // TB
    rev = lambda i: nt - 1 - i

    def body(p_ref, hp_ref, dm_ref, sg_ref, ss_ref, prm_ref, gw_ref, pw_ref, dp_ref, sgc_ref,
             gg_s, gs_s, h_dcv, h_dpl, h_dpre, gsm_ref, dgw_ref, dpw_ref):
        i = pl.program_id(0)
        tile = nt - 1 - i

        @pl.when(i == 0)
        def _():
            for r in (gg_s, gs_s, h_dcv, h_dpl, h_dpre, gsm_ref, dgw_ref, dpw_ref):
                r[...] = jnp.zeros_like(r)

        lane = _iota((1, 256), 1)
        first = (tile > 0).astype(F32)
        ah, ac = p_ref[:, C_AH:C_AH + 256], p_ref[:, C_AC:C_AC + 256]
        ab, az = p_ref[:, C_AB:C_AB + 256], p_ref[:, C_AZ:C_AZ + 256]
        w0, w1, w2 = (prm_ref[R_CAW + j:R_CAW + j + 1, 0:256] for j in range(3))
        u = ac * ah
        ext = jnp.concatenate([hp_ref[8:16, C_AC:C_AC + 256] * hp_ref[8:16, C_AH:C_AH + 256] * first, u], axis=0)
        u1, u2 = _dn(ext, 1, TB, 8), _dn(ext, 2, TB, 8)
        cv = w2 * u + w1 * u1 + w0 * u2
        g = dm_ref[:, 0:256]
        sz = _silu(az)
        dp_ref[:, C_AB:C_AB + 256] = (g * cv * sz).astype(BF16)
        dp_ref[:, C_AZ:C_AZ + 256] = (g * ab * cv * _dsilu(az)).astype(BF16)
        dcv = g * ab * sz
        dext = jnp.concatenate([dcv, h_dcv[...]], axis=0)
        du = w2 * dcv + w1 * _up(dext, 1, TB) + w0 * _up(dext, 2, TB)
        dp_ref[:, C_AC:C_AC + 256] = (du * ah).astype(BF16)
        dp_ref[:, C_AH:C_AH + 256] = (du * ac).astype(BF16)
        gsm_ref[R_CAW:R_CAW + 1, 0:256] += _cs(dcv * u2)
        gsm_ref[R_CAW + 1:R_CAW + 2, 0:256] += _cs(dcv * u1)
        gsm_ref[R_CAW + 2:R_CAW + 3, 0:256] += _cs(dcv * u)
        h_dcv[...] = dcv[0:8, :]
        pu, pz = p_ref[:, C_PU:C_PU + 256], p_ref[:, C_PZ:C_PZ + 256]
        psc = prm_ref[R_PSC:R_PSC + 1, 0:256]
        cnt = _pool_count(tile, TB)
        ext = jnp.concatenate([hp_ref[:, C_PU:C_PU + 256] * first, pu], axis=0)
        pooled = _winsum_dn(ext, lane)[16:] / cnt - pu
        pw_v = pw_ref[...]
        mixed = _dot(pooled, pw_v)
        g = dm_ref[:, 512:768]
        sz = _silu(pz)
        gsm_ref[R_PSC:R_PSC + 1, 0:256] += _cs(g * mixed * sz)
        dp_ref[:, C_PZ:C_PZ + 256] = (g * psc * mixed * _dsilu(pz)).astype(BF16)
        dmixed = g * psc * sz
        dpw_ref[...] += _dot_tn(pooled, dmixed)
        dpooled = _dot_nt(dmixed, pw_v)
        qd = dpooled / cnt
        dext = jnp.concatenate([qd, h_dpl[...]], axis=0)
        dp_ref[:, C_PU:C_PU + 256] = (_winsum_up(dext, lane)[:TB] - dpooled).astype(BF16)
        h_dpl[...] = qd[0:16, :]
        sx = p_ref[:, C_SX:C_SX + 768]
        cw = [prm_ref[R_SCW + j:R_SCW + j + 1, :] for j in range(4)]
        ext = jnp.concatenate([hp_ref[8:16, C_SX:C_SX + 768] * first, sx], axis=0)
        sx1, sx2, sx3 = _dn(ext, 1, TB, 8), _dn(ext, 2, TB, 8), _dn(ext, 3, TB, 8)
        cpre = cw[3] * sx + cw[2] * sx1 + cw[1] * sx2 + cw[0] * sx3 + prm_ref[R_SCB:R_SCB + 1, :]
        xc = _silu(cpre)
        xs, bm, cm = xc[:, 0:256], xc[:, 256:512], xc[:, 512:768]

        gw_v = gw_ref[...]
        tail, pre, dtin, dte, dec, kd, wdec, w, xw, d_s, et, ut_g, ut_s = _mixer_tile_prep(p_ref, xc, prm_ref, gw_v)
        gmean = _group_mean_mat()
        mask_t = _head_mask_t()
        gnw = prm_ref[R_GNW:R_GNW + 1, 0:256]
        a_e = prm_ref[R_AE:R_AE + 1, 0:256]
        d_e = prm_ref[R_DE:R_DE + 1, 0:256]
        snw = prm_ref[R_SNW:R_SNW + 1, 0:256]
        sg_in = [sg_ref[c] for c in range(NCH)]
        ss_in = [ss_ref[c] for c in range(NCH)]
        sg_n = [sg_in[c] * d_s[c] + ut_g[c] for c in range(NCH)]
        ss_n = [ss_in[c] * et[c] + ut_s[c] for c in range(NCH)]
        qs = _chunks(p_ref[:, C_GQ:C_GQ + 128] * GLA_SCALE)
        cm_c, bm_c, xw_c, kd_c = _chunks(cm), _chunks(bm), _chunks(xw), _chunks(kd)
        v_c = _chunks(p_ref[:, C_GV:C_GV + 256])
        o = jnp.concatenate([_dot_nt(qs[c], sg_n[c]) for c in range(NCH)], axis=0)
        y = jnp.concatenate([_halves(_dot, cm_c[c], ss_n[c]) for c in range(NCH)], axis=0) + d_e * xs
        gz = p_ref[:, C_GZ:C_GZ + 256]
        r = lax.rsqrt(_dot2_l(o * o, gmean) + EPS)
        on = o * r
        dyb = dm_ref[:, 256:512]
        dp_ref[:, C_GZ:C_GZ + 256] = (dyb * on * gnw * _dsilu(gz)).astype(BF16)
        tg = dyb * _silu(gz)
        gsm_ref[R_GNW:R_GNW + 1, 0:256] += _cs(tg * on)
        don = tg * gnw
        do_c = _chunks(r * (don - on * _dot2_l(don * on, gmean)))
        ssz = p_ref[:, C_SZ:C_SZ + 256]
        sil = _silu(ssz)
        y2 = y * sil
        r = lax.rsqrt(jnp.mean(y2 * y2, axis=-1, keepdims=True) + EPS)
        yn = y2 * r
        dyd = dm_ref[:, 768:1024]
        gsm_ref[R_SNW:R_SNW + 1, 0:256] += _cs(dyd * yn)
        dn = dyd * snw
        dy2 = r * (dn - yn * jnp.mean(dn * yn, axis=-1, keepdims=True))
        dp_ref[:, C_SZ:C_SZ + 256] = (dy2 * y * _dsilu(ssz)).astype(BF16)
        dy = dy2 * sil
        gsm_ref[R_DE:R_DE + 1, 0:256] += _cs(dy * xs)
        dy_c = _chunks(dy)
        dq = jnp.concatenate([_dot(do_c[c], sg_n[c]) for c in range(NCH)], axis=0)
        dp_ref[:, C_GQ:C_GQ + 128] = (dq * GLA_SCALE).astype(BF16)
        dcm = jnp.concatenate([_halves(_dot_nt, dy_c[c], ss_n[c]) for c in range(NCH)], axis=0)
        gg = [_dot_tn(do_c[c], qs[c]) * mask_t for c in range(NCH)]
        gs = [_halves(_dot_tn, cm_c[c], dy_c[c]) for c in range(NCH)]
        car_g, car_s = gg_s[...], gs_s[...]
        for c in reversed(range(NCH)):
            gg[c] = gg[c] + car_g
            gs[c] = gs[c] + car_s
            car_g = gg[c] * d_s[c]
            car_s = gs[c] * et[c]
        gg_s[...] = car_g
        gs_s[...] = car_s
        dkd = jnp.concatenate([_dot(v_c[c], gg[c]) for c in range(NCH)], axis=0)
        dp_ref[:, C_GV:C_GV + 256] = jnp.concatenate([_dot_nt(kd_c[c], gg[c]) for c in range(NCH)], axis=0).astype(BF16)
        dp_ref[:, C_GK:C_GK + 128] = (dkd * dec).astype(BF16)
        dbm = jnp.concatenate([_halves(_dot_nt, xw_c[c], gs[c]) for c in range(NCH)], axis=0)
        dxw = jnp.concatenate([_halves(_dot, bm_c[c], gs[c]) for c in range(NCH)], axis=0)
        dxs = dy * d_e + dxw * w
        dw = dxw * xs
        dsuf = _dot2_r(_chunk_tri(TB, False), jnp.concatenate([dkd * kd, dw * dte * wdec], axis=1))
        tot_g = jnp.concatenate([jnp.broadcast_to(_cs(gg[c] * sg_in[c]) * d_s[c], (CH, 128)) for c in range(NCH)], axis=0)
        tot_s = jnp.concatenate([jnp.broadcast_to(_cs(gs[c] * ss_in[c]) * et[c], (CH, 256)) for c in range(NCH)], axis=0)
        dpre = (dsuf[:, 0:128] + tot_g) * INV_TAU * jax.nn.sigmoid(-pre)
        dgw_ref[...] += _dot_tn(tail, dpre)
        gsm_ref[R_GB:R_GB + 1, 0:128] += _cs(dpre)
        dda = dsuf[:, 128:384] + tot_s
        gsm_ref[R_AE:R_AE + 1, 0:256] += _cs(dda * dte)
        dtail_s = _dot2_nt(dw * wdec + dda * a_e, _expand_mat()) * jax.nn.sigmoid(dtin)
        gsm_ref[R_DTB:R_DTB + 1, 0:128] += _cs(dtail_s)
        dp_ref[:, C_TL:C_TL + 128] = (_dot_nt(dpre, gw_v) + dtail_s).astype(BF16)
        dpre_c = jnp.concatenate([dxs, dbm, dcm], axis=1) * _dsilu(cpre)
        dext = jnp.concatenate([dpre_c, h_dpre[...]], axis=0)
        dp_ref[:, C_SX:C_SX + 768] = (cw[3] * dpre_c + cw[2] * _up(dext, 1, TB) + cw[1] * _up(dext, 2, TB)
                                      + cw[0] * _up(dext, 3, TB)).astype(BF16)
        gsm_ref[R_SCW + 3:R_SCW + 4, :] += _cs(dpre_c * sx)
        gsm_ref[R_SCW + 2:R_SCW + 3, :] += _cs(dpre_c * sx1)
        gsm_ref[R_SCW + 1:R_SCW + 2, :] += _cs(dpre_c * sx2)
        gsm_ref[R_SCW:R_SCW + 1, :] += _cs(dpre_c * sx3)
        gsm_ref[R_SCB:R_SCB + 1, :] += _cs(dpre_c)
        h_dpre[...] = dpre_c[0:8, :]

        @pl.when(i == nt - 1)
        def _():
            ri, ci = _iota((256, 256), 0), _iota((256, 256), 1)
            per_head = jnp.where((ri >> 6) == ci, 1.0, 0.0).astype(BF16)
            per_dv = jnp.where((ri & 63) == ci, 1.0, 0.0).astype(BF16)
            row = _iota((8, 256), 0)
            top = gsm_ref[0:8, 0:256]
            sgc_ref[0:8, 0:256] = jnp.where(row == R_GNW, _dot3_l(top, per_dv), top)
            bot = gsm_ref[8:16, 0:256]
            fold = _dot3_l(jnp.where(row == R_AE - 8, bot * a_e, bot), per_head)
            sgc_ref[8:16, 0:256] = jnp.where((row == R_AE - 8) | (row == R_DE - 8), fold, bot)
            sgc_ref[0:16, 256:768] = gsm_ref[:, 256:768]
            sgc_ref[0:16, 768:896] = dgw_ref[0:16, :]
            sgc_ref[0:16, 896:1024] = jnp.zeros((16, 128), F32)
            diag = _pool_lane_select(lane, dpw_ref[0:64, :], dpw_ref[64:128, :], dpw_ref[128:192, :], dpw_ref[192:256, :])
            for q in range(4):
                sgc_ref[16:32, 256 * q:256 * q + 256] = diag[16 * q:16 * q + 16, :]

    return _call(
        body, (proj, proj, dmix, sg, ss, prm, gw, pw), grid=(nt,), name=name, sem=("arbitrary",), rider=rider,
        in_specs=[pl.BlockSpec((TB, NP), lambda i: (rev(i), 0)),
                  pl.BlockSpec((16, HALO_W), lambda i: (jnp.maximum(rev(i) * (TB // 16) - 1, 0), 0)),
                  pl.BlockSpec((TB, D), lambda i: (rev(i), 0)),
                  pl.BlockSpec((NCH, 256, 128), lambda i: (rev(i), 0, 0)),
                  pl.BlockSpec((NCH, 128, 256), lambda i: (rev(i), 0, 0)),
                  pl.BlockSpec((16, 768), lambda i: (0, 0)), pl.BlockSpec((128, 128), lambda i: (0, 0)),
                  pl.BlockSpec((256, 256), lambda i: (0, 0))],
        out_specs=[pl.BlockSpec((TB, NP), lambda i: (rev(i), 0)), pl.BlockSpec((32, 1024), lambda i: (0, 0))],
        out_shape=[jax.ShapeDtypeStruct((t, NP), BF16), jax.ShapeDtypeStruct((32, 1024), F32)],
        scratch_shapes=[pltpu.VMEM((256, 128), F32), pltpu.VMEM((128, 256), F32), pltpu.VMEM((8, 256), F32),
                        pltpu.VMEM((16, 256), F32), pltpu.VMEM((8, 768), F32), pltpu.VMEM((16, 768), F32),
                        pltpu.VMEM((128, 128), F32), pltpu.VMEM((256, 256), F32)])


def _half(c, n):
    return pl.ds(pl.multiple_of(c * (n // 2), n // 2), n // 2)


def _other_chips(x, y):
    return ((1 - x, y), (x, 1 - y), (1 - x, 1 - y))


def _remote(src, dst, send, recv, k, dev):
    return pltpu.make_async_remote_copy(src_ref=src, dst_ref=dst, send_sem=send.at[k], recv_sem=recv.at[k], device_id=dev,
                                        device_id_type=MESH)


def _sem(n):
    return pltpu.SemaphoreType.DMA((n,))


def _rider_gather_ici(shards, extra=None):
    shards = tuple(shards) + ((extra,) if extra is not None else ())
    n = len(shards)

    def copies(rins, routs, sems, arrivals=True):
        send, recv, loc = sems
        x, y, c = _place()
        me = 2 * x + y
        own = [pltpu.make_async_copy(rins[k], routs[k].at[me], loc.at[k]) for k in range(n)]
        out, inc = [], []
        for j, (px, py) in enumerate(_other_chips(x, y)):
            for k in range(n):
                whole = extra is not None and k == n - 1
                rows = pl.ds(0, shards[k].shape[0]) if whole else _half(c, shards[k].shape[0])
                out.append(_remote(rins[k].at[rows], routs[k].at[me, rows], send, recv, n * j + k, (px, py, c)))
                if arrivals:
                    inc.append(_remote(rins[k].at[rows], routs[k].at[2 * px + py, rows], send, recv, n * j + k, (px, py, c)))
        return own, out, inc

    def start(rins, routs, sems):
        own, out, _ = copies(rins, routs, sems, arrivals=False)
        for cp in own + out:
            cp.start()

    def finish(rins, routs, sems):
        own, out, inc = copies(rins, routs, sems)
        for cp in inc:
            cp.wait_recv()
        for cp in out:
            cp.wait_send()
        for cp in own:
            cp.wait()

    return _Rider(shards, [jax.ShapeDtypeStruct((4,) + a.shape, a.dtype) for a in shards],
                  [_sem(3 * n), _sem(3 * n), _sem(n)], start, finish)


def _rider_gather_d2d(slabs):
    slabs = tuple(slabs)
    n = len(slabs)

    def copies(routs, sems, arrivals=True):
        send, recv = sems
        x, y, c = _place()
        out, inc = [], []
        for j, (px, py) in enumerate(_other_chips(x, y)):
            for k in range(n):
                rows = slabs[k].shape[1]
                mine, theirs = routs[k].at[2 * px + py, _half(c, rows)], routs[k].at[2 * px + py, _half(1 - c, rows)]
                out.append(_remote(mine, mine, send, recv, n * j + k, (x, y, 1 - c)))
                if arrivals:
                    inc.append(_remote(theirs, theirs, send, recv, n * j + k, (x, y, 1 - c)))
        return out, inc

    def start(rins, routs, sems):
        for cp in copies(routs, sems, arrivals=False)[0]:
            cp.start()

    def finish(rins, routs, sems):
        out, inc = copies(routs, sems)
        for cp in inc:
            cp.wait_recv()
        for cp in out:
            cp.wait_send()

    return _Rider(slabs, [jax.ShapeDtypeStruct(a.shape, a.dtype) for a in slabs], [_sem(3 * n), _sem(3 * n)], start, finish,
                  aliases={k: k for k in range(n)})


def _rider_swap(parts):
    parts = tuple(parts)
    n = len(parts)

    def copies(rins, routs, sems):
        send, recv = sems
        x, y, c = _place()
        return [_remote(rins[k].at[:, _half(1 - c, parts[k].shape[1])], routs[k], send, recv, k, (x, y, 1 - c))
                for k in range(n)]

    def start(rins, routs, sems):
        for cp in copies(rins, routs, sems):
            cp.start()

    def finish(rins, routs, sems):
        for cp in copies(rins, routs, sems):
            cp.wait()

    return _Rider(parts, [jax.ShapeDtypeStruct((4, a.shape[1] // 2, a.shape[2]), a.dtype) for a in parts],
                  [_sem(n), _sem(n)], start, finish)


def _rider_scatter(parts):
    parts = tuple(parts)
    n = len(parts)

    def copies(rins, routs, sems, arrivals=True):
        send, recv, loc = sems
        x, y, c = _place()
        me = 2 * x + y
        own = [pltpu.make_async_copy(rins[k].at[me], routs[k].at[me], loc.at[k]) for k in range(n)]
        out, inc = [], []
        for j, (px, py) in enumerate(_other_chips(x, y)):
            for k in range(n):
                out.append(_remote(rins[k].at[2 * px + py], routs[k].at[me], send, recv, n * j + k, (px, py, c)))
                if arrivals:
                    inc.append(_remote(rins[k].at[me], routs[k].at[2 * px + py], send, recv, n * j + k, (px, py, c)))
        return own, out, inc

    def start(rins, routs, sems):
        own, out, _ = copies(rins, routs, sems, arrivals=False)
        for cp in own + out:
            cp.start()

    def finish(rins, routs, sems):
        own, out, inc = copies(rins, routs, sems)
        for cp in inc:
            cp.wait_recv()
        for cp in out:
            cp.wait_send()
        for cp in own:
            cp.wait()

    return _Rider(parts, [jax.ShapeDtypeStruct(a.shape, a.dtype) for a in parts], [_sem(3 * n), _sem(3 * n), _sem(n)],
                  start, finish)


def _rider_share(halves, layer, prev=None):
    halves = tuple(halves)
    n = len(halves)
    inputs = halves + (tuple(prev) if prev is not None else ())

    def copies(rins, routs, sems, arrivals=True):
        send, recv, loc = sems
        x, y, c = _place()
        own, out, inc = [], [], []
        for k in range(n):
            rows = 2 * halves[k].shape[0]
            own.append(pltpu.make_async_copy(rins[k], routs[k].at[layer, _half(c, rows)], loc.at[k]))
            out.append(_remote(rins[k], routs[k].at[layer, _half(c, rows)], send, recv, k, (x, y, 1 - c)))
            if arrivals:
                inc.append(_remote(rins[k], routs[k].at[layer, _half(1 - c, rows)], send, recv, k, (x, y, 1 - c)))
        return own, out, inc

    def start(rins, routs, sems):
        own, out, _ = copies(rins, routs, sems, arrivals=False)
        for cp in own + out:
            cp.start()

    def finish(rins, routs, sems):
        own, out, inc = copies(rins, routs, sems)
        for cp in inc:
            cp.wait_recv()
        for cp in out:
            cp.wait_send()
        for cp in own:
            cp.wait()

    return _Rider(inputs, [jax.ShapeDtypeStruct((2, 2 * a.shape[0], a.shape[1]), a.dtype) for a in halves],
                  [_sem(n), _sem(n), _sem(n)], start, finish,
                  aliases={n + k: k for k in range(n)} if prev is not None else None)


def _pair_sum(core, full, recv, name, br=128):
    n, rows, cols = recv.shape

    def body(c_ref, a_ref, b_ref, o_ref):
        o_ref[...] = (a_ref[...] + b_ref[...]).astype(BF16)

    nb = rows // br
    return pl.pallas_call(
        body, name=name, out_shape=jax.ShapeDtypeStruct(recv.shape, BF16),
        grid_spec=pltpu.PrefetchScalarGridSpec(
            num_scalar_prefetch=1, grid=(n, nb),
            in_specs=[pl.BlockSpec((1, br, cols), lambda i, j, c: (i, c[0] * nb + j, 0)),
                      pl.BlockSpec((1, br, cols), lambda i, j, c: (i, j, 0))],
            out_specs=pl.BlockSpec((1, br, cols), lambda i, j, c: (i, j, 0))),
        compiler_params=_cparams(("parallel", "parallel")))(core, full, recv)


def _sum4(a, name, br=128):
    _, r, c = a.shape

    def body(a_ref, o_ref):
        o_ref[...] = ((a_ref[0].astype(F32) + a_ref[1].astype(F32)) + a_ref[2].astype(F32)) + a_ref[3].astype(F32)

    return pl.pallas_call(body, grid=(r // br,), name=name,
                          in_specs=[pl.BlockSpec((4, br, c), lambda i: (0, i, 0))],
                          out_specs=pl.BlockSpec((br, c), lambda i: (i, 0)),
                          out_shape=jax.ShapeDtypeStruct((r, c), F32),
                          compiler_params=_cparams(("parallel",)))(a)


def _adamw(w, g, m, v, name, br):
    n, r, c = w.shape

    def body(w_ref, g_ref, m_ref, v_ref, d_ref, m2_ref, v2_ref):
        d_ref[...], m2_ref[...], v2_ref[...] = _adam_math(w_ref[...], g_ref[...], m_ref[...], v_ref[...])

    spec = pl.BlockSpec((1, br, c), lambda i, j: (i, j, 0))
    shp = jax.ShapeDtypeStruct(w.shape, F32)
    return pl.pallas_call(body, grid=(n, r // br), name=name, in_specs=[spec] * 4, out_specs=[spec] * 3,
                          out_shape=[shp] * 3, compiler_params=_cparams(("parallel", "parallel")))(w, g, m, v)


def _adamw_w_in(w, g, m, v, name, bc=31):
    cols = w.shape[2]
    lead = lambda a: jnp.transpose(a, (2, 0, 1))

    def body(w_ref, g_ref, m_ref, v_ref, go_ref, d_ref, m2_ref, v2_ref):
        for l in range(2):
            gv = g_ref[:, l, :]
            d_ref[:, l, :], m2_ref[:, l, :], v2_ref[:, l, :] = _adam_math(w_ref[:, l, :], gv, m_ref[:, l, :], v_ref[:, l, :])
            go_ref[:, l, :] = gv

    spec = pl.BlockSpec((bc, 2, D), lambda i: (i, 0, 0))
    outs = pl.pallas_call(body, grid=(cols // bc,), name=name, in_specs=[spec] * 4, out_specs=[spec] * 4,
                          out_shape=[jax.ShapeDtypeStruct((cols, 2, D), F32)] * 4,
                          compiler_params=_cparams(("parallel",)))(lead(w), lead(g), lead(m), lead(v))
    return [jnp.transpose(o, (1, 2, 0)) for o in outs]


_SMALL_NAMES = ("norm_w", "conv_a_w", "gla_gate_w", "gla_gate_b", "gla_norm_w", "pool_w", "pool_scale", "ssd_conv_w",
                "ssd_conv_b", "ssd_dt_bias", "ssd_a_log", "ssd_d", "ssd_norm_w", "final_norm_w")
SMALL_ROWS = 72


def _adam_math(w, g, m, v):
    m2 = ADAM_B1 * m + (1.0 - ADAM_B1) * g
    v2 = ADAM_B2 * v + (1.0 - ADAM_B2) * (g * g)
    m_hat = m2 / (1.0 - ADAM_B1 ** ADAM_STEP)
    v_hat = v2 / (1.0 - ADAM_B2 ** ADAM_STEP)
    return -ADAM_LR * (m_hat / (jnp.sqrt(v_hat) + ADAM_EPS) + ADAM_WD * w), m2, v2


def _small_slices(name, chip):
    if name == "conv_a_w":
        return [((), slice(R_CAW, R_CAW + 3), slice(64 * chip, 64 * chip + 64))]
    if name == "ssd_conv_w":
        return [((), slice(R_SCW, R_SCW + 4), slice(192 * chip, 192 * chip + 192))]
    if name == "gla_gate_w":
        return [((), slice(0, 16), slice(768, 896))]
    if name == "pool_w":
        return [((g, slice(16 * q, 16 * q + 16)), slice(16, 32), slice(256 * q + 64 * g, 256 * q + 64 * g + 64))
                for g in range(4) for q in range(4)]
    row, lanes = {"gla_gate_b": (R_GB, slice(0, 128)), "gla_norm_w": (R_GNW, slice(0, 64)),
                  "pool_scale": (R_PSC, slice(0, 256)), "ssd_conv_b": (R_SCB, slice(0, 768)),
                  "ssd_dt_bias": (R_DTB, slice(16, 20)), "ssd_a_log": (R_AE, slice(0, 4)), "ssd_d": (R_DE, slice(0, 4)),
                  "ssd_norm_w": (R_SNW, slice(0, 256))}[name]
    return [((), slice(row, row + 1), lanes)]


def _small_step(sg0, sg1, dnw0, dnw1, head, w, m, v):
    n = len(_SMALL_NAMES)

    def body(*refs):
        sg0_ref, sg1_ref, dnw0_ref, dnw1_ref, head_ref = refs[0:5]
        w_refs, m_refs, v_refs = refs[5:5 + n], refs[5 + n:5 + 2 * n], refs[5 + 2 * n:5 + 3 * n]
        o = 5 + 3 * n
        g_out, d_out, m_out, v_out = refs[o:o + n], refs[o + n:o + 2 * n], refs[o + 2 * n:o + 3 * n], refs[o + 3 * n:o + 4 * n]
        loss_ref = refs[o + 4 * n]
        stage, pair, rbuf, acc, send_sems, recv_sems = refs[o + 4 * n + 1:]
        x, y, c = _place()
        chip = 2 * x + y
        stage[0:32, :] = sg0_ref[...]
        stage[32:64, :] = sg1_ref[...]
        stage[64:65, :] = dnw0_ref[0:1, :]
        stage[65:66, :] = dnw1_ref[0:1, :]
        stage[66:68, :] = head_ref[0:2, :]
        stage[68:72, :] = jnp.zeros((4, D), F32)
        sib = pltpu.make_async_remote_copy(src_ref=stage, dst_ref=pair, send_sem=send_sems.at[0], recv_sem=recv_sems.at[0],
                                           device_id=(x, y, 1 - c), device_id_type=MESH)
        sib.start()
        sib.wait()
        rbuf[0] = stage[...] + pair[...]
        sends = []
        for k, (px, py) in enumerate(((1 - x, y), (x, 1 - y), (1 - x, 1 - y)), start=1):
            cp = pltpu.make_async_remote_copy(src_ref=rbuf.at[0], dst_ref=rbuf.at[k], send_sem=send_sems.at[k],
                                              recv_sem=recv_sems.at[k], device_id=(px, py, c), device_id_type=MESH)
            cp.start()
            sends.append(cp)
        for cp in sends:
            cp.wait()
        slab = lambda s: jnp.where(s == 0, 0, jnp.where(s == 2, 1, jnp.where(s == 1, 2, 3)))
        total = rbuf[slab(jnp.bitwise_xor(chip, 0))]
        for s in range(1, 4):
            total = total + rbuf[slab(jnp.bitwise_xor(chip, s))]
        acc[...] = total
        loss_ref[...] = acc[67:68, 0:1]

        def update(i, idx, g):
            wv, mv, vv = w_refs[i][idx], m_refs[i][idx], v_refs[i][idx]
            d, m2, v2 = _adam_math(wv, g, mv, vv)
            g_out[i][idx], d_out[i][idx], m_out[i][idx], v_out[i][idx] = g, d, m2, v2

        for i, name in enumerate(_SMALL_NAMES):
            if name == "final_norm_w":
                update(i, (slice(0, 1), slice(None)), acc[66:67, :])
            elif name == "norm_w":
                for l in range(2):
                    update(i, (slice(l, l + 1), slice(None)), acc[64 + l:65 + l, :])
            elif name in ("conv_a_w", "ssd_conv_w"):
                for s in range(4):
                    @pl.when(chip == s)
                    def _(i=i, name=name, s=s):
                        for l in range(2):
                            (_, rows, lanes), = _small_slices(name, s)
                            update(i, (l,), acc[rows.start + 32 * l:rows.stop + 32 * l, lanes])
            else:
                for l in range(2):
                    for idx, rows, lanes in _small_slices(name, 0):
                        g = acc[rows.start + 32 * l:rows.stop + 32 * l, lanes]
                        if w_refs[i].ndim == 2:
                            update(i, (slice(l, l + 1), slice(None)), g)
                        else:
                            update(i, (l,) + idx, g)

    args = [sg0, sg1, dnw0, dnw1, head] + [d[k] for d in (w, m, v) for k in _SMALL_NAMES]
    shapes = [jax.ShapeDtypeStruct(w[k].shape, F32) for k in _SMALL_NAMES]
    vmem = pl.BlockSpec(memory_space=pltpu.VMEM)
    outs = pl.pallas_call(
        body, name="small_allreduce_adamw", in_specs=[vmem] * len(args), out_specs=[vmem] * (4 * n + 1),
        out_shape=shapes * 4 + [jax.ShapeDtypeStruct((1, 1), F32)],
        scratch_shapes=[pltpu.VMEM((SMALL_ROWS, D), F32), pltpu.VMEM((SMALL_ROWS, D), F32),
                        pltpu.VMEM((4, SMALL_ROWS, D), F32), pltpu.VMEM((SMALL_ROWS, D), F32),
                        pltpu.SemaphoreType.DMA((4,)), pltpu.SemaphoreType.DMA((4,))],
    )(*args)
    return outs[0:n], outs[n:2 * n], outs[2 * n:3 * n], outs[3 * n:4 * n], outs[4 * n]


def _permute_cols(w):
    parts = [w[..., s:s + n] for s, n in _PERM]
    parts.append(jnp.zeros(w.shape[:-1] + (NP - NPROJ,), w.dtype))
    return jnp.concatenate(parts, axis=-1)


def _unpermute_cols(w):
    return jnp.concatenate([w[..., s:s + n] for s, n in _UNPERM], axis=-1)


def _mixer_consts(layer, conv_a_w, gla_gate_w, gla_gate_b, gla_norm_w, pool_w, pool_scale, ssd_conv_w, ssd_conv_b,
                  ssd_dt_bias, ssd_a_log, ssd_d, ssd_norm_w):
    def row(v):
        return jnp.pad(v.reshape(1, -1), ((0, 0), (0, 768 - v.size)))

    dtb = jnp.zeros((128,), F32).at[16:20].set(ssd_dt_bias[layer])
    rows = [jnp.pad(conv_a_w[layer], ((0, 0), (0, 512))), row(gla_gate_b[layer]), row(jnp.tile(gla_norm_w[layer], 4)),
            row(pool_scale[layer]), row(ssd_conv_b[layer]), row(dtb), row(jnp.repeat(-jnp.exp(ssd_a_log[layer]), 64)),
            row(jnp.repeat(ssd_d[layer], 64)), row(ssd_norm_w[layer]), jnp.zeros((1, 768), F32), ssd_conv_w[layer]]
    prm = jnp.concatenate(rows, axis=0)
    gw = jnp.zeros((128, 128), F32).at[0:16].set(gla_gate_w[layer]).astype(BF16)
    pw = jnp.zeros((256, 256), F32)
    for g in range(4):
        pw = pw.at[64 * g:64 * g + 64, 64 * g:64 * g + 64].set(pool_w[layer, g])
    return prm, gw, pw.astype(BF16)


def _layer_weights(s_in, s_out):
    wp = _permute_cols(jnp.transpose(s_in, (1, 0, 2)).reshape(D, NPROJ))
    wo = s_out.reshape(D, D)
    return wp, wp.T, wo, wo.T


def _grad_slabs(dwp, dwo):
    return jnp.transpose(_unpermute_cols(dwp).reshape(D, 4, NPROJ // 4), (1, 0, 2)), dwo.reshape(4, D // 4, D)


class _Comm:
    def __init__(self, w_in16, w_out16):
        self.w_in16, self.w_out16 = w_in16, w_out16
        self.core = lax.axis_index("c").astype(jnp.int32).reshape(1)

    def gather_ici(self, layer, extra=None):
        return _rider_gather_ici((self.w_in16[layer], self.w_out16[layer]), extra)

    def pair_sum(self, layer, slabs, received):
        return [_pair_sum(self.core, a, b, name=f"reduce_pair_sum{layer}_{k}") for k, (a, b) in enumerate(zip(slabs, received))]

    def chip_sum(self, layer, gathered):
        return [_sum4(a, name=f"reduce_chip_sum{layer}_{k}") for k, a in enumerate(gathered)]


def _local_step(x, tgt, norm_w, final_norm_w, consts, wts0, wts1=None, comm=None):
    nw = [norm_w[l:l + 1] for l in range(2)]
    proj0, h0, slabs = _rmsproj(x, nw[0], wts0[0], name="rmsproj0", rider=comm and comm.gather_ici(1))
    (mix0, sg0, ss0), slabs = _mixer_fwd(proj0, *consts[0], name="mixer_fwd0", rider=comm and _rider_gather_d2d(slabs))
    if comm:
        wts1 = _layer_weights(*slabs)
    x1 = _outproj(x, mix0, wts0[2], name="outproj0")
    proj1, h1, _ = _rmsproj(x1, nw[1], wts1[0], name="rmsproj1")
    (mix1, sg1, ss1), _ = _mixer_fwd(proj1, *consts[1], name="mixer_fwd1")
    x2 = _outproj(x1, mix1, wts1[2], name="outproj1")
    dx, head = _head(x2, tgt, final_norm_w.reshape(1, D), name="loss_head")
    dmix = _matmul_rows(dx, wts1[3], name="dmix1")
    dwo1 = _dwout(mix1, dx, name="dwout1")
    (dproj, mgr1), _ = _mixer_bwd(proj1, dmix, sg1, ss1, *consts[1], name="mixer_bwd1")
    dwp1, _ = _dwin(h1, dproj, name="dwin1")
    slabs1 = _grad_slabs(dwp1, dwo1)
    (dx, dnw1), recv = _dxin(dproj, wts1[1], x1, dx, nw[1], name="dxin1", rider=comm and _rider_swap(slabs1))
    dmix = _matmul_rows(dx, wts0[3], name="dmix0")
    dwo0 = _dwout(mix0, dx, name="dwout0")
    scat = comm and _rider_scatter(comm.pair_sum(1, slabs1, recv))
    (dproj, mgr0), gathered = _mixer_bwd(proj0, dmix, sg0, ss0, *consts[0], name="mixer_bwd0", rider=scat)
    share = comm and _rider_share(comm.chip_sum(1, gathered), 1)
    dwp0, big = _dwin(h0, dproj, name="dwin0", rider=share)
    (dx, dnw0), _ = _dxin(dproj, wts0[1], x, dx, nw[0], name="dxin0")
    if comm:
        slabs0 = _grad_slabs(dwp0, dwo0)
        recv = _run_rider(_rider_swap(slabs0), "reduce_swap0")
        gathered = _run_rider(_rider_scatter(comm.pair_sum(0, slabs0, recv)), "reduce_scatter0")
        big = _run_rider(_rider_share(comm.chip_sum(0, gathered), 0, prev=big), "reduce_share0")
    else:
        big = (jnp.stack([dwp0, dwp1]), jnp.stack([dwo0, dwo1]))
    return head, dx, big, (dnw0, dnw1), (mgr0, mgr1)


def kernel(x, norm_w, w_in, conv_a_w, gla_gate_w, gla_gate_b, gla_norm_w, pool_w, pool_scale, ssd_conv_w, ssd_conv_b, ssd_dt_bias, ssd_a_log, ssd_d, ssd_norm_w, w_out, final_norm_w, loss_target, m_norm_w, m_w_in, m_conv_a_w, m_gla_gate_w, m_gla_gate_b, m_gla_norm_w, m_pool_w, m_pool_scale, m_ssd_conv_w, m_ssd_conv_b, m_ssd_dt_bias, m_ssd_a_log, m_ssd_d, m_ssd_norm_w, m_w_out, m_final_norm_w, v_norm_w, v_w_in, v_conv_a_w, v_gla_gate_w, v_gla_gate_b, v_gla_norm_w, v_pool_w, v_pool_scale, v_ssd_conv_w, v_ssd_conv_b, v_ssd_dt_bias, v_ssd_a_log, v_ssd_d, v_ssd_norm_w, v_w_out, v_final_norm_w):
    weights = dict(norm_w=norm_w, w_in=w_in, conv_a_w=conv_a_w, gla_gate_w=gla_gate_w, gla_gate_b=gla_gate_b,
                   gla_norm_w=gla_norm_w, pool_w=pool_w, pool_scale=pool_scale, ssd_conv_w=ssd_conv_w,
                   ssd_conv_b=ssd_conv_b, ssd_dt_bias=ssd_dt_bias, ssd_a_log=ssd_a_log, ssd_d=ssd_d,
                   ssd_norm_w=ssd_norm_w, w_out=w_out, final_norm_w=final_norm_w)
    m_in = dict(norm_w=m_norm_w, w_in=m_w_in, conv_a_w=m_conv_a_w, gla_gate_w=m_gla_gate_w, gla_gate_b=m_gla_gate_b,
                gla_norm_w=m_gla_norm_w, pool_w=m_pool_w, pool_scale=m_pool_scale, ssd_conv_w=m_ssd_conv_w,
                ssd_conv_b=m_ssd_conv_b, ssd_dt_bias=m_ssd_dt_bias, ssd_a_log=m_ssd_a_log, ssd_d=m_ssd_d,
                ssd_norm_w=m_ssd_norm_w, w_out=m_w_out, final_norm_w=m_final_norm_w)
    v_in = dict(norm_w=v_norm_w, w_in=v_w_in, conv_a_w=v_conv_a_w, gla_gate_w=v_gla_gate_w, gla_gate_b=v_gla_gate_b,
                gla_norm_w=v_gla_norm_w, pool_w=v_pool_w, pool_scale=v_pool_scale, ssd_conv_w=v_ssd_conv_w,
                ssd_conv_b=v_ssd_conv_b, ssd_dt_bias=v_ssd_dt_bias, ssd_a_log=v_ssd_a_log, ssd_d=v_ssd_d,
                ssd_norm_w=v_ssd_norm_w, w_out=v_w_out, final_norm_w=v_final_norm_w)
    order = ("norm_w", "w_in", "conv_a_w", "gla_gate_w", "gla_gate_b", "gla_norm_w", "pool_w", "pool_scale",
             "ssd_conv_w", "ssd_conv_b", "ssd_dt_bias", "ssd_a_log", "ssd_d", "ssd_norm_w", "w_out", "final_norm_w")
    t = x.shape[1]

    comm = _Comm(w_in.astype(BF16), w_out.astype(BF16))
    cshard = jnp.zeros((16, 256), F32)
    for l in range(2):
        cshard = cshard.at[8 * l:8 * l + 3, 0:64].set(conv_a_w[l]).at[8 * l + 3:8 * l + 7, 0:192].set(ssd_conv_w[l])
    s_in, s_out, g_c = _run_rider(comm.gather_ici(0, cshard), "gather_ici0")
    s_in, s_out = _run_rider(_rider_gather_d2d((s_in, s_out)), "gather_d2d0")
    conv_a_full = jnp.stack([jnp.concatenate([g_c[s, 8 * l:8 * l + 3, 0:64] for s in range(4)], axis=-1) for l in range(2)])
    ssd_conv_full = jnp.stack([jnp.concatenate([g_c[s, 8 * l + 3:8 * l + 7, 0:192] for s in range(4)], axis=-1)
                               for l in range(2)])
    consts = [_mixer_consts(l, conv_a_full, gla_gate_w, gla_gate_b, gla_norm_w, pool_w, pool_scale, ssd_conv_full,
                            ssd_conv_b, ssd_dt_bias, ssd_a_log, ssd_d, ssd_norm_w) for l in range(2)]

    head, dx, big, dnw, mgr = _local_step(x.reshape(t, D), loss_target.reshape(t, D), norm_w, final_norm_w, consts,
                                          _layer_weights(s_in, s_out), comm=comm)

    as2d = lambda d: {k: (d[k].reshape(1, D) if k == "final_norm_w" else d[k]) for k in _SMALL_NAMES}
    small = _small_step(mgr[0], mgr[1], dnw[0], dnw[1], head, as2d(weights), as2d(m_in), as2d(v_in))
    grads, delta, new_m, new_v = ({k: (a.reshape(D) if k == "final_norm_w" else a) for k, a in zip(_SMALL_NAMES, part)}
                                  for part in small[0:4])
    loss = small[4].reshape(())

    grads["w_out"] = big[1]

    grads["w_in"], delta["w_in"], new_m["w_in"], new_v["w_in"] = _adamw_w_in(w_in, big[0], m_w_in, v_w_in, name="adamw_w_in")
    delta["w_out"], new_m["w_out"], new_v["w_out"] = _adamw(w_out, big[1], m_w_out, v_w_out, name="adamw_w_out", br=256)

    return (loss, dx.reshape(1, t, D), *[grads[k] for k in order], *[delta[k] for k in order],
            *[new_m[k] for k in order], *[new_v[k] for k in order])
```

```python
import functools

import jax
import jax.numpy as jnp
from jax import lax
from jax.experimental import pallas as pl
from jax.experimental.pallas import tpu as pltpu

F32 = jnp.float32
BF16 = jnp.bfloat16
MESH = pl.DeviceIdType.MESH

D = 1024
CH = 64
EPS = 1e-6
NP = 3456
NPROJ = 3348
GLA_SCALE = 32.0 ** -0.5
INV_TAU = 1.0 / 16.0
TB = 256
NCH = TB // CH
assert TB == 256
HALO_W = 1536

C_SX, C_AH, C_AC, C_PU, C_AB, C_AZ = 0, 768, 1024, 1280, 1536, 1792
C_GQ, C_GK, C_GV, C_GZ, C_PZ, C_SZ, C_TL = 2048, 2176, 2304, 2560, 2816, 3072, 3328
_PERM = ((2576, 768), (0, 256), (512, 256), (1808, 256), (256, 256), (768, 256), (1024, 128), (1152, 128),
         (1280, 256), (1552, 256), (2064, 256), (2320, 256), (1536, 16), (3344, 4))
_UNPERM = ((768, 256), (1536, 256), (1024, 256), (1792, 256), (2048, 128), (2176, 128), (2304, 256), (3328, 16),
           (2560, 256), (1280, 256), (2816, 256), (3072, 256), (0, 768), (3344, 4))

R_CAW, R_GB, R_GNW, R_PSC, R_SCB, R_DTB, R_AE, R_DE, R_SNW, R_SCW = 0, 3, 4, 5, 6, 7, 8, 9, 10, 12

ADAM_LR, ADAM_B1, ADAM_B2, ADAM_EPS, ADAM_WD, ADAM_STEP = 0.001, 0.9, 0.999, 1e-08, 0.01, 10

VMEM_LIMIT = 56 * 1024 * 1024


def _cparams(sem, limit=VMEM_LIMIT):
    return pltpu.CompilerParams(dimension_semantics=sem, vmem_limit_bytes=limit)


_ANY = pl.BlockSpec(memory_space=pl.ANY)


def _place():
    return lax.axis_index("x"), lax.axis_index("y"), lax.axis_index("c")


class _Rider:
    def __init__(self, inputs, out_shapes, sems, start, finish, aliases=None):
        self.inputs, self.out_shapes, self.sems = tuple(inputs), tuple(out_shapes), tuple(sems)
        self.start, self.finish, self.aliases = start, finish, dict(aliases or {})


def _call(body, args, *, grid, in_specs, out_specs, out_shape, name, sem, scratch_shapes=(), rider=None):
    if rider is None:
        outs = pl.pallas_call(body, grid=grid, name=name, in_specs=list(in_specs), out_specs=list(out_specs),
                              out_shape=list(out_shape), scratch_shapes=list(scratch_shapes),
                              compiler_params=_cparams(sem))(*args)
        return list(outs), []
    ni, no, ns = len(args), len(out_shape), len(scratch_shapes)
    ri, ro = len(rider.inputs), len(rider.out_shapes)

    def full(*refs):
        ins, rins = refs[:ni], refs[ni:ni + ri]
        outs, routs = refs[ni + ri:ni + ri + no], refs[ni + ri + no:ni + ri + no + ro]
        scr, rsem = refs[ni + ri + no + ro:ni + ri + no + ro + ns], refs[ni + ri + no + ro + ns:]
        first = functools.reduce(jnp.logical_and, [pl.program_id(a) == 0 for a in range(len(grid))])
        last = functools.reduce(jnp.logical_and, [pl.program_id(a) == grid[a] - 1 for a in range(len(grid))])

        @pl.when(first)
        def _():
            rider.start(rins, routs, rsem)

        body(*ins, *outs, *scr)

        @pl.when(last)
        def _():
            rider.finish(rins, routs, rsem)

    outs = pl.pallas_call(
        full, grid=grid, name=name, in_specs=list(in_specs) + [_ANY] * ri, out_specs=list(out_specs) + [_ANY] * ro,
        out_shape=list(out_shape) + list(rider.out_shapes), scratch_shapes=list(scratch_shapes) + list(rider.sems),
        input_output_aliases={ni + k: no + v for k, v in rider.aliases.items()},
        compiler_params=_cparams(("arbitrary",) * len(grid)))(*args, *rider.inputs)
    return list(outs[:no]), list(outs[no:])


def _run_rider(rider, name):
    ri = len(rider.inputs)

    def body(*refs):
        rins, routs, rsem = refs[:ri], refs[ri:ri + len(rider.out_shapes)], refs[ri + len(rider.out_shapes):]
        rider.start(rins, routs, rsem)
        rider.finish(rins, routs, rsem)

    return list(pl.pallas_call(body, name=name, in_specs=[_ANY] * ri, out_specs=[_ANY] * len(rider.out_shapes),
                               out_shape=list(rider.out_shapes), scratch_shapes=list(rider.sems),
                               input_output_aliases=dict(rider.aliases))(*rider.inputs))


def _dot(a, b):
    return jnp.dot(a.astype(BF16), b.astype(BF16), preferred_element_type=F32)


def _dot_nt(a, b):
    return lax.dot_general(a.astype(BF16), b.astype(BF16), (((1,), (1,)), ((), ())), preferred_element_type=F32)


def _dot_tn(a, b):
    return lax.dot_general(a.astype(BF16), b.astype(BF16), (((0,), (0,)), ((), ())), preferred_element_type=F32)


def _split(a):
    hi = a.astype(BF16)
    lo = (a - hi.astype(F32)).astype(BF16)
    return hi, lo


def _dot2_l(a, b):
    hi, lo = _split(a)
    return _dot(hi, b) + _dot(lo, b)


def _dot2_r(a, b):
    hi, lo = _split(b)
    return _dot(a, hi) + _dot(a, lo)


def _dot3_l(a, b):
    hi, lo = _split(a)
    lo2 = ((a - hi.astype(F32)) - lo.astype(F32)).astype(BF16)
    return _dot(hi, b) + _dot(lo, b) + _dot(lo2, b)


def _dot2_nt(a, b):
    hi, lo = _split(a)
    return _dot_nt(hi, b) + _dot_nt(lo, b)


def _silu(z):
    return z * jax.nn.sigmoid(z)


def _lse1(x):
    return jnp.log(1.0 + jnp.exp(-jnp.abs(x)))


def _cs(a):
    return jnp.sum(a, axis=0, keepdims=True)


def _iota(shape, dim):
    return lax.broadcasted_iota(jnp.int32, shape, dim)


def _mixer_matrices():
    r, c = _iota((256, 256), 0), _iota((256, 256), 1)
    same_chunk = (r >> 6) == (c >> 6)
    mats = jnp.stack([jnp.where((c > r) & same_chunk, 1.0, 0.0), jnp.where((c < r) & same_chunk, 1.0, 0.0),
                      jnp.where(same_chunk, 1.0 / 64.0, 0.0), jnp.where((r < 128) & (r - 16 == (c >> 6)), 1.0, 0.0)])
    mask = jnp.where((_iota((256, 128), 0) >> 6) == (_iota((256, 128), 1) >> 5), 1.0, 0.0)
    return mats.astype(BF16), mask.astype(F32)


def _dn(ext, k, n, h):
    return pltpu.roll(ext, k, axis=0)[h:h + n]


def _up(ext, k, n):
    return pltpu.roll(ext, ext.shape[0] - k, axis=0)[:n]


def _pool_lane_select(lane, s2, s4, s8, s16):
    return jnp.where(lane < 64, s2, jnp.where(lane < 128, s4, jnp.where(lane < 192, s8, s16)))


def _winsum_dn(ext, lane):
    s2 = ext + pltpu.roll(ext, 1, axis=0)
    s4 = s2 + pltpu.roll(s2, 2, axis=0)
    s8 = s4 + pltpu.roll(s4, 4, axis=0)
    s16 = s8 + pltpu.roll(s8, 8, axis=0)
    return _pool_lane_select(lane, s2, s4, s8, s16)


def _winsum_up(ext, lane):
    m = ext.shape[0]
    s2 = ext + pltpu.roll(ext, m - 1, axis=0)
    s4 = s2 + pltpu.roll(s2, m - 2, axis=0)
    s8 = s4 + pltpu.roll(s4, m - 4, axis=0)
    s16 = s8 + pltpu.roll(s8, m - 8, axis=0)
    return _pool_lane_select(lane, s2, s4, s8, s16)


def _pool_inv_count(tile, n):
    lane = _iota((1, 256), 1)
    win = _pool_lane_select(lane, 2.0, 4.0, 8.0, 16.0).astype(F32)
    tpos = (tile * n + _iota((n, 1), 0) + 1).astype(F32)
    return jnp.where(tpos >= win, 1.0 / win, 1.0 / tpos)


def _silu_pair(z):
    s = jax.nn.sigmoid(z)
    return z * s, s * (1.0 + z * (1.0 - s))


def _chunks(a):
    return [a[c * CH:(c + 1) * CH] for c in range(a.shape[0] // CH)]


def _halves(fn, a, b):
    return jnp.concatenate([fn(a[:, 0:128], b[:, 0:128]), fn(a[:, 128:256], b[:, 128:256])], axis=1)


def _mixer_tile_prep(p_ref, xc, prm_ref, gw_v, cm_ref, mk_ref):
    tail = p_ref[:, C_TL:C_TL + 128]
    pre = _dot(tail, gw_v) + prm_ref[R_GB:R_GB + 1, 0:128]
    la = (jnp.minimum(pre, 0.0) - _lse1(pre)) * INV_TAU
    dtin = tail + prm_ref[R_DTB:R_DTB + 1, 0:128]
    dtf = jnp.maximum(dtin, 0.0) + _lse1(dtin)
    dte = _dot2_l(dtf, cm_ref[3, 0:128, :])
    da = dte * prm_ref[R_AE:R_AE + 1, 0:256]
    rev = _dot2_r(cm_ref[0], jnp.concatenate([la, da], axis=1))
    dec = jnp.exp(rev[:, 0:128])
    kd = p_ref[:, C_GK:C_GK + 128] * dec
    wdec = jnp.exp(rev[:, 128:384])
    w = wdec * dte
    xw = xc[:, 0:256] * w
    d_s = [jnp.exp(_cs(a)) for a in _chunks(la)]
    et = [jnp.exp(_cs(a)) for a in _chunks(da)]
    mask_t = mk_ref[...]
    ut_g = [_dot_tn(v, k) * mask_t for v, k in zip(_chunks(p_ref[:, C_GV:C_GV + 256]), _chunks(kd))]
    ut_s = [_halves(_dot_tn, b, x) for b, x in zip(_chunks(xc[:, 256:512]), _chunks(xw))]
    return tail, pre, dtin, dte, dec, kd, wdec, w, xw, d_s, et, ut_g, ut_s


def _rmsproj(x, nw, wp, name, tm=512, rider=None):
    t = x.shape[0]

    def body(x_ref, nw_ref, w_ref, o_ref, h_ref):
        xv = x_ref[...]
        rs = lax.rsqrt(jnp.mean(xv * xv, axis=-1, keepdims=True) + EPS)
        h = (xv * rs * nw_ref[...]).astype(BF16)
        h_ref[...] = h
        o_ref[...] = jnp.dot(h, w_ref[...], preferred_element_type=F32)

    (proj, h), extra = _call(
        body, (x, nw, wp), grid=(t // tm,), name=name, sem=("parallel",), rider=rider,
        in_specs=[pl.BlockSpec((tm, D), lambda i: (i, 0)), pl.BlockSpec((1, D), lambda i: (0, 0)),
                  pl.BlockSpec((D, NP), lambda i: (0, 0))],
        out_specs=[pl.BlockSpec((tm, NP), lambda i: (i, 0)), pl.BlockSpec((tm, D), lambda i: (i, 0))],
        out_shape=[jax.ShapeDtypeStruct((t, NP), F32), jax.ShapeDtypeStruct((t, D), BF16)])
    return proj, h, extra


def _head(x, tgt, fw, name, tm=512):
    t = x.shape[0]

    def body(x_ref, t_ref, w_ref, dx_ref, acc_ref):
        @pl.when(pl.program_id(0) == 0)
        def _():
            acc_ref[...] = jnp.zeros_like(acc_ref)

        xv = x_ref[...]
        w = w_ref[...]
        rs = lax.rsqrt(jnp.mean(xv * xv, axis=-1, keepdims=True) + EPS)
        xh = xv * rs
        err = xh * w - t_ref[...]
        dy = err * (1.0 / D)
        dxh = dy * w
        dx_ref[...] = rs * (dxh - xh * jnp.mean(dxh * xh, axis=-1, keepdims=True))
        acc_ref[0:1, :] += _cs(dy * xh)
        acc_ref[1:2, :] += jnp.zeros((1, D), F32) + (0.5 / D) * jnp.sum(err * err)

    return pl.pallas_call(
        body, grid=(t // tm,), name=name,
        in_specs=[pl.BlockSpec((tm, D), lambda i: (i, 0)), pl.BlockSpec((tm, D), lambda i: (i, 0)),
                  pl.BlockSpec((1, D), lambda i: (0, 0))],
        out_specs=[pl.BlockSpec((tm, D), lambda i: (i, 0)), pl.BlockSpec((8, D), lambda i: (0, 0))],
        out_shape=[jax.ShapeDtypeStruct((t, D), F32), jax.ShapeDtypeStruct((8, D), F32)],
        compiler_params=_cparams(("arbitrary",)),
    )(x, tgt, fw)


def _dxin(dp, wpt, x, dxn, nw, name, tm=512, rider=None):
    t = x.shape[0]

    def body(dp_ref, w_ref, x_ref, dxn_ref, nw_ref, dx_ref, dnw_ref):
        @pl.when(pl.program_id(0) == 0)
        def _():
            dnw_ref[...] = jnp.zeros_like(dnw_ref)

        dh = jnp.dot(dp_ref[...].astype(BF16), w_ref[...], preferred_element_type=F32)
        xv = x_ref[...]
        rs = lax.rsqrt(jnp.mean(xv * xv, axis=-1, keepdims=True) + EPS)
        xh = xv * rs
        dnw_ref[0:1, :] += _cs(dh * xh)
        dxh = dh * nw_ref[...]
        dx_ref[...] = dxn_ref[...] + rs * (dxh - xh * jnp.mean(dxh * xh, axis=-1, keepdims=True))

    return _call(
        body, (dp, wpt, x, dxn, nw), grid=(t // tm,), name=name, sem=("arbitrary",), rider=rider,
        in_specs=[pl.BlockSpec((tm, NP), lambda i: (i, 0)), pl.BlockSpec((NP, D), lambda i: (0, 0)),
                  pl.BlockSpec((tm, D), lambda i: (i, 0)), pl.BlockSpec((tm, D), lambda i: (i, 0)),
                  pl.BlockSpec((1, D), lambda i: (0, 0))],
        out_specs=[pl.BlockSpec((tm, D), lambda i: (i, 0)), pl.BlockSpec((8, D), lambda i: (0, 0))],
        out_shape=[jax.ShapeDtypeStruct((t, D), F32), jax.ShapeDtypeStruct((8, D), F32)])


def _dwin(h, dp, name, tm=512, tn=NP, rider=None):
    t = h.shape[0]

    def body(h_ref, dp_ref, o_ref):
        @pl.when(pl.program_id(1) == 0)
        def _():
            o_ref[...] = jnp.zeros_like(o_ref)

        o_ref[...] += _dot_tn(h_ref[...], dp_ref[...])

    (dwp,), extra = _call(
        body, (h, dp), grid=(NP // tn, t // tm), name=name, sem=("parallel", "arbitrary"), rider=rider,
        in_specs=[pl.BlockSpec((tm, D), lambda j, i: (i, 0)), pl.BlockSpec((tm, tn), lambda j, i: (i, j))],
        out_specs=[pl.BlockSpec((D, tn), lambda j, i: (0, j))], out_shape=[jax.ShapeDtypeStruct((D, NP), F32)])
    return dwp, extra


def _mixer_fwd(proj, x, wo, prm, gw, pw, cmat, mask, name, rider=None):
    t = proj.shape[0]
    nt, nc = t // TB, t // CH

    def body(p_ref, x_ref, wo_ref, prm_ref, gw_ref, pw_ref, cm_ref, mk_ref, mix_ref, sg_ref, ss_ref, xn_ref,
             sg_s, ss_s, h_ua, h_pu, h_sx):
        i = pl.program_id(0)

        @pl.when(i == 0)
        def _():
            for r in (sg_s, ss_s, h_ua, h_pu, h_sx):
                r[...] = jnp.zeros_like(r)

        lane = _iota((1, 256), 1)
        u = p_ref[:, C_AC:C_AC + 256] * p_ref[:, C_AH:C_AH + 256]
        ext = jnp.concatenate([h_ua[...], u], axis=0)
        cv = (prm_ref[R_CAW + 2:R_CAW + 3, 0:256] * u + prm_ref[R_CAW + 1:R_CAW + 2, 0:256] * _dn(ext, 1, TB, 8)
              + prm_ref[R_CAW:R_CAW + 1, 0:256] * _dn(ext, 2, TB, 8))
        mix_ref[:, 0:256] = (p_ref[:, C_AB:C_AB + 256] * cv * _silu(p_ref[:, C_AZ:C_AZ + 256])).astype(BF16)
        h_ua[...] = u[TB - 8:, :]
        pu = p_ref[:, C_PU:C_PU + 256]
        ext = jnp.concatenate([h_pu[...], pu], axis=0)
        pooled = _winsum_dn(ext, lane)[16:] * _pool_inv_count(i, TB) - pu
        mixed = _dot(pooled, pw_ref[...])
        mix_ref[:, 512:768] = (prm_ref[R_PSC:R_PSC + 1, 0:256] * mixed * _silu(p_ref[:, C_PZ:C_PZ + 256])).astype(BF16)
        h_pu[...] = pu[TB - 16:, :]
        sx = p_ref[:, C_SX:C_SX + 768]
        ext = jnp.concatenate([h_sx[...], sx], axis=0)
        xc = _silu(prm_ref[R_SCW + 3:R_SCW + 4, :] * sx + prm_ref[R_SCW + 2:R_SCW + 3, :] * _dn(ext, 1, TB, 8)
                   + prm_ref[R_SCW + 1:R_SCW + 2, :] * _dn(ext, 2, TB, 8) + prm_ref[R_SCW:R_SCW + 1, :] * _dn(ext, 3, TB, 8)
                   + prm_ref[R_SCB:R_SCB + 1, :])
        h_sx[...] = sx[TB - 8:, :]

        _, _, _, _, _, _, _, _, _, d_s, et, ut_g, ut_s = _mixer_tile_prep(p_ref, xc, prm_ref, gw_ref[...], cm_ref, mk_ref)
        s_g, s_s = sg_s[...], ss_s[...]
        o, y = [], []
        qs = _chunks(p_ref[:, C_GQ:C_GQ + 128] * GLA_SCALE)
        cm = _chunks(xc[:, 512:768])
        for c in range(NCH):
            sg_ref[c] = s_g
            ss_ref[c] = s_s
            s_g = s_g * d_s[c] + ut_g[c]
            s_s = s_s * et[c] + ut_s[c]
            o.append(_dot_nt(qs[c], s_g))
            y.append(_halves(_dot, cm[c], s_s))
        sg_s[...] = s_g
        ss_s[...] = s_s
        o = jnp.concatenate(o, axis=0)
        on = o * lax.rsqrt(_dot2_l(o * o, cm_ref[2]) + EPS)
        mix_ref[:, 256:512] = (on * prm_ref[R_GNW:R_GNW + 1, 0:256] * _silu(p_ref[:, C_GZ:C_GZ + 256])).astype(BF16)
        y2 = ((jnp.concatenate(y, axis=0) + prm_ref[R_DE:R_DE + 1, 0:256] * xc[:, 0:256])
              * _silu(p_ref[:, C_SZ:C_SZ + 256]))
        mix_ref[:, 768:1024] = (y2 * lax.rsqrt(jnp.mean(y2 * y2, axis=-1, keepdims=True) + EPS)
                                * prm_ref[R_SNW:R_SNW + 1, 0:256]).astype(BF16)
        xn_ref[...] = x_ref[...] + jnp.dot(mix_ref[...], wo_ref[...], preferred_element_type=F32)

    return _call(
        body, (proj, x, wo, prm, gw, pw, cmat, mask), grid=(nt,), name=name, sem=("arbitrary",), rider=rider,
        in_specs=[pl.BlockSpec((TB, NP), lambda i: (i, 0)), pl.BlockSpec((TB, D), lambda i: (i, 0)),
                  pl.BlockSpec((D, D), lambda i: (0, 0)), pl.BlockSpec((16, 768), lambda i: (0, 0)),
                  pl.BlockSpec((128, 128), lambda i: (0, 0)), pl.BlockSpec((256, 256), lambda i: (0, 0)),
                  pl.BlockSpec((4, 256, 256), lambda i: (0, 0, 0)), pl.BlockSpec((256, 128), lambda i: (0, 0))],
        out_specs=[pl.BlockSpec((TB, D), lambda i: (i, 0)), pl.BlockSpec((NCH, 256, 128), lambda i: (i, 0, 0)),
                   pl.BlockSpec((NCH, 128, 256), lambda i: (i, 0, 0)), pl.BlockSpec((TB, D), lambda i: (i, 0))],
        out_shape=[jax.ShapeDtypeStruct((t, D), BF16), jax.ShapeDtypeStruct((nc, 256, 128), F32),
                   jax.ShapeDtypeStruct((nc, 128, 256), F32), jax.ShapeDtypeStruct((t, D), F32)],
        scratch_shapes=[pltpu.VMEM((256, 128), F32), pltpu.VMEM((128, 256), F32), pltpu.VMEM((8, 256), F32),
                        pltpu.VMEM((16, 256), F32), pltpu.VMEM((8, 768), F32)])


def _mixer_bwd(proj, dxn, wot, mix, sg, ss, prm, gw, pw, cmat, mask, name, rider=None):
    t = proj.shape[0]
    nt = t // TB
    rev = lambda i: nt - 1 - i

    def body(p_ref, hp_ref, dxn_ref, wot_ref, mix_ref, sg_ref, ss_ref, prm_ref, gw_ref, pw_ref, cm_ref, mk_ref,
             dp_ref, sgc_ref, dwo_ref,
             gg_s, gs_s, h_dcv, h_dpl, h_dpre, gsm_ref, dgw_ref, dpw_ref, dm_ref):
        i = pl.program_id(0)
        tile = nt - 1 - i

        @pl.when(i == 0)
        def _():
            for r in (gg_s, gs_s, h_dcv, h_dpl, h_dpre, gsm_ref, dgw_ref, dpw_ref, dwo_ref):
                r[...] = jnp.zeros_like(r)

        dxn = dxn_ref[...].astype(BF16)
        dm_ref[...] = jnp.dot(dxn, wot_ref[...], preferred_element_type=F32)
        dwo_ref[...] += _dot_tn(mix_ref[...], dxn)

        lane = _iota((1, 256), 1)
        first = (tile > 0).astype(F32)
        ah, ac = p_ref[:, C_AH:C_AH + 256], p_ref[:, C_AC:C_AC + 256]
        ab, az = p_ref[:, C_AB:C_AB + 256], p_ref[:, C_AZ:C_AZ + 256]
        w0, w1, w2 = (prm_ref[R_CAW + j:R_CAW + j + 1, 0:256] for j in range(3))
        u = ac * ah
        ext = jnp.concatenate([hp_ref[8:16, C_AC:C_AC + 256] * hp_ref[8:16, C_AH:C_AH + 256] * first, u], axis=0)
        u1, u2 = _dn(ext, 1, TB, 8), _dn(ext, 2, TB, 8)
        cv = w2 * u + w1 * u1 + w0 * u2
        g = dm_ref[:, 0:256]
        sz, dsz = _silu_pair(az)
        dp_ref[:, C_AB:C_AB + 256] = (g * cv * sz).astype(BF16)
        dp_ref[:, C_AZ:C_AZ + 256] = (g * ab * cv * dsz).astype(BF16)
        dcv = g * ab * sz
        dext = jnp.concatenate([dcv, h_dcv[...]], axis=0)
        du = w2 * dcv + w1 * _up(dext, 1, TB) + w0 * _up(dext, 2, TB)
        dp_ref[:, C_AC:C_AC + 256] = (du * ah).astype(BF16)
        dp_ref[:, C_AH:C_AH + 256] = (du * ac).astype(BF16)
        gsm_ref[R_CAW:R_CAW + 1, 0:256] += _cs(dcv * u2)
        gsm_ref[R_CAW + 1:R_CAW + 2, 0:256] += _cs(dcv * u1)
        gsm_ref[R_CAW + 2:R_CAW + 3, 0:256] += _cs(dcv * u)
        h_dcv[...] = dcv[0:8, :]
        pu, pz = p_ref[:, C_PU:C_PU + 256], p_ref[:, C_PZ:C_PZ + 256]
        psc = prm_ref[R_PSC:R_PSC + 1, 0:256]
        icnt = _pool_inv_count(tile, TB)
        ext = jnp.concatenate([hp_ref[:, C_PU:C_PU + 256] * first, pu], axis=0)
        pooled = _winsum_dn(ext, lane)[16:] * icnt - pu
        pw_v = pw_ref[...]
        mixed = _dot(pooled, pw_v)
        g = dm_ref[:, 512:768]
        sz, dsz = _silu_pair(pz)
        gsm_ref[R_PSC:R_PSC + 1, 0:256] += _cs(g * mixed * sz)
        dp_ref[:, C_PZ:C_PZ + 256] = (g * psc * mixed * dsz).astype(BF16)
        dmixed = g * psc * sz
        dpw_ref[...] += _dot_tn(pooled, dmixed)
        dpooled = _dot_nt(dmixed, pw_v)
        qd = dpooled * icnt
        dext = jnp.concatenate([qd, h_dpl[...]], axis=0)
        dp_ref[:, C_PU:C_PU + 256] = (_winsum_up(dext, lane)[:TB] - dpooled).astype(BF16)
        h_dpl[...] = qd[0:16, :]
        sx = p_ref[:, C_SX:C_SX + 768]
        cw = [prm_ref[R_SCW + j:R_SCW + j + 1, :] for j in range(4)]
        ext = jnp.concatenate([hp_ref[8:16, C_SX:C_SX + 768] * first, sx], axis=0)
        sx1, sx2, sx3 = _dn(ext, 1, TB, 8), _dn(ext, 2, TB, 8), _dn(ext, 3, TB, 8)
        cpre = cw[3] * sx + cw[2] * sx1 + cw[1] * sx2 + cw[0] * sx3 + prm_ref[R_SCB:R_SCB + 1, :]
        xc, dxc = _silu_pair(cpre)
        xs, bm, cm = xc[:, 0:256], xc[:, 256:512], xc[:, 512:768]

        gw_v = gw_ref[...]
        tail, pre, dtin, dte, dec, kd, wdec, w, xw, d_s, et, ut_g, ut_s = _mixer_tile_prep(p_ref, xc, prm_ref, gw_v,
                                                                                          cm_ref, mk_ref)
        gmean = cm_ref[2]
        mask_t = mk_ref[...]
        gnw = prm_ref[R_GNW:R_GNW + 1, 0:256]
        a_e = prm_ref[R_AE:R_AE + 1, 0:256]
        d_e = prm_ref[R_DE:R_DE + 1, 0:256]
        snw = prm_ref[R_SNW:R_SNW + 1, 0:256]
        sg_in = [sg_ref[c] for c in range(NCH)]
        ss_in = [ss_ref[c] for c in range(NCH)]
        sg_n = [sg_in[c] * d_s[c] + ut_g[c] for c in range(NCH)]
        ss_n = [ss_in[c] * et[c] + ut_s[c] for c in range(NCH)]
        qs = _chunks(p_ref[:, C_GQ:C_GQ + 128] * GLA_SCALE)
        cm_c, bm_c, xw_c, kd_c = _chunks(cm), _chunks(bm), _chunks(xw), _chunks(kd)
        v_c = _chunks(p_ref[:, C_GV:C_GV + 256])
        o = jnp.concatenate([_dot_nt(qs[c], sg_n[c]) for c in range(NCH)], axis=0)
        y = jnp.concatenate([_halves(_dot, cm_c[c], ss_n[c]) for c in range(NCH)], axis=0) + d_e * xs
        gz = p_ref[:, C_GZ:C_GZ + 256]
        r = lax.rsqrt(_dot2_l(o * o, gmean) + EPS)
        on = o * r
        dyb = dm_ref[:, 256:512]
        sz, dsz = _silu_pair(gz)
        dp_ref[:, C_GZ:C_GZ + 256] = (dyb * on * gnw * dsz).astype(BF16)
        tg = dyb * sz
        gsm_ref[R_GNW:R_GNW + 1, 0:256] += _cs(tg * on)
        don = tg * gnw
        do_c = _chunks(r * (don - on * _dot2_l(don * on, gmean)))
        ssz = p_ref[:, C_SZ:C_SZ + 256]
        sil, dsil = _silu_pair(ssz)
        y2 = y * sil
        r = lax.rsqrt(jnp.mean(y2 * y2, axis=-1, keepdims=True) + EPS)
        yn = y2 * r
        dyd = dm_ref[:, 768:1024]
        gsm_ref[R_SNW:R_SNW + 1, 0:256] += _cs(dyd * yn)
        dn = dyd * snw
        dy2 = r * (dn - yn * jnp.mean(dn * yn, axis=-1, keepdims=True))
        dp_ref[:, C_SZ:C_SZ + 256] = (dy2 * y * dsil).astype(BF16)
        dy = dy2 * sil
        gsm_ref[R_DE:R_DE + 1, 0:256] += _cs(dy * xs)
        dy_c = _chunks(dy)
        dq = jnp.concatenate([_dot(do_c[c], sg_n[c]) for c in range(NCH)], axis=0)
        dp_ref[:, C_GQ:C_GQ + 128] = (dq * GLA_SCALE).astype(BF16)
        dcm = jnp.concatenate([_halves(_dot_nt, dy_c[c], ss_n[c]) for c in range(NCH)], axis=0)
        gg = [_dot_tn(do_c[c], qs[c]) * mask_t for c in range(NCH)]
        gs = [_halves(_dot_tn, cm_c[c], dy_c[c]) for c in range(NCH)]
        car_g, car_s = gg_s[...], gs_s[...]
        for c in reversed(range(NCH)):
            gg[c] = gg[c] + car_g
            gs[c] = gs[c] + car_s
            car_g = gg[c] * d_s[c]
            car_s = gs[c] * et[c]
        gg_s[...] = car_g
        gs_s[...] = car_s
        dkd = jnp.concatenate([_dot(v_c[c], gg[c]) for c in range(NCH)], axis=0)
        dp_ref[:, C_GV:C_GV + 256] = jnp.concatenate([_dot_nt(kd_c[c], gg[c]) for c in range(NCH)], axis=0).astype(BF16)
        dp_ref[:, C_GK:C_GK + 128] = (dkd * dec).astype(BF16)
        dbm = jnp.concatenate([_halves(_dot_nt, xw_c[c], gs[c]) for c in range(NCH)], axis=0)
        dxw = jnp.concatenate([_halves(_dot, bm_c[c], gs[c]) for c in range(NCH)], axis=0)
        dxs = dy * d_e + dxw * w
        dw = dxw * xs
        dsuf = _dot2_r(cm_ref[1], jnp.concatenate([dkd * kd, dw * dte * wdec], axis=1))
        tot_g = jnp.concatenate([jnp.broadcast_to(_cs(gg[c] * sg_in[c]) * d_s[c], (CH, 128)) for c in range(NCH)], axis=0)
        tot_s = jnp.concatenate([jnp.broadcast_to(_cs(gs[c] * ss_in[c]) * et[c], (CH, 256)) for c in range(NCH)], axis=0)
        dpre = (dsuf[:, 0:128] + tot_g) * INV_TAU * jax.nn.sigmoid(-pre)
        dgw_ref[...] += _dot_tn(tail, dpre)
        gsm_ref[R_GB:R_GB + 1, 0:128] += _cs(dpre)
        dda = dsuf[:, 128:384] + tot_s
        gsm_ref[R_AE:R_AE + 1, 0:256] += _cs(dda * dte)
        dtail_s = _dot2_nt(dw * wdec + dda * a_e, cm_ref[3, 0:128, :]) * jax.nn.sigmoid(dtin)
        gsm_ref[R_DTB:R_DTB + 1, 0:128] += _cs(dtail_s)
        dp_ref[:, C_TL:C_TL + 128] = (_dot_nt(dpre, gw_v) + dtail_s).astype(BF16)
        dpre_c = jnp.concatenate([dxs, dbm, dcm], axis=1) * dxc
        dext = jnp.concatenate([dpre_c, h_dpre[...]], axis=0)
        dp_ref[:, C_SX:C_SX + 768] = (cw[3] * dpre_c + cw[2] * _up(dext, 1, TB) + cw[1] * _up(dext, 2, TB)
                                      + cw[0] * _up(dext, 3, TB)).astype(BF16)
        gsm_ref[R_SCW + 3:R_SCW + 4, :] += _cs(dpre_c * sx)
        gsm_ref[R_SCW + 2:R_SCW + 3, :] += _cs(dpre_c * sx1)
        gsm_ref[R_SCW + 1:R_SCW + 2, :] += _cs(dpre_c * sx2)
        gsm_ref[R_SCW:R_SCW + 1, :] += _cs(dpre_c * sx3)
        gsm_ref[R_SCB:R_SCB + 1, :] += _cs(dpre_c)
        h_dpre[...] = dpre_c[0:8, :]

        @pl.when(i == nt - 1)
        def _():
            ri, ci = _iota((256, 256), 0), _iota((256, 256), 1)
            per_head = jnp.where((ri >> 6) == ci, 1.0, 0.0).astype(BF16)
            per_dv = jnp.where((ri & 63) == ci, 1.0, 0.0).astype(BF16)
            row = _iota((8, 256), 0)
            top = gsm_ref[0:8, 0:256]
            sgc_ref[0:8, 0:256] = jnp.where(row == R_GNW, _dot3_l(top, per_dv), top)
            bot = gsm_ref[8:16, 0:256]
            fold = _dot3_l(jnp.where(row == R_AE - 8, bot * a_e, bot), per_head)
            sgc_ref[8:16, 0:256] = jnp.where((row == R_AE - 8) | (row == R_DE - 8), fold, bot)
            sgc_ref[0:16, 256:768] = gsm_ref[:, 256:768]
            sgc_ref[0:16, 768:896] = dgw_ref[0:16, :]
            sgc_ref[0:16, 896:1024] = jnp.zeros((16, 128), F32)
            diag = _pool_lane_select(lane, dpw_ref[0:64, :], dpw_ref[64:128, :], dpw_ref[128:192, :], dpw_ref[192:256, :])
            for q in range(4):
                sgc_ref[16:32, 256 * q:256 * q + 256] = diag[16 * q:16 * q + 16, :]

    return _call(
        body, (proj, proj, dxn, wot, mix, sg, ss, prm, gw, pw, cmat, mask), grid=(nt,), name=name, sem=("arbitrary",),
        rider=rider,
        in_specs=[pl.BlockSpec((TB, NP), lambda i: (rev(i), 0)),
                  pl.BlockSpec((16, HALO_W), lambda i: (jnp.maximum(rev(i) * (TB // 16) - 1, 0), 0)),
                  pl.BlockSpec((TB, D), lambda i: (rev(i), 0)), pl.BlockSpec((D, D), lambda i: (0, 0)),
                  pl.BlockSpec((TB, D), lambda i: (rev(i), 0)),
                  pl.BlockSpec((NCH, 256, 128), lambda i: (rev(i), 0, 0)),
                  pl.BlockSpec((NCH, 128, 256), lambda i: (rev(i), 0, 0)),
                  pl.BlockSpec((16, 768), lambda i: (0, 0)), pl.BlockSpec((128, 128), lambda i: (0, 0)),
                  pl.BlockSpec((256, 256), lambda i: (0, 0)), pl.BlockSpec((4, 256, 256), lambda i: (0, 0, 0)),
                  pl.BlockSpec((256, 128), lambda i: (0, 0))],
        out_specs=[pl.BlockSpec((TB, NP), lambda i: (rev(i), 0)), pl.BlockSpec((32, 1024), lambda i: (0, 0)),
                   pl.BlockSpec((D, D), lambda i: (0, 0))],
        out_shape=[jax.ShapeDtypeStruct((t, NP), BF16), jax.ShapeDtypeStruct((32, 1024), F32),
                   jax.ShapeDtypeStruct((D, D), F32)],
        scratch_shapes=[pltpu.VMEM((256, 128), F32), pltpu.VMEM((128, 256), F32), pltpu.VMEM((8, 256), F32),
                        pltpu.VMEM((16, 256), F32), pltpu.VMEM((8, 768), F32), pltpu.VMEM((16, 768), F32),
                        pltpu.VMEM((128, 128), F32), pltpu.VMEM((256, 256), F32), pltpu.VMEM((TB, D), F32)])


def _half(c, n):
    return pl.ds(pl.multiple_of(c * (n // 2), n // 2), n // 2)


def _other_chips(x, y):
    return ((1 - x, y), (x, 1 - y), (1 - x, 1 - y))


def _remote(src, dst, send, recv, k, dev):
    return pltpu.make_async_remote_copy(src_ref=src, dst_ref=dst, send_sem=send.at[k], recv_sem=recv.at[k], device_id=dev,
                                        device_id_type=MESH)


def _sem(n):
    return pltpu.SemaphoreType.DMA((n,))


def _rider_gather_ici(shards, extra=None):
    shards = tuple(shards) + ((extra,) if extra is not None else ())
    n = len(shards)

    def copies(rins, routs, sems, arrivals=True):
        send, recv, loc = sems
        x, y, c = _place()
        me = 2 * x + y
        own = [pltpu.make_async_copy(rins[k], routs[k].at[me], loc.at[k]) for k in range(n)]
        out, inc = [], []
        for j, (px, py) in enumerate(_other_chips(x, y)):
            for k in range(n):
                whole = extra is not None and k == n - 1
                rows = pl.ds(0, shards[k].shape[0]) if whole else _half(c, shards[k].shape[0])
                out.append(_remote(rins[k].at[rows], routs[k].at[me, rows], send, recv, n * j + k, (px, py, c)))
                if arrivals:
                    inc.append(_remote(rins[k].at[rows], routs[k].at[2 * px + py, rows], send, recv, n * j + k, (px, py, c)))
        return own, out, inc

    def start(rins, routs, sems):
        own, out, _ = copies(rins, routs, sems, arrivals=False)
        for cp in own + out:
            cp.start()

    def finish(rins, routs, sems):
        own, out, inc = copies(rins, routs, sems)
        for cp in inc:
            cp.wait_recv()
        for cp in out:
            cp.wait_send()
        for cp in own:
            cp.wait()

    return _Rider(shards, [jax.ShapeDtypeStruct((4,) + a.shape, a.dtype) for a in shards],
                  [_sem(3 * n), _sem(3 * n), _sem(n)], start, finish)


def _rider_gather_d2d(slabs):
    slabs = tuple(slabs)
    n = len(slabs)

    def copies(routs, sems, arrivals=True):
        send, recv = sems
        x, y, c = _place()
        out, inc = [], []
        for j, (px, py) in enumerate(_other_chips(x, y)):
            for k in range(n):
                rows = slabs[k].shape[1]
                mine, theirs = routs[k].at[2 * px + py, _half(c, rows)], routs[k].at[2 * px + py, _half(1 - c, rows)]
                out.append(_remote(mine, mine, send, recv, n * j + k, (x, y, 1 - c)))
                if arrivals:
                    inc.append(_remote(theirs, theirs, send, recv, n * j + k, (x, y, 1 - c)))
        return out, inc

    def start(rins, routs, sems):
        for cp in copies(routs, sems, arrivals=False)[0]:
            cp.start()

    def finish(rins, routs, sems):
        out, inc = copies(routs, sems)
        for cp in inc:
            cp.wait_recv()
        for cp in out:
            cp.wait_send()

    return _Rider(slabs, [jax.ShapeDtypeStruct(a.shape, a.dtype) for a in slabs], [_sem(3 * n), _sem(3 * n)], start, finish,
                  aliases={k: k for k in range(n)})


def _rider_swap(parts):
    parts = tuple(parts)
    n = len(parts)

    def copies(rins, routs, sems):
        send, recv = sems
        x, y, c = _place()
        return [_remote(rins[k].at[:, _half(1 - c, parts[k].shape[1])], routs[k], send, recv, k, (x, y, 1 - c))
                for k in range(n)]

    def start(rins, routs, sems):
        for cp in copies(rins, routs, sems):
            cp.start()

    def finish(rins, routs, sems):
        for cp in copies(rins, routs, sems):
            cp.wait()

    return _Rider(parts, [jax.ShapeDtypeStruct((4, a.shape[1] // 2, a.shape[2]), a.dtype) for a in parts],
                  [_sem(n), _sem(n)], start, finish)


def _rider_scatter(parts):
    parts = tuple(parts)
    n = len(parts)

    def copies(rins, routs, sems, arrivals=True):
        send, recv, loc = sems
        x, y, c = _place()
        me = 2 * x + y
        own = [pltpu.make_async_copy(rins[k].at[me], routs[k].at[me], loc.at[k]) for k in range(n)]
        out, inc = [], []
        for j, (px, py) in enumerate(_other_chips(x, y)):
            for k in range(n):
                out.append(_remote(rins[k].at[2 * px + py], routs[k].at[me], send, recv, n * j + k, (px, py, c)))
                if arrivals:
                    inc.append(_remote(rins[k].at[me], routs[k].at[2 * px + py], send, recv, n * j + k, (px, py, c)))
        return own, out, inc

    def start(rins, routs, sems):
        own, out, _ = copies(rins, routs, sems, arrivals=False)
        for cp in own + out:
            cp.start()

    def finish(rins, routs, sems):
        own, out, inc = copies(rins, routs, sems)
        for cp in inc:
            cp.wait_recv()
        for cp in out:
            cp.wait_send()
        for cp in own:
            cp.wait()

    return _Rider(parts, [jax.ShapeDtypeStruct(a.shape, a.dtype) for a in parts], [_sem(3 * n), _sem(3 * n), _sem(n)],
                  start, finish)


def _rider_share(halves, layer, prev=None):
    halves = tuple(halves)
    n = len(halves)
    inputs = halves + (tuple(prev) if prev is not None else ())

    def copies(rins, routs, sems, arrivals=True):
        send, recv, loc = sems
        x, y, c = _place()
        own, out, inc = [], [], []
        for k in range(n):
            rows = 2 * halves[k].shape[0]
            own.append(pltpu.make_async_copy(rins[k], routs[k].at[layer, _half(c, rows)], loc.at[k]))
            out.append(_remote(rins[k], routs[k].at[layer, _half(c, rows)], send, recv, k, (x, y, 1 - c)))
            if arrivals:
                inc.append(_remote(rins[k], routs[k].at[layer, _half(1 - c, rows)], send, recv, k, (x, y, 1 - c)))
        return own, out, inc

    def start(rins, routs, sems):
        own, out, _ = copies(rins, routs, sems, arrivals=False)
        for cp in own + out:
            cp.start()

    def finish(rins, routs, sems):
        own, out, inc = copies(rins, routs, sems)
        for cp in inc:
            cp.wait_recv()
        for cp in out:
            cp.wait_send()
        for cp in own:
            cp.wait()

    return _Rider(inputs, [jax.ShapeDtypeStruct((2, 2 * a.shape[0], a.shape[1]), a.dtype) for a in halves],
                  [_sem(n), _sem(n), _sem(n)], start, finish,
                  aliases={n + k: k for k in range(n)} if prev is not None else None)


def _pair_sum(core, full, recv, name, br=128):
    n, rows, cols = recv.shape

    def body(c_ref, a_ref, b_ref, o_ref):
        o_ref[...] = (a_ref[...] + b_ref[...]).astype(BF16)

    nb = rows // br
    return pl.pallas_call(
        body, name=name, out_shape=jax.ShapeDtypeStruct(recv.shape, BF16),
        grid_spec=pltpu.PrefetchScalarGridSpec(
            num_scalar_prefetch=1, grid=(n, nb),
            in_specs=[pl.BlockSpec((1, br, cols), lambda i, j, c: (i, c[0] * nb + j, 0)),
                      pl.BlockSpec((1, br, cols), lambda i, j, c: (i, j, 0))],
            out_specs=pl.BlockSpec((1, br, cols), lambda i, j, c: (i, j, 0))),
        compiler_params=_cparams(("parallel", "parallel")))(core, full, recv)


def _sum4(a, name, br=128):
    _, r, c = a.shape

    def body(a_ref, o_ref):
        o_ref[...] = ((a_ref[0].astype(F32) + a_ref[1].astype(F32)) + a_ref[2].astype(F32)) + a_ref[3].astype(F32)

    return pl.pallas_call(body, grid=(r // br,), name=name,
                          in_specs=[pl.BlockSpec((4, br, c), lambda i: (0, i, 0))],
                          out_specs=pl.BlockSpec((br, c), lambda i: (i, 0)),
                          out_shape=jax.ShapeDtypeStruct((r, c), F32),
                          compiler_params=_cparams(("parallel",)))(a)


def _adamw(w, g, m, v, name, br):
    n, r, c = w.shape

    def body(w_ref, g_ref, m_ref, v_ref, d_ref, m2_ref, v2_ref):
        d_ref[...], m2_ref[...], v2_ref[...] = _adam_math(w_ref[...], g_ref[...], m_ref[...], v_ref[...])

    spec = pl.BlockSpec((1, br, c), lambda i, j: (i, j, 0))
    shp = jax.ShapeDtypeStruct(w.shape, F32)
    return pl.pallas_call(body, grid=(n, r // br), name=name, in_specs=[spec] * 4, out_specs=[spec] * 3,
                          out_shape=[shp] * 3, compiler_params=_cparams(("parallel", "parallel")))(w, g, m, v)


def _adamw_w_in(w, g, m, v, name, bc=31):
    cols = w.shape[2]
    lead = lambda a: jnp.transpose(a, (2, 0, 1))

    def body(w_ref, g_ref, m_ref, v_ref, go_ref, d_ref, m2_ref, v2_ref):
        for l in range(2):
            gv = g_ref[:, l, :]
            d_ref[:, l, :], m2_ref[:, l, :], v2_ref[:, l, :] = _adam_math(w_ref[:, l, :], gv, m_ref[:, l, :], v_ref[:, l, :])
            go_ref[:, l, :] = gv

    spec = pl.BlockSpec((bc, 2, D), lambda i: (i, 0, 0))
    outs = pl.pallas_call(body, grid=(cols // bc,), name=name, in_specs=[spec] * 4, out_specs=[spec] * 4,
                          out_shape=[jax.ShapeDtypeStruct((cols, 2, D), F32)] * 4,
                          compiler_params=_cparams(("parallel",)))(lead(w), lead(g), lead(m), lead(v))
    return [jnp.transpose(o, (1, 2, 0)) for o in outs]


_SMALL_NAMES = ("norm_w", "conv_a_w", "gla_gate_w", "gla_gate_b", "gla_norm_w", "pool_w", "pool_scale", "ssd_conv_w",
                "ssd_conv_b", "ssd_dt_bias", "ssd_a_log", "ssd_d", "ssd_norm_w", "final_norm_w")
SMALL_ROWS = 72


def _adam_math(w, g, m, v):
    m2 = ADAM_B1 * m + (1.0 - ADAM_B1) * g
    v2 = ADAM_B2 * v + (1.0 - ADAM_B2) * (g * g)
    m_hat = m2 / (1.0 - ADAM_B1 ** ADAM_STEP)
    v_hat = v2 / (1.0 - ADAM_B2 ** ADAM_STEP)
    return -ADAM_LR * (m_hat / (jnp.sqrt(v_hat) + ADAM_EPS) + ADAM_WD * w), m2, v2


def _small_slices(name, chip):
    if name == "conv_a_w":
        return [((), slice(R_CAW, R_CAW + 3), slice(64 * chip, 64 * chip + 64))]
    if name == "ssd_conv_w":
        return [((), slice(R_SCW, R_SCW + 4), slice(192 * chip, 192 * chip + 192))]
    if name == "gla_gate_w":
        return [((), slice(0, 16), slice(768, 896))]
    if name == "pool_w":
        return [((g, slice(16 * q, 16 * q + 16)), slice(16, 32), slice(256 * q + 64 * g, 256 * q + 64 * g + 64))
                for g in range(4) for q in range(4)]
    row, lanes = {"gla_gate_b": (R_GB, slice(0, 128)), "gla_norm_w": (R_GNW, slice(0, 64)),
                  "pool_scale": (R_PSC, slice(0, 256)), "ssd_conv_b": (R_SCB, slice(0, 768)),
                  "ssd_dt_bias": (R_DTB, slice(16, 20)), "ssd_a_log": (R_AE, slice(0, 4)), "ssd_d": (R_DE, slice(0, 4)),
                  "ssd_norm_w": (R_SNW, slice(0, 256))}[name]
    return [((), slice(row, row + 1), lanes)]


def _small_step(sg0, sg1, dnw0, dnw1, head, w, m, v):
    n = len(_SMALL_NAMES)

    def body(*refs):
        sg0_ref, sg1_ref, dnw0_ref, dnw1_ref, head_ref = refs[0:5]
        w_refs, m_refs, v_refs = refs[5:5 + n], refs[5 + n:5 + 2 * n], refs[5 + 2 * n:5 + 3 * n]
        o = 5 + 3 * n
        g_out, d_out, m_out, v_out = refs[o:o + n], refs[o + n:o + 2 * n], refs[o + 2 * n:o + 3 * n], refs[o + 3 * n:o + 4 * n]
        loss_ref = refs[o + 4 * n]
        stage, pair, rbuf, acc, send_sems, recv_sems = refs[o + 4 * n + 1:]
        x, y, c = _place()
        chip = 2 * x + y
        stage[0:32, :] = sg0_ref[...]
        stage[32:64, :] = sg1_ref[...]
        stage[64:65, :] = dnw0_ref[0:1, :]
        stage[65:66, :] = dnw1_ref[0:1, :]
        stage[66:68, :] = head_ref[0:2, :]
        stage[68:72, :] = jnp.zeros((4, D), F32)
        sib = pltpu.make_async_remote_copy(src_ref=stage, dst_ref=pair, send_sem=send_sems.at[0], recv_sem=recv_sems.at[0],
                                           device_id=(x, y, 1 - c), device_id_type=MESH)
        sib.start()
        sib.wait()
        rbuf[0] = stage[...] + pair[...]
        sends = []
        for k, (px, py) in enumerate(((1 - x, y), (x, 1 - y), (1 - x, 1 - y)), start=1):
            cp = pltpu.make_async_remote_copy(src_ref=rbuf.at[0], dst_ref=rbuf.at[k], send_sem=send_sems.at[k],
                                              recv_sem=recv_sems.at[k], device_id=(px, py, c), device_id_type=MESH)
            cp.start()
            sends.append(cp)
        for cp in sends:
            cp.wait()
        slab = lambda s: jnp.where(s == 0, 0, jnp.where(s == 2, 1, jnp.where(s == 1, 2, 3)))
        total = rbuf[slab(jnp.bitwise_xor(chip, 0))]
        for s in range(1, 4):
            total = total + rbuf[slab(jnp.bitwise_xor(chip, s))]
        acc[...] = total
        loss_ref[...] = acc[67:68, 0:1]

        def update(i, idx, g):
            wv, mv, vv = w_refs[i][idx], m_refs[i][idx], v_refs[i][idx]
            d, m2, v2 = _adam_math(wv, g, mv, vv)
            g_out[i][idx], d_out[i][idx], m_out[i][idx], v_out[i][idx] = g, d, m2, v2

        for i, name in enumerate(_SMALL_NAMES):
            if name == "final_norm_w":
                update(i, (slice(0, 1), slice(None)), acc[66:67, :])
            elif name == "norm_w":
                for l in range(2):
                    update(i, (slice(l, l + 1), slice(None)), acc[64 + l:65 + l, :])
            elif name in ("conv_a_w", "ssd_conv_w"):
                for s in range(4):
                    @pl.when(chip == s)
                    def _(i=i, name=name, s=s):
                        for l in range(2):
                            (_, rows, lanes), = _small_slices(name, s)
                            update(i, (l,), acc[rows.start + 32 * l:rows.stop + 32 * l, lanes])
            else:
                for l in range(2):
                    for idx, rows, lanes in _small_slices(name, 0):
                        g = acc[rows.start + 32 * l:rows.stop + 32 * l, lanes]
                        if w_refs[i].ndim == 2:
                            update(i, (slice(l, l + 1), slice(None)), g)
                        else:
                            update(i, (l,) + idx, g)

    args = [sg0, sg1, dnw0, dnw1, head] + [d[k] for d in (w, m, v) for k in _SMALL_NAMES]
    shapes = [jax.ShapeDtypeStruct(w[k].shape, F32) for k in _SMALL_NAMES]
    vmem = pl.BlockSpec(memory_space=pltpu.VMEM)
    outs = pl.pallas_call(
        body, name="small_allreduce_adamw", in_specs=[vmem] * len(args), out_specs=[vmem] * (4 * n + 1),
        out_shape=shapes * 4 + [jax.ShapeDtypeStruct((1, 1), F32)],
        scratch_shapes=[pltpu.VMEM((SMALL_ROWS, D), F32), pltpu.VMEM((SMALL_ROWS, D), F32),
                        pltpu.VMEM((4, SMALL_ROWS, D), F32), pltpu.VMEM((SMALL_ROWS, D), F32),
                        pltpu.SemaphoreType.DMA((4,)), pltpu.SemaphoreType.DMA((4,))],
    )(*args)
    return outs[0:n], outs[n:2 * n], outs[2 * n:3 * n], outs[3 * n:4 * n], outs[4 * n]


def _permute_cols(w):
    parts = [w[..., s:s + n] for s, n in _PERM]
    parts.append(jnp.zeros(w.shape[:-1] + (NP - NPROJ,), w.dtype))
    return jnp.concatenate(parts, axis=-1)


def _unpermute_cols(w):
    return jnp.concatenate([w[..., s:s + n] for s, n in _UNPERM], axis=-1)


def _mixer_consts(layer, conv_a_w, gla_gate_w, gla_gate_b, gla_norm_w, pool_w, pool_scale, ssd_conv_w, ssd_conv_b,
                  ssd_dt_bias, ssd_a_log, ssd_d, ssd_norm_w):
    def row(v):
        return jnp.pad(v.reshape(1, -1), ((0, 0), (0, 768 - v.size)))

    dtb = jnp.zeros((128,), F32).at[16:20].set(ssd_dt_bias[layer])
    rows = [jnp.pad(conv_a_w[layer], ((0, 0), (0, 512))), row(gla_gate_b[layer]), row(jnp.tile(gla_norm_w[layer], 4)),
            row(pool_scale[layer]), row(ssd_conv_b[layer]), row(dtb), row(jnp.repeat(-jnp.exp(ssd_a_log[layer]), 64)),
            row(jnp.repeat(ssd_d[layer], 64)), row(ssd_norm_w[layer]), jnp.zeros((1, 768), F32), ssd_conv_w[layer]]
    prm = jnp.concatenate(rows, axis=0)
    gw = jnp.zeros((128, 128), F32).at[0:16].set(gla_gate_w[layer]).astype(BF16)
    pw = jnp.zeros((256, 256), F32)
    for g in range(4):
        pw = pw.at[64 * g:64 * g + 64, 64 * g:64 * g + 64].set(pool_w[layer, g])
    return (prm, gw, pw.astype(BF16)) + _mixer_matrices()


def _layer_weights(s_in, s_out):
    wp = _permute_cols(jnp.transpose(s_in, (1, 0, 2)).reshape(D, NPROJ))
    wo = s_out.reshape(D, D)
    return wp, wp.T, wo, wo.T


def _grad_slabs(dwp, dwo):
    return jnp.transpose(_unpermute_cols(dwp).reshape(D, 4, NPROJ // 4), (1, 0, 2)), dwo.reshape(4, D // 4, D)


class _Comm:
    def __init__(self, w_in16, w_out16):
        self.w_in16, self.w_out16 = w_in16, w_out16
        self.core = lax.axis_index("c").astype(jnp.int32).reshape(1)

    def gather_ici(self, layer, extra=None):
        return _rider_gather_ici((self.w_in16[layer], self.w_out16[layer]), extra)

    def pair_sum(self, layer, slabs, received):
        return [_pair_sum(self.core, a, b, name=f"reduce_pair_sum{layer}_{k}") for k, (a, b) in enumerate(zip(slabs, received))]

    def chip_sum(self, layer, gathered):
        return [_sum4(a, name=f"reduce_chip_sum{layer}_{k}") for k, a in enumerate(gathered)]


def _local_step(x, tgt, norm_w, final_norm_w, consts, wts0, wts1=None, comm=None):
    nw = [norm_w[l:l + 1] for l in range(2)]
    proj0, h0, slabs = _rmsproj(x, nw[0], wts0[0], name="rmsproj0", rider=comm and comm.gather_ici(1))
    (mix0, sg0, ss0, x1), slabs = _mixer_fwd(proj0, x, wts0[2], *consts[0], name="mixer_fwd0",
                                             rider=comm and _rider_gather_d2d(slabs))
    if comm:
        wts1 = _layer_weights(*slabs)
    proj1, h1, _ = _rmsproj(x1, nw[1], wts1[0], name="rmsproj1")
    (mix1, sg1, ss1, x2), _ = _mixer_fwd(proj1, x1, wts1[2], *consts[1], name="mixer_fwd1")
    dx, head = _head(x2, tgt, final_norm_w.reshape(1, D), name="loss_head")
    (dproj, mgr1, dwo1), _ = _mixer_bwd(proj1, dx, wts1[3], mix1, sg1, ss1, *consts[1], name="mixer_bwd1")
    dwp1, _ = _dwin(h1, dproj, name="dwin1")
    slabs1 = _grad_slabs(dwp1, dwo1)
    (dx, dnw1), recv = _dxin(dproj, wts1[1], x1, dx, nw[1], name="dxin1", rider=comm and _rider_swap(slabs1))
    scat = comm and _rider_scatter(comm.pair_sum(1, slabs1, recv))
    (dproj, mgr0, dwo0), gathered = _mixer_bwd(proj0, dx, wts0[3], mix0, sg0, ss0, *consts[0], name="mixer_bwd0", rider=scat)
    share = comm and _rider_share(comm.chip_sum(1, gathered), 1)
    dwp0, big = _dwin(h0, dproj, name="dwin0", rider=share)
    (dx, dnw0), _ = _dxin(dproj, wts0[1], x, dx, nw[0], name="dxin0")
    if comm:
        slabs0 = _grad_slabs(dwp0, dwo0)
        recv = _run_rider(_rider_swap(slabs0), "reduce_swap0")
        gathered = _run_rider(_rider_scatter(comm.pair_sum(0, slabs0, recv)), "reduce_scatter0")
        big = _run_rider(_rider_share(comm.chip_sum(0, gathered), 0, prev=big), "reduce_share0")
    else:
        big = (jnp.stack([dwp0, dwp1]), jnp.stack([dwo0, dwo1]))
    return head, dx, big, (dnw0, dnw1), (mgr0, mgr1)


def kernel(x, norm_w, w_in, conv_a_w, gla_gate_w, gla_gate_b, gla_norm_w, pool_w, pool_scale, ssd_conv_w, ssd_conv_b, ssd_dt_bias, ssd_a_log, ssd_d, ssd_norm_w, w_out, final_norm_w, loss_target, m_norm_w, m_w_in, m_conv_a_w, m_gla_gate_w, m_gla_gate_b, m_gla_norm_w, m_pool_w, m_pool_scale, m_ssd_conv_w, m_ssd_conv_b, m_ssd_dt_bias, m_ssd_a_log, m_ssd_d, m_ssd_norm_w, m_w_out, m_final_norm_w, v_norm_w, v_w_in, v_conv_a_w, v_gla_gate_w, v_gla_gate_b, v_gla_norm_w, v_pool_w, v_pool_scale, v_ssd_conv_w, v_ssd_conv_b, v_ssd_dt_bias, v_ssd_a_log, v_ssd_d, v_ssd_norm_w, v_w_out, v_final_norm_w):
    weights = dict(norm_w=norm_w, w_in=w_in, conv_a_w=conv_a_w, gla_gate_w=gla_gate_w, gla_gate_b=gla_gate_b,
                   gla_norm_w=gla_norm_w, pool_w=pool_w, pool_scale=pool_scale, ssd_conv_w=ssd_conv_w,
                   ssd_conv_b=ssd_conv_b, ssd_dt_bias=ssd_dt_bias, ssd_a_log=ssd_a_log, ssd_d=ssd_d,
                   ssd_norm_w=ssd_norm_w, w_out=w_out, final_norm_w=final_norm_w)
    m_in = dict(norm_w=m_norm_w, w_in=m_w_in, conv_a_w=m_conv_a_w, gla_gate_w=m_gla_gate_w, gla_gate_b=m_gla_gate_b,
                gla_norm_w=m_gla_norm_w, pool_w=m_pool_w, pool_scale=m_pool_scale, ssd_conv_w=m_ssd_conv_w,
                ssd_conv_b=m_ssd_conv_b, ssd_dt_bias=m_ssd_dt_bias, ssd_a_log=m_ssd_a_log, ssd_d=m_ssd_d,
                ssd_norm_w=m_ssd_norm_w, w_out=m_w_out, final_norm_w=m_final_norm_w)
    v_in = dict(norm_w=v_norm_w, w_in=v_w_in, conv_a_w=v_conv_a_w, gla_gate_w=v_gla_gate_w, gla_gate_b=v_gla_gate_b,
                gla_norm_w=v_gla_norm_w, pool_w=v_pool_w, pool_scale=v_pool_scale, ssd_conv_w=v_ssd_conv_w,
                ssd_conv_b=v_ssd_conv_b, ssd_dt_bias=v_ssd_dt_bias, ssd_a_log=v_ssd_a_log, ssd_d=v_ssd_d,
                ssd_norm_w=v_ssd_norm_w, w_out=v_w_out, final_norm_w=v_final_norm_w)
    order = ("norm_w", "w_in", "conv_a_w", "gla_gate_w", "gla_gate_b", "gla_norm_w", "pool_w", "pool_scale",
             "ssd_conv_w", "ssd_conv_b", "ssd_dt_bias", "ssd_a_log", "ssd_d", "ssd_norm_w", "w_out", "final_norm_w")
    t = x.shape[1]

    comm = _Comm(w_in.astype(BF16), w_out.astype(BF16))
    cshard = jnp.zeros((16, 256), F32)
    for l in range(2):
        cshard = cshard.at[8 * l:8 * l + 3, 0:64].set(conv_a_w[l]).at[8 * l + 3:8 * l + 7, 0:192].set(ssd_conv_w[l])
    s_in, s_out, g_c = _run_rider(comm.gather_ici(0, cshard), "gather_ici0")
    s_in, s_out = _run_rider(_rider_gather_d2d((s_in, s_out)), "gather_d2d0")
    conv_a_full = jnp.stack([jnp.concatenate([g_c[s, 8 * l:8 * l + 3, 0:64] for s in range(4)], axis=-1) for l in range(2)])
    ssd_conv_full = jnp.stack([jnp.concatenate([g_c[s, 8 * l + 3:8 * l + 7, 0:192] for s in range(4)], axis=-1)
                               for l in range(2)])
    consts = [_mixer_consts(l, conv_a_full, gla_gate_w, gla_gate_b, gla_norm_w, pool_w, pool_scale, ssd_conv_full,
                            ssd_conv_b, ssd_dt_bias, ssd_a_log, ssd_d, ssd_norm_w) for l in range(2)]

    head, dx, big, dnw, mgr = _local_step(x.reshape(t, D), loss_target.reshape(t, D), norm_w, final_norm_w, consts,
                                          _layer_weights(s_in, s_out), comm=comm)

    as2d = lambda d: {k: (d[k].reshape(1, D) if k == "final_norm_w" else d[k]) for k in _SMALL_NAMES}
    small = _small_step(mgr[0], mgr[1], dnw[0], dnw[1], head, as2d(weights), as2d(m_in), as2d(v_in))
    grads, delta, new_m, new_v = ({k: (a.reshape(D) if k == "final_norm_w" else a) for k, a in zip(_SMALL_NAMES, part)}
                                  for part in small[0:4])
    loss = small[4].reshape(())

    grads["w_out"] = big[1]

    grads["w_in"], delta["w_in"], new_m["w_in"], new_v["w_in"] = _adamw_w_in(w_in, big[0], m_w_in, v_w_in, name="adamw_w_in")
    delta["w_out"], new_m["w_out"], new_v["w_out"] = _adamw(w_out, big[1], m_w_out, v_w_out, name="adamw_w_out", br=256)

    return (loss, dx.reshape(1, t, D), *[grads[k] for k in order], *[delta[k] for k in order],
            *[new_m[k] for k in order], *[new_v[k] for k in order])
```

```python
import functools

import jax
import jax.numpy as jnp
from jax import lax
from jax.experimental import pallas as pl
from jax.experimental.pallas import tpu as pltpu

F32 = jnp.float32
BF16 = jnp.bfloat16
MESH = pl.DeviceIdType.MESH

D = 1024
CH = 64
EPS = 1e-6
NP = 3456
NPROJ = 3348
GLA_SCALE = 32.0 ** -0.5
INV_TAU = 1.0 / 16.0
TB = 256
NCH = TB // CH
assert TB == 256
HALO_W = NP

C_AH, C_AB, C_AC, C_AZ, C_GQ, C_GK, C_GV = 0, 256, 512, 768, 1024, 1152, 1280
C_GZ, C_PU, C_PZ, C_SZ, C_SX, C_TL = 1536, 1792, 2048, 2304, 2560, 3328
_PERM = ((0, 1536), (1552, 1792), (1536, 16), (3344, 4))
_UNPERM = ((0, 1536), (3328, 16), (1536, 1792), (3344, 4))

R_CAW, R_GB, R_GNW, R_PSC, R_SCB, R_DTB, R_AE, R_DE, R_SNW, R_SCW = 0, 3, 4, 5, 6, 7, 8, 9, 10, 12

ADAM_LR, ADAM_B1, ADAM_B2, ADAM_EPS, ADAM_WD, ADAM_STEP = 0.001, 0.9, 0.999, 1e-08, 0.01, 10

VMEM_LIMIT = 56 * 1024 * 1024


def _cparams(sem, limit=VMEM_LIMIT):
    return pltpu.CompilerParams(dimension_semantics=sem, vmem_limit_bytes=limit)


_ANY = pl.BlockSpec(memory_space=pl.ANY)


def _place():
    return lax.axis_index("x"), lax.axis_index("y"), lax.axis_index("c")


class _Rider:
    def __init__(self, inputs, out_shapes, sems, start, finish, aliases=None):
        self.inputs, self.out_shapes, self.sems = tuple(inputs), tuple(out_shapes), tuple(sems)
        self.start, self.finish, self.aliases = start, finish, dict(aliases or {})


def _call(body, args, *, grid, in_specs, out_specs, out_shape, name, sem, scratch_shapes=(), rider=None):
    if rider is None:
        outs = pl.pallas_call(body, grid=grid, name=name, in_specs=list(in_specs), out_specs=list(out_specs),
                              out_shape=list(out_shape), scratch_shapes=list(scratch_shapes),
                              compiler_params=_cparams(sem))(*args)
        return list(outs), []
    ni, no, ns = len(args), len(out_shape), len(scratch_shapes)
    ri, ro = len(rider.inputs), len(rider.out_shapes)

    def full(*refs):
        ins, rins = refs[:ni], refs[ni:ni + ri]
        outs, routs = refs[ni + ri:ni + ri + no], refs[ni + ri + no:ni + ri + no + ro]
        scr, rsem = refs[ni + ri + no + ro:ni + ri + no + ro + ns], refs[ni + ri + no + ro + ns:]
        first = functools.reduce(jnp.logical_and, [pl.program_id(a) == 0 for a in range(len(grid))])
        last = functools.reduce(jnp.logical_and, [pl.program_id(a) == grid[a] - 1 for a in range(len(grid))])

        @pl.when(first)
        def _():
            rider.start(rins, routs, rsem)

        body(*ins, *outs, *scr)

        @pl.when(last)
        def _():
            rider.finish(rins, routs, rsem)

    outs = pl.pallas_call(
        full, grid=grid, name=name, in_specs=list(in_specs) + [_ANY] * ri, out_specs=list(out_specs) + [_ANY] * ro,
        out_shape=list(out_shape) + list(rider.out_shapes), scratch_shapes=list(scratch_shapes) + list(rider.sems),
        input_output_aliases={ni + k: no + v for k, v in rider.aliases.items()},
        compiler_params=_cparams(("arbitrary",) * len(grid)))(*args, *rider.inputs)
    return list(outs[:no]), list(outs[no:])


def _run_rider(rider, name):
    ri = len(rider.inputs)

    def body(*refs):
        rins, routs, rsem = refs[:ri], refs[ri:ri + len(rider.out_shapes)], refs[ri + len(rider.out_shapes):]
        rider.start(rins, routs, rsem)
        rider.finish(rins, routs, rsem)

    return list(pl.pallas_call(body, name=name, in_specs=[_ANY] * ri, out_specs=[_ANY] * len(rider.out_shapes),
                               out_shape=list(rider.out_shapes), scratch_shapes=list(rider.sems),
                               input_output_aliases=dict(rider.aliases))(*rider.inputs))


def _dot(a, b):
    return jnp.dot(a.astype(BF16), b.astype(BF16), preferred_element_type=F32)


def _dot_nt(a, b):
    return lax.dot_general(a.astype(BF16), b.astype(BF16), (((1,), (1,)), ((), ())), preferred_element_type=F32)


def _dot_tn(a, b):
    return lax.dot_general(a.astype(BF16), b.astype(BF16), (((0,), (0,)), ((), ())), preferred_element_type=F32)


def _split(a):
    hi = a.astype(BF16)
    lo = (a - hi.astype(F32)).astype(BF16)
    return hi, lo


def _dot2_l(a, b):
    hi, lo = _split(a)
    return _dot(hi, b) + _dot(lo, b)


def _dot2_r(a, b):
    hi, lo = _split(b)
    return _dot(a, hi) + _dot(a, lo)


def _dot3_l(a, b):
    hi, lo = _split(a)
    lo2 = ((a - hi.astype(F32)) - lo.astype(F32)).astype(BF16)
    return _dot(hi, b) + _dot(lo, b) + _dot(lo2, b)


def _dot2_nt(a, b):
    hi, lo = _split(a)
    return _dot_nt(hi, b) + _dot_nt(lo, b)


def _silu(z):
    return z * jax.nn.sigmoid(z)


def _lse1(x):
    return jnp.log(1.0 + jnp.exp(-jnp.abs(x)))


def _cs(a):
    return jnp.sum(a, axis=0, keepdims=True)


def _iota(shape, dim):
    return lax.broadcasted_iota(jnp.int32, shape, dim)


def _mixer_matrices():
    r, c = _iota((256, 256), 0), _iota((256, 256), 1)
    same_chunk = (r >> 6) == (c >> 6)
    mats = jnp.stack([jnp.where((c > r) & same_chunk, 1.0, 0.0), jnp.where((c < r) & same_chunk, 1.0, 0.0),
                      jnp.where(same_chunk, 1.0 / 64.0, 0.0), jnp.where((r < 128) & (r - 16 == (c >> 6)), 1.0, 0.0)])
    mask = jnp.where((_iota((256, 128), 0) >> 6) == (_iota((256, 128), 1) >> 5), 1.0, 0.0)
    return mats.astype(BF16), mask.astype(F32)


def _dn(ext, k, n, h):
    return pltpu.roll(ext, k, axis=0)[h:h + n]


def _up(ext, k, n):
    return pltpu.roll(ext, ext.shape[0] - k, axis=0)[:n]


def _pool_lane_select(lane, s2, s4, s8, s16):
    return jnp.where(lane < 64, s2, jnp.where(lane < 128, s4, jnp.where(lane < 192, s8, s16)))


def _winsum_dn(ext, lane):
    s2 = ext + pltpu.roll(ext, 1, axis=0)
    s4 = s2 + pltpu.roll(s2, 2, axis=0)
    s8 = s4 + pltpu.roll(s4, 4, axis=0)
    s16 = s8 + pltpu.roll(s8, 8, axis=0)
    return _pool_lane_select(lane, s2, s4, s8, s16)


def _winsum_up(ext, lane):
    m = ext.shape[0]
    s2 = ext + pltpu.roll(ext, m - 1, axis=0)
    s4 = s2 + pltpu.roll(s2, m - 2, axis=0)
    s8 = s4 + pltpu.roll(s4, m - 4, axis=0)
    s16 = s8 + pltpu.roll(s8, m - 8, axis=0)
    return _pool_lane_select(lane, s2, s4, s8, s16)


def _pool_inv_count(tile, n):
    lane = _iota((1, 256), 1)
    win = _pool_lane_select(lane, 2.0, 4.0, 8.0, 16.0).astype(F32)
    tpos = (tile * n + _iota((n, 1), 0) + 1).astype(F32)
    return jnp.where(tpos >= win, 1.0 / win, 1.0 / tpos)


def _silu_pair(z):
    s = jax.nn.sigmoid(z)
    return z * s, s * (1.0 + z * (1.0 - s))


def _chunks(a):
    return [a[c * CH:(c + 1) * CH] for c in range(a.shape[0] // CH)]


def _halves(fn, a, b):
    return jnp.concatenate([fn(a[:, 0:128], b[:, 0:128]), fn(a[:, 128:256], b[:, 128:256])], axis=1)


def _mixer_tile_prep(p_ref, xc, prm_ref, gw_v, cm_ref, mk_ref):
    tail = p_ref[:, C_TL:C_TL + 128]
    pre = _dot(tail, gw_v) + prm_ref[R_GB:R_GB + 1, 0:128]
    la = (jnp.minimum(pre, 0.0) - _lse1(pre)) * INV_TAU
    dtin = tail + prm_ref[R_DTB:R_DTB + 1, 0:128]
    dtf = jnp.maximum(dtin, 0.0) + _lse1(dtin)
    dte = _dot2_l(dtf, cm_ref[3, 0:128, :])
    da = dte * prm_ref[R_AE:R_AE + 1, 0:256]
    rev = _dot2_r(cm_ref[0], jnp.concatenate([la, da], axis=1))
    dec = jnp.exp(rev[:, 0:128])
    kd = p_ref[:, C_GK:C_GK + 128] * dec
    wdec = jnp.exp(rev[:, 128:384])
    w = wdec * dte
    xw = xc[:, 0:256] * w
    d_s = [jnp.exp(_cs(a)) for a in _chunks(la)]
    et = [jnp.exp(_cs(a)) for a in _chunks(da)]
    mask_t = mk_ref[...]
    ut_g = [_dot_tn(v, k) * mask_t for v, k in zip(_chunks(p_ref[:, C_GV:C_GV + 256]), _chunks(kd))]
    ut_s = [_halves(_dot_tn, b, x) for b, x in zip(_chunks(xc[:, 256:512]), _chunks(xw))]
    return tail, pre, dtin, dte, dec, kd, wdec, w, xw, d_s, et, ut_g, ut_s


def _rmsproj(x, nw, wp, name, tm=512, rider=None):
    t = x.shape[0]

    def body(x_ref, nw_ref, w_ref, o_ref, h_ref):
        xv = x_ref[...]
        rs = lax.rsqrt(jnp.mean(xv * xv, axis=-1, keepdims=True) + EPS)
        h = (xv * rs * nw_ref[...]).astype(BF16)
        h_ref[...] = h
        o_ref[...] = jnp.dot(h, w_ref[...], preferred_element_type=F32)

    (proj, h), extra = _call(
        body, (x, nw, wp), grid=(t // tm,), name=name, sem=("parallel",), rider=rider,
        in_specs=[pl.BlockSpec((tm, D), lambda i: (i, 0)), pl.BlockSpec((1, D), lambda i: (0, 0)),
                  pl.BlockSpec((D, NP), lambda i: (0, 0))],
        out_specs=[pl.BlockSpec((tm, NP), lambda i: (i, 0)), pl.BlockSpec((tm, D), lambda i: (i, 0))],
        out_shape=[jax.ShapeDtypeStruct((t, NP), F32), jax.ShapeDtypeStruct((t, D), BF16)])
    return proj, h, extra


def _head(x, tgt, fw, name, tm=512):
    t = x.shape[0]

    def body(x_ref, t_ref, w_ref, dx_ref, acc_ref):
        @pl.when(pl.program_id(0) == 0)
        def _():
            acc_ref[...] = jnp.zeros_like(acc_ref)

        xv = x_ref[...]
        w = w_ref[...]
        rs = lax.rsqrt(jnp.mean(xv * xv, axis=-1, keepdims=True) + EPS)
        xh = xv * rs
        err = xh * w - t_ref[...]
        dy = err * (1.0 / D)
        dxh = dy * w
        dx_ref[...] = rs * (dxh - xh * jnp.mean(dxh * xh, axis=-1, keepdims=True))
        acc_ref[0:1, :] += _cs(dy * xh)
        acc_ref[1:2, :] += jnp.zeros((1, D), F32) + (0.5 / D) * jnp.sum(err * err)

    return pl.pallas_call(
        body, grid=(t // tm,), name=name,
        in_specs=[pl.BlockSpec((tm, D), lambda i: (i, 0)), pl.BlockSpec((tm, D), lambda i: (i, 0)),
                  pl.BlockSpec((1, D), lambda i: (0, 0))],
        out_specs=[pl.BlockSpec((tm, D), lambda i: (i, 0)), pl.BlockSpec((8, D), lambda i: (0, 0))],
        out_shape=[jax.ShapeDtypeStruct((t, D), F32), jax.ShapeDtypeStruct((8, D), F32)],
        compiler_params=_cparams(("arbitrary",)),
    )(x, tgt, fw)


def _dxin(dp, wpt, x, dxn, nw, name, tm=512, rider=None):
    t = x.shape[0]

    def body(dp_ref, w_ref, x_ref, dxn_ref, nw_ref, dx_ref, dnw_ref):
        @pl.when(pl.program_id(0) == 0)
        def _():
            dnw_ref[...] = jnp.zeros_like(dnw_ref)

        dh = jnp.dot(dp_ref[...].astype(BF16), w_ref[...], preferred_element_type=F32)
        xv = x_ref[...]
        rs = lax.rsqrt(jnp.mean(xv * xv, axis=-1, keepdims=True) + EPS)
        xh = xv * rs
        dnw_ref[0:1, :] += _cs(dh * xh)
        dxh = dh * nw_ref[...]
        dx_ref[...] = dxn_ref[...] + rs * (dxh - xh * jnp.mean(dxh * xh, axis=-1, keepdims=True))

    return _call(
        body, (dp, wpt, x, dxn, nw), grid=(t // tm,), name=name, sem=("arbitrary",), rider=rider,
        in_specs=[pl.BlockSpec((tm, NP), lambda i: (i, 0)), pl.BlockSpec((NP, D), lambda i: (0, 0)),
                  pl.BlockSpec((tm, D), lambda i: (i, 0)), pl.BlockSpec((tm, D), lambda i: (i, 0)),
                  pl.BlockSpec((1, D), lambda i: (0, 0))],
        out_specs=[pl.BlockSpec((tm, D), lambda i: (i, 0)), pl.BlockSpec((8, D), lambda i: (0, 0))],
        out_shape=[jax.ShapeDtypeStruct((t, D), F32), jax.ShapeDtypeStruct((8, D), F32)])


def _dwin(h, dp, name, tm=512, tn=NP, rider=None):
    t = h.shape[0]

    def body(h_ref, dp_ref, o_ref):
        @pl.when(pl.program_id(1) == 0)
        def _():
            o_ref[...] = jnp.zeros_like(o_ref)

        o_ref[...] += _dot_tn(h_ref[...], dp_ref[...])

    (dwp,), extra = _call(
        body, (h, dp), grid=(NP // tn, t // tm), name=name, sem=("parallel", "arbitrary"), rider=rider,
        in_specs=[pl.BlockSpec((tm, D), lambda j, i: (i, 0)), pl.BlockSpec((tm, tn), lambda j, i: (i, j))],
        out_specs=[pl.BlockSpec((D, tn), lambda j, i: (0, j))], out_shape=[jax.ShapeDtypeStruct((D, NP), F32)])
    return dwp, extra


def _mixer_fwd(proj, x, wo, prm, gw, pw, cmat, mask, name, rider=None):
    t = proj.shape[0]
    nt, nc = t // TB, t // CH

    def body(p_ref, x_ref, wo_ref, prm_ref, gw_ref, pw_ref, cm_ref, mk_ref, mix_ref, sg_ref, ss_ref, xn_ref,
             sg_s, ss_s, h_ua, h_pu, h_sx):
        i = pl.program_id(0)

        @pl.when(i == 0)
        def _():
            for r in (sg_s, ss_s, h_ua, h_pu, h_sx):
                r[...] = jnp.zeros_like(r)

        lane = _iota((1, 256), 1)
        u = p_ref[:, C_AC:C_AC + 256] * p_ref[:, C_AH:C_AH + 256]
        ext = jnp.concatenate([h_ua[...], u], axis=0)
        cv = (prm_ref[R_CAW + 2:R_CAW + 3, 0:256] * u + prm_ref[R_CAW + 1:R_CAW + 2, 0:256] * _dn(ext, 1, TB, 8)
              + prm_ref[R_CAW:R_CAW + 1, 0:256] * _dn(ext, 2, TB, 8))
        mix_ref[:, 0:256] = (p_ref[:, C_AB:C_AB + 256] * cv * _silu(p_ref[:, C_AZ:C_AZ + 256])).astype(BF16)
        h_ua[...] = u[TB - 8:, :]
        pu = p_ref[:, C_PU:C_PU + 256]
        ext = jnp.concatenate([h_pu[...], pu], axis=0)
        pooled = _winsum_dn(ext, lane)[16:] * _pool_inv_count(i, TB) - pu
        mixed = _dot(pooled, pw_ref[...])
        mix_ref[:, 512:768] = (prm_ref[R_PSC:R_PSC + 1, 0:256] * mixed * _silu(p_ref[:, C_PZ:C_PZ + 256])).astype(BF16)
        h_pu[...] = pu[TB - 16:, :]
        sx = p_ref[:, C_SX:C_SX + 768]
        ext = jnp.concatenate([h_sx[...], sx], axis=0)
        xc = _silu(prm_ref[R_SCW + 3:R_SCW + 4, :] * sx + prm_ref[R_SCW + 2:R_SCW + 3, :] * _dn(ext, 1, TB, 8)
                   + prm_ref[R_SCW + 1:R_SCW + 2, :] * _dn(ext, 2, TB, 8) + prm_ref[R_SCW:R_SCW + 1, :] * _dn(ext, 3, TB, 8)
                   + prm_ref[R_SCB:R_SCB + 1, :])
        h_sx[...] = sx[TB - 8:, :]

        _, _, _, _, _, _, _, _, _, d_s, et, ut_g, ut_s = _mixer_tile_prep(p_ref, xc, prm_ref, gw_ref[...], cm_ref, mk_ref)
        s_g, s_s = sg_s[...], ss_s[...]
        o, y = [], []
        qs = _chunks(p_ref[:, C_GQ:C_GQ + 128] * GLA_SCALE)
        cm = _chunks(xc[:, 512:768])
        for c in range(NCH):
            sg_ref[c] = s_g
            ss_ref[c] = s_s
            s_g = s_g * d_s[c] + ut_g[c]
            s_s = s_s * et[c] + ut_s[c]
            o.append(_dot_nt(qs[c], s_g))
            y.append(_halves(_dot, cm[c], s_s))
        sg_s[...] = s_g
        ss_s[...] = s_s
        o = jnp.concatenate(o, axis=0)
        on = o * lax.rsqrt(_dot2_l(o * o, cm_ref[2]) + EPS)
        mix_ref[:, 256:512] = (on * prm_ref[R_GNW:R_GNW + 1, 0:256] * _silu(p_ref[:, C_GZ:C_GZ + 256])).astype(BF16)
        y2 = ((jnp.concatenate(y, axis=0) + prm_ref[R_DE:R_DE + 1, 0:256] * xc[:, 0:256])
              * _silu(p_ref[:, C_SZ:C_SZ + 256]))
        mix_ref[:, 768:1024] = (y2 * lax.rsqrt(jnp.mean(y2 * y2, axis=-1, keepdims=True) + EPS)
                                * prm_ref[R_SNW:R_SNW + 1, 0:256]).astype(BF16)
        xn_ref[...] = x_ref[...] + jnp.dot(mix_ref[...], wo_ref[...], preferred_element_type=F32)

    return _call(
        body, (proj, x, wo, prm, gw, pw, cmat, mask), grid=(nt,), name=name, sem=("arbitrary",), rider=rider,
        in_specs=[pl.BlockSpec((TB, NP), lambda i: (i, 0)), pl.BlockSpec((TB, D), lambda i: (i, 0)),
                  pl.BlockSpec((D, D), lambda i: (0, 0)), pl.BlockSpec((16, 768), lambda i: (0, 0)),
                  pl.BlockSpec((128, 128), lambda i: (0, 0)), pl.BlockSpec((256, 256), lambda i: (0, 0)),
                  pl.BlockSpec((4, 256, 256), lambda i: (0, 0, 0)), pl.BlockSpec((256, 128), lambda i: (0, 0))],
        out_specs=[pl.BlockSpec((TB, D), lambda i: (i, 0)), pl.BlockSpec((NCH, 256, 128), lambda i: (i, 0, 0)),
                   pl.BlockSpec((NCH, 128, 256), lambda i: (i, 0, 0)), pl.BlockSpec((TB, D), lambda i: (i, 0))],
        out_shape=[jax.ShapeDtypeStruct((t, D), BF16), jax.ShapeDtypeStruct((nc, 256, 128), F32),
                   jax.ShapeDtypeStruct((nc, 128, 256), F32), jax.ShapeDtypeStruct((t, D), F32)],
        scratch_shapes=[pltpu.VMEM((256, 128), F32), pltpu.VMEM((128, 256), F32), pltpu.VMEM((8, 256), F32),
                        pltpu.VMEM((16, 256), F32), pltpu.VMEM((8, 768), F32)])


def _mixer_bwd(proj, dxn, wot, mix, sg, ss, prm, gw, pw, cmat, mask, name, rider=None):
    t = proj.shape[0]
    nt = t // TB
    rev = lambda i: nt - 1 - i

    def body(p_ref, hp_ref, dxn_ref, wot_ref, mix_ref, sg_ref, ss_ref, prm_ref, gw_ref, pw_ref, cm_ref, mk_ref,
             dp_ref, sgc_ref, dwo_ref,
             gg_s, gs_s, h_dcv, h_dpl, h_dpre, gsm_ref, dgw_ref, dpw_ref, dm_ref):
        i = pl.program_id(0)
        tile = nt - 1 - i

        @pl.when(i == 0)
        def _():
            for r in (gg_s, gs_s, h_dcv, h_dpl, h_dpre, gsm_ref, dgw_ref, dpw_ref, dwo_ref):
                r[...] = jnp.zeros_like(r)

        dxn = dxn_ref[...].astype(BF16)
        dm_ref[...] = jnp.dot(dxn, wot_ref[...], preferred_element_type=F32)
        dwo_ref[...] += _dot_tn(mix_ref[...], dxn)

        lane = _iota((1, 256), 1)
        first = (tile > 0).astype(F32)
        ah, ac = p_ref[:, C_AH:C_AH + 256], p_ref[:, C_AC:C_AC + 256]
        ab, az = p_ref[:, C_AB:C_AB + 256], p_ref[:, C_AZ:C_AZ + 256]
        w0, w1, w2 = (prm_ref[R_CAW + j:R_CAW + j + 1, 0:256] for j in range(3))
        u = ac * ah
        ext = jnp.concatenate([hp_ref[8:16, C_AC:C_AC + 256] * hp_ref[8:16, C_AH:C_AH + 256] * first, u], axis=0)
        u1, u2 = _dn(ext, 1, TB, 8), _dn(ext, 2, TB, 8)
        cv = w2 * u + w1 * u1 + w0 * u2
        g = dm_ref[:, 0:256]
        sz, dsz = _silu_pair(az)
        dp_ref[:, C_AB:C_AB + 256] = (g * cv * sz).astype(BF16)
        dp_ref[:, C_AZ:C_AZ + 256] = (g * ab * cv * dsz).astype(BF16)
        dcv = g * ab * sz
        dext = jnp.concatenate([dcv, h_dcv[...]], axis=0)
        du = w2 * dcv + w1 * _up(dext, 1, TB) + w0 * _up(dext, 2, TB)
        dp_ref[:, C_AC:C_AC + 256] = (du * ah).astype(BF16)
        dp_ref[:, C_AH:C_AH + 256] = (du * ac).astype(BF16)
        gsm_ref[R_CAW:R_CAW + 1, 0:256] += _cs(dcv * u2)
        gsm_ref[R_CAW + 1:R_CAW + 2, 0:256] += _cs(dcv * u1)
        gsm_ref[R_CAW + 2:R_CAW + 3, 0:256] += _cs(dcv * u)
        h_dcv[...] = dcv[0:8, :]
        pu, pz = p_ref[:, C_PU:C_PU + 256], p_ref[:, C_PZ:C_PZ + 256]
        psc = prm_ref[R_PSC:R_PSC + 1, 0:256]
        icnt = _pool_inv_count(tile, TB)
        ext = jnp.concatenate([hp_ref[:, C_PU:C_PU + 256] * first, pu], axis=0)
        pooled = _winsum_dn(ext, lane)[16:] * icnt - pu
        pw_v = pw_ref[...]
        mixed = _dot(pooled, pw_v)
        g = dm_ref[:, 512:768]
        sz, dsz = _silu_pair(pz)
        gsm_ref[R_PSC:R_PSC + 1, 0:256] += _cs(g * mixed * sz)
        dp_ref[:, C_PZ:C_PZ + 256] = (g * psc * mixed * dsz).astype(BF16)
        dmixed = g * psc * sz
        dpw_ref[...] += _dot_tn(pooled, dmixed)
        dpooled = _dot_nt(dmixed, pw_v)
        qd = dpooled * icnt
        dext = jnp.concatenate([qd, h_dpl[...]], axis=0)
        dp_ref[:, C_PU:C_PU + 256] = (_winsum_up(dext, lane)[:TB] - dpooled).astype(BF16)
        h_dpl[...] = qd[0:16, :]
        sx = p_ref[:, C_SX:C_SX + 768]
        cw = [prm_ref[R_SCW + j:R_SCW + j + 1, :] for j in range(4)]
        ext = jnp.concatenate([hp_ref[8:16, C_SX:C_SX + 768] * first, sx], axis=0)
        sx1, sx2, sx3 = _dn(ext, 1, TB, 8), _dn(ext, 2, TB, 8), _dn(ext, 3, TB, 8)
        cpre = cw[3] * sx + cw[2] * sx1 + cw[1] * sx2 + cw[0] * sx3 + prm_ref[R_SCB:R_SCB + 1, :]
        xc, dxc = _silu_pair(cpre)
        xs, bm, cm = xc[:, 0:256], xc[:, 256:512], xc[:, 512:768]

        gw_v = gw_ref[...]
        tail, pre, dtin, dte, dec, kd, wdec, w, xw, d_s, et, ut_g, ut_s = _mixer_tile_prep(p_ref, xc, prm_ref, gw_v,
                                                                                          cm_ref, mk_ref)
        gmean = cm_ref[2]
        mask_t = mk_ref[...]
        gnw = prm_ref[R_GNW:R_GNW + 1, 0:256]
        a_e = prm_ref[R_AE:R_AE + 1, 0:256]
        d_e = prm_ref[R_DE:R_DE + 1, 0:256]
        snw = prm_ref[R_SNW:R_SNW + 1, 0:256]
        sg_in = [sg_ref[c] for c in range(NCH)]
        ss_in = [ss_ref[c] for c in range(NCH)]
        sg_n = [sg_in[c] * d_s[c] + ut_g[c] for c in range(NCH)]
        ss_n = [ss_in[c] * et[c] + ut_s[c] for c in range(NCH)]
        qs = _chunks(p_ref[:, C_GQ:C_GQ + 128] * GLA_SCALE)
        cm_c, bm_c, xw_c, kd_c = _chunks(cm), _chunks(bm), _chunks(xw), _chunks(kd)
        v_c = _chunks(p_ref[:, C_GV:C_GV + 256])
        o = jnp.concatenate([_dot_nt(qs[c], sg_n[c]) for c in range(NCH)], axis=0)
        y = jnp.concatenate([_halves(_dot, cm_c[c], ss_n[c]) for c in range(NCH)], axis=0) + d_e * xs
        gz = p_ref[:, C_GZ:C_GZ + 256]
        r = lax.rsqrt(_dot2_l(o * o, gmean) + EPS)
        on = o * r
        dyb = dm_ref[:, 256:512]
        sz, dsz = _silu_pair(gz)
        dp_ref[:, C_GZ:C_GZ + 256] = (dyb * on * gnw * dsz).astype(BF16)
        tg = dyb * sz
        gsm_ref[R_GNW:R_GNW + 1, 0:256] += _cs(tg * on)
        don = tg * gnw
        do_c = _chunks(r * (don - on * _dot2_l(don * on, gmean)))
        ssz = p_ref[:, C_SZ:C_SZ + 256]
        sil, dsil = _silu_pair(ssz)
        y2 = y * sil
        r = lax.rsqrt(jnp.mean(y2 * y2, axis=-1, keepdims=True) + EPS)
        yn = y2 * r
        dyd = dm_ref[:, 768:1024]
        gsm_ref[R_SNW:R_SNW + 1, 0:256] += _cs(dyd * yn)
        dn = dyd * snw
        dy2 = r * (dn - yn * jnp.mean(dn * yn, axis=-1, keepdims=True))
        dp_ref[:, C_SZ:C_SZ + 256] = (dy2 * y * dsil).astype(BF16)
        dy = dy2 * sil
        gsm_ref[R_DE:R_DE + 1, 0:256] += _cs(dy * xs)
        dy_c = _chunks(dy)
        dq = jnp.concatenate([_dot(do_c[c], sg_n[c]) for c in range(NCH)], axis=0)
        dp_ref[:, C_GQ:C_GQ + 128] = (dq * GLA_SCALE).astype(BF16)
        dcm = jnp.concatenate([_halves(_dot_nt, dy_c[c], ss_n[c]) for c in range(NCH)], axis=0)
        gg = [_dot_tn(do_c[c], qs[c]) * mask_t for c in range(NCH)]
        gs = [_halves(_dot_tn, cm_c[c], dy_c[c]) for c in range(NCH)]
        car_g, car_s = gg_s[...], gs_s[...]
        for c in reversed(range(NCH)):
            gg[c] = gg[c] + car_g
            gs[c] = gs[c] + car_s
            car_g = gg[c] * d_s[c]
            car_s = gs[c] * et[c]
        gg_s[...] = car_g
        gs_s[...] = car_s
        dkd = jnp.concatenate([_dot(v_c[c], gg[c]) for c in range(NCH)], axis=0)
        dp_ref[:, C_GV:C_GV + 256] = jnp.concatenate([_dot_nt(kd_c[c], gg[c]) for c in range(NCH)], axis=0).astype(BF16)
        dp_ref[:, C_GK:C_GK + 128] = (dkd * dec).astype(BF16)
        dbm = jnp.concatenate([_halves(_dot_nt, xw_c[c], gs[c]) for c in range(NCH)], axis=0)
        dxw = jnp.concatenate([_halves(_dot, bm_c[c], gs[c]) for c in range(NCH)], axis=0)
        dxs = dy * d_e + dxw * w
        dw = dxw * xs
        dsuf = _dot2_r(cm_ref[1], jnp.concatenate([dkd * kd, dw * dte * wdec], axis=1))
        tot_g = jnp.concatenate([jnp.broadcast_to(_cs(gg[c] * sg_in[c]) * d_s[c], (CH, 128)) for c in range(NCH)], axis=0)
        tot_s = jnp.concatenate([jnp.broadcast_to(_cs(gs[c] * ss_in[c]) * et[c], (CH, 256)) for c in range(NCH)], axis=0)
        dpre = (dsuf[:, 0:128] + tot_g) * INV_TAU * jax.nn.sigmoid(-pre)
        dgw_ref[...] += _dot_tn(tail, dpre)
        gsm_ref[R_GB:R_GB + 1, 0:128] += _cs(dpre)
        dda = dsuf[:, 128:384] + tot_s
        gsm_ref[R_AE:R_AE + 1, 0:256] += _cs(dda * dte)
        dtail_s = _dot2_nt(dw * wdec + dda * a_e, cm_ref[3, 0:128, :]) * jax.nn.sigmoid(dtin)
        gsm_ref[R_DTB:R_DTB + 1, 0:128] += _cs(dtail_s)
        dp_ref[:, C_TL:C_TL + 128] = (_dot_nt(dpre, gw_v) + dtail_s).astype(BF16)
        dpre_c = jnp.concatenate([dxs, dbm, dcm], axis=1) * dxc
        dext = jnp.concatenate([dpre_c, h_dpre[...]], axis=0)
        dp_ref[:, C_SX:C_SX + 768] = (cw[3] * dpre_c + cw[2] * _up(dext, 1, TB) + cw[1] * _up(dext, 2, TB)
                                      + cw[0] * _up(dext, 3, TB)).astype(BF16)
        gsm_ref[R_SCW + 3:R_SCW + 4, :] += _cs(dpre_c * sx)
        gsm_ref[R_SCW + 2:R_SCW + 3, :] += _cs(dpre_c * sx1)
        gsm_ref[R_SCW + 1:R_SCW + 2, :] += _cs(dpre_c * sx2)
        gsm_ref[R_SCW:R_SCW + 1, :] += _cs(dpre_c * sx3)
        gsm_ref[R_SCB:R_SCB + 1, :] += _cs(dpre_c)
        h_dpre[...] = dpre_c[0:8, :]

        @pl.when(i == nt - 1)
        def _():
            ri, ci = _iota((256, 256), 0), _iota((256, 256), 1)
            per_head = jnp.where((ri >> 6) == ci, 1.0, 0.0).astype(BF16)
            per_dv = jnp.where((ri & 63) == ci, 1.0, 0.0).astype(BF16)
            row = _iota((8, 256), 0)
            top = gsm_ref[0:8, 0:256]
            sgc_ref[0:8, 0:256] = jnp.where(row == R_GNW, _dot3_l(top, per_dv), top)
            bot = gsm_ref[8:16, 0:256]
            fold = _dot3_l(jnp.where(row == R_AE - 8, bot * a_e, bot), per_head)
            sgc_ref[8:16, 0:256] = jnp.where((row == R_AE - 8) | (row == R_DE - 8), fold, bot)
            sgc_ref[0:16, 256:768] = gsm_ref[:, 256:768]
            sgc_ref[0:16, 768:896] = dgw_ref[0:16, :]
            sgc_ref[0:16, 896:1024] = jnp.zeros((16, 128), F32)
            diag = _pool_lane_select(lane, dpw_ref[0:64, :], dpw_ref[64:128, :], dpw_ref[128:192, :], dpw_ref[192:256, :])
            for q in range(4):
                sgc_ref[16:32, 256 * q:256 * q + 256] = diag[16 * q:16 * q + 16, :]

    return _call(
        body, (proj, proj, dxn, wot, mix, sg, ss, prm, gw, pw, cmat, mask), grid=(nt,), name=name, sem=("arbitrary",),
        rider=rider,
        in_specs=[pl.BlockSpec((TB, NP), lambda i: (rev(i), 0)),
                  pl.BlockSpec((16, HALO_W), lambda i: (jnp.maximum(rev(i) * (TB // 16) - 1, 0), 0)),
                  pl.BlockSpec((TB, D), lambda i: (rev(i), 0)), pl.BlockSpec((D, D), lambda i: (0, 0)),
                  pl.BlockSpec((TB, D), lambda i: (rev(i), 0)),
                  pl.BlockSpec((NCH, 256, 128), lambda i: (rev(i), 0, 0)),
                  pl.BlockSpec((NCH, 128, 256), lambda i: (rev(i), 0, 0)),
                  pl.BlockSpec((16, 768), lambda i: (0, 0)), pl.BlockSpec((128, 128), lambda i: (0, 0)),
                  pl.BlockSpec((256, 256), lambda i: (0, 0)), pl.BlockSpec((4, 256, 256), lambda i: (0, 0, 0)),
                  pl.BlockSpec((256, 128), lambda i: (0, 0))],
        out_specs=[pl.BlockSpec((TB, NP), lambda i: (rev(i), 0)), pl.BlockSpec((32, 1024), lambda i: (0, 0)),
                   pl.BlockSpec((D, D), lambda i: (0, 0))],
        out_shape=[jax.ShapeDtypeStruct((t, NP), BF16), jax.ShapeDtypeStruct((32, 1024), F32),
                   jax.ShapeDtypeStruct((D, D), F32)],
        scratch_shapes=[pltpu.VMEM((256, 128), F32), pltpu.VMEM((128, 256), F32), pltpu.VMEM((8, 256), F32),
                        pltpu.VMEM((16, 256), F32), pltpu.VMEM((8, 768), F32), pltpu.VMEM((16, 768), F32),
                        pltpu.VMEM((128, 128), F32), pltpu.VMEM((256, 256), F32), pltpu.VMEM((TB, D), F32)])


def _half(c, n):
    return pl.ds(pl.multiple_of(c * (n // 2), n // 2), n // 2)


def _other_chips(x, y):
    return ((1 - x, y), (x, 1 - y), (1 - x, 1 - y))


def _remote(src, dst, send, recv, k, dev):
    return pltpu.make_async_remote_copy(src_ref=src, dst_ref=dst, send_sem=send.at[k], recv_sem=recv.at[k], device_id=dev,
                                        device_id_type=MESH)


def _sem(n):
    return pltpu.SemaphoreType.DMA((n,))


def _rider_gather_ici(shards, extra=None):
    shards = tuple(shards) + ((extra,) if extra is not None else ())
    n = len(shards)

    def copies(rins, routs, sems, arrivals=True):
        send, recv, loc = sems
        x, y, c = _place()
        me = 2 * x + y
        own = [pltpu.make_async_copy(rins[k], routs[k].at[me], loc.at[k]) for k in range(n)]
        out, inc = [], []
        for j, (px, py) in enumerate(_other_chips(x, y)):
            for k in range(n):
                whole = extra is not None and k == n - 1
                rows = pl.ds(0, shards[k].shape[0]) if whole else _half(c, shards[k].shape[0])
                out.append(_remote(rins[k].at[rows], routs[k].at[me, rows], send, recv, n * j + k, (px, py, c)))
                if arrivals:
                    inc.append(_remote(rins[k].at[rows], routs[k].at[2 * px + py, rows], send, recv, n * j + k, (px, py, c)))
        return own, out, inc

    def start(rins, routs, sems):
        own, out, _ = copies(rins, routs, sems, arrivals=False)
        for cp in own + out:
            cp.start()

    def finish(rins, routs, sems):
        own, out, inc = copies(rins, routs, sems)
        for cp in inc:
            cp.wait_recv()
        for cp in out:
            cp.wait_send()
        for cp in own:
            cp.wait()

    return _Rider(shards, [jax.ShapeDtypeStruct((4,) + a.shape, a.dtype) for a in shards],
                  [_sem(3 * n), _sem(3 * n), _sem(n)], start, finish)


def _rider_gather_d2d(slabs):
    slabs = tuple(slabs)
    n = len(slabs)

    def copies(routs, sems, arrivals=True):
        send, recv = sems
        x, y, c = _place()
        out, inc = [], []
        for j, (px, py) in enumerate(_other_chips(x, y)):
            for k in range(n):
                rows = slabs[k].shape[1]
                mine, theirs = routs[k].at[2 * px + py, _half(c, rows)], routs[k].at[2 * px + py, _half(1 - c, rows)]
                out.append(_remote(mine, mine, send, recv, n * j + k, (x, y, 1 - c)))
                if arrivals:
                    inc.append(_remote(theirs, theirs, send, recv, n * j + k, (x, y, 1 - c)))
        return out, inc

    def start(rins, routs, sems):
        for cp in copies(routs, sems, arrivals=False)[0]:
            cp.start()

    def finish(rins, routs, sems):
        out, inc = copies(routs, sems)
        for cp in inc:
            cp.wait_recv()
        for cp in out:
            cp.wait_send()

    return _Rider(slabs, [jax.ShapeDtypeStruct(a.shape, a.dtype) for a in slabs], [_sem(3 * n), _sem(3 * n)], start, finish,
                  aliases={k: k for k in range(n)})


def _rider_swap(parts):
    parts = tuple(parts)
    n = len(parts)

    def copies(rins, routs, sems):
        send, recv = sems
        x, y, c = _place()
        return [_remote(rins[k].at[:, _half(1 - c, parts[k].shape[1])], routs[k], send, recv, k, (x, y, 1 - c))
                for k in range(n)]

    def start(rins, routs, sems):
        for cp in copies(rins, routs, sems):
            cp.start()

    def finish(rins, routs, sems):
        for cp in copies(rins, routs, sems):
            cp.wait()

    return _Rider(parts, [jax.ShapeDtypeStruct((4, a.shape[1] // 2, a.shape[2]), a.dtype) for a in parts],
                  [_sem(n), _sem(n)], start, finish)


def _rider_scatter(parts):
    parts = tuple(parts)
    n = len(parts)

    def copies(rins, routs, sems, arrivals=True):
        send, recv, loc = sems
        x, y, c = _place()
        me = 2 * x + y
        own = [pltpu.make_async_copy(rins[k].at[me], routs[k].at[me], loc.at[k]) for k in range(n)]
        out, inc = [], []
        for j, (px, py) in enumerate(_other_chips(x, y)):
            for k in range(n):
                out.append(_remote(rins[k].at[2 * px + py], routs[k].at[me], send, recv, n * j + k, (px, py, c)))
                if arrivals:
                    inc.append(_remote(rins[k].at[me], routs[k].at[2 * px + py], send, recv, n * j + k, (px, py, c)))
        return own, out, inc

    def start(rins, routs, sems):
        own, out, _ = copies(rins, routs, sems, arrivals=False)
        for cp in own + out:
            cp.start()

    def finish(rins, routs, sems):
        own, out, inc = copies(rins, routs, sems)
        for cp in inc:
            cp.wait_recv()
        for cp in out:
            cp.wait_send()
        for cp in own:
            cp.wait()

    return _Rider(parts, [jax.ShapeDtypeStruct(a.shape, a.dtype) for a in parts], [_sem(3 * n), _sem(3 * n), _sem(n)],
                  start, finish)


def _rider_share(halves, layer, prev=None):
    halves = tuple(halves)
    n = len(halves)
    inputs = halves + (tuple(prev) if prev is not None else ())

    def copies(rins, routs, sems, arrivals=True):
        send, recv, loc = sems
        x, y, c = _place()
        own, out, inc = [], [], []
        for k in range(n):
            rows = 2 * halves[k].shape[0]
            own.append(pltpu.make_async_copy(rins[k], routs[k].at[layer, _half(c, rows)], loc.at[k]))
            out.append(_remote(rins[k], routs[k].at[layer, _half(c, rows)], send, recv, k, (x, y, 1 - c)))
            if arrivals:
                inc.append(_remote(rins[k], routs[k].at[layer, _half(1 - c, rows)], send, recv, k, (x, y, 1 - c)))
        return own, out, inc

    def start(rins, routs, sems):
        own, out, _ = copies(rins, routs, sems, arrivals=False)
        for cp in own + out:
            cp.start()

    def finish(rins, routs, sems):
        own, out, inc = copies(rins, routs, sems)
        for cp in inc:
            cp.wait_recv()
        for cp in out:
            cp.wait_send()
        for cp in own:
            cp.wait()

    return _Rider(inputs, [jax.ShapeDtypeStruct((2, 2 * a.shape[0], a.shape[1]), a.dtype) for a in halves],
                  [_sem(n), _sem(n), _sem(n)], start, finish,
                  aliases={n + k: k for k in range(n)} if prev is not None else None)


def _pair_sum(core, full, recv, name, br=128):
    n, rows, cols = recv.shape

    def body(c_ref, a_ref, b_ref, o_ref):
        o_ref[...] = (a_ref[...] + b_ref[...]).astype(BF16)

    nb = rows // br
    return pl.pallas_call(
        body, name=name, out_shape=jax.ShapeDtypeStruct(recv.shape, BF16),
        grid_spec=pltpu.PrefetchScalarGridSpec(
            num_scalar_prefetch=1, grid=(n, nb),
            in_specs=[pl.BlockSpec((1, br, cols), lambda i, j, c: (i, c[0] * nb + j, 0)),
                      pl.BlockSpec((1, br, cols), lambda i, j, c: (i, j, 0))],
            out_specs=pl.BlockSpec((1, br, cols), lambda i, j, c: (i, j, 0))),
        compiler_params=_cparams(("parallel", "parallel")))(core, full, recv)


def _sum4(a, name, br=128):
    _, r, c = a.shape

    def body(a_ref, o_ref):
        o_ref[...] = ((a_ref[0].astype(F32) + a_ref[1].astype(F32)) + a_ref[2].astype(F32)) + a_ref[3].astype(F32)

    return pl.pallas_call(body, grid=(r // br,), name=name,
                          in_specs=[pl.BlockSpec((4, br, c), lambda i: (0, i, 0))],
                          out_specs=pl.BlockSpec((br, c), lambda i: (i, 0)),
                          out_shape=jax.ShapeDtypeStruct((r, c), F32),
                          compiler_params=_cparams(("parallel",)))(a)


def _adamw(w, g, m, v, name, br):
    n, r, c = w.shape

    def body(w_ref, g_ref, m_ref, v_ref, d_ref, m2_ref, v2_ref):
        d_ref[...], m2_ref[...], v2_ref[...] = _adam_math(w_ref[...], g_ref[...], m_ref[...], v_ref[...])

    spec = pl.BlockSpec((1, br, c), lambda i, j: (i, j, 0))
    shp = jax.ShapeDtypeStruct(w.shape, F32)
    return pl.pallas_call(body, grid=(n, r // br), name=name, in_specs=[spec] * 4, out_specs=[spec] * 3,
                          out_shape=[shp] * 3, compiler_params=_cparams(("parallel", "parallel")))(w, g, m, v)


def _adamw_w_in(w, g, m, v, name, bc=31):
    cols = w.shape[2]
    lead = lambda a: jnp.transpose(a, (2, 0, 1))

    def body(w_ref, g_ref, m_ref, v_ref, go_ref, d_ref, m2_ref, v2_ref):
        for l in range(2):
            gv = g_ref[:, l, :]
            d_ref[:, l, :], m2_ref[:, l, :], v2_ref[:, l, :] = _adam_math(w_ref[:, l, :], gv, m_ref[:, l, :], v_ref[:, l, :])
            go_ref[:, l, :] = gv

    spec = pl.BlockSpec((bc, 2, D), lambda i: (i, 0, 0))
    outs = pl.pallas_call(body, grid=(cols // bc,), name=name, in_specs=[spec] * 4, out_specs=[spec] * 4,
                          out_shape=[jax.ShapeDtypeStruct((cols, 2, D), F32)] * 4,
                          compiler_params=_cparams(("parallel",)))(lead(w), lead(g), lead(m), lead(v))
    return [jnp.transpose(o, (1, 2, 0)) for o in outs]


_SMALL_NAMES = ("norm_w", "conv_a_w", "gla_gate_w", "gla_gate_b", "gla_norm_w", "pool_w", "pool_scale", "ssd_conv_w",
                "ssd_conv_b", "ssd_dt_bias", "ssd_a_log", "ssd_d", "ssd_norm_w", "final_norm_w")
SMALL_ROWS = 72


def _adam_math(w, g, m, v):
    m2 = ADAM_B1 * m + (1.0 - ADAM_B1) * g
    v2 = ADAM_B2 * v + (1.0 - ADAM_B2) * (g * g)
    m_hat = m2 / (1.0 - ADAM_B1 ** ADAM_STEP)
    v_hat = v2 / (1.0 - ADAM_B2 ** ADAM_STEP)
    return -ADAM_LR * (m_hat / (jnp.sqrt(v_hat) + ADAM_EPS) + ADAM_WD * w), m2, v2


def _small_slices(name, chip):
    if name == "conv_a_w":
        return [((), slice(R_CAW, R_CAW + 3), slice(64 * chip, 64 * chip + 64))]
    if name == "ssd_conv_w":
        return [((), slice(R_SCW, R_SCW + 4), slice(192 * chip, 192 * chip + 192))]
    if name == "gla_gate_w":
        return [((), slice(0, 16), slice(768, 896))]
    if name == "pool_w":
        return [((g, slice(16 * q, 16 * q + 16)), slice(16, 32), slice(256 * q + 64 * g, 256 * q + 64 * g + 64))
                for g in range(4) for q in range(4)]
    row, lanes = {"gla_gate_b": (R_GB, slice(0, 128)), "gla_norm_w": (R_GNW, slice(0, 64)),
                  "pool_scale": (R_PSC, slice(0, 256)), "ssd_conv_b": (R_SCB, slice(0, 768)),
                  "ssd_dt_bias": (R_DTB, slice(16, 20)), "ssd_a_log": (R_AE, slice(0, 4)), "ssd_d": (R_DE, slice(0, 4)),
                  "ssd_norm_w": (R_SNW, slice(0, 256))}[name]
    return [((), slice(row, row + 1), lanes)]


def _small_allreduce(sg0, sg1, dnw0, dnw1, head):
    def body(sg0_ref, sg1_ref, dnw0_ref, dnw1_ref, head_ref, acc, stage, pair, rbuf, send_sems, recv_sems):
        x, y, c = _place()
        chip = 2 * x + y
        stage[0:32, :] = sg0_ref[...]
        stage[32:64, :] = sg1_ref[...]
        stage[64:65, :] = dnw0_ref[0:1, :]
        stage[65:66, :] = dnw1_ref[0:1, :]
        stage[66:68, :] = head_ref[0:2, :]
        stage[68:72, :] = jnp.zeros((4, D), F32)
        sib = _remote(stage, pair, send_sems, recv_sems, 0, (x, y, 1 - c))
        sib.start()
        sib.wait()
        rbuf[0] = stage[...] + pair[...]
        sends = [_remote(rbuf.at[0], rbuf.at[k], send_sems, recv_sems, k, (px, py, c))
                 for k, (px, py) in enumerate(_other_chips(x, y), start=1)]
        for cp in sends:
            cp.start()
        for cp in sends:
            cp.wait()
        slab = lambda d: jnp.where(d == 0, 0, jnp.where(d == 2, 1, jnp.where(d == 1, 2, 3)))
        total = rbuf[slab(jnp.bitwise_xor(chip, 0))]
        for s in range(1, 4):
            total = total + rbuf[slab(jnp.bitwise_xor(chip, s))]
        acc[...] = total

    vmem = pl.BlockSpec(memory_space=pltpu.VMEM)
    return pl.pallas_call(
        body, name="small_allreduce", in_specs=[vmem] * 5, out_specs=vmem,
        out_shape=jax.ShapeDtypeStruct((SMALL_ROWS, D), F32),
        scratch_shapes=[pltpu.VMEM((SMALL_ROWS, D), F32), pltpu.VMEM((SMALL_ROWS, D), F32),
                        pltpu.VMEM((4, SMALL_ROWS, D), F32), _sem(4), _sem(4)],
    )(sg0, sg1, dnw0, dnw1, head)


def _small_adamw(acc, w, m, v):
    n = len(_SMALL_NAMES)

    def body(*refs):
        acc = refs[0]
        w_refs, m_refs, v_refs = refs[1:1 + n], refs[1 + n:1 + 2 * n], refs[1 + 2 * n:1 + 3 * n]
        o = 1 + 3 * n
        g_out, d_out, m_out, v_out = refs[o:o + n], refs[o + n:o + 2 * n], refs[o + 2 * n:o + 3 * n], refs[o + 3 * n:o + 4 * n]
        loss_ref = refs[o + 4 * n]
        chip = 2 * lax.axis_index("x") + lax.axis_index("y")
        loss_ref[...] = acc[67:68, 0:1]

        def update(i, idx, g):
            d, m2, v2 = _adam_math(w_refs[i][idx], g, m_refs[i][idx], v_refs[i][idx])
            g_out[i][idx], d_out[i][idx], m_out[i][idx], v_out[i][idx] = g, d, m2, v2

        for i, name in enumerate(_SMALL_NAMES):
            if name == "final_norm_w":
                update(i, (slice(0, 1), slice(None)), acc[66:67, :])
            elif name == "norm_w":
                for l in range(2):
                    update(i, (slice(l, l + 1), slice(None)), acc[64 + l:65 + l, :])
            elif name in ("conv_a_w", "ssd_conv_w"):
                for s in range(4):
                    @pl.when(chip == s)
                    def _(i=i, name=name, s=s):
                        for l in range(2):
                            (_, rows, lanes), = _small_slices(name, s)
                            update(i, (l,), acc[rows.start + 32 * l:rows.stop + 32 * l, lanes])
            else:
                for l in range(2):
                    for idx, rows, lanes in _small_slices(name, 0):
                        g = acc[rows.start + 32 * l:rows.stop + 32 * l, lanes]
                        if w_refs[i].ndim == 2:
                            update(i, (slice(l, l + 1), slice(None)), g)
                        else:
                            update(i, (l,) + idx, g)

    args = [acc] + [d[k] for d in (w, m, v) for k in _SMALL_NAMES]
    shapes = [jax.ShapeDtypeStruct(w[k].shape, F32) for k in _SMALL_NAMES]
    vmem = pl.BlockSpec(memory_space=pltpu.VMEM)
    outs = pl.pallas_call(body, name="small_adamw", in_specs=[vmem] * len(args), out_specs=[vmem] * (4 * n + 1),
                          out_shape=shapes * 4 + [jax.ShapeDtypeStruct((1, 1), F32)])(*args)
    return outs[0:n], outs[n:2 * n], outs[2 * n:3 * n], outs[3 * n:4 * n], outs[4 * n]


def _permute_cols(w):
    parts = [w[..., s:s + n] for s, n in _PERM]
    parts.append(jnp.zeros(w.shape[:-1] + (NP - NPROJ,), w.dtype))
    return jnp.concatenate(parts, axis=-1)


def _unpermute_cols(w):
    return jnp.concatenate([w[..., s:s + n] for s, n in _UNPERM], axis=-1)


def _mixer_consts(layer, conv_a_w, gla_gate_w, gla_gate_b, gla_norm_w, pool_w, pool_scale, ssd_conv_w, ssd_conv_b,
                  ssd_dt_bias, ssd_a_log, ssd_d, ssd_norm_w):
    def row(v):
        return jnp.pad(v.reshape(1, -1), ((0, 0), (0, 768 - v.size)))

    dtb = jnp.zeros((128,), F32).at[16:20].set(ssd_dt_bias[layer])
    rows = [jnp.pad(conv_a_w[layer], ((0, 0), (0, 512))), row(gla_gate_b[layer]), row(jnp.tile(gla_norm_w[layer], 4)),
            row(pool_scale[layer]), row(ssd_conv_b[layer]), row(dtb), row(jnp.repeat(-jnp.exp(ssd_a_log[layer]), 64)),
            row(jnp.repeat(ssd_d[layer], 64)), row(ssd_norm_w[layer]), jnp.zeros((1, 768), F32), ssd_conv_w[layer]]
    prm = jnp.concatenate(rows, axis=0)
    gw = jnp.zeros((128, 128), F32).at[0:16].set(gla_gate_w[layer]).astype(BF16)
    pw = jnp.zeros((256, 256), F32)
    for g in range(4):
        pw = pw.at[64 * g:64 * g + 64, 64 * g:64 * g + 64].set(pool_w[layer, g])
    return (prm, gw, pw.astype(BF16)) + _mixer_matrices()


def _layer_weights(s_in, s_out):
    wp = _permute_cols(jnp.transpose(s_in, (1, 0, 2)).reshape(D, NPROJ))
    wo = s_out.reshape(D, D)
    return wp, wp.T, wo, wo.T


def _grad_slabs(dwp, dwo):
    return jnp.transpose(_unpermute_cols(dwp).reshape(D, 4, NPROJ // 4), (1, 0, 2)), dwo.reshape(4, D // 4, D)


class _Comm:
    def __init__(self, w_in16, w_out16):
        self.w_in16, self.w_out16 = w_in16, w_out16
        self.core = lax.axis_index("c").astype(jnp.int32).reshape(1)

    def gather_ici(self, layer, extra=None):
        return _rider_gather_ici((self.w_in16[layer], self.w_out16[layer]), extra)

    def pair_sum(self, layer, slabs, received):
        return [_pair_sum(self.core, a, b, name=f"reduce_pair_sum{layer}_{k}") for k, (a, b) in enumerate(zip(slabs, received))]

    def chip_sum(self, layer, gathered):
        return [_sum4(a, name=f"reduce_chip_sum{layer}_{k}") for k, a in enumerate(gathered)]


def _local_step(x, tgt, norm_w, final_norm_w, consts, wts0, wts1=None, comm=None):
    nw = [norm_w[l:l + 1] for l in range(2)]
    proj0, h0, slabs = _rmsproj(x, nw[0], wts0[0], name="rmsproj0", rider=comm and comm.gather_ici(1))
    (mix0, sg0, ss0, x1), slabs = _mixer_fwd(proj0, x, wts0[2], *consts[0], name="mixer_fwd0",
                                             rider=comm and _rider_gather_d2d(slabs))
    if comm:
        wts1 = _layer_weights(*slabs)
    proj1, h1, _ = _rmsproj(x1, nw[1], wts1[0], name="rmsproj1")
    (mix1, sg1, ss1, x2), _ = _mixer_fwd(proj1, x1, wts1[2], *consts[1], name="mixer_fwd1")
    dx, head = _head(x2, tgt, final_norm_w.reshape(1, D), name="loss_head")
    (dproj, mgr1, dwo1), _ = _mixer_bwd(proj1, dx, wts1[3], mix1, sg1, ss1, *consts[1], name="mixer_bwd1")
    dwp1, _ = _dwin(h1, dproj, name="dwin1")
    slabs1 = _grad_slabs(dwp1, dwo1)
    (dx, dnw1), recv = _dxin(dproj, wts1[1], x1, dx, nw[1], name="dxin1", rider=comm and _rider_swap(slabs1))
    scat = comm and _rider_scatter(comm.pair_sum(1, slabs1, recv))
    (dproj, mgr0, dwo0), gathered = _mixer_bwd(proj0, dx, wts0[3], mix0, sg0, ss0, *consts[0], name="mixer_bwd0", rider=scat)
    share = comm and _rider_share(comm.chip_sum(1, gathered), 1)
    dwp0, big = _dwin(h0, dproj, name="dwin0", rider=share)
    scat = None
    if comm:
        slabs0 = _grad_slabs(dwp0, dwo0)
        scat = _rider_scatter(comm.pair_sum(0, slabs0, _run_rider(_rider_swap(slabs0), "reduce_swap0")))
    (dx, dnw0), gathered = _dxin(dproj, wts0[1], x, dx, nw[0], name="dxin0", rider=scat)
    if comm:
        big = _run_rider(_rider_share(comm.chip_sum(0, gathered), 0, prev=big), "reduce_share0")
    else:
        big = (jnp.stack([dwp0, dwp1]), jnp.stack([dwo0, dwo1]))
    return head, dx, big, (dnw0, dnw1), (mgr0, mgr1)


def kernel(x, norm_w, w_in, conv_a_w, gla_gate_w, gla_gate_b, gla_norm_w, pool_w, pool_scale, ssd_conv_w, ssd_conv_b, ssd_dt_bias, ssd_a_log, ssd_d, ssd_norm_w, w_out, final_norm_w, loss_target, m_norm_w, m_w_in, m_conv_a_w, m_gla_gate_w, m_gla_gate_b, m_gla_norm_w, m_pool_w, m_pool_scale, m_ssd_conv_w, m_ssd_conv_b, m_ssd_dt_bias, m_ssd_a_log, m_ssd_d, m_ssd_norm_w, m_w_out, m_final_norm_w, v_norm_w, v_w_in, v_conv_a_w, v_gla_gate_w, v_gla_gate_b, v_gla_norm_w, v_pool_w, v_pool_scale, v_ssd_conv_w, v_ssd_conv_b, v_ssd_dt_bias, v_ssd_a_log, v_ssd_d, v_ssd_norm_w, v_w_out, v_final_norm_w):
    weights = dict(norm_w=norm_w, w_in=w_in, conv_a_w=conv_a_w, gla_gate_w=gla_gate_w, gla_gate_b=gla_gate_b,
                   gla_norm_w=gla_norm_w, pool_w=pool_w, pool_scale=pool_scale, ssd_conv_w=ssd_conv_w,
                   ssd_conv_b=ssd_conv_b, ssd_dt_bias=ssd_dt_bias, ssd_a_log=ssd_a_log, ssd_d=ssd_d,
                   ssd_norm_w=ssd_norm_w, w_out=w_out, final_norm_w=final_norm_w)
    m_in = dict(norm_w=m_norm_w, w_in=m_w_in, conv_a_w=m_conv_a_w, gla_gate_w=m_gla_gate_w, gla_gate_b=m_gla_gate_b,
                gla_norm_w=m_gla_norm_w, pool_w=m_pool_w, pool_scale=m_pool_scale, ssd_conv_w=m_ssd_conv_w,
                ssd_conv_b=m_ssd_conv_b, ssd_dt_bias=m_ssd_dt_bias, ssd_a_log=m_ssd_a_log, ssd_d=m_ssd_d,
                ssd_norm_w=m_ssd_norm_w, w_out=m_w_out, final_norm_w=m_final_norm_w)
    v_in = dict(norm_w=v_norm_w, w_in=v_w_in, conv_a_w=v_conv_a_w, gla_gate_w=v_gla_gate_w, gla_gate_b=v_gla_gate_b,
                gla_norm_w=v_gla_norm_w, pool_w=v_pool_w, pool_scale=v_pool_scale, ssd_conv_w=v_ssd_conv_w,
                ssd_conv_b=v_ssd_conv_b, ssd_dt_bias=v_ssd_dt_bias, ssd_a_log=v_ssd_a_log, ssd_d=v_ssd_d,
                ssd_norm_w=v_ssd_norm_w, w_out=v_w_out, final_norm_w=v_final_norm_w)
    order = ("norm_w", "w_in", "conv_a_w", "gla_gate_w", "gla_gate_b", "gla_norm_w", "pool_w", "pool_scale",
             "ssd_conv_w", "ssd_conv_b", "ssd_dt_bias", "ssd_a_log", "ssd_d", "ssd_norm_w", "w_out", "final_norm_w")
    t = x.shape[1]

    comm = _Comm(w_in.astype(BF16), w_out.astype(BF16))
    cshard = jnp.zeros((16, 256), F32)
    for l in range(2):
        cshard = cshard.at[8 * l:8 * l + 3, 0:64].set(conv_a_w[l]).at[8 * l + 3:8 * l + 7, 0:192].set(ssd_conv_w[l])
    s_in, s_out, g_c = _run_rider(comm.gather_ici(0, cshard), "gather_ici0")
    s_in, s_out = _run_rider(_rider_gather_d2d((s_in, s_out)), "gather_d2d0")
    conv_a_full = jnp.stack([jnp.concatenate([g_c[s, 8 * l:8 * l + 3, 0:64] for s in range(4)], axis=-1) for l in range(2)])
    ssd_conv_full = jnp.stack([jnp.concatenate([g_c[s, 8 * l + 3:8 * l + 7, 0:192] for s in range(4)], axis=-1)
                               for l in range(2)])
    consts = [_mixer_consts(l, conv_a_full, gla_gate_w, gla_gate_b, gla_norm_w, pool_w, pool_scale, ssd_conv_full,
                            ssd_conv_b, ssd_dt_bias, ssd_a_log, ssd_d, ssd_norm_w) for l in range(2)]

    head, dx, big, dnw, mgr = _local_step(x.reshape(t, D), loss_target.reshape(t, D), norm_w, final_norm_w, consts,
                                          _layer_weights(s_in, s_out), comm=comm)

    as2d = lambda d: {k: (d[k].reshape(1, D) if k == "final_norm_w" else d[k]) for k in _SMALL_NAMES}
    small = _small_adamw(_small_allreduce(mgr[0], mgr[1], dnw[0], dnw[1], head), as2d(weights), as2d(m_in), as2d(v_in))
    grads, delta, new_m, new_v = ({k: (a.reshape(D) if k == "final_norm_w" else a) for k, a in zip(_SMALL_NAMES, part)}
                                  for part in small[0:4])
    loss = small[4].reshape(())

    grads["w_out"] = big[1]

    grads["w_in"], delta["w_in"], new_m["w_in"], new_v["w_in"] = _adamw_w_in(w_in, big[0], m_w_in, v_w_in, name="adamw_w_in")
    delta["w_out"], new_m["w_out"], new_v["w_out"] = _adamw(w_out, big[1], m_w_out, v_w_out, name="adamw_w_out", br=256)

    return (loss, dx.reshape(1, t, D), *[grads[k] for k in order], *[delta[k] for k in order],
            *[new_m[k] for k in order], *[new_v[k] for k in order])
```

```python
import functools

import jax
import jax.numpy as jnp
from jax import lax
from jax.experimental import pallas as pl
from jax.experimental.pallas import tpu as pltpu

F32 = jnp.float32
BF16 = jnp.bfloat16
MESH = pl.DeviceIdType.MESH

D = 1024
CH = 64
EPS = 1e-6
NP = 3456
NPROJ = 3348
GLA_SCALE = 32.0 ** -0.5
INV_TAU = 1.0 / 16.0
TB = 256
NCH = TB // CH
assert TB == 256
HALO_W = NP

C_AH, C_AB, C_AC, C_AZ, C_GQ, C_GK, C_GV = 0, 256, 512, 768, 1024, 1152, 1280
C_GZ, C_PU, C_PZ, C_SZ, C_SX, C_TL = 1536, 1792, 2048, 2304, 2560, 3328
_PERM = ((0, 1536), (1552, 1792), (1536, 16), (3344, 4))
_UNPERM = ((0, 1536), (3328, 16), (1536, 1792), (3344, 4))

R_CAW, R_GB, R_GNW, R_PSC, R_SCB, R_DTB, R_AE, R_DE, R_SNW, R_SCW = 0, 3, 4, 5, 6, 7, 8, 9, 10, 12

ADAM_LR, ADAM_B1, ADAM_B2, ADAM_EPS, ADAM_WD, ADAM_STEP = 0.001, 0.9, 0.999, 1e-08, 0.01, 10

VMEM_LIMIT = 56 * 1024 * 1024


def _cparams(sem, limit=VMEM_LIMIT):
    return pltpu.CompilerParams(dimension_semantics=sem, vmem_limit_bytes=limit)


_ANY = pl.BlockSpec(memory_space=pl.ANY)


def _place():
    return lax.axis_index("x"), lax.axis_index("y"), lax.axis_index("c")


class _Rider:
    def __init__(self, inputs, out_shapes, sems, start, finish, aliases=None):
        self.inputs, self.out_shapes, self.sems = tuple(inputs), tuple(out_shapes), tuple(sems)
        self.start, self.finish, self.aliases = start, finish, dict(aliases or {})


def _call(body, args, *, grid, in_specs, out_specs, out_shape, name, sem, scratch_shapes=(), rider=None):
    if rider is None:
        outs = pl.pallas_call(body, grid=grid, name=name, in_specs=list(in_specs), out_specs=list(out_specs),
                              out_shape=list(out_shape), scratch_shapes=list(scratch_shapes),
                              compiler_params=_cparams(sem))(*args)
        return list(outs), []
    ni, no, ns = len(args), len(out_shape), len(scratch_shapes)
    ri, ro = len(rider.inputs), len(rider.out_shapes)

    def full(*refs):
        ins, rins = refs[:ni], refs[ni:ni + ri]
        outs, routs = refs[ni + ri:ni + ri + no], refs[ni + ri + no:ni + ri + no + ro]
        scr, rsem = refs[ni + ri + no + ro:ni + ri + no + ro + ns], refs[ni + ri + no + ro + ns:]
        first = functools.reduce(jnp.logical_and, [pl.program_id(a) == 0 for a in range(len(grid))])
        last = functools.reduce(jnp.logical_and, [pl.program_id(a) == grid[a] - 1 for a in range(len(grid))])

        @pl.when(first)
        def _():
            rider.start(rins, routs, rsem)

        body(*ins, *outs, *scr)

        @pl.when(last)
        def _():
            rider.finish(rins, routs, rsem)

    outs = pl.pallas_call(
        full, grid=grid, name=name, in_specs=list(in_specs) + [_ANY] * ri, out_specs=list(out_specs) + [_ANY] * ro,
        out_shape=list(out_shape) + list(rider.out_shapes), scratch_shapes=list(scratch_shapes) + list(rider.sems),
        input_output_aliases={ni + k: no + v for k, v in rider.aliases.items()},
        compiler_params=_cparams(("arbitrary",) * len(grid)))(*args, *rider.inputs)
    return list(outs[:no]), list(outs[no:])


def _run_rider(rider, name):
    ri = len(rider.inputs)

    def body(*refs):
        rins, routs, rsem = refs[:ri], refs[ri:ri + len(rider.out_shapes)], refs[ri + len(rider.out_shapes):]
        rider.start(rins, routs, rsem)
        rider.finish(rins, routs, rsem)

    return list(pl.pallas_call(body, name=name, in_specs=[_ANY] * ri, out_specs=[_ANY] * len(rider.out_shapes),
                               out_shape=list(rider.out_shapes), scratch_shapes=list(rider.sems),
                               input_output_aliases=dict(rider.aliases))(*rider.inputs))


def _dot(a, b):
    return jnp.dot(a.astype(BF16), b.astype(BF16), preferred_element_type=F32)


def _dot_nt(a, b):
    return lax.dot_general(a.astype(BF16), b.astype(BF16), (((1,), (1,)), ((), ())), preferred_element_type=F32)


def _dot_tn(a, b):
    return lax.dot_general(a.astype(BF16), b.astype(BF16), (((0,), (0,)), ((), ())), preferred_element_type=F32)


def _split(a):
    hi = a.astype(BF16)
    lo = (a - hi.astype(F32)).astype(BF16)
    return hi, lo


def _dot2_l(a, b):
    hi, lo = _split(a)
    return _dot(hi, b) + _dot(lo, b)


def _dot2_r(a, b):
    hi, lo = _split(b)
    return _dot(a, hi) + _dot(a, lo)


def _dot3_l(a, b):
    hi, lo = _split(a)
    lo2 = ((a - hi.astype(F32)) - lo.astype(F32)).astype(BF16)
    return _dot(hi, b) + _dot(lo, b) + _dot(lo2, b)


def _dot2_nt(a, b):
    hi, lo = _split(a)
    return _dot_nt(hi, b) + _dot_nt(lo, b)


def _silu(z):
    return z * jax.nn.sigmoid(z)


def _lse1(x):
    return jnp.log(1.0 + jnp.exp(-jnp.abs(x)))


def _cs(a):
    return jnp.sum(a, axis=0, keepdims=True)


def _iota(shape, dim):
    return lax.broadcasted_iota(jnp.int32, shape, dim)


def _mixer_matrices():
    r, c = _iota((256, 256), 0), _iota((256, 256), 1)
    same_chunk = (r >> 6) == (c >> 6)
    mats = jnp.stack([jnp.where((c > r) & same_chunk, 1.0, 0.0), jnp.where((c < r) & same_chunk, 1.0, 0.0),
                      jnp.where(same_chunk, 1.0 / 64.0, 0.0), jnp.where((r < 128) & (r - 16 == (c >> 6)), 1.0, 0.0)])
    mask = jnp.where((_iota((256, 128), 0) >> 6) == (_iota((256, 128), 1) >> 5), 1.0, 0.0)
    return mats.astype(BF16), mask.astype(F32)


def _dn(ext, k, n, h):
    return pltpu.roll(ext, k, axis=0)[h:h + n]


def _up(ext, k, n):
    return pltpu.roll(ext, ext.shape[0] - k, axis=0)[:n]


def _pool_lane_select(lane, s2, s4, s8, s16):
    return jnp.where(lane < 64, s2, jnp.where(lane < 128, s4, jnp.where(lane < 192, s8, s16)))


def _winsum_dn(ext, lane):
    s2 = ext + pltpu.roll(ext, 1, axis=0)
    s4 = s2 + pltpu.roll(s2, 2, axis=0)
    s8 = s4 + pltpu.roll(s4, 4, axis=0)
    s16 = s8 + pltpu.roll(s8, 8, axis=0)
    return _pool_lane_select(lane, s2, s4, s8, s16)


def _winsum_up(ext, lane):
    m = ext.shape[0]
    s2 = ext + pltpu.roll(ext, m - 1, axis=0)
    s4 = s2 + pltpu.roll(s2, m - 2, axis=0)
    s8 = s4 + pltpu.roll(s4, m - 4, axis=0)
    s16 = s8 + pltpu.roll(s8, m - 8, axis=0)
    return _pool_lane_select(lane, s2, s4, s8, s16)


def _pool_inv_count(tile, n):
    lane = _iota((1, 256), 1)
    win = _pool_lane_select(lane, 2.0, 4.0, 8.0, 16.0).astype(F32)
    tpos = (tile * n + _iota((n, 1), 0) + 1).astype(F32)
    return jnp.where(tpos >= win, 1.0 / win, 1.0 / tpos)


def _silu_pair(z):
    s = jax.nn.sigmoid(z)
    return z * s, s * (1.0 + z * (1.0 - s))


def _chunks(a):
    return [a[c * CH:(c + 1) * CH] for c in range(a.shape[0] // CH)]


def _halves(fn, a, b):
    return jnp.concatenate([fn(a[:, 0:128], b[:, 0:128]), fn(a[:, 128:256], b[:, 128:256])], axis=1)


def _mixer_tile_prep(p_ref, xc, prm_ref, gw_v, cm_ref, mk_ref):
    tail = p_ref[:, C_TL:C_TL + 128]
    pre = _dot(tail, gw_v) + prm_ref[R_GB:R_GB + 1, 0:128]
    la = (jnp.minimum(pre, 0.0) - _lse1(pre)) * INV_TAU
    dtin = tail + prm_ref[R_DTB:R_DTB + 1, 0:128]
    dtf = jnp.maximum(dtin, 0.0) + _lse1(dtin)
    dte = _dot2_l(dtf, cm_ref[3, 0:128, :])
    da = dte * prm_ref[R_AE:R_AE + 1, 0:256]
    rev = _dot2_r(cm_ref[0], jnp.concatenate([la, da], axis=1))
    dec = jnp.exp(rev[:, 0:128])
    kd = p_ref[:, C_GK:C_GK + 128] * dec
    wdec = jnp.exp(rev[:, 128:384])
    w = wdec * dte
    xw = xc[:, 0:256] * w
    d_s = [jnp.exp(_cs(a)) for a in _chunks(la)]
    et = [jnp.exp(_cs(a)) for a in _chunks(da)]
    mask_t = mk_ref[...]
    ut_g = [_dot_tn(v, k) * mask_t for v, k in zip(_chunks(p_ref[:, C_GV:C_GV + 256]), _chunks(kd))]
    ut_s = [_halves(_dot_tn, b, x) for b, x in zip(_chunks(xc[:, 256:512]), _chunks(xw))]
    return tail, pre, dtin, dte, dec, kd, wdec, w, xw, d_s, et, ut_g, ut_s


def _rmsproj(x, nw, wp, name, tm=512, rider=None):
    t = x.shape[0]

    def body(x_ref, nw_ref, w_ref, o_ref, h_ref):
        xv = x_ref[...]
        rs = lax.rsqrt(jnp.mean(xv * xv, axis=-1, keepdims=True) + EPS)
        h = (xv * rs * nw_ref[...]).astype(BF16)
        h_ref[...] = h
        o_ref[...] = jnp.dot(h, w_ref[...], preferred_element_type=F32)

    (proj, h), extra = _call(
        body, (x, nw, wp), grid=(t // tm,), name=name, sem=("parallel",), rider=rider,
        in_specs=[pl.BlockSpec((tm, D), lambda i: (i, 0)), pl.BlockSpec((1, D), lambda i: (0, 0)),
                  pl.BlockSpec((D, NP), lambda i: (0, 0))],
        out_specs=[pl.BlockSpec((tm, NP), lambda i: (i, 0)), pl.BlockSpec((tm, D), lambda i: (i, 0))],
        out_shape=[jax.ShapeDtypeStruct((t, NP), F32), jax.ShapeDtypeStruct((t, D), BF16)])
    return proj, h, extra


def _head(x, tgt, fw, name, tm=512):
    t = x.shape[0]

    def body(x_ref, t_ref, w_ref, dx_ref, acc_ref):
        @pl.when(pl.program_id(0) == 0)
        def _():
            acc_ref[...] = jnp.zeros_like(acc_ref)

        xv = x_ref[...]
        w = w_ref[...]
        rs = lax.rsqrt(jnp.mean(xv * xv, axis=-1, keepdims=True) + EPS)
        xh = xv * rs
        err = xh * w - t_ref[...]
        dy = err * (1.0 / D)
        dxh = dy * w
        dx_ref[...] = rs * (dxh - xh * jnp.mean(dxh * xh, axis=-1, keepdims=True))
        acc_ref[0:1, :] += _cs(dy * xh)
        acc_ref[1:2, :] += jnp.zeros((1, D), F32) + (0.5 / D) * jnp.sum(err * err)

    return pl.pallas_call(
        body, grid=(t // tm,), name=name,
        in_specs=[pl.BlockSpec((tm, D), lambda i: (i, 0)), pl.BlockSpec((tm, D), lambda i: (i, 0)),
                  pl.BlockSpec((1, D), lambda i: (0, 0))],
        out_specs=[pl.BlockSpec((tm, D), lambda i: (i, 0)), pl.BlockSpec((8, D), lambda i: (0, 0))],
        out_shape=[jax.ShapeDtypeStruct((t, D), F32), jax.ShapeDtypeStruct((8, D), F32)],
        compiler_params=_cparams(("arbitrary",)),
    )(x, tgt, fw)


def _dxin(dp, wpt, x, dxn, nw, name, tm=512, rider=None):
    t = x.shape[0]

    def body(dp_ref, w_ref, x_ref, dxn_ref, nw_ref, dx_ref, dnw_ref):
        @pl.when(pl.program_id(0) == 0)
        def _():
            dnw_ref[...] = jnp.zeros_like(dnw_ref)

        dh = jnp.dot(dp_ref[...].astype(BF16), w_ref[...], preferred_element_type=F32)
        xv = x_ref[...]
        rs = lax.rsqrt(jnp.mean(xv * xv, axis=-1, keepdims=True) + EPS)
        xh = xv * rs
        dnw_ref[0:1, :] += _cs(dh * xh)
        dxh = dh * nw_ref[...]
        dx_ref[...] = dxn_ref[...] + rs * (dxh - xh * jnp.mean(dxh * xh, axis=-1, keepdims=True))

    return _call(
        body, (dp, wpt, x, dxn, nw), grid=(t // tm,), name=name, sem=("arbitrary",), rider=rider,
        in_specs=[pl.BlockSpec((tm, NP), lambda i: (i, 0)), pl.BlockSpec((NP, D), lambda i: (0, 0)),
                  pl.BlockSpec((tm, D), lambda i: (i, 0)), pl.BlockSpec((tm, D), lambda i: (i, 0)),
                  pl.BlockSpec((1, D), lambda i: (0, 0))],
        out_specs=[pl.BlockSpec((tm, D), lambda i: (i, 0)), pl.BlockSpec((8, D), lambda i: (0, 0))],
        out_shape=[jax.ShapeDtypeStruct((t, D), F32), jax.ShapeDtypeStruct((8, D), F32)])


def _dwin(h, dp, name, tm=512, tn=NP, rider=None):
    t = h.shape[0]

    def body(h_ref, dp_ref, o_ref):
        @pl.when(pl.program_id(1) == 0)
        def _():
            o_ref[...] = jnp.zeros_like(o_ref)

        o_ref[...] += _dot_tn(h_ref[...], dp_ref[...])

    (dwp,), extra = _call(
        body, (h, dp), grid=(NP // tn, t // tm), name=name, sem=("parallel", "arbitrary"), rider=rider,
        in_specs=[pl.BlockSpec((tm, D), lambda j, i: (i, 0)), pl.BlockSpec((tm, tn), lambda j, i: (i, j))],
        out_specs=[pl.BlockSpec((D, tn), lambda j, i: (0, j))], out_shape=[jax.ShapeDtypeStruct((D, NP), F32)])
    return dwp, extra


def _mixer_fwd(proj, x, wo, prm, gw, pw, cmat, mask, name, rider=None):
    t = proj.shape[0]
    nt, nc = t // TB, t // CH

    def body(p_ref, x_ref, wo_ref, prm_ref, gw_ref, pw_ref, cm_ref, mk_ref, mix_ref, sg_ref, ss_ref, xn_ref,
             sg_s, ss_s, h_ua, h_pu, h_sx):
        i = pl.program_id(0)

        @pl.when(i == 0)
        def _():
            for r in (sg_s, ss_s, h_ua, h_pu, h_sx):
                r[...] = jnp.zeros_like(r)

        lane = _iota((1, 256), 1)
        u = p_ref[:, C_AC:C_AC + 256] * p_ref[:, C_AH:C_AH + 256]
        ext = jnp.concatenate([h_ua[...], u], axis=0)
        cv = (prm_ref[R_CAW + 2:R_CAW + 3, 0:256] * u + prm_ref[R_CAW + 1:R_CAW + 2, 0:256] * _dn(ext, 1, TB, 8)
              + prm_ref[R_CAW:R_CAW + 1, 0:256] * _dn(ext, 2, TB, 8))
        mix_ref[:, 0:256] = (p_ref[:, C_AB:C_AB + 256] * cv * _silu(p_ref[:, C_AZ:C_AZ + 256])).astype(BF16)
        h_ua[...] = u[TB - 8:, :]
        pu = p_ref[:, C_PU:C_PU + 256]
        ext = jnp.concatenate([h_pu[...], pu], axis=0)
        pooled = _winsum_dn(ext, lane)[16:] * _pool_inv_count(i, TB) - pu
        mixed = _dot(pooled, pw_ref[...])
        mix_ref[:, 512:768] = (prm_ref[R_PSC:R_PSC + 1, 0:256] * mixed * _silu(p_ref[:, C_PZ:C_PZ + 256])).astype(BF16)
        h_pu[...] = pu[TB - 16:, :]
        sx = p_ref[:, C_SX:C_SX + 768]
        ext = jnp.concatenate([h_sx[...], sx], axis=0)
        xc = _silu(prm_ref[R_SCW + 3:R_SCW + 4, :] * sx + prm_ref[R_SCW + 2:R_SCW + 3, :] * _dn(ext, 1, TB, 8)
                   + prm_ref[R_SCW + 1:R_SCW + 2, :] * _dn(ext, 2, TB, 8) + prm_ref[R_SCW:R_SCW + 1, :] * _dn(ext, 3, TB, 8)
                   + prm_ref[R_SCB:R_SCB + 1, :])
        h_sx[...] = sx[TB - 8:, :]

        _, _, _, _, _, _, _, _, _, d_s, et, ut_g, ut_s = _mixer_tile_prep(p_ref, xc, prm_ref, gw_ref[...], cm_ref, mk_ref)
        s_g, s_s = sg_s[...], ss_s[...]
        o, y = [], []
        qs = _chunks(p_ref[:, C_GQ:C_GQ + 128] * GLA_SCALE)
        cm = _chunks(xc[:, 512:768])
        for c in range(NCH):
            sg_ref[c] = s_g
            ss_ref[c] = s_s
            s_g = s_g * d_s[c] + ut_g[c]
            s_s = s_s * et[c] + ut_s[c]
            o.append(_dot_nt(qs[c], s_g))
            y.append(_halves(_dot, cm[c], s_s))
        sg_s[...] = s_g
        ss_s[...] = s_s
        o = jnp.concatenate(o, axis=0)
        on = o * lax.rsqrt(_dot2_l(o * o, cm_ref[2]) + EPS)
        mix_ref[:, 256:512] = (on * prm_ref[R_GNW:R_GNW + 1, 0:256] * _silu(p_ref[:, C_GZ:C_GZ + 256])).astype(BF16)
        y2 = ((jnp.concatenate(y, axis=0) + prm_ref[R_DE:R_DE + 1, 0:256] * xc[:, 0:256])
              * _silu(p_ref[:, C_SZ:C_SZ + 256]))
        mix_ref[:, 768:1024] = (y2 * lax.rsqrt(jnp.mean(y2 * y2, axis=-1, keepdims=True) + EPS)
                                * prm_ref[R_SNW:R_SNW + 1, 0:256]).astype(BF16)
        xn_ref[...] = x_ref[...] + jnp.dot(mix_ref[...], wo_ref[...], preferred_element_type=F32)

    return _call(
        body, (proj, x, wo, prm, gw, pw, cmat, mask), grid=(nt,), name=name, sem=("arbitrary",), rider=rider,
        in_specs=[pl.BlockSpec((TB, NP), lambda i: (i, 0)), pl.BlockSpec((TB, D), lambda i: (i, 0)),
                  pl.BlockSpec((D, D), lambda i: (0, 0)), pl.BlockSpec((16, 768), lambda i: (0, 0)),
                  pl.BlockSpec((128, 128), lambda i: (0, 0)), pl.BlockSpec((256, 256), lambda i: (0, 0)),
                  pl.BlockSpec((4, 256, 256), lambda i: (0, 0, 0)), pl.BlockSpec((256, 128), lambda i: (0, 0))],
        out_specs=[pl.BlockSpec((TB, D), lambda i: (i, 0)), pl.BlockSpec((NCH, 256, 128), lambda i: (i, 0, 0)),
                   pl.BlockSpec((NCH, 128, 256), lambda i: (i, 0, 0)), pl.BlockSpec((TB, D), lambda i: (i, 0))],
        out_shape=[jax.ShapeDtypeStruct((t, D), BF16), jax.ShapeDtypeStruct((nc, 256, 128), F32),
                   jax.ShapeDtypeStruct((nc, 128, 256), F32), jax.ShapeDtypeStruct((t, D), F32)],
        scratch_shapes=[pltpu.VMEM((256, 128), F32), pltpu.VMEM((128, 256), F32), pltpu.VMEM((8, 256), F32),
                        pltpu.VMEM((16, 256), F32), pltpu.VMEM((8, 768), F32)])


def _mixer_bwd(proj, dxn, wot, mix, sg, ss, prm, gw, pw, cmat, mask, name, rider=None):
    t = proj.shape[0]
    nt = t // TB
    rev = lambda i: nt - 1 - i

    def body(p_ref, hp_ref, dxn_ref, wot_ref, mix_ref, sg_ref, ss_ref, prm_ref, gw_ref, pw_ref, cm_ref, mk_ref,
             dp_ref, sgc_ref, dwo_ref,
             gg_s, gs_s, h_dcv, h_dpl, h_dpre, gsm_ref, dgw_ref, dpw_ref, dm_ref):
        i = pl.program_id(0)
        tile = nt - 1 - i

        @pl.when(i == 0)
        def _():
            for r in (gg_s, gs_s, h_dcv, h_dpl, h_dpre, gsm_ref, dgw_ref, dpw_ref, dwo_ref):
                r[...] = jnp.zeros_like(r)

        dxn = dxn_ref[...].astype(BF16)
        dm_ref[...] = jnp.dot(dxn, wot_ref[...], preferred_element_type=F32)
        dwo_ref[...] += _dot_tn(mix_ref[...], dxn)

        lane = _iota((1, 256), 1)
        first = (tile > 0).astype(F32)
        ah, ac = p_ref[:, C_AH:C_AH + 256], p_ref[:, C_AC:C_AC + 256]
        ab, az = p_ref[:, C_AB:C_AB + 256], p_ref[:, C_AZ:C_AZ + 256]
        w0, w1, w2 = (prm_ref[R_CAW + j:R_CAW + j + 1, 0:256] for j in range(3))
        u = ac * ah
        ext = jnp.concatenate([hp_ref[8:16, C_AC:C_AC + 256] * hp_ref[8:16, C_AH:C_AH + 256] * first, u], axis=0)
        u1, u2 = _dn(ext, 1, TB, 8), _dn(ext, 2, TB, 8)
        cv = w2 * u + w1 * u1 + w0 * u2
        g = dm_ref[:, 0:256]
        sz, dsz = _silu_pair(az)
        dp_ref[:, C_AB:C_AB + 256] = (g * cv * sz).astype(BF16)
        dp_ref[:, C_AZ:C_AZ + 256] = (g * ab * cv * dsz).astype(BF16)
        dcv = g * ab * sz
        dext = jnp.concatenate([dcv, h_dcv[...]], axis=0)
        du = w2 * dcv + w1 * _up(dext, 1, TB) + w0 * _up(dext, 2, TB)
        dp_ref[:, C_AC:C_AC + 256] = (du * ah).astype(BF16)
        dp_ref[:, C_AH:C_AH + 256] = (du * ac).astype(BF16)
        gsm_ref[R_CAW:R_CAW + 1, 0:256] += _cs(dcv * u2)
        gsm_ref[R_CAW + 1:R_CAW + 2, 0:256] += _cs(dcv * u1)
        gsm_ref[R_CAW + 2:R_CAW + 3, 0:256] += _cs(dcv * u)
        h_dcv[...] = dcv[0:8, :]
        pu, pz = p_ref[:, C_PU:C_PU + 256], p_ref[:, C_PZ:C_PZ + 256]
        psc = prm_ref[R_PSC:R_PSC + 1, 0:256]
        icnt = _pool_inv_count(tile, TB)
        ext = jnp.concatenate([hp_ref[:, C_PU:C_PU + 256] * first, pu], axis=0)
        pooled = _winsum_dn(ext, lane)[16:] * icnt - pu
        pw_v = pw_ref[...]
        mixed = _dot(pooled, pw_v)
        g = dm_ref[:, 512:768]
        sz, dsz = _silu_pair(pz)
        gsm_ref[R_PSC:R_PSC + 1, 0:256] += _cs(g * mixed * sz)
        dp_ref[:, C_PZ:C_PZ + 256] = (g * psc * mixed * dsz).astype(BF16)
        dmixed = g * psc * sz
        dpw_ref[...] += _dot_tn(pooled, dmixed)
        dpooled = _dot_nt(dmixed, pw_v)
        qd = dpooled * icnt
        dext = jnp.concatenate([qd, h_dpl[...]], axis=0)
        dp_ref[:, C_PU:C_PU + 256] = (_winsum_up(dext, lane)[:TB] - dpooled).astype(BF16)
        h_dpl[...] = qd[0:16, :]
        sx = p_ref[:, C_SX:C_SX + 768]
        cw = [prm_ref[R_SCW + j:R_SCW + j + 1, :] for j in range(4)]
        ext = jnp.concatenate([hp_ref[8:16, C_SX:C_SX + 768] * first, sx], axis=0)
        sx1, sx2, sx3 = _dn(ext, 1, TB, 8), _dn(ext, 2, TB, 8), _dn(ext, 3, TB, 8)
        cpre = cw[3] * sx + cw[2] * sx1 + cw[1] * sx2 + cw[0] * sx3 + prm_ref[R_SCB:R_SCB + 1, :]
        xc, dxc = _silu_pair(cpre)
        xs, bm, cm = xc[:, 0:256], xc[:, 256:512], xc[:, 512:768]

        gw_v = gw_ref[...]
        tail, pre, dtin, dte, dec, kd, wdec, w, xw, d_s, et, ut_g, ut_s = _mixer_tile_prep(p_ref, xc, prm_ref, gw_v,
                                                                                          cm_ref, mk_ref)
        gmean = cm_ref[2]
        mask_t = mk_ref[...]
        gnw = prm_ref[R_GNW:R_GNW + 1, 0:256]
        a_e = prm_ref[R_AE:R_AE + 1, 0:256]
        d_e = prm_ref[R_DE:R_DE + 1, 0:256]
        snw = prm_ref[R_SNW:R_SNW + 1, 0:256]
        sg_in = [sg_ref[c] for c in range(NCH)]
        ss_in = [ss_ref[c] for c in range(NCH)]
        sg_n = [sg_in[c] * d_s[c] + ut_g[c] for c in range(NCH)]
        ss_n = [ss_in[c] * et[c] + ut_s[c] for c in range(NCH)]
        qs = _chunks(p_ref[:, C_GQ:C_GQ + 128] * GLA_SCALE)
        cm_c, bm_c, xw_c, kd_c = _chunks(cm), _chunks(bm), _chunks(xw), _chunks(kd)
        v_c = _chunks(p_ref[:, C_GV:C_GV + 256])
        o = jnp.concatenate([_dot_nt(qs[c], sg_n[c]) for c in range(NCH)], axis=0)
        y = jnp.concatenate([_halves(_dot, cm_c[c], ss_n[c]) for c in range(NCH)], axis=0) + d_e * xs
        gz = p_ref[:, C_GZ:C_GZ + 256]
        r = lax.rsqrt(_dot2_l(o * o, gmean) + EPS)
        on = o * r
        dyb = dm_ref[:, 256:512]
        sz, dsz = _silu_pair(gz)
        dp_ref[:, C_GZ:C_GZ + 256] = (dyb * on * gnw * dsz).astype(BF16)
        tg = dyb * sz
        gsm_ref[R_GNW:R_GNW + 1, 0:256] += _cs(tg * on)
        don = tg * gnw
        do_c = _chunks(r * (don - on * _dot2_l(don * on, gmean)))
        ssz = p_ref[:, C_SZ:C_SZ + 256]
        sil, dsil = _silu_pair(ssz)
        y2 = y * sil
        r = lax.rsqrt(jnp.mean(y2 * y2, axis=-1, keepdims=True) + EPS)
        yn = y2 * r
        dyd = dm_ref[:, 768:1024]
        gsm_ref[R_SNW:R_SNW + 1, 0:256] += _cs(dyd * yn)
        dn = dyd * snw
        dy2 = r * (dn - yn * jnp.mean(dn * yn, axis=-1, keepdims=True))
        dp_ref[:, C_SZ:C_SZ + 256] = (dy2 * y * dsil).astype(BF16)
        dy = dy2 * sil
        gsm_ref[R_DE:R_DE + 1, 0:256] += _cs(dy * xs)
        dy_c = _chunks(dy)
        dq = jnp.concatenate([_dot(do_c[c], sg_n[c]) for c in range(NCH)], axis=0)
        dp_ref[:, C_GQ:C_GQ + 128] = (dq * GLA_SCALE).astype(BF16)
        dcm = jnp.concatenate([_halves(_dot_nt, dy_c[c], ss_n[c]) for c in range(NCH)], axis=0)
        gg = [_dot_tn(do_c[c], qs[c]) * mask_t for c in range(NCH)]
        gs = [_halves(_dot_tn, cm_c[c], dy_c[c]) for c in range(NCH)]
        car_g, car_s = gg_s[...], gs_s[...]
        for c in reversed(range(NCH)):
            gg[c] = gg[c] + car_g
            gs[c] = gs[c] + car_s
            car_g = gg[c] * d_s[c]
            car_s = gs[c] * et[c]
        gg_s[...] = car_g
        gs_s[...] = car_s
        dkd = jnp.concatenate([_dot(v_c[c], gg[c]) for c in range(NCH)], axis=0)
        dp_ref[:, C_GV:C_GV + 256] = jnp.concatenate([_dot_nt(kd_c[c], gg[c]) for c in range(NCH)], axis=0).astype(BF16)
        dp_ref[:, C_GK:C_GK + 128] = (dkd * dec).astype(BF16)
        dbm = jnp.concatenate([_halves(_dot_nt, xw_c[c], gs[c]) for c in range(NCH)], axis=0)
        dxw = jnp.concatenate([_halves(_dot, bm_c[c], gs[c]) for c in range(NCH)], axis=0)
        dxs = dy * d_e + dxw * w
        dw = dxw * xs
        dsuf = _dot2_r(cm_ref[1], jnp.concatenate([dkd * kd, dw * dte * wdec], axis=1))
        tot_g = jnp.concatenate([jnp.broadcast_to(_cs(gg[c] * sg_in[c]) * d_s[c], (CH, 128)) for c in range(NCH)], axis=0)
        tot_s = jnp.concatenate([jnp.broadcast_to(_cs(gs[c] * ss_in[c]) * et[c], (CH, 256)) for c in range(NCH)], axis=0)
        dpre = (dsuf[:, 0:128] + tot_g) * INV_TAU * jax.nn.sigmoid(-pre)
        dgw_ref[...] += _dot_tn(tail, dpre)
        gsm_ref[R_GB:R_GB + 1, 0:128] += _cs(dpre)
        dda = dsuf[:, 128:384] + tot_s
        gsm_ref[R_AE:R_AE + 1, 0:256] += _cs(dda * dte)
        dtail_s = _dot2_nt(dw * wdec + dda * a_e, cm_ref[3, 0:128, :]) * jax.nn.sigmoid(dtin)
        gsm_ref[R_DTB:R_DTB + 1, 0:128] += _cs(dtail_s)
        dp_ref[:, C_TL:C_TL + 128] = (_dot_nt(dpre, gw_v) + dtail_s).astype(BF16)
        dpre_c = jnp.concatenate([dxs, dbm, dcm], axis=1) * dxc
        dext = jnp.concatenate([dpre_c, h_dpre[...]], axis=0)
        dp_ref[:, C_SX:C_SX + 768] = (cw[3] * dpre_c + cw[2] * _up(dext, 1, TB) + cw[1] * _up(dext, 2, TB)
                                      + cw[0] * _up(dext, 3, TB)).astype(BF16)
        gsm_ref[R_SCW + 3:R_SCW + 4, :] += _cs(dpre_c * sx)
        gsm_ref[R_SCW + 2:R_SCW + 3, :] += _cs(dpre_c * sx1)
        gsm_ref[R_SCW + 1:R_SCW + 2, :] += _cs(dpre_c * sx2)
        gsm_ref[R_SCW:R_SCW + 1, :] += _cs(dpre_c * sx3)
        gsm_ref[R_SCB:R_SCB + 1, :] += _cs(dpre_c)
        h_dpre[...] = dpre_c[0:8, :]

        @pl.when(i == nt - 1)
        def _():
            ri, ci = _iota((256, 256), 0), _iota((256, 256), 1)
            per_head = jnp.where((ri >> 6) == ci, 1.0, 0.0).astype(BF16)
            per_dv = jnp.where((ri & 63) == ci, 1.0, 0.0).astype(BF16)
            row = _iota((8, 256), 0)
            top = gsm_ref[0:8, 0:256]
            sgc_ref[0:8, 0:256] = jnp.where(row == R_GNW, _dot3_l(top, per_dv), top)
            bot = gsm_ref[8:16, 0:256]
            fold = _dot3_l(jnp.where(row == R_AE - 8, bot * a_e, bot), per_head)
            sgc_ref[8:16, 0:256] = jnp.where((row == R_AE - 8) | (row == R_DE - 8), fold, bot)
            sgc_ref[0:16, 256:768] = gsm_ref[:, 256:768]
            sgc_ref[0:16, 768:896] = dgw_ref[0:16, :]
            sgc_ref[0:16, 896:1024] = jnp.zeros((16, 128), F32)
            diag = _pool_lane_select(lane, dpw_ref[0:64, :], dpw_ref[64:128, :], dpw_ref[128:192, :], dpw_ref[192:256, :])
            for q in range(4):
                sgc_ref[16:32, 256 * q:256 * q + 256] = diag[16 * q:16 * q + 16, :]

    return _call(
        body, (proj, proj, dxn, wot, mix, sg, ss, prm, gw, pw, cmat, mask), grid=(nt,), name=name, sem=("arbitrary",),
        rider=rider,
        in_specs=[pl.BlockSpec((TB, NP), lambda i: (rev(i), 0)),
                  pl.BlockSpec((16, HALO_W), lambda i: (jnp.maximum(rev(i) * (TB // 16) - 1, 0), 0)),
                  pl.BlockSpec((TB, D), lambda i: (rev(i), 0)), pl.BlockSpec((D, D), lambda i: (0, 0)),
                  pl.BlockSpec((TB, D), lambda i: (rev(i), 0)),
                  pl.BlockSpec((NCH, 256, 128), lambda i: (rev(i), 0, 0)),
                  pl.BlockSpec((NCH, 128, 256), lambda i: (rev(i), 0, 0)),
                  pl.BlockSpec((16, 768), lambda i: (0, 0)), pl.BlockSpec((128, 128), lambda i: (0, 0)),
                  pl.BlockSpec((256, 256), lambda i: (0, 0)), pl.BlockSpec((4, 256, 256), lambda i: (0, 0, 0)),
                  pl.BlockSpec((256, 128), lambda i: (0, 0))],
        out_specs=[pl.BlockSpec((TB, NP), lambda i: (rev(i), 0)), pl.BlockSpec((32, 1024), lambda i: (0, 0)),
                   pl.BlockSpec((D, D), lambda i: (0, 0))],
        out_shape=[jax.ShapeDtypeStruct((t, NP), BF16), jax.ShapeDtypeStruct((32, 1024), F32),
                   jax.ShapeDtypeStruct((D, D), F32)],
        scratch_shapes=[pltpu.VMEM((256, 128), F32), pltpu.VMEM((128, 256), F32), pltpu.VMEM((8, 256), F32),
                        pltpu.VMEM((16, 256), F32), pltpu.VMEM((8, 768), F32), pltpu.VMEM((16, 768), F32),
                        pltpu.VMEM((128, 128), F32), pltpu.VMEM((256, 256), F32), pltpu.VMEM((TB, D), F32)])


def _half(c, n):
    return pl.ds(pl.multiple_of(c * (n // 2), n // 2), n // 2)


def _other_chips(x, y):
    return ((1 - x, y), (x, 1 - y), (1 - x, 1 - y))


def _remote(src, dst, send, recv, k, dev):
    return pltpu.make_async_remote_copy(src_ref=src, dst_ref=dst, send_sem=send.at[k], recv_sem=recv.at[k], device_id=dev,
                                        device_id_type=MESH)


def _sem(n):
    return pltpu.SemaphoreType.DMA((n,))


def _rider_gather_ici(shards, extra=None):
    shards = tuple(shards) + ((extra,) if extra is not None else ())
    n = len(shards)

    def copies(rins, routs, sems, arrivals=True):
        send, recv, loc = sems
        x, y, c = _place()
        me = 2 * x + y
        own = [pltpu.make_async_copy(rins[k], routs[k].at[me], loc.at[k]) for k in range(n)]
        out, inc = [], []
        for j, (px, py) in enumerate(_other_chips(x, y)):
            for k in range(n):
                whole = extra is not None and k == n - 1
                rows = pl.ds(0, shards[k].shape[0]) if whole else _half(c, shards[k].shape[0])
                out.append(_remote(rins[k].at[rows], routs[k].at[me, rows], send, recv, n * j + k, (px, py, c)))
                if arrivals:
                    inc.append(_remote(rins[k].at[rows], routs[k].at[2 * px + py, rows], send, recv, n * j + k, (px, py, c)))
        return own, out, inc

    def start(rins, routs, sems):
        own, out, _ = copies(rins, routs, sems, arrivals=False)
        for cp in own + out:
            cp.start()

    def finish(rins, routs, sems):
        own, out, inc = copies(rins, routs, sems)
        for cp in inc:
            cp.wait_recv()
        for cp in out:
            cp.wait_send()
        for cp in own:
            cp.wait()

    return _Rider(shards, [jax.ShapeDtypeStruct((4,) + a.shape, a.dtype) for a in shards],
                  [_sem(3 * n), _sem(3 * n), _sem(n)], start, finish)


def _rider_gather_d2d(slabs):
    slabs = tuple(slabs)
    n = len(slabs)

    def copies(routs, sems, arrivals=True):
        send, recv = sems
        x, y, c = _place()
        out, inc = [], []
        for j, (px, py) in enumerate(_other_chips(x, y)):
            for k in range(n):
                rows = slabs[k].shape[1]
                mine, theirs = routs[k].at[2 * px + py, _half(c, rows)], routs[k].at[2 * px + py, _half(1 - c, rows)]
                out.append(_remote(mine, mine, send, recv, n * j + k, (x, y, 1 - c)))
                if arrivals:
                    inc.append(_remote(theirs, theirs, send, recv, n * j + k, (x, y, 1 - c)))
        return out, inc

    def start(rins, routs, sems):
        for cp in copies(routs, sems, arrivals=False)[0]:
            cp.start()

    def finish(rins, routs, sems):
        out, inc = copies(routs, sems)
        for cp in inc:
            cp.wait_recv()
        for cp in out:
            cp.wait_send()

    return _Rider(slabs, [jax.ShapeDtypeStruct(a.shape, a.dtype) for a in slabs], [_sem(3 * n), _sem(3 * n)], start, finish,
                  aliases={k: k for k in range(n)})


def _rider_swap(parts):
    parts = tuple(parts)
    n = len(parts)

    def copies(rins, routs, sems):
        send, recv = sems
        x, y, c = _place()
        return [_remote(rins[k].at[:, _half(1 - c, parts[k].shape[1])], routs[k], send, recv, k, (x, y, 1 - c))
                for k in range(n)]

    def start(rins, routs, sems):
        for cp in copies(rins, routs, sems):
            cp.start()

    def finish(rins, routs, sems):
        for cp in copies(rins, routs, sems):
            cp.wait()

    return _Rider(parts, [jax.ShapeDtypeStruct((4, a.shape[1] // 2, a.shape[2]), a.dtype) for a in parts],
                  [_sem(n), _sem(n)], start, finish)


def _rider_scatter(parts):
    parts = tuple(parts)
    n = len(parts)

    def copies(rins, routs, sems, arrivals=True):
        send, recv, loc = sems
        x, y, c = _place()
        me = 2 * x + y
        own = [pltpu.make_async_copy(rins[k].at[me], routs[k].at[me], loc.at[k]) for k in range(n)]
        out, inc = [], []
        for j, (px, py) in enumerate(_other_chips(x, y)):
            for k in range(n):
                out.append(_remote(rins[k].at[2 * px + py], routs[k].at[me], send, recv, n * j + k, (px, py, c)))
                if arrivals:
                    inc.append(_remote(rins[k].at[me], routs[k].at[2 * px + py], send, recv, n * j + k, (px, py, c)))
        return own, out, inc

    def start(rins, routs, sems):
        own, out, _ = copies(rins, routs, sems, arrivals=False)
        for cp in own + out:
            cp.start()

    def finish(rins, routs, sems):
        own, out, inc = copies(rins, routs, sems)
        for cp in inc:
            cp.wait_recv()
        for cp in out:
            cp.wait_send()
        for cp in own:
            cp.wait()

    return _Rider(parts, [jax.ShapeDtypeStruct(a.shape, a.dtype) for a in parts], [_sem(3 * n), _sem(3 * n), _sem(n)],
                  start, finish)


def _rider_share(halves):
    halves = tuple(halves)
    n = len(halves)

    def copies(rins, routs, sems, arrivals=True):
        send, recv, loc = sems
        x, y, c = _place()
        own, out, inc = [], [], []
        for k in range(n):
            rows = 2 * halves[k].shape[0]
            own.append(pltpu.make_async_copy(rins[k], routs[k].at[_half(c, rows)], loc.at[k]))
            out.append(_remote(rins[k], routs[k].at[_half(c, rows)], send, recv, k, (x, y, 1 - c)))
            if arrivals:
                inc.append(_remote(rins[k], routs[k].at[_half(1 - c, rows)], send, recv, k, (x, y, 1 - c)))
        return own, out, inc

    def start(rins, routs, sems):
        own, out, _ = copies(rins, routs, sems, arrivals=False)
        for cp in own + out:
            cp.start()

    def finish(rins, routs, sems):
        own, out, inc = copies(rins, routs, sems)
        for cp in inc:
            cp.wait_recv()
        for cp in out:
            cp.wait_send()
        for cp in own:
            cp.wait()

    return _Rider(halves, [jax.ShapeDtypeStruct((2 * a.shape[0], a.shape[1]), a.dtype) for a in halves],
                  [_sem(n), _sem(n), _sem(n)], start, finish)


def _pair_sum(core, full, recv, name, br=128):
    n, rows, cols = recv.shape

    def body(c_ref, a_ref, b_ref, o_ref):
        o_ref[...] = (a_ref[...] + b_ref[...]).astype(BF16)

    nb = rows // br
    return pl.pallas_call(
        body, name=name, out_shape=jax.ShapeDtypeStruct(recv.shape, BF16),
        grid_spec=pltpu.PrefetchScalarGridSpec(
            num_scalar_prefetch=1, grid=(n, nb),
            in_specs=[pl.BlockSpec((1, br, cols), lambda i, j, c: (i, c[0] * nb + j, 0)),
                      pl.BlockSpec((1, br, cols), lambda i, j, c: (i, j, 0))],
            out_specs=pl.BlockSpec((1, br, cols), lambda i, j, c: (i, j, 0))),
        compiler_params=_cparams(("parallel", "parallel")))(core, full, recv)


def _sum4(a, name, br=128):
    _, r, c = a.shape

    def body(a_ref, o_ref):
        o_ref[...] = ((a_ref[0].astype(F32) + a_ref[1].astype(F32)) + a_ref[2].astype(F32)) + a_ref[3].astype(F32)

    return pl.pallas_call(body, grid=(r // br,), name=name,
                          in_specs=[pl.BlockSpec((4, br, c), lambda i: (0, i, 0))],
                          out_specs=pl.BlockSpec((br, c), lambda i: (i, 0)),
                          out_shape=jax.ShapeDtypeStruct((r, c), F32),
                          compiler_params=_cparams(("parallel",)))(a)


def _adamw(w, g, m, v, name, br):
    n, r, c = w.shape

    def body(w_ref, g_ref, m_ref, v_ref, d_ref, m2_ref, v2_ref):
        d_ref[...], m2_ref[...], v2_ref[...] = _adam_math(w_ref[...], g_ref[...], m_ref[...], v_ref[...])

    spec = pl.BlockSpec((1, br, c), lambda i, j: (i, j, 0))
    shp = jax.ShapeDtypeStruct(w.shape, F32)
    return pl.pallas_call(body, grid=(n, r // br), name=name, in_specs=[spec] * 4, out_specs=[spec] * 3,
                          out_shape=[shp] * 3, compiler_params=_cparams(("parallel", "parallel")))(w, g, m, v)


def _adamw_w_in(w, g, m, v, name, bc=31):
    cols = w.shape[2]
    lead = lambda a: jnp.transpose(a, (2, 0, 1))
    g = jnp.stack(g)

    def body(w_ref, g_ref, m_ref, v_ref, go_ref, d_ref, m2_ref, v2_ref):
        for l in range(2):
            gv = g_ref[:, l, :]
            d_ref[:, l, :], m2_ref[:, l, :], v2_ref[:, l, :] = _adam_math(w_ref[:, l, :], gv, m_ref[:, l, :], v_ref[:, l, :])
            go_ref[:, l, :] = gv

    spec = pl.BlockSpec((bc, 2, D), lambda i: (i, 0, 0))
    outs = pl.pallas_call(body, grid=(cols // bc,), name=name, in_specs=[spec] * 4, out_specs=[spec] * 4,
                          out_shape=[jax.ShapeDtypeStruct((cols, 2, D), F32)] * 4,
                          compiler_params=_cparams(("parallel",)))(lead(w), lead(g), lead(m), lead(v))
    return [jnp.transpose(o, (1, 2, 0)) for o in outs]


_SMALL_NAMES = ("norm_w", "conv_a_w", "gla_gate_w", "gla_gate_b", "gla_norm_w", "pool_w", "pool_scale", "ssd_conv_w",
                "ssd_conv_b", "ssd_dt_bias", "ssd_a_log", "ssd_d", "ssd_norm_w", "final_norm_w")
SMALL_ROWS = 72


def _adam_math(w, g, m, v):
    m2 = ADAM_B1 * m + (1.0 - ADAM_B1) * g
    v2 = ADAM_B2 * v + (1.0 - ADAM_B2) * (g * g)
    m_hat = m2 / (1.0 - ADAM_B1 ** ADAM_STEP)
    v_hat = v2 / (1.0 - ADAM_B2 ** ADAM_STEP)
    return -ADAM_LR * (m_hat / (jnp.sqrt(v_hat) + ADAM_EPS) + ADAM_WD * w), m2, v2


def _small_slices(name, chip):
    if name == "conv_a_w":
        return [((), slice(R_CAW, R_CAW + 3), slice(64 * chip, 64 * chip + 64))]
    if name == "ssd_conv_w":
        return [((), slice(R_SCW, R_SCW + 4), slice(192 * chip, 192 * chip + 192))]
    if name == "gla_gate_w":
        return [((), slice(0, 16), slice(768, 896))]
    if name == "pool_w":
        return [((g, slice(16 * q, 16 * q + 16)), slice(16, 32), slice(256 * q + 64 * g, 256 * q + 64 * g + 64))
                for g in range(4) for q in range(4)]
    row, lanes = {"gla_gate_b": (R_GB, slice(0, 128)), "gla_norm_w": (R_GNW, slice(0, 64)),
                  "pool_scale": (R_PSC, slice(0, 256)), "ssd_conv_b": (R_SCB, slice(0, 768)),
                  "ssd_dt_bias": (R_DTB, slice(16, 20)), "ssd_a_log": (R_AE, slice(0, 4)), "ssd_d": (R_DE, slice(0, 4)),
                  "ssd_norm_w": (R_SNW, slice(0, 256))}[name]
    return [((), slice(row, row + 1), lanes)]


def _small_allreduce(sg0, sg1, dnw0, dnw1, head):
    def body(sg0_ref, sg1_ref, dnw0_ref, dnw1_ref, head_ref, acc, stage, pair, rbuf, send_sems, recv_sems):
        x, y, c = _place()
        chip = 2 * x + y
        stage[0:32, :] = sg0_ref[...]
        stage[32:64, :] = sg1_ref[...]
        stage[64:65, :] = dnw0_ref[0:1, :]
        stage[65:66, :] = dnw1_ref[0:1, :]
        stage[66:68, :] = head_ref[0:2, :]
        stage[68:72, :] = jnp.zeros((4, D), F32)
        sib = _remote(stage, pair, send_sems, recv_sems, 0, (x, y, 1 - c))
        sib.start()
        sib.wait()
        rbuf[0] = stage[...] + pair[...]
        sends = [_remote(rbuf.at[0], rbuf.at[k], send_sems, recv_sems, k, (px, py, c))
                 for k, (px, py) in enumerate(_other_chips(x, y), start=1)]
        for cp in sends:
            cp.start()
        for cp in sends:
            cp.wait()
        slab = lambda d: jnp.where(d == 0, 0, jnp.where(d == 2, 1, jnp.where(d == 1, 2, 3)))
        total = rbuf[slab(jnp.bitwise_xor(chip, 0))]
        for s in range(1, 4):
            total = total + rbuf[slab(jnp.bitwise_xor(chip, s))]
        acc[...] = total

    vmem = pl.BlockSpec(memory_space=pltpu.VMEM)
    return pl.pallas_call(
        body, name="small_allreduce", in_specs=[vmem] * 5, out_specs=vmem,
        out_shape=jax.ShapeDtypeStruct((SMALL_ROWS, D), F32),
        scratch_shapes=[pltpu.VMEM((SMALL_ROWS, D), F32), pltpu.VMEM((SMALL_ROWS, D), F32),
                        pltpu.VMEM((4, SMALL_ROWS, D), F32), _sem(4), _sem(4)],
    )(sg0, sg1, dnw0, dnw1, head)


def _small_adamw(acc, w, m, v):
    n = len(_SMALL_NAMES)

    def body(*refs):
        acc = refs[0]
        w_refs, m_refs, v_refs = refs[1:1 + n], refs[1 + n:1 + 2 * n], refs[1 + 2 * n:1 + 3 * n]
        o = 1 + 3 * n
        g_out, d_out, m_out, v_out = refs[o:o + n], refs[o + n:o + 2 * n], refs[o + 2 * n:o + 3 * n], refs[o + 3 * n:o + 4 * n]
        loss_ref = refs[o + 4 * n]
        chip = 2 * lax.axis_index("x") + lax.axis_index("y")
        loss_ref[...] = acc[67:68, 0:1]

        def update(i, idx, g):
            d, m2, v2 = _adam_math(w_refs[i][idx], g, m_refs[i][idx], v_refs[i][idx])
            g_out[i][idx], d_out[i][idx], m_out[i][idx], v_out[i][idx] = g, d, m2, v2

        for i, name in enumerate(_SMALL_NAMES):
            if name == "final_norm_w":
                update(i, (slice(0, 1), slice(None)), acc[66:67, :])
            elif name == "norm_w":
                for l in range(2):
                    update(i, (slice(l, l + 1), slice(None)), acc[64 + l:65 + l, :])
            elif name in ("conv_a_w", "ssd_conv_w"):
                for s in range(4):
                    @pl.when(chip == s)
                    def _(i=i, name=name, s=s):
                        for l in range(2):
                            (_, rows, lanes), = _small_slices(name, s)
                            update(i, (l,), acc[rows.start + 32 * l:rows.stop + 32 * l, lanes])
            else:
                for l in range(2):
                    for idx, rows, lanes in _small_slices(name, 0):
                        g = acc[rows.start + 32 * l:rows.stop + 32 * l, lanes]
                        if w_refs[i].ndim == 2:
                            update(i, (slice(l, l + 1), slice(None)), g)
                        else:
                            update(i, (l,) + idx, g)

    args = [acc] + [d[k] for d in (w, m, v) for k in _SMALL_NAMES]
    shapes = [jax.ShapeDtypeStruct(w[k].shape, F32) for k in _SMALL_NAMES]
    vmem = pl.BlockSpec(memory_space=pltpu.VMEM)
    outs = pl.pallas_call(body, name="small_adamw", in_specs=[vmem] * len(args), out_specs=[vmem] * (4 * n + 1),
                          out_shape=shapes * 4 + [jax.ShapeDtypeStruct((1, 1), F32)])(*args)
    return outs[0:n], outs[n:2 * n], outs[2 * n:3 * n], outs[3 * n:4 * n], outs[4 * n]


def _permute_cols(w):
    parts = [w[..., s:s + n] for s, n in _PERM]
    parts.append(jnp.zeros(w.shape[:-1] + (NP - NPROJ,), w.dtype))
    return jnp.concatenate(parts, axis=-1)


def _unpermute_cols(w):
    return jnp.concatenate([w[..., s:s + n] for s, n in _UNPERM], axis=-1)


def _mixer_consts(layer, conv_a_w, gla_gate_w, gla_gate_b, gla_norm_w, pool_w, pool_scale, ssd_conv_w, ssd_conv_b,
                  ssd_dt_bias, ssd_a_log, ssd_d, ssd_norm_w):
    def row(v):
        return jnp.pad(v.reshape(1, -1), ((0, 0), (0, 768 - v.size)))

    dtb = jnp.zeros((128,), F32).at[16:20].set(ssd_dt_bias[layer])
    rows = [jnp.pad(conv_a_w[layer], ((0, 0), (0, 512))), row(gla_gate_b[layer]), row(jnp.tile(gla_norm_w[layer], 4)),
            row(pool_scale[layer]), row(ssd_conv_b[layer]), row(dtb), row(jnp.repeat(-jnp.exp(ssd_a_log[layer]), 64)),
            row(jnp.repeat(ssd_d[layer], 64)), row(ssd_norm_w[layer]), jnp.zeros((1, 768), F32), ssd_conv_w[layer]]
    prm = jnp.concatenate(rows, axis=0)
    gw = jnp.zeros((128, 128), F32).at[0:16].set(gla_gate_w[layer]).astype(BF16)
    pw = jnp.zeros((256, 256), F32)
    for g in range(4):
        pw = pw.at[64 * g:64 * g + 64, 64 * g:64 * g + 64].set(pool_w[layer, g])
    return (prm, gw, pw.astype(BF16)) + _mixer_matrices()


def _layer_weights(s_in, s_out):
    wp = _permute_cols(jnp.transpose(s_in, (1, 0, 2)).reshape(D, NPROJ))
    wo = s_out.reshape(D, D)
    return wp, wp.T, wo, wo.T


def _grad_slabs(dwp, dwo):
    return jnp.transpose(_unpermute_cols(dwp).reshape(D, 4, NPROJ // 4), (1, 0, 2)), dwo.reshape(4, D // 4, D)


class _Comm:
    def __init__(self, w_in16, w_out16):
        self.w_in16, self.w_out16 = w_in16, w_out16
        self.core = lax.axis_index("c").astype(jnp.int32).reshape(1)

    def gather_ici(self, layer, extra=None):
        return _rider_gather_ici((self.w_in16[layer], self.w_out16[layer]), extra)

    def pair_sum(self, layer, slabs, received):
        return [_pair_sum(self.core, a, b, name=f"reduce_pair_sum{layer}_{k}") for k, (a, b) in enumerate(zip(slabs, received))]

    def chip_sum(self, layer, gathered):
        return [_sum4(a, name=f"reduce_chip_sum{layer}_{k}") for k, a in enumerate(gathered)]


def _local_step(x, tgt, norm_w, final_norm_w, consts, wts0, wts1=None, comm=None):
    nw = [norm_w[l:l + 1] for l in range(2)]
    proj0, h0, slabs = _rmsproj(x, nw[0], wts0[0], name="rmsproj0", rider=comm and comm.gather_ici(1))
    (mix0, sg0, ss0, x1), slabs = _mixer_fwd(proj0, x, wts0[2], *consts[0], name="mixer_fwd0",
                                             rider=comm and _rider_gather_d2d(slabs))
    if comm:
        wts1 = _layer_weights(*slabs)
    proj1, h1, _ = _rmsproj(x1, nw[1], wts1[0], name="rmsproj1")
    (mix1, sg1, ss1, x2), _ = _mixer_fwd(proj1, x1, wts1[2], *consts[1], name="mixer_fwd1")
    dx, head = _head(x2, tgt, final_norm_w.reshape(1, D), name="loss_head")
    (dproj, mgr1, dwo1), _ = _mixer_bwd(proj1, dx, wts1[3], mix1, sg1, ss1, *consts[1], name="mixer_bwd1")
    dwp1, _ = _dwin(h1, dproj, name="dwin1")
    slabs1 = _grad_slabs(dwp1, dwo1)
    (dx, dnw1), recv = _dxin(dproj, wts1[1], x1, dx, nw[1], name="dxin1", rider=comm and _rider_swap(slabs1))
    scat = comm and _rider_scatter(comm.pair_sum(1, slabs1, recv))
    (dproj, mgr0, dwo0), gathered = _mixer_bwd(proj0, dx, wts0[3], mix0, sg0, ss0, *consts[0], name="mixer_bwd0", rider=scat)
    share = comm and _rider_share(comm.chip_sum(1, gathered))
    dwp0, big1 = _dwin(h0, dproj, name="dwin0", rider=share)
    scat = None
    if comm:
        slabs0 = _grad_slabs(dwp0, dwo0)
        scat = _rider_scatter(comm.pair_sum(0, slabs0, _run_rider(_rider_swap(slabs0), "reduce_swap0")))
    (dx, dnw0), gathered = _dxin(dproj, wts0[1], x, dx, nw[0], name="dxin0", rider=scat)
    if comm:
        big0 = _run_rider(_rider_share(comm.chip_sum(0, gathered)), "reduce_share0")
        big = ((big0[0], big1[0]), (big0[1], big1[1]))
    else:
        big = ((dwp0, dwp1), (dwo0, dwo1))
    return head, dx, big, (dnw0, dnw1), (mgr0, mgr1)


def kernel(x, norm_w, w_in, conv_a_w, gla_gate_w, gla_gate_b, gla_norm_w, pool_w, pool_scale, ssd_conv_w, ssd_conv_b, ssd_dt_bias, ssd_a_log, ssd_d, ssd_norm_w, w_out, final_norm_w, loss_target, m_norm_w, m_w_in, m_conv_a_w, m_gla_gate_w, m_gla_gate_b, m_gla_norm_w, m_pool_w, m_pool_scale, m_ssd_conv_w, m_ssd_conv_b, m_ssd_dt_bias, m_ssd_a_log, m_ssd_d, m_ssd_norm_w, m_w_out, m_final_norm_w, v_norm_w, v_w_in, v_conv_a_w, v_gla_gate_w, v_gla_gate_b, v_gla_norm_w, v_pool_w, v_pool_scale, v_ssd_conv_w, v_ssd_conv_b, v_ssd_dt_bias, v_ssd_a_log, v_ssd_d, v_ssd_norm_w, v_w_out, v_final_norm_w):
    weights = dict(norm_w=norm_w, w_in=w_in, conv_a_w=conv_a_w, gla_gate_w=gla_gate_w, gla_gate_b=gla_gate_b,
                   gla_norm_w=gla_norm_w, pool_w=pool_w, pool_scale=pool_scale, ssd_conv_w=ssd_conv_w,
                   ssd_conv_b=ssd_conv_b, ssd_dt_bias=ssd_dt_bias, ssd_a_log=ssd_a_log, ssd_d=ssd_d,
                   ssd_norm_w=ssd_norm_w, w_out=w_out, final_norm_w=final_norm_w)
    m_in = dict(norm_w=m_norm_w, w_in=m_w_in, conv_a_w=m_conv_a_w, gla_gate_w=m_gla_gate_w, gla_gate_b=m_gla_gate_b,
                gla_norm_w=m_gla_norm_w, pool_w=m_pool_w, pool_scale=m_pool_scale, ssd_conv_w=m_ssd_conv_w,
                ssd_conv_b=m_ssd_conv_b, ssd_dt_bias=m_ssd_dt_bias, ssd_a_log=m_ssd_a_log, ssd_d=m_ssd_d,
                ssd_norm_w=m_ssd_norm_w, w_out=m_w_out, final_norm_w=m_final_norm_w)
    v_in = dict(norm_w=v_norm_w, w_in=v_w_in, conv_a_w=v_conv_a_w, gla_gate_w=v_gla_gate_w, gla_gate_b=v_gla_gate_b,
                gla_norm_w=v_gla_norm_w, pool_w=v_pool_w, pool_scale=v_pool_scale, ssd_conv_w=v_ssd_conv_w,
                ssd_conv_b=v_ssd_conv_b, ssd_dt_bias=v_ssd_dt_bias, ssd_a_log=v_ssd_a_log, ssd_d=v_ssd_d,
                ssd_norm_w=v_ssd_norm_w, w_out=v_w_out, final_norm_w=v_final_norm_w)
    order = ("norm_w", "w_in", "conv_a_w", "gla_gate_w", "gla_gate_b", "gla_norm_w", "pool_w", "pool_scale",
             "ssd_conv_w", "ssd_conv_b", "ssd_dt_bias", "ssd_a_log", "ssd_d", "ssd_norm_w", "w_out", "final_norm_w")
    t = x.shape[1]

    comm = _Comm(w_in.astype(BF16), w_out.astype(BF16))
    cshard = jnp.zeros((16, 256), F32)
    for l in range(2):
        cshard = cshard.at[8 * l:8 * l + 3, 0:64].set(conv_a_w[l]).at[8 * l + 3:8 * l + 7, 0:192].set(ssd_conv_w[l])
    s_in, s_out, g_c = _run_rider(comm.gather_ici(0, cshard), "gather_ici0")
    s_in, s_out = _run_rider(_rider_gather_d2d((s_in, s_out)), "gather_d2d0")
    conv_a_full = jnp.stack([jnp.concatenate([g_c[s, 8 * l:8 * l + 3, 0:64] for s in range(4)], axis=-1) for l in range(2)])
    ssd_conv_full = jnp.stack([jnp.concatenate([g_c[s, 8 * l + 3:8 * l + 7, 0:192] for s in range(4)], axis=-1)
                               for l in range(2)])
    consts = [_mixer_consts(l, conv_a_full, gla_gate_w, gla_gate_b, gla_norm_w, pool_w, pool_scale, ssd_conv_full,
                            ssd_conv_b, ssd_dt_bias, ssd_a_log, ssd_d, ssd_norm_w) for l in range(2)]

    head, dx, big, dnw, mgr = _local_step(x.reshape(t, D), loss_target.reshape(t, D), norm_w, final_norm_w, consts,
                                          _layer_weights(s_in, s_out), comm=comm)

    as2d = lambda d: {k: (d[k].reshape(1, D) if k == "final_norm_w" else d[k]) for k in _SMALL_NAMES}
    small = _small_adamw(_small_allreduce(mgr[0], mgr[1], dnw[0], dnw[1], head), as2d(weights), as2d(m_in), as2d(v_in))
    grads, delta, new_m, new_v = ({k: (a.reshape(D) if k == "final_norm_w" else a) for k, a in zip(_SMALL_NAMES, part)}
                                  for part in small[0:4])
    loss = small[4].reshape(())

    grads["w_out"] = jnp.stack(big[1])

    grads["w_in"], delta["w_in"], new_m["w_in"], new_v["w_in"] = _adamw_w_in(w_in, big[0], m_w_in, v_w_in, name="adamw_w_in")
    delta["w_out"], new_m["w_out"], new_v["w_out"] = _adamw(w_out, grads["w_out"], m_w_out, v_w_out, name="adamw_w_out", br=256)

    return (loss, dx.reshape(1, t, D), *[grads[k] for k in order], *[delta[k] for k in order],
            *[new_m[k] for k in order], *[new_v[k] for k in order])
```

```python
import functools

import jax
import jax.numpy as jnp
from jax import lax
from jax.experimental import pallas as pl
from jax.experimental.pallas import tpu as pltpu

F32 = jnp.float32
BF16 = jnp.bfloat16
MESH = pl.DeviceIdType.MESH

D = 1024
CH = 64
EPS = 1e-6
NP = 3456
NPROJ = 3348
GLA_SCALE = 32.0 ** -0.5
INV_TAU = 1.0 / 16.0
TB = 256
NCH = TB // CH
assert TB == 256
HALO_W = NP

C_AH, C_AB, C_AC, C_AZ, C_GQ, C_GK, C_GV = 0, 256, 512, 768, 1024, 1152, 1280
C_GZ, C_PU, C_PZ, C_SZ, C_SX, C_TL = 1536, 1792, 2048, 2304, 2560, 3328
_PERM = ((0, 1536), (1552, 1792), (1536, 16), (3344, 4))
_UNPERM = ((0, 1536), (3328, 16), (1536, 1792), (3344, 4))

R_CAW, R_GB, R_GNW, R_PSC, R_SCB, R_DTB, R_AE, R_DE, R_SNW, R_SCW = 0, 3, 4, 5, 6, 7, 8, 9, 10, 12

ADAM_LR, ADAM_B1, ADAM_B2, ADAM_EPS, ADAM_WD, ADAM_STEP = 0.001, 0.9, 0.999, 1e-08, 0.01, 10

VMEM_LIMIT = 56 * 1024 * 1024


def _cparams(sem, limit=VMEM_LIMIT):
    return pltpu.CompilerParams(dimension_semantics=sem, vmem_limit_bytes=limit)


_ANY = pl.BlockSpec(memory_space=pl.ANY)


def _place():
    return lax.axis_index("x"), lax.axis_index("y"), lax.axis_index("c")


class _Rider:
    def __init__(self, inputs, out_shapes, sems, start, finish, aliases=None):
        self.inputs, self.out_shapes, self.sems = tuple(inputs), tuple(out_shapes), tuple(sems)
        self.start, self.finish, self.aliases = start, finish, dict(aliases or {})


def _call(body, args, *, grid, in_specs, out_specs, out_shape, name, sem, scratch_shapes=(), rider=None):
    if rider is None:
        outs = pl.pallas_call(body, grid=grid, name=name, in_specs=list(in_specs), out_specs=list(out_specs),
                              out_shape=list(out_shape), scratch_shapes=list(scratch_shapes),
                              compiler_params=_cparams(sem))(*args)
        return list(outs), []
    ni, no, ns = len(args), len(out_shape), len(scratch_shapes)
    ri, ro = len(rider.inputs), len(rider.out_shapes)

    def full(*refs):
        ins, rins = refs[:ni], refs[ni:ni + ri]
        outs, routs = refs[ni + ri:ni + ri + no], refs[ni + ri + no:ni + ri + no + ro]
        scr, rsem = refs[ni + ri + no + ro:ni + ri + no + ro + ns], refs[ni + ri + no + ro + ns:]
        first = functools.reduce(jnp.logical_and, [pl.program_id(a) == 0 for a in range(len(grid))])
        last = functools.reduce(jnp.logical_and, [pl.program_id(a) == grid[a] - 1 for a in range(len(grid))])

        @pl.when(first)
        def _():
            rider.start(rins, routs, rsem)

        body(*ins, *outs, *scr)

        @pl.when(last)
        def _():
            rider.finish(rins, routs, rsem)

    outs = pl.pallas_call(
        full, grid=grid, name=name, in_specs=list(in_specs) + [_ANY] * ri, out_specs=list(out_specs) + [_ANY] * ro,
        out_shape=list(out_shape) + list(rider.out_shapes), scratch_shapes=list(scratch_shapes) + list(rider.sems),
        input_output_aliases={ni + k: no + v for k, v in rider.aliases.items()},
        compiler_params=_cparams(("arbitrary",) * len(grid)))(*args, *rider.inputs)
    return list(outs[:no]), list(outs[no:])


def _run_rider(rider, name):
    ri = len(rider.inputs)

    def body(*refs):
        rins, routs, rsem = refs[:ri], refs[ri:ri + len(rider.out_shapes)], refs[ri + len(rider.out_shapes):]
        rider.start(rins, routs, rsem)
        rider.finish(rins, routs, rsem)

    return list(pl.pallas_call(body, name=name, in_specs=[_ANY] * ri, out_specs=[_ANY] * len(rider.out_shapes),
                               out_shape=list(rider.out_shapes), scratch_shapes=list(rider.sems),
                               input_output_aliases=dict(rider.aliases))(*rider.inputs))


def _dot(a, b):
    return jnp.dot(a.astype(BF16), b.astype(BF16), preferred_element_type=F32)


def _dot_nt(a, b):
    return lax.dot_general(a.astype(BF16), b.astype(BF16), (((1,), (1,)), ((), ())), preferred_element_type=F32)


def _dot_tn(a, b):
    return lax.dot_general(a.astype(BF16), b.astype(BF16), (((0,), (0,)), ((), ())), preferred_element_type=F32)


def _split(a):
    hi = a.astype(BF16)
    lo = (a - hi.astype(F32)).astype(BF16)
    return hi, lo


def _dot2_l(a, b):
    hi, lo = _split(a)
    return _dot(hi, b) + _dot(lo, b)


def _dot2_r(a, b):
    hi, lo = _split(b)
    return _dot(a, hi) + _dot(a, lo)


def _dot3_l(a, b):
    hi, lo = _split(a)
    lo2 = ((a - hi.astype(F32)) - lo.astype(F32)).astype(BF16)
    return _dot(hi, b) + _dot(lo, b) + _dot(lo2, b)


def _dot2_nt(a, b):
    hi, lo = _split(a)
    return _dot_nt(hi, b) + _dot_nt(lo, b)


def _silu(z):
    return z * jax.nn.sigmoid(z)


def _lse1(x):
    return jnp.log(1.0 + jnp.exp(-jnp.abs(x)))


def _cs(a):
    return jnp.sum(a, axis=0, keepdims=True)


def _iota(shape, dim):
    return lax.broadcasted_iota(jnp.int32, shape, dim)


def _mixer_matrices():
    r, c = _iota((256, 256), 0), _iota((256, 256), 1)
    same_chunk = (r >> 6) == (c >> 6)
    mats = jnp.stack([jnp.where((c > r) & same_chunk, 1.0, 0.0), jnp.where((c < r) & same_chunk, 1.0, 0.0),
                      jnp.where(same_chunk, 1.0 / 64.0, 0.0), jnp.where((r < 128) & (r - 16 == (c >> 6)), 1.0, 0.0)])
    mask = jnp.where((_iota((256, 128), 0) >> 6) == (_iota((256, 128), 1) >> 5), 1.0, 0.0)
    return mats.astype(BF16), mask.astype(F32)


def _dn(ext, k, n, h):
    return pltpu.roll(ext, k, axis=0)[h:h + n]


def _up(ext, k, n):
    return pltpu.roll(ext, ext.shape[0] - k, axis=0)[:n]


def _pool_lane_select(lane, s2, s4, s8, s16):
    return jnp.where(lane < 64, s2, jnp.where(lane < 128, s4, jnp.where(lane < 192, s8, s16)))


def _winsum_dn(ext, lane):
    s2 = ext + pltpu.roll(ext, 1, axis=0)
    s4 = s2 + pltpu.roll(s2, 2, axis=0)
    s8 = s4 + pltpu.roll(s4, 4, axis=0)
    s16 = s8 + pltpu.roll(s8, 8, axis=0)
    return _pool_lane_select(lane, s2, s4, s8, s16)


def _winsum_up(ext, lane):
    m = ext.shape[0]
    s2 = ext + pltpu.roll(ext, m - 1, axis=0)
    s4 = s2 + pltpu.roll(s2, m - 2, axis=0)
    s8 = s4 + pltpu.roll(s4, m - 4, axis=0)
    s16 = s8 + pltpu.roll(s8, m - 8, axis=0)
    return _pool_lane_select(lane, s2, s4, s8, s16)


def _pool_inv_count(tile, n):
    lane = _iota((1, 256), 1)
    win = _pool_lane_select(lane, 2.0, 4.0, 8.0, 16.0).astype(F32)
    tpos = (tile * n + _iota((n, 1), 0) + 1).astype(F32)
    return jnp.where(tpos >= win, 1.0 / win, 1.0 / tpos)


def _silu_pair(z):
    s = jax.nn.sigmoid(z)
    return z * s, s * (1.0 + z * (1.0 - s))


def _chunks(a):
    return [a[c * CH:(c + 1) * CH] for c in range(a.shape[0] // CH)]


def _halves(fn, a, b):
    return jnp.concatenate([fn(a[:, 0:128], b[:, 0:128]), fn(a[:, 128:256], b[:, 128:256])], axis=1)


def _mixer_tile_prep(p_ref, xc, prm_ref, gw_v, cm_ref, mk_ref):
    tail = p_ref[:, C_TL:C_TL + 128]
    pre = _dot(tail, gw_v) + prm_ref[R_GB:R_GB + 1, 0:128]
    la = (jnp.minimum(pre, 0.0) - _lse1(pre)) * INV_TAU
    dtin = tail + prm_ref[R_DTB:R_DTB + 1, 0:128]
    dtf = jnp.maximum(dtin, 0.0) + _lse1(dtin)
    dte = _dot2_l(dtf, cm_ref[3, 0:128, :])
    da = dte * prm_ref[R_AE:R_AE + 1, 0:256]
    rev = _dot2_r(cm_ref[0], jnp.concatenate([la, da], axis=1))
    dec = jnp.exp(rev[:, 0:128])
    kd = p_ref[:, C_GK:C_GK + 128] * dec
    wdec = jnp.exp(rev[:, 128:384])
    w = wdec * dte
    xw = xc[:, 0:256] * w
    d_s = [jnp.exp(_cs(a)) for a in _chunks(la)]
    et = [jnp.exp(_cs(a)) for a in _chunks(da)]
    mask_t = mk_ref[...]
    ut_g = [_dot_tn(v, k) * mask_t for v, k in zip(_chunks(p_ref[:, C_GV:C_GV + 256]), _chunks(kd))]
    ut_s = [_halves(_dot_tn, b, x) for b, x in zip(_chunks(xc[:, 256:512]), _chunks(xw))]
    return tail, pre, dtin, dte, dec, kd, wdec, w, xw, d_s, et, ut_g, ut_s


def _rmsproj(x, nw, wp, name, tm=512, rider=None):
    t = x.shape[0]

    def body(x_ref, nw_ref, w_ref, o_ref, h_ref):
        xv = x_ref[...]
        rs = lax.rsqrt(jnp.mean(xv * xv, axis=-1, keepdims=True) + EPS)
        h = (xv * rs * nw_ref[...]).astype(BF16)
        h_ref[...] = h
        o_ref[...] = jnp.dot(h, w_ref[...], preferred_element_type=F32)

    (proj, h), extra = _call(
        body, (x, nw, wp), grid=(t // tm,), name=name, sem=("parallel",), rider=rider,
        in_specs=[pl.BlockSpec((tm, D), lambda i: (i, 0)), pl.BlockSpec((1, D), lambda i: (0, 0)),
                  pl.BlockSpec((D, NP), lambda i: (0, 0))],
        out_specs=[pl.BlockSpec((tm, NP), lambda i: (i, 0)), pl.BlockSpec((tm, D), lambda i: (i, 0))],
        out_shape=[jax.ShapeDtypeStruct((t, NP), F32), jax.ShapeDtypeStruct((t, D), BF16)])
    return proj, h, extra


def _head(x, tgt, fw, name, tm=512):
    t = x.shape[0]

    def body(x_ref, t_ref, w_ref, dx_ref, acc_ref):
        @pl.when(pl.program_id(0) == 0)
        def _():
            acc_ref[...] = jnp.zeros_like(acc_ref)

        xv = x_ref[...]
        w = w_ref[...]
        rs = lax.rsqrt(jnp.mean(xv * xv, axis=-1, keepdims=True) + EPS)
        xh = xv * rs
        err = xh * w - t_ref[...]
        dy = err * (1.0 / D)
        dxh = dy * w
        dx_ref[...] = rs * (dxh - xh * jnp.mean(dxh * xh, axis=-1, keepdims=True))
        acc_ref[0:1, :] += _cs(dy * xh)
        acc_ref[1:2, :] += jnp.zeros((1, D), F32) + (0.5 / D) * jnp.sum(err * err)

    return pl.pallas_call(
        body, grid=(t // tm,), name=name,
        in_specs=[pl.BlockSpec((tm, D), lambda i: (i, 0)), pl.BlockSpec((tm, D), lambda i: (i, 0)),
                  pl.BlockSpec((1, D), lambda i: (0, 0))],
        out_specs=[pl.BlockSpec((tm, D), lambda i: (i, 0)), pl.BlockSpec((8, D), lambda i: (0, 0))],
        out_shape=[jax.ShapeDtypeStruct((t, D), F32), jax.ShapeDtypeStruct((8, D), F32)],
        compiler_params=_cparams(("arbitrary",)),
    )(x, tgt, fw)


def _dxin(dp, wpt, x, dxn, nw, name, tm=512, rider=None):
    t = x.shape[0]

    def body(dp_ref, w_ref, x_ref, dxn_ref, nw_ref, dx_ref, dnw_ref):
        @pl.when(pl.program_id(0) == 0)
        def _():
            dnw_ref[...] = jnp.zeros_like(dnw_ref)

        dh = jnp.dot(dp_ref[...].astype(BF16), w_ref[...], preferred_element_type=F32)
        xv = x_ref[...]
        rs = lax.rsqrt(jnp.mean(xv * xv, axis=-1, keepdims=True) + EPS)
        xh = xv * rs
        dnw_ref[0:1, :] += _cs(dh * xh)
        dxh = dh * nw_ref[...]
        dx_ref[...] = dxn_ref[...] + rs * (dxh - xh * jnp.mean(dxh * xh, axis=-1, keepdims=True))

    return _call(
        body, (dp, wpt, x, dxn, nw), grid=(t // tm,), name=name, sem=("arbitrary",), rider=rider,
        in_specs=[pl.BlockSpec((tm, NP), lambda i: (i, 0)), pl.BlockSpec((NP, D), lambda i: (0, 0)),
                  pl.BlockSpec((tm, D), lambda i: (i, 0)), pl.BlockSpec((tm, D), lambda i: (i, 0)),
                  pl.BlockSpec((1, D), lambda i: (0, 0))],
        out_specs=[pl.BlockSpec((tm, D), lambda i: (i, 0)), pl.BlockSpec((8, D), lambda i: (0, 0))],
        out_shape=[jax.ShapeDtypeStruct((t, D), F32), jax.ShapeDtypeStruct((8, D), F32)])


def _dwin(h, dp, name, tm=512, tn=NP, rider=None):
    t = h.shape[0]

    def body(h_ref, dp_ref, o_ref):
        @pl.when(pl.program_id(1) == 0)
        def _():
            o_ref[...] = jnp.zeros_like(o_ref)

        o_ref[...] += _dot_tn(h_ref[...], dp_ref[...])

    (dwp,), extra = _call(
        body, (h, dp), grid=(NP // tn, t // tm), name=name, sem=("parallel", "arbitrary"), rider=rider,
        in_specs=[pl.BlockSpec((tm, D), lambda j, i: (i, 0)), pl.BlockSpec((tm, tn), lambda j, i: (i, j))],
        out_specs=[pl.BlockSpec((D, tn), lambda j, i: (0, j))], out_shape=[jax.ShapeDtypeStruct((D, NP), F32)])
    return dwp, extra


def _mixer_fwd(proj, x, wo, prm, gw, pw, cmat, mask, name, rider=None):
    t = proj.shape[0]
    nt, nc = t // TB, t // CH

    def body(p_ref, x_ref, wo_ref, prm_ref, gw_ref, pw_ref, cm_ref, mk_ref, mix_ref, sg_ref, ss_ref, xn_ref,
             sg_s, ss_s, h_ua, h_pu, h_sx):
        i = pl.program_id(0)

        @pl.when(i == 0)
        def _():
            for r in (sg_s, ss_s, h_ua, h_pu, h_sx):
                r[...] = jnp.zeros_like(r)

        lane = _iota((1, 256), 1)
        u = p_ref[:, C_AC:C_AC + 256] * p_ref[:, C_AH:C_AH + 256]
        ext = jnp.concatenate([h_ua[...], u], axis=0)
        cv = (prm_ref[R_CAW + 2:R_CAW + 3, 0:256] * u + prm_ref[R_CAW + 1:R_CAW + 2, 0:256] * _dn(ext, 1, TB, 8)
              + prm_ref[R_CAW:R_CAW + 1, 0:256] * _dn(ext, 2, TB, 8))
        mix_ref[:, 0:256] = (p_ref[:, C_AB:C_AB + 256] * cv * _silu(p_ref[:, C_AZ:C_AZ + 256])).astype(BF16)
        h_ua[...] = u[TB - 8:, :]
        pu = p_ref[:, C_PU:C_PU + 256]
        ext = jnp.concatenate([h_pu[...], pu], axis=0)
        pooled = _winsum_dn(ext, lane)[16:] * _pool_inv_count(i, TB) - pu
        mixed = _dot(pooled, pw_ref[...])
        mix_ref[:, 512:768] = (prm_ref[R_PSC:R_PSC + 1, 0:256] * mixed * _silu(p_ref[:, C_PZ:C_PZ + 256])).astype(BF16)
        h_pu[...] = pu[TB - 16:, :]
        sx = p_ref[:, C_SX:C_SX + 768]
        ext = jnp.concatenate([h_sx[...], sx], axis=0)
        xc = _silu(prm_ref[R_SCW + 3:R_SCW + 4, :] * sx + prm_ref[R_SCW + 2:R_SCW + 3, :] * _dn(ext, 1, TB, 8)
                   + prm_ref[R_SCW + 1:R_SCW + 2, :] * _dn(ext, 2, TB, 8) + prm_ref[R_SCW:R_SCW + 1, :] * _dn(ext, 3, TB, 8)
                   + prm_ref[R_SCB:R_SCB + 1, :])
        h_sx[...] = sx[TB - 8:, :]

        _, _, _, _, _, _, _, _, _, d_s, et, ut_g, ut_s = _mixer_tile_prep(p_ref, xc, prm_ref, gw_ref[...], cm_ref, mk_ref)
        s_g, s_s = sg_s[...], ss_s[...]
        o, y = [], []
        qs = _chunks(p_ref[:, C_GQ:C_GQ + 128] * GLA_SCALE)
        cm = _chunks(xc[:, 512:768])
        for c in range(NCH):
            sg_ref[c] = s_g
            ss_ref[c] = s_s
            s_g = s_g * d_s[c] + ut_g[c]
            s_s = s_s * et[c] + ut_s[c]
            o.append(_dot_nt(qs[c], s_g))
            y.append(_halves(_dot, cm[c], s_s))
        sg_s[...] = s_g
        ss_s[...] = s_s
        o = jnp.concatenate(o, axis=0)
        on = o * lax.rsqrt(_dot2_l(o * o, cm_ref[2]) + EPS)
        mix_ref[:, 256:512] = (on * prm_ref[R_GNW:R_GNW + 1, 0:256] * _silu(p_ref[:, C_GZ:C_GZ + 256])).astype(BF16)
        y2 = ((jnp.concatenate(y, axis=0) + prm_ref[R_DE:R_DE + 1, 0:256] * xc[:, 0:256])
              * _silu(p_ref[:, C_SZ:C_SZ + 256]))
        mix_ref[:, 768:1024] = (y2 * lax.rsqrt(jnp.mean(y2 * y2, axis=-1, keepdims=True) + EPS)
                                * prm_ref[R_SNW:R_SNW + 1, 0:256]).astype(BF16)
        xn_ref[...] = x_ref[...] + jnp.dot(mix_ref[...], wo_ref[...], preferred_element_type=F32)

    return _call(
        body, (proj, x, wo, prm, gw, pw, cmat, mask), grid=(nt,), name=name, sem=("arbitrary",), rider=rider,
        in_specs=[pl.BlockSpec((TB, NP), lambda i: (i, 0)), pl.BlockSpec((TB, D), lambda i: (i, 0)),
                  pl.BlockSpec((D, D), lambda i: (0, 0)), pl.BlockSpec((16, 768), lambda i: (0, 0)),
                  pl.BlockSpec((128, 128), lambda i: (0, 0)), pl.BlockSpec((256, 256), lambda i: (0, 0)),
                  pl.BlockSpec((4, 256, 256), lambda i: (0, 0, 0)), pl.BlockSpec((256, 128), lambda i: (0, 0))],
        out_specs=[pl.BlockSpec((TB, D), lambda i: (i, 0)), pl.BlockSpec((NCH, 256, 128), lambda i: (i, 0, 0)),
                   pl.BlockSpec((NCH, 128, 256), lambda i: (i, 0, 0)), pl.BlockSpec((TB, D), lambda i: (i, 0))],
        out_shape=[jax.ShapeDtypeStruct((t, D), BF16), jax.ShapeDtypeStruct((nc, 256, 128), F32),
                   jax.ShapeDtypeStruct((nc, 128, 256), F32), jax.ShapeDtypeStruct((t, D), F32)],
        scratch_shapes=[pltpu.VMEM((256, 128), F32), pltpu.VMEM((128, 256), F32), pltpu.VMEM((8, 256), F32),
                        pltpu.VMEM((16, 256), F32), pltpu.VMEM((8, 768), F32)])


def _mixer_bwd(proj, dxn, wot, mix, sg, ss, prm, gw, pw, cmat, mask, name, rider=None):
    t = proj.shape[0]
    nt = t // TB
    rev = lambda i: nt - 1 - i

    def body(p_ref, hp_ref, dxn_ref, wot_ref, mix_ref, sg_ref, ss_ref, prm_ref, gw_ref, pw_ref, cm_ref, mk_ref,
             dp_ref, sgc_ref, dwo_ref,
             gg_s, gs_s, h_dcv, h_dpl, h_dpre, gsm_ref, dgw_ref, dpw_ref, dm_ref):
        i = pl.program_id(0)
        tile = nt - 1 - i

        @pl.when(i == 0)
        def _():
            for r in (gg_s, gs_s, h_dcv, h_dpl, h_dpre, gsm_ref, dgw_ref, dpw_ref, dwo_ref):
                r[...] = jnp.zeros_like(r)

        dxn = dxn_ref[...].astype(BF16)
        dm_ref[...] = jnp.dot(dxn, wot_ref[...], preferred_element_type=F32)
        dwo_ref[...] += _dot_tn(mix_ref[...], dxn)

        lane = _iota((1, 256), 1)
        first = (tile > 0).astype(F32)
        ah, ac = p_ref[:, C_AH:C_AH + 256], p_ref[:, C_AC:C_AC + 256]
        ab, az = p_ref[:, C_AB:C_AB + 256], p_ref[:, C_AZ:C_AZ + 256]
        w0, w1, w2 = (prm_ref[R_CAW + j:R_CAW + j + 1, 0:256] for j in range(3))
        u = ac * ah
        ext = jnp.concatenate([hp_ref[8:16, C_AC:C_AC + 256] * hp_ref[8:16, C_AH:C_AH + 256] * first, u], axis=0)
        u1, u2 = _dn(ext, 1, TB, 8), _dn(ext, 2, TB, 8)
        cv = w2 * u + w1 * u1 + w0 * u2
        g = dm_ref[:, 0:256]
        sz, dsz = _silu_pair(az)
        dp_ref[:, C_AB:C_AB + 256] = (g * cv * sz).astype(BF16)
        dp_ref[:, C_AZ:C_AZ + 256] = (g * ab * cv * dsz).astype(BF16)
        dcv = g * ab * sz
        dext = jnp.concatenate([dcv, h_dcv[...]], axis=0)
        du = w2 * dcv + w1 * _up(dext, 1, TB) + w0 * _up(dext, 2, TB)
        dp_ref[:, C_AC:C_AC + 256] = (du * ah).astype(BF16)
        dp_ref[:, C_AH:C_AH + 256] = (du * ac).astype(BF16)
        gsm_ref[R_CAW:R_CAW + 1, 0:256] += _cs(dcv * u2)
        gsm_ref[R_CAW + 1:R_CAW + 2, 0:256] += _cs(dcv * u1)
        gsm_ref[R_CAW + 2:R_CAW + 3, 0:256] += _cs(dcv * u)
        h_dcv[...] = dcv[0:8, :]
        pu, pz = p_ref[:, C_PU:C_PU + 256], p_ref[:, C_PZ:C_PZ + 256]
        psc = prm_ref[R_PSC:R_PSC + 1, 0:256]
        icnt = _pool_inv_count(tile, TB)
        ext = jnp.concatenate([hp_ref[:, C_PU:C_PU + 256] * first, pu], axis=0)
        pooled = _winsum_dn(ext, lane)[16:] * icnt - pu
        pw_v = pw_ref[...]
        mixed = _dot(pooled, pw_v)
        g = dm_ref[:, 512:768]
        sz, dsz = _silu_pair(pz)
        gsm_ref[R_PSC:R_PSC + 1, 0:256] += _cs(g * mixed * sz)
        dp_ref[:, C_PZ:C_PZ + 256] = (g * psc * mixed * dsz).astype(BF16)
        dmixed = g * psc * sz
        dpw_ref[...] += _dot_tn(pooled, dmixed)
        dpooled = _dot_nt(dmixed, pw_v)
        qd = dpooled * icnt
        dext = jnp.concatenate([qd, h_dpl[...]], axis=0)
        dp_ref[:, C_PU:C_PU + 256] = (_winsum_up(dext, lane)[:TB] - dpooled).astype(BF16)
        h_dpl[...] = qd[0:16, :]
        sx = p_ref[:, C_SX:C_SX + 768]
        cw = [prm_ref[R_SCW + j:R_SCW + j + 1, :] for j in range(4)]
        ext = jnp.concatenate([hp_ref[8:16, C_SX:C_SX + 768] * first, sx], axis=0)
        sx1, sx2, sx3 = _dn(ext, 1, TB, 8), _dn(ext, 2, TB, 8), _dn(ext, 3, TB, 8)
        cpre = cw[3] * sx + cw[2] * sx1 + cw[1] * sx2 + cw[0] * sx3 + prm_ref[R_SCB:R_SCB + 1, :]
        xc, dxc = _silu_pair(cpre)
        xs, bm, cm = xc[:, 0:256], xc[:, 256:512], xc[:, 512:768]

        gw_v = gw_ref[...]
        tail, pre, dtin, dte, dec, kd, wdec, w, xw, d_s, et, ut_g, ut_s = _mixer_tile_prep(p_ref, xc, prm_ref, gw_v,
                                                                                          cm_ref, mk_ref)
        gmean = cm_ref[2]
        mask_t = mk_ref[...]
        gnw = prm_ref[R_GNW:R_GNW + 1, 0:256]
        a_e = prm_ref[R_AE:R_AE + 1, 0:256]
        d_e = prm_ref[R_DE:R_DE + 1, 0:256]
        snw = prm_ref[R_SNW:R_SNW + 1, 0:256]
        sg_in = [sg_ref[c] for c in range(NCH)]
        ss_in = [ss_ref[c] for c in range(NCH)]
        sg_n = [sg_in[c] * d_s[c] + ut_g[c] for c in range(NCH)]
        ss_n = [ss_in[c] * et[c] + ut_s[c] for c in range(NCH)]
        qs = _chunks(p_ref[:, C_GQ:C_GQ + 128] * GLA_SCALE)
        cm_c, bm_c, xw_c, kd_c = _chunks(cm), _chunks(bm), _chunks(xw), _chunks(kd)
        v_c = _chunks(p_ref[:, C_GV:C_GV + 256])
        o = jnp.concatenate([_dot_nt(qs[c], sg_n[c]) for c in range(NCH)], axis=0)
        y = jnp.concatenate([_halves(_dot, cm_c[c], ss_n[c]) for c in range(NCH)], axis=0) + d_e * xs
        gz = p_ref[:, C_GZ:C_GZ + 256]
        r = lax.rsqrt(_dot2_l(o * o, gmean) + EPS)
        on = o * r
        dyb = dm_ref[:, 256:512]
        sz, dsz = _silu_pair(gz)
        dp_ref[:, C_GZ:C_GZ + 256] = (dyb * on * gnw * dsz).astype(BF16)
        tg = dyb * sz
        gsm_ref[R_GNW:R_GNW + 1, 0:256] += _cs(tg * on)
        don = tg * gnw
        do_c = _chunks(r * (don - on * _dot2_l(don * on, gmean)))
        ssz = p_ref[:, C_SZ:C_SZ + 256]
        sil, dsil = _silu_pair(ssz)
        y2 = y * sil
        r = lax.rsqrt(jnp.mean(y2 * y2, axis=-1, keepdims=True) + EPS)
        yn = y2 * r
        dyd = dm_ref[:, 768:1024]
        gsm_ref[R_SNW:R_SNW + 1, 0:256] += _cs(dyd * yn)
        dn = dyd * snw
        dy2 = r * (dn - yn * jnp.mean(dn * yn, axis=-1, keepdims=True))
        dp_ref[:, C_SZ:C_SZ + 256] = (dy2 * y * dsil).astype(BF16)
        dy = dy2 * sil
        gsm_ref[R_DE:R_DE + 1, 0:256] += _cs(dy * xs)
        dy_c = _chunks(dy)
        dq = jnp.concatenate([_dot(do_c[c], sg_n[c]) for c in range(NCH)], axis=0)
        dp_ref[:, C_GQ:C_GQ + 128] = (dq * GLA_SCALE).astype(BF16)
        dcm = jnp.concatenate([_halves(_dot_nt, dy_c[c], ss_n[c]) for c in range(NCH)], axis=0)
        gg = [_dot_tn(do_c[c], qs[c]) * mask_t for c in range(NCH)]
        gs = [_halves(_dot_tn, cm_c[c], dy_c[c]) for c in range(NCH)]
        car_g, car_s = gg_s[...], gs_s[...]
        for c in reversed(range(NCH)):
            gg[c] = gg[c] + car_g
            gs[c] = gs[c] + car_s
            car_g = gg[c] * d_s[c]
            car_s = gs[c] * et[c]
        gg_s[...] = car_g
        gs_s[...] = car_s
        dkd = jnp.concatenate([_dot(v_c[c], gg[c]) for c in range(NCH)], axis=0)
        dp_ref[:, C_GV:C_GV + 256] = jnp.concatenate([_dot_nt(kd_c[c], gg[c]) for c in range(NCH)], axis=0).astype(BF16)
        dp_ref[:, C_GK:C_GK + 128] = (dkd * dec).astype(BF16)
        dbm = jnp.concatenate([_halves(_dot_nt, xw_c[c], gs[c]) for c in range(NCH)], axis=0)
        dxw = jnp.concatenate([_halves(_dot, bm_c[c], gs[c]) for c in range(NCH)], axis=0)
        dxs = dy * d_e + dxw * w
        dw = dxw * xs
        dsuf = _dot2_r(cm_ref[1], jnp.concatenate([dkd * kd, dw * dte * wdec], axis=1))
        tot_g = jnp.concatenate([jnp.broadcast_to(_cs(gg[c] * sg_in[c]) * d_s[c], (CH, 128)) for c in range(NCH)], axis=0)
        tot_s = jnp.concatenate([jnp.broadcast_to(_cs(gs[c] * ss_in[c]) * et[c], (CH, 256)) for c in range(NCH)], axis=0)
        dpre = (dsuf[:, 0:128] + tot_g) * INV_TAU * jax.nn.sigmoid(-pre)
        dgw_ref[...] += _dot_tn(tail, dpre)
        gsm_ref[R_GB:R_GB + 1, 0:128] += _cs(dpre)
        dda = dsuf[:, 128:384] + tot_s
        gsm_ref[R_AE:R_AE + 1, 0:256] += _cs(dda * dte)
        dtail_s = _dot2_nt(dw * wdec + dda * a_e, cm_ref[3, 0:128, :]) * jax.nn.sigmoid(dtin)
        gsm_ref[R_DTB:R_DTB + 1, 0:128] += _cs(dtail_s)
        dp_ref[:, C_TL:C_TL + 128] = (_dot_nt(dpre, gw_v) + dtail_s).astype(BF16)
        dpre_c = jnp.concatenate([dxs, dbm, dcm], axis=1) * dxc
        dext = jnp.concatenate([dpre_c, h_dpre[...]], axis=0)
        dp_ref[:, C_SX:C_SX + 768] = (cw[3] * dpre_c + cw[2] * _up(dext, 1, TB) + cw[1] * _up(dext, 2, TB)
                                      + cw[0] * _up(dext, 3, TB)).astype(BF16)
        gsm_ref[R_SCW + 3:R_SCW + 4, :] += _cs(dpre_c * sx)
        gsm_ref[R_SCW + 2:R_SCW + 3, :] += _cs(dpre_c * sx1)
        gsm_ref[R_SCW + 1:R_SCW + 2, :] += _cs(dpre_c * sx2)
        gsm_ref[R_SCW:R_SCW + 1, :] += _cs(dpre_c * sx3)
        gsm_ref[R_SCB:R_SCB + 1, :] += _cs(dpre_c)
        h_dpre[...] = dpre_c[0:8, :]

        @pl.when(i == nt - 1)
        def _():
            ri, ci = _iota((256, 256), 0), _iota((256, 256), 1)
            per_head = jnp.where((ri >> 6) == ci, 1.0, 0.0).astype(BF16)
            per_dv = jnp.where((ri & 63) == ci, 1.0, 0.0).astype(BF16)
            row = _iota((8, 256), 0)
            top = gsm_ref[0:8, 0:256]
            sgc_ref[0:8, 0:256] = jnp.where(row == R_GNW, _dot3_l(top, per_dv), top)
            bot = gsm_ref[8:16, 0:256]
            fold = _dot3_l(jnp.where(row == R_AE - 8, bot * a_e, bot), per_head)
            sgc_ref[8:16, 0:256] = jnp.where((row == R_AE - 8) | (row == R_DE - 8), fold, bot)
            sgc_ref[0:16, 256:768] = gsm_ref[:, 256:768]
            sgc_ref[0:16, 768:896] = dgw_ref[0:16, :]
            sgc_ref[0:16, 896:1024] = jnp.zeros((16, 128), F32)
            diag = _pool_lane_select(lane, dpw_ref[0:64, :], dpw_ref[64:128, :], dpw_ref[128:192, :], dpw_ref[192:256, :])
            for q in range(4):
                sgc_ref[16:32, 256 * q:256 * q + 256] = diag[16 * q:16 * q + 16, :]

    return _call(
        body, (proj, proj, dxn, wot, mix, sg, ss, prm, gw, pw, cmat, mask), grid=(nt,), name=name, sem=("arbitrary",),
        rider=rider,
        in_specs=[pl.BlockSpec((TB, NP), lambda i: (rev(i), 0)),
                  pl.BlockSpec((16, HALO_W), lambda i: (jnp.maximum(rev(i) * (TB // 16) - 1, 0), 0)),
                  pl.BlockSpec((TB, D), lambda i: (rev(i), 0)), pl.BlockSpec((D, D), lambda i: (0, 0)),
                  pl.BlockSpec((TB, D), lambda i: (rev(i), 0)),
                  pl.BlockSpec((NCH, 256, 128), lambda i: (rev(i), 0, 0)),
                  pl.BlockSpec((NCH, 128, 256), lambda i: (rev(i), 0, 0)),
                  pl.BlockSpec((16, 768), lambda i: (0, 0)), pl.BlockSpec((128, 128), lambda i: (0, 0)),
                  pl.BlockSpec((256, 256), lambda i: (0, 0)), pl.BlockSpec((4, 256, 256), lambda i: (0, 0, 0)),
                  pl.BlockSpec((256, 128), lambda i: (0, 0))],
        out_specs=[pl.BlockSpec((TB, NP), lambda i: (rev(i), 0)), pl.BlockSpec((32, 1024), lambda i: (0, 0)),
                   pl.BlockSpec((D, D), lambda i: (0, 0))],
        out_shape=[jax.ShapeDtypeStruct((t, NP), BF16), jax.ShapeDtypeStruct((32, 1024), F32),
                   jax.ShapeDtypeStruct((D, D), F32)],
        scratch_shapes=[pltpu.VMEM((256, 128), F32), pltpu.VMEM((128, 256), F32), pltpu.VMEM((8, 256), F32),
                        pltpu.VMEM((16, 256), F32), pltpu.VMEM((8, 768), F32), pltpu.VMEM((16, 768), F32),
                        pltpu.VMEM((128, 128), F32), pltpu.VMEM((256, 256), F32), pltpu.VMEM((TB, D), F32)])


def _half(c, n):
    return pl.ds(pl.multiple_of(c * (n // 2), n // 2), n // 2)


def _other_chips(x, y):
    return ((1 - x, y), (x, 1 - y), (1 - x, 1 - y))


def _remote(src, dst, send, recv, k, dev):
    return pltpu.make_async_remote_copy(src_ref=src, dst_ref=dst, send_sem=send.at[k], recv_sem=recv.at[k], device_id=dev,
                                        device_id_type=MESH)


def _sem(n):
    return pltpu.SemaphoreType.DMA((n,))


def _rider_gather_ici(shards, extra=None):
    shards = tuple(shards) + ((extra,) if extra is not None else ())
    n = len(shards)

    def copies(rins, routs, sems, arrivals=True):
        send, recv = sems
        x, y, c = _place()
        me = 2 * x + y
        out, inc = [], []
        for j, (px, py) in enumerate(_other_chips(x, y)):
            for k in range(n):
                whole = extra is not None and k == n - 1
                rows = pl.ds(0, shards[k].shape[0]) if whole else _half(c, shards[k].shape[0])
                out.append(_remote(rins[k].at[rows], routs[k].at[me, rows], send, recv, n * j + k, (px, py, c)))
                if arrivals:
                    inc.append(_remote(rins[k].at[rows], routs[k].at[2 * px + py, rows], send, recv, n * j + k, (px, py, c)))
        return out, inc

    def start(rins, routs, sems):
        for cp in copies(rins, routs, sems, arrivals=False)[0]:
            cp.start()

    def finish(rins, routs, sems):
        out, inc = copies(rins, routs, sems)
        for cp in inc:
            cp.wait_recv()
        for cp in out:
            cp.wait_send()

    return _Rider(shards, [jax.ShapeDtypeStruct((4,) + a.shape, a.dtype) for a in shards], [_sem(3 * n), _sem(3 * n)],
                  start, finish)


def _rider_gather_d2d(slabs):
    slabs = tuple(slabs)
    n = len(slabs)

    def copies(routs, sems, arrivals=True):
        send, recv = sems
        x, y, c = _place()
        out, inc = [], []
        for j, (px, py) in enumerate(_other_chips(x, y)):
            for k in range(n):
                rows = slabs[k].shape[1]
                mine, theirs = routs[k].at[2 * px + py, _half(c, rows)], routs[k].at[2 * px + py, _half(1 - c, rows)]
                out.append(_remote(mine, mine, send, recv, n * j + k, (x, y, 1 - c)))
                if arrivals:
                    inc.append(_remote(theirs, theirs, send, recv, n * j + k, (x, y, 1 - c)))
        return out, inc

    def start(rins, routs, sems):
        for cp in copies(routs, sems, arrivals=False)[0]:
            cp.start()

    def finish(rins, routs, sems):
        out, inc = copies(routs, sems)
        for cp in inc:
            cp.wait_recv()
        for cp in out:
            cp.wait_send()

    return _Rider(slabs, [jax.ShapeDtypeStruct(a.shape, a.dtype) for a in slabs], [_sem(3 * n), _sem(3 * n)], start, finish,
                  aliases={k: k for k in range(n)})


def _rider_swap(parts):
    parts = tuple(parts)
    n = len(parts)

    def copies(rins, routs, sems):
        send, recv = sems
        x, y, c = _place()
        return [_remote(rins[k].at[:, _half(1 - c, parts[k].shape[1])], routs[k], send, recv, k, (x, y, 1 - c))
                for k in range(n)]

    def start(rins, routs, sems):
        for cp in copies(rins, routs, sems):
            cp.start()

    def finish(rins, routs, sems):
        for cp in copies(rins, routs, sems):
            cp.wait()

    return _Rider(parts, [jax.ShapeDtypeStruct((4, a.shape[1] // 2, a.shape[2]), a.dtype) for a in parts],
                  [_sem(n), _sem(n)], start, finish)


def _rider_scatter(parts):
    parts = tuple(parts)
    n = len(parts)

    def copies(rins, routs, sems, arrivals=True):
        send, recv = sems
        x, y, c = _place()
        me = 2 * x + y
        out, inc = [], []
        for j, (px, py) in enumerate(_other_chips(x, y)):
            for k in range(n):
                out.append(_remote(rins[k].at[2 * px + py], routs[k].at[me], send, recv, n * j + k, (px, py, c)))
                if arrivals:
                    inc.append(_remote(rins[k].at[me], routs[k].at[2 * px + py], send, recv, n * j + k, (px, py, c)))
        return out, inc

    def start(rins, routs, sems):
        for cp in copies(rins, routs, sems, arrivals=False)[0]:
            cp.start()

    def finish(rins, routs, sems):
        out, inc = copies(rins, routs, sems)
        for cp in inc:
            cp.wait_recv()
        for cp in out:
            cp.wait_send()

    return _Rider(parts, [jax.ShapeDtypeStruct(a.shape, a.dtype) for a in parts], [_sem(3 * n), _sem(3 * n)], start, finish)


def _rider_share(fulls):
    fulls = tuple(fulls)
    n = len(fulls)

    def copies(routs, sems, arrivals=True):
        send, recv = sems
        x, y, c = _place()
        out, inc = [], []
        for k in range(n):
            mine, theirs = routs[k].at[_half(c, fulls[k].shape[0])], routs[k].at[_half(1 - c, fulls[k].shape[0])]
            out.append(_remote(mine, mine, send, recv, k, (x, y, 1 - c)))
            if arrivals:
                inc.append(_remote(theirs, theirs, send, recv, k, (x, y, 1 - c)))
        return out, inc

    def start(rins, routs, sems):
        for cp in copies(routs, sems, arrivals=False)[0]:
            cp.start()

    def finish(rins, routs, sems):
        out, inc = copies(routs, sems)
        for cp in inc:
            cp.wait_recv()
        for cp in out:
            cp.wait_send()

    return _Rider(fulls, [jax.ShapeDtypeStruct(a.shape, a.dtype) for a in fulls], [_sem(n), _sem(n)], start, finish,
                  aliases={k: k for k in range(n)})


def _pair_sum(core, full, recv, name, br=128):
    n, rows, cols = recv.shape

    def body(c_ref, a_ref, b_ref, o_ref):
        o_ref[...] = (a_ref[...] + b_ref[...]).astype(BF16)

    nb = rows // br
    return pl.pallas_call(
        body, name=name, out_shape=jax.ShapeDtypeStruct(recv.shape, BF16),
        grid_spec=pltpu.PrefetchScalarGridSpec(
            num_scalar_prefetch=1, grid=(n, nb),
            in_specs=[pl.BlockSpec((1, br, cols), lambda i, j, c: (i, c[0] * nb + j, 0)),
                      pl.BlockSpec((1, br, cols), lambda i, j, c: (i, j, 0))],
            out_specs=pl.BlockSpec((1, br, cols), lambda i, j, c: (i, j, 0))),
        compiler_params=_cparams(("parallel", "parallel")))(core, full, recv)


def _chip_sum(place, gathered, mine, name, br=128):
    _, r, c = gathered.shape
    nb = r // br

    def body(p_ref, g_ref, m_ref, o_ref):
        slab = lambda j: jnp.where(p_ref[1] == j, m_ref[j], g_ref[j]).astype(F32)
        o_ref[...] = ((slab(0) + slab(1)) + slab(2)) + slab(3)

    return pl.pallas_call(
        body, name=name, out_shape=jax.ShapeDtypeStruct((2 * r, c), F32),
        grid_spec=pltpu.PrefetchScalarGridSpec(
            num_scalar_prefetch=1, grid=(nb,),
            in_specs=[pl.BlockSpec((4, br, c), lambda i, p: (0, i, 0)), pl.BlockSpec((4, br, c), lambda i, p: (0, i, 0))],
            out_specs=pl.BlockSpec((br, c), lambda i, p: (p[0] * nb + i, 0))),
        compiler_params=_cparams(("parallel",)))(place, gathered, mine)


def _adamw(w, g, m, v, name, br):
    n, r, c = w.shape

    def body(w_ref, g_ref, m_ref, v_ref, d_ref, m2_ref, v2_ref):
        d_ref[...], m2_ref[...], v2_ref[...] = _adam_math(w_ref[...], g_ref[...], m_ref[...], v_ref[...])

    spec = pl.BlockSpec((1, br, c), lambda i, j: (i, j, 0))
    shp = jax.ShapeDtypeStruct(w.shape, F32)
    return pl.pallas_call(body, grid=(n, r // br), name=name, in_specs=[spec] * 4, out_specs=[spec] * 3,
                          out_shape=[shp] * 3, compiler_params=_cparams(("parallel", "parallel")))(w, g, m, v)


def _adamw_w_in(w, g, m, v, name, bc=31):
    cols = w.shape[2]
    lead = lambda a: jnp.transpose(a, (2, 0, 1))
    g = jnp.stack(g)

    def body(w_ref, g_ref, m_ref, v_ref, go_ref, d_ref, m2_ref, v2_ref):
        for l in range(2):
            gv = g_ref[:, l, :]
            d_ref[:, l, :], m2_ref[:, l, :], v2_ref[:, l, :] = _adam_math(w_ref[:, l, :], gv, m_ref[:, l, :], v_ref[:, l, :])
            go_ref[:, l, :] = gv

    spec = pl.BlockSpec((bc, 2, D), lambda i: (i, 0, 0))
    outs = pl.pallas_call(body, grid=(cols // bc,), name=name, in_specs=[spec] * 4, out_specs=[spec] * 4,
                          out_shape=[jax.ShapeDtypeStruct((cols, 2, D), F32)] * 4,
                          compiler_params=_cparams(("parallel",)))(lead(w), lead(g), lead(m), lead(v))
    return [jnp.transpose(o, (1, 2, 0)) for o in outs]


_SMALL_NAMES = ("norm_w", "conv_a_w", "gla_gate_w", "gla_gate_b", "gla_norm_w", "pool_w", "pool_scale", "ssd_conv_w",
                "ssd_conv_b", "ssd_dt_bias", "ssd_a_log", "ssd_d", "ssd_norm_w", "final_norm_w")
SMALL_ROWS = 72


def _adam_math(w, g, m, v):
    m2 = ADAM_B1 * m + (1.0 - ADAM_B1) * g
    v2 = ADAM_B2 * v + (1.0 - ADAM_B2) * (g * g)
    m_hat = m2 / (1.0 - ADAM_B1 ** ADAM_STEP)
    v_hat = v2 / (1.0 - ADAM_B2 ** ADAM_STEP)
    return -ADAM_LR * (m_hat / (jnp.sqrt(v_hat) + ADAM_EPS) + ADAM_WD * w), m2, v2


def _small_slices(name, chip):
    if name == "conv_a_w":
        return [((), slice(R_CAW, R_CAW + 3), slice(64 * chip, 64 * chip + 64))]
    if name == "ssd_conv_w":
        return [((), slice(R_SCW, R_SCW + 4), slice(192 * chip, 192 * chip + 192))]
    if name == "gla_gate_w":
        return [((), slice(0, 16), slice(768, 896))]
    if name == "pool_w":
        return [((g, slice(16 * q, 16 * q + 16)), slice(16, 32), slice(256 * q + 64 * g, 256 * q + 64 * g + 64))
                for g in range(4) for q in range(4)]
    row, lanes = {"gla_gate_b": (R_GB, slice(0, 128)), "gla_norm_w": (R_GNW, slice(0, 64)),
                  "pool_scale": (R_PSC, slice(0, 256)), "ssd_conv_b": (R_SCB, slice(0, 768)),
                  "ssd_dt_bias": (R_DTB, slice(16, 20)), "ssd_a_log": (R_AE, slice(0, 4)), "ssd_d": (R_DE, slice(0, 4)),
                  "ssd_norm_w": (R_SNW, slice(0, 256))}[name]
    return [((), slice(row, row + 1), lanes)]


def _small_allreduce(sg0, sg1, dnw0, dnw1, head):
    def body(sg0_ref, sg1_ref, dnw0_ref, dnw1_ref, head_ref, acc, stage, pair, rbuf, send_sems, recv_sems):
        x, y, c = _place()
        chip = 2 * x + y
        stage[0:32, :] = sg0_ref[...]
        stage[32:64, :] = sg1_ref[...]
        stage[64:65, :] = dnw0_ref[0:1, :]
        stage[65:66, :] = dnw1_ref[0:1, :]
        stage[66:68, :] = head_ref[0:2, :]
        stage[68:72, :] = jnp.zeros((4, D), F32)
        sib = _remote(stage, pair, send_sems, recv_sems, 0, (x, y, 1 - c))
        sib.start()
        sib.wait()
        rbuf[0] = stage[...] + pair[...]
        sends = [_remote(rbuf.at[0], rbuf.at[k], send_sems, recv_sems, k, (px, py, c))
                 for k, (px, py) in enumerate(_other_chips(x, y), start=1)]
        for cp in sends:
            cp.start()
        for cp in sends:
            cp.wait()
        slab = lambda d: jnp.where(d == 0, 0, jnp.where(d == 2, 1, jnp.where(d == 1, 2, 3)))
        total = rbuf[slab(jnp.bitwise_xor(chip, 0))]
        for s in range(1, 4):
            total = total + rbuf[slab(jnp.bitwise_xor(chip, s))]
        acc[...] = total

    vmem = pl.BlockSpec(memory_space=pltpu.VMEM)
    return pl.pallas_call(
        body, name="small_allreduce", in_specs=[vmem] * 5, out_specs=vmem,
        out_shape=jax.ShapeDtypeStruct((SMALL_ROWS, D), F32),
        scratch_shapes=[pltpu.VMEM((SMALL_ROWS, D), F32), pltpu.VMEM((SMALL_ROWS, D), F32),
                        pltpu.VMEM((4, SMALL_ROWS, D), F32), _sem(4), _sem(4)],
    )(sg0, sg1, dnw0, dnw1, head)


def _small_adamw(acc, w, m, v):
    n = len(_SMALL_NAMES)

    def body(*refs):
        acc = refs[0]
        w_refs, m_refs, v_refs = refs[1:1 + n], refs[1 + n:1 + 2 * n], refs[1 + 2 * n:1 + 3 * n]
        o = 1 + 3 * n
        g_out, d_out, m_out, v_out = refs[o:o + n], refs[o + n:o + 2 * n], refs[o + 2 * n:o + 3 * n], refs[o + 3 * n:o + 4 * n]
        loss_ref = refs[o + 4 * n]
        chip = 2 * lax.axis_index("x") + lax.axis_index("y")
        loss_ref[...] = acc[67:68, 0:1]

        def update(i, idx, g):
            d, m2, v2 = _adam_math(w_refs[i][idx], g, m_refs[i][idx], v_refs[i][idx])
            g_out[i][idx], d_out[i][idx], m_out[i][idx], v_out[i][idx] = g, d, m2, v2

        for i, name in enumerate(_SMALL_NAMES):
            if name == "final_norm_w":
                update(i, (slice(0, 1), slice(None)), acc[66:67, :])
            elif name == "norm_w":
                for l in range(2):
                    update(i, (slice(l, l + 1), slice(None)), acc[64 + l:65 + l, :])
            elif name in ("conv_a_w", "ssd_conv_w"):
                for s in range(4):
                    @pl.when(chip == s)
                    def _(i=i, name=name, s=s):
                        for l in range(2):
                            (_, rows, lanes), = _small_slices(name, s)
                            update(i, (l,), acc[rows.start + 32 * l:rows.stop + 32 * l, lanes])
            else:
                for l in range(2):
                    for idx, rows, lanes in _small_slices(name, 0):
                        g = acc[rows.start + 32 * l:rows.stop + 32 * l, lanes]
                        if w_refs[i].ndim == 2:
                            update(i, (slice(l, l + 1), slice(None)), g)
                        else:
                            update(i, (l,) + idx, g)

    args = [acc] + [d[k] for d in (w, m, v) for k in _SMALL_NAMES]
    shapes = [jax.ShapeDtypeStruct(w[k].shape, F32) for k in _SMALL_NAMES]
    vmem = pl.BlockSpec(memory_space=pltpu.VMEM)
    outs = pl.pallas_call(body, name="small_adamw", in_specs=[vmem] * len(args), out_specs=[vmem] * (4 * n + 1),
                          out_shape=shapes * 4 + [jax.ShapeDtypeStruct((1, 1), F32)])(*args)
    return outs[0:n], outs[n:2 * n], outs[2 * n:3 * n], outs[3 * n:4 * n], outs[4 * n]


def _permute_cols(w):
    parts = [w[..., s:s + n] for s, n in _PERM]
    parts.append(jnp.zeros(w.shape[:-1] + (NP - NPROJ,), w.dtype))
    return jnp.concatenate(parts, axis=-1)


def _unpermute_cols(w):
    return jnp.concatenate([w[..., s:s + n] for s, n in _UNPERM], axis=-1)


def _mixer_consts(layer, conv_a_w, gla_gate_w, gla_gate_b, gla_norm_w, pool_w, pool_scale, ssd_conv_w, ssd_conv_b,
                  ssd_dt_bias, ssd_a_log, ssd_d, ssd_norm_w):
    def row(v):
        return jnp.pad(v.reshape(1, -1), ((0, 0), (0, 768 - v.size)))

    dtb = jnp.zeros((128,), F32).at[16:20].set(ssd_dt_bias[layer])
    rows = [jnp.pad(conv_a_w[layer], ((0, 0), (0, 512))), row(gla_gate_b[layer]), row(jnp.tile(gla_norm_w[layer], 4)),
            row(pool_scale[layer]), row(ssd_conv_b[layer]), row(dtb), row(jnp.repeat(-jnp.exp(ssd_a_log[layer]), 64)),
            row(jnp.repeat(ssd_d[layer], 64)), row(ssd_norm_w[layer]), jnp.zeros((1, 768), F32), ssd_conv_w[layer]]
    prm = jnp.concatenate(rows, axis=0)
    gw = jnp.zeros((128, 128), F32).at[0:16].set(gla_gate_w[layer]).astype(BF16)
    pw = jnp.zeros((256, 256), F32)
    for g in range(4):
        pw = pw.at[64 * g:64 * g + 64, 64 * g:64 * g + 64].set(pool_w[layer, g])
    return (prm, gw, pw.astype(BF16)) + _mixer_matrices()


def _layer_weights(s_in, s_out):
    wp = _permute_cols(jnp.transpose(s_in, (1, 0, 2)).reshape(D, NPROJ))
    wo = s_out.reshape(D, D)
    return wp, wp.T, wo, wo.T


def _grad_slabs(dwp, dwo):
    return jnp.transpose(_unpermute_cols(dwp).reshape(D, 4, NPROJ // 4), (1, 0, 2)), dwo.reshape(4, D // 4, D)


class _Comm:
    def __init__(self, w_in16, w_out16):
        self.w_in16, self.w_out16 = w_in16, w_out16
        self.core = lax.axis_index("c").astype(jnp.int32).reshape(1)
        self.chip = 2 * lax.axis_index("x") + lax.axis_index("y")
        self.place = jnp.stack([lax.axis_index("c"), self.chip]).astype(jnp.int32)

    def gather_ici(self, layer, extra=None):
        return _rider_gather_ici((self.w_in16[layer], self.w_out16[layer]), extra)

    def pair_sum(self, layer, slabs, received):
        return [_pair_sum(self.core, a, b, name=f"reduce_pair_sum{layer}_{k}") for k, (a, b) in enumerate(zip(slabs, received))]

    def chip_sum(self, layer, gathered, mine):
        return [_chip_sum(self.place, a, b, name=f"reduce_chip_sum{layer}_{k}") for k, (a, b) in enumerate(zip(gathered, mine))]

    def layer_weights(self, layer, s_in, s_out):
        own = lambda slabs, shard: jnp.stack([jnp.where(self.chip == s, shard, slabs[s]) for s in range(4)])
        return _layer_weights(own(s_in, self.w_in16[layer]), own(s_out, self.w_out16[layer]))


def _local_step(x, tgt, norm_w, final_norm_w, consts, wts0, wts1=None, comm=None):
    nw = [norm_w[l:l + 1] for l in range(2)]
    proj0, h0, slabs = _rmsproj(x, nw[0], wts0[0], name="rmsproj0", rider=comm and comm.gather_ici(1))
    (mix0, sg0, ss0, x1), slabs = _mixer_fwd(proj0, x, wts0[2], *consts[0], name="mixer_fwd0",
                                             rider=comm and _rider_gather_d2d(slabs))
    if comm:
        wts1 = comm.layer_weights(1, *slabs)
    proj1, h1, _ = _rmsproj(x1, nw[1], wts1[0], name="rmsproj1")
    (mix1, sg1, ss1, x2), _ = _mixer_fwd(proj1, x1, wts1[2], *consts[1], name="mixer_fwd1")
    dx, head = _head(x2, tgt, final_norm_w.reshape(1, D), name="loss_head")
    (dproj, mgr1, dwo1), _ = _mixer_bwd(proj1, dx, wts1[3], mix1, sg1, ss1, *consts[1], name="mixer_bwd1")
    dwp1, _ = _dwin(h1, dproj, name="dwin1")
    slabs1 = _grad_slabs(dwp1, dwo1)
    (dx, dnw1), recv = _dxin(dproj, wts1[1], x1, dx, nw[1], name="dxin1", rider=comm and _rider_swap(slabs1))
    pairs1 = comm and comm.pair_sum(1, slabs1, recv)
    (dproj, mgr0, dwo0), gathered = _mixer_bwd(proj0, dx, wts0[3], mix0, sg0, ss0, *consts[0], name="mixer_bwd0",
                                               rider=comm and _rider_scatter(pairs1))
    dwp0, big1 = _dwin(h0, dproj, name="dwin0", rider=comm and _rider_share(comm.chip_sum(1, gathered, pairs1)))
    scat = None
    if comm:
        slabs0 = _grad_slabs(dwp0, dwo0)
        pairs0 = comm.pair_sum(0, slabs0, _run_rider(_rider_swap(slabs0), "reduce_swap0"))
        scat = _rider_scatter(pairs0)
    (dx, dnw0), gathered = _dxin(dproj, wts0[1], x, dx, nw[0], name="dxin0", rider=scat)
    if comm:
        big0 = _run_rider(_rider_share(comm.chip_sum(0, gathered, pairs0)), "reduce_share0")
        big = ((big0[0], big1[0]), (big0[1], big1[1]))
    else:
        big = ((dwp0, dwp1), (dwo0, dwo1))
    return head, dx, big, (dnw0, dnw1), (mgr0, mgr1)


def kernel(x, norm_w, w_in, conv_a_w, gla_gate_w, gla_gate_b, gla_norm_w, pool_w, pool_scale, ssd_conv_w, ssd_conv_b, ssd_dt_bias, ssd_a_log, ssd_d, ssd_norm_w, w_out, final_norm_w, loss_target, m_norm_w, m_w_in, m_conv_a_w, m_gla_gate_w, m_gla_gate_b, m_gla_norm_w, m_pool_w, m_pool_scale, m_ssd_conv_w, m_ssd_conv_b, m_ssd_dt_bias, m_ssd_a_log, m_ssd_d, m_ssd_norm_w, m_w_out, m_final_norm_w, v_norm_w, v_w_in, v_conv_a_w, v_gla_gate_w, v_gla_gate_b, v_gla_norm_w, v_pool_w, v_pool_scale, v_ssd_conv_w, v_ssd_conv_b, v_ssd_dt_bias, v_ssd_a_log, v_ssd_d, v_ssd_norm_w, v_w_out, v_final_norm_w):
    weights = dict(norm_w=norm_w, w_in=w_in, conv_a_w=conv_a_w, gla_gate_w=gla_gate_w, gla_gate_b=gla_gate_b,
                   gla_norm_w=gla_norm_w, pool_w=pool_w, pool_scale=pool_scale, ssd_conv_w=ssd_conv_w,
                   ssd_conv_b=ssd_conv_b, ssd_dt_bias=ssd_dt_bias, ssd_a_log=ssd_a_log, ssd_d=ssd_d,
                   ssd_norm_w=ssd_norm_w, w_out=w_out, final_norm_w=final_norm_w)
    m_in = dict(norm_w=m_norm_w, w_in=m_w_in, conv_a_w=m_conv_a_w, gla_gate_w=m_gla_gate_w, gla_gate_b=m_gla_gate_b,
                gla_norm_w=m_gla_norm_w, pool_w=m_pool_w, pool_scale=m_pool_scale, ssd_conv_w=m_ssd_conv_w,
                ssd_conv_b=m_ssd_conv_b, ssd_dt_bias=m_ssd_dt_bias, ssd_a_log=m_ssd_a_log, ssd_d=m_ssd_d,
                ssd_norm_w=m_ssd_norm_w, w_out=m_w_out, final_norm_w=m_final_norm_w)
    v_in = dict(norm_w=v_norm_w, w_in=v_w_in, conv_a_w=v_conv_a_w, gla_gate_w=v_gla_gate_w, gla_gate_b=v_gla_gate_b,
                gla_norm_w=v_gla_norm_w, pool_w=v_pool_w, pool_scale=v_pool_scale, ssd_conv_w=v_ssd_conv_w,
                ssd_conv_b=v_ssd_conv_b, ssd_dt_bias=v_ssd_dt_bias, ssd_a_log=v_ssd_a_log, ssd_d=v_ssd_d,
                ssd_norm_w=v_ssd_norm_w, w_out=v_w_out, final_norm_w=v_final_norm_w)
    order = ("norm_w", "w_in", "conv_a_w", "gla_gate_w", "gla_gate_b", "gla_norm_w", "pool_w", "pool_scale",
             "ssd_conv_w", "ssd_conv_b", "ssd_dt_bias", "ssd_a_log", "ssd_d", "ssd_norm_w", "w_out", "final_norm_w")
    t = x.shape[1]

    comm = _Comm(w_in.astype(BF16), w_out.astype(BF16))
    cshard = jnp.zeros((16, 256), F32)
    for l in range(2):
        cshard = cshard.at[8 * l:8 * l + 3, 0:64].set(conv_a_w[l]).at[8 * l + 3:8 * l + 7, 0:192].set(ssd_conv_w[l])
    s_in, s_out, g_c = _run_rider(comm.gather_ici(0, cshard), "gather_ici0")
    s_in, s_out = _run_rider(_rider_gather_d2d((s_in, s_out)), "gather_d2d0")
    g_c = [jnp.where(comm.chip == s, cshard, g_c[s]) for s in range(4)]
    conv_a_full = jnp.stack([jnp.concatenate([g_c[s][8 * l:8 * l + 3, 0:64] for s in range(4)], axis=-1) for l in range(2)])
    ssd_conv_full = jnp.stack([jnp.concatenate([g_c[s][8 * l + 3:8 * l + 7, 0:192] for s in range(4)], axis=-1)
                               for l in range(2)])
    consts = [_mixer_consts(l, conv_a_full, gla_gate_w, gla_gate_b, gla_norm_w, pool_w, pool_scale, ssd_conv_full,
                            ssd_conv_b, ssd_dt_bias, ssd_a_log, ssd_d, ssd_norm_w) for l in range(2)]

    head, dx, big, dnw, mgr = _local_step(x.reshape(t, D), loss_target.reshape(t, D), norm_w, final_norm_w, consts,
                                          comm.layer_weights(0, s_in, s_out), comm=comm)

    as2d = lambda d: {k: (d[k].reshape(1, D) if k == "final_norm_w" else d[k]) for k in _SMALL_NAMES}
    small = _small_adamw(_small_allreduce(mgr[0], mgr[1], dnw[0], dnw[1], head), as2d(weights), as2d(m_in), as2d(v_in))
    grads, delta, new_m, new_v = ({k: (a.reshape(D) if k == "final_norm_w" else a) for k, a in zip(_SMALL_NAMES, part)}
                                  for part in small[0:4])
    loss = small[4].reshape(())

    grads["w_out"] = jnp.stack(big[1])

    grads["w_in"], delta["w_in"], new_m["w_in"], new_v["w_in"] = _adamw_w_in(w_in, big[0], m_w_in, v_w_in, name="adamw_w_in")
    delta["w_out"], new_m["w_out"], new_v["w_out"] = _adamw(w_out, grads["w_out"], m_w_out, v_w_out, name="adamw_w_out", br=256)

    return (loss, dx.reshape(1, t, D), *[grads[k] for k in order], *[delta[k] for k in order],
            *[new_m[k] for k in order], *[new_v[k] for k in order])
```

```python
import functools

import jax
import jax.numpy as jnp
from jax import lax
from jax.experimental import pallas as pl
from jax.experimental.pallas import tpu as pltpu

F32 = jnp.float32
BF16 = jnp.bfloat16
MESH = pl.DeviceIdType.MESH

D = 1024
CH = 64
EPS = 1e-6
NP = 3456
NPROJ = 3348
GLA_SCALE = 32.0 ** -0.5
INV_TAU = 1.0 / 16.0
TB = 256
NCH = TB // CH
assert TB == 256
HALO_W = NP

C_AH, C_AB, C_AC, C_AZ, C_GQ, C_GK, C_GV = 0, 256, 512, 768, 1024, 1152, 1280
C_GZ, C_PU, C_PZ, C_SZ, C_SX, C_TL = 1536, 1792, 2048, 2304, 2560, 3328
_PERM = ((0, 1536), (1552, 1792), (1536, 16), (3344, 4))
_UNPERM = ((0, 1536), (3328, 16), (1536, 1792), (3344, 4))

R_CAW, R_GB, R_GNW, R_PSC, R_SCB, R_DTB, R_AE, R_DE, R_SNW, R_SCW = 0, 3, 4, 5, 6, 7, 8, 9, 10, 12

ADAM_LR, ADAM_B1, ADAM_B2, ADAM_EPS, ADAM_WD, ADAM_STEP = 0.001, 0.9, 0.999, 1e-08, 0.01, 10

VMEM_LIMIT = 56 * 1024 * 1024


def _cparams(sem, limit=VMEM_LIMIT):
    return pltpu.CompilerParams(dimension_semantics=sem, vmem_limit_bytes=limit)


_ANY = pl.BlockSpec(memory_space=pl.ANY)


def _place():
    return lax.axis_index("x"), lax.axis_index("y"), lax.axis_index("c")


class _Rider:
    def __init__(self, inputs, out_shapes, sems, start, finish, aliases=None):
        self.inputs, self.out_shapes, self.sems = tuple(inputs), tuple(out_shapes), tuple(sems)
        self.start, self.finish, self.aliases = start, finish, dict(aliases or {})


def _call(body, args, *, grid, in_specs, out_specs, out_shape, name, sem, scratch_shapes=(), rider=None, aliases=None):
    aliases = dict(aliases or {})
    if rider is None:
        outs = pl.pallas_call(body, grid=grid, name=name, in_specs=list(in_specs), out_specs=list(out_specs),
                              out_shape=list(out_shape), scratch_shapes=list(scratch_shapes), input_output_aliases=aliases,
                              compiler_params=_cparams(sem))(*args)
        return list(outs), []
    ni, no, ns = len(args), len(out_shape), len(scratch_shapes)
    ri, ro = len(rider.inputs), len(rider.out_shapes)

    def full(*refs):
        ins, rins = refs[:ni], refs[ni:ni + ri]
        outs, routs = refs[ni + ri:ni + ri + no], refs[ni + ri + no:ni + ri + no + ro]
        scr, rsem = refs[ni + ri + no + ro:ni + ri + no + ro + ns], refs[ni + ri + no + ro + ns:]
        first = functools.reduce(jnp.logical_and, [pl.program_id(a) == 0 for a in range(len(grid))])
        last = functools.reduce(jnp.logical_and, [pl.program_id(a) == grid[a] - 1 for a in range(len(grid))])

        @pl.when(first)
        def _():
            rider.start(rins, routs, rsem)

        body(*ins, *outs, *scr)

        @pl.when(last)
        def _():
            rider.finish(rins, routs, rsem)

    outs = pl.pallas_call(
        full, grid=grid, name=name, in_specs=list(in_specs) + [_ANY] * ri, out_specs=list(out_specs) + [_ANY] * ro,
        out_shape=list(out_shape) + list(rider.out_shapes), scratch_shapes=list(scratch_shapes) + list(rider.sems),
        input_output_aliases={**aliases, **{ni + k: no + v for k, v in rider.aliases.items()}},
        compiler_params=_cparams(("arbitrary",) * len(grid)))(*args, *rider.inputs)
    return list(outs[:no]), list(outs[no:])


def _run_rider(rider, name):
    ri = len(rider.inputs)

    def body(*refs):
        rins, routs, rsem = refs[:ri], refs[ri:ri + len(rider.out_shapes)], refs[ri + len(rider.out_shapes):]
        rider.start(rins, routs, rsem)
        rider.finish(rins, routs, rsem)

    return list(pl.pallas_call(body, name=name, in_specs=[_ANY] * ri, out_specs=[_ANY] * len(rider.out_shapes),
                               out_shape=list(rider.out_shapes), scratch_shapes=list(rider.sems),
                               input_output_aliases=dict(rider.aliases))(*rider.inputs))


def _dot(a, b):
    return jnp.dot(a.astype(BF16), b.astype(BF16), preferred_element_type=F32)


def _dot_nt(a, b):
    return lax.dot_general(a.astype(BF16), b.astype(BF16), (((1,), (1,)), ((), ())), preferred_element_type=F32)


def _dot_tn(a, b):
    return lax.dot_general(a.astype(BF16), b.astype(BF16), (((0,), (0,)), ((), ())), preferred_element_type=F32)


def _split(a):
    hi = a.astype(BF16)
    lo = (a - hi.astype(F32)).astype(BF16)
    return hi, lo


def _dot2_l(a, b):
    hi, lo = _split(a)
    return _dot(hi, b) + _dot(lo, b)


def _dot2_r(a, b):
    hi, lo = _split(b)
    return _dot(a, hi) + _dot(a, lo)


def _dot3_l(a, b):
    hi, lo = _split(a)
    lo2 = ((a - hi.astype(F32)) - lo.astype(F32)).astype(BF16)
    return _dot(hi, b) + _dot(lo, b) + _dot(lo2, b)


def _dot2_nt(a, b):
    hi, lo = _split(a)
    return _dot_nt(hi, b) + _dot_nt(lo, b)


def _silu(z):
    return z * jax.nn.sigmoid(z)


def _lse1(x):
    return jnp.log(1.0 + jnp.exp(-jnp.abs(x)))


def _cs(a):
    return jnp.sum(a, axis=0, keepdims=True)


def _iota(shape, dim):
    return lax.broadcasted_iota(jnp.int32, shape, dim)


def _mixer_matrices():
    r, c = _iota((256, 256), 0), _iota((256, 256), 1)
    same_chunk = (r >> 6) == (c >> 6)
    mats = jnp.stack([jnp.where((c > r) & same_chunk, 1.0, 0.0), jnp.where((c < r) & same_chunk, 1.0, 0.0),
                      jnp.where(same_chunk, 1.0 / 64.0, 0.0), jnp.where((r < 128) & (r - 16 == (c >> 6)), 1.0, 0.0)])
    mask = jnp.where((_iota((256, 128), 0) >> 6) == (_iota((256, 128), 1) >> 5), 1.0, 0.0)
    return mats.astype(BF16), mask.astype(F32)


def _dn(ext, k, n, h):
    return pltpu.roll(ext, k, axis=0)[h:h + n]


def _up(ext, k, n):
    return pltpu.roll(ext, ext.shape[0] - k, axis=0)[:n]


def _pool_lane_select(lane, s2, s4, s8, s16):
    return jnp.where(lane < 64, s2, jnp.where(lane < 128, s4, jnp.where(lane < 192, s8, s16)))


def _winsum_dn(ext, lane):
    s2 = ext + pltpu.roll(ext, 1, axis=0)
    s4 = s2 + pltpu.roll(s2, 2, axis=0)
    s8 = s4 + pltpu.roll(s4, 4, axis=0)
    s16 = s8 + pltpu.roll(s8, 8, axis=0)
    return _pool_lane_select(lane, s2, s4, s8, s16)


def _winsum_up(ext, lane):
    m = ext.shape[0]
    s2 = ext + pltpu.roll(ext, m - 1, axis=0)
    s4 = s2 + pltpu.roll(s2, m - 2, axis=0)
    s8 = s4 + pltpu.roll(s4, m - 4, axis=0)
    s16 = s8 + pltpu.roll(s8, m - 8, axis=0)
    return _pool_lane_select(lane, s2, s4, s8, s16)


def _pool_inv_count(tile, n):
    lane = _iota((1, 256), 1)
    win = _pool_lane_select(lane, 2.0, 4.0, 8.0, 16.0).astype(F32)
    tpos = (tile * n + _iota((n, 1), 0) + 1).astype(F32)
    return jnp.where(tpos >= win, 1.0 / win, 1.0 / tpos)


def _silu_pair(z):
    s = jax.nn.sigmoid(z)
    return z * s, s * (1.0 + z * (1.0 - s))


def _chunks(a):
    return [a[c * CH:(c + 1) * CH] for c in range(a.shape[0] // CH)]


def _halves(fn, a, b):
    return jnp.concatenate([fn(a[:, 0:128], b[:, 0:128]), fn(a[:, 128:256], b[:, 128:256])], axis=1)


def _mixer_tile_prep(p_ref, xc, prm_ref, gw_v, cm_ref, mk_ref):
    tail = p_ref[:, C_TL:C_TL + 128]
    pre = _dot(tail, gw_v) + prm_ref[R_GB:R_GB + 1, 0:128]
    la = (jnp.minimum(pre, 0.0) - _lse1(pre)) * INV_TAU
    dtin = tail + prm_ref[R_DTB:R_DTB + 1, 0:128]
    dtf = jnp.maximum(dtin, 0.0) + _lse1(dtin)
    dte = _dot2_l(dtf, cm_ref[3, 0:128, :])
    da = dte * prm_ref[R_AE:R_AE + 1, 0:256]
    rev = _dot2_r(cm_ref[0], jnp.concatenate([la, da], axis=1))
    dec = jnp.exp(rev[:, 0:128])
    kd = p_ref[:, C_GK:C_GK + 128] * dec
    wdec = jnp.exp(rev[:, 128:384])
    w = wdec * dte
    xw = xc[:, 0:256] * w
    d_s = [jnp.exp(_cs(a)) for a in _chunks(la)]
    et = [jnp.exp(_cs(a)) for a in _chunks(da)]
    mask_t = mk_ref[...]
    ut_g = [_dot_tn(v, k) * mask_t for v, k in zip(_chunks(p_ref[:, C_GV:C_GV + 256]), _chunks(kd))]
    ut_s = [_halves(_dot_tn, b, x) for b, x in zip(_chunks(xc[:, 256:512]), _chunks(xw))]
    return tail, pre, dtin, dte, dec, kd, wdec, w, xw, d_s, et, ut_g, ut_s


def _rmsproj(x, nw, wp, name, tm=512, rider=None):
    t = x.shape[0]

    def body(x_ref, nw_ref, w_ref, o_ref, h_ref):
        xv = x_ref[...]
        rs = lax.rsqrt(jnp.mean(xv * xv, axis=-1, keepdims=True) + EPS)
        h = (xv * rs * nw_ref[...]).astype(BF16)
        h_ref[...] = h
        o_ref[...] = jnp.dot(h, w_ref[...], preferred_element_type=F32)

    (proj, h), extra = _call(
        body, (x, nw, wp), grid=(t // tm,), name=name, sem=("parallel",), rider=rider,
        in_specs=[pl.BlockSpec((tm, D), lambda i: (i, 0)), pl.BlockSpec((1, D), lambda i: (0, 0)),
                  pl.BlockSpec((D, NP), lambda i: (0, 0))],
        out_specs=[pl.BlockSpec((tm, NP), lambda i: (i, 0)), pl.BlockSpec((tm, D), lambda i: (i, 0))],
        out_shape=[jax.ShapeDtypeStruct((t, NP), F32), jax.ShapeDtypeStruct((t, D), BF16)])
    return proj, h, extra


def _head(x, tgt, fw, name, tm=512):
    t = x.shape[0]

    def body(x_ref, t_ref, w_ref, dx_ref, acc_ref):
        @pl.when(pl.program_id(0) == 0)
        def _():
            acc_ref[...] = jnp.zeros_like(acc_ref)

        xv = x_ref[...]
        w = w_ref[...]
        rs = lax.rsqrt(jnp.mean(xv * xv, axis=-1, keepdims=True) + EPS)
        xh = xv * rs
        err = xh * w - t_ref[...]
        dy = err * (1.0 / D)
        dxh = dy * w
        dx_ref[...] = rs * (dxh - xh * jnp.mean(dxh * xh, axis=-1, keepdims=True))
        acc_ref[0:1, :] += _cs(dy * xh)
        acc_ref[1:2, :] += jnp.zeros((1, D), F32) + (0.5 / D) * jnp.sum(err * err)

    return pl.pallas_call(
        body, grid=(t // tm,), name=name,
        in_specs=[pl.BlockSpec((tm, D), lambda i: (i, 0)), pl.BlockSpec((tm, D), lambda i: (i, 0)),
                  pl.BlockSpec((1, D), lambda i: (0, 0))],
        out_specs=[pl.BlockSpec((tm, D), lambda i: (i, 0)), pl.BlockSpec((8, D), lambda i: (0, 0))],
        out_shape=[jax.ShapeDtypeStruct((t, D), F32), jax.ShapeDtypeStruct((8, D), F32)],
        compiler_params=_cparams(("arbitrary",)),
    )(x, tgt, fw)


def _dxin(dp, wpt, x, dxn, nw, name, tm=512, rider=None, tiles=None, prev=None):
    t = x.shape[0]
    first, end = tiles or (0, t // tm)

    def body(dp_ref, w_ref, x_ref, dxn_ref, nw_ref, *rest):
        dx_ref, dnw_ref = rest[-2:]

        @pl.when(pl.program_id(0) == 0)
        def _():
            dnw_ref[...] = jnp.zeros_like(dnw_ref) if prev is None else rest[1][...]

        dh = jnp.dot(dp_ref[...].astype(BF16), w_ref[...], preferred_element_type=F32)
        xv = x_ref[...]
        rs = lax.rsqrt(jnp.mean(xv * xv, axis=-1, keepdims=True) + EPS)
        xh = xv * rs
        dnw_ref[0:1, :] += _cs(dh * xh)
        dxh = dh * nw_ref[...]
        dx_ref[...] = dxn_ref[...] + rs * (dxh - xh * jnp.mean(dxh * xh, axis=-1, keepdims=True))

    row = lambda i: (first + i, 0)
    fix = lambda i: (0, 0)
    args, specs, aliases = (dp, wpt, x, dxn, nw), [], None
    if prev is not None:
        args, specs, aliases = args + tuple(prev), [_ANY, pl.BlockSpec((8, D), fix)], {5: 0, 6: 1}
    return _call(
        body, args, grid=(end - first,), name=name, sem=("arbitrary",), rider=rider, aliases=aliases,
        in_specs=[pl.BlockSpec((tm, NP), row), pl.BlockSpec((NP, D), fix), pl.BlockSpec((tm, D), row),
                  pl.BlockSpec((tm, D), row), pl.BlockSpec((1, D), fix)] + specs,
        out_specs=[pl.BlockSpec((tm, D), row), pl.BlockSpec((8, D), fix)],
        out_shape=[jax.ShapeDtypeStruct((t, D), F32), jax.ShapeDtypeStruct((8, D), F32)])


def _dwin(h, dp, name, tm=512, tn=NP, rider=None):
    t = h.shape[0]

    def body(h_ref, dp_ref, o_ref):
        @pl.when(pl.program_id(1) == 0)
        def _():
            o_ref[...] = jnp.zeros_like(o_ref)

        o_ref[...] += _dot_tn(h_ref[...], dp_ref[...])

    (dwp,), extra = _call(
        body, (h, dp), grid=(NP // tn, t // tm), name=name, sem=("parallel", "arbitrary"), rider=rider,
        in_specs=[pl.BlockSpec((tm, D), lambda j, i: (i, 0)), pl.BlockSpec((tm, tn), lambda j, i: (i, j))],
        out_specs=[pl.BlockSpec((D, tn), lambda j, i: (0, j))], out_shape=[jax.ShapeDtypeStruct((D, NP), F32)])
    return dwp, extra


def _mixer_fwd(proj, x, wo, prm, gw, pw, cmat, mask, name, rider=None):
    t = proj.shape[0]
    nt, nc = t // TB, t // CH

    def body(p_ref, x_ref, wo_ref, prm_ref, gw_ref, pw_ref, cm_ref, mk_ref, mix_ref, sg_ref, ss_ref, xn_ref,
             sg_s, ss_s, h_ua, h_pu, h_sx):
        i = pl.program_id(0)

        @pl.when(i == 0)
        def _():
            for r in (sg_s, ss_s, h_ua, h_pu, h_sx):
                r[...] = jnp.zeros_like(r)

        lane = _iota((1, 256), 1)
        u = p_ref[:, C_AC:C_AC + 256] * p_ref[:, C_AH:C_AH + 256]
        ext = jnp.concatenate([h_ua[...], u], axis=0)
        cv = (prm_ref[R_CAW + 2:R_CAW + 3, 0:256] * u + prm_ref[R_CAW + 1:R_CAW + 2, 0:256] * _dn(ext, 1, TB, 8)
              + prm_ref[R_CAW:R_CAW + 1, 0:256] * _dn(ext, 2, TB, 8))
        mix_ref[:, 0:256] = (p_ref[:, C_AB:C_AB + 256] * cv * _silu(p_ref[:, C_AZ:C_AZ + 256])).astype(BF16)
        h_ua[...] = u[TB - 8:, :]
        pu = p_ref[:, C_PU:C_PU + 256]
        ext = jnp.concatenate([h_pu[...], pu], axis=0)
        pooled = _winsum_dn(ext, lane)[16:] * _pool_inv_count(i, TB) - pu
        mixed = _dot(pooled, pw_ref[...])
        mix_ref[:, 512:768] = (prm_ref[R_PSC:R_PSC + 1, 0:256] * mixed * _silu(p_ref[:, C_PZ:C_PZ + 256])).astype(BF16)
        h_pu[...] = pu[TB - 16:, :]
        sx = p_ref[:, C_SX:C_SX + 768]
        ext = jnp.concatenate([h_sx[...], sx], axis=0)
        xc = _silu(prm_ref[R_SCW + 3:R_SCW + 4, :] * sx + prm_ref[R_SCW + 2:R_SCW + 3, :] * _dn(ext, 1, TB, 8)
                   + prm_ref[R_SCW + 1:R_SCW + 2, :] * _dn(ext, 2, TB, 8) + prm_ref[R_SCW:R_SCW + 1, :] * _dn(ext, 3, TB, 8)
                   + prm_ref[R_SCB:R_SCB + 1, :])
        h_sx[...] = sx[TB - 8:, :]

        _, _, _, _, _, _, _, _, _, d_s, et, ut_g, ut_s = _mixer_tile_prep(p_ref, xc, prm_ref, gw_ref[...], cm_ref, mk_ref)
        s_g, s_s = sg_s[...], ss_s[...]
        o, y = [], []
        qs = _chunks(p_ref[:, C_GQ:C_GQ + 128] * GLA_SCALE)
        cm = _chunks(xc[:, 512:768])
        for c in range(NCH):
            sg_ref[c] = s_g
            ss_ref[c] = s_s
            s_g = s_g * d_s[c] + ut_g[c]
            s_s = s_s * et[c] + ut_s[c]
            o.append(_dot_nt(qs[c], s_g))
            y.append(_halves(_dot, cm[c], s_s))
        sg_s[...] = s_g
        ss_s[...] = s_s
        o = jnp.concatenate(o, axis=0)
        on = o * lax.rsqrt(_dot2_l(o * o, cm_ref[2]) + EPS)
        mix_ref[:, 256:512] = (on * prm_ref[R_GNW:R_GNW + 1, 0:256] * _silu(p_ref[:, C_GZ:C_GZ + 256])).astype(BF16)
        y2 = ((jnp.concatenate(y, axis=0) + prm_ref[R_DE:R_DE + 1, 0:256] * xc[:, 0:256])
              * _silu(p_ref[:, C_SZ:C_SZ + 256]))
        mix_ref[:, 768:1024] = (y2 * lax.rsqrt(jnp.mean(y2 * y2, axis=-1, keepdims=True) + EPS)
                                * prm_ref[R_SNW:R_SNW + 1, 0:256]).astype(BF16)
        xn_ref[...] = x_ref[...] + jnp.dot(mix_ref[...], wo_ref[...], preferred_element_type=F32)

    return _call(
        body, (proj, x, wo, prm, gw, pw, cmat, mask), grid=(nt,), name=name, sem=("arbitrary",), rider=rider,
        in_specs=[pl.BlockSpec((TB, NP), lambda i: (i, 0)), pl.BlockSpec((TB, D), lambda i: (i, 0)),
                  pl.BlockSpec((D, D), lambda i: (0, 0)), pl.BlockSpec((16, 768), lambda i: (0, 0)),
                  pl.BlockSpec((128, 128), lambda i: (0, 0)), pl.BlockSpec((256, 256), lambda i: (0, 0)),
                  pl.BlockSpec((4, 256, 256), lambda i: (0, 0, 0)), pl.BlockSpec((256, 128), lambda i: (0, 0))],
        out_specs=[pl.BlockSpec((TB, D), lambda i: (i, 0)), pl.BlockSpec((NCH, 256, 128), lambda i: (i, 0, 0)),
                   pl.BlockSpec((NCH, 128, 256), lambda i: (i, 0, 0)), pl.BlockSpec((TB, D), lambda i: (i, 0))],
        out_shape=[jax.ShapeDtypeStruct((t, D), BF16), jax.ShapeDtypeStruct((nc, 256, 128), F32),
                   jax.ShapeDtypeStruct((nc, 128, 256), F32), jax.ShapeDtypeStruct((t, D), F32)],
        scratch_shapes=[pltpu.VMEM((256, 128), F32), pltpu.VMEM((128, 256), F32), pltpu.VMEM((8, 256), F32),
                        pltpu.VMEM((16, 256), F32), pltpu.VMEM((8, 768), F32)])


def _mixer_bwd(proj, dxn, wot, mix, sg, ss, prm, gw, pw, cmat, mask, name, rider=None):
    t = proj.shape[0]
    nt = t // TB
    rev = lambda i: nt - 1 - i

    def body(p_ref, hp_ref, dxn_ref, wot_ref, mix_ref, sg_ref, ss_ref, prm_ref, gw_ref, pw_ref, cm_ref, mk_ref,
             dp_ref, sgc_ref, dwo_ref,
             gg_s, gs_s, h_dcv, h_dpl, h_dpre, gsm_ref, dgw_ref, dpw_ref, dm_ref):
        i = pl.program_id(0)
        tile = nt - 1 - i

        @pl.when(i == 0)
        def _():
            for r in (gg_s, gs_s, h_dcv, h_dpl, h_dpre, gsm_ref, dgw_ref, dpw_ref, dwo_ref):
                r[...] = jnp.zeros_like(r)

        dxn = dxn_ref[...].astype(BF16)
        dm_ref[...] = jnp.dot(dxn, wot_ref[...], preferred_element_type=F32)
        dwo_ref[...] += _dot_tn(mix_ref[...], dxn)

        lane = _iota((1, 256), 1)
        first = (tile > 0).astype(F32)
        ah, ac = p_ref[:, C_AH:C_AH + 256], p_ref[:, C_AC:C_AC + 256]
        ab, az = p_ref[:, C_AB:C_AB + 256], p_ref[:, C_AZ:C_AZ + 256]
        w0, w1, w2 = (prm_ref[R_CAW + j:R_CAW + j + 1, 0:256] for j in range(3))
        u = ac * ah
        ext = jnp.concatenate([hp_ref[8:16, C_AC:C_AC + 256] * hp_ref[8:16, C_AH:C_AH + 256] * first, u], axis=0)
        u1, u2 = _dn(ext, 1, TB, 8), _dn(ext, 2, TB, 8)
        cv = w2 * u + w1 * u1 + w0 * u2
        g = dm_ref[:, 0:256]
        sz, dsz = _silu_pair(az)
        dp_ref[:, C_AB:C_AB + 256] = (g * cv * sz).astype(BF16)
        dp_ref[:, C_AZ:C_AZ + 256] = (g * ab * cv * dsz).astype(BF16)
        dcv = g * ab * sz
        dext = jnp.concatenate([dcv, h_dcv[...]], axis=0)
        du = w2 * dcv + w1 * _up(dext, 1, TB) + w0 * _up(dext, 2, TB)
        dp_ref[:, C_AC:C_AC + 256] = (du * ah).astype(BF16)
        dp_ref[:, C_AH:C_AH + 256] = (du * ac).astype(BF16)
        gsm_ref[R_CAW:R_CAW + 1, 0:256] += _cs(dcv * u2)
        gsm_ref[R_CAW + 1:R_CAW + 2, 0:256] += _cs(dcv * u1)
        gsm_ref[R_CAW + 2:R_CAW + 3, 0:256] += _cs(dcv * u)
        h_dcv[...] = dcv[0:8, :]
        pu, pz = p_ref[:, C_PU:C_PU + 256], p_ref[:, C_PZ:C_PZ + 256]
        psc = prm_ref[R_PSC:R_PSC + 1, 0:256]
        icnt = _pool_inv_count(tile, TB)
        ext = jnp.concatenate([hp_ref[:, C_PU:C_PU + 256] * first, pu], axis=0)
        pooled = _winsum_dn(ext, lane)[16:] * icnt - pu
        pw_v = pw_ref[...]
        mixed = _dot(pooled, pw_v)
        g = dm_ref[:, 512:768]
        sz, dsz = _silu_pair(pz)
        gsm_ref[R_PSC:R_PSC + 1, 0:256] += _cs(g * mixed * sz)
        dp_ref[:, C_PZ:C_PZ + 256] = (g * psc * mixed * dsz).astype(BF16)
        dmixed = g * psc * sz
        dpw_ref[...] += _dot_tn(pooled, dmixed)
        dpooled = _dot_nt(dmixed, pw_v)
        qd = dpooled * icnt
        dext = jnp.concatenate([qd, h_dpl[...]], axis=0)
        dp_ref[:, C_PU:C_PU + 256] = (_winsum_up(dext, lane)[:TB] - dpooled).astype(BF16)
        h_dpl[...] = qd[0:16, :]
        sx = p_ref[:, C_SX:C_SX + 768]
        cw = [prm_ref[R_SCW + j:R_SCW + j + 1, :] for j in range(4)]
        ext = jnp.concatenate([hp_ref[8:16, C_SX:C_SX + 768] * first, sx], axis=0)
        sx1, sx2, sx3 = _dn(ext, 1, TB, 8), _dn(ext, 2, TB, 8), _dn(ext, 3, TB, 8)
        cpre = cw[3] * sx + cw[2] * sx1 + cw[1] * sx2 + cw[0] * sx3 + prm_ref[R_SCB:R_SCB + 1, :]
        xc, dxc = _silu_pair(cpre)
        xs, bm, cm = xc[:, 0:256], xc[:, 256:512], xc[:, 512:768]

        gw_v = gw_ref[...]
        tail, pre, dtin, dte, dec, kd, wdec, w, xw, d_s, et, ut_g, ut_s = _mixer_tile_prep(p_ref, xc, prm_ref, gw_v,
                                                                                          cm_ref, mk_ref)
        gmean = cm_ref[2]
        mask_t = mk_ref[...]
        gnw = prm_ref[R_GNW:R_GNW + 1, 0:256]
        a_e = prm_ref[R_AE:R_AE + 1, 0:256]
        d_e = prm_ref[R_DE:R_DE + 1, 0:256]
        snw = prm_ref[R_SNW:R_SNW + 1, 0:256]
        sg_in = [sg_ref[c] for c in range(NCH)]
        ss_in = [ss_ref[c] for c in range(NCH)]
        sg_n = [sg_in[c] * d_s[c] + ut_g[c] for c in range(NCH)]
        ss_n = [ss_in[c] * et[c] + ut_s[c] for c in range(NCH)]
        qs = _chunks(p_ref[:, C_GQ:C_GQ + 128] * GLA_SCALE)
        cm_c, bm_c, xw_c, kd_c = _chunks(cm), _chunks(bm), _chunks(xw), _chunks(kd)
        v_c = _chunks(p_ref[:, C_GV:C_GV + 256])
        o = jnp.concatenate([_dot_nt(qs[c], sg_n[c]) for c in range(NCH)], axis=0)
        y = jnp.concatenate([_halves(_dot, cm_c[c], ss_n[c]) for c in range(NCH)], axis=0) + d_e * xs
        gz = p_ref[:, C_GZ:C_GZ + 256]
        r = lax.rsqrt(_dot2_l(o * o, gmean) + EPS)
        on = o * r
        dyb = dm_ref[:, 256:512]
        sz, dsz = _silu_pair(gz)
        dp_ref[:, C_GZ:C_GZ + 256] = (dyb * on * gnw * dsz).astype(BF16)
        tg = dyb * sz
        gsm_ref[R_GNW:R_GNW + 1, 0:256] += _cs(tg * on)
        don = tg * gnw
        do_c = _chunks(r * (don - on * _dot2_l(don * on, gmean)))
        ssz = p_ref[:, C_SZ:C_SZ + 256]
        sil, dsil = _silu_pair(ssz)
        y2 = y * sil
        r = lax.rsqrt(jnp.mean(y2 * y2, axis=-1, keepdims=True) + EPS)
        yn = y2 * r
        dyd = dm_ref[:, 768:1024]
        gsm_ref[R_SNW:R_SNW + 1, 0:256] += _cs(dyd * yn)
        dn = dyd * snw
        dy2 = r * (dn - yn * jnp.mean(dn * yn, axis=-1, keepdims=True))
        dp_ref[:, C_SZ:C_SZ + 256] = (dy2 * y * dsil).astype(BF16)
        dy = dy2 * sil
        gsm_ref[R_DE:R_DE + 1, 0:256] += _cs(dy * xs)
        dy_c = _chunks(dy)
        dq = jnp.concatenate([_dot(do_c[c], sg_n[c]) for c in range(NCH)], axis=0)
        dp_ref[:, C_GQ:C_GQ + 128] = (dq * GLA_SCALE).astype(BF16)
        dcm = jnp.concatenate([_halves(_dot_nt, dy_c[c], ss_n[c]) for c in range(NCH)], axis=0)
        gg = [_dot_tn(do_c[c], qs[c]) * mask_t for c in range(NCH)]
        gs = [_halves(_dot_tn, cm_c[c], dy_c[c]) for c in range(NCH)]
        car_g, car_s = gg_s[...], gs_s[...]
        for c in reversed(range(NCH)):
            gg[c] = gg[c] + car_g
            gs[c] = gs[c] + car_s
            car_g = gg[c] * d_s[c]
            car_s = gs[c] * et[c]
        gg_s[...] = car_g
        gs_s[...] = car_s
        dkd = jnp.concatenate([_dot(v_c[c], gg[c]) for c in range(NCH)], axis=0)
        dp_ref[:, C_GV:C_GV + 256] = jnp.concatenate([_dot_nt(kd_c[c], gg[c]) for c in range(NCH)], axis=0).astype(BF16)
        dp_ref[:, C_GK:C_GK + 128] = (dkd * dec).astype(BF16)
        dbm = jnp.concatenate([_halves(_dot_nt, xw_c[c], gs[c]) for c in range(NCH)], axis=0)
        dxw = jnp.concatenate([_halves(_dot, bm_c[c], gs[c]) for c in range(NCH)], axis=0)
        dxs = dy * d_e + dxw * w
        dw = dxw * xs
        dsuf = _dot2_r(cm_ref[1], jnp.concatenate([dkd * kd, dw * dte * wdec], axis=1))
        tot_g = jnp.concatenate([jnp.broadcast_to(_cs(gg[c] * sg_in[c]) * d_s[c], (CH, 128)) for c in range(NCH)], axis=0)
        tot_s = jnp.concatenate([jnp.broadcast_to(_cs(gs[c] * ss_in[c]) * et[c], (CH, 256)) for c in range(NCH)], axis=0)
        dpre = (dsuf[:, 0:128] + tot_g) * INV_TAU * jax.nn.sigmoid(-pre)
        dgw_ref[...] += _dot_tn(tail, dpre)
        gsm_ref[R_GB:R_GB + 1, 0:128] += _cs(dpre)
        dda = dsuf[:, 128:384] + tot_s
        gsm_ref[R_AE:R_AE + 1, 0:256] += _cs(dda * dte)
        dtail_s = _dot2_nt(dw * wdec + dda * a_e, cm_ref[3, 0:128, :]) * jax.nn.sigmoid(dtin)
        gsm_ref[R_DTB:R_DTB + 1, 0:128] += _cs(dtail_s)
        dp_ref[:, C_TL:C_TL + 128] = (_dot_nt(dpre, gw_v) + dtail_s).astype(BF16)
        dpre_c = jnp.concatenate([dxs, dbm, dcm], axis=1) * dxc
        dext = jnp.concatenate([dpre_c, h_dpre[...]], axis=0)
        dp_ref[:, C_SX:C_SX + 768] = (cw[3] * dpre_c + cw[2] * _up(dext, 1, TB) + cw[1] * _up(dext, 2, TB)
                                      + cw[0] * _up(dext, 3, TB)).astype(BF16)
        gsm_ref[R_SCW + 3:R_SCW + 4, :] += _cs(dpre_c * sx)
        gsm_ref[R_SCW + 2:R_SCW + 3, :] += _cs(dpre_c * sx1)
        gsm_ref[R_SCW + 1:R_SCW + 2, :] += _cs(dpre_c * sx2)
        gsm_ref[R_SCW:R_SCW + 1, :] += _cs(dpre_c * sx3)
        gsm_ref[R_SCB:R_SCB + 1, :] += _cs(dpre_c)
        h_dpre[...] = dpre_c[0:8, :]

        @pl.when(i == nt - 1)
        def _():
            ri, ci = _iota((256, 256), 0), _iota((256, 256), 1)
            per_head = jnp.where((ri >> 6) == ci, 1.0, 0.0).astype(BF16)
            per_dv = jnp.where((ri & 63) == ci, 1.0, 0.0).astype(BF16)
            row = _iota((8, 256), 0)
            top = gsm_ref[0:8, 0:256]
            sgc_ref[0:8, 0:256] = jnp.where(row == R_GNW, _dot3_l(top, per_dv), top)
            bot = gsm_ref[8:16, 0:256]
            fold = _dot3_l(jnp.where(row == R_AE - 8, bot * a_e, bot), per_head)
            sgc_ref[8:16, 0:256] = jnp.where((row == R_AE - 8) | (row == R_DE - 8), fold, bot)
            sgc_ref[0:16, 256:768] = gsm_ref[:, 256:768]
            sgc_ref[0:16, 768:896] = dgw_ref[0:16, :]
            sgc_ref[0:16, 896:1024] = jnp.zeros((16, 128), F32)
            diag = _pool_lane_select(lane, dpw_ref[0:64, :], dpw_ref[64:128, :], dpw_ref[128:192, :], dpw_ref[192:256, :])
            for q in range(4):
                sgc_ref[16:32, 256 * q:256 * q + 256] = diag[16 * q:16 * q + 16, :]

    return _call(
        body, (proj, proj, dxn, wot, mix, sg, ss, prm, gw, pw, cmat, mask), grid=(nt,), name=name, sem=("arbitrary",),
        rider=rider,
        in_specs=[pl.BlockSpec((TB, NP), lambda i: (rev(i), 0)),
                  pl.BlockSpec((16, HALO_W), lambda i: (jnp.maximum(rev(i) * (TB // 16) - 1, 0), 0)),
                  pl.BlockSpec((TB, D), lambda i: (rev(i), 0)), pl.BlockSpec((D, D), lambda i: (0, 0)),
                  pl.BlockSpec((TB, D), lambda i: (rev(i), 0)),
                  pl.BlockSpec((NCH, 256, 128), lambda i: (rev(i), 0, 0)),
                  pl.BlockSpec((NCH, 128, 256), lambda i: (rev(i), 0, 0)),
                  pl.BlockSpec((16, 768), lambda i: (0, 0)), pl.BlockSpec((128, 128), lambda i: (0, 0)),
                  pl.BlockSpec((256, 256), lambda i: (0, 0)), pl.BlockSpec((4, 256, 256), lambda i: (0, 0, 0)),
                  pl.BlockSpec((256, 128), lambda i: (0, 0))],
        out_specs=[pl.BlockSpec((TB, NP), lambda i: (rev(i), 0)), pl.BlockSpec((32, 1024), lambda i: (0, 0)),
                   pl.BlockSpec((D, D), lambda i: (0, 0))],
        out_shape=[jax.ShapeDtypeStruct((t, NP), BF16), jax.ShapeDtypeStruct((32, 1024), F32),
                   jax.ShapeDtypeStruct((D, D), F32)],
        scratch_shapes=[pltpu.VMEM((256, 128), F32), pltpu.VMEM((128, 256), F32), pltpu.VMEM((8, 256), F32),
                        pltpu.VMEM((16, 256), F32), pltpu.VMEM((8, 768), F32), pltpu.VMEM((16, 768), F32),
                        pltpu.VMEM((128, 128), F32), pltpu.VMEM((256, 256), F32), pltpu.VMEM((TB, D), F32)])


SHARD = NPROJ // 4
SHARD_PAD = 896


def _ranges_to_perm(o, n):
    out, p = [], 0
    for start, size in _PERM:
        a, b = max(o, start), min(o + n, start + size)
        if a < b:
            out.append((a, b - a, p + a - start))
        p += size
    return out


def _ranges_to_orig(p0, n):
    out, p = [], 0
    for start, size in _PERM:
        a, b = max(p0, p), min(p0 + n, p + size)
        if a < b:
            out.append((a, b - a, start + a - p))
        p += size
    return out


def _lane_window(load, lo, n, d, lane):
    a = 128 * (lo // 128)
    off = lo - a
    w = 128 if off + n <= 128 else 256
    chunk = load(a, w)
    shift = (d - off) % w
    if shift:
        chunk = pltpu.roll(chunk, shift, axis=1)
    return jnp.where((lane >= d) & (lane < d + n), chunk[:, 0:128], 0.0)


def _assemble_w_in(slabs, name, rb=256):
    def body(s_ref, wp_ref, wpt_ref):
        lane = _iota((1, 128), 1)
        for b in range(NP // 128):
            acc = jnp.zeros((rb, 128), F32)
            for p, n, o in _ranges_to_orig(128 * b, 128):
                while n > 0:
                    s, lo = o // SHARD, o % SHARD
                    cnt = min(n, SHARD - lo)
                    acc = acc + _lane_window(lambda a, w, s=s: s_ref[s, :, a:a + w].astype(F32), lo, cnt, p - 128 * b, lane)
                    o, p, n = o + cnt, p + cnt, n - cnt
            wp_ref[:, 128 * b:128 * b + 128] = acc.astype(BF16)
            wpt_ref[128 * b:128 * b + 128, :] = acc.T.astype(BF16)

    return pl.pallas_call(
        body, grid=(D // rb,), name=name,
        in_specs=[pl.BlockSpec((4, rb, SHARD_PAD), lambda i: (0, i, 0))],
        out_specs=[pl.BlockSpec((rb, NP), lambda i: (i, 0)), pl.BlockSpec((NP, rb), lambda i: (0, i))],
        out_shape=[jax.ShapeDtypeStruct((D, NP), BF16), jax.ShapeDtypeStruct((NP, D), BF16)],
        compiler_params=_cparams(("parallel",)))(slabs)


def _split_dw_in(dwp, name, rb=256):
    def body(g_ref, o_ref):
        lane = _iota((1, 128), 1)
        for s in range(4):
            for k in range(SHARD_PAD // 128):
                acc = jnp.zeros((rb, 128), F32)
                n_valid = min(128, SHARD - 128 * k)
                for o, n, p in _ranges_to_perm(SHARD * s + 128 * k, n_valid):
                    acc = acc + _lane_window(lambda a, w: g_ref[:, a:a + w], p, n, o - SHARD * s - 128 * k, lane)
                o_ref[s, :, 128 * k:128 * k + 128] = acc

    return pl.pallas_call(
        body, grid=(D // rb,), name=name,
        in_specs=[pl.BlockSpec((rb, NP), lambda i: (i, 0))],
        out_specs=pl.BlockSpec((4, rb, SHARD_PAD), lambda i: (0, i, 0)),
        out_shape=jax.ShapeDtypeStruct((4, D, SHARD_PAD), F32),
        compiler_params=_cparams(("parallel",)))(dwp)


def _half(c, n):
    return pl.ds(pl.multiple_of(c * (n // 2), n // 2), n // 2)


def _other_chips(x, y):
    return ((1 - x, y), (x, 1 - y), (1 - x, 1 - y))


def _remote(src, dst, send, recv, k, dev):
    return pltpu.make_async_remote_copy(src_ref=src, dst_ref=dst, send_sem=send.at[k], recv_sem=recv.at[k], device_id=dev,
                                        device_id_type=MESH)


def _sem(n):
    return pltpu.SemaphoreType.DMA((n,))


def _rider_gather_ici(shards, extra=None):
    shards = tuple(shards) + ((extra,) if extra is not None else ())
    n = len(shards)

    def copies(rins, routs, sems, arrivals=True):
        send, recv = sems
        x, y, c = _place()
        me = 2 * x + y
        out, inc = [], []
        for j, (px, py) in enumerate(_other_chips(x, y)):
            for k in range(n):
                whole = extra is not None and k == n - 1
                rows = pl.ds(0, shards[k].shape[0]) if whole else _half(c, shards[k].shape[0])
                out.append(_remote(rins[k].at[rows], routs[k].at[me, rows], send, recv, n * j + k, (px, py, c)))
                if arrivals:
                    inc.append(_remote(rins[k].at[rows], routs[k].at[2 * px + py, rows], send, recv, n * j + k, (px, py, c)))
        return out, inc

    def start(rins, routs, sems):
        for cp in copies(rins, routs, sems, arrivals=False)[0]:
            cp.start()

    def finish(rins, routs, sems):
        out, inc = copies(rins, routs, sems)
        for cp in inc:
            cp.wait_recv()
        for cp in out:
            cp.wait_send()

    return _Rider(shards, [jax.ShapeDtypeStruct((4,) + a.shape, a.dtype) for a in shards], [_sem(3 * n), _sem(3 * n)],
                  start, finish)


def _rider_gather_d2d(slabs):
    slabs = tuple(slabs)
    n = len(slabs)

    def copies(routs, sems, arrivals=True):
        send, recv = sems
        x, y, c = _place()
        out, inc = [], []
        for j, (px, py) in enumerate(_other_chips(x, y)):
            for k in range(n):
                rows = slabs[k].shape[1]
                mine, theirs = routs[k].at[2 * px + py, _half(c, rows)], routs[k].at[2 * px + py, _half(1 - c, rows)]
                out.append(_remote(mine, mine, send, recv, n * j + k, (x, y, 1 - c)))
                if arrivals:
                    inc.append(_remote(theirs, theirs, send, recv, n * j + k, (x, y, 1 - c)))
        return out, inc

    def start(rins, routs, sems):
        for cp in copies(routs, sems, arrivals=False)[0]:
            cp.start()

    def finish(rins, routs, sems):
        out, inc = copies(routs, sems)
        for cp in inc:
            cp.wait_recv()
        for cp in out:
            cp.wait_send()

    return _Rider(slabs, [jax.ShapeDtypeStruct(a.shape, a.dtype) for a in slabs], [_sem(3 * n), _sem(3 * n)], start, finish,
                  aliases={k: k for k in range(n)})


def _rider_swap(parts):
    parts = tuple(parts)
    n = len(parts)

    def copies(rins, routs, sems):
        send, recv = sems
        x, y, c = _place()
        return [_remote(rins[k].at[:, _half(1 - c, parts[k].shape[1])], routs[k], send, recv, k, (x, y, 1 - c))
                for k in range(n)]

    def start(rins, routs, sems):
        for cp in copies(rins, routs, sems):
            cp.start()

    def finish(rins, routs, sems):
        for cp in copies(rins, routs, sems):
            cp.wait()

    return _Rider(parts, [jax.ShapeDtypeStruct((4, a.shape[1] // 2, a.shape[2]), a.dtype) for a in parts],
                  [_sem(n), _sem(n)], start, finish)


def _rider_scatter(parts):
    parts = tuple(parts)
    n = len(parts)

    def copies(rins, routs, sems, arrivals=True):
        send, recv = sems
        x, y, c = _place()
        me = 2 * x + y
        out, inc = [], []
        for j, (px, py) in enumerate(_other_chips(x, y)):
            for k in range(n):
                out.append(_remote(rins[k].at[2 * px + py], routs[k].at[me], send, recv, n * j + k, (px, py, c)))
                if arrivals:
                    inc.append(_remote(rins[k].at[me], routs[k].at[2 * px + py], send, recv, n * j + k, (px, py, c)))
        return out, inc

    def start(rins, routs, sems):
        for cp in copies(rins, routs, sems, arrivals=False)[0]:
            cp.start()

    def finish(rins, routs, sems):
        out, inc = copies(rins, routs, sems)
        for cp in inc:
            cp.wait_recv()
        for cp in out:
            cp.wait_send()

    return _Rider(parts, [jax.ShapeDtypeStruct(a.shape, a.dtype) for a in parts], [_sem(3 * n), _sem(3 * n)], start, finish)


def _rider_share(fulls):
    fulls = tuple(fulls)
    n = len(fulls)

    def copies(routs, sems, arrivals=True):
        send, recv = sems
        x, y, c = _place()
        out, inc = [], []
        for k in range(n):
            mine, theirs = routs[k].at[_half(c, fulls[k].shape[0])], routs[k].at[_half(1 - c, fulls[k].shape[0])]
            out.append(_remote(mine, mine, send, recv, k, (x, y, 1 - c)))
            if arrivals:
                inc.append(_remote(theirs, theirs, send, recv, k, (x, y, 1 - c)))
        return out, inc

    def start(rins, routs, sems):
        for cp in copies(routs, sems, arrivals=False)[0]:
            cp.start()

    def finish(rins, routs, sems):
        out, inc = copies(routs, sems)
        for cp in inc:
            cp.wait_recv()
        for cp in out:
            cp.wait_send()

    return _Rider(fulls, [jax.ShapeDtypeStruct(a.shape, a.dtype) for a in fulls], [_sem(n), _sem(n)], start, finish,
                  aliases={k: k for k in range(n)})


def _pair_sum(core, full, recv, name, br=128):
    n, rows, cols = recv.shape

    def body(c_ref, a_ref, b_ref, o_ref):
        o_ref[...] = (a_ref[...] + b_ref[...]).astype(BF16)

    nb = rows // br
    return pl.pallas_call(
        body, name=name, out_shape=jax.ShapeDtypeStruct(recv.shape, BF16),
        grid_spec=pltpu.PrefetchScalarGridSpec(
            num_scalar_prefetch=1, grid=(n, nb),
            in_specs=[pl.BlockSpec((1, br, cols), lambda i, j, c: (i, c[0] * nb + j, 0)),
                      pl.BlockSpec((1, br, cols), lambda i, j, c: (i, j, 0))],
            out_specs=pl.BlockSpec((1, br, cols), lambda i, j, c: (i, j, 0))),
        compiler_params=_cparams(("parallel", "parallel")))(core, full, recv)


def _chip_sum(place, gathered, mine, name, br=128):
    _, r, c = gathered.shape
    nb = r // br

    def body(p_ref, g_ref, m_ref, o_ref):
        slab = lambda j: jnp.where(p_ref[1] == j, m_ref[j], g_ref[j]).astype(F32)
        o_ref[...] = ((slab(0) + slab(1)) + slab(2)) + slab(3)

    return pl.pallas_call(
        body, name=name, out_shape=jax.ShapeDtypeStruct((2 * r, c), F32),
        grid_spec=pltpu.PrefetchScalarGridSpec(
            num_scalar_prefetch=1, grid=(nb,),
            in_specs=[pl.BlockSpec((4, br, c), lambda i, p: (0, i, 0)), pl.BlockSpec((4, br, c), lambda i, p: (0, i, 0))],
            out_specs=pl.BlockSpec((br, c), lambda i, p: (p[0] * nb + i, 0))),
        compiler_params=_cparams(("parallel",)))(place, gathered, mine)


def _adamw(w, g, m, v, name, br):
    n, r, c = w.shape

    def body(w_ref, g_ref, m_ref, v_ref, d_ref, m2_ref, v2_ref):
        d_ref[...], m2_ref[...], v2_ref[...] = _adam_math(w_ref[...], g_ref[...], m_ref[...], v_ref[...])

    spec = pl.BlockSpec((1, br, c), lambda i, j: (i, j, 0))
    shp = jax.ShapeDtypeStruct(w.shape, F32)
    return pl.pallas_call(body, grid=(n, r // br), name=name, in_specs=[spec] * 4, out_specs=[spec] * 3,
                          out_shape=[shp] * 3, compiler_params=_cparams(("parallel", "parallel")))(w, g, m, v)


def _adamw_w_in(w, g, m, v, name, bc=31):
    cols = w.shape[2]
    lead = lambda a: jnp.transpose(a, (2, 0, 1))
    g = jnp.stack([a[:, 0:cols] for a in g])

    def body(w_ref, g_ref, m_ref, v_ref, go_ref, d_ref, m2_ref, v2_ref):
        for l in range(2):
            gv = g_ref[:, l, :]
            d_ref[:, l, :], m2_ref[:, l, :], v2_ref[:, l, :] = _adam_math(w_ref[:, l, :], gv, m_ref[:, l, :], v_ref[:, l, :])
            go_ref[:, l, :] = gv

    spec = pl.BlockSpec((bc, 2, D), lambda i: (i, 0, 0))
    outs = pl.pallas_call(body, grid=(cols // bc,), name=name, in_specs=[spec] * 4, out_specs=[spec] * 4,
                          out_shape=[jax.ShapeDtypeStruct((cols, 2, D), F32)] * 4,
                          compiler_params=_cparams(("parallel",)))(lead(w), lead(g), lead(m), lead(v))
    return [jnp.transpose(o, (1, 2, 0)) for o in outs]


_SMALL_NAMES = ("norm_w", "conv_a_w", "gla_gate_w", "gla_gate_b", "gla_norm_w", "pool_w", "pool_scale", "ssd_conv_w",
                "ssd_conv_b", "ssd_dt_bias", "ssd_a_log", "ssd_d", "ssd_norm_w", "final_norm_w")
SMALL_ROWS = 72


def _adam_math(w, g, m, v):
    m2 = ADAM_B1 * m + (1.0 - ADAM_B1) * g
    v2 = ADAM_B2 * v + (1.0 - ADAM_B2) * (g * g)
    m_hat = m2 / (1.0 - ADAM_B1 ** ADAM_STEP)
    v_hat = v2 / (1.0 - ADAM_B2 ** ADAM_STEP)
    return -ADAM_LR * (m_hat / (jnp.sqrt(v_hat) + ADAM_EPS) + ADAM_WD * w), m2, v2


def _small_slices(name, chip):
    if name == "conv_a_w":
        return [((), slice(R_CAW, R_CAW + 3), slice(64 * chip, 64 * chip + 64))]
    if name == "ssd_conv_w":
        return [((), slice(R_SCW, R_SCW + 4), slice(192 * chip, 192 * chip + 192))]
    if name == "gla_gate_w":
        return [((), slice(0, 16), slice(768, 896))]
    if name == "pool_w":
        return [((g, slice(16 * q, 16 * q + 16)), slice(16, 32), slice(256 * q + 64 * g, 256 * q + 64 * g + 64))
                for g in range(4) for q in range(4)]
    row, lanes = {"gla_gate_b": (R_GB, slice(0, 128)), "gla_norm_w": (R_GNW, slice(0, 64)),
                  "pool_scale": (R_PSC, slice(0, 256)), "ssd_conv_b": (R_SCB, slice(0, 768)),
                  "ssd_dt_bias": (R_DTB, slice(16, 20)), "ssd_a_log": (R_AE, slice(0, 4)), "ssd_d": (R_DE, slice(0, 4)),
                  "ssd_norm_w": (R_SNW, slice(0, 256))}[name]
    return [((), slice(row, row + 1), lanes)]


def _small_allreduce(sg0, sg1, dnw0, dnw1, head):
    def body(sg0_ref, sg1_ref, dnw0_ref, dnw1_ref, head_ref, acc, stage, pair, rbuf, send_sems, recv_sems):
        x, y, c = _place()
        chip = 2 * x + y
        stage[0:32, :] = sg0_ref[...]
        stage[32:64, :] = sg1_ref[...]
        stage[64:65, :] = dnw0_ref[0:1, :]
        stage[65:66, :] = dnw1_ref[0:1, :]
        stage[66:68, :] = head_ref[0:2, :]
        stage[68:72, :] = jnp.zeros((4, D), F32)
        sib = _remote(stage, pair, send_sems, recv_sems, 0, (x, y, 1 - c))
        sib.start()
        sib.wait()
        rbuf[0] = stage[...] + pair[...]
        sends = [_remote(rbuf.at[0], rbuf.at[k], send_sems, recv_sems, k, (px, py, c))
                 for k, (px, py) in enumerate(_other_chips(x, y), start=1)]
        for cp in sends:
            cp.start()
        for cp in sends:
            cp.wait()
        slab = lambda d: jnp.where(d == 0, 0, jnp.where(d == 2, 1, jnp.where(d == 1, 2, 3)))
        total = rbuf[slab(jnp.bitwise_xor(chip, 0))]
        for s in range(1, 4):
            total = total + rbuf[slab(jnp.bitwise_xor(chip, s))]
        acc[...] = total

    vmem = pl.BlockSpec(memory_space=pltpu.VMEM)
    return pl.pallas_call(
        body, name="small_allreduce", in_specs=[vmem] * 5, out_specs=vmem,
        out_shape=jax.ShapeDtypeStruct((SMALL_ROWS, D), F32),
        scratch_shapes=[pltpu.VMEM((SMALL_ROWS, D), F32), pltpu.VMEM((SMALL_ROWS, D), F32),
                        pltpu.VMEM((4, SMALL_ROWS, D), F32), _sem(4), _sem(4)],
    )(sg0, sg1, dnw0, dnw1, head)


def _small_adamw(acc, w, m, v):
    n = len(_SMALL_NAMES)

    def body(*refs):
        acc = refs[0]
        w_refs, m_refs, v_refs = refs[1:1 + n], refs[1 + n:1 + 2 * n], refs[1 + 2 * n:1 + 3 * n]
        o = 1 + 3 * n
        g_out, d_out, m_out, v_out = refs[o:o + n], refs[o + n:o + 2 * n], refs[o + 2 * n:o + 3 * n], refs[o + 3 * n:o + 4 * n]
        loss_ref = refs[o + 4 * n]
        chip = 2 * lax.axis_index("x") + lax.axis_index("y")
        loss_ref[...] = acc[67:68, 0:1]

        def update(i, idx, g):
            d, m2, v2 = _adam_math(w_refs[i][idx], g, m_refs[i][idx], v_refs[i][idx])
            g_out[i][idx], d_out[i][idx], m_out[i][idx], v_out[i][idx] = g, d, m2, v2

        for i, name in enumerate(_SMALL_NAMES):
            if name == "final_norm_w":
                update(i, (slice(0, 1), slice(None)), acc[66:67, :])
            elif name == "norm_w":
                for l in range(2):
                    update(i, (slice(l, l + 1), slice(None)), acc[64 + l:65 + l, :])
            elif name in ("conv_a_w", "ssd_conv_w"):
                for s in range(4):
                    @pl.when(chip == s)
                    def _(i=i, name=name, s=s):
                        for l in range(2):
                            (_, rows, lanes), = _small_slices(name, s)
                            update(i, (l,), acc[rows.start + 32 * l:rows.stop + 32 * l, lanes])
            else:
                for l in range(2):
                    for idx, rows, lanes in _small_slices(name, 0):
                        g = acc[rows.start + 32 * l:rows.stop + 32 * l, lanes]
                        if w_refs[i].ndim == 2:
                            update(i, (slice(l, l + 1), slice(None)), g)
                        else:
                            update(i, (l,) + idx, g)

    args = [acc] + [d[k] for d in (w, m, v) for k in _SMALL_NAMES]
    shapes = [jax.ShapeDtypeStruct(w[k].shape, F32) for k in _SMALL_NAMES]
    vmem = pl.BlockSpec(memory_space=pltpu.VMEM)
    outs = pl.pallas_call(body, name="small_adamw", in_specs=[vmem] * len(args), out_specs=[vmem] * (4 * n + 1),
                          out_shape=shapes * 4 + [jax.ShapeDtypeStruct((1, 1), F32)])(*args)
    return outs[0:n], outs[n:2 * n], outs[2 * n:3 * n], outs[3 * n:4 * n], outs[4 * n]


def _mixer_consts(layer, conv_a_w, gla_gate_w, gla_gate_b, gla_norm_w, pool_w, pool_scale, ssd_conv_w, ssd_conv_b,
                  ssd_dt_bias, ssd_a_log, ssd_d, ssd_norm_w):
    def row(v):
        return jnp.pad(v.reshape(1, -1), ((0, 0), (0, 768 - v.size)))

    dtb = jnp.zeros((128,), F32).at[16:20].set(ssd_dt_bias[layer])
    rows = [jnp.pad(conv_a_w[layer], ((0, 0), (0, 512))), row(gla_gate_b[layer]), row(jnp.tile(gla_norm_w[layer], 4)),
            row(pool_scale[layer]), row(ssd_conv_b[layer]), row(dtb), row(jnp.repeat(-jnp.exp(ssd_a_log[layer]), 64)),
            row(jnp.repeat(ssd_d[layer], 64)), row(ssd_norm_w[layer]), jnp.zeros((1, 768), F32), ssd_conv_w[layer]]
    prm = jnp.concatenate(rows, axis=0)
    gw = jnp.zeros((128, 128), F32).at[0:16].set(gla_gate_w[layer]).astype(BF16)
    pw = jnp.zeros((256, 256), F32)
    for g in range(4):
        pw = pw.at[64 * g:64 * g + 64, 64 * g:64 * g + 64].set(pool_w[layer, g])
    return (prm, gw, pw.astype(BF16)) + _mixer_matrices()


def _grad_slabs(layer, dwp, dwo):
    return _split_dw_in(dwp, name=f"split_dw_in{layer}"), dwo.reshape(4, D // 4, D)


class _Comm:
    def __init__(self, w_in, w_out):
        self.w_in16 = jnp.pad(w_in.astype(BF16), ((0, 0), (0, 0), (0, SHARD_PAD - SHARD)))
        self.w_out16 = w_out.astype(BF16)
        self.core = lax.axis_index("c").astype(jnp.int32).reshape(1)
        self.chip = 2 * lax.axis_index("x") + lax.axis_index("y")
        self.place = jnp.stack([lax.axis_index("c"), self.chip]).astype(jnp.int32)

    def gather_ici(self, layer, extra=None):
        return _rider_gather_ici((self.w_in16[layer], self.w_out16[layer]), extra)

    def pair_sum(self, layer, slabs, received):
        return [_pair_sum(self.core, a, b, name=f"reduce_pair_sum{layer}_{k}") for k, (a, b) in enumerate(zip(slabs, received))]

    def chip_sum(self, layer, gathered, mine):
        return [_chip_sum(self.place, a, b, name=f"reduce_chip_sum{layer}_{k}") for k, (a, b) in enumerate(zip(gathered, mine))]

    def layer_weights(self, layer, s_in, s_out):
        own = lambda slabs, shard: jnp.stack([jnp.where(self.chip == s, shard, slabs[s]) for s in range(4)])
        wp, wpt = _assemble_w_in(own(s_in, self.w_in16[layer]), name=f"assemble_w_in{layer}")
        wo = own(s_out, self.w_out16[layer]).reshape(D, D)
        return wp, wpt, wo, wo.T


def _local_step(x, tgt, norm_w, final_norm_w, consts, wts0, wts1=None, comm=None):
    nw = [norm_w[l:l + 1] for l in range(2)]
    proj0, h0, slabs = _rmsproj(x, nw[0], wts0[0], name="rmsproj0", rider=comm and comm.gather_ici(1))
    (mix0, sg0, ss0, x1), slabs = _mixer_fwd(proj0, x, wts0[2], *consts[0], name="mixer_fwd0",
                                             rider=comm and _rider_gather_d2d(slabs))
    if comm:
        wts1 = comm.layer_weights(1, *slabs)
    proj1, h1, _ = _rmsproj(x1, nw[1], wts1[0], name="rmsproj1")
    (mix1, sg1, ss1, x2), _ = _mixer_fwd(proj1, x1, wts1[2], *consts[1], name="mixer_fwd1")
    dx, head = _head(x2, tgt, final_norm_w.reshape(1, D), name="loss_head")
    (dproj, mgr1, dwo1), _ = _mixer_bwd(proj1, dx, wts1[3], mix1, sg1, ss1, *consts[1], name="mixer_bwd1")
    dwp1, _ = _dwin(h1, dproj, name="dwin1")
    slabs1 = comm and _grad_slabs(1, dwp1, dwo1)
    (dx, dnw1), recv = _dxin(dproj, wts1[1], x1, dx, nw[1], name="dxin1", rider=comm and _rider_swap(slabs1))
    pairs1 = comm and comm.pair_sum(1, slabs1, recv)
    (dproj, mgr0, dwo0), gathered = _mixer_bwd(proj0, dx, wts0[3], mix0, sg0, ss0, *consts[0], name="mixer_bwd0",
                                               rider=comm and _rider_scatter(pairs1))
    dwp0, big1 = _dwin(h0, dproj, name="dwin0", rider=comm and _rider_share(comm.chip_sum(1, gathered, pairs1)))
    scat = None
    if comm:
        slabs0 = _grad_slabs(0, dwp0, dwo0)
        pairs0 = comm.pair_sum(0, slabs0, _run_rider(_rider_swap(slabs0), "reduce_swap0"))
        scat = _rider_scatter(pairs0)
    last = x.shape[0] // 512 - 1
    part, gathered = _dxin(dproj, wts0[1], x, dx, nw[0], name="dxin0", rider=scat, tiles=(0, last))
    (dx, dnw0), _ = _dxin(dproj, wts0[1], x, dx, nw[0], name="dxin0_last", tiles=(last, last + 1), prev=part)
    if comm:
        big0 = _run_rider(_rider_share(comm.chip_sum(0, gathered, pairs0)), "reduce_share0")
        big = ((big0[0], big1[0]), (big0[1], big1[1]))
    else:
        big = ((dwp0, dwp1), (dwo0, dwo1))
    return head, dx, big, (dnw0, dnw1), (mgr0, mgr1)


def kernel(x, norm_w, w_in, conv_a_w, gla_gate_w, gla_gate_b, gla_norm_w, pool_w, pool_scale, ssd_conv_w, ssd_conv_b, ssd_dt_bias, ssd_a_log, ssd_d, ssd_norm_w, w_out, final_norm_w, loss_target, m_norm_w, m_w_in, m_conv_a_w, m_gla_gate_w, m_gla_gate_b, m_gla_norm_w, m_pool_w, m_pool_scale, m_ssd_conv_w, m_ssd_conv_b, m_ssd_dt_bias, m_ssd_a_log, m_ssd_d, m_ssd_norm_w, m_w_out, m_final_norm_w, v_norm_w, v_w_in, v_conv_a_w, v_gla_gate_w, v_gla_gate_b, v_gla_norm_w, v_pool_w, v_pool_scale, v_ssd_conv_w, v_ssd_conv_b, v_ssd_dt_bias, v_ssd_a_log, v_ssd_d, v_ssd_norm_w, v_w_out, v_final_norm_w):
    weights = dict(norm_w=norm_w, w_in=w_in, conv_a_w=conv_a_w, gla_gate_w=gla_gate_w, gla_gate_b=gla_gate_b,
                   gla_norm_w=gla_norm_w, pool_w=pool_w, pool_scale=pool_scale, ssd_conv_w=ssd_conv_w,
                   ssd_conv_b=ssd_conv_b, ssd_dt_bias=ssd_dt_bias, ssd_a_log=ssd_a_log, ssd_d=ssd_d,
                   ssd_norm_w=ssd_norm_w, w_out=w_out, final_norm_w=final_norm_w)
    m_in = dict(norm_w=m_norm_w, w_in=m_w_in, conv_a_w=m_conv_a_w, gla_gate_w=m_gla_gate_w, gla_gate_b=m_gla_gate_b,
                gla_norm_w=m_gla_norm_w, pool_w=m_pool_w, pool_scale=m_pool_scale, ssd_conv_w=m_ssd_conv_w,
                ssd_conv_b=m_ssd_conv_b, ssd_dt_bias=m_ssd_dt_bias, ssd_a_log=m_ssd_a_log, ssd_d=m_ssd_d,
                ssd_norm_w=m_ssd_norm_w, w_out=m_w_out, final_norm_w=m_final_norm_w)
    v_in = dict(norm_w=v_norm_w, w_in=v_w_in, conv_a_w=v_conv_a_w, gla_gate_w=v_gla_gate_w, gla_gate_b=v_gla_gate_b,
                gla_norm_w=v_gla_norm_w, pool_w=v_pool_w, pool_scale=v_pool_scale, ssd_conv_w=v_ssd_conv_w,
                ssd_conv_b=v_ssd_conv_b, ssd_dt_bias=v_ssd_dt_bias, ssd_a_log=v_ssd_a_log, ssd_d=v_ssd_d,
                ssd_norm_w=v_ssd_norm_w, w_out=v_w_out, final_norm_w=v_final_norm_w)
    order = ("norm_w", "w_in", "conv_a_w", "gla_gate_w", "gla_gate_b", "gla_norm_w", "pool_w", "pool_scale",
             "ssd_conv_w", "ssd_conv_b", "ssd_dt_bias", "ssd_a_log", "ssd_d", "ssd_norm_w", "w_out", "final_norm_w")
    t = x.shape[1]

    comm = _Comm(w_in, w_out)
    cshard = jnp.zeros((16, 256), F32)
    for l in range(2):
        cshard = cshard.at[8 * l:8 * l + 3, 0:64].set(conv_a_w[l]).at[8 * l + 3:8 * l + 7, 0:192].set(ssd_conv_w[l])
    s_in, s_out, g_c = _run_rider(comm.gather_ici(0, cshard), "gather_ici0")
    s_in, s_out = _run_rider(_rider_gather_d2d((s_in, s_out)), "gather_d2d0")
    g_c = [jnp.where(comm.chip == s, cshard, g_c[s]) for s in range(4)]
    conv_a_full = jnp.stack([jnp.concatenate([g_c[s][8 * l:8 * l + 3, 0:64] for s in range(4)], axis=-1) for l in range(2)])
    ssd_conv_full = jnp.stack([jnp.concatenate([g_c[s][8 * l + 3:8 * l + 7, 0:192] for s in range(4)], axis=-1)
                               for l in range(2)])
    consts = [_mixer_consts(l, conv_a_full, gla_gate_w, gla_gate_b, gla_norm_w, pool_w, pool_scale, ssd_conv_full,
                            ssd_conv_b, ssd_dt_bias, ssd_a_log, ssd_d, ssd_norm_w) for l in range(2)]

    head, dx, big, dnw, mgr = _local_step(x.reshape(t, D), loss_target.reshape(t, D), norm_w, final_norm_w, consts,
                                          comm.layer_weights(0, s_in, s_out), comm=comm)

    as2d = lambda d: {k: (d[k].reshape(1, D) if k == "final_norm_w" else d[k]) for k in _SMALL_NAMES}
    small = _small_adamw(_small_allreduce(mgr[0], mgr[1], dnw[0], dnw[1], head), as2d(weights), as2d(m_in), as2d(v_in))
    grads, delta, new_m, new_v = ({k: (a.reshape(D) if k == "final_norm_w" else a) for k, a in zip(_SMALL_NAMES, part)}
                                  for part in small[0:4])
    loss = small[4].reshape(())

    grads["w_out"] = jnp.stack(big[1])

    grads["w_in"], delta["w_in"], new_m["w_in"], new_v["w_in"] = _adamw_w_in(w_in, big[0], m_w_in, v_w_in, name="adamw_w_in")
    delta["w_out"], new_m["w_out"], new_v["w_out"] = _adamw(w_out, grads["w_out"], m_w_out, v_w_out, name="adamw_w_out", br=256)

    return (loss, dx.reshape(1, t, D), *[grads[k] for k in order], *[delta[k] for k in order],
            *[new_m[k] for k in order], *[new_v[k] for k in order])
```

```python
import functools

import jax
import jax.numpy as jnp
from jax import lax
from jax.experimental import pallas as pl
from jax.experimental.pallas import tpu as pltpu

F32 = jnp.float32
BF16 = jnp.bfloat16
MESH = pl.DeviceIdType.MESH

D = 1024
CH = 64
EPS = 1e-6
NP = 3456
NPROJ = 3348
GLA_SCALE = 32.0 ** -0.5
INV_TAU = 1.0 / 16.0
TB = 256
NCH = TB // CH
assert TB == 256
HALO_W = NP

C_AH, C_AB, C_AC, C_AZ, C_GQ, C_GK, C_GV = 0, 256, 512, 768, 1024, 1152, 1280
C_GZ, C_PU, C_PZ, C_SZ, C_SX, C_TL = 1536, 1792, 2048, 2304, 2560, 3328
_PERM = ((0, 1536), (1552, 1792), (1536, 16), (3344, 4))
_UNPERM = ((0, 1536), (3328, 16), (1536, 1792), (3344, 4))

R_CAW, R_GB, R_GNW, R_PSC, R_SCB, R_DTB, R_AE, R_DE, R_SNW, R_SCW = 0, 3, 4, 5, 6, 7, 8, 9, 10, 12

ADAM_LR, ADAM_B1, ADAM_B2, ADAM_EPS, ADAM_WD, ADAM_STEP = 0.001, 0.9, 0.999, 1e-08, 0.01, 10

VMEM_LIMIT = 56 * 1024 * 1024


def _cparams(sem, limit=VMEM_LIMIT):
    return pltpu.CompilerParams(dimension_semantics=sem, vmem_limit_bytes=limit)


_ANY = pl.BlockSpec(memory_space=pl.ANY)


def _place():
    return lax.axis_index("x"), lax.axis_index("y"), lax.axis_index("c")


class _Rider:
    def __init__(self, inputs, out_shapes, sems, start, finish, aliases=None):
        self.inputs, self.out_shapes, self.sems = tuple(inputs), tuple(out_shapes), tuple(sems)
        self.start, self.finish, self.aliases = start, finish, dict(aliases or {})


def _call(body, args, *, grid, in_specs, out_specs, out_shape, name, sem, scratch_shapes=(), rider=None):
    if rider is None:
        outs = pl.pallas_call(body, grid=grid, name=name, in_specs=list(in_specs), out_specs=list(out_specs),
                              out_shape=list(out_shape), scratch_shapes=list(scratch_shapes),
                              compiler_params=_cparams(sem))(*args)
        return list(outs), []
    ni, no, ns = len(args), len(out_shape), len(scratch_shapes)
    ri, ro = len(rider.inputs), len(rider.out_shapes)

    def full(*refs):
        ins, rins = refs[:ni], refs[ni:ni + ri]
        outs, routs = refs[ni + ri:ni + ri + no], refs[ni + ri + no:ni + ri + no + ro]
        scr, rsem = refs[ni + ri + no + ro:ni + ri + no + ro + ns], refs[ni + ri + no + ro + ns:]
        first = functools.reduce(jnp.logical_and, [pl.program_id(a) == 0 for a in range(len(grid))])
        last = functools.reduce(jnp.logical_and, [pl.program_id(a) == grid[a] - 1 for a in range(len(grid))])

        @pl.when(first)
        def _():
            rider.start(rins, routs, rsem)

        body(*ins, *outs, *scr)

        @pl.when(last)
        def _():
            rider.finish(rins, routs, rsem)

    outs = pl.pallas_call(
        full, grid=grid, name=name, in_specs=list(in_specs) + [_ANY] * ri, out_specs=list(out_specs) + [_ANY] * ro,
        out_shape=list(out_shape) + list(rider.out_shapes), scratch_shapes=list(scratch_shapes) + list(rider.sems),
        input_output_aliases={ni + k: no + v for k, v in rider.aliases.items()},
        compiler_params=_cparams(("arbitrary",) * len(grid)))(*args, *rider.inputs)
    return list(outs[:no]), list(outs[no:])


def _run_rider(rider, name):
    ri = len(rider.inputs)

    def body(*refs):
        rins, routs, rsem = refs[:ri], refs[ri:ri + len(rider.out_shapes)], refs[ri + len(rider.out_shapes):]
        rider.start(rins, routs, rsem)
        rider.finish(rins, routs, rsem)

    return list(pl.pallas_call(body, name=name, in_specs=[_ANY] * ri, out_specs=[_ANY] * len(rider.out_shapes),
                               out_shape=list(rider.out_shapes), scratch_shapes=list(rider.sems),
                               input_output_aliases=dict(rider.aliases))(*rider.inputs))


def _dot(a, b):
    return jnp.dot(a.astype(BF16), b.astype(BF16), preferred_element_type=F32)


def _dot_nt(a, b):
    return lax.dot_general(a.astype(BF16), b.astype(BF16), (((1,), (1,)), ((), ())), preferred_element_type=F32)


def _dot_tn(a, b):
    return lax.dot_general(a.astype(BF16), b.astype(BF16), (((0,), (0,)), ((), ())), preferred_element_type=F32)


def _split(a):
    hi = a.astype(BF16)
    lo = (a - hi.astype(F32)).astype(BF16)
    return hi, lo


def _dot2_l(a, b):
    hi, lo = _split(a)
    return _dot(hi, b) + _dot(lo, b)


def _dot2_r(a, b):
    hi, lo = _split(b)
    return _dot(a, hi) + _dot(a, lo)


def _dot3_l(a, b):
    hi, lo = _split(a)
    lo2 = ((a - hi.astype(F32)) - lo.astype(F32)).astype(BF16)
    return _dot(hi, b) + _dot(lo, b) + _dot(lo2, b)


def _dot2_nt(a, b):
    hi, lo = _split(a)
    return _dot_nt(hi, b) + _dot_nt(lo, b)


def _silu(z):
    return z * jax.nn.sigmoid(z)


def _lse1(x):
    return jnp.log(1.0 + jnp.exp(-jnp.abs(x)))


def _cs(a):
    return jnp.sum(a, axis=0, keepdims=True)


def _iota(shape, dim):
    return lax.broadcasted_iota(jnp.int32, shape, dim)


def _mixer_matrices():
    r, c = _iota((256, 256), 0), _iota((256, 256), 1)
    same_chunk = (r >> 6) == (c >> 6)
    mats = jnp.stack([jnp.where((c > r) & same_chunk, 1.0, 0.0), jnp.where((c < r) & same_chunk, 1.0, 0.0),
                      jnp.where(same_chunk, 1.0 / 64.0, 0.0), jnp.where((r < 128) & (r - 16 == (c >> 6)), 1.0, 0.0)])
    mask = jnp.where((_iota((256, 128), 0) >> 6) == (_iota((256, 128), 1) >> 5), 1.0, 0.0)
    return mats.astype(BF16), mask.astype(F32)


def _dn(ext, k, n, h):
    return pltpu.roll(ext, k, axis=0)[h:h + n]


def _up(ext, k, n):
    return pltpu.roll(ext, ext.shape[0] - k, axis=0)[:n]


def _pool_lane_select(lane, s2, s4, s8, s16):
    return jnp.where(lane < 64, s2, jnp.where(lane < 128, s4, jnp.where(lane < 192, s8, s16)))


def _winsum_dn(ext, lane):
    s2 = ext + pltpu.roll(ext, 1, axis=0)
    s4 = s2 + pltpu.roll(s2, 2, axis=0)
    s8 = s4 + pltpu.roll(s4, 4, axis=0)
    s16 = s8 + pltpu.roll(s8, 8, axis=0)
    return _pool_lane_select(lane, s2, s4, s8, s16)


def _winsum_up(ext, lane):
    m = ext.shape[0]
    s2 = ext + pltpu.roll(ext, m - 1, axis=0)
    s4 = s2 + pltpu.roll(s2, m - 2, axis=0)
    s8 = s4 + pltpu.roll(s4, m - 4, axis=0)
    s16 = s8 + pltpu.roll(s8, m - 8, axis=0)
    return _pool_lane_select(lane, s2, s4, s8, s16)


def _pool_inv_count(tile, n):
    lane = _iota((1, 256), 1)
    win = _pool_lane_select(lane, 2.0, 4.0, 8.0, 16.0).astype(F32)
    tpos = (tile * n + _iota((n, 1), 0) + 1).astype(F32)
    return jnp.where(tpos >= win, 1.0 / win, 1.0 / tpos)


def _silu_pair(z):
    s = jax.nn.sigmoid(z)
    return z * s, s * (1.0 + z * (1.0 - s))


def _chunks(a):
    return [a[c * CH:(c + 1) * CH] for c in range(a.shape[0] // CH)]


def _halves(fn, a, b):
    return jnp.concatenate([fn(a[:, 0:128], b[:, 0:128]), fn(a[:, 128:256], b[:, 128:256])], axis=1)


def _mixer_tile_prep(p_ref, xc, prm_ref, gw_v, cm_ref, mk_ref):
    tail = p_ref[:, C_TL:C_TL + 128]
    pre = _dot(tail, gw_v) + prm_ref[R_GB:R_GB + 1, 0:128]
    la = (jnp.minimum(pre, 0.0) - _lse1(pre)) * INV_TAU
    dtin = tail + prm_ref[R_DTB:R_DTB + 1, 0:128]
    dtf = jnp.maximum(dtin, 0.0) + _lse1(dtin)
    dte = _dot2_l(dtf, cm_ref[3, 0:128, :])
    da = dte * prm_ref[R_AE:R_AE + 1, 0:256]
    rev = _dot2_r(cm_ref[0], jnp.concatenate([la, da], axis=1))
    dec = jnp.exp(rev[:, 0:128])
    kd = p_ref[:, C_GK:C_GK + 128] * dec
    wdec = jnp.exp(rev[:, 128:384])
    w = wdec * dte
    xw = xc[:, 0:256] * w
    d_s = [jnp.exp(_cs(a)) for a in _chunks(la)]
    et = [jnp.exp(_cs(a)) for a in _chunks(da)]
    mask_t = mk_ref[...]
    ut_g = [_dot_tn(v, k) * mask_t for v, k in zip(_chunks(p_ref[:, C_GV:C_GV + 256]), _chunks(kd))]
    ut_s = [_halves(_dot_tn, b, x) for b, x in zip(_chunks(xc[:, 256:512]), _chunks(xw))]
    return tail, pre, dtin, dte, dec, kd, wdec, w, xw, d_s, et, ut_g, ut_s


def _rmsproj(x, nw, wp, name, tm=512, rider=None):
    t = x.shape[0]

    def body(x_ref, nw_ref, w_ref, o_ref, h_ref):
        for rows in (pl.ds(0, tm // 2), pl.ds(tm // 2, tm // 2)):
            xv = x_ref[rows, :]
            rs = lax.rsqrt(jnp.mean(xv * xv, axis=-1, keepdims=True) + EPS)
            h = (xv * rs * nw_ref[...]).astype(BF16)
            h_ref[rows, :] = h
            o_ref[rows, :] = jnp.dot(h, w_ref[...], preferred_element_type=F32)

    (proj, h), extra = _call(
        body, (x, nw, wp), grid=(t // tm,), name=name, sem=("parallel",), rider=rider,
        in_specs=[pl.BlockSpec((tm, D), lambda i: (i, 0)), pl.BlockSpec((1, D), lambda i: (0, 0)),
                  pl.BlockSpec((D, NP), lambda i: (0, 0))],
        out_specs=[pl.BlockSpec((tm, NP), lambda i: (i, 0)), pl.BlockSpec((tm, D), lambda i: (i, 0))],
        out_shape=[jax.ShapeDtypeStruct((t, NP), F32), jax.ShapeDtypeStruct((t, D), BF16)])
    return proj, h, extra


def _head(x, tgt, fw, name, tm=512):
    t = x.shape[0]

    def body(x_ref, t_ref, w_ref, dx_ref, acc_ref):
        @pl.when(pl.program_id(0) == 0)
        def _():
            acc_ref[...] = jnp.zeros_like(acc_ref)

        xv = x_ref[...]
        w = w_ref[...]
        rs = lax.rsqrt(jnp.mean(xv * xv, axis=-1, keepdims=True) + EPS)
        xh = xv * rs
        err = xh * w - t_ref[...]
        dy = err * (1.0 / D)
        dxh = dy * w
        dx_ref[...] = rs * (dxh - xh * jnp.mean(dxh * xh, axis=-1, keepdims=True))
        acc_ref[0:1, :] += _cs(dy * xh)
        acc_ref[1:2, :] += jnp.zeros((1, D), F32) + (0.5 / D) * jnp.sum(err * err)

    return pl.pallas_call(
        body, grid=(t // tm,), name=name,
        in_specs=[pl.BlockSpec((tm, D), lambda i: (i, 0)), pl.BlockSpec((tm, D), lambda i: (i, 0)),
                  pl.BlockSpec((1, D), lambda i: (0, 0))],
        out_specs=[pl.BlockSpec((tm, D), lambda i: (i, 0)), pl.BlockSpec((8, D), lambda i: (0, 0))],
        out_shape=[jax.ShapeDtypeStruct((t, D), F32), jax.ShapeDtypeStruct((8, D), F32)],
        compiler_params=_cparams(("arbitrary",)),
    )(x, tgt, fw)


def _dxin(dp, wpt, x, dxn, nw, name, tm=512, rider=None):
    t = x.shape[0]

    def body(dp_ref, w_ref, x_ref, dxn_ref, nw_ref, dx_ref, dnw_ref):
        @pl.when(pl.program_id(0) == 0)
        def _():
            dnw_ref[...] = jnp.zeros_like(dnw_ref)

        acc = jnp.zeros((1, D), F32)
        for rows in (pl.ds(0, tm // 2), pl.ds(tm // 2, tm // 2)):
            dh = jnp.dot(dp_ref[rows, :].astype(BF16), w_ref[...], preferred_element_type=F32)
            xv = x_ref[rows, :]
            rs = lax.rsqrt(jnp.mean(xv * xv, axis=-1, keepdims=True) + EPS)
            xh = xv * rs
            acc = acc + _cs(dh * xh)
            dxh = dh * nw_ref[...]
            dx_ref[rows, :] = dxn_ref[rows, :] + rs * (dxh - xh * jnp.mean(dxh * xh, axis=-1, keepdims=True))
        dnw_ref[0:1, :] += acc

    return _call(
        body, (dp, wpt, x, dxn, nw), grid=(t // tm,), name=name, sem=("arbitrary",), rider=rider,
        in_specs=[pl.BlockSpec((tm, NP), lambda i: (i, 0)), pl.BlockSpec((NP, D), lambda i: (0, 0)),
                  pl.BlockSpec((tm, D), lambda i: (i, 0)), pl.BlockSpec((tm, D), lambda i: (i, 0)),
                  pl.BlockSpec((1, D), lambda i: (0, 0))],
        out_specs=[pl.BlockSpec((tm, D), lambda i: (i, 0)), pl.BlockSpec((8, D), lambda i: (0, 0))],
        out_shape=[jax.ShapeDtypeStruct((t, D), F32), jax.ShapeDtypeStruct((8, D), F32)])


def _dwin(h, dp, name, tm=1024, tn=NP, rider=None):
    t = h.shape[0]

    def body(h_ref, dp_ref, o_ref):
        @pl.when(pl.program_id(1) == 0)
        def _():
            o_ref[...] = jnp.zeros_like(o_ref)

        o_ref[...] += _dot_tn(h_ref[...], dp_ref[...])

    (dwp,), extra = _call(
        body, (h, dp), grid=(NP // tn, t // tm), name=name, sem=("parallel", "arbitrary"), rider=rider,
        in_specs=[pl.BlockSpec((tm, D), lambda j, i: (i, 0)), pl.BlockSpec((tm, tn), lambda j, i: (i, j))],
        out_specs=[pl.BlockSpec((D, tn), lambda j, i: (0, j))], out_shape=[jax.ShapeDtypeStruct((D, NP), F32)])
    return dwp, extra


def _mixer_fwd(proj, x, wo, prm, gw, pw, cmat, mask, name, rider=None):
    t = proj.shape[0]
    nt, nc = t // TB, t // CH

    def body(p_ref, x_ref, wo_ref, prm_ref, gw_ref, pw_ref, cm_ref, mk_ref, mix_ref, sg_ref, ss_ref, xn_ref,
             sg_s, ss_s, h_ua, h_pu, h_sx):
        i = pl.program_id(0)

        @pl.when(i == 0)
        def _():
            for r in (sg_s, ss_s, h_ua, h_pu, h_sx):
                r[...] = jnp.zeros_like(r)

        lane = _iota((1, 256), 1)
        u = p_ref[:, C_AC:C_AC + 256] * p_ref[:, C_AH:C_AH + 256]
        ext = jnp.concatenate([h_ua[...], u], axis=0)
        cv = (prm_ref[R_CAW + 2:R_CAW + 3, 0:256] * u + prm_ref[R_CAW + 1:R_CAW + 2, 0:256] * _dn(ext, 1, TB, 8)
              + prm_ref[R_CAW:R_CAW + 1, 0:256] * _dn(ext, 2, TB, 8))
        mix_ref[:, 0:256] = (p_ref[:, C_AB:C_AB + 256] * cv * _silu(p_ref[:, C_AZ:C_AZ + 256])).astype(BF16)
        h_ua[...] = u[TB - 8:, :]
        pu = p_ref[:, C_PU:C_PU + 256]
        ext = jnp.concatenate([h_pu[...], pu], axis=0)
        pooled = _winsum_dn(ext, lane)[16:] * _pool_inv_count(i, TB) - pu
        mixed = _dot(pooled, pw_ref[...])
        mix_ref[:, 512:768] = (prm_ref[R_PSC:R_PSC + 1, 0:256] * mixed * _silu(p_ref[:, C_PZ:C_PZ + 256])).astype(BF16)
        h_pu[...] = pu[TB - 16:, :]
        sx = p_ref[:, C_SX:C_SX + 768]
        ext = jnp.concatenate([h_sx[...], sx], axis=0)
        xc = _silu(prm_ref[R_SCW + 3:R_SCW + 4, :] * sx + prm_ref[R_SCW + 2:R_SCW + 3, :] * _dn(ext, 1, TB, 8)
                   + prm_ref[R_SCW + 1:R_SCW + 2, :] * _dn(ext, 2, TB, 8) + prm_ref[R_SCW:R_SCW + 1, :] * _dn(ext, 3, TB, 8)
                   + prm_ref[R_SCB:R_SCB + 1, :])
        h_sx[...] = sx[TB - 8:, :]

        _, _, _, _, _, _, _, _, _, d_s, et, ut_g, ut_s = _mixer_tile_prep(p_ref, xc, prm_ref, gw_ref[...], cm_ref, mk_ref)
        s_g, s_s = sg_s[...], ss_s[...]
        o, y = [], []
        qs = _chunks(p_ref[:, C_GQ:C_GQ + 128] * GLA_SCALE)
        cm = _chunks(xc[:, 512:768])
        for c in range(NCH):
            sg_ref[c] = s_g
            ss_ref[c] = s_s
            s_g = s_g * d_s[c] + ut_g[c]
            s_s = s_s * et[c] + ut_s[c]
            o.append(_dot_nt(qs[c], s_g))
            y.append(_halves(_dot, cm[c], s_s))
        sg_s[...] = s_g
        ss_s[...] = s_s
        o = jnp.concatenate(o, axis=0)
        on = o * lax.rsqrt(_dot2_l(o * o, cm_ref[2]) + EPS)
        mix_ref[:, 256:512] = (on * prm_ref[R_GNW:R_GNW + 1, 0:256] * _silu(p_ref[:, C_GZ:C_GZ + 256])).astype(BF16)
        y2 = ((jnp.concatenate(y, axis=0) + prm_ref[R_DE:R_DE + 1, 0:256] * xc[:, 0:256])
              * _silu(p_ref[:, C_SZ:C_SZ + 256]))
        mix_ref[:, 768:1024] = (y2 * lax.rsqrt(jnp.mean(y2 * y2, axis=-1, keepdims=True) + EPS)
                                * prm_ref[R_SNW:R_SNW + 1, 0:256]).astype(BF16)
        xn_ref[...] = x_ref[...] + jnp.dot(mix_ref[...], wo_ref[...], preferred_element_type=F32)

    return _call(
        body, (proj, x, wo, prm, gw, pw, cmat, mask), grid=(nt,), name=name, sem=("arbitrary",), rider=rider,
        in_specs=[pl.BlockSpec((TB, NP), lambda i: (i, 0)), pl.BlockSpec((TB, D), lambda i: (i, 0)),
                  pl.BlockSpec((D, D), lambda i: (0, 0)), pl.BlockSpec((16, 768), lambda i: (0, 0)),
                  pl.BlockSpec((128, 128), lambda i: (0, 0)), pl.BlockSpec((256, 256), lambda i: (0, 0)),
                  pl.BlockSpec((4, 256, 256), lambda i: (0, 0, 0)), pl.BlockSpec((256, 128), lambda i: (0, 0))],
        out_specs=[pl.BlockSpec((TB, D), lambda i: (i, 0)), pl.BlockSpec((NCH, 256, 128), lambda i: (i, 0, 0)),
                   pl.BlockSpec((NCH, 128, 256), lambda i: (i, 0, 0)), pl.BlockSpec((TB, D), lambda i: (i, 0))],
        out_shape=[jax.ShapeDtypeStruct((t, D), BF16), jax.ShapeDtypeStruct((nc, 256, 128), F32),
                   jax.ShapeDtypeStruct((nc, 128, 256), F32), jax.ShapeDtypeStruct((t, D), F32)],
        scratch_shapes=[pltpu.VMEM((256, 128), F32), pltpu.VMEM((128, 256), F32), pltpu.VMEM((8, 256), F32),
                        pltpu.VMEM((16, 256), F32), pltpu.VMEM((8, 768), F32)])


def _mixer_bwd(proj, dxn, wot, mix, sg, ss, prm, gw, pw, cmat, mask, name, rider=None):
    t = proj.shape[0]
    nt = t // TB
    rev = lambda i: nt - 1 - i

    def body(p_ref, hp_ref, dxn_ref, wot_ref, mix_ref, sg_ref, ss_ref, prm_ref, gw_ref, pw_ref, cm_ref, mk_ref,
             dp_ref, sgc_ref, dwo_ref,
             gg_s, gs_s, h_dcv, h_dpl, h_dpre, gsm_ref, dgw_ref, dpw_ref, dm_ref):
        i = pl.program_id(0)
        tile = nt - 1 - i

        @pl.when(i == 0)
        def _():
            for r in (gg_s, gs_s, h_dcv, h_dpl, h_dpre, gsm_ref, dgw_ref, dpw_ref, dwo_ref):
                r[...] = jnp.zeros_like(r)

        dxn = dxn_ref[...].astype(BF16)
        dm_ref[...] = jnp.dot(dxn, wot_ref[...], preferred_element_type=F32)
        dwo_ref[...] += _dot_tn(mix_ref[...], dxn)

        lane = _iota((1, 256), 1)
        first = (tile > 0).astype(F32)
        ah, ac = p_ref[:, C_AH:C_AH + 256], p_ref[:, C_AC:C_AC + 256]
        ab, az = p_ref[:, C_AB:C_AB + 256], p_ref[:, C_AZ:C_AZ + 256]
        w0, w1, w2 = (prm_ref[R_CAW + j:R_CAW + j + 1, 0:256] for j in range(3))
        u = ac * ah
        ext = jnp.concatenate([hp_ref[8:16, C_AC:C_AC + 256] * hp_ref[8:16, C_AH:C_AH + 256] * first, u], axis=0)
        u1, u2 = _dn(ext, 1, TB, 8), _dn(ext, 2, TB, 8)
        cv = w2 * u + w1 * u1 + w0 * u2
        g = dm_ref[:, 0:256]
        sz, dsz = _silu_pair(az)
        dp_ref[:, C_AB:C_AB + 256] = (g * cv * sz).astype(BF16)
        dp_ref[:, C_AZ:C_AZ + 256] = (g * ab * cv * dsz).astype(BF16)
        dcv = g * ab * sz
        dext = jnp.concatenate([dcv, h_dcv[...]], axis=0)
        du = w2 * dcv + w1 * _up(dext, 1, TB) + w0 * _up(dext, 2, TB)
        dp_ref[:, C_AC:C_AC + 256] = (du * ah).astype(BF16)
        dp_ref[:, C_AH:C_AH + 256] = (du * ac).astype(BF16)
        gsm_ref[R_CAW:R_CAW + 1, 0:256] += _cs(dcv * u2)
        gsm_ref[R_CAW + 1:R_CAW + 2, 0:256] += _cs(dcv * u1)
        gsm_ref[R_CAW + 2:R_CAW + 3, 0:256] += _cs(dcv * u)
        h_dcv[...] = dcv[0:8, :]
        pu, pz = p_ref[:, C_PU:C_PU + 256], p_ref[:, C_PZ:C_PZ + 256]
        psc = prm_ref[R_PSC:R_PSC + 1, 0:256]
        icnt = _pool_inv_count(tile, TB)
        ext = jnp.concatenate([hp_ref[:, C_PU:C_PU + 256] * first, pu], axis=0)
        pooled = _winsum_dn(ext, lane)[16:] * icnt - pu
        pw_v = pw_ref[...]
        mixed = _dot(pooled, pw_v)
        g = dm_ref[:, 512:768]
        sz, dsz = _silu_pair(pz)
        gsm_ref[R_PSC:R_PSC + 1, 0:256] += _cs(g * mixed * sz)
        dp_ref[:, C_PZ:C_PZ + 256] = (g * psc * mixed * dsz).astype(BF16)
        dmixed = g * psc * sz
        dpw_ref[...] += _dot_tn(pooled, dmixed)
        dpooled = _dot_nt(dmixed, pw_v)
        qd = dpooled * icnt
        dext = jnp.concatenate([qd, h_dpl[...]], axis=0)
        dp_ref[:, C_PU:C_PU + 256] = (_winsum_up(dext, lane)[:TB] - dpooled).astype(BF16)
        h_dpl[...] = qd[0:16, :]
        sx = p_ref[:, C_SX:C_SX + 768]
        cw = [prm_ref[R_SCW + j:R_SCW + j + 1, :] for j in range(4)]
        ext = jnp.concatenate([hp_ref[8:16, C_SX:C_SX + 768] * first, sx], axis=0)
        sx1, sx2, sx3 = _dn(ext, 1, TB, 8), _dn(ext, 2, TB, 8), _dn(ext, 3, TB, 8)
        cpre = cw[3] * sx + cw[2] * sx1 + cw[1] * sx2 + cw[0] * sx3 + prm_ref[R_SCB:R_SCB + 1, :]
        xc, dxc = _silu_pair(cpre)
        xs, bm, cm = xc[:, 0:256], xc[:, 256:512], xc[:, 512:768]

        gw_v = gw_ref[...]
        tail, pre, dtin, dte, dec, kd, wdec, w, xw, d_s, et, ut_g, ut_s = _mixer_tile_prep(p_ref, xc, prm_ref, gw_v,
                                                                                          cm_ref, mk_ref)
        gmean = cm_ref[2]
        mask_t = mk_ref[...]
        gnw = prm_ref[R_GNW:R_GNW + 1, 0:256]
        a_e = prm_ref[R_AE:R_AE + 1, 0:256]
        d_e = prm_ref[R_DE:R_DE + 1, 0:256]
        snw = prm_ref[R_SNW:R_SNW + 1, 0:256]
        sg_in = [sg_ref[c] for c in range(NCH)]
        ss_in = [ss_ref[c] for c in range(NCH)]
        sg_n = [sg_in[c] * d_s[c] + ut_g[c] for c in range(NCH)]
        ss_n = [ss_in[c] * et[c] + ut_s[c] for c in range(NCH)]
        qs = _chunks(p_ref[:, C_GQ:C_GQ + 128] * GLA_SCALE)
        cm_c, bm_c, xw_c, kd_c = _chunks(cm), _chunks(bm), _chunks(xw), _chunks(kd)
        v_c = _chunks(p_ref[:, C_GV:C_GV + 256])
        o = jnp.concatenate([_dot_nt(qs[c], sg_n[c]) for c in range(NCH)], axis=0)
        y = jnp.concatenate([_halves(_dot, cm_c[c], ss_n[c]) for c in range(NCH)], axis=0) + d_e * xs
        gz = p_ref[:, C_GZ:C_GZ + 256]
        r = lax.rsqrt(_dot2_l(o * o, gmean) + EPS)
        on = o * r
        dyb = dm_ref[:, 256:512]
        sz, dsz = _silu_pair(gz)
        dp_ref[:, C_GZ:C_GZ + 256] = (dyb * on * gnw * dsz).astype(BF16)
        tg = dyb * sz
        gsm_ref[R_GNW:R_GNW + 1, 0:256] += _cs(tg * on)
        don = tg * gnw
        do_c = _chunks(r * (don - on * _dot2_l(don * on, gmean)))
        ssz = p_ref[:, C_SZ:C_SZ + 256]
        sil, dsil = _silu_pair(ssz)
        y2 = y * sil
        r = lax.rsqrt(jnp.mean(y2 * y2, axis=-1, keepdims=True) + EPS)
        yn = y2 * r
        dyd = dm_ref[:, 768:1024]
        gsm_ref[R_SNW:R_SNW + 1, 0:256] += _cs(dyd * yn)
        dn = dyd * snw
        dy2 = r * (dn - yn * jnp.mean(dn * yn, axis=-1, keepdims=True))
        dp_ref[:, C_SZ:C_SZ + 256] = (dy2 * y * dsil).astype(BF16)
        dy = dy2 * sil
        gsm_ref[R_DE:R_DE + 1, 0:256] += _cs(dy * xs)
        dy_c = _chunks(dy)
        dq = jnp.concatenate([_dot(do_c[c], sg_n[c]) for c in range(NCH)], axis=0)
        dp_ref[:, C_GQ:C_GQ + 128] = (dq * GLA_SCALE).astype(BF16)
        dcm = jnp.concatenate([_halves(_dot_nt, dy_c[c], ss_n[c]) for c in range(NCH)], axis=0)
        gg = [_dot_tn(do_c[c], qs[c]) * mask_t for c in range(NCH)]
        gs = [_halves(_dot_tn, cm_c[c], dy_c[c]) for c in range(NCH)]
        car_g, car_s = gg_s[...], gs_s[...]
        for c in reversed(range(NCH)):
            gg[c] = gg[c] + car_g
            gs[c] = gs[c] + car_s
            car_g = gg[c] * d_s[c]
            car_s = gs[c] * et[c]
        gg_s[...] = car_g
        gs_s[...] = car_s
        dkd = jnp.concatenate([_dot(v_c[c], gg[c]) for c in range(NCH)], axis=0)
        dp_ref[:, C_GV:C_GV + 256] = jnp.concatenate([_dot_nt(kd_c[c], gg[c]) for c in range(NCH)], axis=0).astype(BF16)
        dp_ref[:, C_GK:C_GK + 128] = (dkd * dec).astype(BF16)
        dbm = jnp.concatenate([_halves(_dot_nt, xw_c[c], gs[c]) for c in range(NCH)], axis=0)
        dxw = jnp.concatenate([_halves(_dot, bm_c[c], gs[c]) for c in range(NCH)], axis=0)
        dxs = dy * d_e + dxw * w
        dw = dxw * xs
        dsuf = _dot2_r(cm_ref[1], jnp.concatenate([dkd * kd, dw * dte * wdec], axis=1))
        tot_g = jnp.concatenate([jnp.broadcast_to(_cs(gg[c] * sg_in[c]) * d_s[c], (CH, 128)) for c in range(NCH)], axis=0)
        tot_s = jnp.concatenate([jnp.broadcast_to(_cs(gs[c] * ss_in[c]) * et[c], (CH, 256)) for c in range(NCH)], axis=0)
        dpre = (dsuf[:, 0:128] + tot_g) * INV_TAU * jax.nn.sigmoid(-pre)
        dgw_ref[...] += _dot_tn(tail, dpre)
        gsm_ref[R_GB:R_GB + 1, 0:128] += _cs(dpre)
        dda = dsuf[:, 128:384] + tot_s
        gsm_ref[R_AE:R_AE + 1, 0:256] += _cs(dda * dte)
        dtail_s = _dot2_nt(dw * wdec + dda * a_e, cm_ref[3, 0:128, :]) * jax.nn.sigmoid(dtin)
        gsm_ref[R_DTB:R_DTB + 1, 0:128] += _cs(dtail_s)
        dp_ref[:, C_TL:C_TL + 128] = (_dot_nt(dpre, gw_v) + dtail_s).astype(BF16)
        dpre_c = jnp.concatenate([dxs, dbm, dcm], axis=1) * dxc
        dext = jnp.concatenate([dpre_c, h_dpre[...]], axis=0)
        dp_ref[:, C_SX:C_SX + 768] = (cw[3] * dpre_c + cw[2] * _up(dext, 1, TB) + cw[1] * _up(dext, 2, TB)
                                      + cw[0] * _up(dext, 3, TB)).astype(BF16)
        gsm_ref[R_SCW + 3:R_SCW + 4, :] += _cs(dpre_c * sx)
        gsm_ref[R_SCW + 2:R_SCW + 3, :] += _cs(dpre_c * sx1)
        gsm_ref[R_SCW + 1:R_SCW + 2, :] += _cs(dpre_c * sx2)
        gsm_ref[R_SCW:R_SCW + 1, :] += _cs(dpre_c * sx3)
        gsm_ref[R_SCB:R_SCB + 1, :] += _cs(dpre_c)
        h_dpre[...] = dpre_c[0:8, :]

        @pl.when(i == nt - 1)
        def _():
            ri, ci = _iota((256, 256), 0), _iota((256, 256), 1)
            per_head = jnp.where((ri >> 6) == ci, 1.0, 0.0).astype(BF16)
            per_dv = jnp.where((ri & 63) == ci, 1.0, 0.0).astype(BF16)
            row = _iota((8, 256), 0)
            top = gsm_ref[0:8, 0:256]
            sgc_ref[0:8, 0:256] = jnp.where(row == R_GNW, _dot3_l(top, per_dv), top)
            bot = gsm_ref[8:16, 0:256]
            fold = _dot3_l(jnp.where(row == R_AE - 8, bot * a_e, bot), per_head)
            sgc_ref[8:16, 0:256] = jnp.where((row == R_AE - 8) | (row == R_DE - 8), fold, bot)
            sgc_ref[0:16, 256:768] = gsm_ref[:, 256:768]
            sgc_ref[0:16, 768:896] = dgw_ref[0:16, :]
            sgc_ref[0:16, 896:1024] = jnp.zeros((16, 128), F32)
            diag = _pool_lane_select(lane, dpw_ref[0:64, :], dpw_ref[64:128, :], dpw_ref[128:192, :], dpw_ref[192:256, :])
            for q in range(4):
                sgc_ref[16:32, 256 * q:256 * q + 256] = diag[16 * q:16 * q + 16, :]

    return _call(
        body, (proj, proj, dxn, wot, mix, sg, ss, prm, gw, pw, cmat, mask), grid=(nt,), name=name, sem=("arbitrary",),
        rider=rider,
        in_specs=[pl.BlockSpec((TB, NP), lambda i: (rev(i), 0)),
                  pl.BlockSpec((16, HALO_W), lambda i: (jnp.maximum(rev(i) * (TB // 16) - 1, 0), 0)),
                  pl.BlockSpec((TB, D), lambda i: (rev(i), 0)), pl.BlockSpec((D, D), lambda i: (0, 0)),
                  pl.BlockSpec((TB, D), lambda i: (rev(i), 0)),
                  pl.BlockSpec((NCH, 256, 128), lambda i: (rev(i), 0, 0)),
                  pl.BlockSpec((NCH, 128, 256), lambda i: (rev(i), 0, 0)),
                  pl.BlockSpec((16, 768), lambda i: (0, 0)), pl.BlockSpec((128, 128), lambda i: (0, 0)),
                  pl.BlockSpec((256, 256), lambda i: (0, 0)), pl.BlockSpec((4, 256, 256), lambda i: (0, 0, 0)),
                  pl.BlockSpec((256, 128), lambda i: (0, 0))],
        out_specs=[pl.BlockSpec((TB, NP), lambda i: (rev(i), 0)), pl.BlockSpec((32, 1024), lambda i: (0, 0)),
                   pl.BlockSpec((D, D), lambda i: (0, 0))],
        out_shape=[jax.ShapeDtypeStruct((t, NP), BF16), jax.ShapeDtypeStruct((32, 1024), F32),
                   jax.ShapeDtypeStruct((D, D), F32)],
        scratch_shapes=[pltpu.VMEM((256, 128), F32), pltpu.VMEM((128, 256), F32), pltpu.VMEM((8, 256), F32),
                        pltpu.VMEM((16, 256), F32), pltpu.VMEM((8, 768), F32), pltpu.VMEM((16, 768), F32),
                        pltpu.VMEM((128, 128), F32), pltpu.VMEM((256, 256), F32), pltpu.VMEM((TB, D), F32)])


SHARD = NPROJ // 4
SHARD_PAD = 896


def _ranges_to_perm(o, n):
    out, p = [], 0
    for start, size in _PERM:
        a, b = max(o, start), min(o + n, start + size)
        if a < b:
            out.append((a, b - a, p + a - start))
        p += size
    return out


def _ranges_to_orig(p0, n):
    out, p = [], 0
    for start, size in _PERM:
        a, b = max(p0, p), min(p0 + n, p + size)
        if a < b:
            out.append((a, b - a, start + a - p))
        p += size
    return out


def _lane_window(load, lo, n, d, lane):
    a = 128 * (lo // 128)
    off = lo - a
    w = 128 if off + n <= 128 else 256
    chunk = load(a, w)
    shift = (d - off) % w
    if shift:
        chunk = pltpu.roll(chunk, shift, axis=1)
    return jnp.where((lane >= d) & (lane < d + n), chunk[:, 0:128], 0.0)


def _assemble_w_in(slabs, name, rb=256):
    def body(s_ref, wp_ref, wpt_ref):
        lane = _iota((1, 128), 1)
        for b in range(NP // 128):
            acc = jnp.zeros((rb, 128), F32)
            for p, n, o in _ranges_to_orig(128 * b, 128):
                while n > 0:
                    s, lo = o // SHARD, o % SHARD
                    cnt = min(n, SHARD - lo)
                    acc = acc + _lane_window(lambda a, w, s=s: s_ref[s, :, a:a + w].astype(F32), lo, cnt, p - 128 * b, lane)
                    o, p, n = o + cnt, p + cnt, n - cnt
            wp_ref[:, 128 * b:128 * b + 128] = acc.astype(BF16)
            wpt_ref[128 * b:128 * b + 128, :] = acc.T.astype(BF16)

    return pl.pallas_call(
        body, grid=(D // rb,), name=name,
        in_specs=[pl.BlockSpec((4, rb, SHARD_PAD), lambda i: (0, i, 0))],
        out_specs=[pl.BlockSpec((rb, NP), lambda i: (i, 0)), pl.BlockSpec((NP, rb), lambda i: (0, i))],
        out_shape=[jax.ShapeDtypeStruct((D, NP), BF16), jax.ShapeDtypeStruct((NP, D), BF16)],
        compiler_params=_cparams(("parallel",)))(slabs)


def _split_dw_in(dwp, name, rb=256):
    def body(g_ref, o_ref):
        lane = _iota((1, 128), 1)
        for s in range(4):
            for k in range(SHARD_PAD // 128):
                acc = jnp.zeros((rb, 128), F32)
                n_valid = min(128, SHARD - 128 * k)
                for o, n, p in _ranges_to_perm(SHARD * s + 128 * k, n_valid):
                    acc = acc + _lane_window(lambda a, w: g_ref[:, a:a + w], p, n, o - SHARD * s - 128 * k, lane)
                o_ref[s, :, 128 * k:128 * k + 128] = acc

    return pl.pallas_call(
        body, grid=(D // rb,), name=name,
        in_specs=[pl.BlockSpec((rb, NP), lambda i: (i, 0))],
        out_specs=pl.BlockSpec((4, rb, SHARD_PAD), lambda i: (0, i, 0)),
        out_shape=jax.ShapeDtypeStruct((4, D, SHARD_PAD), F32),
        compiler_params=_cparams(("parallel",)))(dwp)


def _half(c, n):
    return pl.ds(pl.multiple_of(c * (n // 2), n // 2), n // 2)


def _other_chips(x, y):
    return ((1 - x, y), (x, 1 - y), (1 - x, 1 - y))


def _remote(src, dst, send, recv, k, dev):
    return pltpu.make_async_remote_copy(src_ref=src, dst_ref=dst, send_sem=send.at[k], recv_sem=recv.at[k], device_id=dev,
                                        device_id_type=MESH)


def _sem(n):
    return pltpu.SemaphoreType.DMA((n,))


def _rider_gather_ici(shards, extra=None):
    shards = tuple(shards) + ((extra,) if extra is not None else ())
    n = len(shards)

    def copies(rins, routs, sems, arrivals=True):
        send, recv = sems
        x, y, c = _place()
        me = 2 * x + y
        out, inc = [], []
        for j, (px, py) in enumerate(_other_chips(x, y)):
            for k in range(n):
                whole = extra is not None and k == n - 1
                rows = pl.ds(0, shards[k].shape[0]) if whole else _half(c, shards[k].shape[0])
                out.append(_remote(rins[k].at[rows], routs[k].at[me, rows], send, recv, n * j + k, (px, py, c)))
                if arrivals:
                    inc.append(_remote(rins[k].at[rows], routs[k].at[2 * px + py, rows], send, recv, n * j + k, (px, py, c)))
        return out, inc

    def start(rins, routs, sems):
        for cp in copies(rins, routs, sems, arrivals=False)[0]:
            cp.start()

    def finish(rins, routs, sems):
        out, inc = copies(rins, routs, sems)
        for cp in inc:
            cp.wait_recv()
        for cp in out:
            cp.wait_send()

    return _Rider(shards, [jax.ShapeDtypeStruct((4,) + a.shape, a.dtype) for a in shards], [_sem(3 * n), _sem(3 * n)],
                  start, finish)


def _rider_gather_d2d(slabs):
    slabs = tuple(slabs)
    n = len(slabs)

    def copies(routs, sems, arrivals=True):
        send, recv = sems
        x, y, c = _place()
        out, inc = [], []
        for j, (px, py) in enumerate(_other_chips(x, y)):
            for k in range(n):
                rows = slabs[k].shape[1]
                mine, theirs = routs[k].at[2 * px + py, _half(c, rows)], routs[k].at[2 * px + py, _half(1 - c, rows)]
                out.append(_remote(mine, mine, send, recv, n * j + k, (x, y, 1 - c)))
                if arrivals:
                    inc.append(_remote(theirs, theirs, send, recv, n * j + k, (x, y, 1 - c)))
        return out, inc

    def start(rins, routs, sems):
        for cp in copies(routs, sems, arrivals=False)[0]:
            cp.start()

    def finish(rins, routs, sems):
        out, inc = copies(routs, sems)
        for cp in inc:
            cp.wait_recv()
        for cp in out:
            cp.wait_send()

    return _Rider(slabs, [jax.ShapeDtypeStruct(a.shape, a.dtype) for a in slabs], [_sem(3 * n), _sem(3 * n)], start, finish,
                  aliases={k: k for k in range(n)})


def _rider_swap(parts):
    parts = tuple(parts)
    n = len(parts)

    def copies(rins, routs, sems):
        send, recv = sems
        x, y, c = _place()
        return [_remote(rins[k].at[:, _half(1 - c, parts[k].shape[1])], routs[k], send, recv, k, (x, y, 1 - c))
                for k in range(n)]

    def start(rins, routs, sems):
        for cp in copies(rins, routs, sems):
            cp.start()

    def finish(rins, routs, sems):
        for cp in copies(rins, routs, sems):
            cp.wait()

    return _Rider(parts, [jax.ShapeDtypeStruct((4, a.shape[1] // 2, a.shape[2]), a.dtype) for a in parts],
                  [_sem(n), _sem(n)], start, finish)


def _rider_scatter(parts):
    parts = tuple(parts)
    n = len(parts)

    def copies(rins, routs, sems, arrivals=True):
        send, recv = sems
        x, y, c = _place()
        me = 2 * x + y
        out, inc = [], []
        for j, (px, py) in enumerate(_other_chips(x, y)):
            for k in range(n):
                out.append(_remote(rins[k].at[2 * px + py], routs[k].at[me], send, recv, n * j + k, (px, py, c)))
                if arrivals:
                    inc.append(_remote(rins[k].at[me], routs[k].at[2 * px + py], send, recv, n * j + k, (px, py, c)))
        return out, inc

    def start(rins, routs, sems):
        for cp in copies(rins, routs, sems, arrivals=False)[0]:
            cp.start()

    def finish(rins, routs, sems):
        out, inc = copies(rins, routs, sems)
        for cp in inc:
            cp.wait_recv()
        for cp in out:
            cp.wait_send()

    return _Rider(parts, [jax.ShapeDtypeStruct(a.shape, a.dtype) for a in parts], [_sem(3 * n), _sem(3 * n)], start, finish)


def _rider_share(fulls):
    fulls = tuple(fulls)
    n = len(fulls)

    def copies(routs, sems, arrivals=True):
        send, recv = sems
        x, y, c = _place()
        out, inc = [], []
        for k in range(n):
            mine, theirs = routs[k].at[_half(c, fulls[k].shape[0])], routs[k].at[_half(1 - c, fulls[k].shape[0])]
            out.append(_remote(mine, mine, send, recv, k, (x, y, 1 - c)))
            if arrivals:
                inc.append(_remote(theirs, theirs, send, recv, k, (x, y, 1 - c)))
        return out, inc

    def start(rins, routs, sems):
        for cp in copies(routs, sems, arrivals=False)[0]:
            cp.start()

    def finish(rins, routs, sems):
        out, inc = copies(routs, sems)
        for cp in inc:
            cp.wait_recv()
        for cp in out:
            cp.wait_send()

    return _Rider(fulls, [jax.ShapeDtypeStruct(a.shape, a.dtype) for a in fulls], [_sem(n), _sem(n)], start, finish,
                  aliases={k: k for k in range(n)})


def _pair_sum(core, full, recv, name, br=128):
    n, rows, cols = recv.shape

    def body(c_ref, a_ref, b_ref, o_ref):
        o_ref[...] = (a_ref[...] + b_ref[...]).astype(BF16)

    nb = rows // br
    return pl.pallas_call(
        body, name=name, out_shape=jax.ShapeDtypeStruct(recv.shape, BF16),
        grid_spec=pltpu.PrefetchScalarGridSpec(
            num_scalar_prefetch=1, grid=(n, nb),
            in_specs=[pl.BlockSpec((1, br, cols), lambda i, j, c: (i, c[0] * nb + j, 0)),
                      pl.BlockSpec((1, br, cols), lambda i, j, c: (i, j, 0))],
            out_specs=pl.BlockSpec((1, br, cols), lambda i, j, c: (i, j, 0))),
        compiler_params=_cparams(("parallel", "parallel")))(core, full, recv)


def _chip_sum(place, gathered, mine, name, br=128):
    _, r, c = gathered.shape
    nb = r // br

    def body(p_ref, g_ref, m_ref, o_ref):
        slab = lambda j: jnp.where(p_ref[1] == j, m_ref[j], g_ref[j]).astype(F32)
        o_ref[...] = ((slab(0) + slab(1)) + slab(2)) + slab(3)

    return pl.pallas_call(
        body, name=name, out_shape=jax.ShapeDtypeStruct((2 * r, c), F32),
        grid_spec=pltpu.PrefetchScalarGridSpec(
            num_scalar_prefetch=1, grid=(nb,),
            in_specs=[pl.BlockSpec((4, br, c), lambda i, p: (0, i, 0)), pl.BlockSpec((4, br, c), lambda i, p: (0, i, 0))],
            out_specs=pl.BlockSpec((br, c), lambda i, p: (p[0] * nb + i, 0))),
        compiler_params=_cparams(("parallel",)))(place, gathered, mine)


def _adamw(w, g, m, v, name, br):
    n, r, c = w.shape

    def body(w_ref, g_ref, m_ref, v_ref, d_ref, m2_ref, v2_ref):
        d_ref[...], m2_ref[...], v2_ref[...] = _adam_math(w_ref[...], g_ref[...], m_ref[...], v_ref[...])

    spec = pl.BlockSpec((1, br, c), lambda i, j: (i, j, 0))
    shp = jax.ShapeDtypeStruct(w.shape, F32)
    return pl.pallas_call(body, grid=(n, r // br), name=name, in_specs=[spec] * 4, out_specs=[spec] * 3,
                          out_shape=[shp] * 3, compiler_params=_cparams(("parallel", "parallel")))(w, g, m, v)


def _adamw_w_in(w, g, m, v, name, bc=31):
    cols = w.shape[2]
    lead = lambda a: jnp.transpose(a, (2, 0, 1))
    g = jnp.stack([a[:, 0:cols] for a in g])

    def body(w_ref, g_ref, m_ref, v_ref, go_ref, d_ref, m2_ref, v2_ref):
        for l in range(2):
            gv = g_ref[:, l, :]
            d_ref[:, l, :], m2_ref[:, l, :], v2_ref[:, l, :] = _adam_math(w_ref[:, l, :], gv, m_ref[:, l, :], v_ref[:, l, :])
            go_ref[:, l, :] = gv

    spec = pl.BlockSpec((bc, 2, D), lambda i: (i, 0, 0))
    outs = pl.pallas_call(body, grid=(cols // bc,), name=name, in_specs=[spec] * 4, out_specs=[spec] * 4,
                          out_shape=[jax.ShapeDtypeStruct((cols, 2, D), F32)] * 4,
                          compiler_params=_cparams(("parallel",)))(lead(w), lead(g), lead(m), lead(v))
    return [jnp.transpose(o, (1, 2, 0)) for o in outs]


_SMALL_NAMES = ("norm_w", "conv_a_w", "gla_gate_w", "gla_gate_b", "gla_norm_w", "pool_w", "pool_scale", "ssd_conv_w",
                "ssd_conv_b", "ssd_dt_bias", "ssd_a_log", "ssd_d", "ssd_norm_w", "final_norm_w")
SMALL_ROWS = 72


def _adam_math(w, g, m, v):
    m2 = ADAM_B1 * m + (1.0 - ADAM_B1) * g
    v2 = ADAM_B2 * v + (1.0 - ADAM_B2) * (g * g)
    m_hat = m2 / (1.0 - ADAM_B1 ** ADAM_STEP)
    v_hat = v2 / (1.0 - ADAM_B2 ** ADAM_STEP)
    return -ADAM_LR * (m_hat / (jnp.sqrt(v_hat) + ADAM_EPS) + ADAM_WD * w), m2, v2


def _small_slices(name, chip):
    if name == "conv_a_w":
        return [((), slice(R_CAW, R_CAW + 3), slice(64 * chip, 64 * chip + 64))]
    if name == "ssd_conv_w":
        return [((), slice(R_SCW, R_SCW + 4), slice(192 * chip, 192 * chip + 192))]
    if name == "gla_gate_w":
        return [((), slice(0, 16), slice(768, 896))]
    if name == "pool_w":
        return [((g, slice(16 * q, 16 * q + 16)), slice(16, 32), slice(256 * q + 64 * g, 256 * q + 64 * g + 64))
                for g in range(4) for q in range(4)]
    row, lanes = {"gla_gate_b": (R_GB, slice(0, 128)), "gla_norm_w": (R_GNW, slice(0, 64)),
                  "pool_scale": (R_PSC, slice(0, 256)), "ssd_conv_b": (R_SCB, slice(0, 768)),
                  "ssd_dt_bias": (R_DTB, slice(16, 20)), "ssd_a_log": (R_AE, slice(0, 4)), "ssd_d": (R_DE, slice(0, 4)),
                  "ssd_norm_w": (R_SNW, slice(0, 256))}[name]
    return [((), slice(row, row + 1), lanes)]


def _small_allreduce(sg0, sg1, dnw0, dnw1, head):
    def body(sg0_ref, sg1_ref, dnw0_ref, dnw1_ref, head_ref, acc, stage, pair, rbuf, send_sems, recv_sems):
        x, y, c = _place()
        chip = 2 * x + y
        stage[0:32, :] = sg0_ref[...]
        stage[32:64, :] = sg1_ref[...]
        stage[64:65, :] = dnw0_ref[0:1, :]
        stage[65:66, :] = dnw1_ref[0:1, :]
        stage[66:68, :] = head_ref[0:2, :]
        stage[68:72, :] = jnp.zeros((4, D), F32)
        sib = _remote(stage, pair, send_sems, recv_sems, 0, (x, y, 1 - c))
        sib.start()
        sib.wait()
        rbuf[0] = stage[...] + pair[...]
        sends = [_remote(rbuf.at[0], rbuf.at[k], send_sems, recv_sems, k, (px, py, c))
                 for k, (px, py) in enumerate(_other_chips(x, y), start=1)]
        for cp in sends:
            cp.start()
        for cp in sends:
            cp.wait()
        slab = lambda d: jnp.where(d == 0, 0, jnp.where(d == 2, 1, jnp.where(d == 1, 2, 3)))
        total = rbuf[slab(jnp.bitwise_xor(chip, 0))]
        for s in range(1, 4):
            total = total + rbuf[slab(jnp.bitwise_xor(chip, s))]
        acc[...] = total

    vmem = pl.BlockSpec(memory_space=pltpu.VMEM)
    return pl.pallas_call(
        body, name="small_allreduce", in_specs=[vmem] * 5, out_specs=vmem,
        out_shape=jax.ShapeDtypeStruct((SMALL_ROWS, D), F32),
        scratch_shapes=[pltpu.VMEM((SMALL_ROWS, D), F32), pltpu.VMEM((SMALL_ROWS, D), F32),
                        pltpu.VMEM((4, SMALL_ROWS, D), F32), _sem(4), _sem(4)],
    )(sg0, sg1, dnw0, dnw1, head)


def _small_adamw(acc, w, m, v):
    n = len(_SMALL_NAMES)

    def body(*refs):
        acc = refs[0]
        w_refs, m_refs, v_refs = refs[1:1 + n], refs[1 + n:1 + 2 * n], refs[1 + 2 * n:1 + 3 * n]
        o = 1 + 3 * n
        g_out, d_out, m_out, v_out = refs[o:o + n], refs[o + n:o + 2 * n], refs[o + 2 * n:o + 3 * n], refs[o + 3 * n:o + 4 * n]
        loss_ref = refs[o + 4 * n]
        chip = 2 * lax.axis_index("x") + lax.axis_index("y")
        loss_ref[...] = acc[67:68, 0:1]

        def update(i, idx, g):
            d, m2, v2 = _adam_math(w_refs[i][idx], g, m_refs[i][idx], v_refs[i][idx])
            g_out[i][idx], d_out[i][idx], m_out[i][idx], v_out[i][idx] = g, d, m2, v2

        for i, name in enumerate(_SMALL_NAMES):
            if name == "final_norm_w":
                update(i, (slice(0, 1), slice(None)), acc[66:67, :])
            elif name == "norm_w":
                for l in range(2):
                    update(i, (slice(l, l + 1), slice(None)), acc[64 + l:65 + l, :])
            elif name in ("conv_a_w", "ssd_conv_w"):
                for s in range(4):
                    @pl.when(chip == s)
                    def _(i=i, name=name, s=s):
                        for l in range(2):
                            (_, rows, lanes), = _small_slices(name, s)
                            update(i, (l,), acc[rows.start + 32 * l:rows.stop + 32 * l, lanes])
            else:
                for l in range(2):
                    for idx, rows, lanes in _small_slices(name, 0):
                        g = acc[rows.start + 32 * l:rows.stop + 32 * l, lanes]
                        if w_refs[i].ndim == 2:
                            update(i, (slice(l, l + 1), slice(None)), g)
                        else:
                            update(i, (l,) + idx, g)

    args = [acc] + [d[k] for d in (w, m, v) for k in _SMALL_NAMES]
    shapes = [jax.ShapeDtypeStruct(w[k].shape, F32) for k in _SMALL_NAMES]
    vmem = pl.BlockSpec(memory_space=pltpu.VMEM)
    outs = pl.pallas_call(body, name="small_adamw", in_specs=[vmem] * len(args), out_specs=[vmem] * (4 * n + 1),
                          out_shape=shapes * 4 + [jax.ShapeDtypeStruct((1, 1), F32)])(*args)
    return outs[0:n], outs[n:2 * n], outs[2 * n:3 * n], outs[3 * n:4 * n], outs[4 * n]


def _mixer_consts(layer, conv_a_w, gla_gate_w, gla_gate_b, gla_norm_w, pool_w, pool_scale, ssd_conv_w, ssd_conv_b,
                  ssd_dt_bias, ssd_a_log, ssd_d, ssd_norm_w):
    def row(v):
        return jnp.pad(v.reshape(1, -1), ((0, 0), (0, 768 - v.size)))

    dtb = jnp.zeros((128,), F32).at[16:20].set(ssd_dt_bias[layer])
    rows = [jnp.pad(conv_a_w[layer], ((0, 0), (0, 512))), row(gla_gate_b[layer]), row(jnp.tile(gla_norm_w[layer], 4)),
            row(pool_scale[layer]), row(ssd_conv_b[layer]), row(dtb), row(jnp.repeat(-jnp.exp(ssd_a_log[layer]), 64)),
            row(jnp.repeat(ssd_d[layer], 64)), row(ssd_norm_w[layer]), jnp.zeros((1, 768), F32), ssd_conv_w[layer]]
    prm = jnp.concatenate(rows, axis=0)
    gw = jnp.zeros((128, 128), F32).at[0:16].set(gla_gate_w[layer]).astype(BF16)
    pw = jnp.zeros((256, 256), F32)
    for g in range(4):
        pw = pw.at[64 * g:64 * g + 64, 64 * g:64 * g + 64].set(pool_w[layer, g])
    return (prm, gw, pw.astype(BF16)) + _mixer_matrices()


def _grad_slabs(layer, dwp, dwo):
    return _split_dw_in(dwp, name=f"split_dw_in{layer}"), dwo.reshape(4, D // 4, D)


class _Comm:
    def __init__(self, w_in, w_out):
        self.w_in16 = jnp.pad(w_in.astype(BF16), ((0, 0), (0, 0), (0, SHARD_PAD - SHARD)))
        self.w_out16 = w_out.astype(BF16)
        self.core = lax.axis_index("c").astype(jnp.int32).reshape(1)
        self.chip = 2 * lax.axis_index("x") + lax.axis_index("y")
        self.place = jnp.stack([lax.axis_index("c"), self.chip]).astype(jnp.int32)

    def gather_ici(self, layer, extra=None):
        return _rider_gather_ici((self.w_in16[layer], self.w_out16[layer]), extra)

    def pair_sum(self, layer, slabs, received):
        return [_pair_sum(self.core, a, b, name=f"reduce_pair_sum{layer}_{k}") for k, (a, b) in enumerate(zip(slabs, received))]

    def chip_sum(self, layer, gathered, mine):
        return [_chip_sum(self.place, a, b, name=f"reduce_chip_sum{layer}_{k}") for k, (a, b) in enumerate(zip(gathered, mine))]

    def layer_weights(self, layer, s_in, s_out):
        own = lambda slabs, shard: jnp.stack([jnp.where(self.chip == s, shard, slabs[s]) for s in range(4)])
        wp, wpt = _assemble_w_in(own(s_in, self.w_in16[layer]), name=f"assemble_w_in{layer}")
        wo = own(s_out, self.w_out16[layer]).reshape(D, D)
        return wp, wpt, wo, wo.T


def _local_step(x, tgt, norm_w, final_norm_w, consts, wts0, wts1=None, comm=None):
    nw = [norm_w[l:l + 1] for l in range(2)]
    proj0, h0, slabs = _rmsproj(x, nw[0], wts0[0], name="rmsproj0", rider=comm and comm.gather_ici(1))
    (mix0, sg0, ss0, x1), slabs = _mixer_fwd(proj0, x, wts0[2], *consts[0], name="mixer_fwd0",
                                             rider=comm and _rider_gather_d2d(slabs))
    if comm:
        wts1 = comm.layer_weights(1, *slabs)
    proj1, h1, _ = _rmsproj(x1, nw[1], wts1[0], name="rmsproj1")
    (mix1, sg1, ss1, x2), _ = _mixer_fwd(proj1, x1, wts1[2], *consts[1], name="mixer_fwd1")
    dx, head = _head(x2, tgt, final_norm_w.reshape(1, D), name="loss_head")
    (dproj, mgr1, dwo1), _ = _mixer_bwd(proj1, dx, wts1[3], mix1, sg1, ss1, *consts[1], name="mixer_bwd1")
    dwp1, _ = _dwin(h1, dproj, name="dwin1")
    slabs1 = comm and _grad_slabs(1, dwp1, dwo1)
    (dx, dnw1), recv = _dxin(dproj, wts1[1], x1, dx, nw[1], name="dxin1", rider=comm and _rider_swap(slabs1))
    pairs1 = comm and comm.pair_sum(1, slabs1, recv)
    (dproj, mgr0, dwo0), gathered = _mixer_bwd(proj0, dx, wts0[3], mix0, sg0, ss0, *consts[0], name="mixer_bwd0",
                                               rider=comm and _rider_scatter(pairs1))
    dwp0, big1 = _dwin(h0, dproj, name="dwin0", rider=comm and _rider_share(comm.chip_sum(1, gathered, pairs1)))
    scat = None
    if comm:
        slabs0 = _grad_slabs(0, dwp0, dwo0)
        pairs0 = comm.pair_sum(0, slabs0, _run_rider(_rider_swap(slabs0), "reduce_swap0"))
        scat = _rider_scatter(pairs0)
    (dx, dnw0), gathered = _dxin(dproj, wts0[1], x, dx, nw[0], name="dxin0", rider=scat)
    if comm:
        big0 = _run_rider(_rider_share(comm.chip_sum(0, gathered, pairs0)), "reduce_share0")
        big = ((big0[0], big1[0]), (big0[1], big1[1]))
    else:
        big = ((dwp0, dwp1), (dwo0, dwo1))
    return head, dx, big, (dnw0, dnw1), (mgr0, mgr1)


def kernel(x, norm_w, w_in, conv_a_w, gla_gate_w, gla_gate_b, gla_norm_w, pool_w, pool_scale, ssd_conv_w, ssd_conv_b, ssd_dt_bias, ssd_a_log, ssd_d, ssd_norm_w, w_out, final_norm_w, loss_target, m_norm_w, m_w_in, m_conv_a_w, m_gla_gate_w, m_gla_gate_b, m_gla_norm_w, m_pool_w, m_pool_scale, m_ssd_conv_w, m_ssd_conv_b, m_ssd_dt_bias, m_ssd_a_log, m_ssd_d, m_ssd_norm_w, m_w_out, m_final_norm_w, v_norm_w, v_w_in, v_conv_a_w, v_gla_gate_w, v_gla_gate_b, v_gla_norm_w, v_pool_w, v_pool_scale, v_ssd_conv_w, v_ssd_conv_b, v_ssd_dt_bias, v_ssd_a_log, v_ssd_d, v_ssd_norm_w, v_w_out, v_final_norm_w):
    weights = dict(norm_w=norm_w, w_in=w_in, conv_a_w=conv_a_w, gla_gate_w=gla_gate_w, gla_gate_b=gla_gate_b,
                   gla_norm_w=gla_norm_w, pool_w=pool_w, pool_scale=pool_scale, ssd_conv_w=ssd_conv_w,
                   ssd_conv_b=ssd_conv_b, ssd_dt_bias=ssd_dt_bias, ssd_a_log=ssd_a_log, ssd_d=ssd_d,
                   ssd_norm_w=ssd_norm_w, w_out=w_out, final_norm_w=final_norm_w)
    m_in = dict(norm_w=m_norm_w, w_in=m_w_in, conv_a_w=m_conv_a_w, gla_gate_w=m_gla_gate_w, gla_gate_b=m_gla_gate_b,
                gla_norm_w=m_gla_norm_w, pool_w=m_pool_w, pool_scale=m_pool_scale, ssd_conv_w=m_ssd_conv_w,
                ssd_conv_b=m_ssd_conv_b, ssd_dt_bias=m_ssd_dt_bias, ssd_a_log=m_ssd_a_log, ssd_d=m_ssd_d,
                ssd_norm_w=m_ssd_norm_w, w_out=m_w_out, final_norm_w=m_final_norm_w)
    v_in = dict(norm_w=v_norm_w, w_in=v_w_in, conv_a_w=v_conv_a_w, gla_gate_w=v_gla_gate_w, gla_gate_b=v_gla_gate_b,
                gla_norm_w=v_gla_norm_w, pool_w=v_pool_w, pool_scale=v_pool_scale, ssd_conv_w=v_ssd_conv_w,
                ssd_conv_b=v_ssd_conv_b, ssd_dt_bias=v_ssd_dt_bias, ssd_a_log=v_ssd_a_log, ssd_d=v_ssd_d,
                ssd_norm_w=v_ssd_norm_w, w_out=v_w_out, final_norm_w=v_final_norm_w)
    order = ("norm_w", "w_in", "conv_a_w", "gla_gate_w", "gla_gate_b", "gla_norm_w", "pool_w", "pool_scale",
             "ssd_conv_w", "ssd_conv_b", "ssd_dt_bias", "ssd_a_log", "ssd_d", "ssd_norm_w", "w_out", "final_norm_w")
    t = x.shape[1]

    comm = _Comm(w_in, w_out)
    cshard = jnp.zeros((16, 256), F32)
    for l in range(2):
        cshard = cshard.at[8 * l:8 * l + 3, 0:64].set(conv_a_w[l]).at[8 * l + 3:8 * l + 7, 0:192].set(ssd_conv_w[l])
    s_in, s_out, g_c = _run_rider(comm.gather_ici(0, cshard), "gather_ici0")
    s_in, s_out = _run_rider(_rider_gather_d2d((s_in, s_out)), "gather_d2d0")
    g_c = [jnp.where(comm.chip == s, cshard, g_c[s]) for s in range(4)]
    conv_a_full = jnp.stack([jnp.concatenate([g_c[s][8 * l:8 * l + 3, 0:64] for s in range(4)], axis=-1) for l in range(2)])
    ssd_conv_full = jnp.stack([jnp.concatenate([g_c[s][8 * l + 3:8 * l + 7, 0:192] for s in range(4)], axis=-1)
                               for l in range(2)])
    consts = [_mixer_consts(l, conv_a_full, gla_gate_w, gla_gate_b, gla_norm_w, pool_w, pool_scale, ssd_conv_full,
                            ssd_conv_b, ssd_dt_bias, ssd_a_log, ssd_d, ssd_norm_w) for l in range(2)]

    head, dx, big, dnw, mgr = _local_step(x.reshape(t, D), loss_target.reshape(t, D), norm_w, final_norm_w, consts,
                                          comm.layer_weights(0, s_in, s_out), comm=comm)

    as2d = lambda d: {k: (d[k].reshape(1, D) if k == "final_norm_w" else d[k]) for k in _SMALL_NAMES}
    small = _small_adamw(_small_allreduce(mgr[0], mgr[1], dnw[0], dnw[1], head), as2d(weights), as2d(m_in), as2d(v_in))
    grads, delta, new_m, new_v = ({k: (a.reshape(D) if k == "final_norm_w" else a) for k, a in zip(_SMALL_NAMES, part)}
                                  for part in small[0:4])
    loss = small[4].reshape(())

    grads["w_out"] = jnp.stack(big[1])

    grads["w_in"], delta["w_in"], new_m["w_in"], new_v["w_in"] = _adamw_w_in(w_in, big[0], m_w_in, v_w_in, name="adamw_w_in")
    delta["w_out"], new_m["w_out"], new_v["w_out"] = _adamw(w_out, grads["w_out"], m_w_out, v_w_out, name="adamw_w_out", br=256)

    return (loss, dx.reshape(1, t, D), *[grads[k] for k in order], *[delta[k] for k in order],
            *[new_m[k] for k in order], *[new_v[k] for k in order])
```

```python
import functools

import jax
import jax.numpy as jnp
from jax import lax
from jax.experimental import pallas as pl
from jax.experimental.pallas import tpu as pltpu

F32 = jnp.float32
BF16 = jnp.bfloat16
MESH = pl.DeviceIdType.MESH

D = 1024
CH = 64
EPS = 1e-6
NP = 3456
NPROJ = 3348
NPM = 3328
GLA_SCALE = 32.0 ** -0.5
INV_TAU = 1.0 / 16.0
TB = 256
NCH = TB // CH
assert TB == 256

C_AH, C_AB, C_AC, C_AZ, C_GQ, C_GK, C_GV = 0, 256, 512, 768, 1024, 1152, 1280
C_GZ, C_PU, C_PZ, C_SZ, C_SX, C_TL = 1536, 1792, 2048, 2304, 2560, 3328
_PERM = ((0, 1536), (1552, 1792), (1536, 16), (3344, 4))
_UNPERM = ((0, 1536), (3328, 16), (1536, 1792), (3344, 4))

R_CAW, R_GB, R_GNW, R_PSC, R_SCB, R_DTB, R_AE, R_DE, R_SNW, R_SCW = 0, 3, 4, 5, 6, 7, 8, 9, 10, 12

ADAM_LR, ADAM_B1, ADAM_B2, ADAM_EPS, ADAM_WD, ADAM_STEP = 0.001, 0.9, 0.999, 1e-08, 0.01, 10

VMEM_LIMIT = 56 * 1024 * 1024


def _cparams(sem, limit=VMEM_LIMIT):
    return pltpu.CompilerParams(dimension_semantics=sem, vmem_limit_bytes=limit)


_ANY = pl.BlockSpec(memory_space=pl.ANY)


def _place():
    return lax.axis_index("x"), lax.axis_index("y"), lax.axis_index("c")


class _Rider:
    def __init__(self, inputs, out_shapes, sems, start, finish, aliases=None):
        self.inputs, self.out_shapes, self.sems = tuple(inputs), tuple(out_shapes), tuple(sems)
        self.start, self.finish, self.aliases = start, finish, dict(aliases or {})


def _call(body, args, *, grid, in_specs, out_specs, out_shape, name, sem, scratch_shapes=(), rider=None):
    if rider is None:
        outs = pl.pallas_call(body, grid=grid, name=name, in_specs=list(in_specs), out_specs=list(out_specs),
                              out_shape=list(out_shape), scratch_shapes=list(scratch_shapes),
                              compiler_params=_cparams(sem))(*args)
        return list(outs), []
    ni, no, ns = len(args), len(out_shape), len(scratch_shapes)
    ri, ro = len(rider.inputs), len(rider.out_shapes)

    def full(*refs):
        ins, rins = refs[:ni], refs[ni:ni + ri]
        outs, routs = refs[ni + ri:ni + ri + no], refs[ni + ri + no:ni + ri + no + ro]
        scr, rsem = refs[ni + ri + no + ro:ni + ri + no + ro + ns], refs[ni + ri + no + ro + ns:]
        first = functools.reduce(jnp.logical_and, [pl.program_id(a) == 0 for a in range(len(grid))])
        last = functools.reduce(jnp.logical_and, [pl.program_id(a) == grid[a] - 1 for a in range(len(grid))])

        @pl.when(first)
        def _():
            rider.start(rins, routs, rsem)

        body(*ins, *outs, *scr)

        @pl.when(last)
        def _():
            rider.finish(rins, routs, rsem)

    outs = pl.pallas_call(
        full, grid=grid, name=name, in_specs=list(in_specs) + [_ANY] * ri, out_specs=list(out_specs) + [_ANY] * ro,
        out_shape=list(out_shape) + list(rider.out_shapes), scratch_shapes=list(scratch_shapes) + list(rider.sems),
        input_output_aliases={ni + k: no + v for k, v in rider.aliases.items()},
        compiler_params=_cparams(("arbitrary",) * len(grid)))(*args, *rider.inputs)
    return list(outs[:no]), list(outs[no:])


def _run_rider(rider, name):
    ri = len(rider.inputs)

    def body(*refs):
        rins, routs, rsem = refs[:ri], refs[ri:ri + len(rider.out_shapes)], refs[ri + len(rider.out_shapes):]
        rider.start(rins, routs, rsem)
        rider.finish(rins, routs, rsem)

    return list(pl.pallas_call(body, name=name, in_specs=[_ANY] * ri, out_specs=[_ANY] * len(rider.out_shapes),
                               out_shape=list(rider.out_shapes), scratch_shapes=list(rider.sems),
                               input_output_aliases=dict(rider.aliases))(*rider.inputs))


def _dot(a, b):
    return jnp.dot(a.astype(BF16), b.astype(BF16), preferred_element_type=F32)


def _dot_nt(a, b):
    return lax.dot_general(a.astype(BF16), b.astype(BF16), (((1,), (1,)), ((), ())), preferred_element_type=F32)


def _dot_tn(a, b):
    return lax.dot_general(a.astype(BF16), b.astype(BF16), (((0,), (0,)), ((), ())), preferred_element_type=F32)


def _split(a):
    hi = a.astype(BF16)
    lo = (a - hi.astype(F32)).astype(BF16)
    return hi, lo


def _dot2_l(a, b):
    hi, lo = _split(a)
    return _dot(hi, b) + _dot(lo, b)


def _dot2_r(a, b):
    hi, lo = _split(b)
    return _dot(a, hi) + _dot(a, lo)


def _dot3_l(a, b):
    hi, lo = _split(a)
    lo2 = ((a - hi.astype(F32)) - lo.astype(F32)).astype(BF16)
    return _dot(hi, b) + _dot(lo, b) + _dot(lo2, b)


def _dot2_nt(a, b):
    hi, lo = _split(a)
    return _dot_nt(hi, b) + _dot_nt(lo, b)


def _silu(z):
    return z * jax.nn.sigmoid(z)


def _lse1(x):
    return jnp.log(1.0 + jnp.exp(-jnp.abs(x)))


def _cs(a):
    return jnp.sum(a, axis=0, keepdims=True)


def _iota(shape, dim):
    return lax.broadcasted_iota(jnp.int32, shape, dim)


def _mixer_matrices():
    r, c = _iota((256, 256), 0), _iota((256, 256), 1)
    same_chunk = (r >> 6) == (c >> 6)
    mats = jnp.stack([jnp.where((c > r) & same_chunk, 1.0, 0.0), jnp.where((c < r) & same_chunk, 1.0, 0.0),
                      jnp.where(same_chunk, 1.0 / 64.0, 0.0), jnp.where((r < 128) & (r - 16 == (c >> 6)), 1.0, 0.0)])
    mask = jnp.where((_iota((256, 128), 0) >> 6) == (_iota((256, 128), 1) >> 5), 1.0, 0.0)
    return mats.astype(BF16), mask.astype(F32)


def _dn(ext, k, n, h):
    return pltpu.roll(ext, k, axis=0)[h:h + n]


def _up(ext, k, n):
    return pltpu.roll(ext, ext.shape[0] - k, axis=0)[:n]


def _pool_lane_select(lane, s2, s4, s8, s16):
    return jnp.where(lane < 64, s2, jnp.where(lane < 128, s4, jnp.where(lane < 192, s8, s16)))


def _winsum_dn(ext, lane):
    s2 = ext + pltpu.roll(ext, 1, axis=0)
    s4 = s2 + pltpu.roll(s2, 2, axis=0)
    s8 = s4 + pltpu.roll(s4, 4, axis=0)
    s16 = s8 + pltpu.roll(s8, 8, axis=0)
    return _pool_lane_select(lane, s2, s4, s8, s16)


def _winsum_up(ext, lane):
    m = ext.shape[0]
    s2 = ext + pltpu.roll(ext, m - 1, axis=0)
    s4 = s2 + pltpu.roll(s2, m - 2, axis=0)
    s8 = s4 + pltpu.roll(s4, m - 4, axis=0)
    s16 = s8 + pltpu.roll(s8, m - 8, axis=0)
    return _pool_lane_select(lane, s2, s4, s8, s16)


def _pool_inv_count(tile, n):
    lane = _iota((1, 256), 1)
    win = _pool_lane_select(lane, 2.0, 4.0, 8.0, 16.0).astype(F32)
    tpos = (tile * n + _iota((n, 1), 0) + 1).astype(F32)
    return jnp.where(tpos >= win, 1.0 / win, 1.0 / tpos)


def _silu_pair(z):
    s = jax.nn.sigmoid(z)
    return z * s, s * (1.0 + z * (1.0 - s))


def _chunks(a):
    return [a[c * CH:(c + 1) * CH] for c in range(a.shape[0] // CH)]


def _halves(fn, a, b):
    return jnp.concatenate([fn(a[:, 0:128], b[:, 0:128]), fn(a[:, 128:256], b[:, 128:256])], axis=1)


def _mixer_tile_prep(p_ref, t_ref, xc, prm_ref, gw_v, cm_ref, mk_ref):
    tail = t_ref[...]
    pre = _dot(tail, gw_v) + prm_ref[R_GB:R_GB + 1, 0:128]
    la = (jnp.minimum(pre, 0.0) - _lse1(pre)) * INV_TAU
    dtin = tail + prm_ref[R_DTB:R_DTB + 1, 0:128]
    dtf = jnp.maximum(dtin, 0.0) + _lse1(dtin)
    dte = _dot2_l(dtf, cm_ref[3, 0:128, :])
    da = dte * prm_ref[R_AE:R_AE + 1, 0:256]
    rev = _dot2_r(cm_ref[0], jnp.concatenate([la, da], axis=1))
    dec = jnp.exp(rev[:, 0:128])
    kd = p_ref[:, C_GK:C_GK + 128].astype(F32) * dec
    wdec = jnp.exp(rev[:, 128:384])
    w = wdec * dte
    xw = xc[:, 0:256] * w
    d_s = [jnp.exp(_cs(a)) for a in _chunks(la)]
    et = [jnp.exp(_cs(a)) for a in _chunks(da)]
    mask_t = mk_ref[...]
    ut_g = [_dot_tn(v, k) * mask_t for v, k in zip(_chunks(p_ref[:, C_GV:C_GV + 256].astype(F32)), _chunks(kd))]
    ut_s = [_halves(_dot_tn, b, x) for b, x in zip(_chunks(xc[:, 256:512]), _chunks(xw))]
    return tail, pre, dtin, dte, dec, kd, wdec, w, xw, d_s, et, ut_g, ut_s


def _rmsproj(x, nw, wp, name, tm=512, rider=None):
    t = x.shape[0]

    def body(x_ref, nw_ref, w_ref, o_ref, t_ref, h_ref):
        xv = x_ref[...]
        rs = lax.rsqrt(jnp.mean(xv * xv, axis=-1, keepdims=True) + EPS)
        h = (xv * rs * nw_ref[...]).astype(BF16)
        h_ref[...] = h
        proj = jnp.dot(h, w_ref[...], preferred_element_type=F32)
        o_ref[...] = proj[:, 0:NPM].astype(BF16)
        t_ref[...] = proj[:, NPM:NP]

    (proj, tail, h), extra = _call(
        body, (x, nw, wp), grid=(t // tm,), name=name, sem=("parallel",), rider=rider,
        in_specs=[pl.BlockSpec((tm, D), lambda i: (i, 0)), pl.BlockSpec((1, D), lambda i: (0, 0)),
                  pl.BlockSpec((D, NP), lambda i: (0, 0))],
        out_specs=[pl.BlockSpec((tm, NPM), lambda i: (i, 0)), pl.BlockSpec((tm, NP - NPM), lambda i: (i, 0)),
                   pl.BlockSpec((tm, D), lambda i: (i, 0))],
        out_shape=[jax.ShapeDtypeStruct((t, NPM), BF16), jax.ShapeDtypeStruct((t, NP - NPM), F32),
                   jax.ShapeDtypeStruct((t, D), BF16)])
    return (proj, tail), h, extra


def _head(x, tgt, fw, name, tm=512):
    t = x.shape[0]

    def body(x_ref, t_ref, w_ref, dx_ref, acc_ref):
        @pl.when(pl.program_id(0) == 0)
        def _():
            acc_ref[...] = jnp.zeros_like(acc_ref)

        xv = x_ref[...]
        w = w_ref[...]
        rs = lax.rsqrt(jnp.mean(xv * xv, axis=-1, keepdims=True) + EPS)
        xh = xv * rs
        err = xh * w - t_ref[...]
        dy = err * (1.0 / D)
        dxh = dy * w
        dx_ref[...] = rs * (dxh - xh * jnp.mean(dxh * xh, axis=-1, keepdims=True))
        acc_ref[0:1, :] += _cs(dy * xh)
        acc_ref[1:2, :] += jnp.zeros((1, D), F32) + (0.5 / D) * jnp.sum(err * err)

    return pl.pallas_call(
        body, grid=(t // tm,), name=name,
        in_specs=[pl.BlockSpec((tm, D), lambda i: (i, 0)), pl.BlockSpec((tm, D), lambda i: (i, 0)),
                  pl.BlockSpec((1, D), lambda i: (0, 0))],
        out_specs=[pl.BlockSpec((tm, D), lambda i: (i, 0)), pl.BlockSpec((8, D), lambda i: (0, 0))],
        out_shape=[jax.ShapeDtypeStruct((t, D), F32), jax.ShapeDtypeStruct((8, D), F32)],
        compiler_params=_cparams(("arbitrary",)),
    )(x, tgt, fw)


def _dxin(dp, wpt, x, dxn, nw, name, tm=512, rider=None):
    t = x.shape[0]

    def body(dp_ref, w_ref, x_ref, dxn_ref, nw_ref, dx_ref, dnw_ref):
        @pl.when(pl.program_id(0) == 0)
        def _():
            dnw_ref[...] = jnp.zeros_like(dnw_ref)

        acc = jnp.zeros((1, D), F32)
        for rows in (pl.ds(0, tm // 2), pl.ds(tm // 2, tm // 2)):
            dh = jnp.dot(dp_ref[rows, :].astype(BF16), w_ref[...], preferred_element_type=F32)
            xv = x_ref[rows, :]
            rs = lax.rsqrt(jnp.mean(xv * xv, axis=-1, keepdims=True) + EPS)
            xh = xv * rs
            acc = acc + _cs(dh * xh)
            dxh = dh * nw_ref[...]
            dx_ref[rows, :] = dxn_ref[rows, :] + rs * (dxh - xh * jnp.mean(dxh * xh, axis=-1, keepdims=True))
        dnw_ref[0:1, :] += acc

    return _call(
        body, (dp, wpt, x, dxn, nw), grid=(t // tm,), name=name, sem=("arbitrary",), rider=rider,
        in_specs=[pl.BlockSpec((tm, NP), lambda i: (i, 0)), pl.BlockSpec((NP, D), lambda i: (0, 0)),
                  pl.BlockSpec((tm, D), lambda i: (i, 0)), pl.BlockSpec((tm, D), lambda i: (i, 0)),
                  pl.BlockSpec((1, D), lambda i: (0, 0))],
        out_specs=[pl.BlockSpec((tm, D), lambda i: (i, 0)), pl.BlockSpec((8, D), lambda i: (0, 0))],
        out_shape=[jax.ShapeDtypeStruct((t, D), F32), jax.ShapeDtypeStruct((8, D), F32)])


def _dwin(h, dp, name, tm=1024, tn=NP, rider=None):
    t = h.shape[0]

    def body(h_ref, dp_ref, o_ref):
        @pl.when(pl.program_id(1) == 0)
        def _():
            o_ref[...] = jnp.zeros_like(o_ref)

        o_ref[...] += _dot_tn(h_ref[...], dp_ref[...])

    (dwp,), extra = _call(
        body, (h, dp), grid=(NP // tn, t // tm), name=name, sem=("parallel", "arbitrary"), rider=rider,
        in_specs=[pl.BlockSpec((tm, D), lambda j, i: (i, 0)), pl.BlockSpec((tm, tn), lambda j, i: (i, j))],
        out_specs=[pl.BlockSpec((D, tn), lambda j, i: (0, j))], out_shape=[jax.ShapeDtypeStruct((D, NP), F32)])
    return dwp, extra


def _mixer_fwd(proj, x, wo, prm, gw, pw, cmat, mask, name, rider=None):
    proj, tail = proj
    t = proj.shape[0]
    nt, nc = t // TB, t // CH

    def body(p_ref, t_ref, x_ref, wo_ref, prm_ref, gw_ref, pw_ref, cm_ref, mk_ref, mix_ref, sg_ref, ss_ref, xn_ref,
             sg_s, ss_s, h_ua, h_pu, h_sx):
        i = pl.program_id(0)

        @pl.when(i == 0)
        def _():
            for r in (sg_s, ss_s, h_ua, h_pu, h_sx):
                r[...] = jnp.zeros_like(r)

        lane = _iota((1, 256), 1)
        u = p_ref[:, C_AC:C_AC + 256].astype(F32) * p_ref[:, C_AH:C_AH + 256].astype(F32)
        ext = jnp.concatenate([h_ua[...], u], axis=0)
        cv = (prm_ref[R_CAW + 2:R_CAW + 3, 0:256] * u + prm_ref[R_CAW + 1:R_CAW + 2, 0:256] * _dn(ext, 1, TB, 8)
              + prm_ref[R_CAW:R_CAW + 1, 0:256] * _dn(ext, 2, TB, 8))
        mix_ref[:, 0:256] = (p_ref[:, C_AB:C_AB + 256].astype(F32) * cv * _silu(p_ref[:, C_AZ:C_AZ + 256].astype(F32))).astype(BF16)
        h_ua[...] = u[TB - 8:, :]
        pu = p_ref[:, C_PU:C_PU + 256].astype(F32)
        ext = jnp.concatenate([h_pu[...], pu], axis=0)
        pooled = _winsum_dn(ext, lane)[16:] * _pool_inv_count(i, TB) - pu
        mixed = _dot(pooled, pw_ref[...])
        mix_ref[:, 512:768] = (prm_ref[R_PSC:R_PSC + 1, 0:256] * mixed * _silu(p_ref[:, C_PZ:C_PZ + 256].astype(F32))).astype(BF16)
        h_pu[...] = pu[TB - 16:, :]
        sx = p_ref[:, C_SX:C_SX + 768].astype(F32)
        ext = jnp.concatenate([h_sx[...], sx], axis=0)
        xc = _silu(prm_ref[R_SCW + 3:R_SCW + 4, :] * sx + prm_ref[R_SCW + 2:R_SCW + 3, :] * _dn(ext, 1, TB, 8)
                   + prm_ref[R_SCW + 1:R_SCW + 2, :] * _dn(ext, 2, TB, 8) + prm_ref[R_SCW:R_SCW + 1, :] * _dn(ext, 3, TB, 8)
                   + prm_ref[R_SCB:R_SCB + 1, :])
        h_sx[...] = sx[TB - 8:, :]

        _, _, _, _, _, _, _, _, _, d_s, et, ut_g, ut_s = _mixer_tile_prep(p_ref, t_ref, xc, prm_ref, gw_ref[...], cm_ref, mk_ref)
        s_g, s_s = sg_s[...], ss_s[...]
        o, y = [], []
        qs = _chunks(p_ref[:, C_GQ:C_GQ + 128].astype(F32) * GLA_SCALE)
        cm = _chunks(xc[:, 512:768])
        for c in range(NCH):
            sg_ref[c] = s_g
            ss_ref[c] = s_s
            s_g = s_g * d_s[c] + ut_g[c]
            s_s = s_s * et[c] + ut_s[c]
            o.append(_dot_nt(qs[c], s_g))
            y.append(_halves(_dot, cm[c], s_s))
        sg_s[...] = s_g
        ss_s[...] = s_s
        o = jnp.concatenate(o, axis=0)
        on = o * lax.rsqrt(_dot2_l(o * o, cm_ref[2]) + EPS)
        mix_ref[:, 256:512] = (on * prm_ref[R_GNW:R_GNW + 1, 0:256] * _silu(p_ref[:, C_GZ:C_GZ + 256].astype(F32))).astype(BF16)
        y2 = ((jnp.concatenate(y, axis=0) + prm_ref[R_DE:R_DE + 1, 0:256] * xc[:, 0:256])
              * _silu(p_ref[:, C_SZ:C_SZ + 256].astype(F32)))
        mix_ref[:, 768:1024] = (y2 * lax.rsqrt(jnp.mean(y2 * y2, axis=-1, keepdims=True) + EPS)
                                * prm_ref[R_SNW:R_SNW + 1, 0:256]).astype(BF16)
        xn_ref[...] = x_ref[...] + jnp.dot(mix_ref[...], wo_ref[...], preferred_element_type=F32)

    return _call(
        body, (proj, tail, x, wo, prm, gw, pw, cmat, mask), grid=(nt,), name=name, sem=("arbitrary",), rider=rider,
        in_specs=[pl.BlockSpec((TB, NPM), lambda i: (i, 0)), pl.BlockSpec((TB, NP - NPM), lambda i: (i, 0)),
                  pl.BlockSpec((TB, D), lambda i: (i, 0)),
                  pl.BlockSpec((D, D), lambda i: (0, 0)), pl.BlockSpec((16, 768), lambda i: (0, 0)),
                  pl.BlockSpec((128, 128), lambda i: (0, 0)), pl.BlockSpec((256, 256), lambda i: (0, 0)),
                  pl.BlockSpec((4, 256, 256), lambda i: (0, 0, 0)), pl.BlockSpec((256, 128), lambda i: (0, 0))],
        out_specs=[pl.BlockSpec((TB, D), lambda i: (i, 0)), pl.BlockSpec((NCH, 256, 128), lambda i: (i, 0, 0)),
                   pl.BlockSpec((NCH, 128, 256), lambda i: (i, 0, 0)), pl.BlockSpec((TB, D), lambda i: (i, 0))],
        out_shape=[jax.ShapeDtypeStruct((t, D), BF16), jax.ShapeDtypeStruct((nc, 256, 128), F32),
                   jax.ShapeDtypeStruct((nc, 128, 256), F32), jax.ShapeDtypeStruct((t, D), F32)],
        scratch_shapes=[pltpu.VMEM((256, 128), F32), pltpu.VMEM((128, 256), F32), pltpu.VMEM((8, 256), F32),
                        pltpu.VMEM((16, 256), F32), pltpu.VMEM((8, 768), F32)])


def _mixer_bwd(proj, dxn, wot, mix, sg, ss, prm, gw, pw, cmat, mask, name, rider=None):
    proj, tail = proj
    t = proj.shape[0]
    nt = t // TB
    rev = lambda i: nt - 1 - i

    def body(p_ref, hp_ref, t_ref, dxn_ref, wot_ref, mix_ref, sg_ref, ss_ref, prm_ref, gw_ref, pw_ref, cm_ref, mk_ref,
             dp_ref, sgc_ref, dwo_ref,
             gg_s, gs_s, h_dcv, h_dpl, h_dpre, gsm_ref, dgw_ref, dpw_ref, dm_ref):
        i = pl.program_id(0)
        tile = nt - 1 - i

        @pl.when(i == 0)
        def _():
            for r in (gg_s, gs_s, h_dcv, h_dpl, h_dpre, gsm_ref, dgw_ref, dpw_ref, dwo_ref):
                r[...] = jnp.zeros_like(r)

        dxn = dxn_ref[...].astype(BF16)
        dm_ref[...] = jnp.dot(dxn, wot_ref[...], preferred_element_type=F32)
        dwo_ref[...] += _dot_tn(mix_ref[...], dxn)

        lane = _iota((1, 256), 1)
        first = (tile > 0).astype(F32)
        ah, ac = p_ref[:, C_AH:C_AH + 256].astype(F32), p_ref[:, C_AC:C_AC + 256].astype(F32)
        ab, az = p_ref[:, C_AB:C_AB + 256].astype(F32), p_ref[:, C_AZ:C_AZ + 256].astype(F32)
        w0, w1, w2 = (prm_ref[R_CAW + j:R_CAW + j + 1, 0:256] for j in range(3))
        u = ac * ah
        ext = jnp.concatenate([(hp_ref[:, C_AC:C_AC + 256].astype(F32) * hp_ref[:, C_AH:C_AH + 256].astype(F32))[8:16] * first, u], axis=0)
        u1, u2 = _dn(ext, 1, TB, 8), _dn(ext, 2, TB, 8)
        cv = w2 * u + w1 * u1 + w0 * u2
        g = dm_ref[:, 0:256]
        sz, dsz = _silu_pair(az)
        dp_ref[:, C_AB:C_AB + 256] = (g * cv * sz).astype(BF16)
        dp_ref[:, C_AZ:C_AZ + 256] = (g * ab * cv * dsz).astype(BF16)
        dcv = g * ab * sz
        dext = jnp.concatenate([dcv, h_dcv[...]], axis=0)
        du = w2 * dcv + w1 * _up(dext, 1, TB) + w0 * _up(dext, 2, TB)
        dp_ref[:, C_AC:C_AC + 256] = (du * ah).astype(BF16)
        dp_ref[:, C_AH:C_AH + 256] = (du * ac).astype(BF16)
        gsm_ref[R_CAW:R_CAW + 1, 0:256] += _cs(dcv * u2)
        gsm_ref[R_CAW + 1:R_CAW + 2, 0:256] += _cs(dcv * u1)
        gsm_ref[R_CAW + 2:R_CAW + 3, 0:256] += _cs(dcv * u)
        h_dcv[...] = dcv[0:8, :]
        pu, pz = p_ref[:, C_PU:C_PU + 256].astype(F32), p_ref[:, C_PZ:C_PZ + 256].astype(F32)
        psc = prm_ref[R_PSC:R_PSC + 1, 0:256]
        icnt = _pool_inv_count(tile, TB)
        ext = jnp.concatenate([hp_ref[:, C_PU:C_PU + 256].astype(F32) * first, pu], axis=0)
        pooled = _winsum_dn(ext, lane)[16:] * icnt - pu
        pw_v = pw_ref[...]
        mixed = _dot(pooled, pw_v)
        g = dm_ref[:, 512:768]
        sz, dsz = _silu_pair(pz)
        gsm_ref[R_PSC:R_PSC + 1, 0:256] += _cs(g * mixed * sz)
        dp_ref[:, C_PZ:C_PZ + 256] = (g * psc * mixed * dsz).astype(BF16)
        dmixed = g * psc * sz
        dpw_ref[...] += _dot_tn(pooled, dmixed)
        dpooled = _dot_nt(dmixed, pw_v)
        qd = dpooled * icnt
        dext = jnp.concatenate([qd, h_dpl[...]], axis=0)
        dp_ref[:, C_PU:C_PU + 256] = (_winsum_up(dext, lane)[:TB] - dpooled).astype(BF16)
        h_dpl[...] = qd[0:16, :]
        sx = p_ref[:, C_SX:C_SX + 768].astype(F32)
        cw = [prm_ref[R_SCW + j:R_SCW + j + 1, :] for j in range(4)]
        ext = jnp.concatenate([hp_ref[:, C_SX:C_SX + 768].astype(F32)[8:16] * first, sx], axis=0)
        sx1, sx2, sx3 = _dn(ext, 1, TB, 8), _dn(ext, 2, TB, 8), _dn(ext, 3, TB, 8)
        cpre = cw[3] * sx + cw[2] * sx1 + cw[1] * sx2 + cw[0] * sx3 + prm_ref[R_SCB:R_SCB + 1, :]
        xc, dxc = _silu_pair(cpre)
        xs, bm, cm = xc[:, 0:256], xc[:, 256:512], xc[:, 512:768]

        gw_v = gw_ref[...]
        tail, pre, dtin, dte, dec, kd, wdec, w, xw, d_s, et, ut_g, ut_s = _mixer_tile_prep(p_ref, t_ref, xc, prm_ref,
                                                                                          gw_v, cm_ref, mk_ref)
        gmean = cm_ref[2]
        mask_t = mk_ref[...]
        gnw = prm_ref[R_GNW:R_GNW + 1, 0:256]
        a_e = prm_ref[R_AE:R_AE + 1, 0:256]
        d_e = prm_ref[R_DE:R_DE + 1, 0:256]
        snw = prm_ref[R_SNW:R_SNW + 1, 0:256]
        sg_in = [sg_ref[c] for c in range(NCH)]
        ss_in = [ss_ref[c] for c in range(NCH)]
        sg_n = [sg_in[c] * d_s[c] + ut_g[c] for c in range(NCH)]
        ss_n = [ss_in[c] * et[c] + ut_s[c] for c in range(NCH)]
        qs = _chunks(p_ref[:, C_GQ:C_GQ + 128].astype(F32) * GLA_SCALE)
        cm_c, bm_c, xw_c, kd_c = _chunks(cm), _chunks(bm), _chunks(xw), _chunks(kd)
        v_c = _chunks(p_ref[:, C_GV:C_GV + 256].astype(F32))
        o = jnp.concatenate([_dot_nt(qs[c], sg_n[c]) for c in range(NCH)], axis=0)
        y = jnp.concatenate([_halves(_dot, cm_c[c], ss_n[c]) for c in range(NCH)], axis=0) + d_e * xs
        gz = p_ref[:, C_GZ:C_GZ + 256].astype(F32)
        r = lax.rsqrt(_dot2_l(o * o, gmean) + EPS)
        on = o * r
        dyb = dm_ref[:, 256:512]
        sz, dsz = _silu_pair(gz)
        dp_ref[:, C_GZ:C_GZ + 256] = (dyb * on * gnw * dsz).astype(BF16)
        tg = dyb * sz
        gsm_ref[R_GNW:R_GNW + 1, 0:256] += _cs(tg * on)
        don = tg * gnw
        do_c = _chunks(r * (don - on * _dot2_l(don * on, gmean)))
        ssz = p_ref[:, C_SZ:C_SZ + 256].astype(F32)
        sil, dsil = _silu_pair(ssz)
        y2 = y * sil
        r = lax.rsqrt(jnp.mean(y2 * y2, axis=-1, keepdims=True) + EPS)
        yn = y2 * r
        dyd = dm_ref[:, 768:1024]
        gsm_ref[R_SNW:R_SNW + 1, 0:256] += _cs(dyd * yn)
        dn = dyd * snw
        dy2 = r * (dn - yn * jnp.mean(dn * yn, axis=-1, keepdims=True))
        dp_ref[:, C_SZ:C_SZ + 256] = (dy2 * y * dsil).astype(BF16)
        dy = dy2 * sil
        gsm_ref[R_DE:R_DE + 1, 0:256] += _cs(dy * xs)
        dy_c = _chunks(dy)
        dq = jnp.concatenate([_dot(do_c[c], sg_n[c]) for c in range(NCH)], axis=0)
        dp_ref[:, C_GQ:C_GQ + 128] = (dq * GLA_SCALE).astype(BF16)
        dcm = jnp.concatenate([_halves(_dot_nt, dy_c[c], ss_n[c]) for c in range(NCH)], axis=0)
        gg = [_dot_tn(do_c[c], qs[c]) * mask_t for c in range(NCH)]
        gs = [_halves(_dot_tn, cm_c[c], dy_c[c]) for c in range(NCH)]
        car_g, car_s = gg_s[...], gs_s[...]
        for c in reversed(range(NCH)):
            gg[c] = gg[c] + car_g
            gs[c] = gs[c] + car_s
            car_g = gg[c] * d_s[c]
            car_s = gs[c] * et[c]
        gg_s[...] = car_g
        gs_s[...] = car_s
        dkd = jnp.concatenate([_dot(v_c[c], gg[c]) for c in range(NCH)], axis=0)
        dp_ref[:, C_GV:C_GV + 256] = jnp.concatenate([_dot_nt(kd_c[c], gg[c]) for c in range(NCH)], axis=0).astype(BF16)
        dp_ref[:, C_GK:C_GK + 128] = (dkd * dec).astype(BF16)
        dbm = jnp.concatenate([_halves(_dot_nt, xw_c[c], gs[c]) for c in range(NCH)], axis=0)
        dxw = jnp.concatenate([_halves(_dot, bm_c[c], gs[c]) for c in range(NCH)], axis=0)
        dxs = dy * d_e + dxw * w
        dw = dxw * xs
        dsuf = _dot2_r(cm_ref[1], jnp.concatenate([dkd * kd, dw * dte * wdec], axis=1))
        tot_g = jnp.concatenate([jnp.broadcast_to(_cs(gg[c] * sg_in[c]) * d_s[c], (CH, 128)) for c in range(NCH)], axis=0)
        tot_s = jnp.concatenate([jnp.broadcast_to(_cs(gs[c] * ss_in[c]) * et[c], (CH, 256)) for c in range(NCH)], axis=0)
        dpre = (dsuf[:, 0:128] + tot_g) * INV_TAU * jax.nn.sigmoid(-pre)
        dgw_ref[...] += _dot_tn(tail, dpre)
        gsm_ref[R_GB:R_GB + 1, 0:128] += _cs(dpre)
        dda = dsuf[:, 128:384] + tot_s
        gsm_ref[R_AE:R_AE + 1, 0:256] += _cs(dda * dte)
        dtail_s = _dot2_nt(dw * wdec + dda * a_e, cm_ref[3, 0:128, :]) * jax.nn.sigmoid(dtin)
        gsm_ref[R_DTB:R_DTB + 1, 0:128] += _cs(dtail_s)
        dp_ref[:, C_TL:C_TL + 128] = (_dot_nt(dpre, gw_v) + dtail_s).astype(BF16)
        dpre_c = jnp.concatenate([dxs, dbm, dcm], axis=1) * dxc
        dext = jnp.concatenate([dpre_c, h_dpre[...]], axis=0)
        dp_ref[:, C_SX:C_SX + 768] = (cw[3] * dpre_c + cw[2] * _up(dext, 1, TB) + cw[1] * _up(dext, 2, TB)
                                      + cw[0] * _up(dext, 3, TB)).astype(BF16)
        gsm_ref[R_SCW + 3:R_SCW + 4, :] += _cs(dpre_c * sx)
        gsm_ref[R_SCW + 2:R_SCW + 3, :] += _cs(dpre_c * sx1)
        gsm_ref[R_SCW + 1:R_SCW + 2, :] += _cs(dpre_c * sx2)
        gsm_ref[R_SCW:R_SCW + 1, :] += _cs(dpre_c * sx3)
        gsm_ref[R_SCB:R_SCB + 1, :] += _cs(dpre_c)
        h_dpre[...] = dpre_c[0:8, :]

        @pl.when(i == nt - 1)
        def _():
            ri, ci = _iota((256, 256), 0), _iota((256, 256), 1)
            per_head = jnp.where((ri >> 6) == ci, 1.0, 0.0).astype(BF16)
            per_dv = jnp.where((ri & 63) == ci, 1.0, 0.0).astype(BF16)
            row = _iota((8, 256), 0)
            top = gsm_ref[0:8, 0:256]
            sgc_ref[0:8, 0:256] = jnp.where(row == R_GNW, _dot3_l(top, per_dv), top)
            bot = gsm_ref[8:16, 0:256]
            fold = _dot3_l(jnp.where(row == R_AE - 8, bot * a_e, bot), per_head)
            sgc_ref[8:16, 0:256] = jnp.where((row == R_AE - 8) | (row == R_DE - 8), fold, bot)
            sgc_ref[0:16, 256:768] = gsm_ref[:, 256:768]
            sgc_ref[0:16, 768:896] = dgw_ref[0:16, :]
            sgc_ref[0:16, 896:1024] = jnp.zeros((16, 128), F32)
            diag = _pool_lane_select(lane, dpw_ref[0:64, :], dpw_ref[64:128, :], dpw_ref[128:192, :], dpw_ref[192:256, :])
            for q in range(4):
                sgc_ref[16:32, 256 * q:256 * q + 256] = diag[16 * q:16 * q + 16, :]

    return _call(
        body, (proj, proj, tail, dxn, wot, mix, sg, ss, prm, gw, pw, cmat, mask), grid=(nt,), name=name,
        sem=("arbitrary",), rider=rider,
        in_specs=[pl.BlockSpec((TB, NPM), lambda i: (rev(i), 0)),
                  pl.BlockSpec((16, NPM), lambda i: (jnp.maximum(rev(i) * (TB // 16) - 1, 0), 0)),
                  pl.BlockSpec((TB, NP - NPM), lambda i: (rev(i), 0)),
                  pl.BlockSpec((TB, D), lambda i: (rev(i), 0)), pl.BlockSpec((D, D), lambda i: (0, 0)),
                  pl.BlockSpec((TB, D), lambda i: (rev(i), 0)),
                  pl.BlockSpec((NCH, 256, 128), lambda i: (rev(i), 0, 0)),
                  pl.BlockSpec((NCH, 128, 256), lambda i: (rev(i), 0, 0)),
                  pl.BlockSpec((16, 768), lambda i: (0, 0)), pl.BlockSpec((128, 128), lambda i: (0, 0)),
                  pl.BlockSpec((256, 256), lambda i: (0, 0)), pl.BlockSpec((4, 256, 256), lambda i: (0, 0, 0)),
                  pl.BlockSpec((256, 128), lambda i: (0, 0))],
        out_specs=[pl.BlockSpec((TB, NP), lambda i: (rev(i), 0)), pl.BlockSpec((32, 1024), lambda i: (0, 0)),
                   pl.BlockSpec((D, D), lambda i: (0, 0))],
        out_shape=[jax.ShapeDtypeStruct((t, NP), BF16), jax.ShapeDtypeStruct((32, 1024), F32),
                   jax.ShapeDtypeStruct((D, D), F32)],
        scratch_shapes=[pltpu.VMEM((256, 128), F32), pltpu.VMEM((128, 256), F32), pltpu.VMEM((8, 256), F32),
                        pltpu.VMEM((16, 256), F32), pltpu.VMEM((8, 768), F32), pltpu.VMEM((16, 768), F32),
                        pltpu.VMEM((128, 128), F32), pltpu.VMEM((256, 256), F32), pltpu.VMEM((TB, D), F32)])


SHARD = NPROJ // 4
SHARD_PAD = 896


def _ranges_to_perm(o, n):
    out, p = [], 0
    for start, size in _PERM:
        a, b = max(o, start), min(o + n, start + size)
        if a < b:
            out.append((a, b - a, p + a - start))
        p += size
    return out


def _ranges_to_orig(p0, n):
    out, p = [], 0
    for start, size in _PERM:
        a, b = max(p0, p), min(p0 + n, p + size)
        if a < b:
            out.append((a, b - a, start + a - p))
        p += size
    return out


def _lane_window(load, lo, n, d, lane):
    a = 128 * (lo // 128)
    off = lo - a
    w = 128 if off + n <= 128 else 256
    chunk = load(a, w)
    shift = (d - off) % w
    if shift:
        chunk = pltpu.roll(chunk, shift, axis=1)
    return jnp.where((lane >= d) & (lane < d + n), chunk[:, 0:128], 0.0)


def _assemble_w_in(slabs, name, rb=256):
    def body(s_ref, wp_ref, wpt_ref):
        lane = _iota((1, 128), 1)
        for b in range(NP // 128):
            acc = jnp.zeros((rb, 128), F32)
            for p, n, o in _ranges_to_orig(128 * b, 128):
                while n > 0:
                    s, lo = o // SHARD, o % SHARD
                    cnt = min(n, SHARD - lo)
                    acc = acc + _lane_window(lambda a, w, s=s: s_ref[s, :, a:a + w].astype(F32), lo, cnt, p - 128 * b, lane)
                    o, p, n = o + cnt, p + cnt, n - cnt
            wp_ref[:, 128 * b:128 * b + 128] = acc.astype(BF16)
            wpt_ref[128 * b:128 * b + 128, :] = acc.T.astype(BF16)

    return pl.pallas_call(
        body, grid=(D // rb,), name=name,
        in_specs=[pl.BlockSpec((4, rb, SHARD_PAD), lambda i: (0, i, 0))],
        out_specs=[pl.BlockSpec((rb, NP), lambda i: (i, 0)), pl.BlockSpec((NP, rb), lambda i: (0, i))],
        out_shape=[jax.ShapeDtypeStruct((D, NP), BF16), jax.ShapeDtypeStruct((NP, D), BF16)],
        compiler_params=_cparams(("parallel",)))(slabs)


def _split_dw_in(dwp, name, rb=256):
    def body(g_ref, o_ref):
        lane = _iota((1, 128), 1)
        for s in range(4):
            for k in range(SHARD_PAD // 128):
                acc = jnp.zeros((rb, 128), F32)
                n_valid = min(128, SHARD - 128 * k)
                for o, n, p in _ranges_to_perm(SHARD * s + 128 * k, n_valid):
                    acc = acc + _lane_window(lambda a, w: g_ref[:, a:a + w], p, n, o - SHARD * s - 128 * k, lane)
                o_ref[s, :, 128 * k:128 * k + 128] = acc

    return pl.pallas_call(
        body, grid=(D // rb,), name=name,
        in_specs=[pl.BlockSpec((rb, NP), lambda i: (i, 0))],
        out_specs=pl.BlockSpec((4, rb, SHARD_PAD), lambda i: (0, i, 0)),
        out_shape=jax.ShapeDtypeStruct((4, D, SHARD_PAD), F32),
        compiler_params=_cparams(("parallel",)))(dwp)


def _half(c, n):
    return pl.ds(pl.multiple_of(c * (n // 2), n // 2), n // 2)


def _other_chips(x, y):
    return ((1 - x, y), (x, 1 - y), (1 - x, 1 - y))


def _remote(src, dst, send, recv, k, dev):
    return pltpu.make_async_remote_copy(src_ref=src, dst_ref=dst, send_sem=send.at[k], recv_sem=recv.at[k], device_id=dev,
                                        device_id_type=MESH)


def _sem(n):
    return pltpu.SemaphoreType.DMA((n,))


def _rider_gather_ici(shards, extra=None):
    shards = tuple(shards) + ((extra,) if extra is not None else ())
    n = len(shards)

    def copies(rins, routs, sems, arrivals=True):
        send, recv = sems
        x, y, c = _place()
        me = 2 * x + y
        out, inc = [], []
        for j, (px, py) in enumerate(_other_chips(x, y)):
            for k in range(n):
                whole = extra is not None and k == n - 1
                rows = pl.ds(0, shards[k].shape[0]) if whole else _half(c, shards[k].shape[0])
                out.append(_remote(rins[k].at[rows], routs[k].at[me, rows], send, recv, n * j + k, (px, py, c)))
                if arrivals:
                    inc.append(_remote(rins[k].at[rows], routs[k].at[2 * px + py, rows], send, recv, n * j + k, (px, py, c)))
        return out, inc

    def start(rins, routs, sems):
        for cp in copies(rins, routs, sems, arrivals=False)[0]:
            cp.start()

    def finish(rins, routs, sems):
        out, inc = copies(rins, routs, sems)
        for cp in inc:
            cp.wait_recv()
        for cp in out:
            cp.wait_send()

    return _Rider(shards, [jax.ShapeDtypeStruct((4,) + a.shape, a.dtype) for a in shards], [_sem(3 * n), _sem(3 * n)],
                  start, finish)


def _rider_gather_d2d(slabs):
    slabs = tuple(slabs)
    n = len(slabs)

    def copies(routs, sems, arrivals=True):
        send, recv = sems
        x, y, c = _place()
        out, inc = [], []
        for j, (px, py) in enumerate(_other_chips(x, y)):
            for k in range(n):
                rows = slabs[k].shape[1]
                mine, theirs = routs[k].at[2 * px + py, _half(c, rows)], routs[k].at[2 * px + py, _half(1 - c, rows)]
                out.append(_remote(mine, mine, send, recv, n * j + k, (x, y, 1 - c)))
                if arrivals:
                    inc.append(_remote(theirs, theirs, send, recv, n * j + k, (x, y, 1 - c)))
        return out, inc

    def start(rins, routs, sems):
        for cp in copies(routs, sems, arrivals=False)[0]:
            cp.start()

    def finish(rins, routs, sems):
        out, inc = copies(routs, sems)
        for cp in inc:
            cp.wait_recv()
        for cp in out:
            cp.wait_send()

    return _Rider(slabs, [jax.ShapeDtypeStruct(a.shape, a.dtype) for a in slabs], [_sem(3 * n), _sem(3 * n)], start, finish,
                  aliases={k: k for k in range(n)})


def _rider_swap(parts):
    parts = tuple(parts)
    n = len(parts)

    def copies(rins, routs, sems):
        send, recv = sems
        x, y, c = _place()
        return [_remote(rins[k].at[:, _half(1 - c, parts[k].shape[1])], routs[k], send, recv, k, (x, y, 1 - c))
                for k in range(n)]

    def start(rins, routs, sems):
        for cp in copies(rins, routs, sems):
            cp.start()

    def finish(rins, routs, sems):
        for cp in copies(rins, routs, sems):
            cp.wait()

    return _Rider(parts, [jax.ShapeDtypeStruct((4, a.shape[1] // 2, a.shape[2]), a.dtype) for a in parts],
                  [_sem(n), _sem(n)], start, finish)


def _rider_scatter(parts):
    parts = tuple(parts)
    n = len(parts)

    def copies(rins, routs, sems, arrivals=True):
        send, recv = sems
        x, y, c = _place()
        me = 2 * x + y
        out, inc = [], []
        for j, (px, py) in enumerate(_other_chips(x, y)):
            for k in range(n):
                out.append(_remote(rins[k].at[2 * px + py], routs[k].at[me], send, recv, n * j + k, (px, py, c)))
                if arrivals:
                    inc.append(_remote(rins[k].at[me], routs[k].at[2 * px + py], send, recv, n * j + k, (px, py, c)))
        return out, inc

    def start(rins, routs, sems):
        for cp in copies(rins, routs, sems, arrivals=False)[0]:
            cp.start()

    def finish(rins, routs, sems):
        out, inc = copies(rins, routs, sems)
        for cp in inc:
            cp.wait_recv()
        for cp in out:
            cp.wait_send()

    return _Rider(parts, [jax.ShapeDtypeStruct(a.shape, a.dtype) for a in parts], [_sem(3 * n), _sem(3 * n)], start, finish)


def _rider_share(fulls):
    fulls = tuple(fulls)
    n = len(fulls)

    def copies(routs, sems, arrivals=True):
        send, recv = sems
        x, y, c = _place()
        out, inc = [], []
        for k in range(n):
            mine, theirs = routs[k].at[_half(c, fulls[k].shape[0])], routs[k].at[_half(1 - c, fulls[k].shape[0])]
            out.append(_remote(mine, mine, send, recv, k, (x, y, 1 - c)))
            if arrivals:
                inc.append(_remote(theirs, theirs, send, recv, k, (x, y, 1 - c)))
        return out, inc

    def start(rins, routs, sems):
        for cp in copies(routs, sems, arrivals=False)[0]:
            cp.start()

    def finish(rins, routs, sems):
        out, inc = copies(routs, sems)
        for cp in inc:
            cp.wait_recv()
        for cp in out:
            cp.wait_send()

    return _Rider(fulls, [jax.ShapeDtypeStruct(a.shape, a.dtype) for a in fulls], [_sem(n), _sem(n)], start, finish,
                  aliases={k: k for k in range(n)})


def _pair_sum(core, full, recv, name, br=128):
    n, rows, cols = recv.shape

    def body(c_ref, a_ref, b_ref, o_ref):
        o_ref[...] = (a_ref[...] + b_ref[...]).astype(BF16)

    nb = rows // br
    return pl.pallas_call(
        body, name=name, out_shape=jax.ShapeDtypeStruct(recv.shape, BF16),
        grid_spec=pltpu.PrefetchScalarGridSpec(
            num_scalar_prefetch=1, grid=(n, nb),
            in_specs=[pl.BlockSpec((1, br, cols), lambda i, j, c: (i, c[0] * nb + j, 0)),
                      pl.BlockSpec((1, br, cols), lambda i, j, c: (i, j, 0))],
            out_specs=pl.BlockSpec((1, br, cols), lambda i, j, c: (i, j, 0))),
        compiler_params=_cparams(("parallel", "parallel")))(core, full, recv)


def _chip_sum(place, gathered, mine, name, br=128):
    _, r, c = gathered.shape
    nb = r // br

    def body(p_ref, g_ref, m_ref, o_ref):
        slab = lambda j: jnp.where(p_ref[1] == j, m_ref[j], g_ref[j]).astype(F32)
        o_ref[...] = ((slab(0) + slab(1)) + slab(2)) + slab(3)

    return pl.pallas_call(
        body, name=name, out_shape=jax.ShapeDtypeStruct((2 * r, c), F32),
        grid_spec=pltpu.PrefetchScalarGridSpec(
            num_scalar_prefetch=1, grid=(nb,),
            in_specs=[pl.BlockSpec((4, br, c), lambda i, p: (0, i, 0)), pl.BlockSpec((4, br, c), lambda i, p: (0, i, 0))],
            out_specs=pl.BlockSpec((br, c), lambda i, p: (p[0] * nb + i, 0))),
        compiler_params=_cparams(("parallel",)))(place, gathered, mine)


def _adamw(w, g, m, v, name, br):
    n, r, c = w.shape

    def body(w_ref, g_ref, m_ref, v_ref, d_ref, m2_ref, v2_ref):
        d_ref[...], m2_ref[...], v2_ref[...] = _adam_math(w_ref[...], g_ref[...], m_ref[...], v_ref[...])

    spec = pl.BlockSpec((1, br, c), lambda i, j: (i, j, 0))
    shp = jax.ShapeDtypeStruct(w.shape, F32)
    return pl.pallas_call(body, grid=(n, r // br), name=name, in_specs=[spec] * 4, out_specs=[spec] * 3,
                          out_shape=[shp] * 3, compiler_params=_cparams(("parallel", "parallel")))(w, g, m, v)


def _adamw_w_in(w, g, m, v, name, bc=31):
    cols = w.shape[2]
    lead = lambda a: jnp.transpose(a, (2, 0, 1))
    g = jnp.stack([a[:, 0:cols] for a in g])

    def body(w_ref, g_ref, m_ref, v_ref, go_ref, d_ref, m2_ref, v2_ref):
        for l in range(2):
            gv = g_ref[:, l, :]
            d_ref[:, l, :], m2_ref[:, l, :], v2_ref[:, l, :] = _adam_math(w_ref[:, l, :], gv, m_ref[:, l, :], v_ref[:, l, :])
            go_ref[:, l, :] = gv

    spec = pl.BlockSpec((bc, 2, D), lambda i: (i, 0, 0))
    outs = pl.pallas_call(body, grid=(cols // bc,), name=name, in_specs=[spec] * 4, out_specs=[spec] * 4,
                          out_shape=[jax.ShapeDtypeStruct((cols, 2, D), F32)] * 4,
                          compiler_params=_cparams(("parallel",)))(lead(w), lead(g), lead(m), lead(v))
    return [jnp.transpose(o, (1, 2, 0)) for o in outs]


_SMALL_NAMES = ("norm_w", "conv_a_w", "gla_gate_w", "gla_gate_b", "gla_norm_w", "pool_w", "pool_scale", "ssd_conv_w",
                "ssd_conv_b", "ssd_dt_bias", "ssd_a_log", "ssd_d", "ssd_norm_w", "final_norm_w")
SMALL_ROWS = 72


def _adam_math(w, g, m, v):
    m2 = ADAM_B1 * m + (1.0 - ADAM_B1) * g
    v2 = ADAM_B2 * v + (1.0 - ADAM_B2) * (g * g)
    m_hat = m2 / (1.0 - ADAM_B1 ** ADAM_STEP)
    v_hat = v2 / (1.0 - ADAM_B2 ** ADAM_STEP)
    return -ADAM_LR * (m_hat / (jnp.sqrt(v_hat) + ADAM_EPS) + ADAM_WD * w), m2, v2


def _small_slices(name, chip):
    if name == "conv_a_w":
        return [((), slice(R_CAW, R_CAW + 3), slice(64 * chip, 64 * chip + 64))]
    if name == "ssd_conv_w":
        return [((), slice(R_SCW, R_SCW + 4), slice(192 * chip, 192 * chip + 192))]
    if name == "gla_gate_w":
        return [((), slice(0, 16), slice(768, 896))]
    if name == "pool_w":
        return [((g, slice(16 * q, 16 * q + 16)), slice(16, 32), slice(256 * q + 64 * g, 256 * q + 64 * g + 64))
                for g in range(4) for q in range(4)]
    row, lanes = {"gla_gate_b": (R_GB, slice(0, 128)), "gla_norm_w": (R_GNW, slice(0, 64)),
                  "pool_scale": (R_PSC, slice(0, 256)), "ssd_conv_b": (R_SCB, slice(0, 768)),
                  "ssd_dt_bias": (R_DTB, slice(16, 20)), "ssd_a_log": (R_AE, slice(0, 4)), "ssd_d": (R_DE, slice(0, 4)),
                  "ssd_norm_w": (R_SNW, slice(0, 256))}[name]
    return [((), slice(row, row + 1), lanes)]


def _small_allreduce(sg0, sg1, dnw0, dnw1, head):
    def body(sg0_ref, sg1_ref, dnw0_ref, dnw1_ref, head_ref, acc, stage, pair, rbuf, send_sems, recv_sems):
        x, y, c = _place()
        chip = 2 * x + y
        stage[0:32, :] = sg0_ref[...]
        stage[32:64, :] = sg1_ref[...]
        stage[64:65, :] = dnw0_ref[0:1, :]
        stage[65:66, :] = dnw1_ref[0:1, :]
        stage[66:68, :] = head_ref[0:2, :]
        stage[68:72, :] = jnp.zeros((4, D), F32)
        sib = _remote(stage, pair, send_sems, recv_sems, 0, (x, y, 1 - c))
        sib.start()
        sib.wait()
        rbuf[0] = stage[...] + pair[...]
        sends = [_remote(rbuf.at[0], rbuf.at[k], send_sems, recv_sems, k, (px, py, c))
                 for k, (px, py) in enumerate(_other_chips(x, y), start=1)]
        for cp in sends:
            cp.start()
        for cp in sends:
            cp.wait()
        slab = lambda d: jnp.where(d == 0, 0, jnp.where(d == 2, 1, jnp.where(d == 1, 2, 3)))
        total = rbuf[slab(jnp.bitwise_xor(chip, 0))]
        for s in range(1, 4):
            total = total + rbuf[slab(jnp.bitwise_xor(chip, s))]
        acc[...] = total

    vmem = pl.BlockSpec(memory_space=pltpu.VMEM)
    return pl.pallas_call(
        body, name="small_allreduce", in_specs=[vmem] * 5, out_specs=vmem,
        out_shape=jax.ShapeDtypeStruct((SMALL_ROWS, D), F32),
        scratch_shapes=[pltpu.VMEM((SMALL_ROWS, D), F32), pltpu.VMEM((SMALL_ROWS, D), F32),
                        pltpu.VMEM((4, SMALL_ROWS, D), F32), _sem(4), _sem(4)],
    )(sg0, sg1, dnw0, dnw1, head)


def _small_adamw(acc, w, m, v):
    n = len(_SMALL_NAMES)

    def body(*refs):
        acc = refs[0]
        w_refs, m_refs, v_refs = refs[1:1 + n], refs[1 + n:1 + 2 * n], refs[1 + 2 * n:1 + 3 * n]
        o = 1 + 3 * n
        g_out, d_out, m_out, v_out = refs[o:o + n], refs[o + n:o + 2 * n], refs[o + 2 * n:o + 3 * n], refs[o + 3 * n:o + 4 * n]
        loss_ref = refs[o + 4 * n]
        chip = 2 * lax.axis_index("x") + lax.axis_index("y")
        loss_ref[...] = acc[67:68, 0:1]

        def update(i, idx, g):
            d, m2, v2 = _adam_math(w_refs[i][idx], g, m_refs[i][idx], v_refs[i][idx])
            g_out[i][idx], d_out[i][idx], m_out[i][idx], v_out[i][idx] = g, d, m2, v2

        for i, name in enumerate(_SMALL_NAMES):
            if name == "final_norm_w":
                update(i, (slice(0, 1), slice(None)), acc[66:67, :])
            elif name == "norm_w":
                for l in range(2):
                    update(i, (slice(l, l + 1), slice(None)), acc[64 + l:65 + l, :])
            elif name in ("conv_a_w", "ssd_conv_w"):
                for s in range(4):
                    @pl.when(chip == s)
                    def _(i=i, name=name, s=s):
                        for l in range(2):
                            (_, rows, lanes), = _small_slices(name, s)
                            update(i, (l,), acc[rows.start + 32 * l:rows.stop + 32 * l, lanes])
            else:
                for l in range(2):
                    for idx, rows, lanes in _small_slices(name, 0):
                        g = acc[rows.start + 32 * l:rows.stop + 32 * l, lanes]
                        if w_refs[i].ndim == 2:
                            update(i, (slice(l, l + 1), slice(None)), g)
                        else:
                            update(i, (l,) + idx, g)

    args = [acc] + [d[k] for d in (w, m, v) for k in _SMALL_NAMES]
    shapes = [jax.ShapeDtypeStruct(w[k].shape, F32) for k in _SMALL_NAMES]
    vmem = pl.BlockSpec(memory_space=pltpu.VMEM)
    outs = pl.pallas_call(body, name="small_adamw", in_specs=[vmem] * len(args), out_specs=[vmem] * (4 * n + 1),
                          out_shape=shapes * 4 + [jax.ShapeDtypeStruct((1, 1), F32)])(*args)
    return outs[0:n], outs[n:2 * n], outs[2 * n:3 * n], outs[3 * n:4 * n], outs[4 * n]


def _mixer_consts(layer, conv_a_w, gla_gate_w, gla_gate_b, gla_norm_w, pool_w, pool_scale, ssd_conv_w, ssd_conv_b,
                  ssd_dt_bias, ssd_a_log, ssd_d, ssd_norm_w):
    def row(v):
        return jnp.pad(v.reshape(1, -1), ((0, 0), (0, 768 - v.size)))

    dtb = jnp.zeros((128,), F32).at[16:20].set(ssd_dt_bias[layer])
    rows = [jnp.pad(conv_a_w[layer], ((0, 0), (0, 512))), row(gla_gate_b[layer]), row(jnp.tile(gla_norm_w[layer], 4)),
            row(pool_scale[layer]), row(ssd_conv_b[layer]), row(dtb), row(jnp.repeat(-jnp.exp(ssd_a_log[layer]), 64)),
            row(jnp.repeat(ssd_d[layer], 64)), row(ssd_norm_w[layer]), jnp.zeros((1, 768), F32), ssd_conv_w[layer]]
    prm = jnp.concatenate(rows, axis=0)
    gw = jnp.zeros((128, 128), F32).at[0:16].set(gla_gate_w[layer]).astype(BF16)
    pw = jnp.zeros((256, 256), F32)
    for g in range(4):
        pw = pw.at[64 * g:64 * g + 64, 64 * g:64 * g + 64].set(pool_w[layer, g])
    return (prm, gw, pw.astype(BF16)) + _mixer_matrices()


def _grad_slabs(layer, dwp, dwo):
    return _split_dw_in(dwp, name=f"split_dw_in{layer}"), dwo.reshape(4, D // 4, D)


class _Comm:
    def __init__(self, w_in, w_out):
        self.w_in16 = jnp.pad(w_in.astype(BF16), ((0, 0), (0, 0), (0, SHARD_PAD - SHARD)))
        self.w_out16 = w_out.astype(BF16)
        self.core = lax.axis_index("c").astype(jnp.int32).reshape(1)
        self.chip = 2 * lax.axis_index("x") + lax.axis_index("y")
        self.place = jnp.stack([lax.axis_index("c"), self.chip]).astype(jnp.int32)

    def gather_ici(self, layer, extra=None):
        return _rider_gather_ici((self.w_in16[layer], self.w_out16[layer]), extra)

    def pair_sum(self, layer, slabs, received):
        return [_pair_sum(self.core, a, b, name=f"reduce_pair_sum{layer}_{k}") for k, (a, b) in enumerate(zip(slabs, received))]

    def chip_sum(self, layer, gathered, mine):
        return [_chip_sum(self.place, a, b, name=f"reduce_chip_sum{layer}_{k}") for k, (a, b) in enumerate(zip(gathered, mine))]

    def layer_weights(self, layer, s_in, s_out):
        own = lambda slabs, shard: jnp.stack([jnp.where(self.chip == s, shard, slabs[s]) for s in range(4)])
        wp, wpt = _assemble_w_in(own(s_in, self.w_in16[layer]), name=f"assemble_w_in{layer}")
        wo = own(s_out, self.w_out16[layer]).reshape(D, D)
        return wp, wpt, wo, wo.T


def _local_step(x, tgt, norm_w, final_norm_w, consts, wts0, wts1=None, comm=None):
    nw = [norm_w[l:l + 1] for l in range(2)]
    proj0, h0, slabs = _rmsproj(x, nw[0], wts0[0], name="rmsproj0", rider=comm and comm.gather_ici(1))
    (mix0, sg0, ss0, x1), slabs = _mixer_fwd(proj0, x, wts0[2], *consts[0], name="mixer_fwd0",
                                             rider=comm and _rider_gather_d2d(slabs))
    if comm:
        wts1 = comm.layer_weights(1, *slabs)
    proj1, h1, _ = _rmsproj(x1, nw[1], wts1[0], name="rmsproj1")
    (mix1, sg1, ss1, x2), _ = _mixer_fwd(proj1, x1, wts1[2], *consts[1], name="mixer_fwd1")
    dx, head = _head(x2, tgt, final_norm_w.reshape(1, D), name="loss_head")
    (dproj, mgr1, dwo1), _ = _mixer_bwd(proj1, dx, wts1[3], mix1, sg1, ss1, *consts[1], name="mixer_bwd1")
    dwp1, _ = _dwin(h1, dproj, name="dwin1")
    slabs1 = comm and _grad_slabs(1, dwp1, dwo1)
    (dx, dnw1), recv = _dxin(dproj, wts1[1], x1, dx, nw[1], name="dxin1", rider=comm and _rider_swap(slabs1))
    pairs1 = comm and comm.pair_sum(1, slabs1, recv)
    (dproj, mgr0, dwo0), gathered = _mixer_bwd(proj0, dx, wts0[3], mix0, sg0, ss0, *consts[0], name="mixer_bwd0",
                                               rider=comm and _rider_scatter(pairs1))
    dwp0, big1 = _dwin(h0, dproj, name="dwin0", rider=comm and _rider_share(comm.chip_sum(1, gathered, pairs1)))
    scat = None
    if comm:
        slabs0 = _grad_slabs(0, dwp0, dwo0)
        pairs0 = comm.pair_sum(0, slabs0, _run_rider(_rider_swap(slabs0), "reduce_swap0"))
        scat = _rider_scatter(pairs0)
    (dx, dnw0), gathered = _dxin(dproj, wts0[1], x, dx, nw[0], name="dxin0", rider=scat)
    if comm:
        big0 = _run_rider(_rider_share(comm.chip_sum(0, gathered, pairs0)), "reduce_share0")
        big = ((big0[0], big1[0]), (big0[1], big1[1]))
    else:
        big = ((dwp0, dwp1), (dwo0, dwo1))
    return head, dx, big, (dnw0, dnw1), (mgr0, mgr1)


def kernel(x, norm_w, w_in, conv_a_w, gla_gate_w, gla_gate_b, gla_norm_w, pool_w, pool_scale, ssd_conv_w, ssd_conv_b, ssd_dt_bias, ssd_a_log, ssd_d, ssd_norm_w, w_out, final_norm_w, loss_target, m_norm_w, m_w_in, m_conv_a_w, m_gla_gate_w, m_gla_gate_b, m_gla_norm_w, m_pool_w, m_pool_scale, m_ssd_conv_w, m_ssd_conv_b, m_ssd_dt_bias, m_ssd_a_log, m_ssd_d, m_ssd_norm_w, m_w_out, m_final_norm_w, v_norm_w, v_w_in, v_conv_a_w, v_gla_gate_w, v_gla_gate_b, v_gla_norm_w, v_pool_w, v_pool_scale, v_ssd_conv_w, v_ssd_conv_b, v_ssd_dt_bias, v_ssd_a_log, v_ssd_d, v_ssd_norm_w, v_w_out, v_final_norm_w):
    weights = dict(norm_w=norm_w, w_in=w_in, conv_a_w=conv_a_w, gla_gate_w=gla_gate_w, gla_gate_b=gla_gate_b,
                   gla_norm_w=gla_norm_w, pool_w=pool_w, pool_scale=pool_scale, ssd_conv_w=ssd_conv_w,
                   ssd_conv_b=ssd_conv_b, ssd_dt_bias=ssd_dt_bias, ssd_a_log=ssd_a_log, ssd_d=ssd_d,
                   ssd_norm_w=ssd_norm_w, w_out=w_out, final_norm_w=final_norm_w)
    m_in = dict(norm_w=m_norm_w, w_in=m_w_in, conv_a_w=m_conv_a_w, gla_gate_w=m_gla_gate_w, gla_gate_b=m_gla_gate_b,
                gla_norm_w=m_gla_norm_w, pool_w=m_pool_w, pool_scale=m_pool_scale, ssd_conv_w=m_ssd_conv_w,
                ssd_conv_b=m_ssd_conv_b, ssd_dt_bias=m_ssd_dt_bias, ssd_a_log=m_ssd_a_log, ssd_d=m_ssd_d,
                ssd_norm_w=m_ssd_norm_w, w_out=m_w_out, final_norm_w=m_final_norm_w)
    v_in = dict(norm_w=v_norm_w, w_in=v_w_in, conv_a_w=v_conv_a_w, gla_gate_w=v_gla_gate_w, gla_gate_b=v_gla_gate_b,
                gla_norm_w=v_gla_norm_w, pool_w=v_pool_w, pool_scale=v_pool_scale, ssd_conv_w=v_ssd_conv_w,
                ssd_conv_b=v_ssd_conv_b, ssd_dt_bias=v_ssd_dt_bias, ssd_a_log=v_ssd_a_log, ssd_d=v_ssd_d,
                ssd_norm_w=v_ssd_norm_w, w_out=v_w_out, final_norm_w=v_final_norm_w)
    order = ("norm_w", "w_in", "conv_a_w", "gla_gate_w", "gla_gate_b", "gla_norm_w", "pool_w", "pool_scale",
             "ssd_conv_w", "ssd_conv_b", "ssd_dt_bias", "ssd_a_log", "ssd_d", "ssd_norm_w", "w_out", "final_norm_w")
    t = x.shape[1]

    comm = _Comm(w_in, w_out)
    cshard = jnp.zeros((16, 256), F32)
    for l in range(2):
        cshard = cshard.at[8 * l:8 * l + 3, 0:64].set(conv_a_w[l]).at[8 * l + 3:8 * l + 7, 0:192].set(ssd_conv_w[l])
    s_in, s_out, g_c = _run_rider(comm.gather_ici(0, cshard), "gather_ici0")
    s_in, s_out = _run_rider(_rider_gather_d2d((s_in, s_out)), "gather_d2d0")
    g_c = [jnp.where(comm.chip == s, cshard, g_c[s]) for s in range(4)]
    conv_a_full = jnp.stack([jnp.concatenate([g_c[s][8 * l:8 * l + 3, 0:64] for s in range(4)], axis=-1) for l in range(2)])
    ssd_conv_full = jnp.stack([jnp.concatenate([g_c[s][8 * l + 3:8 * l + 7, 0:192] for s in range(4)], axis=-1)
                               for l in range(2)])
    consts = [_mixer_consts(l, conv_a_full, gla_gate_w, gla_gate_b, gla_norm_w, pool_w, pool_scale, ssd_conv_full,
                            ssd_conv_b, ssd_dt_bias, ssd_a_log, ssd_d, ssd_norm_w) for l in range(2)]

    head, dx, big, dnw, mgr = _local_step(x.reshape(t, D), loss_target.reshape(t, D), norm_w, final_norm_w, consts,
                                          comm.layer_weights(0, s_in, s_out), comm=comm)

    as2d = lambda d: {k: (d[k].reshape(1, D) if k == "final_norm_w" else d[k]) for k in _SMALL_NAMES}
    small = _small_adamw(_small_allreduce(mgr[0], mgr[1], dnw[0], dnw[1], head), as2d(weights), as2d(m_in), as2d(v_in))
    grads, delta, new_m, new_v = ({k: (a.reshape(D) if k == "final_norm_w" else a) for k, a in zip(_SMALL_NAMES, part)}
                                  for part in small[0:4])
    loss = small[4].reshape(())

    grads["w_out"] = jnp.stack(big[1])

    grads["w_in"], delta["w_in"], new_m["w_in"], new_v["w_in"] = _adamw_w_in(w_in, big[0], m_w_in, v_w_in, name="adamw_w_in")
    delta["w_out"], new_m["w_out"], new_v["w_out"] = _adamw(w_out, grads["w_out"], m_w_out, v_w_out, name="adamw_w_out", br=256)

    return (loss, dx.reshape(1, t, D), *[grads[k] for k in order], *[delta[k] for k in order],
            *[new_m[k] for k in order], *[new_v[k] for k in order])
```

```python
import functools

import jax
import jax.numpy as jnp
from jax import lax
from jax.experimental import pallas as pl
from jax.experimental.pallas import tpu as pltpu

F32 = jnp.float32
BF16 = jnp.bfloat16
MESH = pl.DeviceIdType.MESH

D = 1024
CH = 64
EPS = 1e-6
NP = 3456
NPROJ = 3348
NPM = 3328
GLA_SCALE = 32.0 ** -0.5
INV_TAU = 1.0 / 16.0
TB = 512
NCH = TB // CH
assert TB % 256 == 0

C_AH, C_AB, C_AC, C_AZ, C_GQ, C_GK, C_GV = 0, 256, 512, 768, 1024, 1152, 1280
C_GZ, C_PU, C_PZ, C_SZ, C_SX, C_TL = 1536, 1792, 2048, 2304, 2560, 3328
_PERM = ((0, 1536), (1552, 1792), (1536, 16), (3344, 4))
_UNPERM = ((0, 1536), (3328, 16), (1536, 1792), (3344, 4))

R_CAW, R_GB, R_GNW, R_PSC, R_SCB, R_DTB, R_AE, R_DE, R_SNW, R_SCW = 0, 3, 4, 5, 6, 7, 8, 9, 10, 12

ADAM_LR, ADAM_B1, ADAM_B2, ADAM_EPS, ADAM_WD, ADAM_STEP = 0.001, 0.9, 0.999, 1e-08, 0.01, 10

VMEM_LIMIT = 56 * 1024 * 1024


def _cparams(sem, limit=VMEM_LIMIT):
    return pltpu.CompilerParams(dimension_semantics=sem, vmem_limit_bytes=limit)


_ANY = pl.BlockSpec(memory_space=pl.ANY)


def _place():
    return lax.axis_index("x"), lax.axis_index("y"), lax.axis_index("c")


class _Rider:
    def __init__(self, inputs, out_shapes, sems, start, finish, aliases=None):
        self.inputs, self.out_shapes, self.sems = tuple(inputs), tuple(out_shapes), tuple(sems)
        self.start, self.finish, self.aliases = start, finish, dict(aliases or {})


def _call(body, args, *, grid, in_specs, out_specs, out_shape, name, sem, scratch_shapes=(), rider=None):
    if rider is None:
        outs = pl.pallas_call(body, grid=grid, name=name, in_specs=list(in_specs), out_specs=list(out_specs),
                              out_shape=list(out_shape), scratch_shapes=list(scratch_shapes),
                              compiler_params=_cparams(sem))(*args)
        return list(outs), []
    ni, no, ns = len(args), len(out_shape), len(scratch_shapes)
    ri, ro = len(rider.inputs), len(rider.out_shapes)

    def full(*refs):
        ins, rins = refs[:ni], refs[ni:ni + ri]
        outs, routs = refs[ni + ri:ni + ri + no], refs[ni + ri + no:ni + ri + no + ro]
        scr, rsem = refs[ni + ri + no + ro:ni + ri + no + ro + ns], refs[ni + ri + no + ro + ns:]
        first = functools.reduce(jnp.logical_and, [pl.program_id(a) == 0 for a in range(len(grid))])
        last = functools.reduce(jnp.logical_and, [pl.program_id(a) == grid[a] - 1 for a in range(len(grid))])

        @pl.when(first)
        def _():
            rider.start(rins, routs, rsem)

        body(*ins, *outs, *scr)

        @pl.when(last)
        def _():
            rider.finish(rins, routs, rsem)

    outs = pl.pallas_call(
        full, grid=grid, name=name, in_specs=list(in_specs) + [_ANY] * ri, out_specs=list(out_specs) + [_ANY] * ro,
        out_shape=list(out_shape) + list(rider.out_shapes), scratch_shapes=list(scratch_shapes) + list(rider.sems),
        input_output_aliases={ni + k: no + v for k, v in rider.aliases.items()},
        compiler_params=_cparams(("arbitrary",) * len(grid)))(*args, *rider.inputs)
    return list(outs[:no]), list(outs[no:])


def _run_rider(rider, name):
    ri = len(rider.inputs)

    def body(*refs):
        rins, routs, rsem = refs[:ri], refs[ri:ri + len(rider.out_shapes)], refs[ri + len(rider.out_shapes):]
        rider.start(rins, routs, rsem)
        rider.finish(rins, routs, rsem)

    return list(pl.pallas_call(body, name=name, in_specs=[_ANY] * ri, out_specs=[_ANY] * len(rider.out_shapes),
                               out_shape=list(rider.out_shapes), scratch_shapes=list(rider.sems),
                               input_output_aliases=dict(rider.aliases))(*rider.inputs))


def _dot(a, b):
    return jnp.dot(a.astype(BF16), b.astype(BF16), preferred_element_type=F32)


def _dot_nt(a, b):
    return lax.dot_general(a.astype(BF16), b.astype(BF16), (((1,), (1,)), ((), ())), preferred_element_type=F32)


def _dot_tn(a, b):
    return lax.dot_general(a.astype(BF16), b.astype(BF16), (((0,), (0,)), ((), ())), preferred_element_type=F32)


def _split(a):
    hi = a.astype(BF16)
    lo = (a - hi.astype(F32)).astype(BF16)
    return hi, lo


def _dot2_l(a, b):
    hi, lo = _split(a)
    return _dot(hi, b) + _dot(lo, b)


def _dot2_r(a, b):
    hi, lo = _split(b)
    return _dot(a, hi) + _dot(a, lo)


def _dot3_l(a, b):
    hi, lo = _split(a)
    lo2 = ((a - hi.astype(F32)) - lo.astype(F32)).astype(BF16)
    return _dot(hi, b) + _dot(lo, b) + _dot(lo2, b)


def _dot2_nt(a, b):
    hi, lo = _split(a)
    return _dot_nt(hi, b) + _dot_nt(lo, b)


def _silu(z):
    return z * jax.nn.sigmoid(z)


def _lse1(x):
    return jnp.log(1.0 + jnp.exp(-jnp.abs(x)))


def _cs(a):
    return jnp.sum(a, axis=0, keepdims=True)


def _iota(shape, dim):
    return lax.broadcasted_iota(jnp.int32, shape, dim)


def _mixer_matrices():
    r, c = _iota((256, 256), 0), _iota((256, 256), 1)
    same_chunk = (r >> 6) == (c >> 6)
    mats = jnp.stack([jnp.where((c > r) & same_chunk, 1.0, 0.0), jnp.where((c < r) & same_chunk, 1.0, 0.0),
                      jnp.where(same_chunk, 1.0 / 64.0, 0.0), jnp.where((r < 128) & (r - 16 == (c >> 6)), 1.0, 0.0)])
    mask = jnp.where((_iota((256, 128), 0) >> 6) == (_iota((256, 128), 1) >> 5), 1.0, 0.0)
    return mats.astype(BF16), mask.astype(F32)


def _dn(ext, k, n, h):
    return pltpu.roll(ext, k, axis=0)[h:h + n]


def _up(ext, k, n):
    return pltpu.roll(ext, ext.shape[0] - k, axis=0)[:n]


def _pool_lane_select(lane, s2, s4, s8, s16):
    return jnp.where(lane < 64, s2, jnp.where(lane < 128, s4, jnp.where(lane < 192, s8, s16)))


def _winsum_dn(ext, lane):
    s2 = ext + pltpu.roll(ext, 1, axis=0)
    s4 = s2 + pltpu.roll(s2, 2, axis=0)
    s8 = s4 + pltpu.roll(s4, 4, axis=0)
    s16 = s8 + pltpu.roll(s8, 8, axis=0)
    return _pool_lane_select(lane, s2, s4, s8, s16)


def _winsum_up(ext, lane):
    m = ext.shape[0]
    s2 = ext + pltpu.roll(ext, m - 1, axis=0)
    s4 = s2 + pltpu.roll(s2, m - 2, axis=0)
    s8 = s4 + pltpu.roll(s4, m - 4, axis=0)
    s16 = s8 + pltpu.roll(s8, m - 8, axis=0)
    return _pool_lane_select(lane, s2, s4, s8, s16)


def _pool_inv_count(tile, n):
    lane = _iota((1, 256), 1)
    win = _pool_lane_select(lane, 2.0, 4.0, 8.0, 16.0).astype(F32)
    tpos = (tile * n + _iota((n, 1), 0) + 1).astype(F32)
    return jnp.where(tpos >= win, 1.0 / win, 1.0 / tpos)


def _silu_pair(z):
    s = jax.nn.sigmoid(z)
    return z * s, s * (1.0 + z * (1.0 - s))


def _chunks(a):
    return [a[c * CH:(c + 1) * CH] for c in range(a.shape[0] // CH)]


def _halves(fn, a, b):
    return jnp.concatenate([fn(a[:, 0:128], b[:, 0:128]), fn(a[:, 128:256], b[:, 128:256])], axis=1)


def _chunk_sums(tri, a):
    return jnp.concatenate([_dot2_r(tri, a[r:r + 256]) for r in range(0, a.shape[0], 256)], axis=0)


def _mixer_tile_prep(p_ref, t_ref, xc, prm_ref, gw_v, cm_ref, mk_ref):
    tail = t_ref[...]
    pre = _dot(tail, gw_v) + prm_ref[R_GB:R_GB + 1, 0:128]
    la = (jnp.minimum(pre, 0.0) - _lse1(pre)) * INV_TAU
    dtin = tail + prm_ref[R_DTB:R_DTB + 1, 0:128]
    dtf = jnp.maximum(dtin, 0.0) + _lse1(dtin)
    dte = _dot2_l(dtf, cm_ref[3, 0:128, :])
    da = dte * prm_ref[R_AE:R_AE + 1, 0:256]
    rev = _chunk_sums(cm_ref[0], jnp.concatenate([la, da], axis=1))
    dec = jnp.exp(rev[:, 0:128])
    kd = p_ref[:, C_GK:C_GK + 128].astype(F32) * dec
    wdec = jnp.exp(rev[:, 128:384])
    w = wdec * dte
    xw = xc[:, 0:256] * w
    d_s = [jnp.exp(_cs(a)) for a in _chunks(la)]
    et = [jnp.exp(_cs(a)) for a in _chunks(da)]
    mask_t = mk_ref[...]
    ut_g = [_dot_tn(v, k) * mask_t for v, k in zip(_chunks(p_ref[:, C_GV:C_GV + 256].astype(F32)), _chunks(kd))]
    ut_s = [_halves(_dot_tn, b, x) for b, x in zip(_chunks(xc[:, 256:512]), _chunks(xw))]
    return tail, pre, dtin, dte, dec, kd, wdec, w, xw, d_s, et, ut_g, ut_s


def _rmsproj(x, nw, wp, name, tm=512, rider=None):
    t = x.shape[0]

    def body(x_ref, nw_ref, w_ref, o_ref, t_ref, h_ref):
        xv = x_ref[...]
        rs = lax.rsqrt(jnp.mean(xv * xv, axis=-1, keepdims=True) + EPS)
        h = (xv * rs * nw_ref[...]).astype(BF16)
        h_ref[...] = h
        proj = jnp.dot(h, w_ref[...], preferred_element_type=F32)
        o_ref[...] = proj[:, 0:NPM].astype(BF16)
        t_ref[...] = proj[:, NPM:NP]

    (proj, tail, h), extra = _call(
        body, (x, nw, wp), grid=(t // tm,), name=name, sem=("parallel",), rider=rider,
        in_specs=[pl.BlockSpec((tm, D), lambda i: (i, 0)), pl.BlockSpec((1, D), lambda i: (0, 0)),
                  pl.BlockSpec((D, NP), lambda i: (0, 0))],
        out_specs=[pl.BlockSpec((tm, NPM), lambda i: (i, 0)), pl.BlockSpec((tm, NP - NPM), lambda i: (i, 0)),
                   pl.BlockSpec((tm, D), lambda i: (i, 0))],
        out_shape=[jax.ShapeDtypeStruct((t, NPM), BF16), jax.ShapeDtypeStruct((t, NP - NPM), F32),
                   jax.ShapeDtypeStruct((t, D), BF16)])
    return (proj, tail), h, extra


def _head(x, tgt, fw, name, tm=512):
    t = x.shape[0]

    def body(x_ref, t_ref, w_ref, dx_ref, acc_ref):
        @pl.when(pl.program_id(0) == 0)
        def _():
            acc_ref[...] = jnp.zeros_like(acc_ref)

        xv = x_ref[...]
        w = w_ref[...]
        rs = lax.rsqrt(jnp.mean(xv * xv, axis=-1, keepdims=True) + EPS)
        xh = xv * rs
        err = xh * w - t_ref[...]
        dy = err * (1.0 / D)
        dxh = dy * w
        dx_ref[...] = rs * (dxh - xh * jnp.mean(dxh * xh, axis=-1, keepdims=True))
        acc_ref[0:1, :] += _cs(dy * xh)
        acc_ref[1:2, :] += jnp.zeros((1, D), F32) + (0.5 / D) * jnp.sum(err * err)

    return pl.pallas_call(
        body, grid=(t // tm,), name=name,
        in_specs=[pl.BlockSpec((tm, D), lambda i: (i, 0)), pl.BlockSpec((tm, D), lambda i: (i, 0)),
                  pl.BlockSpec((1, D), lambda i: (0, 0))],
        out_specs=[pl.BlockSpec((tm, D), lambda i: (i, 0)), pl.BlockSpec((8, D), lambda i: (0, 0))],
        out_shape=[jax.ShapeDtypeStruct((t, D), F32), jax.ShapeDtypeStruct((8, D), F32)],
        compiler_params=_cparams(("arbitrary",)),
    )(x, tgt, fw)


def _dxin(dp, wpt, x, dxn, nw, name, tm=512, rider=None):
    t = x.shape[0]

    def body(dp_ref, w_ref, x_ref, dxn_ref, nw_ref, dx_ref, dnw_ref):
        @pl.when(pl.program_id(0) == 0)
        def _():
            dnw_ref[...] = jnp.zeros_like(dnw_ref)

        acc = jnp.zeros((1, D), F32)
        for rows in (pl.ds(0, tm // 2), pl.ds(tm // 2, tm // 2)):
            dh = jnp.dot(dp_ref[rows, :].astype(BF16), w_ref[...], preferred_element_type=F32)
            xv = x_ref[rows, :]
            rs = lax.rsqrt(jnp.mean(xv * xv, axis=-1, keepdims=True) + EPS)
            xh = xv * rs
            acc = acc + _cs(dh * xh)
            dxh = dh * nw_ref[...]
            dx_ref[rows, :] = dxn_ref[rows, :] + rs * (dxh - xh * jnp.mean(dxh * xh, axis=-1, keepdims=True))
        dnw_ref[0:1, :] += acc

    return _call(
        body, (dp, wpt, x, dxn, nw), grid=(t // tm,), name=name, sem=("arbitrary",), rider=rider,
        in_specs=[pl.BlockSpec((tm, NP), lambda i: (i, 0)), pl.BlockSpec((NP, D), lambda i: (0, 0)),
                  pl.BlockSpec((tm, D), lambda i: (i, 0)), pl.BlockSpec((tm, D), lambda i: (i, 0)),
                  pl.BlockSpec((1, D), lambda i: (0, 0))],
        out_specs=[pl.BlockSpec((tm, D), lambda i: (i, 0)), pl.BlockSpec((8, D), lambda i: (0, 0))],
        out_shape=[jax.ShapeDtypeStruct((t, D), F32), jax.ShapeDtypeStruct((8, D), F32)])


def _dwin(h, dp, name, tm=1024, tn=NP, rider=None):
    t = h.shape[0]

    def body(h_ref, dp_ref, o_ref):
        @pl.when(pl.program_id(1) == 0)
        def _():
            o_ref[...] = jnp.zeros_like(o_ref)

        o_ref[...] += _dot_tn(h_ref[...], dp_ref[...])

    (dwp,), extra = _call(
        body, (h, dp), grid=(NP // tn, t // tm), name=name, sem=("parallel", "arbitrary"), rider=rider,
        in_specs=[pl.BlockSpec((tm, D), lambda j, i: (i, 0)), pl.BlockSpec((tm, tn), lambda j, i: (i, j))],
        out_specs=[pl.BlockSpec((D, tn), lambda j, i: (0, j))], out_shape=[jax.ShapeDtypeStruct((D, NP), F32)])
    return dwp, extra


def _mixer_fwd(proj, x, wo, prm, gw, pw, cmat, mask, name, rider=None):
    proj, tail = proj
    t = proj.shape[0]
    nt, nc = t // TB, t // CH

    def body(p_ref, t_ref, x_ref, wo_ref, prm_ref, gw_ref, pw_ref, cm_ref, mk_ref, mix_ref, sg_ref, ss_ref, xn_ref,
             sg_s, ss_s, h_ua, h_pu, h_sx):
        i = pl.program_id(0)

        @pl.when(i == 0)
        def _():
            for r in (sg_s, ss_s, h_ua, h_pu, h_sx):
                r[...] = jnp.zeros_like(r)

        lane = _iota((1, 256), 1)
        u = p_ref[:, C_AC:C_AC + 256].astype(F32) * p_ref[:, C_AH:C_AH + 256].astype(F32)
        ext = jnp.concatenate([h_ua[...], u], axis=0)
        cv = (prm_ref[R_CAW + 2:R_CAW + 3, 0:256] * u + prm_ref[R_CAW + 1:R_CAW + 2, 0:256] * _dn(ext, 1, TB, 8)
              + prm_ref[R_CAW:R_CAW + 1, 0:256] * _dn(ext, 2, TB, 8))
        mix_ref[:, 0:256] = (p_ref[:, C_AB:C_AB + 256].astype(F32) * cv * _silu(p_ref[:, C_AZ:C_AZ + 256].astype(F32))).astype(BF16)
        h_ua[...] = u[TB - 8:, :]
        pu = p_ref[:, C_PU:C_PU + 256].astype(F32)
        ext = jnp.concatenate([h_pu[...], pu], axis=0)
        pooled = _winsum_dn(ext, lane)[16:] * _pool_inv_count(i, TB) - pu
        mixed = _dot(pooled, pw_ref[...])
        mix_ref[:, 512:768] = (prm_ref[R_PSC:R_PSC + 1, 0:256] * mixed * _silu(p_ref[:, C_PZ:C_PZ + 256].astype(F32))).astype(BF16)
        h_pu[...] = pu[TB - 16:, :]
        sx = p_ref[:, C_SX:C_SX + 768].astype(F32)
        ext = jnp.concatenate([h_sx[...], sx], axis=0)
        xc = _silu(prm_ref[R_SCW + 3:R_SCW + 4, :] * sx + prm_ref[R_SCW + 2:R_SCW + 3, :] * _dn(ext, 1, TB, 8)
                   + prm_ref[R_SCW + 1:R_SCW + 2, :] * _dn(ext, 2, TB, 8) + prm_ref[R_SCW:R_SCW + 1, :] * _dn(ext, 3, TB, 8)
                   + prm_ref[R_SCB:R_SCB + 1, :])
        h_sx[...] = sx[TB - 8:, :]

        _, _, _, _, _, _, _, _, _, d_s, et, ut_g, ut_s = _mixer_tile_prep(p_ref, t_ref, xc, prm_ref, gw_ref[...], cm_ref, mk_ref)
        s_g, s_s = sg_s[...], ss_s[...]
        o, y = [], []
        qs = _chunks(p_ref[:, C_GQ:C_GQ + 128].astype(F32) * GLA_SCALE)
        cm = _chunks(xc[:, 512:768])
        for c in range(NCH):
            sg_ref[c] = s_g
            ss_ref[c] = s_s
            s_g = s_g * d_s[c] + ut_g[c]
            s_s = s_s * et[c] + ut_s[c]
            o.append(_dot_nt(qs[c], s_g))
            y.append(_halves(_dot, cm[c], s_s))
        sg_s[...] = s_g
        ss_s[...] = s_s
        o = jnp.concatenate(o, axis=0)
        on = o * lax.rsqrt(_dot2_l(o * o, cm_ref[2]) + EPS)
        mix_ref[:, 256:512] = (on * prm_ref[R_GNW:R_GNW + 1, 0:256] * _silu(p_ref[:, C_GZ:C_GZ + 256].astype(F32))).astype(BF16)
        y2 = ((jnp.concatenate(y, axis=0) + prm_ref[R_DE:R_DE + 1, 0:256] * xc[:, 0:256])
              * _silu(p_ref[:, C_SZ:C_SZ + 256].astype(F32)))
        mix_ref[:, 768:1024] = (y2 * lax.rsqrt(jnp.mean(y2 * y2, axis=-1, keepdims=True) + EPS)
                                * prm_ref[R_SNW:R_SNW + 1, 0:256]).astype(BF16)
        xn_ref[...] = x_ref[...] + jnp.dot(mix_ref[...], wo_ref[...], preferred_element_type=F32)

    return _call(
        body, (proj, tail, x, wo, prm, gw, pw, cmat, mask), grid=(nt,), name=name, sem=("arbitrary",), rider=rider,
        in_specs=[pl.BlockSpec((TB, NPM), lambda i: (i, 0)), pl.BlockSpec((TB, NP - NPM), lambda i: (i, 0)),
                  pl.BlockSpec((TB, D), lambda i: (i, 0)),
                  pl.BlockSpec((D, D), lambda i: (0, 0)), pl.BlockSpec((16, 768), lambda i: (0, 0)),
                  pl.BlockSpec((128, 128), lambda i: (0, 0)), pl.BlockSpec((256, 256), lambda i: (0, 0)),
                  pl.BlockSpec((4, 256, 256), lambda i: (0, 0, 0)), pl.BlockSpec((256, 128), lambda i: (0, 0))],
        out_specs=[pl.BlockSpec((TB, D), lambda i: (i, 0)), pl.BlockSpec((NCH, 256, 128), lambda i: (i, 0, 0)),
                   pl.BlockSpec((NCH, 128, 256), lambda i: (i, 0, 0)), pl.BlockSpec((TB, D), lambda i: (i, 0))],
        out_shape=[jax.ShapeDtypeStruct((t, D), BF16), jax.ShapeDtypeStruct((nc, 256, 128), F32),
                   jax.ShapeDtypeStruct((nc, 128, 256), F32), jax.ShapeDtypeStruct((t, D), F32)],
        scratch_shapes=[pltpu.VMEM((256, 128), F32), pltpu.VMEM((128, 256), F32), pltpu.VMEM((8, 256), F32),
                        pltpu.VMEM((16, 256), F32), pltpu.VMEM((8, 768), F32)])


def _mixer_bwd(proj, dxn, wot, mix, sg, ss, prm, gw, pw, cmat, mask, name, rider=None):
    proj, tail = proj
    t = proj.shape[0]
    nt = t // TB
    rev = lambda i: nt - 1 - i

    def body(p_ref, hp_ref, t_ref, dxn_ref, wot_ref, mix_ref, sg_ref, ss_ref, prm_ref, gw_ref, pw_ref, cm_ref, mk_ref,
             dp_ref, sgc_ref, dwo_ref,
             gg_s, gs_s, h_dcv, h_dpl, h_dpre, gsm_ref, dgw_ref, dpw_ref, dm_ref):
        i = pl.program_id(0)
        tile = nt - 1 - i

        @pl.when(i == 0)
        def _():
            for r in (gg_s, gs_s, h_dcv, h_dpl, h_dpre, gsm_ref, dgw_ref, dpw_ref, dwo_ref):
                r[...] = jnp.zeros_like(r)

        dxn = dxn_ref[...].astype(BF16)
        dm_ref[...] = jnp.dot(dxn, wot_ref[...], preferred_element_type=F32)
        dwo_ref[...] += _dot_tn(mix_ref[...], dxn)

        lane = _iota((1, 256), 1)
        first = (tile > 0).astype(F32)
        ah, ac = p_ref[:, C_AH:C_AH + 256].astype(F32), p_ref[:, C_AC:C_AC + 256].astype(F32)
        ab, az = p_ref[:, C_AB:C_AB + 256].astype(F32), p_ref[:, C_AZ:C_AZ + 256].astype(F32)
        w0, w1, w2 = (prm_ref[R_CAW + j:R_CAW + j + 1, 0:256] for j in range(3))
        u = ac * ah
        ext = jnp.concatenate([(hp_ref[:, C_AC:C_AC + 256].astype(F32) * hp_ref[:, C_AH:C_AH + 256].astype(F32))[8:16] * first, u], axis=0)
        u1, u2 = _dn(ext, 1, TB, 8), _dn(ext, 2, TB, 8)
        cv = w2 * u + w1 * u1 + w0 * u2
        g = dm_ref[:, 0:256]
        sz, dsz = _silu_pair(az)
        dp_ref[:, C_AB:C_AB + 256] = (g * cv * sz).astype(BF16)
        dp_ref[:, C_AZ:C_AZ + 256] = (g * ab * cv * dsz).astype(BF16)
        dcv = g * ab * sz
        dext = jnp.concatenate([dcv, h_dcv[...]], axis=0)
        du = w2 * dcv + w1 * _up(dext, 1, TB) + w0 * _up(dext, 2, TB)
        dp_ref[:, C_AC:C_AC + 256] = (du * ah).astype(BF16)
        dp_ref[:, C_AH:C_AH + 256] = (du * ac).astype(BF16)
        gsm_ref[R_CAW:R_CAW + 1, 0:256] += _cs(dcv * u2)
        gsm_ref[R_CAW + 1:R_CAW + 2, 0:256] += _cs(dcv * u1)
        gsm_ref[R_CAW + 2:R_CAW + 3, 0:256] += _cs(dcv * u)
        h_dcv[...] = dcv[0:8, :]
        pu, pz = p_ref[:, C_PU:C_PU + 256].astype(F32), p_ref[:, C_PZ:C_PZ + 256].astype(F32)
        psc = prm_ref[R_PSC:R_PSC + 1, 0:256]
        icnt = _pool_inv_count(tile, TB)
        ext = jnp.concatenate([hp_ref[:, C_PU:C_PU + 256].astype(F32) * first, pu], axis=0)
        pooled = _winsum_dn(ext, lane)[16:] * icnt - pu
        pw_v = pw_ref[...]
        mixed = _dot(pooled, pw_v)
        g = dm_ref[:, 512:768]
        sz, dsz = _silu_pair(pz)
        gsm_ref[R_PSC:R_PSC + 1, 0:256] += _cs(g * mixed * sz)
        dp_ref[:, C_PZ:C_PZ + 256] = (g * psc * mixed * dsz).astype(BF16)
        dmixed = g * psc * sz
        dpw_ref[...] += _dot_tn(pooled, dmixed)
        dpooled = _dot_nt(dmixed, pw_v)
        qd = dpooled * icnt
        dext = jnp.concatenate([qd, h_dpl[...]], axis=0)
        dp_ref[:, C_PU:C_PU + 256] = (_winsum_up(dext, lane)[:TB] - dpooled).astype(BF16)
        h_dpl[...] = qd[0:16, :]
        sx = p_ref[:, C_SX:C_SX + 768].astype(F32)
        cw = [prm_ref[R_SCW + j:R_SCW + j + 1, :] for j in range(4)]
        ext = jnp.concatenate([hp_ref[:, C_SX:C_SX + 768].astype(F32)[8:16] * first, sx], axis=0)
        sx1, sx2, sx3 = _dn(ext, 1, TB, 8), _dn(ext, 2, TB, 8), _dn(ext, 3, TB, 8)
        cpre = cw[3] * sx + cw[2] * sx1 + cw[1] * sx2 + cw[0] * sx3 + prm_ref[R_SCB:R_SCB + 1, :]
        xc, dxc = _silu_pair(cpre)
        xs, bm, cm = xc[:, 0:256], xc[:, 256:512], xc[:, 512:768]

        gw_v = gw_ref[...]
        tail, pre, dtin, dte, dec, kd, wdec, w, xw, d_s, et, ut_g, ut_s = _mixer_tile_prep(p_ref, t_ref, xc, prm_ref,
                                                                                          gw_v, cm_ref, mk_ref)
        gmean = cm_ref[2]
        mask_t = mk_ref[...]
        gnw = prm_ref[R_GNW:R_GNW + 1, 0:256]
        a_e = prm_ref[R_AE:R_AE + 1, 0:256]
        d_e = prm_ref[R_DE:R_DE + 1, 0:256]
        snw = prm_ref[R_SNW:R_SNW + 1, 0:256]
        sg_in = [sg_ref[c] for c in range(NCH)]
        ss_in = [ss_ref[c] for c in range(NCH)]
        sg_n = [sg_in[c] * d_s[c] + ut_g[c] for c in range(NCH)]
        ss_n = [ss_in[c] * et[c] + ut_s[c] for c in range(NCH)]
        qs = _chunks(p_ref[:, C_GQ:C_GQ + 128].astype(F32) * GLA_SCALE)
        cm_c, bm_c, xw_c, kd_c = _chunks(cm), _chunks(bm), _chunks(xw), _chunks(kd)
        v_c = _chunks(p_ref[:, C_GV:C_GV + 256].astype(F32))
        o = jnp.concatenate([_dot_nt(qs[c], sg_n[c]) for c in range(NCH)], axis=0)
        y = jnp.concatenate([_halves(_dot, cm_c[c], ss_n[c]) for c in range(NCH)], axis=0) + d_e * xs
        gz = p_ref[:, C_GZ:C_GZ + 256].astype(F32)
        r = lax.rsqrt(_dot2_l(o * o, gmean) + EPS)
        on = o * r
        dyb = dm_ref[:, 256:512]
        sz, dsz = _silu_pair(gz)
        dp_ref[:, C_GZ:C_GZ + 256] = (dyb * on * gnw * dsz).astype(BF16)
        tg = dyb * sz
        gsm_ref[R_GNW:R_GNW + 1, 0:256] += _cs(tg * on)
        don = tg * gnw
        do_c = _chunks(r * (don - on * _dot2_l(don * on, gmean)))
        ssz = p_ref[:, C_SZ:C_SZ + 256].astype(F32)
        sil, dsil = _silu_pair(ssz)
        y2 = y * sil
        r = lax.rsqrt(jnp.mean(y2 * y2, axis=-1, keepdims=True) + EPS)
        yn = y2 * r
        dyd = dm_ref[:, 768:1024]
        gsm_ref[R_SNW:R_SNW + 1, 0:256] += _cs(dyd * yn)
        dn = dyd * snw
        dy2 = r * (dn - yn * jnp.mean(dn * yn, axis=-1, keepdims=True))
        dp_ref[:, C_SZ:C_SZ + 256] = (dy2 * y * dsil).astype(BF16)
        dy = dy2 * sil
        gsm_ref[R_DE:R_DE + 1, 0:256] += _cs(dy * xs)
        dy_c = _chunks(dy)
        dq = jnp.concatenate([_dot(do_c[c], sg_n[c]) for c in range(NCH)], axis=0)
        dp_ref[:, C_GQ:C_GQ + 128] = (dq * GLA_SCALE).astype(BF16)
        dcm = jnp.concatenate([_halves(_dot_nt, dy_c[c], ss_n[c]) for c in range(NCH)], axis=0)
        gg = [_dot_tn(do_c[c], qs[c]) * mask_t for c in range(NCH)]
        gs = [_halves(_dot_tn, cm_c[c], dy_c[c]) for c in range(NCH)]
        car_g, car_s = gg_s[...], gs_s[...]
        for c in reversed(range(NCH)):
            gg[c] = gg[c] + car_g
            gs[c] = gs[c] + car_s
            car_g = gg[c] * d_s[c]
            car_s = gs[c] * et[c]
        gg_s[...] = car_g
        gs_s[...] = car_s
        dkd = jnp.concatenate([_dot(v_c[c], gg[c]) for c in range(NCH)], axis=0)
        dp_ref[:, C_GV:C_GV + 256] = jnp.concatenate([_dot_nt(kd_c[c], gg[c]) for c in range(NCH)], axis=0).astype(BF16)
        dp_ref[:, C_GK:C_GK + 128] = (dkd * dec).astype(BF16)
        dbm = jnp.concatenate([_halves(_dot_nt, xw_c[c], gs[c]) for c in range(NCH)], axis=0)
        dxw = jnp.concatenate([_halves(_dot, bm_c[c], gs[c]) for c in range(NCH)], axis=0)
        dxs = dy * d_e + dxw * w
        dw = dxw * xs
        dsuf = _chunk_sums(cm_ref[1], jnp.concatenate([dkd * kd, dw * dte * wdec], axis=1))
        tot_g = jnp.concatenate([jnp.broadcast_to(_cs(gg[c] * sg_in[c]) * d_s[c], (CH, 128)) for c in range(NCH)], axis=0)
        tot_s = jnp.concatenate([jnp.broadcast_to(_cs(gs[c] * ss_in[c]) * et[c], (CH, 256)) for c in range(NCH)], axis=0)
        dpre = (dsuf[:, 0:128] + tot_g) * INV_TAU * jax.nn.sigmoid(-pre)
        dgw_ref[...] += _dot_tn(tail, dpre)
        gsm_ref[R_GB:R_GB + 1, 0:128] += _cs(dpre)
        dda = dsuf[:, 128:384] + tot_s
        gsm_ref[R_AE:R_AE + 1, 0:256] += _cs(dda * dte)
        dtail_s = _dot2_nt(dw * wdec + dda * a_e, cm_ref[3, 0:128, :]) * jax.nn.sigmoid(dtin)
        gsm_ref[R_DTB:R_DTB + 1, 0:128] += _cs(dtail_s)
        dp_ref[:, C_TL:C_TL + 128] = (_dot_nt(dpre, gw_v) + dtail_s).astype(BF16)
        dpre_c = jnp.concatenate([dxs, dbm, dcm], axis=1) * dxc
        dext = jnp.concatenate([dpre_c, h_dpre[...]], axis=0)
        dp_ref[:, C_SX:C_SX + 768] = (cw[3] * dpre_c + cw[2] * _up(dext, 1, TB) + cw[1] * _up(dext, 2, TB)
                                      + cw[0] * _up(dext, 3, TB)).astype(BF16)
        gsm_ref[R_SCW + 3:R_SCW + 4, :] += _cs(dpre_c * sx)
        gsm_ref[R_SCW + 2:R_SCW + 3, :] += _cs(dpre_c * sx1)
        gsm_ref[R_SCW + 1:R_SCW + 2, :] += _cs(dpre_c * sx2)
        gsm_ref[R_SCW:R_SCW + 1, :] += _cs(dpre_c * sx3)
        gsm_ref[R_SCB:R_SCB + 1, :] += _cs(dpre_c)
        h_dpre[...] = dpre_c[0:8, :]

        @pl.when(i == nt - 1)
        def _():
            ri, ci = _iota((256, 256), 0), _iota((256, 256), 1)
            per_head = jnp.where((ri >> 6) == ci, 1.0, 0.0).astype(BF16)
            per_dv = jnp.where((ri & 63) == ci, 1.0, 0.0).astype(BF16)
            row = _iota((8, 256), 0)
            top = gsm_ref[0:8, 0:256]
            sgc_ref[0:8, 0:256] = jnp.where(row == R_GNW, _dot3_l(top, per_dv), top)
            bot = gsm_ref[8:16, 0:256]
            fold = _dot3_l(jnp.where(row == R_AE - 8, bot * a_e, bot), per_head)
            sgc_ref[8:16, 0:256] = jnp.where((row == R_AE - 8) | (row == R_DE - 8), fold, bot)
            sgc_ref[0:16, 256:768] = gsm_ref[:, 256:768]
            sgc_ref[0:16, 768:896] = dgw_ref[0:16, :]
            sgc_ref[0:16, 896:1024] = jnp.zeros((16, 128), F32)
            diag = _pool_lane_select(lane, dpw_ref[0:64, :], dpw_ref[64:128, :], dpw_ref[128:192, :], dpw_ref[192:256, :])
            for q in range(4):
                sgc_ref[16:32, 256 * q:256 * q + 256] = diag[16 * q:16 * q + 16, :]

    return _call(
        body, (proj, proj, tail, dxn, wot, mix, sg, ss, prm, gw, pw, cmat, mask), grid=(nt,), name=name,
        sem=("arbitrary",), rider=rider,
        in_specs=[pl.BlockSpec((TB, NPM), lambda i: (rev(i), 0)),
                  pl.BlockSpec((16, NPM), lambda i: (jnp.maximum(rev(i) * (TB // 16) - 1, 0), 0)),
                  pl.BlockSpec((TB, NP - NPM), lambda i: (rev(i), 0)),
                  pl.BlockSpec((TB, D), lambda i: (rev(i), 0)), pl.BlockSpec((D, D), lambda i: (0, 0)),
                  pl.BlockSpec((TB, D), lambda i: (rev(i), 0)),
                  pl.BlockSpec((NCH, 256, 128), lambda i: (rev(i), 0, 0)),
                  pl.BlockSpec((NCH, 128, 256), lambda i: (rev(i), 0, 0)),
                  pl.BlockSpec((16, 768), lambda i: (0, 0)), pl.BlockSpec((128, 128), lambda i: (0, 0)),
                  pl.BlockSpec((256, 256), lambda i: (0, 0)), pl.BlockSpec((4, 256, 256), lambda i: (0, 0, 0)),
                  pl.BlockSpec((256, 128), lambda i: (0, 0))],
        out_specs=[pl.BlockSpec((TB, NP), lambda i: (rev(i), 0)), pl.BlockSpec((32, 1024), lambda i: (0, 0)),
                   pl.BlockSpec((D, D), lambda i: (0, 0))],
        out_shape=[jax.ShapeDtypeStruct((t, NP), BF16), jax.ShapeDtypeStruct((32, 1024), F32),
                   jax.ShapeDtypeStruct((D, D), F32)],
        scratch_shapes=[pltpu.VMEM((256, 128), F32), pltpu.VMEM((128, 256), F32), pltpu.VMEM((8, 256), F32),
                        pltpu.VMEM((16, 256), F32), pltpu.VMEM((8, 768), F32), pltpu.VMEM((16, 768), F32),
                        pltpu.VMEM((128, 128), F32), pltpu.VMEM((256, 256), F32), pltpu.VMEM((TB, D), F32)])


SHARD = NPROJ // 4
SHARD_PAD = 896


def _ranges_to_perm(o, n):
    out, p = [], 0
    for start, size in _PERM:
        a, b = max(o, start), min(o + n, start + size)
        if a < b:
            out.append((a, b - a, p + a - start))
        p += size
    return out


def _ranges_to_orig(p0, n):
    out, p = [], 0
    for start, size in _PERM:
        a, b = max(p0, p), min(p0 + n, p + size)
        if a < b:
            out.append((a, b - a, start + a - p))
        p += size
    return out


def _lane_window(load, lo, n, d, lane):
    a = 128 * (lo // 128)
    off = lo - a
    w = 128 if off + n <= 128 else 256
    chunk = load(a, w)
    shift = (d - off) % w
    if shift:
        chunk = pltpu.roll(chunk, shift, axis=1)
    return jnp.where((lane >= d) & (lane < d + n), chunk[:, 0:128], 0.0)


def _assemble_w_in(slabs, name, rb=256):
    def body(s_ref, wp_ref, wpt_ref):
        lane = _iota((1, 128), 1)
        for b in range(NP // 128):
            acc = jnp.zeros((rb, 128), F32)
            for p, n, o in _ranges_to_orig(128 * b, 128):
                while n > 0:
                    s, lo = o // SHARD, o % SHARD
                    cnt = min(n, SHARD - lo)
                    acc = acc + _lane_window(lambda a, w, s=s: s_ref[s, :, a:a + w].astype(F32), lo, cnt, p - 128 * b, lane)
                    o, p, n = o + cnt, p + cnt, n - cnt
            wp_ref[:, 128 * b:128 * b + 128] = acc.astype(BF16)
            wpt_ref[128 * b:128 * b + 128, :] = acc.T.astype(BF16)

    return pl.pallas_call(
        body, grid=(D // rb,), name=name,
        in_specs=[pl.BlockSpec((4, rb, SHARD_PAD), lambda i: (0, i, 0))],
        out_specs=[pl.BlockSpec((rb, NP), lambda i: (i, 0)), pl.BlockSpec((NP, rb), lambda i: (0, i))],
        out_shape=[jax.ShapeDtypeStruct((D, NP), BF16), jax.ShapeDtypeStruct((NP, D), BF16)],
        compiler_params=_cparams(("parallel",)))(slabs)


def _split_dw_in(dwp, name, rb=256):
    def body(g_ref, o_ref):
        lane = _iota((1, 128), 1)
        for s in range(4):
            for k in range(SHARD_PAD // 128):
                acc = jnp.zeros((rb, 128), F32)
                n_valid = min(128, SHARD - 128 * k)
                for o, n, p in _ranges_to_perm(SHARD * s + 128 * k, n_valid):
                    acc = acc + _lane_window(lambda a, w: g_ref[:, a:a + w], p, n, o - SHARD * s - 128 * k, lane)
                o_ref[s, :, 128 * k:128 * k + 128] = acc

    return pl.pallas_call(
        body, grid=(D // rb,), name=name,
        in_specs=[pl.BlockSpec((rb, NP), lambda i: (i, 0))],
        out_specs=pl.BlockSpec((4, rb, SHARD_PAD), lambda i: (0, i, 0)),
        out_shape=jax.ShapeDtypeStruct((4, D, SHARD_PAD), F32),
        compiler_params=_cparams(("parallel",)))(dwp)


def _half(c, n):
    return pl.ds(pl.multiple_of(c * (n // 2), n // 2), n // 2)


def _other_chips(x, y):
    return ((1 - x, y), (x, 1 - y), (1 - x, 1 - y))


def _remote(src, dst, send, recv, k, dev):
    return pltpu.make_async_remote_copy(src_ref=src, dst_ref=dst, send_sem=send.at[k], recv_sem=recv.at[k], device_id=dev,
                                        device_id_type=MESH)


def _sem(n):
    return pltpu.SemaphoreType.DMA((n,))


def _rider_gather_ici(shards, extra=None):
    shards = tuple(shards) + ((extra,) if extra is not None else ())
    n = len(shards)

    def copies(rins, routs, sems, arrivals=True):
        send, recv = sems
        x, y, c = _place()
        me = 2 * x + y
        out, inc = [], []
        for j, (px, py) in enumerate(_other_chips(x, y)):
            for k in range(n):
                whole = extra is not None and k == n - 1
                rows = pl.ds(0, shards[k].shape[0]) if whole else _half(c, shards[k].shape[0])
                out.append(_remote(rins[k].at[rows], routs[k].at[me, rows], send, recv, n * j + k, (px, py, c)))
                if arrivals:
                    inc.append(_remote(rins[k].at[rows], routs[k].at[2 * px + py, rows], send, recv, n * j + k, (px, py, c)))
        return out, inc

    def start(rins, routs, sems):
        for cp in copies(rins, routs, sems, arrivals=False)[0]:
            cp.start()

    def finish(rins, routs, sems):
        out, inc = copies(rins, routs, sems)
        for cp in inc:
            cp.wait_recv()
        for cp in out:
            cp.wait_send()

    return _Rider(shards, [jax.ShapeDtypeStruct((4,) + a.shape, a.dtype) for a in shards], [_sem(3 * n), _sem(3 * n)],
                  start, finish)


def _rider_gather_d2d(slabs):
    slabs = tuple(slabs)
    n = len(slabs)

    def copies(routs, sems, arrivals=True):
        send, recv = sems
        x, y, c = _place()
        out, inc = [], []
        for j, (px, py) in enumerate(_other_chips(x, y)):
            for k in range(n):
                rows = slabs[k].shape[1]
                mine, theirs = routs[k].at[2 * px + py, _half(c, rows)], routs[k].at[2 * px + py, _half(1 - c, rows)]
                out.append(_remote(mine, mine, send, recv, n * j + k, (x, y, 1 - c)))
                if arrivals:
                    inc.append(_remote(theirs, theirs, send, recv, n * j + k, (x, y, 1 - c)))
        return out, inc

    def start(rins, routs, sems):
        for cp in copies(routs, sems, arrivals=False)[0]:
            cp.start()

    def finish(rins, routs, sems):
        out, inc = copies(routs, sems)
        for cp in inc:
            cp.wait_recv()
        for cp in out:
            cp.wait_send()

    return _Rider(slabs, [jax.ShapeDtypeStruct(a.shape, a.dtype) for a in slabs], [_sem(3 * n), _sem(3 * n)], start, finish,
                  aliases={k: k for k in range(n)})


def _rider_swap(parts):
    parts = tuple(parts)
    n = len(parts)

    def copies(rins, routs, sems):
        send, recv = sems
        x, y, c = _place()
        return [_remote(rins[k].at[:, _half(1 - c, parts[k].shape[1])], routs[k], send, recv, k, (x, y, 1 - c))
                for k in range(n)]

    def start(rins, routs, sems):
        for cp in copies(rins, routs, sems):
            cp.start()

    def finish(rins, routs, sems):
        for cp in copies(rins, routs, sems):
            cp.wait()

    return _Rider(parts, [jax.ShapeDtypeStruct((4, a.shape[1] // 2, a.shape[2]), a.dtype) for a in parts],
                  [_sem(n), _sem(n)], start, finish)


def _rider_scatter(parts):
    parts = tuple(parts)
    n = len(parts)

    def copies(rins, routs, sems, arrivals=True):
        send, recv = sems
        x, y, c = _place()
        me = 2 * x + y
        out, inc = [], []
        for j, (px, py) in enumerate(_other_chips(x, y)):
            for k in range(n):
                out.append(_remote(rins[k].at[2 * px + py], routs[k].at[me], send, recv, n * j + k, (px, py, c)))
                if arrivals:
                    inc.append(_remote(rins[k].at[me], routs[k].at[2 * px + py], send, recv, n * j + k, (px, py, c)))
        return out, inc

    def start(rins, routs, sems):
        for cp in copies(rins, routs, sems, arrivals=False)[0]:
            cp.start()

    def finish(rins, routs, sems):
        out, inc = copies(rins, routs, sems)
        for cp in inc:
            cp.wait_recv()
        for cp in out:
            cp.wait_send()

    return _Rider(parts, [jax.ShapeDtypeStruct(a.shape, a.dtype) for a in parts], [_sem(3 * n), _sem(3 * n)], start, finish)


def _rider_share(fulls):
    fulls = tuple(fulls)
    n = len(fulls)

    def copies(routs, sems, arrivals=True):
        send, recv = sems
        x, y, c = _place()
        out, inc = [], []
        for k in range(n):
            mine, theirs = routs[k].at[_half(c, fulls[k].shape[0])], routs[k].at[_half(1 - c, fulls[k].shape[0])]
            out.append(_remote(mine, mine, send, recv, k, (x, y, 1 - c)))
            if arrivals:
                inc.append(_remote(theirs, theirs, send, recv, k, (x, y, 1 - c)))
        return out, inc

    def start(rins, routs, sems):
        for cp in copies(routs, sems, arrivals=False)[0]:
            cp.start()

    def finish(rins, routs, sems):
        out, inc = copies(routs, sems)
        for cp in inc:
            cp.wait_recv()
        for cp in out:
            cp.wait_send()

    return _Rider(fulls, [jax.ShapeDtypeStruct(a.shape, a.dtype) for a in fulls], [_sem(n), _sem(n)], start, finish,
                  aliases={k: k for k in range(n)})


def _pair_sum(core, full, recv, name, br=128):
    n, rows, cols = recv.shape

    def body(c_ref, a_ref, b_ref, o_ref):
        o_ref[...] = (a_ref[...] + b_ref[...]).astype(BF16)

    nb = rows // br
    return pl.pallas_call(
        body, name=name, out_shape=jax.ShapeDtypeStruct(recv.shape, BF16),
        grid_spec=pltpu.PrefetchScalarGridSpec(
            num_scalar_prefetch=1, grid=(n, nb),
            in_specs=[pl.BlockSpec((1, br, cols), lambda i, j, c: (i, c[0] * nb + j, 0)),
                      pl.BlockSpec((1, br, cols), lambda i, j, c: (i, j, 0))],
            out_specs=pl.BlockSpec((1, br, cols), lambda i, j, c: (i, j, 0))),
        compiler_params=_cparams(("parallel", "parallel")))(core, full, recv)


def _chip_sum(place, gathered, mine, name, br=128):
    _, r, c = gathered.shape
    nb = r // br

    def body(p_ref, g_ref, m_ref, o_ref):
        slab = lambda j: jnp.where(p_ref[1] == j, m_ref[j], g_ref[j]).astype(F32)
        o_ref[...] = ((slab(0) + slab(1)) + slab(2)) + slab(3)

    return pl.pallas_call(
        body, name=name, out_shape=jax.ShapeDtypeStruct((2 * r, c), F32),
        grid_spec=pltpu.PrefetchScalarGridSpec(
            num_scalar_prefetch=1, grid=(nb,),
            in_specs=[pl.BlockSpec((4, br, c), lambda i, p: (0, i, 0)), pl.BlockSpec((4, br, c), lambda i, p: (0, i, 0))],
            out_specs=pl.BlockSpec((br, c), lambda i, p: (p[0] * nb + i, 0))),
        compiler_params=_cparams(("parallel",)))(place, gathered, mine)


def _adamw(w, g, m, v, name, br):
    n, r, c = w.shape

    def body(w_ref, g_ref, m_ref, v_ref, d_ref, m2_ref, v2_ref):
        d_ref[...], m2_ref[...], v2_ref[...] = _adam_math(w_ref[...], g_ref[...], m_ref[...], v_ref[...])

    spec = pl.BlockSpec((1, br, c), lambda i, j: (i, j, 0))
    shp = jax.ShapeDtypeStruct(w.shape, F32)
    return pl.pallas_call(body, grid=(n, r // br), name=name, in_specs=[spec] * 4, out_specs=[spec] * 3,
                          out_shape=[shp] * 3, compiler_params=_cparams(("parallel", "parallel")))(w, g, m, v)


def _adamw_w_in(w, g, m, v, name, bc=31):
    cols = w.shape[2]
    lead = lambda a: jnp.transpose(a, (2, 0, 1))
    g = jnp.stack([a[:, 0:cols] for a in g])

    def body(w_ref, g_ref, m_ref, v_ref, go_ref, d_ref, m2_ref, v2_ref):
        for l in range(2):
            gv = g_ref[:, l, :]
            d_ref[:, l, :], m2_ref[:, l, :], v2_ref[:, l, :] = _adam_math(w_ref[:, l, :], gv, m_ref[:, l, :], v_ref[:, l, :])
            go_ref[:, l, :] = gv

    spec = pl.BlockSpec((bc, 2, D), lambda i: (i, 0, 0))
    outs = pl.pallas_call(body, grid=(cols // bc,), name=name, in_specs=[spec] * 4, out_specs=[spec] * 4,
                          out_shape=[jax.ShapeDtypeStruct((cols, 2, D), F32)] * 4,
                          compiler_params=_cparams(("parallel",)))(lead(w), lead(g), lead(m), lead(v))
    return [jnp.transpose(o, (1, 2, 0)) for o in outs]


_SMALL_NAMES = ("norm_w", "conv_a_w", "gla_gate_w", "gla_gate_b", "gla_norm_w", "pool_w", "pool_scale", "ssd_conv_w",
                "ssd_conv_b", "ssd_dt_bias", "ssd_a_log", "ssd_d", "ssd_norm_w", "final_norm_w")
SMALL_ROWS = 72


def _adam_math(w, g, m, v):
    m2 = ADAM_B1 * m + (1.0 - ADAM_B1) * g
    v2 = ADAM_B2 * v + (1.0 - ADAM_B2) * (g * g)
    m_hat = m2 / (1.0 - ADAM_B1 ** ADAM_STEP)
    v_hat = v2 / (1.0 - ADAM_B2 ** ADAM_STEP)
    return -ADAM_LR * (m_hat / (jnp.sqrt(v_hat) + ADAM_EPS) + ADAM_WD * w), m2, v2


def _small_slices(name, chip):
    if name == "conv_a_w":
        return [((), slice(R_CAW, R_CAW + 3), slice(64 * chip, 64 * chip + 64))]
    if name == "ssd_conv_w":
        return [((), slice(R_SCW, R_SCW + 4), slice(192 * chip, 192 * chip + 192))]
    if name == "gla_gate_w":
        return [((), slice(0, 16), slice(768, 896))]
    if name == "pool_w":
        return [((g, slice(16 * q, 16 * q + 16)), slice(16, 32), slice(256 * q + 64 * g, 256 * q + 64 * g + 64))
                for g in range(4) for q in range(4)]
    row, lanes = {"gla_gate_b": (R_GB, slice(0, 128)), "gla_norm_w": (R_GNW, slice(0, 64)),
                  "pool_scale": (R_PSC, slice(0, 256)), "ssd_conv_b": (R_SCB, slice(0, 768)),
                  "ssd_dt_bias": (R_DTB, slice(16, 20)), "ssd_a_log": (R_AE, slice(0, 4)), "ssd_d": (R_DE, slice(0, 4)),
                  "ssd_norm_w": (R_SNW, slice(0, 256))}[name]
    return [((), slice(row, row + 1), lanes)]


def _small_allreduce(sg0, sg1, dnw0, dnw1, head):
    def body(sg0_ref, sg1_ref, dnw0_ref, dnw1_ref, head_ref, acc, stage, pair, rbuf, send_sems, recv_sems):
        x, y, c = _place()
        chip = 2 * x + y
        stage[0:32, :] = sg0_ref[...]
        stage[32:64, :] = sg1_ref[...]
        stage[64:65, :] = dnw0_ref[0:1, :]
        stage[65:66, :] = dnw1_ref[0:1, :]
        stage[66:68, :] = head_ref[0:2, :]
        stage[68:72, :] = jnp.zeros((4, D), F32)
        sib = _remote(stage, pair, send_sems, recv_sems, 0, (x, y, 1 - c))
        sib.start()
        sib.wait()
        rbuf[0] = stage[...] + pair[...]
        sends = [_remote(rbuf.at[0], rbuf.at[k], send_sems, recv_sems, k, (px, py, c))
                 for k, (px, py) in enumerate(_other_chips(x, y), start=1)]
        for cp in sends:
            cp.start()
        for cp in sends:
            cp.wait()
        slab = lambda d: jnp.where(d == 0, 0, jnp.where(d == 2, 1, jnp.where(d == 1, 2, 3)))
        total = rbuf[slab(jnp.bitwise_xor(chip, 0))]
        for s in range(1, 4):
            total = total + rbuf[slab(jnp.bitwise_xor(chip, s))]
        acc[...] = total

    vmem = pl.BlockSpec(memory_space=pltpu.VMEM)
    return pl.pallas_call(
        body, name="small_allreduce", in_specs=[vmem] * 5, out_specs=vmem,
        out_shape=jax.ShapeDtypeStruct((SMALL_ROWS, D), F32),
        scratch_shapes=[pltpu.VMEM((SMALL_ROWS, D), F32), pltpu.VMEM((SMALL_ROWS, D), F32),
                        pltpu.VMEM((4, SMALL_ROWS, D), F32), _sem(4), _sem(4)],
    )(sg0, sg1, dnw0, dnw1, head)


def _small_adamw(acc, w, m, v):
    n = len(_SMALL_NAMES)

    def body(*refs):
        acc = refs[0]
        w_refs, m_refs, v_refs = refs[1:1 + n], refs[1 + n:1 + 2 * n], refs[1 + 2 * n:1 + 3 * n]
        o = 1 + 3 * n
        g_out, d_out, m_out, v_out = refs[o:o + n], refs[o + n:o + 2 * n], refs[o + 2 * n:o + 3 * n], refs[o + 3 * n:o + 4 * n]
        loss_ref = refs[o + 4 * n]
        chip = 2 * lax.axis_index("x") + lax.axis_index("y")
        loss_ref[...] = acc[67:68, 0:1]

        def update(i, idx, g):
            d, m2, v2 = _adam_math(w_refs[i][idx], g, m_refs[i][idx], v_refs[i][idx])
            g_out[i][idx], d_out[i][idx], m_out[i][idx], v_out[i][idx] = g, d, m2, v2

        for i, name in enumerate(_SMALL_NAMES):
            if name == "final_norm_w":
                update(i, (slice(0, 1), slice(None)), acc[66:67, :])
            elif name == "norm_w":
                for l in range(2):
                    update(i, (slice(l, l + 1), slice(None)), acc[64 + l:65 + l, :])
            elif name in ("conv_a_w", "ssd_conv_w"):
                for s in range(4):
                    @pl.when(chip == s)
                    def _(i=i, name=name, s=s):
                        for l in range(2):
                            (_, rows, lanes), = _small_slices(name, s)
                            update(i, (l,), acc[rows.start + 32 * l:rows.stop + 32 * l, lanes])
            else:
                for l in range(2):
                    for idx, rows, lanes in _small_slices(name, 0):
                        g = acc[rows.start + 32 * l:rows.stop + 32 * l, lanes]
                        if w_refs[i].ndim == 2:
                            update(i, (slice(l, l + 1), slice(None)), g)
                        else:
                            update(i, (l,) + idx, g)

    args = [acc] + [d[k] for d in (w, m, v) for k in _SMALL_NAMES]
    shapes = [jax.ShapeDtypeStruct(w[k].shape, F32) for k in _SMALL_NAMES]
    vmem = pl.BlockSpec(memory_space=pltpu.VMEM)
    outs = pl.pallas_call(body, name="small_adamw", in_specs=[vmem] * len(args), out_specs=[vmem] * (4 * n + 1),
                          out_shape=shapes * 4 + [jax.ShapeDtypeStruct((1, 1), F32)])(*args)
    return outs[0:n], outs[n:2 * n], outs[2 * n:3 * n], outs[3 * n:4 * n], outs[4 * n]


def _mixer_consts(layer, conv_a_w, gla_gate_w, gla_gate_b, gla_norm_w, pool_w, pool_scale, ssd_conv_w, ssd_conv_b,
                  ssd_dt_bias, ssd_a_log, ssd_d, ssd_norm_w):
    def row(v):
        return jnp.pad(v.reshape(1, -1), ((0, 0), (0, 768 - v.size)))

    dtb = jnp.zeros((128,), F32).at[16:20].set(ssd_dt_bias[layer])
    rows = [jnp.pad(conv_a_w[layer], ((0, 0), (0, 512))), row(gla_gate_b[layer]), row(jnp.tile(gla_norm_w[layer], 4)),
            row(pool_scale[layer]), row(ssd_conv_b[layer]), row(dtb), row(jnp.repeat(-jnp.exp(ssd_a_log[layer]), 64)),
            row(jnp.repeat(ssd_d[layer], 64)), row(ssd_norm_w[layer]), jnp.zeros((1, 768), F32), ssd_conv_w[layer]]
    prm = jnp.concatenate(rows, axis=0)
    gw = jnp.zeros((128, 128), F32).at[0:16].set(gla_gate_w[layer]).astype(BF16)
    pw = jnp.zeros((256, 256), F32)
    for g in range(4):
        pw = pw.at[64 * g:64 * g + 64, 64 * g:64 * g + 64].set(pool_w[layer, g])
    return (prm, gw, pw.astype(BF16)) + _mixer_matrices()


def _grad_slabs(layer, dwp, dwo):
    return _split_dw_in(dwp, name=f"split_dw_in{layer}"), dwo.reshape(4, D // 4, D)


class _Comm:
    def __init__(self, w_in, w_out):
        self.w_in16 = jnp.pad(w_in.astype(BF16), ((0, 0), (0, 0), (0, SHARD_PAD - SHARD)))
        self.w_out16 = w_out.astype(BF16)
        self.core = lax.axis_index("c").astype(jnp.int32).reshape(1)
        self.chip = 2 * lax.axis_index("x") + lax.axis_index("y")
        self.place = jnp.stack([lax.axis_index("c"), self.chip]).astype(jnp.int32)

    def gather_ici(self, layer, extra=None):
        return _rider_gather_ici((self.w_in16[layer], self.w_out16[layer]), extra)

    def pair_sum(self, layer, slabs, received):
        return [_pair_sum(self.core, a, b, name=f"reduce_pair_sum{layer}_{k}") for k, (a, b) in enumerate(zip(slabs, received))]

    def chip_sum(self, layer, gathered, mine):
        return [_chip_sum(self.place, a, b, name=f"reduce_chip_sum{layer}_{k}") for k, (a, b) in enumerate(zip(gathered, mine))]

    def layer_weights(self, layer, s_in, s_out):
        own = lambda slabs, shard: jnp.stack([jnp.where(self.chip == s, shard, slabs[s]) for s in range(4)])
        wp, wpt = _assemble_w_in(own(s_in, self.w_in16[layer]), name=f"assemble_w_in{layer}")
        wo = own(s_out, self.w_out16[layer]).reshape(D, D)
        return wp, wpt, wo, wo.T


def _local_step(x, tgt, norm_w, final_norm_w, consts, wts0, wts1=None, comm=None):
    nw = [norm_w[l:l + 1] for l in range(2)]
    proj0, h0, slabs = _rmsproj(x, nw[0], wts0[0], name="rmsproj0", rider=comm and comm.gather_ici(1))
    (mix0, sg0, ss0, x1), slabs = _mixer_fwd(proj0, x, wts0[2], *consts[0], name="mixer_fwd0",
                                             rider=comm and _rider_gather_d2d(slabs))
    if comm:
        wts1 = comm.layer_weights(1, *slabs)
    proj1, h1, _ = _rmsproj(x1, nw[1], wts1[0], name="rmsproj1")
    (mix1, sg1, ss1, x2), _ = _mixer_fwd(proj1, x1, wts1[2], *consts[1], name="mixer_fwd1")
    dx, head = _head(x2, tgt, final_norm_w.reshape(1, D), name="loss_head")
    (dproj, mgr1, dwo1), _ = _mixer_bwd(proj1, dx, wts1[3], mix1, sg1, ss1, *consts[1], name="mixer_bwd1")
    dwp1, _ = _dwin(h1, dproj, name="dwin1")
    slabs1 = comm and _grad_slabs(1, dwp1, dwo1)
    (dx, dnw1), recv = _dxin(dproj, wts1[1], x1, dx, nw[1], name="dxin1", rider=comm and _rider_swap(slabs1))
    pairs1 = comm and comm.pair_sum(1, slabs1, recv)
    (dproj, mgr0, dwo0), gathered = _mixer_bwd(proj0, dx, wts0[3], mix0, sg0, ss0, *consts[0], name="mixer_bwd0",
                                               rider=comm and _rider_scatter(pairs1))
    dwp0, big1 = _dwin(h0, dproj, name="dwin0", rider=comm and _rider_share(comm.chip_sum(1, gathered, pairs1)))
    scat = None
    if comm:
        slabs0 = _grad_slabs(0, dwp0, dwo0)
        pairs0 = comm.pair_sum(0, slabs0, _run_rider(_rider_swap(slabs0), "reduce_swap0"))
        scat = _rider_scatter(pairs0)
    (dx, dnw0), gathered = _dxin(dproj, wts0[1], x, dx, nw[0], name="dxin0", rider=scat)
    if comm:
        big0 = _run_rider(_rider_share(comm.chip_sum(0, gathered, pairs0)), "reduce_share0")
        big = ((big0[0], big1[0]), (big0[1], big1[1]))
    else:
        big = ((dwp0, dwp1), (dwo0, dwo1))
    return head, dx, big, (dnw0, dnw1), (mgr0, mgr1)


def kernel(x, norm_w, w_in, conv_a_w, gla_gate_w, gla_gate_b, gla_norm_w, pool_w, pool_scale, ssd_conv_w, ssd_conv_b, ssd_dt_bias, ssd_a_log, ssd_d, ssd_norm_w, w_out, final_norm_w, loss_target, m_norm_w, m_w_in, m_conv_a_w, m_gla_gate_w, m_gla_gate_b, m_gla_norm_w, m_pool_w, m_pool_scale, m_ssd_conv_w, m_ssd_conv_b, m_ssd_dt_bias, m_ssd_a_log, m_ssd_d, m_ssd_norm_w, m_w_out, m_final_norm_w, v_norm_w, v_w_in, v_conv_a_w, v_gla_gate_w, v_gla_gate_b, v_gla_norm_w, v_pool_w, v_pool_scale, v_ssd_conv_w, v_ssd_conv_b, v_ssd_dt_bias, v_ssd_a_log, v_ssd_d, v_ssd_norm_w, v_w_out, v_final_norm_w):
    weights = dict(norm_w=norm_w, w_in=w_in, conv_a_w=conv_a_w, gla_gate_w=gla_gate_w, gla_gate_b=gla_gate_b,
                   gla_norm_w=gla_norm_w, pool_w=pool_w, pool_scale=pool_scale, ssd_conv_w=ssd_conv_w,
                   ssd_conv_b=ssd_conv_b, ssd_dt_bias=ssd_dt_bias, ssd_a_log=ssd_a_log, ssd_d=ssd_d,
                   ssd_norm_w=ssd_norm_w, w_out=w_out, final_norm_w=final_norm_w)
    m_in = dict(norm_w=m_norm_w, w_in=m_w_in, conv_a_w=m_conv_a_w, gla_gate_w=m_gla_gate_w, gla_gate_b=m_gla_gate_b,
                gla_norm_w=m_gla_norm_w, pool_w=m_pool_w, pool_scale=m_pool_scale, ssd_conv_w=m_ssd_conv_w,
                ssd_conv_b=m_ssd_conv_b, ssd_dt_bias=m_ssd_dt_bias, ssd_a_log=m_ssd_a_log, ssd_d=m_ssd_d,
                ssd_norm_w=m_ssd_norm_w, w_out=m_w_out, final_norm_w=m_final_norm_w)
    v_in = dict(norm_w=v_norm_w, w_in=v_w_in, conv_a_w=v_conv_a_w, gla_gate_w=v_gla_gate_w, gla_gate_b=v_gla_gate_b,
                gla_norm_w=v_gla_norm_w, pool_w=v_pool_w, pool_scale=v_pool_scale, ssd_conv_w=v_ssd_conv_w,
                ssd_conv_b=v_ssd_conv_b, ssd_dt_bias=v_ssd_dt_bias, ssd_a_log=v_ssd_a_log, ssd_d=v_ssd_d,
                ssd_norm_w=v_ssd_norm_w, w_out=v_w_out, final_norm_w=v_final_norm_w)
    order = ("norm_w", "w_in", "conv_a_w", "gla_gate_w", "gla_gate_b", "gla_norm_w", "pool_w", "pool_scale",
             "ssd_conv_w", "ssd_conv_b", "ssd_dt_bias", "ssd_a_log", "ssd_d", "ssd_norm_w", "w_out", "final_norm_w")
    t = x.shape[1]

    comm = _Comm(w_in, w_out)
    cshard = jnp.zeros((16, 256), F32)
    for l in range(2):
        cshard = cshard.at[8 * l:8 * l + 3, 0:64].set(conv_a_w[l]).at[8 * l + 3:8 * l + 7, 0:192].set(ssd_conv_w[l])
    s_in, s_out, g_c = _run_rider(comm.gather_ici(0, cshard), "gather_ici0")
    s_in, s_out = _run_rider(_rider_gather_d2d((s_in, s_out)), "gather_d2d0")
    g_c = [jnp.where(comm.chip == s, cshard, g_c[s]) for s in range(4)]
    conv_a_full = jnp.stack([jnp.concatenate([g_c[s][8 * l:8 * l + 3, 0:64] for s in range(4)], axis=-1) for l in range(2)])
    ssd_conv_full = jnp.stack([jnp.concatenate([g_c[s][8 * l + 3:8 * l + 7, 0:192] for s in range(4)], axis=-1)
                               for l in range(2)])
    consts = [_mixer_consts(l, conv_a_full, gla_gate_w, gla_gate_b, gla_norm_w, pool_w, pool_scale, ssd_conv_full,
                            ssd_conv_b, ssd_dt_bias, ssd_a_log, ssd_d, ssd_norm_w) for l in range(2)]

    head, dx, big, dnw, mgr = _local_step(x.reshape(t, D), loss_target.reshape(t, D), norm_w, final_norm_w, consts,
                                          comm.layer_weights(0, s_in, s_out), comm=comm)

    as2d = lambda d: {k: (d[k].reshape(1, D) if k == "final_norm_w" else d[k]) for k in _SMALL_NAMES}
    small = _small_adamw(_small_allreduce(mgr[0], mgr[1], dnw[0], dnw[1], head), as2d(weights), as2d(m_in), as2d(v_in))
    grads, delta, new_m, new_v = ({k: (a.reshape(D) if k == "final_norm_w" else a) for k, a in zip(_SMALL_NAMES, part)}
                                  for part in small[0:4])
    loss = small[4].reshape(())

    grads["w_out"] = jnp.stack(big[1])

    grads["w_in"], delta["w_in"], new_m["w_in"], new_v["w_in"] = _adamw_w_in(w_in, big[0], m_w_in, v_w_in, name="adamw_w_in")
    delta["w_out"], new_m["w_out"], new_v["w_out"] = _adamw(w_out, grads["w_out"], m_w_out, v_w_out, name="adamw_w_out", br=256)

    return (loss, dx.reshape(1, t, D), *[grads[k] for k in order], *[delta[k] for k in order],
            *[new_m[k] for k in order], *[new_v[k] for k in order])
```

```python
import functools

import jax
import jax.numpy as jnp
from jax import lax
from jax.experimental import pallas as pl
from jax.experimental.pallas import tpu as pltpu

F32 = jnp.float32
BF16 = jnp.bfloat16
MESH = pl.DeviceIdType.MESH

D = 1024
CH = 64
EPS = 1e-6
NP = 3456
NPROJ = 3348
NPM = 3328
GLA_SCALE = 32.0 ** -0.5
INV_TAU = 1.0 / 16.0
TB = 512
NCH = TB // CH
assert TB % 256 == 0

C_AH, C_AB, C_AC, C_AZ, C_GQ, C_GK, C_GV = 0, 256, 512, 768, 1024, 1152, 1280
C_GZ, C_PU, C_PZ, C_SZ, C_SX, C_TL = 1536, 1792, 2048, 2304, 2560, 3328
_PERM = ((0, 1536), (1552, 1792), (1536, 16), (3344, 4))
_UNPERM = ((0, 1536), (3328, 16), (1536, 1792), (3344, 4))

R_CAW, R_GB, R_GNW, R_PSC, R_SCB, R_DTB, R_AE, R_DE, R_SNW, R_SCW = 0, 3, 4, 5, 6, 7, 8, 9, 10, 12

ADAM_LR, ADAM_B1, ADAM_B2, ADAM_EPS, ADAM_WD, ADAM_STEP = 0.001, 0.9, 0.999, 1e-08, 0.01, 10

VMEM_LIMIT = 56 * 1024 * 1024


def _cparams(sem, limit=VMEM_LIMIT):
    return pltpu.CompilerParams(dimension_semantics=sem, vmem_limit_bytes=limit)


_ANY = pl.BlockSpec(memory_space=pl.ANY)


def _place():
    return lax.axis_index("x"), lax.axis_index("y"), lax.axis_index("c")


class _Rider:
    def __init__(self, inputs, out_shapes, sems, start, finish, aliases=None):
        self.inputs, self.out_shapes, self.sems = tuple(inputs), tuple(out_shapes), tuple(sems)
        self.start, self.finish, self.aliases = start, finish, dict(aliases or {})


def _call(body, args, *, grid, in_specs, out_specs, out_shape, name, sem, scratch_shapes=(), rider=None):
    if rider is None:
        outs = pl.pallas_call(body, grid=grid, name=name, in_specs=list(in_specs), out_specs=list(out_specs),
                              out_shape=list(out_shape), scratch_shapes=list(scratch_shapes),
                              compiler_params=_cparams(sem))(*args)
        return list(outs), []
    ni, no, ns = len(args), len(out_shape), len(scratch_shapes)
    ri, ro = len(rider.inputs), len(rider.out_shapes)

    def full(*refs):
        ins, rins = refs[:ni], refs[ni:ni + ri]
        outs, routs = refs[ni + ri:ni + ri + no], refs[ni + ri + no:ni + ri + no + ro]
        scr, rsem = refs[ni + ri + no + ro:ni + ri + no + ro + ns], refs[ni + ri + no + ro + ns:]
        first = functools.reduce(jnp.logical_and, [pl.program_id(a) == 0 for a in range(len(grid))])
        last = functools.reduce(jnp.logical_and, [pl.program_id(a) == grid[a] - 1 for a in range(len(grid))])

        @pl.when(first)
        def _():
            rider.start(rins, routs, rsem)

        body(*ins, *outs, *scr)

        @pl.when(last)
        def _():
            rider.finish(rins, routs, rsem)

    outs = pl.pallas_call(
        full, grid=grid, name=name, in_specs=list(in_specs) + [_ANY] * ri, out_specs=list(out_specs) + [_ANY] * ro,
        out_shape=list(out_shape) + list(rider.out_shapes), scratch_shapes=list(scratch_shapes) + list(rider.sems),
        input_output_aliases={ni + k: no + v for k, v in rider.aliases.items()},
        compiler_params=_cparams(("arbitrary",) * len(grid)))(*args, *rider.inputs)
    return list(outs[:no]), list(outs[no:])


def _run_rider(rider, name):
    ri = len(rider.inputs)

    def body(*refs):
        rins, routs, rsem = refs[:ri], refs[ri:ri + len(rider.out_shapes)], refs[ri + len(rider.out_shapes):]
        rider.start(rins, routs, rsem)
        rider.finish(rins, routs, rsem)

    return list(pl.pallas_call(body, name=name, in_specs=[_ANY] * ri, out_specs=[_ANY] * len(rider.out_shapes),
                               out_shape=list(rider.out_shapes), scratch_shapes=list(rider.sems),
                               input_output_aliases=dict(rider.aliases))(*rider.inputs))


def _dot(a, b):
    return jnp.dot(a.astype(BF16), b.astype(BF16), preferred_element_type=F32)


def _dot_nt(a, b):
    return lax.dot_general(a.astype(BF16), b.astype(BF16), (((1,), (1,)), ((), ())), preferred_element_type=F32)


def _dot_tn(a, b):
    return lax.dot_general(a.astype(BF16), b.astype(BF16), (((0,), (0,)), ((), ())), preferred_element_type=F32)


def _split(a):
    hi = a.astype(BF16)
    lo = (a - hi.astype(F32)).astype(BF16)
    return hi, lo


def _dot2_l(a, b):
    hi, lo = _split(a)
    return _dot(hi, b) + _dot(lo, b)


def _dot2_r(a, b):
    hi, lo = _split(b)
    return _dot(a, hi) + _dot(a, lo)


def _dot3_l(a, b):
    hi, lo = _split(a)
    lo2 = ((a - hi.astype(F32)) - lo.astype(F32)).astype(BF16)
    return _dot(hi, b) + _dot(lo, b) + _dot(lo2, b)


def _dot2_nt(a, b):
    hi, lo = _split(a)
    return _dot_nt(hi, b) + _dot_nt(lo, b)


def _silu(z):
    return z * jax.nn.sigmoid(z)


def _lse1(x):
    return jnp.log(1.0 + jnp.exp(-jnp.abs(x)))


def _cs(a):
    return jnp.sum(a, axis=0, keepdims=True)


def _iota(shape, dim):
    return lax.broadcasted_iota(jnp.int32, shape, dim)


def _mixer_matrices():
    r, c = _iota((256, 256), 0), _iota((256, 256), 1)
    same_chunk = (r >> 6) == (c >> 6)
    mats = jnp.stack([jnp.where((c > r) & same_chunk, 1.0, 0.0), jnp.where((c < r) & same_chunk, 1.0, 0.0),
                      jnp.where(same_chunk, 1.0 / 64.0, 0.0), jnp.where((r < 128) & (r - 16 == (c >> 6)), 1.0, 0.0)])
    mask = jnp.where((_iota((256, 128), 0) >> 6) == (_iota((256, 128), 1) >> 5), 1.0, 0.0)
    return mats.astype(BF16), mask.astype(F32)


def _dn(ext, k, n, h):
    return pltpu.roll(ext, k, axis=0)[h:h + n]


def _up(ext, k, n):
    return pltpu.roll(ext, ext.shape[0] - k, axis=0)[:n]


def _pool_lane_select(lane, s2, s4, s8, s16):
    return jnp.where(lane < 64, s2, jnp.where(lane < 128, s4, jnp.where(lane < 192, s8, s16)))


def _winsum_dn(ext, lane):
    s2 = ext + pltpu.roll(ext, 1, axis=0)
    s4 = s2 + pltpu.roll(s2, 2, axis=0)
    s8 = s4 + pltpu.roll(s4, 4, axis=0)
    s16 = s8 + pltpu.roll(s8, 8, axis=0)
    return _pool_lane_select(lane, s2, s4, s8, s16)


def _winsum_up(ext, lane):
    m = ext.shape[0]
    s2 = ext + pltpu.roll(ext, m - 1, axis=0)
    s4 = s2 + pltpu.roll(s2, m - 2, axis=0)
    s8 = s4 + pltpu.roll(s4, m - 4, axis=0)
    s16 = s8 + pltpu.roll(s8, m - 8, axis=0)
    return _pool_lane_select(lane, s2, s4, s8, s16)


def _pool_inv_count(tile, n):
    lane = _iota((1, 256), 1)
    win = _pool_lane_select(lane, 2.0, 4.0, 8.0, 16.0).astype(F32)
    tpos = (tile * n + _iota((n, 1), 0) + 1).astype(F32)
    return jnp.where(tpos >= win, 1.0 / win, 1.0 / tpos)


def _silu_pair(z):
    s = jax.nn.sigmoid(z)
    return z * s, s * (1.0 + z * (1.0 - s))


def _chunks(a):
    return [a[c * CH:(c + 1) * CH] for c in range(a.shape[0] // CH)]


def _halves(fn, a, b):
    return jnp.concatenate([fn(a[:, 0:128], b[:, 0:128]), fn(a[:, 128:256], b[:, 128:256])], axis=1)


def _chunk_sums(tri, a):
    return jnp.concatenate([_dot2_r(tri, a[r:r + 256]) for r in range(0, a.shape[0], 256)], axis=0)


def _mixer_tile_prep(p_ref, t_ref, xc, prm_ref, gw_v, cm_ref, mk_ref):
    tail = t_ref[...]
    pre = _dot(tail, gw_v) + prm_ref[R_GB:R_GB + 1, 0:128]
    la = (jnp.minimum(pre, 0.0) - _lse1(pre)) * INV_TAU
    dtin = tail + prm_ref[R_DTB:R_DTB + 1, 0:128]
    dtf = jnp.maximum(dtin, 0.0) + _lse1(dtin)
    dte = _dot2_l(dtf, cm_ref[3, 0:128, :])
    da = dte * prm_ref[R_AE:R_AE + 1, 0:256]
    rev = _chunk_sums(cm_ref[0], jnp.concatenate([la, da], axis=1))
    dec = jnp.exp(rev[:, 0:128])
    kd = p_ref[:, C_GK:C_GK + 128].astype(F32) * dec
    wdec = jnp.exp(rev[:, 128:384])
    w = wdec * dte
    xw = xc[:, 0:256] * w
    d_s = [jnp.exp(_cs(a)) for a in _chunks(la)]
    et = [jnp.exp(_cs(a)) for a in _chunks(da)]
    mask_t = mk_ref[...]
    ut_g = [_dot_tn(v, k) * mask_t for v, k in zip(_chunks(p_ref[:, C_GV:C_GV + 256].astype(F32)), _chunks(kd))]
    ut_s = [_halves(_dot_tn, b, x) for b, x in zip(_chunks(xc[:, 256:512]), _chunks(xw))]
    return tail, pre, dtin, dte, dec, kd, wdec, w, xw, d_s, et, ut_g, ut_s


def _rmsproj(x, nw, wp, name, tm=512, rider=None):
    t = x.shape[0]

    def body(x_ref, nw_ref, w_ref, o_ref, t_ref, h_ref):
        xv = x_ref[...]
        rs = lax.rsqrt(jnp.mean(xv * xv, axis=-1, keepdims=True) + EPS)
        h = (xv * rs * nw_ref[...]).astype(BF16)
        h_ref[...] = h
        proj = jnp.dot(h, w_ref[...], preferred_element_type=F32)
        o_ref[...] = proj[:, 0:NPM].astype(BF16)
        t_ref[...] = proj[:, NPM:NP]

    (proj, tail, h), extra = _call(
        body, (x, nw, wp), grid=(t // tm,), name=name, sem=("parallel",), rider=rider,
        in_specs=[pl.BlockSpec((tm, D), lambda i: (i, 0)), pl.BlockSpec((1, D), lambda i: (0, 0)),
                  pl.BlockSpec((D, NP), lambda i: (0, 0))],
        out_specs=[pl.BlockSpec((tm, NPM), lambda i: (i, 0)), pl.BlockSpec((tm, NP - NPM), lambda i: (i, 0)),
                   pl.BlockSpec((tm, D), lambda i: (i, 0))],
        out_shape=[jax.ShapeDtypeStruct((t, NPM), BF16), jax.ShapeDtypeStruct((t, NP - NPM), F32),
                   jax.ShapeDtypeStruct((t, D), BF16)])
    return (proj, tail), h, extra


def _head_tile(xv, tgt, w):
    rs = lax.rsqrt(jnp.mean(xv * xv, axis=-1, keepdims=True) + EPS)
    xh = xv * rs
    err = xh * w - tgt
    dy = err * (1.0 / D)
    dxh = dy * w
    dx = rs * (dxh - xh * jnp.mean(dxh * xh, axis=-1, keepdims=True))
    return dx, _cs(dy * xh), (0.5 / D) * jnp.sum(err * err)


def _dxin(dp, wpt, x, dxn, nw, name, tm=512, rider=None):
    t = x.shape[0]

    def body(dp_ref, w_ref, x_ref, dxn_ref, nw_ref, dx_ref, dnw_ref):
        @pl.when(pl.program_id(0) == 0)
        def _():
            dnw_ref[...] = jnp.zeros_like(dnw_ref)

        acc = jnp.zeros((1, D), F32)
        for rows in (pl.ds(0, tm // 2), pl.ds(tm // 2, tm // 2)):
            dh = jnp.dot(dp_ref[rows, :].astype(BF16), w_ref[...], preferred_element_type=F32)
            xv = x_ref[rows, :]
            rs = lax.rsqrt(jnp.mean(xv * xv, axis=-1, keepdims=True) + EPS)
            xh = xv * rs
            acc = acc + _cs(dh * xh)
            dxh = dh * nw_ref[...]
            dx_ref[rows, :] = dxn_ref[rows, :] + rs * (dxh - xh * jnp.mean(dxh * xh, axis=-1, keepdims=True))
        dnw_ref[0:1, :] += acc

    return _call(
        body, (dp, wpt, x, dxn, nw), grid=(t // tm,), name=name, sem=("arbitrary",), rider=rider,
        in_specs=[pl.BlockSpec((tm, NP), lambda i: (i, 0)), pl.BlockSpec((NP, D), lambda i: (0, 0)),
                  pl.BlockSpec((tm, D), lambda i: (i, 0)), pl.BlockSpec((tm, D), lambda i: (i, 0)),
                  pl.BlockSpec((1, D), lambda i: (0, 0))],
        out_specs=[pl.BlockSpec((tm, D), lambda i: (i, 0)), pl.BlockSpec((8, D), lambda i: (0, 0))],
        out_shape=[jax.ShapeDtypeStruct((t, D), F32), jax.ShapeDtypeStruct((8, D), F32)])


def _dwin(h, dp, name, tm=1024, tn=NP, rider=None):
    t = h.shape[0]

    def body(h_ref, dp_ref, o_ref):
        @pl.when(pl.program_id(1) == 0)
        def _():
            o_ref[...] = jnp.zeros_like(o_ref)

        o_ref[...] += _dot_tn(h_ref[...], dp_ref[...])

    (dwp,), extra = _call(
        body, (h, dp), grid=(NP // tn, t // tm), name=name, sem=("parallel", "arbitrary"), rider=rider,
        in_specs=[pl.BlockSpec((tm, D), lambda j, i: (i, 0)), pl.BlockSpec((tm, tn), lambda j, i: (i, j))],
        out_specs=[pl.BlockSpec((D, tn), lambda j, i: (0, j))], out_shape=[jax.ShapeDtypeStruct((D, NP), F32)])
    return dwp, extra


def _mixer_fwd(proj, x, wo, prm, gw, pw, cmat, mask, name, rider=None, head=None):
    proj, tail = proj
    t = proj.shape[0]
    nt, nc = t // TB, t // CH

    def body(p_ref, t_ref, x_ref, wo_ref, prm_ref, gw_ref, pw_ref, cm_ref, mk_ref, *rest):
        (tgt_ref, fw_ref), rest = (rest[:2], rest[2:]) if head else ((None, None), rest)
        mix_ref, sg_ref, ss_ref, xn_ref = rest[:4]
        acc_ref = rest[4] if head else None
        sg_s, ss_s, h_ua, h_pu, h_sx = rest[-5:]
        i = pl.program_id(0)

        @pl.when(i == 0)
        def _():
            for r in (sg_s, ss_s, h_ua, h_pu, h_sx) + ((acc_ref,) if head else ()):
                r[...] = jnp.zeros_like(r)

        lane = _iota((1, 256), 1)
        u = p_ref[:, C_AC:C_AC + 256].astype(F32) * p_ref[:, C_AH:C_AH + 256].astype(F32)
        ext = jnp.concatenate([h_ua[...], u], axis=0)
        cv = (prm_ref[R_CAW + 2:R_CAW + 3, 0:256] * u + prm_ref[R_CAW + 1:R_CAW + 2, 0:256] * _dn(ext, 1, TB, 8)
              + prm_ref[R_CAW:R_CAW + 1, 0:256] * _dn(ext, 2, TB, 8))
        mix_ref[:, 0:256] = (p_ref[:, C_AB:C_AB + 256].astype(F32) * cv * _silu(p_ref[:, C_AZ:C_AZ + 256].astype(F32))).astype(BF16)
        h_ua[...] = u[TB - 8:, :]
        pu = p_ref[:, C_PU:C_PU + 256].astype(F32)
        ext = jnp.concatenate([h_pu[...], pu], axis=0)
        pooled = _winsum_dn(ext, lane)[16:] * _pool_inv_count(i, TB) - pu
        mixed = _dot(pooled, pw_ref[...])
        mix_ref[:, 512:768] = (prm_ref[R_PSC:R_PSC + 1, 0:256] * mixed * _silu(p_ref[:, C_PZ:C_PZ + 256].astype(F32))).astype(BF16)
        h_pu[...] = pu[TB - 16:, :]
        sx = p_ref[:, C_SX:C_SX + 768].astype(F32)
        ext = jnp.concatenate([h_sx[...], sx], axis=0)
        xc = _silu(prm_ref[R_SCW + 3:R_SCW + 4, :] * sx + prm_ref[R_SCW + 2:R_SCW + 3, :] * _dn(ext, 1, TB, 8)
                   + prm_ref[R_SCW + 1:R_SCW + 2, :] * _dn(ext, 2, TB, 8) + prm_ref[R_SCW:R_SCW + 1, :] * _dn(ext, 3, TB, 8)
                   + prm_ref[R_SCB:R_SCB + 1, :])
        h_sx[...] = sx[TB - 8:, :]

        _, _, _, _, _, _, _, _, _, d_s, et, ut_g, ut_s = _mixer_tile_prep(p_ref, t_ref, xc, prm_ref, gw_ref[...], cm_ref, mk_ref)
        s_g, s_s = sg_s[...], ss_s[...]
        o, y = [], []
        qs = _chunks(p_ref[:, C_GQ:C_GQ + 128].astype(F32) * GLA_SCALE)
        cm = _chunks(xc[:, 512:768])
        for c in range(NCH):
            sg_ref[c] = s_g
            ss_ref[c] = s_s
            s_g = s_g * d_s[c] + ut_g[c]
            s_s = s_s * et[c] + ut_s[c]
            o.append(_dot_nt(qs[c], s_g))
            y.append(_halves(_dot, cm[c], s_s))
        sg_s[...] = s_g
        ss_s[...] = s_s
        o = jnp.concatenate(o, axis=0)
        on = o * lax.rsqrt(_dot2_l(o * o, cm_ref[2]) + EPS)
        mix_ref[:, 256:512] = (on * prm_ref[R_GNW:R_GNW + 1, 0:256] * _silu(p_ref[:, C_GZ:C_GZ + 256].astype(F32))).astype(BF16)
        y2 = ((jnp.concatenate(y, axis=0) + prm_ref[R_DE:R_DE + 1, 0:256] * xc[:, 0:256])
              * _silu(p_ref[:, C_SZ:C_SZ + 256].astype(F32)))
        mix_ref[:, 768:1024] = (y2 * lax.rsqrt(jnp.mean(y2 * y2, axis=-1, keepdims=True) + EPS)
                                * prm_ref[R_SNW:R_SNW + 1, 0:256]).astype(BF16)
        xn = x_ref[...] + jnp.dot(mix_ref[...], wo_ref[...], preferred_element_type=F32)
        if head:
            xn_ref[...], dfw, loss = _head_tile(xn, tgt_ref[...], fw_ref[...])
            acc_ref[0:1, :] += dfw
            acc_ref[1:2, :] += jnp.zeros((1, D), F32) + loss
        else:
            xn_ref[...] = xn

    row = pl.BlockSpec((TB, D), lambda i: (i, 0))
    return _call(
        body, (proj, tail, x, wo, prm, gw, pw, cmat, mask) + tuple(head or ()), grid=(nt,), name=name, sem=("arbitrary",),
        rider=rider,
        in_specs=[pl.BlockSpec((TB, NPM), lambda i: (i, 0)), pl.BlockSpec((TB, NP - NPM), lambda i: (i, 0)), row,
                  pl.BlockSpec((D, D), lambda i: (0, 0)), pl.BlockSpec((16, 768), lambda i: (0, 0)),
                  pl.BlockSpec((128, 128), lambda i: (0, 0)), pl.BlockSpec((256, 256), lambda i: (0, 0)),
                  pl.BlockSpec((4, 256, 256), lambda i: (0, 0, 0)), pl.BlockSpec((256, 128), lambda i: (0, 0))]
        + ([row, pl.BlockSpec((1, D), lambda i: (0, 0))] if head else []),
        out_specs=[row, pl.BlockSpec((NCH, 256, 128), lambda i: (i, 0, 0)),
                   pl.BlockSpec((NCH, 128, 256), lambda i: (i, 0, 0)), row]
        + ([pl.BlockSpec((8, D), lambda i: (0, 0))] if head else []),
        out_shape=[jax.ShapeDtypeStruct((t, D), BF16), jax.ShapeDtypeStruct((nc, 256, 128), F32),
                   jax.ShapeDtypeStruct((nc, 128, 256), F32), jax.ShapeDtypeStruct((t, D), F32)]
        + ([jax.ShapeDtypeStruct((8, D), F32)] if head else []),
        scratch_shapes=[pltpu.VMEM((256, 128), F32), pltpu.VMEM((128, 256), F32), pltpu.VMEM((8, 256), F32),
                        pltpu.VMEM((16, 256), F32), pltpu.VMEM((8, 768), F32)])


def _mixer_bwd(proj, dxn, wot, mix, sg, ss, prm, gw, pw, cmat, mask, name, rider=None):
    proj, tail = proj
    t = proj.shape[0]
    nt = t // TB
    rev = lambda i: nt - 1 - i

    def body(p_ref, hp_ref, t_ref, dxn_ref, wot_ref, mix_ref, sg_ref, ss_ref, prm_ref, gw_ref, pw_ref, cm_ref, mk_ref,
             dp_ref, sgc_ref, dwo_ref,
             gg_s, gs_s, h_dcv, h_dpl, h_dpre, gsm_ref, dgw_ref, dpw_ref, dm_ref):
        i = pl.program_id(0)
        tile = nt - 1 - i

        @pl.when(i == 0)
        def _():
            for r in (gg_s, gs_s, h_dcv, h_dpl, h_dpre, gsm_ref, dgw_ref, dpw_ref, dwo_ref):
                r[...] = jnp.zeros_like(r)

        dxn = dxn_ref[...].astype(BF16)
        dm_ref[...] = jnp.dot(dxn, wot_ref[...], preferred_element_type=F32)
        dwo_ref[...] += _dot_tn(mix_ref[...], dxn)

        lane = _iota((1, 256), 1)
        first = (tile > 0).astype(F32)
        ah, ac = p_ref[:, C_AH:C_AH + 256].astype(F32), p_ref[:, C_AC:C_AC + 256].astype(F32)
        ab, az = p_ref[:, C_AB:C_AB + 256].astype(F32), p_ref[:, C_AZ:C_AZ + 256].astype(F32)
        w0, w1, w2 = (prm_ref[R_CAW + j:R_CAW + j + 1, 0:256] for j in range(3))
        u = ac * ah
        ext = jnp.concatenate([(hp_ref[:, C_AC:C_AC + 256].astype(F32) * hp_ref[:, C_AH:C_AH + 256].astype(F32))[8:16] * first, u], axis=0)
        u1, u2 = _dn(ext, 1, TB, 8), _dn(ext, 2, TB, 8)
        cv = w2 * u + w1 * u1 + w0 * u2
        g = dm_ref[:, 0:256]
        sz, dsz = _silu_pair(az)
        dp_ref[:, C_AB:C_AB + 256] = (g * cv * sz).astype(BF16)
        dp_ref[:, C_AZ:C_AZ + 256] = (g * ab * cv * dsz).astype(BF16)
        dcv = g * ab * sz
        dext = jnp.concatenate([dcv, h_dcv[...]], axis=0)
        du = w2 * dcv + w1 * _up(dext, 1, TB) + w0 * _up(dext, 2, TB)
        dp_ref[:, C_AC:C_AC + 256] = (du * ah).astype(BF16)
        dp_ref[:, C_AH:C_AH + 256] = (du * ac).astype(BF16)
        gsm_ref[R_CAW:R_CAW + 1, 0:256] += _cs(dcv * u2)
        gsm_ref[R_CAW + 1:R_CAW + 2, 0:256] += _cs(dcv * u1)
        gsm_ref[R_CAW + 2:R_CAW + 3, 0:256] += _cs(dcv * u)
        h_dcv[...] = dcv[0:8, :]
        pu, pz = p_ref[:, C_PU:C_PU + 256].astype(F32), p_ref[:, C_PZ:C_PZ + 256].astype(F32)
        psc = prm_ref[R_PSC:R_PSC + 1, 0:256]
        icnt = _pool_inv_count(tile, TB)
        ext = jnp.concatenate([hp_ref[:, C_PU:C_PU + 256].astype(F32) * first, pu], axis=0)
        pooled = _winsum_dn(ext, lane)[16:] * icnt - pu
        pw_v = pw_ref[...]
        mixed = _dot(pooled, pw_v)
        g = dm_ref[:, 512:768]
        sz, dsz = _silu_pair(pz)
        gsm_ref[R_PSC:R_PSC + 1, 0:256] += _cs(g * mixed * sz)
        dp_ref[:, C_PZ:C_PZ + 256] = (g * psc * mixed * dsz).astype(BF16)
        dmixed = g * psc * sz
        dpw_ref[...] += _dot_tn(pooled, dmixed)
        dpooled = _dot_nt(dmixed, pw_v)
        qd = dpooled * icnt
        dext = jnp.concatenate([qd, h_dpl[...]], axis=0)
        dp_ref[:, C_PU:C_PU + 256] = (_winsum_up(dext, lane)[:TB] - dpooled).astype(BF16)
        h_dpl[...] = qd[0:16, :]
        sx = p_ref[:, C_SX:C_SX + 768].astype(F32)
        cw = [prm_ref[R_SCW + j:R_SCW + j + 1, :] for j in range(4)]
        ext = jnp.concatenate([hp_ref[:, C_SX:C_SX + 768].astype(F32)[8:16] * first, sx], axis=0)
        sx1, sx2, sx3 = _dn(ext, 1, TB, 8), _dn(ext, 2, TB, 8), _dn(ext, 3, TB, 8)
        cpre = cw[3] * sx + cw[2] * sx1 + cw[1] * sx2 + cw[0] * sx3 + prm_ref[R_SCB:R_SCB + 1, :]
        xc, dxc = _silu_pair(cpre)
        xs, bm, cm = xc[:, 0:256], xc[:, 256:512], xc[:, 512:768]

        gw_v = gw_ref[...]
        tail, pre, dtin, dte, dec, kd, wdec, w, xw, d_s, et, ut_g, ut_s = _mixer_tile_prep(p_ref, t_ref, xc, prm_ref,
                                                                                          gw_v, cm_ref, mk_ref)
        gmean = cm_ref[2]
        mask_t = mk_ref[...]
        gnw = prm_ref[R_GNW:R_GNW + 1, 0:256]
        a_e = prm_ref[R_AE:R_AE + 1, 0:256]
        d_e = prm_ref[R_DE:R_DE + 1, 0:256]
        snw = prm_ref[R_SNW:R_SNW + 1, 0:256]
        sg_in = [sg_ref[c] for c in range(NCH)]
        ss_in = [ss_ref[c] for c in range(NCH)]
        sg_n = [sg_in[c] * d_s[c] + ut_g[c] for c in range(NCH)]
        ss_n = [ss_in[c] * et[c] + ut_s[c] for c in range(NCH)]
        qs = _chunks(p_ref[:, C_GQ:C_GQ + 128].astype(F32) * GLA_SCALE)
        cm_c, bm_c, xw_c, kd_c = _chunks(cm), _chunks(bm), _chunks(xw), _chunks(kd)
        v_c = _chunks(p_ref[:, C_GV:C_GV + 256].astype(F32))
        o = jnp.concatenate([_dot_nt(qs[c], sg_n[c]) for c in range(NCH)], axis=0)
        y = jnp.concatenate([_halves(_dot, cm_c[c], ss_n[c]) for c in range(NCH)], axis=0) + d_e * xs
        gz = p_ref[:, C_GZ:C_GZ + 256].astype(F32)
        r = lax.rsqrt(_dot2_l(o * o, gmean) + EPS)
        on = o * r
        dyb = dm_ref[:, 256:512]
        sz, dsz = _silu_pair(gz)
        dp_ref[:, C_GZ:C_GZ + 256] = (dyb * on * gnw * dsz).astype(BF16)
        tg = dyb * sz
        gsm_ref[R_GNW:R_GNW + 1, 0:256] += _cs(tg * on)
        don = tg * gnw
        do_c = _chunks(r * (don - on * _dot2_l(don * on, gmean)))
        ssz = p_ref[:, C_SZ:C_SZ + 256].astype(F32)
        sil, dsil = _silu_pair(ssz)
        y2 = y * sil
        r = lax.rsqrt(jnp.mean(y2 * y2, axis=-1, keepdims=True) + EPS)
        yn = y2 * r
        dyd = dm_ref[:, 768:1024]
        gsm_ref[R_SNW:R_SNW + 1, 0:256] += _cs(dyd * yn)
        dn = dyd * snw
        dy2 = r * (dn - yn * jnp.mean(dn * yn, axis=-1, keepdims=True))
        dp_ref[:, C_SZ:C_SZ + 256] = (dy2 * y * dsil).astype(BF16)
        dy = dy2 * sil
        gsm_ref[R_DE:R_DE + 1, 0:256] += _cs(dy * xs)
        dy_c = _chunks(dy)
        dq = jnp.concatenate([_dot(do_c[c], sg_n[c]) for c in range(NCH)], axis=0)
        dp_ref[:, C_GQ:C_GQ + 128] = (dq * GLA_SCALE).astype(BF16)
        dcm = jnp.concatenate([_halves(_dot_nt, dy_c[c], ss_n[c]) for c in range(NCH)], axis=0)
        gg = [_dot_tn(do_c[c], qs[c]) * mask_t for c in range(NCH)]
        gs = [_halves(_dot_tn, cm_c[c], dy_c[c]) for c in range(NCH)]
        car_g, car_s = gg_s[...], gs_s[...]
        for c in reversed(range(NCH)):
            gg[c] = gg[c] + car_g
            gs[c] = gs[c] + car_s
            car_g = gg[c] * d_s[c]
            car_s = gs[c] * et[c]
        gg_s[...] = car_g
        gs_s[...] = car_s
        dkd = jnp.concatenate([_dot(v_c[c], gg[c]) for c in range(NCH)], axis=0)
        dp_ref[:, C_GV:C_GV + 256] = jnp.concatenate([_dot_nt(kd_c[c], gg[c]) for c in range(NCH)], axis=0).astype(BF16)
        dp_ref[:, C_GK:C_GK + 128] = (dkd * dec).astype(BF16)
        dbm = jnp.concatenate([_halves(_dot_nt, xw_c[c], gs[c]) for c in range(NCH)], axis=0)
        dxw = jnp.concatenate([_halves(_dot, bm_c[c], gs[c]) for c in range(NCH)], axis=0)
        dxs = dy * d_e + dxw * w
        dw = dxw * xs
        dsuf = _chunk_sums(cm_ref[1], jnp.concatenate([dkd * kd, dw * dte * wdec], axis=1))
        tot_g = jnp.concatenate([jnp.broadcast_to(_cs(gg[c] * sg_in[c]) * d_s[c], (CH, 128)) for c in range(NCH)], axis=0)
        tot_s = jnp.concatenate([jnp.broadcast_to(_cs(gs[c] * ss_in[c]) * et[c], (CH, 256)) for c in range(NCH)], axis=0)
        dpre = (dsuf[:, 0:128] + tot_g) * INV_TAU * jax.nn.sigmoid(-pre)
        dgw_ref[...] += _dot_tn(tail, dpre)
        gsm_ref[R_GB:R_GB + 1, 0:128] += _cs(dpre)
        dda = dsuf[:, 128:384] + tot_s
        gsm_ref[R_AE:R_AE + 1, 0:256] += _cs(dda * dte)
        dtail_s = _dot2_nt(dw * wdec + dda * a_e, cm_ref[3, 0:128, :]) * jax.nn.sigmoid(dtin)
        gsm_ref[R_DTB:R_DTB + 1, 0:128] += _cs(dtail_s)
        dp_ref[:, C_TL:C_TL + 128] = (_dot_nt(dpre, gw_v) + dtail_s).astype(BF16)
        dpre_c = jnp.concatenate([dxs, dbm, dcm], axis=1) * dxc
        dext = jnp.concatenate([dpre_c, h_dpre[...]], axis=0)
        dp_ref[:, C_SX:C_SX + 768] = (cw[3] * dpre_c + cw[2] * _up(dext, 1, TB) + cw[1] * _up(dext, 2, TB)
                                      + cw[0] * _up(dext, 3, TB)).astype(BF16)
        gsm_ref[R_SCW + 3:R_SCW + 4, :] += _cs(dpre_c * sx)
        gsm_ref[R_SCW + 2:R_SCW + 3, :] += _cs(dpre_c * sx1)
        gsm_ref[R_SCW + 1:R_SCW + 2, :] += _cs(dpre_c * sx2)
        gsm_ref[R_SCW:R_SCW + 1, :] += _cs(dpre_c * sx3)
        gsm_ref[R_SCB:R_SCB + 1, :] += _cs(dpre_c)
        h_dpre[...] = dpre_c[0:8, :]

        @pl.when(i == nt - 1)
        def _():
            ri, ci = _iota((256, 256), 0), _iota((256, 256), 1)
            per_head = jnp.where((ri >> 6) == ci, 1.0, 0.0).astype(BF16)
            per_dv = jnp.where((ri & 63) == ci, 1.0, 0.0).astype(BF16)
            row = _iota((8, 256), 0)
            top = gsm_ref[0:8, 0:256]
            sgc_ref[0:8, 0:256] = jnp.where(row == R_GNW, _dot3_l(top, per_dv), top)
            bot = gsm_ref[8:16, 0:256]
            fold = _dot3_l(jnp.where(row == R_AE - 8, bot * a_e, bot), per_head)
            sgc_ref[8:16, 0:256] = jnp.where((row == R_AE - 8) | (row == R_DE - 8), fold, bot)
            sgc_ref[0:16, 256:768] = gsm_ref[:, 256:768]
            sgc_ref[0:16, 768:896] = dgw_ref[0:16, :]
            sgc_ref[0:16, 896:1024] = jnp.zeros((16, 128), F32)
            diag = _pool_lane_select(lane, dpw_ref[0:64, :], dpw_ref[64:128, :], dpw_ref[128:192, :], dpw_ref[192:256, :])
            for q in range(4):
                sgc_ref[16:32, 256 * q:256 * q + 256] = diag[16 * q:16 * q + 16, :]

    return _call(
        body, (proj, proj, tail, dxn, wot, mix, sg, ss, prm, gw, pw, cmat, mask), grid=(nt,), name=name,
        sem=("arbitrary",), rider=rider,
        in_specs=[pl.BlockSpec((TB, NPM), lambda i: (rev(i), 0)),
                  pl.BlockSpec((16, NPM), lambda i: (jnp.maximum(rev(i) * (TB // 16) - 1, 0), 0)),
                  pl.BlockSpec((TB, NP - NPM), lambda i: (rev(i), 0)),
                  pl.BlockSpec((TB, D), lambda i: (rev(i), 0)), pl.BlockSpec((D, D), lambda i: (0, 0)),
                  pl.BlockSpec((TB, D), lambda i: (rev(i), 0)),
                  pl.BlockSpec((NCH, 256, 128), lambda i: (rev(i), 0, 0)),
                  pl.BlockSpec((NCH, 128, 256), lambda i: (rev(i), 0, 0)),
                  pl.BlockSpec((16, 768), lambda i: (0, 0)), pl.BlockSpec((128, 128), lambda i: (0, 0)),
                  pl.BlockSpec((256, 256), lambda i: (0, 0)), pl.BlockSpec((4, 256, 256), lambda i: (0, 0, 0)),
                  pl.BlockSpec((256, 128), lambda i: (0, 0))],
        out_specs=[pl.BlockSpec((TB, NP), lambda i: (rev(i), 0)), pl.BlockSpec((32, 1024), lambda i: (0, 0)),
                   pl.BlockSpec((D, D), lambda i: (0, 0))],
        out_shape=[jax.ShapeDtypeStruct((t, NP), BF16), jax.ShapeDtypeStruct((32, 1024), F32),
                   jax.ShapeDtypeStruct((D, D), F32)],
        scratch_shapes=[pltpu.VMEM((256, 128), F32), pltpu.VMEM((128, 256), F32), pltpu.VMEM((8, 256), F32),
                        pltpu.VMEM((16, 256), F32), pltpu.VMEM((8, 768), F32), pltpu.VMEM((16, 768), F32),
                        pltpu.VMEM((128, 128), F32), pltpu.VMEM((256, 256), F32), pltpu.VMEM((TB, D), F32)])


SHARD = NPROJ // 4
SHARD_PAD = 896


def _ranges_to_perm(o, n):
    out, p = [], 0
    for start, size in _PERM:
        a, b = max(o, start), min(o + n, start + size)
        if a < b:
            out.append((a, b - a, p + a - start))
        p += size
    return out


def _ranges_to_orig(p0, n):
    out, p = [], 0
    for start, size in _PERM:
        a, b = max(p0, p), min(p0 + n, p + size)
        if a < b:
            out.append((a, b - a, start + a - p))
        p += size
    return out


def _lane_window(load, lo, n, d, lane):
    a = 128 * (lo // 128)
    off = lo - a
    w = 128 if off + n <= 128 else 256
    chunk = load(a, w)
    shift = (d - off) % w
    if shift:
        chunk = pltpu.roll(chunk, shift, axis=1)
    return jnp.where((lane >= d) & (lane < d + n), chunk[:, 0:128], 0.0)


def _assemble_w_in(slabs, name, rb=256):
    def body(s_ref, wp_ref, wpt_ref):
        lane = _iota((1, 128), 1)
        for b in range(NP // 128):
            acc = jnp.zeros((rb, 128), F32)
            for p, n, o in _ranges_to_orig(128 * b, 128):
                while n > 0:
                    s, lo = o // SHARD, o % SHARD
                    cnt = min(n, SHARD - lo)
                    acc = acc + _lane_window(lambda a, w, s=s: s_ref[s, :, a:a + w].astype(F32), lo, cnt, p - 128 * b, lane)
                    o, p, n = o + cnt, p + cnt, n - cnt
            wp_ref[:, 128 * b:128 * b + 128] = acc.astype(BF16)
            wpt_ref[128 * b:128 * b + 128, :] = acc.T.astype(BF16)

    return pl.pallas_call(
        body, grid=(D // rb,), name=name,
        in_specs=[pl.BlockSpec((4, rb, SHARD_PAD), lambda i: (0, i, 0))],
        out_specs=[pl.BlockSpec((rb, NP), lambda i: (i, 0)), pl.BlockSpec((NP, rb), lambda i: (0, i))],
        out_shape=[jax.ShapeDtypeStruct((D, NP), BF16), jax.ShapeDtypeStruct((NP, D), BF16)],
        compiler_params=_cparams(("parallel",)))(slabs)


def _split_dw_in(dwp, name, rb=256):
    def body(g_ref, o_ref):
        lane = _iota((1, 128), 1)
        for s in range(4):
            for k in range(SHARD_PAD // 128):
                acc = jnp.zeros((rb, 128), F32)
                n_valid = min(128, SHARD - 128 * k)
                for o, n, p in _ranges_to_perm(SHARD * s + 128 * k, n_valid):
                    acc = acc + _lane_window(lambda a, w: g_ref[:, a:a + w], p, n, o - SHARD * s - 128 * k, lane)
                o_ref[s, :, 128 * k:128 * k + 128] = acc

    return pl.pallas_call(
        body, grid=(D // rb,), name=name,
        in_specs=[pl.BlockSpec((rb, NP), lambda i: (i, 0))],
        out_specs=pl.BlockSpec((4, rb, SHARD_PAD), lambda i: (0, i, 0)),
        out_shape=jax.ShapeDtypeStruct((4, D, SHARD_PAD), F32),
        compiler_params=_cparams(("parallel",)))(dwp)


def _half(c, n):
    return pl.ds(pl.multiple_of(c * (n // 2), n // 2), n // 2)


def _other_chips(x, y):
    return ((1 - x, y), (x, 1 - y), (1 - x, 1 - y))


def _remote(src, dst, send, recv, k, dev):
    return pltpu.make_async_remote_copy(src_ref=src, dst_ref=dst, send_sem=send.at[k], recv_sem=recv.at[k], device_id=dev,
                                        device_id_type=MESH)


def _sem(n):
    return pltpu.SemaphoreType.DMA((n,))


def _rider_gather_ici(shards, extra=None):
    shards = tuple(shards) + ((extra,) if extra is not None else ())
    n = len(shards)

    def copies(rins, routs, sems, arrivals=True):
        send, recv = sems
        x, y, c = _place()
        me = 2 * x + y
        out, inc = [], []
        for j, (px, py) in enumerate(_other_chips(x, y)):
            for k in range(n):
                whole = extra is not None and k == n - 1
                rows = pl.ds(0, shards[k].shape[0]) if whole else _half(c, shards[k].shape[0])
                out.append(_remote(rins[k].at[rows], routs[k].at[me, rows], send, recv, n * j + k, (px, py, c)))
                if arrivals:
                    inc.append(_remote(rins[k].at[rows], routs[k].at[2 * px + py, rows], send, recv, n * j + k, (px, py, c)))
        return out, inc

    def start(rins, routs, sems):
        for cp in copies(rins, routs, sems, arrivals=False)[0]:
            cp.start()

    def finish(rins, routs, sems):
        out, inc = copies(rins, routs, sems)
        for cp in inc:
            cp.wait_recv()
        for cp in out:
            cp.wait_send()

    return _Rider(shards, [jax.ShapeDtypeStruct((4,) + a.shape, a.dtype) for a in shards], [_sem(3 * n), _sem(3 * n)],
                  start, finish)


def _rider_gather_d2d(slabs):
    slabs = tuple(slabs)
    n = len(slabs)

    def copies(routs, sems, arrivals=True):
        send, recv = sems
        x, y, c = _place()
        out, inc = [], []
        for j, (px, py) in enumerate(_other_chips(x, y)):
            for k in range(n):
                rows = slabs[k].shape[1]
                mine, theirs = routs[k].at[2 * px + py, _half(c, rows)], routs[k].at[2 * px + py, _half(1 - c, rows)]
                out.append(_remote(mine, mine, send, recv, n * j + k, (x, y, 1 - c)))
                if arrivals:
                    inc.append(_remote(theirs, theirs, send, recv, n * j + k, (x, y, 1 - c)))
        return out, inc

    def start(rins, routs, sems):
        for cp in copies(routs, sems, arrivals=False)[0]:
            cp.start()

    def finish(rins, routs, sems):
        out, inc = copies(routs, sems)
        for cp in inc:
            cp.wait_recv()
        for cp in out:
            cp.wait_send()

    return _Rider(slabs, [jax.ShapeDtypeStruct(a.shape, a.dtype) for a in slabs], [_sem(3 * n), _sem(3 * n)], start, finish,
                  aliases={k: k for k in range(n)})


def _rider_swap(parts):
    parts = tuple(parts)
    n = len(parts)

    def copies(rins, routs, sems):
        send, recv = sems
        x, y, c = _place()
        return [_remote(rins[k].at[:, _half(1 - c, parts[k].shape[1])], routs[k], send, recv, k, (x, y, 1 - c))
                for k in range(n)]

    def start(rins, routs, sems):
        for cp in copies(rins, routs, sems):
            cp.start()

    def finish(rins, routs, sems):
        for cp in copies(rins, routs, sems):
            cp.wait()

    return _Rider(parts, [jax.ShapeDtypeStruct((4, a.shape[1] // 2, a.shape[2]), a.dtype) for a in parts],
                  [_sem(n), _sem(n)], start, finish)


def _rider_scatter(parts):
    parts = tuple(parts)
    n = len(parts)

    def copies(rins, routs, sems, arrivals=True):
        send, recv = sems
        x, y, c = _place()
        me = 2 * x + y
        out, inc = [], []
        for j, (px, py) in enumerate(_other_chips(x, y)):
            for k in range(n):
                out.append(_remote(rins[k].at[2 * px + py], routs[k].at[me], send, recv, n * j + k, (px, py, c)))
                if arrivals:
                    inc.append(_remote(rins[k].at[me], routs[k].at[2 * px + py], send, recv, n * j + k, (px, py, c)))
        return out, inc

    def start(rins, routs, sems):
        for cp in copies(rins, routs, sems, arrivals=False)[0]:
            cp.start()

    def finish(rins, routs, sems):
        out, inc = copies(rins, routs, sems)
        for cp in inc:
            cp.wait_recv()
        for cp in out:
            cp.wait_send()

    return _Rider(parts, [jax.ShapeDtypeStruct(a.shape, a.dtype) for a in parts], [_sem(3 * n), _sem(3 * n)], start, finish)


def _rider_share(fulls):
    fulls = tuple(fulls)
    n = len(fulls)

    def copies(routs, sems, arrivals=True):
        send, recv = sems
        x, y, c = _place()
        out, inc = [], []
        for k in range(n):
            mine, theirs = routs[k].at[_half(c, fulls[k].shape[0])], routs[k].at[_half(1 - c, fulls[k].shape[0])]
            out.append(_remote(mine, mine, send, recv, k, (x, y, 1 - c)))
            if arrivals:
                inc.append(_remote(theirs, theirs, send, recv, k, (x, y, 1 - c)))
        return out, inc

    def start(rins, routs, sems):
        for cp in copies(routs, sems, arrivals=False)[0]:
            cp.start()

    def finish(rins, routs, sems):
        out, inc = copies(routs, sems)
        for cp in inc:
            cp.wait_recv()
        for cp in out:
            cp.wait_send()

    return _Rider(fulls, [jax.ShapeDtypeStruct(a.shape, a.dtype) for a in fulls], [_sem(n), _sem(n)], start, finish,
                  aliases={k: k for k in range(n)})


def _pair_sum(core, full, recv, name, br=128):
    n, rows, cols = recv.shape

    def body(c_ref, a_ref, b_ref, o_ref):
        o_ref[...] = (a_ref[...] + b_ref[...]).astype(BF16)

    nb = rows // br
    return pl.pallas_call(
        body, name=name, out_shape=jax.ShapeDtypeStruct(recv.shape, BF16),
        grid_spec=pltpu.PrefetchScalarGridSpec(
            num_scalar_prefetch=1, grid=(n, nb),
            in_specs=[pl.BlockSpec((1, br, cols), lambda i, j, c: (i, c[0] * nb + j, 0)),
                      pl.BlockSpec((1, br, cols), lambda i, j, c: (i, j, 0))],
            out_specs=pl.BlockSpec((1, br, cols), lambda i, j, c: (i, j, 0))),
        compiler_params=_cparams(("parallel", "parallel")))(core, full, recv)


def _chip_sum(place, gathered, mine, name, br=128):
    _, r, c = gathered.shape
    nb = r // br

    def body(p_ref, g_ref, m_ref, o_ref):
        slab = lambda j: jnp.where(p_ref[1] == j, m_ref[j], g_ref[j]).astype(F32)
        o_ref[...] = ((slab(0) + slab(1)) + slab(2)) + slab(3)

    return pl.pallas_call(
        body, name=name, out_shape=jax.ShapeDtypeStruct((2 * r, c), F32),
        grid_spec=pltpu.PrefetchScalarGridSpec(
            num_scalar_prefetch=1, grid=(nb,),
            in_specs=[pl.BlockSpec((4, br, c), lambda i, p: (0, i, 0)), pl.BlockSpec((4, br, c), lambda i, p: (0, i, 0))],
            out_specs=pl.BlockSpec((br, c), lambda i, p: (p[0] * nb + i, 0))),
        compiler_params=_cparams(("parallel",)))(place, gathered, mine)


def _adamw(w, g, m, v, name, br):
    n, r, c = w.shape

    def body(w_ref, g_ref, m_ref, v_ref, d_ref, m2_ref, v2_ref):
        d_ref[...], m2_ref[...], v2_ref[...] = _adam_math(w_ref[...], g_ref[...], m_ref[...], v_ref[...])

    spec = pl.BlockSpec((1, br, c), lambda i, j: (i, j, 0))
    shp = jax.ShapeDtypeStruct(w.shape, F32)
    return pl.pallas_call(body, grid=(n, r // br), name=name, in_specs=[spec] * 4, out_specs=[spec] * 3,
                          out_shape=[shp] * 3, compiler_params=_cparams(("parallel", "parallel")))(w, g, m, v)


def _adamw_w_in(w, g, m, v, name, bc=31):
    cols = w.shape[2]
    lead = lambda a: jnp.transpose(a, (2, 0, 1))
    g = jnp.stack([a[:, 0:cols] for a in g])

    def body(w_ref, g_ref, m_ref, v_ref, go_ref, d_ref, m2_ref, v2_ref):
        for l in range(2):
            gv = g_ref[:, l, :]
            d_ref[:, l, :], m2_ref[:, l, :], v2_ref[:, l, :] = _adam_math(w_ref[:, l, :], gv, m_ref[:, l, :], v_ref[:, l, :])
            go_ref[:, l, :] = gv

    spec = pl.BlockSpec((bc, 2, D), lambda i: (i, 0, 0))
    outs = pl.pallas_call(body, grid=(cols // bc,), name=name, in_specs=[spec] * 4, out_specs=[spec] * 4,
                          out_shape=[jax.ShapeDtypeStruct((cols, 2, D), F32)] * 4,
                          compiler_params=_cparams(("parallel",)))(lead(w), lead(g), lead(m), lead(v))
    return [jnp.transpose(o, (1, 2, 0)) for o in outs]


_SMALL_NAMES = ("norm_w", "conv_a_w", "gla_gate_w", "gla_gate_b", "gla_norm_w", "pool_w", "pool_scale", "ssd_conv_w",
                "ssd_conv_b", "ssd_dt_bias", "ssd_a_log", "ssd_d", "ssd_norm_w", "final_norm_w")
SMALL_ROWS = 72


def _adam_math(w, g, m, v):
    m2 = ADAM_B1 * m + (1.0 - ADAM_B1) * g
    v2 = ADAM_B2 * v + (1.0 - ADAM_B2) * (g * g)
    m_hat = m2 / (1.0 - ADAM_B1 ** ADAM_STEP)
    v_hat = v2 / (1.0 - ADAM_B2 ** ADAM_STEP)
    return -ADAM_LR * (m_hat / (jnp.sqrt(v_hat) + ADAM_EPS) + ADAM_WD * w), m2, v2


def _small_slices(name, chip):
    if name == "conv_a_w":
        return [((), slice(R_CAW, R_CAW + 3), slice(64 * chip, 64 * chip + 64))]
    if name == "ssd_conv_w":
        return [((), slice(R_SCW, R_SCW + 4), slice(192 * chip, 192 * chip + 192))]
    if name == "gla_gate_w":
        return [((), slice(0, 16), slice(768, 896))]
    if name == "pool_w":
        return [((g, slice(16 * q, 16 * q + 16)), slice(16, 32), slice(256 * q + 64 * g, 256 * q + 64 * g + 64))
                for g in range(4) for q in range(4)]
    row, lanes = {"gla_gate_b": (R_GB, slice(0, 128)), "gla_norm_w": (R_GNW, slice(0, 64)),
                  "pool_scale": (R_PSC, slice(0, 256)), "ssd_conv_b": (R_SCB, slice(0, 768)),
                  "ssd_dt_bias": (R_DTB, slice(16, 20)), "ssd_a_log": (R_AE, slice(0, 4)), "ssd_d": (R_DE, slice(0, 4)),
                  "ssd_norm_w": (R_SNW, slice(0, 256))}[name]
    return [((), slice(row, row + 1), lanes)]


def _small_allreduce(sg0, sg1, dnw0, dnw1, head):
    def body(sg0_ref, sg1_ref, dnw0_ref, dnw1_ref, head_ref, acc, stage, pair, rbuf, send_sems, recv_sems):
        x, y, c = _place()
        chip = 2 * x + y
        stage[0:32, :] = sg0_ref[...]
        stage[32:64, :] = sg1_ref[...]
        stage[64:65, :] = dnw0_ref[0:1, :]
        stage[65:66, :] = dnw1_ref[0:1, :]
        stage[66:68, :] = head_ref[0:2, :]
        stage[68:72, :] = jnp.zeros((4, D), F32)
        sib = _remote(stage, pair, send_sems, recv_sems, 0, (x, y, 1 - c))
        sib.start()
        sib.wait()
        rbuf[0] = stage[...] + pair[...]
        sends = [_remote(rbuf.at[0], rbuf.at[k], send_sems, recv_sems, k, (px, py, c))
                 for k, (px, py) in enumerate(_other_chips(x, y), start=1)]
        for cp in sends:
            cp.start()
        for cp in sends:
            cp.wait()
        slab = lambda d: jnp.where(d == 0, 0, jnp.where(d == 2, 1, jnp.where(d == 1, 2, 3)))
        total = rbuf[slab(jnp.bitwise_xor(chip, 0))]
        for s in range(1, 4):
            total = total + rbuf[slab(jnp.bitwise_xor(chip, s))]
        acc[...] = total

    vmem = pl.BlockSpec(memory_space=pltpu.VMEM)
    return pl.pallas_call(
        body, name="small_allreduce", in_specs=[vmem] * 5, out_specs=vmem,
        out_shape=jax.ShapeDtypeStruct((SMALL_ROWS, D), F32),
        scratch_shapes=[pltpu.VMEM((SMALL_ROWS, D), F32), pltpu.VMEM((SMALL_ROWS, D), F32),
                        pltpu.VMEM((4, SMALL_ROWS, D), F32), _sem(4), _sem(4)],
    )(sg0, sg1, dnw0, dnw1, head)


def _small_adamw(acc, w, m, v):
    n = len(_SMALL_NAMES)

    def body(*refs):
        acc = refs[0]
        w_refs, m_refs, v_refs = refs[1:1 + n], refs[1 + n:1 + 2 * n], refs[1 + 2 * n:1 + 3 * n]
        o = 1 + 3 * n
        g_out, d_out, m_out, v_out = refs[o:o + n], refs[o + n:o + 2 * n], refs[o + 2 * n:o + 3 * n], refs[o + 3 * n:o + 4 * n]
        loss_ref = refs[o + 4 * n]
        chip = 2 * lax.axis_index("x") + lax.axis_index("y")
        loss_ref[...] = acc[67:68, 0:1]

        def update(i, idx, g):
            d, m2, v2 = _adam_math(w_refs[i][idx], g, m_refs[i][idx], v_refs[i][idx])
            g_out[i][idx], d_out[i][idx], m_out[i][idx], v_out[i][idx] = g, d, m2, v2

        for i, name in enumerate(_SMALL_NAMES):
            if name == "final_norm_w":
                update(i, (slice(0, 1), slice(None)), acc[66:67, :])
            elif name == "norm_w":
                for l in range(2):
                    update(i, (slice(l, l + 1), slice(None)), acc[64 + l:65 + l, :])
            elif name in ("conv_a_w", "ssd_conv_w"):
                for s in range(4):
                    @pl.when(chip == s)
                    def _(i=i, name=name, s=s):
                        for l in range(2):
                            (_, rows, lanes), = _small_slices(name, s)
                            update(i, (l,), acc[rows.start + 32 * l:rows.stop + 32 * l, lanes])
            else:
                for l in range(2):
                    for idx, rows, lanes in _small_slices(name, 0):
                        g = acc[rows.start + 32 * l:rows.stop + 32 * l, lanes]
                        if w_refs[i].ndim == 2:
                            update(i, (slice(l, l + 1), slice(None)), g)
                        else:
                            update(i, (l,) + idx, g)

    args = [acc] + [d[k] for d in (w, m, v) for k in _SMALL_NAMES]
    shapes = [jax.ShapeDtypeStruct(w[k].shape, F32) for k in _SMALL_NAMES]
    vmem = pl.BlockSpec(memory_space=pltpu.VMEM)
    outs = pl.pallas_call(body, name="small_adamw", in_specs=[vmem] * len(args), out_specs=[vmem] * (4 * n + 1),
                          out_shape=shapes * 4 + [jax.ShapeDtypeStruct((1, 1), F32)])(*args)
    return outs[0:n], outs[n:2 * n], outs[2 * n:3 * n], outs[3 * n:4 * n], outs[4 * n]


def _mixer_consts(layer, conv_a_w, gla_gate_w, gla_gate_b, gla_norm_w, pool_w, pool_scale, ssd_conv_w, ssd_conv_b,
                  ssd_dt_bias, ssd_a_log, ssd_d, ssd_norm_w):
    def row(v):
        return jnp.pad(v.reshape(1, -1), ((0, 0), (0, 768 - v.size)))

    dtb = jnp.pad(ssd_dt_bias[layer], (16, 108))
    rows = [jnp.pad(conv_a_w[layer], ((0, 0), (0, 512))), row(gla_gate_b[layer]), row(jnp.tile(gla_norm_w[layer], 4)),
            row(pool_scale[layer]), row(ssd_conv_b[layer]), row(dtb), row(jnp.repeat(-jnp.exp(ssd_a_log[layer]), 64)),
            row(jnp.repeat(ssd_d[layer], 64)), row(ssd_norm_w[layer]), jnp.zeros((1, 768), F32), ssd_conv_w[layer]]
    prm = jnp.concatenate(rows, axis=0)
    gw = jnp.pad(gla_gate_w[layer], ((0, 112), (0, 0))).astype(BF16)
    on_diag = (_iota((256, 256), 0) >> 6) == (_iota((256, 256), 1) >> 6)
    pw = jnp.where(on_diag, jnp.tile(pool_w[layer].reshape(256, 64), (1, 4)), 0.0)
    return (prm, gw, pw.astype(BF16)) + _mixer_matrices()


def _grad_slabs(layer, dwp, dwo):
    return _split_dw_in(dwp, name=f"split_dw_in{layer}"), dwo.reshape(4, D // 4, D)


class _Comm:
    def __init__(self, w_in, w_out):
        self.w_in16 = jnp.pad(w_in.astype(BF16), ((0, 0), (0, 0), (0, SHARD_PAD - SHARD)))
        self.w_out16 = w_out.astype(BF16)
        self.core = lax.axis_index("c").astype(jnp.int32).reshape(1)
        self.chip = 2 * lax.axis_index("x") + lax.axis_index("y")
        self.place = jnp.stack([lax.axis_index("c"), self.chip]).astype(jnp.int32)

    def gather_ici(self, layer, extra=None):
        return _rider_gather_ici((self.w_in16[layer], self.w_out16[layer]), extra)

    def pair_sum(self, layer, slabs, received):
        return [_pair_sum(self.core, a, b, name=f"reduce_pair_sum{layer}_{k}") for k, (a, b) in enumerate(zip(slabs, received))]

    def chip_sum(self, layer, gathered, mine):
        return [_chip_sum(self.place, a, b, name=f"reduce_chip_sum{layer}_{k}") for k, (a, b) in enumerate(zip(gathered, mine))]

    def layer_weights(self, layer, s_in, s_out):
        own = lambda slabs, shard: jnp.stack([jnp.where(self.chip == s, shard, slabs[s]) for s in range(4)])
        wp, wpt = _assemble_w_in(own(s_in, self.w_in16[layer]), name=f"assemble_w_in{layer}")
        wo = own(s_out, self.w_out16[layer]).reshape(D, D)
        return wp, wpt, wo, wo.T


def _local_step(x, tgt, norm_w, final_norm_w, consts, wts0, wts1=None, comm=None):
    nw = [norm_w[l:l + 1] for l in range(2)]
    proj0, h0, slabs = _rmsproj(x, nw[0], wts0[0], name="rmsproj0", rider=comm and comm.gather_ici(1))
    (mix0, sg0, ss0, x1), slabs = _mixer_fwd(proj0, x, wts0[2], *consts[0], name="mixer_fwd0",
                                             rider=comm and _rider_gather_d2d(slabs))
    if comm:
        wts1 = comm.layer_weights(1, *slabs)
    proj1, h1, _ = _rmsproj(x1, nw[1], wts1[0], name="rmsproj1")
    (mix1, sg1, ss1, dx, head), _ = _mixer_fwd(proj1, x1, wts1[2], *consts[1], name="mixer_fwd1",
                                               head=(tgt, final_norm_w.reshape(1, D)))
    (dproj, mgr1, dwo1), _ = _mixer_bwd(proj1, dx, wts1[3], mix1, sg1, ss1, *consts[1], name="mixer_bwd1")
    dwp1, _ = _dwin(h1, dproj, name="dwin1")
    slabs1 = comm and _grad_slabs(1, dwp1, dwo1)
    (dx, dnw1), recv = _dxin(dproj, wts1[1], x1, dx, nw[1], name="dxin1", rider=comm and _rider_swap(slabs1))
    pairs1 = comm and comm.pair_sum(1, slabs1, recv)
    (dproj, mgr0, dwo0), gathered = _mixer_bwd(proj0, dx, wts0[3], mix0, sg0, ss0, *consts[0], name="mixer_bwd0",
                                               rider=comm and _rider_scatter(pairs1))
    dwp0, big1 = _dwin(h0, dproj, name="dwin0", rider=comm and _rider_share(comm.chip_sum(1, gathered, pairs1)))
    scat = None
    if comm:
        slabs0 = _grad_slabs(0, dwp0, dwo0)
        pairs0 = comm.pair_sum(0, slabs0, _run_rider(_rider_swap(slabs0), "reduce_swap0"))
        scat = _rider_scatter(pairs0)
    (dx, dnw0), gathered = _dxin(dproj, wts0[1], x, dx, nw[0], name="dxin0", rider=scat)
    if comm:
        big0 = _run_rider(_rider_share(comm.chip_sum(0, gathered, pairs0)), "reduce_share0")
        big = ((big0[0], big1[0]), (big0[1], big1[1]))
    else:
        big = ((dwp0, dwp1), (dwo0, dwo1))
    return head, dx, big, (dnw0, dnw1), (mgr0, mgr1)


def kernel(x, norm_w, w_in, conv_a_w, gla_gate_w, gla_gate_b, gla_norm_w, pool_w, pool_scale, ssd_conv_w, ssd_conv_b, ssd_dt_bias, ssd_a_log, ssd_d, ssd_norm_w, w_out, final_norm_w, loss_target, m_norm_w, m_w_in, m_conv_a_w, m_gla_gate_w, m_gla_gate_b, m_gla_norm_w, m_pool_w, m_pool_scale, m_ssd_conv_w, m_ssd_conv_b, m_ssd_dt_bias, m_ssd_a_log, m_ssd_d, m_ssd_norm_w, m_w_out, m_final_norm_w, v_norm_w, v_w_in, v_conv_a_w, v_gla_gate_w, v_gla_gate_b, v_gla_norm_w, v_pool_w, v_pool_scale, v_ssd_conv_w, v_ssd_conv_b, v_ssd_dt_bias, v_ssd_a_log, v_ssd_d, v_ssd_norm_w, v_w_out, v_final_norm_w):
    weights = dict(norm_w=norm_w, w_in=w_in, conv_a_w=conv_a_w, gla_gate_w=gla_gate_w, gla_gate_b=gla_gate_b,
                   gla_norm_w=gla_norm_w, pool_w=pool_w, pool_scale=pool_scale, ssd_conv_w=ssd_conv_w,
                   ssd_conv_b=ssd_conv_b, ssd_dt_bias=ssd_dt_bias, ssd_a_log=ssd_a_log, ssd_d=ssd_d,
                   ssd_norm_w=ssd_norm_w, w_out=w_out, final_norm_w=final_norm_w)
    m_in = dict(norm_w=m_norm_w, w_in=m_w_in, conv_a_w=m_conv_a_w, gla_gate_w=m_gla_gate_w, gla_gate_b=m_gla_gate_b,
                gla_norm_w=m_gla_norm_w, pool_w=m_pool_w, pool_scale=m_pool_scale, ssd_conv_w=m_ssd_conv_w,
                ssd_conv_b=m_ssd_conv_b, ssd_dt_bias=m_ssd_dt_bias, ssd_a_log=m_ssd_a_log, ssd_d=m_ssd_d,
                ssd_norm_w=m_ssd_norm_w, w_out=m_w_out, final_norm_w=m_final_norm_w)
    v_in = dict(norm_w=v_norm_w, w_in=v_w_in, conv_a_w=v_conv_a_w, gla_gate_w=v_gla_gate_w, gla_gate_b=v_gla_gate_b,
                gla_norm_w=v_gla_norm_w, pool_w=v_pool_w, pool_scale=v_pool_scale, ssd_conv_w=v_ssd_conv_w,
                ssd_conv_b=v_ssd_conv_b, ssd_dt_bias=v_ssd_dt_bias, ssd_a_log=v_ssd_a_log, ssd_d=v_ssd_d,
                ssd_norm_w=v_ssd_norm_w, w_out=v_w_out, final_norm_w=v_final_norm_w)
    order = ("norm_w", "w_in", "conv_a_w", "gla_gate_w", "gla_gate_b", "gla_norm_w", "pool_w", "pool_scale",
             "ssd_conv_w", "ssd_conv_b", "ssd_dt_bias", "ssd_a_log", "ssd_d", "ssd_norm_w", "w_out", "final_norm_w")
    t = x.shape[1]

    comm = _Comm(w_in, w_out)
    cshard = jnp.zeros((16, 256), F32)
    for l in range(2):
        cshard = cshard.at[8 * l:8 * l + 3, 0:64].set(conv_a_w[l]).at[8 * l + 3:8 * l + 7, 0:192].set(ssd_conv_w[l])
    s_in, s_out, g_c = _run_rider(comm.gather_ici(0, cshard), "gather_ici0")
    s_in, s_out = _run_rider(_rider_gather_d2d((s_in, s_out)), "gather_d2d0")
    g_c = [jnp.where(comm.chip == s, cshard, g_c[s]) for s in range(4)]
    conv_a_full = jnp.stack([jnp.concatenate([g_c[s][8 * l:8 * l + 3, 0:64] for s in range(4)], axis=-1) for l in range(2)])
    ssd_conv_full = jnp.stack([jnp.concatenate([g_c[s][8 * l + 3:8 * l + 7, 0:192] for s in range(4)], axis=-1)
                               for l in range(2)])
    consts = [_mixer_consts(l, conv_a_full, gla_gate_w, gla_gate_b, gla_norm_w, pool_w, pool_scale, ssd_conv_full,
                            ssd_conv_b, ssd_dt_bias, ssd_a_log, ssd_d, ssd_norm_w) for l in range(2)]

    head, dx, big, dnw, mgr = _local_step(x.reshape(t, D), loss_target.reshape(t, D), norm_w, final_norm_w, consts,
                                          comm.layer_weights(0, s_in, s_out), comm=comm)

    as2d = lambda d: {k: (d[k].reshape(1, D) if k == "final_norm_w" else d[k]) for k in _SMALL_NAMES}
    small = _small_adamw(_small_allreduce(mgr[0], mgr[1], dnw[0], dnw[1], head), as2d(weights), as2d(m_in), as2d(v_in))
    grads, delta, new_m, new_v = ({k: (a.reshape(D) if k == "final_norm_w" else a) for k, a in zip(_SMALL_NAMES, part)}
                                  for part in small[0:4])
    loss = small[4].reshape(())

    grads["w_out"] = jnp.stack(big[1])

    grads["w_in"], delta["w_in"], new_m["w_in"], new_v["w_in"] = _adamw_w_in(w_in, big[0], m_w_in, v_w_in, name="adamw_w_in")
    delta["w_out"], new_m["w_out"], new_v["w_out"] = _adamw(w_out, grads["w_out"], m_w_out, v_w_out, name="adamw_w_out", br=256)

    return (loss, dx.reshape(1, t, D), *[grads[k] for k in order], *[delta[k] for k in order],
            *[new_m[k] for k in order], *[new_v[k] for k in order])
```

```python
import functools

import jax
import jax.numpy as jnp
from jax import lax
from jax.experimental import pallas as pl
from jax.experimental.pallas import tpu as pltpu

F32 = jnp.float32
BF16 = jnp.bfloat16
MESH = pl.DeviceIdType.MESH

D = 1024
CH = 64
EPS = 1e-6
NP = 3456
NPROJ = 3348
NPM = 3328
GLA_SCALE = 32.0 ** -0.5
INV_TAU = 1.0 / 16.0
TB = 512
NCH = TB // CH
assert TB % 256 == 0

C_AH, C_AB, C_AC, C_AZ, C_GQ, C_GK, C_GV = 0, 256, 512, 768, 1024, 1152, 1280
C_GZ, C_PU, C_PZ, C_SZ, C_SX, C_TL = 1536, 1792, 2048, 2304, 2560, 3328
_PERM = ((0, 1536), (1552, 1792), (1536, 16), (3344, 4))
_UNPERM = ((0, 1536), (3328, 16), (1536, 1792), (3344, 4))

R_CAW, R_GB, R_GNW, R_PSC, R_SCB, R_DTB, R_AE, R_DE, R_SNW, R_SCW = 0, 3, 4, 5, 6, 7, 8, 9, 10, 12

ADAM_LR, ADAM_B1, ADAM_B2, ADAM_EPS, ADAM_WD, ADAM_STEP = 0.001, 0.9, 0.999, 1e-08, 0.01, 10

VMEM_LIMIT = 56 * 1024 * 1024


def _cparams(sem, limit=VMEM_LIMIT):
    return pltpu.CompilerParams(dimension_semantics=sem, vmem_limit_bytes=limit)


_ANY = pl.BlockSpec(memory_space=pl.ANY)


def _place():
    return lax.axis_index("x"), lax.axis_index("y"), lax.axis_index("c")


class _Rider:
    def __init__(self, inputs, out_shapes, sems, start, finish, aliases=None):
        self.inputs, self.out_shapes, self.sems = tuple(inputs), tuple(out_shapes), tuple(sems)
        self.start, self.finish, self.aliases = start, finish, dict(aliases or {})


def _call(body, args, *, grid, in_specs, out_specs, out_shape, name, sem, scratch_shapes=(), rider=None):
    if rider is None:
        outs = pl.pallas_call(body, grid=grid, name=name, in_specs=list(in_specs), out_specs=list(out_specs),
                              out_shape=list(out_shape), scratch_shapes=list(scratch_shapes),
                              compiler_params=_cparams(sem))(*args)
        return list(outs), []
    ni, no, ns = len(args), len(out_shape), len(scratch_shapes)
    ri, ro = len(rider.inputs), len(rider.out_shapes)

    def full(*refs):
        ins, rins = refs[:ni], refs[ni:ni + ri]
        outs, routs = refs[ni + ri:ni + ri + no], refs[ni + ri + no:ni + ri + no + ro]
        scr, rsem = refs[ni + ri + no + ro:ni + ri + no + ro + ns], refs[ni + ri + no + ro + ns:]
        first = functools.reduce(jnp.logical_and, [pl.program_id(a) == 0 for a in range(len(grid))])
        last = functools.reduce(jnp.logical_and, [pl.program_id(a) == grid[a] - 1 for a in range(len(grid))])

        @pl.when(first)
        def _():
            rider.start(rins, routs, rsem)

        body(*ins, *outs, *scr)

        @pl.when(last)
        def _():
            rider.finish(rins, routs, rsem)

    outs = pl.pallas_call(
        full, grid=grid, name=name, in_specs=list(in_specs) + [_ANY] * ri, out_specs=list(out_specs) + [_ANY] * ro,
        out_shape=list(out_shape) + list(rider.out_shapes), scratch_shapes=list(scratch_shapes) + list(rider.sems),
        input_output_aliases={ni + k: no + v for k, v in rider.aliases.items()},
        compiler_params=_cparams(("arbitrary",) * len(grid)))(*args, *rider.inputs)
    return list(outs[:no]), list(outs[no:])


def _run_rider(rider, name):
    ri = len(rider.inputs)

    def body(*refs):
        rins, routs, rsem = refs[:ri], refs[ri:ri + len(rider.out_shapes)], refs[ri + len(rider.out_shapes):]
        rider.start(rins, routs, rsem)
        rider.finish(rins, routs, rsem)

    return list(pl.pallas_call(body, name=name, in_specs=[_ANY] * ri, out_specs=[_ANY] * len(rider.out_shapes),
                               out_shape=list(rider.out_shapes), scratch_shapes=list(rider.sems),
                               input_output_aliases=dict(rider.aliases))(*rider.inputs))


def _dot(a, b):
    return jnp.dot(a.astype(BF16), b.astype(BF16), preferred_element_type=F32)


def _dot_nt(a, b):
    return lax.dot_general(a.astype(BF16), b.astype(BF16), (((1,), (1,)), ((), ())), preferred_element_type=F32)


def _dot_tn(a, b):
    return lax.dot_general(a.astype(BF16), b.astype(BF16), (((0,), (0,)), ((), ())), preferred_element_type=F32)


def _split(a):
    hi = a.astype(BF16)
    lo = (a - hi.astype(F32)).astype(BF16)
    return hi, lo


def _dot2_l(a, b):
    hi, lo = _split(a)
    return _dot(hi, b) + _dot(lo, b)


def _dot2_r(a, b):
    hi, lo = _split(b)
    return _dot(a, hi) + _dot(a, lo)


def _dot3_l(a, b):
    hi, lo = _split(a)
    lo2 = ((a - hi.astype(F32)) - lo.astype(F32)).astype(BF16)
    return _dot(hi, b) + _dot(lo, b) + _dot(lo2, b)


def _dot2_nt(a, b):
    hi, lo = _split(a)
    return _dot_nt(hi, b) + _dot_nt(lo, b)


def _silu(z):
    return z * jax.nn.sigmoid(z)


def _lse1(x):
    return jnp.log(1.0 + jnp.exp(-jnp.abs(x)))


def _cs(a):
    return jnp.sum(a, axis=0, keepdims=True)


def _iota(shape, dim):
    return lax.broadcasted_iota(jnp.int32, shape, dim)


def _mixer_matrices():
    r, c = _iota((256, 256), 0), _iota((256, 256), 1)
    same_chunk = (r >> 6) == (c >> 6)
    mats = jnp.stack([jnp.where((c > r) & same_chunk, 1.0, 0.0), jnp.where((c < r) & same_chunk, 1.0, 0.0),
                      jnp.where(same_chunk, 1.0 / 64.0, 0.0), jnp.where((r < 128) & (r - 16 == (c >> 6)), 1.0, 0.0)])
    mask = jnp.where((_iota((256, 128), 0) >> 6) == (_iota((256, 128), 1) >> 5), 1.0, 0.0)
    return mats.astype(BF16), mask.astype(F32)


def _dn(ext, k, n, h):
    return pltpu.roll(ext, k, axis=0)[h:h + n]


def _up(ext, k, n):
    return pltpu.roll(ext, ext.shape[0] - k, axis=0)[:n]


def _pool_lane_select(lane, s2, s4, s8, s16):
    return jnp.where(lane < 64, s2, jnp.where(lane < 128, s4, jnp.where(lane < 192, s8, s16)))


def _winsum_dn(ext, lane):
    s2 = ext + pltpu.roll(ext, 1, axis=0)
    s4 = s2 + pltpu.roll(s2, 2, axis=0)
    s8 = s4 + pltpu.roll(s4, 4, axis=0)
    s16 = s8 + pltpu.roll(s8, 8, axis=0)
    return _pool_lane_select(lane, s2, s4, s8, s16)


def _winsum_up(ext, lane):
    m = ext.shape[0]
    s2 = ext + pltpu.roll(ext, m - 1, axis=0)
    s4 = s2 + pltpu.roll(s2, m - 2, axis=0)
    s8 = s4 + pltpu.roll(s4, m - 4, axis=0)
    s16 = s8 + pltpu.roll(s8, m - 8, axis=0)
    return _pool_lane_select(lane, s2, s4, s8, s16)


def _pool_inv_count(tile, n):
    lane = _iota((1, 256), 1)
    win = _pool_lane_select(lane, 2.0, 4.0, 8.0, 16.0).astype(F32)
    tpos = (tile * n + _iota((n, 1), 0) + 1).astype(F32)
    return jnp.where(tpos >= win, 1.0 / win, 1.0 / tpos)


def _silu_pair(z):
    s = jax.nn.sigmoid(z)
    return z * s, s * (1.0 + z * (1.0 - s))


def _chunks(a):
    return [a[c * CH:(c + 1) * CH] for c in range(a.shape[0] // CH)]


def _halves(fn, a, b):
    return jnp.concatenate([fn(a[:, 0:128], b[:, 0:128]), fn(a[:, 128:256], b[:, 128:256])], axis=1)


def _chunk_sums(tri, a):
    return jnp.concatenate([_dot2_r(tri, a[r:r + 256]) for r in range(0, a.shape[0], 256)], axis=0)


def _mixer_tile_prep(p_ref, t_ref, xc, prm_ref, gw_v, cm_ref, mk_ref):
    tail = t_ref[...]
    pre = _dot(tail, gw_v) + prm_ref[R_GB:R_GB + 1, 0:128]
    la = (jnp.minimum(pre, 0.0) - _lse1(pre)) * INV_TAU
    dtin = tail + prm_ref[R_DTB:R_DTB + 1, 0:128]
    dtf = jnp.maximum(dtin, 0.0) + _lse1(dtin)
    dte = _dot2_l(dtf, cm_ref[3, 0:128, :])
    da = dte * prm_ref[R_AE:R_AE + 1, 0:256]
    rev = _chunk_sums(cm_ref[0], jnp.concatenate([la, da], axis=1))
    dec = jnp.exp(rev[:, 0:128])
    kd = p_ref[:, C_GK:C_GK + 128].astype(F32) * dec
    wdec = jnp.exp(rev[:, 128:384])
    w = wdec * dte
    xw = xc[:, 0:256] * w
    d_s = [jnp.exp(_cs(a)) for a in _chunks(la)]
    et = [jnp.exp(_cs(a)) for a in _chunks(da)]
    mask_t = mk_ref[...]
    ut_g = [_dot_tn(v, k) * mask_t for v, k in zip(_chunks(p_ref[:, C_GV:C_GV + 256].astype(F32)), _chunks(kd))]
    ut_s = [_halves(_dot_tn, b, x) for b, x in zip(_chunks(xc[:, 256:512]), _chunks(xw))]
    return tail, pre, dtin, dte, dec, kd, wdec, w, xw, d_s, et, ut_g, ut_s


def _rmsproj(x, nw, wp, name, tm=512, rider=None):
    t = x.shape[0]

    def body(x_ref, nw_ref, w_ref, o_ref, t_ref, h_ref):
        xv = x_ref[...]
        rs = lax.rsqrt(jnp.mean(xv * xv, axis=-1, keepdims=True) + EPS)
        h = (xv * rs * nw_ref[...]).astype(BF16)
        h_ref[...] = h
        proj = jnp.dot(h, w_ref[...], preferred_element_type=F32)
        o_ref[...] = proj[:, 0:NPM].astype(BF16)
        t_ref[...] = proj[:, NPM:NP]

    (proj, tail, h), extra = _call(
        body, (x, nw, wp), grid=(t // tm,), name=name, sem=("parallel",), rider=rider,
        in_specs=[pl.BlockSpec((tm, D), lambda i: (i, 0)), pl.BlockSpec((1, D), lambda i: (0, 0)),
                  pl.BlockSpec((D, NP), lambda i: (0, 0))],
        out_specs=[pl.BlockSpec((tm, NPM), lambda i: (i, 0)), pl.BlockSpec((tm, NP - NPM), lambda i: (i, 0)),
                   pl.BlockSpec((tm, D), lambda i: (i, 0))],
        out_shape=[jax.ShapeDtypeStruct((t, NPM), BF16), jax.ShapeDtypeStruct((t, NP - NPM), F32),
                   jax.ShapeDtypeStruct((t, D), BF16)])
    return (proj, tail), h, extra


def _head_tile(xv, tgt, w):
    rs = lax.rsqrt(jnp.mean(xv * xv, axis=-1, keepdims=True) + EPS)
    xh = xv * rs
    err = xh * w - tgt
    dy = err * (1.0 / D)
    dxh = dy * w
    dx = rs * (dxh - xh * jnp.mean(dxh * xh, axis=-1, keepdims=True))
    return dx, _cs(dy * xh), (0.5 / D) * jnp.sum(err * err)


def _dxin(dp, wpt, x, dxn, nw, name, tm=512, rider=None):
    t = x.shape[0]

    def body(dp_ref, w_ref, x_ref, dxn_ref, nw_ref, dx_ref, dnw_ref):
        @pl.when(pl.program_id(0) == 0)
        def _():
            dnw_ref[...] = jnp.zeros_like(dnw_ref)

        acc = jnp.zeros((1, D), F32)
        for rows in (pl.ds(0, tm // 2), pl.ds(tm // 2, tm // 2)):
            dh = jnp.dot(dp_ref[rows, :].astype(BF16), w_ref[...], preferred_element_type=F32)
            xv = x_ref[rows, :]
            rs = lax.rsqrt(jnp.mean(xv * xv, axis=-1, keepdims=True) + EPS)
            xh = xv * rs
            acc = acc + _cs(dh * xh)
            dxh = dh * nw_ref[...]
            dx_ref[rows, :] = dxn_ref[rows, :] + rs * (dxh - xh * jnp.mean(dxh * xh, axis=-1, keepdims=True))
        dnw_ref[0:1, :] += acc

    return _call(
        body, (dp, wpt, x, dxn, nw), grid=(t // tm,), name=name, sem=("arbitrary",), rider=rider,
        in_specs=[pl.BlockSpec((tm, NP), lambda i: (i, 0)), pl.BlockSpec((NP, D), lambda i: (0, 0)),
                  pl.BlockSpec((tm, D), lambda i: (i, 0)), pl.BlockSpec((tm, D), lambda i: (i, 0)),
                  pl.BlockSpec((1, D), lambda i: (0, 0))],
        out_specs=[pl.BlockSpec((tm, D), lambda i: (i, 0)), pl.BlockSpec((8, D), lambda i: (0, 0))],
        out_shape=[jax.ShapeDtypeStruct((t, D), F32), jax.ShapeDtypeStruct((8, D), F32)])


def _dwin(h, dp, name, tm=1024, tn=NP, rider=None):
    t = h.shape[0]

    def body(h_ref, dp_ref, o_ref):
        @pl.when(pl.program_id(1) == 0)
        def _():
            o_ref[...] = jnp.zeros_like(o_ref)

        o_ref[...] += _dot_tn(h_ref[...], dp_ref[...])

    (dwp,), extra = _call(
        body, (h, dp), grid=(NP // tn, t // tm), name=name, sem=("parallel", "arbitrary"), rider=rider,
        in_specs=[pl.BlockSpec((tm, D), lambda j, i: (i, 0)), pl.BlockSpec((tm, tn), lambda j, i: (i, j))],
        out_specs=[pl.BlockSpec((D, tn), lambda j, i: (0, j))], out_shape=[jax.ShapeDtypeStruct((D, NP), F32)])
    return dwp, extra


def _mixer_fwd(proj, x, wo, prm, gw, pw, cmat, mask, name, rider=None, head=None):
    proj, tail = proj
    t = proj.shape[0]
    nt, nc = t // TB, t // CH

    def body(p_ref, t_ref, x_ref, wo_ref, prm_ref, gw_ref, pw_ref, cm_ref, mk_ref, *rest):
        (tgt_ref, fw_ref), rest = (rest[:2], rest[2:]) if head else ((None, None), rest)
        mix_ref, sg_ref, ss_ref, xn_ref = rest[:4]
        acc_ref = rest[4] if head else None
        sg_s, ss_s, h_ua, h_pu, h_sx = rest[-5:]
        i = pl.program_id(0)

        @pl.when(i == 0)
        def _():
            for r in (sg_s, ss_s, h_ua, h_pu, h_sx) + ((acc_ref,) if head else ()):
                r[...] = jnp.zeros_like(r)

        lane = _iota((1, 256), 1)
        u = p_ref[:, C_AC:C_AC + 256].astype(F32) * p_ref[:, C_AH:C_AH + 256].astype(F32)
        ext = jnp.concatenate([h_ua[...], u], axis=0)
        cv = (prm_ref[R_CAW + 2:R_CAW + 3, 0:256] * u + prm_ref[R_CAW + 1:R_CAW + 2, 0:256] * _dn(ext, 1, TB, 8)
              + prm_ref[R_CAW:R_CAW + 1, 0:256] * _dn(ext, 2, TB, 8))
        mix_ref[:, 0:256] = (p_ref[:, C_AB:C_AB + 256].astype(F32) * cv * _silu(p_ref[:, C_AZ:C_AZ + 256].astype(F32))).astype(BF16)
        h_ua[...] = u[TB - 8:, :]
        pu = p_ref[:, C_PU:C_PU + 256].astype(F32)
        ext = jnp.concatenate([h_pu[...], pu], axis=0)
        pooled = _winsum_dn(ext, lane)[16:] * _pool_inv_count(i, TB) - pu
        mixed = _dot(pooled, pw_ref[...])
        mix_ref[:, 512:768] = (prm_ref[R_PSC:R_PSC + 1, 0:256] * mixed * _silu(p_ref[:, C_PZ:C_PZ + 256].astype(F32))).astype(BF16)
        h_pu[...] = pu[TB - 16:, :]
        sx = p_ref[:, C_SX:C_SX + 768].astype(F32)
        ext = jnp.concatenate([h_sx[...], sx], axis=0)
        xc = _silu(prm_ref[R_SCW + 3:R_SCW + 4, :] * sx + prm_ref[R_SCW + 2:R_SCW + 3, :] * _dn(ext, 1, TB, 8)
                   + prm_ref[R_SCW + 1:R_SCW + 2, :] * _dn(ext, 2, TB, 8) + prm_ref[R_SCW:R_SCW + 1, :] * _dn(ext, 3, TB, 8)
                   + prm_ref[R_SCB:R_SCB + 1, :])
        h_sx[...] = sx[TB - 8:, :]

        _, _, _, _, _, _, _, _, _, d_s, et, ut_g, ut_s = _mixer_tile_prep(p_ref, t_ref, xc, prm_ref, gw_ref[...], cm_ref, mk_ref)
        s_g, s_s = sg_s[...], ss_s[...]
        o, y = [], []
        qs = _chunks(p_ref[:, C_GQ:C_GQ + 128].astype(F32) * GLA_SCALE)
        cm = _chunks(xc[:, 512:768])
        for c in range(NCH):
            sg_ref[c] = s_g
            ss_ref[c] = s_s
            s_g = s_g * d_s[c] + ut_g[c]
            s_s = s_s * et[c] + ut_s[c]
            o.append(_dot_nt(qs[c], s_g))
            y.append(_halves(_dot, cm[c], s_s))
        sg_s[...] = s_g
        ss_s[...] = s_s
        o = jnp.concatenate(o, axis=0)
        on = o * lax.rsqrt(_dot2_l(o * o, cm_ref[2]) + EPS)
        mix_ref[:, 256:512] = (on * prm_ref[R_GNW:R_GNW + 1, 0:256] * _silu(p_ref[:, C_GZ:C_GZ + 256].astype(F32))).astype(BF16)
        y2 = ((jnp.concatenate(y, axis=0) + prm_ref[R_DE:R_DE + 1, 0:256] * xc[:, 0:256])
              * _silu(p_ref[:, C_SZ:C_SZ + 256].astype(F32)))
        mix_ref[:, 768:1024] = (y2 * lax.rsqrt(jnp.mean(y2 * y2, axis=-1, keepdims=True) + EPS)
                                * prm_ref[R_SNW:R_SNW + 1, 0:256]).astype(BF16)
        xn = x_ref[...] + jnp.dot(mix_ref[...], wo_ref[...], preferred_element_type=F32)
        if head:
            xn_ref[...], dfw, loss = _head_tile(xn, tgt_ref[...], fw_ref[...])
            acc_ref[0:1, :] += dfw
            acc_ref[1:2, :] += jnp.zeros((1, D), F32) + loss
        else:
            xn_ref[...] = xn

    row = pl.BlockSpec((TB, D), lambda i: (i, 0))
    return _call(
        body, (proj, tail, x, wo, prm, gw, pw, cmat, mask) + tuple(head or ()), grid=(nt,), name=name, sem=("arbitrary",),
        rider=rider,
        in_specs=[pl.BlockSpec((TB, NPM), lambda i: (i, 0)), pl.BlockSpec((TB, NP - NPM), lambda i: (i, 0)), row,
                  pl.BlockSpec((D, D), lambda i: (0, 0)), pl.BlockSpec((16, 768), lambda i: (0, 0)),
                  pl.BlockSpec((128, 128), lambda i: (0, 0)), pl.BlockSpec((256, 256), lambda i: (0, 0)),
                  pl.BlockSpec((4, 256, 256), lambda i: (0, 0, 0)), pl.BlockSpec((256, 128), lambda i: (0, 0))]
        + ([row, pl.BlockSpec((1, D), lambda i: (0, 0))] if head else []),
        out_specs=[row, pl.BlockSpec((NCH, 256, 128), lambda i: (i, 0, 0)),
                   pl.BlockSpec((NCH, 128, 256), lambda i: (i, 0, 0)), row]
        + ([pl.BlockSpec((8, D), lambda i: (0, 0))] if head else []),
        out_shape=[jax.ShapeDtypeStruct((t, D), BF16), jax.ShapeDtypeStruct((nc, 256, 128), F32),
                   jax.ShapeDtypeStruct((nc, 128, 256), F32), jax.ShapeDtypeStruct((t, D), F32)]
        + ([jax.ShapeDtypeStruct((8, D), F32)] if head else []),
        scratch_shapes=[pltpu.VMEM((256, 128), F32), pltpu.VMEM((128, 256), F32), pltpu.VMEM((8, 256), F32),
                        pltpu.VMEM((16, 256), F32), pltpu.VMEM((8, 768), F32)])


def _mixer_bwd(proj, dxn, wot, mix, sg, ss, prm, gw, pw, cmat, mask, name, rider=None):
    proj, tail = proj
    t = proj.shape[0]
    nt = t // TB
    rev = lambda i: nt - 1 - i

    def body(p_ref, hp_ref, t_ref, dxn_ref, wot_ref, mix_ref, sg_ref, ss_ref, prm_ref, gw_ref, pw_ref, cm_ref, mk_ref,
             dp_ref, sgc_ref, dwo_ref,
             gg_s, gs_s, h_dcv, h_dpl, h_dpre, gsm_ref, dgw_ref, dpw_ref, dm_ref):
        i = pl.program_id(0)
        tile = nt - 1 - i

        @pl.when(i == 0)
        def _():
            for r in (gg_s, gs_s, h_dcv, h_dpl, h_dpre, gsm_ref, dgw_ref, dpw_ref, dwo_ref):
                r[...] = jnp.zeros_like(r)

        dxn = dxn_ref[...].astype(BF16)
        dm_ref[...] = jnp.dot(dxn, wot_ref[...], preferred_element_type=F32)
        dwo_ref[...] += _dot_tn(mix_ref[...], dxn)

        lane = _iota((1, 256), 1)
        first = (tile > 0).astype(F32)
        ah, ac = p_ref[:, C_AH:C_AH + 256].astype(F32), p_ref[:, C_AC:C_AC + 256].astype(F32)
        ab, az = p_ref[:, C_AB:C_AB + 256].astype(F32), p_ref[:, C_AZ:C_AZ + 256].astype(F32)
        w0, w1, w2 = (prm_ref[R_CAW + j:R_CAW + j + 1, 0:256] for j in range(3))
        u = ac * ah
        ext = jnp.concatenate([(hp_ref[:, C_AC:C_AC + 256].astype(F32) * hp_ref[:, C_AH:C_AH + 256].astype(F32))[8:16] * first, u], axis=0)
        u1, u2 = _dn(ext, 1, TB, 8), _dn(ext, 2, TB, 8)
        cv = w2 * u + w1 * u1 + w0 * u2
        g = dm_ref[:, 0:256]
        sz, dsz = _silu_pair(az)
        dp_ref[:, C_AB:C_AB + 256] = (g * cv * sz).astype(BF16)
        dp_ref[:, C_AZ:C_AZ + 256] = (g * ab * cv * dsz).astype(BF16)
        dcv = g * ab * sz
        dext = jnp.concatenate([dcv, h_dcv[...]], axis=0)
        du = w2 * dcv + w1 * _up(dext, 1, TB) + w0 * _up(dext, 2, TB)
        dp_ref[:, C_AC:C_AC + 256] = (du * ah).astype(BF16)
        dp_ref[:, C_AH:C_AH + 256] = (du * ac).astype(BF16)
        gsm_ref[R_CAW:R_CAW + 1, 0:256] += _cs(dcv * u2)
        gsm_ref[R_CAW + 1:R_CAW + 2, 0:256] += _cs(dcv * u1)
        gsm_ref[R_CAW + 2:R_CAW + 3, 0:256] += _cs(dcv * u)
        h_dcv[...] = dcv[0:8, :]
        pu, pz = p_ref[:, C_PU:C_PU + 256].astype(F32), p_ref[:, C_PZ:C_PZ + 256].astype(F32)
        psc = prm_ref[R_PSC:R_PSC + 1, 0:256]
        icnt = _pool_inv_count(tile, TB)
        ext = jnp.concatenate([hp_ref[:, C_PU:C_PU + 256].astype(F32) * first, pu], axis=0)
        pooled = _winsum_dn(ext, lane)[16:] * icnt - pu
        pw_v = pw_ref[...]
        mixed = _dot(pooled, pw_v)
        g = dm_ref[:, 512:768]
        sz, dsz = _silu_pair(pz)
        gsm_ref[R_PSC:R_PSC + 1, 0:256] += _cs(g * mixed * sz)
        dp_ref[:, C_PZ:C_PZ + 256] = (g * psc * mixed * dsz).astype(BF16)
        dmixed = g * psc * sz
        dpw_ref[...] += _dot_tn(pooled, dmixed)
        dpooled = _dot_nt(dmixed, pw_v)
        qd = dpooled * icnt
        dext = jnp.concatenate([qd, h_dpl[...]], axis=0)
        dp_ref[:, C_PU:C_PU + 256] = (_winsum_up(dext, lane)[:TB] - dpooled).astype(BF16)
        h_dpl[...] = qd[0:16, :]
        sx = p_ref[:, C_SX:C_SX + 768].astype(F32)
        cw = [prm_ref[R_SCW + j:R_SCW + j + 1, :] for j in range(4)]
        ext = jnp.concatenate([hp_ref[:, C_SX:C_SX + 768].astype(F32)[8:16] * first, sx], axis=0)
        sx1, sx2, sx3 = _dn(ext, 1, TB, 8), _dn(ext, 2, TB, 8), _dn(ext, 3, TB, 8)
        cpre = cw[3] * sx + cw[2] * sx1 + cw[1] * sx2 + cw[0] * sx3 + prm_ref[R_SCB:R_SCB + 1, :]
        xc, dxc = _silu_pair(cpre)
        xs, bm, cm = xc[:, 0:256], xc[:, 256:512], xc[:, 512:768]

        gw_v = gw_ref[...]
        tail, pre, dtin, dte, dec, kd, wdec, w, xw, d_s, et, ut_g, ut_s = _mixer_tile_prep(p_ref, t_ref, xc, prm_ref,
                                                                                          gw_v, cm_ref, mk_ref)
        gmean = cm_ref[2]
        mask_t = mk_ref[...]
        gnw = prm_ref[R_GNW:R_GNW + 1, 0:256]
        a_e = prm_ref[R_AE:R_AE + 1, 0:256]
        d_e = prm_ref[R_DE:R_DE + 1, 0:256]
        snw = prm_ref[R_SNW:R_SNW + 1, 0:256]
        sg_in = [sg_ref[c] for c in range(NCH)]
        ss_in = [ss_ref[c] for c in range(NCH)]
        sg_n = [sg_in[c] * d_s[c] + ut_g[c] for c in range(NCH)]
        ss_n = [ss_in[c] * et[c] + ut_s[c] for c in range(NCH)]
        qs = _chunks(p_ref[:, C_GQ:C_GQ + 128].astype(F32) * GLA_SCALE)
        cm_c, bm_c, xw_c, kd_c = _chunks(cm), _chunks(bm), _chunks(xw), _chunks(kd)
        v_c = _chunks(p_ref[:, C_GV:C_GV + 256].astype(F32))
        o = jnp.concatenate([_dot_nt(qs[c], sg_n[c]) for c in range(NCH)], axis=0)
        y = jnp.concatenate([_halves(_dot, cm_c[c], ss_n[c]) for c in range(NCH)], axis=0) + d_e * xs
        gz = p_ref[:, C_GZ:C_GZ + 256].astype(F32)
        r = lax.rsqrt(_dot2_l(o * o, gmean) + EPS)
        on = o * r
        dyb = dm_ref[:, 256:512]
        sz, dsz = _silu_pair(gz)
        dp_ref[:, C_GZ:C_GZ + 256] = (dyb * on * gnw * dsz).astype(BF16)
        tg = dyb * sz
        gsm_ref[R_GNW:R_GNW + 1, 0:256] += _cs(tg * on)
        don = tg * gnw
        do_c = _chunks(r * (don - on * _dot2_l(don * on, gmean)))
        ssz = p_ref[:, C_SZ:C_SZ + 256].astype(F32)
        sil, dsil = _silu_pair(ssz)
        y2 = y * sil
        r = lax.rsqrt(jnp.mean(y2 * y2, axis=-1, keepdims=True) + EPS)
        yn = y2 * r
        dyd = dm_ref[:, 768:1024]
        gsm_ref[R_SNW:R_SNW + 1, 0:256] += _cs(dyd * yn)
        dn = dyd * snw
        dy2 = r * (dn - yn * jnp.mean(dn * yn, axis=-1, keepdims=True))
        dp_ref[:, C_SZ:C_SZ + 256] = (dy2 * y * dsil).astype(BF16)
        dy = dy2 * sil
        gsm_ref[R_DE:R_DE + 1, 0:256] += _cs(dy * xs)
        dy_c = _chunks(dy)
        dq = jnp.concatenate([_dot(do_c[c], sg_n[c]) for c in range(NCH)], axis=0)
        dp_ref[:, C_GQ:C_GQ + 128] = (dq * GLA_SCALE).astype(BF16)
        dcm = jnp.concatenate([_halves(_dot_nt, dy_c[c], ss_n[c]) for c in range(NCH)], axis=0)
        gg = [_dot_tn(do_c[c], qs[c]) * mask_t for c in range(NCH)]
        gs = [_halves(_dot_tn, cm_c[c], dy_c[c]) for c in range(NCH)]
        car_g, car_s = gg_s[...], gs_s[...]
        for c in reversed(range(NCH)):
            gg[c] = gg[c] + car_g
            gs[c] = gs[c] + car_s
            car_g = gg[c] * d_s[c]
            car_s = gs[c] * et[c]
        gg_s[...] = car_g
        gs_s[...] = car_s
        dkd = jnp.concatenate([_dot(v_c[c], gg[c]) for c in range(NCH)], axis=0)
        dp_ref[:, C_GV:C_GV + 256] = jnp.concatenate([_dot_nt(kd_c[c], gg[c]) for c in range(NCH)], axis=0).astype(BF16)
        dp_ref[:, C_GK:C_GK + 128] = (dkd * dec).astype(BF16)
        dbm = jnp.concatenate([_halves(_dot_nt, xw_c[c], gs[c]) for c in range(NCH)], axis=0)
        dxw = jnp.concatenate([_halves(_dot, bm_c[c], gs[c]) for c in range(NCH)], axis=0)
        dxs = dy * d_e + dxw * w
        dw = dxw * xs
        dsuf = _chunk_sums(cm_ref[1], jnp.concatenate([dkd * kd, dw * dte * wdec], axis=1))
        tot_g = jnp.concatenate([jnp.broadcast_to(_cs(gg[c] * sg_in[c]) * d_s[c], (CH, 128)) for c in range(NCH)], axis=0)
        tot_s = jnp.concatenate([jnp.broadcast_to(_cs(gs[c] * ss_in[c]) * et[c], (CH, 256)) for c in range(NCH)], axis=0)
        dpre = (dsuf[:, 0:128] + tot_g) * INV_TAU * jax.nn.sigmoid(-pre)
        dgw_ref[...] += _dot_tn(tail, dpre)
        gsm_ref[R_GB:R_GB + 1, 0:128] += _cs(dpre)
        dda = dsuf[:, 128:384] + tot_s
        gsm_ref[R_AE:R_AE + 1, 0:256] += _cs(dda * dte)
        dtail_s = _dot2_nt(dw * wdec + dda * a_e, cm_ref[3, 0:128, :]) * jax.nn.sigmoid(dtin)
        gsm_ref[R_DTB:R_DTB + 1, 0:128] += _cs(dtail_s)
        dp_ref[:, C_TL:C_TL + 128] = (_dot_nt(dpre, gw_v) + dtail_s).astype(BF16)
        dpre_c = jnp.concatenate([dxs, dbm, dcm], axis=1) * dxc
        dext = jnp.concatenate([dpre_c, h_dpre[...]], axis=0)
        dp_ref[:, C_SX:C_SX + 768] = (cw[3] * dpre_c + cw[2] * _up(dext, 1, TB) + cw[1] * _up(dext, 2, TB)
                                      + cw[0] * _up(dext, 3, TB)).astype(BF16)
        gsm_ref[R_SCW + 3:R_SCW + 4, :] += _cs(dpre_c * sx)
        gsm_ref[R_SCW + 2:R_SCW + 3, :] += _cs(dpre_c * sx1)
        gsm_ref[R_SCW + 1:R_SCW + 2, :] += _cs(dpre_c * sx2)
        gsm_ref[R_SCW:R_SCW + 1, :] += _cs(dpre_c * sx3)
        gsm_ref[R_SCB:R_SCB + 1, :] += _cs(dpre_c)
        h_dpre[...] = dpre_c[0:8, :]

        @pl.when(i == nt - 1)
        def _():
            ri, ci = _iota((256, 256), 0), _iota((256, 256), 1)
            per_head = jnp.where((ri >> 6) == ci, 1.0, 0.0).astype(BF16)
            per_dv = jnp.where((ri & 63) == ci, 1.0, 0.0).astype(BF16)
            row = _iota((8, 256), 0)
            top = gsm_ref[0:8, 0:256]
            sgc_ref[0:8, 0:256] = jnp.where(row == R_GNW, _dot3_l(top, per_dv), top)
            bot = gsm_ref[8:16, 0:256]
            fold = _dot3_l(jnp.where(row == R_AE - 8, bot * a_e, bot), per_head)
            sgc_ref[8:16, 0:256] = jnp.where((row == R_AE - 8) | (row == R_DE - 8), fold, bot)
            sgc_ref[0:16, 256:768] = gsm_ref[:, 256:768]
            sgc_ref[0:16, 768:896] = dgw_ref[0:16, :]
            sgc_ref[0:16, 896:1024] = jnp.zeros((16, 128), F32)
            diag = _pool_lane_select(lane, dpw_ref[0:64, :], dpw_ref[64:128, :], dpw_ref[128:192, :], dpw_ref[192:256, :])
            for q in range(4):
                sgc_ref[16:32, 256 * q:256 * q + 256] = diag[16 * q:16 * q + 16, :]

    return _call(
        body, (proj, proj, tail, dxn, wot, mix, sg, ss, prm, gw, pw, cmat, mask), grid=(nt,), name=name,
        sem=("arbitrary",), rider=rider,
        in_specs=[pl.BlockSpec((TB, NPM), lambda i: (rev(i), 0)),
                  pl.BlockSpec((16, NPM), lambda i: (jnp.maximum(rev(i) * (TB // 16) - 1, 0), 0)),
                  pl.BlockSpec((TB, NP - NPM), lambda i: (rev(i), 0)),
                  pl.BlockSpec((TB, D), lambda i: (rev(i), 0)), pl.BlockSpec((D, D), lambda i: (0, 0)),
                  pl.BlockSpec((TB, D), lambda i: (rev(i), 0)),
                  pl.BlockSpec((NCH, 256, 128), lambda i: (rev(i), 0, 0)),
                  pl.BlockSpec((NCH, 128, 256), lambda i: (rev(i), 0, 0)),
                  pl.BlockSpec((16, 768), lambda i: (0, 0)), pl.BlockSpec((128, 128), lambda i: (0, 0)),
                  pl.BlockSpec((256, 256), lambda i: (0, 0)), pl.BlockSpec((4, 256, 256), lambda i: (0, 0, 0)),
                  pl.BlockSpec((256, 128), lambda i: (0, 0))],
        out_specs=[pl.BlockSpec((TB, NP), lambda i: (rev(i), 0)), pl.BlockSpec((32, 1024), lambda i: (0, 0)),
                   pl.BlockSpec((D, D), lambda i: (0, 0))],
        out_shape=[jax.ShapeDtypeStruct((t, NP), BF16), jax.ShapeDtypeStruct((32, 1024), F32),
                   jax.ShapeDtypeStruct((D, D), F32)],
        scratch_shapes=[pltpu.VMEM((256, 128), F32), pltpu.VMEM((128, 256), F32), pltpu.VMEM((8, 256), F32),
                        pltpu.VMEM((16, 256), F32), pltpu.VMEM((8, 768), F32), pltpu.VMEM((16, 768), F32),
                        pltpu.VMEM((128, 128), F32), pltpu.VMEM((256, 256), F32), pltpu.VMEM((TB, D), F32)])


SHARD = NPROJ // 4
SHARD_PAD = 896


def _ranges_to_perm(o, n):
    out, p = [], 0
    for start, size in _PERM:
        a, b = max(o, start), min(o + n, start + size)
        if a < b:
            out.append((a, b - a, p + a - start))
        p += size
    return out


def _ranges_to_orig(p0, n):
    out, p = [], 0
    for start, size in _PERM:
        a, b = max(p0, p), min(p0 + n, p + size)
        if a < b:
            out.append((a, b - a, start + a - p))
        p += size
    return out


def _lane_window(load, lo, n, d, lane):
    a = 128 * (lo // 128)
    off = lo - a
    w = 128 if off + n <= 128 else 256
    chunk = load(a, w)
    shift = (d - off) % w
    if shift:
        chunk = pltpu.roll(chunk, shift, axis=1)
    return jnp.where((lane >= d) & (lane < d + n), chunk[:, 0:128], 0.0)


def _assemble_w_in(slabs, name, rb=256):
    def body(s_ref, wp_ref, wpt_ref):
        lane = _iota((1, 128), 1)
        for b in range(NP // 128):
            acc = jnp.zeros((rb, 128), F32)
            for p, n, o in _ranges_to_orig(128 * b, 128):
                while n > 0:
                    s, lo = o // SHARD, o % SHARD
                    cnt = min(n, SHARD - lo)
                    acc = acc + _lane_window(lambda a, w, s=s: s_ref[s, :, a:a + w].astype(F32), lo, cnt, p - 128 * b, lane)
                    o, p, n = o + cnt, p + cnt, n - cnt
            wp_ref[:, 128 * b:128 * b + 128] = acc.astype(BF16)
            wpt_ref[128 * b:128 * b + 128, :] = acc.T.astype(BF16)

    return pl.pallas_call(
        body, grid=(D // rb,), name=name,
        in_specs=[pl.BlockSpec((4, rb, SHARD_PAD), lambda i: (0, i, 0))],
        out_specs=[pl.BlockSpec((rb, NP), lambda i: (i, 0)), pl.BlockSpec((NP, rb), lambda i: (0, i))],
        out_shape=[jax.ShapeDtypeStruct((D, NP), BF16), jax.ShapeDtypeStruct((NP, D), BF16)],
        compiler_params=_cparams(("parallel",)))(slabs)


def _split_dw_in(dwp, name, rb=256):
    rows = dwp.shape[0]

    def body(g_ref, o_ref):
        lane = _iota((1, 128), 1)
        for s in range(4):
            for k in range(SHARD_PAD // 128):
                acc = jnp.zeros((rb, 128), F32)
                n_valid = min(128, SHARD - 128 * k)
                for o, n, p in _ranges_to_perm(SHARD * s + 128 * k, n_valid):
                    acc = acc + _lane_window(lambda a, w: g_ref[:, a:a + w].astype(F32), p, n, o - SHARD * s - 128 * k, lane)
                o_ref[s, :, 128 * k:128 * k + 128] = acc.astype(o_ref.dtype)

    return pl.pallas_call(
        body, grid=(rows // rb,), name=name,
        in_specs=[pl.BlockSpec((rb, NP), lambda i: (i, 0))],
        out_specs=pl.BlockSpec((4, rb, SHARD_PAD), lambda i: (0, i, 0)),
        out_shape=jax.ShapeDtypeStruct((4, rows, SHARD_PAD), dwp.dtype),
        compiler_params=_cparams(("parallel",)))(dwp)


def _half(c, n):
    return pl.ds(pl.multiple_of(c * (n // 2), n // 2), n // 2)


def _other_chips(x, y):
    return ((1 - x, y), (x, 1 - y), (1 - x, 1 - y))


def _remote(src, dst, send, recv, k, dev):
    return pltpu.make_async_remote_copy(src_ref=src, dst_ref=dst, send_sem=send.at[k], recv_sem=recv.at[k], device_id=dev,
                                        device_id_type=MESH)


def _sem(n):
    return pltpu.SemaphoreType.DMA((n,))


def _rider_gather_ici(shards, extra=None):
    shards = tuple(shards) + ((extra,) if extra is not None else ())
    n = len(shards)

    def copies(rins, routs, sems, arrivals=True):
        send, recv = sems
        x, y, c = _place()
        me = 2 * x + y
        out, inc = [], []
        for j, (px, py) in enumerate(_other_chips(x, y)):
            for k in range(n):
                whole = extra is not None and k == n - 1
                rows = pl.ds(0, shards[k].shape[0]) if whole else _half(c, shards[k].shape[0])
                out.append(_remote(rins[k].at[rows], routs[k].at[me, rows], send, recv, n * j + k, (px, py, c)))
                if arrivals:
                    inc.append(_remote(rins[k].at[rows], routs[k].at[2 * px + py, rows], send, recv, n * j + k, (px, py, c)))
        return out, inc

    def start(rins, routs, sems):
        for cp in copies(rins, routs, sems, arrivals=False)[0]:
            cp.start()

    def finish(rins, routs, sems):
        out, inc = copies(rins, routs, sems)
        for cp in inc:
            cp.wait_recv()
        for cp in out:
            cp.wait_send()

    return _Rider(shards, [jax.ShapeDtypeStruct((4,) + a.shape, a.dtype) for a in shards], [_sem(3 * n), _sem(3 * n)],
                  start, finish)


def _rider_gather_d2d(slabs):
    slabs = tuple(slabs)
    n = len(slabs)

    def copies(routs, sems, arrivals=True):
        send, recv = sems
        x, y, c = _place()
        out, inc = [], []
        for j, (px, py) in enumerate(_other_chips(x, y)):
            for k in range(n):
                rows = slabs[k].shape[1]
                mine, theirs = routs[k].at[2 * px + py, _half(c, rows)], routs[k].at[2 * px + py, _half(1 - c, rows)]
                out.append(_remote(mine, mine, send, recv, n * j + k, (x, y, 1 - c)))
                if arrivals:
                    inc.append(_remote(theirs, theirs, send, recv, n * j + k, (x, y, 1 - c)))
        return out, inc

    def start(rins, routs, sems):
        for cp in copies(routs, sems, arrivals=False)[0]:
            cp.start()

    def finish(rins, routs, sems):
        out, inc = copies(routs, sems)
        for cp in inc:
            cp.wait_recv()
        for cp in out:
            cp.wait_send()

    return _Rider(slabs, [jax.ShapeDtypeStruct(a.shape, a.dtype) for a in slabs], [_sem(3 * n), _sem(3 * n)], start, finish,
                  aliases={k: k for k in range(n)})


def _rider_swap(parts):
    parts = tuple(parts)
    n = len(parts)

    def copies(rins, routs, sems):
        send, recv = sems
        x, y, c = _place()
        return [_remote(rins[k].at[:, _half(1 - c, parts[k].shape[1])], routs[k], send, recv, k, (x, y, 1 - c))
                for k in range(n)]

    def start(rins, routs, sems):
        for cp in copies(rins, routs, sems):
            cp.start()

    def finish(rins, routs, sems):
        for cp in copies(rins, routs, sems):
            cp.wait()

    return _Rider(parts, [jax.ShapeDtypeStruct((a.shape[0], a.shape[1] // 2, a.shape[2]), a.dtype) for a in parts],
                  [_sem(n), _sem(n)], start, finish)


def _rider_scatter(parts):
    parts = tuple(parts)
    n = len(parts)

    def copies(rins, routs, sems, arrivals=True):
        send, recv = sems
        x, y, c = _place()
        me = 2 * x + y
        out, inc = [], []
        for j, (px, py) in enumerate(_other_chips(x, y)):
            for k in range(n):
                out.append(_remote(rins[k].at[2 * px + py], routs[k].at[me], send, recv, n * j + k, (px, py, c)))
                if arrivals:
                    inc.append(_remote(rins[k].at[me], routs[k].at[2 * px + py], send, recv, n * j + k, (px, py, c)))
        return out, inc

    def start(rins, routs, sems):
        for cp in copies(rins, routs, sems, arrivals=False)[0]:
            cp.start()

    def finish(rins, routs, sems):
        out, inc = copies(rins, routs, sems)
        for cp in inc:
            cp.wait_recv()
        for cp in out:
            cp.wait_send()

    return _Rider(parts, [jax.ShapeDtypeStruct(a.shape, a.dtype) for a in parts], [_sem(3 * n), _sem(3 * n)], start, finish)


def _rider_share(fulls):
    fulls = tuple(fulls)
    n = len(fulls)

    def copies(routs, sems, arrivals=True):
        send, recv = sems
        x, y, c = _place()
        out, inc = [], []
        for k in range(n):
            mine, theirs = routs[k].at[_half(c, fulls[k].shape[0])], routs[k].at[_half(1 - c, fulls[k].shape[0])]
            out.append(_remote(mine, mine, send, recv, k, (x, y, 1 - c)))
            if arrivals:
                inc.append(_remote(theirs, theirs, send, recv, k, (x, y, 1 - c)))
        return out, inc

    def start(rins, routs, sems):
        for cp in copies(routs, sems, arrivals=False)[0]:
            cp.start()

    def finish(rins, routs, sems):
        out, inc = copies(routs, sems)
        for cp in inc:
            cp.wait_recv()
        for cp in out:
            cp.wait_send()

    return _Rider(fulls, [jax.ShapeDtypeStruct(a.shape, a.dtype) for a in fulls], [_sem(n), _sem(n)], start, finish,
                  aliases={k: k for k in range(n)})


def _pair_sum(core, full, recv, name, br=128):
    n, rows, cols = recv.shape

    def body(c_ref, a_ref, b_ref, o_ref):
        o_ref[...] = (a_ref[...] + b_ref[...]).astype(BF16)

    nb = rows // br
    return pl.pallas_call(
        body, name=name, out_shape=jax.ShapeDtypeStruct(recv.shape, BF16),
        grid_spec=pltpu.PrefetchScalarGridSpec(
            num_scalar_prefetch=1, grid=(n, nb),
            in_specs=[pl.BlockSpec((1, br, cols), lambda i, j, c: (i, c[0] * nb + j, 0)),
                      pl.BlockSpec((1, br, cols), lambda i, j, c: (i, j, 0))],
            out_specs=pl.BlockSpec((1, br, cols), lambda i, j, c: (i, j, 0))),
        compiler_params=_cparams(("parallel", "parallel")))(core, full, recv)


def _chip_sum(place, gathered, mine, name, br=128):
    _, r, c = gathered.shape
    nb = r // br

    def body(p_ref, g_ref, m_ref, o_ref):
        slab = lambda j: jnp.where(p_ref[1] == j, m_ref[j], g_ref[j]).astype(F32)
        o_ref[...] = ((slab(0) + slab(1)) + slab(2)) + slab(3)

    return pl.pallas_call(
        body, name=name, out_shape=jax.ShapeDtypeStruct((2 * r, c), F32),
        grid_spec=pltpu.PrefetchScalarGridSpec(
            num_scalar_prefetch=1, grid=(nb,),
            in_specs=[pl.BlockSpec((4, br, c), lambda i, p: (0, i, 0)), pl.BlockSpec((4, br, c), lambda i, p: (0, i, 0))],
            out_specs=pl.BlockSpec((br, c), lambda i, p: (p[0] * nb + i, 0))),
        compiler_params=_cparams(("parallel",)))(place, gathered, mine)


def _adamw(w, g, m, v, name, br):
    n, r, c = w.shape

    def body(w_ref, g_ref, m_ref, v_ref, d_ref, m2_ref, v2_ref):
        d_ref[...], m2_ref[...], v2_ref[...] = _adam_math(w_ref[...], g_ref[...], m_ref[...], v_ref[...])

    spec = pl.BlockSpec((1, br, c), lambda i, j: (i, j, 0))
    shp = jax.ShapeDtypeStruct(w.shape, F32)
    return pl.pallas_call(body, grid=(n, r // br), name=name, in_specs=[spec] * 4, out_specs=[spec] * 3,
                          out_shape=[shp] * 3, compiler_params=_cparams(("parallel", "parallel")))(w, g, m, v)


def _adamw_w_in(w, g, m, v, name, bc=31):
    cols = w.shape[2]
    lead = lambda a: jnp.transpose(a, (2, 0, 1))
    g = jnp.stack([a[:, 0:cols] for a in g])

    def body(w_ref, g_ref, m_ref, v_ref, go_ref, d_ref, m2_ref, v2_ref):
        for l in range(2):
            gv = g_ref[:, l, :]
            d_ref[:, l, :], m2_ref[:, l, :], v2_ref[:, l, :] = _adam_math(w_ref[:, l, :], gv, m_ref[:, l, :], v_ref[:, l, :])
            go_ref[:, l, :] = gv

    spec = pl.BlockSpec((bc, 2, D), lambda i: (i, 0, 0))
    outs = pl.pallas_call(body, grid=(cols // bc,), name=name, in_specs=[spec] * 4, out_specs=[spec] * 4,
                          out_shape=[jax.ShapeDtypeStruct((cols, 2, D), F32)] * 4,
                          compiler_params=_cparams(("parallel",)))(lead(w), lead(g), lead(m), lead(v))
    return [jnp.transpose(o, (1, 2, 0)) for o in outs]


_SMALL_NAMES = ("norm_w", "conv_a_w", "gla_gate_w", "gla_gate_b", "gla_norm_w", "pool_w", "pool_scale", "ssd_conv_w",
                "ssd_conv_b", "ssd_dt_bias", "ssd_a_log", "ssd_d", "ssd_norm_w", "final_norm_w")
SMALL_ROWS = 72


def _adam_math(w, g, m, v):
    m2 = ADAM_B1 * m + (1.0 - ADAM_B1) * g
    v2 = ADAM_B2 * v + (1.0 - ADAM_B2) * (g * g)
    m_hat = m2 / (1.0 - ADAM_B1 ** ADAM_STEP)
    v_hat = v2 / (1.0 - ADAM_B2 ** ADAM_STEP)
    return -ADAM_LR * (m_hat / (jnp.sqrt(v_hat) + ADAM_EPS) + ADAM_WD * w), m2, v2


def _small_slices(name, chip):
    if name == "conv_a_w":
        return [((), slice(R_CAW, R_CAW + 3), slice(64 * chip, 64 * chip + 64))]
    if name == "ssd_conv_w":
        return [((), slice(R_SCW, R_SCW + 4), slice(192 * chip, 192 * chip + 192))]
    if name == "gla_gate_w":
        return [((), slice(0, 16), slice(768, 896))]
    if name == "pool_w":
        return [((g, slice(16 * q, 16 * q + 16)), slice(16, 32), slice(256 * q + 64 * g, 256 * q + 64 * g + 64))
                for g in range(4) for q in range(4)]
    row, lanes = {"gla_gate_b": (R_GB, slice(0, 128)), "gla_norm_w": (R_GNW, slice(0, 64)),
                  "pool_scale": (R_PSC, slice(0, 256)), "ssd_conv_b": (R_SCB, slice(0, 768)),
                  "ssd_dt_bias": (R_DTB, slice(16, 20)), "ssd_a_log": (R_AE, slice(0, 4)), "ssd_d": (R_DE, slice(0, 4)),
                  "ssd_norm_w": (R_SNW, slice(0, 256))}[name]
    return [((), slice(row, row + 1), lanes)]


def _small_allreduce(sg0, sg1, dnw0, dnw1, head):
    def body(sg0_ref, sg1_ref, dnw0_ref, dnw1_ref, head_ref, acc, stage, pair, rbuf, send_sems, recv_sems):
        x, y, c = _place()
        chip = 2 * x + y
        stage[0:32, :] = sg0_ref[...]
        stage[32:64, :] = sg1_ref[...]
        stage[64:65, :] = dnw0_ref[0:1, :]
        stage[65:66, :] = dnw1_ref[0:1, :]
        stage[66:68, :] = head_ref[0:2, :]
        stage[68:72, :] = jnp.zeros((4, D), F32)
        sib = _remote(stage, pair, send_sems, recv_sems, 0, (x, y, 1 - c))
        sib.start()
        sib.wait()
        rbuf[0] = stage[...] + pair[...]
        sends = [_remote(rbuf.at[0], rbuf.at[k], send_sems, recv_sems, k, (px, py, c))
                 for k, (px, py) in enumerate(_other_chips(x, y), start=1)]
        for cp in sends:
            cp.start()
        for cp in sends:
            cp.wait()
        slab = lambda d: jnp.where(d == 0, 0, jnp.where(d == 2, 1, jnp.where(d == 1, 2, 3)))
        total = rbuf[slab(jnp.bitwise_xor(chip, 0))]
        for s in range(1, 4):
            total = total + rbuf[slab(jnp.bitwise_xor(chip, s))]
        acc[...] = total

    vmem = pl.BlockSpec(memory_space=pltpu.VMEM)
    return pl.pallas_call(
        body, name="small_allreduce", in_specs=[vmem] * 5, out_specs=vmem,
        out_shape=jax.ShapeDtypeStruct((SMALL_ROWS, D), F32),
        scratch_shapes=[pltpu.VMEM((SMALL_ROWS, D), F32), pltpu.VMEM((SMALL_ROWS, D), F32),
                        pltpu.VMEM((4, SMALL_ROWS, D), F32), _sem(4), _sem(4)],
    )(sg0, sg1, dnw0, dnw1, head)


def _small_adamw(acc, w, m, v):
    n = len(_SMALL_NAMES)

    def body(*refs):
        acc = refs[0]
        w_refs, m_refs, v_refs = refs[1:1 + n], refs[1 + n:1 + 2 * n], refs[1 + 2 * n:1 + 3 * n]
        o = 1 + 3 * n
        g_out, d_out, m_out, v_out = refs[o:o + n], refs[o + n:o + 2 * n], refs[o + 2 * n:o + 3 * n], refs[o + 3 * n:o + 4 * n]
        loss_ref = refs[o + 4 * n]
        chip = 2 * lax.axis_index("x") + lax.axis_index("y")
        loss_ref[...] = acc[67:68, 0:1]

        def update(i, idx, g):
            d, m2, v2 = _adam_math(w_refs[i][idx], g, m_refs[i][idx], v_refs[i][idx])
            g_out[i][idx], d_out[i][idx], m_out[i][idx], v_out[i][idx] = g, d, m2, v2

        for i, name in enumerate(_SMALL_NAMES):
            if name == "final_norm_w":
                update(i, (slice(0, 1), slice(None)), acc[66:67, :])
            elif name == "norm_w":
                for l in range(2):
                    update(i, (slice(l, l + 1), slice(None)), acc[64 + l:65 + l, :])
            elif name in ("conv_a_w", "ssd_conv_w"):
                for s in range(4):
                    @pl.when(chip == s)
                    def _(i=i, name=name, s=s):
                        for l in range(2):
                            (_, rows, lanes), = _small_slices(name, s)
                            update(i, (l,), acc[rows.start + 32 * l:rows.stop + 32 * l, lanes])
            else:
                for l in range(2):
                    for idx, rows, lanes in _small_slices(name, 0):
                        g = acc[rows.start + 32 * l:rows.stop + 32 * l, lanes]
                        if w_refs[i].ndim == 2:
                            update(i, (slice(l, l + 1), slice(None)), g)
                        else:
                            update(i, (l,) + idx, g)

    args = [acc] + [d[k] for d in (w, m, v) for k in _SMALL_NAMES]
    shapes = [jax.ShapeDtypeStruct(w[k].shape, F32) for k in _SMALL_NAMES]
    vmem = pl.BlockSpec(memory_space=pltpu.VMEM)
    outs = pl.pallas_call(body, name="small_adamw", in_specs=[vmem] * len(args), out_specs=[vmem] * (4 * n + 1),
                          out_shape=shapes * 4 + [jax.ShapeDtypeStruct((1, 1), F32)])(*args)
    return outs[0:n], outs[n:2 * n], outs[2 * n:3 * n], outs[3 * n:4 * n], outs[4 * n]


def _mixer_consts(layer, conv_a_w, gla_gate_w, gla_gate_b, gla_norm_w, pool_w, pool_scale, ssd_conv_w, ssd_conv_b,
                  ssd_dt_bias, ssd_a_log, ssd_d, ssd_norm_w):
    def row(v):
        return jnp.pad(v.reshape(1, -1), ((0, 0), (0, 768 - v.size)))

    dtb = jnp.pad(ssd_dt_bias[layer], (16, 108))
    rows = [jnp.pad(conv_a_w[layer], ((0, 0), (0, 512))), row(gla_gate_b[layer]), row(jnp.tile(gla_norm_w[layer], 4)),
            row(pool_scale[layer]), row(ssd_conv_b[layer]), row(dtb), row(jnp.repeat(-jnp.exp(ssd_a_log[layer]), 64)),
            row(jnp.repeat(ssd_d[layer], 64)), row(ssd_norm_w[layer]), jnp.zeros((1, 768), F32), ssd_conv_w[layer]]
    prm = jnp.concatenate(rows, axis=0)
    gw = jnp.pad(gla_gate_w[layer], ((0, 112), (0, 0))).astype(BF16)
    on_diag = (_iota((256, 256), 0) >> 6) == (_iota((256, 256), 1) >> 6)
    pw = jnp.where(on_diag, jnp.tile(pool_w[layer].reshape(256, 64), (1, 4)), 0.0)
    return (prm, gw, pw.astype(BF16)) + _mixer_matrices()


def _grad_slabs(dwp, dwo):
    return dwp.reshape(1, D, NP), dwo.reshape(4, D // 4, D)


class _Comm:
    def __init__(self, w_in, w_out):
        self.w_in16 = jnp.pad(w_in.astype(BF16), ((0, 0), (0, 0), (0, SHARD_PAD - SHARD)))
        self.w_out16 = w_out.astype(BF16)
        self.core = lax.axis_index("c").astype(jnp.int32).reshape(1)
        self.chip = 2 * lax.axis_index("x") + lax.axis_index("y")
        self.place = jnp.stack([lax.axis_index("c"), self.chip]).astype(jnp.int32)

    def gather_ici(self, layer, extra=None):
        return _rider_gather_ici((self.w_in16[layer], self.w_out16[layer]), extra)

    def pair_sum(self, layer, slabs, received):
        d_in, d_out = [_pair_sum(self.core, a, b, name=f"reduce_pair_sum{layer}_{k}")
                       for k, (a, b) in enumerate(zip(slabs, received))]
        return [_split_dw_in(d_in[0], name=f"split_dw_in{layer}"), d_out]

    def chip_sum(self, layer, gathered, mine):
        return [_chip_sum(self.place, a, b, name=f"reduce_chip_sum{layer}_{k}") for k, (a, b) in enumerate(zip(gathered, mine))]

    def layer_weights(self, layer, s_in, s_out):
        own = lambda slabs, shard: jnp.stack([jnp.where(self.chip == s, shard, slabs[s]) for s in range(4)])
        wp, wpt = _assemble_w_in(own(s_in, self.w_in16[layer]), name=f"assemble_w_in{layer}")
        wo = own(s_out, self.w_out16[layer]).reshape(D, D)
        return wp, wpt, wo, wo.T


def _local_step(x, tgt, norm_w, final_norm_w, consts, wts0, wts1=None, comm=None):
    nw = [norm_w[l:l + 1] for l in range(2)]
    proj0, h0, slabs = _rmsproj(x, nw[0], wts0[0], name="rmsproj0", rider=comm and comm.gather_ici(1))
    (mix0, sg0, ss0, x1), slabs = _mixer_fwd(proj0, x, wts0[2], *consts[0], name="mixer_fwd0",
                                             rider=comm and _rider_gather_d2d(slabs))
    if comm:
        wts1 = comm.layer_weights(1, *slabs)
    proj1, h1, _ = _rmsproj(x1, nw[1], wts1[0], name="rmsproj1")
    (mix1, sg1, ss1, dx, head), _ = _mixer_fwd(proj1, x1, wts1[2], *consts[1], name="mixer_fwd1",
                                               head=(tgt, final_norm_w.reshape(1, D)))
    (dproj, mgr1, dwo1), _ = _mixer_bwd(proj1, dx, wts1[3], mix1, sg1, ss1, *consts[1], name="mixer_bwd1")
    dwp1, _ = _dwin(h1, dproj, name="dwin1")
    slabs1 = comm and _grad_slabs(dwp1, dwo1)
    (dx, dnw1), recv = _dxin(dproj, wts1[1], x1, dx, nw[1], name="dxin1", rider=comm and _rider_swap(slabs1))
    pairs1 = comm and comm.pair_sum(1, slabs1, recv)
    (dproj, mgr0, dwo0), gathered = _mixer_bwd(proj0, dx, wts0[3], mix0, sg0, ss0, *consts[0], name="mixer_bwd0",
                                               rider=comm and _rider_scatter(pairs1))
    dwp0, big1 = _dwin(h0, dproj, name="dwin0", rider=comm and _rider_share(comm.chip_sum(1, gathered, pairs1)))
    scat = None
    if comm:
        slabs0 = _grad_slabs(dwp0, dwo0)
        pairs0 = comm.pair_sum(0, slabs0, _run_rider(_rider_swap(slabs0), "reduce_swap0"))
        scat = _rider_scatter(pairs0)
    (dx, dnw0), gathered = _dxin(dproj, wts0[1], x, dx, nw[0], name="dxin0", rider=scat)
    if comm:
        big0 = _run_rider(_rider_share(comm.chip_sum(0, gathered, pairs0)), "reduce_share0")
        big = ((big0[0], big1[0]), (big0[1], big1[1]))
    else:
        big = ((dwp0, dwp1), (dwo0, dwo1))
    return head, dx, big, (dnw0, dnw1), (mgr0, mgr1)


def kernel(x, norm_w, w_in, conv_a_w, gla_gate_w, gla_gate_b, gla_norm_w, pool_w, pool_scale, ssd_conv_w, ssd_conv_b, ssd_dt_bias, ssd_a_log, ssd_d, ssd_norm_w, w_out, final_norm_w, loss_target, m_norm_w, m_w_in, m_conv_a_w, m_gla_gate_w, m_gla_gate_b, m_gla_norm_w, m_pool_w, m_pool_scale, m_ssd_conv_w, m_ssd_conv_b, m_ssd_dt_bias, m_ssd_a_log, m_ssd_d, m_ssd_norm_w, m_w_out, m_final_norm_w, v_norm_w, v_w_in, v_conv_a_w, v_gla_gate_w, v_gla_gate_b, v_gla_norm_w, v_pool_w, v_pool_scale, v_ssd_conv_w, v_ssd_conv_b, v_ssd_dt_bias, v_ssd_a_log, v_ssd_d, v_ssd_norm_w, v_w_out, v_final_norm_w):
    weights = dict(norm_w=norm_w, w_in=w_in, conv_a_w=conv_a_w, gla_gate_w=gla_gate_w, gla_gate_b=gla_gate_b,
                   gla_norm_w=gla_norm_w, pool_w=pool_w, pool_scale=pool_scale, ssd_conv_w=ssd_conv_w,
                   ssd_conv_b=ssd_conv_b, ssd_dt_bias=ssd_dt_bias, ssd_a_log=ssd_a_log, ssd_d=ssd_d,
                   ssd_norm_w=ssd_norm_w, w_out=w_out, final_norm_w=final_norm_w)
    m_in = dict(norm_w=m_norm_w, w_in=m_w_in, conv_a_w=m_conv_a_w, gla_gate_w=m_gla_gate_w, gla_gate_b=m_gla_gate_b,
                gla_norm_w=m_gla_norm_w, pool_w=m_pool_w, pool_scale=m_pool_scale, ssd_conv_w=m_ssd_conv_w,
                ssd_conv_b=m_ssd_conv_b, ssd_dt_bias=m_ssd_dt_bias, ssd_a_log=m_ssd_a_log, ssd_d=m_ssd_d,
                ssd_norm_w=m_ssd_norm_w, w_out=m_w_out, final_norm_w=m_final_norm_w)
    v_in = dict(norm_w=v_norm_w, w_in=v_w_in, conv_a_w=v_conv_a_w, gla_gate_w=v_gla_gate_w, gla_gate_b=v_gla_gate_b,
                gla_norm_w=v_gla_norm_w, pool_w=v_pool_w, pool_scale=v_pool_scale, ssd_conv_w=v_ssd_conv_w,
                ssd_conv_b=v_ssd_conv_b, ssd_dt_bias=v_ssd_dt_bias, ssd_a_log=v_ssd_a_log, ssd_d=v_ssd_d,
                ssd_norm_w=v_ssd_norm_w, w_out=v_w_out, final_norm_w=v_final_norm_w)
    order = ("norm_w", "w_in", "conv_a_w", "gla_gate_w", "gla_gate_b", "gla_norm_w", "pool_w", "pool_scale",
             "ssd_conv_w", "ssd_conv_b", "ssd_dt_bias", "ssd_a_log", "ssd_d", "ssd_norm_w", "w_out", "final_norm_w")
    t = x.shape[1]

    comm = _Comm(w_in, w_out)
    cshard = jnp.zeros((16, 256), F32)
    for l in range(2):
        cshard = cshard.at[8 * l:8 * l + 3, 0:64].set(conv_a_w[l]).at[8 * l + 3:8 * l + 7, 0:192].set(ssd_conv_w[l])
    s_in, s_out, g_c = _run_rider(comm.gather_ici(0, cshard), "gather_ici0")
    s_in, s_out = _run_rider(_rider_gather_d2d((s_in, s_out)), "gather_d2d0")
    g_c = [jnp.where(comm.chip == s, cshard, g_c[s]) for s in range(4)]
    conv_a_full = jnp.stack([jnp.concatenate([g_c[s][8 * l:8 * l + 3, 0:64] for s in range(4)], axis=-1) for l in range(2)])
    ssd_conv_full = jnp.stack([jnp.concatenate([g_c[s][8 * l + 3:8 * l + 7, 0:192] for s in range(4)], axis=-1)
                               for l in range(2)])
    consts = [_mixer_consts(l, conv_a_full, gla_gate_w, gla_gate_b, gla_norm_w, pool_w, pool_scale, ssd_conv_full,
                            ssd_conv_b, ssd_dt_bias, ssd_a_log, ssd_d, ssd_norm_w) for l in range(2)]

    head, dx, big, dnw, mgr = _local_step(x.reshape(t, D), loss_target.reshape(t, D), norm_w, final_norm_w, consts,
                                          comm.layer_weights(0, s_in, s_out), comm=comm)

    as2d = lambda d: {k: (d[k].reshape(1, D) if k == "final_norm_w" else d[k]) for k in _SMALL_NAMES}
    small = _small_adamw(_small_allreduce(mgr[0], mgr[1], dnw[0], dnw[1], head), as2d(weights), as2d(m_in), as2d(v_in))
    grads, delta, new_m, new_v = ({k: (a.reshape(D) if k == "final_norm_w" else a) for k, a in zip(_SMALL_NAMES, part)}
                                  for part in small[0:4])
    loss = small[4].reshape(())

    grads["w_out"] = jnp.stack(big[1])

    grads["w_in"], delta["w_in"], new_m["w_in"], new_v["w_in"] = _adamw_w_in(w_in, big[0], m_w_in, v_w_in, name="adamw_w_in")
    delta["w_out"], new_m["w_out"], new_v["w_out"] = _adamw(w_out, grads["w_out"], m_w_out, v_w_out, name="adamw_w_out", br=256)

    return (loss, dx.reshape(1, t, D), *[grads[k] for k in order], *[delta[k] for k in order],
            *[new_m[k] for k in order], *[new_v[k] for k in order])
```

```python
import functools

import jax
import jax.numpy as jnp
from jax import lax
from jax.experimental import pallas as pl
from jax.experimental.pallas import tpu as pltpu

F32 = jnp.float32
BF16 = jnp.bfloat16
MESH = pl.DeviceIdType.MESH

D = 1024
CH = 64
EPS = 1e-6
NP = 3456
NPROJ = 3348
NPM = 3328
GLA_SCALE = 32.0 ** -0.5
INV_TAU = 1.0 / 16.0
TB = 512
NCH = TB // CH
assert TB % 256 == 0

C_AH, C_AB, C_AC, C_AZ, C_GQ, C_GK, C_GV = 0, 256, 512, 768, 1024, 1152, 1280
C_GZ, C_PU, C_PZ, C_SZ, C_SX, C_TL = 1536, 1792, 2048, 2304, 2560, 3328
_PERM = ((0, 1536), (1552, 1792), (1536, 16), (3344, 4))
_UNPERM = ((0, 1536), (3328, 16), (1536, 1792), (3344, 4))

R_CAW, R_GB, R_GNW, R_PSC, R_SCB, R_DTB, R_AE, R_DE, R_SNW, R_SCW = 0, 3, 4, 5, 6, 7, 8, 9, 10, 12

ADAM_LR, ADAM_B1, ADAM_B2, ADAM_EPS, ADAM_WD, ADAM_STEP = 0.001, 0.9, 0.999, 1e-08, 0.01, 10

VMEM_LIMIT = 56 * 1024 * 1024


def _cparams(sem, limit=VMEM_LIMIT):
    return pltpu.CompilerParams(dimension_semantics=sem, vmem_limit_bytes=limit)


_ANY = pl.BlockSpec(memory_space=pl.ANY)


def _place():
    return lax.axis_index("x"), lax.axis_index("y"), lax.axis_index("c")


class _Rider:
    def __init__(self, inputs, out_shapes, sems, start, finish, aliases=None):
        self.inputs, self.out_shapes, self.sems = tuple(inputs), tuple(out_shapes), tuple(sems)
        self.start, self.finish, self.aliases = start, finish, dict(aliases or {})


def _call(body, args, *, grid, in_specs, out_specs, out_shape, name, sem, scratch_shapes=(), rider=None, prefetch=None):
    ni, no, ns = len(args), len(out_shape), len(scratch_shapes)
    ri, ro = (len(rider.inputs), len(rider.out_shapes)) if rider else (0, 0)
    np_ = 0 if prefetch is None else 1

    def full(*refs):
        refs = refs[np_:]
        ins, rins = refs[:ni], refs[ni:ni + ri]
        outs, routs = refs[ni + ri:ni + ri + no], refs[ni + ri + no:ni + ri + no + ro]
        scr, rsem = refs[ni + ri + no + ro:ni + ri + no + ro + ns], refs[ni + ri + no + ro + ns:]
        first = functools.reduce(jnp.logical_and, [pl.program_id(a) == 0 for a in range(len(grid))])
        last = functools.reduce(jnp.logical_and, [pl.program_id(a) == grid[a] - 1 for a in range(len(grid))])

        if rider:
            @pl.when(first)
            def _():
                rider.start(rins, routs, rsem)

        body(*ins, *outs, *scr)

        if rider:
            @pl.when(last)
            def _():
                rider.finish(rins, routs, rsem)

    spec = dict(grid=grid, in_specs=list(in_specs) + [_ANY] * ri, out_specs=list(out_specs) + [_ANY] * ro,
                scratch_shapes=list(scratch_shapes) + (list(rider.sems) if rider else []))
    if prefetch is not None:
        spec = dict(grid_spec=pltpu.PrefetchScalarGridSpec(num_scalar_prefetch=1, **spec))
    outs = pl.pallas_call(
        full, name=name, out_shape=list(out_shape) + (list(rider.out_shapes) if rider else []),
        input_output_aliases={np_ + ni + k: no + v for k, v in rider.aliases.items()} if rider else {},
        compiler_params=_cparams(("arbitrary",) * len(grid) if rider else sem), **spec,
    )(*(() if prefetch is None else (prefetch,)), *args, *(rider.inputs if rider else ()))
    return list(outs[:no]), list(outs[no:])


def _run_rider(rider, name):
    ri = len(rider.inputs)

    def body(*refs):
        rins, routs, rsem = refs[:ri], refs[ri:ri + len(rider.out_shapes)], refs[ri + len(rider.out_shapes):]
        rider.start(rins, routs, rsem)
        rider.finish(rins, routs, rsem)

    return list(pl.pallas_call(body, name=name, in_specs=[_ANY] * ri, out_specs=[_ANY] * len(rider.out_shapes),
                               out_shape=list(rider.out_shapes), scratch_shapes=list(rider.sems),
                               input_output_aliases=dict(rider.aliases))(*rider.inputs))


def _dot(a, b):
    return jnp.dot(a.astype(BF16), b.astype(BF16), preferred_element_type=F32)


def _dot_nt(a, b):
    return lax.dot_general(a.astype(BF16), b.astype(BF16), (((1,), (1,)), ((), ())), preferred_element_type=F32)


def _dot_tn(a, b):
    return lax.dot_general(a.astype(BF16), b.astype(BF16), (((0,), (0,)), ((), ())), preferred_element_type=F32)


def _split(a):
    hi = a.astype(BF16)
    lo = (a - hi.astype(F32)).astype(BF16)
    return hi, lo


def _dot2_l(a, b):
    hi, lo = _split(a)
    return _dot(hi, b) + _dot(lo, b)


def _dot2_r(a, b):
    hi, lo = _split(b)
    return _dot(a, hi) + _dot(a, lo)


def _dot3_l(a, b):
    hi, lo = _split(a)
    lo2 = ((a - hi.astype(F32)) - lo.astype(F32)).astype(BF16)
    return _dot(hi, b) + _dot(lo, b) + _dot(lo2, b)


def _dot2_nt(a, b):
    hi, lo = _split(a)
    return _dot_nt(hi, b) + _dot_nt(lo, b)


def _silu(z):
    return z * jax.nn.sigmoid(z)


def _lse1(x):
    return jnp.log(1.0 + jnp.exp(-jnp.abs(x)))


def _cs(a):
    return jnp.sum(a, axis=0, keepdims=True)


def _iota(shape, dim):
    return lax.broadcasted_iota(jnp.int32, shape, dim)


def _mixer_matrices():
    r, c = _iota((256, 256), 0), _iota((256, 256), 1)
    same_chunk = (r >> 6) == (c >> 6)
    mats = jnp.stack([jnp.where((c > r) & same_chunk, 1.0, 0.0), jnp.where((c < r) & same_chunk, 1.0, 0.0),
                      jnp.where(same_chunk, 1.0 / 64.0, 0.0), jnp.where((r < 128) & (r - 16 == (c >> 6)), 1.0, 0.0)])
    mask = jnp.where((_iota((256, 128), 0) >> 6) == (_iota((256, 128), 1) >> 5), 1.0, 0.0)
    return mats.astype(BF16), mask.astype(F32)


def _dn(ext, k, n, h):
    return pltpu.roll(ext, k, axis=0)[h:h + n]


def _up(ext, k, n):
    return pltpu.roll(ext, ext.shape[0] - k, axis=0)[:n]


def _pool_lane_select(lane, s2, s4, s8, s16):
    return jnp.where(lane < 64, s2, jnp.where(lane < 128, s4, jnp.where(lane < 192, s8, s16)))


def _winsum_dn(ext, lane):
    s2 = ext + pltpu.roll(ext, 1, axis=0)
    s4 = s2 + pltpu.roll(s2, 2, axis=0)
    s8 = s4 + pltpu.roll(s4, 4, axis=0)
    s16 = s8 + pltpu.roll(s8, 8, axis=0)
    return _pool_lane_select(lane, s2, s4, s8, s16)


def _winsum_up(ext, lane):
    m = ext.shape[0]
    s2 = ext + pltpu.roll(ext, m - 1, axis=0)
    s4 = s2 + pltpu.roll(s2, m - 2, axis=0)
    s8 = s4 + pltpu.roll(s4, m - 4, axis=0)
    s16 = s8 + pltpu.roll(s8, m - 8, axis=0)
    return _pool_lane_select(lane, s2, s4, s8, s16)


def _pool_inv_count(tile, n):
    lane = _iota((1, 256), 1)
    win = _pool_lane_select(lane, 2.0, 4.0, 8.0, 16.0).astype(F32)
    tpos = (tile * n + _iota((n, 1), 0) + 1).astype(F32)
    return jnp.where(tpos >= win, 1.0 / win, 1.0 / tpos)


def _silu_pair(z):
    s = jax.nn.sigmoid(z)
    return z * s, s * (1.0 + z * (1.0 - s))


def _chunks(a):
    return [a[c * CH:(c + 1) * CH] for c in range(a.shape[0] // CH)]


def _halves(fn, a, b):
    return jnp.concatenate([fn(a[:, 0:128], b[:, 0:128]), fn(a[:, 128:256], b[:, 128:256])], axis=1)


def _chunk_sums(tri, a):
    return jnp.concatenate([_dot2_r(tri, a[r:r + 256]) for r in range(0, a.shape[0], 256)], axis=0)


def _mixer_tile_prep(p_ref, t_ref, xc, prm_ref, gw_v, cm_ref, mk_ref):
    tail = t_ref[...]
    pre = _dot(tail, gw_v) + prm_ref[R_GB:R_GB + 1, 0:128]
    la = (jnp.minimum(pre, 0.0) - _lse1(pre)) * INV_TAU
    dtin = tail + prm_ref[R_DTB:R_DTB + 1, 0:128]
    dtf = jnp.maximum(dtin, 0.0) + _lse1(dtin)
    dte = _dot2_l(dtf, cm_ref[3, 0:128, :])
    da = dte * prm_ref[R_AE:R_AE + 1, 0:256]
    rev = _chunk_sums(cm_ref[0], jnp.concatenate([la, da], axis=1))
    dec = jnp.exp(rev[:, 0:128])
    kd = p_ref[:, C_GK:C_GK + 128].astype(F32) * dec
    wdec = jnp.exp(rev[:, 128:384])
    w = wdec * dte
    xw = xc[:, 0:256] * w
    d_s = [jnp.exp(_cs(a)) for a in _chunks(la)]
    et = [jnp.exp(_cs(a)) for a in _chunks(da)]
    mask_t = mk_ref[...]
    ut_g = [_dot_tn(v, k) * mask_t for v, k in zip(_chunks(p_ref[:, C_GV:C_GV + 256].astype(F32)), _chunks(kd))]
    ut_s = [_halves(_dot_tn, b, x) for b, x in zip(_chunks(xc[:, 256:512]), _chunks(xw))]
    return tail, pre, dtin, dte, dec, kd, wdec, w, xw, d_s, et, ut_g, ut_s


def _rmsproj(x, nw, wp, name, tm=512, rider=None):
    t = x.shape[0]

    def body(x_ref, nw_ref, w_ref, o_ref, t_ref, h_ref):
        xv = x_ref[...]
        rs = lax.rsqrt(jnp.mean(xv * xv, axis=-1, keepdims=True) + EPS)
        h = (xv * rs * nw_ref[...]).astype(BF16)
        h_ref[...] = h
        proj = jnp.dot(h, w_ref[...], preferred_element_type=F32)
        o_ref[...] = proj[:, 0:NPM].astype(BF16)
        t_ref[...] = proj[:, NPM:NP]

    (proj, tail, h), extra = _call(
        body, (x, nw, wp), grid=(t // tm,), name=name, sem=("parallel",), rider=rider,
        in_specs=[pl.BlockSpec((tm, D), lambda i: (i, 0)), pl.BlockSpec((1, D), lambda i: (0, 0)),
                  pl.BlockSpec((D, NP), lambda i: (0, 0))],
        out_specs=[pl.BlockSpec((tm, NPM), lambda i: (i, 0)), pl.BlockSpec((tm, NP - NPM), lambda i: (i, 0)),
                   pl.BlockSpec((tm, D), lambda i: (i, 0))],
        out_shape=[jax.ShapeDtypeStruct((t, NPM), BF16), jax.ShapeDtypeStruct((t, NP - NPM), F32),
                   jax.ShapeDtypeStruct((t, D), BF16)])
    return (proj, tail), h, extra


def _head_tile(xv, tgt, w):
    rs = lax.rsqrt(jnp.mean(xv * xv, axis=-1, keepdims=True) + EPS)
    xh = xv * rs
    err = xh * w - tgt
    dy = err * (1.0 / D)
    dxh = dy * w
    dx = rs * (dxh - xh * jnp.mean(dxh * xh, axis=-1, keepdims=True))
    return dx, _cs(dy * xh), (0.5 / D) * jnp.sum(err * err)


def _dxin(dp, wpt, x, dxn, nw, name, tm=512, rider=None):
    t = x.shape[0]

    def body(dp_ref, w_ref, x_ref, dxn_ref, nw_ref, dx_ref, dnw_ref):
        @pl.when(pl.program_id(0) == 0)
        def _():
            dnw_ref[...] = jnp.zeros_like(dnw_ref)

        acc = jnp.zeros((1, D), F32)
        for rows in (pl.ds(0, tm // 2), pl.ds(tm // 2, tm // 2)):
            dh = jnp.dot(dp_ref[rows, :].astype(BF16), w_ref[...], preferred_element_type=F32)
            xv = x_ref[rows, :]
            rs = lax.rsqrt(jnp.mean(xv * xv, axis=-1, keepdims=True) + EPS)
            xh = xv * rs
            acc = acc + _cs(dh * xh)
            dxh = dh * nw_ref[...]
            dx_ref[rows, :] = dxn_ref[rows, :] + rs * (dxh - xh * jnp.mean(dxh * xh, axis=-1, keepdims=True))
        dnw_ref[0:1, :] += acc

    return _call(
        body, (dp, wpt, x, dxn, nw), grid=(t // tm,), name=name, sem=("arbitrary",), rider=rider,
        in_specs=[pl.BlockSpec((tm, NP), lambda i: (i, 0)), pl.BlockSpec((NP, D), lambda i: (0, 0)),
                  pl.BlockSpec((tm, D), lambda i: (i, 0)), pl.BlockSpec((tm, D), lambda i: (i, 0)),
                  pl.BlockSpec((1, D), lambda i: (0, 0))],
        out_specs=[pl.BlockSpec((tm, D), lambda i: (i, 0)), pl.BlockSpec((8, D), lambda i: (0, 0))],
        out_shape=[jax.ShapeDtypeStruct((t, D), F32), jax.ShapeDtypeStruct((8, D), F32)])


def _dwin(h, dp, name, tm=1024, rider=None, half=None):
    t = h.shape[0]
    rows = D // 2 if half else D

    def body(h_ref, dp_ref, o_ref):
        @pl.when(pl.program_id(0) == 0)
        def _():
            o_ref[...] = jnp.zeros_like(o_ref)

        o_ref[...] += _dot_tn(h_ref[...], dp_ref[...])

    col = (lambda i, c: (i, c[0] if half[1] == 0 else 1 - c[0])) if half else (lambda i: (i, 0))
    fix, tok = (lambda i, *c: (0, 0)), (lambda i, *c: (i, 0))
    (dwp,), extra = _call(
        body, (h, dp), grid=(t // tm,), name=name, sem=("arbitrary",), rider=rider, prefetch=half and half[0],
        in_specs=[pl.BlockSpec((tm, rows), col), pl.BlockSpec((tm, NP), tok)],
        out_specs=[pl.BlockSpec((rows, NP), fix)], out_shape=[jax.ShapeDtypeStruct((rows, NP), F32)])
    return dwp, extra


def _mixer_fwd(proj, x, wo, prm, gw, pw, cmat, mask, name, rider=None, head=None):
    proj, tail = proj
    t = proj.shape[0]
    nt, nc = t // TB, t // CH

    def body(p_ref, t_ref, x_ref, wo_ref, prm_ref, gw_ref, pw_ref, cm_ref, mk_ref, *rest):
        (tgt_ref, fw_ref), rest = (rest[:2], rest[2:]) if head else ((None, None), rest)
        mix_ref, sg_ref, ss_ref, xn_ref = rest[:4]
        acc_ref = rest[4] if head else None
        sg_s, ss_s, h_ua, h_pu, h_sx = rest[-5:]
        i = pl.program_id(0)

        @pl.when(i == 0)
        def _():
            for r in (sg_s, ss_s, h_ua, h_pu, h_sx) + ((acc_ref,) if head else ()):
                r[...] = jnp.zeros_like(r)

        lane = _iota((1, 256), 1)
        u = p_ref[:, C_AC:C_AC + 256].astype(F32) * p_ref[:, C_AH:C_AH + 256].astype(F32)
        ext = jnp.concatenate([h_ua[...], u], axis=0)
        cv = (prm_ref[R_CAW + 2:R_CAW + 3, 0:256] * u + prm_ref[R_CAW + 1:R_CAW + 2, 0:256] * _dn(ext, 1, TB, 8)
              + prm_ref[R_CAW:R_CAW + 1, 0:256] * _dn(ext, 2, TB, 8))
        mix_ref[:, 0:256] = (p_ref[:, C_AB:C_AB + 256].astype(F32) * cv * _silu(p_ref[:, C_AZ:C_AZ + 256].astype(F32))).astype(BF16)
        h_ua[...] = u[TB - 8:, :]
        pu = p_ref[:, C_PU:C_PU + 256].astype(F32)
        ext = jnp.concatenate([h_pu[...], pu], axis=0)
        pooled = _winsum_dn(ext, lane)[16:] * _pool_inv_count(i, TB) - pu
        mixed = _dot(pooled, pw_ref[...])
        mix_ref[:, 512:768] = (prm_ref[R_PSC:R_PSC + 1, 0:256] * mixed * _silu(p_ref[:, C_PZ:C_PZ + 256].astype(F32))).astype(BF16)
        h_pu[...] = pu[TB - 16:, :]
        sx = p_ref[:, C_SX:C_SX + 768].astype(F32)
        ext = jnp.concatenate([h_sx[...], sx], axis=0)
        xc = _silu(prm_ref[R_SCW + 3:R_SCW + 4, :] * sx + prm_ref[R_SCW + 2:R_SCW + 3, :] * _dn(ext, 1, TB, 8)
                   + prm_ref[R_SCW + 1:R_SCW + 2, :] * _dn(ext, 2, TB, 8) + prm_ref[R_SCW:R_SCW + 1, :] * _dn(ext, 3, TB, 8)
                   + prm_ref[R_SCB:R_SCB + 1, :])
        h_sx[...] = sx[TB - 8:, :]

        _, _, _, _, _, _, _, _, _, d_s, et, ut_g, ut_s = _mixer_tile_prep(p_ref, t_ref, xc, prm_ref, gw_ref[...], cm_ref, mk_ref)
        s_g, s_s = sg_s[...], ss_s[...]
        o, y = [], []
        qs = _chunks(p_ref[:, C_GQ:C_GQ + 128].astype(F32) * GLA_SCALE)
        cm = _chunks(xc[:, 512:768])
        for c in range(NCH):
            sg_ref[c] = s_g
            ss_ref[c] = s_s
            s_g = s_g * d_s[c] + ut_g[c]
            s_s = s_s * et[c] + ut_s[c]
            o.append(_dot_nt(qs[c], s_g))
            y.append(_halves(_dot, cm[c], s_s))
        sg_s[...] = s_g
        ss_s[...] = s_s
        o = jnp.concatenate(o, axis=0)
        on = o * lax.rsqrt(_dot2_l(o * o, cm_ref[2]) + EPS)
        mix_ref[:, 256:512] = (on * prm_ref[R_GNW:R_GNW + 1, 0:256] * _silu(p_ref[:, C_GZ:C_GZ + 256].astype(F32))).astype(BF16)
        y2 = ((jnp.concatenate(y, axis=0) + prm_ref[R_DE:R_DE + 1, 0:256] * xc[:, 0:256])
              * _silu(p_ref[:, C_SZ:C_SZ + 256].astype(F32)))
        mix_ref[:, 768:1024] = (y2 * lax.rsqrt(jnp.mean(y2 * y2, axis=-1, keepdims=True) + EPS)
                                * prm_ref[R_SNW:R_SNW + 1, 0:256]).astype(BF16)
        xn = x_ref[...] + jnp.dot(mix_ref[...], wo_ref[...], preferred_element_type=F32)
        if head:
            xn_ref[...], dfw, loss = _head_tile(xn, tgt_ref[...], fw_ref[...])
            acc_ref[0:1, :] += dfw
            acc_ref[1:2, :] += jnp.zeros((1, D), F32) + loss
        else:
            xn_ref[...] = xn

    row = pl.BlockSpec((TB, D), lambda i: (i, 0))
    return _call(
        body, (proj, tail, x, wo, prm, gw, pw, cmat, mask) + tuple(head or ()), grid=(nt,), name=name, sem=("arbitrary",),
        rider=rider,
        in_specs=[pl.BlockSpec((TB, NPM), lambda i: (i, 0)), pl.BlockSpec((TB, NP - NPM), lambda i: (i, 0)), row,
                  pl.BlockSpec((D, D), lambda i: (0, 0)), pl.BlockSpec((16, 768), lambda i: (0, 0)),
                  pl.BlockSpec((128, 128), lambda i: (0, 0)), pl.BlockSpec((256, 256), lambda i: (0, 0)),
                  pl.BlockSpec((4, 256, 256), lambda i: (0, 0, 0)), pl.BlockSpec((256, 128), lambda i: (0, 0))]
        + ([row, pl.BlockSpec((1, D), lambda i: (0, 0))] if head else []),
        out_specs=[row, pl.BlockSpec((NCH, 256, 128), lambda i: (i, 0, 0)),
                   pl.BlockSpec((NCH, 128, 256), lambda i: (i, 0, 0)), row]
        + ([pl.BlockSpec((8, D), lambda i: (0, 0))] if head else []),
        out_shape=[jax.ShapeDtypeStruct((t, D), BF16), jax.ShapeDtypeStruct((nc, 256, 128), F32),
                   jax.ShapeDtypeStruct((nc, 128, 256), F32), jax.ShapeDtypeStruct((t, D), F32)]
        + ([jax.ShapeDtypeStruct((8, D), F32)] if head else []),
        scratch_shapes=[pltpu.VMEM((256, 128), F32), pltpu.VMEM((128, 256), F32), pltpu.VMEM((8, 256), F32),
                        pltpu.VMEM((16, 256), F32), pltpu.VMEM((8, 768), F32)])


def _mixer_bwd(proj, dxn, wot, mix, sg, ss, prm, gw, pw, cmat, mask, name, rider=None):
    proj, tail = proj
    t = proj.shape[0]
    nt = t // TB
    rev = lambda i: nt - 1 - i

    def body(p_ref, hp_ref, t_ref, dxn_ref, wot_ref, mix_ref, sg_ref, ss_ref, prm_ref, gw_ref, pw_ref, cm_ref, mk_ref,
             dp_ref, sgc_ref, dwo_ref,
             gg_s, gs_s, h_dcv, h_dpl, h_dpre, gsm_ref, dgw_ref, dpw_ref, dm_ref):
        i = pl.program_id(0)
        tile = nt - 1 - i

        @pl.when(i == 0)
        def _():
            for r in (gg_s, gs_s, h_dcv, h_dpl, h_dpre, gsm_ref, dgw_ref, dpw_ref, dwo_ref):
                r[...] = jnp.zeros_like(r)

        dxn = dxn_ref[...].astype(BF16)
        dm_ref[...] = jnp.dot(dxn, wot_ref[...], preferred_element_type=F32)
        dwo_ref[...] += _dot_tn(mix_ref[...], dxn)

        lane = _iota((1, 256), 1)
        first = (tile > 0).astype(F32)
        ah, ac = p_ref[:, C_AH:C_AH + 256].astype(F32), p_ref[:, C_AC:C_AC + 256].astype(F32)
        ab, az = p_ref[:, C_AB:C_AB + 256].astype(F32), p_ref[:, C_AZ:C_AZ + 256].astype(F32)
        w0, w1, w2 = (prm_ref[R_CAW + j:R_CAW + j + 1, 0:256] for j in range(3))
        u = ac * ah
        ext = jnp.concatenate([(hp_ref[:, C_AC:C_AC + 256].astype(F32) * hp_ref[:, C_AH:C_AH + 256].astype(F32))[8:16] * first, u], axis=0)
        u1, u2 = _dn(ext, 1, TB, 8), _dn(ext, 2, TB, 8)
        cv = w2 * u + w1 * u1 + w0 * u2
        g = dm_ref[:, 0:256]
        sz, dsz = _silu_pair(az)
        dp_ref[:, C_AB:C_AB + 256] = (g * cv * sz).astype(BF16)
        dp_ref[:, C_AZ:C_AZ + 256] = (g * ab * cv * dsz).astype(BF16)
        dcv = g * ab * sz
        dext = jnp.concatenate([dcv, h_dcv[...]], axis=0)
        du = w2 * dcv + w1 * _up(dext, 1, TB) + w0 * _up(dext, 2, TB)
        dp_ref[:, C_AC:C_AC + 256] = (du * ah).astype(BF16)
        dp_ref[:, C_AH:C_AH + 256] = (du * ac).astype(BF16)
        gsm_ref[R_CAW:R_CAW + 1, 0:256] += _cs(dcv * u2)
        gsm_ref[R_CAW + 1:R_CAW + 2, 0:256] += _cs(dcv * u1)
        gsm_ref[R_CAW + 2:R_CAW + 3, 0:256] += _cs(dcv * u)
        h_dcv[...] = dcv[0:8, :]
        pu, pz = p_ref[:, C_PU:C_PU + 256].astype(F32), p_ref[:, C_PZ:C_PZ + 256].astype(F32)
        psc = prm_ref[R_PSC:R_PSC + 1, 0:256]
        icnt = _pool_inv_count(tile, TB)
        ext = jnp.concatenate([hp_ref[:, C_PU:C_PU + 256].astype(F32) * first, pu], axis=0)
        pooled = _winsum_dn(ext, lane)[16:] * icnt - pu
        pw_v = pw_ref[...]
        mixed = _dot(pooled, pw_v)
        g = dm_ref[:, 512:768]
        sz, dsz = _silu_pair(pz)
        gsm_ref[R_PSC:R_PSC + 1, 0:256] += _cs(g * mixed * sz)
        dp_ref[:, C_PZ:C_PZ + 256] = (g * psc * mixed * dsz).astype(BF16)
        dmixed = g * psc * sz
        dpw_ref[...] += _dot_tn(pooled, dmixed)
        dpooled = _dot_nt(dmixed, pw_v)
        qd = dpooled * icnt
        dext = jnp.concatenate([qd, h_dpl[...]], axis=0)
        dp_ref[:, C_PU:C_PU + 256] = (_winsum_up(dext, lane)[:TB] - dpooled).astype(BF16)
        h_dpl[...] = qd[0:16, :]
        sx = p_ref[:, C_SX:C_SX + 768].astype(F32)
        cw = [prm_ref[R_SCW + j:R_SCW + j + 1, :] for j in range(4)]
        ext = jnp.concatenate([hp_ref[:, C_SX:C_SX + 768].astype(F32)[8:16] * first, sx], axis=0)
        sx1, sx2, sx3 = _dn(ext, 1, TB, 8), _dn(ext, 2, TB, 8), _dn(ext, 3, TB, 8)
        cpre = cw[3] * sx + cw[2] * sx1 + cw[1] * sx2 + cw[0] * sx3 + prm_ref[R_SCB:R_SCB + 1, :]
        xc, dxc = _silu_pair(cpre)
        xs, bm, cm = xc[:, 0:256], xc[:, 256:512], xc[:, 512:768]

        gw_v = gw_ref[...]
        tail, pre, dtin, dte, dec, kd, wdec, w, xw, d_s, et, ut_g, ut_s = _mixer_tile_prep(p_ref, t_ref, xc, prm_ref,
                                                                                          gw_v, cm_ref, mk_ref)
        gmean = cm_ref[2]
        mask_t = mk_ref[...]
        gnw = prm_ref[R_GNW:R_GNW + 1, 0:256]
        a_e = prm_ref[R_AE:R_AE + 1, 0:256]
        d_e = prm_ref[R_DE:R_DE + 1, 0:256]
        snw = prm_ref[R_SNW:R_SNW + 1, 0:256]
        sg_in = [sg_ref[c] for c in range(NCH)]
        ss_in = [ss_ref[c] for c in range(NCH)]
        sg_n = [sg_in[c] * d_s[c] + ut_g[c] for c in range(NCH)]
        ss_n = [ss_in[c] * et[c] + ut_s[c] for c in range(NCH)]
        qs = _chunks(p_ref[:, C_GQ:C_GQ + 128].astype(F32) * GLA_SCALE)
        cm_c, bm_c, xw_c, kd_c = _chunks(cm), _chunks(bm), _chunks(xw), _chunks(kd)
        v_c = _chunks(p_ref[:, C_GV:C_GV + 256].astype(F32))
        o = jnp.concatenate([_dot_nt(qs[c], sg_n[c]) for c in range(NCH)], axis=0)
        y = jnp.concatenate([_halves(_dot, cm_c[c], ss_n[c]) for c in range(NCH)], axis=0) + d_e * xs
        gz = p_ref[:, C_GZ:C_GZ + 256].astype(F32)
        r = lax.rsqrt(_dot2_l(o * o, gmean) + EPS)
        on = o * r
        dyb = dm_ref[:, 256:512]
        sz, dsz = _silu_pair(gz)
        dp_ref[:, C_GZ:C_GZ + 256] = (dyb * on * gnw * dsz).astype(BF16)
        tg = dyb * sz
        gsm_ref[R_GNW:R_GNW + 1, 0:256] += _cs(tg * on)
        don = tg * gnw
        do_c = _chunks(r * (don - on * _dot2_l(don * on, gmean)))
        ssz = p_ref[:, C_SZ:C_SZ + 256].astype(F32)
        sil, dsil = _silu_pair(ssz)
        y2 = y * sil
        r = lax.rsqrt(jnp.mean(y2 * y2, axis=-1, keepdims=True) + EPS)
        yn = y2 * r
        dyd = dm_ref[:, 768:1024]
        gsm_ref[R_SNW:R_SNW + 1, 0:256] += _cs(dyd * yn)
        dn = dyd * snw
        dy2 = r * (dn - yn * jnp.mean(dn * yn, axis=-1, keepdims=True))
        dp_ref[:, C_SZ:C_SZ + 256] = (dy2 * y * dsil).astype(BF16)
        dy = dy2 * sil
        gsm_ref[R_DE:R_DE + 1, 0:256] += _cs(dy * xs)
        dy_c = _chunks(dy)
        dq = jnp.concatenate([_dot(do_c[c], sg_n[c]) for c in range(NCH)], axis=0)
        dp_ref[:, C_GQ:C_GQ + 128] = (dq * GLA_SCALE).astype(BF16)
        dcm = jnp.concatenate([_halves(_dot_nt, dy_c[c], ss_n[c]) for c in range(NCH)], axis=0)
        gg = [_dot_tn(do_c[c], qs[c]) * mask_t for c in range(NCH)]
        gs = [_halves(_dot_tn, cm_c[c], dy_c[c]) for c in range(NCH)]
        car_g, car_s = gg_s[...], gs_s[...]
        for c in reversed(range(NCH)):
            gg[c] = gg[c] + car_g
            gs[c] = gs[c] + car_s
            car_g = gg[c] * d_s[c]
            car_s = gs[c] * et[c]
        gg_s[...] = car_g
        gs_s[...] = car_s
        dkd = jnp.concatenate([_dot(v_c[c], gg[c]) for c in range(NCH)], axis=0)
        dp_ref[:, C_GV:C_GV + 256] = jnp.concatenate([_dot_nt(kd_c[c], gg[c]) for c in range(NCH)], axis=0).astype(BF16)
        dp_ref[:, C_GK:C_GK + 128] = (dkd * dec).astype(BF16)
        dbm = jnp.concatenate([_halves(_dot_nt, xw_c[c], gs[c]) for c in range(NCH)], axis=0)
        dxw = jnp.concatenate([_halves(_dot, bm_c[c], gs[c]) for c in range(NCH)], axis=0)
        dxs = dy * d_e + dxw * w
        dw = dxw * xs
        dsuf = _chunk_sums(cm_ref[1], jnp.concatenate([dkd * kd, dw * dte * wdec], axis=1))
        tot_g = jnp.concatenate([jnp.broadcast_to(_cs(gg[c] * sg_in[c]) * d_s[c], (CH, 128)) for c in range(NCH)], axis=0)
        tot_s = jnp.concatenate([jnp.broadcast_to(_cs(gs[c] * ss_in[c]) * et[c], (CH, 256)) for c in range(NCH)], axis=0)
        dpre = (dsuf[:, 0:128] + tot_g) * INV_TAU * jax.nn.sigmoid(-pre)
        dgw_ref[...] += _dot_tn(tail, dpre)
        gsm_ref[R_GB:R_GB + 1, 0:128] += _cs(dpre)
        dda = dsuf[:, 128:384] + tot_s
        gsm_ref[R_AE:R_AE + 1, 0:256] += _cs(dda * dte)
        dtail_s = _dot2_nt(dw * wdec + dda * a_e, cm_ref[3, 0:128, :]) * jax.nn.sigmoid(dtin)
        gsm_ref[R_DTB:R_DTB + 1, 0:128] += _cs(dtail_s)
        dp_ref[:, C_TL:C_TL + 128] = (_dot_nt(dpre, gw_v) + dtail_s).astype(BF16)
        dpre_c = jnp.concatenate([dxs, dbm, dcm], axis=1) * dxc
        dext = jnp.concatenate([dpre_c, h_dpre[...]], axis=0)
        dp_ref[:, C_SX:C_SX + 768] = (cw[3] * dpre_c + cw[2] * _up(dext, 1, TB) + cw[1] * _up(dext, 2, TB)
                                      + cw[0] * _up(dext, 3, TB)).astype(BF16)
        gsm_ref[R_SCW + 3:R_SCW + 4, :] += _cs(dpre_c * sx)
        gsm_ref[R_SCW + 2:R_SCW + 3, :] += _cs(dpre_c * sx1)
        gsm_ref[R_SCW + 1:R_SCW + 2, :] += _cs(dpre_c * sx2)
        gsm_ref[R_SCW:R_SCW + 1, :] += _cs(dpre_c * sx3)
        gsm_ref[R_SCB:R_SCB + 1, :] += _cs(dpre_c)
        h_dpre[...] = dpre_c[0:8, :]

        @pl.when(i == nt - 1)
        def _():
            ri, ci = _iota((256, 256), 0), _iota((256, 256), 1)
            per_head = jnp.where((ri >> 6) == ci, 1.0, 0.0).astype(BF16)
            per_dv = jnp.where((ri & 63) == ci, 1.0, 0.0).astype(BF16)
            row = _iota((8, 256), 0)
            top = gsm_ref[0:8, 0:256]
            sgc_ref[0:8, 0:256] = jnp.where(row == R_GNW, _dot3_l(top, per_dv), top)
            bot = gsm_ref[8:16, 0:256]
            fold = _dot3_l(jnp.where(row == R_AE - 8, bot * a_e, bot), per_head)
            sgc_ref[8:16, 0:256] = jnp.where((row == R_AE - 8) | (row == R_DE - 8), fold, bot)
            sgc_ref[0:16, 256:768] = gsm_ref[:, 256:768]
            sgc_ref[0:16, 768:896] = dgw_ref[0:16, :]
            sgc_ref[0:16, 896:1024] = jnp.zeros((16, 128), F32)
            diag = _pool_lane_select(lane, dpw_ref[0:64, :], dpw_ref[64:128, :], dpw_ref[128:192, :], dpw_ref[192:256, :])
            for q in range(4):
                sgc_ref[16:32, 256 * q:256 * q + 256] = diag[16 * q:16 * q + 16, :]

    return _call(
        body, (proj, proj, tail, dxn, wot, mix, sg, ss, prm, gw, pw, cmat, mask), grid=(nt,), name=name,
        sem=("arbitrary",), rider=rider,
        in_specs=[pl.BlockSpec((TB, NPM), lambda i: (rev(i), 0)),
                  pl.BlockSpec((16, NPM), lambda i: (jnp.maximum(rev(i) * (TB // 16) - 1, 0), 0)),
                  pl.BlockSpec((TB, NP - NPM), lambda i: (rev(i), 0)),
                  pl.BlockSpec((TB, D), lambda i: (rev(i), 0)), pl.BlockSpec((D, D), lambda i: (0, 0)),
                  pl.BlockSpec((TB, D), lambda i: (rev(i), 0)),
                  pl.BlockSpec((NCH, 256, 128), lambda i: (rev(i), 0, 0)),
                  pl.BlockSpec((NCH, 128, 256), lambda i: (rev(i), 0, 0)),
                  pl.BlockSpec((16, 768), lambda i: (0, 0)), pl.BlockSpec((128, 128), lambda i: (0, 0)),
                  pl.BlockSpec((256, 256), lambda i: (0, 0)), pl.BlockSpec((4, 256, 256), lambda i: (0, 0, 0)),
                  pl.BlockSpec((256, 128), lambda i: (0, 0))],
        out_specs=[pl.BlockSpec((TB, NP), lambda i: (rev(i), 0)), pl.BlockSpec((32, 1024), lambda i: (0, 0)),
                   pl.BlockSpec((D, D), lambda i: (0, 0))],
        out_shape=[jax.ShapeDtypeStruct((t, NP), BF16), jax.ShapeDtypeStruct((32, 1024), F32),
                   jax.ShapeDtypeStruct((D, D), F32)],
        scratch_shapes=[pltpu.VMEM((256, 128), F32), pltpu.VMEM((128, 256), F32), pltpu.VMEM((8, 256), F32),
                        pltpu.VMEM((16, 256), F32), pltpu.VMEM((8, 768), F32), pltpu.VMEM((16, 768), F32),
                        pltpu.VMEM((128, 128), F32), pltpu.VMEM((256, 256), F32), pltpu.VMEM((TB, D), F32)])


SHARD = NPROJ // 4
SHARD_PAD = 896


def _ranges_to_perm(o, n):
    out, p = [], 0
    for start, size in _PERM:
        a, b = max(o, start), min(o + n, start + size)
        if a < b:
            out.append((a, b - a, p + a - start))
        p += size
    return out


def _ranges_to_orig(p0, n):
    out, p = [], 0
    for start, size in _PERM:
        a, b = max(p0, p), min(p0 + n, p + size)
        if a < b:
            out.append((a, b - a, start + a - p))
        p += size
    return out


def _lane_window(load, lo, n, d, lane):
    a = 128 * (lo // 128)
    off = lo - a
    w = 128 if off + n <= 128 else 256
    chunk = load(a, w)
    shift = (d - off) % w
    if shift:
        chunk = pltpu.roll(chunk, shift, axis=1)
    return jnp.where((lane >= d) & (lane < d + n), chunk[:, 0:128], 0.0)


def _assemble_w_in(slabs, name, rb=256):
    def body(s_ref, wp_ref, wpt_ref):
        lane = _iota((1, 128), 1)
        for b in range(NP // 128):
            acc = jnp.zeros((rb, 128), F32)
            for p, n, o in _ranges_to_orig(128 * b, 128):
                while n > 0:
                    s, lo = o // SHARD, o % SHARD
                    cnt = min(n, SHARD - lo)
                    acc = acc + _lane_window(lambda a, w, s=s: s_ref[s, :, a:a + w].astype(F32), lo, cnt, p - 128 * b, lane)
                    o, p, n = o + cnt, p + cnt, n - cnt
            wp_ref[:, 128 * b:128 * b + 128] = acc.astype(BF16)
            wpt_ref[128 * b:128 * b + 128, :] = acc.T.astype(BF16)

    return pl.pallas_call(
        body, grid=(D // rb,), name=name,
        in_specs=[pl.BlockSpec((4, rb, SHARD_PAD), lambda i: (0, i, 0))],
        out_specs=[pl.BlockSpec((rb, NP), lambda i: (i, 0)), pl.BlockSpec((NP, rb), lambda i: (0, i))],
        out_shape=[jax.ShapeDtypeStruct((D, NP), BF16), jax.ShapeDtypeStruct((NP, D), BF16)],
        compiler_params=_cparams(("parallel",)))(slabs)


def _split_dw_in(dwp, name, rb=256):
    rows = dwp.shape[0]

    def body(g_ref, o_ref):
        lane = _iota((1, 128), 1)
        for s in range(4):
            for k in range(SHARD_PAD // 128):
                acc = jnp.zeros((rb, 128), F32)
                n_valid = min(128, SHARD - 128 * k)
                for o, n, p in _ranges_to_perm(SHARD * s + 128 * k, n_valid):
                    acc = acc + _lane_window(lambda a, w: g_ref[:, a:a + w].astype(F32), p, n, o - SHARD * s - 128 * k, lane)
                o_ref[s, :, 128 * k:128 * k + 128] = acc.astype(o_ref.dtype)

    return pl.pallas_call(
        body, grid=(rows // rb,), name=name,
        in_specs=[pl.BlockSpec((rb, NP), lambda i: (i, 0))],
        out_specs=pl.BlockSpec((4, rb, SHARD_PAD), lambda i: (0, i, 0)),
        out_shape=jax.ShapeDtypeStruct((4, rows, SHARD_PAD), dwp.dtype),
        compiler_params=_cparams(("parallel",)))(dwp)


def _half(c, n):
    return pl.ds(pl.multiple_of(c * (n // 2), n // 2), n // 2)


def _other_chips(x, y):
    return ((1 - x, y), (x, 1 - y), (1 - x, 1 - y))


def _remote(src, dst, send, recv, k, dev):
    return pltpu.make_async_remote_copy(src_ref=src, dst_ref=dst, send_sem=send.at[k], recv_sem=recv.at[k], device_id=dev,
                                        device_id_type=MESH)


def _sem(n):
    return pltpu.SemaphoreType.DMA((n,))


def _rider_gather_ici(shards, extra=None):
    shards = tuple(shards) + ((extra,) if extra is not None else ())
    n = len(shards)

    def copies(rins, routs, sems, arrivals=True):
        send, recv = sems
        x, y, c = _place()
        me = 2 * x + y
        out, inc = [], []
        for j, (px, py) in enumerate(_other_chips(x, y)):
            for k in range(n):
                whole = extra is not None and k == n - 1
                rows = pl.ds(0, shards[k].shape[0]) if whole else _half(c, shards[k].shape[0])
                out.append(_remote(rins[k].at[rows], routs[k].at[me, rows], send, recv, n * j + k, (px, py, c)))
                if arrivals:
                    inc.append(_remote(rins[k].at[rows], routs[k].at[2 * px + py, rows], send, recv, n * j + k, (px, py, c)))
        return out, inc

    def start(rins, routs, sems):
        for cp in copies(rins, routs, sems, arrivals=False)[0]:
            cp.start()

    def finish(rins, routs, sems):
        out, inc = copies(rins, routs, sems)
        for cp in inc:
            cp.wait_recv()
        for cp in out:
            cp.wait_send()

    return _Rider(shards, [jax.ShapeDtypeStruct((4,) + a.shape, a.dtype) for a in shards], [_sem(3 * n), _sem(3 * n)],
                  start, finish)


def _rider_gather_d2d(slabs):
    slabs = tuple(slabs)
    n = len(slabs)

    def copies(routs, sems, arrivals=True):
        send, recv = sems
        x, y, c = _place()
        out, inc = [], []
        for j, (px, py) in enumerate(_other_chips(x, y)):
            for k in range(n):
                rows = slabs[k].shape[1]
                mine, theirs = routs[k].at[2 * px + py, _half(c, rows)], routs[k].at[2 * px + py, _half(1 - c, rows)]
                out.append(_remote(mine, mine, send, recv, n * j + k, (x, y, 1 - c)))
                if arrivals:
                    inc.append(_remote(theirs, theirs, send, recv, n * j + k, (x, y, 1 - c)))
        return out, inc

    def start(rins, routs, sems):
        for cp in copies(routs, sems, arrivals=False)[0]:
            cp.start()

    def finish(rins, routs, sems):
        out, inc = copies(routs, sems)
        for cp in inc:
            cp.wait_recv()
        for cp in out:
            cp.wait_send()

    return _Rider(slabs, [jax.ShapeDtypeStruct(a.shape, a.dtype) for a in slabs], [_sem(3 * n), _sem(3 * n)], start, finish,
                  aliases={k: k for k in range(n)})


def _rider_swap(parts, whole=()):
    parts = tuple(parts)
    n = len(parts)
    halved = lambda k, a: a if k in whole else a // 2

    def copies(rins, routs, sems):
        send, recv = sems
        x, y, c = _place()
        return [_remote(rins[k] if k in whole else rins[k].at[:, _half(1 - c, parts[k].shape[1])], routs[k], send, recv, k,
                        (x, y, 1 - c)) for k in range(n)]

    def start(rins, routs, sems):
        for cp in copies(rins, routs, sems):
            cp.start()

    def finish(rins, routs, sems):
        for cp in copies(rins, routs, sems):
            cp.wait()

    return _Rider(parts, [jax.ShapeDtypeStruct((a.shape[0], halved(k, a.shape[1]), a.shape[2]), a.dtype)
                          for k, a in enumerate(parts)], [_sem(n), _sem(n)], start, finish)


def _rider_scatter(parts):
    parts = tuple(parts)
    n = len(parts)

    def copies(rins, routs, sems, arrivals=True):
        send, recv = sems
        x, y, c = _place()
        me = 2 * x + y
        out, inc = [], []
        for j, (px, py) in enumerate(_other_chips(x, y)):
            for k in range(n):
                out.append(_remote(rins[k].at[2 * px + py], routs[k].at[me], send, recv, n * j + k, (px, py, c)))
                if arrivals:
                    inc.append(_remote(rins[k].at[me], routs[k].at[2 * px + py], send, recv, n * j + k, (px, py, c)))
        return out, inc

    def start(rins, routs, sems):
        for cp in copies(rins, routs, sems, arrivals=False)[0]:
            cp.start()

    def finish(rins, routs, sems):
        out, inc = copies(rins, routs, sems)
        for cp in inc:
            cp.wait_recv()
        for cp in out:
            cp.wait_send()

    return _Rider(parts, [jax.ShapeDtypeStruct(a.shape, a.dtype) for a in parts], [_sem(3 * n), _sem(3 * n)], start, finish)


def _rider_share(fulls):
    fulls = tuple(fulls)
    n = len(fulls)

    def copies(routs, sems, arrivals=True):
        send, recv = sems
        x, y, c = _place()
        out, inc = [], []
        for k in range(n):
            mine, theirs = routs[k].at[_half(c, fulls[k].shape[0])], routs[k].at[_half(1 - c, fulls[k].shape[0])]
            out.append(_remote(mine, mine, send, recv, k, (x, y, 1 - c)))
            if arrivals:
                inc.append(_remote(theirs, theirs, send, recv, k, (x, y, 1 - c)))
        return out, inc

    def start(rins, routs, sems):
        for cp in copies(routs, sems, arrivals=False)[0]:
            cp.start()

    def finish(rins, routs, sems):
        out, inc = copies(routs, sems)
        for cp in inc:
            cp.wait_recv()
        for cp in out:
            cp.wait_send()

    return _Rider(fulls, [jax.ShapeDtypeStruct(a.shape, a.dtype) for a in fulls], [_sem(n), _sem(n)], start, finish,
                  aliases={k: k for k in range(n)})


def _pair_sum(core, full, recv, name, br=128):
    n, rows, cols = recv.shape

    def body(c_ref, a_ref, b_ref, o_ref):
        o_ref[...] = (a_ref[...] + b_ref[...]).astype(BF16)

    nb = rows // br
    first = 0 if full.shape == recv.shape else nb
    return pl.pallas_call(
        body, name=name, out_shape=jax.ShapeDtypeStruct(recv.shape, BF16),
        grid_spec=pltpu.PrefetchScalarGridSpec(
            num_scalar_prefetch=1, grid=(n, nb),
            in_specs=[pl.BlockSpec((1, br, cols), lambda i, j, c: (i, c[0] * first + j, 0)),
                      pl.BlockSpec((1, br, cols), lambda i, j, c: (i, j, 0))],
            out_specs=pl.BlockSpec((1, br, cols), lambda i, j, c: (i, j, 0))),
        compiler_params=_cparams(("parallel", "parallel")))(core, full, recv)


def _chip_sum(place, gathered, mine, name, br=128):
    _, r, c = gathered.shape
    nb = r // br

    def body(p_ref, g_ref, m_ref, o_ref):
        slab = lambda j: jnp.where(p_ref[1] == j, m_ref[j], g_ref[j]).astype(F32)
        o_ref[...] = ((slab(0) + slab(1)) + slab(2)) + slab(3)

    return pl.pallas_call(
        body, name=name, out_shape=jax.ShapeDtypeStruct((2 * r, c), F32),
        grid_spec=pltpu.PrefetchScalarGridSpec(
            num_scalar_prefetch=1, grid=(nb,),
            in_specs=[pl.BlockSpec((4, br, c), lambda i, p: (0, i, 0)), pl.BlockSpec((4, br, c), lambda i, p: (0, i, 0))],
            out_specs=pl.BlockSpec((br, c), lambda i, p: (p[0] * nb + i, 0))),
        compiler_params=_cparams(("parallel",)))(place, gathered, mine)


def _adamw(w, g, m, v, name, br):
    n, r, c = w.shape

    def body(w_ref, g_ref, m_ref, v_ref, d_ref, m2_ref, v2_ref):
        d_ref[...], m2_ref[...], v2_ref[...] = _adam_math(w_ref[...], g_ref[...], m_ref[...], v_ref[...])

    spec = pl.BlockSpec((1, br, c), lambda i, j: (i, j, 0))
    shp = jax.ShapeDtypeStruct(w.shape, F32)
    return pl.pallas_call(body, grid=(n, r // br), name=name, in_specs=[spec] * 4, out_specs=[spec] * 3,
                          out_shape=[shp] * 3, compiler_params=_cparams(("parallel", "parallel")))(w, g, m, v)


def _adamw_w_in(w, g, m, v, name, bc=31):
    cols = w.shape[2]
    lead = lambda a: jnp.transpose(a, (2, 0, 1))
    g = jnp.stack([a[:, 0:cols] for a in g])

    def body(w_ref, g_ref, m_ref, v_ref, go_ref, d_ref, m2_ref, v2_ref):
        for l in range(2):
            gv = g_ref[:, l, :]
            d_ref[:, l, :], m2_ref[:, l, :], v2_ref[:, l, :] = _adam_math(w_ref[:, l, :], gv, m_ref[:, l, :], v_ref[:, l, :])
            go_ref[:, l, :] = gv

    spec = pl.BlockSpec((bc, 2, D), lambda i: (i, 0, 0))
    outs = pl.pallas_call(body, grid=(cols // bc,), name=name, in_specs=[spec] * 4, out_specs=[spec] * 4,
                          out_shape=[jax.ShapeDtypeStruct((cols, 2, D), F32)] * 4,
                          compiler_params=_cparams(("parallel",)))(lead(w), lead(g), lead(m), lead(v))
    return [jnp.transpose(o, (1, 2, 0)) for o in outs]


_SMALL_NAMES = ("norm_w", "conv_a_w", "gla_gate_w", "gla_gate_b", "gla_norm_w", "pool_w", "pool_scale", "ssd_conv_w",
                "ssd_conv_b", "ssd_dt_bias", "ssd_a_log", "ssd_d", "ssd_norm_w", "final_norm_w")
SMALL_ROWS = 72


def _adam_math(w, g, m, v):
    m2 = ADAM_B1 * m + (1.0 - ADAM_B1) * g
    v2 = ADAM_B2 * v + (1.0 - ADAM_B2) * (g * g)
    m_hat = m2 / (1.0 - ADAM_B1 ** ADAM_STEP)
    v_hat = v2 / (1.0 - ADAM_B2 ** ADAM_STEP)
    return -ADAM_LR * (m_hat / (jnp.sqrt(v_hat) + ADAM_EPS) + ADAM_WD * w), m2, v2


def _small_slices(name, chip):
    if name == "conv_a_w":
        return [((), slice(R_CAW, R_CAW + 3), slice(64 * chip, 64 * chip + 64))]
    if name == "ssd_conv_w":
        return [((), slice(R_SCW, R_SCW + 4), slice(192 * chip, 192 * chip + 192))]
    if name == "gla_gate_w":
        return [((), slice(0, 16), slice(768, 896))]
    if name == "pool_w":
        return [((g, slice(16 * q, 16 * q + 16)), slice(16, 32), slice(256 * q + 64 * g, 256 * q + 64 * g + 64))
                for g in range(4) for q in range(4)]
    row, lanes = {"gla_gate_b": (R_GB, slice(0, 128)), "gla_norm_w": (R_GNW, slice(0, 64)),
                  "pool_scale": (R_PSC, slice(0, 256)), "ssd_conv_b": (R_SCB, slice(0, 768)),
                  "ssd_dt_bias": (R_DTB, slice(16, 20)), "ssd_a_log": (R_AE, slice(0, 4)), "ssd_d": (R_DE, slice(0, 4)),
                  "ssd_norm_w": (R_SNW, slice(0, 256))}[name]
    return [((), slice(row, row + 1), lanes)]


def _small_allreduce(sg0, sg1, dnw0, dnw1, head):
    def body(sg0_ref, sg1_ref, dnw0_ref, dnw1_ref, head_ref, acc, stage, pair, rbuf, send_sems, recv_sems):
        x, y, c = _place()
        chip = 2 * x + y
        stage[0:32, :] = sg0_ref[...]
        stage[32:64, :] = sg1_ref[...]
        stage[64:65, :] = dnw0_ref[0:1, :]
        stage[65:66, :] = dnw1_ref[0:1, :]
        stage[66:68, :] = head_ref[0:2, :]
        stage[68:72, :] = jnp.zeros((4, D), F32)
        sib = _remote(stage, pair, send_sems, recv_sems, 0, (x, y, 1 - c))
        sib.start()
        sib.wait()
        rbuf[0] = stage[...] + pair[...]
        sends = [_remote(rbuf.at[0], rbuf.at[k], send_sems, recv_sems, k, (px, py, c))
                 for k, (px, py) in enumerate(_other_chips(x, y), start=1)]
        for cp in sends:
            cp.start()
        for cp in sends:
            cp.wait()
        slab = lambda d: jnp.where(d == 0, 0, jnp.where(d == 2, 1, jnp.where(d == 1, 2, 3)))
        total = rbuf[slab(jnp.bitwise_xor(chip, 0))]
        for s in range(1, 4):
            total = total + rbuf[slab(jnp.bitwise_xor(chip, s))]
        acc[...] = total

    vmem = pl.BlockSpec(memory_space=pltpu.VMEM)
    return pl.pallas_call(
        body, name="small_allreduce", in_specs=[vmem] * 5, out_specs=vmem,
        out_shape=jax.ShapeDtypeStruct((SMALL_ROWS, D), F32),
        scratch_shapes=[pltpu.VMEM((SMALL_ROWS, D), F32), pltpu.VMEM((SMALL_ROWS, D), F32),
                        pltpu.VMEM((4, SMALL_ROWS, D), F32), _sem(4), _sem(4)],
    )(sg0, sg1, dnw0, dnw1, head)


def _small_adamw(acc, w, m, v):
    n = len(_SMALL_NAMES)

    def body(*refs):
        acc = refs[0]
        w_refs, m_refs, v_refs = refs[1:1 + n], refs[1 + n:1 + 2 * n], refs[1 + 2 * n:1 + 3 * n]
        o = 1 + 3 * n
        g_out, d_out, m_out, v_out = refs[o:o + n], refs[o + n:o + 2 * n], refs[o + 2 * n:o + 3 * n], refs[o + 3 * n:o + 4 * n]
        loss_ref = refs[o + 4 * n]
        chip = 2 * lax.axis_index("x") + lax.axis_index("y")
        loss_ref[...] = acc[67:68, 0:1]

        def update(i, idx, g):
            d, m2, v2 = _adam_math(w_refs[i][idx], g, m_refs[i][idx], v_refs[i][idx])
            g_out[i][idx], d_out[i][idx], m_out[i][idx], v_out[i][idx] = g, d, m2, v2

        for i, name in enumerate(_SMALL_NAMES):
            if name == "final_norm_w":
                update(i, (slice(0, 1), slice(None)), acc[66:67, :])
            elif name == "norm_w":
                for l in range(2):
                    update(i, (slice(l, l + 1), slice(None)), acc[64 + l:65 + l, :])
            elif name in ("conv_a_w", "ssd_conv_w"):
                for s in range(4):
                    @pl.when(chip == s)
                    def _(i=i, name=name, s=s):
                        for l in range(2):
                            (_, rows, lanes), = _small_slices(name, s)
                            update(i, (l,), acc[rows.start + 32 * l:rows.stop + 32 * l, lanes])
            else:
                for l in range(2):
                    for idx, rows, lanes in _small_slices(name, 0):
                        g = acc[rows.start + 32 * l:rows.stop + 32 * l, lanes]
                        if w_refs[i].ndim == 2:
                            update(i, (slice(l, l + 1), slice(None)), g)
                        else:
                            update(i, (l,) + idx, g)

    args = [acc] + [d[k] for d in (w, m, v) for k in _SMALL_NAMES]
    shapes = [jax.ShapeDtypeStruct(w[k].shape, F32) for k in _SMALL_NAMES]
    vmem = pl.BlockSpec(memory_space=pltpu.VMEM)
    outs = pl.pallas_call(body, name="small_adamw", in_specs=[vmem] * len(args), out_specs=[vmem] * (4 * n + 1),
                          out_shape=shapes * 4 + [jax.ShapeDtypeStruct((1, 1), F32)])(*args)
    return outs[0:n], outs[n:2 * n], outs[2 * n:3 * n], outs[3 * n:4 * n], outs[4 * n]


def _mixer_consts(layer, conv_a_w, gla_gate_w, gla_gate_b, gla_norm_w, pool_w, pool_scale, ssd_conv_w, ssd_conv_b,
                  ssd_dt_bias, ssd_a_log, ssd_d, ssd_norm_w):
    def row(v):
        return jnp.pad(v.reshape(1, -1), ((0, 0), (0, 768 - v.size)))

    dtb = jnp.pad(ssd_dt_bias[layer], (16, 108))
    rows = [jnp.pad(conv_a_w[layer], ((0, 0), (0, 512))), row(gla_gate_b[layer]), row(jnp.tile(gla_norm_w[layer], 4)),
            row(pool_scale[layer]), row(ssd_conv_b[layer]), row(dtb), row(jnp.repeat(-jnp.exp(ssd_a_log[layer]), 64)),
            row(jnp.repeat(ssd_d[layer], 64)), row(ssd_norm_w[layer]), jnp.zeros((1, 768), F32), ssd_conv_w[layer]]
    prm = jnp.concatenate(rows, axis=0)
    gw = jnp.pad(gla_gate_w[layer], ((0, 112), (0, 0))).astype(BF16)
    on_diag = (_iota((256, 256), 0) >> 6) == (_iota((256, 256), 1) >> 6)
    pw = jnp.where(on_diag, jnp.tile(pool_w[layer].reshape(256, 64), (1, 4)), 0.0)
    return (prm, gw, pw.astype(BF16)) + _mixer_matrices()


def _grad_slabs(dwp, dwo):
    return dwp.reshape(1, D, NP), dwo.reshape(4, D // 4, D)


class _Comm:
    def __init__(self, w_in, w_out):
        self.w_in16 = jnp.pad(w_in.astype(BF16), ((0, 0), (0, 0), (0, SHARD_PAD - SHARD)))
        self.w_out16 = w_out.astype(BF16)
        self.core = lax.axis_index("c").astype(jnp.int32).reshape(1)
        self.chip = 2 * lax.axis_index("x") + lax.axis_index("y")
        self.place = jnp.stack([lax.axis_index("c"), self.chip]).astype(jnp.int32)

    def gather_ici(self, layer, extra=None):
        return _rider_gather_ici((self.w_in16[layer], self.w_out16[layer]), extra)

    def pair_sum(self, layer, slabs, received):
        d_in, d_out = [_pair_sum(self.core, a, b, name=f"reduce_pair_sum{layer}_{k}")
                       for k, (a, b) in enumerate(zip(slabs, received))]
        return [_split_dw_in(d_in[0], name=f"split_dw_in{layer}"), d_out]

    def chip_sum(self, layer, gathered, mine):
        return [_chip_sum(self.place, a, b, name=f"reduce_chip_sum{layer}_{k}") for k, (a, b) in enumerate(zip(gathered, mine))]

    def layer_weights(self, layer, s_in, s_out):
        own = lambda slabs, shard: jnp.stack([jnp.where(self.chip == s, shard, slabs[s]) for s in range(4)])
        wp, wpt = _assemble_w_in(own(s_in, self.w_in16[layer]), name=f"assemble_w_in{layer}")
        wo = own(s_out, self.w_out16[layer]).reshape(D, D)
        return wp, wpt, wo, wo.T


def _local_step(x, tgt, norm_w, final_norm_w, consts, wts0, wts1=None, comm=None):
    nw = [norm_w[l:l + 1] for l in range(2)]
    proj0, h0, slabs = _rmsproj(x, nw[0], wts0[0], name="rmsproj0", rider=comm and comm.gather_ici(1))
    (mix0, sg0, ss0, x1), slabs = _mixer_fwd(proj0, x, wts0[2], *consts[0], name="mixer_fwd0",
                                             rider=comm and _rider_gather_d2d(slabs))
    if comm:
        wts1 = comm.layer_weights(1, *slabs)
    proj1, h1, _ = _rmsproj(x1, nw[1], wts1[0], name="rmsproj1")
    (mix1, sg1, ss1, dx, head), _ = _mixer_fwd(proj1, x1, wts1[2], *consts[1], name="mixer_fwd1",
                                               head=(tgt, final_norm_w.reshape(1, D)))
    (dproj, mgr1, dwo1), _ = _mixer_bwd(proj1, dx, wts1[3], mix1, sg1, ss1, *consts[1], name="mixer_bwd1")
    dwp1, _ = _dwin(h1, dproj, name="dwin1")
    slabs1 = comm and _grad_slabs(dwp1, dwo1)
    (dx, dnw1), recv = _dxin(dproj, wts1[1], x1, dx, nw[1], name="dxin1", rider=comm and _rider_swap(slabs1))
    pairs1 = comm and comm.pair_sum(1, slabs1, recv)
    (dproj, mgr0, dwo0), gathered = _mixer_bwd(proj0, dx, wts0[3], mix0, sg0, ss0, *consts[0], name="mixer_bwd0",
                                               rider=comm and _rider_scatter(pairs1))
    if comm:
        away, big1 = _dwin(h0, dproj, name="dwin0_away", half=(comm.core, 1),
                           rider=_rider_share(comm.chip_sum(1, gathered, pairs1)))
        sent = away.reshape(1, D // 2, NP), dwo0.reshape(4, D // 4, D)
        mine, recv = _dwin(h0, dproj, name="dwin0_mine", half=(comm.core, 0), rider=_rider_swap(sent, whole=(0,)))
        pairs0 = comm.pair_sum(0, (mine.reshape(1, D // 2, NP), sent[1]), recv)
        (dx, dnw0), gathered = _dxin(dproj, wts0[1], x, dx, nw[0], name="dxin0", rider=_rider_scatter(pairs0))
        big0 = _run_rider(_rider_share(comm.chip_sum(0, gathered, pairs0)), "reduce_share0")
        big = ((big0[0], big1[0]), (big0[1], big1[1]))
    else:
        dwp0, _ = _dwin(h0, dproj, name="dwin0")
        (dx, dnw0), _ = _dxin(dproj, wts0[1], x, dx, nw[0], name="dxin0")
        big = ((dwp0, dwp1), (dwo0, dwo1))
    return head, dx, big, (dnw0, dnw1), (mgr0, mgr1)


def kernel(x, norm_w, w_in, conv_a_w, gla_gate_w, gla_gate_b, gla_norm_w, pool_w, pool_scale, ssd_conv_w, ssd_conv_b, ssd_dt_bias, ssd_a_log, ssd_d, ssd_norm_w, w_out, final_norm_w, loss_target, m_norm_w, m_w_in, m_conv_a_w, m_gla_gate_w, m_gla_gate_b, m_gla_norm_w, m_pool_w, m_pool_scale, m_ssd_conv_w, m_ssd_conv_b, m_ssd_dt_bias, m_ssd_a_log, m_ssd_d, m_ssd_norm_w, m_w_out, m_final_norm_w, v_norm_w, v_w_in, v_conv_a_w, v_gla_gate_w, v_gla_gate_b, v_gla_norm_w, v_pool_w, v_pool_scale, v_ssd_conv_w, v_ssd_conv_b, v_ssd_dt_bias, v_ssd_a_log, v_ssd_d, v_ssd_norm_w, v_w_out, v_final_norm_w):
    weights = dict(norm_w=norm_w, w_in=w_in, conv_a_w=conv_a_w, gla_gate_w=gla_gate_w, gla_gate_b=gla_gate_b,
                   gla_norm_w=gla_norm_w, pool_w=pool_w, pool_scale=pool_scale, ssd_conv_w=ssd_conv_w,
                   ssd_conv_b=ssd_conv_b, ssd_dt_bias=ssd_dt_bias, ssd_a_log=ssd_a_log, ssd_d=ssd_d,
                   ssd_norm_w=ssd_norm_w, w_out=w_out, final_norm_w=final_norm_w)
    m_in = dict(norm_w=m_norm_w, w_in=m_w_in, conv_a_w=m_conv_a_w, gla_gate_w=m_gla_gate_w, gla_gate_b=m_gla_gate_b,
                gla_norm_w=m_gla_norm_w, pool_w=m_pool_w, pool_scale=m_pool_scale, ssd_conv_w=m_ssd_conv_w,
                ssd_conv_b=m_ssd_conv_b, ssd_dt_bias=m_ssd_dt_bias, ssd_a_log=m_ssd_a_log, ssd_d=m_ssd_d,
                ssd_norm_w=m_ssd_norm_w, w_out=m_w_out, final_norm_w=m_final_norm_w)
    v_in = dict(norm_w=v_norm_w, w_in=v_w_in, conv_a_w=v_conv_a_w, gla_gate_w=v_gla_gate_w, gla_gate_b=v_gla_gate_b,
                gla_norm_w=v_gla_norm_w, pool_w=v_pool_w, pool_scale=v_pool_scale, ssd_conv_w=v_ssd_conv_w,
                ssd_conv_b=v_ssd_conv_b, ssd_dt_bias=v_ssd_dt_bias, ssd_a_log=v_ssd_a_log, ssd_d=v_ssd_d,
                ssd_norm_w=v_ssd_norm_w, w_out=v_w_out, final_norm_w=v_final_norm_w)
    order = ("norm_w", "w_in", "conv_a_w", "gla_gate_w", "gla_gate_b", "gla_norm_w", "pool_w", "pool_scale",
             "ssd_conv_w", "ssd_conv_b", "ssd_dt_bias", "ssd_a_log", "ssd_d", "ssd_norm_w", "w_out", "final_norm_w")
    t = x.shape[1]

    comm = _Comm(w_in, w_out)
    cshard = jnp.zeros((16, 256), F32)
    for l in range(2):
        cshard = cshard.at[8 * l:8 * l + 3, 0:64].set(conv_a_w[l]).at[8 * l + 3:8 * l + 7, 0:192].set(ssd_conv_w[l])
    s_in, s_out, g_c = _run_rider(comm.gather_ici(0, cshard), "gather_ici0")
    s_in, s_out = _run_rider(_rider_gather_d2d((s_in, s_out)), "gather_d2d0")
    g_c = [jnp.where(comm.chip == s, cshard, g_c[s]) for s in range(4)]
    conv_a_full = jnp.stack([jnp.concatenate([g_c[s][8 * l:8 * l + 3, 0:64] for s in range(4)], axis=-1) for l in range(2)])
    ssd_conv_full = jnp.stack([jnp.concatenate([g_c[s][8 * l + 3:8 * l + 7, 0:192] for s in range(4)], axis=-1)
                               for l in range(2)])
    consts = [_mixer_consts(l, conv_a_full, gla_gate_w, gla_gate_b, gla_norm_w, pool_w, pool_scale, ssd_conv_full,
                            ssd_conv_b, ssd_dt_bias, ssd_a_log, ssd_d, ssd_norm_w) for l in range(2)]

    head, dx, big, dnw, mgr = _local_step(x.reshape(t, D), loss_target.reshape(t, D), norm_w, final_norm_w, consts,
                                          comm.layer_weights(0, s_in, s_out), comm=comm)

    as2d = lambda d: {k: (d[k].reshape(1, D) if k == "final_norm_w" else d[k]) for k in _SMALL_NAMES}
    small = _small_adamw(_small_allreduce(mgr[0], mgr[1], dnw[0], dnw[1], head), as2d(weights), as2d(m_in), as2d(v_in))
    grads, delta, new_m, new_v = ({k: (a.reshape(D) if k == "final_norm_w" else a) for k, a in zip(_SMALL_NAMES, part)}
                                  for part in small[0:4])
    loss = small[4].reshape(())

    grads["w_out"] = jnp.stack(big[1])

    grads["w_in"], delta["w_in"], new_m["w_in"], new_v["w_in"] = _adamw_w_in(w_in, big[0], m_w_in, v_w_in, name="adamw_w_in")
    delta["w_out"], new_m["w_out"], new_v["w_out"] = _adamw(w_out, grads["w_out"], m_w_out, v_w_out, name="adamw_w_out", br=256)

    return (loss, dx.reshape(1, t, D), *[grads[k] for k in order], *[delta[k] for k in order],
            *[new_m[k] for k in order], *[new_v[k] for k in order])
```

```python
import functools

import jax
import jax.numpy as jnp
from jax import lax
from jax.experimental import pallas as pl
from jax.experimental.pallas import tpu as pltpu

F32 = jnp.float32
BF16 = jnp.bfloat16
MESH = pl.DeviceIdType.MESH

D = 1024
CH = 64
EPS = 1e-6
NP = 3456
NPROJ = 3348
NPM = 3328
GLA_SCALE = 32.0 ** -0.5
INV_TAU = 1.0 / 16.0
TB = 512
NCH = TB // CH
assert TB % 256 == 0

C_AH, C_AB, C_AC, C_AZ, C_GQ, C_GK, C_GV = 0, 256, 512, 768, 1024, 1152, 1280
C_GZ, C_PU, C_PZ, C_SZ, C_SX, C_TL = 1536, 1792, 2048, 2304, 2560, 3328
_PERM = ((0, 1536), (1552, 1792), (1536, 16), (3344, 4))

R_CAW, R_GB, R_GNW, R_PSC, R_SCB, R_DTB, R_AE, R_DE, R_SNW, R_SCW = 0, 3, 4, 5, 6, 7, 8, 9, 10, 12

ADAM_LR, ADAM_B1, ADAM_B2, ADAM_EPS, ADAM_WD, ADAM_STEP = 0.001, 0.9, 0.999, 1e-08, 0.01, 10

VMEM_LIMIT = 56 * 1024 * 1024


def _cparams(sem, limit=VMEM_LIMIT):
    return pltpu.CompilerParams(dimension_semantics=sem, vmem_limit_bytes=limit)


_ANY = pl.BlockSpec(memory_space=pl.ANY)


def _place():
    return lax.axis_index("x"), lax.axis_index("y"), lax.axis_index("c")


class _Rider:
    def __init__(self, inputs, out_shapes, sems, start, finish, aliases=None):
        self.inputs, self.out_shapes, self.sems = tuple(inputs), tuple(out_shapes), tuple(sems)
        self.start, self.finish, self.aliases = start, finish, dict(aliases or {})


def _call(body, args, *, grid, in_specs, out_specs, out_shape, name, sem, scratch_shapes=(), rider=None):
    if rider is None:
        outs = pl.pallas_call(body, grid=grid, name=name, in_specs=list(in_specs), out_specs=list(out_specs),
                              out_shape=list(out_shape), scratch_shapes=list(scratch_shapes),
                              compiler_params=_cparams(sem))(*args)
        return list(outs), []
    ni, no, ns = len(args), len(out_shape), len(scratch_shapes)
    ri, ro = len(rider.inputs), len(rider.out_shapes)

    def full(*refs):
        ins, rins = refs[:ni], refs[ni:ni + ri]
        outs, routs = refs[ni + ri:ni + ri + no], refs[ni + ri + no:ni + ri + no + ro]
        scr, rsem = refs[ni + ri + no + ro:ni + ri + no + ro + ns], refs[ni + ri + no + ro + ns:]
        first = functools.reduce(jnp.logical_and, [pl.program_id(a) == 0 for a in range(len(grid))])
        last = functools.reduce(jnp.logical_and, [pl.program_id(a) == grid[a] - 1 for a in range(len(grid))])

        @pl.when(first)
        def _():
            rider.start(rins, routs, rsem)

        body(*ins, *outs, *scr)

        @pl.when(last)
        def _():
            rider.finish(rins, routs, rsem)

    outs = pl.pallas_call(
        full, grid=grid, name=name, in_specs=list(in_specs) + [_ANY] * ri, out_specs=list(out_specs) + [_ANY] * ro,
        out_shape=list(out_shape) + list(rider.out_shapes), scratch_shapes=list(scratch_shapes) + list(rider.sems),
        input_output_aliases={ni + k: no + v for k, v in rider.aliases.items()},
        compiler_params=_cparams(("arbitrary",) * len(grid)))(*args, *rider.inputs)
    return list(outs[:no]), list(outs[no:])


def _run_rider(rider, name):
    ri = len(rider.inputs)

    def body(*refs):
        rins, routs, rsem = refs[:ri], refs[ri:ri + len(rider.out_shapes)], refs[ri + len(rider.out_shapes):]
        rider.start(rins, routs, rsem)
        rider.finish(rins, routs, rsem)

    return list(pl.pallas_call(body, name=name, in_specs=[_ANY] * ri, out_specs=[_ANY] * len(rider.out_shapes),
                               out_shape=list(rider.out_shapes), scratch_shapes=list(rider.sems),
                               input_output_aliases=dict(rider.aliases))(*rider.inputs))


def _dot(a, b):
    return jnp.dot(a.astype(BF16), b.astype(BF16), preferred_element_type=F32)


def _dot_nt(a, b):
    return lax.dot_general(a.astype(BF16), b.astype(BF16), (((1,), (1,)), ((), ())), preferred_element_type=F32)


def _dot_tn(a, b):
    return lax.dot_general(a.astype(BF16), b.astype(BF16), (((0,), (0,)), ((), ())), preferred_element_type=F32)


def _split(a):
    hi = a.astype(BF16)
    lo = (a - hi.astype(F32)).astype(BF16)
    return hi, lo


def _dot2_l(a, b):
    hi, lo = _split(a)
    return _dot(hi, b) + _dot(lo, b)


def _dot2_r(a, b):
    hi, lo = _split(b)
    return _dot(a, hi) + _dot(a, lo)


def _dot3_l(a, b):
    hi, lo = _split(a)
    lo2 = ((a - hi.astype(F32)) - lo.astype(F32)).astype(BF16)
    return _dot(hi, b) + _dot(lo, b) + _dot(lo2, b)


def _dot2_nt(a, b):
    hi, lo = _split(a)
    return _dot_nt(hi, b) + _dot_nt(lo, b)


def _silu(z):
    return z * jax.nn.sigmoid(z)


def _lse1(x):
    return jnp.log(1.0 + jnp.exp(-jnp.abs(x)))


def _cs(a):
    return jnp.sum(a, axis=0, keepdims=True)


def _iota(shape, dim):
    return lax.broadcasted_iota(jnp.int32, shape, dim)


def _mixer_matrices():
    r, c = _iota((256, 256), 0), _iota((256, 256), 1)
    same_chunk = (r >> 6) == (c >> 6)
    mats = jnp.stack([jnp.where((c > r) & same_chunk, 1.0, 0.0), jnp.where((c < r) & same_chunk, 1.0, 0.0),
                      jnp.where(same_chunk, 1.0 / 64.0, 0.0), jnp.where((r < 128) & (r - 16 == (c >> 6)), 1.0, 0.0)])
    mask = jnp.where((_iota((256, 128), 0) >> 6) == (_iota((256, 128), 1) >> 5), 1.0, 0.0)
    return mats.astype(BF16), mask.astype(F32)


def _dn(ext, k, n, h):
    return pltpu.roll(ext, k, axis=0)[h:h + n]


def _up(ext, k, n):
    return pltpu.roll(ext, ext.shape[0] - k, axis=0)[:n]


def _pool_lane_select(lane, s2, s4, s8, s16):
    return jnp.where(lane < 64, s2, jnp.where(lane < 128, s4, jnp.where(lane < 192, s8, s16)))


def _winsum_dn(ext, lane):
    s2 = ext + pltpu.roll(ext, 1, axis=0)
    s4 = s2 + pltpu.roll(s2, 2, axis=0)
    s8 = s4 + pltpu.roll(s4, 4, axis=0)
    s16 = s8 + pltpu.roll(s8, 8, axis=0)
    return _pool_lane_select(lane, s2, s4, s8, s16)


def _winsum_up(ext, lane):
    m = ext.shape[0]
    s2 = ext + pltpu.roll(ext, m - 1, axis=0)
    s4 = s2 + pltpu.roll(s2, m - 2, axis=0)
    s8 = s4 + pltpu.roll(s4, m - 4, axis=0)
    s16 = s8 + pltpu.roll(s8, m - 8, axis=0)
    return _pool_lane_select(lane, s2, s4, s8, s16)


def _pool_inv_count(tile, n):
    lane = _iota((1, 256), 1)
    win = _pool_lane_select(lane, 2.0, 4.0, 8.0, 16.0).astype(F32)
    tpos = (tile * n + _iota((n, 1), 0) + 1).astype(F32)
    return jnp.where(tpos >= win, 1.0 / win, 1.0 / tpos)


def _silu_pair(z):
    s = jax.nn.sigmoid(z)
    return z * s, s * (1.0 + z * (1.0 - s))


def _chunks(a):
    return [a[c * CH:(c + 1) * CH] for c in range(a.shape[0] // CH)]


def _halves(fn, a, b):
    return jnp.concatenate([fn(a[:, 0:128], b[:, 0:128]), fn(a[:, 128:256], b[:, 128:256])], axis=1)


def _chunk_sums(tri, a):
    return jnp.concatenate([_dot2_r(tri, a[r:r + 256]) for r in range(0, a.shape[0], 256)], axis=0)


def _mixer_tile_prep(p_ref, t_ref, xc, prm_ref, gw_v, cm_ref, mk_ref):
    tail = t_ref[...]
    pre = _dot(tail, gw_v) + prm_ref[R_GB:R_GB + 1, 0:128]
    la = (jnp.minimum(pre, 0.0) - _lse1(pre)) * INV_TAU
    dtin = tail + prm_ref[R_DTB:R_DTB + 1, 0:128]
    dtf = jnp.maximum(dtin, 0.0) + _lse1(dtin)
    dte = _dot2_l(dtf, cm_ref[3, 0:128, :])
    da = dte * prm_ref[R_AE:R_AE + 1, 0:256]
    rev = _chunk_sums(cm_ref[0], jnp.concatenate([la, da], axis=1))
    dec = jnp.exp(rev[:, 0:128])
    kd = p_ref[:, C_GK:C_GK + 128].astype(F32) * dec
    wdec = jnp.exp(rev[:, 128:384])
    w = wdec * dte
    xw = xc[:, 0:256] * w
    d_s = [jnp.exp(_cs(a)) for a in _chunks(la)]
    et = [jnp.exp(_cs(a)) for a in _chunks(da)]
    mask_t = mk_ref[...]
    ut_g = [_dot_tn(v, k) * mask_t for v, k in zip(_chunks(p_ref[:, C_GV:C_GV + 256].astype(F32)), _chunks(kd))]
    ut_s = [_halves(_dot_tn, b, x) for b, x in zip(_chunks(xc[:, 256:512]), _chunks(xw))]
    return tail, pre, dtin, dte, dec, kd, wdec, w, xw, d_s, et, ut_g, ut_s


def _rmsproj(x, nw, wp, name, tm=512, rider=None):
    t = x.shape[0]

    def body(x_ref, nw_ref, w_ref, o_ref, t_ref, h_ref):
        xv = x_ref[...]
        rs = lax.rsqrt(jnp.mean(xv * xv, axis=-1, keepdims=True) + EPS)
        h = (xv * rs * nw_ref[...]).astype(BF16)
        h_ref[...] = h
        proj = jnp.dot(h, w_ref[...], preferred_element_type=F32)
        o_ref[...] = proj[:, 0:NPM].astype(BF16)
        t_ref[...] = proj[:, NPM:NP]

    (proj, tail, h), extra = _call(
        body, (x, nw, wp), grid=(t // tm,), name=name, sem=("parallel",), rider=rider,
        in_specs=[pl.BlockSpec((tm, D), lambda i: (i, 0)), pl.BlockSpec((1, D), lambda i: (0, 0)),
                  pl.BlockSpec((D, NP), lambda i: (0, 0))],
        out_specs=[pl.BlockSpec((tm, NPM), lambda i: (i, 0)), pl.BlockSpec((tm, NP - NPM), lambda i: (i, 0)),
                   pl.BlockSpec((tm, D), lambda i: (i, 0))],
        out_shape=[jax.ShapeDtypeStruct((t, NPM), BF16), jax.ShapeDtypeStruct((t, NP - NPM), F32),
                   jax.ShapeDtypeStruct((t, D), BF16)])
    return (proj, tail), h, extra


def _head_tile(xv, tgt, w):
    rs = lax.rsqrt(jnp.mean(xv * xv, axis=-1, keepdims=True) + EPS)
    xh = xv * rs
    err = xh * w - tgt
    dy = err * (1.0 / D)
    dxh = dy * w
    dx = rs * (dxh - xh * jnp.mean(dxh * xh, axis=-1, keepdims=True))
    return dx, _cs(dy * xh), (0.5 / D) * jnp.sum(err * err)


def _dxin(dp, wpt, x, dxn, nw, name, tm=512, rider=None):
    t = x.shape[0]

    def body(dp_ref, w_ref, x_ref, dxn_ref, nw_ref, dx_ref, dnw_ref):
        @pl.when(pl.program_id(0) == 0)
        def _():
            dnw_ref[...] = jnp.zeros_like(dnw_ref)

        dh = jnp.dot(dp_ref[...], w_ref[...], preferred_element_type=F32)
        xv = x_ref[...]
        rs = lax.rsqrt(jnp.mean(xv * xv, axis=-1, keepdims=True) + EPS)
        xh = xv * rs
        dnw_ref[0:1, :] += _cs(dh * xh)
        dxh = dh * nw_ref[...]
        dx_ref[...] = dxn_ref[...] + rs * (dxh - xh * jnp.mean(dxh * xh, axis=-1, keepdims=True))

    return _call(
        body, (dp, wpt, x, dxn, nw), grid=(t // tm,), name=name, sem=("arbitrary",), rider=rider,
        in_specs=[pl.BlockSpec((tm, NP), lambda i: (i, 0)), pl.BlockSpec((NP, D), lambda i: (0, 0)),
                  pl.BlockSpec((tm, D), lambda i: (i, 0)), pl.BlockSpec((tm, D), lambda i: (i, 0)),
                  pl.BlockSpec((1, D), lambda i: (0, 0))],
        out_specs=[pl.BlockSpec((tm, D), lambda i: (i, 0)), pl.BlockSpec((8, D), lambda i: (0, 0))],
        out_shape=[jax.ShapeDtypeStruct((t, D), F32), jax.ShapeDtypeStruct((8, D), F32)])


def _dwin(h, dp, name, tm=1024, rider=None):
    t = h.shape[0]

    def body(h_ref, dp_ref, o_ref):
        @pl.when(pl.program_id(0) == 0)
        def _():
            o_ref[...] = jnp.zeros_like(o_ref)

        o_ref[...] += _dot_tn(h_ref[...], dp_ref[...])

    (dwp,), extra = _call(
        body, (h, dp), grid=(t // tm,), name=name, sem=("arbitrary",), rider=rider,
        in_specs=[pl.BlockSpec((tm, D), lambda i: (i, 0)), pl.BlockSpec((tm, NP), lambda i: (i, 0))],
        out_specs=[pl.BlockSpec((D, NP), lambda i: (0, 0))], out_shape=[jax.ShapeDtypeStruct((D, NP), F32)])
    return dwp, extra


def _mixer_fwd(proj, x, wo, prm, gw, pw, cmat, mask, name, rider=None, head=None):
    proj, tail = proj
    t = proj.shape[0]
    nt, nc = t // TB, t // CH

    def body(p_ref, t_ref, x_ref, wo_ref, prm_ref, gw_ref, pw_ref, cm_ref, mk_ref, *rest):
        (tgt_ref, fw_ref), rest = (rest[:2], rest[2:]) if head else ((None, None), rest)
        mix_ref, sg_ref, ss_ref, xn_ref = rest[:4]
        acc_ref = rest[4] if head else None
        sg_s, ss_s, h_ua, h_pu, h_sx = rest[-5:]
        i = pl.program_id(0)

        @pl.when(i == 0)
        def _():
            for r in (sg_s, ss_s, h_ua, h_pu, h_sx) + ((acc_ref,) if head else ()):
                r[...] = jnp.zeros_like(r)

        lane = _iota((1, 256), 1)
        u = p_ref[:, C_AC:C_AC + 256].astype(F32) * p_ref[:, C_AH:C_AH + 256].astype(F32)
        ext = jnp.concatenate([h_ua[...], u], axis=0)
        cv = (prm_ref[R_CAW + 2:R_CAW + 3, 0:256] * u + prm_ref[R_CAW + 1:R_CAW + 2, 0:256] * _dn(ext, 1, TB, 8)
              + prm_ref[R_CAW:R_CAW + 1, 0:256] * _dn(ext, 2, TB, 8))
        mix_ref[:, 0:256] = (p_ref[:, C_AB:C_AB + 256].astype(F32) * cv * _silu(p_ref[:, C_AZ:C_AZ + 256].astype(F32))).astype(BF16)
        h_ua[...] = u[TB - 8:, :]
        pu = p_ref[:, C_PU:C_PU + 256].astype(F32)
        ext = jnp.concatenate([h_pu[...], pu], axis=0)
        pooled = _winsum_dn(ext, lane)[16:] * _pool_inv_count(i, TB) - pu
        mixed = _dot(pooled, pw_ref[...])
        mix_ref[:, 512:768] = (prm_ref[R_PSC:R_PSC + 1, 0:256] * mixed * _silu(p_ref[:, C_PZ:C_PZ + 256].astype(F32))).astype(BF16)
        h_pu[...] = pu[TB - 16:, :]
        sx = p_ref[:, C_SX:C_SX + 768].astype(F32)
        ext = jnp.concatenate([h_sx[...], sx], axis=0)
        xc = _silu(prm_ref[R_SCW + 3:R_SCW + 4, :] * sx + prm_ref[R_SCW + 2:R_SCW + 3, :] * _dn(ext, 1, TB, 8)
                   + prm_ref[R_SCW + 1:R_SCW + 2, :] * _dn(ext, 2, TB, 8) + prm_ref[R_SCW:R_SCW + 1, :] * _dn(ext, 3, TB, 8)
                   + prm_ref[R_SCB:R_SCB + 1, :])
        h_sx[...] = sx[TB - 8:, :]

        _, _, _, _, _, _, _, _, _, d_s, et, ut_g, ut_s = _mixer_tile_prep(p_ref, t_ref, xc, prm_ref, gw_ref[...], cm_ref, mk_ref)
        s_g, s_s = sg_s[...], ss_s[...]
        o, y = [], []
        qs = _chunks(p_ref[:, C_GQ:C_GQ + 128].astype(F32) * GLA_SCALE)
        cm = _chunks(xc[:, 512:768])
        for c in range(NCH):
            sg_ref[c] = s_g
            ss_ref[c] = s_s
            s_g = s_g * d_s[c] + ut_g[c]
            s_s = s_s * et[c] + ut_s[c]
            o.append(_dot_nt(qs[c], s_g))
            y.append(_halves(_dot, cm[c], s_s))
        sg_s[...] = s_g
        ss_s[...] = s_s
        o = jnp.concatenate(o, axis=0)
        on = o * lax.rsqrt(_dot2_l(o * o, cm_ref[2]) + EPS)
        mix_ref[:, 256:512] = (on * prm_ref[R_GNW:R_GNW + 1, 0:256] * _silu(p_ref[:, C_GZ:C_GZ + 256].astype(F32))).astype(BF16)
        y2 = ((jnp.concatenate(y, axis=0) + prm_ref[R_DE:R_DE + 1, 0:256] * xc[:, 0:256])
              * _silu(p_ref[:, C_SZ:C_SZ + 256].astype(F32)))
        mix_ref[:, 768:1024] = (y2 * lax.rsqrt(jnp.mean(y2 * y2, axis=-1, keepdims=True) + EPS)
                                * prm_ref[R_SNW:R_SNW + 1, 0:256]).astype(BF16)
        xn = x_ref[...] + jnp.dot(mix_ref[...], wo_ref[...], preferred_element_type=F32)
        if head:
            xn_ref[...], dfw, loss = _head_tile(xn, tgt_ref[...], fw_ref[...])
            acc_ref[0:1, :] += dfw
            acc_ref[1:2, :] += jnp.zeros((1, D), F32) + loss
        else:
            xn_ref[...] = xn

    row = pl.BlockSpec((TB, D), lambda i: (i, 0))
    return _call(
        body, (proj, tail, x, wo, prm, gw, pw, cmat, mask) + tuple(head or ()), grid=(nt,), name=name, sem=("arbitrary",),
        rider=rider,
        in_specs=[pl.BlockSpec((TB, NPM), lambda i: (i, 0)), pl.BlockSpec((TB, NP - NPM), lambda i: (i, 0)), row,
                  pl.BlockSpec((D, D), lambda i: (0, 0)), pl.BlockSpec((16, 768), lambda i: (0, 0)),
                  pl.BlockSpec((128, 128), lambda i: (0, 0)), pl.BlockSpec((256, 256), lambda i: (0, 0)),
                  pl.BlockSpec((4, 256, 256), lambda i: (0, 0, 0)), pl.BlockSpec((256, 128), lambda i: (0, 0))]
        + ([row, pl.BlockSpec((1, D), lambda i: (0, 0))] if head else []),
        out_specs=[row, pl.BlockSpec((NCH, 256, 128), lambda i: (i, 0, 0)),
                   pl.BlockSpec((NCH, 128, 256), lambda i: (i, 0, 0)), row]
        + ([pl.BlockSpec((8, D), lambda i: (0, 0))] if head else []),
        out_shape=[jax.ShapeDtypeStruct((t, D), BF16), jax.ShapeDtypeStruct((nc, 256, 128), F32),
                   jax.ShapeDtypeStruct((nc, 128, 256), F32), jax.ShapeDtypeStruct((t, D), F32)]
        + ([jax.ShapeDtypeStruct((8, D), F32)] if head else []),
        scratch_shapes=[pltpu.VMEM((256, 128), F32), pltpu.VMEM((128, 256), F32), pltpu.VMEM((8, 256), F32),
                        pltpu.VMEM((16, 256), F32), pltpu.VMEM((8, 768), F32)])


def _mixer_bwd(proj, dxn, wot, mix, sg, ss, prm, gw, pw, cmat, mask, name, rider=None):
    proj, tail = proj
    t = proj.shape[0]
    nt = t // TB
    rev = lambda i: nt - 1 - i

    def body(p_ref, hp_ref, t_ref, dxn_ref, wot_ref, mix_ref, sg_ref, ss_ref, prm_ref, gw_ref, pw_ref, cm_ref, mk_ref,
             dp_ref, sgc_ref, dwo_ref,
             gg_s, gs_s, h_dcv, h_dpl, h_dpre, gsm_ref, dgw_ref, dpw_ref, dm_ref):
        i = pl.program_id(0)
        tile = nt - 1 - i

        @pl.when(i == 0)
        def _():
            for r in (gg_s, gs_s, h_dcv, h_dpl, h_dpre, gsm_ref, dgw_ref, dpw_ref, dwo_ref):
                r[...] = jnp.zeros_like(r)

        dxn = dxn_ref[...].astype(BF16)
        dm_ref[...] = jnp.dot(dxn, wot_ref[...], preferred_element_type=F32)
        dwo_ref[...] += _dot_tn(mix_ref[...], dxn)

        lane = _iota((1, 256), 1)
        first = (tile > 0).astype(F32)
        ah, ac = p_ref[:, C_AH:C_AH + 256].astype(F32), p_ref[:, C_AC:C_AC + 256].astype(F32)
        ab, az = p_ref[:, C_AB:C_AB + 256].astype(F32), p_ref[:, C_AZ:C_AZ + 256].astype(F32)
        w0, w1, w2 = (prm_ref[R_CAW + j:R_CAW + j + 1, 0:256] for j in range(3))
        u = ac * ah
        ext = jnp.concatenate([(hp_ref[:, C_AC:C_AC + 256].astype(F32) * hp_ref[:, C_AH:C_AH + 256].astype(F32))[8:16] * first, u], axis=0)
        u1, u2 = _dn(ext, 1, TB, 8), _dn(ext, 2, TB, 8)
        cv = w2 * u + w1 * u1 + w0 * u2
        g = dm_ref[:, 0:256]
        sz, dsz = _silu_pair(az)
        dp_ref[:, C_AB:C_AB + 256] = (g * cv * sz).astype(BF16)
        dp_ref[:, C_AZ:C_AZ + 256] = (g * ab * cv * dsz).astype(BF16)
        dcv = g * ab * sz
        dext = jnp.concatenate([dcv, h_dcv[...]], axis=0)
        du = w2 * dcv + w1 * _up(dext, 1, TB) + w0 * _up(dext, 2, TB)
        dp_ref[:, C_AC:C_AC + 256] = (du * ah).astype(BF16)
        dp_ref[:, C_AH:C_AH + 256] = (du * ac).astype(BF16)
        gsm_ref[R_CAW:R_CAW + 1, 0:256] += _cs(dcv * u2)
        gsm_ref[R_CAW + 1:R_CAW + 2, 0:256] += _cs(dcv * u1)
        gsm_ref[R_CAW + 2:R_CAW + 3, 0:256] += _cs(dcv * u)
        h_dcv[...] = dcv[0:8, :]
        pu, pz = p_ref[:, C_PU:C_PU + 256].astype(F32), p_ref[:, C_PZ:C_PZ + 256].astype(F32)
        psc = prm_ref[R_PSC:R_PSC + 1, 0:256]
        icnt = _pool_inv_count(tile, TB)
        ext = jnp.concatenate([hp_ref[:, C_PU:C_PU + 256].astype(F32) * first, pu], axis=0)
        pooled = _winsum_dn(ext, lane)[16:] * icnt - pu
        pw_v = pw_ref[...]
        mixed = _dot(pooled, pw_v)
        g = dm_ref[:, 512:768]
        sz, dsz = _silu_pair(pz)
        gsm_ref[R_PSC:R_PSC + 1, 0:256] += _cs(g * mixed * sz)
        dp_ref[:, C_PZ:C_PZ + 256] = (g * psc * mixed * dsz).astype(BF16)
        dmixed = g * psc * sz
        dpw_ref[...] += _dot_tn(pooled, dmixed)
        dpooled = _dot_nt(dmixed, pw_v)
        qd = dpooled * icnt
        dext = jnp.concatenate([qd, h_dpl[...]], axis=0)
        dp_ref[:, C_PU:C_PU + 256] = (_winsum_up(dext, lane)[:TB] - dpooled).astype(BF16)
        h_dpl[...] = qd[0:16, :]
        sx = p_ref[:, C_SX:C_SX + 768].astype(F32)
        cw = [prm_ref[R_SCW + j:R_SCW + j + 1, :] for j in range(4)]
        ext = jnp.concatenate([hp_ref[:, C_SX:C_SX + 768].astype(F32)[8:16] * first, sx], axis=0)
        sx1, sx2, sx3 = _dn(ext, 1, TB, 8), _dn(ext, 2, TB, 8), _dn(ext, 3, TB, 8)
        cpre = cw[3] * sx + cw[2] * sx1 + cw[1] * sx2 + cw[0] * sx3 + prm_ref[R_SCB:R_SCB + 1, :]
        xc, dxc = _silu_pair(cpre)
        xs, bm, cm = xc[:, 0:256], xc[:, 256:512], xc[:, 512:768]

        gw_v = gw_ref[...]
        tail, pre, dtin, dte, dec, kd, wdec, w, xw, d_s, et, ut_g, ut_s = _mixer_tile_prep(p_ref, t_ref, xc, prm_ref,
                                                                                          gw_v, cm_ref, mk_ref)
        gmean = cm_ref[2]
        mask_t = mk_ref[...]
        gnw = prm_ref[R_GNW:R_GNW + 1, 0:256]
        a_e = prm_ref[R_AE:R_AE + 1, 0:256]
        d_e = prm_ref[R_DE:R_DE + 1, 0:256]
        snw = prm_ref[R_SNW:R_SNW + 1, 0:256]
        sg_in = [sg_ref[c] for c in range(NCH)]
        ss_in = [ss_ref[c] for c in range(NCH)]
        sg_n = [sg_in[c] * d_s[c] + ut_g[c] for c in range(NCH)]
        ss_n = [ss_in[c] * et[c] + ut_s[c] for c in range(NCH)]
        qs = _chunks(p_ref[:, C_GQ:C_GQ + 128].astype(F32) * GLA_SCALE)
        cm_c, bm_c, xw_c, kd_c = _chunks(cm), _chunks(bm), _chunks(xw), _chunks(kd)
        v_c = _chunks(p_ref[:, C_GV:C_GV + 256].astype(F32))
        o = jnp.concatenate([_dot_nt(qs[c], sg_n[c]) for c in range(NCH)], axis=0)
        y = jnp.concatenate([_halves(_dot, cm_c[c], ss_n[c]) for c in range(NCH)], axis=0) + d_e * xs
        gz = p_ref[:, C_GZ:C_GZ + 256].astype(F32)
        r = lax.rsqrt(_dot2_l(o * o, gmean) + EPS)
        on = o * r
        dyb = dm_ref[:, 256:512]
        sz, dsz = _silu_pair(gz)
        dp_ref[:, C_GZ:C_GZ + 256] = (dyb * on * gnw * dsz).astype(BF16)
        tg = dyb * sz
        gsm_ref[R_GNW:R_GNW + 1, 0:256] += _cs(tg * on)
        don = tg * gnw
        do_c = _chunks(r * (don - on * _dot2_l(don * on, gmean)))
        ssz = p_ref[:, C_SZ:C_SZ + 256].astype(F32)
        sil, dsil = _silu_pair(ssz)
        y2 = y * sil
        r = lax.rsqrt(jnp.mean(y2 * y2, axis=-1, keepdims=True) + EPS)
        yn = y2 * r
        dyd = dm_ref[:, 768:1024]
        gsm_ref[R_SNW:R_SNW + 1, 0:256] += _cs(dyd * yn)
        dn = dyd * snw
        dy2 = r * (dn - yn * jnp.mean(dn * yn, axis=-1, keepdims=True))
        dp_ref[:, C_SZ:C_SZ + 256] = (dy2 * y * dsil).astype(BF16)
        dy = dy2 * sil
        gsm_ref[R_DE:R_DE + 1, 0:256] += _cs(dy * xs)
        dy_c = _chunks(dy)
        dq = jnp.concatenate([_dot(do_c[c], sg_n[c]) for c in range(NCH)], axis=0)
        dp_ref[:, C_GQ:C_GQ + 128] = (dq * GLA_SCALE).astype(BF16)
        dcm = jnp.concatenate([_halves(_dot_nt, dy_c[c], ss_n[c]) for c in range(NCH)], axis=0)
        gg = [_dot_tn(do_c[c], qs[c]) * mask_t for c in range(NCH)]
        gs = [_halves(_dot_tn, cm_c[c], dy_c[c]) for c in range(NCH)]
        car_g, car_s = gg_s[...], gs_s[...]
        for c in reversed(range(NCH)):
            gg[c] = gg[c] + car_g
            gs[c] = gs[c] + car_s
            car_g = gg[c] * d_s[c]
            car_s = gs[c] * et[c]
        gg_s[...] = car_g
        gs_s[...] = car_s
        dkd = jnp.concatenate([_dot(v_c[c], gg[c]) for c in range(NCH)], axis=0)
        dp_ref[:, C_GV:C_GV + 256] = jnp.concatenate([_dot_nt(kd_c[c], gg[c]) for c in range(NCH)], axis=0).astype(BF16)
        dp_ref[:, C_GK:C_GK + 128] = (dkd * dec).astype(BF16)
        dbm = jnp.concatenate([_halves(_dot_nt, xw_c[c], gs[c]) for c in range(NCH)], axis=0)
        dxw = jnp.concatenate([_halves(_dot, bm_c[c], gs[c]) for c in range(NCH)], axis=0)
        dxs = dy * d_e + dxw * w
        dw = dxw * xs
        dsuf = _chunk_sums(cm_ref[1], jnp.concatenate([dkd * kd, dw * dte * wdec], axis=1))
        tot_g = jnp.concatenate([jnp.broadcast_to(_cs(gg[c] * sg_in[c]) * d_s[c], (CH, 128)) for c in range(NCH)], axis=0)
        tot_s = jnp.concatenate([jnp.broadcast_to(_cs(gs[c] * ss_in[c]) * et[c], (CH, 256)) for c in range(NCH)], axis=0)
        dpre = (dsuf[:, 0:128] + tot_g) * INV_TAU * jax.nn.sigmoid(-pre)
        dgw_ref[...] += _dot_tn(tail, dpre)
        gsm_ref[R_GB:R_GB + 1, 0:128] += _cs(dpre)
        dda = dsuf[:, 128:384] + tot_s
        gsm_ref[R_AE:R_AE + 1, 0:256] += _cs(dda * dte)
        dtail_s = _dot2_nt(dw * wdec + dda * a_e, cm_ref[3, 0:128, :]) * jax.nn.sigmoid(dtin)
        gsm_ref[R_DTB:R_DTB + 1, 0:128] += _cs(dtail_s)
        dp_ref[:, C_TL:C_TL + 128] = (_dot_nt(dpre, gw_v) + dtail_s).astype(BF16)
        dpre_c = jnp.concatenate([dxs, dbm, dcm], axis=1) * dxc
        dext = jnp.concatenate([dpre_c, h_dpre[...]], axis=0)
        dp_ref[:, C_SX:C_SX + 768] = (cw[3] * dpre_c + cw[2] * _up(dext, 1, TB) + cw[1] * _up(dext, 2, TB)
                                      + cw[0] * _up(dext, 3, TB)).astype(BF16)
        gsm_ref[R_SCW + 3:R_SCW + 4, :] += _cs(dpre_c * sx)
        gsm_ref[R_SCW + 2:R_SCW + 3, :] += _cs(dpre_c * sx1)
        gsm_ref[R_SCW + 1:R_SCW + 2, :] += _cs(dpre_c * sx2)
        gsm_ref[R_SCW:R_SCW + 1, :] += _cs(dpre_c * sx3)
        gsm_ref[R_SCB:R_SCB + 1, :] += _cs(dpre_c)
        h_dpre[...] = dpre_c[0:8, :]

        @pl.when(i == nt - 1)
        def _():
            ri, ci = _iota((256, 256), 0), _iota((256, 256), 1)
            per_head = jnp.where((ri >> 6) == ci, 1.0, 0.0).astype(BF16)
            per_dv = jnp.where((ri & 63) == ci, 1.0, 0.0).astype(BF16)
            row = _iota((8, 256), 0)
            top = gsm_ref[0:8, 0:256]
            sgc_ref[0:8, 0:256] = jnp.where(row == R_GNW, _dot3_l(top, per_dv), top)
            bot = gsm_ref[8:16, 0:256]
            fold = _dot3_l(jnp.where(row == R_AE - 8, bot * a_e, bot), per_head)
            sgc_ref[8:16, 0:256] = jnp.where((row == R_AE - 8) | (row == R_DE - 8), fold, bot)
            sgc_ref[0:16, 256:768] = gsm_ref[:, 256:768]
            sgc_ref[0:16, 768:896] = dgw_ref[0:16, :]
            sgc_ref[0:16, 896:1024] = jnp.zeros((16, 128), F32)
            diag = _pool_lane_select(lane, dpw_ref[0:64, :], dpw_ref[64:128, :], dpw_ref[128:192, :], dpw_ref[192:256, :])
            for q in range(4):
                sgc_ref[16:32, 256 * q:256 * q + 256] = diag[16 * q:16 * q + 16, :]

    return _call(
        body, (proj, proj, tail, dxn, wot, mix, sg, ss, prm, gw, pw, cmat, mask), grid=(nt,), name=name,
        sem=("arbitrary",), rider=rider,
        in_specs=[pl.BlockSpec((TB, NPM), lambda i: (rev(i), 0)),
                  pl.BlockSpec((16, NPM), lambda i: (jnp.maximum(rev(i) * (TB // 16) - 1, 0), 0)),
                  pl.BlockSpec((TB, NP - NPM), lambda i: (rev(i), 0)),
                  pl.BlockSpec((TB, D), lambda i: (rev(i), 0)), pl.BlockSpec((D, D), lambda i: (0, 0)),
                  pl.BlockSpec((TB, D), lambda i: (rev(i), 0)),
                  pl.BlockSpec((NCH, 256, 128), lambda i: (rev(i), 0, 0)),
                  pl.BlockSpec((NCH, 128, 256), lambda i: (rev(i), 0, 0)),
                  pl.BlockSpec((16, 768), lambda i: (0, 0)), pl.BlockSpec((128, 128), lambda i: (0, 0)),
                  pl.BlockSpec((256, 256), lambda i: (0, 0)), pl.BlockSpec((4, 256, 256), lambda i: (0, 0, 0)),
                  pl.BlockSpec((256, 128), lambda i: (0, 0))],
        out_specs=[pl.BlockSpec((TB, NP), lambda i: (rev(i), 0)), pl.BlockSpec((32, 1024), lambda i: (0, 0)),
                   pl.BlockSpec((D, D), lambda i: (0, 0))],
        out_shape=[jax.ShapeDtypeStruct((t, NP), BF16), jax.ShapeDtypeStruct((32, 1024), F32),
                   jax.ShapeDtypeStruct((D, D), F32)],
        scratch_shapes=[pltpu.VMEM((256, 128), F32), pltpu.VMEM((128, 256), F32), pltpu.VMEM((8, 256), F32),
                        pltpu.VMEM((16, 256), F32), pltpu.VMEM((8, 768), F32), pltpu.VMEM((16, 768), F32),
                        pltpu.VMEM((128, 128), F32), pltpu.VMEM((256, 256), F32), pltpu.VMEM((TB, D), F32)])


SHARD = NPROJ // 4
SHARD_PAD = 896


def _ranges_to_perm(o, n):
    out, p = [], 0
    for start, size in _PERM:
        a, b = max(o, start), min(o + n, start + size)
        if a < b:
            out.append((a, b - a, p + a - start))
        p += size
    return out


def _ranges_to_orig(p0, n):
    out, p = [], 0
    for start, size in _PERM:
        a, b = max(p0, p), min(p0 + n, p + size)
        if a < b:
            out.append((a, b - a, start + a - p))
        p += size
    return out


def _lane_window(load, lo, n, d, lane):
    a = 128 * (lo // 128)
    off = lo - a
    w = 128 if off + n <= 128 else 256
    chunk = load(a, w)
    shift = (d - off) % w
    if shift:
        chunk = pltpu.roll(chunk, shift, axis=1)
    return jnp.where((lane >= d) & (lane < d + n), chunk[:, 0:128], 0.0)


def _assemble_w_in(slabs, name, rb=256):
    def body(s_ref, wp_ref, wpt_ref):
        lane = _iota((1, 128), 1)
        for b in range(NP // 128):
            acc = jnp.zeros((rb, 128), F32)
            for p, n, o in _ranges_to_orig(128 * b, 128):
                while n > 0:
                    s, lo = o // SHARD, o % SHARD
                    cnt = min(n, SHARD - lo)
                    acc = acc + _lane_window(lambda a, w, s=s: s_ref[s, :, a:a + w].astype(F32), lo, cnt, p - 128 * b, lane)
                    o, p, n = o + cnt, p + cnt, n - cnt
            wp_ref[:, 128 * b:128 * b + 128] = acc.astype(BF16)
            wpt_ref[128 * b:128 * b + 128, :] = acc.T.astype(BF16)

    return pl.pallas_call(
        body, grid=(D // rb,), name=name,
        in_specs=[pl.BlockSpec((4, rb, SHARD_PAD), lambda i: (0, i, 0))],
        out_specs=[pl.BlockSpec((rb, NP), lambda i: (i, 0)), pl.BlockSpec((NP, rb), lambda i: (0, i))],
        out_shape=[jax.ShapeDtypeStruct((D, NP), BF16), jax.ShapeDtypeStruct((NP, D), BF16)],
        compiler_params=_cparams(("parallel",)))(slabs)


def _split_dw_in(dwp, name, rb=256):
    rows = dwp.shape[0]

    def body(g_ref, o_ref):
        lane = _iota((1, 128), 1)
        for s in range(4):
            for k in range(SHARD_PAD // 128):
                acc = jnp.zeros((rb, 128), F32)
                n_valid = min(128, SHARD - 128 * k)
                for o, n, p in _ranges_to_perm(SHARD * s + 128 * k, n_valid):
                    acc = acc + _lane_window(lambda a, w: g_ref[:, a:a + w].astype(F32), p, n, o - SHARD * s - 128 * k, lane)
                o_ref[s, :, 128 * k:128 * k + 128] = acc.astype(o_ref.dtype)

    return pl.pallas_call(
        body, grid=(rows // rb,), name=name,
        in_specs=[pl.BlockSpec((rb, NP), lambda i: (i, 0))],
        out_specs=pl.BlockSpec((4, rb, SHARD_PAD), lambda i: (0, i, 0)),
        out_shape=jax.ShapeDtypeStruct((4, rows, SHARD_PAD), dwp.dtype),
        compiler_params=_cparams(("parallel",)))(dwp)


def _half(c, n):
    return pl.ds(pl.multiple_of(c * (n // 2), n // 2), n // 2)


def _other_chips(x, y):
    return ((1 - x, y), (x, 1 - y), (1 - x, 1 - y))


def _remote(src, dst, send, recv, k, dev):
    return pltpu.make_async_remote_copy(src_ref=src, dst_ref=dst, send_sem=send.at[k], recv_sem=recv.at[k], device_id=dev,
                                        device_id_type=MESH)


def _sem(n):
    return pltpu.SemaphoreType.DMA((n,))


def _rider_gather_ici(shards, extra=None):
    shards = tuple(shards) + ((extra,) if extra is not None else ())
    n = len(shards)

    def copies(rins, routs, sems, arrivals=True):
        send, recv = sems
        x, y, c = _place()
        me = 2 * x + y
        out, inc = [], []
        for j, (px, py) in enumerate(_other_chips(x, y)):
            for k in range(n):
                whole = extra is not None and k == n - 1
                rows = pl.ds(0, shards[k].shape[0]) if whole else _half(c, shards[k].shape[0])
                out.append(_remote(rins[k].at[rows], routs[k].at[me, rows], send, recv, n * j + k, (px, py, c)))
                if arrivals:
                    inc.append(_remote(rins[k].at[rows], routs[k].at[2 * px + py, rows], send, recv, n * j + k, (px, py, c)))
        return out, inc

    def start(rins, routs, sems):
        for cp in copies(rins, routs, sems, arrivals=False)[0]:
            cp.start()

    def finish(rins, routs, sems):
        out, inc = copies(rins, routs, sems)
        for cp in inc:
            cp.wait_recv()
        for cp in out:
            cp.wait_send()

    return _Rider(shards, [jax.ShapeDtypeStruct((4,) + a.shape, a.dtype) for a in shards], [_sem(3 * n), _sem(3 * n)],
                  start, finish)


def _rider_gather_d2d(slabs):
    slabs = tuple(slabs)
    n = len(slabs)

    def copies(routs, sems, arrivals=True):
        send, recv = sems
        x, y, c = _place()
        out, inc = [], []
        for j, (px, py) in enumerate(_other_chips(x, y)):
            for k in range(n):
                rows = slabs[k].shape[1]
                mine, theirs = routs[k].at[2 * px + py, _half(c, rows)], routs[k].at[2 * px + py, _half(1 - c, rows)]
                out.append(_remote(mine, mine, send, recv, n * j + k, (x, y, 1 - c)))
                if arrivals:
                    inc.append(_remote(theirs, theirs, send, recv, n * j + k, (x, y, 1 - c)))
        return out, inc

    def start(rins, routs, sems):
        for cp in copies(routs, sems, arrivals=False)[0]:
            cp.start()

    def finish(rins, routs, sems):
        out, inc = copies(routs, sems)
        for cp in inc:
            cp.wait_recv()
        for cp in out:
            cp.wait_send()

    return _Rider(slabs, [jax.ShapeDtypeStruct(a.shape, a.dtype) for a in slabs], [_sem(3 * n), _sem(3 * n)], start, finish,
                  aliases={k: k for k in range(n)})


def _rider_swap(parts):
    parts = tuple(parts)
    n = len(parts)

    def copies(rins, routs, sems):
        send, recv = sems
        x, y, c = _place()
        return [_remote(rins[k].at[:, _half(1 - c, parts[k].shape[1])], routs[k], send, recv, k, (x, y, 1 - c))
                for k in range(n)]

    def start(rins, routs, sems):
        for cp in copies(rins, routs, sems):
            cp.start()

    def finish(rins, routs, sems):
        for cp in copies(rins, routs, sems):
            cp.wait()

    return _Rider(parts, [jax.ShapeDtypeStruct((a.shape[0], a.shape[1] // 2, a.shape[2]), a.dtype) for a in parts],
                  [_sem(n), _sem(n)], start, finish)


def _rider_scatter(parts):
    parts = tuple(parts)
    n = len(parts)

    def copies(rins, routs, sems, arrivals=True):
        send, recv = sems
        x, y, c = _place()
        me = 2 * x + y
        out, inc = [], []
        for j, (px, py) in enumerate(_other_chips(x, y)):
            for k in range(n):
                out.append(_remote(rins[k].at[2 * px + py], routs[k].at[me], send, recv, n * j + k, (px, py, c)))
                if arrivals:
                    inc.append(_remote(rins[k].at[me], routs[k].at[2 * px + py], send, recv, n * j + k, (px, py, c)))
        return out, inc

    def start(rins, routs, sems):
        for cp in copies(rins, routs, sems, arrivals=False)[0]:
            cp.start()

    def finish(rins, routs, sems):
        out, inc = copies(rins, routs, sems)
        for cp in inc:
            cp.wait_recv()
        for cp in out:
            cp.wait_send()

    return _Rider(parts, [jax.ShapeDtypeStruct(a.shape, a.dtype) for a in parts], [_sem(3 * n), _sem(3 * n)], start, finish)


def _rider_share(fulls):
    fulls = tuple(fulls)
    n = len(fulls)

    def copies(routs, sems, arrivals=True):
        send, recv = sems
        x, y, c = _place()
        out, inc = [], []
        for k in range(n):
            mine, theirs = routs[k].at[_half(c, fulls[k].shape[0])], routs[k].at[_half(1 - c, fulls[k].shape[0])]
            out.append(_remote(mine, mine, send, recv, k, (x, y, 1 - c)))
            if arrivals:
                inc.append(_remote(theirs, theirs, send, recv, k, (x, y, 1 - c)))
        return out, inc

    def start(rins, routs, sems):
        for cp in copies(routs, sems, arrivals=False)[0]:
            cp.start()

    def finish(rins, routs, sems):
        out, inc = copies(routs, sems)
        for cp in inc:
            cp.wait_recv()
        for cp in out:
            cp.wait_send()

    return _Rider(fulls, [jax.ShapeDtypeStruct(a.shape, a.dtype) for a in fulls], [_sem(n), _sem(n)], start, finish,
                  aliases={k: k for k in range(n)})


def _pair_sum(core, full, recv, name, br=128):
    n, rows, cols = recv.shape

    def body(c_ref, a_ref, b_ref, o_ref):
        o_ref[...] = (a_ref[...] + b_ref[...]).astype(BF16)

    nb = rows // br
    return pl.pallas_call(
        body, name=name, out_shape=jax.ShapeDtypeStruct(recv.shape, BF16),
        grid_spec=pltpu.PrefetchScalarGridSpec(
            num_scalar_prefetch=1, grid=(n, nb),
            in_specs=[pl.BlockSpec((1, br, cols), lambda i, j, c: (i, c[0] * nb + j, 0)),
                      pl.BlockSpec((1, br, cols), lambda i, j, c: (i, j, 0))],
            out_specs=pl.BlockSpec((1, br, cols), lambda i, j, c: (i, j, 0))),
        compiler_params=_cparams(("parallel", "parallel")))(core, full, recv)


def _chip_sum(place, gathered, mine, name, br=128):
    _, r, c = gathered.shape
    nb = r // br

    def body(p_ref, g_ref, m_ref, o_ref):
        slab = lambda j: jnp.where(p_ref[1] == j, m_ref[j], g_ref[j]).astype(F32)
        o_ref[...] = ((slab(0) + slab(1)) + slab(2)) + slab(3)

    return pl.pallas_call(
        body, name=name, out_shape=jax.ShapeDtypeStruct((2 * r, c), F32),
        grid_spec=pltpu.PrefetchScalarGridSpec(
            num_scalar_prefetch=1, grid=(nb,),
            in_specs=[pl.BlockSpec((4, br, c), lambda i, p: (0, i, 0)), pl.BlockSpec((4, br, c), lambda i, p: (0, i, 0))],
            out_specs=pl.BlockSpec((br, c), lambda i, p: (p[0] * nb + i, 0))),
        compiler_params=_cparams(("parallel",)))(place, gathered, mine)


def _adamw(w, g, m, v, name, br):
    n, r, c = w.shape

    def body(w_ref, g_ref, m_ref, v_ref, d_ref, m2_ref, v2_ref):
        d_ref[...], m2_ref[...], v2_ref[...] = _adam_math(w_ref[...], g_ref[...], m_ref[...], v_ref[...])

    spec = pl.BlockSpec((1, br, c), lambda i, j: (i, j, 0))
    shp = jax.ShapeDtypeStruct(w.shape, F32)
    return pl.pallas_call(body, grid=(n, r // br), name=name, in_specs=[spec] * 4, out_specs=[spec] * 3,
                          out_shape=[shp] * 3, compiler_params=_cparams(("parallel", "parallel")))(w, g, m, v)


def _adamw_w_in(w, g, m, v, name, bc=31):
    cols = w.shape[2]
    lead = lambda a: jnp.transpose(a, (2, 0, 1))
    g = jnp.stack([a[:, 0:cols] for a in g])

    def body(w_ref, g_ref, m_ref, v_ref, go_ref, d_ref, m2_ref, v2_ref):
        for l in range(2):
            gv = g_ref[:, l, :]
            d_ref[:, l, :], m2_ref[:, l, :], v2_ref[:, l, :] = _adam_math(w_ref[:, l, :], gv, m_ref[:, l, :], v_ref[:, l, :])
            go_ref[:, l, :] = gv

    spec = pl.BlockSpec((bc, 2, D), lambda i: (i, 0, 0))
    outs = pl.pallas_call(body, grid=(cols // bc,), name=name, in_specs=[spec] * 4, out_specs=[spec] * 4,
                          out_shape=[jax.ShapeDtypeStruct((cols, 2, D), F32)] * 4,
                          compiler_params=_cparams(("parallel",)))(lead(w), lead(g), lead(m), lead(v))
    return [jnp.transpose(o, (1, 2, 0)) for o in outs]


_SMALL_NAMES = ("norm_w", "conv_a_w", "gla_gate_w", "gla_gate_b", "gla_norm_w", "pool_w", "pool_scale", "ssd_conv_w",
                "ssd_conv_b", "ssd_dt_bias", "ssd_a_log", "ssd_d", "ssd_norm_w", "final_norm_w")
SMALL_ROWS = 80


def _adam_math(w, g, m, v):
    m2 = ADAM_B1 * m + (1.0 - ADAM_B1) * g
    v2 = ADAM_B2 * v + (1.0 - ADAM_B2) * (g * g)
    m_hat = m2 / (1.0 - ADAM_B1 ** ADAM_STEP)
    v_hat = v2 / (1.0 - ADAM_B2 ** ADAM_STEP)
    return -ADAM_LR * (m_hat / (jnp.sqrt(v_hat) + ADAM_EPS) + ADAM_WD * w), m2, v2


def _small_slices(name, chip):
    if name == "conv_a_w":
        return [((), slice(R_CAW, R_CAW + 3), slice(64 * chip, 64 * chip + 64))]
    if name == "ssd_conv_w":
        return [((), slice(R_SCW, R_SCW + 4), slice(192 * chip, 192 * chip + 192))]
    if name == "gla_gate_w":
        return [((), slice(0, 16), slice(768, 896))]
    if name == "pool_w":
        return [((g, slice(16 * q, 16 * q + 16)), slice(16, 32), slice(256 * q + 64 * g, 256 * q + 64 * g + 64))
                for g in range(4) for q in range(4)]
    row, lanes = {"gla_gate_b": (R_GB, slice(0, 128)), "gla_norm_w": (R_GNW, slice(0, 64)),
                  "pool_scale": (R_PSC, slice(0, 256)), "ssd_conv_b": (R_SCB, slice(0, 768)),
                  "ssd_dt_bias": (R_DTB, slice(16, 20)), "ssd_a_log": (R_AE, slice(0, 4)), "ssd_d": (R_DE, slice(0, 4)),
                  "ssd_norm_w": (R_SNW, slice(0, 256))}[name]
    return [((), slice(row, row + 1), lanes)]


def _rider_exchange(block):
    def copies(rins, routs, sems):
        send, recv = sems
        x, y, c = _place()
        flip = lambda v, bit: 1 - v if bit else v
        return [_remote(rins[0], routs[0].at[k], send, recv, k - 1, (flip(x, k & 4), flip(y, k & 2), flip(c, k & 1)))
                for k in range(1, 8)]

    def start(rins, routs, sems):
        for cp in copies(rins, routs, sems):
            cp.start()

    def finish(rins, routs, sems):
        for cp in copies(rins, routs, sems):
            cp.wait()

    return _Rider((block,), [jax.ShapeDtypeStruct((8,) + block.shape, block.dtype)], [_sem(7), _sem(7)], start, finish)


def _join_riders(a, b):
    na, oa, sa = len(a.inputs), len(a.out_shapes), len(a.sems)

    def start(rins, routs, sems):
        a.start(rins[:na], routs[:oa], sems[:sa])
        b.start(rins[na:], routs[oa:], sems[sa:])

    def finish(rins, routs, sems):
        a.finish(rins[:na], routs[:oa], sems[:sa])
        b.finish(rins[na:], routs[oa:], sems[sa:])

    aliases = {**a.aliases, **{na + k: oa + v for k, v in b.aliases.items()}}
    return _Rider(a.inputs + b.inputs, a.out_shapes + b.out_shapes, a.sems + b.sems, start, finish, aliases)


def _small_adamw(blocks, w, m, v):
    n = len(_SMALL_NAMES)

    def body(*refs):
        (own, ex), (own0, ex0) = refs[0:2], refs[2:4]
        refs = refs[3:]
        w_refs, m_refs, v_refs = refs[1:1 + n], refs[1 + n:1 + 2 * n], refs[1 + 2 * n:1 + 3 * n]
        o = 1 + 3 * n
        g_out, d_out, m_out, v_out = refs[o:o + n], refs[o + n:o + 2 * n], refs[o + 2 * n:o + 3 * n], refs[o + 3 * n:o + 4 * n]
        loss_ref, acc, acc0 = refs[o + 4 * n:o + 4 * n + 3]
        chip = 2 * lax.axis_index("x") + lax.axis_index("y")
        me = 2 * chip + lax.axis_index("c")
        acc[...] = jnp.zeros_like(acc)
        acc0[...] = jnp.zeros_like(acc0)
        for src in range(8):
            @pl.when(me == src)
            def _():
                acc[...] += own[...]
                acc0[...] += own0[...]

            @pl.when(me != src)
            def _(src=src):
                acc[...] += ex[jnp.bitwise_xor(me, src)]
                acc0[...] += ex0[jnp.bitwise_xor(me, src)]

        loss_ref[...] = acc[73:74, 0:1]

        def update(i, idx, g):
            d, m2, v2 = _adam_math(w_refs[i][idx], g, m_refs[i][idx], v_refs[i][idx])
            g_out[i][idx], d_out[i][idx], m_out[i][idx], v_out[i][idx] = g, d, m2, v2

        for i, name in enumerate(_SMALL_NAMES):
            if name == "final_norm_w":
                update(i, (slice(0, 1), slice(None)), acc[72:73, :])
            elif name == "norm_w":
                update(i, (slice(0, 1), slice(None)), acc0[0:1, :])
                update(i, (slice(1, 2), slice(None)), acc[64:65, :])
            elif name in ("conv_a_w", "ssd_conv_w"):
                for s in range(4):
                    @pl.when(chip == s)
                    def _(i=i, name=name, s=s):
                        for l in range(2):
                            (_, rows, lanes), = _small_slices(name, s)
                            update(i, (l,), acc[rows.start + 32 * l:rows.stop + 32 * l, lanes])
            else:
                for l in range(2):
                    for idx, rows, lanes in _small_slices(name, 0):
                        g = acc[rows.start + 32 * l:rows.stop + 32 * l, lanes]
                        if w_refs[i].ndim == 2:
                            update(i, (slice(l, l + 1), slice(None)), g)
                        else:
                            update(i, (l,) + idx, g)

    args = [a for pair in blocks for a in pair] + [d[k] for d in (w, m, v) for k in _SMALL_NAMES]
    shapes = [jax.ShapeDtypeStruct(w[k].shape, F32) for k in _SMALL_NAMES]
    vmem = pl.BlockSpec(memory_space=pltpu.VMEM)
    outs = pl.pallas_call(body, name="small_adamw", in_specs=[vmem] * len(args), out_specs=[vmem] * (4 * n + 1),
                          out_shape=shapes * 4 + [jax.ShapeDtypeStruct((1, 1), F32)],
                          scratch_shapes=[pltpu.VMEM((SMALL_ROWS, D), F32), pltpu.VMEM((8, D), F32)])(*args)
    return outs[0:n], outs[n:2 * n], outs[2 * n:3 * n], outs[3 * n:4 * n], outs[4 * n]


def _mixer_consts(layer, conv_a_w, gla_gate_w, gla_gate_b, gla_norm_w, pool_w, pool_scale, ssd_conv_w, ssd_conv_b,
                  ssd_dt_bias, ssd_a_log, ssd_d, ssd_norm_w):
    def row(v):
        return jnp.pad(v.reshape(1, -1), ((0, 0), (0, 768 - v.size)))

    dtb = jnp.pad(ssd_dt_bias[layer], (16, 108))
    rows = [jnp.pad(conv_a_w[layer], ((0, 0), (0, 512))), row(gla_gate_b[layer]), row(jnp.tile(gla_norm_w[layer], 4)),
            row(pool_scale[layer]), row(ssd_conv_b[layer]), row(dtb), row(jnp.repeat(-jnp.exp(ssd_a_log[layer]), 64)),
            row(jnp.repeat(ssd_d[layer], 64)), row(ssd_norm_w[layer]), jnp.zeros((1, 768), F32), ssd_conv_w[layer]]
    prm = jnp.concatenate(rows, axis=0)
    gw = jnp.pad(gla_gate_w[layer], ((0, 112), (0, 0))).astype(BF16)
    on_diag = (_iota((256, 256), 0) >> 6) == (_iota((256, 256), 1) >> 6)
    pw = jnp.where(on_diag, jnp.tile(pool_w[layer].reshape(256, 64), (1, 4)), 0.0)
    return (prm, gw, pw.astype(BF16)) + _mixer_matrices()


def _grad_slabs(dwp, dwo):
    return dwp.reshape(1, D, NP), dwo.reshape(4, D // 4, D)


class _Comm:
    def __init__(self, w_in, w_out):
        self.w_in16 = jnp.pad(w_in.astype(BF16), ((0, 0), (0, 0), (0, SHARD_PAD - SHARD)))
        self.w_out16 = w_out.astype(BF16)
        self.core = lax.axis_index("c").astype(jnp.int32).reshape(1)
        self.chip = 2 * lax.axis_index("x") + lax.axis_index("y")
        self.place = jnp.stack([lax.axis_index("c"), self.chip]).astype(jnp.int32)

    def gather_ici(self, layer, extra=None):
        return _rider_gather_ici((self.w_in16[layer], self.w_out16[layer]), extra)

    def pair_sum(self, layer, slabs, received):
        d_in, d_out = [_pair_sum(self.core, a, b, name=f"reduce_pair_sum{layer}_{k}")
                       for k, (a, b) in enumerate(zip(slabs, received))]
        return [_split_dw_in(d_in[0], name=f"split_dw_in{layer}"), d_out]

    def chip_sum(self, layer, gathered, mine):
        return [_chip_sum(self.place, a, b, name=f"reduce_chip_sum{layer}_{k}") for k, (a, b) in enumerate(zip(gathered, mine))]

    def layer_weights(self, layer, s_in, s_out):
        own = lambda slabs, shard: jnp.stack([jnp.where(self.chip == s, shard, slabs[s]) for s in range(4)])
        wp, wpt = _assemble_w_in(own(s_in, self.w_in16[layer]), name=f"assemble_w_in{layer}")
        wo = own(s_out, self.w_out16[layer]).reshape(D, D)
        return wp, wpt, wo, wo.T


def _local_step(x, tgt, norm_w, final_norm_w, consts, wts0, wts1=None, comm=None):
    nw = [norm_w[l:l + 1] for l in range(2)]
    proj0, h0, slabs = _rmsproj(x, nw[0], wts0[0], name="rmsproj0", rider=comm and comm.gather_ici(1))
    (mix0, sg0, ss0, x1), slabs = _mixer_fwd(proj0, x, wts0[2], *consts[0], name="mixer_fwd0",
                                             rider=comm and _rider_gather_d2d(slabs))
    if comm:
        wts1 = comm.layer_weights(1, *slabs)
    proj1, h1, _ = _rmsproj(x1, nw[1], wts1[0], name="rmsproj1")
    (mix1, sg1, ss1, dx, head), _ = _mixer_fwd(proj1, x1, wts1[2], *consts[1], name="mixer_fwd1",
                                               head=(tgt, final_norm_w.reshape(1, D)))
    (dproj, mgr1, dwo1), _ = _mixer_bwd(proj1, dx, wts1[3], mix1, sg1, ss1, *consts[1], name="mixer_bwd1")
    dwp1, _ = _dwin(h1, dproj, name="dwin1")
    slabs1 = comm and _grad_slabs(dwp1, dwo1)
    (dx, dnw1), recv = _dxin(dproj, wts1[1], x1, dx, nw[1], name="dxin1", rider=comm and _rider_swap(slabs1))
    pairs1 = comm and comm.pair_sum(1, slabs1, recv)
    (dproj, mgr0, dwo0), gathered = _mixer_bwd(proj0, dx, wts0[3], mix0, sg0, ss0, *consts[0], name="mixer_bwd0",
                                               rider=comm and _rider_scatter(pairs1))
    dwp0, big1 = _dwin(h0, dproj, name="dwin0", rider=comm and _rider_share(comm.chip_sum(1, gathered, pairs1)))
    if not comm:
        (dx, dnw0), _ = _dxin(dproj, wts0[1], x, dx, nw[0], name="dxin0")
        return head, dx, ((dwp0, dwp1), (dwo0, dwo1)), (dnw0, dnw1), (mgr0, mgr1)
    slabs0 = _grad_slabs(dwp0, dwo0)
    pairs0 = comm.pair_sum(0, slabs0, _run_rider(_rider_swap(slabs0), "reduce_swap0"))
    small = jnp.concatenate([mgr0, mgr1, dnw1, head], axis=0)
    (dx, dnw0), gathered = _dxin(dproj, wts0[1], x, dx, nw[0], name="dxin0",
                                 rider=_join_riders(_rider_scatter(pairs0), _rider_exchange(small)))
    big0 = _run_rider(_rider_share(comm.chip_sum(0, gathered[0:2], pairs0)), "reduce_share0")
    blocks = ((small, gathered[2]), (dnw0, _run_rider(_rider_exchange(dnw0), "exchange_dnw0")[0]))
    return dx, ((big0[0], big1[0]), (big0[1], big1[1])), blocks


def kernel(x, norm_w, w_in, conv_a_w, gla_gate_w, gla_gate_b, gla_norm_w, pool_w, pool_scale, ssd_conv_w, ssd_conv_b, ssd_dt_bias, ssd_a_log, ssd_d, ssd_norm_w, w_out, final_norm_w, loss_target, m_norm_w, m_w_in, m_conv_a_w, m_gla_gate_w, m_gla_gate_b, m_gla_norm_w, m_pool_w, m_pool_scale, m_ssd_conv_w, m_ssd_conv_b, m_ssd_dt_bias, m_ssd_a_log, m_ssd_d, m_ssd_norm_w, m_w_out, m_final_norm_w, v_norm_w, v_w_in, v_conv_a_w, v_gla_gate_w, v_gla_gate_b, v_gla_norm_w, v_pool_w, v_pool_scale, v_ssd_conv_w, v_ssd_conv_b, v_ssd_dt_bias, v_ssd_a_log, v_ssd_d, v_ssd_norm_w, v_w_out, v_final_norm_w):
    weights = dict(norm_w=norm_w, w_in=w_in, conv_a_w=conv_a_w, gla_gate_w=gla_gate_w, gla_gate_b=gla_gate_b,
                   gla_norm_w=gla_norm_w, pool_w=pool_w, pool_scale=pool_scale, ssd_conv_w=ssd_conv_w,
                   ssd_conv_b=ssd_conv_b, ssd_dt_bias=ssd_dt_bias, ssd_a_log=ssd_a_log, ssd_d=ssd_d,
                   ssd_norm_w=ssd_norm_w, w_out=w_out, final_norm_w=final_norm_w)
    m_in = dict(norm_w=m_norm_w, w_in=m_w_in, conv_a_w=m_conv_a_w, gla_gate_w=m_gla_gate_w, gla_gate_b=m_gla_gate_b,
                gla_norm_w=m_gla_norm_w, pool_w=m_pool_w, pool_scale=m_pool_scale, ssd_conv_w=m_ssd_conv_w,
                ssd_conv_b=m_ssd_conv_b, ssd_dt_bias=m_ssd_dt_bias, ssd_a_log=m_ssd_a_log, ssd_d=m_ssd_d,
                ssd_norm_w=m_ssd_norm_w, w_out=m_w_out, final_norm_w=m_final_norm_w)
    v_in = dict(norm_w=v_norm_w, w_in=v_w_in, conv_a_w=v_conv_a_w, gla_gate_w=v_gla_gate_w, gla_gate_b=v_gla_gate_b,
                gla_norm_w=v_gla_norm_w, pool_w=v_pool_w, pool_scale=v_pool_scale, ssd_conv_w=v_ssd_conv_w,
                ssd_conv_b=v_ssd_conv_b, ssd_dt_bias=v_ssd_dt_bias, ssd_a_log=v_ssd_a_log, ssd_d=v_ssd_d,
                ssd_norm_w=v_ssd_norm_w, w_out=v_w_out, final_norm_w=v_final_norm_w)
    order = ("norm_w", "w_in", "conv_a_w", "gla_gate_w", "gla_gate_b", "gla_norm_w", "pool_w", "pool_scale",
             "ssd_conv_w", "ssd_conv_b", "ssd_dt_bias", "ssd_a_log", "ssd_d", "ssd_norm_w", "w_out", "final_norm_w")
    t = x.shape[1]

    comm = _Comm(w_in, w_out)
    cshard = jnp.zeros((16, 256), F32)
    for l in range(2):
        cshard = cshard.at[8 * l:8 * l + 3, 0:64].set(conv_a_w[l]).at[8 * l + 3:8 * l + 7, 0:192].set(ssd_conv_w[l])
    s_in, s_out, g_c = _run_rider(comm.gather_ici(0, cshard), "gather_ici0")
    s_in, s_out = _run_rider(_rider_gather_d2d((s_in, s_out)), "gather_d2d0")
    g_c = [jnp.where(comm.chip == s, cshard, g_c[s]) for s in range(4)]
    conv_a_full = jnp.stack([jnp.concatenate([g_c[s][8 * l:8 * l + 3, 0:64] for s in range(4)], axis=-1) for l in range(2)])
    ssd_conv_full = jnp.stack([jnp.concatenate([g_c[s][8 * l + 3:8 * l + 7, 0:192] for s in range(4)], axis=-1)
                               for l in range(2)])
    consts = [_mixer_consts(l, conv_a_full, gla_gate_w, gla_gate_b, gla_norm_w, pool_w, pool_scale, ssd_conv_full,
                            ssd_conv_b, ssd_dt_bias, ssd_a_log, ssd_d, ssd_norm_w) for l in range(2)]

    dx, big, blocks = _local_step(x.reshape(t, D), loss_target.reshape(t, D), norm_w, final_norm_w, consts,
                                  comm.layer_weights(0, s_in, s_out), comm=comm)

    as2d = lambda d: {k: (d[k].reshape(1, D) if k == "final_norm_w" else d[k]) for k in _SMALL_NAMES}
    small = _small_adamw(blocks, as2d(weights), as2d(m_in), as2d(v_in))
    grads, delta, new_m, new_v = ({k: (a.reshape(D) if k == "final_norm_w" else a) for k, a in zip(_SMALL_NAMES, part)}
                                  for part in small[0:4])
    loss = small[4].reshape(())

    grads["w_out"] = jnp.stack(big[1])

    grads["w_in"], delta["w_in"], new_m["w_in"], new_v["w_in"] = _adamw_w_in(w_in, big[0], m_w_in, v_w_in, name="adamw_w_in")
    delta["w_out"], new_m["w_out"], new_v["w_out"] = _adamw(w_out, grads["w_out"], m_w_out, v_w_out, name="adamw_w_out", br=256)

    return (loss, dx.reshape(1, t, D), *[grads[k] for k in order], *[delta[k] for k in order],
            *[new_m[k] for k in order], *[new_v[k] for k in order])
```

```python
import functools

import jax
import jax.numpy as jnp
from jax import lax
from jax.experimental import pallas as pl
from jax.experimental.pallas import tpu as pltpu

F32 = jnp.float32
BF16 = jnp.bfloat16
MESH = pl.DeviceIdType.MESH

D = 1024
CH = 64
EPS = 1e-6
NP = 3456
NPROJ = 3348
NPM = 3328
GLA_SCALE = 32.0 ** -0.5
INV_TAU = 1.0 / 16.0
TB = 512
NCH = TB // CH
assert TB % 256 == 0

C_AH, C_AB, C_AC, C_AZ, C_GQ, C_GK, C_GV = 0, 256, 512, 768, 1024, 1152, 1280
C_GZ, C_PU, C_PZ, C_SZ, C_SX, C_TL = 1536, 1792, 2048, 2304, 2560, 3328
_PERM = ((0, 1536), (1552, 1792), (1536, 16), (3344, 4))

R_CAW, R_GB, R_GNW, R_PSC, R_SCB, R_DTB, R_AE, R_DE, R_SNW, R_SCW = 0, 3, 4, 5, 6, 7, 8, 9, 10, 12

ADAM_LR, ADAM_B1, ADAM_B2, ADAM_EPS, ADAM_WD, ADAM_STEP = 0.001, 0.9, 0.999, 1e-08, 0.01, 10

VMEM_LIMIT = 56 * 1024 * 1024


def _cparams(sem, limit=VMEM_LIMIT):
    return pltpu.CompilerParams(dimension_semantics=sem, vmem_limit_bytes=limit)


_ANY = pl.BlockSpec(memory_space=pl.ANY)


def _place():
    return lax.axis_index("x"), lax.axis_index("y"), lax.axis_index("c")


class _Rider:
    def __init__(self, inputs, out_shapes, sems, start, finish, aliases=None):
        self.inputs, self.out_shapes, self.sems = tuple(inputs), tuple(out_shapes), tuple(sems)
        self.start, self.finish, self.aliases = start, finish, dict(aliases or {})


def _call(body, args, *, grid, in_specs, out_specs, out_shape, name, sem, scratch_shapes=(), rider=None):
    if rider is None:
        outs = pl.pallas_call(body, grid=grid, name=name, in_specs=list(in_specs), out_specs=list(out_specs),
                              out_shape=list(out_shape), scratch_shapes=list(scratch_shapes),
                              compiler_params=_cparams(sem))(*args)
        return list(outs), []
    ni, no, ns = len(args), len(out_shape), len(scratch_shapes)
    ri, ro = len(rider.inputs), len(rider.out_shapes)

    def full(*refs):
        ins, rins = refs[:ni], refs[ni:ni + ri]
        outs, routs = refs[ni + ri:ni + ri + no], refs[ni + ri + no:ni + ri + no + ro]
        scr, rsem = refs[ni + ri + no + ro:ni + ri + no + ro + ns], refs[ni + ri + no + ro + ns:]
        first = functools.reduce(jnp.logical_and, [pl.program_id(a) == 0 for a in range(len(grid))])
        last = functools.reduce(jnp.logical_and, [pl.program_id(a) == grid[a] - 1 for a in range(len(grid))])

        @pl.when(first)
        def _():
            rider.start(rins, routs, rsem)

        body(*ins, *outs, *scr)

        @pl.when(last)
        def _():
            rider.finish(rins, routs, rsem)

    outs = pl.pallas_call(
        full, grid=grid, name=name, in_specs=list(in_specs) + [_ANY] * ri, out_specs=list(out_specs) + [_ANY] * ro,
        out_shape=list(out_shape) + list(rider.out_shapes), scratch_shapes=list(scratch_shapes) + list(rider.sems),
        input_output_aliases={ni + k: no + v for k, v in rider.aliases.items()},
        compiler_params=_cparams(("arbitrary",) * len(grid)))(*args, *rider.inputs)
    return list(outs[:no]), list(outs[no:])


def _run_rider(rider, name):
    ri = len(rider.inputs)

    def body(*refs):
        rins, routs, rsem = refs[:ri], refs[ri:ri + len(rider.out_shapes)], refs[ri + len(rider.out_shapes):]
        rider.start(rins, routs, rsem)
        rider.finish(rins, routs, rsem)

    return list(pl.pallas_call(body, name=name, in_specs=[_ANY] * ri, out_specs=[_ANY] * len(rider.out_shapes),
                               out_shape=list(rider.out_shapes), scratch_shapes=list(rider.sems),
                               input_output_aliases=dict(rider.aliases))(*rider.inputs))


def _dot(a, b):
    return jnp.dot(a.astype(BF16), b.astype(BF16), preferred_element_type=F32)


def _dot_nt(a, b):
    return lax.dot_general(a.astype(BF16), b.astype(BF16), (((1,), (1,)), ((), ())), preferred_element_type=F32)


def _dot_tn(a, b):
    return lax.dot_general(a.astype(BF16), b.astype(BF16), (((0,), (0,)), ((), ())), preferred_element_type=F32)


def _split(a):
    hi = a.astype(BF16)
    lo = (a - hi.astype(F32)).astype(BF16)
    return hi, lo


def _dot2_l(a, b):
    hi, lo = _split(a)
    return _dot(hi, b) + _dot(lo, b)


def _dot2_r(a, b):
    hi, lo = _split(b)
    return _dot(a, hi) + _dot(a, lo)


def _dot3_l(a, b):
    hi, lo = _split(a)
    lo2 = ((a - hi.astype(F32)) - lo.astype(F32)).astype(BF16)
    return _dot(hi, b) + _dot(lo, b) + _dot(lo2, b)


def _dot2_nt(a, b):
    hi, lo = _split(a)
    return _dot_nt(hi, b) + _dot_nt(lo, b)


def _silu(z):
    return z * jax.nn.sigmoid(z)


def _lse1(x):
    return jnp.log(1.0 + jnp.exp(-jnp.abs(x)))


def _cs(a):
    return jnp.sum(a, axis=0, keepdims=True)


def _iota(shape, dim):
    return lax.broadcasted_iota(jnp.int32, shape, dim)


def _mixer_matrices():
    r, c = _iota((256, 256), 0), _iota((256, 256), 1)
    same_chunk = (r >> 6) == (c >> 6)
    mats = jnp.stack([jnp.where((c > r) & same_chunk, 1.0, 0.0), jnp.where((c < r) & same_chunk, 1.0, 0.0),
                      jnp.where(same_chunk, 1.0 / 64.0, 0.0), jnp.where((r < 128) & (r - 16 == (c >> 6)), 1.0, 0.0)])
    mask = jnp.where((_iota((256, 128), 0) >> 6) == (_iota((256, 128), 1) >> 5), 1.0, 0.0)
    return mats.astype(BF16), mask.astype(F32)


def _dn(ext, k, n, h):
    return pltpu.roll(ext, k, axis=0)[h:h + n]


def _up(ext, k, n):
    return pltpu.roll(ext, ext.shape[0] - k, axis=0)[:n]


def _pool_lane_select(lane, s2, s4, s8, s16):
    return jnp.where(lane < 64, s2, jnp.where(lane < 128, s4, jnp.where(lane < 192, s8, s16)))


def _winsum_dn(ext, lane):
    s2 = ext + pltpu.roll(ext, 1, axis=0)
    s4 = s2 + pltpu.roll(s2, 2, axis=0)
    s8 = s4 + pltpu.roll(s4, 4, axis=0)
    s16 = s8 + pltpu.roll(s8, 8, axis=0)
    return _pool_lane_select(lane, s2, s4, s8, s16)


def _winsum_up(ext, lane):
    m = ext.shape[0]
    s2 = ext + pltpu.roll(ext, m - 1, axis=0)
    s4 = s2 + pltpu.roll(s2, m - 2, axis=0)
    s8 = s4 + pltpu.roll(s4, m - 4, axis=0)
    s16 = s8 + pltpu.roll(s8, m - 8, axis=0)
    return _pool_lane_select(lane, s2, s4, s8, s16)


def _pool_inv_count(tile, n):
    lane = _iota((1, 256), 1)
    win = _pool_lane_select(lane, 2.0, 4.0, 8.0, 16.0).astype(F32)
    tpos = (tile * n + _iota((n, 1), 0) + 1).astype(F32)
    return jnp.where(tpos >= win, 1.0 / win, 1.0 / tpos)


def _silu_pair(z):
    s = jax.nn.sigmoid(z)
    return z * s, s * (1.0 + z * (1.0 - s))


def _chunks(a):
    return [a[c * CH:(c + 1) * CH] for c in range(a.shape[0] // CH)]


def _halves(fn, a, b):
    return jnp.concatenate([fn(a[:, 0:128], b[:, 0:128]), fn(a[:, 128:256], b[:, 128:256])], axis=1)


def _chunk_sums(tri, a):
    return jnp.concatenate([_dot2_r(tri, a[r:r + 256]) for r in range(0, a.shape[0], 256)], axis=0)


def _mixer_tile_prep(p_ref, t_ref, xc, prm_ref, gw_v, cm_ref, mk_ref):
    tail = t_ref[...]
    pre = _dot(tail, gw_v) + prm_ref[R_GB:R_GB + 1, 0:128]
    la = (jnp.minimum(pre, 0.0) - _lse1(pre)) * INV_TAU
    dtin = tail + prm_ref[R_DTB:R_DTB + 1, 0:128]
    dtf = jnp.maximum(dtin, 0.0) + _lse1(dtin)
    dte = _dot2_l(dtf, cm_ref[3, 0:128, :])
    da = dte * prm_ref[R_AE:R_AE + 1, 0:256]
    rev = _chunk_sums(cm_ref[0], jnp.concatenate([la, da], axis=1))
    dec = jnp.exp(rev[:, 0:128])
    kd = p_ref[:, C_GK:C_GK + 128].astype(F32) * dec
    wdec = jnp.exp(rev[:, 128:384])
    w = wdec * dte
    xw = xc[:, 0:256] * w
    d_s = [jnp.exp(_cs(a)) for a in _chunks(la)]
    et = [jnp.exp(_cs(a)) for a in _chunks(da)]
    mask_t = mk_ref[...]
    ut_g = [_dot_tn(v, k) * mask_t for v, k in zip(_chunks(p_ref[:, C_GV:C_GV + 256].astype(F32)), _chunks(kd))]
    ut_s = [_halves(_dot_tn, b, x) for b, x in zip(_chunks(xc[:, 256:512]), _chunks(xw))]
    return tail, pre, dtin, dte, dec, kd, wdec, w, xw, d_s, et, ut_g, ut_s


def _rmsproj(x, nw, wp, name, tm=512, rider=None):
    t = x.shape[0]

    def body(x_ref, nw_ref, w_ref, o_ref, t_ref, h_ref):
        xv = x_ref[...]
        rs = lax.rsqrt(jnp.mean(xv * xv, axis=-1, keepdims=True) + EPS)
        h = (xv * rs * nw_ref[...]).astype(BF16)
        h_ref[...] = h
        proj = jnp.dot(h, w_ref[...], preferred_element_type=F32)
        o_ref[...] = proj[:, 0:NPM].astype(BF16)
        t_ref[...] = proj[:, NPM:NP]

    (proj, tail, h), extra = _call(
        body, (x, nw, wp), grid=(t // tm,), name=name, sem=("parallel",), rider=rider,
        in_specs=[pl.BlockSpec((tm, D), lambda i: (i, 0)), pl.BlockSpec((1, D), lambda i: (0, 0)),
                  pl.BlockSpec((D, NP), lambda i: (0, 0))],
        out_specs=[pl.BlockSpec((tm, NPM), lambda i: (i, 0)), pl.BlockSpec((tm, NP - NPM), lambda i: (i, 0)),
                   pl.BlockSpec((tm, D), lambda i: (i, 0))],
        out_shape=[jax.ShapeDtypeStruct((t, NPM), BF16), jax.ShapeDtypeStruct((t, NP - NPM), F32),
                   jax.ShapeDtypeStruct((t, D), BF16)])
    return (proj, tail), h, extra


def _head_tile(xv, tgt, w):
    rs = lax.rsqrt(jnp.mean(xv * xv, axis=-1, keepdims=True) + EPS)
    xh = xv * rs
    err = xh * w - tgt
    dy = err * (1.0 / D)
    dxh = dy * w
    dx = rs * (dxh - xh * jnp.mean(dxh * xh, axis=-1, keepdims=True))
    return dx, _cs(dy * xh), (0.5 / D) * jnp.sum(err * err)


def _dxin(dp, wpt, x, dxn, nw, name, tm=512, rider=None):
    t = x.shape[0]

    def body(dp_ref, w_ref, x_ref, dxn_ref, nw_ref, dx_ref, dnw_ref):
        @pl.when(pl.program_id(0) == 0)
        def _():
            dnw_ref[...] = jnp.zeros_like(dnw_ref)

        dh = jnp.dot(dp_ref[...], w_ref[...], preferred_element_type=F32)
        xv = x_ref[...]
        rs = lax.rsqrt(jnp.mean(xv * xv, axis=-1, keepdims=True) + EPS)
        xh = xv * rs
        dnw_ref[0:1, :] += _cs(dh * xh)
        dxh = dh * nw_ref[...]
        dx_ref[...] = dxn_ref[...] + rs * (dxh - xh * jnp.mean(dxh * xh, axis=-1, keepdims=True))

    return _call(
        body, (dp, wpt, x, dxn, nw), grid=(t // tm,), name=name, sem=("arbitrary",), rider=rider,
        in_specs=[pl.BlockSpec((tm, NP), lambda i: (i, 0)), pl.BlockSpec((NP, D), lambda i: (0, 0)),
                  pl.BlockSpec((tm, D), lambda i: (i, 0)), pl.BlockSpec((tm, D), lambda i: (i, 0)),
                  pl.BlockSpec((1, D), lambda i: (0, 0))],
        out_specs=[pl.BlockSpec((tm, D), lambda i: (i, 0)), pl.BlockSpec((8, D), lambda i: (0, 0))],
        out_shape=[jax.ShapeDtypeStruct((t, D), F32), jax.ShapeDtypeStruct((8, D), F32)])


def _dwin(h, dp, name, tm=1024, rider=None):
    t = h.shape[0]

    def body(h_ref, dp_ref, o_ref):
        @pl.when(pl.program_id(0) == 0)
        def _():
            o_ref[...] = jnp.zeros_like(o_ref)

        o_ref[...] += _dot_tn(h_ref[...], dp_ref[...])

    (dwp,), extra = _call(
        body, (h, dp), grid=(t // tm,), name=name, sem=("arbitrary",), rider=rider,
        in_specs=[pl.BlockSpec((tm, D), lambda i: (i, 0)), pl.BlockSpec((tm, NP), lambda i: (i, 0))],
        out_specs=[pl.BlockSpec((D, NP), lambda i: (0, 0))], out_shape=[jax.ShapeDtypeStruct((D, NP), F32)])
    return dwp, extra


def _mixer_fwd(proj, x, wo, prm, gw, pw, cmat, mask, name, rider=None, head=None):
    proj, tail = proj
    t = proj.shape[0]
    nt, nc = t // TB, t // CH

    def body(p_ref, t_ref, x_ref, wo_ref, prm_ref, gw_ref, pw_ref, cm_ref, mk_ref, *rest):
        (tgt_ref, fw_ref), rest = (rest[:2], rest[2:]) if head else ((None, None), rest)
        mix_ref, sg_ref, ss_ref, xn_ref = rest[:4]
        acc_ref = rest[4] if head else None
        sg_s, ss_s, h_ua, h_pu, h_sx = rest[-5:]
        i = pl.program_id(0)

        @pl.when(i == 0)
        def _():
            for r in (sg_s, ss_s, h_ua, h_pu, h_sx) + ((acc_ref,) if head else ()):
                r[...] = jnp.zeros_like(r)

        lane = _iota((1, 256), 1)
        u = p_ref[:, C_AC:C_AC + 256].astype(F32) * p_ref[:, C_AH:C_AH + 256].astype(F32)
        ext = jnp.concatenate([h_ua[...], u], axis=0)
        cv = (prm_ref[R_CAW + 2:R_CAW + 3, 0:256] * u + prm_ref[R_CAW + 1:R_CAW + 2, 0:256] * _dn(ext, 1, TB, 8)
              + prm_ref[R_CAW:R_CAW + 1, 0:256] * _dn(ext, 2, TB, 8))
        mix_ref[:, 0:256] = (p_ref[:, C_AB:C_AB + 256].astype(F32) * cv * _silu(p_ref[:, C_AZ:C_AZ + 256].astype(F32))).astype(BF16)
        h_ua[...] = u[TB - 8:, :]
        pu = p_ref[:, C_PU:C_PU + 256].astype(F32)
        ext = jnp.concatenate([h_pu[...], pu], axis=0)
        pooled = _winsum_dn(ext, lane)[16:] * _pool_inv_count(i, TB) - pu
        mixed = _dot(pooled, pw_ref[...])
        mix_ref[:, 512:768] = (prm_ref[R_PSC:R_PSC + 1, 0:256] * mixed * _silu(p_ref[:, C_PZ:C_PZ + 256].astype(F32))).astype(BF16)
        h_pu[...] = pu[TB - 16:, :]
        sx = p_ref[:, C_SX:C_SX + 768].astype(F32)
        ext = jnp.concatenate([h_sx[...], sx], axis=0)
        xc = _silu(prm_ref[R_SCW + 3:R_SCW + 4, :] * sx + prm_ref[R_SCW + 2:R_SCW + 3, :] * _dn(ext, 1, TB, 8)
                   + prm_ref[R_SCW + 1:R_SCW + 2, :] * _dn(ext, 2, TB, 8) + prm_ref[R_SCW:R_SCW + 1, :] * _dn(ext, 3, TB, 8)
                   + prm_ref[R_SCB:R_SCB + 1, :])
        h_sx[...] = sx[TB - 8:, :]

        _, _, _, _, _, _, _, _, _, d_s, et, ut_g, ut_s = _mixer_tile_prep(p_ref, t_ref, xc, prm_ref, gw_ref[...], cm_ref, mk_ref)
        s_g, s_s = sg_s[...], ss_s[...]
        o, y = [], []
        qs = _chunks(p_ref[:, C_GQ:C_GQ + 128].astype(F32) * GLA_SCALE)
        cm = _chunks(xc[:, 512:768])
        for c in range(NCH):
            sg_ref[c] = s_g
            ss_ref[c] = s_s
            s_g = s_g * d_s[c] + ut_g[c]
            s_s = s_s * et[c] + ut_s[c]
            o.append(_dot_nt(qs[c], s_g))
            y.append(_halves(_dot, cm[c], s_s))
        sg_s[...] = s_g
        ss_s[...] = s_s
        o = jnp.concatenate(o, axis=0)
        on = o * lax.rsqrt(_dot2_l(o * o, cm_ref[2]) + EPS)
        mix_ref[:, 256:512] = (on * prm_ref[R_GNW:R_GNW + 1, 0:256] * _silu(p_ref[:, C_GZ:C_GZ + 256].astype(F32))).astype(BF16)
        y2 = ((jnp.concatenate(y, axis=0) + prm_ref[R_DE:R_DE + 1, 0:256] * xc[:, 0:256])
              * _silu(p_ref[:, C_SZ:C_SZ + 256].astype(F32)))
        mix_ref[:, 768:1024] = (y2 * lax.rsqrt(jnp.mean(y2 * y2, axis=-1, keepdims=True) + EPS)
                                * prm_ref[R_SNW:R_SNW + 1, 0:256]).astype(BF16)
        xn = x_ref[...] + jnp.dot(mix_ref[...], wo_ref[...], preferred_element_type=F32)
        if head:
            xn_ref[...], dfw, loss = _head_tile(xn, tgt_ref[...], fw_ref[...])
            acc_ref[0:1, :] += dfw
            acc_ref[1:2, :] += jnp.zeros((1, D), F32) + loss
        else:
            xn_ref[...] = xn

    row = pl.BlockSpec((TB, D), lambda i: (i, 0))
    return _call(
        body, (proj, tail, x, wo, prm, gw, pw, cmat, mask) + tuple(head or ()), grid=(nt,), name=name, sem=("arbitrary",),
        rider=rider,
        in_specs=[pl.BlockSpec((TB, NPM), lambda i: (i, 0)), pl.BlockSpec((TB, NP - NPM), lambda i: (i, 0)), row,
                  pl.BlockSpec((D, D), lambda i: (0, 0)), pl.BlockSpec((16, 768), lambda i: (0, 0)),
                  pl.BlockSpec((128, 128), lambda i: (0, 0)), pl.BlockSpec((256, 256), lambda i: (0, 0)),
                  pl.BlockSpec((4, 256, 256), lambda i: (0, 0, 0)), pl.BlockSpec((256, 128), lambda i: (0, 0))]
        + ([row, pl.BlockSpec((1, D), lambda i: (0, 0))] if head else []),
        out_specs=[row, pl.BlockSpec((NCH, 256, 128), lambda i: (i, 0, 0)),
                   pl.BlockSpec((NCH, 128, 256), lambda i: (i, 0, 0)), row]
        + ([pl.BlockSpec((8, D), lambda i: (0, 0))] if head else []),
        out_shape=[jax.ShapeDtypeStruct((t, D), BF16), jax.ShapeDtypeStruct((nc, 256, 128), F32),
                   jax.ShapeDtypeStruct((nc, 128, 256), F32), jax.ShapeDtypeStruct((t, D), F32)]
        + ([jax.ShapeDtypeStruct((8, D), F32)] if head else []),
        scratch_shapes=[pltpu.VMEM((256, 128), F32), pltpu.VMEM((128, 256), F32), pltpu.VMEM((8, 256), F32),
                        pltpu.VMEM((16, 256), F32), pltpu.VMEM((8, 768), F32)])


def _mixer_bwd(proj, dxn, wot, mix, sg, ss, prm, gw, pw, cmat, mask, name, rider=None):
    proj, tail = proj
    t = proj.shape[0]
    nt = t // TB
    rev = lambda i: nt - 1 - i

    def body(p_ref, hp_ref, t_ref, dxn_ref, wot_ref, mix_ref, sg_ref, ss_ref, prm_ref, gw_ref, pw_ref, cm_ref, mk_ref,
             dp_ref, sgc_ref, dwo_ref,
             gg_s, gs_s, h_dcv, h_dpl, h_dpre, gsm_ref, dgw_ref, dpw_ref, dm_ref):
        i = pl.program_id(0)
        tile = nt - 1 - i

        @pl.when(i == 0)
        def _():
            for r in (gg_s, gs_s, h_dcv, h_dpl, h_dpre, gsm_ref, dgw_ref, dpw_ref, dwo_ref):
                r[...] = jnp.zeros_like(r)

        dxn = dxn_ref[...].astype(BF16)
        dm_ref[...] = jnp.dot(dxn, wot_ref[...], preferred_element_type=F32)
        dwo_ref[...] += _dot_tn(mix_ref[...], dxn)

        lane = _iota((1, 256), 1)
        first = (tile > 0).astype(F32)
        ah, ac = p_ref[:, C_AH:C_AH + 256].astype(F32), p_ref[:, C_AC:C_AC + 256].astype(F32)
        ab, az = p_ref[:, C_AB:C_AB + 256].astype(F32), p_ref[:, C_AZ:C_AZ + 256].astype(F32)
        w0, w1, w2 = (prm_ref[R_CAW + j:R_CAW + j + 1, 0:256] for j in range(3))
        u = ac * ah
        ext = jnp.concatenate([(hp_ref[:, C_AC:C_AC + 256].astype(F32) * hp_ref[:, C_AH:C_AH + 256].astype(F32))[8:16] * first, u], axis=0)
        u1, u2 = _dn(ext, 1, TB, 8), _dn(ext, 2, TB, 8)
        cv = w2 * u + w1 * u1 + w0 * u2
        g = dm_ref[:, 0:256]
        sz, dsz = _silu_pair(az)
        dp_ref[:, C_AB:C_AB + 256] = (g * cv * sz).astype(BF16)
        dp_ref[:, C_AZ:C_AZ + 256] = (g * ab * cv * dsz).astype(BF16)
        dcv = g * ab * sz
        dext = jnp.concatenate([dcv, h_dcv[...]], axis=0)
        du = w2 * dcv + w1 * _up(dext, 1, TB) + w0 * _up(dext, 2, TB)
        dp_ref[:, C_AC:C_AC + 256] = (du * ah).astype(BF16)
        dp_ref[:, C_AH:C_AH + 256] = (du * ac).astype(BF16)
        gsm_ref[R_CAW:R_CAW + 1, 0:256] += _cs(dcv * u2)
        gsm_ref[R_CAW + 1:R_CAW + 2, 0:256] += _cs(dcv * u1)
        gsm_ref[R_CAW + 2:R_CAW + 3, 0:256] += _cs(dcv * u)
        h_dcv[...] = dcv[0:8, :]
        pu, pz = p_ref[:, C_PU:C_PU + 256].astype(F32), p_ref[:, C_PZ:C_PZ + 256].astype(F32)
        psc = prm_ref[R_PSC:R_PSC + 1, 0:256]
        icnt = _pool_inv_count(tile, TB)
        ext = jnp.concatenate([hp_ref[:, C_PU:C_PU + 256].astype(F32) * first, pu], axis=0)
        pooled = _winsum_dn(ext, lane)[16:] * icnt - pu
        pw_v = pw_ref[...]
        mixed = _dot(pooled, pw_v)
        g = dm_ref[:, 512:768]
        sz, dsz = _silu_pair(pz)
        gsm_ref[R_PSC:R_PSC + 1, 0:256] += _cs(g * mixed * sz)
        dp_ref[:, C_PZ:C_PZ + 256] = (g * psc * mixed * dsz).astype(BF16)
        dmixed = g * psc * sz
        dpw_ref[...] += _dot_tn(pooled, dmixed)
        dpooled = _dot_nt(dmixed, pw_v)
        qd = dpooled * icnt
        dext = jnp.concatenate([qd, h_dpl[...]], axis=0)
        dp_ref[:, C_PU:C_PU + 256] = (_winsum_up(dext, lane)[:TB] - dpooled).astype(BF16)
        h_dpl[...] = qd[0:16, :]
        sx = p_ref[:, C_SX:C_SX + 768].astype(F32)
        cw = [prm_ref[R_SCW + j:R_SCW + j + 1, :] for j in range(4)]
        ext = jnp.concatenate([hp_ref[:, C_SX:C_SX + 768].astype(F32)[8:16] * first, sx], axis=0)
        sx1, sx2, sx3 = _dn(ext, 1, TB, 8), _dn(ext, 2, TB, 8), _dn(ext, 3, TB, 8)
        cpre = cw[3] * sx + cw[2] * sx1 + cw[1] * sx2 + cw[0] * sx3 + prm_ref[R_SCB:R_SCB + 1, :]
        xc, dxc = _silu_pair(cpre)
        xs, bm, cm = xc[:, 0:256], xc[:, 256:512], xc[:, 512:768]

        gw_v = gw_ref[...]
        tail, pre, dtin, dte, dec, kd, wdec, w, xw, d_s, et, ut_g, ut_s = _mixer_tile_prep(p_ref, t_ref, xc, prm_ref,
                                                                                          gw_v, cm_ref, mk_ref)
        gmean = cm_ref[2]
        mask_t = mk_ref[...]
        gnw = prm_ref[R_GNW:R_GNW + 1, 0:256]
        a_e = prm_ref[R_AE:R_AE + 1, 0:256]
        d_e = prm_ref[R_DE:R_DE + 1, 0:256]
        snw = prm_ref[R_SNW:R_SNW + 1, 0:256]
        sg_in = [sg_ref[c] for c in range(NCH)]
        ss_in = [ss_ref[c] for c in range(NCH)]
        sg_n = [sg_in[c] * d_s[c] + ut_g[c] for c in range(NCH)]
        ss_n = [ss_in[c] * et[c] + ut_s[c] for c in range(NCH)]
        qs = _chunks(p_ref[:, C_GQ:C_GQ + 128].astype(F32) * GLA_SCALE)
        cm_c, bm_c, xw_c, kd_c = _chunks(cm), _chunks(bm), _chunks(xw), _chunks(kd)
        v_c = _chunks(p_ref[:, C_GV:C_GV + 256].astype(F32))
        o = jnp.concatenate([_dot_nt(qs[c], sg_n[c]) for c in range(NCH)], axis=0)
        y = jnp.concatenate([_halves(_dot, cm_c[c], ss_n[c]) for c in range(NCH)], axis=0) + d_e * xs
        gz = p_ref[:, C_GZ:C_GZ + 256].astype(F32)
        r = lax.rsqrt(_dot2_l(o * o, gmean) + EPS)
        on = o * r
        dyb = dm_ref[:, 256:512]
        sz, dsz = _silu_pair(gz)
        dp_ref[:, C_GZ:C_GZ + 256] = (dyb * on * gnw * dsz).astype(BF16)
        tg = dyb * sz
        gsm_ref[R_GNW:R_GNW + 1, 0:256] += _cs(tg * on)
        don = tg * gnw
        do_c = _chunks(r * (don - on * _dot2_l(don * on, gmean)))
        ssz = p_ref[:, C_SZ:C_SZ + 256].astype(F32)
        sil, dsil = _silu_pair(ssz)
        y2 = y * sil
        r = lax.rsqrt(jnp.mean(y2 * y2, axis=-1, keepdims=True) + EPS)
        yn = y2 * r
        dyd = dm_ref[:, 768:1024]
        gsm_ref[R_SNW:R_SNW + 1, 0:256] += _cs(dyd * yn)
        dn = dyd * snw
        dy2 = r * (dn - yn * jnp.mean(dn * yn, axis=-1, keepdims=True))
        dp_ref[:, C_SZ:C_SZ + 256] = (dy2 * y * dsil).astype(BF16)
        dy = dy2 * sil
        gsm_ref[R_DE:R_DE + 1, 0:256] += _cs(dy * xs)
        dy_c = _chunks(dy)
        dq = jnp.concatenate([_dot(do_c[c], sg_n[c]) for c in range(NCH)], axis=0)
        dp_ref[:, C_GQ:C_GQ + 128] = (dq * GLA_SCALE).astype(BF16)
        dcm = jnp.concatenate([_halves(_dot_nt, dy_c[c], ss_n[c]) for c in range(NCH)], axis=0)
        gg = [_dot_tn(do_c[c], qs[c]) * mask_t for c in range(NCH)]
        gs = [_halves(_dot_tn, cm_c[c], dy_c[c]) for c in range(NCH)]
        car_g, car_s = gg_s[...], gs_s[...]
        for c in reversed(range(NCH)):
            gg[c] = gg[c] + car_g
            gs[c] = gs[c] + car_s
            car_g = gg[c] * d_s[c]
            car_s = gs[c] * et[c]
        gg_s[...] = car_g
        gs_s[...] = car_s
        dkd = jnp.concatenate([_dot(v_c[c], gg[c]) for c in range(NCH)], axis=0)
        dp_ref[:, C_GV:C_GV + 256] = jnp.concatenate([_dot_nt(kd_c[c], gg[c]) for c in range(NCH)], axis=0).astype(BF16)
        dp_ref[:, C_GK:C_GK + 128] = (dkd * dec).astype(BF16)
        dbm = jnp.concatenate([_halves(_dot_nt, xw_c[c], gs[c]) for c in range(NCH)], axis=0)
        dxw = jnp.concatenate([_halves(_dot, bm_c[c], gs[c]) for c in range(NCH)], axis=0)
        dxs = dy * d_e + dxw * w
        dw = dxw * xs
        dsuf = _chunk_sums(cm_ref[1], jnp.concatenate([dkd * kd, dw * dte * wdec], axis=1))
        tot_g = jnp.concatenate([jnp.broadcast_to(_cs(gg[c] * sg_in[c]) * d_s[c], (CH, 128)) for c in range(NCH)], axis=0)
        tot_s = jnp.concatenate([jnp.broadcast_to(_cs(gs[c] * ss_in[c]) * et[c], (CH, 256)) for c in range(NCH)], axis=0)
        dpre = (dsuf[:, 0:128] + tot_g) * INV_TAU * jax.nn.sigmoid(-pre)
        dgw_ref[...] += _dot_tn(tail, dpre)
        gsm_ref[R_GB:R_GB + 1, 0:128] += _cs(dpre)
        dda = dsuf[:, 128:384] + tot_s
        gsm_ref[R_AE:R_AE + 1, 0:256] += _cs(dda * dte)
        dtail_s = _dot2_nt(dw * wdec + dda * a_e, cm_ref[3, 0:128, :]) * jax.nn.sigmoid(dtin)
        gsm_ref[R_DTB:R_DTB + 1, 0:128] += _cs(dtail_s)
        dp_ref[:, C_TL:C_TL + 128] = (_dot_nt(dpre, gw_v) + dtail_s).astype(BF16)
        dpre_c = jnp.concatenate([dxs, dbm, dcm], axis=1) * dxc
        dext = jnp.concatenate([dpre_c, h_dpre[...]], axis=0)
        dp_ref[:, C_SX:C_SX + 768] = (cw[3] * dpre_c + cw[2] * _up(dext, 1, TB) + cw[1] * _up(dext, 2, TB)
                                      + cw[0] * _up(dext, 3, TB)).astype(BF16)
        gsm_ref[R_SCW + 3:R_SCW + 4, :] += _cs(dpre_c * sx)
        gsm_ref[R_SCW + 2:R_SCW + 3, :] += _cs(dpre_c * sx1)
        gsm_ref[R_SCW + 1:R_SCW + 2, :] += _cs(dpre_c * sx2)
        gsm_ref[R_SCW:R_SCW + 1, :] += _cs(dpre_c * sx3)
        gsm_ref[R_SCB:R_SCB + 1, :] += _cs(dpre_c)
        h_dpre[...] = dpre_c[0:8, :]

        @pl.when(i == nt - 1)
        def _():
            ri, ci = _iota((256, 256), 0), _iota((256, 256), 1)
            per_head = jnp.where((ri >> 6) == ci, 1.0, 0.0).astype(BF16)
            per_dv = jnp.where((ri & 63) == ci, 1.0, 0.0).astype(BF16)
            row = _iota((8, 256), 0)
            top = gsm_ref[0:8, 0:256]
            sgc_ref[0:8, 0:256] = jnp.where(row == R_GNW, _dot3_l(top, per_dv), top)
            bot = gsm_ref[8:16, 0:256]
            fold = _dot3_l(jnp.where(row == R_AE - 8, bot * a_e, bot), per_head)
            sgc_ref[8:16, 0:256] = jnp.where((row == R_AE - 8) | (row == R_DE - 8), fold, bot)
            sgc_ref[0:16, 256:768] = gsm_ref[:, 256:768]
            sgc_ref[0:16, 768:896] = dgw_ref[0:16, :]
            sgc_ref[0:16, 896:1024] = jnp.zeros((16, 128), F32)
            diag = _pool_lane_select(lane, dpw_ref[0:64, :], dpw_ref[64:128, :], dpw_ref[128:192, :], dpw_ref[192:256, :])
            for q in range(4):
                sgc_ref[16:32, 256 * q:256 * q + 256] = diag[16 * q:16 * q + 16, :]

    return _call(
        body, (proj, proj, tail, dxn, wot, mix, sg, ss, prm, gw, pw, cmat, mask), grid=(nt,), name=name,
        sem=("arbitrary",), rider=rider,
        in_specs=[pl.BlockSpec((TB, NPM), lambda i: (rev(i), 0)),
                  pl.BlockSpec((16, NPM), lambda i: (jnp.maximum(rev(i) * (TB // 16) - 1, 0), 0)),
                  pl.BlockSpec((TB, NP - NPM), lambda i: (rev(i), 0)),
                  pl.BlockSpec((TB, D), lambda i: (rev(i), 0)), pl.BlockSpec((D, D), lambda i: (0, 0)),
                  pl.BlockSpec((TB, D), lambda i: (rev(i), 0)),
                  pl.BlockSpec((NCH, 256, 128), lambda i: (rev(i), 0, 0)),
                  pl.BlockSpec((NCH, 128, 256), lambda i: (rev(i), 0, 0)),
                  pl.BlockSpec((16, 768), lambda i: (0, 0)), pl.BlockSpec((128, 128), lambda i: (0, 0)),
                  pl.BlockSpec((256, 256), lambda i: (0, 0)), pl.BlockSpec((4, 256, 256), lambda i: (0, 0, 0)),
                  pl.BlockSpec((256, 128), lambda i: (0, 0))],
        out_specs=[pl.BlockSpec((TB, NP), lambda i: (rev(i), 0)), pl.BlockSpec((32, 1024), lambda i: (0, 0)),
                   pl.BlockSpec((D, D), lambda i: (0, 0))],
        out_shape=[jax.ShapeDtypeStruct((t, NP), BF16), jax.ShapeDtypeStruct((32, 1024), F32),
                   jax.ShapeDtypeStruct((D, D), F32)],
        scratch_shapes=[pltpu.VMEM((256, 128), F32), pltpu.VMEM((128, 256), F32), pltpu.VMEM((8, 256), F32),
                        pltpu.VMEM((16, 256), F32), pltpu.VMEM((8, 768), F32), pltpu.VMEM((16, 768), F32),
                        pltpu.VMEM((128, 128), F32), pltpu.VMEM((256, 256), F32), pltpu.VMEM((TB, D), F32)])


SHARD = NPROJ // 4
SHARD_PAD = 896


def _ranges_to_perm(o, n):
    out, p = [], 0
    for start, size in _PERM:
        a, b = max(o, start), min(o + n, start + size)
        if a < b:
            out.append((a, b - a, p + a - start))
        p += size
    return out


def _ranges_to_orig(p0, n):
    out, p = [], 0
    for start, size in _PERM:
        a, b = max(p0, p), min(p0 + n, p + size)
        if a < b:
            out.append((a, b - a, start + a - p))
        p += size
    return out


def _lane_window(load, lo, n, d, lane):
    a = 128 * (lo // 128)
    off = lo - a
    w = 128 if off + n <= 128 else 256
    chunk = load(a, w)
    shift = (d - off) % w
    if shift:
        chunk = pltpu.roll(chunk, shift, axis=1)
    return jnp.where((lane >= d) & (lane < d + n), chunk[:, 0:128], 0.0)


def _assemble_w_in(slabs, name, rb=256):
    def body(s_ref, wp_ref, wpt_ref):
        lane = _iota((1, 128), 1)
        for b in range(NP // 128):
            acc = jnp.zeros((rb, 128), F32)
            for p, n, o in _ranges_to_orig(128 * b, 128):
                while n > 0:
                    s, lo = o // SHARD, o % SHARD
                    cnt = min(n, SHARD - lo)
                    acc = acc + _lane_window(lambda a, w, s=s: s_ref[s, :, a:a + w].astype(F32), lo, cnt, p - 128 * b, lane)
                    o, p, n = o + cnt, p + cnt, n - cnt
            wp_ref[:, 128 * b:128 * b + 128] = acc.astype(BF16)
            wpt_ref[128 * b:128 * b + 128, :] = acc.T.astype(BF16)

    return pl.pallas_call(
        body, grid=(D // rb,), name=name,
        in_specs=[pl.BlockSpec((4, rb, SHARD_PAD), lambda i: (0, i, 0))],
        out_specs=[pl.BlockSpec((rb, NP), lambda i: (i, 0)), pl.BlockSpec((NP, rb), lambda i: (0, i))],
        out_shape=[jax.ShapeDtypeStruct((D, NP), BF16), jax.ShapeDtypeStruct((NP, D), BF16)],
        compiler_params=_cparams(("parallel",)))(slabs)


def _split_dw_in(dwp, name, rb=256):
    rows = dwp.shape[0]

    def body(g_ref, o_ref):
        lane = _iota((1, 128), 1)
        for s in range(4):
            for k in range(SHARD_PAD // 128):
                acc = jnp.zeros((rb, 128), F32)
                n_valid = min(128, SHARD - 128 * k)
                for o, n, p in _ranges_to_perm(SHARD * s + 128 * k, n_valid):
                    acc = acc + _lane_window(lambda a, w: g_ref[:, a:a + w].astype(F32), p, n, o - SHARD * s - 128 * k, lane)
                o_ref[s, :, 128 * k:128 * k + 128] = acc.astype(o_ref.dtype)

    return pl.pallas_call(
        body, grid=(rows // rb,), name=name,
        in_specs=[pl.BlockSpec((rb, NP), lambda i: (i, 0))],
        out_specs=pl.BlockSpec((4, rb, SHARD_PAD), lambda i: (0, i, 0)),
        out_shape=jax.ShapeDtypeStruct((4, rows, SHARD_PAD), dwp.dtype),
        compiler_params=_cparams(("parallel",)))(dwp)


def _half(c, n):
    return pl.ds(pl.multiple_of(c * (n // 2), n // 2), n // 2)


def _other_chips(x, y):
    return ((1 - x, y), (x, 1 - y), (1 - x, 1 - y))


def _remote(src, dst, send, recv, k, dev):
    return pltpu.make_async_remote_copy(src_ref=src, dst_ref=dst, send_sem=send.at[k], recv_sem=recv.at[k], device_id=dev,
                                        device_id_type=MESH)


def _sem(n):
    return pltpu.SemaphoreType.DMA((n,))


def _rider_gather_ici(shards):
    shards = tuple(shards)
    n = len(shards)

    def copies(rins, routs, sems, arrivals=True):
        send, recv = sems
        x, y, c = _place()
        me = 2 * x + y
        out, inc = [], []
        for j, (px, py) in enumerate(_other_chips(x, y)):
            for k in range(n):
                rows = _half(c, shards[k].shape[0])
                out.append(_remote(rins[k].at[rows], routs[k].at[me, rows], send, recv, n * j + k, (px, py, c)))
                if arrivals:
                    inc.append(_remote(rins[k].at[rows], routs[k].at[2 * px + py, rows], send, recv, n * j + k, (px, py, c)))
        return out, inc

    def start(rins, routs, sems):
        for cp in copies(rins, routs, sems, arrivals=False)[0]:
            cp.start()

    def finish(rins, routs, sems):
        out, inc = copies(rins, routs, sems)
        for cp in inc:
            cp.wait_recv()
        for cp in out:
            cp.wait_send()

    return _Rider(shards, [jax.ShapeDtypeStruct((4,) + a.shape, a.dtype) for a in shards], [_sem(3 * n), _sem(3 * n)],
                  start, finish)


def _gather_ici_two_hops(shards, extra):
    shards = tuple(shards)
    n = len(shards)

    def body(*refs):
        ins, e_in, outs, e_out = refs[:n], refs[n], refs[n + 1:2 * n + 1], refs[2 * n + 1]
        send, recv = refs[2 * n + 2:]
        x, y, c = _place()
        slab = lambda px, py: 2 * px + py
        xn, yn, dg = (1 - x, y), (x, 1 - y), (1 - x, 1 - y)

        def part(k, q):
            r = shards[k].shape[0] // 4
            return pl.ds(pl.multiple_of(c * 2 * r + q * r, r), r)

        def hop(k, q, src_chip, to, sem):
            rows = part(k, q)
            src = ins[k].at[rows] if src_chip is None else outs[k].at[slab(*src_chip), rows]
            own = (x, y) if src_chip is None else src_chip
            return _remote(src, outs[k].at[slab(*own), rows], send, recv, sem, (*to, c))

        small = [_remote(e_in, e_out.at[slab(x, y)], send, recv, 6 * n + j, (*to, c)) for j, to in enumerate((xn, yn, dg))]
        first = [hop(k, q, None, (xn, yn)[q], 2 * k + q) for k in range(n) for q in (0, 1)]
        for cp in small + first:
            cp.start()
        for k in range(n):
            for q in (0, 1):
                nb = (xn, yn)[q]
                _remote(ins[k].at[part(k, q)], outs[k].at[slab(*nb), part(k, q)], send, recv, 2 * k + q, (*nb, c)).wait_recv()
        second = []
        for k in range(n):
            for q in (0, 1):
                to, via = (yn, xn)[q], (xn, yn)[q]
                second.append(hop(k, q, None, to, 2 * n + 4 * k + 2 * q))
                second.append(hop(k, q, via, to, 2 * n + 4 * k + 2 * q + 1))
        for cp in second:
            cp.start()
        for k in range(n):
            for q in (0, 1):
                frm, rows = (yn, xn)[q], part(k, q)
                for j, origin in enumerate((frm, dg)):
                    _remote(ins[k].at[rows], outs[k].at[slab(*origin), rows], send, recv, 2 * n + 4 * k + 2 * q + j,
                            (*frm, c)).wait_recv()
        for j, frm in enumerate((xn, yn, dg)):
            _remote(e_in, e_out.at[slab(*frm)], send, recv, 6 * n + j, (*frm, c)).wait_recv()
        for cp in small + first + second:
            cp.wait_send()

    outs = pl.pallas_call(
        body, name="gather_ici0", in_specs=[_ANY] * (n + 1), out_specs=[_ANY] * (n + 1),
        out_shape=[jax.ShapeDtypeStruct((4,) + a.shape, a.dtype) for a in shards + (extra,)],
        scratch_shapes=[_sem(6 * n + 3), _sem(6 * n + 3)])(*shards, extra)
    return list(outs)


def _rider_gather_d2d(slabs):
    slabs = tuple(slabs)
    n = len(slabs)

    def copies(routs, sems, arrivals=True):
        send, recv = sems
        x, y, c = _place()
        out, inc = [], []
        for j, (px, py) in enumerate(_other_chips(x, y)):
            for k in range(n):
                rows = slabs[k].shape[1]
                mine, theirs = routs[k].at[2 * px + py, _half(c, rows)], routs[k].at[2 * px + py, _half(1 - c, rows)]
                out.append(_remote(mine, mine, send, recv, n * j + k, (x, y, 1 - c)))
                if arrivals:
                    inc.append(_remote(theirs, theirs, send, recv, n * j + k, (x, y, 1 - c)))
        return out, inc

    def start(rins, routs, sems):
        for cp in copies(routs, sems, arrivals=False)[0]:
            cp.start()

    def finish(rins, routs, sems):
        out, inc = copies(routs, sems)
        for cp in inc:
            cp.wait_recv()
        for cp in out:
            cp.wait_send()

    return _Rider(slabs, [jax.ShapeDtypeStruct(a.shape, a.dtype) for a in slabs], [_sem(3 * n), _sem(3 * n)], start, finish,
                  aliases={k: k for k in range(n)})


def _rider_swap(parts):
    parts = tuple(parts)
    n = len(parts)

    def copies(rins, routs, sems):
        send, recv = sems
        x, y, c = _place()
        return [_remote(rins[k].at[:, _half(1 - c, parts[k].shape[1])], routs[k], send, recv, k, (x, y, 1 - c))
                for k in range(n)]

    def start(rins, routs, sems):
        for cp in copies(rins, routs, sems):
            cp.start()

    def finish(rins, routs, sems):
        for cp in copies(rins, routs, sems):
            cp.wait()

    return _Rider(parts, [jax.ShapeDtypeStruct((a.shape[0], a.shape[1] // 2, a.shape[2]), a.dtype) for a in parts],
                  [_sem(n), _sem(n)], start, finish)


def _rider_scatter(parts):
    parts = tuple(parts)
    n = len(parts)

    def copies(rins, routs, sems, arrivals=True):
        send, recv = sems
        x, y, c = _place()
        me = 2 * x + y
        out, inc = [], []
        for j, (px, py) in enumerate(_other_chips(x, y)):
            for k in range(n):
                out.append(_remote(rins[k].at[2 * px + py], routs[k].at[me], send, recv, n * j + k, (px, py, c)))
                if arrivals:
                    inc.append(_remote(rins[k].at[me], routs[k].at[2 * px + py], send, recv, n * j + k, (px, py, c)))
        return out, inc

    def start(rins, routs, sems):
        for cp in copies(rins, routs, sems, arrivals=False)[0]:
            cp.start()

    def finish(rins, routs, sems):
        out, inc = copies(rins, routs, sems)
        for cp in inc:
            cp.wait_recv()
        for cp in out:
            cp.wait_send()

    return _Rider(parts, [jax.ShapeDtypeStruct(a.shape, a.dtype) for a in parts], [_sem(3 * n), _sem(3 * n)], start, finish)


def _rider_share(fulls):
    fulls = tuple(fulls)
    n = len(fulls)

    def copies(routs, sems, arrivals=True):
        send, recv = sems
        x, y, c = _place()
        out, inc = [], []
        for k in range(n):
            mine, theirs = routs[k].at[_half(c, fulls[k].shape[0])], routs[k].at[_half(1 - c, fulls[k].shape[0])]
            out.append(_remote(mine, mine, send, recv, k, (x, y, 1 - c)))
            if arrivals:
                inc.append(_remote(theirs, theirs, send, recv, k, (x, y, 1 - c)))
        return out, inc

    def start(rins, routs, sems):
        for cp in copies(routs, sems, arrivals=False)[0]:
            cp.start()

    def finish(rins, routs, sems):
        out, inc = copies(routs, sems)
        for cp in inc:
            cp.wait_recv()
        for cp in out:
            cp.wait_send()

    return _Rider(fulls, [jax.ShapeDtypeStruct(a.shape, a.dtype) for a in fulls], [_sem(n), _sem(n)], start, finish,
                  aliases={k: k for k in range(n)})


def _pair_sum(core, full, recv, name, br=128):
    n, rows, cols = recv.shape

    def body(c_ref, a_ref, b_ref, o_ref):
        o_ref[...] = (a_ref[...] + b_ref[...]).astype(BF16)

    nb = rows // br
    return pl.pallas_call(
        body, name=name, out_shape=jax.ShapeDtypeStruct(recv.shape, BF16),
        grid_spec=pltpu.PrefetchScalarGridSpec(
            num_scalar_prefetch=1, grid=(n, nb),
            in_specs=[pl.BlockSpec((1, br, cols), lambda i, j, c: (i, c[0] * nb + j, 0)),
                      pl.BlockSpec((1, br, cols), lambda i, j, c: (i, j, 0))],
            out_specs=pl.BlockSpec((1, br, cols), lambda i, j, c: (i, j, 0))),
        compiler_params=_cparams(("parallel", "parallel")))(core, full, recv)


def _chip_sum(place, gathered, mine, name, br=128):
    _, r, c = gathered.shape
    nb = r // br

    def body(p_ref, g_ref, m_ref, o_ref):
        slab = lambda j: jnp.where(p_ref[1] == j, m_ref[j], g_ref[j]).astype(F32)
        o_ref[...] = ((slab(0) + slab(1)) + slab(2)) + slab(3)

    return pl.pallas_call(
        body, name=name, out_shape=jax.ShapeDtypeStruct((2 * r, c), F32),
        grid_spec=pltpu.PrefetchScalarGridSpec(
            num_scalar_prefetch=1, grid=(nb,),
            in_specs=[pl.BlockSpec((4, br, c), lambda i, p: (0, i, 0)), pl.BlockSpec((4, br, c), lambda i, p: (0, i, 0))],
            out_specs=pl.BlockSpec((br, c), lambda i, p: (p[0] * nb + i, 0))),
        compiler_params=_cparams(("parallel",)))(place, gathered, mine)


def _adamw(w, g, m, v, name, br):
    n, r, c = w.shape

    def body(w_ref, g_ref, m_ref, v_ref, d_ref, m2_ref, v2_ref):
        d_ref[...], m2_ref[...], v2_ref[...] = _adam_math(w_ref[...], g_ref[...], m_ref[...], v_ref[...])

    spec = pl.BlockSpec((1, br, c), lambda i, j: (i, j, 0))
    shp = jax.ShapeDtypeStruct(w.shape, F32)
    return pl.pallas_call(body, grid=(n, r // br), name=name, in_specs=[spec] * 4, out_specs=[spec] * 3,
                          out_shape=[shp] * 3, compiler_params=_cparams(("parallel", "parallel")))(w, g, m, v)


def _adamw_w_in(w, g, m, v, name, bc=31):
    cols = w.shape[2]
    lead = lambda a: jnp.transpose(a, (2, 0, 1))
    g = jnp.stack([a[:, 0:cols] for a in g])

    def body(w_ref, g_ref, m_ref, v_ref, go_ref, d_ref, m2_ref, v2_ref):
        for l in range(2):
            gv = g_ref[:, l, :]
            d_ref[:, l, :], m2_ref[:, l, :], v2_ref[:, l, :] = _adam_math(w_ref[:, l, :], gv, m_ref[:, l, :], v_ref[:, l, :])
            go_ref[:, l, :] = gv

    spec = pl.BlockSpec((bc, 2, D), lambda i: (i, 0, 0))
    outs = pl.pallas_call(body, grid=(cols // bc,), name=name, in_specs=[spec] * 4, out_specs=[spec] * 4,
                          out_shape=[jax.ShapeDtypeStruct((cols, 2, D), F32)] * 4,
                          compiler_params=_cparams(("parallel",)))(lead(w), lead(g), lead(m), lead(v))
    return [jnp.transpose(o, (1, 2, 0)) for o in outs]


_SMALL_NAMES = ("norm_w", "conv_a_w", "gla_gate_w", "gla_gate_b", "gla_norm_w", "pool_w", "pool_scale", "ssd_conv_w",
                "ssd_conv_b", "ssd_dt_bias", "ssd_a_log", "ssd_d", "ssd_norm_w", "final_norm_w")
SMALL_ROWS = 80


def _adam_math(w, g, m, v):
    m2 = ADAM_B1 * m + (1.0 - ADAM_B1) * g
    v2 = ADAM_B2 * v + (1.0 - ADAM_B2) * (g * g)
    m_hat = m2 / (1.0 - ADAM_B1 ** ADAM_STEP)
    v_hat = v2 / (1.0 - ADAM_B2 ** ADAM_STEP)
    return -ADAM_LR * (m_hat / (jnp.sqrt(v_hat) + ADAM_EPS) + ADAM_WD * w), m2, v2


def _small_slices(name, chip):
    if name == "conv_a_w":
        return [((), slice(R_CAW, R_CAW + 3), slice(64 * chip, 64 * chip + 64))]
    if name == "ssd_conv_w":
        return [((), slice(R_SCW, R_SCW + 4), slice(192 * chip, 192 * chip + 192))]
    if name == "gla_gate_w":
        return [((), slice(0, 16), slice(768, 896))]
    if name == "pool_w":
        return [((g, slice(16 * q, 16 * q + 16)), slice(16, 32), slice(256 * q + 64 * g, 256 * q + 64 * g + 64))
                for g in range(4) for q in range(4)]
    row, lanes = {"gla_gate_b": (R_GB, slice(0, 128)), "gla_norm_w": (R_GNW, slice(0, 64)),
                  "pool_scale": (R_PSC, slice(0, 256)), "ssd_conv_b": (R_SCB, slice(0, 768)),
                  "ssd_dt_bias": (R_DTB, slice(16, 20)), "ssd_a_log": (R_AE, slice(0, 4)), "ssd_d": (R_DE, slice(0, 4)),
                  "ssd_norm_w": (R_SNW, slice(0, 256))}[name]
    return [((), slice(row, row + 1), lanes)]


def _rider_exchange(block):
    def copies(rins, routs, sems):
        send, recv = sems
        x, y, c = _place()
        flip = lambda v, bit: 1 - v if bit else v
        return [_remote(rins[0], routs[0].at[k], send, recv, k - 1, (flip(x, k & 4), flip(y, k & 2), flip(c, k & 1)))
                for k in range(1, 8)]

    def start(rins, routs, sems):
        for cp in copies(rins, routs, sems):
            cp.start()

    def finish(rins, routs, sems):
        for cp in copies(rins, routs, sems):
            cp.wait()

    return _Rider((block,), [jax.ShapeDtypeStruct((8,) + block.shape, block.dtype)], [_sem(7), _sem(7)], start, finish)


def _join_riders(a, b):
    na, oa, sa = len(a.inputs), len(a.out_shapes), len(a.sems)

    def start(rins, routs, sems):
        a.start(rins[:na], routs[:oa], sems[:sa])
        b.start(rins[na:], routs[oa:], sems[sa:])

    def finish(rins, routs, sems):
        a.finish(rins[:na], routs[:oa], sems[:sa])
        b.finish(rins[na:], routs[oa:], sems[sa:])

    aliases = {**a.aliases, **{na + k: oa + v for k, v in b.aliases.items()}}
    return _Rider(a.inputs + b.inputs, a.out_shapes + b.out_shapes, a.sems + b.sems, start, finish, aliases)


def _small_adamw(blocks, w, m, v):
    n = len(_SMALL_NAMES)

    def body(*refs):
        (own, ex), (own0, ex0) = refs[0:2], refs[2:4]
        refs = refs[3:]
        w_refs, m_refs, v_refs = refs[1:1 + n], refs[1 + n:1 + 2 * n], refs[1 + 2 * n:1 + 3 * n]
        o = 1 + 3 * n
        g_out, d_out, m_out, v_out = refs[o:o + n], refs[o + n:o + 2 * n], refs[o + 2 * n:o + 3 * n], refs[o + 3 * n:o + 4 * n]
        loss_ref, acc, acc0 = refs[o + 4 * n:o + 4 * n + 3]
        chip = 2 * lax.axis_index("x") + lax.axis_index("y")
        me = 2 * chip + lax.axis_index("c")
        acc[...] = jnp.zeros_like(acc)
        acc0[...] = jnp.zeros_like(acc0)
        for src in range(8):
            @pl.when(me == src)
            def _():
                acc[...] += own[...]
                acc0[...] += own0[...]

            @pl.when(me != src)
            def _(src=src):
                acc[...] += ex[jnp.bitwise_xor(me, src)]
                acc0[...] += ex0[jnp.bitwise_xor(me, src)]

        loss_ref[...] = acc[73:74, 0:1]

        def update(i, idx, g):
            d, m2, v2 = _adam_math(w_refs[i][idx], g, m_refs[i][idx], v_refs[i][idx])
            g_out[i][idx], d_out[i][idx], m_out[i][idx], v_out[i][idx] = g, d, m2, v2

        for i, name in enumerate(_SMALL_NAMES):
            if name == "final_norm_w":
                update(i, (slice(0, 1), slice(None)), acc[72:73, :])
            elif name == "norm_w":
                update(i, (slice(0, 1), slice(None)), acc0[0:1, :])
                update(i, (slice(1, 2), slice(None)), acc[64:65, :])
            elif name in ("conv_a_w", "ssd_conv_w"):
                for s in range(4):
                    @pl.when(chip == s)
                    def _(i=i, name=name, s=s):
                        for l in range(2):
                            (_, rows, lanes), = _small_slices(name, s)
                            update(i, (l,), acc[rows.start + 32 * l:rows.stop + 32 * l, lanes])
            else:
                for l in range(2):
                    for idx, rows, lanes in _small_slices(name, 0):
                        g = acc[rows.start + 32 * l:rows.stop + 32 * l, lanes]
                        if w_refs[i].ndim == 2:
                            update(i, (slice(l, l + 1), slice(None)), g)
                        else:
                            update(i, (l,) + idx, g)

    args = [a for pair in blocks for a in pair] + [d[k] for d in (w, m, v) for k in _SMALL_NAMES]
    shapes = [jax.ShapeDtypeStruct(w[k].shape, F32) for k in _SMALL_NAMES]
    vmem = pl.BlockSpec(memory_space=pltpu.VMEM)
    outs = pl.pallas_call(body, name="small_adamw", in_specs=[vmem] * len(args), out_specs=[vmem] * (4 * n + 1),
                          out_shape=shapes * 4 + [jax.ShapeDtypeStruct((1, 1), F32)],
                          scratch_shapes=[pltpu.VMEM((SMALL_ROWS, D), F32), pltpu.VMEM((8, D), F32)])(*args)
    return outs[0:n], outs[n:2 * n], outs[2 * n:3 * n], outs[3 * n:4 * n], outs[4 * n]


def _mixer_consts(layer, conv_a_w, gla_gate_w, gla_gate_b, gla_norm_w, pool_w, pool_scale, ssd_conv_w, ssd_conv_b,
                  ssd_dt_bias, ssd_a_log, ssd_d, ssd_norm_w):
    def row(v):
        return jnp.pad(v.reshape(1, -1), ((0, 0), (0, 768 - v.size)))

    dtb = jnp.pad(ssd_dt_bias[layer], (16, 108))
    rows = [jnp.pad(conv_a_w[layer], ((0, 0), (0, 512))), row(gla_gate_b[layer]), row(jnp.tile(gla_norm_w[layer], 4)),
            row(pool_scale[layer]), row(ssd_conv_b[layer]), row(dtb), row(jnp.repeat(-jnp.exp(ssd_a_log[layer]), 64)),
            row(jnp.repeat(ssd_d[layer], 64)), row(ssd_norm_w[layer]), jnp.zeros((1, 768), F32), ssd_conv_w[layer]]
    prm = jnp.concatenate(rows, axis=0)
    gw = jnp.pad(gla_gate_w[layer], ((0, 112), (0, 0))).astype(BF16)
    on_diag = (_iota((256, 256), 0) >> 6) == (_iota((256, 256), 1) >> 6)
    pw = jnp.where(on_diag, jnp.tile(pool_w[layer].reshape(256, 64), (1, 4)), 0.0)
    return (prm, gw, pw.astype(BF16)) + _mixer_matrices()


def _grad_slabs(dwp, dwo):
    return dwp.reshape(1, D, NP), dwo.reshape(4, D // 4, D)


class _Comm:
    def __init__(self, w_in, w_out):
        self.w_in16 = jnp.pad(w_in.astype(BF16), ((0, 0), (0, 0), (0, SHARD_PAD - SHARD)))
        self.w_out16 = w_out.astype(BF16)
        self.core = lax.axis_index("c").astype(jnp.int32).reshape(1)
        self.chip = 2 * lax.axis_index("x") + lax.axis_index("y")
        self.place = jnp.stack([lax.axis_index("c"), self.chip]).astype(jnp.int32)

    def gather_ici(self, layer):
        return _rider_gather_ici((self.w_in16[layer], self.w_out16[layer]))

    def pair_sum(self, layer, slabs, received):
        d_in, d_out = [_pair_sum(self.core, a, b, name=f"reduce_pair_sum{layer}_{k}")
                       for k, (a, b) in enumerate(zip(slabs, received))]
        return [_split_dw_in(d_in[0], name=f"split_dw_in{layer}"), d_out]

    def chip_sum(self, layer, gathered, mine):
        return [_chip_sum(self.place, a, b, name=f"reduce_chip_sum{layer}_{k}") for k, (a, b) in enumerate(zip(gathered, mine))]

    def layer_weights(self, layer, s_in, s_out):
        own = lambda slabs, shard: jnp.stack([jnp.where(self.chip == s, shard, slabs[s]) for s in range(4)])
        wp, wpt = _assemble_w_in(own(s_in, self.w_in16[layer]), name=f"assemble_w_in{layer}")
        wo = own(s_out, self.w_out16[layer]).reshape(D, D)
        return wp, wpt, wo, wo.T


def _local_step(x, tgt, norm_w, final_norm_w, consts, wts0, wts1=None, comm=None):
    nw = [norm_w[l:l + 1] for l in range(2)]
    proj0, h0, slabs = _rmsproj(x, nw[0], wts0[0], name="rmsproj0", rider=comm and comm.gather_ici(1))
    (mix0, sg0, ss0, x1), slabs = _mixer_fwd(proj0, x, wts0[2], *consts[0], name="mixer_fwd0",
                                             rider=comm and _rider_gather_d2d(slabs))
    if comm:
        wts1 = comm.layer_weights(1, *slabs)
    proj1, h1, _ = _rmsproj(x1, nw[1], wts1[0], name="rmsproj1")
    (mix1, sg1, ss1, dx, head), _ = _mixer_fwd(proj1, x1, wts1[2], *consts[1], name="mixer_fwd1",
                                               head=(tgt, final_norm_w.reshape(1, D)))
    (dproj, mgr1, dwo1), _ = _mixer_bwd(proj1, dx, wts1[3], mix1, sg1, ss1, *consts[1], name="mixer_bwd1")
    dwp1, _ = _dwin(h1, dproj, name="dwin1")
    slabs1 = comm and _grad_slabs(dwp1, dwo1)
    (dx, dnw1), recv = _dxin(dproj, wts1[1], x1, dx, nw[1], name="dxin1", rider=comm and _rider_swap(slabs1))
    pairs1 = comm and comm.pair_sum(1, slabs1, recv)
    (dproj, mgr0, dwo0), gathered = _mixer_bwd(proj0, dx, wts0[3], mix0, sg0, ss0, *consts[0], name="mixer_bwd0",
                                               rider=comm and _rider_scatter(pairs1))
    dwp0, big1 = _dwin(h0, dproj, name="dwin0", rider=comm and _rider_share(comm.chip_sum(1, gathered, pairs1)))
    if not comm:
        (dx, dnw0), _ = _dxin(dproj, wts0[1], x, dx, nw[0], name="dxin0")
        return head, dx, ((dwp0, dwp1), (dwo0, dwo1)), (dnw0, dnw1), (mgr0, mgr1)
    slabs0 = _grad_slabs(dwp0, dwo0)
    pairs0 = comm.pair_sum(0, slabs0, _run_rider(_rider_swap(slabs0), "reduce_swap0"))
    small = jnp.concatenate([mgr0, mgr1, dnw1, head], axis=0)
    (dx, dnw0), gathered = _dxin(dproj, wts0[1], x, dx, nw[0], name="dxin0",
                                 rider=_join_riders(_rider_scatter(pairs0), _rider_exchange(small)))
    last = _run_rider(_join_riders(_rider_share(comm.chip_sum(0, gathered[0:2], pairs0)), _rider_exchange(dnw0)),
                      "reduce_share0")
    return dx, ((last[0], big1[0]), (last[1], big1[1])), ((small, gathered[2]), (dnw0, last[2]))


def kernel(x, norm_w, w_in, conv_a_w, gla_gate_w, gla_gate_b, gla_norm_w, pool_w, pool_scale, ssd_conv_w, ssd_conv_b, ssd_dt_bias, ssd_a_log, ssd_d, ssd_norm_w, w_out, final_norm_w, loss_target, m_norm_w, m_w_in, m_conv_a_w, m_gla_gate_w, m_gla_gate_b, m_gla_norm_w, m_pool_w, m_pool_scale, m_ssd_conv_w, m_ssd_conv_b, m_ssd_dt_bias, m_ssd_a_log, m_ssd_d, m_ssd_norm_w, m_w_out, m_final_norm_w, v_norm_w, v_w_in, v_conv_a_w, v_gla_gate_w, v_gla_gate_b, v_gla_norm_w, v_pool_w, v_pool_scale, v_ssd_conv_w, v_ssd_conv_b, v_ssd_dt_bias, v_ssd_a_log, v_ssd_d, v_ssd_norm_w, v_w_out, v_final_norm_w):
    weights = dict(norm_w=norm_w, w_in=w_in, conv_a_w=conv_a_w, gla_gate_w=gla_gate_w, gla_gate_b=gla_gate_b,
                   gla_norm_w=gla_norm_w, pool_w=pool_w, pool_scale=pool_scale, ssd_conv_w=ssd_conv_w,
                   ssd_conv_b=ssd_conv_b, ssd_dt_bias=ssd_dt_bias, ssd_a_log=ssd_a_log, ssd_d=ssd_d,
                   ssd_norm_w=ssd_norm_w, w_out=w_out, final_norm_w=final_norm_w)
    m_in = dict(norm_w=m_norm_w, w_in=m_w_in, conv_a_w=m_conv_a_w, gla_gate_w=m_gla_gate_w, gla_gate_b=m_gla_gate_b,
                gla_norm_w=m_gla_norm_w, pool_w=m_pool_w, pool_scale=m_pool_scale, ssd_conv_w=m_ssd_conv_w,
                ssd_conv_b=m_ssd_conv_b, ssd_dt_bias=m_ssd_dt_bias, ssd_a_log=m_ssd_a_log, ssd_d=m_ssd_d,
                ssd_norm_w=m_ssd_norm_w, w_out=m_w_out, final_norm_w=m_final_norm_w)
    v_in = dict(norm_w=v_norm_w, w_in=v_w_in, conv_a_w=v_conv_a_w, gla_gate_w=v_gla_gate_w, gla_gate_b=v_gla_gate_b,
                gla_norm_w=v_gla_norm_w, pool_w=v_pool_w, pool_scale=v_pool_scale, ssd_conv_w=v_ssd_conv_w,
                ssd_conv_b=v_ssd_conv_b, ssd_dt_bias=v_ssd_dt_bias, ssd_a_log=v_ssd_a_log, ssd_d=v_ssd_d,
                ssd_norm_w=v_ssd_norm_w, w_out=v_w_out, final_norm_w=v_final_norm_w)
    order = ("norm_w", "w_in", "conv_a_w", "gla_gate_w", "gla_gate_b", "gla_norm_w", "pool_w", "pool_scale",
             "ssd_conv_w", "ssd_conv_b", "ssd_dt_bias", "ssd_a_log", "ssd_d", "ssd_norm_w", "w_out", "final_norm_w")
    t = x.shape[1]

    comm = _Comm(w_in, w_out)
    cshard = jnp.zeros((16, 256), F32)
    for l in range(2):
        cshard = cshard.at[8 * l:8 * l + 3, 0:64].set(conv_a_w[l]).at[8 * l + 3:8 * l + 7, 0:192].set(ssd_conv_w[l])
    s_in, s_out, g_c = _gather_ici_two_hops((comm.w_in16[0], comm.w_out16[0]), cshard)
    s_in, s_out = _run_rider(_rider_gather_d2d((s_in, s_out)), "gather_d2d0")
    g_c = [jnp.where(comm.chip == s, cshard, g_c[s]) for s in range(4)]
    conv_a_full = jnp.stack([jnp.concatenate([g_c[s][8 * l:8 * l + 3, 0:64] for s in range(4)], axis=-1) for l in range(2)])
    ssd_conv_full = jnp.stack([jnp.concatenate([g_c[s][8 * l + 3:8 * l + 7, 0:192] for s in range(4)], axis=-1)
                               for l in range(2)])
    consts = [_mixer_consts(l, conv_a_full, gla_gate_w, gla_gate_b, gla_norm_w, pool_w, pool_scale, ssd_conv_full,
                            ssd_conv_b, ssd_dt_bias, ssd_a_log, ssd_d, ssd_norm_w) for l in range(2)]

    dx, big, blocks = _local_step(x.reshape(t, D), loss_target.reshape(t, D), norm_w, final_norm_w, consts,
                                  comm.layer_weights(0, s_in, s_out), comm=comm)

    as2d = lambda d: {k: (d[k].reshape(1, D) if k == "final_norm_w" else d[k]) for k in _SMALL_NAMES}
    small = _small_adamw(blocks, as2d(weights), as2d(m_in), as2d(v_in))
    grads, delta, new_m, new_v = ({k: (a.reshape(D) if k == "final_norm_w" else a) for k, a in zip(_SMALL_NAMES, part)}
                                  for part in small[0:4])
    loss = small[4].reshape(())

    grads["w_out"] = jnp.stack(big[1])

    grads["w_in"], delta["w_in"], new_m["w_in"], new_v["w_in"] = _adamw_w_in(w_in, big[0], m_w_in, v_w_in, name="adamw_w_in")
    delta["w_out"], new_m["w_out"], new_v["w_out"] = _adamw(w_out, grads["w_out"], m_w_out, v_w_out, name="adamw_w_out", br=256)

    return (loss, dx.reshape(1, t, D), *[grads[k] for k in order], *[delta[k] for k in order],
            *[new_m[k] for k in order], *[new_v[k] for k in order])
```

```python
import functools

import jax
import jax.numpy as jnp
from jax import lax
from jax.experimental import pallas as pl
from jax.experimental.pallas import tpu as pltpu

F32 = jnp.float32
BF16 = jnp.bfloat16
MESH = pl.DeviceIdType.MESH

D = 1024
CH = 64
EPS = 1e-6
NP = 3456
NPROJ = 3348
NPM = 3328
GLA_SCALE = 32.0 ** -0.5
INV_TAU = 1.0 / 16.0
TB = 512
NCH = TB // CH
assert TB % 256 == 0

C_AH, C_AB, C_AC, C_AZ, C_GQ, C_GK, C_GV = 0, 256, 512, 768, 1024, 1152, 1280
C_GZ, C_PU, C_PZ, C_SZ, C_SX, C_TL = 1536, 1792, 2048, 2304, 2560, 3328
_PERM = ((0, 1536), (1552, 1792), (1536, 16), (3344, 4))

R_CAW, R_GB, R_GNW, R_PSC, R_SCB, R_DTB, R_AE, R_DE, R_SNW, R_SCW = 0, 3, 4, 5, 6, 7, 8, 9, 10, 12

ADAM_LR, ADAM_B1, ADAM_B2, ADAM_EPS, ADAM_WD, ADAM_STEP = 0.001, 0.9, 0.999, 1e-08, 0.01, 10

VMEM_LIMIT = 56 * 1024 * 1024


def _cparams(sem, limit=VMEM_LIMIT):
    return pltpu.CompilerParams(dimension_semantics=sem, vmem_limit_bytes=limit)


_ANY = pl.BlockSpec(memory_space=pl.ANY)


def _place():
    return lax.axis_index("x"), lax.axis_index("y"), lax.axis_index("c")


class _Rider:
    def __init__(self, inputs, out_shapes, sems, start, finish, aliases=None):
        self.inputs, self.out_shapes, self.sems = tuple(inputs), tuple(out_shapes), tuple(sems)
        self.start, self.finish, self.aliases = start, finish, dict(aliases or {})


def _call(body, args, *, grid, in_specs, out_specs, out_shape, name, sem, scratch_shapes=(), rider=None):
    if rider is None:
        outs = pl.pallas_call(body, grid=grid, name=name, in_specs=list(in_specs), out_specs=list(out_specs),
                              out_shape=list(out_shape), scratch_shapes=list(scratch_shapes),
                              compiler_params=_cparams(sem))(*args)
        return list(outs), []
    ni, no, ns = len(args), len(out_shape), len(scratch_shapes)
    ri, ro = len(rider.inputs), len(rider.out_shapes)

    def full(*refs):
        ins, rins = refs[:ni], refs[ni:ni + ri]
        outs, routs = refs[ni + ri:ni + ri + no], refs[ni + ri + no:ni + ri + no + ro]
        scr, rsem = refs[ni + ri + no + ro:ni + ri + no + ro + ns], refs[ni + ri + no + ro + ns:]
        first = functools.reduce(jnp.logical_and, [pl.program_id(a) == 0 for a in range(len(grid))])
        last = functools.reduce(jnp.logical_and, [pl.program_id(a) == grid[a] - 1 for a in range(len(grid))])

        @pl.when(first)
        def _():
            rider.start(rins, routs, rsem)

        body(*ins, *outs, *scr)

        @pl.when(last)
        def _():
            rider.finish(rins, routs, rsem)

    outs = pl.pallas_call(
        full, grid=grid, name=name, in_specs=list(in_specs) + [_ANY] * ri, out_specs=list(out_specs) + [_ANY] * ro,
        out_shape=list(out_shape) + list(rider.out_shapes), scratch_shapes=list(scratch_shapes) + list(rider.sems),
        input_output_aliases={ni + k: no + v for k, v in rider.aliases.items()},
        compiler_params=_cparams(("arbitrary",) * len(grid)))(*args, *rider.inputs)
    return list(outs[:no]), list(outs[no:])


def _run_rider(rider, name):
    ri = len(rider.inputs)

    def body(*refs):
        rins, routs, rsem = refs[:ri], refs[ri:ri + len(rider.out_shapes)], refs[ri + len(rider.out_shapes):]
        rider.start(rins, routs, rsem)
        rider.finish(rins, routs, rsem)

    return list(pl.pallas_call(body, name=name, in_specs=[_ANY] * ri, out_specs=[_ANY] * len(rider.out_shapes),
                               out_shape=list(rider.out_shapes), scratch_shapes=list(rider.sems),
                               input_output_aliases=dict(rider.aliases))(*rider.inputs))


def _dot(a, b):
    return jnp.dot(a.astype(BF16), b.astype(BF16), preferred_element_type=F32)


def _dot_nt(a, b):
    return lax.dot_general(a.astype(BF16), b.astype(BF16), (((1,), (1,)), ((), ())), preferred_element_type=F32)


def _dot_tn(a, b):
    return lax.dot_general(a.astype(BF16), b.astype(BF16), (((0,), (0,)), ((), ())), preferred_element_type=F32)


def _split(a):
    hi = a.astype(BF16)
    lo = (a - hi.astype(F32)).astype(BF16)
    return hi, lo


def _dot2_l(a, b):
    hi, lo = _split(a)
    return _dot(hi, b) + _dot(lo, b)


def _dot2_r(a, b):
    hi, lo = _split(b)
    return _dot(a, hi) + _dot(a, lo)


def _dot3_l(a, b):
    hi, lo = _split(a)
    lo2 = ((a - hi.astype(F32)) - lo.astype(F32)).astype(BF16)
    return _dot(hi, b) + _dot(lo, b) + _dot(lo2, b)


def _dot2_nt(a, b):
    hi, lo = _split(a)
    return _dot_nt(hi, b) + _dot_nt(lo, b)


def _silu(z):
    return z * jax.nn.sigmoid(z)


def _lse1(x):
    return jnp.log(1.0 + jnp.exp(-jnp.abs(x)))


def _cs(a):
    return jnp.sum(a, axis=0, keepdims=True)


def _iota(shape, dim):
    return lax.broadcasted_iota(jnp.int32, shape, dim)


def _mixer_matrices():
    r, c = _iota((256, 256), 0), _iota((256, 256), 1)
    same_chunk = (r >> 6) == (c >> 6)
    mats = jnp.stack([jnp.where((c > r) & same_chunk, 1.0, 0.0), jnp.where((c < r) & same_chunk, 1.0, 0.0),
                      jnp.where(same_chunk, 1.0 / 64.0, 0.0), jnp.where((r < 128) & (r - 16 == (c >> 6)), 1.0, 0.0)])
    mask = jnp.where((_iota((256, 128), 0) >> 6) == (_iota((256, 128), 1) >> 5), 1.0, 0.0)
    return mats.astype(BF16), mask.astype(F32)


def _dn(ext, k, n, h):
    return pltpu.roll(ext, k, axis=0)[h:h + n]


def _up(ext, k, n):
    return pltpu.roll(ext, ext.shape[0] - k, axis=0)[:n]


def _pool_lane_select(lane, s2, s4, s8, s16):
    return jnp.where(lane < 64, s2, jnp.where(lane < 128, s4, jnp.where(lane < 192, s8, s16)))


def _winsum_dn(ext, lane):
    s2 = ext + pltpu.roll(ext, 1, axis=0)
    s4 = s2 + pltpu.roll(s2, 2, axis=0)
    s8 = s4 + pltpu.roll(s4, 4, axis=0)
    s16 = s8 + pltpu.roll(s8, 8, axis=0)
    return _pool_lane_select(lane, s2, s4, s8, s16)


def _winsum_up(ext, lane):
    m = ext.shape[0]
    s2 = ext + pltpu.roll(ext, m - 1, axis=0)
    s4 = s2 + pltpu.roll(s2, m - 2, axis=0)
    s8 = s4 + pltpu.roll(s4, m - 4, axis=0)
    s16 = s8 + pltpu.roll(s8, m - 8, axis=0)
    return _pool_lane_select(lane, s2, s4, s8, s16)


def _pool_inv_count(tile, n):
    lane = _iota((1, 256), 1)
    win = _pool_lane_select(lane, 2.0, 4.0, 8.0, 16.0).astype(F32)
    tpos = (tile * n + _iota((n, 1), 0) + 1).astype(F32)
    return jnp.where(tpos >= win, 1.0 / win, 1.0 / tpos)


def _silu_pair(z):
    s = jax.nn.sigmoid(z)
    return z * s, s * (1.0 + z * (1.0 - s))


def _chunks(a):
    return [a[c * CH:(c + 1) * CH] for c in range(a.shape[0] // CH)]


def _halves(fn, a, b):
    return jnp.concatenate([fn(a[:, 0:128], b[:, 0:128]), fn(a[:, 128:256], b[:, 128:256])], axis=1)


def _chunk_sums(tri, a):
    return jnp.concatenate([_dot2_r(tri, a[r:r + 256]) for r in range(0, a.shape[0], 256)], axis=0)


def _mixer_tile_prep(p_ref, t_ref, xc, prm_ref, gw_v, cm_ref, mk_ref):
    tail = t_ref[...]
    pre = _dot(tail, gw_v) + prm_ref[R_GB:R_GB + 1, 0:128]
    la = (jnp.minimum(pre, 0.0) - _lse1(pre)) * INV_TAU
    dtin = tail + prm_ref[R_DTB:R_DTB + 1, 0:128]
    dtf = jnp.maximum(dtin, 0.0) + _lse1(dtin)
    dte = _dot2_l(dtf, cm_ref[3, 0:128, :])
    da = dte * prm_ref[R_AE:R_AE + 1, 0:256]
    rev = _chunk_sums(cm_ref[0], jnp.concatenate([la, da], axis=1))
    dec = jnp.exp(rev[:, 0:128])
    kd = p_ref[:, C_GK:C_GK + 128].astype(F32) * dec
    wdec = jnp.exp(rev[:, 128:384])
    w = wdec * dte
    xw = xc[:, 0:256] * w
    d_s = [jnp.exp(_cs(a)) for a in _chunks(la)]
    et = [jnp.exp(_cs(a)) for a in _chunks(da)]
    mask_t = mk_ref[...]
    ut_g = [_dot_tn(v, k) * mask_t for v, k in zip(_chunks(p_ref[:, C_GV:C_GV + 256].astype(F32)), _chunks(kd))]
    ut_s = [_halves(_dot_tn, b, x) for b, x in zip(_chunks(xc[:, 256:512]), _chunks(xw))]
    return tail, pre, dtin, dte, dec, kd, wdec, w, xw, d_s, et, ut_g, ut_s


def _rmsproj(x, nw, wp, name, tm=512, rider=None):
    t = x.shape[0]

    def body(x_ref, nw_ref, w_ref, o_ref, t_ref, h_ref):
        xv = x_ref[...]
        rs = lax.rsqrt(jnp.mean(xv * xv, axis=-1, keepdims=True) + EPS)
        h = (xv * rs * nw_ref[...]).astype(BF16)
        h_ref[...] = h
        proj = jnp.dot(h, w_ref[...], preferred_element_type=F32)
        o_ref[...] = proj[:, 0:NPM].astype(BF16)
        t_ref[...] = proj[:, NPM:NP]

    (proj, tail, h), extra = _call(
        body, (x, nw, wp), grid=(t // tm,), name=name, sem=("parallel",), rider=rider,
        in_specs=[pl.BlockSpec((tm, D), lambda i: (i, 0)), pl.BlockSpec((1, D), lambda i: (0, 0)),
                  pl.BlockSpec((D, NP), lambda i: (0, 0))],
        out_specs=[pl.BlockSpec((tm, NPM), lambda i: (i, 0)), pl.BlockSpec((tm, NP - NPM), lambda i: (i, 0)),
                   pl.BlockSpec((tm, D), lambda i: (i, 0))],
        out_shape=[jax.ShapeDtypeStruct((t, NPM), BF16), jax.ShapeDtypeStruct((t, NP - NPM), F32),
                   jax.ShapeDtypeStruct((t, D), BF16)])
    return (proj, tail), h, extra


def _head_tile(xv, tgt, w):
    rs = lax.rsqrt(jnp.mean(xv * xv, axis=-1, keepdims=True) + EPS)
    xh = xv * rs
    err = xh * w - tgt
    dy = err * (1.0 / D)
    dxh = dy * w
    dx = rs * (dxh - xh * jnp.mean(dxh * xh, axis=-1, keepdims=True))
    return dx, _cs(dy * xh), (0.5 / D) * jnp.sum(err * err)


def _dxin(dp, wpt, x, dxn, nw, name, tm=512, rider=None):
    t = x.shape[0]

    def body(dp_ref, w_ref, x_ref, dxn_ref, nw_ref, dx_ref, dnw_ref):
        @pl.when(pl.program_id(0) == 0)
        def _():
            dnw_ref[...] = jnp.zeros_like(dnw_ref)

        dh = jnp.dot(dp_ref[...], w_ref[...], preferred_element_type=F32)
        xv = x_ref[...]
        rs = lax.rsqrt(jnp.mean(xv * xv, axis=-1, keepdims=True) + EPS)
        xh = xv * rs
        dnw_ref[0:1, :] += _cs(dh * xh)
        dxh = dh * nw_ref[...]
        dx_ref[...] = dxn_ref[...] + rs * (dxh - xh * jnp.mean(dxh * xh, axis=-1, keepdims=True))

    return _call(
        body, (dp, wpt, x, dxn, nw), grid=(t // tm,), name=name, sem=("arbitrary",), rider=rider,
        in_specs=[pl.BlockSpec((tm, NP), lambda i: (i, 0)), pl.BlockSpec((NP, D), lambda i: (0, 0)),
                  pl.BlockSpec((tm, D), lambda i: (i, 0)), pl.BlockSpec((tm, D), lambda i: (i, 0)),
                  pl.BlockSpec((1, D), lambda i: (0, 0))],
        out_specs=[pl.BlockSpec((tm, D), lambda i: (i, 0)), pl.BlockSpec((8, D), lambda i: (0, 0))],
        out_shape=[jax.ShapeDtypeStruct((t, D), F32), jax.ShapeDtypeStruct((8, D), F32)])


def _dwin(h, dp, name, tm=1024, rider=None):
    t = h.shape[0]

    def body(h_ref, dp_ref, o_ref):
        @pl.when(pl.program_id(0) == 0)
        def _():
            o_ref[...] = jnp.zeros_like(o_ref)

        o_ref[...] += _dot_tn(h_ref[...], dp_ref[...])

    (dwp,), extra = _call(
        body, (h, dp), grid=(t // tm,), name=name, sem=("arbitrary",), rider=rider,
        in_specs=[pl.BlockSpec((tm, D), lambda i: (i, 0)), pl.BlockSpec((tm, NP), lambda i: (i, 0))],
        out_specs=[pl.BlockSpec((D, NP), lambda i: (0, 0))], out_shape=[jax.ShapeDtypeStruct((D, NP), F32)])
    return dwp, extra


def _mixer_fwd(proj, x, wo, prm, gw, pw, cmat, mask, name, rider=None, head=None):
    proj, tail = proj
    t = proj.shape[0]
    nt, nc = t // TB, t // CH

    def body(p_ref, t_ref, x_ref, wo_ref, prm_ref, gw_ref, pw_ref, cm_ref, mk_ref, *rest):
        (tgt_ref, fw_ref), rest = (rest[:2], rest[2:]) if head else ((None, None), rest)
        mix_ref, sg_ref, ss_ref, xn_ref = rest[:4]
        acc_ref = rest[4] if head else None
        sg_s, ss_s, h_ua, h_pu, h_sx = rest[-5:]
        i = pl.program_id(0)

        @pl.when(i == 0)
        def _():
            for r in (sg_s, ss_s, h_ua, h_pu, h_sx) + ((acc_ref,) if head else ()):
                r[...] = jnp.zeros_like(r)

        lane = _iota((1, 256), 1)
        u = p_ref[:, C_AC:C_AC + 256].astype(F32) * p_ref[:, C_AH:C_AH + 256].astype(F32)
        ext = jnp.concatenate([h_ua[...], u], axis=0)
        cv = (prm_ref[R_CAW + 2:R_CAW + 3, 0:256] * u + prm_ref[R_CAW + 1:R_CAW + 2, 0:256] * _dn(ext, 1, TB, 8)
              + prm_ref[R_CAW:R_CAW + 1, 0:256] * _dn(ext, 2, TB, 8))
        mix_ref[:, 0:256] = (p_ref[:, C_AB:C_AB + 256].astype(F32) * cv * _silu(p_ref[:, C_AZ:C_AZ + 256].astype(F32))).astype(BF16)
        h_ua[...] = u[TB - 8:, :]
        pu = p_ref[:, C_PU:C_PU + 256].astype(F32)
        ext = jnp.concatenate([h_pu[...], pu], axis=0)
        pooled = _winsum_dn(ext, lane)[16:] * _pool_inv_count(i, TB) - pu
        mixed = _dot(pooled, pw_ref[...])
        mix_ref[:, 512:768] = (prm_ref[R_PSC:R_PSC + 1, 0:256] * mixed * _silu(p_ref[:, C_PZ:C_PZ + 256].astype(F32))).astype(BF16)
        h_pu[...] = pu[TB - 16:, :]
        sx = p_ref[:, C_SX:C_SX + 768].astype(F32)
        ext = jnp.concatenate([h_sx[...], sx], axis=0)
        xc = _silu(prm_ref[R_SCW + 3:R_SCW + 4, :] * sx + prm_ref[R_SCW + 2:R_SCW + 3, :] * _dn(ext, 1, TB, 8)
                   + prm_ref[R_SCW + 1:R_SCW + 2, :] * _dn(ext, 2, TB, 8) + prm_ref[R_SCW:R_SCW + 1, :] * _dn(ext, 3, TB, 8)
                   + prm_ref[R_SCB:R_SCB + 1, :])
        h_sx[...] = sx[TB - 8:, :]

        _, _, _, _, _, _, _, _, _, d_s, et, ut_g, ut_s = _mixer_tile_prep(p_ref, t_ref, xc, prm_ref, gw_ref[...], cm_ref, mk_ref)
        s_g, s_s = sg_s[...], ss_s[...]
        o, y = [], []
        qs = _chunks(p_ref[:, C_GQ:C_GQ + 128].astype(F32) * GLA_SCALE)
        cm = _chunks(xc[:, 512:768])
        for c in range(NCH):
            sg_ref[c] = s_g
            ss_ref[c] = s_s
            s_g = s_g * d_s[c] + ut_g[c]
            s_s = s_s * et[c] + ut_s[c]
            o.append(_dot_nt(qs[c], s_g))
            y.append(_halves(_dot, cm[c], s_s))
        sg_s[...] = s_g
        ss_s[...] = s_s
        o = jnp.concatenate(o, axis=0)
        on = o * lax.rsqrt(_dot2_l(o * o, cm_ref[2]) + EPS)
        mix_ref[:, 256:512] = (on * prm_ref[R_GNW:R_GNW + 1, 0:256] * _silu(p_ref[:, C_GZ:C_GZ + 256].astype(F32))).astype(BF16)
        y2 = ((jnp.concatenate(y, axis=0) + prm_ref[R_DE:R_DE + 1, 0:256] * xc[:, 0:256])
              * _silu(p_ref[:, C_SZ:C_SZ + 256].astype(F32)))
        mix_ref[:, 768:1024] = (y2 * lax.rsqrt(jnp.mean(y2 * y2, axis=-1, keepdims=True) + EPS)
                                * prm_ref[R_SNW:R_SNW + 1, 0:256]).astype(BF16)
        xn = x_ref[...] + jnp.dot(mix_ref[...], wo_ref[...], preferred_element_type=F32)
        if head:
            xn_ref[...], dfw, loss = _head_tile(xn, tgt_ref[...], fw_ref[...])
            acc_ref[0:1, :] += dfw
            acc_ref[1:2, :] += jnp.zeros((1, D), F32) + loss
        else:
            xn_ref[...] = xn

    row = pl.BlockSpec((TB, D), lambda i: (i, 0))
    return _call(
        body, (proj, tail, x, wo, prm, gw, pw, cmat, mask) + tuple(head or ()), grid=(nt,), name=name, sem=("arbitrary",),
        rider=rider,
        in_specs=[pl.BlockSpec((TB, NPM), lambda i: (i, 0)), pl.BlockSpec((TB, NP - NPM), lambda i: (i, 0)), row,
                  pl.BlockSpec((D, D), lambda i: (0, 0)), pl.BlockSpec((16, 768), lambda i: (0, 0)),
                  pl.BlockSpec((128, 128), lambda i: (0, 0)), pl.BlockSpec((256, 256), lambda i: (0, 0)),
                  pl.BlockSpec((4, 256, 256), lambda i: (0, 0, 0)), pl.BlockSpec((256, 128), lambda i: (0, 0))]
        + ([row, pl.BlockSpec((1, D), lambda i: (0, 0))] if head else []),
        out_specs=[row, pl.BlockSpec((NCH, 256, 128), lambda i: (i, 0, 0)),
                   pl.BlockSpec((NCH, 128, 256), lambda i: (i, 0, 0)), row]
        + ([pl.BlockSpec((8, D), lambda i: (0, 0))] if head else []),
        out_shape=[jax.ShapeDtypeStruct((t, D), BF16), jax.ShapeDtypeStruct((nc, 256, 128), F32),
                   jax.ShapeDtypeStruct((nc, 128, 256), F32), jax.ShapeDtypeStruct((t, D), F32)]
        + ([jax.ShapeDtypeStruct((8, D), F32)] if head else []),
        scratch_shapes=[pltpu.VMEM((256, 128), F32), pltpu.VMEM((128, 256), F32), pltpu.VMEM((8, 256), F32),
                        pltpu.VMEM((16, 256), F32), pltpu.VMEM((8, 768), F32)])


def _mixer_bwd(proj, dxn, wot, mix, sg, ss, prm, gw, pw, cmat, mask, name, rider=None):
    proj, tail = proj
    t = proj.shape[0]
    nt = t // TB
    rev = lambda i: nt - 1 - i

    def body(p_ref, hp_ref, t_ref, dxn_ref, wot_ref, mix_ref, sg_ref, ss_ref, prm_ref, gw_ref, pw_ref, cm_ref, mk_ref,
             dp_ref, sgc_ref, dwo_ref,
             gg_s, gs_s, h_dcv, h_dpl, h_dpre, gsm_ref, dgw_ref, dpw_ref, dm_ref):
        i = pl.program_id(0)
        tile = nt - 1 - i

        @pl.when(i == 0)
        def _():
            for r in (gg_s, gs_s, h_dcv, h_dpl, h_dpre, gsm_ref, dgw_ref, dpw_ref, dwo_ref):
                r[...] = jnp.zeros_like(r)

        dxn = dxn_ref[...].astype(BF16)
        dm_ref[...] = jnp.dot(dxn, wot_ref[...], preferred_element_type=F32)
        dwo_ref[...] += _dot_tn(mix_ref[...], dxn)

        lane = _iota((1, 256), 1)
        first = (tile > 0).astype(F32)
        ah, ac = p_ref[:, C_AH:C_AH + 256].astype(F32), p_ref[:, C_AC:C_AC + 256].astype(F32)
        ab, az = p_ref[:, C_AB:C_AB + 256].astype(F32), p_ref[:, C_AZ:C_AZ + 256].astype(F32)
        w0, w1, w2 = (prm_ref[R_CAW + j:R_CAW + j + 1, 0:256] for j in range(3))
        u = ac * ah
        ext = jnp.concatenate([(hp_ref[:, C_AC:C_AC + 256].astype(F32) * hp_ref[:, C_AH:C_AH + 256].astype(F32))[8:16] * first, u], axis=0)
        u1, u2 = _dn(ext, 1, TB, 8), _dn(ext, 2, TB, 8)
        cv = w2 * u + w1 * u1 + w0 * u2
        g = dm_ref[:, 0:256]
        sz, dsz = _silu_pair(az)
        dp_ref[:, C_AB:C_AB + 256] = (g * cv * sz).astype(BF16)
        dp_ref[:, C_AZ:C_AZ + 256] = (g * ab * cv * dsz).astype(BF16)
        dcv = g * ab * sz
        dext = jnp.concatenate([dcv, h_dcv[...]], axis=0)
        du = w2 * dcv + w1 * _up(dext, 1, TB) + w0 * _up(dext, 2, TB)
        dp_ref[:, C_AC:C_AC + 256] = (du * ah).astype(BF16)
        dp_ref[:, C_AH:C_AH + 256] = (du * ac).astype(BF16)
        gsm_ref[R_CAW:R_CAW + 1, 0:256] += _cs(dcv * u2)
        gsm_ref[R_CAW + 1:R_CAW + 2, 0:256] += _cs(dcv * u1)
        gsm_ref[R_CAW + 2:R_CAW + 3, 0:256] += _cs(dcv * u)
        h_dcv[...] = dcv[0:8, :]
        pu, pz = p_ref[:, C_PU:C_PU + 256].astype(F32), p_ref[:, C_PZ:C_PZ + 256].astype(F32)
        psc = prm_ref[R_PSC:R_PSC + 1, 0:256]
        icnt = _pool_inv_count(tile, TB)
        ext = jnp.concatenate([hp_ref[:, C_PU:C_PU + 256].astype(F32) * first, pu], axis=0)
        pooled = _winsum_dn(ext, lane)[16:] * icnt - pu
        pw_v = pw_ref[...]
        mixed = _dot(pooled, pw_v)
        g = dm_ref[:, 512:768]
        sz, dsz = _silu_pair(pz)
        gsm_ref[R_PSC:R_PSC + 1, 0:256] += _cs(g * mixed * sz)
        dp_ref[:, C_PZ:C_PZ + 256] = (g * psc * mixed * dsz).astype(BF16)
        dmixed = g * psc * sz
        dpw_ref[...] += _dot_tn(pooled, dmixed)
        dpooled = _dot_nt(dmixed, pw_v)
        qd = dpooled * icnt
        dext = jnp.concatenate([qd, h_dpl[...]], axis=0)
        dp_ref[:, C_PU:C_PU + 256] = (_winsum_up(dext, lane)[:TB] - dpooled).astype(BF16)
        h_dpl[...] = qd[0:16, :]
        sx = p_ref[:, C_SX:C_SX + 768].astype(F32)
        cw = [prm_ref[R_SCW + j:R_SCW + j + 1, :] for j in range(4)]
        ext = jnp.concatenate([hp_ref[:, C_SX:C_SX + 768].astype(F32)[8:16] * first, sx], axis=0)
        sx1, sx2, sx3 = _dn(ext, 1, TB, 8), _dn(ext, 2, TB, 8), _dn(ext, 3, TB, 8)
        cpre = cw[3] * sx + cw[2] * sx1 + cw[1] * sx2 + cw[0] * sx3 + prm_ref[R_SCB:R_SCB + 1, :]
        xc, dxc = _silu_pair(cpre)
        xs, bm, cm = xc[:, 0:256], xc[:, 256:512], xc[:, 512:768]

        gw_v = gw_ref[...]
        tail, pre, dtin, dte, dec, kd, wdec, w, xw, d_s, et, ut_g, ut_s = _mixer_tile_prep(p_ref, t_ref, xc, prm_ref,
                                                                                          gw_v, cm_ref, mk_ref)
        gmean = cm_ref[2]
        mask_t = mk_ref[...]
        gnw = prm_ref[R_GNW:R_GNW + 1, 0:256]
        a_e = prm_ref[R_AE:R_AE + 1, 0:256]
        d_e = prm_ref[R_DE:R_DE + 1, 0:256]
        snw = prm_ref[R_SNW:R_SNW + 1, 0:256]
        sg_in = [sg_ref[c] for c in range(NCH)]
        ss_in = [ss_ref[c] for c in range(NCH)]
        sg_n = [sg_in[c] * d_s[c] + ut_g[c] for c in range(NCH)]
        ss_n = [ss_in[c] * et[c] + ut_s[c] for c in range(NCH)]
        qs = _chunks(p_ref[:, C_GQ:C_GQ + 128].astype(F32) * GLA_SCALE)
        cm_c, bm_c, xw_c, kd_c = _chunks(cm), _chunks(bm), _chunks(xw), _chunks(kd)
        v_c = _chunks(p_ref[:, C_GV:C_GV + 256].astype(F32))
        o = jnp.concatenate([_dot_nt(qs[c], sg_n[c]) for c in range(NCH)], axis=0)
        y = jnp.concatenate([_halves(_dot, cm_c[c], ss_n[c]) for c in range(NCH)], axis=0) + d_e * xs
        gz = p_ref[:, C_GZ:C_GZ + 256].astype(F32)
        r = lax.rsqrt(_dot2_l(o * o, gmean) + EPS)
        on = o * r
        dyb = dm_ref[:, 256:512]
        sz, dsz = _silu_pair(gz)
        dp_ref[:, C_GZ:C_GZ + 256] = (dyb * on * gnw * dsz).astype(BF16)
        tg = dyb * sz
        gsm_ref[R_GNW:R_GNW + 1, 0:256] += _cs(tg * on)
        don = tg * gnw
        do_c = _chunks(r * (don - on * _dot2_l(don * on, gmean)))
        ssz = p_ref[:, C_SZ:C_SZ + 256].astype(F32)
        sil, dsil = _silu_pair(ssz)
        y2 = y * sil
        r = lax.rsqrt(jnp.mean(y2 * y2, axis=-1, keepdims=True) + EPS)
        yn = y2 * r
        dyd = dm_ref[:, 768:1024]
        gsm_ref[R_SNW:R_SNW + 1, 0:256] += _cs(dyd * yn)
        dn = dyd * snw
        dy2 = r * (dn - yn * jnp.mean(dn * yn, axis=-1, keepdims=True))
        dp_ref[:, C_SZ:C_SZ + 256] = (dy2 * y * dsil).astype(BF16)
        dy = dy2 * sil
        gsm_ref[R_DE:R_DE + 1, 0:256] += _cs(dy * xs)
        dy_c = _chunks(dy)
        dq = jnp.concatenate([_dot(do_c[c], sg_n[c]) for c in range(NCH)], axis=0)
        dp_ref[:, C_GQ:C_GQ + 128] = (dq * GLA_SCALE).astype(BF16)
        dcm = jnp.concatenate([_halves(_dot_nt, dy_c[c], ss_n[c]) for c in range(NCH)], axis=0)
        gg = [_dot_tn(do_c[c], qs[c]) * mask_t for c in range(NCH)]
        gs = [_halves(_dot_tn, cm_c[c], dy_c[c]) for c in range(NCH)]
        car_g, car_s = gg_s[...], gs_s[...]
        for c in reversed(range(NCH)):
            gg[c] = gg[c] + car_g
            gs[c] = gs[c] + car_s
            car_g = gg[c] * d_s[c]
            car_s = gs[c] * et[c]
        gg_s[...] = car_g
        gs_s[...] = car_s
        dkd = jnp.concatenate([_dot(v_c[c], gg[c]) for c in range(NCH)], axis=0)
        dp_ref[:, C_GV:C_GV + 256] = jnp.concatenate([_dot_nt(kd_c[c], gg[c]) for c in range(NCH)], axis=0).astype(BF16)
        dp_ref[:, C_GK:C_GK + 128] = (dkd * dec).astype(BF16)
        dbm = jnp.concatenate([_halves(_dot_nt, xw_c[c], gs[c]) for c in range(NCH)], axis=0)
        dxw = jnp.concatenate([_halves(_dot, bm_c[c], gs[c]) for c in range(NCH)], axis=0)
        dxs = dy * d_e + dxw * w
        dw = dxw * xs
        dsuf = _chunk_sums(cm_ref[1], jnp.concatenate([dkd * kd, dw * dte * wdec], axis=1))
        tot_g = jnp.concatenate([jnp.broadcast_to(_cs(gg[c] * sg_in[c]) * d_s[c], (CH, 128)) for c in range(NCH)], axis=0)
        tot_s = jnp.concatenate([jnp.broadcast_to(_cs(gs[c] * ss_in[c]) * et[c], (CH, 256)) for c in range(NCH)], axis=0)
        dpre = (dsuf[:, 0:128] + tot_g) * INV_TAU * jax.nn.sigmoid(-pre)
        dgw_ref[...] += _dot_tn(tail, dpre)
        gsm_ref[R_GB:R_GB + 1, 0:128] += _cs(dpre)
        dda = dsuf[:, 128:384] + tot_s
        gsm_ref[R_AE:R_AE + 1, 0:256] += _cs(dda * dte)
        dtail_s = _dot2_nt(dw * wdec + dda * a_e, cm_ref[3, 0:128, :]) * jax.nn.sigmoid(dtin)
        gsm_ref[R_DTB:R_DTB + 1, 0:128] += _cs(dtail_s)
        dp_ref[:, C_TL:C_TL + 128] = (_dot_nt(dpre, gw_v) + dtail_s).astype(BF16)
        dpre_c = jnp.concatenate([dxs, dbm, dcm], axis=1) * dxc
        dext = jnp.concatenate([dpre_c, h_dpre[...]], axis=0)
        dp_ref[:, C_SX:C_SX + 768] = (cw[3] * dpre_c + cw[2] * _up(dext, 1, TB) + cw[1] * _up(dext, 2, TB)
                                      + cw[0] * _up(dext, 3, TB)).astype(BF16)
        gsm_ref[R_SCW + 3:R_SCW + 4, :] += _cs(dpre_c * sx)
        gsm_ref[R_SCW + 2:R_SCW + 3, :] += _cs(dpre_c * sx1)
        gsm_ref[R_SCW + 1:R_SCW + 2, :] += _cs(dpre_c * sx2)
        gsm_ref[R_SCW:R_SCW + 1, :] += _cs(dpre_c * sx3)
        gsm_ref[R_SCB:R_SCB + 1, :] += _cs(dpre_c)
        h_dpre[...] = dpre_c[0:8, :]

        @pl.when(i == nt - 1)
        def _():
            ri, ci = _iota((256, 256), 0), _iota((256, 256), 1)
            per_head = jnp.where((ri >> 6) == ci, 1.0, 0.0).astype(BF16)
            per_dv = jnp.where((ri & 63) == ci, 1.0, 0.0).astype(BF16)
            row = _iota((8, 256), 0)
            top = gsm_ref[0:8, 0:256]
            sgc_ref[0:8, 0:256] = jnp.where(row == R_GNW, _dot3_l(top, per_dv), top)
            bot = gsm_ref[8:16, 0:256]
            fold = _dot3_l(jnp.where(row == R_AE - 8, bot * a_e, bot), per_head)
            sgc_ref[8:16, 0:256] = jnp.where((row == R_AE - 8) | (row == R_DE - 8), fold, bot)
            sgc_ref[0:16, 256:768] = gsm_ref[:, 256:768]
            sgc_ref[0:16, 768:896] = dgw_ref[0:16, :]
            sgc_ref[0:16, 896:1024] = jnp.zeros((16, 128), F32)
            diag = _pool_lane_select(lane, dpw_ref[0:64, :], dpw_ref[64:128, :], dpw_ref[128:192, :], dpw_ref[192:256, :])
            for q in range(4):
                sgc_ref[16:32, 256 * q:256 * q + 256] = diag[16 * q:16 * q + 16, :]

    return _call(
        body, (proj, proj, tail, dxn, wot, mix, sg, ss, prm, gw, pw, cmat, mask), grid=(nt,), name=name,
        sem=("arbitrary",), rider=rider,
        in_specs=[pl.BlockSpec((TB, NPM), lambda i: (rev(i), 0)),
                  pl.BlockSpec((16, NPM), lambda i: (jnp.maximum(rev(i) * (TB // 16) - 1, 0), 0)),
                  pl.BlockSpec((TB, NP - NPM), lambda i: (rev(i), 0)),
                  pl.BlockSpec((TB, D), lambda i: (rev(i), 0)), pl.BlockSpec((D, D), lambda i: (0, 0)),
                  pl.BlockSpec((TB, D), lambda i: (rev(i), 0)),
                  pl.BlockSpec((NCH, 256, 128), lambda i: (rev(i), 0, 0)),
                  pl.BlockSpec((NCH, 128, 256), lambda i: (rev(i), 0, 0)),
                  pl.BlockSpec((16, 768), lambda i: (0, 0)), pl.BlockSpec((128, 128), lambda i: (0, 0)),
                  pl.BlockSpec((256, 256), lambda i: (0, 0)), pl.BlockSpec((4, 256, 256), lambda i: (0, 0, 0)),
                  pl.BlockSpec((256, 128), lambda i: (0, 0))],
        out_specs=[pl.BlockSpec((TB, NP), lambda i: (rev(i), 0)), pl.BlockSpec((32, 1024), lambda i: (0, 0)),
                   pl.BlockSpec((D, D), lambda i: (0, 0))],
        out_shape=[jax.ShapeDtypeStruct((t, NP), BF16), jax.ShapeDtypeStruct((32, 1024), F32),
                   jax.ShapeDtypeStruct((D, D), F32)],
        scratch_shapes=[pltpu.VMEM((256, 128), F32), pltpu.VMEM((128, 256), F32), pltpu.VMEM((8, 256), F32),
                        pltpu.VMEM((16, 256), F32), pltpu.VMEM((8, 768), F32), pltpu.VMEM((16, 768), F32),
                        pltpu.VMEM((128, 128), F32), pltpu.VMEM((256, 256), F32), pltpu.VMEM((TB, D), F32)])


SHARD = NPROJ // 4
SHARD_PAD = 896


def _ranges_to_perm(o, n):
    out, p = [], 0
    for start, size in _PERM:
        a, b = max(o, start), min(o + n, start + size)
        if a < b:
            out.append((a, b - a, p + a - start))
        p += size
    return out


def _ranges_to_orig(p0, n):
    out, p = [], 0
    for start, size in _PERM:
        a, b = max(p0, p), min(p0 + n, p + size)
        if a < b:
            out.append((a, b - a, start + a - p))
        p += size
    return out


def _lane_window(load, lo, n, d, lane):
    a = 128 * (lo // 128)
    off = lo - a
    w = 128 if off + n <= 128 else 256
    chunk = load(a, w)
    shift = (d - off) % w
    if shift:
        chunk = pltpu.roll(chunk, shift, axis=1)
    return jnp.where((lane >= d) & (lane < d + n), chunk[:, 0:128], 0.0)


def _assemble_w_in(slabs, name, rb=256):
    def body(s_ref, wp_ref, wpt_ref):
        lane = _iota((1, 128), 1)
        for b in range(NP // 128):
            acc = jnp.zeros((rb, 128), F32)
            for p, n, o in _ranges_to_orig(128 * b, 128):
                while n > 0:
                    s, lo = o // SHARD, o % SHARD
                    cnt = min(n, SHARD - lo)
                    acc = acc + _lane_window(lambda a, w, s=s: s_ref[s, :, a:a + w].astype(F32), lo, cnt, p - 128 * b, lane)
                    o, p, n = o + cnt, p + cnt, n - cnt
            wp_ref[:, 128 * b:128 * b + 128] = acc.astype(BF16)
            wpt_ref[128 * b:128 * b + 128, :] = acc.T.astype(BF16)

    return pl.pallas_call(
        body, grid=(D // rb,), name=name,
        in_specs=[pl.BlockSpec((4, rb, SHARD_PAD), lambda i: (0, i, 0))],
        out_specs=[pl.BlockSpec((rb, NP), lambda i: (i, 0)), pl.BlockSpec((NP, rb), lambda i: (0, i))],
        out_shape=[jax.ShapeDtypeStruct((D, NP), BF16), jax.ShapeDtypeStruct((NP, D), BF16)],
        compiler_params=_cparams(("parallel",)))(slabs)


def _split_dw_in(dwp, name, rb=256):
    rows = dwp.shape[0]

    def body(g_ref, o_ref):
        lane = _iota((1, 128), 1)
        for s in range(4):
            for k in range(SHARD_PAD // 128):
                acc = jnp.zeros((rb, 128), F32)
                n_valid = min(128, SHARD - 128 * k)
                for o, n, p in _ranges_to_perm(SHARD * s + 128 * k, n_valid):
                    acc = acc + _lane_window(lambda a, w: g_ref[:, a:a + w].astype(F32), p, n, o - SHARD * s - 128 * k, lane)
                o_ref[s, :, 128 * k:128 * k + 128] = acc.astype(o_ref.dtype)

    return pl.pallas_call(
        body, grid=(rows // rb,), name=name,
        in_specs=[pl.BlockSpec((rb, NP), lambda i: (i, 0))],
        out_specs=pl.BlockSpec((4, rb, SHARD_PAD), lambda i: (0, i, 0)),
        out_shape=jax.ShapeDtypeStruct((4, rows, SHARD_PAD), dwp.dtype),
        compiler_params=_cparams(("parallel",)))(dwp)


def _half(c, n):
    return pl.ds(pl.multiple_of(c * (n // 2), n // 2), n // 2)


def _other_chips(x, y):
    return ((1 - x, y), (x, 1 - y), (1 - x, 1 - y))


def _remote(src, dst, send, recv, k, dev):
    return pltpu.make_async_remote_copy(src_ref=src, dst_ref=dst, send_sem=send.at[k], recv_sem=recv.at[k], device_id=dev,
                                        device_id_type=MESH)


def _sem(n):
    return pltpu.SemaphoreType.DMA((n,))


def _rider_gather_ici(shards):
    shards = tuple(shards)
    n = len(shards)

    def copies(rins, routs, sems, arrivals=True):
        send, recv = sems
        x, y, c = _place()
        me = 2 * x + y
        out, inc = [], []
        for j, (px, py) in enumerate(_other_chips(x, y)):
            for k in range(n):
                rows = _half(c, shards[k].shape[0])
                out.append(_remote(rins[k].at[rows], routs[k].at[me, rows], send, recv, n * j + k, (px, py, c)))
                if arrivals:
                    inc.append(_remote(rins[k].at[rows], routs[k].at[2 * px + py, rows], send, recv, n * j + k, (px, py, c)))
        return out, inc

    def start(rins, routs, sems):
        for cp in copies(rins, routs, sems, arrivals=False)[0]:
            cp.start()

    def finish(rins, routs, sems):
        out, inc = copies(rins, routs, sems)
        for cp in inc:
            cp.wait_recv()
        for cp in out:
            cp.wait_send()

    return _Rider(shards, [jax.ShapeDtypeStruct((4,) + a.shape, a.dtype) for a in shards], [_sem(3 * n), _sem(3 * n)],
                  start, finish)


def _gather_ici_two_hops(shards, extra):
    shards = tuple(shards)
    n = len(shards)

    def body(*refs):
        ins, e_in, outs, e_out = refs[:n], refs[n], refs[n + 1:2 * n + 1], refs[2 * n + 1]
        send, recv = refs[2 * n + 2:]
        x, y, c = _place()
        slab = lambda px, py: 2 * px + py
        xn, yn, dg = (1 - x, y), (x, 1 - y), (1 - x, 1 - y)

        def part(k, q):
            r = shards[k].shape[0] // 4
            return pl.ds(pl.multiple_of(c * 2 * r + q * r, r), r)

        def hop(k, q, src_chip, to, sem):
            rows = part(k, q)
            src = ins[k].at[rows] if src_chip is None else outs[k].at[slab(*src_chip), rows]
            own = (x, y) if src_chip is None else src_chip
            return _remote(src, outs[k].at[slab(*own), rows], send, recv, sem, (*to, c))

        small = [_remote(e_in, e_out.at[slab(x, y)], send, recv, 6 * n + j, (*to, c)) for j, to in enumerate((xn, yn, dg))]
        first = [hop(k, q, None, (xn, yn)[q], 2 * k + q) for k in range(n) for q in (0, 1)]
        for cp in small + first:
            cp.start()
        for k in range(n):
            for q in (0, 1):
                nb = (xn, yn)[q]
                _remote(ins[k].at[part(k, q)], outs[k].at[slab(*nb), part(k, q)], send, recv, 2 * k + q, (*nb, c)).wait_recv()
        second = []
        for k in range(n):
            for q in (0, 1):
                to, via = (yn, xn)[q], (xn, yn)[q]
                second.append(hop(k, q, None, to, 2 * n + 4 * k + 2 * q))
                second.append(hop(k, q, via, to, 2 * n + 4 * k + 2 * q + 1))
        for cp in second:
            cp.start()
        for k in range(n):
            for q in (0, 1):
                frm, rows = (yn, xn)[q], part(k, q)
                for j, origin in enumerate((frm, dg)):
                    _remote(ins[k].at[rows], outs[k].at[slab(*origin), rows], send, recv, 2 * n + 4 * k + 2 * q + j,
                            (*frm, c)).wait_recv()
        for j, frm in enumerate((xn, yn, dg)):
            _remote(e_in, e_out.at[slab(*frm)], send, recv, 6 * n + j, (*frm, c)).wait_recv()
        for cp in small + first + second:
            cp.wait_send()

    outs = pl.pallas_call(
        body, name="gather_ici0", in_specs=[_ANY] * (n + 1), out_specs=[_ANY] * (n + 1),
        out_shape=[jax.ShapeDtypeStruct((4,) + a.shape, a.dtype) for a in shards + (extra,)],
        scratch_shapes=[_sem(6 * n + 3), _sem(6 * n + 3)])(*shards, extra)
    return list(outs)


def _rider_gather_d2d(slabs):
    slabs = tuple(slabs)
    n = len(slabs)

    def copies(routs, sems, arrivals=True):
        send, recv = sems
        x, y, c = _place()
        out, inc = [], []
        for j, (px, py) in enumerate(_other_chips(x, y)):
            for k in range(n):
                rows = slabs[k].shape[1]
                mine, theirs = routs[k].at[2 * px + py, _half(c, rows)], routs[k].at[2 * px + py, _half(1 - c, rows)]
                out.append(_remote(mine, mine, send, recv, n * j + k, (x, y, 1 - c)))
                if arrivals:
                    inc.append(_remote(theirs, theirs, send, recv, n * j + k, (x, y, 1 - c)))
        return out, inc

    def start(rins, routs, sems):
        for cp in copies(routs, sems, arrivals=False)[0]:
            cp.start()

    def finish(rins, routs, sems):
        out, inc = copies(routs, sems)
        for cp in inc:
            cp.wait_recv()
        for cp in out:
            cp.wait_send()

    return _Rider(slabs, [jax.ShapeDtypeStruct(a.shape, a.dtype) for a in slabs], [_sem(3 * n), _sem(3 * n)], start, finish,
                  aliases={k: k for k in range(n)})


def _rider_swap(parts):
    parts = tuple(parts)
    n = len(parts)

    def copies(rins, routs, sems):
        send, recv = sems
        x, y, c = _place()
        return [_remote(rins[k].at[:, _half(1 - c, parts[k].shape[1])], routs[k], send, recv, k, (x, y, 1 - c))
                for k in range(n)]

    def start(rins, routs, sems):
        for cp in copies(rins, routs, sems):
            cp.start()

    def finish(rins, routs, sems):
        for cp in copies(rins, routs, sems):
            cp.wait()

    return _Rider(parts, [jax.ShapeDtypeStruct((a.shape[0], a.shape[1] // 2, a.shape[2]), a.dtype) for a in parts],
                  [_sem(n), _sem(n)], start, finish)


def _rider_scatter(parts):
    parts = tuple(parts)
    n = len(parts)

    def copies(rins, routs, sems, arrivals=True):
        send, recv = sems
        x, y, c = _place()
        me = 2 * x + y
        out, inc = [], []
        for j, (px, py) in enumerate(_other_chips(x, y)):
            for k in range(n):
                out.append(_remote(rins[k].at[2 * px + py], routs[k].at[me], send, recv, n * j + k, (px, py, c)))
                if arrivals:
                    inc.append(_remote(rins[k].at[me], routs[k].at[2 * px + py], send, recv, n * j + k, (px, py, c)))
        return out, inc

    def start(rins, routs, sems):
        for cp in copies(rins, routs, sems, arrivals=False)[0]:
            cp.start()

    def finish(rins, routs, sems):
        out, inc = copies(rins, routs, sems)
        for cp in inc:
            cp.wait_recv()
        for cp in out:
            cp.wait_send()

    return _Rider(parts, [jax.ShapeDtypeStruct(a.shape, a.dtype) for a in parts], [_sem(3 * n), _sem(3 * n)], start, finish)


def _rider_share(fulls):
    fulls = tuple(fulls)
    n = len(fulls)

    def copies(routs, sems, arrivals=True):
        send, recv = sems
        x, y, c = _place()
        out, inc = [], []
        for k in range(n):
            mine, theirs = routs[k].at[_half(c, fulls[k].shape[0])], routs[k].at[_half(1 - c, fulls[k].shape[0])]
            out.append(_remote(mine, mine, send, recv, k, (x, y, 1 - c)))
            if arrivals:
                inc.append(_remote(theirs, theirs, send, recv, k, (x, y, 1 - c)))
        return out, inc

    def start(rins, routs, sems):
        for cp in copies(routs, sems, arrivals=False)[0]:
            cp.start()

    def finish(rins, routs, sems):
        out, inc = copies(routs, sems)
        for cp in inc:
            cp.wait_recv()
        for cp in out:
            cp.wait_send()

    return _Rider(fulls, [jax.ShapeDtypeStruct(a.shape, a.dtype) for a in fulls], [_sem(n), _sem(n)], start, finish,
                  aliases={k: k for k in range(n)})


def _pair_sum(core, full, recv, name, br=128):
    n, rows, cols = recv.shape

    def body(c_ref, a_ref, b_ref, o_ref):
        o_ref[...] = (a_ref[...] + b_ref[...]).astype(BF16)

    nb = rows // br
    return pl.pallas_call(
        body, name=name, out_shape=jax.ShapeDtypeStruct(recv.shape, BF16),
        grid_spec=pltpu.PrefetchScalarGridSpec(
            num_scalar_prefetch=1, grid=(n, nb),
            in_specs=[pl.BlockSpec((1, br, cols), lambda i, j, c: (i, c[0] * nb + j, 0)),
                      pl.BlockSpec((1, br, cols), lambda i, j, c: (i, j, 0))],
            out_specs=pl.BlockSpec((1, br, cols), lambda i, j, c: (i, j, 0))),
        compiler_params=_cparams(("parallel", "parallel")))(core, full, recv)


def _chip_sum(place, gathered, mine, name, br=128):
    _, r, c = gathered.shape
    nb = r // br

    def body(p_ref, g_ref, m_ref, o_ref):
        slab = lambda j: jnp.where(p_ref[1] == j, m_ref[j], g_ref[j]).astype(F32)
        o_ref[...] = ((slab(0) + slab(1)) + slab(2)) + slab(3)

    return pl.pallas_call(
        body, name=name, out_shape=jax.ShapeDtypeStruct((2 * r, c), F32),
        grid_spec=pltpu.PrefetchScalarGridSpec(
            num_scalar_prefetch=1, grid=(nb,),
            in_specs=[pl.BlockSpec((4, br, c), lambda i, p: (0, i, 0)), pl.BlockSpec((4, br, c), lambda i, p: (0, i, 0))],
            out_specs=pl.BlockSpec((br, c), lambda i, p: (p[0] * nb + i, 0))),
        compiler_params=_cparams(("parallel",)))(place, gathered, mine)


def _adamw(w, g, m, v, name, br):
    n, r, c = w.shape

    def body(w_ref, g_ref, m_ref, v_ref, d_ref, m2_ref, v2_ref):
        d_ref[...], m2_ref[...], v2_ref[...] = _adam_math(w_ref[...], g_ref[...], m_ref[...], v_ref[...])

    spec = pl.BlockSpec((1, br, c), lambda i, j: (i, j, 0))
    shp = jax.ShapeDtypeStruct(w.shape, F32)
    return pl.pallas_call(body, grid=(n, r // br), name=name, in_specs=[spec] * 4, out_specs=[spec] * 3,
                          out_shape=[shp] * 3, compiler_params=_cparams(("parallel", "parallel")))(w, g, m, v)


def _adamw_w_in(w, g, m, v, name, bc=93):
    cols = w.shape[2]
    lead = lambda a: jnp.transpose(a, (2, 0, 1))
    g = jnp.stack([a[:, 0:cols] for a in g])

    def body(w_ref, g_ref, m_ref, v_ref, go_ref, d_ref, m2_ref, v2_ref):
        for l in range(2):
            gv = g_ref[:, l, :]
            d_ref[:, l, :], m2_ref[:, l, :], v2_ref[:, l, :] = _adam_math(w_ref[:, l, :], gv, m_ref[:, l, :], v_ref[:, l, :])
            go_ref[:, l, :] = gv

    spec = pl.BlockSpec((bc, 2, D), lambda i: (i, 0, 0))
    outs = pl.pallas_call(body, grid=(cols // bc,), name=name, in_specs=[spec] * 4, out_specs=[spec] * 4,
                          out_shape=[jax.ShapeDtypeStruct((cols, 2, D), F32)] * 4,
                          compiler_params=_cparams(("parallel",)))(lead(w), lead(g), lead(m), lead(v))
    return [jnp.transpose(o, (1, 2, 0)) for o in outs]


_SMALL_NAMES = ("norm_w", "conv_a_w", "gla_gate_w", "gla_gate_b", "gla_norm_w", "pool_w", "pool_scale", "ssd_conv_w",
                "ssd_conv_b", "ssd_dt_bias", "ssd_a_log", "ssd_d", "ssd_norm_w", "final_norm_w")
SMALL_ROWS = 80


def _adam_math(w, g, m, v):
    m2 = ADAM_B1 * m + (1.0 - ADAM_B1) * g
    v2 = ADAM_B2 * v + (1.0 - ADAM_B2) * (g * g)
    m_hat = m2 / (1.0 - ADAM_B1 ** ADAM_STEP)
    v_hat = v2 / (1.0 - ADAM_B2 ** ADAM_STEP)
    return -ADAM_LR * (m_hat / (jnp.sqrt(v_hat) + ADAM_EPS) + ADAM_WD * w), m2, v2


def _small_slices(name, chip):
    if name == "conv_a_w":
        return [((), slice(R_CAW, R_CAW + 3), slice(64 * chip, 64 * chip + 64))]
    if name == "ssd_conv_w":
        return [((), slice(R_SCW, R_SCW + 4), slice(192 * chip, 192 * chip + 192))]
    if name == "gla_gate_w":
        return [((), slice(0, 16), slice(768, 896))]
    if name == "pool_w":
        return [((g, slice(16 * q, 16 * q + 16)), slice(16, 32), slice(256 * q + 64 * g, 256 * q + 64 * g + 64))
                for g in range(4) for q in range(4)]
    row, lanes = {"gla_gate_b": (R_GB, slice(0, 128)), "gla_norm_w": (R_GNW, slice(0, 64)),
                  "pool_scale": (R_PSC, slice(0, 256)), "ssd_conv_b": (R_SCB, slice(0, 768)),
                  "ssd_dt_bias": (R_DTB, slice(16, 20)), "ssd_a_log": (R_AE, slice(0, 4)), "ssd_d": (R_DE, slice(0, 4)),
                  "ssd_norm_w": (R_SNW, slice(0, 256))}[name]
    return [((), slice(row, row + 1), lanes)]


def _rider_exchange(block):
    def copies(rins, routs, sems):
        send, recv = sems
        x, y, c = _place()
        flip = lambda v, bit: 1 - v if bit else v
        return [_remote(rins[0], routs[0].at[k], send, recv, k - 1, (flip(x, k & 4), flip(y, k & 2), flip(c, k & 1)))
                for k in range(1, 8)]

    def start(rins, routs, sems):
        for cp in copies(rins, routs, sems):
            cp.start()

    def finish(rins, routs, sems):
        for cp in copies(rins, routs, sems):
            cp.wait()

    return _Rider((block,), [jax.ShapeDtypeStruct((8,) + block.shape, block.dtype)], [_sem(7), _sem(7)], start, finish)


def _join_riders(a, b):
    na, oa, sa = len(a.inputs), len(a.out_shapes), len(a.sems)

    def start(rins, routs, sems):
        a.start(rins[:na], routs[:oa], sems[:sa])
        b.start(rins[na:], routs[oa:], sems[sa:])

    def finish(rins, routs, sems):
        a.finish(rins[:na], routs[:oa], sems[:sa])
        b.finish(rins[na:], routs[oa:], sems[sa:])

    aliases = {**a.aliases, **{na + k: oa + v for k, v in b.aliases.items()}}
    return _Rider(a.inputs + b.inputs, a.out_shapes + b.out_shapes, a.sems + b.sems, start, finish, aliases)


def _small_adamw(blocks, w, m, v):
    n = len(_SMALL_NAMES)

    def body(*refs):
        (own, ex), (own0, ex0) = refs[0:2], refs[2:4]
        refs = refs[3:]
        w_refs, m_refs, v_refs = refs[1:1 + n], refs[1 + n:1 + 2 * n], refs[1 + 2 * n:1 + 3 * n]
        o = 1 + 3 * n
        g_out, d_out, m_out, v_out = refs[o:o + n], refs[o + n:o + 2 * n], refs[o + 2 * n:o + 3 * n], refs[o + 3 * n:o + 4 * n]
        loss_ref, acc, acc0 = refs[o + 4 * n:o + 4 * n + 3]
        chip = 2 * lax.axis_index("x") + lax.axis_index("y")
        me = 2 * chip + lax.axis_index("c")
        acc[...] = jnp.zeros_like(acc)
        acc0[...] = jnp.zeros_like(acc0)
        for src in range(8):
            @pl.when(me == src)
            def _():
                acc[...] += own[...]
                acc0[...] += own0[...]

            @pl.when(me != src)
            def _(src=src):
                acc[...] += ex[jnp.bitwise_xor(me, src)]
                acc0[...] += ex0[jnp.bitwise_xor(me, src)]

        loss_ref[...] = acc[73:74, 0:1]

        def update(i, idx, g):
            d, m2, v2 = _adam_math(w_refs[i][idx], g, m_refs[i][idx], v_refs[i][idx])
            g_out[i][idx], d_out[i][idx], m_out[i][idx], v_out[i][idx] = g, d, m2, v2

        for i, name in enumerate(_SMALL_NAMES):
            if name == "final_norm_w":
                update(i, (slice(0, 1), slice(None)), acc[72:73, :])
            elif name == "norm_w":
                update(i, (slice(0, 1), slice(None)), acc0[0:1, :])
                update(i, (slice(1, 2), slice(None)), acc[64:65, :])
            elif name in ("conv_a_w", "ssd_conv_w"):
                for s in range(4):
                    @pl.when(chip == s)
                    def _(i=i, name=name, s=s):
                        for l in range(2):
                            (_, rows, lanes), = _small_slices(name, s)
                            update(i, (l,), acc[rows.start + 32 * l:rows.stop + 32 * l, lanes])
            else:
                for l in range(2):
                    for idx, rows, lanes in _small_slices(name, 0):
                        g = acc[rows.start + 32 * l:rows.stop + 32 * l, lanes]
                        if w_refs[i].ndim == 2:
                            update(i, (slice(l, l + 1), slice(None)), g)
                        else:
                            update(i, (l,) + idx, g)

    args = [a for pair in blocks for a in pair] + [d[k] for d in (w, m, v) for k in _SMALL_NAMES]
    shapes = [jax.ShapeDtypeStruct(w[k].shape, F32) for k in _SMALL_NAMES]
    vmem = pl.BlockSpec(memory_space=pltpu.VMEM)
    outs = pl.pallas_call(body, name="small_adamw", in_specs=[vmem] * len(args), out_specs=[vmem] * (4 * n + 1),
                          out_shape=shapes * 4 + [jax.ShapeDtypeStruct((1, 1), F32)],
                          scratch_shapes=[pltpu.VMEM((SMALL_ROWS, D), F32), pltpu.VMEM((8, D), F32)])(*args)
    return outs[0:n], outs[n:2 * n], outs[2 * n:3 * n], outs[3 * n:4 * n], outs[4 * n]


def _mixer_consts(layer, conv_a_w, gla_gate_w, gla_gate_b, gla_norm_w, pool_w, pool_scale, ssd_conv_w, ssd_conv_b,
                  ssd_dt_bias, ssd_a_log, ssd_d, ssd_norm_w):
    def row(v):
        return jnp.pad(v.reshape(1, -1), ((0, 0), (0, 768 - v.size)))

    dtb = jnp.pad(ssd_dt_bias[layer], (16, 108))
    rows = [jnp.pad(conv_a_w[layer], ((0, 0), (0, 512))), row(gla_gate_b[layer]), row(jnp.tile(gla_norm_w[layer], 4)),
            row(pool_scale[layer]), row(ssd_conv_b[layer]), row(dtb), row(jnp.repeat(-jnp.exp(ssd_a_log[layer]), 64)),
            row(jnp.repeat(ssd_d[layer], 64)), row(ssd_norm_w[layer]), jnp.zeros((1, 768), F32), ssd_conv_w[layer]]
    prm = jnp.concatenate(rows, axis=0)
    gw = jnp.pad(gla_gate_w[layer], ((0, 112), (0, 0))).astype(BF16)
    on_diag = (_iota((256, 256), 0) >> 6) == (_iota((256, 256), 1) >> 6)
    pw = jnp.where(on_diag, jnp.tile(pool_w[layer].reshape(256, 64), (1, 4)), 0.0)
    return (prm, gw, pw.astype(BF16)) + _mixer_matrices()


def _grad_slabs(dwp, dwo):
    return dwp.reshape(1, D, NP), dwo.reshape(4, D // 4, D)


class _Comm:
    def __init__(self, w_in, w_out):
        self.w_in16 = jnp.pad(w_in.astype(BF16), ((0, 0), (0, 0), (0, SHARD_PAD - SHARD)))
        self.w_out16 = w_out.astype(BF16)
        self.core = lax.axis_index("c").astype(jnp.int32).reshape(1)
        self.chip = 2 * lax.axis_index("x") + lax.axis_index("y")
        self.place = jnp.stack([lax.axis_index("c"), self.chip]).astype(jnp.int32)

    def gather_ici(self, layer):
        return _rider_gather_ici((self.w_in16[layer], self.w_out16[layer]))

    def pair_sum(self, layer, slabs, received):
        d_in, d_out = [_pair_sum(self.core, a, b, name=f"reduce_pair_sum{layer}_{k}")
                       for k, (a, b) in enumerate(zip(slabs, received))]
        return [_split_dw_in(d_in[0], name=f"split_dw_in{layer}"), d_out]

    def chip_sum(self, layer, gathered, mine):
        return [_chip_sum(self.place, a, b, name=f"reduce_chip_sum{layer}_{k}") for k, (a, b) in enumerate(zip(gathered, mine))]

    def layer_weights(self, layer, s_in, s_out):
        own = lambda slabs, shard: jnp.stack([jnp.where(self.chip == s, shard, slabs[s]) for s in range(4)])
        wp, wpt = _assemble_w_in(own(s_in, self.w_in16[layer]), name=f"assemble_w_in{layer}")
        wo = own(s_out, self.w_out16[layer]).reshape(D, D)
        return wp, wpt, wo, wo.T


def _local_step(x, tgt, norm_w, final_norm_w, consts, wts0, wts1=None, comm=None):
    nw = [norm_w[l:l + 1] for l in range(2)]
    proj0, h0, slabs = _rmsproj(x, nw[0], wts0[0], name="rmsproj0", rider=comm and comm.gather_ici(1))
    (mix0, sg0, ss0, x1), slabs = _mixer_fwd(proj0, x, wts0[2], *consts[0], name="mixer_fwd0",
                                             rider=comm and _rider_gather_d2d(slabs))
    if comm:
        wts1 = comm.layer_weights(1, *slabs)
    proj1, h1, _ = _rmsproj(x1, nw[1], wts1[0], name="rmsproj1")
    (mix1, sg1, ss1, dx, head), _ = _mixer_fwd(proj1, x1, wts1[2], *consts[1], name="mixer_fwd1",
                                               head=(tgt, final_norm_w.reshape(1, D)))
    (dproj, mgr1, dwo1), _ = _mixer_bwd(proj1, dx, wts1[3], mix1, sg1, ss1, *consts[1], name="mixer_bwd1")
    dwp1, _ = _dwin(h1, dproj, name="dwin1")
    slabs1 = comm and _grad_slabs(dwp1, dwo1)
    (dx, dnw1), recv = _dxin(dproj, wts1[1], x1, dx, nw[1], name="dxin1", rider=comm and _rider_swap(slabs1))
    pairs1 = comm and comm.pair_sum(1, slabs1, recv)
    (dproj, mgr0, dwo0), gathered = _mixer_bwd(proj0, dx, wts0[3], mix0, sg0, ss0, *consts[0], name="mixer_bwd0",
                                               rider=comm and _rider_scatter(pairs1))
    dwp0, big1 = _dwin(h0, dproj, name="dwin0", rider=comm and _rider_share(comm.chip_sum(1, gathered, pairs1)))
    if not comm:
        (dx, dnw0), _ = _dxin(dproj, wts0[1], x, dx, nw[0], name="dxin0")
        return head, dx, ((dwp0, dwp1), (dwo0, dwo1)), (dnw0, dnw1), (mgr0, mgr1)
    slabs0 = _grad_slabs(dwp0, dwo0)
    pairs0 = comm.pair_sum(0, slabs0, _run_rider(_rider_swap(slabs0), "reduce_swap0"))
    small = jnp.concatenate([mgr0, mgr1, dnw1, head], axis=0)
    (dx, dnw0), gathered = _dxin(dproj, wts0[1], x, dx, nw[0], name="dxin0",
                                 rider=_join_riders(_rider_scatter(pairs0), _rider_exchange(small)))
    last = _run_rider(_join_riders(_rider_share(comm.chip_sum(0, gathered[0:2], pairs0)), _rider_exchange(dnw0)),
                      "reduce_share0")
    return dx, ((last[0], big1[0]), (last[1], big1[1])), ((small, gathered[2]), (dnw0, last[2]))


def kernel(x, norm_w, w_in, conv_a_w, gla_gate_w, gla_gate_b, gla_norm_w, pool_w, pool_scale, ssd_conv_w, ssd_conv_b, ssd_dt_bias, ssd_a_log, ssd_d, ssd_norm_w, w_out, final_norm_w, loss_target, m_norm_w, m_w_in, m_conv_a_w, m_gla_gate_w, m_gla_gate_b, m_gla_norm_w, m_pool_w, m_pool_scale, m_ssd_conv_w, m_ssd_conv_b, m_ssd_dt_bias, m_ssd_a_log, m_ssd_d, m_ssd_norm_w, m_w_out, m_final_norm_w, v_norm_w, v_w_in, v_conv_a_w, v_gla_gate_w, v_gla_gate_b, v_gla_norm_w, v_pool_w, v_pool_scale, v_ssd_conv_w, v_ssd_conv_b, v_ssd_dt_bias, v_ssd_a_log, v_ssd_d, v_ssd_norm_w, v_w_out, v_final_norm_w):
    weights = dict(norm_w=norm_w, w_in=w_in, conv_a_w=conv_a_w, gla_gate_w=gla_gate_w, gla_gate_b=gla_gate_b,
                   gla_norm_w=gla_norm_w, pool_w=pool_w, pool_scale=pool_scale, ssd_conv_w=ssd_conv_w,
                   ssd_conv_b=ssd_conv_b, ssd_dt_bias=ssd_dt_bias, ssd_a_log=ssd_a_log, ssd_d=ssd_d,
                   ssd_norm_w=ssd_norm_w, w_out=w_out, final_norm_w=final_norm_w)
    m_in = dict(norm_w=m_norm_w, w_in=m_w_in, conv_a_w=m_conv_a_w, gla_gate_w=m_gla_gate_w, gla_gate_b=m_gla_gate_b,
                gla_norm_w=m_gla_norm_w, pool_w=m_pool_w, pool_scale=m_pool_scale, ssd_conv_w=m_ssd_conv_w,
                ssd_conv_b=m_ssd_conv_b, ssd_dt_bias=m_ssd_dt_bias, ssd_a_log=m_ssd_a_log, ssd_d=m_ssd_d,
                ssd_norm_w=m_ssd_norm_w, w_out=m_w_out, final_norm_w=m_final_norm_w)
    v_in = dict(norm_w=v_norm_w, w_in=v_w_in, conv_a_w=v_conv_a_w, gla_gate_w=v_gla_gate_w, gla_gate_b=v_gla_gate_b,
                gla_norm_w=v_gla_norm_w, pool_w=v_pool_w, pool_scale=v_pool_scale, ssd_conv_w=v_ssd_conv_w,
                ssd_conv_b=v_ssd_conv_b, ssd_dt_bias=v_ssd_dt_bias, ssd_a_log=v_ssd_a_log, ssd_d=v_ssd_d,
                ssd_norm_w=v_ssd_norm_w, w_out=v_w_out, final_norm_w=v_final_norm_w)
    order = ("norm_w", "w_in", "conv_a_w", "gla_gate_w", "gla_gate_b", "gla_norm_w", "pool_w", "pool_scale",
             "ssd_conv_w", "ssd_conv_b", "ssd_dt_bias", "ssd_a_log", "ssd_d", "ssd_norm_w", "w_out", "final_norm_w")
    t = x.shape[1]

    comm = _Comm(w_in, w_out)
    cshard = jnp.zeros((16, 256), F32)
    for l in range(2):
        cshard = cshard.at[8 * l:8 * l + 3, 0:64].set(conv_a_w[l]).at[8 * l + 3:8 * l + 7, 0:192].set(ssd_conv_w[l])
    s_in, s_out, g_c = _gather_ici_two_hops((comm.w_in16[0], comm.w_out16[0]), cshard)
    s_in, s_out = _run_rider(_rider_gather_d2d((s_in, s_out)), "gather_d2d0")
    g_c = [jnp.where(comm.chip == s, cshard, g_c[s]) for s in range(4)]
    conv_a_full = jnp.stack([jnp.concatenate([g_c[s][8 * l:8 * l + 3, 0:64] for s in range(4)], axis=-1) for l in range(2)])
    ssd_conv_full = jnp.stack([jnp.concatenate([g_c[s][8 * l + 3:8 * l + 7, 0:192] for s in range(4)], axis=-1)
                               for l in range(2)])
    consts = [_mixer_consts(l, conv_a_full, gla_gate_w, gla_gate_b, gla_norm_w, pool_w, pool_scale, ssd_conv_full,
                            ssd_conv_b, ssd_dt_bias, ssd_a_log, ssd_d, ssd_norm_w) for l in range(2)]

    dx, big, blocks = _local_step(x.reshape(t, D), loss_target.reshape(t, D), norm_w, final_norm_w, consts,
                                  comm.layer_weights(0, s_in, s_out), comm=comm)

    as2d = lambda d: {k: (d[k].reshape(1, D) if k == "final_norm_w" else d[k]) for k in _SMALL_NAMES}
    small = _small_adamw(blocks, as2d(weights), as2d(m_in), as2d(v_in))
    grads, delta, new_m, new_v = ({k: (a.reshape(D) if k == "final_norm_w" else a) for k, a in zip(_SMALL_NAMES, part)}
                                  for part in small[0:4])
    loss = small[4].reshape(())

    grads["w_out"] = jnp.stack(big[1])

    grads["w_in"], delta["w_in"], new_m["w_in"], new_v["w_in"] = _adamw_w_in(w_in, big[0], m_w_in, v_w_in, name="adamw_w_in")
    delta["w_out"], new_m["w_out"], new_v["w_out"] = _adamw(w_out, grads["w_out"], m_w_out, v_w_out, name="adamw_w_out", br=256)

    return (loss, dx.reshape(1, t, D), *[grads[k] for k in order], *[delta[k] for k in order],
            *[new_m[k] for k in order], *[new_v[k] for k in order])
```

```python
import functools

import jax
import jax.numpy as jnp
from jax import lax
from jax.experimental import pallas as pl
from jax.experimental.pallas import tpu as pltpu

F32 = jnp.float32
BF16 = jnp.bfloat16
MESH = pl.DeviceIdType.MESH

D = 1024
CH = 64
EPS = 1e-6
NP = 3456
NPROJ = 3348
NPM = 3328
GLA_SCALE = 32.0 ** -0.5
INV_TAU = 1.0 / 16.0
TB = 512
NCH = TB // CH
assert TB % 256 == 0

C_AH, C_AB, C_AC, C_AZ, C_GQ, C_GK, C_GV = 0, 256, 512, 768, 1024, 1152, 1280
C_GZ, C_PU, C_PZ, C_SZ, C_SX, C_TL = 1536, 1792, 2048, 2304, 2560, 3328
_PERM = ((0, 1536), (1552, 1792), (1536, 16), (3344, 4))

R_CAW, R_GB, R_GNW, R_PSC, R_SCB, R_DTB, R_AE, R_DE, R_SNW, R_SCW = 0, 3, 4, 5, 6, 7, 8, 9, 10, 12

ADAM_LR, ADAM_B1, ADAM_B2, ADAM_EPS, ADAM_WD, ADAM_STEP = 0.001, 0.9, 0.999, 1e-08, 0.01, 10

VMEM_LIMIT = 56 * 1024 * 1024


def _cparams(sem, limit=VMEM_LIMIT):
    return pltpu.CompilerParams(dimension_semantics=sem, vmem_limit_bytes=limit)


_ANY = pl.BlockSpec(memory_space=pl.ANY)


def _place():
    return lax.axis_index("x"), lax.axis_index("y"), lax.axis_index("c")


class _Rider:
    def __init__(self, inputs, out_shapes, sems, start, finish, aliases=None):
        self.inputs, self.out_shapes, self.sems = tuple(inputs), tuple(out_shapes), tuple(sems)
        self.start, self.finish, self.aliases = start, finish, dict(aliases or {})


def _call(body, args, *, grid, in_specs, out_specs, out_shape, name, sem, scratch_shapes=(), rider=None):
    if rider is None:
        outs = pl.pallas_call(body, grid=grid, name=name, in_specs=list(in_specs), out_specs=list(out_specs),
                              out_shape=list(out_shape), scratch_shapes=list(scratch_shapes),
                              compiler_params=_cparams(sem))(*args)
        return list(outs), []
    ni, no, ns = len(args), len(out_shape), len(scratch_shapes)
    ri, ro = len(rider.inputs), len(rider.out_shapes)

    def full(*refs):
        ins, rins = refs[:ni], refs[ni:ni + ri]
        outs, routs = refs[ni + ri:ni + ri + no], refs[ni + ri + no:ni + ri + no + ro]
        scr, rsem = refs[ni + ri + no + ro:ni + ri + no + ro + ns], refs[ni + ri + no + ro + ns:]
        first = functools.reduce(jnp.logical_and, [pl.program_id(a) == 0 for a in range(len(grid))])
        last = functools.reduce(jnp.logical_and, [pl.program_id(a) == grid[a] - 1 for a in range(len(grid))])

        @pl.when(first)
        def _():
            rider.start(rins, routs, rsem)

        body(*ins, *outs, *scr)

        @pl.when(last)
        def _():
            rider.finish(rins, routs, rsem)

    outs = pl.pallas_call(
        full, grid=grid, name=name, in_specs=list(in_specs) + [_ANY] * ri, out_specs=list(out_specs) + [_ANY] * ro,
        out_shape=list(out_shape) + list(rider.out_shapes), scratch_shapes=list(scratch_shapes) + list(rider.sems),
        input_output_aliases={ni + k: no + v for k, v in rider.aliases.items()},
        compiler_params=_cparams(("arbitrary",) * len(grid)))(*args, *rider.inputs)
    return list(outs[:no]), list(outs[no:])


def _run_rider(rider, name):
    ri = len(rider.inputs)

    def body(*refs):
        rins, routs, rsem = refs[:ri], refs[ri:ri + len(rider.out_shapes)], refs[ri + len(rider.out_shapes):]
        rider.start(rins, routs, rsem)
        rider.finish(rins, routs, rsem)

    return list(pl.pallas_call(body, name=name, in_specs=[_ANY] * ri, out_specs=[_ANY] * len(rider.out_shapes),
                               out_shape=list(rider.out_shapes), scratch_shapes=list(rider.sems),
                               input_output_aliases=dict(rider.aliases))(*rider.inputs))


def _dot(a, b):
    return jnp.dot(a.astype(BF16), b.astype(BF16), preferred_element_type=F32)


def _dot_nt(a, b):
    return lax.dot_general(a.astype(BF16), b.astype(BF16), (((1,), (1,)), ((), ())), preferred_element_type=F32)


def _dot_tn(a, b):
    return lax.dot_general(a.astype(BF16), b.astype(BF16), (((0,), (0,)), ((), ())), preferred_element_type=F32)


def _split(a):
    hi = a.astype(BF16)
    lo = (a - hi.astype(F32)).astype(BF16)
    return hi, lo


def _dot2_l(a, b):
    hi, lo = _split(a)
    return _dot(hi, b) + _dot(lo, b)


def _dot2_r(a, b):
    hi, lo = _split(b)
    return _dot(a, hi) + _dot(a, lo)


def _dot3_l(a, b):
    hi, lo = _split(a)
    lo2 = ((a - hi.astype(F32)) - lo.astype(F32)).astype(BF16)
    return _dot(hi, b) + _dot(lo, b) + _dot(lo2, b)


def _dot2_nt(a, b):
    hi, lo = _split(a)
    return _dot_nt(hi, b) + _dot_nt(lo, b)


def _silu(z):
    return z * jax.nn.sigmoid(z)


def _lse1(x):
    return jnp.log(1.0 + jnp.exp(-jnp.abs(x)))


def _cs(a):
    return jnp.sum(a, axis=0, keepdims=True)


def _iota(shape, dim):
    return lax.broadcasted_iota(jnp.int32, shape, dim)


def _mixer_matrices():
    r, c = _iota((256, 256), 0), _iota((256, 256), 1)
    same_chunk = (r >> 6) == (c >> 6)
    mats = jnp.stack([jnp.where((c > r) & same_chunk, 1.0, 0.0), jnp.where((c < r) & same_chunk, 1.0, 0.0),
                      jnp.where(same_chunk, 1.0 / 64.0, 0.0), jnp.where((r < 128) & (r - 16 == (c >> 6)), 1.0, 0.0)])
    mask = jnp.where((_iota((256, 128), 0) >> 6) == (_iota((256, 128), 1) >> 5), 1.0, 0.0)
    return mats.astype(BF16), mask.astype(F32)


def _dn(ext, k, n, h):
    return pltpu.roll(ext, k, axis=0)[h:h + n]


def _up(ext, k, n):
    return pltpu.roll(ext, ext.shape[0] - k, axis=0)[:n]


def _pool_lane_select(lane, s2, s4, s8, s16):
    return jnp.where(lane < 64, s2, jnp.where(lane < 128, s4, jnp.where(lane < 192, s8, s16)))


def _winsum_dn(ext, lane):
    s2 = ext + pltpu.roll(ext, 1, axis=0)
    s4 = s2 + pltpu.roll(s2, 2, axis=0)
    s8 = s4 + pltpu.roll(s4, 4, axis=0)
    s16 = s8 + pltpu.roll(s8, 8, axis=0)
    return _pool_lane_select(lane, s2, s4, s8, s16)


def _winsum_up(ext, lane):
    m = ext.shape[0]
    s2 = ext + pltpu.roll(ext, m - 1, axis=0)
    s4 = s2 + pltpu.roll(s2, m - 2, axis=0)
    s8 = s4 + pltpu.roll(s4, m - 4, axis=0)
    s16 = s8 + pltpu.roll(s8, m - 8, axis=0)
    return _pool_lane_select(lane, s2, s4, s8, s16)


def _pool_inv_count(tile, n):
    lane = _iota((1, 256), 1)
    win = _pool_lane_select(lane, 2.0, 4.0, 8.0, 16.0).astype(F32)
    tpos = (tile * n + _iota((n, 1), 0) + 1).astype(F32)
    return jnp.where(tpos >= win, 1.0 / win, 1.0 / tpos)


def _silu_pair(z):
    s = jax.nn.sigmoid(z)
    return z * s, s * (1.0 + z * (1.0 - s))


def _chunks(a):
    return [a[c * CH:(c + 1) * CH] for c in range(a.shape[0] // CH)]


def _halves(fn, a, b):
    return jnp.concatenate([fn(a[:, 0:128], b[:, 0:128]), fn(a[:, 128:256], b[:, 128:256])], axis=1)


def _chunk_sums(tri, a):
    return jnp.concatenate([_dot2_r(tri, a[r:r + 256]) for r in range(0, a.shape[0], 256)], axis=0)


def _mixer_tile_prep(p_ref, t_ref, xc, prm_ref, gw_v, cm_ref, mk_ref):
    tail = t_ref[...]
    pre = _dot(tail, gw_v) + prm_ref[R_GB:R_GB + 1, 0:128]
    la = (jnp.minimum(pre, 0.0) - _lse1(pre)) * INV_TAU
    dtin = tail + prm_ref[R_DTB:R_DTB + 1, 0:128]
    dtf = jnp.maximum(dtin, 0.0) + _lse1(dtin)
    dte = _dot2_l(dtf, cm_ref[3, 0:128, :])
    da = dte * prm_ref[R_AE:R_AE + 1, 0:256]
    rev = _chunk_sums(cm_ref[0], jnp.concatenate([la, da], axis=1))
    dec = jnp.exp(rev[:, 0:128])
    kd = p_ref[:, C_GK:C_GK + 128].astype(F32) * dec
    wdec = jnp.exp(rev[:, 128:384])
    w = wdec * dte
    xw = xc[:, 0:256] * w
    d_s = [jnp.exp(_cs(a)) for a in _chunks(la)]
    et = [jnp.exp(_cs(a)) for a in _chunks(da)]
    mask_t = mk_ref[...]
    ut_g = [_dot_tn(v, k) * mask_t for v, k in zip(_chunks(p_ref[:, C_GV:C_GV + 256].astype(F32)), _chunks(kd))]
    ut_s = [_halves(_dot_tn, b, x) for b, x in zip(_chunks(xc[:, 256:512]), _chunks(xw))]
    return tail, pre, dtin, dte, dec, kd, wdec, w, xw, d_s, et, ut_g, ut_s


def _rmsproj(x, nw, wp, name, tm=1024, rider=None):
    t = x.shape[0]

    def body(x_ref, nw_ref, w_ref, o_ref, t_ref, h_ref):
        xv = x_ref[...]
        rs = lax.rsqrt(jnp.mean(xv * xv, axis=-1, keepdims=True) + EPS)
        h = (xv * rs * nw_ref[...]).astype(BF16)
        h_ref[...] = h
        proj = jnp.dot(h, w_ref[...], preferred_element_type=F32)
        o_ref[...] = proj[:, 0:NPM].astype(BF16)
        t_ref[...] = proj[:, NPM:NP]

    (proj, tail, h), extra = _call(
        body, (x, nw, wp), grid=(t // tm,), name=name, sem=("parallel",), rider=rider,
        in_specs=[pl.BlockSpec((tm, D), lambda i: (i, 0)), pl.BlockSpec((1, D), lambda i: (0, 0)),
                  pl.BlockSpec((D, NP), lambda i: (0, 0))],
        out_specs=[pl.BlockSpec((tm, NPM), lambda i: (i, 0)), pl.BlockSpec((tm, NP - NPM), lambda i: (i, 0)),
                   pl.BlockSpec((tm, D), lambda i: (i, 0))],
        out_shape=[jax.ShapeDtypeStruct((t, NPM), BF16), jax.ShapeDtypeStruct((t, NP - NPM), F32),
                   jax.ShapeDtypeStruct((t, D), BF16)])
    return (proj, tail), h, extra


def _head_tile(xv, tgt, w):
    rs = lax.rsqrt(jnp.mean(xv * xv, axis=-1, keepdims=True) + EPS)
    xh = xv * rs
    err = xh * w - tgt
    dy = err * (1.0 / D)
    dxh = dy * w
    dx = rs * (dxh - xh * jnp.mean(dxh * xh, axis=-1, keepdims=True))
    return dx, _cs(dy * xh), (0.5 / D) * jnp.sum(err * err)


def _dxin(dp, wpt, x, dxn, nw, name, tm=1024, rider=None):
    t = x.shape[0]

    def body(dp_ref, w_ref, x_ref, dxn_ref, nw_ref, dx_ref, dnw_ref):
        @pl.when(pl.program_id(0) == 0)
        def _():
            dnw_ref[...] = jnp.zeros_like(dnw_ref)

        dh = jnp.dot(dp_ref[...], w_ref[...], preferred_element_type=F32)
        xv = x_ref[...]
        rs = lax.rsqrt(jnp.mean(xv * xv, axis=-1, keepdims=True) + EPS)
        xh = xv * rs
        dnw_ref[0:1, :] += _cs(dh * xh)
        dxh = dh * nw_ref[...]
        dx_ref[...] = dxn_ref[...] + rs * (dxh - xh * jnp.mean(dxh * xh, axis=-1, keepdims=True))

    return _call(
        body, (dp, wpt, x, dxn, nw), grid=(t // tm,), name=name, sem=("arbitrary",), rider=rider,
        in_specs=[pl.BlockSpec((tm, NP), lambda i: (i, 0)), pl.BlockSpec((NP, D), lambda i: (0, 0)),
                  pl.BlockSpec((tm, D), lambda i: (i, 0)), pl.BlockSpec((tm, D), lambda i: (i, 0)),
                  pl.BlockSpec((1, D), lambda i: (0, 0))],
        out_specs=[pl.BlockSpec((tm, D), lambda i: (i, 0)), pl.BlockSpec((8, D), lambda i: (0, 0))],
        out_shape=[jax.ShapeDtypeStruct((t, D), F32), jax.ShapeDtypeStruct((8, D), F32)])


def _dwin(h, dp, name, tm=1024, rider=None):
    t = h.shape[0]

    def body(h_ref, dp_ref, o_ref):
        @pl.when(pl.program_id(0) == 0)
        def _():
            o_ref[...] = jnp.zeros_like(o_ref)

        o_ref[...] += _dot_tn(h_ref[...], dp_ref[...])

    (dwp,), extra = _call(
        body, (h, dp), grid=(t // tm,), name=name, sem=("arbitrary",), rider=rider,
        in_specs=[pl.BlockSpec((tm, D), lambda i: (i, 0)), pl.BlockSpec((tm, NP), lambda i: (i, 0))],
        out_specs=[pl.BlockSpec((D, NP), lambda i: (0, 0))], out_shape=[jax.ShapeDtypeStruct((D, NP), F32)])
    return dwp, extra


def _mixer_fwd(proj, x, wo, prm, gw, pw, cmat, mask, name, rider=None, head=None):
    proj, tail = proj
    t = proj.shape[0]
    nt, nc = t // TB, t // CH

    def body(p_ref, t_ref, x_ref, wo_ref, prm_ref, gw_ref, pw_ref, cm_ref, mk_ref, *rest):
        (tgt_ref, fw_ref), rest = (rest[:2], rest[2:]) if head else ((None, None), rest)
        mix_ref, sg_ref, ss_ref, xn_ref = rest[:4]
        acc_ref = rest[4] if head else None
        sg_s, ss_s, h_ua, h_pu, h_sx = rest[-5:]
        i = pl.program_id(0)

        @pl.when(i == 0)
        def _():
            for r in (sg_s, ss_s, h_ua, h_pu, h_sx) + ((acc_ref,) if head else ()):
                r[...] = jnp.zeros_like(r)

        lane = _iota((1, 256), 1)
        u = p_ref[:, C_AC:C_AC + 256].astype(F32) * p_ref[:, C_AH:C_AH + 256].astype(F32)
        ext = jnp.concatenate([h_ua[...], u], axis=0)
        cv = (prm_ref[R_CAW + 2:R_CAW + 3, 0:256] * u + prm_ref[R_CAW + 1:R_CAW + 2, 0:256] * _dn(ext, 1, TB, 8)
              + prm_ref[R_CAW:R_CAW + 1, 0:256] * _dn(ext, 2, TB, 8))
        mix_ref[:, 0:256] = (p_ref[:, C_AB:C_AB + 256].astype(F32) * cv * _silu(p_ref[:, C_AZ:C_AZ + 256].astype(F32))).astype(BF16)
        h_ua[...] = u[TB - 8:, :]
        pu = p_ref[:, C_PU:C_PU + 256].astype(F32)
        ext = jnp.concatenate([h_pu[...], pu], axis=0)
        pooled = _winsum_dn(ext, lane)[16:] * _pool_inv_count(i, TB) - pu
        mixed = _dot(pooled, pw_ref[...])
        mix_ref[:, 512:768] = (prm_ref[R_PSC:R_PSC + 1, 0:256] * mixed * _silu(p_ref[:, C_PZ:C_PZ + 256].astype(F32))).astype(BF16)
        h_pu[...] = pu[TB - 16:, :]
        sx = p_ref[:, C_SX:C_SX + 768].astype(F32)
        ext = jnp.concatenate([h_sx[...], sx], axis=0)
        xc = _silu(prm_ref[R_SCW + 3:R_SCW + 4, :] * sx + prm_ref[R_SCW + 2:R_SCW + 3, :] * _dn(ext, 1, TB, 8)
                   + prm_ref[R_SCW + 1:R_SCW + 2, :] * _dn(ext, 2, TB, 8) + prm_ref[R_SCW:R_SCW + 1, :] * _dn(ext, 3, TB, 8)
                   + prm_ref[R_SCB:R_SCB + 1, :])
        h_sx[...] = sx[TB - 8:, :]

        _, _, _, _, _, _, _, _, _, d_s, et, ut_g, ut_s = _mixer_tile_prep(p_ref, t_ref, xc, prm_ref, gw_ref[...], cm_ref, mk_ref)
        s_g, s_s = sg_s[...], ss_s[...]
        o, y = [], []
        qs = _chunks(p_ref[:, C_GQ:C_GQ + 128].astype(F32) * GLA_SCALE)
        cm = _chunks(xc[:, 512:768])
        for c in range(NCH):
            sg_ref[c] = s_g
            ss_ref[c] = s_s
            s_g = s_g * d_s[c] + ut_g[c]
            s_s = s_s * et[c] + ut_s[c]
            o.append(_dot_nt(qs[c], s_g))
            y.append(_halves(_dot, cm[c], s_s))
        sg_s[...] = s_g
        ss_s[...] = s_s
        o = jnp.concatenate(o, axis=0)
        on = o * lax.rsqrt(_dot2_l(o * o, cm_ref[2]) + EPS)
        mix_ref[:, 256:512] = (on * prm_ref[R_GNW:R_GNW + 1, 0:256] * _silu(p_ref[:, C_GZ:C_GZ + 256].astype(F32))).astype(BF16)
        y2 = ((jnp.concatenate(y, axis=0) + prm_ref[R_DE:R_DE + 1, 0:256] * xc[:, 0:256])
              * _silu(p_ref[:, C_SZ:C_SZ + 256].astype(F32)))
        mix_ref[:, 768:1024] = (y2 * lax.rsqrt(jnp.mean(y2 * y2, axis=-1, keepdims=True) + EPS)
                                * prm_ref[R_SNW:R_SNW + 1, 0:256]).astype(BF16)
        xn = x_ref[...] + jnp.dot(mix_ref[...], wo_ref[...], preferred_element_type=F32)
        if head:
            xn_ref[...], dfw, loss = _head_tile(xn, tgt_ref[...], fw_ref[...])
            acc_ref[0:1, :] += dfw
            acc_ref[1:2, :] += jnp.zeros((1, D), F32) + loss
        else:
            xn_ref[...] = xn

    row = pl.BlockSpec((TB, D), lambda i: (i, 0))
    return _call(
        body, (proj, tail, x, wo, prm, gw, pw, cmat, mask) + tuple(head or ()), grid=(nt,), name=name, sem=("arbitrary",),
        rider=rider,
        in_specs=[pl.BlockSpec((TB, NPM), lambda i: (i, 0)), pl.BlockSpec((TB, NP - NPM), lambda i: (i, 0)), row,
                  pl.BlockSpec((D, D), lambda i: (0, 0)), pl.BlockSpec((16, 768), lambda i: (0, 0)),
                  pl.BlockSpec((128, 128), lambda i: (0, 0)), pl.BlockSpec((256, 256), lambda i: (0, 0)),
                  pl.BlockSpec((4, 256, 256), lambda i: (0, 0, 0)), pl.BlockSpec((256, 128), lambda i: (0, 0))]
        + ([row, pl.BlockSpec((1, D), lambda i: (0, 0))] if head else []),
        out_specs=[row, pl.BlockSpec((NCH, 256, 128), lambda i: (i, 0, 0)),
                   pl.BlockSpec((NCH, 128, 256), lambda i: (i, 0, 0)), row]
        + ([pl.BlockSpec((8, D), lambda i: (0, 0))] if head else []),
        out_shape=[jax.ShapeDtypeStruct((t, D), BF16), jax.ShapeDtypeStruct((nc, 256, 128), F32),
                   jax.ShapeDtypeStruct((nc, 128, 256), F32), jax.ShapeDtypeStruct((t, D), F32)]
        + ([jax.ShapeDtypeStruct((8, D), F32)] if head else []),
        scratch_shapes=[pltpu.VMEM((256, 128), F32), pltpu.VMEM((128, 256), F32), pltpu.VMEM((8, 256), F32),
                        pltpu.VMEM((16, 256), F32), pltpu.VMEM((8, 768), F32)])


def _mixer_bwd(proj, dxn, wot, mix, sg, ss, prm, gw, pw, cmat, mask, name, rider=None):
    proj, tail = proj
    t = proj.shape[0]
    nt = t // TB
    rev = lambda i: nt - 1 - i

    def body(p_ref, hp_ref, t_ref, dxn_ref, wot_ref, mix_ref, sg_ref, ss_ref, prm_ref, gw_ref, pw_ref, cm_ref, mk_ref,
             dp_ref, sgc_ref, dwo_ref,
             gg_s, gs_s, h_dcv, h_dpl, h_dpre, gsm_ref, dgw_ref, dpw_ref, dm_ref):
        i = pl.program_id(0)
        tile = nt - 1 - i

        @pl.when(i == 0)
        def _():
            for r in (gg_s, gs_s, h_dcv, h_dpl, h_dpre, gsm_ref, dgw_ref, dpw_ref, dwo_ref):
                r[...] = jnp.zeros_like(r)

        dxn = dxn_ref[...].astype(BF16)
        dm_ref[...] = jnp.dot(dxn, wot_ref[...], preferred_element_type=F32)
        dwo_ref[...] += _dot_tn(mix_ref[...], dxn)

        lane = _iota((1, 256), 1)
        first = (tile > 0).astype(F32)
        ah, ac = p_ref[:, C_AH:C_AH + 256].astype(F32), p_ref[:, C_AC:C_AC + 256].astype(F32)
        ab, az = p_ref[:, C_AB:C_AB + 256].astype(F32), p_ref[:, C_AZ:C_AZ + 256].astype(F32)
        w0, w1, w2 = (prm_ref[R_CAW + j:R_CAW + j + 1, 0:256] for j in range(3))
        u = ac * ah
        ext = jnp.concatenate([(hp_ref[:, C_AC:C_AC + 256].astype(F32) * hp_ref[:, C_AH:C_AH + 256].astype(F32))[8:16] * first, u], axis=0)
        u1, u2 = _dn(ext, 1, TB, 8), _dn(ext, 2, TB, 8)
        cv = w2 * u + w1 * u1 + w0 * u2
        g = dm_ref[:, 0:256]
        sz, dsz = _silu_pair(az)
        dp_ref[:, C_AB:C_AB + 256] = (g * cv * sz).astype(BF16)
        dp_ref[:, C_AZ:C_AZ + 256] = (g * ab * cv * dsz).astype(BF16)
        dcv = g * ab * sz
        dext = jnp.concatenate([dcv, h_dcv[...]], axis=0)
        du = w2 * dcv + w1 * _up(dext, 1, TB) + w0 * _up(dext, 2, TB)
        dp_ref[:, C_AC:C_AC + 256] = (du * ah).astype(BF16)
        dp_ref[:, C_AH:C_AH + 256] = (du * ac).astype(BF16)
        gsm_ref[R_CAW:R_CAW + 1, 0:256] += _cs(dcv * u2)
        gsm_ref[R_CAW + 1:R_CAW + 2, 0:256] += _cs(dcv * u1)
        gsm_ref[R_CAW + 2:R_CAW + 3, 0:256] += _cs(dcv * u)
        h_dcv[...] = dcv[0:8, :]
        pu, pz = p_ref[:, C_PU:C_PU + 256].astype(F32), p_ref[:, C_PZ:C_PZ + 256].astype(F32)
        psc = prm_ref[R_PSC:R_PSC + 1, 0:256]
        icnt = _pool_inv_count(tile, TB)
        ext = jnp.concatenate([hp_ref[:, C_PU:C_PU + 256].astype(F32) * first, pu], axis=0)
        pooled = _winsum_dn(ext, lane)[16:] * icnt - pu
        pw_v = pw_ref[...]
        mixed = _dot(pooled, pw_v)
        g = dm_ref[:, 512:768]
        sz, dsz = _silu_pair(pz)
        gsm_ref[R_PSC:R_PSC + 1, 0:256] += _cs(g * mixed * sz)
        dp_ref[:, C_PZ:C_PZ + 256] = (g * psc * mixed * dsz).astype(BF16)
        dmixed = g * psc * sz
        dpw_ref[...] += _dot_tn(pooled, dmixed)
        dpooled = _dot_nt(dmixed, pw_v)
        qd = dpooled * icnt
        dext = jnp.concatenate([qd, h_dpl[...]], axis=0)
        dp_ref[:, C_PU:C_PU + 256] = (_winsum_up(dext, lane)[:TB] - dpooled).astype(BF16)
        h_dpl[...] = qd[0:16, :]
        sx = p_ref[:, C_SX:C_SX + 768].astype(F32)
        cw = [prm_ref[R_SCW + j:R_SCW + j + 1, :] for j in range(4)]
        ext = jnp.concatenate([hp_ref[:, C_SX:C_SX + 768].astype(F32)[8:16] * first, sx], axis=0)
        sx1, sx2, sx3 = _dn(ext, 1, TB, 8), _dn(ext, 2, TB, 8), _dn(ext, 3, TB, 8)
        cpre = cw[3] * sx + cw[2] * sx1 + cw[1] * sx2 + cw[0] * sx3 + prm_ref[R_SCB:R_SCB + 1, :]
        xc, dxc = _silu_pair(cpre)
        xs, bm, cm = xc[:, 0:256], xc[:, 256:512], xc[:, 512:768]

        gw_v = gw_ref[...]
        tail, pre, dtin, dte, dec, kd, wdec, w, xw, d_s, et, ut_g, ut_s = _mixer_tile_prep(p_ref, t_ref, xc, prm_ref,
                                                                                          gw_v, cm_ref, mk_ref)
        gmean = cm_ref[2]
        mask_t = mk_ref[...]
        gnw = prm_ref[R_GNW:R_GNW + 1, 0:256]
        a_e = prm_ref[R_AE:R_AE + 1, 0:256]
        d_e = prm_ref[R_DE:R_DE + 1, 0:256]
        snw = prm_ref[R_SNW:R_SNW + 1, 0:256]
        sg_in = [sg_ref[c] for c in range(NCH)]
        ss_in = [ss_ref[c] for c in range(NCH)]
        sg_n = [sg_in[c] * d_s[c] + ut_g[c] for c in range(NCH)]
        ss_n = [ss_in[c] * et[c] + ut_s[c] for c in range(NCH)]
        qs = _chunks(p_ref[:, C_GQ:C_GQ + 128].astype(F32) * GLA_SCALE)
        cm_c, bm_c, xw_c, kd_c = _chunks(cm), _chunks(bm), _chunks(xw), _chunks(kd)
        v_c = _chunks(p_ref[:, C_GV:C_GV + 256].astype(F32))
        o = jnp.concatenate([_dot_nt(qs[c], sg_n[c]) for c in range(NCH)], axis=0)
        y = jnp.concatenate([_halves(_dot, cm_c[c], ss_n[c]) for c in range(NCH)], axis=0) + d_e * xs
        gz = p_ref[:, C_GZ:C_GZ + 256].astype(F32)
        r = lax.rsqrt(_dot2_l(o * o, gmean) + EPS)
        on = o * r
        dyb = dm_ref[:, 256:512]
        sz, dsz = _silu_pair(gz)
        dp_ref[:, C_GZ:C_GZ + 256] = (dyb * on * gnw * dsz).astype(BF16)
        tg = dyb * sz
        gsm_ref[R_GNW:R_GNW + 1, 0:256] += _cs(tg * on)
        don = tg * gnw
        do_c = _chunks(r * (don - on * _dot2_l(don * on, gmean)))
        ssz = p_ref[:, C_SZ:C_SZ + 256].astype(F32)
        sil, dsil = _silu_pair(ssz)
        y2 = y * sil
        r = lax.rsqrt(jnp.mean(y2 * y2, axis=-1, keepdims=True) + EPS)
        yn = y2 * r
        dyd = dm_ref[:, 768:1024]
        gsm_ref[R_SNW:R_SNW + 1, 0:256] += _cs(dyd * yn)
        dn = dyd * snw
        dy2 = r * (dn - yn * jnp.mean(dn * yn, axis=-1, keepdims=True))
        dp_ref[:, C_SZ:C_SZ + 256] = (dy2 * y * dsil).astype(BF16)
        dy = dy2 * sil
        gsm_ref[R_DE:R_DE + 1, 0:256] += _cs(dy * xs)
        dy_c = _chunks(dy)
        dq = jnp.concatenate([_dot(do_c[c], sg_n[c]) for c in range(NCH)], axis=0)
        dp_ref[:, C_GQ:C_GQ + 128] = (dq * GLA_SCALE).astype(BF16)
        dcm = jnp.concatenate([_halves(_dot_nt, dy_c[c], ss_n[c]) for c in range(NCH)], axis=0)
        gg = [_dot_tn(do_c[c], qs[c]) * mask_t for c in range(NCH)]
        gs = [_halves(_dot_tn, cm_c[c], dy_c[c]) for c in range(NCH)]
        car_g, car_s = gg_s[...], gs_s[...]
        for c in reversed(range(NCH)):
            gg[c] = gg[c] + car_g
            gs[c] = gs[c] + car_s
            car_g = gg[c] * d_s[c]
            car_s = gs[c] * et[c]
        gg_s[...] = car_g
        gs_s[...] = car_s
        dkd = jnp.concatenate([_dot(v_c[c], gg[c]) for c in range(NCH)], axis=0)
        dp_ref[:, C_GV:C_GV + 256] = jnp.concatenate([_dot_nt(kd_c[c], gg[c]) for c in range(NCH)], axis=0).astype(BF16)
        dp_ref[:, C_GK:C_GK + 128] = (dkd * dec).astype(BF16)
        dbm = jnp.concatenate([_halves(_dot_nt, xw_c[c], gs[c]) for c in range(NCH)], axis=0)
        dxw = jnp.concatenate([_halves(_dot, bm_c[c], gs[c]) for c in range(NCH)], axis=0)
        dxs = dy * d_e + dxw * w
        dw = dxw * xs
        dsuf = _chunk_sums(cm_ref[1], jnp.concatenate([dkd * kd, dw * dte * wdec], axis=1))
        tot_g = jnp.concatenate([jnp.broadcast_to(_cs(gg[c] * sg_in[c]) * d_s[c], (CH, 128)) for c in range(NCH)], axis=0)
        tot_s = jnp.concatenate([jnp.broadcast_to(_cs(gs[c] * ss_in[c]) * et[c], (CH, 256)) for c in range(NCH)], axis=0)
        dpre = (dsuf[:, 0:128] + tot_g) * INV_TAU * jax.nn.sigmoid(-pre)
        dgw_ref[...] += _dot_tn(tail, dpre)
        gsm_ref[R_GB:R_GB + 1, 0:128] += _cs(dpre)
        dda = dsuf[:, 128:384] + tot_s
        gsm_ref[R_AE:R_AE + 1, 0:256] += _cs(dda * dte)
        dtail_s = _dot2_nt(dw * wdec + dda * a_e, cm_ref[3, 0:128, :]) * jax.nn.sigmoid(dtin)
        gsm_ref[R_DTB:R_DTB + 1, 0:128] += _cs(dtail_s)
        dp_ref[:, C_TL:C_TL + 128] = (_dot_nt(dpre, gw_v) + dtail_s).astype(BF16)
        dpre_c = jnp.concatenate([dxs, dbm, dcm], axis=1) * dxc
        dext = jnp.concatenate([dpre_c, h_dpre[...]], axis=0)
        dp_ref[:, C_SX:C_SX + 768] = (cw[3] * dpre_c + cw[2] * _up(dext, 1, TB) + cw[1] * _up(dext, 2, TB)
                                      + cw[0] * _up(dext, 3, TB)).astype(BF16)
        gsm_ref[R_SCW + 3:R_SCW + 4, :] += _cs(dpre_c * sx)
        gsm_ref[R_SCW + 2:R_SCW + 3, :] += _cs(dpre_c * sx1)
        gsm_ref[R_SCW + 1:R_SCW + 2, :] += _cs(dpre_c * sx2)
        gsm_ref[R_SCW:R_SCW + 1, :] += _cs(dpre_c * sx3)
        gsm_ref[R_SCB:R_SCB + 1, :] += _cs(dpre_c)
        h_dpre[...] = dpre_c[0:8, :]

        @pl.when(i == nt - 1)
        def _():
            ri, ci = _iota((256, 256), 0), _iota((256, 256), 1)
            per_head = jnp.where((ri >> 6) == ci, 1.0, 0.0).astype(BF16)
            per_dv = jnp.where((ri & 63) == ci, 1.0, 0.0).astype(BF16)
            row = _iota((8, 256), 0)
            top = gsm_ref[0:8, 0:256]
            sgc_ref[0:8, 0:256] = jnp.where(row == R_GNW, _dot3_l(top, per_dv), top)
            bot = gsm_ref[8:16, 0:256]
            fold = _dot3_l(jnp.where(row == R_AE - 8, bot * a_e, bot), per_head)
            sgc_ref[8:16, 0:256] = jnp.where((row == R_AE - 8) | (row == R_DE - 8), fold, bot)
            sgc_ref[0:16, 256:768] = gsm_ref[:, 256:768]
            sgc_ref[0:16, 768:896] = dgw_ref[0:16, :]
            sgc_ref[0:16, 896:1024] = jnp.zeros((16, 128), F32)
            diag = _pool_lane_select(lane, dpw_ref[0:64, :], dpw_ref[64:128, :], dpw_ref[128:192, :], dpw_ref[192:256, :])
            for q in range(4):
                sgc_ref[16:32, 256 * q:256 * q + 256] = diag[16 * q:16 * q + 16, :]

    return _call(
        body, (proj, proj, tail, dxn, wot, mix, sg, ss, prm, gw, pw, cmat, mask), grid=(nt,), name=name,
        sem=("arbitrary",), rider=rider,
        in_specs=[pl.BlockSpec((TB, NPM), lambda i: (rev(i), 0)),
                  pl.BlockSpec((16, NPM), lambda i: (jnp.maximum(rev(i) * (TB // 16) - 1, 0), 0)),
                  pl.BlockSpec((TB, NP - NPM), lambda i: (rev(i), 0)),
                  pl.BlockSpec((TB, D), lambda i: (rev(i), 0)), pl.BlockSpec((D, D), lambda i: (0, 0)),
                  pl.BlockSpec((TB, D), lambda i: (rev(i), 0)),
                  pl.BlockSpec((NCH, 256, 128), lambda i: (rev(i), 0, 0)),
                  pl.BlockSpec((NCH, 128, 256), lambda i: (rev(i), 0, 0)),
                  pl.BlockSpec((16, 768), lambda i: (0, 0)), pl.BlockSpec((128, 128), lambda i: (0, 0)),
                  pl.BlockSpec((256, 256), lambda i: (0, 0)), pl.BlockSpec((4, 256, 256), lambda i: (0, 0, 0)),
                  pl.BlockSpec((256, 128), lambda i: (0, 0))],
        out_specs=[pl.BlockSpec((TB, NP), lambda i: (rev(i), 0)), pl.BlockSpec((32, 1024), lambda i: (0, 0)),
                   pl.BlockSpec((D, D), lambda i: (0, 0))],
        out_shape=[jax.ShapeDtypeStruct((t, NP), BF16), jax.ShapeDtypeStruct((32, 1024), F32),
                   jax.ShapeDtypeStruct((D, D), F32)],
        scratch_shapes=[pltpu.VMEM((256, 128), F32), pltpu.VMEM((128, 256), F32), pltpu.VMEM((8, 256), F32),
                        pltpu.VMEM((16, 256), F32), pltpu.VMEM((8, 768), F32), pltpu.VMEM((16, 768), F32),
                        pltpu.VMEM((128, 128), F32), pltpu.VMEM((256, 256), F32), pltpu.VMEM((TB, D), F32)])


SHARD = NPROJ // 4
SHARD_PAD = 896


def _ranges_to_perm(o, n):
    out, p = [], 0
    for start, size in _PERM:
        a, b = max(o, start), min(o + n, start + size)
        if a < b:
            out.append((a, b - a, p + a - start))
        p += size
    return out


def _ranges_to_orig(p0, n):
    out, p = [], 0
    for start, size in _PERM:
        a, b = max(p0, p), min(p0 + n, p + size)
        if a < b:
            out.append((a, b - a, start + a - p))
        p += size
    return out


def _lane_window(load, lo, n, d, lane):
    a = 128 * (lo // 128)
    off = lo - a
    w = 128 if off + n <= 128 else 256
    chunk = load(a, w)
    shift = (d - off) % w
    if shift:
        chunk = pltpu.roll(chunk, shift, axis=1)
    return jnp.where((lane >= d) & (lane < d + n), chunk[:, 0:128], 0.0)


def _assemble_w_in(slabs, name, rb=256):
    def body(s_ref, wp_ref, wpt_ref):
        lane = _iota((1, 128), 1)
        for b in range(NP // 128):
            acc = jnp.zeros((rb, 128), F32)
            for p, n, o in _ranges_to_orig(128 * b, 128):
                while n > 0:
                    s, lo = o // SHARD, o % SHARD
                    cnt = min(n, SHARD - lo)
                    acc = acc + _lane_window(lambda a, w, s=s: s_ref[s, :, a:a + w].astype(F32), lo, cnt, p - 128 * b, lane)
                    o, p, n = o + cnt, p + cnt, n - cnt
            wp_ref[:, 128 * b:128 * b + 128] = acc.astype(BF16)
            wpt_ref[128 * b:128 * b + 128, :] = acc.T.astype(BF16)

    return pl.pallas_call(
        body, grid=(D // rb,), name=name,
        in_specs=[pl.BlockSpec((4, rb, SHARD_PAD), lambda i: (0, i, 0))],
        out_specs=[pl.BlockSpec((rb, NP), lambda i: (i, 0)), pl.BlockSpec((NP, rb), lambda i: (0, i))],
        out_shape=[jax.ShapeDtypeStruct((D, NP), BF16), jax.ShapeDtypeStruct((NP, D), BF16)],
        compiler_params=_cparams(("parallel",)))(slabs)


def _split_dw_in(dwp, name, rb=256):
    rows = dwp.shape[0]

    def body(g_ref, o_ref):
        lane = _iota((1, 128), 1)
        for s in range(4):
            for k in range(SHARD_PAD // 128):
                acc = jnp.zeros((rb, 128), F32)
                n_valid = min(128, SHARD - 128 * k)
                for o, n, p in _ranges_to_perm(SHARD * s + 128 * k, n_valid):
                    acc = acc + _lane_window(lambda a, w: g_ref[:, a:a + w].astype(F32), p, n, o - SHARD * s - 128 * k, lane)
                o_ref[s, :, 128 * k:128 * k + 128] = acc.astype(o_ref.dtype)

    return pl.pallas_call(
        body, grid=(rows // rb,), name=name,
        in_specs=[pl.BlockSpec((rb, NP), lambda i: (i, 0))],
        out_specs=pl.BlockSpec((4, rb, SHARD_PAD), lambda i: (0, i, 0)),
        out_shape=jax.ShapeDtypeStruct((4, rows, SHARD_PAD), dwp.dtype),
        compiler_params=_cparams(("parallel",)))(dwp)


def _half(c, n):
    return pl.ds(pl.multiple_of(c * (n // 2), n // 2), n // 2)


def _other_chips(x, y):
    return ((1 - x, y), (x, 1 - y), (1 - x, 1 - y))


def _remote(src, dst, send, recv, k, dev):
    return pltpu.make_async_remote_copy(src_ref=src, dst_ref=dst, send_sem=send.at[k], recv_sem=recv.at[k], device_id=dev,
                                        device_id_type=MESH)


def _sem(n):
    return pltpu.SemaphoreType.DMA((n,))


def _rider_gather_ici(shards):
    shards = tuple(shards)
    n = len(shards)

    def copies(rins, routs, sems, arrivals=True):
        send, recv = sems
        x, y, c = _place()
        me = 2 * x + y
        out, inc = [], []
        for j, (px, py) in enumerate(_other_chips(x, y)):
            for k in range(n):
                rows = _half(c, shards[k].shape[0])
                out.append(_remote(rins[k].at[rows], routs[k].at[me, rows], send, recv, n * j + k, (px, py, c)))
                if arrivals:
                    inc.append(_remote(rins[k].at[rows], routs[k].at[2 * px + py, rows], send, recv, n * j + k, (px, py, c)))
        return out, inc

    def start(rins, routs, sems):
        for cp in copies(rins, routs, sems, arrivals=False)[0]:
            cp.start()

    def finish(rins, routs, sems):
        out, inc = copies(rins, routs, sems)
        for cp in inc:
            cp.wait_recv()
        for cp in out:
            cp.wait_send()

    return _Rider(shards, [jax.ShapeDtypeStruct((4,) + a.shape, a.dtype) for a in shards], [_sem(3 * n), _sem(3 * n)],
                  start, finish)


def _gather_ici_two_hops(shards, extra):
    shards = tuple(shards)
    n = len(shards)

    def body(*refs):
        ins, e_in, outs, e_out = refs[:n], refs[n], refs[n + 1:2 * n + 1], refs[2 * n + 1]
        send, recv = refs[2 * n + 2:]
        x, y, c = _place()
        slab = lambda px, py: 2 * px + py
        xn, yn, dg = (1 - x, y), (x, 1 - y), (1 - x, 1 - y)

        def part(k, q):
            r = shards[k].shape[0] // 4
            return pl.ds(pl.multiple_of(c * 2 * r + q * r, r), r)

        def hop(k, q, src_chip, to, sem):
            rows = part(k, q)
            src = ins[k].at[rows] if src_chip is None else outs[k].at[slab(*src_chip), rows]
            own = (x, y) if src_chip is None else src_chip
            return _remote(src, outs[k].at[slab(*own), rows], send, recv, sem, (*to, c))

        small = [_remote(e_in, e_out.at[slab(x, y)], send, recv, 6 * n + j, (*to, c)) for j, to in enumerate((xn, yn, dg))]
        first = [hop(k, q, None, (xn, yn)[q], 2 * k + q) for k in range(n) for q in (0, 1)]
        for cp in small + first:
            cp.start()
        for k in range(n):
            for q in (0, 1):
                nb = (xn, yn)[q]
                _remote(ins[k].at[part(k, q)], outs[k].at[slab(*nb), part(k, q)], send, recv, 2 * k + q, (*nb, c)).wait_recv()
        second = []
        for k in range(n):
            for q in (0, 1):
                to, via = (yn, xn)[q], (xn, yn)[q]
                second.append(hop(k, q, None, to, 2 * n + 4 * k + 2 * q))
                second.append(hop(k, q, via, to, 2 * n + 4 * k + 2 * q + 1))
        for cp in second:
            cp.start()
        for k in range(n):
            for q in (0, 1):
                frm, rows = (yn, xn)[q], part(k, q)
                for j, origin in enumerate((frm, dg)):
                    _remote(ins[k].at[rows], outs[k].at[slab(*origin), rows], send, recv, 2 * n + 4 * k + 2 * q + j,
                            (*frm, c)).wait_recv()
        for j, frm in enumerate((xn, yn, dg)):
            _remote(e_in, e_out.at[slab(*frm)], send, recv, 6 * n + j, (*frm, c)).wait_recv()
        for cp in small + first + second:
            cp.wait_send()

    outs = pl.pallas_call(
        body, name="gather_ici0", in_specs=[_ANY] * (n + 1), out_specs=[_ANY] * (n + 1),
        out_shape=[jax.ShapeDtypeStruct((4,) + a.shape, a.dtype) for a in shards + (extra,)],
        scratch_shapes=[_sem(6 * n + 3), _sem(6 * n + 3)])(*shards, extra)
    return list(outs)


def _rider_gather_d2d(slabs):
    slabs = tuple(slabs)
    n = len(slabs)

    def copies(routs, sems, arrivals=True):
        send, recv = sems
        x, y, c = _place()
        out, inc = [], []
        for j, (px, py) in enumerate(_other_chips(x, y)):
            for k in range(n):
                rows = slabs[k].shape[1]
                mine, theirs = routs[k].at[2 * px + py, _half(c, rows)], routs[k].at[2 * px + py, _half(1 - c, rows)]
                out.append(_remote(mine, mine, send, recv, n * j + k, (x, y, 1 - c)))
                if arrivals:
                    inc.append(_remote(theirs, theirs, send, recv, n * j + k, (x, y, 1 - c)))
        return out, inc

    def start(rins, routs, sems):
        for cp in copies(routs, sems, arrivals=False)[0]:
            cp.start()

    def finish(rins, routs, sems):
        out, inc = copies(routs, sems)
        for cp in inc:
            cp.wait_recv()
        for cp in out:
            cp.wait_send()

    return _Rider(slabs, [jax.ShapeDtypeStruct(a.shape, a.dtype) for a in slabs], [_sem(3 * n), _sem(3 * n)], start, finish,
                  aliases={k: k for k in range(n)})


def _rider_swap(parts):
    parts = tuple(parts)
    n = len(parts)

    def copies(rins, routs, sems):
        send, recv = sems
        x, y, c = _place()
        return [_remote(rins[k].at[:, _half(1 - c, parts[k].shape[1])], routs[k], send, recv, k, (x, y, 1 - c))
                for k in range(n)]

    def start(rins, routs, sems):
        for cp in copies(rins, routs, sems):
            cp.start()

    def finish(rins, routs, sems):
        for cp in copies(rins, routs, sems):
            cp.wait()

    return _Rider(parts, [jax.ShapeDtypeStruct((a.shape[0], a.shape[1] // 2, a.shape[2]), a.dtype) for a in parts],
                  [_sem(n), _sem(n)], start, finish)


def _rider_scatter(parts):
    parts = tuple(parts)
    n = len(parts)

    def copies(rins, routs, sems, arrivals=True):
        send, recv = sems
        x, y, c = _place()
        me = 2 * x + y
        out, inc = [], []
        for j, (px, py) in enumerate(_other_chips(x, y)):
            for k in range(n):
                out.append(_remote(rins[k].at[2 * px + py], routs[k].at[me], send, recv, n * j + k, (px, py, c)))
                if arrivals:
                    inc.append(_remote(rins[k].at[me], routs[k].at[2 * px + py], send, recv, n * j + k, (px, py, c)))
        return out, inc

    def start(rins, routs, sems):
        for cp in copies(rins, routs, sems, arrivals=False)[0]:
            cp.start()

    def finish(rins, routs, sems):
        out, inc = copies(rins, routs, sems)
        for cp in inc:
            cp.wait_recv()
        for cp in out:
            cp.wait_send()

    return _Rider(parts, [jax.ShapeDtypeStruct(a.shape, a.dtype) for a in parts], [_sem(3 * n), _sem(3 * n)], start, finish)


def _rider_share(fulls):
    fulls = tuple(fulls)
    n = len(fulls)

    def copies(routs, sems, arrivals=True):
        send, recv = sems
        x, y, c = _place()
        out, inc = [], []
        for k in range(n):
            mine, theirs = routs[k].at[_half(c, fulls[k].shape[0])], routs[k].at[_half(1 - c, fulls[k].shape[0])]
            out.append(_remote(mine, mine, send, recv, k, (x, y, 1 - c)))
            if arrivals:
                inc.append(_remote(theirs, theirs, send, recv, k, (x, y, 1 - c)))
        return out, inc

    def start(rins, routs, sems):
        for cp in copies(routs, sems, arrivals=False)[0]:
            cp.start()

    def finish(rins, routs, sems):
        out, inc = copies(routs, sems)
        for cp in inc:
            cp.wait_recv()
        for cp in out:
            cp.wait_send()

    return _Rider(fulls, [jax.ShapeDtypeStruct(a.shape, a.dtype) for a in fulls], [_sem(n), _sem(n)], start, finish,
                  aliases={k: k for k in range(n)})


def _pair_sum(core, full, recv, name, br=128):
    n, rows, cols = recv.shape

    def body(c_ref, a_ref, b_ref, o_ref):
        o_ref[...] = (a_ref[...] + b_ref[...]).astype(BF16)

    nb = rows // br
    return pl.pallas_call(
        body, name=name, out_shape=jax.ShapeDtypeStruct(recv.shape, BF16),
        grid_spec=pltpu.PrefetchScalarGridSpec(
            num_scalar_prefetch=1, grid=(n, nb),
            in_specs=[pl.BlockSpec((1, br, cols), lambda i, j, c: (i, c[0] * nb + j, 0)),
                      pl.BlockSpec((1, br, cols), lambda i, j, c: (i, j, 0))],
            out_specs=pl.BlockSpec((1, br, cols), lambda i, j, c: (i, j, 0))),
        compiler_params=_cparams(("parallel", "parallel")))(core, full, recv)


def _chip_sum(place, gathered, mine, name, br=128):
    _, r, c = gathered.shape
    nb = r // br

    def body(p_ref, g_ref, m_ref, o_ref):
        slab = lambda j: jnp.where(p_ref[1] == j, m_ref[j], g_ref[j]).astype(F32)
        o_ref[...] = ((slab(0) + slab(1)) + slab(2)) + slab(3)

    return pl.pallas_call(
        body, name=name, out_shape=jax.ShapeDtypeStruct((2 * r, c), F32),
        grid_spec=pltpu.PrefetchScalarGridSpec(
            num_scalar_prefetch=1, grid=(nb,),
            in_specs=[pl.BlockSpec((4, br, c), lambda i, p: (0, i, 0)), pl.BlockSpec((4, br, c), lambda i, p: (0, i, 0))],
            out_specs=pl.BlockSpec((br, c), lambda i, p: (p[0] * nb + i, 0))),
        compiler_params=_cparams(("parallel",)))(place, gathered, mine)


def _adamw(w, g, m, v, name, br):
    n, r, c = w.shape

    def body(w_ref, g_ref, m_ref, v_ref, d_ref, m2_ref, v2_ref):
        d_ref[...], m2_ref[...], v2_ref[...] = _adam_math(w_ref[...], g_ref[...], m_ref[...], v_ref[...])

    spec = pl.BlockSpec((1, br, c), lambda i, j: (i, j, 0))
    shp = jax.ShapeDtypeStruct(w.shape, F32)
    return pl.pallas_call(body, grid=(n, r // br), name=name, in_specs=[spec] * 4, out_specs=[spec] * 3,
                          out_shape=[shp] * 3, compiler_params=_cparams(("parallel", "parallel")))(w, g, m, v)


def _adamw_w_in(w, g, m, v, name, bc=93):
    cols = w.shape[2]
    lead = lambda a: jnp.transpose(a, (2, 0, 1))
    g = jnp.stack([a[:, 0:cols] for a in g])

    def body(w_ref, g_ref, m_ref, v_ref, go_ref, d_ref, m2_ref, v2_ref):
        for l in range(2):
            gv = g_ref[:, l, :]
            d_ref[:, l, :], m2_ref[:, l, :], v2_ref[:, l, :] = _adam_math(w_ref[:, l, :], gv, m_ref[:, l, :], v_ref[:, l, :])
            go_ref[:, l, :] = gv

    spec = pl.BlockSpec((bc, 2, D), lambda i: (i, 0, 0))
    outs = pl.pallas_call(body, grid=(cols // bc,), name=name, in_specs=[spec] * 4, out_specs=[spec] * 4,
                          out_shape=[jax.ShapeDtypeStruct((cols, 2, D), F32)] * 4,
                          compiler_params=_cparams(("parallel",)))(lead(w), lead(g), lead(m), lead(v))
    return [jnp.transpose(o, (1, 2, 0)) for o in outs]


_SMALL_NAMES = ("norm_w", "conv_a_w", "gla_gate_w", "gla_gate_b", "gla_norm_w", "pool_w", "pool_scale", "ssd_conv_w",
                "ssd_conv_b", "ssd_dt_bias", "ssd_a_log", "ssd_d", "ssd_norm_w", "final_norm_w")
SMALL_ROWS = 80


def _adam_math(w, g, m, v):
    m2 = ADAM_B1 * m + (1.0 - ADAM_B1) * g
    v2 = ADAM_B2 * v + (1.0 - ADAM_B2) * (g * g)
    m_hat = m2 / (1.0 - ADAM_B1 ** ADAM_STEP)
    v_hat = v2 / (1.0 - ADAM_B2 ** ADAM_STEP)
    return -ADAM_LR * (m_hat / (jnp.sqrt(v_hat) + ADAM_EPS) + ADAM_WD * w), m2, v2


def _small_slices(name, chip):
    if name == "conv_a_w":
        return [((), slice(R_CAW, R_CAW + 3), slice(64 * chip, 64 * chip + 64))]
    if name == "ssd_conv_w":
        return [((), slice(R_SCW, R_SCW + 4), slice(192 * chip, 192 * chip + 192))]
    if name == "gla_gate_w":
        return [((), slice(0, 16), slice(768, 896))]
    if name == "pool_w":
        return [((g, slice(16 * q, 16 * q + 16)), slice(16, 32), slice(256 * q + 64 * g, 256 * q + 64 * g + 64))
                for g in range(4) for q in range(4)]
    row, lanes = {"gla_gate_b": (R_GB, slice(0, 128)), "gla_norm_w": (R_GNW, slice(0, 64)),
                  "pool_scale": (R_PSC, slice(0, 256)), "ssd_conv_b": (R_SCB, slice(0, 768)),
                  "ssd_dt_bias": (R_DTB, slice(16, 20)), "ssd_a_log": (R_AE, slice(0, 4)), "ssd_d": (R_DE, slice(0, 4)),
                  "ssd_norm_w": (R_SNW, slice(0, 256))}[name]
    return [((), slice(row, row + 1), lanes)]


def _rider_exchange(block):
    def copies(rins, routs, sems):
        send, recv = sems
        x, y, c = _place()
        flip = lambda v, bit: 1 - v if bit else v
        return [_remote(rins[0], routs[0].at[k], send, recv, k - 1, (flip(x, k & 4), flip(y, k & 2), flip(c, k & 1)))
                for k in range(1, 8)]

    def start(rins, routs, sems):
        for cp in copies(rins, routs, sems):
            cp.start()

    def finish(rins, routs, sems):
        for cp in copies(rins, routs, sems):
            cp.wait()

    return _Rider((block,), [jax.ShapeDtypeStruct((8,) + block.shape, block.dtype)], [_sem(7), _sem(7)], start, finish)


def _join_riders(a, b):
    na, oa, sa = len(a.inputs), len(a.out_shapes), len(a.sems)

    def start(rins, routs, sems):
        a.start(rins[:na], routs[:oa], sems[:sa])
        b.start(rins[na:], routs[oa:], sems[sa:])

    def finish(rins, routs, sems):
        a.finish(rins[:na], routs[:oa], sems[:sa])
        b.finish(rins[na:], routs[oa:], sems[sa:])

    aliases = {**a.aliases, **{na + k: oa + v for k, v in b.aliases.items()}}
    return _Rider(a.inputs + b.inputs, a.out_shapes + b.out_shapes, a.sems + b.sems, start, finish, aliases)


def _small_adamw(blocks, w, m, v):
    n = len(_SMALL_NAMES)

    def body(*refs):
        (own, ex), (own0, ex0) = refs[0:2], refs[2:4]
        refs = refs[3:]
        w_refs, m_refs, v_refs = refs[1:1 + n], refs[1 + n:1 + 2 * n], refs[1 + 2 * n:1 + 3 * n]
        o = 1 + 3 * n
        g_out, d_out, m_out, v_out = refs[o:o + n], refs[o + n:o + 2 * n], refs[o + 2 * n:o + 3 * n], refs[o + 3 * n:o + 4 * n]
        loss_ref, acc, acc0 = refs[o + 4 * n:o + 4 * n + 3]
        chip = 2 * lax.axis_index("x") + lax.axis_index("y")
        me = 2 * chip + lax.axis_index("c")
        acc[...] = jnp.zeros_like(acc)
        acc0[...] = jnp.zeros_like(acc0)
        for src in range(8):
            @pl.when(me == src)
            def _():
                acc[...] += own[...]
                acc0[...] += own0[...]

            @pl.when(me != src)
            def _(src=src):
                acc[...] += ex[jnp.bitwise_xor(me, src)]
                acc0[...] += ex0[jnp.bitwise_xor(me, src)]

        loss_ref[...] = acc[73:74, 0:1]

        def update(i, idx, g):
            d, m2, v2 = _adam_math(w_refs[i][idx], g, m_refs[i][idx], v_refs[i][idx])
            g_out[i][idx], d_out[i][idx], m_out[i][idx], v_out[i][idx] = g, d, m2, v2

        for i, name in enumerate(_SMALL_NAMES):
            if name == "final_norm_w":
                update(i, (slice(0, 1), slice(None)), acc[72:73, :])
            elif name == "norm_w":
                update(i, (slice(0, 1), slice(None)), acc0[0:1, :])
                update(i, (slice(1, 2), slice(None)), acc[64:65, :])
            elif name in ("conv_a_w", "ssd_conv_w"):
                for s in range(4):
                    @pl.when(chip == s)
                    def _(i=i, name=name, s=s):
                        for l in range(2):
                            (_, rows, lanes), = _small_slices(name, s)
                            update(i, (l,), acc[rows.start + 32 * l:rows.stop + 32 * l, lanes])
            else:
                for l in range(2):
                    for idx, rows, lanes in _small_slices(name, 0):
                        g = acc[rows.start + 32 * l:rows.stop + 32 * l, lanes]
                        if w_refs[i].ndim == 2:
                            update(i, (slice(l, l + 1), slice(None)), g)
                        else:
                            update(i, (l,) + idx, g)

    args = [a for pair in blocks for a in pair] + [d[k] for d in (w, m, v) for k in _SMALL_NAMES]
    shapes = [jax.ShapeDtypeStruct(w[k].shape, F32) for k in _SMALL_NAMES]
    vmem = pl.BlockSpec(memory_space=pltpu.VMEM)
    outs = pl.pallas_call(body, name="small_adamw", in_specs=[vmem] * len(args), out_specs=[vmem] * (4 * n + 1),
                          out_shape=shapes * 4 + [jax.ShapeDtypeStruct((1, 1), F32)],
                          scratch_shapes=[pltpu.VMEM((SMALL_ROWS, D), F32), pltpu.VMEM((8, D), F32)])(*args)
    return outs[0:n], outs[n:2 * n], outs[2 * n:3 * n], outs[3 * n:4 * n], outs[4 * n]


def _mixer_consts(layer, conv_a_w, gla_gate_w, gla_gate_b, gla_norm_w, pool_w, pool_scale, ssd_conv_w, ssd_conv_b,
                  ssd_dt_bias, ssd_a_log, ssd_d, ssd_norm_w):
    def row(v):
        return jnp.pad(v.reshape(1, -1), ((0, 0), (0, 768 - v.size)))

    dtb = jnp.pad(ssd_dt_bias[layer], (16, 108))
    rows = [jnp.pad(conv_a_w[layer], ((0, 0), (0, 512))), row(gla_gate_b[layer]), row(jnp.tile(gla_norm_w[layer], 4)),
            row(pool_scale[layer]), row(ssd_conv_b[layer]), row(dtb), row(jnp.repeat(-jnp.exp(ssd_a_log[layer]), 64)),
            row(jnp.repeat(ssd_d[layer], 64)), row(ssd_norm_w[layer]), jnp.zeros((1, 768), F32), ssd_conv_w[layer]]
    prm = jnp.concatenate(rows, axis=0)
    gw = jnp.pad(gla_gate_w[layer], ((0, 112), (0, 0))).astype(BF16)
    on_diag = (_iota((256, 256), 0) >> 6) == (_iota((256, 256), 1) >> 6)
    pw = jnp.where(on_diag, jnp.tile(pool_w[layer].reshape(256, 64), (1, 4)), 0.0)
    return (prm, gw, pw.astype(BF16)) + _mixer_matrices()


def _grad_slabs(dwp, dwo):
    return dwp.reshape(1, D, NP), dwo.reshape(4, D // 4, D)


class _Comm:
    def __init__(self, w_in, w_out):
        self.w_in16 = jnp.pad(w_in.astype(BF16), ((0, 0), (0, 0), (0, SHARD_PAD - SHARD)))
        self.w_out16 = w_out.astype(BF16)
        self.core = lax.axis_index("c").astype(jnp.int32).reshape(1)
        self.chip = 2 * lax.axis_index("x") + lax.axis_index("y")
        self.place = jnp.stack([lax.axis_index("c"), self.chip]).astype(jnp.int32)

    def gather_ici(self, layer):
        return _rider_gather_ici((self.w_in16[layer], self.w_out16[layer]))

    def pair_sum(self, layer, slabs, received):
        d_in, d_out = [_pair_sum(self.core, a, b, name=f"reduce_pair_sum{layer}_{k}")
                       for k, (a, b) in enumerate(zip(slabs, received))]
        return [_split_dw_in(d_in[0], name=f"split_dw_in{layer}"), d_out]

    def chip_sum(self, layer, gathered, mine):
        return [_chip_sum(self.place, a, b, name=f"reduce_chip_sum{layer}_{k}") for k, (a, b) in enumerate(zip(gathered, mine))]

    def layer_weights(self, layer, s_in, s_out):
        own = lambda slabs, shard: jnp.stack([jnp.where(self.chip == s, shard, slabs[s]) for s in range(4)])
        wp, wpt = _assemble_w_in(own(s_in, self.w_in16[layer]), name=f"assemble_w_in{layer}")
        wo = own(s_out, self.w_out16[layer]).reshape(D, D)
        return wp, wpt, wo, wo.T


def _local_step(x, tgt, norm_w, final_norm_w, consts, wts0, wts1=None, comm=None):
    nw = [norm_w[l:l + 1] for l in range(2)]
    proj0, h0, slabs = _rmsproj(x, nw[0], wts0[0], name="rmsproj0", rider=comm and comm.gather_ici(1))
    (mix0, sg0, ss0, x1), slabs = _mixer_fwd(proj0, x, wts0[2], *consts[0], name="mixer_fwd0",
                                             rider=comm and _rider_gather_d2d(slabs))
    if comm:
        wts1 = comm.layer_weights(1, *slabs)
    proj1, h1, _ = _rmsproj(x1, nw[1], wts1[0], name="rmsproj1")
    (mix1, sg1, ss1, dx, head), _ = _mixer_fwd(proj1, x1, wts1[2], *consts[1], name="mixer_fwd1",
                                               head=(tgt, final_norm_w.reshape(1, D)))
    (dproj, mgr1, dwo1), _ = _mixer_bwd(proj1, dx, wts1[3], mix1, sg1, ss1, *consts[1], name="mixer_bwd1")
    dwp1, _ = _dwin(h1, dproj, name="dwin1")
    slabs1 = comm and _grad_slabs(dwp1, dwo1)
    (dx, dnw1), recv = _dxin(dproj, wts1[1], x1, dx, nw[1], name="dxin1", rider=comm and _rider_swap(slabs1))
    pairs1 = comm and comm.pair_sum(1, slabs1, recv)
    (dproj, mgr0, dwo0), gathered = _mixer_bwd(proj0, dx, wts0[3], mix0, sg0, ss0, *consts[0], name="mixer_bwd0",
                                               rider=comm and _rider_scatter(pairs1))
    dwp0, big1 = _dwin(h0, dproj, name="dwin0", rider=comm and _rider_share(comm.chip_sum(1, gathered, pairs1)))
    if not comm:
        (dx, dnw0), _ = _dxin(dproj, wts0[1], x, dx, nw[0], name="dxin0")
        return head, dx, ((dwp0, dwp1), (dwo0, dwo1)), (dnw0, dnw1), (mgr0, mgr1)
    slabs0 = _grad_slabs(dwp0, dwo0)
    pairs0 = comm.pair_sum(0, slabs0, _run_rider(_rider_swap(slabs0), "reduce_swap0"))
    small = jnp.concatenate([mgr0, mgr1, dnw1, head], axis=0)
    (dx, dnw0), gathered = _dxin(dproj, wts0[1], x, dx, nw[0], name="dxin0",
                                 rider=_join_riders(_rider_scatter(pairs0), _rider_exchange(small)))
    last = _run_rider(_join_riders(_rider_share(comm.chip_sum(0, gathered[0:2], pairs0)), _rider_exchange(dnw0)),
                      "reduce_share0")
    return dx, ((last[0], big1[0]), (last[1], big1[1])), ((small, gathered[2]), (dnw0, last[2]))


def kernel(x, norm_w, w_in, conv_a_w, gla_gate_w, gla_gate_b, gla_norm_w, pool_w, pool_scale, ssd_conv_w, ssd_conv_b, ssd_dt_bias, ssd_a_log, ssd_d, ssd_norm_w, w_out, final_norm_w, loss_target, m_norm_w, m_w_in, m_conv_a_w, m_gla_gate_w, m_gla_gate_b, m_gla_norm_w, m_pool_w, m_pool_scale, m_ssd_conv_w, m_ssd_conv_b, m_ssd_dt_bias, m_ssd_a_log, m_ssd_d, m_ssd_norm_w, m_w_out, m_final_norm_w, v_norm_w, v_w_in, v_conv_a_w, v_gla_gate_w, v_gla_gate_b, v_gla_norm_w, v_pool_w, v_pool_scale, v_ssd_conv_w, v_ssd_conv_b, v_ssd_dt_bias, v_ssd_a_log, v_ssd_d, v_ssd_norm_w, v_w_out, v_final_norm_w):
    weights = dict(norm_w=norm_w, w_in=w_in, conv_a_w=conv_a_w, gla_gate_w=gla_gate_w, gla_gate_b=gla_gate_b,
                   gla_norm_w=gla_norm_w, pool_w=pool_w, pool_scale=pool_scale, ssd_conv_w=ssd_conv_w,
                   ssd_conv_b=ssd_conv_b, ssd_dt_bias=ssd_dt_bias, ssd_a_log=ssd_a_log, ssd_d=ssd_d,
                   ssd_norm_w=ssd_norm_w, w_out=w_out, final_norm_w=final_norm_w)
    m_in = dict(norm_w=m_norm_w, w_in=m_w_in, conv_a_w=m_conv_a_w, gla_gate_w=m_gla_gate_w, gla_gate_b=m_gla_gate_b,
                gla_norm_w=m_gla_norm_w, pool_w=m_pool_w, pool_scale=m_pool_scale, ssd_conv_w=m_ssd_conv_w,
                ssd_conv_b=m_ssd_conv_b, ssd_dt_bias=m_ssd_dt_bias, ssd_a_log=m_ssd_a_log, ssd_d=m_ssd_d,
                ssd_norm_w=m_ssd_norm_w, w_out=m_w_out, final_norm_w=m_final_norm_w)
    v_in = dict(norm_w=v_norm_w, w_in=v_w_in, conv_a_w=v_conv_a_w, gla_gate_w=v_gla_gate_w, gla_gate_b=v_gla_gate_b,
                gla_norm_w=v_gla_norm_w, pool_w=v_pool_w, pool_scale=v_pool_scale, ssd_conv_w=v_ssd_conv_w,
                ssd_conv_b=v_ssd_conv_b, ssd_dt_bias=v_ssd_dt_bias, ssd_a_log=v_ssd_a_log, ssd_d=v_ssd_d,
                ssd_norm_w=v_ssd_norm_w, w_out=v_w_out, final_norm_w=v_final_norm_w)
    order = ("norm_w", "w_in", "conv_a_w", "gla_gate_w", "gla_gate_b", "gla_norm_w", "pool_w", "pool_scale",
             "ssd_conv_w", "ssd_conv_b", "ssd_dt_bias", "ssd_a_log", "ssd_d", "ssd_norm_w", "w_out", "final_norm_w")
    t = x.shape[1]

    comm = _Comm(w_in, w_out)
    cshard = jnp.zeros((16, 256), F32)
    for l in range(2):
        cshard = cshard.at[8 * l:8 * l + 3, 0:64].set(conv_a_w[l]).at[8 * l + 3:8 * l + 7, 0:192].set(ssd_conv_w[l])
    s_in, s_out, g_c = _gather_ici_two_hops((comm.w_in16[0], comm.w_out16[0]), cshard)
    s_in, s_out = _run_rider(_rider_gather_d2d((s_in, s_out)), "gather_d2d0")
    g_c = [jnp.where(comm.chip == s, cshard, g_c[s]) for s in range(4)]
    conv_a_full = jnp.stack([jnp.concatenate([g_c[s][8 * l:8 * l + 3, 0:64] for s in range(4)], axis=-1) for l in range(2)])
    ssd_conv_full = jnp.stack([jnp.concatenate([g_c[s][8 * l + 3:8 * l + 7, 0:192] for s in range(4)], axis=-1)
                               for l in range(2)])
    consts = [_mixer_consts(l, conv_a_full, gla_gate_w, gla_gate_b, gla_norm_w, pool_w, pool_scale, ssd_conv_full,
                            ssd_conv_b, ssd_dt_bias, ssd_a_log, ssd_d, ssd_norm_w) for l in range(2)]

    dx, big, blocks = _local_step(x.reshape(t, D), loss_target.reshape(t, D), norm_w, final_norm_w, consts,
                                  comm.layer_weights(0, s_in, s_out), comm=comm)

    as2d = lambda d: {k: (d[k].reshape(1, D) if k == "final_norm_w" else d[k]) for k in _SMALL_NAMES}
    small = _small_adamw(blocks, as2d(weights), as2d(m_in), as2d(v_in))
    grads, delta, new_m, new_v = ({k: (a.reshape(D) if k == "final_norm_w" else a) for k, a in zip(_SMALL_NAMES, part)}
                                  for part in small[0:4])
    loss = small[4].reshape(())

    grads["w_out"] = jnp.stack(big[1])

    grads["w_in"], delta["w_in"], new_m["w_in"], new_v["w_in"] = _adamw_w_in(w_in, big[0], m_w_in, v_w_in, name="adamw_w_in")
    delta["w_out"], new_m["w_out"], new_v["w_out"] = _adamw(w_out, grads["w_out"], m_w_out, v_w_out, name="adamw_w_out", br=256)

    return (loss, dx.reshape(1, t, D), *[grads[k] for k in order], *[delta[k] for k in order],
            *[new_m[k] for k in order], *[new_v[k] for k in order])
```

```python
import functools

import jax
import jax.numpy as jnp
from jax import lax
from jax.experimental import pallas as pl
from jax.experimental.pallas import tpu as pltpu

F32 = jnp.float32
BF16 = jnp.bfloat16
MESH = pl.DeviceIdType.MESH

D = 1024
CH = 64
EPS = 1e-6
NP = 3456
NPROJ = 3348
NPM = 3328
GLA_SCALE = 32.0 ** -0.5
INV_TAU = 1.0 / 16.0
TB = 512
NCH = TB // CH
assert TB % 256 == 0

C_AH, C_AB, C_AC, C_AZ, C_GQ, C_GK, C_GV = 0, 256, 512, 768, 1024, 1152, 1280
C_GZ, C_PU, C_PZ, C_SZ, C_SX, C_TL = 1536, 1792, 2048, 2304, 2560, 3328
_PERM = ((0, 1536), (1552, 1792), (1536, 16), (3344, 4))

R_CAW, R_GB, R_GNW, R_PSC, R_SCB, R_DTB, R_AE, R_DE, R_SNW, R_SCW = 0, 3, 4, 5, 6, 7, 8, 9, 10, 12

ADAM_LR, ADAM_B1, ADAM_B2, ADAM_EPS, ADAM_WD, ADAM_STEP = 0.001, 0.9, 0.999, 1e-08, 0.01, 10

VMEM_LIMIT = 56 * 1024 * 1024


def _cparams(sem, limit=VMEM_LIMIT):
    return pltpu.CompilerParams(dimension_semantics=sem, vmem_limit_bytes=limit)


_ANY = pl.BlockSpec(memory_space=pl.ANY)


def _place():
    return lax.axis_index("x"), lax.axis_index("y"), lax.axis_index("c")


class _Rider:
    def __init__(self, inputs, out_shapes, sems, start, finish, aliases=None):
        self.inputs, self.out_shapes, self.sems = tuple(inputs), tuple(out_shapes), tuple(sems)
        self.start, self.finish, self.aliases = start, finish, dict(aliases or {})


def _call(body, args, *, grid, in_specs, out_specs, out_shape, name, sem, scratch_shapes=(), rider=None):
    if rider is None:
        outs = pl.pallas_call(body, grid=grid, name=name, in_specs=list(in_specs), out_specs=list(out_specs),
                              out_shape=list(out_shape), scratch_shapes=list(scratch_shapes),
                              compiler_params=_cparams(sem))(*args)
        return list(outs), []
    ni, no, ns = len(args), len(out_shape), len(scratch_shapes)
    ri, ro = len(rider.inputs), len(rider.out_shapes)

    def full(*refs):
        ins, rins = refs[:ni], refs[ni:ni + ri]
        outs, routs = refs[ni + ri:ni + ri + no], refs[ni + ri + no:ni + ri + no + ro]
        scr, rsem = refs[ni + ri + no + ro:ni + ri + no + ro + ns], refs[ni + ri + no + ro + ns:]
        first = functools.reduce(jnp.logical_and, [pl.program_id(a) == 0 for a in range(len(grid))])
        last = functools.reduce(jnp.logical_and, [pl.program_id(a) == grid[a] - 1 for a in range(len(grid))])

        @pl.when(first)
        def _():
            rider.start(rins, routs, rsem)

        body(*ins, *outs, *scr)

        @pl.when(last)
        def _():
            rider.finish(rins, routs, rsem)

    outs = pl.pallas_call(
        full, grid=grid, name=name, in_specs=list(in_specs) + [_ANY] * ri, out_specs=list(out_specs) + [_ANY] * ro,
        out_shape=list(out_shape) + list(rider.out_shapes), scratch_shapes=list(scratch_shapes) + list(rider.sems),
        input_output_aliases={ni + k: no + v for k, v in rider.aliases.items()},
        compiler_params=_cparams(("arbitrary",) * len(grid)))(*args, *rider.inputs)
    return list(outs[:no]), list(outs[no:])


def _run_rider(rider, name):
    ri = len(rider.inputs)

    def body(*refs):
        rins, routs, rsem = refs[:ri], refs[ri:ri + len(rider.out_shapes)], refs[ri + len(rider.out_shapes):]
        rider.start(rins, routs, rsem)
        rider.finish(rins, routs, rsem)

    return list(pl.pallas_call(body, name=name, in_specs=[_ANY] * ri, out_specs=[_ANY] * len(rider.out_shapes),
                               out_shape=list(rider.out_shapes), scratch_shapes=list(rider.sems),
                               input_output_aliases=dict(rider.aliases))(*rider.inputs))


def _dot(a, b):
    return jnp.dot(a.astype(BF16), b.astype(BF16), preferred_element_type=F32)


def _dot_nt(a, b):
    return lax.dot_general(a.astype(BF16), b.astype(BF16), (((1,), (1,)), ((), ())), preferred_element_type=F32)


def _dot_tn(a, b):
    return lax.dot_general(a.astype(BF16), b.astype(BF16), (((0,), (0,)), ((), ())), preferred_element_type=F32)


def _split(a):
    hi = a.astype(BF16)
    lo = (a - hi.astype(F32)).astype(BF16)
    return hi, lo


def _dot2_l(a, b):
    hi, lo = _split(a)
    return _dot(hi, b) + _dot(lo, b)


def _dot2_r(a, b):
    hi, lo = _split(b)
    return _dot(a, hi) + _dot(a, lo)


def _dot3_l(a, b):
    hi, lo = _split(a)
    lo2 = ((a - hi.astype(F32)) - lo.astype(F32)).astype(BF16)
    return _dot(hi, b) + _dot(lo, b) + _dot(lo2, b)


def _dot2_nt(a, b):
    hi, lo = _split(a)
    return _dot_nt(hi, b) + _dot_nt(lo, b)


def _silu(z):
    return z * jax.nn.sigmoid(z)


def _lse1(x):
    return jnp.log(1.0 + jnp.exp(-jnp.abs(x)))


def _cs(a):
    return jnp.sum(a, axis=0, keepdims=True)


def _iota(shape, dim):
    return lax.broadcasted_iota(jnp.int32, shape, dim)


def _mixer_matrices():
    r, c = _iota((256, 256), 0), _iota((256, 256), 1)
    same_chunk = (r >> 6) == (c >> 6)
    mats = jnp.stack([jnp.where((c > r) & same_chunk, 1.0, 0.0), jnp.where((c < r) & same_chunk, 1.0, 0.0),
                      jnp.where(same_chunk, 1.0 / 64.0, 0.0), jnp.where((r < 128) & (r - 16 == (c >> 6)), 1.0, 0.0)])
    mask = jnp.where((_iota((256, 128), 0) >> 6) == (_iota((256, 128), 1) >> 5), 1.0, 0.0)
    return mats.astype(BF16), mask.astype(F32)


def _dn(ext, k, n, h):
    return pltpu.roll(ext, k, axis=0)[h:h + n]


def _up(ext, k, n):
    return pltpu.roll(ext, ext.shape[0] - k, axis=0)[:n]


def _pool_lane_select(lane, s2, s4, s8, s16):
    return jnp.where(lane < 64, s2, jnp.where(lane < 128, s4, jnp.where(lane < 192, s8, s16)))


def _winsum_dn(ext, lane):
    s2 = ext + pltpu.roll(ext, 1, axis=0)
    s4 = s2 + pltpu.roll(s2, 2, axis=0)
    s8 = s4 + pltpu.roll(s4, 4, axis=0)
    s16 = s8 + pltpu.roll(s8, 8, axis=0)
    return _pool_lane_select(lane, s2, s4, s8, s16)


def _winsum_up(ext, lane):
    m = ext.shape[0]
    s2 = ext + pltpu.roll(ext, m - 1, axis=0)
    s4 = s2 + pltpu.roll(s2, m - 2, axis=0)
    s8 = s4 + pltpu.roll(s4, m - 4, axis=0)
    s16 = s8 + pltpu.roll(s8, m - 8, axis=0)
    return _pool_lane_select(lane, s2, s4, s8, s16)


def _pool_inv_count(tile, n):
    lane = _iota((1, 256), 1)
    win = _pool_lane_select(lane, 2.0, 4.0, 8.0, 16.0).astype(F32)
    tpos = (tile * n + _iota((n, 1), 0) + 1).astype(F32)
    return jnp.where(tpos >= win, 1.0 / win, 1.0 / tpos)


def _silu_pair(z):
    s = jax.nn.sigmoid(z)
    return z * s, s * (1.0 + z * (1.0 - s))


def _chunks(a):
    return [a[c * CH:(c + 1) * CH] for c in range(a.shape[0] // CH)]


def _halves(fn, a, b):
    return jnp.concatenate([fn(a[:, 0:128], b[:, 0:128]), fn(a[:, 128:256], b[:, 128:256])], axis=1)


def _chunk_sums(tri, a):
    return jnp.concatenate([_dot2_r(tri, a[r:r + 256]) for r in range(0, a.shape[0], 256)], axis=0)


def _mixer_tile_prep(p_ref, t_ref, xc, prm_ref, gw_v, cm_ref, mk_ref):
    tail = t_ref[...]
    pre = _dot(tail, gw_v) + prm_ref[R_GB:R_GB + 1, 0:128]
    la = (jnp.minimum(pre, 0.0) - _lse1(pre)) * INV_TAU
    dtin = tail + prm_ref[R_DTB:R_DTB + 1, 0:128]
    dtf = jnp.maximum(dtin, 0.0) + _lse1(dtin)
    dte = _dot2_l(dtf, cm_ref[3, 0:128, :])
    da = dte * prm_ref[R_AE:R_AE + 1, 0:256]
    rev = _chunk_sums(cm_ref[0], jnp.concatenate([la, da], axis=1))
    dec = jnp.exp(rev[:, 0:128])
    kd = p_ref[:, C_GK:C_GK + 128].astype(F32) * dec
    wdec = jnp.exp(rev[:, 128:384])
    w = wdec * dte
    xw = xc[:, 0:256] * w
    d_s = [jnp.exp(_cs(a)) for a in _chunks(la)]
    et = [jnp.exp(_cs(a)) for a in _chunks(da)]
    mask_t = mk_ref[...]
    ut_g = [_dot_tn(v, k) * mask_t for v, k in zip(_chunks(p_ref[:, C_GV:C_GV + 256].astype(F32)), _chunks(kd))]
    ut_s = [_halves(_dot_tn, b, x) for b, x in zip(_chunks(xc[:, 256:512]), _chunks(xw))]
    return tail, pre, dtin, dte, dec, kd, wdec, w, xw, d_s, et, ut_g, ut_s


def _rmsproj(x, nw, wp, name, tm=512, rider=None):
    t = x.shape[0]

    def body(x_ref, nw_ref, w_ref, o_ref, t_ref, h_ref):
        xv = x_ref[...]
        rs = lax.rsqrt(jnp.mean(xv * xv, axis=-1, keepdims=True) + EPS)
        h = (xv * rs * nw_ref[...]).astype(BF16)
        h_ref[...] = h
        proj = jnp.dot(h, w_ref[...], preferred_element_type=F32)
        o_ref[...] = proj[:, 0:NPM].astype(BF16)
        t_ref[...] = proj[:, NPM:NP]

    (proj, tail, h), extra = _call(
        body, (x, nw, wp), grid=(t // tm,), name=name, sem=("parallel",), rider=rider,
        in_specs=[pl.BlockSpec((tm, D), lambda i: (i, 0)), pl.BlockSpec((1, D), lambda i: (0, 0)),
                  pl.BlockSpec((D, NP), lambda i: (0, 0))],
        out_specs=[pl.BlockSpec((tm, NPM), lambda i: (i, 0)), pl.BlockSpec((tm, NP - NPM), lambda i: (i, 0)),
                   pl.BlockSpec((tm, D), lambda i: (i, 0))],
        out_shape=[jax.ShapeDtypeStruct((t, NPM), BF16), jax.ShapeDtypeStruct((t, NP - NPM), F32),
                   jax.ShapeDtypeStruct((t, D), BF16)])
    return (proj, tail), h, extra


def _head_tile(xv, tgt, w):
    rs = lax.rsqrt(jnp.mean(xv * xv, axis=-1, keepdims=True) + EPS)
    xh = xv * rs
    err = xh * w - tgt
    dy = err * (1.0 / D)
    dxh = dy * w
    dx = rs * (dxh - xh * jnp.mean(dxh * xh, axis=-1, keepdims=True))
    return dx, _cs(dy * xh), (0.5 / D) * jnp.sum(err * err)


def _dxin(dp, wpt, x, dxn, nw, name, tm=512, rider=None):
    t = x.shape[0]

    def body(dp_ref, w_ref, x_ref, dxn_ref, nw_ref, dx_ref, dnw_ref):
        @pl.when(pl.program_id(0) == 0)
        def _():
            dnw_ref[...] = jnp.zeros_like(dnw_ref)

        dh = jnp.dot(dp_ref[...], w_ref[...], preferred_element_type=F32)
        xv = x_ref[...]
        rs = lax.rsqrt(jnp.mean(xv * xv, axis=-1, keepdims=True) + EPS)
        xh = xv * rs
        dnw_ref[0:1, :] += _cs(dh * xh)
        dxh = dh * nw_ref[...]
        dx_ref[...] = dxn_ref[...] + rs * (dxh - xh * jnp.mean(dxh * xh, axis=-1, keepdims=True))

    return _call(
        body, (dp, wpt, x, dxn, nw), grid=(t // tm,), name=name, sem=("arbitrary",), rider=rider,
        in_specs=[pl.BlockSpec((tm, NP), lambda i: (i, 0)), pl.BlockSpec((NP, D), lambda i: (0, 0)),
                  pl.BlockSpec((tm, D), lambda i: (i, 0)), pl.BlockSpec((tm, D), lambda i: (i, 0)),
                  pl.BlockSpec((1, D), lambda i: (0, 0))],
        out_specs=[pl.BlockSpec((tm, D), lambda i: (i, 0)), pl.BlockSpec((8, D), lambda i: (0, 0))],
        out_shape=[jax.ShapeDtypeStruct((t, D), F32), jax.ShapeDtypeStruct((8, D), F32)])


def _dwin(h, dp, name, tm=1024, rider=None):
    t = h.shape[0]

    def body(h_ref, dp_ref, o_ref):
        @pl.when(pl.program_id(0) == 0)
        def _():
            o_ref[...] = jnp.zeros_like(o_ref)

        o_ref[...] += _dot_tn(h_ref[...], dp_ref[...])

    (dwp,), extra = _call(
        body, (h, dp), grid=(t // tm,), name=name, sem=("arbitrary",), rider=rider,
        in_specs=[pl.BlockSpec((tm, D), lambda i: (i, 0)), pl.BlockSpec((tm, NP), lambda i: (i, 0))],
        out_specs=[pl.BlockSpec((D, NP), lambda i: (0, 0))], out_shape=[jax.ShapeDtypeStruct((D, NP), F32)])
    return dwp, extra


def _mixer_fwd(proj, x, wo, prm, gw, pw, cmat, mask, name, rider=None, head=None):
    proj, tail = proj
    t = proj.shape[0]
    nt, nc = t // TB, t // CH

    def body(p_ref, t_ref, x_ref, wo_ref, prm_ref, gw_ref, pw_ref, cm_ref, mk_ref, *rest):
        (tgt_ref, fw_ref), rest = (rest[:2], rest[2:]) if head else ((None, None), rest)
        mix_ref, sg_ref, ss_ref, xn_ref, xc_ref, dxc_ref = rest[:6]
        acc_ref = rest[6] if head else None
        sg_s, ss_s, h_ua, h_pu, h_sx = rest[-5:]
        i = pl.program_id(0)

        @pl.when(i == 0)
        def _():
            for r in (sg_s, ss_s, h_ua, h_pu, h_sx) + ((acc_ref,) if head else ()):
                r[...] = jnp.zeros_like(r)

        lane = _iota((1, 256), 1)
        u = p_ref[:, C_AC:C_AC + 256].astype(F32) * p_ref[:, C_AH:C_AH + 256].astype(F32)
        ext = jnp.concatenate([h_ua[...], u], axis=0)
        cv = (prm_ref[R_CAW + 2:R_CAW + 3, 0:256] * u + prm_ref[R_CAW + 1:R_CAW + 2, 0:256] * _dn(ext, 1, TB, 8)
              + prm_ref[R_CAW:R_CAW + 1, 0:256] * _dn(ext, 2, TB, 8))
        mix_ref[:, 0:256] = (p_ref[:, C_AB:C_AB + 256].astype(F32) * cv * _silu(p_ref[:, C_AZ:C_AZ + 256].astype(F32))).astype(BF16)
        h_ua[...] = u[TB - 8:, :]
        pu = p_ref[:, C_PU:C_PU + 256].astype(F32)
        ext = jnp.concatenate([h_pu[...], pu], axis=0)
        pooled = _winsum_dn(ext, lane)[16:] * _pool_inv_count(i, TB) - pu
        mixed = _dot(pooled, pw_ref[...])
        mix_ref[:, 512:768] = (prm_ref[R_PSC:R_PSC + 1, 0:256] * mixed * _silu(p_ref[:, C_PZ:C_PZ + 256].astype(F32))).astype(BF16)
        h_pu[...] = pu[TB - 16:, :]
        sx = p_ref[:, C_SX:C_SX + 768].astype(F32)
        ext = jnp.concatenate([h_sx[...], sx], axis=0)
        xc, dxc = _silu_pair(prm_ref[R_SCW + 3:R_SCW + 4, :] * sx + prm_ref[R_SCW + 2:R_SCW + 3, :] * _dn(ext, 1, TB, 8)
                             + prm_ref[R_SCW + 1:R_SCW + 2, :] * _dn(ext, 2, TB, 8)
                             + prm_ref[R_SCW:R_SCW + 1, :] * _dn(ext, 3, TB, 8) + prm_ref[R_SCB:R_SCB + 1, :])
        xc_ref[...] = xc.astype(BF16)
        dxc_ref[...] = dxc.astype(BF16)
        h_sx[...] = sx[TB - 8:, :]

        _, _, _, _, _, _, _, _, _, d_s, et, ut_g, ut_s = _mixer_tile_prep(p_ref, t_ref, xc, prm_ref, gw_ref[...], cm_ref, mk_ref)
        s_g, s_s = sg_s[...], ss_s[...]
        o, y = [], []
        qs = _chunks(p_ref[:, C_GQ:C_GQ + 128].astype(F32) * GLA_SCALE)
        cm = _chunks(xc[:, 512:768])
        for c in range(NCH):
            sg_ref[c] = s_g
            ss_ref[c] = s_s
            s_g = s_g * d_s[c] + ut_g[c]
            s_s = s_s * et[c] + ut_s[c]
            o.append(_dot_nt(qs[c], s_g))
            y.append(_halves(_dot, cm[c], s_s))
        sg_s[...] = s_g
        ss_s[...] = s_s
        o = jnp.concatenate(o, axis=0)
        on = o * lax.rsqrt(_dot2_l(o * o, cm_ref[2]) + EPS)
        mix_ref[:, 256:512] = (on * prm_ref[R_GNW:R_GNW + 1, 0:256] * _silu(p_ref[:, C_GZ:C_GZ + 256].astype(F32))).astype(BF16)
        y2 = ((jnp.concatenate(y, axis=0) + prm_ref[R_DE:R_DE + 1, 0:256] * xc[:, 0:256])
              * _silu(p_ref[:, C_SZ:C_SZ + 256].astype(F32)))
        mix_ref[:, 768:1024] = (y2 * lax.rsqrt(jnp.mean(y2 * y2, axis=-1, keepdims=True) + EPS)
                                * prm_ref[R_SNW:R_SNW + 1, 0:256]).astype(BF16)
        xn = x_ref[...] + jnp.dot(mix_ref[...], wo_ref[...], preferred_element_type=F32)
        if head:
            xn_ref[...], dfw, loss = _head_tile(xn, tgt_ref[...], fw_ref[...])
            acc_ref[0:1, :] += dfw
            acc_ref[1:2, :] += jnp.zeros((1, D), F32) + loss
        else:
            xn_ref[...] = xn

    row = pl.BlockSpec((TB, D), lambda i: (i, 0))
    return _call(
        body, (proj, tail, x, wo, prm, gw, pw, cmat, mask) + tuple(head or ()), grid=(nt,), name=name, sem=("arbitrary",),
        rider=rider,
        in_specs=[pl.BlockSpec((TB, NPM), lambda i: (i, 0)), pl.BlockSpec((TB, NP - NPM), lambda i: (i, 0)), row,
                  pl.BlockSpec((D, D), lambda i: (0, 0)), pl.BlockSpec((16, 768), lambda i: (0, 0)),
                  pl.BlockSpec((128, 128), lambda i: (0, 0)), pl.BlockSpec((256, 256), lambda i: (0, 0)),
                  pl.BlockSpec((4, 256, 256), lambda i: (0, 0, 0)), pl.BlockSpec((256, 128), lambda i: (0, 0))]
        + ([row, pl.BlockSpec((1, D), lambda i: (0, 0))] if head else []),
        out_specs=[row, pl.BlockSpec((NCH, 256, 128), lambda i: (i, 0, 0)),
                   pl.BlockSpec((NCH, 128, 256), lambda i: (i, 0, 0)), row] + [pl.BlockSpec((TB, 768), lambda i: (i, 0))] * 2
        + ([pl.BlockSpec((8, D), lambda i: (0, 0))] if head else []),
        out_shape=[jax.ShapeDtypeStruct((t, D), BF16), jax.ShapeDtypeStruct((nc, 256, 128), F32),
                   jax.ShapeDtypeStruct((nc, 128, 256), F32), jax.ShapeDtypeStruct((t, D), F32)]
        + [jax.ShapeDtypeStruct((t, 768), BF16)] * 2 + ([jax.ShapeDtypeStruct((8, D), F32)] if head else []),
        scratch_shapes=[pltpu.VMEM((256, 128), F32), pltpu.VMEM((128, 256), F32), pltpu.VMEM((8, 256), F32),
                        pltpu.VMEM((16, 256), F32), pltpu.VMEM((8, 768), F32)])


def _mixer_bwd(proj, dxn, wot, mix, sg, ss, xc16, dxc16, prm, gw, pw, cmat, mask, name, rider=None):
    proj, tail = proj
    t = proj.shape[0]
    nt = t // TB
    rev = lambda i: nt - 1 - i

    def body(p_ref, hp_ref, t_ref, dxn_ref, wot_ref, mix_ref, sg_ref, ss_ref, xc_ref, dxc_ref, prm_ref, gw_ref, pw_ref,
             cm_ref, mk_ref, dp_ref, sgc_ref, dwo_ref,
             gg_s, gs_s, h_dcv, h_dpl, h_dpre, gsm_ref, dgw_ref, dpw_ref, dm_ref):
        i = pl.program_id(0)
        tile = nt - 1 - i

        @pl.when(i == 0)
        def _():
            for r in (gg_s, gs_s, h_dcv, h_dpl, h_dpre, gsm_ref, dgw_ref, dpw_ref, dwo_ref):
                r[...] = jnp.zeros_like(r)

        dxn = dxn_ref[...].astype(BF16)
        dm_ref[...] = jnp.dot(dxn, wot_ref[...], preferred_element_type=F32)
        dwo_ref[...] += _dot_tn(mix_ref[...], dxn)

        lane = _iota((1, 256), 1)
        first = (tile > 0).astype(F32)
        ah, ac = p_ref[:, C_AH:C_AH + 256].astype(F32), p_ref[:, C_AC:C_AC + 256].astype(F32)
        ab, az = p_ref[:, C_AB:C_AB + 256].astype(F32), p_ref[:, C_AZ:C_AZ + 256].astype(F32)
        w0, w1, w2 = (prm_ref[R_CAW + j:R_CAW + j + 1, 0:256] for j in range(3))
        u = ac * ah
        ext = jnp.concatenate([(hp_ref[:, C_AC:C_AC + 256].astype(F32) * hp_ref[:, C_AH:C_AH + 256].astype(F32))[8:16] * first, u], axis=0)
        u1, u2 = _dn(ext, 1, TB, 8), _dn(ext, 2, TB, 8)
        cv = w2 * u + w1 * u1 + w0 * u2
        g = dm_ref[:, 0:256]
        sz, dsz = _silu_pair(az)
        dp_ref[:, C_AB:C_AB + 256] = (g * cv * sz).astype(BF16)
        dp_ref[:, C_AZ:C_AZ + 256] = (g * ab * cv * dsz).astype(BF16)
        dcv = g * ab * sz
        dext = jnp.concatenate([dcv, h_dcv[...]], axis=0)
        du = w2 * dcv + w1 * _up(dext, 1, TB) + w0 * _up(dext, 2, TB)
        dp_ref[:, C_AC:C_AC + 256] = (du * ah).astype(BF16)
        dp_ref[:, C_AH:C_AH + 256] = (du * ac).astype(BF16)
        gsm_ref[R_CAW:R_CAW + 1, 0:256] += _cs(dcv * u2)
        gsm_ref[R_CAW + 1:R_CAW + 2, 0:256] += _cs(dcv * u1)
        gsm_ref[R_CAW + 2:R_CAW + 3, 0:256] += _cs(dcv * u)
        h_dcv[...] = dcv[0:8, :]
        pu, pz = p_ref[:, C_PU:C_PU + 256].astype(F32), p_ref[:, C_PZ:C_PZ + 256].astype(F32)
        psc = prm_ref[R_PSC:R_PSC + 1, 0:256]
        icnt = _pool_inv_count(tile, TB)
        ext = jnp.concatenate([hp_ref[:, C_PU:C_PU + 256].astype(F32) * first, pu], axis=0)
        pooled = _winsum_dn(ext, lane)[16:] * icnt - pu
        pw_v = pw_ref[...]
        mixed = _dot(pooled, pw_v)
        g = dm_ref[:, 512:768]
        sz, dsz = _silu_pair(pz)
        gsm_ref[R_PSC:R_PSC + 1, 0:256] += _cs(g * mixed * sz)
        dp_ref[:, C_PZ:C_PZ + 256] = (g * psc * mixed * dsz).astype(BF16)
        dmixed = g * psc * sz
        dpw_ref[...] += _dot_tn(pooled, dmixed)
        dpooled = _dot_nt(dmixed, pw_v)
        qd = dpooled * icnt
        dext = jnp.concatenate([qd, h_dpl[...]], axis=0)
        dp_ref[:, C_PU:C_PU + 256] = (_winsum_up(dext, lane)[:TB] - dpooled).astype(BF16)
        h_dpl[...] = qd[0:16, :]
        cw = [prm_ref[R_SCW + j:R_SCW + j + 1, :] for j in range(4)]
        xc = xc_ref[...].astype(F32)
        xs, bm, cm = xc[:, 0:256], xc[:, 256:512], xc[:, 512:768]

        gw_v = gw_ref[...]
        tail, pre, dtin, dte, dec, kd, wdec, w, xw, d_s, et, ut_g, ut_s = _mixer_tile_prep(p_ref, t_ref, xc, prm_ref,
                                                                                          gw_v, cm_ref, mk_ref)
        gmean = cm_ref[2]
        mask_t = mk_ref[...]
        gnw = prm_ref[R_GNW:R_GNW + 1, 0:256]
        a_e = prm_ref[R_AE:R_AE + 1, 0:256]
        d_e = prm_ref[R_DE:R_DE + 1, 0:256]
        snw = prm_ref[R_SNW:R_SNW + 1, 0:256]
        sg_in = [sg_ref[c] for c in range(NCH)]
        ss_in = [ss_ref[c] for c in range(NCH)]
        sg_n = [sg_in[c] * d_s[c] + ut_g[c] for c in range(NCH)]
        ss_n = [ss_in[c] * et[c] + ut_s[c] for c in range(NCH)]
        qs = _chunks(p_ref[:, C_GQ:C_GQ + 128].astype(F32) * GLA_SCALE)
        cm_c, bm_c, xw_c, kd_c = _chunks(cm), _chunks(bm), _chunks(xw), _chunks(kd)
        v_c = _chunks(p_ref[:, C_GV:C_GV + 256].astype(F32))
        o = jnp.concatenate([_dot_nt(qs[c], sg_n[c]) for c in range(NCH)], axis=0)
        y = jnp.concatenate([_halves(_dot, cm_c[c], ss_n[c]) for c in range(NCH)], axis=0) + d_e * xs
        gz = p_ref[:, C_GZ:C_GZ + 256].astype(F32)
        r = lax.rsqrt(_dot2_l(o * o, gmean) + EPS)
        on = o * r
        dyb = dm_ref[:, 256:512]
        sz, dsz = _silu_pair(gz)
        dp_ref[:, C_GZ:C_GZ + 256] = (dyb * on * gnw * dsz).astype(BF16)
        tg = dyb * sz
        gsm_ref[R_GNW:R_GNW + 1, 0:256] += _cs(tg * on)
        don = tg * gnw
        do_c = _chunks(r * (don - on * _dot2_l(don * on, gmean)))
        ssz = p_ref[:, C_SZ:C_SZ + 256].astype(F32)
        sil, dsil = _silu_pair(ssz)
        y2 = y * sil
        r = lax.rsqrt(jnp.mean(y2 * y2, axis=-1, keepdims=True) + EPS)
        yn = y2 * r
        dyd = dm_ref[:, 768:1024]
        gsm_ref[R_SNW:R_SNW + 1, 0:256] += _cs(dyd * yn)
        dn = dyd * snw
        dy2 = r * (dn - yn * jnp.mean(dn * yn, axis=-1, keepdims=True))
        dp_ref[:, C_SZ:C_SZ + 256] = (dy2 * y * dsil).astype(BF16)
        dy = dy2 * sil
        gsm_ref[R_DE:R_DE + 1, 0:256] += _cs(dy * xs)
        dy_c = _chunks(dy)
        dq = jnp.concatenate([_dot(do_c[c], sg_n[c]) for c in range(NCH)], axis=0)
        dp_ref[:, C_GQ:C_GQ + 128] = (dq * GLA_SCALE).astype(BF16)
        dcm = jnp.concatenate([_halves(_dot_nt, dy_c[c], ss_n[c]) for c in range(NCH)], axis=0)
        gg = [_dot_tn(do_c[c], qs[c]) * mask_t for c in range(NCH)]
        gs = [_halves(_dot_tn, cm_c[c], dy_c[c]) for c in range(NCH)]
        car_g, car_s = gg_s[...], gs_s[...]
        for c in reversed(range(NCH)):
            gg[c] = gg[c] + car_g
            gs[c] = gs[c] + car_s
            car_g = gg[c] * d_s[c]
            car_s = gs[c] * et[c]
        gg_s[...] = car_g
        gs_s[...] = car_s
        dkd = jnp.concatenate([_dot(v_c[c], gg[c]) for c in range(NCH)], axis=0)
        dp_ref[:, C_GV:C_GV + 256] = jnp.concatenate([_dot_nt(kd_c[c], gg[c]) for c in range(NCH)], axis=0).astype(BF16)
        dp_ref[:, C_GK:C_GK + 128] = (dkd * dec).astype(BF16)
        dbm = jnp.concatenate([_halves(_dot_nt, xw_c[c], gs[c]) for c in range(NCH)], axis=0)
        dxw = jnp.concatenate([_halves(_dot, bm_c[c], gs[c]) for c in range(NCH)], axis=0)
        dxs = dy * d_e + dxw * w
        dw = dxw * xs
        dsuf = _chunk_sums(cm_ref[1], jnp.concatenate([dkd * kd, dw * dte * wdec], axis=1))
        tot_g = jnp.concatenate([jnp.broadcast_to(_cs(gg[c] * sg_in[c]) * d_s[c], (CH, 128)) for c in range(NCH)], axis=0)
        tot_s = jnp.concatenate([jnp.broadcast_to(_cs(gs[c] * ss_in[c]) * et[c], (CH, 256)) for c in range(NCH)], axis=0)
        dpre = (dsuf[:, 0:128] + tot_g) * INV_TAU * jax.nn.sigmoid(-pre)
        dgw_ref[...] += _dot_tn(tail, dpre)
        gsm_ref[R_GB:R_GB + 1, 0:128] += _cs(dpre)
        dda = dsuf[:, 128:384] + tot_s
        gsm_ref[R_AE:R_AE + 1, 0:256] += _cs(dda * dte)
        dtail_s = _dot2_nt(dw * wdec + dda * a_e, cm_ref[3, 0:128, :]) * jax.nn.sigmoid(dtin)
        gsm_ref[R_DTB:R_DTB + 1, 0:128] += _cs(dtail_s)
        dp_ref[:, C_TL:C_TL + 128] = (_dot_nt(dpre, gw_v) + dtail_s).astype(BF16)
        dpre_c = jnp.concatenate([dxs, dbm, dcm], axis=1) * dxc_ref[...].astype(F32)
        dext = jnp.concatenate([dpre_c, h_dpre[...]], axis=0)
        ups = [dpre_c, _up(dext, 1, TB), _up(dext, 2, TB), _up(dext, 3, TB)]
        dp_ref[:, C_SX:C_SX + 768] = (cw[3] * ups[0] + cw[2] * ups[1] + cw[1] * ups[2] + cw[0] * ups[3]).astype(BF16)
        sx = p_ref[:, C_SX:C_SX + 768].astype(F32)
        for k in range(4):
            gsm_ref[R_SCW + k:R_SCW + k + 1, :] += _cs(sx * ups[3 - k])
        gsm_ref[R_SCB:R_SCB + 1, :] += _cs(dpre_c)
        h_dpre[...] = dpre_c[0:8, :]

        @pl.when(i == nt - 1)
        def _():
            ri, ci = _iota((256, 256), 0), _iota((256, 256), 1)
            per_head = jnp.where((ri >> 6) == ci, 1.0, 0.0).astype(BF16)
            per_dv = jnp.where((ri & 63) == ci, 1.0, 0.0).astype(BF16)
            row = _iota((8, 256), 0)
            top = gsm_ref[0:8, 0:256]
            sgc_ref[0:8, 0:256] = jnp.where(row == R_GNW, _dot3_l(top, per_dv), top)
            bot = gsm_ref[8:16, 0:256]
            fold = _dot3_l(jnp.where(row == R_AE - 8, bot * a_e, bot), per_head)
            sgc_ref[8:16, 0:256] = jnp.where((row == R_AE - 8) | (row == R_DE - 8), fold, bot)
            sgc_ref[0:16, 256:768] = gsm_ref[:, 256:768]
            sgc_ref[0:16, 768:896] = dgw_ref[0:16, :]
            sgc_ref[0:16, 896:1024] = jnp.zeros((16, 128), F32)
            diag = _pool_lane_select(lane, dpw_ref[0:64, :], dpw_ref[64:128, :], dpw_ref[128:192, :], dpw_ref[192:256, :])
            for q in range(4):
                sgc_ref[16:32, 256 * q:256 * q + 256] = diag[16 * q:16 * q + 16, :]

    return _call(
        body, (proj, proj, tail, dxn, wot, mix, sg, ss, xc16, dxc16, prm, gw, pw, cmat, mask), grid=(nt,), name=name,
        sem=("arbitrary",), rider=rider,
        in_specs=[pl.BlockSpec((TB, NPM), lambda i: (rev(i), 0)),
                  pl.BlockSpec((16, NPM), lambda i: (jnp.maximum(rev(i) * (TB // 16) - 1, 0), 0)),
                  pl.BlockSpec((TB, NP - NPM), lambda i: (rev(i), 0)),
                  pl.BlockSpec((TB, D), lambda i: (rev(i), 0)), pl.BlockSpec((D, D), lambda i: (0, 0)),
                  pl.BlockSpec((TB, D), lambda i: (rev(i), 0)),
                  pl.BlockSpec((NCH, 256, 128), lambda i: (rev(i), 0, 0)),
                  pl.BlockSpec((NCH, 128, 256), lambda i: (rev(i), 0, 0)),
                  pl.BlockSpec((TB, 768), lambda i: (rev(i), 0)), pl.BlockSpec((TB, 768), lambda i: (rev(i), 0)),
                  pl.BlockSpec((16, 768), lambda i: (0, 0)), pl.BlockSpec((128, 128), lambda i: (0, 0)),
                  pl.BlockSpec((256, 256), lambda i: (0, 0)), pl.BlockSpec((4, 256, 256), lambda i: (0, 0, 0)),
                  pl.BlockSpec((256, 128), lambda i: (0, 0))],
        out_specs=[pl.BlockSpec((TB, NP), lambda i: (rev(i), 0)), pl.BlockSpec((32, 1024), lambda i: (0, 0)),
                   pl.BlockSpec((D, D), lambda i: (0, 0))],
        out_shape=[jax.ShapeDtypeStruct((t, NP), BF16), jax.ShapeDtypeStruct((32, 1024), F32),
                   jax.ShapeDtypeStruct((D, D), F32)],
        scratch_shapes=[pltpu.VMEM((256, 128), F32), pltpu.VMEM((128, 256), F32), pltpu.VMEM((8, 256), F32),
                        pltpu.VMEM((16, 256), F32), pltpu.VMEM((8, 768), F32), pltpu.VMEM((16, 768), F32),
                        pltpu.VMEM((128, 128), F32), pltpu.VMEM((256, 256), F32), pltpu.VMEM((TB, D), F32)])


SHARD = NPROJ // 4
SHARD_PAD = 896


def _ranges_to_perm(o, n):
    out, p = [], 0
    for start, size in _PERM:
        a, b = max(o, start), min(o + n, start + size)
        if a < b:
            out.append((a, b - a, p + a - start))
        p += size
    return out


def _ranges_to_orig(p0, n):
    out, p = [], 0
    for start, size in _PERM:
        a, b = max(p0, p), min(p0 + n, p + size)
        if a < b:
            out.append((a, b - a, start + a - p))
        p += size
    return out


def _lane_window(load, lo, n, d, lane):
    a = 128 * (lo // 128)
    off = lo - a
    w = 128 if off + n <= 128 else 256
    chunk = load(a, w)
    shift = (d - off) % w
    if shift:
        chunk = pltpu.roll(chunk, shift, axis=1)
    return jnp.where((lane >= d) & (lane < d + n), chunk[:, 0:128], 0.0)


def _assemble_w_in(slabs, name, rb=256):
    def body(s_ref, wp_ref, wpt_ref):
        lane = _iota((1, 128), 1)
        for b in range(NP // 128):
            acc = jnp.zeros((rb, 128), F32)
            for p, n, o in _ranges_to_orig(128 * b, 128):
                while n > 0:
                    s, lo = o // SHARD, o % SHARD
                    cnt = min(n, SHARD - lo)
                    acc = acc + _lane_window(lambda a, w, s=s: s_ref[s, :, a:a + w].astype(F32), lo, cnt, p - 128 * b, lane)
                    o, p, n = o + cnt, p + cnt, n - cnt
            wp_ref[:, 128 * b:128 * b + 128] = acc.astype(BF16)
            wpt_ref[128 * b:128 * b + 128, :] = acc.T.astype(BF16)

    return pl.pallas_call(
        body, grid=(D // rb,), name=name,
        in_specs=[pl.BlockSpec((4, rb, SHARD_PAD), lambda i: (0, i, 0))],
        out_specs=[pl.BlockSpec((rb, NP), lambda i: (i, 0)), pl.BlockSpec((NP, rb), lambda i: (0, i))],
        out_shape=[jax.ShapeDtypeStruct((D, NP), BF16), jax.ShapeDtypeStruct((NP, D), BF16)],
        compiler_params=_cparams(("parallel",)))(slabs)


def _split_dw_in(dwp, name, rb=256):
    rows = dwp.shape[0]

    def body(g_ref, o_ref):
        lane = _iota((1, 128), 1)
        for s in range(4):
            for k in range(SHARD_PAD // 128):
                acc = jnp.zeros((rb, 128), F32)
                n_valid = min(128, SHARD - 128 * k)
                for o, n, p in _ranges_to_perm(SHARD * s + 128 * k, n_valid):
                    acc = acc + _lane_window(lambda a, w: g_ref[:, a:a + w].astype(F32), p, n, o - SHARD * s - 128 * k, lane)
                o_ref[s, :, 128 * k:128 * k + 128] = acc.astype(o_ref.dtype)

    return pl.pallas_call(
        body, grid=(rows // rb,), name=name,
        in_specs=[pl.BlockSpec((rb, NP), lambda i: (i, 0))],
        out_specs=pl.BlockSpec((4, rb, SHARD_PAD), lambda i: (0, i, 0)),
        out_shape=jax.ShapeDtypeStruct((4, rows, SHARD_PAD), dwp.dtype),
        compiler_params=_cparams(("parallel",)))(dwp)


def _half(c, n):
    return pl.ds(pl.multiple_of(c * (n // 2), n // 2), n // 2)


def _other_chips(x, y):
    return ((1 - x, y), (x, 1 - y), (1 - x, 1 - y))


def _remote(src, dst, send, recv, k, dev):
    return pltpu.make_async_remote_copy(src_ref=src, dst_ref=dst, send_sem=send.at[k], recv_sem=recv.at[k], device_id=dev,
                                        device_id_type=MESH)


def _sem(n):
    return pltpu.SemaphoreType.DMA((n,))


def _rider_gather_ici(shards):
    shards = tuple(shards)
    n = len(shards)

    def copies(rins, routs, sems, arrivals=True):
        send, recv = sems
        x, y, c = _place()
        me = 2 * x + y
        out, inc = [], []
        for j, (px, py) in enumerate(_other_chips(x, y)):
            for k in range(n):
                rows = _half(c, shards[k].shape[0])
                out.append(_remote(rins[k].at[rows], routs[k].at[me, rows], send, recv, n * j + k, (px, py, c)))
                if arrivals:
                    inc.append(_remote(rins[k].at[rows], routs[k].at[2 * px + py, rows], send, recv, n * j + k, (px, py, c)))
        return out, inc

    def start(rins, routs, sems):
        for cp in copies(rins, routs, sems, arrivals=False)[0]:
            cp.start()

    def finish(rins, routs, sems):
        out, inc = copies(rins, routs, sems)
        for cp in inc:
            cp.wait_recv()
        for cp in out:
            cp.wait_send()

    return _Rider(shards, [jax.ShapeDtypeStruct((4,) + a.shape, a.dtype) for a in shards], [_sem(3 * n), _sem(3 * n)],
                  start, finish)


def _gather_ici_two_hops(shards, extra):
    shards = tuple(shards)
    n = len(shards)

    def body(*refs):
        ins, e_in, outs, e_out = refs[:n], refs[n], refs[n + 1:2 * n + 1], refs[2 * n + 1]
        send, recv = refs[2 * n + 2:]
        x, y, c = _place()
        slab = lambda px, py: 2 * px + py
        xn, yn, dg = (1 - x, y), (x, 1 - y), (1 - x, 1 - y)

        def part(k, q):
            r = shards[k].shape[0] // 4
            return pl.ds(pl.multiple_of(c * 2 * r + q * r, r), r)

        def hop(k, q, src_chip, to, sem):
            rows = part(k, q)
            src = ins[k].at[rows] if src_chip is None else outs[k].at[slab(*src_chip), rows]
            own = (x, y) if src_chip is None else src_chip
            return _remote(src, outs[k].at[slab(*own), rows], send, recv, sem, (*to, c))

        small = [_remote(e_in, e_out.at[slab(x, y)], send, recv, 6 * n + j, (*to, c)) for j, to in enumerate((xn, yn, dg))]
        first = [hop(k, q, None, (xn, yn)[q], 2 * k + q) for k in range(n) for q in (0, 1)]
        for cp in small + first:
            cp.start()
        for k in range(n):
            for q in (0, 1):
                nb = (xn, yn)[q]
                _remote(ins[k].at[part(k, q)], outs[k].at[slab(*nb), part(k, q)], send, recv, 2 * k + q, (*nb, c)).wait_recv()
        second = []
        for k in range(n):
            for q in (0, 1):
                to, via = (yn, xn)[q], (xn, yn)[q]
                second.append(hop(k, q, None, to, 2 * n + 4 * k + 2 * q))
                second.append(hop(k, q, via, to, 2 * n + 4 * k + 2 * q + 1))
        for cp in second:
            cp.start()
        for k in range(n):
            for q in (0, 1):
                frm, rows = (yn, xn)[q], part(k, q)
                for j, origin in enumerate((frm, dg)):
                    _remote(ins[k].at[rows], outs[k].at[slab(*origin), rows], send, recv, 2 * n + 4 * k + 2 * q + j,
                            (*frm, c)).wait_recv()
        for j, frm in enumerate((xn, yn, dg)):
            _remote(e_in, e_out.at[slab(*frm)], send, recv, 6 * n + j, (*frm, c)).wait_recv()
        for cp in small + first + second:
            cp.wait_send()

    outs = pl.pallas_call(
        body, name="gather_ici0", in_specs=[_ANY] * (n + 1), out_specs=[_ANY] * (n + 1),
        out_shape=[jax.ShapeDtypeStruct((4,) + a.shape, a.dtype) for a in shards + (extra,)],
        scratch_shapes=[_sem(6 * n + 3), _sem(6 * n + 3)])(*shards, extra)
    return list(outs)


def _rider_gather_d2d(slabs):
    slabs = tuple(slabs)
    n = len(slabs)

    def copies(routs, sems, arrivals=True):
        send, recv = sems
        x, y, c = _place()
        out, inc = [], []
        for j, (px, py) in enumerate(_other_chips(x, y)):
            for k in range(n):
                rows = slabs[k].shape[1]
                mine, theirs = routs[k].at[2 * px + py, _half(c, rows)], routs[k].at[2 * px + py, _half(1 - c, rows)]
                out.append(_remote(mine, mine, send, recv, n * j + k, (x, y, 1 - c)))
                if arrivals:
                    inc.append(_remote(theirs, theirs, send, recv, n * j + k, (x, y, 1 - c)))
        return out, inc

    def start(rins, routs, sems):
        for cp in copies(routs, sems, arrivals=False)[0]:
            cp.start()

    def finish(rins, routs, sems):
        out, inc = copies(routs, sems)
        for cp in inc:
            cp.wait_recv()
        for cp in out:
            cp.wait_send()

    return _Rider(slabs, [jax.ShapeDtypeStruct(a.shape, a.dtype) for a in slabs], [_sem(3 * n), _sem(3 * n)], start, finish,
                  aliases={k: k for k in range(n)})


def _rider_swap(parts):
    parts = tuple(parts)
    n = len(parts)

    def copies(rins, routs, sems):
        send, recv = sems
        x, y, c = _place()
        return [_remote(rins[k].at[:, _half(1 - c, parts[k].shape[1])], routs[k], send, recv, k, (x, y, 1 - c))
                for k in range(n)]

    def start(rins, routs, sems):
        for cp in copies(rins, routs, sems):
            cp.start()

    def finish(rins, routs, sems):
        for cp in copies(rins, routs, sems):
            cp.wait()

    return _Rider(parts, [jax.ShapeDtypeStruct((a.shape[0], a.shape[1] // 2, a.shape[2]), a.dtype) for a in parts],
                  [_sem(n), _sem(n)], start, finish)


def _rider_scatter(parts):
    parts = tuple(parts)
    n = len(parts)

    def copies(rins, routs, sems, arrivals=True):
        send, recv = sems
        x, y, c = _place()
        me = 2 * x + y
        out, inc = [], []
        for j, (px, py) in enumerate(_other_chips(x, y)):
            for k in range(n):
                out.append(_remote(rins[k].at[2 * px + py], routs[k].at[me], send, recv, n * j + k, (px, py, c)))
                if arrivals:
                    inc.append(_remote(rins[k].at[me], routs[k].at[2 * px + py], send, recv, n * j + k, (px, py, c)))
        return out, inc

    def start(rins, routs, sems):
        for cp in copies(rins, routs, sems, arrivals=False)[0]:
            cp.start()

    def finish(rins, routs, sems):
        out, inc = copies(rins, routs, sems)
        for cp in inc:
            cp.wait_recv()
        for cp in out:
            cp.wait_send()

    return _Rider(parts, [jax.ShapeDtypeStruct(a.shape, a.dtype) for a in parts], [_sem(3 * n), _sem(3 * n)], start, finish)


def _rider_share(fulls):
    fulls = tuple(fulls)
    n = len(fulls)

    def copies(routs, sems, arrivals=True):
        send, recv = sems
        x, y, c = _place()
        out, inc = [], []
        for k in range(n):
            mine, theirs = routs[k].at[_half(c, fulls[k].shape[0])], routs[k].at[_half(1 - c, fulls[k].shape[0])]
            out.append(_remote(mine, mine, send, recv, k, (x, y, 1 - c)))
            if arrivals:
                inc.append(_remote(theirs, theirs, send, recv, k, (x, y, 1 - c)))
        return out, inc

    def start(rins, routs, sems):
        for cp in copies(routs, sems, arrivals=False)[0]:
            cp.start()

    def finish(rins, routs, sems):
        out, inc = copies(routs, sems)
        for cp in inc:
            cp.wait_recv()
        for cp in out:
            cp.wait_send()

    return _Rider(fulls, [jax.ShapeDtypeStruct(a.shape, a.dtype) for a in fulls], [_sem(n), _sem(n)], start, finish,
                  aliases={k: k for k in range(n)})


def _pair_sum(core, full, recv, name, br=128):
    n, rows, cols = recv.shape

    def body(c_ref, a_ref, b_ref, o_ref):
        o_ref[...] = (a_ref[...] + b_ref[...]).astype(BF16)

    nb = rows // br
    return pl.pallas_call(
        body, name=name, out_shape=jax.ShapeDtypeStruct(recv.shape, BF16),
        grid_spec=pltpu.PrefetchScalarGridSpec(
            num_scalar_prefetch=1, grid=(n, nb),
            in_specs=[pl.BlockSpec((1, br, cols), lambda i, j, c: (i, c[0] * nb + j, 0)),
                      pl.BlockSpec((1, br, cols), lambda i, j, c: (i, j, 0))],
            out_specs=pl.BlockSpec((1, br, cols), lambda i, j, c: (i, j, 0))),
        compiler_params=_cparams(("parallel", "parallel")))(core, full, recv)


def _chip_sum(place, gathered, mine, name, br=128):
    _, r, c = gathered.shape
    nb = r // br

    def body(p_ref, g_ref, m_ref, o_ref):
        slab = lambda j: jnp.where(p_ref[1] == j, m_ref[j], g_ref[j]).astype(F32)
        o_ref[...] = ((slab(0) + slab(1)) + slab(2)) + slab(3)

    return pl.pallas_call(
        body, name=name, out_shape=jax.ShapeDtypeStruct((2 * r, c), F32),
        grid_spec=pltpu.PrefetchScalarGridSpec(
            num_scalar_prefetch=1, grid=(nb,),
            in_specs=[pl.BlockSpec((4, br, c), lambda i, p: (0, i, 0)), pl.BlockSpec((4, br, c), lambda i, p: (0, i, 0))],
            out_specs=pl.BlockSpec((br, c), lambda i, p: (p[0] * nb + i, 0))),
        compiler_params=_cparams(("parallel",)))(place, gathered, mine)


def _adamw(w, g, m, v, name, br):
    n, r, c = w.shape

    def body(w_ref, g_ref, m_ref, v_ref, d_ref, m2_ref, v2_ref):
        d_ref[...], m2_ref[...], v2_ref[...] = _adam_math(w_ref[...], g_ref[...], m_ref[...], v_ref[...])

    spec = pl.BlockSpec((1, br, c), lambda i, j: (i, j, 0))
    shp = jax.ShapeDtypeStruct(w.shape, F32)
    return pl.pallas_call(body, grid=(n, r // br), name=name, in_specs=[spec] * 4, out_specs=[spec] * 3,
                          out_shape=[shp] * 3, compiler_params=_cparams(("parallel", "parallel")))(w, g, m, v)


def _adamw_w_in(w, g, m, v, name, bc=93):
    cols = w.shape[2]
    lead = lambda a: jnp.transpose(a, (2, 0, 1))
    g = jnp.stack([a[:, 0:cols] for a in g])

    def body(w_ref, g_ref, m_ref, v_ref, go_ref, d_ref, m2_ref, v2_ref):
        for l in range(2):
            gv = g_ref[:, l, :]
            d_ref[:, l, :], m2_ref[:, l, :], v2_ref[:, l, :] = _adam_math(w_ref[:, l, :], gv, m_ref[:, l, :], v_ref[:, l, :])
            go_ref[:, l, :] = gv

    spec = pl.BlockSpec((bc, 2, D), lambda i: (i, 0, 0))
    outs = pl.pallas_call(body, grid=(cols // bc,), name=name, in_specs=[spec] * 4, out_specs=[spec] * 4,
                          out_shape=[jax.ShapeDtypeStruct((cols, 2, D), F32)] * 4,
                          compiler_params=_cparams(("parallel",)))(lead(w), lead(g), lead(m), lead(v))
    return [jnp.transpose(o, (1, 2, 0)) for o in outs]


_SMALL_NAMES = ("norm_w", "conv_a_w", "gla_gate_w", "gla_gate_b", "gla_norm_w", "pool_w", "pool_scale", "ssd_conv_w",
                "ssd_conv_b", "ssd_dt_bias", "ssd_a_log", "ssd_d", "ssd_norm_w", "final_norm_w")
SMALL_ROWS = 80


def _adam_math(w, g, m, v):
    m2 = ADAM_B1 * m + (1.0 - ADAM_B1) * g
    v2 = ADAM_B2 * v + (1.0 - ADAM_B2) * (g * g)
    m_hat = m2 / (1.0 - ADAM_B1 ** ADAM_STEP)
    v_hat = v2 / (1.0 - ADAM_B2 ** ADAM_STEP)
    return -ADAM_LR * (m_hat / (jnp.sqrt(v_hat) + ADAM_EPS) + ADAM_WD * w), m2, v2


def _small_slices(name, chip):
    if name == "conv_a_w":
        return [((), slice(R_CAW, R_CAW + 3), slice(64 * chip, 64 * chip + 64))]
    if name == "ssd_conv_w":
        return [((), slice(R_SCW, R_SCW + 4), slice(192 * chip, 192 * chip + 192))]
    if name == "gla_gate_w":
        return [((), slice(0, 16), slice(768, 896))]
    if name == "pool_w":
        return [((g, slice(16 * q, 16 * q + 16)), slice(16, 32), slice(256 * q + 64 * g, 256 * q + 64 * g + 64))
                for g in range(4) for q in range(4)]
    row, lanes = {"gla_gate_b": (R_GB, slice(0, 128)), "gla_norm_w": (R_GNW, slice(0, 64)),
                  "pool_scale": (R_PSC, slice(0, 256)), "ssd_conv_b": (R_SCB, slice(0, 768)),
                  "ssd_dt_bias": (R_DTB, slice(16, 20)), "ssd_a_log": (R_AE, slice(0, 4)), "ssd_d": (R_DE, slice(0, 4)),
                  "ssd_norm_w": (R_SNW, slice(0, 256))}[name]
    return [((), slice(row, row + 1), lanes)]


def _rider_exchange(block):
    def copies(rins, routs, sems):
        send, recv = sems
        x, y, c = _place()
        flip = lambda v, bit: 1 - v if bit else v
        return [_remote(rins[0], routs[0].at[k], send, recv, k - 1, (flip(x, k & 4), flip(y, k & 2), flip(c, k & 1)))
                for k in range(1, 8)]

    def start(rins, routs, sems):
        for cp in copies(rins, routs, sems):
            cp.start()

    def finish(rins, routs, sems):
        for cp in copies(rins, routs, sems):
            cp.wait()

    return _Rider((block,), [jax.ShapeDtypeStruct((8,) + block.shape, block.dtype)], [_sem(7), _sem(7)], start, finish)


def _join_riders(a, b):
    na, oa, sa = len(a.inputs), len(a.out_shapes), len(a.sems)

    def start(rins, routs, sems):
        a.start(rins[:na], routs[:oa], sems[:sa])
        b.start(rins[na:], routs[oa:], sems[sa:])

    def finish(rins, routs, sems):
        a.finish(rins[:na], routs[:oa], sems[:sa])
        b.finish(rins[na:], routs[oa:], sems[sa:])

    aliases = {**a.aliases, **{na + k: oa + v for k, v in b.aliases.items()}}
    return _Rider(a.inputs + b.inputs, a.out_shapes + b.out_shapes, a.sems + b.sems, start, finish, aliases)


def _small_adamw(blocks, w, m, v):
    n = len(_SMALL_NAMES)

    def body(*refs):
        (own, ex), (own0, ex0) = refs[0:2], refs[2:4]
        refs = refs[3:]
        w_refs, m_refs, v_refs = refs[1:1 + n], refs[1 + n:1 + 2 * n], refs[1 + 2 * n:1 + 3 * n]
        o = 1 + 3 * n
        g_out, d_out, m_out, v_out = refs[o:o + n], refs[o + n:o + 2 * n], refs[o + 2 * n:o + 3 * n], refs[o + 3 * n:o + 4 * n]
        loss_ref, acc, acc0 = refs[o + 4 * n:o + 4 * n + 3]
        chip = 2 * lax.axis_index("x") + lax.axis_index("y")
        me = 2 * chip + lax.axis_index("c")
        acc[...] = jnp.zeros_like(acc)
        acc0[...] = jnp.zeros_like(acc0)
        for src in range(8):
            @pl.when(me == src)
            def _():
                acc[...] += own[...]
                acc0[...] += own0[...]

            @pl.when(me != src)
            def _(src=src):
                acc[...] += ex[jnp.bitwise_xor(me, src)]
                acc0[...] += ex0[jnp.bitwise_xor(me, src)]

        loss_ref[...] = acc[73:74, 0:1]

        def update(i, idx, g):
            d, m2, v2 = _adam_math(w_refs[i][idx], g, m_refs[i][idx], v_refs[i][idx])
            g_out[i][idx], d_out[i][idx], m_out[i][idx], v_out[i][idx] = g, d, m2, v2

        for i, name in enumerate(_SMALL_NAMES):
            if name == "final_norm_w":
                update(i, (slice(0, 1), slice(None)), acc[72:73, :])
            elif name == "norm_w":
                update(i, (slice(0, 1), slice(None)), acc0[0:1, :])
                update(i, (slice(1, 2), slice(None)), acc[64:65, :])
            elif name in ("conv_a_w", "ssd_conv_w"):
                for s in range(4):
                    @pl.when(chip == s)
                    def _(i=i, name=name, s=s):
                        for l in range(2):
                            (_, rows, lanes), = _small_slices(name, s)
                            update(i, (l,), acc[rows.start + 32 * l:rows.stop + 32 * l, lanes])
            else:
                for l in range(2):
                    for idx, rows, lanes in _small_slices(name, 0):
                        g = acc[rows.start + 32 * l:rows.stop + 32 * l, lanes]
                        if w_refs[i].ndim == 2:
                            update(i, (slice(l, l + 1), slice(None)), g)
                        else:
                            update(i, (l,) + idx, g)

    args = [a for pair in blocks for a in pair] + [d[k] for d in (w, m, v) for k in _SMALL_NAMES]
    shapes = [jax.ShapeDtypeStruct(w[k].shape, F32) for k in _SMALL_NAMES]
    vmem = pl.BlockSpec(memory_space=pltpu.VMEM)
    outs = pl.pallas_call(body, name="small_adamw", in_specs=[vmem] * len(args), out_specs=[vmem] * (4 * n + 1),
                          out_shape=shapes * 4 + [jax.ShapeDtypeStruct((1, 1), F32)],
                          scratch_shapes=[pltpu.VMEM((SMALL_ROWS, D), F32), pltpu.VMEM((8, D), F32)])(*args)
    return outs[0:n], outs[n:2 * n], outs[2 * n:3 * n], outs[3 * n:4 * n], outs[4 * n]


def _mixer_consts(layer, conv_a_w, gla_gate_w, gla_gate_b, gla_norm_w, pool_w, pool_scale, ssd_conv_w, ssd_conv_b,
                  ssd_dt_bias, ssd_a_log, ssd_d, ssd_norm_w):
    def row(v):
        return jnp.pad(v.reshape(1, -1), ((0, 0), (0, 768 - v.size)))

    dtb = jnp.pad(ssd_dt_bias[layer], (16, 108))
    rows = [jnp.pad(conv_a_w[layer], ((0, 0), (0, 512))), row(gla_gate_b[layer]), row(jnp.tile(gla_norm_w[layer], 4)),
            row(pool_scale[layer]), row(ssd_conv_b[layer]), row(dtb), row(jnp.repeat(-jnp.exp(ssd_a_log[layer]), 64)),
            row(jnp.repeat(ssd_d[layer], 64)), row(ssd_norm_w[layer]), jnp.zeros((1, 768), F32), ssd_conv_w[layer]]
    prm = jnp.concatenate(rows, axis=0)
    gw = jnp.pad(gla_gate_w[layer], ((0, 112), (0, 0))).astype(BF16)
    on_diag = (_iota((256, 256), 0) >> 6) == (_iota((256, 256), 1) >> 6)
    pw = jnp.where(on_diag, jnp.tile(pool_w[layer].reshape(256, 64), (1, 4)), 0.0)
    return (prm, gw, pw.astype(BF16)) + _mixer_matrices()


def _grad_slabs(dwp, dwo):
    return dwp.reshape(1, D, NP), dwo.reshape(4, D // 4, D)


class _Comm:
    def __init__(self, w_in, w_out):
        self.w_in16 = jnp.pad(w_in.astype(BF16), ((0, 0), (0, 0), (0, SHARD_PAD - SHARD)))
        self.w_out16 = w_out.astype(BF16)
        self.core = lax.axis_index("c").astype(jnp.int32).reshape(1)
        self.chip = 2 * lax.axis_index("x") + lax.axis_index("y")
        self.place = jnp.stack([lax.axis_index("c"), self.chip]).astype(jnp.int32)

    def gather_ici(self, layer):
        return _rider_gather_ici((self.w_in16[layer], self.w_out16[layer]))

    def pair_sum(self, layer, slabs, received):
        d_in, d_out = [_pair_sum(self.core, a, b, name=f"reduce_pair_sum{layer}_{k}")
                       for k, (a, b) in enumerate(zip(slabs, received))]
        return [_split_dw_in(d_in[0], name=f"split_dw_in{layer}"), d_out]

    def chip_sum(self, layer, gathered, mine):
        return [_chip_sum(self.place, a, b, name=f"reduce_chip_sum{layer}_{k}") for k, (a, b) in enumerate(zip(gathered, mine))]

    def layer_weights(self, layer, s_in, s_out):
        own = lambda slabs, shard: jnp.stack([jnp.where(self.chip == s, shard, slabs[s]) for s in range(4)])
        wp, wpt = _assemble_w_in(own(s_in, self.w_in16[layer]), name=f"assemble_w_in{layer}")
        wo = own(s_out, self.w_out16[layer]).reshape(D, D)
        return wp, wpt, wo, wo.T


def _local_step(x, tgt, norm_w, final_norm_w, consts, wts0, wts1=None, comm=None):
    nw = [norm_w[l:l + 1] for l in range(2)]
    proj0, h0, slabs = _rmsproj(x, nw[0], wts0[0], name="rmsproj0", rider=comm and comm.gather_ici(1))
    (mix0, sg0, ss0, x1, *conv0), slabs = _mixer_fwd(proj0, x, wts0[2], *consts[0], name="mixer_fwd0",
                                                     rider=comm and _rider_gather_d2d(slabs))
    if comm:
        wts1 = comm.layer_weights(1, *slabs)
    proj1, h1, _ = _rmsproj(x1, nw[1], wts1[0], name="rmsproj1")
    (mix1, sg1, ss1, dx, *conv1, head), _ = _mixer_fwd(proj1, x1, wts1[2], *consts[1], name="mixer_fwd1",
                                                       head=(tgt, final_norm_w.reshape(1, D)))
    (dproj, mgr1, dwo1), _ = _mixer_bwd(proj1, dx, wts1[3], mix1, sg1, ss1, *conv1, *consts[1], name="mixer_bwd1")
    dwp1, _ = _dwin(h1, dproj, name="dwin1")
    slabs1 = comm and _grad_slabs(dwp1, dwo1)
    (dx, dnw1), recv = _dxin(dproj, wts1[1], x1, dx, nw[1], name="dxin1", rider=comm and _rider_swap(slabs1))
    pairs1 = comm and comm.pair_sum(1, slabs1, recv)
    (dproj, mgr0, dwo0), gathered = _mixer_bwd(proj0, dx, wts0[3], mix0, sg0, ss0, *conv0, *consts[0], name="mixer_bwd0",
                                               rider=comm and _rider_scatter(pairs1))
    dwp0, big1 = _dwin(h0, dproj, name="dwin0", rider=comm and _rider_share(comm.chip_sum(1, gathered, pairs1)))
    if not comm:
        (dx, dnw0), _ = _dxin(dproj, wts0[1], x, dx, nw[0], name="dxin0")
        return head, dx, ((dwp0, dwp1), (dwo0, dwo1)), (dnw0, dnw1), (mgr0, mgr1)
    slabs0 = _grad_slabs(dwp0, dwo0)
    pairs0 = comm.pair_sum(0, slabs0, _run_rider(_rider_swap(slabs0), "reduce_swap0"))
    small = jnp.concatenate([mgr0, mgr1, dnw1, head], axis=0)
    (dx, dnw0), gathered = _dxin(dproj, wts0[1], x, dx, nw[0], name="dxin0",
                                 rider=_join_riders(_rider_scatter(pairs0), _rider_exchange(small)))
    last = _run_rider(_join_riders(_rider_share(comm.chip_sum(0, gathered[0:2], pairs0)), _rider_exchange(dnw0)),
                      "reduce_share0")
    return dx, ((last[0], big1[0]), (last[1], big1[1])), ((small, gathered[2]), (dnw0, last[2]))


def kernel(x, norm_w, w_in, conv_a_w, gla_gate_w, gla_gate_b, gla_norm_w, pool_w, pool_scale, ssd_conv_w, ssd_conv_b, ssd_dt_bias, ssd_a_log, ssd_d, ssd_norm_w, w_out, final_norm_w, loss_target, m_norm_w, m_w_in, m_conv_a_w, m_gla_gate_w, m_gla_gate_b, m_gla_norm_w, m_pool_w, m_pool_scale, m_ssd_conv_w, m_ssd_conv_b, m_ssd_dt_bias, m_ssd_a_log, m_ssd_d, m_ssd_norm_w, m_w_out, m_final_norm_w, v_norm_w, v_w_in, v_conv_a_w, v_gla_gate_w, v_gla_gate_b, v_gla_norm_w, v_pool_w, v_pool_scale, v_ssd_conv_w, v_ssd_conv_b, v_ssd_dt_bias, v_ssd_a_log, v_ssd_d, v_ssd_norm_w, v_w_out, v_final_norm_w):
    weights = dict(norm_w=norm_w, w_in=w_in, conv_a_w=conv_a_w, gla_gate_w=gla_gate_w, gla_gate_b=gla_gate_b,
                   gla_norm_w=gla_norm_w, pool_w=pool_w, pool_scale=pool_scale, ssd_conv_w=ssd_conv_w,
                   ssd_conv_b=ssd_conv_b, ssd_dt_bias=ssd_dt_bias, ssd_a_log=ssd_a_log, ssd_d=ssd_d,
                   ssd_norm_w=ssd_norm_w, w_out=w_out, final_norm_w=final_norm_w)
    m_in = dict(norm_w=m_norm_w, w_in=m_w_in, conv_a_w=m_conv_a_w, gla_gate_w=m_gla_gate_w, gla_gate_b=m_gla_gate_b,
                gla_norm_w=m_gla_norm_w, pool_w=m_pool_w, pool_scale=m_pool_scale, ssd_conv_w=m_ssd_conv_w,
                ssd_conv_b=m_ssd_conv_b, ssd_dt_bias=m_ssd_dt_bias, ssd_a_log=m_ssd_a_log, ssd_d=m_ssd_d,
                ssd_norm_w=m_ssd_norm_w, w_out=m_w_out, final_norm_w=m_final_norm_w)
    v_in = dict(norm_w=v_norm_w, w_in=v_w_in, conv_a_w=v_conv_a_w, gla_gate_w=v_gla_gate_w, gla_gate_b=v_gla_gate_b,
                gla_norm_w=v_gla_norm_w, pool_w=v_pool_w, pool_scale=v_pool_scale, ssd_conv_w=v_ssd_conv_w,
                ssd_conv_b=v_ssd_conv_b, ssd_dt_bias=v_ssd_dt_bias, ssd_a_log=v_ssd_a_log, ssd_d=v_ssd_d,
                ssd_norm_w=v_ssd_norm_w, w_out=v_w_out, final_norm_w=v_final_norm_w)
    order = ("norm_w", "w_in", "conv_a_w", "gla_gate_w", "gla_gate_b", "gla_norm_w", "pool_w", "pool_scale",
             "ssd_conv_w", "ssd_conv_b", "ssd_dt_bias", "ssd_a_log", "ssd_d", "ssd_norm_w", "w_out", "final_norm_w")
    t = x.shape[1]

    comm = _Comm(w_in, w_out)
    cshard = jnp.zeros((16, 256), F32)
    for l in range(2):
        cshard = cshard.at[8 * l:8 * l + 3, 0:64].set(conv_a_w[l]).at[8 * l + 3:8 * l + 7, 0:192].set(ssd_conv_w[l])
    s_in, s_out, g_c = _gather_ici_two_hops((comm.w_in16[0], comm.w_out16[0]), cshard)
    s_in, s_out = _run_rider(_rider_gather_d2d((s_in, s_out)), "gather_d2d0")
    g_c = [jnp.where(comm.chip == s, cshard, g_c[s]) for s in range(4)]
    conv_a_full = jnp.stack([jnp.concatenate([g_c[s][8 * l:8 * l + 3, 0:64] for s in range(4)], axis=-1) for l in range(2)])
    ssd_conv_full = jnp.stack([jnp.concatenate([g_c[s][8 * l + 3:8 * l + 7, 0:192] for s in range(4)], axis=-1)
                               for l in range(2)])
    consts = [_mixer_consts(l, conv_a_full, gla_gate_w, gla_gate_b, gla_norm_w, pool_w, pool_scale, ssd_conv_full,
                            ssd_conv_b, ssd_dt_bias, ssd_a_log, ssd_d, ssd_norm_w) for l in range(2)]

    dx, big, blocks = _local_step(x.reshape(t, D), loss_target.reshape(t, D), norm_w, final_norm_w, consts,
                                  comm.layer_weights(0, s_in, s_out), comm=comm)

    as2d = lambda d: {k: (d[k].reshape(1, D) if k == "final_norm_w" else d[k]) for k in _SMALL_NAMES}
    small = _small_adamw(blocks, as2d(weights), as2d(m_in), as2d(v_in))
    grads, delta, new_m, new_v = ({k: (a.reshape(D) if k == "final_norm_w" else a) for k, a in zip(_SMALL_NAMES, part)}
                                  for part in small[0:4])
    loss = small[4].reshape(())

    grads["w_out"] = jnp.stack(big[1])

    grads["w_in"], delta["w_in"], new_m["w_in"], new_v["w_in"] = _adamw_w_in(w_in, big[0], m_w_in, v_w_in, name="adamw_w_in")
    delta["w_out"], new_m["w_out"], new_v["w_out"] = _adamw(w_out, grads["w_out"], m_w_out, v_w_out, name="adamw_w_out", br=256)

    return (loss, dx.reshape(1, t, D), *[grads[k] for k in order], *[delta[k] for k in order],
            *[new_m[k] for k in order], *[new_v[k] for k in order])
```

```python
import functools

import jax
import jax.numpy as jnp
from jax import lax
from jax.experimental import pallas as pl
from jax.experimental.pallas import tpu as pltpu

F32 = jnp.float32
BF16 = jnp.bfloat16
MESH = pl.DeviceIdType.MESH

D = 1024
CH = 64
EPS = 1e-6
NP = 3456
NPROJ = 3348
NPM = 3328
GLA_SCALE = 32.0 ** -0.5
INV_TAU = 1.0 / 16.0
TB = 512
NCH = TB // CH
assert TB % 256 == 0

C_AH, C_AB, C_AC, C_AZ, C_GQ, C_GK, C_GV = 0, 256, 512, 768, 1024, 1152, 1280
C_GZ, C_PU, C_PZ, C_SZ, C_SX, C_TL = 1536, 1792, 2048, 2304, 2560, 3328
_PERM = ((0, 1536), (1552, 1792), (1536, 16), (3344, 4))

R_CAW, R_GB, R_GNW, R_PSC, R_SCB, R_DTB, R_AE, R_DE, R_SNW, R_SCW = 0, 3, 4, 5, 6, 7, 8, 9, 10, 12

ADAM_LR, ADAM_B1, ADAM_B2, ADAM_EPS, ADAM_WD, ADAM_STEP = 0.001, 0.9, 0.999, 1e-08, 0.01, 10

VMEM_LIMIT = 56 * 1024 * 1024


def _cparams(sem, limit=VMEM_LIMIT):
    return pltpu.CompilerParams(dimension_semantics=sem, vmem_limit_bytes=limit)


_ANY = pl.BlockSpec(memory_space=pl.ANY)


def _place():
    return lax.axis_index("x"), lax.axis_index("y"), lax.axis_index("c")


class _Rider:
    def __init__(self, inputs, out_shapes, sems, start, finish, aliases=None):
        self.inputs, self.out_shapes, self.sems = tuple(inputs), tuple(out_shapes), tuple(sems)
        self.start, self.finish, self.aliases = start, finish, dict(aliases or {})


def _call(body, args, *, grid, in_specs, out_specs, out_shape, name, sem, scratch_shapes=(), rider=None):
    if rider is None:
        outs = pl.pallas_call(body, grid=grid, name=name, in_specs=list(in_specs), out_specs=list(out_specs),
                              out_shape=list(out_shape), scratch_shapes=list(scratch_shapes),
                              compiler_params=_cparams(sem))(*args)
        return list(outs), []
    ni, no, ns = len(args), len(out_shape), len(scratch_shapes)
    ri, ro = len(rider.inputs), len(rider.out_shapes)

    def full(*refs):
        ins, rins = refs[:ni], refs[ni:ni + ri]
        outs, routs = refs[ni + ri:ni + ri + no], refs[ni + ri + no:ni + ri + no + ro]
        scr, rsem = refs[ni + ri + no + ro:ni + ri + no + ro + ns], refs[ni + ri + no + ro + ns:]
        first = functools.reduce(jnp.logical_and, [pl.program_id(a) == 0 for a in range(len(grid))])
        last = functools.reduce(jnp.logical_and, [pl.program_id(a) == grid[a] - 1 for a in range(len(grid))])

        @pl.when(first)
        def _():
            rider.start(rins, routs, rsem)

        body(*ins, *outs, *scr)

        @pl.when(last)
        def _():
            rider.finish(rins, routs, rsem)

    outs = pl.pallas_call(
        full, grid=grid, name=name, in_specs=list(in_specs) + [_ANY] * ri, out_specs=list(out_specs) + [_ANY] * ro,
        out_shape=list(out_shape) + list(rider.out_shapes), scratch_shapes=list(scratch_shapes) + list(rider.sems),
        input_output_aliases={ni + k: no + v for k, v in rider.aliases.items()},
        compiler_params=_cparams(("arbitrary",) * len(grid)))(*args, *rider.inputs)
    return list(outs[:no]), list(outs[no:])


def _run_rider(rider, name):
    ri = len(rider.inputs)

    def body(*refs):
        rins, routs, rsem = refs[:ri], refs[ri:ri + len(rider.out_shapes)], refs[ri + len(rider.out_shapes):]
        rider.start(rins, routs, rsem)
        rider.finish(rins, routs, rsem)

    return list(pl.pallas_call(body, name=name, in_specs=[_ANY] * ri, out_specs=[_ANY] * len(rider.out_shapes),
                               out_shape=list(rider.out_shapes), scratch_shapes=list(rider.sems),
                               input_output_aliases=dict(rider.aliases))(*rider.inputs))


def _dot(a, b):
    return jnp.dot(a.astype(BF16), b.astype(BF16), preferred_element_type=F32)


def _dot_nt(a, b):
    return lax.dot_general(a.astype(BF16), b.astype(BF16), (((1,), (1,)), ((), ())), preferred_element_type=F32)


def _dot_tn(a, b):
    return lax.dot_general(a.astype(BF16), b.astype(BF16), (((0,), (0,)), ((), ())), preferred_element_type=F32)


def _split(a):
    hi = a.astype(BF16)
    lo = (a - hi.astype(F32)).astype(BF16)
    return hi, lo


def _dot2_l(a, b):
    hi, lo = _split(a)
    return _dot(hi, b) + _dot(lo, b)


def _dot2_r(a, b):
    hi, lo = _split(b)
    return _dot(a, hi) + _dot(a, lo)


def _dot3_l(a, b):
    hi, lo = _split(a)
    lo2 = ((a - hi.astype(F32)) - lo.astype(F32)).astype(BF16)
    return _dot(hi, b) + _dot(lo, b) + _dot(lo2, b)


def _dot2_nt(a, b):
    hi, lo = _split(a)
    return _dot_nt(hi, b) + _dot_nt(lo, b)


def _silu(z):
    return z * jax.nn.sigmoid(z)


def _lse1(x):
    return jnp.log(1.0 + jnp.exp(-jnp.abs(x)))


def _cs(a):
    return jnp.sum(a, axis=0, keepdims=True)


def _iota(shape, dim):
    return lax.broadcasted_iota(jnp.int32, shape, dim)


def _mixer_matrices():
    r, c = _iota((256, 256), 0), _iota((256, 256), 1)
    same_chunk = (r >> 6) == (c >> 6)
    mats = jnp.stack([jnp.where((c > r) & same_chunk, 1.0, 0.0), jnp.where((c < r) & same_chunk, 1.0, 0.0),
                      jnp.where(same_chunk, 1.0 / 64.0, 0.0), jnp.where((r < 128) & (r - 16 == (c >> 6)), 1.0, 0.0)])
    mask = jnp.where((_iota((256, 128), 0) >> 6) == (_iota((256, 128), 1) >> 5), 1.0, 0.0)
    return mats.astype(BF16), mask.astype(F32)


def _dn(ext, k, n, h):
    return pltpu.roll(ext, k, axis=0)[h:h + n]


def _up(ext, k, n):
    return pltpu.roll(ext, ext.shape[0] - k, axis=0)[:n]


def _pool_lane_select(lane, s2, s4, s8, s16):
    return jnp.where(lane < 64, s2, jnp.where(lane < 128, s4, jnp.where(lane < 192, s8, s16)))


def _winsum_dn(ext, lane):
    s2 = ext + pltpu.roll(ext, 1, axis=0)
    s4 = s2 + pltpu.roll(s2, 2, axis=0)
    s8 = s4 + pltpu.roll(s4, 4, axis=0)
    s16 = s8 + pltpu.roll(s8, 8, axis=0)
    return _pool_lane_select(lane, s2, s4, s8, s16)


def _winsum_up(ext, lane):
    m = ext.shape[0]
    s2 = ext + pltpu.roll(ext, m - 1, axis=0)
    s4 = s2 + pltpu.roll(s2, m - 2, axis=0)
    s8 = s4 + pltpu.roll(s4, m - 4, axis=0)
    s16 = s8 + pltpu.roll(s8, m - 8, axis=0)
    return _pool_lane_select(lane, s2, s4, s8, s16)


def _pool_inv_count(tile, n):
    lane = _iota((1, 256), 1)
    win = _pool_lane_select(lane, 2.0, 4.0, 8.0, 16.0).astype(F32)
    tpos = (tile * n + _iota((n, 1), 0) + 1).astype(F32)
    return jnp.where(tpos >= win, 1.0 / win, 1.0 / tpos)


def _silu_pair(z):
    s = jax.nn.sigmoid(z)
    return z * s, s * (1.0 + z * (1.0 - s))


def _chunks(a):
    return [a[c * CH:(c + 1) * CH] for c in range(a.shape[0] // CH)]


def _halves(fn, a, b):
    return jnp.concatenate([fn(a[:, 0:128], b[:, 0:128]), fn(a[:, 128:256], b[:, 128:256])], axis=1)


def _chunk_sums(tri, a):
    return jnp.concatenate([_dot2_r(tri, a[r:r + 256]) for r in range(0, a.shape[0], 256)], axis=0)


def _mixer_tile_prep(p_ref, t_ref, xc, prm_ref, gw_v, cm_ref, mk_ref):
    tail = t_ref[...]
    pre = _dot(tail, gw_v) + prm_ref[R_GB:R_GB + 1, 0:128]
    la = (jnp.minimum(pre, 0.0) - _lse1(pre)) * INV_TAU
    dtin = tail + prm_ref[R_DTB:R_DTB + 1, 0:128]
    dtf = jnp.maximum(dtin, 0.0) + _lse1(dtin)
    dte = _dot2_l(dtf, cm_ref[3, 0:128, :])
    da = dte * prm_ref[R_AE:R_AE + 1, 0:256]
    rev = _chunk_sums(cm_ref[0], jnp.concatenate([la, da], axis=1))
    dec = jnp.exp(rev[:, 0:128])
    kd = p_ref[:, C_GK:C_GK + 128].astype(F32) * dec
    wdec = jnp.exp(rev[:, 128:384])
    w = wdec * dte
    xw = xc[:, 0:256] * w
    d_s = [jnp.exp(_cs(a)) for a in _chunks(la)]
    et = [jnp.exp(_cs(a)) for a in _chunks(da)]
    mask_t = mk_ref[...]
    ut_g = [_dot_tn(v, k) * mask_t for v, k in zip(_chunks(p_ref[:, C_GV:C_GV + 256].astype(F32)), _chunks(kd))]
    ut_s = [_halves(_dot_tn, b, x) for b, x in zip(_chunks(xc[:, 256:512]), _chunks(xw))]
    return tail, pre, dtin, dte, dec, kd, wdec, w, xw, d_s, et, ut_g, ut_s


def _rmsproj(x, nw, wp, name, tm=512, rider=None):
    t = x.shape[0]

    def body(x_ref, nw_ref, w_ref, o_ref, t_ref, h_ref):
        xv = x_ref[...]
        rs = lax.rsqrt(jnp.mean(xv * xv, axis=-1, keepdims=True) + EPS)
        h = (xv * rs * nw_ref[...]).astype(BF16)
        h_ref[...] = h
        proj = jnp.dot(h, w_ref[...], preferred_element_type=F32)
        o_ref[...] = proj[:, 0:NPM].astype(BF16)
        t_ref[...] = proj[:, NPM:NP]

    (proj, tail, h), extra = _call(
        body, (x, nw, wp), grid=(t // tm,), name=name, sem=("parallel",), rider=rider,
        in_specs=[pl.BlockSpec((tm, D), lambda i: (i, 0)), pl.BlockSpec((1, D), lambda i: (0, 0)),
                  pl.BlockSpec((D, NP), lambda i: (0, 0))],
        out_specs=[pl.BlockSpec((tm, NPM), lambda i: (i, 0)), pl.BlockSpec((tm, NP - NPM), lambda i: (i, 0)),
                   pl.BlockSpec((tm, D), lambda i: (i, 0))],
        out_shape=[jax.ShapeDtypeStruct((t, NPM), BF16), jax.ShapeDtypeStruct((t, NP - NPM), F32),
                   jax.ShapeDtypeStruct((t, D), BF16)])
    return (proj, tail), h, extra


def _head_tile(xv, tgt, w):
    rs = lax.rsqrt(jnp.mean(xv * xv, axis=-1, keepdims=True) + EPS)
    xh = xv * rs
    err = xh * w - tgt
    dy = err * (1.0 / D)
    dxh = dy * w
    dx = rs * (dxh - xh * jnp.mean(dxh * xh, axis=-1, keepdims=True))
    return dx, _cs(dy * xh), (0.5 / D) * jnp.sum(err * err)


def _dxin(dp, wpt, x, dxn, nw, name, tm=512, rider=None):
    t = x.shape[0]

    def body(dp_ref, w_ref, x_ref, dxn_ref, nw_ref, dx_ref, dnw_ref):
        @pl.when(pl.program_id(0) == 0)
        def _():
            dnw_ref[...] = jnp.zeros_like(dnw_ref)

        dh = jnp.dot(dp_ref[...], w_ref[...], preferred_element_type=F32)
        xv = x_ref[...]
        rs = lax.rsqrt(jnp.mean(xv * xv, axis=-1, keepdims=True) + EPS)
        xh = xv * rs
        dnw_ref[0:1, :] += _cs(dh * xh)
        dxh = dh * nw_ref[...]
        dx_ref[...] = dxn_ref[...] + rs * (dxh - xh * jnp.mean(dxh * xh, axis=-1, keepdims=True))

    return _call(
        body, (dp, wpt, x, dxn, nw), grid=(t // tm,), name=name, sem=("arbitrary",), rider=rider,
        in_specs=[pl.BlockSpec((tm, NP), lambda i: (i, 0)), pl.BlockSpec((NP, D), lambda i: (0, 0)),
                  pl.BlockSpec((tm, D), lambda i: (i, 0)), pl.BlockSpec((tm, D), lambda i: (i, 0)),
                  pl.BlockSpec((1, D), lambda i: (0, 0))],
        out_specs=[pl.BlockSpec((tm, D), lambda i: (i, 0)), pl.BlockSpec((8, D), lambda i: (0, 0))],
        out_shape=[jax.ShapeDtypeStruct((t, D), F32), jax.ShapeDtypeStruct((8, D), F32)])


def _dwin(h, dp, name, tm=1024, rider=None):
    t = h.shape[0]

    def body(h_ref, dp_ref, o_ref):
        @pl.when(pl.program_id(0) == 0)
        def _():
            o_ref[...] = jnp.zeros_like(o_ref)

        o_ref[...] += _dot_tn(h_ref[...], dp_ref[...])

    (dwp,), extra = _call(
        body, (h, dp), grid=(t // tm,), name=name, sem=("arbitrary",), rider=rider,
        in_specs=[pl.BlockSpec((tm, D), lambda i: (i, 0)), pl.BlockSpec((tm, NP), lambda i: (i, 0))],
        out_specs=[pl.BlockSpec((D, NP), lambda i: (0, 0))], out_shape=[jax.ShapeDtypeStruct((D, NP), F32)])
    return dwp, extra


def _mixer_fwd(proj, x, wo, prm, gw, pw, cmat, mask, name, rider=None, head=None):
    proj, tail = proj
    t = proj.shape[0]
    nt, nc = t // TB, t // CH

    def body(p_ref, t_ref, x_ref, wo_ref, prm_ref, gw_ref, pw_ref, cm_ref, mk_ref, *rest):
        (tgt_ref, fw_ref), rest = (rest[:2], rest[2:]) if head else ((None, None), rest)
        mix_ref, sg_ref, ss_ref, xn_ref, xc_ref, dxc_ref, cv_ref, pool_ref = rest[:8]
        acc_ref = rest[8] if head else None
        sg_s, ss_s, h_ua, h_pu, h_sx = rest[-5:]
        i = pl.program_id(0)

        @pl.when(i == 0)
        def _():
            for r in (sg_s, ss_s, h_ua, h_pu, h_sx) + ((acc_ref,) if head else ()):
                r[...] = jnp.zeros_like(r)

        lane = _iota((1, 256), 1)
        u = p_ref[:, C_AC:C_AC + 256].astype(F32) * p_ref[:, C_AH:C_AH + 256].astype(F32)
        ext = jnp.concatenate([h_ua[...], u], axis=0)
        cv = (prm_ref[R_CAW + 2:R_CAW + 3, 0:256] * u + prm_ref[R_CAW + 1:R_CAW + 2, 0:256] * _dn(ext, 1, TB, 8)
              + prm_ref[R_CAW:R_CAW + 1, 0:256] * _dn(ext, 2, TB, 8))
        cv_ref[...] = cv.astype(BF16)
        mix_ref[:, 0:256] = (p_ref[:, C_AB:C_AB + 256].astype(F32) * cv * _silu(p_ref[:, C_AZ:C_AZ + 256].astype(F32))).astype(BF16)
        h_ua[...] = u[TB - 8:, :]
        pu = p_ref[:, C_PU:C_PU + 256].astype(F32)
        ext = jnp.concatenate([h_pu[...], pu], axis=0)
        pooled = (_winsum_dn(ext, lane)[16:] * _pool_inv_count(i, TB) - pu).astype(BF16)
        pool_ref[...] = pooled
        mixed = jnp.dot(pooled, pw_ref[...], preferred_element_type=F32)
        mix_ref[:, 512:768] = (prm_ref[R_PSC:R_PSC + 1, 0:256] * mixed * _silu(p_ref[:, C_PZ:C_PZ + 256].astype(F32))).astype(BF16)
        h_pu[...] = pu[TB - 16:, :]
        sx = p_ref[:, C_SX:C_SX + 768].astype(F32)
        ext = jnp.concatenate([h_sx[...], sx], axis=0)
        xc, dxc = _silu_pair(prm_ref[R_SCW + 3:R_SCW + 4, :] * sx + prm_ref[R_SCW + 2:R_SCW + 3, :] * _dn(ext, 1, TB, 8)
                             + prm_ref[R_SCW + 1:R_SCW + 2, :] * _dn(ext, 2, TB, 8)
                             + prm_ref[R_SCW:R_SCW + 1, :] * _dn(ext, 3, TB, 8) + prm_ref[R_SCB:R_SCB + 1, :])
        xc_ref[...] = xc.astype(BF16)
        dxc_ref[...] = dxc.astype(BF16)
        h_sx[...] = sx[TB - 8:, :]

        _, _, _, _, _, _, _, _, _, d_s, et, ut_g, ut_s = _mixer_tile_prep(p_ref, t_ref, xc, prm_ref, gw_ref[...], cm_ref, mk_ref)
        s_g, s_s = sg_s[...], ss_s[...]
        o, y = [], []
        qs = _chunks(p_ref[:, C_GQ:C_GQ + 128].astype(F32) * GLA_SCALE)
        cm = _chunks(xc[:, 512:768])
        for c in range(NCH):
            sg_ref[c] = s_g
            ss_ref[c] = s_s
            s_g = s_g * d_s[c] + ut_g[c]
            s_s = s_s * et[c] + ut_s[c]
            o.append(_dot_nt(qs[c], s_g))
            y.append(_halves(_dot, cm[c], s_s))
        sg_s[...] = s_g
        ss_s[...] = s_s
        o = jnp.concatenate(o, axis=0)
        on = o * lax.rsqrt(_dot2_l(o * o, cm_ref[2]) + EPS)
        mix_ref[:, 256:512] = (on * prm_ref[R_GNW:R_GNW + 1, 0:256] * _silu(p_ref[:, C_GZ:C_GZ + 256].astype(F32))).astype(BF16)
        y2 = ((jnp.concatenate(y, axis=0) + prm_ref[R_DE:R_DE + 1, 0:256] * xc[:, 0:256])
              * _silu(p_ref[:, C_SZ:C_SZ + 256].astype(F32)))
        mix_ref[:, 768:1024] = (y2 * lax.rsqrt(jnp.mean(y2 * y2, axis=-1, keepdims=True) + EPS)
                                * prm_ref[R_SNW:R_SNW + 1, 0:256]).astype(BF16)
        xn = x_ref[...] + jnp.dot(mix_ref[...], wo_ref[...], preferred_element_type=F32)
        if head:
            xn_ref[...], dfw, loss = _head_tile(xn, tgt_ref[...], fw_ref[...])
            acc_ref[0:1, :] += dfw
            acc_ref[1:2, :] += jnp.zeros((1, D), F32) + loss
        else:
            xn_ref[...] = xn

    row = pl.BlockSpec((TB, D), lambda i: (i, 0))
    return _call(
        body, (proj, tail, x, wo, prm, gw, pw, cmat, mask) + tuple(head or ()), grid=(nt,), name=name, sem=("arbitrary",),
        rider=rider,
        in_specs=[pl.BlockSpec((TB, NPM), lambda i: (i, 0)), pl.BlockSpec((TB, NP - NPM), lambda i: (i, 0)), row,
                  pl.BlockSpec((D, D), lambda i: (0, 0)), pl.BlockSpec((16, 768), lambda i: (0, 0)),
                  pl.BlockSpec((128, 128), lambda i: (0, 0)), pl.BlockSpec((256, 256), lambda i: (0, 0)),
                  pl.BlockSpec((4, 256, 256), lambda i: (0, 0, 0)), pl.BlockSpec((256, 128), lambda i: (0, 0))]
        + ([row, pl.BlockSpec((1, D), lambda i: (0, 0))] if head else []),
        out_specs=[row, pl.BlockSpec((NCH, 256, 128), lambda i: (i, 0, 0)),
                   pl.BlockSpec((NCH, 128, 256), lambda i: (i, 0, 0)), row] + [pl.BlockSpec((TB, 768), lambda i: (i, 0))] * 2
        + [pl.BlockSpec((TB, 256), lambda i: (i, 0))] * 2 + ([pl.BlockSpec((8, D), lambda i: (0, 0))] if head else []),
        out_shape=[jax.ShapeDtypeStruct((t, D), BF16), jax.ShapeDtypeStruct((nc, 256, 128), F32),
                   jax.ShapeDtypeStruct((nc, 128, 256), F32), jax.ShapeDtypeStruct((t, D), F32)]
        + [jax.ShapeDtypeStruct((t, 768), BF16)] * 2 + [jax.ShapeDtypeStruct((t, 256), BF16)] * 2
        + ([jax.ShapeDtypeStruct((8, D), F32)] if head else []),
        scratch_shapes=[pltpu.VMEM((256, 128), F32), pltpu.VMEM((128, 256), F32), pltpu.VMEM((8, 256), F32),
                        pltpu.VMEM((16, 256), F32), pltpu.VMEM((8, 768), F32)])


def _mixer_bwd(proj, dxn, wot, mix, sg, ss, xc16, dxc16, cv16, pool16, prm, gw, pw, cmat, mask, name, rider=None):
    proj, tail = proj
    t = proj.shape[0]
    nt = t // TB
    rev = lambda i: nt - 1 - i

    def body(p_ref, t_ref, dxn_ref, wot_ref, mix_ref, sg_ref, ss_ref, xc_ref, dxc_ref, cv_ref, pool_ref, prm_ref, gw_ref,
             pw_ref, cm_ref, mk_ref, dp_ref, sgc_ref, dwo_ref,
             gg_s, gs_s, h_dcv, h_dpl, h_dpre, gsm_ref, dgw_ref, dpw_ref, dm_ref):
        i = pl.program_id(0)
        tile = nt - 1 - i

        @pl.when(i == 0)
        def _():
            for r in (gg_s, gs_s, h_dcv, h_dpl, h_dpre, gsm_ref, dgw_ref, dpw_ref, dwo_ref):
                r[...] = jnp.zeros_like(r)

        dxn = dxn_ref[...].astype(BF16)
        dm_ref[...] = jnp.dot(dxn, wot_ref[...], preferred_element_type=F32)
        dwo_ref[...] += _dot_tn(mix_ref[...], dxn)

        lane = _iota((1, 256), 1)
        ah, ac = p_ref[:, C_AH:C_AH + 256].astype(F32), p_ref[:, C_AC:C_AC + 256].astype(F32)
        ab, az = p_ref[:, C_AB:C_AB + 256].astype(F32), p_ref[:, C_AZ:C_AZ + 256].astype(F32)
        w0, w1, w2 = (prm_ref[R_CAW + j:R_CAW + j + 1, 0:256] for j in range(3))
        u = ac * ah
        cv = cv_ref[...].astype(F32)
        g = dm_ref[:, 0:256]
        sz, dsz = _silu_pair(az)
        dp_ref[:, C_AB:C_AB + 256] = (g * cv * sz).astype(BF16)
        dp_ref[:, C_AZ:C_AZ + 256] = (g * ab * cv * dsz).astype(BF16)
        dcv = g * ab * sz
        dext = jnp.concatenate([dcv, h_dcv[...]], axis=0)
        dcv1, dcv2 = _up(dext, 1, TB), _up(dext, 2, TB)
        du = w2 * dcv + w1 * dcv1 + w0 * dcv2
        dp_ref[:, C_AC:C_AC + 256] = (du * ah).astype(BF16)
        dp_ref[:, C_AH:C_AH + 256] = (du * ac).astype(BF16)
        gsm_ref[R_CAW:R_CAW + 1, 0:256] += _cs(u * dcv2)
        gsm_ref[R_CAW + 1:R_CAW + 2, 0:256] += _cs(u * dcv1)
        gsm_ref[R_CAW + 2:R_CAW + 3, 0:256] += _cs(u * dcv)
        h_dcv[...] = dcv[0:8, :]
        pz = p_ref[:, C_PZ:C_PZ + 256].astype(F32)
        psc = prm_ref[R_PSC:R_PSC + 1, 0:256]
        icnt = _pool_inv_count(tile, TB)
        pooled = pool_ref[...]
        pw_v = pw_ref[...]
        mixed = jnp.dot(pooled, pw_v, preferred_element_type=F32)
        g = dm_ref[:, 512:768]
        sz, dsz = _silu_pair(pz)
        gsm_ref[R_PSC:R_PSC + 1, 0:256] += _cs(g * mixed * sz)
        dp_ref[:, C_PZ:C_PZ + 256] = (g * psc * mixed * dsz).astype(BF16)
        dmixed = g * psc * sz
        dpw_ref[...] += _dot_tn(pooled, dmixed)
        dpooled = _dot_nt(dmixed, pw_v)
        qd = dpooled * icnt
        dext = jnp.concatenate([qd, h_dpl[...]], axis=0)
        dp_ref[:, C_PU:C_PU + 256] = (_winsum_up(dext, lane)[:TB] - dpooled).astype(BF16)
        h_dpl[...] = qd[0:16, :]
        cw = [prm_ref[R_SCW + j:R_SCW + j + 1, :] for j in range(4)]
        xc = xc_ref[...].astype(F32)
        xs, bm, cm = xc[:, 0:256], xc[:, 256:512], xc[:, 512:768]

        gw_v = gw_ref[...]
        tail, pre, dtin, dte, dec, kd, wdec, w, xw, d_s, et, ut_g, ut_s = _mixer_tile_prep(p_ref, t_ref, xc, prm_ref,
                                                                                          gw_v, cm_ref, mk_ref)
        gmean = cm_ref[2]
        mask_t = mk_ref[...]
        gnw = prm_ref[R_GNW:R_GNW + 1, 0:256]
        a_e = prm_ref[R_AE:R_AE + 1, 0:256]
        d_e = prm_ref[R_DE:R_DE + 1, 0:256]
        snw = prm_ref[R_SNW:R_SNW + 1, 0:256]
        sg_in = [sg_ref[c] for c in range(NCH)]
        ss_in = [ss_ref[c] for c in range(NCH)]
        sg_n = [sg_in[c] * d_s[c] + ut_g[c] for c in range(NCH)]
        ss_n = [ss_in[c] * et[c] + ut_s[c] for c in range(NCH)]
        qs = _chunks(p_ref[:, C_GQ:C_GQ + 128].astype(F32) * GLA_SCALE)
        cm_c, bm_c, xw_c, kd_c = _chunks(cm), _chunks(bm), _chunks(xw), _chunks(kd)
        v_c = _chunks(p_ref[:, C_GV:C_GV + 256].astype(F32))
        o = jnp.concatenate([_dot_nt(qs[c], sg_n[c]) for c in range(NCH)], axis=0)
        y = jnp.concatenate([_halves(_dot, cm_c[c], ss_n[c]) for c in range(NCH)], axis=0) + d_e * xs
        gz = p_ref[:, C_GZ:C_GZ + 256].astype(F32)
        r = lax.rsqrt(_dot2_l(o * o, gmean) + EPS)
        on = o * r
        dyb = dm_ref[:, 256:512]
        sz, dsz = _silu_pair(gz)
        dp_ref[:, C_GZ:C_GZ + 256] = (dyb * on * gnw * dsz).astype(BF16)
        tg = dyb * sz
        gsm_ref[R_GNW:R_GNW + 1, 0:256] += _cs(tg * on)
        don = tg * gnw
        do_c = _chunks(r * (don - on * _dot2_l(don * on, gmean)))
        ssz = p_ref[:, C_SZ:C_SZ + 256].astype(F32)
        sil, dsil = _silu_pair(ssz)
        y2 = y * sil
        r = lax.rsqrt(jnp.mean(y2 * y2, axis=-1, keepdims=True) + EPS)
        yn = y2 * r
        dyd = dm_ref[:, 768:1024]
        gsm_ref[R_SNW:R_SNW + 1, 0:256] += _cs(dyd * yn)
        dn = dyd * snw
        dy2 = r * (dn - yn * jnp.mean(dn * yn, axis=-1, keepdims=True))
        dp_ref[:, C_SZ:C_SZ + 256] = (dy2 * y * dsil).astype(BF16)
        dy = dy2 * sil
        gsm_ref[R_DE:R_DE + 1, 0:256] += _cs(dy * xs)
        dy_c = _chunks(dy)
        dq = jnp.concatenate([_dot(do_c[c], sg_n[c]) for c in range(NCH)], axis=0)
        dp_ref[:, C_GQ:C_GQ + 128] = (dq * GLA_SCALE).astype(BF16)
        dcm = jnp.concatenate([_halves(_dot_nt, dy_c[c], ss_n[c]) for c in range(NCH)], axis=0)
        gg = [_dot_tn(do_c[c], qs[c]) * mask_t for c in range(NCH)]
        gs = [_halves(_dot_tn, cm_c[c], dy_c[c]) for c in range(NCH)]
        car_g, car_s = gg_s[...], gs_s[...]
        for c in reversed(range(NCH)):
            gg[c] = gg[c] + car_g
            gs[c] = gs[c] + car_s
            car_g = gg[c] * d_s[c]
            car_s = gs[c] * et[c]
        gg_s[...] = car_g
        gs_s[...] = car_s
        dkd = jnp.concatenate([_dot(v_c[c], gg[c]) for c in range(NCH)], axis=0)
        dp_ref[:, C_GV:C_GV + 256] = jnp.concatenate([_dot_nt(kd_c[c], gg[c]) for c in range(NCH)], axis=0).astype(BF16)
        dp_ref[:, C_GK:C_GK + 128] = (dkd * dec).astype(BF16)
        dbm = jnp.concatenate([_halves(_dot_nt, xw_c[c], gs[c]) for c in range(NCH)], axis=0)
        dxw = jnp.concatenate([_halves(_dot, bm_c[c], gs[c]) for c in range(NCH)], axis=0)
        dxs = dy * d_e + dxw * w
        dw = dxw * xs
        dsuf = _chunk_sums(cm_ref[1], jnp.concatenate([dkd * kd, dw * dte * wdec], axis=1))
        tot_g = jnp.concatenate([jnp.broadcast_to(_cs(gg[c] * sg_in[c]) * d_s[c], (CH, 128)) for c in range(NCH)], axis=0)
        tot_s = jnp.concatenate([jnp.broadcast_to(_cs(gs[c] * ss_in[c]) * et[c], (CH, 256)) for c in range(NCH)], axis=0)
        dpre = (dsuf[:, 0:128] + tot_g) * INV_TAU * jax.nn.sigmoid(-pre)
        dgw_ref[...] += _dot_tn(tail, dpre)
        gsm_ref[R_GB:R_GB + 1, 0:128] += _cs(dpre)
        dda = dsuf[:, 128:384] + tot_s
        gsm_ref[R_AE:R_AE + 1, 0:256] += _cs(dda * dte)
        dtail_s = _dot2_nt(dw * wdec + dda * a_e, cm_ref[3, 0:128, :]) * jax.nn.sigmoid(dtin)
        gsm_ref[R_DTB:R_DTB + 1, 0:128] += _cs(dtail_s)
        dp_ref[:, C_TL:C_TL + 128] = (_dot_nt(dpre, gw_v) + dtail_s).astype(BF16)
        dpre_c = jnp.concatenate([dxs, dbm, dcm], axis=1) * dxc_ref[...].astype(F32)
        dext = jnp.concatenate([dpre_c, h_dpre[...]], axis=0)
        ups = [dpre_c, _up(dext, 1, TB), _up(dext, 2, TB), _up(dext, 3, TB)]
        dp_ref[:, C_SX:C_SX + 768] = (cw[3] * ups[0] + cw[2] * ups[1] + cw[1] * ups[2] + cw[0] * ups[3]).astype(BF16)
        sx = p_ref[:, C_SX:C_SX + 768].astype(F32)
        for k in range(4):
            gsm_ref[R_SCW + k:R_SCW + k + 1, :] += _cs(sx * ups[3 - k])
        gsm_ref[R_SCB:R_SCB + 1, :] += _cs(dpre_c)
        h_dpre[...] = dpre_c[0:8, :]

        @pl.when(i == nt - 1)
        def _():
            ri, ci = _iota((256, 256), 0), _iota((256, 256), 1)
            per_head = jnp.where((ri >> 6) == ci, 1.0, 0.0).astype(BF16)
            per_dv = jnp.where((ri & 63) == ci, 1.0, 0.0).astype(BF16)
            row = _iota((8, 256), 0)
            top = gsm_ref[0:8, 0:256]
            sgc_ref[0:8, 0:256] = jnp.where(row == R_GNW, _dot3_l(top, per_dv), top)
            bot = gsm_ref[8:16, 0:256]
            fold = _dot3_l(jnp.where(row == R_AE - 8, bot * a_e, bot), per_head)
            sgc_ref[8:16, 0:256] = jnp.where((row == R_AE - 8) | (row == R_DE - 8), fold, bot)
            sgc_ref[0:16, 256:768] = gsm_ref[:, 256:768]
            sgc_ref[0:16, 768:896] = dgw_ref[0:16, :]
            sgc_ref[0:16, 896:1024] = jnp.zeros((16, 128), F32)
            diag = _pool_lane_select(lane, dpw_ref[0:64, :], dpw_ref[64:128, :], dpw_ref[128:192, :], dpw_ref[192:256, :])
            for q in range(4):
                sgc_ref[16:32, 256 * q:256 * q + 256] = diag[16 * q:16 * q + 16, :]

    return _call(
        body, (proj, tail, dxn, wot, mix, sg, ss, xc16, dxc16, cv16, pool16, prm, gw, pw, cmat, mask), grid=(nt,), name=name,
        sem=("arbitrary",), rider=rider,
        in_specs=[pl.BlockSpec((TB, NPM), lambda i: (rev(i), 0)),
                  pl.BlockSpec((TB, NP - NPM), lambda i: (rev(i), 0)),
                  pl.BlockSpec((TB, D), lambda i: (rev(i), 0)), pl.BlockSpec((D, D), lambda i: (0, 0)),
                  pl.BlockSpec((TB, D), lambda i: (rev(i), 0)),
                  pl.BlockSpec((NCH, 256, 128), lambda i: (rev(i), 0, 0)),
                  pl.BlockSpec((NCH, 128, 256), lambda i: (rev(i), 0, 0)),
                  pl.BlockSpec((TB, 768), lambda i: (rev(i), 0)), pl.BlockSpec((TB, 768), lambda i: (rev(i), 0)),
                  pl.BlockSpec((TB, 256), lambda i: (rev(i), 0)), pl.BlockSpec((TB, 256), lambda i: (rev(i), 0)),
                  pl.BlockSpec((16, 768), lambda i: (0, 0)), pl.BlockSpec((128, 128), lambda i: (0, 0)),
                  pl.BlockSpec((256, 256), lambda i: (0, 0)), pl.BlockSpec((4, 256, 256), lambda i: (0, 0, 0)),
                  pl.BlockSpec((256, 128), lambda i: (0, 0))],
        out_specs=[pl.BlockSpec((TB, NP), lambda i: (rev(i), 0)), pl.BlockSpec((32, 1024), lambda i: (0, 0)),
                   pl.BlockSpec((D, D), lambda i: (0, 0))],
        out_shape=[jax.ShapeDtypeStruct((t, NP), BF16), jax.ShapeDtypeStruct((32, 1024), F32),
                   jax.ShapeDtypeStruct((D, D), F32)],
        scratch_shapes=[pltpu.VMEM((256, 128), F32), pltpu.VMEM((128, 256), F32), pltpu.VMEM((8, 256), F32),
                        pltpu.VMEM((16, 256), F32), pltpu.VMEM((8, 768), F32), pltpu.VMEM((16, 768), F32),
                        pltpu.VMEM((128, 128), F32), pltpu.VMEM((256, 256), F32), pltpu.VMEM((TB, D), F32)])


SHARD = NPROJ // 4
SHARD_PAD = 896


def _ranges_to_perm(o, n):
    out, p = [], 0
    for start, size in _PERM:
        a, b = max(o, start), min(o + n, start + size)
        if a < b:
            out.append((a, b - a, p + a - start))
        p += size
    return out


def _ranges_to_orig(p0, n):
    out, p = [], 0
    for start, size in _PERM:
        a, b = max(p0, p), min(p0 + n, p + size)
        if a < b:
            out.append((a, b - a, start + a - p))
        p += size
    return out


def _lane_window(load, lo, n, d, lane):
    a = 128 * (lo // 128)
    off = lo - a
    w = 128 if off + n <= 128 else 256
    chunk = load(a, w)
    shift = (d - off) % w
    if shift:
        chunk = pltpu.roll(chunk, shift, axis=1)
    return jnp.where((lane >= d) & (lane < d + n), chunk[:, 0:128], 0.0)


def _assemble_w_in(slabs, name, rb=256):
    def body(s_ref, wp_ref, wpt_ref):
        lane = _iota((1, 128), 1)
        for b in range(NP // 128):
            acc = jnp.zeros((rb, 128), F32)
            for p, n, o in _ranges_to_orig(128 * b, 128):
                while n > 0:
                    s, lo = o // SHARD, o % SHARD
                    cnt = min(n, SHARD - lo)
                    acc = acc + _lane_window(lambda a, w, s=s: s_ref[s, :, a:a + w].astype(F32), lo, cnt, p - 128 * b, lane)
                    o, p, n = o + cnt, p + cnt, n - cnt
            wp_ref[:, 128 * b:128 * b + 128] = acc.astype(BF16)
            wpt_ref[128 * b:128 * b + 128, :] = acc.T.astype(BF16)

    return pl.pallas_call(
        body, grid=(D // rb,), name=name,
        in_specs=[pl.BlockSpec((4, rb, SHARD_PAD), lambda i: (0, i, 0))],
        out_specs=[pl.BlockSpec((rb, NP), lambda i: (i, 0)), pl.BlockSpec((NP, rb), lambda i: (0, i))],
        out_shape=[jax.ShapeDtypeStruct((D, NP), BF16), jax.ShapeDtypeStruct((NP, D), BF16)],
        compiler_params=_cparams(("parallel",)))(slabs)


def _split_dw_in(dwp, name, rb=256):
    rows = dwp.shape[0]

    def body(g_ref, o_ref):
        lane = _iota((1, 128), 1)
        for s in range(4):
            for k in range(SHARD_PAD // 128):
                acc = jnp.zeros((rb, 128), F32)
                n_valid = min(128, SHARD - 128 * k)
                for o, n, p in _ranges_to_perm(SHARD * s + 128 * k, n_valid):
                    acc = acc + _lane_window(lambda a, w: g_ref[:, a:a + w].astype(F32), p, n, o - SHARD * s - 128 * k, lane)
                o_ref[s, :, 128 * k:128 * k + 128] = acc.astype(o_ref.dtype)

    return pl.pallas_call(
        body, grid=(rows // rb,), name=name,
        in_specs=[pl.BlockSpec((rb, NP), lambda i: (i, 0))],
        out_specs=pl.BlockSpec((4, rb, SHARD_PAD), lambda i: (0, i, 0)),
        out_shape=jax.ShapeDtypeStruct((4, rows, SHARD_PAD), dwp.dtype),
        compiler_params=_cparams(("parallel",)))(dwp)


def _half(c, n):
    return pl.ds(pl.multiple_of(c * (n // 2), n // 2), n // 2)


def _other_chips(x, y):
    return ((1 - x, y), (x, 1 - y), (1 - x, 1 - y))


def _remote(src, dst, send, recv, k, dev):
    return pltpu.make_async_remote_copy(src_ref=src, dst_ref=dst, send_sem=send.at[k], recv_sem=recv.at[k], device_id=dev,
                                        device_id_type=MESH)


def _sem(n):
    return pltpu.SemaphoreType.DMA((n,))


def _rider_gather_ici(shards):
    shards = tuple(shards)
    n = len(shards)

    def copies(rins, routs, sems, arrivals=True):
        send, recv = sems
        x, y, c = _place()
        me = 2 * x + y
        out, inc = [], []
        for j, (px, py) in enumerate(_other_chips(x, y)):
            for k in range(n):
                rows = _half(c, shards[k].shape[0])
                out.append(_remote(rins[k].at[rows], routs[k].at[me, rows], send, recv, n * j + k, (px, py, c)))
                if arrivals:
                    inc.append(_remote(rins[k].at[rows], routs[k].at[2 * px + py, rows], send, recv, n * j + k, (px, py, c)))
        return out, inc

    def start(rins, routs, sems):
        for cp in copies(rins, routs, sems, arrivals=False)[0]:
            cp.start()

    def finish(rins, routs, sems):
        out, inc = copies(rins, routs, sems)
        for cp in inc:
            cp.wait_recv()
        for cp in out:
            cp.wait_send()

    return _Rider(shards, [jax.ShapeDtypeStruct((4,) + a.shape, a.dtype) for a in shards], [_sem(3 * n), _sem(3 * n)],
                  start, finish)


def _gather_ici_two_hops(shards, extra):
    shards = tuple(shards)
    n = len(shards)

    def body(*refs):
        ins, e_in, outs, e_out = refs[:n], refs[n], refs[n + 1:2 * n + 1], refs[2 * n + 1]
        send, recv = refs[2 * n + 2:]
        x, y, c = _place()
        slab = lambda px, py: 2 * px + py
        xn, yn, dg = (1 - x, y), (x, 1 - y), (1 - x, 1 - y)

        def part(k, q):
            r = shards[k].shape[0] // 4
            return pl.ds(pl.multiple_of(c * 2 * r + q * r, r), r)

        def hop(k, q, src_chip, to, sem):
            rows = part(k, q)
            src = ins[k].at[rows] if src_chip is None else outs[k].at[slab(*src_chip), rows]
            own = (x, y) if src_chip is None else src_chip
            return _remote(src, outs[k].at[slab(*own), rows], send, recv, sem, (*to, c))

        small = [_remote(e_in, e_out.at[slab(x, y)], send, recv, 6 * n + j, (*to, c)) for j, to in enumerate((xn, yn, dg))]
        first = [hop(k, q, None, (xn, yn)[q], 2 * k + q) for k in range(n) for q in (0, 1)]
        for cp in small + first:
            cp.start()
        for k in range(n):
            for q in (0, 1):
                nb = (xn, yn)[q]
                _remote(ins[k].at[part(k, q)], outs[k].at[slab(*nb), part(k, q)], send, recv, 2 * k + q, (*nb, c)).wait_recv()
        second = []
        for k in range(n):
            for q in (0, 1):
                to, via = (yn, xn)[q], (xn, yn)[q]
                second.append(hop(k, q, None, to, 2 * n + 4 * k + 2 * q))
                second.append(hop(k, q, via, to, 2 * n + 4 * k + 2 * q + 1))
        for cp in second:
            cp.start()
        for k in range(n):
            for q in (0, 1):
                frm, rows = (yn, xn)[q], part(k, q)
                for j, origin in enumerate((frm, dg)):
                    _remote(ins[k].at[rows], outs[k].at[slab(*origin), rows], send, recv, 2 * n + 4 * k + 2 * q + j,
                            (*frm, c)).wait_recv()
        for j, frm in enumerate((xn, yn, dg)):
            _remote(e_in, e_out.at[slab(*frm)], send, recv, 6 * n + j, (*frm, c)).wait_recv()
        for cp in small + first + second:
            cp.wait_send()

    outs = pl.pallas_call(
        body, name="gather_ici0", in_specs=[_ANY] * (n + 1), out_specs=[_ANY] * (n + 1),
        out_shape=[jax.ShapeDtypeStruct((4,) + a.shape, a.dtype) for a in shards + (extra,)],
        scratch_shapes=[_sem(6 * n + 3), _sem(6 * n + 3)])(*shards, extra)
    return list(outs)


def _rider_gather_d2d(slabs):
    slabs = tuple(slabs)
    n = len(slabs)

    def copies(routs, sems, arrivals=True):
        send, recv = sems
        x, y, c = _place()
        out, inc = [], []
        for j, (px, py) in enumerate(_other_chips(x, y)):
            for k in range(n):
                rows = slabs[k].shape[1]
                mine, theirs = routs[k].at[2 * px + py, _half(c, rows)], routs[k].at[2 * px + py, _half(1 - c, rows)]
                out.append(_remote(mine, mine, send, recv, n * j + k, (x, y, 1 - c)))
                if arrivals:
                    inc.append(_remote(theirs, theirs, send, recv, n * j + k, (x, y, 1 - c)))
        return out, inc

    def start(rins, routs, sems):
        for cp in copies(routs, sems, arrivals=False)[0]:
            cp.start()

    def finish(rins, routs, sems):
        out, inc = copies(routs, sems)
        for cp in inc:
            cp.wait_recv()
        for cp in out:
            cp.wait_send()

    return _Rider(slabs, [jax.ShapeDtypeStruct(a.shape, a.dtype) for a in slabs], [_sem(3 * n), _sem(3 * n)], start, finish,
                  aliases={k: k for k in range(n)})


def _rider_swap(parts):
    parts = tuple(parts)
    n = len(parts)

    def copies(rins, routs, sems):
        send, recv = sems
        x, y, c = _place()
        return [_remote(rins[k].at[:, _half(1 - c, parts[k].shape[1])], routs[k], send, recv, k, (x, y, 1 - c))
                for k in range(n)]

    def start(rins, routs, sems):
        for cp in copies(rins, routs, sems):
            cp.start()

    def finish(rins, routs, sems):
        for cp in copies(rins, routs, sems):
            cp.wait()

    return _Rider(parts, [jax.ShapeDtypeStruct((a.shape[0], a.shape[1] // 2, a.shape[2]), a.dtype) for a in parts],
                  [_sem(n), _sem(n)], start, finish)


def _rider_scatter(parts):
    parts = tuple(parts)
    n = len(parts)

    def copies(rins, routs, sems, arrivals=True):
        send, recv = sems
        x, y, c = _place()
        me = 2 * x + y
        out, inc = [], []
        for j, (px, py) in enumerate(_other_chips(x, y)):
            for k in range(n):
                out.append(_remote(rins[k].at[2 * px + py], routs[k].at[me], send, recv, n * j + k, (px, py, c)))
                if arrivals:
                    inc.append(_remote(rins[k].at[me], routs[k].at[2 * px + py], send, recv, n * j + k, (px, py, c)))
        return out, inc

    def start(rins, routs, sems):
        for cp in copies(rins, routs, sems, arrivals=False)[0]:
            cp.start()

    def finish(rins, routs, sems):
        out, inc = copies(rins, routs, sems)
        for cp in inc:
            cp.wait_recv()
        for cp in out:
            cp.wait_send()

    return _Rider(parts, [jax.ShapeDtypeStruct(a.shape, a.dtype) for a in parts], [_sem(3 * n), _sem(3 * n)], start, finish)


def _rider_share(fulls):
    fulls = tuple(fulls)
    n = len(fulls)

    def copies(routs, sems, arrivals=True):
        send, recv = sems
        x, y, c = _place()
        out, inc = [], []
        for k in range(n):
            mine, theirs = routs[k].at[_half(c, fulls[k].shape[0])], routs[k].at[_half(1 - c, fulls[k].shape[0])]
            out.append(_remote(mine, mine, send, recv, k, (x, y, 1 - c)))
            if arrivals:
                inc.append(_remote(theirs, theirs, send, recv, k, (x, y, 1 - c)))
        return out, inc

    def start(rins, routs, sems):
        for cp in copies(routs, sems, arrivals=False)[0]:
            cp.start()

    def finish(rins, routs, sems):
        out, inc = copies(routs, sems)
        for cp in inc:
            cp.wait_recv()
        for cp in out:
            cp.wait_send()

    return _Rider(fulls, [jax.ShapeDtypeStruct(a.shape, a.dtype) for a in fulls], [_sem(n), _sem(n)], start, finish,
                  aliases={k: k for k in range(n)})


def _pair_sum(core, full, recv, name, br=128):
    n, rows, cols = recv.shape

    def body(c_ref, a_ref, b_ref, o_ref):
        o_ref[...] = (a_ref[...] + b_ref[...]).astype(BF16)

    nb = rows // br
    return pl.pallas_call(
        body, name=name, out_shape=jax.ShapeDtypeStruct(recv.shape, BF16),
        grid_spec=pltpu.PrefetchScalarGridSpec(
            num_scalar_prefetch=1, grid=(n, nb),
            in_specs=[pl.BlockSpec((1, br, cols), lambda i, j, c: (i, c[0] * nb + j, 0)),
                      pl.BlockSpec((1, br, cols), lambda i, j, c: (i, j, 0))],
            out_specs=pl.BlockSpec((1, br, cols), lambda i, j, c: (i, j, 0))),
        compiler_params=_cparams(("parallel", "parallel")))(core, full, recv)


def _chip_sum(place, gathered, mine, name, br=128):
    _, r, c = gathered.shape
    nb = r // br

    def body(p_ref, g_ref, m_ref, o_ref):
        slab = lambda j: jnp.where(p_ref[1] == j, m_ref[j], g_ref[j]).astype(F32)
        o_ref[...] = ((slab(0) + slab(1)) + slab(2)) + slab(3)

    return pl.pallas_call(
        body, name=name, out_shape=jax.ShapeDtypeStruct((2 * r, c), F32),
        grid_spec=pltpu.PrefetchScalarGridSpec(
            num_scalar_prefetch=1, grid=(nb,),
            in_specs=[pl.BlockSpec((4, br, c), lambda i, p: (0, i, 0)), pl.BlockSpec((4, br, c), lambda i, p: (0, i, 0))],
            out_specs=pl.BlockSpec((br, c), lambda i, p: (p[0] * nb + i, 0))),
        compiler_params=_cparams(("parallel",)))(place, gathered, mine)


def _adamw(w, g, m, v, name, br):
    n, r, c = w.shape

    def body(w_ref, g_ref, m_ref, v_ref, d_ref, m2_ref, v2_ref):
        d_ref[...], m2_ref[...], v2_ref[...] = _adam_math(w_ref[...], g_ref[...], m_ref[...], v_ref[...])

    spec = pl.BlockSpec((1, br, c), lambda i, j: (i, j, 0))
    shp = jax.ShapeDtypeStruct(w.shape, F32)
    return pl.pallas_call(body, grid=(n, r // br), name=name, in_specs=[spec] * 4, out_specs=[spec] * 3,
                          out_shape=[shp] * 3, compiler_params=_cparams(("parallel", "parallel")))(w, g, m, v)


def _adamw_w_in(w, g, m, v, name, bc=93):
    cols = w.shape[2]
    lead = lambda a: jnp.transpose(a, (2, 0, 1))
    g = jnp.stack([a[:, 0:cols] for a in g])

    def body(w_ref, g_ref, m_ref, v_ref, go_ref, d_ref, m2_ref, v2_ref):
        for l in range(2):
            gv = g_ref[:, l, :]
            d_ref[:, l, :], m2_ref[:, l, :], v2_ref[:, l, :] = _adam_math(w_ref[:, l, :], gv, m_ref[:, l, :], v_ref[:, l, :])
            go_ref[:, l, :] = gv

    spec = pl.BlockSpec((bc, 2, D), lambda i: (i, 0, 0))
    outs = pl.pallas_call(body, grid=(cols // bc,), name=name, in_specs=[spec] * 4, out_specs=[spec] * 4,
                          out_shape=[jax.ShapeDtypeStruct((cols, 2, D), F32)] * 4,
                          compiler_params=_cparams(("parallel",)))(lead(w), lead(g), lead(m), lead(v))
    return [jnp.transpose(o, (1, 2, 0)) for o in outs]


_SMALL_NAMES = ("norm_w", "conv_a_w", "gla_gate_w", "gla_gate_b", "gla_norm_w", "pool_w", "pool_scale", "ssd_conv_w",
                "ssd_conv_b", "ssd_dt_bias", "ssd_a_log", "ssd_d", "ssd_norm_w", "final_norm_w")
SMALL_ROWS = 80


def _adam_math(w, g, m, v):
    m2 = ADAM_B1 * m + (1.0 - ADAM_B1) * g
    v2 = ADAM_B2 * v + (1.0 - ADAM_B2) * (g * g)
    m_hat = m2 / (1.0 - ADAM_B1 ** ADAM_STEP)
    v_hat = v2 / (1.0 - ADAM_B2 ** ADAM_STEP)
    return -ADAM_LR * (m_hat / (jnp.sqrt(v_hat) + ADAM_EPS) + ADAM_WD * w), m2, v2


def _small_slices(name, chip):
    if name == "conv_a_w":
        return [((), slice(R_CAW, R_CAW + 3), slice(64 * chip, 64 * chip + 64))]
    if name == "ssd_conv_w":
        return [((), slice(R_SCW, R_SCW + 4), slice(192 * chip, 192 * chip + 192))]
    if name == "gla_gate_w":
        return [((), slice(0, 16), slice(768, 896))]
    if name == "pool_w":
        return [((g, slice(16 * q, 16 * q + 16)), slice(16, 32), slice(256 * q + 64 * g, 256 * q + 64 * g + 64))
                for g in range(4) for q in range(4)]
    row, lanes = {"gla_gate_b": (R_GB, slice(0, 128)), "gla_norm_w": (R_GNW, slice(0, 64)),
                  "pool_scale": (R_PSC, slice(0, 256)), "ssd_conv_b": (R_SCB, slice(0, 768)),
                  "ssd_dt_bias": (R_DTB, slice(16, 20)), "ssd_a_log": (R_AE, slice(0, 4)), "ssd_d": (R_DE, slice(0, 4)),
                  "ssd_norm_w": (R_SNW, slice(0, 256))}[name]
    return [((), slice(row, row + 1), lanes)]


def _rider_exchange(block):
    def copies(rins, routs, sems):
        send, recv = sems
        x, y, c = _place()
        flip = lambda v, bit: 1 - v if bit else v
        return [_remote(rins[0], routs[0].at[k], send, recv, k - 1, (flip(x, k & 4), flip(y, k & 2), flip(c, k & 1)))
                for k in range(1, 8)]

    def start(rins, routs, sems):
        for cp in copies(rins, routs, sems):
            cp.start()

    def finish(rins, routs, sems):
        for cp in copies(rins, routs, sems):
            cp.wait()

    return _Rider((block,), [jax.ShapeDtypeStruct((8,) + block.shape, block.dtype)], [_sem(7), _sem(7)], start, finish)


def _join_riders(a, b):
    na, oa, sa = len(a.inputs), len(a.out_shapes), len(a.sems)

    def start(rins, routs, sems):
        a.start(rins[:na], routs[:oa], sems[:sa])
        b.start(rins[na:], routs[oa:], sems[sa:])

    def finish(rins, routs, sems):
        a.finish(rins[:na], routs[:oa], sems[:sa])
        b.finish(rins[na:], routs[oa:], sems[sa:])

    aliases = {**a.aliases, **{na + k: oa + v for k, v in b.aliases.items()}}
    return _Rider(a.inputs + b.inputs, a.out_shapes + b.out_shapes, a.sems + b.sems, start, finish, aliases)


def _small_adamw(blocks, w, m, v):
    n = len(_SMALL_NAMES)

    def body(*refs):
        (own, ex), (own0, ex0) = refs[0:2], refs[2:4]
        refs = refs[3:]
        w_refs, m_refs, v_refs = refs[1:1 + n], refs[1 + n:1 + 2 * n], refs[1 + 2 * n:1 + 3 * n]
        o = 1 + 3 * n
        g_out, d_out, m_out, v_out = refs[o:o + n], refs[o + n:o + 2 * n], refs[o + 2 * n:o + 3 * n], refs[o + 3 * n:o + 4 * n]
        loss_ref, acc, acc0 = refs[o + 4 * n:o + 4 * n + 3]
        chip = 2 * lax.axis_index("x") + lax.axis_index("y")
        me = 2 * chip + lax.axis_index("c")
        acc[...] = jnp.zeros_like(acc)
        acc0[...] = jnp.zeros_like(acc0)
        for src in range(8):
            @pl.when(me == src)
            def _():
                acc[...] += own[...]
                acc0[...] += own0[...]

            @pl.when(me != src)
            def _(src=src):
                acc[...] += ex[jnp.bitwise_xor(me, src)]
                acc0[...] += ex0[jnp.bitwise_xor(me, src)]

        loss_ref[...] = acc[73:74, 0:1]

        def update(i, idx, g):
            d, m2, v2 = _adam_math(w_refs[i][idx], g, m_refs[i][idx], v_refs[i][idx])
            g_out[i][idx], d_out[i][idx], m_out[i][idx], v_out[i][idx] = g, d, m2, v2

        for i, name in enumerate(_SMALL_NAMES):
            if name == "final_norm_w":
                update(i, (slice(0, 1), slice(None)), acc[72:73, :])
            elif name == "norm_w":
                update(i, (slice(0, 1), slice(None)), acc0[0:1, :])
                update(i, (slice(1, 2), slice(None)), acc[64:65, :])
            elif name in ("conv_a_w", "ssd_conv_w"):
                for s in range(4):
                    @pl.when(chip == s)
                    def _(i=i, name=name, s=s):
                        for l in range(2):
                            (_, rows, lanes), = _small_slices(name, s)
                            update(i, (l,), acc[rows.start + 32 * l:rows.stop + 32 * l, lanes])
            else:
                for l in range(2):
                    for idx, rows, lanes in _small_slices(name, 0):
                        g = acc[rows.start + 32 * l:rows.stop + 32 * l, lanes]
                        if w_refs[i].ndim == 2:
                            update(i, (slice(l, l + 1), slice(None)), g)
                        else:
                            update(i, (l,) + idx, g)

    args = [a for pair in blocks for a in pair] + [d[k] for d in (w, m, v) for k in _SMALL_NAMES]
    shapes = [jax.ShapeDtypeStruct(w[k].shape, F32) for k in _SMALL_NAMES]
    vmem = pl.BlockSpec(memory_space=pltpu.VMEM)
    outs = pl.pallas_call(body, name="small_adamw", in_specs=[vmem] * len(args), out_specs=[vmem] * (4 * n + 1),
                          out_shape=shapes * 4 + [jax.ShapeDtypeStruct((1, 1), F32)],
                          scratch_shapes=[pltpu.VMEM((SMALL_ROWS, D), F32), pltpu.VMEM((8, D), F32)])(*args)
    return outs[0:n], outs[n:2 * n], outs[2 * n:3 * n], outs[3 * n:4 * n], outs[4 * n]


def _mixer_consts(layer, conv_a_w, gla_gate_w, gla_gate_b, gla_norm_w, pool_w, pool_scale, ssd_conv_w, ssd_conv_b,
                  ssd_dt_bias, ssd_a_log, ssd_d, ssd_norm_w):
    def row(v):
        return jnp.pad(v.reshape(1, -1), ((0, 0), (0, 768 - v.size)))

    dtb = jnp.pad(ssd_dt_bias[layer], (16, 108))
    rows = [jnp.pad(conv_a_w[layer], ((0, 0), (0, 512))), row(gla_gate_b[layer]), row(jnp.tile(gla_norm_w[layer], 4)),
            row(pool_scale[layer]), row(ssd_conv_b[layer]), row(dtb), row(jnp.repeat(-jnp.exp(ssd_a_log[layer]), 64)),
            row(jnp.repeat(ssd_d[layer], 64)), row(ssd_norm_w[layer]), jnp.zeros((1, 768), F32), ssd_conv_w[layer]]
    prm = jnp.concatenate(rows, axis=0)
    gw = jnp.pad(gla_gate_w[layer], ((0, 112), (0, 0))).astype(BF16)
    on_diag = (_iota((256, 256), 0) >> 6) == (_iota((256, 256), 1) >> 6)
    pw = jnp.where(on_diag, jnp.tile(pool_w[layer].reshape(256, 64), (1, 4)), 0.0)
    return (prm, gw, pw.astype(BF16)) + _mixer_matrices()


def _grad_slabs(dwp, dwo):
    return dwp.reshape(1, D, NP), dwo.reshape(4, D // 4, D)


class _Comm:
    def __init__(self, w_in, w_out):
        self.w_in16 = jnp.pad(w_in.astype(BF16), ((0, 0), (0, 0), (0, SHARD_PAD - SHARD)))
        self.w_out16 = w_out.astype(BF16)
        self.core = lax.axis_index("c").astype(jnp.int32).reshape(1)
        self.chip = 2 * lax.axis_index("x") + lax.axis_index("y")
        self.place = jnp.stack([lax.axis_index("c"), self.chip]).astype(jnp.int32)

    def gather_ici(self, layer):
        return _rider_gather_ici((self.w_in16[layer], self.w_out16[layer]))

    def pair_sum(self, layer, slabs, received):
        d_in, d_out = [_pair_sum(self.core, a, b, name=f"reduce_pair_sum{layer}_{k}")
                       for k, (a, b) in enumerate(zip(slabs, received))]
        return [_split_dw_in(d_in[0], name=f"split_dw_in{layer}"), d_out]

    def chip_sum(self, layer, gathered, mine):
        return [_chip_sum(self.place, a, b, name=f"reduce_chip_sum{layer}_{k}") for k, (a, b) in enumerate(zip(gathered, mine))]

    def layer_weights(self, layer, s_in, s_out):
        own = lambda slabs, shard: jnp.stack([jnp.where(self.chip == s, shard, slabs[s]) for s in range(4)])
        wp, wpt = _assemble_w_in(own(s_in, self.w_in16[layer]), name=f"assemble_w_in{layer}")
        wo = own(s_out, self.w_out16[layer]).reshape(D, D)
        return wp, wpt, wo, wo.T


def _local_step(x, tgt, norm_w, final_norm_w, consts, wts0, wts1=None, comm=None):
    nw = [norm_w[l:l + 1] for l in range(2)]
    proj0, h0, slabs = _rmsproj(x, nw[0], wts0[0], name="rmsproj0", rider=comm and comm.gather_ici(1))
    (mix0, sg0, ss0, x1, *conv0), slabs = _mixer_fwd(proj0, x, wts0[2], *consts[0], name="mixer_fwd0",
                                                     rider=comm and _rider_gather_d2d(slabs))
    if comm:
        wts1 = comm.layer_weights(1, *slabs)
    proj1, h1, _ = _rmsproj(x1, nw[1], wts1[0], name="rmsproj1")
    (mix1, sg1, ss1, dx, *conv1, head), _ = _mixer_fwd(proj1, x1, wts1[2], *consts[1], name="mixer_fwd1",
                                                       head=(tgt, final_norm_w.reshape(1, D)))
    (dproj, mgr1, dwo1), _ = _mixer_bwd(proj1, dx, wts1[3], mix1, sg1, ss1, *conv1, *consts[1], name="mixer_bwd1")
    dwp1, _ = _dwin(h1, dproj, name="dwin1")
    slabs1 = comm and _grad_slabs(dwp1, dwo1)
    (dx, dnw1), recv = _dxin(dproj, wts1[1], x1, dx, nw[1], name="dxin1", rider=comm and _rider_swap(slabs1))
    pairs1 = comm and comm.pair_sum(1, slabs1, recv)
    (dproj, mgr0, dwo0), gathered = _mixer_bwd(proj0, dx, wts0[3], mix0, sg0, ss0, *conv0, *consts[0], name="mixer_bwd0",
                                               rider=comm and _rider_scatter(pairs1))
    dwp0, big1 = _dwin(h0, dproj, name="dwin0", rider=comm and _rider_share(comm.chip_sum(1, gathered, pairs1)))
    if not comm:
        (dx, dnw0), _ = _dxin(dproj, wts0[1], x, dx, nw[0], name="dxin0")
        return head, dx, ((dwp0, dwp1), (dwo0, dwo1)), (dnw0, dnw1), (mgr0, mgr1)
    slabs0 = _grad_slabs(dwp0, dwo0)
    pairs0 = comm.pair_sum(0, slabs0, _run_rider(_rider_swap(slabs0), "reduce_swap0"))
    small = jnp.concatenate([mgr0, mgr1, dnw1, head], axis=0)
    (dx, dnw0), gathered = _dxin(dproj, wts0[1], x, dx, nw[0], name="dxin0",
                                 rider=_join_riders(_rider_scatter(pairs0), _rider_exchange(small)))
    last = _run_rider(_join_riders(_rider_share(comm.chip_sum(0, gathered[0:2], pairs0)), _rider_exchange(dnw0)),
                      "reduce_share0")
    return dx, ((last[0], big1[0]), (last[1], big1[1])), ((small, gathered[2]), (dnw0, last[2]))


def kernel(x, norm_w, w_in, conv_a_w, gla_gate_w, gla_gate_b, gla_norm_w, pool_w, pool_scale, ssd_conv_w, ssd_conv_b, ssd_dt_bias, ssd_a_log, ssd_d, ssd_norm_w, w_out, final_norm_w, loss_target, m_norm_w, m_w_in, m_conv_a_w, m_gla_gate_w, m_gla_gate_b, m_gla_norm_w, m_pool_w, m_pool_scale, m_ssd_conv_w, m_ssd_conv_b, m_ssd_dt_bias, m_ssd_a_log, m_ssd_d, m_ssd_norm_w, m_w_out, m_final_norm_w, v_norm_w, v_w_in, v_conv_a_w, v_gla_gate_w, v_gla_gate_b, v_gla_norm_w, v_pool_w, v_pool_scale, v_ssd_conv_w, v_ssd_conv_b, v_ssd_dt_bias, v_ssd_a_log, v_ssd_d, v_ssd_norm_w, v_w_out, v_final_norm_w):
    weights = dict(norm_w=norm_w, w_in=w_in, conv_a_w=conv_a_w, gla_gate_w=gla_gate_w, gla_gate_b=gla_gate_b,
                   gla_norm_w=gla_norm_w, pool_w=pool_w, pool_scale=pool_scale, ssd_conv_w=ssd_conv_w,
                   ssd_conv_b=ssd_conv_b, ssd_dt_bias=ssd_dt_bias, ssd_a_log=ssd_a_log, ssd_d=ssd_d,
                   ssd_norm_w=ssd_norm_w, w_out=w_out, final_norm_w=final_norm_w)
    m_in = dict(norm_w=m_norm_w, w_in=m_w_in, conv_a_w=m_conv_a_w, gla_gate_w=m_gla_gate_w, gla_gate_b=m_gla_gate_b,
                gla_norm_w=m_gla_norm_w, pool_w=m_pool_w, pool_scale=m_pool_scale, ssd_conv_w=m_ssd_conv_w,
                ssd_conv_b=m_ssd_conv_b, ssd_dt_bias=m_ssd_dt_bias, ssd_a_log=m_ssd_a_log, ssd_d=m_ssd_d,
                ssd_norm_w=m_ssd_norm_w, w_out=m_w_out, final_norm_w=m_final_norm_w)
    v_in = dict(norm_w=v_norm_w, w_in=v_w_in, conv_a_w=v_conv_a_w, gla_gate_w=v_gla_gate_w, gla_gate_b=v_gla_gate_b,
                gla_norm_w=v_gla_norm_w, pool_w=v_pool_w, pool_scale=v_pool_scale, ssd_conv_w=v_ssd_conv_w,
                ssd_conv_b=v_ssd_conv_b, ssd_dt_bias=v_ssd_dt_bias, ssd_a_log=v_ssd_a_log, ssd_d=v_ssd_d,
                ssd_norm_w=v_ssd_norm_w, w_out=v_w_out, final_norm_w=v_final_norm_w)
    order = ("norm_w", "w_in", "conv_a_w", "gla_gate_w", "gla_gate_b", "gla_norm_w", "pool_w", "pool_scale",
             "ssd_conv_w", "ssd_conv_b", "ssd_dt_bias", "ssd_a_log", "ssd_d", "ssd_norm_w", "w_out", "final_norm_w")
    t = x.shape[1]

    comm = _Comm(w_in, w_out)
    cshard = jnp.zeros((16, 256), F32)
    for l in range(2):
        cshard = cshard.at[8 * l:8 * l + 3, 0:64].set(conv_a_w[l]).at[8 * l + 3:8 * l + 7, 0:192].set(ssd_conv_w[l])
    s_in, s_out, g_c = _gather_ici_two_hops((comm.w_in16[0], comm.w_out16[0]), cshard)
    s_in, s_out = _run_rider(_rider_gather_d2d((s_in, s_out)), "gather_d2d0")
    g_c = [jnp.where(comm.chip == s, cshard, g_c[s]) for s in range(4)]
    conv_a_full = jnp.stack([jnp.concatenate([g_c[s][8 * l:8 * l + 3, 0:64] for s in range(4)], axis=-1) for l in range(2)])
    ssd_conv_full = jnp.stack([jnp.concatenate([g_c[s][8 * l + 3:8 * l + 7, 0:192] for s in range(4)], axis=-1)
                               for l in range(2)])
    consts = [_mixer_consts(l, conv_a_full, gla_gate_w, gla_gate_b, gla_norm_w, pool_w, pool_scale, ssd_conv_full,
                            ssd_conv_b, ssd_dt_bias, ssd_a_log, ssd_d, ssd_norm_w) for l in range(2)]

    dx, big, blocks = _local_step(x.reshape(t, D), loss_target.reshape(t, D), norm_w, final_norm_w, consts,
                                  comm.layer_weights(0, s_in, s_out), comm=comm)

    as2d = lambda d: {k: (d[k].reshape(1, D) if k == "final_norm_w" else d[k]) for k in _SMALL_NAMES}
    small = _small_adamw(blocks, as2d(weights), as2d(m_in), as2d(v_in))
    grads, delta, new_m, new_v = ({k: (a.reshape(D) if k == "final_norm_w" else a) for k, a in zip(_SMALL_NAMES, part)}
                                  for part in small[0:4])
    loss = small[4].reshape(())

    grads["w_out"] = jnp.stack(big[1])

    grads["w_in"], delta["w_in"], new_m["w_in"], new_v["w_in"] = _adamw_w_in(w_in, big[0], m_w_in, v_w_in, name="adamw_w_in")
    delta["w_out"], new_m["w_out"], new_v["w_out"] = _adamw(w_out, grads["w_out"], m_w_out, v_w_out, name="adamw_w_out", br=256)

    return (loss, dx.reshape(1, t, D), *[grads[k] for k in order], *[delta[k] for k in order],
            *[new_m[k] for k in order], *[new_v[k] for k in order])
```

```python
import functools

import jax
import jax.numpy as jnp
from jax import lax
from jax.experimental import pallas as pl
from jax.experimental.pallas import tpu as pltpu

F32 = jnp.float32
BF16 = jnp.bfloat16
MESH = pl.DeviceIdType.MESH

D = 1024
CH = 64
EPS = 1e-6
NP = 3456
NPROJ = 3348
NPM = 3328
GLA_SCALE = 32.0 ** -0.5
INV_TAU = 1.0 / 16.0
TB = 512
NCH = TB // CH
assert TB % 256 == 0

C_AH, C_AB, C_AC, C_AZ, C_GQ, C_GK, C_GV = 0, 256, 512, 768, 1024, 1152, 1280
C_GZ, C_PU, C_PZ, C_SZ, C_SX, C_TL = 1536, 1792, 2048, 2304, 2560, 3328
_PERM = ((0, 1536), (1552, 1792), (1536, 16), (3344, 4))

R_CAW, R_GB, R_GNW, R_PSC, R_SCB, R_DTB, R_AE, R_DE, R_SNW, R_SCW = 0, 3, 4, 5, 6, 7, 8, 9, 10, 12

ADAM_LR, ADAM_B1, ADAM_B2, ADAM_EPS, ADAM_WD, ADAM_STEP = 0.001, 0.9, 0.999, 1e-08, 0.01, 10

VMEM_LIMIT = 56 * 1024 * 1024


def _cparams(sem, limit=VMEM_LIMIT):
    return pltpu.CompilerParams(dimension_semantics=sem, vmem_limit_bytes=limit)


_ANY = pl.BlockSpec(memory_space=pl.ANY)


def _place():
    return lax.axis_index("x"), lax.axis_index("y"), lax.axis_index("c")


class _Rider:
    def __init__(self, inputs, out_shapes, sems, start, finish, aliases=None):
        self.inputs, self.out_shapes, self.sems = tuple(inputs), tuple(out_shapes), tuple(sems)
        self.start, self.finish, self.aliases = start, finish, dict(aliases or {})


def _call(body, args, *, grid, in_specs, out_specs, out_shape, name, sem, scratch_shapes=(), rider=None):
    if rider is None:
        outs = pl.pallas_call(body, grid=grid, name=name, in_specs=list(in_specs), out_specs=list(out_specs),
                              out_shape=list(out_shape), scratch_shapes=list(scratch_shapes),
                              compiler_params=_cparams(sem))(*args)
        return list(outs), []
    ni, no, ns = len(args), len(out_shape), len(scratch_shapes)
    ri, ro = len(rider.inputs), len(rider.out_shapes)

    def full(*refs):
        ins, rins = refs[:ni], refs[ni:ni + ri]
        outs, routs = refs[ni + ri:ni + ri + no], refs[ni + ri + no:ni + ri + no + ro]
        scr, rsem = refs[ni + ri + no + ro:ni + ri + no + ro + ns], refs[ni + ri + no + ro + ns:]
        first = functools.reduce(jnp.logical_and, [pl.program_id(a) == 0 for a in range(len(grid))])
        last = functools.reduce(jnp.logical_and, [pl.program_id(a) == grid[a] - 1 for a in range(len(grid))])

        @pl.when(first)
        def _():
            rider.start(rins, routs, rsem)

        body(*ins, *outs, *scr)

        @pl.when(last)
        def _():
            rider.finish(rins, routs, rsem)

    outs = pl.pallas_call(
        full, grid=grid, name=name, in_specs=list(in_specs) + [_ANY] * ri, out_specs=list(out_specs) + [_ANY] * ro,
        out_shape=list(out_shape) + list(rider.out_shapes), scratch_shapes=list(scratch_shapes) + list(rider.sems),
        input_output_aliases={ni + k: no + v for k, v in rider.aliases.items()},
        compiler_params=_cparams(("arbitrary",) * len(grid)))(*args, *rider.inputs)
    return list(outs[:no]), list(outs[no:])


def _run_rider(rider, name):
    ri = len(rider.inputs)

    def body(*refs):
        rins, routs, rsem = refs[:ri], refs[ri:ri + len(rider.out_shapes)], refs[ri + len(rider.out_shapes):]
        rider.start(rins, routs, rsem)
        rider.finish(rins, routs, rsem)

    return list(pl.pallas_call(body, name=name, in_specs=[_ANY] * ri, out_specs=[_ANY] * len(rider.out_shapes),
                               out_shape=list(rider.out_shapes), scratch_shapes=list(rider.sems),
                               input_output_aliases=dict(rider.aliases))(*rider.inputs))


def _dot(a, b):
    return jnp.dot(a.astype(BF16), b.astype(BF16), preferred_element_type=F32)


def _dot_nt(a, b):
    return lax.dot_general(a.astype(BF16), b.astype(BF16), (((1,), (1,)), ((), ())), preferred_element_type=F32)


def _dot_tn(a, b):
    return lax.dot_general(a.astype(BF16), b.astype(BF16), (((0,), (0,)), ((), ())), preferred_element_type=F32)


def _split(a):
    hi = a.astype(BF16)
    lo = (a - hi.astype(F32)).astype(BF16)
    return hi, lo


def _dot2_l(a, b):
    hi, lo = _split(a)
    return _dot(hi, b) + _dot(lo, b)


def _dot2_r(a, b):
    hi, lo = _split(b)
    return _dot(a, hi) + _dot(a, lo)


def _dot3_l(a, b):
    hi, lo = _split(a)
    lo2 = ((a - hi.astype(F32)) - lo.astype(F32)).astype(BF16)
    return _dot(hi, b) + _dot(lo, b) + _dot(lo2, b)


def _dot2_nt(a, b):
    hi, lo = _split(a)
    return _dot_nt(hi, b) + _dot_nt(lo, b)


def _silu(z):
    return z * jax.nn.sigmoid(z)


def _lse1(x):
    return jnp.log(1.0 + jnp.exp(-jnp.abs(x)))


def _cs(a):
    return jnp.sum(a, axis=0, keepdims=True)


def _iota(shape, dim):
    return lax.broadcasted_iota(jnp.int32, shape, dim)


def _mixer_matrices():
    r, c = _iota((256, 256), 0), _iota((256, 256), 1)
    same_chunk = (r >> 6) == (c >> 6)
    mats = jnp.stack([jnp.where((c > r) & same_chunk, 1.0, 0.0), jnp.where((c < r) & same_chunk, 1.0, 0.0),
                      jnp.where(same_chunk, 1.0 / 64.0, 0.0), jnp.where((r < 128) & (r - 16 == (c >> 6)), 1.0, 0.0)])
    mask = jnp.where((_iota((256, 128), 0) >> 6) == (_iota((256, 128), 1) >> 5), 1.0, 0.0)
    return mats.astype(BF16), mask.astype(F32)


def _dn(ext, k, n, h):
    return pltpu.roll(ext, k, axis=0)[h:h + n]


def _up(ext, k, n):
    return pltpu.roll(ext, ext.shape[0] - k, axis=0)[:n]


def _pool_lane_select(lane, s2, s4, s8, s16):
    return jnp.where(lane < 64, s2, jnp.where(lane < 128, s4, jnp.where(lane < 192, s8, s16)))


def _winsum_dn(ext, lane):
    s2 = ext + pltpu.roll(ext, 1, axis=0)
    s4 = s2 + pltpu.roll(s2, 2, axis=0)
    s8 = s4 + pltpu.roll(s4, 4, axis=0)
    s16 = s8 + pltpu.roll(s8, 8, axis=0)
    return _pool_lane_select(lane, s2, s4, s8, s16)


def _winsum_up(ext, lane):
    m = ext.shape[0]
    s2 = ext + pltpu.roll(ext, m - 1, axis=0)
    s4 = s2 + pltpu.roll(s2, m - 2, axis=0)
    s8 = s4 + pltpu.roll(s4, m - 4, axis=0)
    s16 = s8 + pltpu.roll(s8, m - 8, axis=0)
    return _pool_lane_select(lane, s2, s4, s8, s16)


def _pool_inv_count(tile, n):
    lane = _iota((1, 256), 1)
    win = _pool_lane_select(lane, 2.0, 4.0, 8.0, 16.0).astype(F32)
    tpos = (tile * n + _iota((n, 1), 0) + 1).astype(F32)
    return jnp.where(tpos >= win, 1.0 / win, 1.0 / tpos)


def _silu_pair(z):
    s = jax.nn.sigmoid(z)
    return z * s, s * (1.0 + z * (1.0 - s))


def _chunks(a):
    return [a[c * CH:(c + 1) * CH] for c in range(a.shape[0] // CH)]


def _halves(fn, a, b):
    return jnp.concatenate([fn(a[:, 0:128], b[:, 0:128]), fn(a[:, 128:256], b[:, 128:256])], axis=1)


def _chunk_sums(tri, a):
    return jnp.concatenate([_dot2_r(tri, a[r:r + 256]) for r in range(0, a.shape[0], 256)], axis=0)


def _mixer_tile_prep(p_ref, t_ref, xc, prm_ref, gw_v, cm_ref, mk_ref):
    tail = t_ref[...]
    pre = _dot(tail, gw_v) + prm_ref[R_GB:R_GB + 1, 0:128]
    la = (jnp.minimum(pre, 0.0) - _lse1(pre)) * INV_TAU
    dtin = tail + prm_ref[R_DTB:R_DTB + 1, 0:128]
    dtf = jnp.maximum(dtin, 0.0) + _lse1(dtin)
    dte = _dot2_l(dtf, cm_ref[3, 0:128, :])
    da = dte * prm_ref[R_AE:R_AE + 1, 0:256]
    rev = _chunk_sums(cm_ref[0], jnp.concatenate([la, da], axis=1))
    dec = jnp.exp(rev[:, 0:128])
    kd = p_ref[:, C_GK:C_GK + 128].astype(F32) * dec
    wdec = jnp.exp(rev[:, 128:384])
    w = wdec * dte
    xw = xc[:, 0:256] * w
    d_s = [jnp.exp(_cs(a)) for a in _chunks(la)]
    et = [jnp.exp(_cs(a)) for a in _chunks(da)]
    mask_t = mk_ref[...]
    ut_g = [_dot_tn(v, k) * mask_t for v, k in zip(_chunks(p_ref[:, C_GV:C_GV + 256].astype(F32)), _chunks(kd))]
    ut_s = [_halves(_dot_tn, b, x) for b, x in zip(_chunks(xc[:, 256:512]), _chunks(xw))]
    return tail, pre, dtin, dte, dec, kd, wdec, w, xw, d_s, et, ut_g, ut_s


def _rmsproj(x, nw, wp, name, tm=512, rider=None):
    t = x.shape[0]

    def body(x_ref, nw_ref, w_ref, o_ref, t_ref, h_ref):
        xv = x_ref[...]
        rs = lax.rsqrt(jnp.mean(xv * xv, axis=-1, keepdims=True) + EPS)
        h = (xv * rs * nw_ref[...]).astype(BF16)
        h_ref[...] = h
        proj = jnp.dot(h, w_ref[...], preferred_element_type=F32)
        o_ref[...] = proj[:, 0:NPM].astype(BF16)
        t_ref[...] = proj[:, NPM:NP]

    (proj, tail, h), extra = _call(
        body, (x, nw, wp), grid=(t // tm,), name=name, sem=("parallel",), rider=rider,
        in_specs=[pl.BlockSpec((tm, D), lambda i: (i, 0)), pl.BlockSpec((1, D), lambda i: (0, 0)),
                  pl.BlockSpec((D, NP), lambda i: (0, 0))],
        out_specs=[pl.BlockSpec((tm, NPM), lambda i: (i, 0)), pl.BlockSpec((tm, NP - NPM), lambda i: (i, 0)),
                   pl.BlockSpec((tm, D), lambda i: (i, 0))],
        out_shape=[jax.ShapeDtypeStruct((t, NPM), BF16), jax.ShapeDtypeStruct((t, NP - NPM), F32),
                   jax.ShapeDtypeStruct((t, D), BF16)])
    return (proj, tail), h, extra


def _head_tile(xv, tgt, w):
    rs = lax.rsqrt(jnp.mean(xv * xv, axis=-1, keepdims=True) + EPS)
    xh = xv * rs
    err = xh * w - tgt
    dy = err * (1.0 / D)
    dxh = dy * w
    dx = rs * (dxh - xh * jnp.mean(dxh * xh, axis=-1, keepdims=True))
    return dx, _cs(dy * xh), (0.5 / D) * jnp.sum(err * err)


def _dxin(dp, wpt, x, dxn, nw, name, tm=512, rider=None):
    t = x.shape[0]

    def body(dp_ref, w_ref, x_ref, dxn_ref, nw_ref, dx_ref, dnw_ref):
        @pl.when(pl.program_id(0) == 0)
        def _():
            dnw_ref[...] = jnp.zeros_like(dnw_ref)

        dh = jnp.dot(dp_ref[...], w_ref[...], preferred_element_type=F32)
        xv = x_ref[...]
        rs = lax.rsqrt(jnp.mean(xv * xv, axis=-1, keepdims=True) + EPS)
        xh = xv * rs
        dnw_ref[0:1, :] += _cs(dh * xh)
        dxh = dh * nw_ref[...]
        dx_ref[...] = dxn_ref[...] + rs * (dxh - xh * jnp.mean(dxh * xh, axis=-1, keepdims=True))

    return _call(
        body, (dp, wpt, x, dxn, nw), grid=(t // tm,), name=name, sem=("arbitrary",), rider=rider,
        in_specs=[pl.BlockSpec((tm, NP), lambda i: (i, 0)), pl.BlockSpec((NP, D), lambda i: (0, 0)),
                  pl.BlockSpec((tm, D), lambda i: (i, 0)), pl.BlockSpec((tm, D), lambda i: (i, 0)),
                  pl.BlockSpec((1, D), lambda i: (0, 0))],
        out_specs=[pl.BlockSpec((tm, D), lambda i: (i, 0)), pl.BlockSpec((8, D), lambda i: (0, 0))],
        out_shape=[jax.ShapeDtypeStruct((t, D), F32), jax.ShapeDtypeStruct((8, D), F32)])


def _dwin(h, dp, name, tm=1024, rider=None):
    t = h.shape[0]

    def body(h_ref, dp_ref, o_ref):
        @pl.when(pl.program_id(0) == 0)
        def _():
            o_ref[...] = jnp.zeros_like(o_ref)

        o_ref[...] += _dot_tn(h_ref[...], dp_ref[...])

    (dwp,), extra = _call(
        body, (h, dp), grid=(t // tm,), name=name, sem=("arbitrary",), rider=rider,
        in_specs=[pl.BlockSpec((tm, D), lambda i: (i, 0)), pl.BlockSpec((tm, NP), lambda i: (i, 0))],
        out_specs=[pl.BlockSpec((D, NP), lambda i: (0, 0))], out_shape=[jax.ShapeDtypeStruct((D, NP), F32)])
    return dwp, extra


def _mixer_fwd(proj, x, wo, prm, gw, pw, cmat, mask, name, rider=None, head=None):
    proj, tail = proj
    t = proj.shape[0]
    nt, nc = t // TB, t // CH

    def body(p_ref, t_ref, x_ref, wo_ref, prm_ref, gw_ref, pw_ref, cm_ref, mk_ref, *rest):
        (tgt_ref, fw_ref), rest = (rest[:2], rest[2:]) if head else ((None, None), rest)
        mix_ref, sg_ref, ss_ref, xn_ref, xc_ref, dxc_ref, cv_ref, pool_ref = rest[:8]
        acc_ref = rest[8] if head else None
        sg_s, ss_s, h_ua, h_pu, h_sx = rest[-5:]
        i = pl.program_id(0)

        @pl.when(i == 0)
        def _():
            for r in (sg_s, ss_s, h_ua, h_pu, h_sx) + ((acc_ref,) if head else ()):
                r[...] = jnp.zeros_like(r)

        lane = _iota((1, 256), 1)
        u = p_ref[:, C_AC:C_AC + 256].astype(F32) * p_ref[:, C_AH:C_AH + 256].astype(F32)
        ext = jnp.concatenate([h_ua[...], u], axis=0)
        cv = (prm_ref[R_CAW + 2:R_CAW + 3, 0:256] * u + prm_ref[R_CAW + 1:R_CAW + 2, 0:256] * _dn(ext, 1, TB, 8)
              + prm_ref[R_CAW:R_CAW + 1, 0:256] * _dn(ext, 2, TB, 8))
        cv_ref[...] = cv.astype(BF16)
        mix_ref[:, 0:256] = (p_ref[:, C_AB:C_AB + 256].astype(F32) * cv * _silu(p_ref[:, C_AZ:C_AZ + 256].astype(F32))).astype(BF16)
        h_ua[...] = u[TB - 8:, :]
        pu = p_ref[:, C_PU:C_PU + 256].astype(F32)
        ext = jnp.concatenate([h_pu[...], pu], axis=0)
        pooled = (_winsum_dn(ext, lane)[16:] * _pool_inv_count(i, TB) - pu).astype(BF16)
        pool_ref[...] = pooled
        mixed = jnp.dot(pooled, pw_ref[...], preferred_element_type=F32)
        mix_ref[:, 512:768] = (prm_ref[R_PSC:R_PSC + 1, 0:256] * mixed * _silu(p_ref[:, C_PZ:C_PZ + 256].astype(F32))).astype(BF16)
        h_pu[...] = pu[TB - 16:, :]
        sx = p_ref[:, C_SX:C_SX + 768].astype(F32)
        ext = jnp.concatenate([h_sx[...], sx], axis=0)
        xc, dxc = _silu_pair(prm_ref[R_SCW + 3:R_SCW + 4, :] * sx + prm_ref[R_SCW + 2:R_SCW + 3, :] * _dn(ext, 1, TB, 8)
                             + prm_ref[R_SCW + 1:R_SCW + 2, :] * _dn(ext, 2, TB, 8)
                             + prm_ref[R_SCW:R_SCW + 1, :] * _dn(ext, 3, TB, 8) + prm_ref[R_SCB:R_SCB + 1, :])
        xc_ref[...] = xc.astype(BF16)
        dxc_ref[...] = dxc.astype(BF16)
        h_sx[...] = sx[TB - 8:, :]

        _, _, _, _, _, _, _, _, _, d_s, et, ut_g, ut_s = _mixer_tile_prep(p_ref, t_ref, xc, prm_ref, gw_ref[...], cm_ref, mk_ref)
        s_g, s_s = sg_s[...], ss_s[...]
        o, y = [], []
        qs = _chunks(p_ref[:, C_GQ:C_GQ + 128].astype(F32) * GLA_SCALE)
        cm = _chunks(xc[:, 512:768])
        for c in range(NCH):
            sg_ref[c] = s_g
            ss_ref[c] = s_s
            s_g = s_g * d_s[c] + ut_g[c]
            s_s = s_s * et[c] + ut_s[c]
            o.append(_dot_nt(qs[c], s_g))
            y.append(_halves(_dot, cm[c], s_s))
        sg_s[...] = s_g
        ss_s[...] = s_s
        o = jnp.concatenate(o, axis=0)
        on = o * lax.rsqrt(_dot2_l(o * o, cm_ref[2]) + EPS)
        mix_ref[:, 256:512] = (on * prm_ref[R_GNW:R_GNW + 1, 0:256] * _silu(p_ref[:, C_GZ:C_GZ + 256].astype(F32))).astype(BF16)
        y2 = ((jnp.concatenate(y, axis=0) + prm_ref[R_DE:R_DE + 1, 0:256] * xc[:, 0:256])
              * _silu(p_ref[:, C_SZ:C_SZ + 256].astype(F32)))
        mix_ref[:, 768:1024] = (y2 * lax.rsqrt(jnp.mean(y2 * y2, axis=-1, keepdims=True) + EPS)
                                * prm_ref[R_SNW:R_SNW + 1, 0:256]).astype(BF16)
        xn = x_ref[...] + jnp.dot(mix_ref[...], wo_ref[...], preferred_element_type=F32)
        if head:
            xn_ref[...], dfw, loss = _head_tile(xn, tgt_ref[...], fw_ref[...])
            acc_ref[0:1, :] += dfw
            acc_ref[1:2, :] += jnp.zeros((1, D), F32) + loss
        else:
            xn_ref[...] = xn

    row = pl.BlockSpec((TB, D), lambda i: (i, 0))
    return _call(
        body, (proj, tail, x, wo, prm, gw, pw, cmat, mask) + tuple(head or ()), grid=(nt,), name=name, sem=("arbitrary",),
        rider=rider,
        in_specs=[pl.BlockSpec((TB, NPM), lambda i: (i, 0)), pl.BlockSpec((TB, NP - NPM), lambda i: (i, 0)), row,
                  pl.BlockSpec((D, D), lambda i: (0, 0)), pl.BlockSpec((16, 768), lambda i: (0, 0)),
                  pl.BlockSpec((128, 128), lambda i: (0, 0)), pl.BlockSpec((256, 256), lambda i: (0, 0)),
                  pl.BlockSpec((4, 256, 256), lambda i: (0, 0, 0)), pl.BlockSpec((256, 128), lambda i: (0, 0))]
        + ([row, pl.BlockSpec((1, D), lambda i: (0, 0))] if head else []),
        out_specs=[row, pl.BlockSpec((NCH, 256, 128), lambda i: (i, 0, 0)),
                   pl.BlockSpec((NCH, 128, 256), lambda i: (i, 0, 0)), row] + [pl.BlockSpec((TB, 768), lambda i: (i, 0))] * 2
        + [pl.BlockSpec((TB, 256), lambda i: (i, 0))] * 2 + ([pl.BlockSpec((8, D), lambda i: (0, 0))] if head else []),
        out_shape=[jax.ShapeDtypeStruct((t, D), BF16), jax.ShapeDtypeStruct((nc, 256, 128), F32),
                   jax.ShapeDtypeStruct((nc, 128, 256), F32), jax.ShapeDtypeStruct((t, D), F32)]
        + [jax.ShapeDtypeStruct((t, 768), BF16)] * 2 + [jax.ShapeDtypeStruct((t, 256), BF16)] * 2
        + ([jax.ShapeDtypeStruct((8, D), F32)] if head else []),
        scratch_shapes=[pltpu.VMEM((256, 128), F32), pltpu.VMEM((128, 256), F32), pltpu.VMEM((8, 256), F32),
                        pltpu.VMEM((16, 256), F32), pltpu.VMEM((8, 768), F32)])


def _mixer_bwd(proj, dxn, wot, mix, sg, ss, xc16, dxc16, cv16, pool16, prm, gw, pw, cmat, mask, name, rider=None):
    proj, tail = proj
    t = proj.shape[0]
    nt = t // TB
    rev = lambda i: nt - 1 - i

    def body(p_ref, t_ref, dxn_ref, wot_ref, mix_ref, sg_ref, ss_ref, xc_ref, dxc_ref, cv_ref, pool_ref, prm_ref, gw_ref,
             pw_ref, cm_ref, mk_ref, dp_ref, sgc_ref, dwo_ref,
             gg_s, gs_s, h_dcv, h_dpl, h_dpre, gsm_ref, dgw_ref, dpw_ref, dm_ref):
        i = pl.program_id(0)
        tile = nt - 1 - i

        @pl.when(i == 0)
        def _():
            for r in (gg_s, gs_s, h_dcv, h_dpl, h_dpre, gsm_ref, dgw_ref, dpw_ref, dwo_ref):
                r[...] = jnp.zeros_like(r)

        dxn = dxn_ref[...].astype(BF16)
        dm_ref[...] = jnp.dot(dxn, wot_ref[...], preferred_element_type=F32)
        dwo_ref[...] += _dot_tn(mix_ref[...], dxn)

        lane = _iota((1, 256), 1)
        ah, ac = p_ref[:, C_AH:C_AH + 256].astype(F32), p_ref[:, C_AC:C_AC + 256].astype(F32)
        ab, az = p_ref[:, C_AB:C_AB + 256].astype(F32), p_ref[:, C_AZ:C_AZ + 256].astype(F32)
        w0, w1, w2 = (prm_ref[R_CAW + j:R_CAW + j + 1, 0:256] for j in range(3))
        u = ac * ah
        cv = cv_ref[...].astype(F32)
        g = dm_ref[:, 0:256]
        sz, dsz = _silu_pair(az)
        dp_ref[:, C_AB:C_AB + 256] = (g * cv * sz).astype(BF16)
        dp_ref[:, C_AZ:C_AZ + 256] = (g * ab * cv * dsz).astype(BF16)
        dcv = g * ab * sz
        dext = jnp.concatenate([dcv, h_dcv[...]], axis=0)
        dcv1, dcv2 = _up(dext, 1, TB), _up(dext, 2, TB)
        du = w2 * dcv + w1 * dcv1 + w0 * dcv2
        dp_ref[:, C_AC:C_AC + 256] = (du * ah).astype(BF16)
        dp_ref[:, C_AH:C_AH + 256] = (du * ac).astype(BF16)
        gsm_ref[R_CAW:R_CAW + 1, 0:256] += _cs(u * dcv2)
        gsm_ref[R_CAW + 1:R_CAW + 2, 0:256] += _cs(u * dcv1)
        gsm_ref[R_CAW + 2:R_CAW + 3, 0:256] += _cs(u * dcv)
        h_dcv[...] = dcv[0:8, :]
        pz = p_ref[:, C_PZ:C_PZ + 256].astype(F32)
        psc = prm_ref[R_PSC:R_PSC + 1, 0:256]
        icnt = _pool_inv_count(tile, TB)
        pooled = pool_ref[...]
        pw_v = pw_ref[...]
        mixed = jnp.dot(pooled, pw_v, preferred_element_type=F32)
        g = dm_ref[:, 512:768]
        sz, dsz = _silu_pair(pz)
        gsm_ref[R_PSC:R_PSC + 1, 0:256] += _cs(g * mixed * sz)
        dp_ref[:, C_PZ:C_PZ + 256] = (g * psc * mixed * dsz).astype(BF16)
        dmixed = g * psc * sz
        dpw_ref[...] += _dot_tn(pooled, dmixed)
        dpooled = _dot_nt(dmixed, pw_v)
        qd = dpooled * icnt
        dext = jnp.concatenate([qd, h_dpl[...]], axis=0)
        dp_ref[:, C_PU:C_PU + 256] = (_winsum_up(dext, lane)[:TB] - dpooled).astype(BF16)
        h_dpl[...] = qd[0:16, :]
        cw = [prm_ref[R_SCW + j:R_SCW + j + 1, :] for j in range(4)]
        xc = xc_ref[...].astype(F32)
        xs, bm, cm = xc[:, 0:256], xc[:, 256:512], xc[:, 512:768]

        gw_v = gw_ref[...]
        tail, pre, dtin, dte, dec, kd, wdec, w, xw, d_s, et, ut_g, ut_s = _mixer_tile_prep(p_ref, t_ref, xc, prm_ref,
                                                                                          gw_v, cm_ref, mk_ref)
        gmean = cm_ref[2]
        mask_t = mk_ref[...]
        gnw = prm_ref[R_GNW:R_GNW + 1, 0:256]
        a_e = prm_ref[R_AE:R_AE + 1, 0:256]
        d_e = prm_ref[R_DE:R_DE + 1, 0:256]
        snw = prm_ref[R_SNW:R_SNW + 1, 0:256]
        sg_in = [sg_ref[c] for c in range(NCH)]
        ss_in = [ss_ref[c] for c in range(NCH)]
        sg_n = [sg_in[c] * d_s[c] + ut_g[c] for c in range(NCH)]
        ss_n = [ss_in[c] * et[c] + ut_s[c] for c in range(NCH)]
        qs = _chunks(p_ref[:, C_GQ:C_GQ + 128].astype(F32) * GLA_SCALE)
        cm_c, bm_c, xw_c, kd_c = _chunks(cm), _chunks(bm), _chunks(xw), _chunks(kd)
        v_c = _chunks(p_ref[:, C_GV:C_GV + 256].astype(F32))
        o = jnp.concatenate([_dot_nt(qs[c], sg_n[c]) for c in range(NCH)], axis=0)
        y = jnp.concatenate([_halves(_dot, cm_c[c], ss_n[c]) for c in range(NCH)], axis=0) + d_e * xs
        gz = p_ref[:, C_GZ:C_GZ + 256].astype(F32)
        r = lax.rsqrt(_dot2_l(o * o, gmean) + EPS)
        on = o * r
        dyb = dm_ref[:, 256:512]
        sz, dsz = _silu_pair(gz)
        dp_ref[:, C_GZ:C_GZ + 256] = (dyb * on * gnw * dsz).astype(BF16)
        tg = dyb * sz
        gsm_ref[R_GNW:R_GNW + 1, 0:256] += _cs(tg * on)
        don = tg * gnw
        do_c = _chunks(r * (don - on * _dot2_l(don * on, gmean)))
        ssz = p_ref[:, C_SZ:C_SZ + 256].astype(F32)
        sil, dsil = _silu_pair(ssz)
        y2 = y * sil
        r = lax.rsqrt(jnp.mean(y2 * y2, axis=-1, keepdims=True) + EPS)
        yn = y2 * r
        dyd = dm_ref[:, 768:1024]
        gsm_ref[R_SNW:R_SNW + 1, 0:256] += _cs(dyd * yn)
        dn = dyd * snw
        dy2 = r * (dn - yn * jnp.mean(dn * yn, axis=-1, keepdims=True))
        dp_ref[:, C_SZ:C_SZ + 256] = (dy2 * y * dsil).astype(BF16)
        dy = dy2 * sil
        gsm_ref[R_DE:R_DE + 1, 0:256] += _cs(dy * xs)
        dy_c = _chunks(dy)
        dq = jnp.concatenate([_dot(do_c[c], sg_n[c]) for c in range(NCH)], axis=0)
        dp_ref[:, C_GQ:C_GQ + 128] = (dq * GLA_SCALE).astype(BF16)
        dcm = jnp.concatenate([_halves(_dot_nt, dy_c[c], ss_n[c]) for c in range(NCH)], axis=0)
        gg = [_dot_tn(do_c[c], qs[c]) * mask_t for c in range(NCH)]
        gs = [_halves(_dot_tn, cm_c[c], dy_c[c]) for c in range(NCH)]
        car_g, car_s = gg_s[...], gs_s[...]
        for c in reversed(range(NCH)):
            gg[c] = gg[c] + car_g
            gs[c] = gs[c] + car_s
            car_g = gg[c] * d_s[c]
            car_s = gs[c] * et[c]
        gg_s[...] = car_g
        gs_s[...] = car_s
        dkd = jnp.concatenate([_dot(v_c[c], gg[c]) for c in range(NCH)], axis=0)
        dp_ref[:, C_GV:C_GV + 256] = jnp.concatenate([_dot_nt(kd_c[c], gg[c]) for c in range(NCH)], axis=0).astype(BF16)
        dp_ref[:, C_GK:C_GK + 128] = (dkd * dec).astype(BF16)
        dbm = jnp.concatenate([_halves(_dot_nt, xw_c[c], gs[c]) for c in range(NCH)], axis=0)
        dxw = jnp.concatenate([_halves(_dot, bm_c[c], gs[c]) for c in range(NCH)], axis=0)
        dxs = dy * d_e + dxw * w
        dw = dxw * xs
        dsuf = _chunk_sums(cm_ref[1], jnp.concatenate([dkd * kd, dw * dte * wdec], axis=1))
        tot_g = jnp.concatenate([jnp.broadcast_to(_cs(gg[c] * sg_in[c]) * d_s[c], (CH, 128)) for c in range(NCH)], axis=0)
        tot_s = jnp.concatenate([jnp.broadcast_to(_cs(gs[c] * ss_in[c]) * et[c], (CH, 256)) for c in range(NCH)], axis=0)
        dpre = (dsuf[:, 0:128] + tot_g) * INV_TAU * jax.nn.sigmoid(-pre)
        dgw_ref[...] += _dot_tn(tail, dpre)
        gsm_ref[R_GB:R_GB + 1, 0:128] += _cs(dpre)
        dda = dsuf[:, 128:384] + tot_s
        gsm_ref[R_AE:R_AE + 1, 0:256] += _cs(dda * dte)
        dtail_s = _dot2_nt(dw * wdec + dda * a_e, cm_ref[3, 0:128, :]) * jax.nn.sigmoid(dtin)
        gsm_ref[R_DTB:R_DTB + 1, 0:128] += _cs(dtail_s)
        dp_ref[:, C_TL:C_TL + 128] = (_dot_nt(dpre, gw_v) + dtail_s).astype(BF16)
        dpre_c = jnp.concatenate([dxs, dbm, dcm], axis=1) * dxc_ref[...].astype(F32)
        dext = jnp.concatenate([dpre_c, h_dpre[...]], axis=0)
        ups = [dpre_c, _up(dext, 1, TB), _up(dext, 2, TB), _up(dext, 3, TB)]
        dp_ref[:, C_SX:C_SX + 768] = (cw[3] * ups[0] + cw[2] * ups[1] + cw[1] * ups[2] + cw[0] * ups[3]).astype(BF16)
        sx = p_ref[:, C_SX:C_SX + 768].astype(F32)
        for k in range(4):
            gsm_ref[R_SCW + k:R_SCW + k + 1, :] += _cs(sx * ups[3 - k])
        gsm_ref[R_SCB:R_SCB + 1, :] += _cs(dpre_c)
        h_dpre[...] = dpre_c[0:8, :]

        @pl.when(i == nt - 1)
        def _():
            ri, ci = _iota((256, 256), 0), _iota((256, 256), 1)
            per_head = jnp.where((ri >> 6) == ci, 1.0, 0.0).astype(BF16)
            per_dv = jnp.where((ri & 63) == ci, 1.0, 0.0).astype(BF16)
            row = _iota((8, 256), 0)
            top = gsm_ref[0:8, 0:256]
            sgc_ref[0:8, 0:256] = jnp.where(row == R_GNW, _dot3_l(top, per_dv), top)
            bot = gsm_ref[8:16, 0:256]
            fold = _dot3_l(jnp.where(row == R_AE - 8, bot * a_e, bot), per_head)
            sgc_ref[8:16, 0:256] = jnp.where((row == R_AE - 8) | (row == R_DE - 8), fold, bot)
            sgc_ref[0:16, 256:768] = gsm_ref[:, 256:768]
            sgc_ref[0:16, 768:896] = dgw_ref[0:16, :]
            sgc_ref[0:16, 896:1024] = jnp.zeros((16, 128), F32)
            diag = _pool_lane_select(lane, dpw_ref[0:64, :], dpw_ref[64:128, :], dpw_ref[128:192, :], dpw_ref[192:256, :])
            for q in range(4):
                sgc_ref[16:32, 256 * q:256 * q + 256] = diag[16 * q:16 * q + 16, :]

    return _call(
        body, (proj, tail, dxn, wot, mix, sg, ss, xc16, dxc16, cv16, pool16, prm, gw, pw, cmat, mask), grid=(nt,), name=name,
        sem=("arbitrary",), rider=rider,
        in_specs=[pl.BlockSpec((TB, NPM), lambda i: (rev(i), 0)),
                  pl.BlockSpec((TB, NP - NPM), lambda i: (rev(i), 0)),
                  pl.BlockSpec((TB, D), lambda i: (rev(i), 0)), pl.BlockSpec((D, D), lambda i: (0, 0)),
                  pl.BlockSpec((TB, D), lambda i: (rev(i), 0)),
                  pl.BlockSpec((NCH, 256, 128), lambda i: (rev(i), 0, 0)),
                  pl.BlockSpec((NCH, 128, 256), lambda i: (rev(i), 0, 0)),
                  pl.BlockSpec((TB, 768), lambda i: (rev(i), 0)), pl.BlockSpec((TB, 768), lambda i: (rev(i), 0)),
                  pl.BlockSpec((TB, 256), lambda i: (rev(i), 0)), pl.BlockSpec((TB, 256), lambda i: (rev(i), 0)),
                  pl.BlockSpec((16, 768), lambda i: (0, 0)), pl.BlockSpec((128, 128), lambda i: (0, 0)),
                  pl.BlockSpec((256, 256), lambda i: (0, 0)), pl.BlockSpec((4, 256, 256), lambda i: (0, 0, 0)),
                  pl.BlockSpec((256, 128), lambda i: (0, 0))],
        out_specs=[pl.BlockSpec((TB, NP), lambda i: (rev(i), 0)), pl.BlockSpec((32, 1024), lambda i: (0, 0)),
                   pl.BlockSpec((D, D), lambda i: (0, 0))],
        out_shape=[jax.ShapeDtypeStruct((t, NP), BF16), jax.ShapeDtypeStruct((32, 1024), F32),
                   jax.ShapeDtypeStruct((D, D), F32)],
        scratch_shapes=[pltpu.VMEM((256, 128), F32), pltpu.VMEM((128, 256), F32), pltpu.VMEM((8, 256), F32),
                        pltpu.VMEM((16, 256), F32), pltpu.VMEM((8, 768), F32), pltpu.VMEM((16, 768), F32),
                        pltpu.VMEM((128, 128), F32), pltpu.VMEM((256, 256), F32), pltpu.VMEM((TB, D), F32)])


SHARD = NPROJ // 4
SHARD_PAD = 896


def _ranges_to_perm(o, n):
    out, p = [], 0
    for start, size in _PERM:
        a, b = max(o, start), min(o + n, start + size)
        if a < b:
            out.append((a, b - a, p + a - start))
        p += size
    return out


def _ranges_to_orig(p0, n):
    out, p = [], 0
    for start, size in _PERM:
        a, b = max(p0, p), min(p0 + n, p + size)
        if a < b:
            out.append((a, b - a, start + a - p))
        p += size
    return out


def _lane_window(load, lo, n, d, lane):
    a = 128 * (lo // 128)
    off = lo - a
    w = 128 if off + n <= 128 else 256
    chunk = load(a, w)
    shift = (d - off) % w
    if shift:
        chunk = pltpu.roll(chunk, shift, axis=1)
    return jnp.where((lane >= d) & (lane < d + n), chunk[:, 0:128], 0.0)


def _assemble_w_in(slabs, name, rb=256):
    def body(s_ref, wp_ref, wpt_ref):
        lane = _iota((1, 128), 1)
        for b in range(NP // 128):
            acc = jnp.zeros((rb, 128), F32)
            for p, n, o in _ranges_to_orig(128 * b, 128):
                while n > 0:
                    s, lo = o // SHARD, o % SHARD
                    cnt = min(n, SHARD - lo)
                    acc = acc + _lane_window(lambda a, w, s=s: s_ref[s, :, a:a + w].astype(F32), lo, cnt, p - 128 * b, lane)
                    o, p, n = o + cnt, p + cnt, n - cnt
            wp_ref[:, 128 * b:128 * b + 128] = acc.astype(BF16)
            wpt_ref[128 * b:128 * b + 128, :] = acc.T.astype(BF16)

    return pl.pallas_call(
        body, grid=(D // rb,), name=name,
        in_specs=[pl.BlockSpec((4, rb, SHARD_PAD), lambda i: (0, i, 0))],
        out_specs=[pl.BlockSpec((rb, NP), lambda i: (i, 0)), pl.BlockSpec((NP, rb), lambda i: (0, i))],
        out_shape=[jax.ShapeDtypeStruct((D, NP), BF16), jax.ShapeDtypeStruct((NP, D), BF16)],
        compiler_params=_cparams(("parallel",)))(slabs)


def _split_dw_in(dwp, name, rb=256):
    rows = dwp.shape[0]

    def body(g_ref, o_ref):
        lane = _iota((1, 128), 1)
        for s in range(4):
            for k in range(SHARD_PAD // 128):
                acc = jnp.zeros((rb, 128), F32)
                n_valid = min(128, SHARD - 128 * k)
                for o, n, p in _ranges_to_perm(SHARD * s + 128 * k, n_valid):
                    acc = acc + _lane_window(lambda a, w: g_ref[:, a:a + w].astype(F32), p, n, o - SHARD * s - 128 * k, lane)
                o_ref[s, :, 128 * k:128 * k + 128] = acc.astype(o_ref.dtype)

    return pl.pallas_call(
        body, grid=(rows // rb,), name=name,
        in_specs=[pl.BlockSpec((rb, NP), lambda i: (i, 0))],
        out_specs=pl.BlockSpec((4, rb, SHARD_PAD), lambda i: (0, i, 0)),
        out_shape=jax.ShapeDtypeStruct((4, rows, SHARD_PAD), dwp.dtype),
        compiler_params=_cparams(("parallel",)))(dwp)


def _half(c, n):
    return pl.ds(pl.multiple_of(c * (n // 2), n // 2), n // 2)


def _other_chips(x, y):
    return ((1 - x, y), (x, 1 - y), (1 - x, 1 - y))


def _remote(src, dst, send, recv, k, dev):
    return pltpu.make_async_remote_copy(src_ref=src, dst_ref=dst, send_sem=send.at[k], recv_sem=recv.at[k], device_id=dev,
                                        device_id_type=MESH)


def _sem(n):
    return pltpu.SemaphoreType.DMA((n,))


def _rider_gather_ici(shards):
    shards = tuple(shards)
    n = len(shards)

    def copies(rins, routs, sems, arrivals=True):
        send, recv = sems
        x, y, c = _place()
        me = 2 * x + y
        out, inc = [], []
        for j, (px, py) in enumerate(_other_chips(x, y)):
            for k in range(n):
                rows = _half(c, shards[k].shape[0])
                out.append(_remote(rins[k].at[rows], routs[k].at[me, rows], send, recv, n * j + k, (px, py, c)))
                if arrivals:
                    inc.append(_remote(rins[k].at[rows], routs[k].at[2 * px + py, rows], send, recv, n * j + k, (px, py, c)))
        return out, inc

    def start(rins, routs, sems):
        for cp in copies(rins, routs, sems, arrivals=False)[0]:
            cp.start()

    def finish(rins, routs, sems):
        out, inc = copies(rins, routs, sems)
        for cp in inc:
            cp.wait_recv()
        for cp in out:
            cp.wait_send()

    return _Rider(shards, [jax.ShapeDtypeStruct((4,) + a.shape, a.dtype) for a in shards], [_sem(3 * n), _sem(3 * n)],
                  start, finish)


def _gather_ici_two_hops(shards, extra):
    shards = tuple(shards)
    n = len(shards)

    def body(*refs):
        ins, e_in, outs, e_out = refs[:n], refs[n], refs[n + 1:2 * n + 1], refs[2 * n + 1]
        send, recv = refs[2 * n + 2:]
        x, y, c = _place()
        slab = lambda px, py: 2 * px + py
        xn, yn, dg = (1 - x, y), (x, 1 - y), (1 - x, 1 - y)

        def part(k, q):
            r = shards[k].shape[0] // 4
            return pl.ds(pl.multiple_of(c * 2 * r + q * r, r), r)

        def hop(k, q, src_chip, to, sem):
            rows = part(k, q)
            src = ins[k].at[rows] if src_chip is None else outs[k].at[slab(*src_chip), rows]
            own = (x, y) if src_chip is None else src_chip
            return _remote(src, outs[k].at[slab(*own), rows], send, recv, sem, (*to, c))

        small = [_remote(e_in, e_out.at[slab(x, y)], send, recv, 6 * n + j, (*to, c)) for j, to in enumerate((xn, yn, dg))]
        first = [hop(k, q, None, (xn, yn)[q], 2 * k + q) for k in range(n) for q in (0, 1)]
        for cp in small + first:
            cp.start()
        for k in range(n):
            for q in (0, 1):
                nb = (xn, yn)[q]
                _remote(ins[k].at[part(k, q)], outs[k].at[slab(*nb), part(k, q)], send, recv, 2 * k + q, (*nb, c)).wait_recv()
        second = []
        for k in range(n):
            for q in (0, 1):
                to, via = (yn, xn)[q], (xn, yn)[q]
                second.append(hop(k, q, None, to, 2 * n + 4 * k + 2 * q))
                second.append(hop(k, q, via, to, 2 * n + 4 * k + 2 * q + 1))
        for cp in second:
            cp.start()
        for k in range(n):
            for q in (0, 1):
                frm, rows = (yn, xn)[q], part(k, q)
                for j, origin in enumerate((frm, dg)):
                    _remote(ins[k].at[rows], outs[k].at[slab(*origin), rows], send, recv, 2 * n + 4 * k + 2 * q + j,
                            (*frm, c)).wait_recv()
        for j, frm in enumerate((xn, yn, dg)):
            _remote(e_in, e_out.at[slab(*frm)], send, recv, 6 * n + j, (*frm, c)).wait_recv()
        for cp in small + first + second:
            cp.wait_send()

    outs = pl.pallas_call(
        body, name="gather_ici0", in_specs=[_ANY] * (n + 1), out_specs=[_ANY] * (n + 1),
        out_shape=[jax.ShapeDtypeStruct((4,) + a.shape, a.dtype) for a in shards + (extra,)],
        scratch_shapes=[_sem(6 * n + 3), _sem(6 * n + 3)])(*shards, extra)
    return list(outs)


def _rider_gather_d2d(slabs):
    slabs = tuple(slabs)
    n = len(slabs)

    def copies(routs, sems, arrivals=True):
        send, recv = sems
        x, y, c = _place()
        out, inc = [], []
        for j, (px, py) in enumerate(_other_chips(x, y)):
            for k in range(n):
                rows = slabs[k].shape[1]
                mine, theirs = routs[k].at[2 * px + py, _half(c, rows)], routs[k].at[2 * px + py, _half(1 - c, rows)]
                out.append(_remote(mine, mine, send, recv, n * j + k, (x, y, 1 - c)))
                if arrivals:
                    inc.append(_remote(theirs, theirs, send, recv, n * j + k, (x, y, 1 - c)))
        return out, inc

    def start(rins, routs, sems):
        for cp in copies(routs, sems, arrivals=False)[0]:
            cp.start()

    def finish(rins, routs, sems):
        out, inc = copies(routs, sems)
        for cp in inc:
            cp.wait_recv()
        for cp in out:
            cp.wait_send()

    return _Rider(slabs, [jax.ShapeDtypeStruct(a.shape, a.dtype) for a in slabs], [_sem(3 * n), _sem(3 * n)], start, finish,
                  aliases={k: k for k in range(n)})


def _rider_swap(parts):
    parts = tuple(parts)
    n = len(parts)

    def copies(rins, routs, sems):
        send, recv = sems
        x, y, c = _place()
        return [_remote(rins[k].at[:, _half(1 - c, parts[k].shape[1])], routs[k], send, recv, k, (x, y, 1 - c))
                for k in range(n)]

    def start(rins, routs, sems):
        for cp in copies(rins, routs, sems):
            cp.start()

    def finish(rins, routs, sems):
        for cp in copies(rins, routs, sems):
            cp.wait()

    return _Rider(parts, [jax.ShapeDtypeStruct((a.shape[0], a.shape[1] // 2, a.shape[2]), a.dtype) for a in parts],
                  [_sem(n), _sem(n)], start, finish)


def _rider_scatter(parts):
    parts = tuple(parts)
    n = len(parts)

    def copies(rins, routs, sems, arrivals=True):
        send, recv = sems
        x, y, c = _place()
        me = 2 * x + y
        out, inc = [], []
        for j, (px, py) in enumerate(_other_chips(x, y)):
            for k in range(n):
                out.append(_remote(rins[k].at[2 * px + py], routs[k].at[me], send, recv, n * j + k, (px, py, c)))
                if arrivals:
                    inc.append(_remote(rins[k].at[me], routs[k].at[2 * px + py], send, recv, n * j + k, (px, py, c)))
        return out, inc

    def start(rins, routs, sems):
        for cp in copies(rins, routs, sems, arrivals=False)[0]:
            cp.start()

    def finish(rins, routs, sems):
        out, inc = copies(rins, routs, sems)
        for cp in inc:
            cp.wait_recv()
        for cp in out:
            cp.wait_send()

    return _Rider(parts, [jax.ShapeDtypeStruct(a.shape, a.dtype) for a in parts], [_sem(3 * n), _sem(3 * n)], start, finish)


def _rider_share(fulls):
    fulls = tuple(fulls)
    n = len(fulls)

    def copies(routs, sems, arrivals=True):
        send, recv = sems
        x, y, c = _place()
        out, inc = [], []
        for k in range(n):
            mine, theirs = routs[k].at[_half(c, fulls[k].shape[0])], routs[k].at[_half(1 - c, fulls[k].shape[0])]
            out.append(_remote(mine, mine, send, recv, k, (x, y, 1 - c)))
            if arrivals:
                inc.append(_remote(theirs, theirs, send, recv, k, (x, y, 1 - c)))
        return out, inc

    def start(rins, routs, sems):
        for cp in copies(routs, sems, arrivals=False)[0]:
            cp.start()

    def finish(rins, routs, sems):
        out, inc = copies(routs, sems)
        for cp in inc:
            cp.wait_recv()
        for cp in out:
            cp.wait_send()

    return _Rider(fulls, [jax.ShapeDtypeStruct(a.shape, a.dtype) for a in fulls], [_sem(n), _sem(n)], start, finish,
                  aliases={k: k for k in range(n)})


def _pair_sum(core, full, recv, name, br=128):
    n, rows, cols = recv.shape

    def body(c_ref, a_ref, b_ref, o_ref):
        o_ref[...] = (a_ref[...] + b_ref[...]).astype(BF16)

    nb = rows // br
    return pl.pallas_call(
        body, name=name, out_shape=jax.ShapeDtypeStruct(recv.shape, BF16),
        grid_spec=pltpu.PrefetchScalarGridSpec(
            num_scalar_prefetch=1, grid=(n, nb),
            in_specs=[pl.BlockSpec((1, br, cols), lambda i, j, c: (i, c[0] * nb + j, 0)),
                      pl.BlockSpec((1, br, cols), lambda i, j, c: (i, j, 0))],
            out_specs=pl.BlockSpec((1, br, cols), lambda i, j, c: (i, j, 0))),
        compiler_params=_cparams(("parallel", "parallel")))(core, full, recv)


def _chip_sum(place, gathered, mine, name, br=128):
    _, r, c = gathered.shape
    nb = r // br

    def body(p_ref, g_ref, m_ref, o_ref):
        slab = lambda j: jnp.where(p_ref[1] == j, m_ref[j], g_ref[j]).astype(F32)
        o_ref[...] = ((slab(0) + slab(1)) + slab(2)) + slab(3)

    return pl.pallas_call(
        body, name=name, out_shape=jax.ShapeDtypeStruct((2 * r, c), F32),
        grid_spec=pltpu.PrefetchScalarGridSpec(
            num_scalar_prefetch=1, grid=(nb,),
            in_specs=[pl.BlockSpec((4, br, c), lambda i, p: (0, i, 0)), pl.BlockSpec((4, br, c), lambda i, p: (0, i, 0))],
            out_specs=pl.BlockSpec((br, c), lambda i, p: (p[0] * nb + i, 0))),
        compiler_params=_cparams(("parallel",)))(place, gathered, mine)


def _adamw(w, g, m, v, name, br):
    n, r, c = w.shape

    def body(w_ref, g_ref, m_ref, v_ref, d_ref, m2_ref, v2_ref):
        d_ref[...], m2_ref[...], v2_ref[...] = _adam_math(w_ref[...], g_ref[...], m_ref[...], v_ref[...])

    spec = pl.BlockSpec((1, br, c), lambda i, j: (i, j, 0))
    shp = jax.ShapeDtypeStruct(w.shape, F32)
    return pl.pallas_call(body, grid=(n, r // br), name=name, in_specs=[spec] * 4, out_specs=[spec] * 3,
                          out_shape=[shp] * 3, compiler_params=_cparams(("parallel", "parallel")))(w, g, m, v)


def _adamw_w_in(w, g, m, v, name, bc=93):
    cols = w.shape[2]
    lead = lambda a: jnp.transpose(a, (2, 0, 1))
    g = jnp.stack([a[:, 0:cols] for a in g])

    def body(w_ref, g_ref, m_ref, v_ref, go_ref, d_ref, m2_ref, v2_ref):
        for l in range(2):
            gv = g_ref[:, l, :]
            d_ref[:, l, :], m2_ref[:, l, :], v2_ref[:, l, :] = _adam_math(w_ref[:, l, :], gv, m_ref[:, l, :], v_ref[:, l, :])
            go_ref[:, l, :] = gv

    spec = pl.BlockSpec((bc, 2, D), lambda i: (i, 0, 0))
    outs = pl.pallas_call(body, grid=(cols // bc,), name=name, in_specs=[spec] * 4, out_specs=[spec] * 4,
                          out_shape=[jax.ShapeDtypeStruct((cols, 2, D), F32)] * 4,
                          compiler_params=_cparams(("parallel",)))(lead(w), lead(g), lead(m), lead(v))
    return [jnp.transpose(o, (1, 2, 0)) for o in outs]


_SMALL_NAMES = ("norm_w", "conv_a_w", "gla_gate_w", "gla_gate_b", "gla_norm_w", "pool_w", "pool_scale", "ssd_conv_w",
                "ssd_conv_b", "ssd_dt_bias", "ssd_a_log", "ssd_d", "ssd_norm_w", "final_norm_w")
SMALL_ROWS = 80


def _adam_math(w, g, m, v):
    m2 = ADAM_B1 * m + (1.0 - ADAM_B1) * g
    v2 = ADAM_B2 * v + (1.0 - ADAM_B2) * (g * g)
    m_hat = m2 / (1.0 - ADAM_B1 ** ADAM_STEP)
    v_hat = v2 / (1.0 - ADAM_B2 ** ADAM_STEP)
    return -ADAM_LR * (m_hat / (jnp.sqrt(v_hat) + ADAM_EPS) + ADAM_WD * w), m2, v2


def _small_slices(name, chip):
    if name == "conv_a_w":
        return [((), slice(R_CAW, R_CAW + 3), slice(64 * chip, 64 * chip + 64))]
    if name == "ssd_conv_w":
        return [((), slice(R_SCW, R_SCW + 4), slice(192 * chip, 192 * chip + 192))]
    if name == "gla_gate_w":
        return [((), slice(0, 16), slice(768, 896))]
    if name == "pool_w":
        return [((g, slice(16 * q, 16 * q + 16)), slice(16, 32), slice(256 * q + 64 * g, 256 * q + 64 * g + 64))
                for g in range(4) for q in range(4)]
    row, lanes = {"gla_gate_b": (R_GB, slice(0, 128)), "gla_norm_w": (R_GNW, slice(0, 64)),
                  "pool_scale": (R_PSC, slice(0, 256)), "ssd_conv_b": (R_SCB, slice(0, 768)),
                  "ssd_dt_bias": (R_DTB, slice(16, 20)), "ssd_a_log": (R_AE, slice(0, 4)), "ssd_d": (R_DE, slice(0, 4)),
                  "ssd_norm_w": (R_SNW, slice(0, 256))}[name]
    return [((), slice(row, row + 1), lanes)]


def _rider_exchange(block):
    def copies(rins, routs, sems):
        send, recv = sems
        x, y, c = _place()
        flip = lambda v, bit: 1 - v if bit else v
        return [_remote(rins[0], routs[0].at[k], send, recv, k - 1, (flip(x, k & 4), flip(y, k & 2), flip(c, k & 1)))
                for k in range(1, 8)]

    def start(rins, routs, sems):
        for cp in copies(rins, routs, sems):
            cp.start()

    def finish(rins, routs, sems):
        for cp in copies(rins, routs, sems):
            cp.wait()

    return _Rider((block,), [jax.ShapeDtypeStruct((8,) + block.shape, block.dtype)], [_sem(7), _sem(7)], start, finish)


def _join_riders(a, b):
    na, oa, sa = len(a.inputs), len(a.out_shapes), len(a.sems)

    def start(rins, routs, sems):
        a.start(rins[:na], routs[:oa], sems[:sa])
        b.start(rins[na:], routs[oa:], sems[sa:])

    def finish(rins, routs, sems):
        a.finish(rins[:na], routs[:oa], sems[:sa])
        b.finish(rins[na:], routs[oa:], sems[sa:])

    aliases = {**a.aliases, **{na + k: oa + v for k, v in b.aliases.items()}}
    return _Rider(a.inputs + b.inputs, a.out_shapes + b.out_shapes, a.sems + b.sems, start, finish, aliases)


def _small_adamw(blocks, w, m, v):
    n = len(_SMALL_NAMES)

    def body(*refs):
        (own, ex), (own0, ex0) = refs[0:2], refs[2:4]
        refs = refs[3:]
        w_refs, m_refs, v_refs = refs[1:1 + n], refs[1 + n:1 + 2 * n], refs[1 + 2 * n:1 + 3 * n]
        o = 1 + 3 * n
        g_out, d_out, m_out, v_out = refs[o:o + n], refs[o + n:o + 2 * n], refs[o + 2 * n:o + 3 * n], refs[o + 3 * n:o + 4 * n]
        loss_ref, acc, acc0 = refs[o + 4 * n:o + 4 * n + 3]
        chip = 2 * lax.axis_index("x") + lax.axis_index("y")
        me = 2 * chip + lax.axis_index("c")
        acc[...] = jnp.zeros_like(acc)
        acc0[...] = jnp.zeros_like(acc0)
        for src in range(8):
            @pl.when(me == src)
            def _():
                acc[...] += own[...]
                acc0[...] += own0[...]

            @pl.when(me != src)
            def _(src=src):
                acc[...] += ex[jnp.bitwise_xor(me, src)]
                acc0[...] += ex0[jnp.bitwise_xor(me, src)]

        loss_ref[...] = acc[73:74, 0:1]

        def update(i, idx, g):
            d, m2, v2 = _adam_math(w_refs[i][idx], g, m_refs[i][idx], v_refs[i][idx])
            g_out[i][idx], d_out[i][idx], m_out[i][idx], v_out[i][idx] = g, d, m2, v2

        for i, name in enumerate(_SMALL_NAMES):
            if name == "final_norm_w":
                update(i, (slice(0, 1), slice(None)), acc[72:73, :])
            elif name == "norm_w":
                update(i, (slice(0, 1), slice(None)), acc0[0:1, :])
                update(i, (slice(1, 2), slice(None)), acc[64:65, :])
            elif name in ("conv_a_w", "ssd_conv_w"):
                for s in range(4):
                    @pl.when(chip == s)
                    def _(i=i, name=name, s=s):
                        for l in range(2):
                            (_, rows, lanes), = _small_slices(name, s)
                            update(i, (l,), acc[rows.start + 32 * l:rows.stop + 32 * l, lanes])
            else:
                for l in range(2):
                    for idx, rows, lanes in _small_slices(name, 0):
                        g = acc[rows.start + 32 * l:rows.stop + 32 * l, lanes]
                        if w_refs[i].ndim == 2:
                            update(i, (slice(l, l + 1), slice(None)), g)
                        else:
                            update(i, (l,) + idx, g)

    args = [a for pair in blocks for a in pair] + [d[k] for d in (w, m, v) for k in _SMALL_NAMES]
    shapes = [jax.ShapeDtypeStruct(w[k].shape, F32) for k in _SMALL_NAMES]
    vmem = pl.BlockSpec(memory_space=pltpu.VMEM)
    outs = pl.pallas_call(body, name="small_adamw", in_specs=[vmem] * len(args), out_specs=[vmem] * (4 * n + 1),
                          out_shape=shapes * 4 + [jax.ShapeDtypeStruct((1, 1), F32)],
                          scratch_shapes=[pltpu.VMEM((SMALL_ROWS, D), F32), pltpu.VMEM((8, D), F32)])(*args)
    return outs[0:n], outs[n:2 * n], outs[2 * n:3 * n], outs[3 * n:4 * n], outs[4 * n]


def _mixer_consts(layer, conv_a_w, gla_gate_w, gla_gate_b, gla_norm_w, pool_w, pool_scale, ssd_conv_w, ssd_conv_b,
                  ssd_dt_bias, ssd_a_log, ssd_d, ssd_norm_w):
    def row(v):
        return jnp.pad(v.reshape(1, -1), ((0, 0), (0, 768 - v.size)))

    dtb = jnp.pad(ssd_dt_bias[layer], (16, 108))
    rows = [jnp.pad(conv_a_w[layer], ((0, 0), (0, 512))), row(gla_gate_b[layer]), row(jnp.tile(gla_norm_w[layer], 4)),
            row(pool_scale[layer]), row(ssd_conv_b[layer]), row(dtb), row(jnp.repeat(-jnp.exp(ssd_a_log[layer]), 64)),
            row(jnp.repeat(ssd_d[layer], 64)), row(ssd_norm_w[layer]), jnp.zeros((1, 768), F32), ssd_conv_w[layer]]
    prm = jnp.concatenate(rows, axis=0)
    gw = jnp.pad(gla_gate_w[layer], ((0, 112), (0, 0))).astype(BF16)
    on_diag = (_iota((256, 256), 0) >> 6) == (_iota((256, 256), 1) >> 6)
    pw = jnp.where(on_diag, jnp.tile(pool_w[layer].reshape(256, 64), (1, 4)), 0.0)
    return (prm, gw, pw.astype(BF16)) + _mixer_matrices()


def _grad_slabs(dwp, dwo):
    return dwp.reshape(1, D, NP), dwo.reshape(4, D // 4, D)


class _Comm:
    def __init__(self, w_in, w_out):
        self.w_in16 = jnp.pad(w_in.astype(BF16), ((0, 0), (0, 0), (0, SHARD_PAD - SHARD)))
        self.w_out16 = w_out.astype(BF16)
        self.core = lax.axis_index("c").astype(jnp.int32).reshape(1)
        self.chip = 2 * lax.axis_index("x") + lax.axis_index("y")
        self.place = jnp.stack([lax.axis_index("c"), self.chip]).astype(jnp.int32)

    def gather_ici(self, layer):
        return _rider_gather_ici((self.w_in16[layer], self.w_out16[layer]))

    def pair_sum(self, layer, slabs, received):
        d_in, d_out = [_pair_sum(self.core, a, b, name=f"reduce_pair_sum{layer}_{k}")
                       for k, (a, b) in enumerate(zip(slabs, received))]
        return [_split_dw_in(d_in[0], name=f"split_dw_in{layer}"), d_out]

    def chip_sum(self, layer, gathered, mine):
        return [_chip_sum(self.place, a, b, name=f"reduce_chip_sum{layer}_{k}") for k, (a, b) in enumerate(zip(gathered, mine))]

    def layer_weights(self, layer, s_in, s_out):
        own = lambda slabs, shard: jnp.stack([jnp.where(self.chip == s, shard, slabs[s]) for s in range(4)])
        wp, wpt = _assemble_w_in(own(s_in, self.w_in16[layer]), name=f"assemble_w_in{layer}")
        wo = own(s_out, self.w_out16[layer]).reshape(D, D)
        return wp, wpt, wo, wo.T


def _local_step(x, tgt, norm_w, final_norm_w, consts, wts0, wts1=None, comm=None):
    nw = [norm_w[l:l + 1] for l in range(2)]
    proj0, h0, slabs = _rmsproj(x, nw[0], wts0[0], name="rmsproj0", rider=comm and comm.gather_ici(1))
    (mix0, sg0, ss0, x1, *conv0), slabs = _mixer_fwd(proj0, x, wts0[2], *consts[0], name="mixer_fwd0",
                                                     rider=comm and _rider_gather_d2d(slabs))
    if comm:
        wts1 = comm.layer_weights(1, *slabs)
    proj1, h1, _ = _rmsproj(x1, nw[1], wts1[0], name="rmsproj1")
    (mix1, sg1, ss1, dx, *conv1, head), _ = _mixer_fwd(proj1, x1, wts1[2], *consts[1], name="mixer_fwd1",
                                                       head=(tgt, final_norm_w.reshape(1, D)))
    (dproj, mgr1, dwo1), _ = _mixer_bwd(proj1, dx, wts1[3], mix1, sg1, ss1, *conv1, *consts[1], name="mixer_bwd1")
    dwp1, _ = _dwin(h1, dproj, name="dwin1")
    slabs1 = comm and _grad_slabs(dwp1, dwo1)
    (dx, dnw1), recv = _dxin(dproj, wts1[1], x1, dx, nw[1], name="dxin1", rider=comm and _rider_swap(slabs1))
    pairs1 = comm and comm.pair_sum(1, slabs1, recv)
    (dproj, mgr0, dwo0), gathered = _mixer_bwd(proj0, dx, wts0[3], mix0, sg0, ss0, *conv0, *consts[0], name="mixer_bwd0",
                                               rider=comm and _rider_scatter(pairs1))
    if not comm:
        dwp0, _ = _dwin(h0, dproj, name="dwin0")
        (dx, dnw0), _ = _dxin(dproj, wts0[1], x, dx, nw[0], name="dxin0")
        return head, dx, ((dwp0, dwp1), (dwo0, dwo1)), (dnw0, dnw1), (mgr0, mgr1)
    dwo0 = dwo0.reshape(4, D // 4, D)
    dwp0, (*big1, recv_out) = _dwin(h0, dproj, name="dwin0", rider=_join_riders(
        _rider_share(comm.chip_sum(1, gathered, pairs1)), _rider_swap((dwo0,))))
    slabs0 = (dwp0.reshape(1, D, NP), dwo0)
    pairs0 = comm.pair_sum(0, slabs0, (_run_rider(_rider_swap(slabs0[0:1]), "reduce_swap0")[0], recv_out))
    small = jnp.concatenate([mgr0, mgr1, dnw1, head], axis=0)
    (dx, dnw0), gathered = _dxin(dproj, wts0[1], x, dx, nw[0], name="dxin0",
                                 rider=_join_riders(_rider_scatter(pairs0), _rider_exchange(small)))
    last = _run_rider(_join_riders(_rider_share(comm.chip_sum(0, gathered[0:2], pairs0)), _rider_exchange(dnw0)),
                      "reduce_share0")
    return dx, ((last[0], big1[0]), (last[1], big1[1])), ((small, gathered[2]), (dnw0, last[2]))


def kernel(x, norm_w, w_in, conv_a_w, gla_gate_w, gla_gate_b, gla_norm_w, pool_w, pool_scale, ssd_conv_w, ssd_conv_b, ssd_dt_bias, ssd_a_log, ssd_d, ssd_norm_w, w_out, final_norm_w, loss_target, m_norm_w, m_w_in, m_conv_a_w, m_gla_gate_w, m_gla_gate_b, m_gla_norm_w, m_pool_w, m_pool_scale, m_ssd_conv_w, m_ssd_conv_b, m_ssd_dt_bias, m_ssd_a_log, m_ssd_d, m_ssd_norm_w, m_w_out, m_final_norm_w, v_norm_w, v_w_in, v_conv_a_w, v_gla_gate_w, v_gla_gate_b, v_gla_norm_w, v_pool_w, v_pool_scale, v_ssd_conv_w, v_ssd_conv_b, v_ssd_dt_bias, v_ssd_a_log, v_ssd_d, v_ssd_norm_w, v_w_out, v_final_norm_w):
    weights = dict(norm_w=norm_w, w_in=w_in, conv_a_w=conv_a_w, gla_gate_w=gla_gate_w, gla_gate_b=gla_gate_b,
                   gla_norm_w=gla_norm_w, pool_w=pool_w, pool_scale=pool_scale, ssd_conv_w=ssd_conv_w,
                   ssd_conv_b=ssd_conv_b, ssd_dt_bias=ssd_dt_bias, ssd_a_log=ssd_a_log, ssd_d=ssd_d,
                   ssd_norm_w=ssd_norm_w, w_out=w_out, final_norm_w=final_norm_w)
    m_in = dict(norm_w=m_norm_w, w_in=m_w_in, conv_a_w=m_conv_a_w, gla_gate_w=m_gla_gate_w, gla_gate_b=m_gla_gate_b,
                gla_norm_w=m_gla_norm_w, pool_w=m_pool_w, pool_scale=m_pool_scale, ssd_conv_w=m_ssd_conv_w,
                ssd_conv_b=m_ssd_conv_b, ssd_dt_bias=m_ssd_dt_bias, ssd_a_log=m_ssd_a_log, ssd_d=m_ssd_d,
                ssd_norm_w=m_ssd_norm_w, w_out=m_w_out, final_norm_w=m_final_norm_w)
    v_in = dict(norm_w=v_norm_w, w_in=v_w_in, conv_a_w=v_conv_a_w, gla_gate_w=v_gla_gate_w, gla_gate_b=v_gla_gate_b,
                gla_norm_w=v_gla_norm_w, pool_w=v_pool_w, pool_scale=v_pool_scale, ssd_conv_w=v_ssd_conv_w,
                ssd_conv_b=v_ssd_conv_b, ssd_dt_bias=v_ssd_dt_bias, ssd_a_log=v_ssd_a_log, ssd_d=v_ssd_d,
                ssd_norm_w=v_ssd_norm_w, w_out=v_w_out, final_norm_w=v_final_norm_w)
    order = ("norm_w", "w_in", "conv_a_w", "gla_gate_w", "gla_gate_b", "gla_norm_w", "pool_w", "pool_scale",
             "ssd_conv_w", "ssd_conv_b", "ssd_dt_bias", "ssd_a_log", "ssd_d", "ssd_norm_w", "w_out", "final_norm_w")
    t = x.shape[1]

    comm = _Comm(w_in, w_out)
    cshard = jnp.zeros((16, 256), F32)
    for l in range(2):
        cshard = cshard.at[8 * l:8 * l + 3, 0:64].set(conv_a_w[l]).at[8 * l + 3:8 * l + 7, 0:192].set(ssd_conv_w[l])
    s_in, s_out, g_c = _gather_ici_two_hops((comm.w_in16[0], comm.w_out16[0]), cshard)
    s_in, s_out = _run_rider(_rider_gather_d2d((s_in, s_out)), "gather_d2d0")
    g_c = [jnp.where(comm.chip == s, cshard, g_c[s]) for s in range(4)]
    conv_a_full = jnp.stack([jnp.concatenate([g_c[s][8 * l:8 * l + 3, 0:64] for s in range(4)], axis=-1) for l in range(2)])
    ssd_conv_full = jnp.stack([jnp.concatenate([g_c[s][8 * l + 3:8 * l + 7, 0:192] for s in range(4)], axis=-1)
                               for l in range(2)])
    consts = [_mixer_consts(l, conv_a_full, gla_gate_w, gla_gate_b, gla_norm_w, pool_w, pool_scale, ssd_conv_full,
                            ssd_conv_b, ssd_dt_bias, ssd_a_log, ssd_d, ssd_norm_w) for l in range(2)]

    dx, big, blocks = _local_step(x.reshape(t, D), loss_target.reshape(t, D), norm_w, final_norm_w, consts,
                                  comm.layer_weights(0, s_in, s_out), comm=comm)

    as2d = lambda d: {k: (d[k].reshape(1, D) if k == "final_norm_w" else d[k]) for k in _SMALL_NAMES}
    small = _small_adamw(blocks, as2d(weights), as2d(m_in), as2d(v_in))
    grads, delta, new_m, new_v = ({k: (a.reshape(D) if k == "final_norm_w" else a) for k, a in zip(_SMALL_NAMES, part)}
                                  for part in small[0:4])
    loss = small[4].reshape(())

    grads["w_out"] = jnp.stack(big[1])

    grads["w_in"], delta["w_in"], new_m["w_in"], new_v["w_in"] = _adamw_w_in(w_in, big[0], m_w_in, v_w_in, name="adamw_w_in")
    delta["w_out"], new_m["w_out"], new_v["w_out"] = _adamw(w_out, grads["w_out"], m_w_out, v_w_out, name="adamw_w_out", br=256)

    return (loss, dx.reshape(1, t, D), *[grads[k] for k in order], *[delta[k] for k in order],
            *[new_m[k] for k in order], *[new_v[k] for k in order])
```

```python
import functools

import jax
import jax.numpy as jnp
from jax import lax
from jax.experimental import pallas as pl
from jax.experimental.pallas import tpu as pltpu

F32 = jnp.float32
BF16 = jnp.bfloat16
MESH = pl.DeviceIdType.MESH

D = 1024
CH = 64
EPS = 1e-6
NP = 3456
NPROJ = 3348
NPM = 3328
GLA_SCALE = 32.0 ** -0.5
INV_TAU = 1.0 / 16.0
TB = 512
NCH = TB // CH
assert TB % 256 == 0

C_AH, C_AB, C_AC, C_AZ, C_GQ, C_GK, C_GV = 0, 256, 512, 768, 1024, 1152, 1280
C_GZ, C_PU, C_PZ, C_SZ, C_SX, C_TL = 1536, 1792, 2048, 2304, 2560, 3328
_PERM = ((0, 1536), (1552, 1792), (1536, 16), (3344, 4))

R_CAW, R_GB, R_GNW, R_PSC, R_SCB, R_DTB, R_AE, R_DE, R_SNW, R_SCW = 0, 3, 4, 5, 6, 7, 8, 9, 10, 12

ADAM_LR, ADAM_B1, ADAM_B2, ADAM_EPS, ADAM_WD, ADAM_STEP = 0.001, 0.9, 0.999, 1e-08, 0.01, 10

VMEM_LIMIT = 56 * 1024 * 1024


def _cparams(sem, limit=VMEM_LIMIT):
    return pltpu.CompilerParams(dimension_semantics=sem, vmem_limit_bytes=limit)


_ANY = pl.BlockSpec(memory_space=pl.ANY)


def _place():
    return lax.axis_index("x"), lax.axis_index("y"), lax.axis_index("c")


class _Rider:
    def __init__(self, inputs, out_shapes, sems, start, finish, aliases=None):
        self.inputs, self.out_shapes, self.sems = tuple(inputs), tuple(out_shapes), tuple(sems)
        self.start, self.finish, self.aliases = start, finish, dict(aliases or {})


def _call(body, args, *, grid, in_specs, out_specs, out_shape, name, sem, scratch_shapes=(), rider=None):
    if rider is None:
        outs = pl.pallas_call(body, grid=grid, name=name, in_specs=list(in_specs), out_specs=list(out_specs),
                              out_shape=list(out_shape), scratch_shapes=list(scratch_shapes),
                              compiler_params=_cparams(sem))(*args)
        return list(outs), []
    ni, no, ns = len(args), len(out_shape), len(scratch_shapes)
    ri, ro = len(rider.inputs), len(rider.out_shapes)

    def full(*refs):
        ins, rins = refs[:ni], refs[ni:ni + ri]
        outs, routs = refs[ni + ri:ni + ri + no], refs[ni + ri + no:ni + ri + no + ro]
        scr, rsem = refs[ni + ri + no + ro:ni + ri + no + ro + ns], refs[ni + ri + no + ro + ns:]
        first = functools.reduce(jnp.logical_and, [pl.program_id(a) == 0 for a in range(len(grid))])
        last = functools.reduce(jnp.logical_and, [pl.program_id(a) == grid[a] - 1 for a in range(len(grid))])

        @pl.when(first)
        def _():
            rider.start(rins, routs, rsem)

        body(*ins, *outs, *scr)

        @pl.when(last)
        def _():
            rider.finish(rins, routs, rsem)

    outs = pl.pallas_call(
        full, grid=grid, name=name, in_specs=list(in_specs) + [_ANY] * ri, out_specs=list(out_specs) + [_ANY] * ro,
        out_shape=list(out_shape) + list(rider.out_shapes), scratch_shapes=list(scratch_shapes) + list(rider.sems),
        input_output_aliases={ni + k: no + v for k, v in rider.aliases.items()},
        compiler_params=_cparams(("arbitrary",) * len(grid)))(*args, *rider.inputs)
    return list(outs[:no]), list(outs[no:])


def _run_rider(rider, name):
    ri = len(rider.inputs)

    def body(*refs):
        rins, routs, rsem = refs[:ri], refs[ri:ri + len(rider.out_shapes)], refs[ri + len(rider.out_shapes):]
        rider.start(rins, routs, rsem)
        rider.finish(rins, routs, rsem)

    return list(pl.pallas_call(body, name=name, in_specs=[_ANY] * ri, out_specs=[_ANY] * len(rider.out_shapes),
                               out_shape=list(rider.out_shapes), scratch_shapes=list(rider.sems),
                               input_output_aliases=dict(rider.aliases))(*rider.inputs))


def _dot(a, b):
    return jnp.dot(a.astype(BF16), b.astype(BF16), preferred_element_type=F32)


def _dot_nt(a, b):
    return lax.dot_general(a.astype(BF16), b.astype(BF16), (((1,), (1,)), ((), ())), preferred_element_type=F32)


def _dot_tn(a, b):
    return lax.dot_general(a.astype(BF16), b.astype(BF16), (((0,), (0,)), ((), ())), preferred_element_type=F32)


def _split(a):
    hi = a.astype(BF16)
    lo = (a - hi.astype(F32)).astype(BF16)
    return hi, lo


def _dot2_l(a, b):
    hi, lo = _split(a)
    return _dot(hi, b) + _dot(lo, b)


def _dot2_r(a, b):
    hi, lo = _split(b)
    return _dot(a, hi) + _dot(a, lo)


def _dot3_l(a, b):
    hi, lo = _split(a)
    lo2 = ((a - hi.astype(F32)) - lo.astype(F32)).astype(BF16)
    return _dot(hi, b) + _dot(lo, b) + _dot(lo2, b)


def _dot2_nt(a, b):
    hi, lo = _split(a)
    return _dot_nt(hi, b) + _dot_nt(lo, b)


def _silu(z):
    return z * jax.nn.sigmoid(z)


def _lse1(x):
    return jnp.log(1.0 + jnp.exp(-jnp.abs(x)))


def _cs(a):
    return jnp.sum(a, axis=0, keepdims=True)


def _iota(shape, dim):
    return lax.broadcasted_iota(jnp.int32, shape, dim)


def _mixer_matrices():
    r, c = _iota((256, 256), 0), _iota((256, 256), 1)
    same_chunk = (r >> 6) == (c >> 6)
    mats = jnp.stack([jnp.where((c > r) & same_chunk, 1.0, 0.0), jnp.where((c < r) & same_chunk, 1.0, 0.0),
                      jnp.where(same_chunk, 1.0 / 64.0, 0.0), jnp.where((r < 128) & (r - 16 == (c >> 6)), 1.0, 0.0)])
    mask = jnp.where((_iota((256, 128), 0) >> 6) == (_iota((256, 128), 1) >> 5), 1.0, 0.0)
    return mats.astype(BF16), mask.astype(F32)


def _dn(ext, k, n, h):
    return pltpu.roll(ext, k, axis=0)[h:h + n]


def _up(ext, k, n):
    return pltpu.roll(ext, ext.shape[0] - k, axis=0)[:n]


def _pool_lane_select(lane, s2, s4, s8, s16):
    return jnp.where(lane < 64, s2, jnp.where(lane < 128, s4, jnp.where(lane < 192, s8, s16)))


def _winsum_dn(ext, lane):
    s2 = ext + pltpu.roll(ext, 1, axis=0)
    s4 = s2 + pltpu.roll(s2, 2, axis=0)
    s8 = s4 + pltpu.roll(s4, 4, axis=0)
    s16 = s8 + pltpu.roll(s8, 8, axis=0)
    return _pool_lane_select(lane, s2, s4, s8, s16)


def _winsum_up(ext, lane):
    m = ext.shape[0]
    s2 = ext + pltpu.roll(ext, m - 1, axis=0)
    s4 = s2 + pltpu.roll(s2, m - 2, axis=0)
    s8 = s4 + pltpu.roll(s4, m - 4, axis=0)
    s16 = s8 + pltpu.roll(s8, m - 8, axis=0)
    return _pool_lane_select(lane, s2, s4, s8, s16)


def _pool_inv_count(tile, n):
    lane = _iota((1, 256), 1)
    win = _pool_lane_select(lane, 2.0, 4.0, 8.0, 16.0).astype(F32)
    tpos = (tile * n + _iota((n, 1), 0) + 1).astype(F32)
    return jnp.where(tpos >= win, 1.0 / win, 1.0 / tpos)


def _silu_pair(z):
    s = jax.nn.sigmoid(z)
    return z * s, s * (1.0 + z * (1.0 - s))


def _chunks(a):
    return [a[c * CH:(c + 1) * CH] for c in range(a.shape[0] // CH)]


def _halves(fn, a, b):
    return jnp.concatenate([fn(a[:, 0:128], b[:, 0:128]), fn(a[:, 128:256], b[:, 128:256])], axis=1)


def _chunk_sums(tri, a):
    return jnp.concatenate([_dot2_r(tri, a[r:r + 256]) for r in range(0, a.shape[0], 256)], axis=0)


def _mixer_tile_prep(p_ref, t_ref, xc, prm_ref, gw_v, cm_ref, mk_ref):
    tail = t_ref[...]
    pre = _dot(tail, gw_v) + prm_ref[R_GB:R_GB + 1, 0:128]
    la = (jnp.minimum(pre, 0.0) - _lse1(pre)) * INV_TAU
    dtin = tail + prm_ref[R_DTB:R_DTB + 1, 0:128]
    dtf = jnp.maximum(dtin, 0.0) + _lse1(dtin)
    dte = _dot2_l(dtf, cm_ref[3, 0:128, :])
    da = dte * prm_ref[R_AE:R_AE + 1, 0:256]
    rev = _chunk_sums(cm_ref[0], jnp.concatenate([la, da], axis=1))
    dec = jnp.exp(rev[:, 0:128])
    kd = p_ref[:, C_GK:C_GK + 128].astype(F32) * dec
    wdec = jnp.exp(rev[:, 128:384])
    w = wdec * dte
    xw = xc[:, 0:256] * w
    d_s = [jnp.exp(_cs(a)) for a in _chunks(la)]
    et = [jnp.exp(_cs(a)) for a in _chunks(da)]
    mask_t = mk_ref[...]
    ut_g = [_dot_tn(v, k) * mask_t for v, k in zip(_chunks(p_ref[:, C_GV:C_GV + 256].astype(F32)), _chunks(kd))]
    ut_s = [_halves(_dot_tn, b, x) for b, x in zip(_chunks(xc[:, 256:512]), _chunks(xw))]
    return tail, pre, dtin, dte, dec, kd, wdec, w, xw, d_s, et, ut_g, ut_s


def _rmsproj(x, nw, wp, name, tm=512, rider=None):
    t = x.shape[0]

    def body(x_ref, nw_ref, w_ref, o_ref, t_ref, h_ref):
        xv = x_ref[...]
        rs = lax.rsqrt(jnp.mean(xv * xv, axis=-1, keepdims=True) + EPS)
        h = (xv * rs * nw_ref[...]).astype(BF16)
        h_ref[...] = h
        proj = jnp.dot(h, w_ref[...], preferred_element_type=F32)
        o_ref[...] = proj[:, 0:NPM].astype(BF16)
        t_ref[...] = proj[:, NPM:NP]

    (proj, tail, h), extra = _call(
        body, (x, nw, wp), grid=(t // tm,), name=name, sem=("parallel",), rider=rider,
        in_specs=[pl.BlockSpec((tm, D), lambda i: (i, 0)), pl.BlockSpec((1, D), lambda i: (0, 0)),
                  pl.BlockSpec((D, NP), lambda i: (0, 0))],
        out_specs=[pl.BlockSpec((tm, NPM), lambda i: (i, 0)), pl.BlockSpec((tm, NP - NPM), lambda i: (i, 0)),
                   pl.BlockSpec((tm, D), lambda i: (i, 0))],
        out_shape=[jax.ShapeDtypeStruct((t, NPM), BF16), jax.ShapeDtypeStruct((t, NP - NPM), F32),
                   jax.ShapeDtypeStruct((t, D), BF16)])
    return (proj, tail), h, extra


def _head_tile(xv, tgt, w):
    rs = lax.rsqrt(jnp.mean(xv * xv, axis=-1, keepdims=True) + EPS)
    xh = xv * rs
    err = xh * w - tgt
    dy = err * (1.0 / D)
    dxh = dy * w
    dx = rs * (dxh - xh * jnp.mean(dxh * xh, axis=-1, keepdims=True))
    return dx, _cs(dy * xh), (0.5 / D) * jnp.sum(err * err)


def _dxin(dp, wpt, x, dxn, nw, name, tm=512, rider=None):
    t = x.shape[0]

    def body(dp_ref, w_ref, x_ref, dxn_ref, nw_ref, dx_ref, dnw_ref):
        @pl.when(pl.program_id(0) == 0)
        def _():
            dnw_ref[...] = jnp.zeros_like(dnw_ref)

        dh = jnp.dot(dp_ref[...], w_ref[...], preferred_element_type=F32)
        xv = x_ref[...]
        rs = lax.rsqrt(jnp.mean(xv * xv, axis=-1, keepdims=True) + EPS)
        xh = xv * rs
        dnw_ref[0:1, :] += _cs(dh * xh)
        dxh = dh * nw_ref[...]
        dx_ref[...] = dxn_ref[...] + rs * (dxh - xh * jnp.mean(dxh * xh, axis=-1, keepdims=True))

    return _call(
        body, (dp, wpt, x, dxn, nw), grid=(t // tm,), name=name, sem=("arbitrary",), rider=rider,
        in_specs=[pl.BlockSpec((tm, NP), lambda i: (i, 0)), pl.BlockSpec((NP, D), lambda i: (0, 0)),
                  pl.BlockSpec((tm, D), lambda i: (i, 0)), pl.BlockSpec((tm, D), lambda i: (i, 0)),
                  pl.BlockSpec((1, D), lambda i: (0, 0))],
        out_specs=[pl.BlockSpec((tm, D), lambda i: (i, 0)), pl.BlockSpec((8, D), lambda i: (0, 0))],
        out_shape=[jax.ShapeDtypeStruct((t, D), F32), jax.ShapeDtypeStruct((8, D), F32)])


def _dwin(h, dp, name, tm=1024, rider=None):
    t = h.shape[0]

    def body(h_ref, dp_ref, o_ref):
        @pl.when(pl.program_id(0) == 0)
        def _():
            o_ref[...] = jnp.zeros_like(o_ref)

        o_ref[...] += _dot_tn(h_ref[...], dp_ref[...])

    (dwp,), extra = _call(
        body, (h, dp), grid=(t // tm,), name=name, sem=("arbitrary",), rider=rider,
        in_specs=[pl.BlockSpec((tm, D), lambda i: (i, 0)), pl.BlockSpec((tm, NP), lambda i: (i, 0))],
        out_specs=[pl.BlockSpec((D, NP), lambda i: (0, 0))], out_shape=[jax.ShapeDtypeStruct((D, NP), F32)])
    return dwp, extra


def _mixer_fwd(proj, x, wo, prm, gw, pw, cmat, mask, name, rider=None, head=None):
    proj, tail = proj
    t = proj.shape[0]
    nt, nc = t // TB, t // CH

    def body(p_ref, t_ref, x_ref, wo_ref, prm_ref, gw_ref, pw_ref, cm_ref, mk_ref, *rest):
        (tgt_ref, fw_ref), rest = (rest[:2], rest[2:]) if head else ((None, None), rest)
        mix_ref, sg_ref, ss_ref, xn_ref, xc_ref, dxc_ref, cv_ref, pool_ref = rest[:8]
        acc_ref = rest[8] if head else None
        sg_s, ss_s, h_ua, h_pu, h_sx = rest[-5:]
        i = pl.program_id(0)

        @pl.when(i == 0)
        def _():
            for r in (sg_s, ss_s, h_ua, h_pu, h_sx) + ((acc_ref,) if head else ()):
                r[...] = jnp.zeros_like(r)

        lane = _iota((1, 256), 1)
        u = p_ref[:, C_AC:C_AC + 256].astype(F32) * p_ref[:, C_AH:C_AH + 256].astype(F32)
        ext = jnp.concatenate([h_ua[...], u], axis=0)
        cv = (prm_ref[R_CAW + 2:R_CAW + 3, 0:256] * u + prm_ref[R_CAW + 1:R_CAW + 2, 0:256] * _dn(ext, 1, TB, 8)
              + prm_ref[R_CAW:R_CAW + 1, 0:256] * _dn(ext, 2, TB, 8))
        cv_ref[...] = cv.astype(BF16)
        mix_ref[:, 0:256] = (p_ref[:, C_AB:C_AB + 256].astype(F32) * cv * _silu(p_ref[:, C_AZ:C_AZ + 256].astype(F32))).astype(BF16)
        h_ua[...] = u[TB - 8:, :]
        pu = p_ref[:, C_PU:C_PU + 256].astype(F32)
        ext = jnp.concatenate([h_pu[...], pu], axis=0)
        pooled = (_winsum_dn(ext, lane)[16:] * _pool_inv_count(i, TB) - pu).astype(BF16)
        pool_ref[...] = pooled
        mixed = jnp.dot(pooled, pw_ref[...], preferred_element_type=F32)
        mix_ref[:, 512:768] = (prm_ref[R_PSC:R_PSC + 1, 0:256] * mixed * _silu(p_ref[:, C_PZ:C_PZ + 256].astype(F32))).astype(BF16)
        h_pu[...] = pu[TB - 16:, :]
        sx = p_ref[:, C_SX:C_SX + 768].astype(F32)
        ext = jnp.concatenate([h_sx[...], sx], axis=0)
        xc, dxc = _silu_pair(prm_ref[R_SCW + 3:R_SCW + 4, :] * sx + prm_ref[R_SCW + 2:R_SCW + 3, :] * _dn(ext, 1, TB, 8)
                             + prm_ref[R_SCW + 1:R_SCW + 2, :] * _dn(ext, 2, TB, 8)
                             + prm_ref[R_SCW:R_SCW + 1, :] * _dn(ext, 3, TB, 8) + prm_ref[R_SCB:R_SCB + 1, :])
        xc_ref[...] = xc.astype(BF16)
        dxc_ref[...] = dxc.astype(BF16)
        h_sx[...] = sx[TB - 8:, :]

        _, _, _, _, _, _, _, _, _, d_s, et, ut_g, ut_s = _mixer_tile_prep(p_ref, t_ref, xc, prm_ref, gw_ref[...], cm_ref, mk_ref)
        s_g, s_s = sg_s[...], ss_s[...]
        o, y = [], []
        qs = _chunks(p_ref[:, C_GQ:C_GQ + 128].astype(F32) * GLA_SCALE)
        cm = _chunks(xc[:, 512:768])
        for c in range(NCH):
            sg_ref[c] = s_g.astype(BF16)
            ss_ref[c] = s_s.astype(BF16)
            s_g = s_g * d_s[c] + ut_g[c]
            s_s = s_s * et[c] + ut_s[c]
            o.append(_dot_nt(qs[c], s_g))
            y.append(_halves(_dot, cm[c], s_s))
        sg_s[...] = s_g
        ss_s[...] = s_s
        o = jnp.concatenate(o, axis=0)
        on = o * lax.rsqrt(_dot2_l(o * o, cm_ref[2]) + EPS)
        mix_ref[:, 256:512] = (on * prm_ref[R_GNW:R_GNW + 1, 0:256] * _silu(p_ref[:, C_GZ:C_GZ + 256].astype(F32))).astype(BF16)
        y2 = ((jnp.concatenate(y, axis=0) + prm_ref[R_DE:R_DE + 1, 0:256] * xc[:, 0:256])
              * _silu(p_ref[:, C_SZ:C_SZ + 256].astype(F32)))
        mix_ref[:, 768:1024] = (y2 * lax.rsqrt(jnp.mean(y2 * y2, axis=-1, keepdims=True) + EPS)
                                * prm_ref[R_SNW:R_SNW + 1, 0:256]).astype(BF16)
        xn = x_ref[...] + jnp.dot(mix_ref[...], wo_ref[...], preferred_element_type=F32)
        if head:
            xn_ref[...], dfw, loss = _head_tile(xn, tgt_ref[...], fw_ref[...])
            acc_ref[0:1, :] += dfw
            acc_ref[1:2, :] += jnp.zeros((1, D), F32) + loss
        else:
            xn_ref[...] = xn

    row = pl.BlockSpec((TB, D), lambda i: (i, 0))
    return _call(
        body, (proj, tail, x, wo, prm, gw, pw, cmat, mask) + tuple(head or ()), grid=(nt,), name=name, sem=("arbitrary",),
        rider=rider,
        in_specs=[pl.BlockSpec((TB, NPM), lambda i: (i, 0)), pl.BlockSpec((TB, NP - NPM), lambda i: (i, 0)), row,
                  pl.BlockSpec((D, D), lambda i: (0, 0)), pl.BlockSpec((16, 768), lambda i: (0, 0)),
                  pl.BlockSpec((128, 128), lambda i: (0, 0)), pl.BlockSpec((256, 256), lambda i: (0, 0)),
                  pl.BlockSpec((4, 256, 256), lambda i: (0, 0, 0)), pl.BlockSpec((256, 128), lambda i: (0, 0))]
        + ([row, pl.BlockSpec((1, D), lambda i: (0, 0))] if head else []),
        out_specs=[row, pl.BlockSpec((NCH, 256, 128), lambda i: (i, 0, 0)),
                   pl.BlockSpec((NCH, 128, 256), lambda i: (i, 0, 0)), row] + [pl.BlockSpec((TB, 768), lambda i: (i, 0))] * 2
        + [pl.BlockSpec((TB, 256), lambda i: (i, 0))] * 2 + ([pl.BlockSpec((8, D), lambda i: (0, 0))] if head else []),
        out_shape=[jax.ShapeDtypeStruct((t, D), BF16), jax.ShapeDtypeStruct((nc, 256, 128), BF16),
                   jax.ShapeDtypeStruct((nc, 128, 256), BF16), jax.ShapeDtypeStruct((t, D), F32)]
        + [jax.ShapeDtypeStruct((t, 768), BF16)] * 2 + [jax.ShapeDtypeStruct((t, 256), BF16)] * 2
        + ([jax.ShapeDtypeStruct((8, D), F32)] if head else []),
        scratch_shapes=[pltpu.VMEM((256, 128), F32), pltpu.VMEM((128, 256), F32), pltpu.VMEM((8, 256), F32),
                        pltpu.VMEM((16, 256), F32), pltpu.VMEM((8, 768), F32)])


def _mixer_bwd(proj, dxn, wot, mix, sg, ss, xc16, dxc16, cv16, pool16, prm, gw, pw, cmat, mask, name, rider=None):
    proj, tail = proj
    t = proj.shape[0]
    nt = t // TB
    rev = lambda i: nt - 1 - i

    def body(p_ref, t_ref, dxn_ref, wot_ref, mix_ref, sg_ref, ss_ref, xc_ref, dxc_ref, cv_ref, pool_ref, prm_ref, gw_ref,
             pw_ref, cm_ref, mk_ref, dp_ref, sgc_ref, dwo_ref,
             gg_s, gs_s, h_dcv, h_dpl, h_dpre, gsm_ref, dgw_ref, dpw_ref, dm_ref):
        i = pl.program_id(0)
        tile = nt - 1 - i

        @pl.when(i == 0)
        def _():
            for r in (gg_s, gs_s, h_dcv, h_dpl, h_dpre, gsm_ref, dgw_ref, dpw_ref, dwo_ref):
                r[...] = jnp.zeros_like(r)

        dxn = dxn_ref[...].astype(BF16)
        dm_ref[...] = jnp.dot(dxn, wot_ref[...], preferred_element_type=F32)
        dwo_ref[...] += _dot_tn(mix_ref[...], dxn)

        lane = _iota((1, 256), 1)
        ah, ac = p_ref[:, C_AH:C_AH + 256].astype(F32), p_ref[:, C_AC:C_AC + 256].astype(F32)
        ab, az = p_ref[:, C_AB:C_AB + 256].astype(F32), p_ref[:, C_AZ:C_AZ + 256].astype(F32)
        w0, w1, w2 = (prm_ref[R_CAW + j:R_CAW + j + 1, 0:256] for j in range(3))
        u = ac * ah
        cv = cv_ref[...].astype(F32)
        g = dm_ref[:, 0:256]
        sz, dsz = _silu_pair(az)
        dp_ref[:, C_AB:C_AB + 256] = (g * cv * sz).astype(BF16)
        dp_ref[:, C_AZ:C_AZ + 256] = (g * ab * cv * dsz).astype(BF16)
        dcv = g * ab * sz
        dext = jnp.concatenate([dcv, h_dcv[...]], axis=0)
        dcv1, dcv2 = _up(dext, 1, TB), _up(dext, 2, TB)
        du = w2 * dcv + w1 * dcv1 + w0 * dcv2
        dp_ref[:, C_AC:C_AC + 256] = (du * ah).astype(BF16)
        dp_ref[:, C_AH:C_AH + 256] = (du * ac).astype(BF16)
        gsm_ref[R_CAW:R_CAW + 1, 0:256] += _cs(u * dcv2)
        gsm_ref[R_CAW + 1:R_CAW + 2, 0:256] += _cs(u * dcv1)
        gsm_ref[R_CAW + 2:R_CAW + 3, 0:256] += _cs(u * dcv)
        h_dcv[...] = dcv[0:8, :]
        pz = p_ref[:, C_PZ:C_PZ + 256].astype(F32)
        psc = prm_ref[R_PSC:R_PSC + 1, 0:256]
        icnt = _pool_inv_count(tile, TB)
        pooled = pool_ref[...]
        pw_v = pw_ref[...]
        mixed = jnp.dot(pooled, pw_v, preferred_element_type=F32)
        g = dm_ref[:, 512:768]
        sz, dsz = _silu_pair(pz)
        gsm_ref[R_PSC:R_PSC + 1, 0:256] += _cs(g * mixed * sz)
        dp_ref[:, C_PZ:C_PZ + 256] = (g * psc * mixed * dsz).astype(BF16)
        dmixed = g * psc * sz
        dpw_ref[...] += _dot_tn(pooled, dmixed)
        dpooled = _dot_nt(dmixed, pw_v)
        qd = dpooled * icnt
        dext = jnp.concatenate([qd, h_dpl[...]], axis=0)
        dp_ref[:, C_PU:C_PU + 256] = (_winsum_up(dext, lane)[:TB] - dpooled).astype(BF16)
        h_dpl[...] = qd[0:16, :]
        cw = [prm_ref[R_SCW + j:R_SCW + j + 1, :] for j in range(4)]
        xc = xc_ref[...].astype(F32)
        xs, bm, cm = xc[:, 0:256], xc[:, 256:512], xc[:, 512:768]

        gw_v = gw_ref[...]
        tail, pre, dtin, dte, dec, kd, wdec, w, xw, d_s, et, ut_g, ut_s = _mixer_tile_prep(p_ref, t_ref, xc, prm_ref,
                                                                                          gw_v, cm_ref, mk_ref)
        gmean = cm_ref[2]
        mask_t = mk_ref[...]
        gnw = prm_ref[R_GNW:R_GNW + 1, 0:256]
        a_e = prm_ref[R_AE:R_AE + 1, 0:256]
        d_e = prm_ref[R_DE:R_DE + 1, 0:256]
        snw = prm_ref[R_SNW:R_SNW + 1, 0:256]
        sg_in = [sg_ref[c].astype(F32) for c in range(NCH)]
        ss_in = [ss_ref[c].astype(F32) for c in range(NCH)]
        sg_n = [sg_in[c] * d_s[c] + ut_g[c] for c in range(NCH)]
        ss_n = [ss_in[c] * et[c] + ut_s[c] for c in range(NCH)]
        qs = _chunks(p_ref[:, C_GQ:C_GQ + 128].astype(F32) * GLA_SCALE)
        cm_c, bm_c, xw_c, kd_c = _chunks(cm), _chunks(bm), _chunks(xw), _chunks(kd)
        v_c = _chunks(p_ref[:, C_GV:C_GV + 256].astype(F32))
        o = jnp.concatenate([_dot_nt(qs[c], sg_n[c]) for c in range(NCH)], axis=0)
        y = jnp.concatenate([_halves(_dot, cm_c[c], ss_n[c]) for c in range(NCH)], axis=0) + d_e * xs
        gz = p_ref[:, C_GZ:C_GZ + 256].astype(F32)
        r = lax.rsqrt(_dot2_l(o * o, gmean) + EPS)
        on = o * r
        dyb = dm_ref[:, 256:512]
        sz, dsz = _silu_pair(gz)
        dp_ref[:, C_GZ:C_GZ + 256] = (dyb * on * gnw * dsz).astype(BF16)
        tg = dyb * sz
        gsm_ref[R_GNW:R_GNW + 1, 0:256] += _cs(tg * on)
        don = tg * gnw
        do_c = _chunks(r * (don - on * _dot2_l(don * on, gmean)))
        ssz = p_ref[:, C_SZ:C_SZ + 256].astype(F32)
        sil, dsil = _silu_pair(ssz)
        y2 = y * sil
        r = lax.rsqrt(jnp.mean(y2 * y2, axis=-1, keepdims=True) + EPS)
        yn = y2 * r
        dyd = dm_ref[:, 768:1024]
        gsm_ref[R_SNW:R_SNW + 1, 0:256] += _cs(dyd * yn)
        dn = dyd * snw
        dy2 = r * (dn - yn * jnp.mean(dn * yn, axis=-1, keepdims=True))
        dp_ref[:, C_SZ:C_SZ + 256] = (dy2 * y * dsil).astype(BF16)
        dy = dy2 * sil
        gsm_ref[R_DE:R_DE + 1, 0:256] += _cs(dy * xs)
        dy_c = _chunks(dy)
        dq = jnp.concatenate([_dot(do_c[c], sg_n[c]) for c in range(NCH)], axis=0)
        dp_ref[:, C_GQ:C_GQ + 128] = (dq * GLA_SCALE).astype(BF16)
        dcm = jnp.concatenate([_halves(_dot_nt, dy_c[c], ss_n[c]) for c in range(NCH)], axis=0)
        gg = [_dot_tn(do_c[c], qs[c]) * mask_t for c in range(NCH)]
        gs = [_halves(_dot_tn, cm_c[c], dy_c[c]) for c in range(NCH)]
        car_g, car_s = gg_s[...], gs_s[...]
        for c in reversed(range(NCH)):
            gg[c] = gg[c] + car_g
            gs[c] = gs[c] + car_s
            car_g = gg[c] * d_s[c]
            car_s = gs[c] * et[c]
        gg_s[...] = car_g
        gs_s[...] = car_s
        dkd = jnp.concatenate([_dot(v_c[c], gg[c]) for c in range(NCH)], axis=0)
        dp_ref[:, C_GV:C_GV + 256] = jnp.concatenate([_dot_nt(kd_c[c], gg[c]) for c in range(NCH)], axis=0).astype(BF16)
        dp_ref[:, C_GK:C_GK + 128] = (dkd * dec).astype(BF16)
        dbm = jnp.concatenate([_halves(_dot_nt, xw_c[c], gs[c]) for c in range(NCH)], axis=0)
        dxw = jnp.concatenate([_halves(_dot, bm_c[c], gs[c]) for c in range(NCH)], axis=0)
        dxs = dy * d_e + dxw * w
        dw = dxw * xs
        dsuf = _chunk_sums(cm_ref[1], jnp.concatenate([dkd * kd, dw * dte * wdec], axis=1))
        tot_g = jnp.concatenate([jnp.broadcast_to(_cs(gg[c] * sg_in[c]) * d_s[c], (CH, 128)) for c in range(NCH)], axis=0)
        tot_s = jnp.concatenate([jnp.broadcast_to(_cs(gs[c] * ss_in[c]) * et[c], (CH, 256)) for c in range(NCH)], axis=0)
        dpre = (dsuf[:, 0:128] + tot_g) * INV_TAU * jax.nn.sigmoid(-pre)
        dgw_ref[...] += _dot_tn(tail, dpre)
        gsm_ref[R_GB:R_GB + 1, 0:128] += _cs(dpre)
        dda = dsuf[:, 128:384] + tot_s
        gsm_ref[R_AE:R_AE + 1, 0:256] += _cs(dda * dte)
        dtail_s = _dot2_nt(dw * wdec + dda * a_e, cm_ref[3, 0:128, :]) * jax.nn.sigmoid(dtin)
        gsm_ref[R_DTB:R_DTB + 1, 0:128] += _cs(dtail_s)
        dp_ref[:, C_TL:C_TL + 128] = (_dot_nt(dpre, gw_v) + dtail_s).astype(BF16)
        dpre_c = jnp.concatenate([dxs, dbm, dcm], axis=1) * dxc_ref[...].astype(F32)
        dext = jnp.concatenate([dpre_c, h_dpre[...]], axis=0)
        ups = [dpre_c, _up(dext, 1, TB), _up(dext, 2, TB), _up(dext, 3, TB)]
        dp_ref[:, C_SX:C_SX + 768] = (cw[3] * ups[0] + cw[2] * ups[1] + cw[1] * ups[2] + cw[0] * ups[3]).astype(BF16)
        sx = p_ref[:, C_SX:C_SX + 768].astype(F32)
        for k in range(4):
            gsm_ref[R_SCW + k:R_SCW + k + 1, :] += _cs(sx * ups[3 - k])
        gsm_ref[R_SCB:R_SCB + 1, :] += _cs(dpre_c)
        h_dpre[...] = dpre_c[0:8, :]

        @pl.when(i == nt - 1)
        def _():
            ri, ci = _iota((256, 256), 0), _iota((256, 256), 1)
            per_head = jnp.where((ri >> 6) == ci, 1.0, 0.0).astype(BF16)
            per_dv = jnp.where((ri & 63) == ci, 1.0, 0.0).astype(BF16)
            row = _iota((8, 256), 0)
            top = gsm_ref[0:8, 0:256]
            sgc_ref[0:8, 0:256] = jnp.where(row == R_GNW, _dot3_l(top, per_dv), top)
            bot = gsm_ref[8:16, 0:256]
            fold = _dot3_l(jnp.where(row == R_AE - 8, bot * a_e, bot), per_head)
            sgc_ref[8:16, 0:256] = jnp.where((row == R_AE - 8) | (row == R_DE - 8), fold, bot)
            sgc_ref[0:16, 256:768] = gsm_ref[:, 256:768]
            sgc_ref[0:16, 768:896] = dgw_ref[0:16, :]
            sgc_ref[0:16, 896:1024] = jnp.zeros((16, 128), F32)
            diag = _pool_lane_select(lane, dpw_ref[0:64, :], dpw_ref[64:128, :], dpw_ref[128:192, :], dpw_ref[192:256, :])
            for q in range(4):
                sgc_ref[16:32, 256 * q:256 * q + 256] = diag[16 * q:16 * q + 16, :]

    return _call(
        body, (proj, tail, dxn, wot, mix, sg, ss, xc16, dxc16, cv16, pool16, prm, gw, pw, cmat, mask), grid=(nt,), name=name,
        sem=("arbitrary",), rider=rider,
        in_specs=[pl.BlockSpec((TB, NPM), lambda i: (rev(i), 0)),
                  pl.BlockSpec((TB, NP - NPM), lambda i: (rev(i), 0)),
                  pl.BlockSpec((TB, D), lambda i: (rev(i), 0)), pl.BlockSpec((D, D), lambda i: (0, 0)),
                  pl.BlockSpec((TB, D), lambda i: (rev(i), 0)),
                  pl.BlockSpec((NCH, 256, 128), lambda i: (rev(i), 0, 0)),
                  pl.BlockSpec((NCH, 128, 256), lambda i: (rev(i), 0, 0)),
                  pl.BlockSpec((TB, 768), lambda i: (rev(i), 0)), pl.BlockSpec((TB, 768), lambda i: (rev(i), 0)),
                  pl.BlockSpec((TB, 256), lambda i: (rev(i), 0)), pl.BlockSpec((TB, 256), lambda i: (rev(i), 0)),
                  pl.BlockSpec((16, 768), lambda i: (0, 0)), pl.BlockSpec((128, 128), lambda i: (0, 0)),
                  pl.BlockSpec((256, 256), lambda i: (0, 0)), pl.BlockSpec((4, 256, 256), lambda i: (0, 0, 0)),
                  pl.BlockSpec((256, 128), lambda i: (0, 0))],
        out_specs=[pl.BlockSpec((TB, NP), lambda i: (rev(i), 0)), pl.BlockSpec((32, 1024), lambda i: (0, 0)),
                   pl.BlockSpec((D, D), lambda i: (0, 0))],
        out_shape=[jax.ShapeDtypeStruct((t, NP), BF16), jax.ShapeDtypeStruct((32, 1024), F32),
                   jax.ShapeDtypeStruct((D, D), F32)],
        scratch_shapes=[pltpu.VMEM((256, 128), F32), pltpu.VMEM((128, 256), F32), pltpu.VMEM((8, 256), F32),
                        pltpu.VMEM((16, 256), F32), pltpu.VMEM((8, 768), F32), pltpu.VMEM((16, 768), F32),
                        pltpu.VMEM((128, 128), F32), pltpu.VMEM((256, 256), F32), pltpu.VMEM((TB, D), F32)])


SHARD = NPROJ // 4
SHARD_PAD = 896


def _ranges_to_perm(o, n):
    out, p = [], 0
    for start, size in _PERM:
        a, b = max(o, start), min(o + n, start + size)
        if a < b:
            out.append((a, b - a, p + a - start))
        p += size
    return out


def _ranges_to_orig(p0, n):
    out, p = [], 0
    for start, size in _PERM:
        a, b = max(p0, p), min(p0 + n, p + size)
        if a < b:
            out.append((a, b - a, start + a - p))
        p += size
    return out


def _lane_window(load, lo, n, d, lane):
    a = 128 * (lo // 128)
    off = lo - a
    w = 128 if off + n <= 128 else 256
    chunk = load(a, w)
    shift = (d - off) % w
    if shift:
        chunk = pltpu.roll(chunk, shift, axis=1)
    return jnp.where((lane >= d) & (lane < d + n), chunk[:, 0:128], 0.0)


def _assemble_w_in(slabs, name, rb=256):
    def body(s_ref, wp_ref, wpt_ref):
        lane = _iota((1, 128), 1)
        for b in range(NP // 128):
            acc = jnp.zeros((rb, 128), F32)
            for p, n, o in _ranges_to_orig(128 * b, 128):
                while n > 0:
                    s, lo = o // SHARD, o % SHARD
                    cnt = min(n, SHARD - lo)
                    acc = acc + _lane_window(lambda a, w, s=s: s_ref[s, :, a:a + w].astype(F32), lo, cnt, p - 128 * b, lane)
                    o, p, n = o + cnt, p + cnt, n - cnt
            wp_ref[:, 128 * b:128 * b + 128] = acc.astype(BF16)
            wpt_ref[128 * b:128 * b + 128, :] = acc.T.astype(BF16)

    return pl.pallas_call(
        body, grid=(D // rb,), name=name,
        in_specs=[pl.BlockSpec((4, rb, SHARD_PAD), lambda i: (0, i, 0))],
        out_specs=[pl.BlockSpec((rb, NP), lambda i: (i, 0)), pl.BlockSpec((NP, rb), lambda i: (0, i))],
        out_shape=[jax.ShapeDtypeStruct((D, NP), BF16), jax.ShapeDtypeStruct((NP, D), BF16)],
        compiler_params=_cparams(("parallel",)))(slabs)


def _split_dw_in(dwp, name, rb=256):
    rows = dwp.shape[0]

    def body(g_ref, o_ref):
        lane = _iota((1, 128), 1)
        for s in range(4):
            for k in range(SHARD_PAD // 128):
                acc = jnp.zeros((rb, 128), F32)
                n_valid = min(128, SHARD - 128 * k)
                for o, n, p in _ranges_to_perm(SHARD * s + 128 * k, n_valid):
                    acc = acc + _lane_window(lambda a, w: g_ref[:, a:a + w].astype(F32), p, n, o - SHARD * s - 128 * k, lane)
                o_ref[s, :, 128 * k:128 * k + 128] = acc.astype(o_ref.dtype)

    return pl.pallas_call(
        body, grid=(rows // rb,), name=name,
        in_specs=[pl.BlockSpec((rb, NP), lambda i: (i, 0))],
        out_specs=pl.BlockSpec((4, rb, SHARD_PAD), lambda i: (0, i, 0)),
        out_shape=jax.ShapeDtypeStruct((4, rows, SHARD_PAD), dwp.dtype),
        compiler_params=_cparams(("parallel",)))(dwp)


def _half(c, n):
    return pl.ds(pl.multiple_of(c * (n // 2), n // 2), n // 2)


def _other_chips(x, y):
    return ((1 - x, y), (x, 1 - y), (1 - x, 1 - y))


def _remote(src, dst, send, recv, k, dev):
    return pltpu.make_async_remote_copy(src_ref=src, dst_ref=dst, send_sem=send.at[k], recv_sem=recv.at[k], device_id=dev,
                                        device_id_type=MESH)


def _sem(n):
    return pltpu.SemaphoreType.DMA((n,))


def _rider_gather_ici(shards):
    shards = tuple(shards)
    n = len(shards)

    def copies(rins, routs, sems, arrivals=True):
        send, recv = sems
        x, y, c = _place()
        me = 2 * x + y
        out, inc = [], []
        for j, (px, py) in enumerate(_other_chips(x, y)):
            for k in range(n):
                rows = _half(c, shards[k].shape[0])
                out.append(_remote(rins[k].at[rows], routs[k].at[me, rows], send, recv, n * j + k, (px, py, c)))
                if arrivals:
                    inc.append(_remote(rins[k].at[rows], routs[k].at[2 * px + py, rows], send, recv, n * j + k, (px, py, c)))
        return out, inc

    def start(rins, routs, sems):
        for cp in copies(rins, routs, sems, arrivals=False)[0]:
            cp.start()

    def finish(rins, routs, sems):
        out, inc = copies(rins, routs, sems)
        for cp in inc:
            cp.wait_recv()
        for cp in out:
            cp.wait_send()

    return _Rider(shards, [jax.ShapeDtypeStruct((4,) + a.shape, a.dtype) for a in shards], [_sem(3 * n), _sem(3 * n)],
                  start, finish)


def _gather_ici_two_hops(shards, extra):
    shards = tuple(shards)
    n = len(shards)

    def body(*refs):
        ins, e_in, outs, e_out = refs[:n], refs[n], refs[n + 1:2 * n + 1], refs[2 * n + 1]
        send, recv = refs[2 * n + 2:]
        x, y, c = _place()
        slab = lambda px, py: 2 * px + py
        xn, yn, dg = (1 - x, y), (x, 1 - y), (1 - x, 1 - y)

        def part(k, q):
            r = shards[k].shape[0] // 4
            return pl.ds(pl.multiple_of(c * 2 * r + q * r, r), r)

        def hop(k, q, src_chip, to, sem):
            rows = part(k, q)
            src = ins[k].at[rows] if src_chip is None else outs[k].at[slab(*src_chip), rows]
            own = (x, y) if src_chip is None else src_chip
            return _remote(src, outs[k].at[slab(*own), rows], send, recv, sem, (*to, c))

        small = [_remote(e_in, e_out.at[slab(x, y)], send, recv, 6 * n + j, (*to, c)) for j, to in enumerate((xn, yn, dg))]
        first = [hop(k, q, None, (xn, yn)[q], 2 * k + q) for k in range(n) for q in (0, 1)]
        for cp in small + first:
            cp.start()
        for k in range(n):
            for q in (0, 1):
                nb = (xn, yn)[q]
                _remote(ins[k].at[part(k, q)], outs[k].at[slab(*nb), part(k, q)], send, recv, 2 * k + q, (*nb, c)).wait_recv()
        second = []
        for k in range(n):
            for q in (0, 1):
                to, via = (yn, xn)[q], (xn, yn)[q]
                second.append(hop(k, q, None, to, 2 * n + 4 * k + 2 * q))
                second.append(hop(k, q, via, to, 2 * n + 4 * k + 2 * q + 1))
        for cp in second:
            cp.start()
        for k in range(n):
            for q in (0, 1):
                frm, rows = (yn, xn)[q], part(k, q)
                for j, origin in enumerate((frm, dg)):
                    _remote(ins[k].at[rows], outs[k].at[slab(*origin), rows], send, recv, 2 * n + 4 * k + 2 * q + j,
                            (*frm, c)).wait_recv()
        for j, frm in enumerate((xn, yn, dg)):
            _remote(e_in, e_out.at[slab(*frm)], send, recv, 6 * n + j, (*frm, c)).wait_recv()
        for cp in small + first + second:
            cp.wait_send()

    outs = pl.pallas_call(
        body, name="gather_ici0", in_specs=[_ANY] * (n + 1), out_specs=[_ANY] * (n + 1),
        out_shape=[jax.ShapeDtypeStruct((4,) + a.shape, a.dtype) for a in shards + (extra,)],
        scratch_shapes=[_sem(6 * n + 3), _sem(6 * n + 3)])(*shards, extra)
    return list(outs)


def _rider_gather_d2d(slabs):
    slabs = tuple(slabs)
    n = len(slabs)

    def copies(routs, sems, arrivals=True):
        send, recv = sems
        x, y, c = _place()
        out, inc = [], []
        for j, (px, py) in enumerate(_other_chips(x, y)):
            for k in range(n):
                rows = slabs[k].shape[1]
                mine, theirs = routs[k].at[2 * px + py, _half(c, rows)], routs[k].at[2 * px + py, _half(1 - c, rows)]
                out.append(_remote(mine, mine, send, recv, n * j + k, (x, y, 1 - c)))
                if arrivals:
                    inc.append(_remote(theirs, theirs, send, recv, n * j + k, (x, y, 1 - c)))
        return out, inc

    def start(rins, routs, sems):
        for cp in copies(routs, sems, arrivals=False)[0]:
            cp.start()

    def finish(rins, routs, sems):
        out, inc = copies(routs, sems)
        for cp in inc:
            cp.wait_recv()
        for cp in out:
            cp.wait_send()

    return _Rider(slabs, [jax.ShapeDtypeStruct(a.shape, a.dtype) for a in slabs], [_sem(3 * n), _sem(3 * n)], start, finish,
                  aliases={k: k for k in range(n)})


def _rider_swap(parts):
    parts = tuple(parts)
    n = len(parts)

    def copies(rins, routs, sems):
        send, recv = sems
        x, y, c = _place()
        return [_remote(rins[k].at[:, _half(1 - c, parts[k].shape[1])], routs[k], send, recv, k, (x, y, 1 - c))
                for k in range(n)]

    def start(rins, routs, sems):
        for cp in copies(rins, routs, sems):
            cp.start()

    def finish(rins, routs, sems):
        for cp in copies(rins, routs, sems):
            cp.wait()

    return _Rider(parts, [jax.ShapeDtypeStruct((a.shape[0], a.shape[1] // 2, a.shape[2]), a.dtype) for a in parts],
                  [_sem(n), _sem(n)], start, finish)


def _rider_scatter(parts):
    parts = tuple(parts)
    n = len(parts)

    def copies(rins, routs, sems, arrivals=True):
        send, recv = sems
        x, y, c = _place()
        me = 2 * x + y
        out, inc = [], []
        for j, (px, py) in enumerate(_other_chips(x, y)):
            for k in range(n):
                out.append(_remote(rins[k].at[2 * px + py], routs[k].at[me], send, recv, n * j + k, (px, py, c)))
                if arrivals:
                    inc.append(_remote(rins[k].at[me], routs[k].at[2 * px + py], send, recv, n * j + k, (px, py, c)))
        return out, inc

    def start(rins, routs, sems):
        for cp in copies(rins, routs, sems, arrivals=False)[0]:
            cp.start()

    def finish(rins, routs, sems):
        out, inc = copies(rins, routs, sems)
        for cp in inc:
            cp.wait_recv()
        for cp in out:
            cp.wait_send()

    return _Rider(parts, [jax.ShapeDtypeStruct(a.shape, a.dtype) for a in parts], [_sem(3 * n), _sem(3 * n)], start, finish)


def _rider_share(fulls):
    fulls = tuple(fulls)
    n = len(fulls)

    def copies(routs, sems, arrivals=True):
        send, recv = sems
        x, y, c = _place()
        out, inc = [], []
        for k in range(n):
            mine, theirs = routs[k].at[_half(c, fulls[k].shape[0])], routs[k].at[_half(1 - c, fulls[k].shape[0])]
            out.append(_remote(mine, mine, send, recv, k, (x, y, 1 - c)))
            if arrivals:
                inc.append(_remote(theirs, theirs, send, recv, k, (x, y, 1 - c)))
        return out, inc

    def start(rins, routs, sems):
        for cp in copies(routs, sems, arrivals=False)[0]:
            cp.start()

    def finish(rins, routs, sems):
        out, inc = copies(routs, sems)
        for cp in inc:
            cp.wait_recv()
        for cp in out:
            cp.wait_send()

    return _Rider(fulls, [jax.ShapeDtypeStruct(a.shape, a.dtype) for a in fulls], [_sem(n), _sem(n)], start, finish,
                  aliases={k: k for k in range(n)})


def _pair_sum(core, full, recv, name, br=128):
    n, rows, cols = recv.shape

    def body(c_ref, a_ref, b_ref, o_ref):
        o_ref[...] = (a_ref[...] + b_ref[...]).astype(BF16)

    nb = rows // br
    return pl.pallas_call(
        body, name=name, out_shape=jax.ShapeDtypeStruct(recv.shape, BF16),
        grid_spec=pltpu.PrefetchScalarGridSpec(
            num_scalar_prefetch=1, grid=(n, nb),
            in_specs=[pl.BlockSpec((1, br, cols), lambda i, j, c: (i, c[0] * nb + j, 0)),
                      pl.BlockSpec((1, br, cols), lambda i, j, c: (i, j, 0))],
            out_specs=pl.BlockSpec((1, br, cols), lambda i, j, c: (i, j, 0))),
        compiler_params=_cparams(("parallel", "parallel")))(core, full, recv)


def _chip_sum(place, gathered, mine, name, br=128):
    _, r, c = gathered.shape
    nb = r // br

    def body(p_ref, g_ref, m_ref, o_ref):
        slab = lambda j: jnp.where(p_ref[1] == j, m_ref[j], g_ref[j]).astype(F32)
        o_ref[...] = ((slab(0) + slab(1)) + slab(2)) + slab(3)

    return pl.pallas_call(
        body, name=name, out_shape=jax.ShapeDtypeStruct((2 * r, c), F32),
        grid_spec=pltpu.PrefetchScalarGridSpec(
            num_scalar_prefetch=1, grid=(nb,),
            in_specs=[pl.BlockSpec((4, br, c), lambda i, p: (0, i, 0)), pl.BlockSpec((4, br, c), lambda i, p: (0, i, 0))],
            out_specs=pl.BlockSpec((br, c), lambda i, p: (p[0] * nb + i, 0))),
        compiler_params=_cparams(("parallel",)))(place, gathered, mine)


def _adamw(w, g, m, v, name, br):
    n, r, c = w.shape

    def body(w_ref, g_ref, m_ref, v_ref, d_ref, m2_ref, v2_ref):
        d_ref[...], m2_ref[...], v2_ref[...] = _adam_math(w_ref[...], g_ref[...], m_ref[...], v_ref[...])

    spec = pl.BlockSpec((1, br, c), lambda i, j: (i, j, 0))
    shp = jax.ShapeDtypeStruct(w.shape, F32)
    return pl.pallas_call(body, grid=(n, r // br), name=name, in_specs=[spec] * 4, out_specs=[spec] * 3,
                          out_shape=[shp] * 3, compiler_params=_cparams(("parallel", "parallel")))(w, g, m, v)


def _adamw_w_in(w, g, m, v, name, bc=93):
    cols = w.shape[2]
    lead = lambda a: jnp.transpose(a, (2, 0, 1))
    g = jnp.stack([a[:, 0:cols] for a in g])

    def body(w_ref, g_ref, m_ref, v_ref, go_ref, d_ref, m2_ref, v2_ref):
        for l in range(2):
            gv = g_ref[:, l, :]
            d_ref[:, l, :], m2_ref[:, l, :], v2_ref[:, l, :] = _adam_math(w_ref[:, l, :], gv, m_ref[:, l, :], v_ref[:, l, :])
            go_ref[:, l, :] = gv

    spec = pl.BlockSpec((bc, 2, D), lambda i: (i, 0, 0))
    outs = pl.pallas_call(body, grid=(cols // bc,), name=name, in_specs=[spec] * 4, out_specs=[spec] * 4,
                          out_shape=[jax.ShapeDtypeStruct((cols, 2, D), F32)] * 4,
                          compiler_params=_cparams(("parallel",)))(lead(w), lead(g), lead(m), lead(v))
    return [jnp.transpose(o, (1, 2, 0)) for o in outs]


_SMALL_NAMES = ("norm_w", "conv_a_w", "gla_gate_w", "gla_gate_b", "gla_norm_w", "pool_w", "pool_scale", "ssd_conv_w",
                "ssd_conv_b", "ssd_dt_bias", "ssd_a_log", "ssd_d", "ssd_norm_w", "final_norm_w")
SMALL_ROWS = 80


def _adam_math(w, g, m, v):
    m2 = ADAM_B1 * m + (1.0 - ADAM_B1) * g
    v2 = ADAM_B2 * v + (1.0 - ADAM_B2) * (g * g)
    m_hat = m2 / (1.0 - ADAM_B1 ** ADAM_STEP)
    v_hat = v2 / (1.0 - ADAM_B2 ** ADAM_STEP)
    return -ADAM_LR * (m_hat / (jnp.sqrt(v_hat) + ADAM_EPS) + ADAM_WD * w), m2, v2


def _small_slices(name, chip):
    if name == "conv_a_w":
        return [((), slice(R_CAW, R_CAW + 3), slice(64 * chip, 64 * chip + 64))]
    if name == "ssd_conv_w":
        return [((), slice(R_SCW, R_SCW + 4), slice(192 * chip, 192 * chip + 192))]
    if name == "gla_gate_w":
        return [((), slice(0, 16), slice(768, 896))]
    if name == "pool_w":
        return [((g, slice(16 * q, 16 * q + 16)), slice(16, 32), slice(256 * q + 64 * g, 256 * q + 64 * g + 64))
                for g in range(4) for q in range(4)]
    row, lanes = {"gla_gate_b": (R_GB, slice(0, 128)), "gla_norm_w": (R_GNW, slice(0, 64)),
                  "pool_scale": (R_PSC, slice(0, 256)), "ssd_conv_b": (R_SCB, slice(0, 768)),
                  "ssd_dt_bias": (R_DTB, slice(16, 20)), "ssd_a_log": (R_AE, slice(0, 4)), "ssd_d": (R_DE, slice(0, 4)),
                  "ssd_norm_w": (R_SNW, slice(0, 256))}[name]
    return [((), slice(row, row + 1), lanes)]


def _rider_exchange(block):
    def copies(rins, routs, sems):
        send, recv = sems
        x, y, c = _place()
        flip = lambda v, bit: 1 - v if bit else v
        return [_remote(rins[0], routs[0].at[k], send, recv, k - 1, (flip(x, k & 4), flip(y, k & 2), flip(c, k & 1)))
                for k in range(1, 8)]

    def start(rins, routs, sems):
        for cp in copies(rins, routs, sems):
            cp.start()

    def finish(rins, routs, sems):
        for cp in copies(rins, routs, sems):
            cp.wait()

    return _Rider((block,), [jax.ShapeDtypeStruct((8,) + block.shape, block.dtype)], [_sem(7), _sem(7)], start, finish)


def _join_riders(a, b):
    na, oa, sa = len(a.inputs), len(a.out_shapes), len(a.sems)

    def start(rins, routs, sems):
        a.start(rins[:na], routs[:oa], sems[:sa])
        b.start(rins[na:], routs[oa:], sems[sa:])

    def finish(rins, routs, sems):
        a.finish(rins[:na], routs[:oa], sems[:sa])
        b.finish(rins[na:], routs[oa:], sems[sa:])

    aliases = {**a.aliases, **{na + k: oa + v for k, v in b.aliases.items()}}
    return _Rider(a.inputs + b.inputs, a.out_shapes + b.out_shapes, a.sems + b.sems, start, finish, aliases)


def _small_adamw(blocks, w, m, v):
    n = len(_SMALL_NAMES)

    def body(*refs):
        (own, ex), (own0, ex0) = refs[0:2], refs[2:4]
        refs = refs[3:]
        w_refs, m_refs, v_refs = refs[1:1 + n], refs[1 + n:1 + 2 * n], refs[1 + 2 * n:1 + 3 * n]
        o = 1 + 3 * n
        g_out, d_out, m_out, v_out = refs[o:o + n], refs[o + n:o + 2 * n], refs[o + 2 * n:o + 3 * n], refs[o + 3 * n:o + 4 * n]
        loss_ref, acc, acc0 = refs[o + 4 * n:o + 4 * n + 3]
        chip = 2 * lax.axis_index("x") + lax.axis_index("y")
        me = 2 * chip + lax.axis_index("c")
        acc[...] = jnp.zeros_like(acc)
        acc0[...] = jnp.zeros_like(acc0)
        for src in range(8):
            @pl.when(me == src)
            def _():
                acc[...] += own[...]
                acc0[...] += own0[...]

            @pl.when(me != src)
            def _(src=src):
                acc[...] += ex[jnp.bitwise_xor(me, src)]
                acc0[...] += ex0[jnp.bitwise_xor(me, src)]

        loss_ref[...] = acc[73:74, 0:1]

        def update(i, idx, g):
            d, m2, v2 = _adam_math(w_refs[i][idx], g, m_refs[i][idx], v_refs[i][idx])
            g_out[i][idx], d_out[i][idx], m_out[i][idx], v_out[i][idx] = g, d, m2, v2

        for i, name in enumerate(_SMALL_NAMES):
            if name == "final_norm_w":
                update(i, (slice(0, 1), slice(None)), acc[72:73, :])
            elif name == "norm_w":
                update(i, (slice(0, 1), slice(None)), acc0[0:1, :])
                update(i, (slice(1, 2), slice(None)), acc[64:65, :])
            elif name in ("conv_a_w", "ssd_conv_w"):
                for s in range(4):
                    @pl.when(chip == s)
                    def _(i=i, name=name, s=s):
                        for l in range(2):
                            (_, rows, lanes), = _small_slices(name, s)
                            update(i, (l,), acc[rows.start + 32 * l:rows.stop + 32 * l, lanes])
            else:
                for l in range(2):
                    for idx, rows, lanes in _small_slices(name, 0):
                        g = acc[rows.start + 32 * l:rows.stop + 32 * l, lanes]
                        if w_refs[i].ndim == 2:
                            update(i, (slice(l, l + 1), slice(None)), g)
                        else:
                            update(i, (l,) + idx, g)

    args = [a for pair in blocks for a in pair] + [d[k] for d in (w, m, v) for k in _SMALL_NAMES]
    shapes = [jax.ShapeDtypeStruct(w[k].shape, F32) for k in _SMALL_NAMES]
    vmem = pl.BlockSpec(memory_space=pltpu.VMEM)
    outs = pl.pallas_call(body, name="small_adamw", in_specs=[vmem] * len(args), out_specs=[vmem] * (4 * n + 1),
                          out_shape=shapes * 4 + [jax.ShapeDtypeStruct((1, 1), F32)],
                          scratch_shapes=[pltpu.VMEM((SMALL_ROWS, D), F32), pltpu.VMEM((8, D), F32)])(*args)
    return outs[0:n], outs[n:2 * n], outs[2 * n:3 * n], outs[3 * n:4 * n], outs[4 * n]


def _mixer_consts(layer, conv_a_w, gla_gate_w, gla_gate_b, gla_norm_w, pool_w, pool_scale, ssd_conv_w, ssd_conv_b,
                  ssd_dt_bias, ssd_a_log, ssd_d, ssd_norm_w):
    def row(v):
        return jnp.pad(v.reshape(1, -1), ((0, 0), (0, 768 - v.size)))

    dtb = jnp.pad(ssd_dt_bias[layer], (16, 108))
    rows = [jnp.pad(conv_a_w[layer], ((0, 0), (0, 512))), row(gla_gate_b[layer]), row(jnp.tile(gla_norm_w[layer], 4)),
            row(pool_scale[layer]), row(ssd_conv_b[layer]), row(dtb), row(jnp.repeat(-jnp.exp(ssd_a_log[layer]), 64)),
            row(jnp.repeat(ssd_d[layer], 64)), row(ssd_norm_w[layer]), jnp.zeros((1, 768), F32), ssd_conv_w[layer]]
    prm = jnp.concatenate(rows, axis=0)
    gw = jnp.pad(gla_gate_w[layer], ((0, 112), (0, 0))).astype(BF16)
    on_diag = (_iota((256, 256), 0) >> 6) == (_iota((256, 256), 1) >> 6)
    pw = jnp.where(on_diag, jnp.tile(pool_w[layer].reshape(256, 64), (1, 4)), 0.0)
    return (prm, gw, pw.astype(BF16)) + _mixer_matrices()


def _grad_slabs(dwp, dwo):
    return dwp.reshape(1, D, NP), dwo.reshape(4, D // 4, D)


class _Comm:
    def __init__(self, w_in, w_out):
        self.w_in16 = jnp.pad(w_in.astype(BF16), ((0, 0), (0, 0), (0, SHARD_PAD - SHARD)))
        self.w_out16 = w_out.astype(BF16)
        self.core = lax.axis_index("c").astype(jnp.int32).reshape(1)
        self.chip = 2 * lax.axis_index("x") + lax.axis_index("y")
        self.place = jnp.stack([lax.axis_index("c"), self.chip]).astype(jnp.int32)

    def gather_ici(self, layer):
        return _rider_gather_ici((self.w_in16[layer], self.w_out16[layer]))

    def pair_sum(self, layer, slabs, received):
        d_in, d_out = [_pair_sum(self.core, a, b, name=f"reduce_pair_sum{layer}_{k}")
                       for k, (a, b) in enumerate(zip(slabs, received))]
        return [_split_dw_in(d_in[0], name=f"split_dw_in{layer}"), d_out]

    def chip_sum(self, layer, gathered, mine):
        return [_chip_sum(self.place, a, b, name=f"reduce_chip_sum{layer}_{k}") for k, (a, b) in enumerate(zip(gathered, mine))]

    def layer_weights(self, layer, s_in, s_out):
        own = lambda slabs, shard: jnp.stack([jnp.where(self.chip == s, shard, slabs[s]) for s in range(4)])
        wp, wpt = _assemble_w_in(own(s_in, self.w_in16[layer]), name=f"assemble_w_in{layer}")
        wo = own(s_out, self.w_out16[layer]).reshape(D, D)
        return wp, wpt, wo, wo.T


def _local_step(x, tgt, norm_w, final_norm_w, consts, wts0, wts1=None, comm=None):
    nw = [norm_w[l:l + 1] for l in range(2)]
    proj0, h0, slabs = _rmsproj(x, nw[0], wts0[0], name="rmsproj0", rider=comm and comm.gather_ici(1))
    (mix0, sg0, ss0, x1, *conv0), slabs = _mixer_fwd(proj0, x, wts0[2], *consts[0], name="mixer_fwd0",
                                                     rider=comm and _rider_gather_d2d(slabs))
    if comm:
        wts1 = comm.layer_weights(1, *slabs)
    proj1, h1, _ = _rmsproj(x1, nw[1], wts1[0], name="rmsproj1")
    (mix1, sg1, ss1, dx, *conv1, head), _ = _mixer_fwd(proj1, x1, wts1[2], *consts[1], name="mixer_fwd1",
                                                       head=(tgt, final_norm_w.reshape(1, D)))
    (dproj, mgr1, dwo1), _ = _mixer_bwd(proj1, dx, wts1[3], mix1, sg1, ss1, *conv1, *consts[1], name="mixer_bwd1")
    dwp1, _ = _dwin(h1, dproj, name="dwin1")
    slabs1 = comm and _grad_slabs(dwp1, dwo1)
    (dx, dnw1), recv = _dxin(dproj, wts1[1], x1, dx, nw[1], name="dxin1", rider=comm and _rider_swap(slabs1))
    pairs1 = comm and comm.pair_sum(1, slabs1, recv)
    (dproj, mgr0, dwo0), gathered = _mixer_bwd(proj0, dx, wts0[3], mix0, sg0, ss0, *conv0, *consts[0], name="mixer_bwd0",
                                               rider=comm and _rider_scatter(pairs1))
    if not comm:
        dwp0, _ = _dwin(h0, dproj, name="dwin0")
        (dx, dnw0), _ = _dxin(dproj, wts0[1], x, dx, nw[0], name="dxin0")
        return head, dx, ((dwp0, dwp1), (dwo0, dwo1)), (dnw0, dnw1), (mgr0, mgr1)
    dwo0 = dwo0.reshape(4, D // 4, D)
    dwp0, (*big1, recv_out) = _dwin(h0, dproj, name="dwin0", rider=_join_riders(
        _rider_share(comm.chip_sum(1, gathered, pairs1)), _rider_swap((dwo0,))))
    slabs0 = (dwp0.reshape(1, D, NP), dwo0)
    pairs0 = comm.pair_sum(0, slabs0, (_run_rider(_rider_swap(slabs0[0:1]), "reduce_swap0")[0], recv_out))
    small = jnp.concatenate([mgr0, mgr1, dnw1, head], axis=0)
    (dx, dnw0), gathered = _dxin(dproj, wts0[1], x, dx, nw[0], name="dxin0",
                                 rider=_join_riders(_rider_scatter(pairs0), _rider_exchange(small)))
    last = _run_rider(_join_riders(_rider_share(comm.chip_sum(0, gathered[0:2], pairs0)), _rider_exchange(dnw0)),
                      "reduce_share0")
    return dx, ((last[0], big1[0]), (last[1], big1[1])), ((small, gathered[2]), (dnw0, last[2]))


def kernel(x, norm_w, w_in, conv_a_w, gla_gate_w, gla_gate_b, gla_norm_w, pool_w, pool_scale, ssd_conv_w, ssd_conv_b, ssd_dt_bias, ssd_a_log, ssd_d, ssd_norm_w, w_out, final_norm_w, loss_target, m_norm_w, m_w_in, m_conv_a_w, m_gla_gate_w, m_gla_gate_b, m_gla_norm_w, m_pool_w, m_pool_scale, m_ssd_conv_w, m_ssd_conv_b, m_ssd_dt_bias, m_ssd_a_log, m_ssd_d, m_ssd_norm_w, m_w_out, m_final_norm_w, v_norm_w, v_w_in, v_conv_a_w, v_gla_gate_w, v_gla_gate_b, v_gla_norm_w, v_pool_w, v_pool_scale, v_ssd_conv_w, v_ssd_conv_b, v_ssd_dt_bias, v_ssd_a_log, v_ssd_d, v_ssd_norm_w, v_w_out, v_final_norm_w):
    weights = dict(norm_w=norm_w, w_in=w_in, conv_a_w=conv_a_w, gla_gate_w=gla_gate_w, gla_gate_b=gla_gate_b,
                   gla_norm_w=gla_norm_w, pool_w=pool_w, pool_scale=pool_scale, ssd_conv_w=ssd_conv_w,
                   ssd_conv_b=ssd_conv_b, ssd_dt_bias=ssd_dt_bias, ssd_a_log=ssd_a_log, ssd_d=ssd_d,
                   ssd_norm_w=ssd_norm_w, w_out=w_out, final_norm_w=final_norm_w)
    m_in = dict(norm_w=m_norm_w, w_in=m_w_in, conv_a_w=m_conv_a_w, gla_gate_w=m_gla_gate_w, gla_gate_b=m_gla_gate_b,
                gla_norm_w=m_gla_norm_w, pool_w=m_pool_w, pool_scale=m_pool_scale, ssd_conv_w=m_ssd_conv_w,
                ssd_conv_b=m_ssd_conv_b, ssd_dt_bias=m_ssd_dt_bias, ssd_a_log=m_ssd_a_log, ssd_d=m_ssd_d,
                ssd_norm_w=m_ssd_norm_w, w_out=m_w_out, final_norm_w=m_final_norm_w)
    v_in = dict(norm_w=v_norm_w, w_in=v_w_in, conv_a_w=v_conv_a_w, gla_gate_w=v_gla_gate_w, gla_gate_b=v_gla_gate_b,
                gla_norm_w=v_gla_norm_w, pool_w=v_pool_w, pool_scale=v_pool_scale, ssd_conv_w=v_ssd_conv_w,
                ssd_conv_b=v_ssd_conv_b, ssd_dt_bias=v_ssd_dt_bias, ssd_a_log=v_ssd_a_log, ssd_d=v_ssd_d,
                ssd_norm_w=v_ssd_norm_w, w_out=v_w_out, final_norm_w=v_final_norm_w)
    order = ("norm_w", "w_in", "conv_a_w", "gla_gate_w", "gla_gate_b", "gla_norm_w", "pool_w", "pool_scale",
             "ssd_conv_w", "ssd_conv_b", "ssd_dt_bias", "ssd_a_log", "ssd_d", "ssd_norm_w", "w_out", "final_norm_w")
    t = x.shape[1]

    comm = _Comm(w_in, w_out)
    cshard = jnp.zeros((16, 256), F32)
    for l in range(2):
        cshard = cshard.at[8 * l:8 * l + 3, 0:64].set(conv_a_w[l]).at[8 * l + 3:8 * l + 7, 0:192].set(ssd_conv_w[l])
    s_in, s_out, g_c = _gather_ici_two_hops((comm.w_in16[0], comm.w_out16[0]), cshard)
    s_in, s_out = _run_rider(_rider_gather_d2d((s_in, s_out)), "gather_d2d0")
    g_c = [jnp.where(comm.chip == s, cshard, g_c[s]) for s in range(4)]
    conv_a_full = jnp.stack([jnp.concatenate([g_c[s][8 * l:8 * l + 3, 0:64] for s in range(4)], axis=-1) for l in range(2)])
    ssd_conv_full = jnp.stack([jnp.concatenate([g_c[s][8 * l + 3:8 * l + 7, 0:192] for s in range(4)], axis=-1)
                               for l in range(2)])
    consts = [_mixer_consts(l, conv_a_full, gla_gate_w, gla_gate_b, gla_norm_w, pool_w, pool_scale, ssd_conv_full,
                            ssd_conv_b, ssd_dt_bias, ssd_a_log, ssd_d, ssd_norm_w) for l in range(2)]

    dx, big, blocks = _local_step(x.reshape(t, D), loss_target.reshape(t, D), norm_w, final_norm_w, consts,
                                  comm.layer_weights(0, s_in, s_out), comm=comm)

    as2d = lambda d: {k: (d[k].reshape(1, D) if k == "final_norm_w" else d[k]) for k in _SMALL_NAMES}
    small = _small_adamw(blocks, as2d(weights), as2d(m_in), as2d(v_in))
    grads, delta, new_m, new_v = ({k: (a.reshape(D) if k == "final_norm_w" else a) for k, a in zip(_SMALL_NAMES, part)}
                                  for part in small[0:4])
    loss = small[4].reshape(())

    grads["w_out"] = jnp.stack(big[1])

    grads["w_in"], delta["w_in"], new_m["w_in"], new_v["w_in"] = _adamw_w_in(w_in, big[0], m_w_in, v_w_in, name="adamw_w_in")
    delta["w_out"], new_m["w_out"], new_v["w_out"] = _adamw(w_out, grads["w_out"], m_w_out, v_w_out, name="adamw_w_out", br=256)

    return (loss, dx.reshape(1, t, D), *[grads[k] for k in order], *[delta[k] for k in order],
            *[new_m[k] for k in order], *[new_v[k] for k in order])
```

```python
import functools

import jax
import jax.numpy as jnp
from jax import lax
from jax.experimental import pallas as pl
from jax.experimental.pallas import tpu as pltpu

F32 = jnp.float32
BF16 = jnp.bfloat16
MESH = pl.DeviceIdType.MESH

D = 1024
CH = 64
EPS = 1e-6
NP = 3456
NPROJ = 3348
NPM = 3328
GLA_SCALE = 32.0 ** -0.5
INV_TAU = 1.0 / 16.0
TB = 512
NCH = TB // CH
assert TB % 256 == 0

C_AH, C_AB, C_AC, C_AZ, C_GQ, C_GK, C_GV = 0, 256, 512, 768, 1024, 1152, 1280
C_GZ, C_PU, C_PZ, C_SZ, C_SX, C_TL = 1536, 1792, 2048, 2304, 2560, 3328
_PERM = ((0, 1536), (1552, 1792), (1536, 16), (3344, 4))

R_CAW, R_GB, R_GNW, R_PSC, R_SCB, R_DTB, R_AE, R_DE, R_SNW, R_SCW = 0, 3, 4, 5, 6, 7, 8, 9, 10, 12

ADAM_LR, ADAM_B1, ADAM_B2, ADAM_EPS, ADAM_WD, ADAM_STEP = 0.001, 0.9, 0.999, 1e-08, 0.01, 10

VMEM_LIMIT = 56 * 1024 * 1024


def _cparams(sem, limit=VMEM_LIMIT):
    return pltpu.CompilerParams(dimension_semantics=sem, vmem_limit_bytes=limit)


_ANY = pl.BlockSpec(memory_space=pl.ANY)


def _place():
    return lax.axis_index("x"), lax.axis_index("y"), lax.axis_index("c")


class _Rider:
    def __init__(self, inputs, out_shapes, sems, start, finish, aliases=None):
        self.inputs, self.out_shapes, self.sems = tuple(inputs), tuple(out_shapes), tuple(sems)
        self.start, self.finish, self.aliases = start, finish, dict(aliases or {})


def _call(body, args, *, grid, in_specs, out_specs, out_shape, name, sem, scratch_shapes=(), rider=None):
    if rider is None:
        outs = pl.pallas_call(body, grid=grid, name=name, in_specs=list(in_specs), out_specs=list(out_specs),
                              out_shape=list(out_shape), scratch_shapes=list(scratch_shapes),
                              compiler_params=_cparams(sem))(*args)
        return list(outs), []
    ni, no, ns = len(args), len(out_shape), len(scratch_shapes)
    ri, ro = len(rider.inputs), len(rider.out_shapes)

    def full(*refs):
        ins, rins = refs[:ni], refs[ni:ni + ri]
        outs, routs = refs[ni + ri:ni + ri + no], refs[ni + ri + no:ni + ri + no + ro]
        scr, rsem = refs[ni + ri + no + ro:ni + ri + no + ro + ns], refs[ni + ri + no + ro + ns:]
        first = functools.reduce(jnp.logical_and, [pl.program_id(a) == 0 for a in range(len(grid))])
        last = functools.reduce(jnp.logical_and, [pl.program_id(a) == grid[a] - 1 for a in range(len(grid))])

        @pl.when(first)
        def _():
            rider.start(rins, routs, rsem)

        body(*ins, *outs, *scr)

        @pl.when(last)
        def _():
            rider.finish(rins, routs, rsem)

    outs = pl.pallas_call(
        full, grid=grid, name=name, in_specs=list(in_specs) + [_ANY] * ri, out_specs=list(out_specs) + [_ANY] * ro,
        out_shape=list(out_shape) + list(rider.out_shapes), scratch_shapes=list(scratch_shapes) + list(rider.sems),
        input_output_aliases={ni + k: no + v for k, v in rider.aliases.items()},
        compiler_params=_cparams(("arbitrary",) * len(grid)))(*args, *rider.inputs)
    return list(outs[:no]), list(outs[no:])


def _run_rider(rider, name):
    ri = len(rider.inputs)

    def body(*refs):
        rins, routs, rsem = refs[:ri], refs[ri:ri + len(rider.out_shapes)], refs[ri + len(rider.out_shapes):]
        rider.start(rins, routs, rsem)
        rider.finish(rins, routs, rsem)

    return list(pl.pallas_call(body, name=name, in_specs=[_ANY] * ri, out_specs=[_ANY] * len(rider.out_shapes),
                               out_shape=list(rider.out_shapes), scratch_shapes=list(rider.sems),
                               input_output_aliases=dict(rider.aliases))(*rider.inputs))


def _dot(a, b):
    return jnp.dot(a.astype(BF16), b.astype(BF16), preferred_element_type=F32)


def _dot_nt(a, b):
    return lax.dot_general(a.astype(BF16), b.astype(BF16), (((1,), (1,)), ((), ())), preferred_element_type=F32)


def _dot_tn(a, b):
    return lax.dot_general(a.astype(BF16), b.astype(BF16), (((0,), (0,)), ((), ())), preferred_element_type=F32)


def _split(a):
    hi = a.astype(BF16)
    lo = (a - hi.astype(F32)).astype(BF16)
    return hi, lo


def _dot2_l(a, b):
    hi, lo = _split(a)
    return _dot(hi, b) + _dot(lo, b)


def _dot2_r(a, b):
    hi, lo = _split(b)
    return _dot(a, hi) + _dot(a, lo)


def _dot3_l(a, b):
    hi, lo = _split(a)
    lo2 = ((a - hi.astype(F32)) - lo.astype(F32)).astype(BF16)
    return _dot(hi, b) + _dot(lo, b) + _dot(lo2, b)


def _dot2_nt(a, b):
    hi, lo = _split(a)
    return _dot_nt(hi, b) + _dot_nt(lo, b)


def _silu(z):
    return z * jax.nn.sigmoid(z)


def _lse1(x):
    return jnp.log(1.0 + jnp.exp(-jnp.abs(x)))


def _cs(a):
    return jnp.sum(a, axis=0, keepdims=True)


def _iota(shape, dim):
    return lax.broadcasted_iota(jnp.int32, shape, dim)


def _mixer_matrices():
    r, c = _iota((256, 256), 0), _iota((256, 256), 1)
    same_chunk = (r >> 6) == (c >> 6)
    mats = jnp.stack([jnp.where((c > r) & same_chunk, 1.0, 0.0), jnp.where((c < r) & same_chunk, 1.0, 0.0),
                      jnp.where(same_chunk, 1.0 / 64.0, 0.0), jnp.where((r < 128) & (r - 16 == (c >> 6)), 1.0, 0.0)])
    mask = jnp.where((_iota((256, 128), 0) >> 6) == (_iota((256, 128), 1) >> 5), 1.0, 0.0)
    return mats.astype(BF16), mask.astype(F32)


def _dn(ext, k, n, h):
    return pltpu.roll(ext, k, axis=0)[h:h + n]


def _up(ext, k, n):
    return pltpu.roll(ext, ext.shape[0] - k, axis=0)[:n]


def _pool_lane_select(lane, s2, s4, s8, s16):
    return jnp.where(lane < 64, s2, jnp.where(lane < 128, s4, jnp.where(lane < 192, s8, s16)))


def _winsum_dn(ext, lane):
    s2 = ext + pltpu.roll(ext, 1, axis=0)
    s4 = s2 + pltpu.roll(s2, 2, axis=0)
    s8 = s4 + pltpu.roll(s4, 4, axis=0)
    s16 = s8 + pltpu.roll(s8, 8, axis=0)
    return _pool_lane_select(lane, s2, s4, s8, s16)


def _winsum_up(ext, lane):
    m = ext.shape[0]
    s2 = ext + pltpu.roll(ext, m - 1, axis=0)
    s4 = s2 + pltpu.roll(s2, m - 2, axis=0)
    s8 = s4 + pltpu.roll(s4, m - 4, axis=0)
    s16 = s8 + pltpu.roll(s8, m - 8, axis=0)
    return _pool_lane_select(lane, s2, s4, s8, s16)


def _pool_inv_count(tile, n):
    lane = _iota((1, 256), 1)
    win = _pool_lane_select(lane, 2.0, 4.0, 8.0, 16.0).astype(F32)
    tpos = (tile * n + _iota((n, 1), 0) + 1).astype(F32)
    return jnp.where(tpos >= win, 1.0 / win, 1.0 / tpos)


def _silu_pair(z):
    s = jax.nn.sigmoid(z)
    return z * s, s * (1.0 + z * (1.0 - s))


def _chunks(a):
    return [a[c * CH:(c + 1) * CH] for c in range(a.shape[0] // CH)]


def _halves(fn, a, b):
    return jnp.concatenate([fn(a[:, 0:128], b[:, 0:128]), fn(a[:, 128:256], b[:, 128:256])], axis=1)


def _chunk_sums(tri, a):
    return jnp.concatenate([_dot2_r(tri, a[r:r + 256]) for r in range(0, a.shape[0], 256)], axis=0)


def _mixer_tile_prep(p_ref, t_ref, xc, prm_ref, gw_v, cm_ref, mk_ref):
    tail = t_ref[...]
    pre = _dot(tail, gw_v) + prm_ref[R_GB:R_GB + 1, 0:128]
    la = (jnp.minimum(pre, 0.0) - _lse1(pre)) * INV_TAU
    dtin = tail + prm_ref[R_DTB:R_DTB + 1, 0:128]
    dtf = jnp.maximum(dtin, 0.0) + _lse1(dtin)
    dte = _dot2_l(dtf, cm_ref[3, 0:128, :])
    da = dte * prm_ref[R_AE:R_AE + 1, 0:256]
    rev = _chunk_sums(cm_ref[0], jnp.concatenate([la, da], axis=1))
    dec = jnp.exp(rev[:, 0:128])
    kd = p_ref[:, C_GK:C_GK + 128].astype(F32) * dec
    wdec = jnp.exp(rev[:, 128:384])
    w = wdec * dte
    xw = xc[:, 0:256] * w
    d_s = [jnp.exp(_cs(a)) for a in _chunks(la)]
    et = [jnp.exp(_cs(a)) for a in _chunks(da)]
    mask_t = mk_ref[...]
    ut_g = [_dot_tn(v, k) * mask_t for v, k in zip(_chunks(p_ref[:, C_GV:C_GV + 256].astype(F32)), _chunks(kd))]
    ut_s = [_halves(_dot_tn, b, x) for b, x in zip(_chunks(xc[:, 256:512]), _chunks(xw))]
    return tail, pre, dtin, dte, dec, kd, wdec, w, xw, d_s, et, ut_g, ut_s


def _rmsproj(x, nw, wp, name, tm=512, rider=None):
    t = x.shape[0]

    def body(x_ref, nw_ref, w_ref, o_ref, t_ref, h_ref):
        xv = x_ref[...]
        rs = lax.rsqrt(jnp.mean(xv * xv, axis=-1, keepdims=True) + EPS)
        h = (xv * rs * nw_ref[...]).astype(BF16)
        h_ref[...] = h
        proj = jnp.dot(h, w_ref[...], preferred_element_type=F32)
        o_ref[...] = proj[:, 0:NPM].astype(BF16)
        t_ref[...] = proj[:, NPM:NP]

    (proj, tail, h), extra = _call(
        body, (x, nw, wp), grid=(t // tm,), name=name, sem=("parallel",), rider=rider,
        in_specs=[pl.BlockSpec((tm, D), lambda i: (i, 0)), pl.BlockSpec((1, D), lambda i: (0, 0)),
                  pl.BlockSpec((D, NP), lambda i: (0, 0))],
        out_specs=[pl.BlockSpec((tm, NPM), lambda i: (i, 0)), pl.BlockSpec((tm, NP - NPM), lambda i: (i, 0)),
                   pl.BlockSpec((tm, D), lambda i: (i, 0))],
        out_shape=[jax.ShapeDtypeStruct((t, NPM), BF16), jax.ShapeDtypeStruct((t, NP - NPM), F32),
                   jax.ShapeDtypeStruct((t, D), BF16)])
    return (proj, tail), h, extra


def _head_tile(xv, tgt, w):
    rs = lax.rsqrt(jnp.mean(xv * xv, axis=-1, keepdims=True) + EPS)
    xh = xv * rs
    err = xh * w - tgt
    dy = err * (1.0 / D)
    dxh = dy * w
    dx = rs * (dxh - xh * jnp.mean(dxh * xh, axis=-1, keepdims=True))
    return dx, _cs(dy * xh), (0.5 / D) * jnp.sum(err * err)


def _dxin(dp, wpt, x, dxn, nw, name, tm=512, rider=None):
    t = x.shape[0]

    def body(dp_ref, w_ref, x_ref, dxn_ref, nw_ref, dx_ref, dnw_ref):
        @pl.when(pl.program_id(0) == 0)
        def _():
            dnw_ref[...] = jnp.zeros_like(dnw_ref)

        dh = jnp.dot(dp_ref[...], w_ref[...], preferred_element_type=F32)
        xv = x_ref[...]
        rs = lax.rsqrt(jnp.mean(xv * xv, axis=-1, keepdims=True) + EPS)
        xh = xv * rs
        dnw_ref[0:1, :] += _cs(dh * xh)
        dxh = dh * nw_ref[...]
        dx_ref[...] = dxn_ref[...] + rs * (dxh - xh * jnp.mean(dxh * xh, axis=-1, keepdims=True))

    return _call(
        body, (dp, wpt, x, dxn, nw), grid=(t // tm,), name=name, sem=("arbitrary",), rider=rider,
        in_specs=[pl.BlockSpec((tm, NP), lambda i: (i, 0)), pl.BlockSpec((NP, D), lambda i: (0, 0)),
                  pl.BlockSpec((tm, D), lambda i: (i, 0)), pl.BlockSpec((tm, D), lambda i: (i, 0)),
                  pl.BlockSpec((1, D), lambda i: (0, 0))],
        out_specs=[pl.BlockSpec((tm, D), lambda i: (i, 0)), pl.BlockSpec((8, D), lambda i: (0, 0))],
        out_shape=[jax.ShapeDtypeStruct((t, D), F32), jax.ShapeDtypeStruct((8, D), F32)])


def _dwin(h, dp, name, tm=1024, rider=None):
    t = h.shape[0]

    def body(h_ref, dp_ref, o_ref):
        @pl.when(pl.program_id(0) == 0)
        def _():
            o_ref[...] = jnp.zeros_like(o_ref)

        o_ref[...] += _dot_tn(h_ref[...], dp_ref[...])

    (dwp,), extra = _call(
        body, (h, dp), grid=(t // tm,), name=name, sem=("arbitrary",), rider=rider,
        in_specs=[pl.BlockSpec((tm, D), lambda i: (i, 0)), pl.BlockSpec((tm, NP), lambda i: (i, 0))],
        out_specs=[pl.BlockSpec((D, NP), lambda i: (0, 0))], out_shape=[jax.ShapeDtypeStruct((D, NP), F32)])
    return dwp, extra


def _mixer_fwd(proj, x, wo, prm, gw, pw, cmat, mask, name, rider=None, head=None):
    proj, tail = proj
    t = proj.shape[0]
    nt, nc = t // TB, t // CH

    def body(p_ref, t_ref, x_ref, wo_ref, prm_ref, gw_ref, pw_ref, cm_ref, mk_ref, *rest):
        (tgt_ref, fw_ref), rest = (rest[:2], rest[2:]) if head else ((None, None), rest)
        mix_ref, sg_ref, ss_ref, xn_ref, xc_ref, dxc_ref, cv_ref, pool_ref = rest[:8]
        acc_ref = rest[8] if head else None
        sg_s, ss_s, h_ua, h_pu, h_sx = rest[-5:]
        i = pl.program_id(0)

        @pl.when(i == 0)
        def _():
            for r in (sg_s, ss_s, h_ua, h_pu, h_sx) + ((acc_ref,) if head else ()):
                r[...] = jnp.zeros_like(r)

        lane = _iota((1, 256), 1)
        u = p_ref[:, C_AC:C_AC + 256].astype(F32) * p_ref[:, C_AH:C_AH + 256].astype(F32)
        ext = jnp.concatenate([h_ua[...], u], axis=0)
        cv = (prm_ref[R_CAW + 2:R_CAW + 3, 0:256] * u + prm_ref[R_CAW + 1:R_CAW + 2, 0:256] * _dn(ext, 1, TB, 8)
              + prm_ref[R_CAW:R_CAW + 1, 0:256] * _dn(ext, 2, TB, 8))
        cv_ref[...] = cv.astype(BF16)
        mix_ref[:, 0:256] = (p_ref[:, C_AB:C_AB + 256].astype(F32) * cv * _silu(p_ref[:, C_AZ:C_AZ + 256].astype(F32))).astype(BF16)
        h_ua[...] = u[TB - 8:, :]
        pu = p_ref[:, C_PU:C_PU + 256].astype(F32)
        ext = jnp.concatenate([h_pu[...], pu], axis=0)
        pooled = (_winsum_dn(ext, lane)[16:] * _pool_inv_count(i, TB) - pu).astype(BF16)
        pool_ref[...] = pooled
        mixed = jnp.dot(pooled, pw_ref[...], preferred_element_type=F32)
        mix_ref[:, 512:768] = (prm_ref[R_PSC:R_PSC + 1, 0:256] * mixed * _silu(p_ref[:, C_PZ:C_PZ + 256].astype(F32))).astype(BF16)
        h_pu[...] = pu[TB - 16:, :]
        sx = p_ref[:, C_SX:C_SX + 768].astype(F32)
        ext = jnp.concatenate([h_sx[...], sx], axis=0)
        xc, dxc = _silu_pair(prm_ref[R_SCW + 3:R_SCW + 4, :] * sx + prm_ref[R_SCW + 2:R_SCW + 3, :] * _dn(ext, 1, TB, 8)
                             + prm_ref[R_SCW + 1:R_SCW + 2, :] * _dn(ext, 2, TB, 8)
                             + prm_ref[R_SCW:R_SCW + 1, :] * _dn(ext, 3, TB, 8) + prm_ref[R_SCB:R_SCB + 1, :])
        xc_ref[...] = xc.astype(BF16)
        dxc_ref[...] = dxc.astype(BF16)
        h_sx[...] = sx[TB - 8:, :]

        _, _, _, _, _, _, _, _, _, d_s, et, ut_g, ut_s = _mixer_tile_prep(p_ref, t_ref, xc, prm_ref, gw_ref[...], cm_ref, mk_ref)
        s_g, s_s = sg_s[...], ss_s[...]
        o, y = [], []
        qs = _chunks(p_ref[:, C_GQ:C_GQ + 128].astype(F32) * GLA_SCALE)
        cm = _chunks(xc[:, 512:768])
        for c in range(NCH):
            sg_ref[c] = s_g
            ss_ref[c] = s_s
            s_g = s_g * d_s[c] + ut_g[c]
            s_s = s_s * et[c] + ut_s[c]
            o.append(_dot_nt(qs[c], s_g))
            y.append(_halves(_dot, cm[c], s_s))
        sg_s[...] = s_g
        ss_s[...] = s_s
        o = jnp.concatenate(o, axis=0)
        on = o * lax.rsqrt(_dot2_l(o * o, cm_ref[2]) + EPS)
        mix_ref[:, 256:512] = (on * prm_ref[R_GNW:R_GNW + 1, 0:256] * _silu(p_ref[:, C_GZ:C_GZ + 256].astype(F32))).astype(BF16)
        y2 = ((jnp.concatenate(y, axis=0) + prm_ref[R_DE:R_DE + 1, 0:256] * xc[:, 0:256])
              * _silu(p_ref[:, C_SZ:C_SZ + 256].astype(F32)))
        mix_ref[:, 768:1024] = (y2 * lax.rsqrt(jnp.mean(y2 * y2, axis=-1, keepdims=True) + EPS)
                                * prm_ref[R_SNW:R_SNW + 1, 0:256]).astype(BF16)
        xn = x_ref[...] + jnp.dot(mix_ref[...], wo_ref[...], preferred_element_type=F32)
        if head:
            xn_ref[...], dfw, loss = _head_tile(xn, tgt_ref[...], fw_ref[...])
            acc_ref[0:1, :] += dfw
            acc_ref[1:2, :] += jnp.zeros((1, D), F32) + loss
        else:
            xn_ref[...] = xn

    row = pl.BlockSpec((TB, D), lambda i: (i, 0))
    return _call(
        body, (proj, tail, x, wo, prm, gw, pw, cmat, mask) + tuple(head or ()), grid=(nt,), name=name, sem=("arbitrary",),
        rider=rider,
        in_specs=[pl.BlockSpec((TB, NPM), lambda i: (i, 0)), pl.BlockSpec((TB, NP - NPM), lambda i: (i, 0)), row,
                  pl.BlockSpec((D, D), lambda i: (0, 0)), pl.BlockSpec((16, 768), lambda i: (0, 0)),
                  pl.BlockSpec((128, 128), lambda i: (0, 0)), pl.BlockSpec((256, 256), lambda i: (0, 0)),
                  pl.BlockSpec((4, 256, 256), lambda i: (0, 0, 0)), pl.BlockSpec((256, 128), lambda i: (0, 0))]
        + ([row, pl.BlockSpec((1, D), lambda i: (0, 0))] if head else []),
        out_specs=[row, pl.BlockSpec((NCH, 256, 128), lambda i: (i, 0, 0)),
                   pl.BlockSpec((NCH, 128, 256), lambda i: (i, 0, 0)), row] + [pl.BlockSpec((TB, 768), lambda i: (i, 0))] * 2
        + [pl.BlockSpec((TB, 256), lambda i: (i, 0))] * 2 + ([pl.BlockSpec((8, D), lambda i: (0, 0))] if head else []),
        out_shape=[jax.ShapeDtypeStruct((t, D), BF16), jax.ShapeDtypeStruct((nc, 256, 128), F32),
                   jax.ShapeDtypeStruct((nc, 128, 256), F32), jax.ShapeDtypeStruct((t, D), F32)]
        + [jax.ShapeDtypeStruct((t, 768), BF16)] * 2 + [jax.ShapeDtypeStruct((t, 256), BF16)] * 2
        + ([jax.ShapeDtypeStruct((8, D), F32)] if head else []),
        scratch_shapes=[pltpu.VMEM((256, 128), F32), pltpu.VMEM((128, 256), F32), pltpu.VMEM((8, 256), F32),
                        pltpu.VMEM((16, 256), F32), pltpu.VMEM((8, 768), F32)])


def _mixer_bwd(proj, dxn, wot, mix, sg, ss, xc16, dxc16, cv16, pool16, prm, gw, pw, cmat, mask, name, rider=None):
    proj, tail = proj
    t = proj.shape[0]
    nt = t // TB
    rev = lambda i: nt - 1 - i

    def body(p_ref, t_ref, dxn_ref, wot_ref, mix_ref, sg_ref, ss_ref, xc_ref, dxc_ref, cv_ref, pool_ref, prm_ref, gw_ref,
             pw_ref, cm_ref, mk_ref, dp_ref, sgc_ref, dwo_ref,
             gg_s, gs_s, h_dcv, h_dpl, h_dpre, gsm_ref, dgw_ref, dpw_ref, dm_ref):
        i = pl.program_id(0)
        tile = nt - 1 - i

        @pl.when(i == 0)
        def _():
            for r in (gg_s, gs_s, h_dcv, h_dpl, h_dpre, gsm_ref, dgw_ref, dpw_ref, dwo_ref):
                r[...] = jnp.zeros_like(r)

        dxn = dxn_ref[...].astype(BF16)
        dm_ref[...] = jnp.dot(dxn, wot_ref[...], preferred_element_type=F32)
        dwo_ref[...] += _dot_tn(mix_ref[...], dxn)

        lane = _iota((1, 256), 1)
        ah, ac = p_ref[:, C_AH:C_AH + 256].astype(F32), p_ref[:, C_AC:C_AC + 256].astype(F32)
        ab, az = p_ref[:, C_AB:C_AB + 256].astype(F32), p_ref[:, C_AZ:C_AZ + 256].astype(F32)
        w0, w1, w2 = (prm_ref[R_CAW + j:R_CAW + j + 1, 0:256] for j in range(3))
        u = ac * ah
        cv = cv_ref[...].astype(F32)
        g = dm_ref[:, 0:256]
        sz, dsz = _silu_pair(az)
        dp_ref[:, C_AB:C_AB + 256] = (g * cv * sz).astype(BF16)
        dp_ref[:, C_AZ:C_AZ + 256] = (g * ab * cv * dsz).astype(BF16)
        dcv = g * ab * sz
        dext = jnp.concatenate([dcv, h_dcv[...]], axis=0)
        dcv1, dcv2 = _up(dext, 1, TB), _up(dext, 2, TB)
        du = w2 * dcv + w1 * dcv1 + w0 * dcv2
        dp_ref[:, C_AC:C_AC + 256] = (du * ah).astype(BF16)
        dp_ref[:, C_AH:C_AH + 256] = (du * ac).astype(BF16)
        gsm_ref[R_CAW:R_CAW + 1, 0:256] += _cs(u * dcv2)
        gsm_ref[R_CAW + 1:R_CAW + 2, 0:256] += _cs(u * dcv1)
        gsm_ref[R_CAW + 2:R_CAW + 3, 0:256] += _cs(u * dcv)
        h_dcv[...] = dcv[0:8, :]
        pz = p_ref[:, C_PZ:C_PZ + 256].astype(F32)
        psc = prm_ref[R_PSC:R_PSC + 1, 0:256]
        icnt = _pool_inv_count(tile, TB)
        pooled = pool_ref[...]
        pw_v = pw_ref[...]
        mixed = jnp.dot(pooled, pw_v, preferred_element_type=F32)
        g = dm_ref[:, 512:768]
        sz, dsz = _silu_pair(pz)
        gsm_ref[R_PSC:R_PSC + 1, 0:256] += _cs(g * mixed * sz)
        dp_ref[:, C_PZ:C_PZ + 256] = (g * psc * mixed * dsz).astype(BF16)
        dmixed = g * psc * sz
        dpw_ref[...] += _dot_tn(pooled, dmixed)
        dpooled = _dot_nt(dmixed, pw_v)
        qd = dpooled * icnt
        dext = jnp.concatenate([qd, h_dpl[...]], axis=0)
        dp_ref[:, C_PU:C_PU + 256] = (_winsum_up(dext, lane)[:TB] - dpooled).astype(BF16)
        h_dpl[...] = qd[0:16, :]
        cw = [prm_ref[R_SCW + j:R_SCW + j + 1, :] for j in range(4)]
        xc = xc_ref[...].astype(F32)
        xs, bm, cm = xc[:, 0:256], xc[:, 256:512], xc[:, 512:768]

        gw_v = gw_ref[...]
        tail, pre, dtin, dte, dec, kd, wdec, w, xw, d_s, et, ut_g, ut_s = _mixer_tile_prep(p_ref, t_ref, xc, prm_ref,
                                                                                          gw_v, cm_ref, mk_ref)
        gmean = cm_ref[2]
        mask_t = mk_ref[...]
        gnw = prm_ref[R_GNW:R_GNW + 1, 0:256]
        a_e = prm_ref[R_AE:R_AE + 1, 0:256]
        d_e = prm_ref[R_DE:R_DE + 1, 0:256]
        snw = prm_ref[R_SNW:R_SNW + 1, 0:256]
        sg_in = [sg_ref[c] for c in range(NCH)]
        ss_in = [ss_ref[c] for c in range(NCH)]
        sg_n = [sg_in[c] * d_s[c] + ut_g[c] for c in range(NCH)]
        ss_n = [ss_in[c] * et[c] + ut_s[c] for c in range(NCH)]
        qs = _chunks(p_ref[:, C_GQ:C_GQ + 128].astype(F32) * GLA_SCALE)
        cm_c, bm_c, xw_c, kd_c = _chunks(cm), _chunks(bm), _chunks(xw), _chunks(kd)
        v_c = _chunks(p_ref[:, C_GV:C_GV + 256].astype(F32))
        o = jnp.concatenate([_dot_nt(qs[c], sg_n[c]) for c in range(NCH)], axis=0)
        y = jnp.concatenate([_halves(_dot, cm_c[c], ss_n[c]) for c in range(NCH)], axis=0) + d_e * xs
        gz = p_ref[:, C_GZ:C_GZ + 256].astype(F32)
        r = lax.rsqrt(_dot2_l(o * o, gmean) + EPS)
        on = o * r
        dyb = dm_ref[:, 256:512]
        sz, dsz = _silu_pair(gz)
        dp_ref[:, C_GZ:C_GZ + 256] = (dyb * on * gnw * dsz).astype(BF16)
        tg = dyb * sz
        gsm_ref[R_GNW:R_GNW + 1, 0:256] += _cs(tg * on)
        don = tg * gnw
        do_c = _chunks(r * (don - on * _dot2_l(don * on, gmean)))
        ssz = p_ref[:, C_SZ:C_SZ + 256].astype(F32)
        sil, dsil = _silu_pair(ssz)
        y2 = y * sil
        r = lax.rsqrt(jnp.mean(y2 * y2, axis=-1, keepdims=True) + EPS)
        yn = y2 * r
        dyd = dm_ref[:, 768:1024]
        gsm_ref[R_SNW:R_SNW + 1, 0:256] += _cs(dyd * yn)
        dn = dyd * snw
        dy2 = r * (dn - yn * jnp.mean(dn * yn, axis=-1, keepdims=True))
        dp_ref[:, C_SZ:C_SZ + 256] = (dy2 * y * dsil).astype(BF16)
        dy = dy2 * sil
        gsm_ref[R_DE:R_DE + 1, 0:256] += _cs(dy * xs)
        dy_c = _chunks(dy)
        dq = jnp.concatenate([_dot(do_c[c], sg_n[c]) for c in range(NCH)], axis=0)
        dp_ref[:, C_GQ:C_GQ + 128] = (dq * GLA_SCALE).astype(BF16)
        dcm = jnp.concatenate([_halves(_dot_nt, dy_c[c], ss_n[c]) for c in range(NCH)], axis=0)
        gg = [_dot_tn(do_c[c], qs[c]) * mask_t for c in range(NCH)]
        gs = [_halves(_dot_tn, cm_c[c], dy_c[c]) for c in range(NCH)]
        car_g, car_s = gg_s[...], gs_s[...]
        for c in reversed(range(NCH)):
            gg[c] = gg[c] + car_g
            gs[c] = gs[c] + car_s
            car_g = gg[c] * d_s[c]
            car_s = gs[c] * et[c]
        gg_s[...] = car_g
        gs_s[...] = car_s
        dkd = jnp.concatenate([_dot(v_c[c], gg[c]) for c in range(NCH)], axis=0)
        dp_ref[:, C_GV:C_GV + 256] = jnp.concatenate([_dot_nt(kd_c[c], gg[c]) for c in range(NCH)], axis=0).astype(BF16)
        dp_ref[:, C_GK:C_GK + 128] = (dkd * dec).astype(BF16)
        dbm = jnp.concatenate([_halves(_dot_nt, xw_c[c], gs[c]) for c in range(NCH)], axis=0)
        dxw = jnp.concatenate([_halves(_dot, bm_c[c], gs[c]) for c in range(NCH)], axis=0)
        dxs = dy * d_e + dxw * w
        dw = dxw * xs
        dsuf = _chunk_sums(cm_ref[1], jnp.concatenate([dkd * kd, dw * dte * wdec], axis=1))
        tot_g = jnp.concatenate([jnp.broadcast_to(_cs(gg[c] * sg_in[c]) * d_s[c], (CH, 128)) for c in range(NCH)], axis=0)
        tot_s = jnp.concatenate([jnp.broadcast_to(_cs(gs[c] * ss_in[c]) * et[c], (CH, 256)) for c in range(NCH)], axis=0)
        dpre = (dsuf[:, 0:128] + tot_g) * INV_TAU * jax.nn.sigmoid(-pre)
        dgw_ref[...] += _dot_tn(tail, dpre)
        gsm_ref[R_GB:R_GB + 1, 0:128] += _cs(dpre)
        dda = dsuf[:, 128:384] + tot_s
        gsm_ref[R_AE:R_AE + 1, 0:256] += _cs(dda * dte)
        dtail_s = _dot2_nt(dw * wdec + dda * a_e, cm_ref[3, 0:128, :]) * jax.nn.sigmoid(dtin)
        gsm_ref[R_DTB:R_DTB + 1, 0:128] += _cs(dtail_s)
        dp_ref[:, C_TL:C_TL + 128] = (_dot_nt(dpre, gw_v) + dtail_s).astype(BF16)
        dpre_c = jnp.concatenate([dxs, dbm, dcm], axis=1) * dxc_ref[...].astype(F32)
        dext = jnp.concatenate([dpre_c, h_dpre[...]], axis=0)
        ups = [dpre_c, _up(dext, 1, TB), _up(dext, 2, TB), _up(dext, 3, TB)]
        dp_ref[:, C_SX:C_SX + 768] = (cw[3] * ups[0] + cw[2] * ups[1] + cw[1] * ups[2] + cw[0] * ups[3]).astype(BF16)
        sx = p_ref[:, C_SX:C_SX + 768].astype(F32)
        for k in range(4):
            gsm_ref[R_SCW + k:R_SCW + k + 1, :] += _cs(sx * ups[3 - k])
        gsm_ref[R_SCB:R_SCB + 1, :] += _cs(dpre_c)
        h_dpre[...] = dpre_c[0:8, :]

        @pl.when(i == nt - 1)
        def _():
            ri, ci = _iota((256, 256), 0), _iota((256, 256), 1)
            per_head = jnp.where((ri >> 6) == ci, 1.0, 0.0).astype(BF16)
            per_dv = jnp.where((ri & 63) == ci, 1.0, 0.0).astype(BF16)
            row = _iota((8, 256), 0)
            top = gsm_ref[0:8, 0:256]
            sgc_ref[0:8, 0:256] = jnp.where(row == R_GNW, _dot3_l(top, per_dv), top)
            bot = gsm_ref[8:16, 0:256]
            fold = _dot3_l(jnp.where(row == R_AE - 8, bot * a_e, bot), per_head)
            sgc_ref[8:16, 0:256] = jnp.where((row == R_AE - 8) | (row == R_DE - 8), fold, bot)
            sgc_ref[0:16, 256:768] = gsm_ref[:, 256:768]
            sgc_ref[0:16, 768:896] = dgw_ref[0:16, :]
            sgc_ref[0:16, 896:1024] = jnp.zeros((16, 128), F32)
            diag = _pool_lane_select(lane, dpw_ref[0:64, :], dpw_ref[64:128, :], dpw_ref[128:192, :], dpw_ref[192:256, :])
            for q in range(4):
                sgc_ref[16:32, 256 * q:256 * q + 256] = diag[16 * q:16 * q + 16, :]

    return _call(
        body, (proj, tail, dxn, wot, mix, sg, ss, xc16, dxc16, cv16, pool16, prm, gw, pw, cmat, mask), grid=(nt,), name=name,
        sem=("arbitrary",), rider=rider,
        in_specs=[pl.BlockSpec((TB, NPM), lambda i: (rev(i), 0)),
                  pl.BlockSpec((TB, NP - NPM), lambda i: (rev(i), 0)),
                  pl.BlockSpec((TB, D), lambda i: (rev(i), 0)), pl.BlockSpec((D, D), lambda i: (0, 0)),
                  pl.BlockSpec((TB, D), lambda i: (rev(i), 0)),
                  pl.BlockSpec((NCH, 256, 128), lambda i: (rev(i), 0, 0)),
                  pl.BlockSpec((NCH, 128, 256), lambda i: (rev(i), 0, 0)),
                  pl.BlockSpec((TB, 768), lambda i: (rev(i), 0)), pl.BlockSpec((TB, 768), lambda i: (rev(i), 0)),
                  pl.BlockSpec((TB, 256), lambda i: (rev(i), 0)), pl.BlockSpec((TB, 256), lambda i: (rev(i), 0)),
                  pl.BlockSpec((16, 768), lambda i: (0, 0)), pl.BlockSpec((128, 128), lambda i: (0, 0)),
                  pl.BlockSpec((256, 256), lambda i: (0, 0)), pl.BlockSpec((4, 256, 256), lambda i: (0, 0, 0)),
                  pl.BlockSpec((256, 128), lambda i: (0, 0))],
        out_specs=[pl.BlockSpec((TB, NP), lambda i: (rev(i), 0)), pl.BlockSpec((32, 1024), lambda i: (0, 0)),
                   pl.BlockSpec((D, D), lambda i: (0, 0))],
        out_shape=[jax.ShapeDtypeStruct((t, NP), BF16), jax.ShapeDtypeStruct((32, 1024), F32),
                   jax.ShapeDtypeStruct((D, D), F32)],
        scratch_shapes=[pltpu.VMEM((256, 128), F32), pltpu.VMEM((128, 256), F32), pltpu.VMEM((8, 256), F32),
                        pltpu.VMEM((16, 256), F32), pltpu.VMEM((8, 768), F32), pltpu.VMEM((16, 768), F32),
                        pltpu.VMEM((128, 128), F32), pltpu.VMEM((256, 256), F32), pltpu.VMEM((TB, D), F32)])


SHARD = NPROJ // 4
SHARD_PAD = 896


def _ranges_to_perm(o, n):
    out, p = [], 0
    for start, size in _PERM:
        a, b = max(o, start), min(o + n, start + size)
        if a < b:
            out.append((a, b - a, p + a - start))
        p += size
    return out


def _ranges_to_orig(p0, n):
    out, p = [], 0
    for start, size in _PERM:
        a, b = max(p0, p), min(p0 + n, p + size)
        if a < b:
            out.append((a, b - a, start + a - p))
        p += size
    return out


def _lane_window(load, lo, n, d, lane):
    a = 128 * (lo // 128)
    off = lo - a
    w = 128 if off + n <= 128 else 256
    chunk = load(a, w)
    shift = (d - off) % w
    if shift:
        chunk = pltpu.roll(chunk, shift, axis=1)
    return jnp.where((lane >= d) & (lane < d + n), chunk[:, 0:128], 0.0)


def _assemble_w_in(slabs, name, rb=256):
    def body(s_ref, wp_ref, wpt_ref):
        lane = _iota((1, 128), 1)
        for b in range(NP // 128):
            acc = jnp.zeros((rb, 128), F32)
            for p, n, o in _ranges_to_orig(128 * b, 128):
                while n > 0:
                    s, lo = o // SHARD, o % SHARD
                    cnt = min(n, SHARD - lo)
                    acc = acc + _lane_window(lambda a, w, s=s: s_ref[s, :, a:a + w].astype(F32), lo, cnt, p - 128 * b, lane)
                    o, p, n = o + cnt, p + cnt, n - cnt
            wp_ref[:, 128 * b:128 * b + 128] = acc.astype(BF16)
            wpt_ref[128 * b:128 * b + 128, :] = acc.T.astype(BF16)

    return pl.pallas_call(
        body, grid=(D // rb,), name=name,
        in_specs=[pl.BlockSpec((4, rb, SHARD_PAD), lambda i: (0, i, 0))],
        out_specs=[pl.BlockSpec((rb, NP), lambda i: (i, 0)), pl.BlockSpec((NP, rb), lambda i: (0, i))],
        out_shape=[jax.ShapeDtypeStruct((D, NP), BF16), jax.ShapeDtypeStruct((NP, D), BF16)],
        compiler_params=_cparams(("parallel",)))(slabs)


def _split_dw_in(dwp, name, rb=256):
    rows = dwp.shape[0]

    def body(g_ref, o_ref):
        lane = _iota((1, 128), 1)
        for s in range(4):
            for k in range(SHARD_PAD // 128):
                acc = jnp.zeros((rb, 128), F32)
                n_valid = min(128, SHARD - 128 * k)
                for o, n, p in _ranges_to_perm(SHARD * s + 128 * k, n_valid):
                    acc = acc + _lane_window(lambda a, w: g_ref[:, a:a + w].astype(F32), p, n, o - SHARD * s - 128 * k, lane)
                o_ref[s, :, 128 * k:128 * k + 128] = acc.astype(o_ref.dtype)

    return pl.pallas_call(
        body, grid=(rows // rb,), name=name,
        in_specs=[pl.BlockSpec((rb, NP), lambda i: (i, 0))],
        out_specs=pl.BlockSpec((4, rb, SHARD_PAD), lambda i: (0, i, 0)),
        out_shape=jax.ShapeDtypeStruct((4, rows, SHARD_PAD), dwp.dtype),
        compiler_params=_cparams(("parallel",)))(dwp)


def _half(c, n):
    return pl.ds(pl.multiple_of(c * (n // 2), n // 2), n // 2)


def _other_chips(x, y):
    return ((1 - x, y), (x, 1 - y), (1 - x, 1 - y))


def _remote(src, dst, send, recv, k, dev):
    return pltpu.make_async_remote_copy(src_ref=src, dst_ref=dst, send_sem=send.at[k], recv_sem=recv.at[k], device_id=dev,
                                        device_id_type=MESH)


def _sem(n):
    return pltpu.SemaphoreType.DMA((n,))


def _rider_gather_ici(shards):
    shards = tuple(shards)
    n = len(shards)

    def copies(rins, routs, sems, arrivals=True):
        send, recv = sems
        x, y, c = _place()
        me = 2 * x + y
        out, inc = [], []
        for j, (px, py) in enumerate(_other_chips(x, y)):
            for k in range(n):
                rows = _half(c, shards[k].shape[0])
                out.append(_remote(rins[k].at[rows], routs[k].at[me, rows], send, recv, n * j + k, (px, py, c)))
                if arrivals:
                    inc.append(_remote(rins[k].at[rows], routs[k].at[2 * px + py, rows], send, recv, n * j + k, (px, py, c)))
        return out, inc

    def start(rins, routs, sems):
        for cp in copies(rins, routs, sems, arrivals=False)[0]:
            cp.start()

    def finish(rins, routs, sems):
        out, inc = copies(rins, routs, sems)
        for cp in inc:
            cp.wait_recv()
        for cp in out:
            cp.wait_send()

    return _Rider(shards, [jax.ShapeDtypeStruct((4,) + a.shape, a.dtype) for a in shards], [_sem(3 * n), _sem(3 * n)],
                  start, finish)


def _gather_ici_two_hops(shards, extra):
    shards = tuple(shards)
    n = len(shards)

    def body(*refs):
        ins, e_in, outs, e_out = refs[:n], refs[n], refs[n + 1:2 * n + 1], refs[2 * n + 1]
        send, recv = refs[2 * n + 2:]
        x, y, c = _place()
        slab = lambda px, py: 2 * px + py
        xn, yn, dg = (1 - x, y), (x, 1 - y), (1 - x, 1 - y)

        def part(k, q):
            r = shards[k].shape[0] // 4
            return pl.ds(pl.multiple_of(c * 2 * r + q * r, r), r)

        def hop(k, q, src_chip, to, sem):
            rows = part(k, q)
            src = ins[k].at[rows] if src_chip is None else outs[k].at[slab(*src_chip), rows]
            own = (x, y) if src_chip is None else src_chip
            return _remote(src, outs[k].at[slab(*own), rows], send, recv, sem, (*to, c))

        small = [_remote(e_in, e_out.at[slab(x, y)], send, recv, 6 * n + j, (*to, c)) for j, to in enumerate((xn, yn, dg))]
        first = [hop(k, q, None, (xn, yn)[q], 2 * k + q) for k in range(n) for q in (0, 1)]
        for cp in small + first:
            cp.start()
        for k in range(n):
            for q in (0, 1):
                nb = (xn, yn)[q]
                _remote(ins[k].at[part(k, q)], outs[k].at[slab(*nb), part(k, q)], send, recv, 2 * k + q, (*nb, c)).wait_recv()
        second = []
        for k in range(n):
            for q in (0, 1):
                to, via = (yn, xn)[q], (xn, yn)[q]
                second.append(hop(k, q, None, to, 2 * n + 4 * k + 2 * q))
                second.append(hop(k, q, via, to, 2 * n + 4 * k + 2 * q + 1))
        for cp in second:
            cp.start()
        for k in range(n):
            for q in (0, 1):
                frm, rows = (yn, xn)[q], part(k, q)
                for j, origin in enumerate((frm, dg)):
                    _remote(ins[k].at[rows], outs[k].at[slab(*origin), rows], send, recv, 2 * n + 4 * k + 2 * q + j,
                            (*frm, c)).wait_recv()
        for j, frm in enumerate((xn, yn, dg)):
            _remote(e_in, e_out.at[slab(*frm)], send, recv, 6 * n + j, (*frm, c)).wait_recv()
        third, theirs = [], []
        for k in range(n):
            rows = shards[k].shape[0]
            for j, chip in enumerate((xn, yn, dg)):
                got, missing = outs[k].at[slab(*chip), _half(c, rows)], outs[k].at[slab(*chip), _half(1 - c, rows)]
                third.append(_remote(got, got, send, recv, 6 * n + 3 + 3 * k + j, (x, y, 1 - c)))
                theirs.append(_remote(missing, missing, send, recv, 6 * n + 3 + 3 * k + j, (x, y, 1 - c)))
        for cp in third:
            cp.start()
        for cp in theirs:
            cp.wait_recv()
        for cp in small + first + second + third:
            cp.wait_send()

    outs = pl.pallas_call(
        body, name="gather0", in_specs=[_ANY] * (n + 1), out_specs=[_ANY] * (n + 1),
        out_shape=[jax.ShapeDtypeStruct((4,) + a.shape, a.dtype) for a in shards + (extra,)],
        scratch_shapes=[_sem(9 * n + 3), _sem(9 * n + 3)])(*shards, extra)
    return list(outs)


def _rider_gather_d2d(slabs):
    slabs = tuple(slabs)
    n = len(slabs)

    def copies(routs, sems, arrivals=True):
        send, recv = sems
        x, y, c = _place()
        out, inc = [], []
        for j, (px, py) in enumerate(_other_chips(x, y)):
            for k in range(n):
                rows = slabs[k].shape[1]
                mine, theirs = routs[k].at[2 * px + py, _half(c, rows)], routs[k].at[2 * px + py, _half(1 - c, rows)]
                out.append(_remote(mine, mine, send, recv, n * j + k, (x, y, 1 - c)))
                if arrivals:
                    inc.append(_remote(theirs, theirs, send, recv, n * j + k, (x, y, 1 - c)))
        return out, inc

    def start(rins, routs, sems):
        for cp in copies(routs, sems, arrivals=False)[0]:
            cp.start()

    def finish(rins, routs, sems):
        out, inc = copies(routs, sems)
        for cp in inc:
            cp.wait_recv()
        for cp in out:
            cp.wait_send()

    return _Rider(slabs, [jax.ShapeDtypeStruct(a.shape, a.dtype) for a in slabs], [_sem(3 * n), _sem(3 * n)], start, finish,
                  aliases={k: k for k in range(n)})


def _rider_swap(parts):
    parts = tuple(parts)
    n = len(parts)

    def copies(rins, routs, sems):
        send, recv = sems
        x, y, c = _place()
        return [_remote(rins[k].at[:, _half(1 - c, parts[k].shape[1])], routs[k], send, recv, k, (x, y, 1 - c))
                for k in range(n)]

    def start(rins, routs, sems):
        for cp in copies(rins, routs, sems):
            cp.start()

    def finish(rins, routs, sems):
        for cp in copies(rins, routs, sems):
            cp.wait()

    return _Rider(parts, [jax.ShapeDtypeStruct((a.shape[0], a.shape[1] // 2, a.shape[2]), a.dtype) for a in parts],
                  [_sem(n), _sem(n)], start, finish)


def _rider_scatter(parts):
    parts = tuple(parts)
    n = len(parts)

    def copies(rins, routs, sems, arrivals=True):
        send, recv = sems
        x, y, c = _place()
        me = 2 * x + y
        out, inc = [], []
        for j, (px, py) in enumerate(_other_chips(x, y)):
            for k in range(n):
                out.append(_remote(rins[k].at[2 * px + py], routs[k].at[me], send, recv, n * j + k, (px, py, c)))
                if arrivals:
                    inc.append(_remote(rins[k].at[me], routs[k].at[2 * px + py], send, recv, n * j + k, (px, py, c)))
        return out, inc

    def start(rins, routs, sems):
        for cp in copies(rins, routs, sems, arrivals=False)[0]:
            cp.start()

    def finish(rins, routs, sems):
        out, inc = copies(rins, routs, sems)
        for cp in inc:
            cp.wait_recv()
        for cp in out:
            cp.wait_send()

    return _Rider(parts, [jax.ShapeDtypeStruct(a.shape, a.dtype) for a in parts], [_sem(3 * n), _sem(3 * n)], start, finish)


def _rider_share(fulls):
    fulls = tuple(fulls)
    n = len(fulls)

    def copies(routs, sems, arrivals=True):
        send, recv = sems
        x, y, c = _place()
        out, inc = [], []
        for k in range(n):
            mine, theirs = routs[k].at[_half(c, fulls[k].shape[0])], routs[k].at[_half(1 - c, fulls[k].shape[0])]
            out.append(_remote(mine, mine, send, recv, k, (x, y, 1 - c)))
            if arrivals:
                inc.append(_remote(theirs, theirs, send, recv, k, (x, y, 1 - c)))
        return out, inc

    def start(rins, routs, sems):
        for cp in copies(routs, sems, arrivals=False)[0]:
            cp.start()

    def finish(rins, routs, sems):
        out, inc = copies(routs, sems)
        for cp in inc:
            cp.wait_recv()
        for cp in out:
            cp.wait_send()

    return _Rider(fulls, [jax.ShapeDtypeStruct(a.shape, a.dtype) for a in fulls], [_sem(n), _sem(n)], start, finish,
                  aliases={k: k for k in range(n)})


def _pair_sum(core, full, recv, name, br=128):
    n, rows, cols = recv.shape

    def body(c_ref, a_ref, b_ref, o_ref):
        o_ref[...] = (a_ref[...] + b_ref[...]).astype(BF16)

    nb = rows // br
    return pl.pallas_call(
        body, name=name, out_shape=jax.ShapeDtypeStruct(recv.shape, BF16),
        grid_spec=pltpu.PrefetchScalarGridSpec(
            num_scalar_prefetch=1, grid=(n, nb),
            in_specs=[pl.BlockSpec((1, br, cols), lambda i, j, c: (i, c[0] * nb + j, 0)),
                      pl.BlockSpec((1, br, cols), lambda i, j, c: (i, j, 0))],
            out_specs=pl.BlockSpec((1, br, cols), lambda i, j, c: (i, j, 0))),
        compiler_params=_cparams(("parallel", "parallel")))(core, full, recv)


def _chip_sum(place, gathered, mine, name, br=128):
    _, r, c = gathered.shape
    nb = r // br

    def body(p_ref, g_ref, m_ref, o_ref):
        slab = lambda j: jnp.where(p_ref[1] == j, m_ref[j], g_ref[j]).astype(F32)
        o_ref[...] = ((slab(0) + slab(1)) + slab(2)) + slab(3)

    return pl.pallas_call(
        body, name=name, out_shape=jax.ShapeDtypeStruct((2 * r, c), F32),
        grid_spec=pltpu.PrefetchScalarGridSpec(
            num_scalar_prefetch=1, grid=(nb,),
            in_specs=[pl.BlockSpec((4, br, c), lambda i, p: (0, i, 0)), pl.BlockSpec((4, br, c), lambda i, p: (0, i, 0))],
            out_specs=pl.BlockSpec((br, c), lambda i, p: (p[0] * nb + i, 0))),
        compiler_params=_cparams(("parallel",)))(place, gathered, mine)


def _adamw(w, g, m, v, name, br):
    n, r, c = w.shape

    def body(w_ref, g_ref, m_ref, v_ref, d_ref, m2_ref, v2_ref):
        d_ref[...], m2_ref[...], v2_ref[...] = _adam_math(w_ref[...], g_ref[...], m_ref[...], v_ref[...])

    spec = pl.BlockSpec((1, br, c), lambda i, j: (i, j, 0))
    shp = jax.ShapeDtypeStruct(w.shape, F32)
    return pl.pallas_call(body, grid=(n, r // br), name=name, in_specs=[spec] * 4, out_specs=[spec] * 3,
                          out_shape=[shp] * 3, compiler_params=_cparams(("parallel", "parallel")))(w, g, m, v)


def _adamw_w_in(w, g, m, v, name, bc=93):
    cols = w.shape[2]
    lead = lambda a: jnp.transpose(a, (2, 0, 1))
    g = jnp.stack([a[:, 0:cols] for a in g])

    def body(w_ref, g_ref, m_ref, v_ref, go_ref, d_ref, m2_ref, v2_ref):
        for l in range(2):
            gv = g_ref[:, l, :]
            d_ref[:, l, :], m2_ref[:, l, :], v2_ref[:, l, :] = _adam_math(w_ref[:, l, :], gv, m_ref[:, l, :], v_ref[:, l, :])
            go_ref[:, l, :] = gv

    spec = pl.BlockSpec((bc, 2, D), lambda i: (i, 0, 0))
    outs = pl.pallas_call(body, grid=(cols // bc,), name=name, in_specs=[spec] * 4, out_specs=[spec] * 4,
                          out_shape=[jax.ShapeDtypeStruct((cols, 2, D), F32)] * 4,
                          compiler_params=_cparams(("parallel",)))(lead(w), lead(g), lead(m), lead(v))
    return [jnp.transpose(o, (1, 2, 0)) for o in outs]


_SMALL_NAMES = ("norm_w", "conv_a_w", "gla_gate_w", "gla_gate_b", "gla_norm_w", "pool_w", "pool_scale", "ssd_conv_w",
                "ssd_conv_b", "ssd_dt_bias", "ssd_a_log", "ssd_d", "ssd_norm_w", "final_norm_w")
SMALL_ROWS = 80


def _adam_math(w, g, m, v):
    m2 = ADAM_B1 * m + (1.0 - ADAM_B1) * g
    v2 = ADAM_B2 * v + (1.0 - ADAM_B2) * (g * g)
    m_hat = m2 / (1.0 - ADAM_B1 ** ADAM_STEP)
    v_hat = v2 / (1.0 - ADAM_B2 ** ADAM_STEP)
    return -ADAM_LR * (m_hat / (jnp.sqrt(v_hat) + ADAM_EPS) + ADAM_WD * w), m2, v2


def _small_slices(name, chip):
    if name == "conv_a_w":
        return [((), slice(R_CAW, R_CAW + 3), slice(64 * chip, 64 * chip + 64))]
    if name == "ssd_conv_w":
        return [((), slice(R_SCW, R_SCW + 4), slice(192 * chip, 192 * chip + 192))]
    if name == "gla_gate_w":
        return [((), slice(0, 16), slice(768, 896))]
    if name == "pool_w":
        return [((g, slice(16 * q, 16 * q + 16)), slice(16, 32), slice(256 * q + 64 * g, 256 * q + 64 * g + 64))
                for g in range(4) for q in range(4)]
    row, lanes = {"gla_gate_b": (R_GB, slice(0, 128)), "gla_norm_w": (R_GNW, slice(0, 64)),
                  "pool_scale": (R_PSC, slice(0, 256)), "ssd_conv_b": (R_SCB, slice(0, 768)),
                  "ssd_dt_bias": (R_DTB, slice(16, 20)), "ssd_a_log": (R_AE, slice(0, 4)), "ssd_d": (R_DE, slice(0, 4)),
                  "ssd_norm_w": (R_SNW, slice(0, 256))}[name]
    return [((), slice(row, row + 1), lanes)]


def _rider_exchange(block):
    def copies(rins, routs, sems):
        send, recv = sems
        x, y, c = _place()
        flip = lambda v, bit: 1 - v if bit else v
        return [_remote(rins[0], routs[0].at[k], send, recv, k - 1, (flip(x, k & 4), flip(y, k & 2), flip(c, k & 1)))
                for k in range(1, 8)]

    def start(rins, routs, sems):
        for cp in copies(rins, routs, sems):
            cp.start()

    def finish(rins, routs, sems):
        for cp in copies(rins, routs, sems):
            cp.wait()

    return _Rider((block,), [jax.ShapeDtypeStruct((8,) + block.shape, block.dtype)], [_sem(7), _sem(7)], start, finish)


def _join_riders(a, b):
    na, oa, sa = len(a.inputs), len(a.out_shapes), len(a.sems)

    def start(rins, routs, sems):
        a.start(rins[:na], routs[:oa], sems[:sa])
        b.start(rins[na:], routs[oa:], sems[sa:])

    def finish(rins, routs, sems):
        a.finish(rins[:na], routs[:oa], sems[:sa])
        b.finish(rins[na:], routs[oa:], sems[sa:])

    aliases = {**a.aliases, **{na + k: oa + v for k, v in b.aliases.items()}}
    return _Rider(a.inputs + b.inputs, a.out_shapes + b.out_shapes, a.sems + b.sems, start, finish, aliases)


def _small_adamw(blocks, w, m, v):
    n = len(_SMALL_NAMES)

    def body(*refs):
        (own, ex), (own0, ex0) = refs[0:2], refs[2:4]
        refs = refs[3:]
        w_refs, m_refs, v_refs = refs[1:1 + n], refs[1 + n:1 + 2 * n], refs[1 + 2 * n:1 + 3 * n]
        o = 1 + 3 * n
        g_out, d_out, m_out, v_out = refs[o:o + n], refs[o + n:o + 2 * n], refs[o + 2 * n:o + 3 * n], refs[o + 3 * n:o + 4 * n]
        loss_ref, acc, acc0 = refs[o + 4 * n:o + 4 * n + 3]
        chip = 2 * lax.axis_index("x") + lax.axis_index("y")
        me = 2 * chip + lax.axis_index("c")
        acc[...] = jnp.zeros_like(acc)
        acc0[...] = jnp.zeros_like(acc0)
        for src in range(8):
            @pl.when(me == src)
            def _():
                acc[...] += own[...]
                acc0[...] += own0[...]

            @pl.when(me != src)
            def _(src=src):
                acc[...] += ex[jnp.bitwise_xor(me, src)]
                acc0[...] += ex0[jnp.bitwise_xor(me, src)]

        loss_ref[...] = acc[73:74, 0:1]

        def update(i, idx, g):
            d, m2, v2 = _adam_math(w_refs[i][idx], g, m_refs[i][idx], v_refs[i][idx])
            g_out[i][idx], d_out[i][idx], m_out[i][idx], v_out[i][idx] = g, d, m2, v2

        for i, name in enumerate(_SMALL_NAMES):
            if name == "final_norm_w":
                update(i, (slice(0, 1), slice(None)), acc[72:73, :])
            elif name == "norm_w":
                update(i, (slice(0, 1), slice(None)), acc0[0:1, :])
                update(i, (slice(1, 2), slice(None)), acc[64:65, :])
            elif name in ("conv_a_w", "ssd_conv_w"):
                for s in range(4):
                    @pl.when(chip == s)
                    def _(i=i, name=name, s=s):
                        for l in range(2):
                            (_, rows, lanes), = _small_slices(name, s)
                            update(i, (l,), acc[rows.start + 32 * l:rows.stop + 32 * l, lanes])
            else:
                for l in range(2):
                    for idx, rows, lanes in _small_slices(name, 0):
                        g = acc[rows.start + 32 * l:rows.stop + 32 * l, lanes]
                        if w_refs[i].ndim == 2:
                            update(i, (slice(l, l + 1), slice(None)), g)
                        else:
                            update(i, (l,) + idx, g)

    args = [a for pair in blocks for a in pair] + [d[k] for d in (w, m, v) for k in _SMALL_NAMES]
    shapes = [jax.ShapeDtypeStruct(w[k].shape, F32) for k in _SMALL_NAMES]
    vmem = pl.BlockSpec(memory_space=pltpu.VMEM)
    outs = pl.pallas_call(body, name="small_adamw", in_specs=[vmem] * len(args), out_specs=[vmem] * (4 * n + 1),
                          out_shape=shapes * 4 + [jax.ShapeDtypeStruct((1, 1), F32)],
                          scratch_shapes=[pltpu.VMEM((SMALL_ROWS, D), F32), pltpu.VMEM((8, D), F32)])(*args)
    return outs[0:n], outs[n:2 * n], outs[2 * n:3 * n], outs[3 * n:4 * n], outs[4 * n]


def _mixer_consts(layer, conv_a_w, gla_gate_w, gla_gate_b, gla_norm_w, pool_w, pool_scale, ssd_conv_w, ssd_conv_b,
                  ssd_dt_bias, ssd_a_log, ssd_d, ssd_norm_w):
    def row(v):
        return jnp.pad(v.reshape(1, -1), ((0, 0), (0, 768 - v.size)))

    dtb = jnp.pad(ssd_dt_bias[layer], (16, 108))
    rows = [jnp.pad(conv_a_w[layer], ((0, 0), (0, 512))), row(gla_gate_b[layer]), row(jnp.tile(gla_norm_w[layer], 4)),
            row(pool_scale[layer]), row(ssd_conv_b[layer]), row(dtb), row(jnp.repeat(-jnp.exp(ssd_a_log[layer]), 64)),
            row(jnp.repeat(ssd_d[layer], 64)), row(ssd_norm_w[layer]), jnp.zeros((1, 768), F32), ssd_conv_w[layer]]
    prm = jnp.concatenate(rows, axis=0)
    gw = jnp.pad(gla_gate_w[layer], ((0, 112), (0, 0))).astype(BF16)
    on_diag = (_iota((256, 256), 0) >> 6) == (_iota((256, 256), 1) >> 6)
    pw = jnp.where(on_diag, jnp.tile(pool_w[layer].reshape(256, 64), (1, 4)), 0.0)
    return (prm, gw, pw.astype(BF16)) + _mixer_matrices()


def _grad_slabs(dwp, dwo):
    return dwp.reshape(1, D, NP), dwo.reshape(4, D // 4, D)


class _Comm:
    def __init__(self, w_in, w_out):
        self.w_in16 = jnp.pad(w_in.astype(BF16), ((0, 0), (0, 0), (0, SHARD_PAD - SHARD)))
        self.w_out16 = w_out.astype(BF16)
        self.core = lax.axis_index("c").astype(jnp.int32).reshape(1)
        self.chip = 2 * lax.axis_index("x") + lax.axis_index("y")
        self.place = jnp.stack([lax.axis_index("c"), self.chip]).astype(jnp.int32)

    def gather_ici(self, layer):
        return _rider_gather_ici((self.w_in16[layer], self.w_out16[layer]))

    def pair_sum(self, layer, slabs, received):
        d_in, d_out = [_pair_sum(self.core, a, b, name=f"reduce_pair_sum{layer}_{k}")
                       for k, (a, b) in enumerate(zip(slabs, received))]
        return [_split_dw_in(d_in[0], name=f"split_dw_in{layer}"), d_out]

    def chip_sum(self, layer, gathered, mine):
        return [_chip_sum(self.place, a, b, name=f"reduce_chip_sum{layer}_{k}") for k, (a, b) in enumerate(zip(gathered, mine))]

    def layer_weights(self, layer, s_in, s_out):
        own = lambda slabs, shard: jnp.stack([jnp.where(self.chip == s, shard, slabs[s]) for s in range(4)])
        wp, wpt = _assemble_w_in(own(s_in, self.w_in16[layer]), name=f"assemble_w_in{layer}")
        wo = own(s_out, self.w_out16[layer]).reshape(D, D)
        return wp, wpt, wo, wo.T


def _local_step(x, tgt, norm_w, final_norm_w, consts, wts0, wts1=None, comm=None):
    nw = [norm_w[l:l + 1] for l in range(2)]
    proj0, h0, slabs = _rmsproj(x, nw[0], wts0[0], name="rmsproj0", rider=comm and comm.gather_ici(1))
    (mix0, sg0, ss0, x1, *conv0), slabs = _mixer_fwd(proj0, x, wts0[2], *consts[0], name="mixer_fwd0",
                                                     rider=comm and _rider_gather_d2d(slabs))
    if comm:
        wts1 = comm.layer_weights(1, *slabs)
    proj1, h1, _ = _rmsproj(x1, nw[1], wts1[0], name="rmsproj1")
    (mix1, sg1, ss1, dx, *conv1, head), _ = _mixer_fwd(proj1, x1, wts1[2], *consts[1], name="mixer_fwd1",
                                                       head=(tgt, final_norm_w.reshape(1, D)))
    (dproj, mgr1, dwo1), _ = _mixer_bwd(proj1, dx, wts1[3], mix1, sg1, ss1, *conv1, *consts[1], name="mixer_bwd1")
    dwp1, _ = _dwin(h1, dproj, name="dwin1")
    slabs1 = comm and _grad_slabs(dwp1, dwo1)
    (dx, dnw1), recv = _dxin(dproj, wts1[1], x1, dx, nw[1], name="dxin1", rider=comm and _rider_swap(slabs1))
    pairs1 = comm and comm.pair_sum(1, slabs1, recv)
    (dproj, mgr0, dwo0), gathered = _mixer_bwd(proj0, dx, wts0[3], mix0, sg0, ss0, *conv0, *consts[0], name="mixer_bwd0",
                                               rider=comm and _rider_scatter(pairs1))
    if not comm:
        dwp0, _ = _dwin(h0, dproj, name="dwin0")
        (dx, dnw0), _ = _dxin(dproj, wts0[1], x, dx, nw[0], name="dxin0")
        return head, dx, ((dwp0, dwp1), (dwo0, dwo1)), (dnw0, dnw1), (mgr0, mgr1)
    dwo0 = dwo0.reshape(4, D // 4, D)
    dwp0, (*big1, recv_out) = _dwin(h0, dproj, name="dwin0", rider=_join_riders(
        _rider_share(comm.chip_sum(1, gathered, pairs1)), _rider_swap((dwo0,))))
    slabs0 = (dwp0.reshape(1, D, NP), dwo0)
    pairs0 = comm.pair_sum(0, slabs0, (_run_rider(_rider_swap(slabs0[0:1]), "reduce_swap0")[0], recv_out))
    small = jnp.concatenate([mgr0, mgr1, dnw1, head], axis=0)
    (dx, dnw0), gathered = _dxin(dproj, wts0[1], x, dx, nw[0], name="dxin0",
                                 rider=_join_riders(_rider_scatter(pairs0), _rider_exchange(small)))
    last = _run_rider(_join_riders(_rider_share(comm.chip_sum(0, gathered[0:2], pairs0)), _rider_exchange(dnw0)),
                      "reduce_share0")
    return dx, ((last[0], big1[0]), (last[1], big1[1])), ((small, gathered[2]), (dnw0, last[2]))


def kernel(x, norm_w, w_in, conv_a_w, gla_gate_w, gla_gate_b, gla_norm_w, pool_w, pool_scale, ssd_conv_w, ssd_conv_b, ssd_dt_bias, ssd_a_log, ssd_d, ssd_norm_w, w_out, final_norm_w, loss_target, m_norm_w, m_w_in, m_conv_a_w, m_gla_gate_w, m_gla_gate_b, m_gla_norm_w, m_pool_w, m_pool_scale, m_ssd_conv_w, m_ssd_conv_b, m_ssd_dt_bias, m_ssd_a_log, m_ssd_d, m_ssd_norm_w, m_w_out, m_final_norm_w, v_norm_w, v_w_in, v_conv_a_w, v_gla_gate_w, v_gla_gate_b, v_gla_norm_w, v_pool_w, v_pool_scale, v_ssd_conv_w, v_ssd_conv_b, v_ssd_dt_bias, v_ssd_a_log, v_ssd_d, v_ssd_norm_w, v_w_out, v_final_norm_w):
    weights = dict(norm_w=norm_w, w_in=w_in, conv_a_w=conv_a_w, gla_gate_w=gla_gate_w, gla_gate_b=gla_gate_b,
                   gla_norm_w=gla_norm_w, pool_w=pool_w, pool_scale=pool_scale, ssd_conv_w=ssd_conv_w,
                   ssd_conv_b=ssd_conv_b, ssd_dt_bias=ssd_dt_bias, ssd_a_log=ssd_a_log, ssd_d=ssd_d,
                   ssd_norm_w=ssd_norm_w, w_out=w_out, final_norm_w=final_norm_w)
    m_in = dict(norm_w=m_norm_w, w_in=m_w_in, conv_a_w=m_conv_a_w, gla_gate_w=m_gla_gate_w, gla_gate_b=m_gla_gate_b,
                gla_norm_w=m_gla_norm_w, pool_w=m_pool_w, pool_scale=m_pool_scale, ssd_conv_w=m_ssd_conv_w,
                ssd_conv_b=m_ssd_conv_b, ssd_dt_bias=m_ssd_dt_bias, ssd_a_log=m_ssd_a_log, ssd_d=m_ssd_d,
                ssd_norm_w=m_ssd_norm_w, w_out=m_w_out, final_norm_w=m_final_norm_w)
    v_in = dict(norm_w=v_norm_w, w_in=v_w_in, conv_a_w=v_conv_a_w, gla_gate_w=v_gla_gate_w, gla_gate_b=v_gla_gate_b,
                gla_norm_w=v_gla_norm_w, pool_w=v_pool_w, pool_scale=v_pool_scale, ssd_conv_w=v_ssd_conv_w,
                ssd_conv_b=v_ssd_conv_b, ssd_dt_bias=v_ssd_dt_bias, ssd_a_log=v_ssd_a_log, ssd_d=v_ssd_d,
                ssd_norm_w=v_ssd_norm_w, w_out=v_w_out, final_norm_w=v_final_norm_w)
    order = ("norm_w", "w_in", "conv_a_w", "gla_gate_w", "gla_gate_b", "gla_norm_w", "pool_w", "pool_scale",
             "ssd_conv_w", "ssd_conv_b", "ssd_dt_bias", "ssd_a_log", "ssd_d", "ssd_norm_w", "w_out", "final_norm_w")
    t = x.shape[1]

    comm = _Comm(w_in, w_out)
    cshard = jnp.zeros((16, 256), F32)
    for l in range(2):
        cshard = cshard.at[8 * l:8 * l + 3, 0:64].set(conv_a_w[l]).at[8 * l + 3:8 * l + 7, 0:192].set(ssd_conv_w[l])
    s_in, s_out, g_c = _gather_ici_two_hops((comm.w_in16[0], comm.w_out16[0]), cshard)
    g_c = [jnp.where(comm.chip == s, cshard, g_c[s]) for s in range(4)]
    conv_a_full = jnp.stack([jnp.concatenate([g_c[s][8 * l:8 * l + 3, 0:64] for s in range(4)], axis=-1) for l in range(2)])
    ssd_conv_full = jnp.stack([jnp.concatenate([g_c[s][8 * l + 3:8 * l + 7, 0:192] for s in range(4)], axis=-1)
                               for l in range(2)])
    consts = [_mixer_consts(l, conv_a_full, gla_gate_w, gla_gate_b, gla_norm_w, pool_w, pool_scale, ssd_conv_full,
                            ssd_conv_b, ssd_dt_bias, ssd_a_log, ssd_d, ssd_norm_w) for l in range(2)]

    dx, big, blocks = _local_step(x.reshape(t, D), loss_target.reshape(t, D), norm_w, final_norm_w, consts,
                                  comm.layer_weights(0, s_in, s_out), comm=comm)

    as2d = lambda d: {k: (d[k].reshape(1, D) if k == "final_norm_w" else d[k]) for k in _SMALL_NAMES}
    small = _small_adamw(blocks, as2d(weights), as2d(m_in), as2d(v_in))
    grads, delta, new_m, new_v = ({k: (a.reshape(D) if k == "final_norm_w" else a) for k, a in zip(_SMALL_NAMES, part)}
                                  for part in small[0:4])
    loss = small[4].reshape(())

    grads["w_out"] = jnp.stack(big[1])

    grads["w_in"], delta["w_in"], new_m["w_in"], new_v["w_in"] = _adamw_w_in(w_in, big[0], m_w_in, v_w_in, name="adamw_w_in")
    delta["w_out"], new_m["w_out"], new_v["w_out"] = _adamw(w_out, grads["w_out"], m_w_out, v_w_out, name="adamw_w_out", br=256)

    return (loss, dx.reshape(1, t, D), *[grads[k] for k in order], *[delta[k] for k in order],
            *[new_m[k] for k in order], *[new_v[k] for k in order])
```

```python
import functools

import jax
import jax.numpy as jnp
from jax import lax
from jax.experimental import pallas as pl
from jax.experimental.pallas import tpu as pltpu

F32 = jnp.float32
BF16 = jnp.bfloat16
MESH = pl.DeviceIdType.MESH

D = 1024
CH = 64
EPS = 1e-6
NP = 3456
NPROJ = 3348
NPM = 3328
GLA_SCALE = 32.0 ** -0.5
INV_TAU = 1.0 / 16.0
TB = 512
NCH = TB // CH
assert TB % 256 == 0

C_AH, C_AB, C_AC, C_AZ, C_GQ, C_GK, C_GV = 0, 256, 512, 768, 1024, 1152, 1280
C_GZ, C_PU, C_PZ, C_SZ, C_SX, C_TL = 1536, 1792, 2048, 2304, 2560, 3328
_PERM = ((0, 1536), (1552, 1792), (1536, 16), (3344, 4))

R_CAW, R_GB, R_GNW, R_PSC, R_SCB, R_DTB, R_AE, R_DE, R_SNW, R_SCW = 0, 3, 4, 5, 6, 7, 8, 9, 10, 12

ADAM_LR, ADAM_B1, ADAM_B2, ADAM_EPS, ADAM_WD, ADAM_STEP = 0.001, 0.9, 0.999, 1e-08, 0.01, 10

VMEM_LIMIT = 56 * 1024 * 1024


def _cparams(sem, limit=VMEM_LIMIT):
    return pltpu.CompilerParams(dimension_semantics=sem, vmem_limit_bytes=limit)


_ANY = pl.BlockSpec(memory_space=pl.ANY)


def _place():
    return lax.axis_index("x"), lax.axis_index("y"), lax.axis_index("c")


class _Rider:
    def __init__(self, inputs, out_shapes, sems, start, finish, aliases=None):
        self.inputs, self.out_shapes, self.sems = tuple(inputs), tuple(out_shapes), tuple(sems)
        self.start, self.finish, self.aliases = start, finish, dict(aliases or {})


def _call(body, args, *, grid, in_specs, out_specs, out_shape, name, sem, scratch_shapes=(), rider=None):
    if rider is None:
        outs = pl.pallas_call(body, grid=grid, name=name, in_specs=list(in_specs), out_specs=list(out_specs),
                              out_shape=list(out_shape), scratch_shapes=list(scratch_shapes),
                              compiler_params=_cparams(sem))(*args)
        return list(outs), []
    ni, no, ns = len(args), len(out_shape), len(scratch_shapes)
    ri, ro = len(rider.inputs), len(rider.out_shapes)

    def full(*refs):
        ins, rins = refs[:ni], refs[ni:ni + ri]
        outs, routs = refs[ni + ri:ni + ri + no], refs[ni + ri + no:ni + ri + no + ro]
        scr, rsem = refs[ni + ri + no + ro:ni + ri + no + ro + ns], refs[ni + ri + no + ro + ns:]
        first = functools.reduce(jnp.logical_and, [pl.program_id(a) == 0 for a in range(len(grid))])
        last = functools.reduce(jnp.logical_and, [pl.program_id(a) == grid[a] - 1 for a in range(len(grid))])

        @pl.when(first)
        def _():
            rider.start(rins, routs, rsem)

        body(*ins, *outs, *scr)

        @pl.when(last)
        def _():
            rider.finish(rins, routs, rsem)

    outs = pl.pallas_call(
        full, grid=grid, name=name, in_specs=list(in_specs) + [_ANY] * ri, out_specs=list(out_specs) + [_ANY] * ro,
        out_shape=list(out_shape) + list(rider.out_shapes), scratch_shapes=list(scratch_shapes) + list(rider.sems),
        input_output_aliases={ni + k: no + v for k, v in rider.aliases.items()},
        compiler_params=_cparams(("arbitrary",) * len(grid)))(*args, *rider.inputs)
    return list(outs[:no]), list(outs[no:])


def _run_rider(rider, name):
    ri = len(rider.inputs)

    def body(*refs):
        rins, routs, rsem = refs[:ri], refs[ri:ri + len(rider.out_shapes)], refs[ri + len(rider.out_shapes):]
        rider.start(rins, routs, rsem)
        rider.finish(rins, routs, rsem)

    return list(pl.pallas_call(body, name=name, in_specs=[_ANY] * ri, out_specs=[_ANY] * len(rider.out_shapes),
                               out_shape=list(rider.out_shapes), scratch_shapes=list(rider.sems),
                               input_output_aliases=dict(rider.aliases))(*rider.inputs))


def _dot(a, b):
    return jnp.dot(a.astype(BF16), b.astype(BF16), preferred_element_type=F32)


def _dot_nt(a, b):
    return lax.dot_general(a.astype(BF16), b.astype(BF16), (((1,), (1,)), ((), ())), preferred_element_type=F32)


def _dot_tn(a, b):
    return lax.dot_general(a.astype(BF16), b.astype(BF16), (((0,), (0,)), ((), ())), preferred_element_type=F32)


def _split(a):
    hi = a.astype(BF16)
    lo = (a - hi.astype(F32)).astype(BF16)
    return hi, lo


def _dot2_l(a, b):
    hi, lo = _split(a)
    return _dot(hi, b) + _dot(lo, b)


def _dot2_r(a, b):
    hi, lo = _split(b)
    return _dot(a, hi) + _dot(a, lo)


def _dot3_l(a, b):
    hi, lo = _split(a)
    lo2 = ((a - hi.astype(F32)) - lo.astype(F32)).astype(BF16)
    return _dot(hi, b) + _dot(lo, b) + _dot(lo2, b)


def _dot2_nt(a, b):
    hi, lo = _split(a)
    return _dot_nt(hi, b) + _dot_nt(lo, b)


def _silu(z):
    return z * jax.nn.sigmoid(z)


def _lse1(x):
    return jnp.log(1.0 + jnp.exp(-jnp.abs(x)))


def _cs(a):
    return jnp.sum(a, axis=0, keepdims=True)


def _iota(shape, dim):
    return lax.broadcasted_iota(jnp.int32, shape, dim)


def _mixer_matrices():
    r, c = _iota((256, 256), 0), _iota((256, 256), 1)
    same_chunk = (r >> 6) == (c >> 6)
    mats = jnp.stack([jnp.where((c > r) & same_chunk, 1.0, 0.0), jnp.where((c < r) & same_chunk, 1.0, 0.0),
                      jnp.where(same_chunk, 1.0 / 64.0, 0.0), jnp.where((r < 128) & (r - 16 == (c >> 6)), 1.0, 0.0)])
    mask = jnp.where((_iota((256, 128), 0) >> 6) == (_iota((256, 128), 1) >> 5), 1.0, 0.0)
    return mats.astype(BF16), mask.astype(F32)


def _dn(ext, k, n, h):
    return pltpu.roll(ext, k, axis=0)[h:h + n]


def _up(ext, k, n):
    return pltpu.roll(ext, ext.shape[0] - k, axis=0)[:n]


def _pool_lane_select(lane, s2, s4, s8, s16):
    return jnp.where(lane < 64, s2, jnp.where(lane < 128, s4, jnp.where(lane < 192, s8, s16)))


def _winsum_dn(ext, lane):
    s2 = ext + pltpu.roll(ext, 1, axis=0)
    s4 = s2 + pltpu.roll(s2, 2, axis=0)
    s8 = s4 + pltpu.roll(s4, 4, axis=0)
    s16 = s8 + pltpu.roll(s8, 8, axis=0)
    return _pool_lane_select(lane, s2, s4, s8, s16)


def _winsum_up(ext, lane):
    m = ext.shape[0]
    s2 = ext + pltpu.roll(ext, m - 1, axis=0)
    s4 = s2 + pltpu.roll(s2, m - 2, axis=0)
    s8 = s4 + pltpu.roll(s4, m - 4, axis=0)
    s16 = s8 + pltpu.roll(s8, m - 8, axis=0)
    return _pool_lane_select(lane, s2, s4, s8, s16)


def _pool_inv_count(tile, n):
    lane = _iota((1, 256), 1)
    win = _pool_lane_select(lane, 2.0, 4.0, 8.0, 16.0).astype(F32)
    tpos = (tile * n + _iota((n, 1), 0) + 1).astype(F32)
    return jnp.where(tpos >= win, 1.0 / win, 1.0 / tpos)


def _silu_pair(z):
    s = jax.nn.sigmoid(z)
    return z * s, s * (1.0 + z * (1.0 - s))


def _chunks(a):
    return [a[c * CH:(c + 1) * CH] for c in range(a.shape[0] // CH)]


def _halves(fn, a, b):
    return jnp.concatenate([fn(a[:, 0:128], b[:, 0:128]), fn(a[:, 128:256], b[:, 128:256])], axis=1)


def _chunk_sums(tri, a):
    return jnp.concatenate([_dot2_r(tri, a[r:r + 256]) for r in range(0, a.shape[0], 256)], axis=0)


def _mixer_tile_prep(p_ref, t_ref, xc, prm_ref, gw_v, cm_ref, mk_ref):
    tail = t_ref[...]
    pre = _dot(tail, gw_v) + prm_ref[R_GB:R_GB + 1, 0:128]
    la = (jnp.minimum(pre, 0.0) - _lse1(pre)) * INV_TAU
    dtin = tail + prm_ref[R_DTB:R_DTB + 1, 0:128]
    dtf = jnp.maximum(dtin, 0.0) + _lse1(dtin)
    dte = _dot2_l(dtf, cm_ref[3, 0:128, :])
    da = dte * prm_ref[R_AE:R_AE + 1, 0:256]
    rev = _chunk_sums(cm_ref[0], jnp.concatenate([la, da], axis=1))
    dec = jnp.exp(rev[:, 0:128])
    kd = p_ref[:, C_GK:C_GK + 128].astype(F32) * dec
    wdec = jnp.exp(rev[:, 128:384])
    w = wdec * dte
    xw = xc[:, 0:256] * w
    d_s = [jnp.exp(_cs(a)) for a in _chunks(la)]
    et = [jnp.exp(_cs(a)) for a in _chunks(da)]
    mask_t = mk_ref[...]
    ut_g = [_dot_tn(v, k) * mask_t for v, k in zip(_chunks(p_ref[:, C_GV:C_GV + 256].astype(F32)), _chunks(kd))]
    ut_s = [_halves(_dot_tn, b, x) for b, x in zip(_chunks(xc[:, 256:512]), _chunks(xw))]
    return tail, pre, dtin, dte, dec, kd, wdec, w, xw, d_s, et, ut_g, ut_s


def _rmsproj(x, nw, wp, name, tm=512, rider=None):
    t = x.shape[0]

    def body(x_ref, nw_ref, w_ref, o_ref, t_ref, h_ref):
        xv = x_ref[...]
        rs = lax.rsqrt(jnp.mean(xv * xv, axis=-1, keepdims=True) + EPS)
        h = (xv * rs * nw_ref[...]).astype(BF16)
        h_ref[...] = h
        proj = jnp.dot(h, w_ref[...], preferred_element_type=F32)
        o_ref[...] = proj[:, 0:NPM].astype(BF16)
        t_ref[...] = proj[:, NPM:NP]

    (proj, tail, h), extra = _call(
        body, (x, nw, wp), grid=(t // tm,), name=name, sem=("parallel",), rider=rider,
        in_specs=[pl.BlockSpec((tm, D), lambda i: (i, 0)), pl.BlockSpec((1, D), lambda i: (0, 0)),
                  pl.BlockSpec((D, NP), lambda i: (0, 0))],
        out_specs=[pl.BlockSpec((tm, NPM), lambda i: (i, 0)), pl.BlockSpec((tm, NP - NPM), lambda i: (i, 0)),
                   pl.BlockSpec((tm, D), lambda i: (i, 0))],
        out_shape=[jax.ShapeDtypeStruct((t, NPM), BF16), jax.ShapeDtypeStruct((t, NP - NPM), F32),
                   jax.ShapeDtypeStruct((t, D), BF16)])
    return (proj, tail), h, extra


def _head_tile(xv, tgt, w):
    rs = lax.rsqrt(jnp.mean(xv * xv, axis=-1, keepdims=True) + EPS)
    xh = xv * rs
    err = xh * w - tgt
    dy = err * (1.0 / D)
    dxh = dy * w
    dx = rs * (dxh - xh * jnp.mean(dxh * xh, axis=-1, keepdims=True))
    return dx, _cs(dy * xh), (0.5 / D) * jnp.sum(err * err)


def _dxin(dp, wpt, x, dxn, nw, name, tm=512, rider=None):
    t = x.shape[0]

    def body(dp_ref, w_ref, x_ref, dxn_ref, nw_ref, dx_ref, dnw_ref):
        @pl.when(pl.program_id(0) == 0)
        def _():
            dnw_ref[...] = jnp.zeros_like(dnw_ref)

        dh = jnp.dot(dp_ref[...], w_ref[...], preferred_element_type=F32)
        xv = x_ref[...]
        rs = lax.rsqrt(jnp.mean(xv * xv, axis=-1, keepdims=True) + EPS)
        xh = xv * rs
        dnw_ref[0:1, :] += _cs(dh * xh)
        dxh = dh * nw_ref[...]
        dx_ref[...] = dxn_ref[...] + rs * (dxh - xh * jnp.mean(dxh * xh, axis=-1, keepdims=True))

    return _call(
        body, (dp, wpt, x, dxn, nw), grid=(t // tm,), name=name, sem=("arbitrary",), rider=rider,
        in_specs=[pl.BlockSpec((tm, NP), lambda i: (i, 0)), pl.BlockSpec((NP, D), lambda i: (0, 0)),
                  pl.BlockSpec((tm, D), lambda i: (i, 0)), pl.BlockSpec((tm, D), lambda i: (i, 0)),
                  pl.BlockSpec((1, D), lambda i: (0, 0))],
        out_specs=[pl.BlockSpec((tm, D), lambda i: (i, 0)), pl.BlockSpec((8, D), lambda i: (0, 0))],
        out_shape=[jax.ShapeDtypeStruct((t, D), F32), jax.ShapeDtypeStruct((8, D), F32)])


def _dwin(h, dp, name, tm=1024, rider=None):
    t = h.shape[0]

    def body(h_ref, dp_ref, o_ref):
        @pl.when(pl.program_id(0) == 0)
        def _():
            o_ref[...] = jnp.zeros_like(o_ref)

        o_ref[...] += _dot_tn(h_ref[...], dp_ref[...])

    (dwp,), extra = _call(
        body, (h, dp), grid=(t // tm,), name=name, sem=("arbitrary",), rider=rider,
        in_specs=[pl.BlockSpec((tm, D), lambda i: (i, 0)), pl.BlockSpec((tm, NP), lambda i: (i, 0))],
        out_specs=[pl.BlockSpec((D, NP), lambda i: (0, 0))], out_shape=[jax.ShapeDtypeStruct((D, NP), F32)])
    return dwp, extra


def _mixer_fwd(proj, x, wo, prm, gw, pw, cmat, mask, name, rider=None, head=None):
    proj, tail = proj
    t = proj.shape[0]
    nt, nc = t // TB, t // CH

    def body(p_ref, t_ref, x_ref, wo_ref, prm_ref, gw_ref, pw_ref, cm_ref, mk_ref, *rest):
        (tgt_ref, fw_ref), rest = (rest[:2], rest[2:]) if head else ((None, None), rest)
        mix_ref, sg_ref, ss_ref, xn_ref, xc_ref, dxc_ref, cv_ref, pool_ref = rest[:8]
        acc_ref = rest[8] if head else None
        sg_s, ss_s, h_ua, h_pu, h_sx = rest[-5:]
        i = pl.program_id(0)

        @pl.when(i == 0)
        def _():
            for r in (sg_s, ss_s, h_ua, h_pu, h_sx) + ((acc_ref,) if head else ()):
                r[...] = jnp.zeros_like(r)

        lane = _iota((1, 256), 1)
        u = p_ref[:, C_AC:C_AC + 256].astype(F32) * p_ref[:, C_AH:C_AH + 256].astype(F32)
        ext = jnp.concatenate([h_ua[...], u], axis=0)
        cv = (prm_ref[R_CAW + 2:R_CAW + 3, 0:256] * u + prm_ref[R_CAW + 1:R_CAW + 2, 0:256] * _dn(ext, 1, TB, 8)
              + prm_ref[R_CAW:R_CAW + 1, 0:256] * _dn(ext, 2, TB, 8))
        cv_ref[...] = cv.astype(BF16)
        mix_ref[:, 0:256] = (p_ref[:, C_AB:C_AB + 256].astype(F32) * cv * _silu(p_ref[:, C_AZ:C_AZ + 256].astype(F32))).astype(BF16)
        h_ua[...] = u[TB - 8:, :]
        pu = p_ref[:, C_PU:C_PU + 256].astype(F32)
        ext = jnp.concatenate([h_pu[...], pu], axis=0)
        pooled = (_winsum_dn(ext, lane)[16:] * _pool_inv_count(i, TB) - pu).astype(BF16)
        pool_ref[...] = pooled
        mixed = jnp.dot(pooled, pw_ref[...], preferred_element_type=F32)
        mix_ref[:, 512:768] = (prm_ref[R_PSC:R_PSC + 1, 0:256] * mixed * _silu(p_ref[:, C_PZ:C_PZ + 256].astype(F32))).astype(BF16)
        h_pu[...] = pu[TB - 16:, :]
        sx = p_ref[:, C_SX:C_SX + 768].astype(F32)
        ext = jnp.concatenate([h_sx[...], sx], axis=0)
        xc, dxc = _silu_pair(prm_ref[R_SCW + 3:R_SCW + 4, :] * sx + prm_ref[R_SCW + 2:R_SCW + 3, :] * _dn(ext, 1, TB, 8)
                             + prm_ref[R_SCW + 1:R_SCW + 2, :] * _dn(ext, 2, TB, 8)
                             + prm_ref[R_SCW:R_SCW + 1, :] * _dn(ext, 3, TB, 8) + prm_ref[R_SCB:R_SCB + 1, :])
        xc_ref[...] = xc.astype(BF16)
        dxc_ref[...] = dxc.astype(BF16)
        h_sx[...] = sx[TB - 8:, :]

        _, _, _, _, _, _, _, _, _, d_s, et, ut_g, ut_s = _mixer_tile_prep(p_ref, t_ref, xc, prm_ref, gw_ref[...], cm_ref, mk_ref)
        s_g, s_s = sg_s[...], ss_s[...]
        o, y = [], []
        qs = _chunks(p_ref[:, C_GQ:C_GQ + 128].astype(F32) * GLA_SCALE)
        cm = _chunks(xc[:, 512:768])
        for c in range(NCH):
            sg_ref[c] = s_g
            ss_ref[c] = s_s
            s_g = s_g * d_s[c] + ut_g[c]
            s_s = s_s * et[c] + ut_s[c]
            o.append(_dot_nt(qs[c], s_g))
            y.append(_halves(_dot, cm[c], s_s))
        sg_s[...] = s_g
        ss_s[...] = s_s
        o = jnp.concatenate(o, axis=0)
        on = o * lax.rsqrt(_dot2_l(o * o, cm_ref[2]) + EPS)
        mix_ref[:, 256:512] = (on * prm_ref[R_GNW:R_GNW + 1, 0:256] * _silu(p_ref[:, C_GZ:C_GZ + 256].astype(F32))).astype(BF16)
        y2 = ((jnp.concatenate(y, axis=0) + prm_ref[R_DE:R_DE + 1, 0:256] * xc[:, 0:256])
              * _silu(p_ref[:, C_SZ:C_SZ + 256].astype(F32)))
        mix_ref[:, 768:1024] = (y2 * lax.rsqrt(jnp.mean(y2 * y2, axis=-1, keepdims=True) + EPS)
                                * prm_ref[R_SNW:R_SNW + 1, 0:256]).astype(BF16)
        xn = x_ref[...] + jnp.dot(mix_ref[...], wo_ref[...], preferred_element_type=F32)
        if head:
            xn_ref[...], dfw, loss = _head_tile(xn, tgt_ref[...], fw_ref[...])
            acc_ref[0:1, :] += dfw
            acc_ref[1:2, :] += jnp.zeros((1, D), F32) + loss
        else:
            xn_ref[...] = xn

    row = pl.BlockSpec((TB, D), lambda i: (i, 0))
    return _call(
        body, (proj, tail, x, wo, prm, gw, pw, cmat, mask) + tuple(head or ()), grid=(nt,), name=name, sem=("arbitrary",),
        rider=rider,
        in_specs=[pl.BlockSpec((TB, NPM), lambda i: (i, 0)), pl.BlockSpec((TB, NP - NPM), lambda i: (i, 0)), row,
                  pl.BlockSpec((D, D), lambda i: (0, 0)), pl.BlockSpec((16, 768), lambda i: (0, 0)),
                  pl.BlockSpec((128, 128), lambda i: (0, 0)), pl.BlockSpec((256, 256), lambda i: (0, 0)),
                  pl.BlockSpec((4, 256, 256), lambda i: (0, 0, 0)), pl.BlockSpec((256, 128), lambda i: (0, 0))]
        + ([row, pl.BlockSpec((1, D), lambda i: (0, 0))] if head else []),
        out_specs=[row, pl.BlockSpec((NCH, 256, 128), lambda i: (i, 0, 0)),
                   pl.BlockSpec((NCH, 128, 256), lambda i: (i, 0, 0)), row] + [pl.BlockSpec((TB, 768), lambda i: (i, 0))] * 2
        + [pl.BlockSpec((TB, 256), lambda i: (i, 0))] * 2 + ([pl.BlockSpec((8, D), lambda i: (0, 0))] if head else []),
        out_shape=[jax.ShapeDtypeStruct((t, D), BF16), jax.ShapeDtypeStruct((nc, 256, 128), F32),
                   jax.ShapeDtypeStruct((nc, 128, 256), F32), jax.ShapeDtypeStruct((t, D), F32)]
        + [jax.ShapeDtypeStruct((t, 768), BF16)] * 2 + [jax.ShapeDtypeStruct((t, 256), BF16)] * 2
        + ([jax.ShapeDtypeStruct((8, D), F32)] if head else []),
        scratch_shapes=[pltpu.VMEM((256, 128), F32), pltpu.VMEM((128, 256), F32), pltpu.VMEM((8, 256), F32),
                        pltpu.VMEM((16, 256), F32), pltpu.VMEM((8, 768), F32)])


def _mixer_bwd(proj, dxn, wot, mix, sg, ss, xc16, dxc16, cv16, pool16, prm, gw, pw, cmat, mask, name, rider=None):
    proj, tail = proj
    t = proj.shape[0]
    nt = t // TB
    rev = lambda i: nt - 1 - i

    def body(p_ref, t_ref, dxn_ref, wot_ref, mix_ref, sg_ref, ss_ref, xc_ref, dxc_ref, cv_ref, pool_ref, prm_ref, gw_ref,
             pw_ref, cm_ref, mk_ref, dp_ref, sgc_ref, dwo_ref,
             gg_s, gs_s, h_dcv, h_dpl, h_dpre, gsm_ref, dgw_ref, dpw_ref, dm_ref):
        i = pl.program_id(0)
        tile = nt - 1 - i

        @pl.when(i == 0)
        def _():
            for r in (gg_s, gs_s, h_dcv, h_dpl, h_dpre, gsm_ref, dgw_ref, dpw_ref, dwo_ref):
                r[...] = jnp.zeros_like(r)

        dxn = dxn_ref[...].astype(BF16)
        dm_ref[...] = jnp.dot(dxn, wot_ref[...], preferred_element_type=F32)
        dwo_ref[...] += _dot_tn(mix_ref[...], dxn)

        lane = _iota((1, 256), 1)
        ah, ac = p_ref[:, C_AH:C_AH + 256].astype(F32), p_ref[:, C_AC:C_AC + 256].astype(F32)
        ab, az = p_ref[:, C_AB:C_AB + 256].astype(F32), p_ref[:, C_AZ:C_AZ + 256].astype(F32)
        w0, w1, w2 = (prm_ref[R_CAW + j:R_CAW + j + 1, 0:256] for j in range(3))
        u = ac * ah
        cv = cv_ref[...].astype(F32)
        g = dm_ref[:, 0:256]
        sz, dsz = _silu_pair(az)
        dp_ref[:, C_AB:C_AB + 256] = (g * cv * sz).astype(BF16)
        dp_ref[:, C_AZ:C_AZ + 256] = (g * ab * cv * dsz).astype(BF16)
        dcv = g * ab * sz
        dext = jnp.concatenate([dcv, h_dcv[...]], axis=0)
        dcv1, dcv2 = _up(dext, 1, TB), _up(dext, 2, TB)
        du = w2 * dcv + w1 * dcv1 + w0 * dcv2
        dp_ref[:, C_AC:C_AC + 256] = (du * ah).astype(BF16)
        dp_ref[:, C_AH:C_AH + 256] = (du * ac).astype(BF16)
        gsm_ref[R_CAW:R_CAW + 1, 0:256] += _cs(u * dcv2)
        gsm_ref[R_CAW + 1:R_CAW + 2, 0:256] += _cs(u * dcv1)
        gsm_ref[R_CAW + 2:R_CAW + 3, 0:256] += _cs(u * dcv)
        h_dcv[...] = dcv[0:8, :]
        pz = p_ref[:, C_PZ:C_PZ + 256].astype(F32)
        psc = prm_ref[R_PSC:R_PSC + 1, 0:256]
        icnt = _pool_inv_count(tile, TB)
        pooled = pool_ref[...]
        pw_v = pw_ref[...]
        mixed = jnp.dot(pooled, pw_v, preferred_element_type=F32)
        g = dm_ref[:, 512:768]
        sz, dsz = _silu_pair(pz)
        gsm_ref[R_PSC:R_PSC + 1, 0:256] += _cs(g * mixed * sz)
        dp_ref[:, C_PZ:C_PZ + 256] = (g * psc * mixed * dsz).astype(BF16)
        dmixed = g * psc * sz
        dpw_ref[...] += _dot_tn(pooled, dmixed)
        dpooled = _dot_nt(dmixed, pw_v)
        qd = dpooled * icnt
        dext = jnp.concatenate([qd, h_dpl[...]], axis=0)
        dp_ref[:, C_PU:C_PU + 256] = (_winsum_up(dext, lane)[:TB] - dpooled).astype(BF16)
        h_dpl[...] = qd[0:16, :]
        cw = [prm_ref[R_SCW + j:R_SCW + j + 1, :] for j in range(4)]
        xc = xc_ref[...].astype(F32)
        xs, bm, cm = xc[:, 0:256], xc[:, 256:512], xc[:, 512:768]

        gw_v = gw_ref[...]
        tail, pre, dtin, dte, dec, kd, wdec, w, xw, d_s, et, ut_g, ut_s = _mixer_tile_prep(p_ref, t_ref, xc, prm_ref,
                                                                                          gw_v, cm_ref, mk_ref)
        gmean = cm_ref[2]
        mask_t = mk_ref[...]
        gnw = prm_ref[R_GNW:R_GNW + 1, 0:256]
        a_e = prm_ref[R_AE:R_AE + 1, 0:256]
        d_e = prm_ref[R_DE:R_DE + 1, 0:256]
        snw = prm_ref[R_SNW:R_SNW + 1, 0:256]
        sg_in = [sg_ref[c] for c in range(NCH)]
        ss_in = [ss_ref[c] for c in range(NCH)]
        sg_n = [sg_in[c] * d_s[c] + ut_g[c] for c in range(NCH)]
        ss_n = [ss_in[c] * et[c] + ut_s[c] for c in range(NCH)]
        qs = _chunks(p_ref[:, C_GQ:C_GQ + 128].astype(F32) * GLA_SCALE)
        cm_c, bm_c, xw_c, kd_c = _chunks(cm), _chunks(bm), _chunks(xw), _chunks(kd)
        v_c = _chunks(p_ref[:, C_GV:C_GV + 256].astype(F32))
        o = jnp.concatenate([_dot_nt(qs[c], sg_n[c]) for c in range(NCH)], axis=0)
        y = jnp.concatenate([_halves(_dot, cm_c[c], ss_n[c]) for c in range(NCH)], axis=0) + d_e * xs
        gz = p_ref[:, C_GZ:C_GZ + 256].astype(F32)
        r = lax.rsqrt(_dot2_l(o * o, gmean) + EPS)
        on = o * r
        dyb = dm_ref[:, 256:512]
        sz, dsz = _silu_pair(gz)
        dp_ref[:, C_GZ:C_GZ + 256] = (dyb * on * gnw * dsz).astype(BF16)
        tg = dyb * sz
        gsm_ref[R_GNW:R_GNW + 1, 0:256] += _cs(tg * on)
        don = tg * gnw
        do_c = _chunks(r * (don - on * _dot2_l(don * on, gmean)))
        ssz = p_ref[:, C_SZ:C_SZ + 256].astype(F32)
        sil, dsil = _silu_pair(ssz)
        y2 = y * sil
        r = lax.rsqrt(jnp.mean(y2 * y2, axis=-1, keepdims=True) + EPS)
        yn = y2 * r
        dyd = dm_ref[:, 768:1024]
        gsm_ref[R_SNW:R_SNW + 1, 0:256] += _cs(dyd * yn)
        dn = dyd * snw
        dy2 = r * (dn - yn * jnp.mean(dn * yn, axis=-1, keepdims=True))
        dp_ref[:, C_SZ:C_SZ + 256] = (dy2 * y * dsil).astype(BF16)
        dy = dy2 * sil
        gsm_ref[R_DE:R_DE + 1, 0:256] += _cs(dy * xs)
        dy_c = _chunks(dy)
        dq = jnp.concatenate([_dot(do_c[c], sg_n[c]) for c in range(NCH)], axis=0)
        dp_ref[:, C_GQ:C_GQ + 128] = (dq * GLA_SCALE).astype(BF16)
        dcm = jnp.concatenate([_halves(_dot_nt, dy_c[c], ss_n[c]) for c in range(NCH)], axis=0)
        gg = [_dot_tn(do_c[c], qs[c]) * mask_t for c in range(NCH)]
        gs = [_halves(_dot_tn, cm_c[c], dy_c[c]) for c in range(NCH)]
        car_g, car_s = gg_s[...], gs_s[...]
        for c in reversed(range(NCH)):
            gg[c] = gg[c] + car_g
            gs[c] = gs[c] + car_s
            car_g = gg[c] * d_s[c]
            car_s = gs[c] * et[c]
        gg_s[...] = car_g
        gs_s[...] = car_s
        dkd = jnp.concatenate([_dot(v_c[c], gg[c]) for c in range(NCH)], axis=0)
        dp_ref[:, C_GV:C_GV + 256] = jnp.concatenate([_dot_nt(kd_c[c], gg[c]) for c in range(NCH)], axis=0).astype(BF16)
        dp_ref[:, C_GK:C_GK + 128] = (dkd * dec).astype(BF16)
        dbm = jnp.concatenate([_halves(_dot_nt, xw_c[c], gs[c]) for c in range(NCH)], axis=0)
        dxw = jnp.concatenate([_halves(_dot, bm_c[c], gs[c]) for c in range(NCH)], axis=0)
        dxs = dy * d_e + dxw * w
        dw = dxw * xs
        dsuf = _chunk_sums(cm_ref[1], jnp.concatenate([dkd * kd, dw * dte * wdec], axis=1))
        tot_g = jnp.concatenate([jnp.broadcast_to(_cs(gg[c] * sg_in[c]) * d_s[c], (CH, 128)) for c in range(NCH)], axis=0)
        tot_s = jnp.concatenate([jnp.broadcast_to(_cs(gs[c] * ss_in[c]) * et[c], (CH, 256)) for c in range(NCH)], axis=0)
        dpre = (dsuf[:, 0:128] + tot_g) * INV_TAU * jax.nn.sigmoid(-pre)
        dgw_ref[...] += _dot_tn(tail, dpre)
        gsm_ref[R_GB:R_GB + 1, 0:128] += _cs(dpre)
        dda = dsuf[:, 128:384] + tot_s
        gsm_ref[R_AE:R_AE + 1, 0:256] += _cs(dda * dte)
        dtail_s = _dot2_nt(dw * wdec + dda * a_e, cm_ref[3, 0:128, :]) * jax.nn.sigmoid(dtin)
        gsm_ref[R_DTB:R_DTB + 1, 0:128] += _cs(dtail_s)
        dp_ref[:, C_TL:C_TL + 128] = (_dot_nt(dpre, gw_v) + dtail_s).astype(BF16)
        dpre_c = jnp.concatenate([dxs, dbm, dcm], axis=1) * dxc_ref[...].astype(F32)
        dext = jnp.concatenate([dpre_c, h_dpre[...]], axis=0)
        ups = [dpre_c, _up(dext, 1, TB), _up(dext, 2, TB), _up(dext, 3, TB)]
        dp_ref[:, C_SX:C_SX + 768] = (cw[3] * ups[0] + cw[2] * ups[1] + cw[1] * ups[2] + cw[0] * ups[3]).astype(BF16)
        sx = p_ref[:, C_SX:C_SX + 768].astype(F32)
        for k in range(4):
            gsm_ref[R_SCW + k:R_SCW + k + 1, :] += _cs(sx * ups[3 - k])
        gsm_ref[R_SCB:R_SCB + 1, :] += _cs(dpre_c)
        h_dpre[...] = dpre_c[0:8, :]

        @pl.when(i == nt - 1)
        def _():
            ri, ci = _iota((256, 256), 0), _iota((256, 256), 1)
            per_head = jnp.where((ri >> 6) == ci, 1.0, 0.0).astype(BF16)
            per_dv = jnp.where((ri & 63) == ci, 1.0, 0.0).astype(BF16)
            row = _iota((8, 256), 0)
            top = gsm_ref[0:8, 0:256]
            sgc_ref[0:8, 0:256] = jnp.where(row == R_GNW, _dot3_l(top, per_dv), top)
            bot = gsm_ref[8:16, 0:256]
            fold = _dot3_l(jnp.where(row == R_AE - 8, bot * a_e, bot), per_head)
            sgc_ref[8:16, 0:256] = jnp.where((row == R_AE - 8) | (row == R_DE - 8), fold, bot)
            sgc_ref[0:16, 256:768] = gsm_ref[:, 256:768]
            sgc_ref[0:16, 768:896] = dgw_ref[0:16, :]
            sgc_ref[0:16, 896:1024] = jnp.zeros((16, 128), F32)
            diag = _pool_lane_select(lane, dpw_ref[0:64, :], dpw_ref[64:128, :], dpw_ref[128:192, :], dpw_ref[192:256, :])
            for q in range(4):
                sgc_ref[16:32, 256 * q:256 * q + 256] = diag[16 * q:16 * q + 16, :]

    return _call(
        body, (proj, tail, dxn, wot, mix, sg, ss, xc16, dxc16, cv16, pool16, prm, gw, pw, cmat, mask), grid=(nt,), name=name,
        sem=("arbitrary",), rider=rider,
        in_specs=[pl.BlockSpec((TB, NPM), lambda i: (rev(i), 0)),
                  pl.BlockSpec((TB, NP - NPM), lambda i: (rev(i), 0)),
                  pl.BlockSpec((TB, D), lambda i: (rev(i), 0)), pl.BlockSpec((D, D), lambda i: (0, 0)),
                  pl.BlockSpec((TB, D), lambda i: (rev(i), 0)),
                  pl.BlockSpec((NCH, 256, 128), lambda i: (rev(i), 0, 0)),
                  pl.BlockSpec((NCH, 128, 256), lambda i: (rev(i), 0, 0)),
                  pl.BlockSpec((TB, 768), lambda i: (rev(i), 0)), pl.BlockSpec((TB, 768), lambda i: (rev(i), 0)),
                  pl.BlockSpec((TB, 256), lambda i: (rev(i), 0)), pl.BlockSpec((TB, 256), lambda i: (rev(i), 0)),
                  pl.BlockSpec((16, 768), lambda i: (0, 0)), pl.BlockSpec((128, 128), lambda i: (0, 0)),
                  pl.BlockSpec((256, 256), lambda i: (0, 0)), pl.BlockSpec((4, 256, 256), lambda i: (0, 0, 0)),
                  pl.BlockSpec((256, 128), lambda i: (0, 0))],
        out_specs=[pl.BlockSpec((TB, NP), lambda i: (rev(i), 0)), pl.BlockSpec((32, 1024), lambda i: (0, 0)),
                   pl.BlockSpec((D, D), lambda i: (0, 0))],
        out_shape=[jax.ShapeDtypeStruct((t, NP), BF16), jax.ShapeDtypeStruct((32, 1024), F32),
                   jax.ShapeDtypeStruct((D, D), F32)],
        scratch_shapes=[pltpu.VMEM((256, 128), F32), pltpu.VMEM((128, 256), F32), pltpu.VMEM((8, 256), F32),
                        pltpu.VMEM((16, 256), F32), pltpu.VMEM((8, 768), F32), pltpu.VMEM((16, 768), F32),
                        pltpu.VMEM((128, 128), F32), pltpu.VMEM((256, 256), F32), pltpu.VMEM((TB, D), F32)])


SHARD = NPROJ // 4
SHARD_PAD = 896


def _ranges_to_perm(o, n):
    out, p = [], 0
    for start, size in _PERM:
        a, b = max(o, start), min(o + n, start + size)
        if a < b:
            out.append((a, b - a, p + a - start))
        p += size
    return out


def _ranges_to_orig(p0, n):
    out, p = [], 0
    for start, size in _PERM:
        a, b = max(p0, p), min(p0 + n, p + size)
        if a < b:
            out.append((a, b - a, start + a - p))
        p += size
    return out


def _lane_window(load, lo, n, d, lane):
    a = 128 * (lo // 128)
    off = lo - a
    w = 128 if off + n <= 128 else 256
    chunk = load(a, w)
    shift = (d - off) % w
    if shift:
        chunk = pltpu.roll(chunk, shift, axis=1)
    return jnp.where((lane >= d) & (lane < d + n), chunk[:, 0:128], 0.0)


def _assemble_w_in(slabs, name, rb=256):
    def body(s_ref, wp_ref, wpt_ref):
        lane = _iota((1, 128), 1)
        for b in range(NP // 128):
            acc = jnp.zeros((rb, 128), F32)
            for p, n, o in _ranges_to_orig(128 * b, 128):
                while n > 0:
                    s, lo = o // SHARD, o % SHARD
                    cnt = min(n, SHARD - lo)
                    acc = acc + _lane_window(lambda a, w, s=s: s_ref[s, :, a:a + w].astype(F32), lo, cnt, p - 128 * b, lane)
                    o, p, n = o + cnt, p + cnt, n - cnt
            wp_ref[:, 128 * b:128 * b + 128] = acc.astype(BF16)
            wpt_ref[128 * b:128 * b + 128, :] = acc.T.astype(BF16)

    return pl.pallas_call(
        body, grid=(D // rb,), name=name,
        in_specs=[pl.BlockSpec((4, rb, SHARD_PAD), lambda i: (0, i, 0))],
        out_specs=[pl.BlockSpec((rb, NP), lambda i: (i, 0)), pl.BlockSpec((NP, rb), lambda i: (0, i))],
        out_shape=[jax.ShapeDtypeStruct((D, NP), BF16), jax.ShapeDtypeStruct((NP, D), BF16)],
        compiler_params=_cparams(("parallel",)))(slabs)


def _split_dw_in(dwp, name, rb=256):
    rows = dwp.shape[0]

    def body(g_ref, o_ref):
        lane = _iota((1, 128), 1)
        for s in range(4):
            for k in range(SHARD_PAD // 128):
                acc = jnp.zeros((rb, 128), F32)
                n_valid = min(128, SHARD - 128 * k)
                for o, n, p in _ranges_to_perm(SHARD * s + 128 * k, n_valid):
                    acc = acc + _lane_window(lambda a, w: g_ref[:, a:a + w].astype(F32), p, n, o - SHARD * s - 128 * k, lane)
                o_ref[s, :, 128 * k:128 * k + 128] = acc.astype(o_ref.dtype)

    return pl.pallas_call(
        body, grid=(rows // rb,), name=name,
        in_specs=[pl.BlockSpec((rb, NP), lambda i: (i, 0))],
        out_specs=pl.BlockSpec((4, rb, SHARD_PAD), lambda i: (0, i, 0)),
        out_shape=jax.ShapeDtypeStruct((4, rows, SHARD_PAD), dwp.dtype),
        compiler_params=_cparams(("parallel",)))(dwp)


def _half(c, n):
    return pl.ds(pl.multiple_of(c * (n // 2), n // 2), n // 2)


def _other_chips(x, y):
    return ((1 - x, y), (x, 1 - y), (1 - x, 1 - y))


def _remote(src, dst, send, recv, k, dev):
    return pltpu.make_async_remote_copy(src_ref=src, dst_ref=dst, send_sem=send.at[k], recv_sem=recv.at[k], device_id=dev,
                                        device_id_type=MESH)


def _sem(n):
    return pltpu.SemaphoreType.DMA((n,))


def _rider_gather_ici(shards):
    shards = tuple(shards)
    n = len(shards)

    def copies(rins, routs, sems, arrivals=True):
        send, recv = sems
        x, y, c = _place()
        me = 2 * x + y
        out, inc = [], []
        for j, (px, py) in enumerate(_other_chips(x, y)):
            for k in range(n):
                rows = _half(c, shards[k].shape[0])
                out.append(_remote(rins[k].at[rows], routs[k].at[me, rows], send, recv, n * j + k, (px, py, c)))
                if arrivals:
                    inc.append(_remote(rins[k].at[rows], routs[k].at[2 * px + py, rows], send, recv, n * j + k, (px, py, c)))
        return out, inc

    def start(rins, routs, sems):
        for cp in copies(rins, routs, sems, arrivals=False)[0]:
            cp.start()

    def finish(rins, routs, sems):
        out, inc = copies(rins, routs, sems)
        for cp in inc:
            cp.wait_recv()
        for cp in out:
            cp.wait_send()

    return _Rider(shards, [jax.ShapeDtypeStruct((4,) + a.shape, a.dtype) for a in shards], [_sem(3 * n), _sem(3 * n)],
                  start, finish)


def _gather_ici_two_hops(shards, extra):
    shards = tuple(shards)
    n = len(shards)

    def body(*refs):
        ins, e_in, outs, e_out = refs[:n], refs[n], refs[n + 1:2 * n + 1], refs[2 * n + 1]
        send, recv = refs[2 * n + 2:]
        x, y, c = _place()
        slab = lambda px, py: 2 * px + py
        xn, yn, dg = (1 - x, y), (x, 1 - y), (1 - x, 1 - y)

        def part(k, q):
            r = shards[k].shape[0] // 4
            return pl.ds(pl.multiple_of(c * 2 * r + q * r, r), r)

        def hop(k, q, src_chip, to, sem):
            rows = part(k, q)
            src = ins[k].at[rows] if src_chip is None else outs[k].at[slab(*src_chip), rows]
            own = (x, y) if src_chip is None else src_chip
            return _remote(src, outs[k].at[slab(*own), rows], send, recv, sem, (*to, c))

        small = [_remote(e_in, e_out.at[slab(x, y)], send, recv, 6 * n + j, (*to, c)) for j, to in enumerate((xn, yn, dg))]
        first = [hop(k, q, None, (xn, yn)[q], 2 * k + q) for k in range(n) for q in (0, 1)]
        for cp in small + first:
            cp.start()
        for k in range(n):
            for q in (0, 1):
                nb = (xn, yn)[q]
                _remote(ins[k].at[part(k, q)], outs[k].at[slab(*nb), part(k, q)], send, recv, 2 * k + q, (*nb, c)).wait_recv()
        second = []
        for k in range(n):
            for q in (0, 1):
                to, via = (yn, xn)[q], (xn, yn)[q]
                second.append(hop(k, q, None, to, 2 * n + 4 * k + 2 * q))
                second.append(hop(k, q, via, to, 2 * n + 4 * k + 2 * q + 1))
        for cp in second:
            cp.start()
        for k in range(n):
            for q in (0, 1):
                frm, rows = (yn, xn)[q], part(k, q)
                for j, origin in enumerate((frm, dg)):
                    _remote(ins[k].at[rows], outs[k].at[slab(*origin), rows], send, recv, 2 * n + 4 * k + 2 * q + j,
                            (*frm, c)).wait_recv()
        for j, frm in enumerate((xn, yn, dg)):
            _remote(e_in, e_out.at[slab(*frm)], send, recv, 6 * n + j, (*frm, c)).wait_recv()
        third, theirs = [], []
        for k in range(n):
            rows = shards[k].shape[0]
            for j, chip in enumerate((xn, yn, dg)):
                got, missing = outs[k].at[slab(*chip), _half(c, rows)], outs[k].at[slab(*chip), _half(1 - c, rows)]
                third.append(_remote(got, got, send, recv, 6 * n + 3 + 3 * k + j, (x, y, 1 - c)))
                theirs.append(_remote(missing, missing, send, recv, 6 * n + 3 + 3 * k + j, (x, y, 1 - c)))
        for cp in third:
            cp.start()
        for cp in theirs:
            cp.wait_recv()
        for cp in small + first + second + third:
            cp.wait_send()

    outs = pl.pallas_call(
        body, name="gather0", in_specs=[_ANY] * (n + 1), out_specs=[_ANY] * (n + 1),
        out_shape=[jax.ShapeDtypeStruct((4,) + a.shape, a.dtype) for a in shards + (extra,)],
        scratch_shapes=[_sem(9 * n + 3), _sem(9 * n + 3)])(*shards, extra)
    return list(outs)


def _rider_gather_d2d(slabs):
    slabs = tuple(slabs)
    n = len(slabs)

    def copies(routs, sems, arrivals=True):
        send, recv = sems
        x, y, c = _place()
        out, inc = [], []
        for j, (px, py) in enumerate(_other_chips(x, y)):
            for k in range(n):
                rows = slabs[k].shape[1]
                mine, theirs = routs[k].at[2 * px + py, _half(c, rows)], routs[k].at[2 * px + py, _half(1 - c, rows)]
                out.append(_remote(mine, mine, send, recv, n * j + k, (x, y, 1 - c)))
                if arrivals:
                    inc.append(_remote(theirs, theirs, send, recv, n * j + k, (x, y, 1 - c)))
        return out, inc

    def start(rins, routs, sems):
        for cp in copies(routs, sems, arrivals=False)[0]:
            cp.start()

    def finish(rins, routs, sems):
        out, inc = copies(routs, sems)
        for cp in inc:
            cp.wait_recv()
        for cp in out:
            cp.wait_send()

    return _Rider(slabs, [jax.ShapeDtypeStruct(a.shape, a.dtype) for a in slabs], [_sem(3 * n), _sem(3 * n)], start, finish,
                  aliases={k: k for k in range(n)})


def _rider_swap(parts):
    parts = tuple(parts)
    n = len(parts)

    def copies(rins, routs, sems):
        send, recv = sems
        x, y, c = _place()
        return [_remote(rins[k].at[:, _half(1 - c, parts[k].shape[1])], routs[k], send, recv, k, (x, y, 1 - c))
                for k in range(n)]

    def start(rins, routs, sems):
        for cp in copies(rins, routs, sems):
            cp.start()

    def finish(rins, routs, sems):
        for cp in copies(rins, routs, sems):
            cp.wait()

    return _Rider(parts, [jax.ShapeDtypeStruct((a.shape[0], a.shape[1] // 2, a.shape[2]), a.dtype) for a in parts],
                  [_sem(n), _sem(n)], start, finish)


def _rider_scatter(parts):
    parts = tuple(parts)
    n = len(parts)

    def copies(rins, routs, sems, arrivals=True):
        send, recv = sems
        x, y, c = _place()
        me = 2 * x + y
        out, inc = [], []
        for j, (px, py) in enumerate(_other_chips(x, y)):
            for k in range(n):
                out.append(_remote(rins[k].at[2 * px + py], routs[k].at[me], send, recv, n * j + k, (px, py, c)))
                if arrivals:
                    inc.append(_remote(rins[k].at[me], routs[k].at[2 * px + py], send, recv, n * j + k, (px, py, c)))
        return out, inc

    def start(rins, routs, sems):
        for cp in copies(rins, routs, sems, arrivals=False)[0]:
            cp.start()

    def finish(rins, routs, sems):
        out, inc = copies(rins, routs, sems)
        for cp in inc:
            cp.wait_recv()
        for cp in out:
            cp.wait_send()

    return _Rider(parts, [jax.ShapeDtypeStruct(a.shape, a.dtype) for a in parts], [_sem(3 * n), _sem(3 * n)], start, finish)


def _rider_share(fulls):
    fulls = tuple(fulls)
    n = len(fulls)

    def copies(routs, sems, arrivals=True):
        send, recv = sems
        x, y, c = _place()
        out, inc = [], []
        for k in range(n):
            mine, theirs = routs[k].at[_half(c, fulls[k].shape[0])], routs[k].at[_half(1 - c, fulls[k].shape[0])]
            out.append(_remote(mine, mine, send, recv, k, (x, y, 1 - c)))
            if arrivals:
                inc.append(_remote(theirs, theirs, send, recv, k, (x, y, 1 - c)))
        return out, inc

    def start(rins, routs, sems):
        for cp in copies(routs, sems, arrivals=False)[0]:
            cp.start()

    def finish(rins, routs, sems):
        out, inc = copies(routs, sems)
        for cp in inc:
            cp.wait_recv()
        for cp in out:
            cp.wait_send()

    return _Rider(fulls, [jax.ShapeDtypeStruct(a.shape, a.dtype) for a in fulls], [_sem(n), _sem(n)], start, finish,
                  aliases={k: k for k in range(n)})


def _pair_sum(core, full, recv, name, br=128):
    n, rows, cols = recv.shape

    def body(c_ref, a_ref, b_ref, o_ref):
        o_ref[...] = (a_ref[...] + b_ref[...]).astype(BF16)

    nb = rows // br
    return pl.pallas_call(
        body, name=name, out_shape=jax.ShapeDtypeStruct(recv.shape, BF16),
        grid_spec=pltpu.PrefetchScalarGridSpec(
            num_scalar_prefetch=1, grid=(n, nb),
            in_specs=[pl.BlockSpec((1, br, cols), lambda i, j, c: (i, c[0] * nb + j, 0)),
                      pl.BlockSpec((1, br, cols), lambda i, j, c: (i, j, 0))],
            out_specs=pl.BlockSpec((1, br, cols), lambda i, j, c: (i, j, 0))),
        compiler_params=_cparams(("parallel", "parallel")))(core, full, recv)


def _chip_sum(place, gathered, mine, name, br=128):
    _, r, c = gathered.shape
    nb = r // br

    def body(p_ref, g_ref, m_ref, o_ref):
        slab = lambda j: jnp.where(p_ref[1] == j, m_ref[j], g_ref[j]).astype(F32)
        o_ref[...] = ((slab(0) + slab(1)) + slab(2)) + slab(3)

    return pl.pallas_call(
        body, name=name, out_shape=jax.ShapeDtypeStruct((2 * r, c), F32),
        grid_spec=pltpu.PrefetchScalarGridSpec(
            num_scalar_prefetch=1, grid=(nb,),
            in_specs=[pl.BlockSpec((4, br, c), lambda i, p: (0, i, 0)), pl.BlockSpec((4, br, c), lambda i, p: (0, i, 0))],
            out_specs=pl.BlockSpec((br, c), lambda i, p: (p[0] * nb + i, 0))),
        compiler_params=_cparams(("parallel",)))(place, gathered, mine)


def _adamw(w, g, m, v, name, br):
    n, r, c = w.shape

    def body(w_ref, g_ref, m_ref, v_ref, d_ref, m2_ref, v2_ref):
        d_ref[...], m2_ref[...], v2_ref[...] = _adam_math(w_ref[...], g_ref[...], m_ref[...], v_ref[...])

    spec = pl.BlockSpec((1, br, c), lambda i, j: (i, j, 0))
    shp = jax.ShapeDtypeStruct(w.shape, F32)
    return pl.pallas_call(body, grid=(n, r // br), name=name, in_specs=[spec] * 4, out_specs=[spec] * 3,
                          out_shape=[shp] * 3, compiler_params=_cparams(("parallel", "parallel")))(w, g, m, v)


def _adamw_w_in(w, g, m, v, name, bc=93):
    cols = w.shape[2]
    lead = lambda a: jnp.transpose(a, (2, 0, 1))
    g = jnp.stack([a[:, 0:cols] for a in g])

    def body(w_ref, g_ref, m_ref, v_ref, go_ref, d_ref, m2_ref, v2_ref):
        for l in range(2):
            gv = g_ref[:, l, :]
            d_ref[:, l, :], m2_ref[:, l, :], v2_ref[:, l, :] = _adam_math(w_ref[:, l, :], gv, m_ref[:, l, :], v_ref[:, l, :])
            go_ref[:, l, :] = gv

    spec = pl.BlockSpec((bc, 2, D), lambda i: (i, 0, 0))
    outs = pl.pallas_call(body, grid=(cols // bc,), name=name, in_specs=[spec] * 4, out_specs=[spec] * 4,
                          out_shape=[jax.ShapeDtypeStruct((cols, 2, D), F32)] * 4,
                          compiler_params=_cparams(("parallel",)))(lead(w), lead(g), lead(m), lead(v))
    return [jnp.transpose(o, (1, 2, 0)) for o in outs]


_SMALL_NAMES = ("norm_w", "conv_a_w", "gla_gate_w", "gla_gate_b", "gla_norm_w", "pool_w", "pool_scale", "ssd_conv_w",
                "ssd_conv_b", "ssd_dt_bias", "ssd_a_log", "ssd_d", "ssd_norm_w", "final_norm_w")
SMALL_ROWS = 80


def _adam_math(w, g, m, v):
    m2 = ADAM_B1 * m + (1.0 - ADAM_B1) * g
    v2 = ADAM_B2 * v + (1.0 - ADAM_B2) * (g * g)
    m_hat = m2 / (1.0 - ADAM_B1 ** ADAM_STEP)
    v_hat = v2 / (1.0 - ADAM_B2 ** ADAM_STEP)
    return -ADAM_LR * (m_hat / (jnp.sqrt(v_hat) + ADAM_EPS) + ADAM_WD * w), m2, v2


def _small_slices(name, chip):
    if name == "conv_a_w":
        return [((), slice(R_CAW, R_CAW + 3), slice(64 * chip, 64 * chip + 64))]
    if name == "ssd_conv_w":
        return [((), slice(R_SCW, R_SCW + 4), slice(192 * chip, 192 * chip + 192))]
    if name == "gla_gate_w":
        return [((), slice(0, 16), slice(768, 896))]
    if name == "pool_w":
        return [((g, slice(16 * q, 16 * q + 16)), slice(16, 32), slice(256 * q + 64 * g, 256 * q + 64 * g + 64))
                for g in range(4) for q in range(4)]
    row, lanes = {"gla_gate_b": (R_GB, slice(0, 128)), "gla_norm_w": (R_GNW, slice(0, 64)),
                  "pool_scale": (R_PSC, slice(0, 256)), "ssd_conv_b": (R_SCB, slice(0, 768)),
                  "ssd_dt_bias": (R_DTB, slice(16, 20)), "ssd_a_log": (R_AE, slice(0, 4)), "ssd_d": (R_DE, slice(0, 4)),
                  "ssd_norm_w": (R_SNW, slice(0, 256))}[name]
    return [((), slice(row, row + 1), lanes)]


def _rider_exchange(block):
    def copies(rins, routs, sems):
        send, recv = sems
        x, y, c = _place()
        flip = lambda v, bit: 1 - v if bit else v
        return [_remote(rins[0], routs[0].at[k], send, recv, k - 1, (flip(x, k & 4), flip(y, k & 2), flip(c, k & 1)))
                for k in range(1, 8)]

    def start(rins, routs, sems):
        for cp in copies(rins, routs, sems):
            cp.start()

    def finish(rins, routs, sems):
        for cp in copies(rins, routs, sems):
            cp.wait()

    return _Rider((block,), [jax.ShapeDtypeStruct((8,) + block.shape, block.dtype)], [_sem(7), _sem(7)], start, finish)


def _join_riders(a, b):
    na, oa, sa = len(a.inputs), len(a.out_shapes), len(a.sems)

    def start(rins, routs, sems):
        a.start(rins[:na], routs[:oa], sems[:sa])
        b.start(rins[na:], routs[oa:], sems[sa:])

    def finish(rins, routs, sems):
        a.finish(rins[:na], routs[:oa], sems[:sa])
        b.finish(rins[na:], routs[oa:], sems[sa:])

    aliases = {**a.aliases, **{na + k: oa + v for k, v in b.aliases.items()}}
    return _Rider(a.inputs + b.inputs, a.out_shapes + b.out_shapes, a.sems + b.sems, start, finish, aliases)


def _small_adamw(blocks, w, m, v):
    n = len(_SMALL_NAMES)

    def body(*refs):
        (own, ex), (own0, ex0) = refs[0:2], refs[2:4]
        refs = refs[3:]
        w_refs, m_refs, v_refs = refs[1:1 + n], refs[1 + n:1 + 2 * n], refs[1 + 2 * n:1 + 3 * n]
        o = 1 + 3 * n
        g_out, d_out, m_out, v_out = refs[o:o + n], refs[o + n:o + 2 * n], refs[o + 2 * n:o + 3 * n], refs[o + 3 * n:o + 4 * n]
        loss_ref, acc, acc0 = refs[o + 4 * n:o + 4 * n + 3]
        chip = 2 * lax.axis_index("x") + lax.axis_index("y")
        me = 2 * chip + lax.axis_index("c")
        acc[...] = jnp.zeros_like(acc)
        acc0[...] = jnp.zeros_like(acc0)
        for src in range(8):
            @pl.when(me == src)
            def _():
                acc[...] += own[...]
                acc0[...] += own0[...]

            @pl.when(me != src)
            def _(src=src):
                acc[...] += ex[jnp.bitwise_xor(me, src)]
                acc0[...] += ex0[jnp.bitwise_xor(me, src)]

        loss_ref[...] = acc[73:74, 0:1]

        def update(i, idx, g):
            d, m2, v2 = _adam_math(w_refs[i][idx], g, m_refs[i][idx], v_refs[i][idx])
            g_out[i][idx], d_out[i][idx], m_out[i][idx], v_out[i][idx] = g, d, m2, v2

        for i, name in enumerate(_SMALL_NAMES):
            if name == "final_norm_w":
                update(i, (slice(0, 1), slice(None)), acc[72:73, :])
            elif name == "norm_w":
                update(i, (slice(0, 1), slice(None)), acc0[0:1, :])
                update(i, (slice(1, 2), slice(None)), acc[64:65, :])
            elif name in ("conv_a_w", "ssd_conv_w"):
                for s in range(4):
                    @pl.when(chip == s)
                    def _(i=i, name=name, s=s):
                        for l in range(2):
                            (_, rows, lanes), = _small_slices(name, s)
                            update(i, (l,), acc[rows.start + 32 * l:rows.stop + 32 * l, lanes])
            else:
                for l in range(2):
                    for idx, rows, lanes in _small_slices(name, 0):
                        g = acc[rows.start + 32 * l:rows.stop + 32 * l, lanes]
                        if w_refs[i].ndim == 2:
                            update(i, (slice(l, l + 1), slice(None)), g)
                        else:
                            update(i, (l,) + idx, g)

    args = [a for pair in blocks for a in pair] + [d[k] for d in (w, m, v) for k in _SMALL_NAMES]
    shapes = [jax.ShapeDtypeStruct(w[k].shape, F32) for k in _SMALL_NAMES]
    vmem = pl.BlockSpec(memory_space=pltpu.VMEM)
    outs = pl.pallas_call(body, name="small_adamw", in_specs=[vmem] * len(args), out_specs=[vmem] * (4 * n + 1),
                          out_shape=shapes * 4 + [jax.ShapeDtypeStruct((1, 1), F32)],
                          scratch_shapes=[pltpu.VMEM((SMALL_ROWS, D), F32), pltpu.VMEM((8, D), F32)])(*args)
    return outs[0:n], outs[n:2 * n], outs[2 * n:3 * n], outs[3 * n:4 * n], outs[4 * n]


def _mixer_consts(layer, conv_a_w, gla_gate_w, gla_gate_b, gla_norm_w, pool_w, pool_scale, ssd_conv_w, ssd_conv_b,
                  ssd_dt_bias, ssd_a_log, ssd_d, ssd_norm_w):
    def row(v):
        return jnp.pad(v.reshape(1, -1), ((0, 0), (0, 768 - v.size)))

    dtb = jnp.pad(ssd_dt_bias[layer], (16, 108))
    rows = [jnp.pad(conv_a_w[layer], ((0, 0), (0, 512))), row(gla_gate_b[layer]), row(jnp.tile(gla_norm_w[layer], 4)),
            row(pool_scale[layer]), row(ssd_conv_b[layer]), row(dtb), row(jnp.repeat(-jnp.exp(ssd_a_log[layer]), 64)),
            row(jnp.repeat(ssd_d[layer], 64)), row(ssd_norm_w[layer]), jnp.zeros((1, 768), F32), ssd_conv_w[layer]]
    prm = jnp.concatenate(rows, axis=0)
    gw = jnp.pad(gla_gate_w[layer], ((0, 112), (0, 0))).astype(BF16)
    on_diag = (_iota((256, 256), 0) >> 6) == (_iota((256, 256), 1) >> 6)
    pw = jnp.where(on_diag, jnp.tile(pool_w[layer].reshape(256, 64), (1, 4)), 0.0)
    return (prm, gw, pw.astype(BF16)) + _mixer_matrices()


def _grad_slabs(dwp, dwo):
    return dwp.reshape(1, D, NP), dwo.reshape(4, D // 4, D)


class _Comm:
    def __init__(self, w_in, w_out):
        self.w_in16 = jnp.pad(w_in.astype(BF16), ((0, 0), (0, 0), (0, SHARD_PAD - SHARD)))
        self.w_out16 = w_out.astype(BF16)
        self.core = lax.axis_index("c").astype(jnp.int32).reshape(1)
        self.chip = 2 * lax.axis_index("x") + lax.axis_index("y")
        self.place = jnp.stack([lax.axis_index("c"), self.chip]).astype(jnp.int32)

    def gather_ici(self, layer):
        return _rider_gather_ici((self.w_in16[layer], self.w_out16[layer]))

    def pair_sum(self, layer, slabs, received):
        d_in, d_out = [_pair_sum(self.core, a, b, name=f"reduce_pair_sum{layer}_{k}")
                       for k, (a, b) in enumerate(zip(slabs, received))]
        return [_split_dw_in(d_in[0], name=f"split_dw_in{layer}"), d_out]

    def chip_sum(self, layer, gathered, mine):
        return [_chip_sum(self.place, a, b, name=f"reduce_chip_sum{layer}_{k}") for k, (a, b) in enumerate(zip(gathered, mine))]

    def layer_weights(self, layer, s_in, s_out):
        own = lambda slabs, shard: jnp.stack([jnp.where(self.chip == s, shard, slabs[s]) for s in range(4)])
        wp, wpt = _assemble_w_in(own(s_in, self.w_in16[layer]), name=f"assemble_w_in{layer}")
        wo = own(s_out, self.w_out16[layer]).reshape(D, D)
        return wp, wpt, wo, wo.T


def _local_step(x, tgt, norm_w, final_norm_w, consts, wts0, wts1=None, comm=None):
    nw = [norm_w[l:l + 1] for l in range(2)]
    proj0, h0, slabs = _rmsproj(x, nw[0], wts0[0], name="rmsproj0", rider=comm and comm.gather_ici(1))
    (mix0, sg0, ss0, x1, *conv0), slabs = _mixer_fwd(proj0, x, wts0[2], *consts[0], name="mixer_fwd0",
                                                     rider=comm and _rider_gather_d2d(slabs))
    if comm:
        wts1 = comm.layer_weights(1, *slabs)
    proj1, h1, _ = _rmsproj(x1, nw[1], wts1[0], name="rmsproj1")
    (mix1, sg1, ss1, dx, *conv1, head), _ = _mixer_fwd(proj1, x1, wts1[2], *consts[1], name="mixer_fwd1",
                                                       head=(tgt, final_norm_w.reshape(1, D)))
    (dproj, mgr1, dwo1), _ = _mixer_bwd(proj1, dx, wts1[3], mix1, sg1, ss1, *conv1, *consts[1], name="mixer_bwd1")
    dwp1, _ = _dwin(h1, dproj, name="dwin1")
    slabs1 = comm and _grad_slabs(dwp1, dwo1)
    (dx, dnw1), recv = _dxin(dproj, wts1[1], x1, dx, nw[1], name="dxin1", rider=comm and _rider_swap(slabs1))
    pairs1 = comm and comm.pair_sum(1, slabs1, recv)
    (dproj, mgr0, dwo0), gathered = _mixer_bwd(proj0, dx, wts0[3], mix0, sg0, ss0, *conv0, *consts[0], name="mixer_bwd0",
                                               rider=comm and _rider_scatter(pairs1))
    if not comm:
        dwp0, _ = _dwin(h0, dproj, name="dwin0")
        (dx, dnw0), _ = _dxin(dproj, wts0[1], x, dx, nw[0], name="dxin0")
        return head, dx, ((dwp0, dwp1), (dwo0, dwo1)), (dnw0, dnw1), (mgr0, mgr1)
    dwo0 = dwo0.reshape(4, D // 4, D)
    small = jnp.concatenate([mgr0, mgr1, dnw1, head], axis=0)
    dwp0, (*big1, recv_out, got_small) = _dwin(h0, dproj, name="dwin0", rider=_join_riders(_join_riders(
        _rider_share(comm.chip_sum(1, gathered, pairs1)), _rider_swap((dwo0,))), _rider_exchange(small)))
    slabs0 = (dwp0.reshape(1, D, NP), dwo0)
    pairs0 = comm.pair_sum(0, slabs0, (_run_rider(_rider_swap(slabs0[0:1]), "reduce_swap0")[0], recv_out))
    (dx, dnw0), gathered = _dxin(dproj, wts0[1], x, dx, nw[0], name="dxin0", rider=_rider_scatter(pairs0))
    last = _run_rider(_join_riders(_rider_share(comm.chip_sum(0, gathered, pairs0)), _rider_exchange(dnw0)),
                      "reduce_share0")
    return dx, ((last[0], big1[0]), (last[1], big1[1])), ((small, got_small), (dnw0, last[2]))


def kernel(x, norm_w, w_in, conv_a_w, gla_gate_w, gla_gate_b, gla_norm_w, pool_w, pool_scale, ssd_conv_w, ssd_conv_b, ssd_dt_bias, ssd_a_log, ssd_d, ssd_norm_w, w_out, final_norm_w, loss_target, m_norm_w, m_w_in, m_conv_a_w, m_gla_gate_w, m_gla_gate_b, m_gla_norm_w, m_pool_w, m_pool_scale, m_ssd_conv_w, m_ssd_conv_b, m_ssd_dt_bias, m_ssd_a_log, m_ssd_d, m_ssd_norm_w, m_w_out, m_final_norm_w, v_norm_w, v_w_in, v_conv_a_w, v_gla_gate_w, v_gla_gate_b, v_gla_norm_w, v_pool_w, v_pool_scale, v_ssd_conv_w, v_ssd_conv_b, v_ssd_dt_bias, v_ssd_a_log, v_ssd_d, v_ssd_norm_w, v_w_out, v_final_norm_w):
    weights = dict(norm_w=norm_w, w_in=w_in, conv_a_w=conv_a_w, gla_gate_w=gla_gate_w, gla_gate_b=gla_gate_b,
                   gla_norm_w=gla_norm_w, pool_w=pool_w, pool_scale=pool_scale, ssd_conv_w=ssd_conv_w,
                   ssd_conv_b=ssd_conv_b, ssd_dt_bias=ssd_dt_bias, ssd_a_log=ssd_a_log, ssd_d=ssd_d,
                   ssd_norm_w=ssd_norm_w, w_out=w_out, final_norm_w=final_norm_w)
    m_in = dict(norm_w=m_norm_w, w_in=m_w_in, conv_a_w=m_conv_a_w, gla_gate_w=m_gla_gate_w, gla_gate_b=m_gla_gate_b,
                gla_norm_w=m_gla_norm_w, pool_w=m_pool_w, pool_scale=m_pool_scale, ssd_conv_w=m_ssd_conv_w,
                ssd_conv_b=m_ssd_conv_b, ssd_dt_bias=m_ssd_dt_bias, ssd_a_log=m_ssd_a_log, ssd_d=m_ssd_d,
                ssd_norm_w=m_ssd_norm_w, w_out=m_w_out, final_norm_w=m_final_norm_w)
    v_in = dict(norm_w=v_norm_w, w_in=v_w_in, conv_a_w=v_conv_a_w, gla_gate_w=v_gla_gate_w, gla_gate_b=v_gla_gate_b,
                gla_norm_w=v_gla_norm_w, pool_w=v_pool_w, pool_scale=v_pool_scale, ssd_conv_w=v_ssd_conv_w,
                ssd_conv_b=v_ssd_conv_b, ssd_dt_bias=v_ssd_dt_bias, ssd_a_log=v_ssd_a_log, ssd_d=v_ssd_d,
                ssd_norm_w=v_ssd_norm_w, w_out=v_w_out, final_norm_w=v_final_norm_w)
    order = ("norm_w", "w_in", "conv_a_w", "gla_gate_w", "gla_gate_b", "gla_norm_w", "pool_w", "pool_scale",
             "ssd_conv_w", "ssd_conv_b", "ssd_dt_bias", "ssd_a_log", "ssd_d", "ssd_norm_w", "w_out", "final_norm_w")
    t = x.shape[1]

    comm = _Comm(w_in, w_out)
    cshard = jnp.zeros((16, 256), F32)
    for l in range(2):
        cshard = cshard.at[8 * l:8 * l + 3, 0:64].set(conv_a_w[l]).at[8 * l + 3:8 * l + 7, 0:192].set(ssd_conv_w[l])
    s_in, s_out, g_c = _gather_ici_two_hops((comm.w_in16[0], comm.w_out16[0]), cshard)
    g_c = [jnp.where(comm.chip == s, cshard, g_c[s]) for s in range(4)]
    conv_a_full = jnp.stack([jnp.concatenate([g_c[s][8 * l:8 * l + 3, 0:64] for s in range(4)], axis=-1) for l in range(2)])
    ssd_conv_full = jnp.stack([jnp.concatenate([g_c[s][8 * l + 3:8 * l + 7, 0:192] for s in range(4)], axis=-1)
                               for l in range(2)])
    consts = [_mixer_consts(l, conv_a_full, gla_gate_w, gla_gate_b, gla_norm_w, pool_w, pool_scale, ssd_conv_full,
                            ssd_conv_b, ssd_dt_bias, ssd_a_log, ssd_d, ssd_norm_w) for l in range(2)]

    dx, big, blocks = _local_step(x.reshape(t, D), loss_target.reshape(t, D), norm_w, final_norm_w, consts,
                                  comm.layer_weights(0, s_in, s_out), comm=comm)

    as2d = lambda d: {k: (d[k].reshape(1, D) if k == "final_norm_w" else d[k]) for k in _SMALL_NAMES}
    small = _small_adamw(blocks, as2d(weights), as2d(m_in), as2d(v_in))
    grads, delta, new_m, new_v = ({k: (a.reshape(D) if k == "final_norm_w" else a) for k, a in zip(_SMALL_NAMES, part)}
                                  for part in small[0:4])
    loss = small[4].reshape(())

    grads["w_out"] = jnp.stack(big[1])

    grads["w_in"], delta["w_in"], new_m["w_in"], new_v["w_in"] = _adamw_w_in(w_in, big[0], m_w_in, v_w_in, name="adamw_w_in")
    delta["w_out"], new_m["w_out"], new_v["w_out"] = _adamw(w_out, grads["w_out"], m_w_out, v_w_out, name="adamw_w_out", br=256)

    return (loss, dx.reshape(1, t, D), *[grads[k] for k in order], *[delta[k] for k in order],
            *[new_m[k] for k in order], *[new_v[k] for k in order])
```

```python
import functools

import jax
import jax.numpy as jnp
from jax import lax
from jax.experimental import pallas as pl
from jax.experimental.pallas import tpu as pltpu

F32 = jnp.float32
BF16 = jnp.bfloat16
MESH = pl.DeviceIdType.MESH

D = 1024
CH = 64
EPS = 1e-6
NP = 3456
NPROJ = 3348
NPM = 3328
GLA_SCALE = 32.0 ** -0.5
INV_TAU = 1.0 / 16.0
TB = 512
NCH = TB // CH
assert TB % 256 == 0

C_AH, C_AB, C_AC, C_AZ, C_GQ, C_GK, C_GV = 0, 256, 512, 768, 1024, 1152, 1280
C_GZ, C_PU, C_PZ, C_SZ, C_SX, C_TL = 1536, 1792, 2048, 2304, 2560, 3328
_PERM = ((0, 1536), (1552, 1792), (1536, 16), (3344, 4))

R_CAW, R_GB, R_GNW, R_PSC, R_SCB, R_DTB, R_AE, R_DE, R_SNW, R_SCW = 0, 3, 4, 5, 6, 7, 8, 9, 10, 12

ADAM_LR, ADAM_B1, ADAM_B2, ADAM_EPS, ADAM_WD, ADAM_STEP = 0.001, 0.9, 0.999, 1e-08, 0.01, 10

VMEM_LIMIT = 56 * 1024 * 1024


def _cparams(sem, limit=VMEM_LIMIT):
    return pltpu.CompilerParams(dimension_semantics=sem, vmem_limit_bytes=limit)


_ANY = pl.BlockSpec(memory_space=pl.ANY)


def _place():
    return lax.axis_index("x"), lax.axis_index("y"), lax.axis_index("c")


class _Rider:
    def __init__(self, inputs, out_shapes, sems, start, finish, aliases=None):
        self.inputs, self.out_shapes, self.sems = tuple(inputs), tuple(out_shapes), tuple(sems)
        self.start, self.finish, self.aliases = start, finish, dict(aliases or {})


def _call(body, args, *, grid, in_specs, out_specs, out_shape, name, sem, scratch_shapes=(), rider=None):
    if rider is None:
        outs = pl.pallas_call(body, grid=grid, name=name, in_specs=list(in_specs), out_specs=list(out_specs),
                              out_shape=list(out_shape), scratch_shapes=list(scratch_shapes),
                              compiler_params=_cparams(sem))(*args)
        return list(outs), []
    ni, no, ns = len(args), len(out_shape), len(scratch_shapes)
    ri, ro = len(rider.inputs), len(rider.out_shapes)

    def full(*refs):
        ins, rins = refs[:ni], refs[ni:ni + ri]
        outs, routs = refs[ni + ri:ni + ri + no], refs[ni + ri + no:ni + ri + no + ro]
        scr, rsem = refs[ni + ri + no + ro:ni + ri + no + ro + ns], refs[ni + ri + no + ro + ns:]
        first = functools.reduce(jnp.logical_and, [pl.program_id(a) == 0 for a in range(len(grid))])
        last = functools.reduce(jnp.logical_and, [pl.program_id(a) == grid[a] - 1 for a in range(len(grid))])

        @pl.when(first)
        def _():
            rider.start(rins, routs, rsem)

        body(*ins, *outs, *scr)

        @pl.when(last)
        def _():
            rider.finish(rins, routs, rsem)

    outs = pl.pallas_call(
        full, grid=grid, name=name, in_specs=list(in_specs) + [_ANY] * ri, out_specs=list(out_specs) + [_ANY] * ro,
        out_shape=list(out_shape) + list(rider.out_shapes), scratch_shapes=list(scratch_shapes) + list(rider.sems),
        input_output_aliases={ni + k: no + v for k, v in rider.aliases.items()},
        compiler_params=_cparams(("arbitrary",) * len(grid)))(*args, *rider.inputs)
    return list(outs[:no]), list(outs[no:])


def _run_rider(rider, name):
    ri = len(rider.inputs)

    def body(*refs):
        rins, routs, rsem = refs[:ri], refs[ri:ri + len(rider.out_shapes)], refs[ri + len(rider.out_shapes):]
        rider.start(rins, routs, rsem)
        rider.finish(rins, routs, rsem)

    return list(pl.pallas_call(body, name=name, in_specs=[_ANY] * ri, out_specs=[_ANY] * len(rider.out_shapes),
                               out_shape=list(rider.out_shapes), scratch_shapes=list(rider.sems),
                               input_output_aliases=dict(rider.aliases))(*rider.inputs))


def _dot(a, b):
    return jnp.dot(a.astype(BF16), b.astype(BF16), preferred_element_type=F32)


def _dot_nt(a, b):
    return lax.dot_general(a.astype(BF16), b.astype(BF16), (((1,), (1,)), ((), ())), preferred_element_type=F32)


def _dot_tn(a, b):
    return lax.dot_general(a.astype(BF16), b.astype(BF16), (((0,), (0,)), ((), ())), preferred_element_type=F32)


def _split(a):
    hi = a.astype(BF16)
    lo = (a - hi.astype(F32)).astype(BF16)
    return hi, lo


def _dot2_l(a, b):
    hi, lo = _split(a)
    return _dot(hi, b) + _dot(lo, b)


def _dot2_r(a, b):
    hi, lo = _split(b)
    return _dot(a, hi) + _dot(a, lo)


def _dot3_l(a, b):
    hi, lo = _split(a)
    lo2 = ((a - hi.astype(F32)) - lo.astype(F32)).astype(BF16)
    return _dot(hi, b) + _dot(lo, b) + _dot(lo2, b)


def _dot2_nt(a, b):
    hi, lo = _split(a)
    return _dot_nt(hi, b) + _dot_nt(lo, b)


def _silu(z):
    return z * jax.nn.sigmoid(z)


def _lse1(x):
    return jnp.log(1.0 + jnp.exp(-jnp.abs(x)))


def _cs(a):
    return jnp.sum(a, axis=0, keepdims=True)


def _iota(shape, dim):
    return lax.broadcasted_iota(jnp.int32, shape, dim)


def _mixer_matrices():
    r, c = _iota((256, 256), 0), _iota((256, 256), 1)
    same_chunk = (r >> 6) == (c >> 6)
    mats = jnp.stack([jnp.where((c > r) & same_chunk, 1.0, 0.0), jnp.where((c < r) & same_chunk, 1.0, 0.0),
                      jnp.where(same_chunk, 1.0 / 64.0, 0.0), jnp.where((r < 128) & (r - 16 == (c >> 6)), 1.0, 0.0)])
    mask = jnp.where((_iota((256, 128), 0) >> 6) == (_iota((256, 128), 1) >> 5), 1.0, 0.0)
    return mats.astype(BF16), mask.astype(F32)


def _dn(ext, k, n, h):
    return pltpu.roll(ext, k, axis=0)[h:h + n]


def _up(ext, k, n):
    return pltpu.roll(ext, ext.shape[0] - k, axis=0)[:n]


def _pool_lane_select(lane, s2, s4, s8, s16):
    return jnp.where(lane < 64, s2, jnp.where(lane < 128, s4, jnp.where(lane < 192, s8, s16)))


def _winsum_dn(ext, lane):
    s2 = ext + pltpu.roll(ext, 1, axis=0)
    s4 = s2 + pltpu.roll(s2, 2, axis=0)
    s8 = s4 + pltpu.roll(s4, 4, axis=0)
    s16 = s8 + pltpu.roll(s8, 8, axis=0)
    return _pool_lane_select(lane, s2, s4, s8, s16)


def _winsum_up(ext, lane):
    m = ext.shape[0]
    s2 = ext + pltpu.roll(ext, m - 1, axis=0)
    s4 = s2 + pltpu.roll(s2, m - 2, axis=0)
    s8 = s4 + pltpu.roll(s4, m - 4, axis=0)
    s16 = s8 + pltpu.roll(s8, m - 8, axis=0)
    return _pool_lane_select(lane, s2, s4, s8, s16)


def _pool_inv_count(tile, n):
    lane = _iota((1, 256), 1)
    win = _pool_lane_select(lane, 2.0, 4.0, 8.0, 16.0).astype(F32)
    tpos = (tile * n + _iota((n, 1), 0) + 1).astype(F32)
    return jnp.where(tpos >= win, 1.0 / win, 1.0 / tpos)


def _silu_pair(z):
    s = jax.nn.sigmoid(z)
    return z * s, s * (1.0 + z * (1.0 - s))


def _chunks(a):
    return [a[c * CH:(c + 1) * CH] for c in range(a.shape[0] // CH)]


def _halves(fn, a, b):
    return jnp.concatenate([fn(a[:, 0:128], b[:, 0:128]), fn(a[:, 128:256], b[:, 128:256])], axis=1)


def _chunk_sums(tri, a):
    return jnp.concatenate([_dot2_r(tri, a[r:r + 256]) for r in range(0, a.shape[0], 256)], axis=0)


def _mixer_tile_prep(p_ref, t_ref, xc, prm_ref, gw_v, cm_ref, mk_ref):
    tail = t_ref[...]
    pre = _dot(tail, gw_v) + prm_ref[R_GB:R_GB + 1, 0:128]
    la = (jnp.minimum(pre, 0.0) - _lse1(pre)) * INV_TAU
    dtin = tail + prm_ref[R_DTB:R_DTB + 1, 0:128]
    dtf = jnp.maximum(dtin, 0.0) + _lse1(dtin)
    dte = _dot2_l(dtf, cm_ref[3, 0:128, :])
    da = dte * prm_ref[R_AE:R_AE + 1, 0:256]
    rev = _chunk_sums(cm_ref[0], jnp.concatenate([la, da], axis=1))
    dec = jnp.exp(rev[:, 0:128])
    kd = p_ref[:, C_GK:C_GK + 128].astype(F32) * dec
    wdec = jnp.exp(rev[:, 128:384])
    w = wdec * dte
    xw = xc[:, 0:256] * w
    d_s = [jnp.exp(_cs(a)) for a in _chunks(la)]
    et = [jnp.exp(_cs(a)) for a in _chunks(da)]
    mask_t = mk_ref[...]
    ut_g = [_dot_tn(v, k) * mask_t for v, k in zip(_chunks(p_ref[:, C_GV:C_GV + 256].astype(F32)), _chunks(kd))]
    ut_s = [_halves(_dot_tn, b, x) for b, x in zip(_chunks(xc[:, 256:512]), _chunks(xw))]
    return tail, pre, dtin, dte, dec, kd, wdec, w, xw, d_s, et, ut_g, ut_s


def _rmsproj(x, nw, wp, name, tm=512, rider=None):
    t = x.shape[0]

    def body(x_ref, nw_ref, w_ref, o_ref, t_ref, h_ref):
        xv = x_ref[...]
        rs = lax.rsqrt(jnp.mean(xv * xv, axis=-1, keepdims=True) + EPS)
        h = (xv * rs * nw_ref[...]).astype(BF16)
        h_ref[...] = h
        proj = jnp.dot(h, w_ref[...], preferred_element_type=F32)
        o_ref[...] = proj[:, 0:NPM].astype(BF16)
        t_ref[...] = proj[:, NPM:NP]

    (proj, tail, h), extra = _call(
        body, (x, nw, wp), grid=(t // tm,), name=name, sem=("parallel",), rider=rider,
        in_specs=[pl.BlockSpec((tm, D), lambda i: (i, 0)), pl.BlockSpec((1, D), lambda i: (0, 0)),
                  pl.BlockSpec((D, NP), lambda i: (0, 0))],
        out_specs=[pl.BlockSpec((tm, NPM), lambda i: (i, 0)), pl.BlockSpec((tm, NP - NPM), lambda i: (i, 0)),
                   pl.BlockSpec((tm, D), lambda i: (i, 0))],
        out_shape=[jax.ShapeDtypeStruct((t, NPM), BF16), jax.ShapeDtypeStruct((t, NP - NPM), F32),
                   jax.ShapeDtypeStruct((t, D), BF16)])
    return (proj, tail), h, extra


def _head_tile(xv, tgt, w):
    rs = lax.rsqrt(jnp.mean(xv * xv, axis=-1, keepdims=True) + EPS)
    xh = xv * rs
    err = xh * w - tgt
    dy = err * (1.0 / D)
    dxh = dy * w
    dx = rs * (dxh - xh * jnp.mean(dxh * xh, axis=-1, keepdims=True))
    return dx, _cs(dy * xh), (0.5 / D) * jnp.sum(err * err)


def _dxin(dp, wpt, x, dxn, nw, name, tm=512, rider=None):
    t = x.shape[0]

    def body(dp_ref, w_ref, x_ref, dxn_ref, nw_ref, dx_ref, dnw_ref):
        @pl.when(pl.program_id(0) == 0)
        def _():
            dnw_ref[...] = jnp.zeros_like(dnw_ref)

        dh = jnp.dot(dp_ref[...], w_ref[...], preferred_element_type=F32)
        xv = x_ref[...]
        rs = lax.rsqrt(jnp.mean(xv * xv, axis=-1, keepdims=True) + EPS)
        xh = xv * rs
        dnw_ref[0:1, :] += _cs(dh * xh)
        dxh = dh * nw_ref[...]
        dx_ref[...] = dxn_ref[...] + rs * (dxh - xh * jnp.mean(dxh * xh, axis=-1, keepdims=True))

    return _call(
        body, (dp, wpt, x, dxn, nw), grid=(t // tm,), name=name, sem=("arbitrary",), rider=rider,
        in_specs=[pl.BlockSpec((tm, NP), lambda i: (i, 0)), pl.BlockSpec((NP, D), lambda i: (0, 0)),
                  pl.BlockSpec((tm, D), lambda i: (i, 0)), pl.BlockSpec((tm, D), lambda i: (i, 0)),
                  pl.BlockSpec((1, D), lambda i: (0, 0))],
        out_specs=[pl.BlockSpec((tm, D), lambda i: (i, 0)), pl.BlockSpec((8, D), lambda i: (0, 0))],
        out_shape=[jax.ShapeDtypeStruct((t, D), F32), jax.ShapeDtypeStruct((8, D), F32)])


def _dwin(h, dp, name, tm=1024, rider=None):
    t = h.shape[0]

    def body(h_ref, dp_ref, o_ref):
        @pl.when(pl.program_id(0) == 0)
        def _():
            o_ref[...] = jnp.zeros_like(o_ref)

        o_ref[...] += _dot_tn(h_ref[...], dp_ref[...])

    (dwp,), extra = _call(
        body, (h, dp), grid=(t // tm,), name=name, sem=("arbitrary",), rider=rider,
        in_specs=[pl.BlockSpec((tm, D), lambda i: (i, 0)), pl.BlockSpec((tm, NP), lambda i: (i, 0))],
        out_specs=[pl.BlockSpec((D, NP), lambda i: (0, 0))], out_shape=[jax.ShapeDtypeStruct((D, NP), F32)])
    return dwp, extra


def _mixer_fwd(proj, x, wo, prm, gw, pw, cmat, mask, name, rider=None, head=None):
    proj, tail = proj
    t = proj.shape[0]
    nt, nc = t // TB, t // CH

    def body(p_ref, t_ref, x_ref, wo_ref, prm_ref, gw_ref, pw_ref, cm_ref, mk_ref, *rest):
        (tgt_ref, fw_ref), rest = (rest[:2], rest[2:]) if head else ((None, None), rest)
        mix_ref, sg_ref, ss_ref, xn_ref, xc_ref, dxc_ref, cv_ref, pool_ref = rest[:8]
        acc_ref = rest[8] if head else None
        sg_s, ss_s, h_ua, h_pu, h_sx = rest[-5:]
        i = pl.program_id(0)

        @pl.when(i == 0)
        def _():
            for r in (sg_s, ss_s, h_ua, h_pu, h_sx) + ((acc_ref,) if head else ()):
                r[...] = jnp.zeros_like(r)

        lane = _iota((1, 256), 1)
        u = p_ref[:, C_AC:C_AC + 256].astype(F32) * p_ref[:, C_AH:C_AH + 256].astype(F32)
        ext = jnp.concatenate([h_ua[...], u], axis=0)
        cv = (prm_ref[R_CAW + 2:R_CAW + 3, 0:256] * u + prm_ref[R_CAW + 1:R_CAW + 2, 0:256] * _dn(ext, 1, TB, 8)
              + prm_ref[R_CAW:R_CAW + 1, 0:256] * _dn(ext, 2, TB, 8))
        cv_ref[...] = cv.astype(BF16)
        mix_ref[:, 0:256] = (p_ref[:, C_AB:C_AB + 256].astype(F32) * cv * _silu(p_ref[:, C_AZ:C_AZ + 256].astype(F32))).astype(BF16)
        xn = x_ref[...] + jnp.dot(mix_ref[:, 0:256], wo_ref[0:256, :], preferred_element_type=F32)
        h_ua[...] = u[TB - 8:, :]
        pu = p_ref[:, C_PU:C_PU + 256].astype(F32)
        ext = jnp.concatenate([h_pu[...], pu], axis=0)
        pooled = (_winsum_dn(ext, lane)[16:] * _pool_inv_count(i, TB) - pu).astype(BF16)
        pool_ref[...] = pooled
        mixed = jnp.dot(pooled, pw_ref[...], preferred_element_type=F32)
        mix_ref[:, 512:768] = (prm_ref[R_PSC:R_PSC + 1, 0:256] * mixed * _silu(p_ref[:, C_PZ:C_PZ + 256].astype(F32))).astype(BF16)
        xn += jnp.dot(mix_ref[:, 512:768], wo_ref[512:768, :], preferred_element_type=F32)
        h_pu[...] = pu[TB - 16:, :]
        sx = p_ref[:, C_SX:C_SX + 768].astype(F32)
        ext = jnp.concatenate([h_sx[...], sx], axis=0)
        xc, dxc = _silu_pair(prm_ref[R_SCW + 3:R_SCW + 4, :] * sx + prm_ref[R_SCW + 2:R_SCW + 3, :] * _dn(ext, 1, TB, 8)
                             + prm_ref[R_SCW + 1:R_SCW + 2, :] * _dn(ext, 2, TB, 8)
                             + prm_ref[R_SCW:R_SCW + 1, :] * _dn(ext, 3, TB, 8) + prm_ref[R_SCB:R_SCB + 1, :])
        xc_ref[...] = xc.astype(BF16)
        dxc_ref[...] = dxc.astype(BF16)
        h_sx[...] = sx[TB - 8:, :]

        _, _, _, _, _, _, _, _, _, d_s, et, ut_g, ut_s = _mixer_tile_prep(p_ref, t_ref, xc, prm_ref, gw_ref[...], cm_ref, mk_ref)
        s_g, s_s = sg_s[...], ss_s[...]
        o, y = [], []
        qs = _chunks(p_ref[:, C_GQ:C_GQ + 128].astype(F32) * GLA_SCALE)
        cm = _chunks(xc[:, 512:768])
        for c in range(NCH):
            sg_ref[c] = s_g
            ss_ref[c] = s_s
            s_g = s_g * d_s[c] + ut_g[c]
            s_s = s_s * et[c] + ut_s[c]
            o.append(_dot_nt(qs[c], s_g))
            y.append(_halves(_dot, cm[c], s_s))
        sg_s[...] = s_g
        ss_s[...] = s_s
        o = jnp.concatenate(o, axis=0)
        on = o * lax.rsqrt(_dot2_l(o * o, cm_ref[2]) + EPS)
        mix_ref[:, 256:512] = (on * prm_ref[R_GNW:R_GNW + 1, 0:256] * _silu(p_ref[:, C_GZ:C_GZ + 256].astype(F32))).astype(BF16)
        xn += jnp.dot(mix_ref[:, 256:512], wo_ref[256:512, :], preferred_element_type=F32)
        y2 = ((jnp.concatenate(y, axis=0) + prm_ref[R_DE:R_DE + 1, 0:256] * xc[:, 0:256])
              * _silu(p_ref[:, C_SZ:C_SZ + 256].astype(F32)))
        mix_ref[:, 768:1024] = (y2 * lax.rsqrt(jnp.mean(y2 * y2, axis=-1, keepdims=True) + EPS)
                                * prm_ref[R_SNW:R_SNW + 1, 0:256]).astype(BF16)
        xn += jnp.dot(mix_ref[:, 768:1024], wo_ref[768:1024, :], preferred_element_type=F32)
        if head:
            xn_ref[...], dfw, loss = _head_tile(xn, tgt_ref[...], fw_ref[...])
            acc_ref[0:1, :] += dfw
            acc_ref[1:2, :] += jnp.zeros((1, D), F32) + loss
        else:
            xn_ref[...] = xn

    row = pl.BlockSpec((TB, D), lambda i: (i, 0))
    return _call(
        body, (proj, tail, x, wo, prm, gw, pw, cmat, mask) + tuple(head or ()), grid=(nt,), name=name, sem=("arbitrary",),
        rider=rider,
        in_specs=[pl.BlockSpec((TB, NPM), lambda i: (i, 0)), pl.BlockSpec((TB, NP - NPM), lambda i: (i, 0)), row,
                  pl.BlockSpec((D, D), lambda i: (0, 0)), pl.BlockSpec((16, 768), lambda i: (0, 0)),
                  pl.BlockSpec((128, 128), lambda i: (0, 0)), pl.BlockSpec((256, 256), lambda i: (0, 0)),
                  pl.BlockSpec((4, 256, 256), lambda i: (0, 0, 0)), pl.BlockSpec((256, 128), lambda i: (0, 0))]
        + ([row, pl.BlockSpec((1, D), lambda i: (0, 0))] if head else []),
        out_specs=[row, pl.BlockSpec((NCH, 256, 128), lambda i: (i, 0, 0)),
                   pl.BlockSpec((NCH, 128, 256), lambda i: (i, 0, 0)), row] + [pl.BlockSpec((TB, 768), lambda i: (i, 0))] * 2
        + [pl.BlockSpec((TB, 256), lambda i: (i, 0))] * 2 + ([pl.BlockSpec((8, D), lambda i: (0, 0))] if head else []),
        out_shape=[jax.ShapeDtypeStruct((t, D), BF16), jax.ShapeDtypeStruct((nc, 256, 128), F32),
                   jax.ShapeDtypeStruct((nc, 128, 256), F32), jax.ShapeDtypeStruct((t, D), F32)]
        + [jax.ShapeDtypeStruct((t, 768), BF16)] * 2 + [jax.ShapeDtypeStruct((t, 256), BF16)] * 2
        + ([jax.ShapeDtypeStruct((8, D), F32)] if head else []),
        scratch_shapes=[pltpu.VMEM((256, 128), F32), pltpu.VMEM((128, 256), F32), pltpu.VMEM((8, 256), F32),
                        pltpu.VMEM((16, 256), F32), pltpu.VMEM((8, 768), F32)])


def _mixer_bwd(proj, dxn, wot, mix, sg, ss, xc16, dxc16, cv16, pool16, prm, gw, pw, cmat, mask, name, rider=None):
    proj, tail = proj
    t = proj.shape[0]
    nt = t // TB
    rev = lambda i: nt - 1 - i

    def body(p_ref, t_ref, dxn_ref, wot_ref, mix_ref, sg_ref, ss_ref, xc_ref, dxc_ref, cv_ref, pool_ref, prm_ref, gw_ref,
             pw_ref, cm_ref, mk_ref, dp_ref, sgc_ref, dwo_ref,
             gg_s, gs_s, h_dcv, h_dpl, h_dpre, gsm_ref, dgw_ref, dpw_ref, dm_ref):
        i = pl.program_id(0)
        tile = nt - 1 - i

        @pl.when(i == 0)
        def _():
            for r in (gg_s, gs_s, h_dcv, h_dpl, h_dpre, gsm_ref, dgw_ref, dpw_ref, dwo_ref):
                r[...] = jnp.zeros_like(r)

        dxn = dxn_ref[...].astype(BF16)
        dm_ref[...] = jnp.dot(dxn, wot_ref[...], preferred_element_type=F32)
        dwo_ref[...] += _dot_tn(mix_ref[...], dxn)

        lane = _iota((1, 256), 1)
        ah, ac = p_ref[:, C_AH:C_AH + 256].astype(F32), p_ref[:, C_AC:C_AC + 256].astype(F32)
        ab, az = p_ref[:, C_AB:C_AB + 256].astype(F32), p_ref[:, C_AZ:C_AZ + 256].astype(F32)
        w0, w1, w2 = (prm_ref[R_CAW + j:R_CAW + j + 1, 0:256] for j in range(3))
        u = ac * ah
        cv = cv_ref[...].astype(F32)
        g = dm_ref[:, 0:256]
        sz, dsz = _silu_pair(az)
        dp_ref[:, C_AB:C_AB + 256] = (g * cv * sz).astype(BF16)
        dp_ref[:, C_AZ:C_AZ + 256] = (g * ab * cv * dsz).astype(BF16)
        dcv = g * ab * sz
        dext = jnp.concatenate([dcv, h_dcv[...]], axis=0)
        dcv1, dcv2 = _up(dext, 1, TB), _up(dext, 2, TB)
        du = w2 * dcv + w1 * dcv1 + w0 * dcv2
        dp_ref[:, C_AC:C_AC + 256] = (du * ah).astype(BF16)
        dp_ref[:, C_AH:C_AH + 256] = (du * ac).astype(BF16)
        gsm_ref[R_CAW:R_CAW + 1, 0:256] += _cs(u * dcv2)
        gsm_ref[R_CAW + 1:R_CAW + 2, 0:256] += _cs(u * dcv1)
        gsm_ref[R_CAW + 2:R_CAW + 3, 0:256] += _cs(u * dcv)
        h_dcv[...] = dcv[0:8, :]
        pz = p_ref[:, C_PZ:C_PZ + 256].astype(F32)
        psc = prm_ref[R_PSC:R_PSC + 1, 0:256]
        icnt = _pool_inv_count(tile, TB)
        pooled = pool_ref[...]
        pw_v = pw_ref[...]
        mixed = jnp.dot(pooled, pw_v, preferred_element_type=F32)
        g = dm_ref[:, 512:768]
        sz, dsz = _silu_pair(pz)
        gsm_ref[R_PSC:R_PSC + 1, 0:256] += _cs(g * mixed * sz)
        dp_ref[:, C_PZ:C_PZ + 256] = (g * psc * mixed * dsz).astype(BF16)
        dmixed = g * psc * sz
        dpw_ref[...] += _dot_tn(pooled, dmixed)
        dpooled = _dot_nt(dmixed, pw_v)
        qd = dpooled * icnt
        dext = jnp.concatenate([qd, h_dpl[...]], axis=0)
        dp_ref[:, C_PU:C_PU + 256] = (_winsum_up(dext, lane)[:TB] - dpooled).astype(BF16)
        h_dpl[...] = qd[0:16, :]
        cw = [prm_ref[R_SCW + j:R_SCW + j + 1, :] for j in range(4)]
        xc = xc_ref[...].astype(F32)
        xs, bm, cm = xc[:, 0:256], xc[:, 256:512], xc[:, 512:768]

        gw_v = gw_ref[...]
        tail, pre, dtin, dte, dec, kd, wdec, w, xw, d_s, et, ut_g, ut_s = _mixer_tile_prep(p_ref, t_ref, xc, prm_ref,
                                                                                          gw_v, cm_ref, mk_ref)
        gmean = cm_ref[2]
        mask_t = mk_ref[...]
        gnw = prm_ref[R_GNW:R_GNW + 1, 0:256]
        a_e = prm_ref[R_AE:R_AE + 1, 0:256]
        d_e = prm_ref[R_DE:R_DE + 1, 0:256]
        snw = prm_ref[R_SNW:R_SNW + 1, 0:256]
        sg_in = [sg_ref[c] for c in range(NCH)]
        ss_in = [ss_ref[c] for c in range(NCH)]
        sg_n = [sg_in[c] * d_s[c] + ut_g[c] for c in range(NCH)]
        ss_n = [ss_in[c] * et[c] + ut_s[c] for c in range(NCH)]
        qs = _chunks(p_ref[:, C_GQ:C_GQ + 128].astype(F32) * GLA_SCALE)
        cm_c, bm_c, xw_c, kd_c = _chunks(cm), _chunks(bm), _chunks(xw), _chunks(kd)
        v_c = _chunks(p_ref[:, C_GV:C_GV + 256].astype(F32))
        o = jnp.concatenate([_dot_nt(qs[c], sg_n[c]) for c in range(NCH)], axis=0)
        y = jnp.concatenate([_halves(_dot, cm_c[c], ss_n[c]) for c in range(NCH)], axis=0) + d_e * xs
        gz = p_ref[:, C_GZ:C_GZ + 256].astype(F32)
        r = lax.rsqrt(_dot2_l(o * o, gmean) + EPS)
        on = o * r
        dyb = dm_ref[:, 256:512]
        sz, dsz = _silu_pair(gz)
        dp_ref[:, C_GZ:C_GZ + 256] = (dyb * on * gnw * dsz).astype(BF16)
        tg = dyb * sz
        gsm_ref[R_GNW:R_GNW + 1, 0:256] += _cs(tg * on)
        don = tg * gnw
        do_c = _chunks(r * (don - on * _dot2_l(don * on, gmean)))
        ssz = p_ref[:, C_SZ:C_SZ + 256].astype(F32)
        sil, dsil = _silu_pair(ssz)
        y2 = y * sil
        r = lax.rsqrt(jnp.mean(y2 * y2, axis=-1, keepdims=True) + EPS)
        yn = y2 * r
        dyd = dm_ref[:, 768:1024]
        gsm_ref[R_SNW:R_SNW + 1, 0:256] += _cs(dyd * yn)
        dn = dyd * snw
        dy2 = r * (dn - yn * jnp.mean(dn * yn, axis=-1, keepdims=True))
        dp_ref[:, C_SZ:C_SZ + 256] = (dy2 * y * dsil).astype(BF16)
        dy = dy2 * sil
        gsm_ref[R_DE:R_DE + 1, 0:256] += _cs(dy * xs)
        dy_c = _chunks(dy)
        dq = jnp.concatenate([_dot(do_c[c], sg_n[c]) for c in range(NCH)], axis=0)
        dp_ref[:, C_GQ:C_GQ + 128] = (dq * GLA_SCALE).astype(BF16)
        dcm = jnp.concatenate([_halves(_dot_nt, dy_c[c], ss_n[c]) for c in range(NCH)], axis=0)
        gg = [_dot_tn(do_c[c], qs[c]) * mask_t for c in range(NCH)]
        gs = [_halves(_dot_tn, cm_c[c], dy_c[c]) for c in range(NCH)]
        car_g, car_s = gg_s[...], gs_s[...]
        for c in reversed(range(NCH)):
            gg[c] = gg[c] + car_g
            gs[c] = gs[c] + car_s
            car_g = gg[c] * d_s[c]
            car_s = gs[c] * et[c]
        gg_s[...] = car_g
        gs_s[...] = car_s
        dkd = jnp.concatenate([_dot(v_c[c], gg[c]) for c in range(NCH)], axis=0)
        dp_ref[:, C_GV:C_GV + 256] = jnp.concatenate([_dot_nt(kd_c[c], gg[c]) for c in range(NCH)], axis=0).astype(BF16)
        dp_ref[:, C_GK:C_GK + 128] = (dkd * dec).astype(BF16)
        dbm = jnp.concatenate([_halves(_dot_nt, xw_c[c], gs[c]) for c in range(NCH)], axis=0)
        dxw = jnp.concatenate([_halves(_dot, bm_c[c], gs[c]) for c in range(NCH)], axis=0)
        dxs = dy * d_e + dxw * w
        dw = dxw * xs
        dsuf = _chunk_sums(cm_ref[1], jnp.concatenate([dkd * kd, dw * dte * wdec], axis=1))
        tot_g = jnp.concatenate([jnp.broadcast_to(_cs(gg[c] * sg_in[c]) * d_s[c], (CH, 128)) for c in range(NCH)], axis=0)
        tot_s = jnp.concatenate([jnp.broadcast_to(_cs(gs[c] * ss_in[c]) * et[c], (CH, 256)) for c in range(NCH)], axis=0)
        dpre = (dsuf[:, 0:128] + tot_g) * INV_TAU * jax.nn.sigmoid(-pre)
        dgw_ref[...] += _dot_tn(tail, dpre)
        gsm_ref[R_GB:R_GB + 1, 0:128] += _cs(dpre)
        dda = dsuf[:, 128:384] + tot_s
        gsm_ref[R_AE:R_AE + 1, 0:256] += _cs(dda * dte)
        dtail_s = _dot2_nt(dw * wdec + dda * a_e, cm_ref[3, 0:128, :]) * jax.nn.sigmoid(dtin)
        gsm_ref[R_DTB:R_DTB + 1, 0:128] += _cs(dtail_s)
        dp_ref[:, C_TL:C_TL + 128] = (_dot_nt(dpre, gw_v) + dtail_s).astype(BF16)
        dpre_c = jnp.concatenate([dxs, dbm, dcm], axis=1) * dxc_ref[...].astype(F32)
        dext = jnp.concatenate([dpre_c, h_dpre[...]], axis=0)
        ups = [dpre_c, _up(dext, 1, TB), _up(dext, 2, TB), _up(dext, 3, TB)]
        dp_ref[:, C_SX:C_SX + 768] = (cw[3] * ups[0] + cw[2] * ups[1] + cw[1] * ups[2] + cw[0] * ups[3]).astype(BF16)
        sx = p_ref[:, C_SX:C_SX + 768].astype(F32)
        for k in range(4):
            gsm_ref[R_SCW + k:R_SCW + k + 1, :] += _cs(sx * ups[3 - k])
        gsm_ref[R_SCB:R_SCB + 1, :] += _cs(dpre_c)
        h_dpre[...] = dpre_c[0:8, :]

        @pl.when(i == nt - 1)
        def _():
            ri, ci = _iota((256, 256), 0), _iota((256, 256), 1)
            per_head = jnp.where((ri >> 6) == ci, 1.0, 0.0).astype(BF16)
            per_dv = jnp.where((ri & 63) == ci, 1.0, 0.0).astype(BF16)
            row = _iota((8, 256), 0)
            top = gsm_ref[0:8, 0:256]
            sgc_ref[0:8, 0:256] = jnp.where(row == R_GNW, _dot3_l(top, per_dv), top)
            bot = gsm_ref[8:16, 0:256]
            fold = _dot3_l(jnp.where(row == R_AE - 8, bot * a_e, bot), per_head)
            sgc_ref[8:16, 0:256] = jnp.where((row == R_AE - 8) | (row == R_DE - 8), fold, bot)
            sgc_ref[0:16, 256:768] = gsm_ref[:, 256:768]
            sgc_ref[0:16, 768:896] = dgw_ref[0:16, :]
            sgc_ref[0:16, 896:1024] = jnp.zeros((16, 128), F32)
            diag = _pool_lane_select(lane, dpw_ref[0:64, :], dpw_ref[64:128, :], dpw_ref[128:192, :], dpw_ref[192:256, :])
            for q in range(4):
                sgc_ref[16:32, 256 * q:256 * q + 256] = diag[16 * q:16 * q + 16, :]

    return _call(
        body, (proj, tail, dxn, wot, mix, sg, ss, xc16, dxc16, cv16, pool16, prm, gw, pw, cmat, mask), grid=(nt,), name=name,
        sem=("arbitrary",), rider=rider,
        in_specs=[pl.BlockSpec((TB, NPM), lambda i: (rev(i), 0)),
                  pl.BlockSpec((TB, NP - NPM), lambda i: (rev(i), 0)),
                  pl.BlockSpec((TB, D), lambda i: (rev(i), 0)), pl.BlockSpec((D, D), lambda i: (0, 0)),
                  pl.BlockSpec((TB, D), lambda i: (rev(i), 0)),
                  pl.BlockSpec((NCH, 256, 128), lambda i: (rev(i), 0, 0)),
                  pl.BlockSpec((NCH, 128, 256), lambda i: (rev(i), 0, 0)),
                  pl.BlockSpec((TB, 768), lambda i: (rev(i), 0)), pl.BlockSpec((TB, 768), lambda i: (rev(i), 0)),
                  pl.BlockSpec((TB, 256), lambda i: (rev(i), 0)), pl.BlockSpec((TB, 256), lambda i: (rev(i), 0)),
                  pl.BlockSpec((16, 768), lambda i: (0, 0)), pl.BlockSpec((128, 128), lambda i: (0, 0)),
                  pl.BlockSpec((256, 256), lambda i: (0, 0)), pl.BlockSpec((4, 256, 256), lambda i: (0, 0, 0)),
                  pl.BlockSpec((256, 128), lambda i: (0, 0))],
        out_specs=[pl.BlockSpec((TB, NP), lambda i: (rev(i), 0)), pl.BlockSpec((32, 1024), lambda i: (0, 0)),
                   pl.BlockSpec((D, D), lambda i: (0, 0))],
        out_shape=[jax.ShapeDtypeStruct((t, NP), BF16), jax.ShapeDtypeStruct((32, 1024), F32),
                   jax.ShapeDtypeStruct((D, D), F32)],
        scratch_shapes=[pltpu.VMEM((256, 128), F32), pltpu.VMEM((128, 256), F32), pltpu.VMEM((8, 256), F32),
                        pltpu.VMEM((16, 256), F32), pltpu.VMEM((8, 768), F32), pltpu.VMEM((16, 768), F32),
                        pltpu.VMEM((128, 128), F32), pltpu.VMEM((256, 256), F32), pltpu.VMEM((TB, D), F32)])


SHARD = NPROJ // 4
SHARD_PAD = 896


def _ranges_to_perm(o, n):
    out, p = [], 0
    for start, size in _PERM:
        a, b = max(o, start), min(o + n, start + size)
        if a < b:
            out.append((a, b - a, p + a - start))
        p += size
    return out


def _ranges_to_orig(p0, n):
    out, p = [], 0
    for start, size in _PERM:
        a, b = max(p0, p), min(p0 + n, p + size)
        if a < b:
            out.append((a, b - a, start + a - p))
        p += size
    return out


def _lane_window(load, lo, n, d, lane):
    a = 128 * (lo // 128)
    off = lo - a
    w = 128 if off + n <= 128 else 256
    chunk = load(a, w)
    shift = (d - off) % w
    if shift:
        chunk = pltpu.roll(chunk, shift, axis=1)
    return jnp.where((lane >= d) & (lane < d + n), chunk[:, 0:128], 0.0)


def _assemble_w_in(slabs, name, rb=256):
    def body(s_ref, wp_ref, wpt_ref):
        lane = _iota((1, 128), 1)
        for b in range(NP // 128):
            acc = jnp.zeros((rb, 128), F32)
            for p, n, o in _ranges_to_orig(128 * b, 128):
                while n > 0:
                    s, lo = o // SHARD, o % SHARD
                    cnt = min(n, SHARD - lo)
                    acc = acc + _lane_window(lambda a, w, s=s: s_ref[s, :, a:a + w].astype(F32), lo, cnt, p - 128 * b, lane)
                    o, p, n = o + cnt, p + cnt, n - cnt
            wp_ref[:, 128 * b:128 * b + 128] = acc.astype(BF16)
            wpt_ref[128 * b:128 * b + 128, :] = acc.T.astype(BF16)

    return pl.pallas_call(
        body, grid=(D // rb,), name=name,
        in_specs=[pl.BlockSpec((4, rb, SHARD_PAD), lambda i: (0, i, 0))],
        out_specs=[pl.BlockSpec((rb, NP), lambda i: (i, 0)), pl.BlockSpec((NP, rb), lambda i: (0, i))],
        out_shape=[jax.ShapeDtypeStruct((D, NP), BF16), jax.ShapeDtypeStruct((NP, D), BF16)],
        compiler_params=_cparams(("parallel",)))(slabs)


def _split_dw_in(dwp, name, rb=256):
    rows = dwp.shape[0]

    def body(g_ref, o_ref):
        lane = _iota((1, 128), 1)
        for s in range(4):
            for k in range(SHARD_PAD // 128):
                acc = jnp.zeros((rb, 128), F32)
                n_valid = min(128, SHARD - 128 * k)
                for o, n, p in _ranges_to_perm(SHARD * s + 128 * k, n_valid):
                    acc = acc + _lane_window(lambda a, w: g_ref[:, a:a + w].astype(F32), p, n, o - SHARD * s - 128 * k, lane)
                o_ref[s, :, 128 * k:128 * k + 128] = acc.astype(o_ref.dtype)

    return pl.pallas_call(
        body, grid=(rows // rb,), name=name,
        in_specs=[pl.BlockSpec((rb, NP), lambda i: (i, 0))],
        out_specs=pl.BlockSpec((4, rb, SHARD_PAD), lambda i: (0, i, 0)),
        out_shape=jax.ShapeDtypeStruct((4, rows, SHARD_PAD), dwp.dtype),
        compiler_params=_cparams(("parallel",)))(dwp)


def _half(c, n):
    return pl.ds(pl.multiple_of(c * (n // 2), n // 2), n // 2)


def _other_chips(x, y):
    return ((1 - x, y), (x, 1 - y), (1 - x, 1 - y))


def _remote(src, dst, send, recv, k, dev):
    return pltpu.make_async_remote_copy(src_ref=src, dst_ref=dst, send_sem=send.at[k], recv_sem=recv.at[k], device_id=dev,
                                        device_id_type=MESH)


def _sem(n):
    return pltpu.SemaphoreType.DMA((n,))


def _rider_gather_ici(shards):
    shards = tuple(shards)
    n = len(shards)

    def copies(rins, routs, sems, arrivals=True):
        send, recv = sems
        x, y, c = _place()
        me = 2 * x + y
        out, inc = [], []
        for j, (px, py) in enumerate(_other_chips(x, y)):
            for k in range(n):
                rows = _half(c, shards[k].shape[0])
                out.append(_remote(rins[k].at[rows], routs[k].at[me, rows], send, recv, n * j + k, (px, py, c)))
                if arrivals:
                    inc.append(_remote(rins[k].at[rows], routs[k].at[2 * px + py, rows], send, recv, n * j + k, (px, py, c)))
        return out, inc

    def start(rins, routs, sems):
        for cp in copies(rins, routs, sems, arrivals=False)[0]:
            cp.start()

    def finish(rins, routs, sems):
        out, inc = copies(rins, routs, sems)
        for cp in inc:
            cp.wait_recv()
        for cp in out:
            cp.wait_send()

    return _Rider(shards, [jax.ShapeDtypeStruct((4,) + a.shape, a.dtype) for a in shards], [_sem(3 * n), _sem(3 * n)],
                  start, finish)


def _gather_ici_two_hops(shards, extra):
    shards = tuple(shards)
    n = len(shards)

    def body(*refs):
        ins, e_in, outs, e_out = refs[:n], refs[n], refs[n + 1:2 * n + 1], refs[2 * n + 1]
        send, recv = refs[2 * n + 2:]
        x, y, c = _place()
        slab = lambda px, py: 2 * px + py
        xn, yn, dg = (1 - x, y), (x, 1 - y), (1 - x, 1 - y)

        def part(k, q):
            r = shards[k].shape[0] // 4
            return pl.ds(pl.multiple_of(c * 2 * r + q * r, r), r)

        def hop(k, q, src_chip, to, sem):
            rows = part(k, q)
            src = ins[k].at[rows] if src_chip is None else outs[k].at[slab(*src_chip), rows]
            own = (x, y) if src_chip is None else src_chip
            return _remote(src, outs[k].at[slab(*own), rows], send, recv, sem, (*to, c))

        small = [_remote(e_in, e_out.at[slab(x, y)], send, recv, 6 * n + j, (*to, c)) for j, to in enumerate((xn, yn, dg))]
        first = [hop(k, q, None, (xn, yn)[q], 2 * k + q) for k in range(n) for q in (0, 1)]
        for cp in small + first:
            cp.start()
        for k in range(n):
            for q in (0, 1):
                nb = (xn, yn)[q]
                _remote(ins[k].at[part(k, q)], outs[k].at[slab(*nb), part(k, q)], send, recv, 2 * k + q, (*nb, c)).wait_recv()
        second = []
        for k in range(n):
            for q in (0, 1):
                to, via = (yn, xn)[q], (xn, yn)[q]
                second.append(hop(k, q, None, to, 2 * n + 4 * k + 2 * q))
                second.append(hop(k, q, via, to, 2 * n + 4 * k + 2 * q + 1))
        for cp in second:
            cp.start()
        for k in range(n):
            for q in (0, 1):
                frm, rows = (yn, xn)[q], part(k, q)
                for j, origin in enumerate((frm, dg)):
                    _remote(ins[k].at[rows], outs[k].at[slab(*origin), rows], send, recv, 2 * n + 4 * k + 2 * q + j,
                            (*frm, c)).wait_recv()
        for j, frm in enumerate((xn, yn, dg)):
            _remote(e_in, e_out.at[slab(*frm)], send, recv, 6 * n + j, (*frm, c)).wait_recv()
        third, theirs = [], []
        for k in range(n):
            rows = shards[k].shape[0]
            for j, chip in enumerate((xn, yn, dg)):
                got, missing = outs[k].at[slab(*chip), _half(c, rows)], outs[k].at[slab(*chip), _half(1 - c, rows)]
                third.append(_remote(got, got, send, recv, 6 * n + 3 + 3 * k + j, (x, y, 1 - c)))
                theirs.append(_remote(missing, missing, send, recv, 6 * n + 3 + 3 * k + j, (x, y, 1 - c)))
        for cp in third:
            cp.start()
        for cp in theirs:
            cp.wait_recv()
        for cp in small + first + second + third:
            cp.wait_send()

    outs = pl.pallas_call(
        body, name="gather0", in_specs=[_ANY] * (n + 1), out_specs=[_ANY] * (n + 1),
        out_shape=[jax.ShapeDtypeStruct((4,) + a.shape, a.dtype) for a in shards + (extra,)],
        scratch_shapes=[_sem(9 * n + 3), _sem(9 * n + 3)])(*shards, extra)
    return list(outs)


def _rider_gather_d2d(slabs):
    slabs = tuple(slabs)
    n = len(slabs)

    def copies(routs, sems, arrivals=True):
        send, recv = sems
        x, y, c = _place()
        out, inc = [], []
        for j, (px, py) in enumerate(_other_chips(x, y)):
            for k in range(n):
                rows = slabs[k].shape[1]
                mine, theirs = routs[k].at[2 * px + py, _half(c, rows)], routs[k].at[2 * px + py, _half(1 - c, rows)]
                out.append(_remote(mine, mine, send, recv, n * j + k, (x, y, 1 - c)))
                if arrivals:
                    inc.append(_remote(theirs, theirs, send, recv, n * j + k, (x, y, 1 - c)))
        return out, inc

    def start(rins, routs, sems):
        for cp in copies(routs, sems, arrivals=False)[0]:
            cp.start()

    def finish(rins, routs, sems):
        out, inc = copies(routs, sems)
        for cp in inc:
            cp.wait_recv()
        for cp in out:
            cp.wait_send()

    return _Rider(slabs, [jax.ShapeDtypeStruct(a.shape, a.dtype) for a in slabs], [_sem(3 * n), _sem(3 * n)], start, finish,
                  aliases={k: k for k in range(n)})


def _rider_swap(parts):
    parts = tuple(parts)
    n = len(parts)

    def copies(rins, routs, sems):
        send, recv = sems
        x, y, c = _place()
        return [_remote(rins[k].at[:, _half(1 - c, parts[k].shape[1])], routs[k], send, recv, k, (x, y, 1 - c))
                for k in range(n)]

    def start(rins, routs, sems):
        for cp in copies(rins, routs, sems):
            cp.start()

    def finish(rins, routs, sems):
        for cp in copies(rins, routs, sems):
            cp.wait()

    return _Rider(parts, [jax.ShapeDtypeStruct((a.shape[0], a.shape[1] // 2, a.shape[2]), a.dtype) for a in parts],
                  [_sem(n), _sem(n)], start, finish)


def _rider_scatter(parts):
    parts = tuple(parts)
    n = len(parts)

    def copies(rins, routs, sems, arrivals=True):
        send, recv = sems
        x, y, c = _place()
        me = 2 * x + y
        out, inc = [], []
        for j, (px, py) in enumerate(_other_chips(x, y)):
            for k in range(n):
                out.append(_remote(rins[k].at[2 * px + py], routs[k].at[me], send, recv, n * j + k, (px, py, c)))
                if arrivals:
                    inc.append(_remote(rins[k].at[me], routs[k].at[2 * px + py], send, recv, n * j + k, (px, py, c)))
        return out, inc

    def start(rins, routs, sems):
        for cp in copies(rins, routs, sems, arrivals=False)[0]:
            cp.start()

    def finish(rins, routs, sems):
        out, inc = copies(rins, routs, sems)
        for cp in inc:
            cp.wait_recv()
        for cp in out:
            cp.wait_send()

    return _Rider(parts, [jax.ShapeDtypeStruct(a.shape, a.dtype) for a in parts], [_sem(3 * n), _sem(3 * n)], start, finish)


def _rider_share(fulls):
    fulls = tuple(fulls)
    n = len(fulls)

    def copies(routs, sems, arrivals=True):
        send, recv = sems
        x, y, c = _place()
        out, inc = [], []
        for k in range(n):
            mine, theirs = routs[k].at[_half(c, fulls[k].shape[0])], routs[k].at[_half(1 - c, fulls[k].shape[0])]
            out.append(_remote(mine, mine, send, recv, k, (x, y, 1 - c)))
            if arrivals:
                inc.append(_remote(theirs, theirs, send, recv, k, (x, y, 1 - c)))
        return out, inc

    def start(rins, routs, sems):
        for cp in copies(routs, sems, arrivals=False)[0]:
            cp.start()

    def finish(rins, routs, sems):
        out, inc = copies(routs, sems)
        for cp in inc:
            cp.wait_recv()
        for cp in out:
            cp.wait_send()

    return _Rider(fulls, [jax.ShapeDtypeStruct(a.shape, a.dtype) for a in fulls], [_sem(n), _sem(n)], start, finish,
                  aliases={k: k for k in range(n)})


def _pair_sum(core, full, recv, name, br=128):
    n, rows, cols = recv.shape

    def body(c_ref, a_ref, b_ref, o_ref):
        o_ref[...] = (a_ref[...] + b_ref[...]).astype(BF16)

    nb = rows // br
    return pl.pallas_call(
        body, name=name, out_shape=jax.ShapeDtypeStruct(recv.shape, BF16),
        grid_spec=pltpu.PrefetchScalarGridSpec(
            num_scalar_prefetch=1, grid=(n, nb),
            in_specs=[pl.BlockSpec((1, br, cols), lambda i, j, c: (i, c[0] * nb + j, 0)),
                      pl.BlockSpec((1, br, cols), lambda i, j, c: (i, j, 0))],
            out_specs=pl.BlockSpec((1, br, cols), lambda i, j, c: (i, j, 0))),
        compiler_params=_cparams(("parallel", "parallel")))(core, full, recv)


def _chip_sum(place, gathered, mine, name, br=128):
    _, r, c = gathered.shape
    nb = r // br

    def body(p_ref, g_ref, m_ref, o_ref):
        slab = lambda j: jnp.where(p_ref[1] == j, m_ref[j], g_ref[j]).astype(F32)
        o_ref[...] = ((slab(0) + slab(1)) + slab(2)) + slab(3)

    return pl.pallas_call(
        body, name=name, out_shape=jax.ShapeDtypeStruct((2 * r, c), F32),
        grid_spec=pltpu.PrefetchScalarGridSpec(
            num_scalar_prefetch=1, grid=(nb,),
            in_specs=[pl.BlockSpec((4, br, c), lambda i, p: (0, i, 0)), pl.BlockSpec((4, br, c), lambda i, p: (0, i, 0))],
            out_specs=pl.BlockSpec((br, c), lambda i, p: (p[0] * nb + i, 0))),
        compiler_params=_cparams(("parallel",)))(place, gathered, mine)


def _adamw(w, g, m, v, name, br):
    n, r, c = w.shape

    def body(w_ref, g_ref, m_ref, v_ref, d_ref, m2_ref, v2_ref):
        d_ref[...], m2_ref[...], v2_ref[...] = _adam_math(w_ref[...], g_ref[...], m_ref[...], v_ref[...])

    spec = pl.BlockSpec((1, br, c), lambda i, j: (i, j, 0))
    shp = jax.ShapeDtypeStruct(w.shape, F32)
    return pl.pallas_call(body, grid=(n, r // br), name=name, in_specs=[spec] * 4, out_specs=[spec] * 3,
                          out_shape=[shp] * 3, compiler_params=_cparams(("parallel", "parallel")))(w, g, m, v)


def _adamw_w_in(w, g, m, v, name, bc=93):
    cols = w.shape[2]
    lead = lambda a: jnp.transpose(a, (2, 0, 1))
    g = jnp.stack([a[:, 0:cols] for a in g])

    def body(w_ref, g_ref, m_ref, v_ref, go_ref, d_ref, m2_ref, v2_ref):
        for l in range(2):
            gv = g_ref[:, l, :]
            d_ref[:, l, :], m2_ref[:, l, :], v2_ref[:, l, :] = _adam_math(w_ref[:, l, :], gv, m_ref[:, l, :], v_ref[:, l, :])
            go_ref[:, l, :] = gv

    spec = pl.BlockSpec((bc, 2, D), lambda i: (i, 0, 0))
    outs = pl.pallas_call(body, grid=(cols // bc,), name=name, in_specs=[spec] * 4, out_specs=[spec] * 4,
                          out_shape=[jax.ShapeDtypeStruct((cols, 2, D), F32)] * 4,
                          compiler_params=_cparams(("parallel",)))(lead(w), lead(g), lead(m), lead(v))
    return [jnp.transpose(o, (1, 2, 0)) for o in outs]


_SMALL_NAMES = ("norm_w", "conv_a_w", "gla_gate_w", "gla_gate_b", "gla_norm_w", "pool_w", "pool_scale", "ssd_conv_w",
                "ssd_conv_b", "ssd_dt_bias", "ssd_a_log", "ssd_d", "ssd_norm_w", "final_norm_w")
SMALL_ROWS = 80


def _adam_math(w, g, m, v):
    m2 = ADAM_B1 * m + (1.0 - ADAM_B1) * g
    v2 = ADAM_B2 * v + (1.0 - ADAM_B2) * (g * g)
    m_hat = m2 / (1.0 - ADAM_B1 ** ADAM_STEP)
    v_hat = v2 / (1.0 - ADAM_B2 ** ADAM_STEP)
    return -ADAM_LR * (m_hat / (jnp.sqrt(v_hat) + ADAM_EPS) + ADAM_WD * w), m2, v2


def _small_slices(name, chip):
    if name == "conv_a_w":
        return [((), slice(R_CAW, R_CAW + 3), slice(64 * chip, 64 * chip + 64))]
    if name == "ssd_conv_w":
        return [((), slice(R_SCW, R_SCW + 4), slice(192 * chip, 192 * chip + 192))]
    if name == "gla_gate_w":
        return [((), slice(0, 16), slice(768, 896))]
    if name == "pool_w":
        return [((g, slice(16 * q, 16 * q + 16)), slice(16, 32), slice(256 * q + 64 * g, 256 * q + 64 * g + 64))
                for g in range(4) for q in range(4)]
    row, lanes = {"gla_gate_b": (R_GB, slice(0, 128)), "gla_norm_w": (R_GNW, slice(0, 64)),
                  "pool_scale": (R_PSC, slice(0, 256)), "ssd_conv_b": (R_SCB, slice(0, 768)),
                  "ssd_dt_bias": (R_DTB, slice(16, 20)), "ssd_a_log": (R_AE, slice(0, 4)), "ssd_d": (R_DE, slice(0, 4)),
                  "ssd_norm_w": (R_SNW, slice(0, 256))}[name]
    return [((), slice(row, row + 1), lanes)]


def _rider_exchange(block):
    def copies(rins, routs, sems):
        send, recv = sems
        x, y, c = _place()
        flip = lambda v, bit: 1 - v if bit else v
        return [_remote(rins[0], routs[0].at[k], send, recv, k - 1, (flip(x, k & 4), flip(y, k & 2), flip(c, k & 1)))
                for k in range(1, 8)]

    def start(rins, routs, sems):
        for cp in copies(rins, routs, sems):
            cp.start()

    def finish(rins, routs, sems):
        for cp in copies(rins, routs, sems):
            cp.wait()

    return _Rider((block,), [jax.ShapeDtypeStruct((8,) + block.shape, block.dtype)], [_sem(7), _sem(7)], start, finish)


def _join_riders(a, b):
    na, oa, sa = len(a.inputs), len(a.out_shapes), len(a.sems)

    def start(rins, routs, sems):
        a.start(rins[:na], routs[:oa], sems[:sa])
        b.start(rins[na:], routs[oa:], sems[sa:])

    def finish(rins, routs, sems):
        a.finish(rins[:na], routs[:oa], sems[:sa])
        b.finish(rins[na:], routs[oa:], sems[sa:])

    aliases = {**a.aliases, **{na + k: oa + v for k, v in b.aliases.items()}}
    return _Rider(a.inputs + b.inputs, a.out_shapes + b.out_shapes, a.sems + b.sems, start, finish, aliases)


def _small_adamw(blocks, w, m, v):
    n = len(_SMALL_NAMES)

    def body(*refs):
        (own, ex), (own0, ex0) = refs[0:2], refs[2:4]
        refs = refs[3:]
        w_refs, m_refs, v_refs = refs[1:1 + n], refs[1 + n:1 + 2 * n], refs[1 + 2 * n:1 + 3 * n]
        o = 1 + 3 * n
        g_out, d_out, m_out, v_out = refs[o:o + n], refs[o + n:o + 2 * n], refs[o + 2 * n:o + 3 * n], refs[o + 3 * n:o + 4 * n]
        loss_ref, acc, acc0 = refs[o + 4 * n:o + 4 * n + 3]
        chip = 2 * lax.axis_index("x") + lax.axis_index("y")
        me = 2 * chip + lax.axis_index("c")
        acc[...] = jnp.zeros_like(acc)
        acc0[...] = jnp.zeros_like(acc0)
        for src in range(8):
            @pl.when(me == src)
            def _():
                acc[...] += own[...]
                acc0[...] += own0[...]

            @pl.when(me != src)
            def _(src=src):
                acc[...] += ex[jnp.bitwise_xor(me, src)]
                acc0[...] += ex0[jnp.bitwise_xor(me, src)]

        loss_ref[...] = acc[73:74, 0:1]

        def update(i, idx, g):
            d, m2, v2 = _adam_math(w_refs[i][idx], g, m_refs[i][idx], v_refs[i][idx])
            g_out[i][idx], d_out[i][idx], m_out[i][idx], v_out[i][idx] = g, d, m2, v2

        for i, name in enumerate(_SMALL_NAMES):
            if name == "final_norm_w":
                update(i, (slice(0, 1), slice(None)), acc[72:73, :])
            elif name == "norm_w":
                update(i, (slice(0, 1), slice(None)), acc0[0:1, :])
                update(i, (slice(1, 2), slice(None)), acc[64:65, :])
            elif name in ("conv_a_w", "ssd_conv_w"):
                for s in range(4):
                    @pl.when(chip == s)
                    def _(i=i, name=name, s=s):
                        for l in range(2):
                            (_, rows, lanes), = _small_slices(name, s)
                            update(i, (l,), acc[rows.start + 32 * l:rows.stop + 32 * l, lanes])
            else:
                for l in range(2):
                    for idx, rows, lanes in _small_slices(name, 0):
                        g = acc[rows.start + 32 * l:rows.stop + 32 * l, lanes]
                        if w_refs[i].ndim == 2:
                            update(i, (slice(l, l + 1), slice(None)), g)
                        else:
                            update(i, (l,) + idx, g)

    args = [a for pair in blocks for a in pair] + [d[k] for d in (w, m, v) for k in _SMALL_NAMES]
    shapes = [jax.ShapeDtypeStruct(w[k].shape, F32) for k in _SMALL_NAMES]
    vmem = pl.BlockSpec(memory_space=pltpu.VMEM)
    outs = pl.pallas_call(body, name="small_adamw", in_specs=[vmem] * len(args), out_specs=[vmem] * (4 * n + 1),
                          out_shape=shapes * 4 + [jax.ShapeDtypeStruct((1, 1), F32)],
                          scratch_shapes=[pltpu.VMEM((SMALL_ROWS, D), F32), pltpu.VMEM((8, D), F32)])(*args)
    return outs[0:n], outs[n:2 * n], outs[2 * n:3 * n], outs[3 * n:4 * n], outs[4 * n]


def _mixer_consts(layer, conv_a_w, gla_gate_w, gla_gate_b, gla_norm_w, pool_w, pool_scale, ssd_conv_w, ssd_conv_b,
                  ssd_dt_bias, ssd_a_log, ssd_d, ssd_norm_w):
    def row(v):
        return jnp.pad(v.reshape(1, -1), ((0, 0), (0, 768 - v.size)))

    dtb = jnp.pad(ssd_dt_bias[layer], (16, 108))
    rows = [jnp.pad(conv_a_w[layer], ((0, 0), (0, 512))), row(gla_gate_b[layer]), row(jnp.tile(gla_norm_w[layer], 4)),
            row(pool_scale[layer]), row(ssd_conv_b[layer]), row(dtb), row(jnp.repeat(-jnp.exp(ssd_a_log[layer]), 64)),
            row(jnp.repeat(ssd_d[layer], 64)), row(ssd_norm_w[layer]), jnp.zeros((1, 768), F32), ssd_conv_w[layer]]
    prm = jnp.concatenate(rows, axis=0)
    gw = jnp.pad(gla_gate_w[layer], ((0, 112), (0, 0))).astype(BF16)
    on_diag = (_iota((256, 256), 0) >> 6) == (_iota((256, 256), 1) >> 6)
    pw = jnp.where(on_diag, jnp.tile(pool_w[layer].reshape(256, 64), (1, 4)), 0.0)
    return (prm, gw, pw.astype(BF16)) + _mixer_matrices()


def _grad_slabs(dwp, dwo):
    return dwp.reshape(1, D, NP), dwo.reshape(4, D // 4, D)


class _Comm:
    def __init__(self, w_in, w_out):
        self.w_in16 = jnp.pad(w_in.astype(BF16), ((0, 0), (0, 0), (0, SHARD_PAD - SHARD)))
        self.w_out16 = w_out.astype(BF16)
        self.core = lax.axis_index("c").astype(jnp.int32).reshape(1)
        self.chip = 2 * lax.axis_index("x") + lax.axis_index("y")
        self.place = jnp.stack([lax.axis_index("c"), self.chip]).astype(jnp.int32)

    def gather_ici(self, layer):
        return _rider_gather_ici((self.w_in16[layer], self.w_out16[layer]))

    def pair_sum(self, layer, slabs, received):
        d_in, d_out = [_pair_sum(self.core, a, b, name=f"reduce_pair_sum{layer}_{k}")
                       for k, (a, b) in enumerate(zip(slabs, received))]
        return [_split_dw_in(d_in[0], name=f"split_dw_in{layer}"), d_out]

    def chip_sum(self, layer, gathered, mine):
        return [_chip_sum(self.place, a, b, name=f"reduce_chip_sum{layer}_{k}") for k, (a, b) in enumerate(zip(gathered, mine))]

    def layer_weights(self, layer, s_in, s_out):
        own = lambda slabs, shard: jnp.stack([jnp.where(self.chip == s, shard, slabs[s]) for s in range(4)])
        wp, wpt = _assemble_w_in(own(s_in, self.w_in16[layer]), name=f"assemble_w_in{layer}")
        wo = own(s_out, self.w_out16[layer]).reshape(D, D)
        return wp, wpt, wo, wo.T


def _local_step(x, tgt, norm_w, final_norm_w, consts, wts0, wts1=None, comm=None):
    nw = [norm_w[l:l + 1] for l in range(2)]
    proj0, h0, slabs = _rmsproj(x, nw[0], wts0[0], name="rmsproj0", rider=comm and comm.gather_ici(1))
    (mix0, sg0, ss0, x1, *conv0), slabs = _mixer_fwd(proj0, x, wts0[2], *consts[0], name="mixer_fwd0",
                                                     rider=comm and _rider_gather_d2d(slabs))
    if comm:
        wts1 = comm.layer_weights(1, *slabs)
    proj1, h1, _ = _rmsproj(x1, nw[1], wts1[0], name="rmsproj1")
    (mix1, sg1, ss1, dx, *conv1, head), _ = _mixer_fwd(proj1, x1, wts1[2], *consts[1], name="mixer_fwd1",
                                                       head=(tgt, final_norm_w.reshape(1, D)))
    (dproj, mgr1, dwo1), _ = _mixer_bwd(proj1, dx, wts1[3], mix1, sg1, ss1, *conv1, *consts[1], name="mixer_bwd1")
    dwp1, _ = _dwin(h1, dproj, name="dwin1")
    slabs1 = comm and _grad_slabs(dwp1, dwo1)
    (dx, dnw1), recv = _dxin(dproj, wts1[1], x1, dx, nw[1], name="dxin1", rider=comm and _rider_swap(slabs1))
    pairs1 = comm and comm.pair_sum(1, slabs1, recv)
    (dproj, mgr0, dwo0), gathered = _mixer_bwd(proj0, dx, wts0[3], mix0, sg0, ss0, *conv0, *consts[0], name="mixer_bwd0",
                                               rider=comm and _rider_scatter(pairs1))
    if not comm:
        dwp0, _ = _dwin(h0, dproj, name="dwin0")
        (dx, dnw0), _ = _dxin(dproj, wts0[1], x, dx, nw[0], name="dxin0")
        return head, dx, ((dwp0, dwp1), (dwo0, dwo1)), (dnw0, dnw1), (mgr0, mgr1)
    dwo0 = dwo0.reshape(4, D // 4, D)
    small = jnp.concatenate([mgr0, mgr1, dnw1, head], axis=0)
    dwp0, (*big1, recv_out, got_small) = _dwin(h0, dproj, name="dwin0", rider=_join_riders(_join_riders(
        _rider_share(comm.chip_sum(1, gathered, pairs1)), _rider_swap((dwo0,))), _rider_exchange(small)))
    slabs0 = (dwp0.reshape(1, D, NP), dwo0)
    pairs0 = comm.pair_sum(0, slabs0, (_run_rider(_rider_swap(slabs0[0:1]), "reduce_swap0")[0], recv_out))
    (dx, dnw0), gathered = _dxin(dproj, wts0[1], x, dx, nw[0], name="dxin0", rider=_rider_scatter(pairs0))
    last = _run_rider(_join_riders(_rider_share(comm.chip_sum(0, gathered, pairs0)), _rider_exchange(dnw0)),
                      "reduce_share0")
    return dx, ((last[0], big1[0]), (last[1], big1[1])), ((small, got_small), (dnw0, last[2]))


def kernel(x, norm_w, w_in, conv_a_w, gla_gate_w, gla_gate_b, gla_norm_w, pool_w, pool_scale, ssd_conv_w, ssd_conv_b, ssd_dt_bias, ssd_a_log, ssd_d, ssd_norm_w, w_out, final_norm_w, loss_target, m_norm_w, m_w_in, m_conv_a_w, m_gla_gate_w, m_gla_gate_b, m_gla_norm_w, m_pool_w, m_pool_scale, m_ssd_conv_w, m_ssd_conv_b, m_ssd_dt_bias, m_ssd_a_log, m_ssd_d, m_ssd_norm_w, m_w_out, m_final_norm_w, v_norm_w, v_w_in, v_conv_a_w, v_gla_gate_w, v_gla_gate_b, v_gla_norm_w, v_pool_w, v_pool_scale, v_ssd_conv_w, v_ssd_conv_b, v_ssd_dt_bias, v_ssd_a_log, v_ssd_d, v_ssd_norm_w, v_w_out, v_final_norm_w):
    weights = dict(norm_w=norm_w, w_in=w_in, conv_a_w=conv_a_w, gla_gate_w=gla_gate_w, gla_gate_b=gla_gate_b,
                   gla_norm_w=gla_norm_w, pool_w=pool_w, pool_scale=pool_scale, ssd_conv_w=ssd_conv_w,
                   ssd_conv_b=ssd_conv_b, ssd_dt_bias=ssd_dt_bias, ssd_a_log=ssd_a_log, ssd_d=ssd_d,
                   ssd_norm_w=ssd_norm_w, w_out=w_out, final_norm_w=final_norm_w)
    m_in = dict(norm_w=m_norm_w, w_in=m_w_in, conv_a_w=m_conv_a_w, gla_gate_w=m_gla_gate_w, gla_gate_b=m_gla_gate_b,
                gla_norm_w=m_gla_norm_w, pool_w=m_pool_w, pool_scale=m_pool_scale, ssd_conv_w=m_ssd_conv_w,
                ssd_conv_b=m_ssd_conv_b, ssd_dt_bias=m_ssd_dt_bias, ssd_a_log=m_ssd_a_log, ssd_d=m_ssd_d,
                ssd_norm_w=m_ssd_norm_w, w_out=m_w_out, final_norm_w=m_final_norm_w)
    v_in = dict(norm_w=v_norm_w, w_in=v_w_in, conv_a_w=v_conv_a_w, gla_gate_w=v_gla_gate_w, gla_gate_b=v_gla_gate_b,
                gla_norm_w=v_gla_norm_w, pool_w=v_pool_w, pool_scale=v_pool_scale, ssd_conv_w=v_ssd_conv_w,
                ssd_conv_b=v_ssd_conv_b, ssd_dt_bias=v_ssd_dt_bias, ssd_a_log=v_ssd_a_log, ssd_d=v_ssd_d,
                ssd_norm_w=v_ssd_norm_w, w_out=v_w_out, final_norm_w=v_final_norm_w)
    order = ("norm_w", "w_in", "conv_a_w", "gla_gate_w", "gla_gate_b", "gla_norm_w", "pool_w", "pool_scale",
             "ssd_conv_w", "ssd_conv_b", "ssd_dt_bias", "ssd_a_log", "ssd_d", "ssd_norm_w", "w_out", "final_norm_w")
    t = x.shape[1]

    comm = _Comm(w_in, w_out)
    cshard = jnp.zeros((16, 256), F32)
    for l in range(2):
        cshard = cshard.at[8 * l:8 * l + 3, 0:64].set(conv_a_w[l]).at[8 * l + 3:8 * l + 7, 0:192].set(ssd_conv_w[l])
    s_in, s_out, g_c = _gather_ici_two_hops((comm.w_in16[0], comm.w_out16[0]), cshard)
    g_c = [jnp.where(comm.chip == s, cshard, g_c[s]) for s in range(4)]
    conv_a_full = jnp.stack([jnp.concatenate([g_c[s][8 * l:8 * l + 3, 0:64] for s in range(4)], axis=-1) for l in range(2)])
    ssd_conv_full = jnp.stack([jnp.concatenate([g_c[s][8 * l + 3:8 * l + 7, 0:192] for s in range(4)], axis=-1)
                               for l in range(2)])
    consts = [_mixer_consts(l, conv_a_full, gla_gate_w, gla_gate_b, gla_norm_w, pool_w, pool_scale, ssd_conv_full,
                            ssd_conv_b, ssd_dt_bias, ssd_a_log, ssd_d, ssd_norm_w) for l in range(2)]

    dx, big, blocks = _local_step(x.reshape(t, D), loss_target.reshape(t, D), norm_w, final_norm_w, consts,
                                  comm.layer_weights(0, s_in, s_out), comm=comm)

    as2d = lambda d: {k: (d[k].reshape(1, D) if k == "final_norm_w" else d[k]) for k in _SMALL_NAMES}
    small = _small_adamw(blocks, as2d(weights), as2d(m_in), as2d(v_in))
    grads, delta, new_m, new_v = ({k: (a.reshape(D) if k == "final_norm_w" else a) for k, a in zip(_SMALL_NAMES, part)}
                                  for part in small[0:4])
    loss = small[4].reshape(())

    grads["w_out"] = jnp.stack(big[1])

    grads["w_in"], delta["w_in"], new_m["w_in"], new_v["w_in"] = _adamw_w_in(w_in, big[0], m_w_in, v_w_in, name="adamw_w_in")
    delta["w_out"], new_m["w_out"], new_v["w_out"] = _adamw(w_out, grads["w_out"], m_w_out, v_w_out, name="adamw_w_out", br=256)

    return (loss, dx.reshape(1, t, D), *[grads[k] for k in order], *[delta[k] for k in order],
            *[new_m[k] for k in order], *[new_v[k] for k in order])
```

```python
import functools

import jax
import jax.numpy as jnp
from jax import lax
from jax.experimental import pallas as pl
from jax.experimental.pallas import tpu as pltpu

F32 = jnp.float32
BF16 = jnp.bfloat16
MESH = pl.DeviceIdType.MESH

D = 1024
CH = 64
EPS = 1e-6
NP = 3456
NPROJ = 3348
NPM = 3328
GLA_SCALE = 32.0 ** -0.5
INV_TAU = 1.0 / 16.0
TB = 512
NCH = TB // CH
assert TB % 256 == 0

C_AH, C_AB, C_AC, C_AZ, C_GQ, C_GK, C_GV = 0, 256, 512, 768, 1024, 1152, 1280
C_GZ, C_PU, C_PZ, C_SZ, C_SX, C_TL = 1536, 1792, 2048, 2304, 2560, 3328
_PERM = ((0, 1536), (1552, 1792), (1536, 16), (3344, 4))

R_CAW, R_GB, R_GNW, R_PSC, R_SCB, R_DTB, R_AE, R_DE, R_SNW, R_SCW = 0, 3, 4, 5, 6, 7, 8, 9, 10, 12

ADAM_LR, ADAM_B1, ADAM_B2, ADAM_EPS, ADAM_WD, ADAM_STEP = 0.001, 0.9, 0.999, 1e-08, 0.01, 10

VMEM_LIMIT = 56 * 1024 * 1024


def _cparams(sem, limit=VMEM_LIMIT):
    return pltpu.CompilerParams(dimension_semantics=sem, vmem_limit_bytes=limit)


_ANY = pl.BlockSpec(memory_space=pl.ANY)


def _place():
    return lax.axis_index("x"), lax.axis_index("y"), lax.axis_index("c")


class _Rider:
    def __init__(self, inputs, out_shapes, sems, start, finish, aliases=None):
        self.inputs, self.out_shapes, self.sems = tuple(inputs), tuple(out_shapes), tuple(sems)
        self.start, self.finish, self.aliases = start, finish, dict(aliases or {})


def _call(body, args, *, grid, in_specs, out_specs, out_shape, name, sem, scratch_shapes=(), rider=None):
    if rider is None:
        outs = pl.pallas_call(body, grid=grid, name=name, in_specs=list(in_specs), out_specs=list(out_specs),
                              out_shape=list(out_shape), scratch_shapes=list(scratch_shapes),
                              compiler_params=_cparams(sem))(*args)
        return list(outs), []
    ni, no, ns = len(args), len(out_shape), len(scratch_shapes)
    ri, ro = len(rider.inputs), len(rider.out_shapes)

    def full(*refs):
        ins, rins = refs[:ni], refs[ni:ni + ri]
        outs, routs = refs[ni + ri:ni + ri + no], refs[ni + ri + no:ni + ri + no + ro]
        scr, rsem = refs[ni + ri + no + ro:ni + ri + no + ro + ns], refs[ni + ri + no + ro + ns:]
        first = functools.reduce(jnp.logical_and, [pl.program_id(a) == 0 for a in range(len(grid))])
        last = functools.reduce(jnp.logical_and, [pl.program_id(a) == grid[a] - 1 for a in range(len(grid))])

        @pl.when(first)
        def _():
            rider.start(rins, routs, rsem)

        body(*ins, *outs, *scr)

        @pl.when(last)
        def _():
            rider.finish(rins, routs, rsem)

    outs = pl.pallas_call(
        full, grid=grid, name=name, in_specs=list(in_specs) + [_ANY] * ri, out_specs=list(out_specs) + [_ANY] * ro,
        out_shape=list(out_shape) + list(rider.out_shapes), scratch_shapes=list(scratch_shapes) + list(rider.sems),
        input_output_aliases={ni + k: no + v for k, v in rider.aliases.items()},
        compiler_params=_cparams(("arbitrary",) * len(grid)))(*args, *rider.inputs)
    return list(outs[:no]), list(outs[no:])


def _run_rider(rider, name):
    ri = len(rider.inputs)

    def body(*refs):
        rins, routs, rsem = refs[:ri], refs[ri:ri + len(rider.out_shapes)], refs[ri + len(rider.out_shapes):]
        rider.start(rins, routs, rsem)
        rider.finish(rins, routs, rsem)

    return list(pl.pallas_call(body, name=name, in_specs=[_ANY] * ri, out_specs=[_ANY] * len(rider.out_shapes),
                               out_shape=list(rider.out_shapes), scratch_shapes=list(rider.sems),
                               input_output_aliases=dict(rider.aliases))(*rider.inputs))


def _dot(a, b):
    return jnp.dot(a.astype(BF16), b.astype(BF16), preferred_element_type=F32)


def _dot_nt(a, b):
    return lax.dot_general(a.astype(BF16), b.astype(BF16), (((1,), (1,)), ((), ())), preferred_element_type=F32)


def _dot_tn(a, b):
    return lax.dot_general(a.astype(BF16), b.astype(BF16), (((0,), (0,)), ((), ())), preferred_element_type=F32)


def _split(a):
    hi = a.astype(BF16)
    lo = (a - hi.astype(F32)).astype(BF16)
    return hi, lo


def _dot2_l(a, b):
    hi, lo = _split(a)
    return _dot(jnp.concatenate([hi, lo], axis=1), jnp.concatenate([b, b], axis=0))


def _dot2_r(a, b):
    hi, lo = _split(b)
    return _dot(jnp.concatenate([a, a], axis=1), jnp.concatenate([hi, lo], axis=0))


def _dot3_l(a, b):
    hi, lo = _split(a)
    lo2 = ((a - hi.astype(F32)) - lo.astype(F32)).astype(BF16)
    return _dot(hi, b) + _dot(lo, b) + _dot(lo2, b)


def _dot2_nt(a, b):
    hi, lo = _split(a)
    return _dot_nt(jnp.concatenate([hi, lo], axis=1), jnp.concatenate([b, b], axis=1))


def _silu(z):
    return z * jax.nn.sigmoid(z)


def _lse1(x):
    return jnp.log(1.0 + jnp.exp(-jnp.abs(x)))


def _cs(a):
    return jnp.sum(a, axis=0, keepdims=True)


def _iota(shape, dim):
    return lax.broadcasted_iota(jnp.int32, shape, dim)


def _mixer_matrices():
    r, c = _iota((256, 256), 0), _iota((256, 256), 1)
    same_chunk = (r >> 6) == (c >> 6)
    mats = jnp.stack([jnp.where((c > r) & same_chunk, 1.0, 0.0), jnp.where((c < r) & same_chunk, 1.0, 0.0),
                      jnp.where(same_chunk, 1.0 / 64.0, 0.0), jnp.where((r < 128) & (r - 16 == (c >> 6)), 1.0, 0.0)])
    mask = jnp.where((_iota((256, 128), 0) >> 6) == (_iota((256, 128), 1) >> 5), 1.0, 0.0)
    return mats.astype(BF16), mask.astype(F32)


def _dn(ext, k, n, h):
    return pltpu.roll(ext, k, axis=0)[h:h + n]


def _up(ext, k, n):
    return pltpu.roll(ext, ext.shape[0] - k, axis=0)[:n]


def _pool_lane_select(lane, s2, s4, s8, s16):
    return jnp.where(lane < 64, s2, jnp.where(lane < 128, s4, jnp.where(lane < 192, s8, s16)))


def _winsum_dn(ext, lane):
    s2 = ext + pltpu.roll(ext, 1, axis=0)
    s4 = s2 + pltpu.roll(s2, 2, axis=0)
    s8 = s4 + pltpu.roll(s4, 4, axis=0)
    s16 = s8 + pltpu.roll(s8, 8, axis=0)
    return _pool_lane_select(lane, s2, s4, s8, s16)


def _winsum_up(ext, lane):
    m = ext.shape[0]
    s2 = ext + pltpu.roll(ext, m - 1, axis=0)
    s4 = s2 + pltpu.roll(s2, m - 2, axis=0)
    s8 = s4 + pltpu.roll(s4, m - 4, axis=0)
    s16 = s8 + pltpu.roll(s8, m - 8, axis=0)
    return _pool_lane_select(lane, s2, s4, s8, s16)


def _pool_inv_count(tile, n):
    lane = _iota((1, 256), 1)
    win = _pool_lane_select(lane, 2.0, 4.0, 8.0, 16.0).astype(F32)
    tpos = (tile * n + _iota((n, 1), 0) + 1).astype(F32)
    return jnp.where(tpos >= win, 1.0 / win, 1.0 / tpos)


def _silu_pair(z):
    s = jax.nn.sigmoid(z)
    return z * s, s * (1.0 + z * (1.0 - s))


def _chunks(a):
    return [a[c * CH:(c + 1) * CH] for c in range(a.shape[0] // CH)]


def _halves(fn, a, b):
    return jnp.concatenate([fn(a[:, 0:128], b[:, 0:128]), fn(a[:, 128:256], b[:, 128:256])], axis=1)


def _chunk_sums(tri, a):
    return jnp.concatenate([_dot2_r(tri, a[r:r + 256]) for r in range(0, a.shape[0], 256)], axis=0)


def _mixer_tile_prep(p_ref, t_ref, xc, prm_ref, gw_v, cm_ref, mk_ref):
    tail = t_ref[...]
    pre = _dot(tail, gw_v) + prm_ref[R_GB:R_GB + 1, 0:128]
    la = (jnp.minimum(pre, 0.0) - _lse1(pre)) * INV_TAU
    dtin = tail + prm_ref[R_DTB:R_DTB + 1, 0:128]
    dtf = jnp.maximum(dtin, 0.0) + _lse1(dtin)
    dte = _dot2_l(dtf, cm_ref[3, 0:128, :])
    da = dte * prm_ref[R_AE:R_AE + 1, 0:256]
    rev = _chunk_sums(cm_ref[0], jnp.concatenate([la, da], axis=1))
    dec = jnp.exp(rev[:, 0:128])
    kd = p_ref[:, C_GK:C_GK + 128].astype(F32) * dec
    wdec = jnp.exp(rev[:, 128:384])
    w = wdec * dte
    xw = xc[:, 0:256] * w
    d_s = [jnp.exp(_cs(a)) for a in _chunks(la)]
    et = [jnp.exp(_cs(a)) for a in _chunks(da)]
    mask_t = mk_ref[...]
    ut_g = [_dot_tn(v, k) * mask_t for v, k in zip(_chunks(p_ref[:, C_GV:C_GV + 256].astype(F32)), _chunks(kd))]
    ut_s = [_halves(_dot_tn, b, x) for b, x in zip(_chunks(xc[:, 256:512]), _chunks(xw))]
    return tail, pre, dtin, dte, dec, kd, wdec, w, xw, d_s, et, ut_g, ut_s


def _rmsproj(x, nw, wp, name, tm=512, rider=None):
    t = x.shape[0]

    def body(x_ref, nw_ref, w_ref, o_ref, t_ref, h_ref):
        xv = x_ref[...]
        rs = lax.rsqrt(jnp.mean(xv * xv, axis=-1, keepdims=True) + EPS)
        h = (xv * rs * nw_ref[...]).astype(BF16)
        h_ref[...] = h
        proj = jnp.dot(h, w_ref[...], preferred_element_type=F32)
        o_ref[...] = proj[:, 0:NPM].astype(BF16)
        t_ref[...] = proj[:, NPM:NP]

    (proj, tail, h), extra = _call(
        body, (x, nw, wp), grid=(t // tm,), name=name, sem=("parallel",), rider=rider,
        in_specs=[pl.BlockSpec((tm, D), lambda i: (i, 0)), pl.BlockSpec((1, D), lambda i: (0, 0)),
                  pl.BlockSpec((D, NP), lambda i: (0, 0))],
        out_specs=[pl.BlockSpec((tm, NPM), lambda i: (i, 0)), pl.BlockSpec((tm, NP - NPM), lambda i: (i, 0)),
                   pl.BlockSpec((tm, D), lambda i: (i, 0))],
        out_shape=[jax.ShapeDtypeStruct((t, NPM), BF16), jax.ShapeDtypeStruct((t, NP - NPM), F32),
                   jax.ShapeDtypeStruct((t, D), BF16)])
    return (proj, tail), h, extra


def _head_tile(xv, tgt, w):
    rs = lax.rsqrt(jnp.mean(xv * xv, axis=-1, keepdims=True) + EPS)
    xh = xv * rs
    err = xh * w - tgt
    dy = err * (1.0 / D)
    dxh = dy * w
    dx = rs * (dxh - xh * jnp.mean(dxh * xh, axis=-1, keepdims=True))
    return dx, _cs(dy * xh), (0.5 / D) * jnp.sum(err * err)


def _dxin(dp, wpt, x, dxn, nw, name, tm=512, rider=None):
    t = x.shape[0]

    def body(dp_ref, w_ref, x_ref, dxn_ref, nw_ref, dx_ref, dnw_ref):
        @pl.when(pl.program_id(0) == 0)
        def _():
            dnw_ref[...] = jnp.zeros_like(dnw_ref)

        dh = jnp.dot(dp_ref[...], w_ref[...], preferred_element_type=F32)
        xv = x_ref[...]
        rs = lax.rsqrt(jnp.mean(xv * xv, axis=-1, keepdims=True) + EPS)
        xh = xv * rs
        dnw_ref[0:1, :] += _cs(dh * xh)
        dxh = dh * nw_ref[...]
        dx_ref[...] = dxn_ref[...] + rs * (dxh - xh * jnp.mean(dxh * xh, axis=-1, keepdims=True))

    return _call(
        body, (dp, wpt, x, dxn, nw), grid=(t // tm,), name=name, sem=("arbitrary",), rider=rider,
        in_specs=[pl.BlockSpec((tm, NP), lambda i: (i, 0)), pl.BlockSpec((NP, D), lambda i: (0, 0)),
                  pl.BlockSpec((tm, D), lambda i: (i, 0)), pl.BlockSpec((tm, D), lambda i: (i, 0)),
                  pl.BlockSpec((1, D), lambda i: (0, 0))],
        out_specs=[pl.BlockSpec((tm, D), lambda i: (i, 0)), pl.BlockSpec((8, D), lambda i: (0, 0))],
        out_shape=[jax.ShapeDtypeStruct((t, D), F32), jax.ShapeDtypeStruct((8, D), F32)])


def _dwin(h, dp, name, tm=1024, rider=None):
    t = h.shape[0]

    def body(h_ref, dp_ref, o_ref):
        @pl.when(pl.program_id(0) == 0)
        def _():
            o_ref[...] = jnp.zeros_like(o_ref)

        o_ref[...] += _dot_tn(h_ref[...], dp_ref[...])

    (dwp,), extra = _call(
        body, (h, dp), grid=(t // tm,), name=name, sem=("arbitrary",), rider=rider,
        in_specs=[pl.BlockSpec((tm, D), lambda i: (i, 0)), pl.BlockSpec((tm, NP), lambda i: (i, 0))],
        out_specs=[pl.BlockSpec((D, NP), lambda i: (0, 0))], out_shape=[jax.ShapeDtypeStruct((D, NP), F32)])
    return dwp, extra


def _mixer_fwd(proj, x, wo, prm, gw, pw, cmat, mask, name, rider=None, head=None):
    proj, tail = proj
    t = proj.shape[0]
    nt, nc = t // TB, t // CH

    def body(p_ref, t_ref, x_ref, wo_ref, prm_ref, gw_ref, pw_ref, cm_ref, mk_ref, *rest):
        (tgt_ref, fw_ref), rest = (rest[:2], rest[2:]) if head else ((None, None), rest)
        mix_ref, sg_ref, ss_ref, xn_ref, xc_ref, dxc_ref, cv_ref, pool_ref = rest[:8]
        acc_ref = rest[8] if head else None
        sg_s, ss_s, h_ua, h_pu, h_sx = rest[-5:]
        i = pl.program_id(0)

        @pl.when(i == 0)
        def _():
            for r in (sg_s, ss_s, h_ua, h_pu, h_sx) + ((acc_ref,) if head else ()):
                r[...] = jnp.zeros_like(r)

        lane = _iota((1, 256), 1)
        u = p_ref[:, C_AC:C_AC + 256].astype(F32) * p_ref[:, C_AH:C_AH + 256].astype(F32)
        ext = jnp.concatenate([h_ua[...], u], axis=0)
        cv = (prm_ref[R_CAW + 2:R_CAW + 3, 0:256] * u + prm_ref[R_CAW + 1:R_CAW + 2, 0:256] * _dn(ext, 1, TB, 8)
              + prm_ref[R_CAW:R_CAW + 1, 0:256] * _dn(ext, 2, TB, 8))
        cv_ref[...] = cv.astype(BF16)
        mix_ref[:, 0:256] = (p_ref[:, C_AB:C_AB + 256].astype(F32) * cv * _silu(p_ref[:, C_AZ:C_AZ + 256].astype(F32))).astype(BF16)
        h_ua[...] = u[TB - 8:, :]
        pu = p_ref[:, C_PU:C_PU + 256].astype(F32)
        ext = jnp.concatenate([h_pu[...], pu], axis=0)
        pooled = (_winsum_dn(ext, lane)[16:] * _pool_inv_count(i, TB) - pu).astype(BF16)
        pool_ref[...] = pooled
        mixed = jnp.dot(pooled, pw_ref[...], preferred_element_type=F32)
        mix_ref[:, 512:768] = (prm_ref[R_PSC:R_PSC + 1, 0:256] * mixed * _silu(p_ref[:, C_PZ:C_PZ + 256].astype(F32))).astype(BF16)
        h_pu[...] = pu[TB - 16:, :]
        sx = p_ref[:, C_SX:C_SX + 768].astype(F32)
        ext = jnp.concatenate([h_sx[...], sx], axis=0)
        xc, dxc = _silu_pair(prm_ref[R_SCW + 3:R_SCW + 4, :] * sx + prm_ref[R_SCW + 2:R_SCW + 3, :] * _dn(ext, 1, TB, 8)
                             + prm_ref[R_SCW + 1:R_SCW + 2, :] * _dn(ext, 2, TB, 8)
                             + prm_ref[R_SCW:R_SCW + 1, :] * _dn(ext, 3, TB, 8) + prm_ref[R_SCB:R_SCB + 1, :])
        xc_ref[...] = xc.astype(BF16)
        dxc_ref[...] = dxc.astype(BF16)
        h_sx[...] = sx[TB - 8:, :]

        _, _, _, _, _, _, _, _, _, d_s, et, ut_g, ut_s = _mixer_tile_prep(p_ref, t_ref, xc, prm_ref, gw_ref[...], cm_ref, mk_ref)
        s_g, s_s = sg_s[...], ss_s[...]
        o, y = [], []
        qs = _chunks(p_ref[:, C_GQ:C_GQ + 128].astype(F32) * GLA_SCALE)
        cm = _chunks(xc[:, 512:768])
        for c in range(NCH):
            sg_ref[c] = s_g
            ss_ref[c] = s_s
            s_g = s_g * d_s[c] + ut_g[c]
            s_s = s_s * et[c] + ut_s[c]
            o.append(_dot_nt(qs[c], s_g))
            y.append(_halves(_dot, cm[c], s_s))
        sg_s[...] = s_g
        ss_s[...] = s_s
        o = jnp.concatenate(o, axis=0)
        on = o * lax.rsqrt(_dot2_l(o * o, cm_ref[2]) + EPS)
        mix_ref[:, 256:512] = (on * prm_ref[R_GNW:R_GNW + 1, 0:256] * _silu(p_ref[:, C_GZ:C_GZ + 256].astype(F32))).astype(BF16)
        y2 = ((jnp.concatenate(y, axis=0) + prm_ref[R_DE:R_DE + 1, 0:256] * xc[:, 0:256])
              * _silu(p_ref[:, C_SZ:C_SZ + 256].astype(F32)))
        mix_ref[:, 768:1024] = (y2 * lax.rsqrt(jnp.mean(y2 * y2, axis=-1, keepdims=True) + EPS)
                                * prm_ref[R_SNW:R_SNW + 1, 0:256]).astype(BF16)
        xn = x_ref[...] + jnp.dot(mix_ref[...], wo_ref[...], preferred_element_type=F32)
        if head:
            xn_ref[...], dfw, loss = _head_tile(xn, tgt_ref[...], fw_ref[...])
            acc_ref[0:1, :] += dfw
            acc_ref[1:2, :] += jnp.zeros((1, D), F32) + loss
        else:
            xn_ref[...] = xn

    row = pl.BlockSpec((TB, D), lambda i: (i, 0))
    return _call(
        body, (proj, tail, x, wo, prm, gw, pw, cmat, mask) + tuple(head or ()), grid=(nt,), name=name, sem=("arbitrary",),
        rider=rider,
        in_specs=[pl.BlockSpec((TB, NPM), lambda i: (i, 0)), pl.BlockSpec((TB, NP - NPM), lambda i: (i, 0)), row,
                  pl.BlockSpec((D, D), lambda i: (0, 0)), pl.BlockSpec((16, 768), lambda i: (0, 0)),
                  pl.BlockSpec((128, 128), lambda i: (0, 0)), pl.BlockSpec((256, 256), lambda i: (0, 0)),
                  pl.BlockSpec((4, 256, 256), lambda i: (0, 0, 0)), pl.BlockSpec((256, 128), lambda i: (0, 0))]
        + ([row, pl.BlockSpec((1, D), lambda i: (0, 0))] if head else []),
        out_specs=[row, pl.BlockSpec((NCH, 256, 128), lambda i: (i, 0, 0)),
                   pl.BlockSpec((NCH, 128, 256), lambda i: (i, 0, 0)), row] + [pl.BlockSpec((TB, 768), lambda i: (i, 0))] * 2
        + [pl.BlockSpec((TB, 256), lambda i: (i, 0))] * 2 + ([pl.BlockSpec((8, D), lambda i: (0, 0))] if head else []),
        out_shape=[jax.ShapeDtypeStruct((t, D), BF16), jax.ShapeDtypeStruct((nc, 256, 128), F32),
                   jax.ShapeDtypeStruct((nc, 128, 256), F32), jax.ShapeDtypeStruct((t, D), F32)]
        + [jax.ShapeDtypeStruct((t, 768), BF16)] * 2 + [jax.ShapeDtypeStruct((t, 256), BF16)] * 2
        + ([jax.ShapeDtypeStruct((8, D), F32)] if head else []),
        scratch_shapes=[pltpu.VMEM((256, 128), F32), pltpu.VMEM((128, 256), F32), pltpu.VMEM((8, 256), F32),
                        pltpu.VMEM((16, 256), F32), pltpu.VMEM((8, 768), F32)])


def _mixer_bwd(proj, dxn, wot, mix, sg, ss, xc16, dxc16, cv16, pool16, prm, gw, pw, cmat, mask, name, rider=None):
    proj, tail = proj
    t = proj.shape[0]
    nt = t // TB
    rev = lambda i: nt - 1 - i

    def body(p_ref, t_ref, dxn_ref, wot_ref, mix_ref, sg_ref, ss_ref, xc_ref, dxc_ref, cv_ref, pool_ref, prm_ref, gw_ref,
             pw_ref, cm_ref, mk_ref, dp_ref, sgc_ref, dwo_ref,
             gg_s, gs_s, h_dcv, h_dpl, h_dpre, gsm_ref, dgw_ref, dpw_ref, dm_ref):
        i = pl.program_id(0)
        tile = nt - 1 - i

        @pl.when(i == 0)
        def _():
            for r in (gg_s, gs_s, h_dcv, h_dpl, h_dpre, gsm_ref, dgw_ref, dpw_ref, dwo_ref):
                r[...] = jnp.zeros_like(r)

        dxn = dxn_ref[...].astype(BF16)
        dm_ref[...] = jnp.dot(dxn, wot_ref[...], preferred_element_type=F32)
        dwo_ref[...] += _dot_tn(mix_ref[...], dxn)

        lane = _iota((1, 256), 1)
        ah, ac = p_ref[:, C_AH:C_AH + 256].astype(F32), p_ref[:, C_AC:C_AC + 256].astype(F32)
        ab, az = p_ref[:, C_AB:C_AB + 256].astype(F32), p_ref[:, C_AZ:C_AZ + 256].astype(F32)
        w0, w1, w2 = (prm_ref[R_CAW + j:R_CAW + j + 1, 0:256] for j in range(3))
        u = ac * ah
        cv = cv_ref[...].astype(F32)
        g = dm_ref[:, 0:256]
        sz, dsz = _silu_pair(az)
        dp_ref[:, C_AB:C_AB + 256] = (g * cv * sz).astype(BF16)
        dp_ref[:, C_AZ:C_AZ + 256] = (g * ab * cv * dsz).astype(BF16)
        dcv = g * ab * sz
        dext = jnp.concatenate([dcv, h_dcv[...]], axis=0)
        dcv1, dcv2 = _up(dext, 1, TB), _up(dext, 2, TB)
        du = w2 * dcv + w1 * dcv1 + w0 * dcv2
        dp_ref[:, C_AC:C_AC + 256] = (du * ah).astype(BF16)
        dp_ref[:, C_AH:C_AH + 256] = (du * ac).astype(BF16)
        gsm_ref[R_CAW:R_CAW + 1, 0:256] += _cs(u * dcv2)
        gsm_ref[R_CAW + 1:R_CAW + 2, 0:256] += _cs(u * dcv1)
        gsm_ref[R_CAW + 2:R_CAW + 3, 0:256] += _cs(u * dcv)
        h_dcv[...] = dcv[0:8, :]
        pz = p_ref[:, C_PZ:C_PZ + 256].astype(F32)
        psc = prm_ref[R_PSC:R_PSC + 1, 0:256]
        icnt = _pool_inv_count(tile, TB)
        pooled = pool_ref[...]
        pw_v = pw_ref[...]
        mixed = jnp.dot(pooled, pw_v, preferred_element_type=F32)
        g = dm_ref[:, 512:768]
        sz, dsz = _silu_pair(pz)
        gsm_ref[R_PSC:R_PSC + 1, 0:256] += _cs(g * mixed * sz)
        dp_ref[:, C_PZ:C_PZ + 256] = (g * psc * mixed * dsz).astype(BF16)
        dmixed = g * psc * sz
        dpw_ref[...] += _dot_tn(pooled, dmixed)
        dpooled = _dot_nt(dmixed, pw_v)
        qd = dpooled * icnt
        dext = jnp.concatenate([qd, h_dpl[...]], axis=0)
        dp_ref[:, C_PU:C_PU + 256] = (_winsum_up(dext, lane)[:TB] - dpooled).astype(BF16)
        h_dpl[...] = qd[0:16, :]
        cw = [prm_ref[R_SCW + j:R_SCW + j + 1, :] for j in range(4)]
        xc = xc_ref[...].astype(F32)
        xs, bm, cm = xc[:, 0:256], xc[:, 256:512], xc[:, 512:768]

        gw_v = gw_ref[...]
        tail, pre, dtin, dte, dec, kd, wdec, w, xw, d_s, et, ut_g, ut_s = _mixer_tile_prep(p_ref, t_ref, xc, prm_ref,
                                                                                          gw_v, cm_ref, mk_ref)
        gmean = cm_ref[2]
        mask_t = mk_ref[...]
        gnw = prm_ref[R_GNW:R_GNW + 1, 0:256]
        a_e = prm_ref[R_AE:R_AE + 1, 0:256]
        d_e = prm_ref[R_DE:R_DE + 1, 0:256]
        snw = prm_ref[R_SNW:R_SNW + 1, 0:256]
        sg_in = [sg_ref[c] for c in range(NCH)]
        ss_in = [ss_ref[c] for c in range(NCH)]
        sg_n = [sg_in[c] * d_s[c] + ut_g[c] for c in range(NCH)]
        ss_n = [ss_in[c] * et[c] + ut_s[c] for c in range(NCH)]
        qs = _chunks(p_ref[:, C_GQ:C_GQ + 128].astype(F32) * GLA_SCALE)
        cm_c, bm_c, xw_c, kd_c = _chunks(cm), _chunks(bm), _chunks(xw), _chunks(kd)
        v_c = _chunks(p_ref[:, C_GV:C_GV + 256].astype(F32))
        o = jnp.concatenate([_dot_nt(qs[c], sg_n[c]) for c in range(NCH)], axis=0)
        y = jnp.concatenate([_halves(_dot, cm_c[c], ss_n[c]) for c in range(NCH)], axis=0) + d_e * xs
        gz = p_ref[:, C_GZ:C_GZ + 256].astype(F32)
        r = lax.rsqrt(_dot2_l(o * o, gmean) + EPS)
        on = o * r
        dyb = dm_ref[:, 256:512]
        sz, dsz = _silu_pair(gz)
        dp_ref[:, C_GZ:C_GZ + 256] = (dyb * on * gnw * dsz).astype(BF16)
        tg = dyb * sz
        gsm_ref[R_GNW:R_GNW + 1, 0:256] += _cs(tg * on)
        don = tg * gnw
        do_c = _chunks(r * (don - on * _dot2_l(don * on, gmean)))
        ssz = p_ref[:, C_SZ:C_SZ + 256].astype(F32)
        sil, dsil = _silu_pair(ssz)
        y2 = y * sil
        r = lax.rsqrt(jnp.mean(y2 * y2, axis=-1, keepdims=True) + EPS)
        yn = y2 * r
        dyd = dm_ref[:, 768:1024]
        gsm_ref[R_SNW:R_SNW + 1, 0:256] += _cs(dyd * yn)
        dn = dyd * snw
        dy2 = r * (dn - yn * jnp.mean(dn * yn, axis=-1, keepdims=True))
        dp_ref[:, C_SZ:C_SZ + 256] = (dy2 * y * dsil).astype(BF16)
        dy = dy2 * sil
        gsm_ref[R_DE:R_DE + 1, 0:256] += _cs(dy * xs)
        dy_c = _chunks(dy)
        dq = jnp.concatenate([_dot(do_c[c], sg_n[c]) for c in range(NCH)], axis=0)
        dp_ref[:, C_GQ:C_GQ + 128] = (dq * GLA_SCALE).astype(BF16)
        dcm = jnp.concatenate([_halves(_dot_nt, dy_c[c], ss_n[c]) for c in range(NCH)], axis=0)
        gg = [_dot_tn(do_c[c], qs[c]) * mask_t for c in range(NCH)]
        gs = [_halves(_dot_tn, cm_c[c], dy_c[c]) for c in range(NCH)]
        car_g, car_s = gg_s[...], gs_s[...]
        for c in reversed(range(NCH)):
            gg[c] = gg[c] + car_g
            gs[c] = gs[c] + car_s
            car_g = gg[c] * d_s[c]
            car_s = gs[c] * et[c]
        gg_s[...] = car_g
        gs_s[...] = car_s
        dkd = jnp.concatenate([_dot(v_c[c], gg[c]) for c in range(NCH)], axis=0)
        dp_ref[:, C_GV:C_GV + 256] = jnp.concatenate([_dot_nt(kd_c[c], gg[c]) for c in range(NCH)], axis=0).astype(BF16)
        dp_ref[:, C_GK:C_GK + 128] = (dkd * dec).astype(BF16)
        dbm = jnp.concatenate([_halves(_dot_nt, xw_c[c], gs[c]) for c in range(NCH)], axis=0)
        dxw = jnp.concatenate([_halves(_dot, bm_c[c], gs[c]) for c in range(NCH)], axis=0)
        dxs = dy * d_e + dxw * w
        dw = dxw * xs
        dsuf = _chunk_sums(cm_ref[1], jnp.concatenate([dkd * kd, dw * dte * wdec], axis=1))
        tot_g = jnp.concatenate([jnp.broadcast_to(_cs(gg[c] * sg_in[c]) * d_s[c], (CH, 128)) for c in range(NCH)], axis=0)
        tot_s = jnp.concatenate([jnp.broadcast_to(_cs(gs[c] * ss_in[c]) * et[c], (CH, 256)) for c in range(NCH)], axis=0)
        dpre = (dsuf[:, 0:128] + tot_g) * INV_TAU * jax.nn.sigmoid(-pre)
        dgw_ref[...] += _dot_tn(tail, dpre)
        gsm_ref[R_GB:R_GB + 1, 0:128] += _cs(dpre)
        dda = dsuf[:, 128:384] + tot_s
        gsm_ref[R_AE:R_AE + 1, 0:256] += _cs(dda * dte)
        dtail_s = _dot2_nt(dw * wdec + dda * a_e, cm_ref[3, 0:128, :]) * jax.nn.sigmoid(dtin)
        gsm_ref[R_DTB:R_DTB + 1, 0:128] += _cs(dtail_s)
        dp_ref[:, C_TL:C_TL + 128] = (_dot_nt(dpre, gw_v) + dtail_s).astype(BF16)
        dpre_c = jnp.concatenate([dxs, dbm, dcm], axis=1) * dxc_ref[...].astype(F32)
        dext = jnp.concatenate([dpre_c, h_dpre[...]], axis=0)
        ups = [dpre_c, _up(dext, 1, TB), _up(dext, 2, TB), _up(dext, 3, TB)]
        dp_ref[:, C_SX:C_SX + 768] = (cw[3] * ups[0] + cw[2] * ups[1] + cw[1] * ups[2] + cw[0] * ups[3]).astype(BF16)
        sx = p_ref[:, C_SX:C_SX + 768].astype(F32)
        for k in range(4):
            gsm_ref[R_SCW + k:R_SCW + k + 1, :] += _cs(sx * ups[3 - k])
        gsm_ref[R_SCB:R_SCB + 1, :] += _cs(dpre_c)
        h_dpre[...] = dpre_c[0:8, :]

        @pl.when(i == nt - 1)
        def _():
            ri, ci = _iota((256, 256), 0), _iota((256, 256), 1)
            per_head = jnp.where((ri >> 6) == ci, 1.0, 0.0).astype(BF16)
            per_dv = jnp.where((ri & 63) == ci, 1.0, 0.0).astype(BF16)
            row = _iota((8, 256), 0)
            top = gsm_ref[0:8, 0:256]
            sgc_ref[0:8, 0:256] = jnp.where(row == R_GNW, _dot3_l(top, per_dv), top)
            bot = gsm_ref[8:16, 0:256]
            fold = _dot3_l(jnp.where(row == R_AE - 8, bot * a_e, bot), per_head)
            sgc_ref[8:16, 0:256] = jnp.where((row == R_AE - 8) | (row == R_DE - 8), fold, bot)
            sgc_ref[0:16, 256:768] = gsm_ref[:, 256:768]
            sgc_ref[0:16, 768:896] = dgw_ref[0:16, :]
            sgc_ref[0:16, 896:1024] = jnp.zeros((16, 128), F32)
            diag = _pool_lane_select(lane, dpw_ref[0:64, :], dpw_ref[64:128, :], dpw_ref[128:192, :], dpw_ref[192:256, :])
            for q in range(4):
                sgc_ref[16:32, 256 * q:256 * q + 256] = diag[16 * q:16 * q + 16, :]

    return _call(
        body, (proj, tail, dxn, wot, mix, sg, ss, xc16, dxc16, cv16, pool16, prm, gw, pw, cmat, mask), grid=(nt,), name=name,
        sem=("arbitrary",), rider=rider,
        in_specs=[pl.BlockSpec((TB, NPM), lambda i: (rev(i), 0)),
                  pl.BlockSpec((TB, NP - NPM), lambda i: (rev(i), 0)),
                  pl.BlockSpec((TB, D), lambda i: (rev(i), 0)), pl.BlockSpec((D, D), lambda i: (0, 0)),
                  pl.BlockSpec((TB, D), lambda i: (rev(i), 0)),
                  pl.BlockSpec((NCH, 256, 128), lambda i: (rev(i), 0, 0)),
                  pl.BlockSpec((NCH, 128, 256), lambda i: (rev(i), 0, 0)),
                  pl.BlockSpec((TB, 768), lambda i: (rev(i), 0)), pl.BlockSpec((TB, 768), lambda i: (rev(i), 0)),
                  pl.BlockSpec((TB, 256), lambda i: (rev(i), 0)), pl.BlockSpec((TB, 256), lambda i: (rev(i), 0)),
                  pl.BlockSpec((16, 768), lambda i: (0, 0)), pl.BlockSpec((128, 128), lambda i: (0, 0)),
                  pl.BlockSpec((256, 256), lambda i: (0, 0)), pl.BlockSpec((4, 256, 256), lambda i: (0, 0, 0)),
                  pl.BlockSpec((256, 128), lambda i: (0, 0))],
        out_specs=[pl.BlockSpec((TB, NP), lambda i: (rev(i), 0)), pl.BlockSpec((32, 1024), lambda i: (0, 0)),
                   pl.BlockSpec((D, D), lambda i: (0, 0))],
        out_shape=[jax.ShapeDtypeStruct((t, NP), BF16), jax.ShapeDtypeStruct((32, 1024), F32),
                   jax.ShapeDtypeStruct((D, D), F32)],
        scratch_shapes=[pltpu.VMEM((256, 128), F32), pltpu.VMEM((128, 256), F32), pltpu.VMEM((8, 256), F32),
                        pltpu.VMEM((16, 256), F32), pltpu.VMEM((8, 768), F32), pltpu.VMEM((16, 768), F32),
                        pltpu.VMEM((128, 128), F32), pltpu.VMEM((256, 256), F32), pltpu.VMEM((TB, D), F32)])


SHARD = NPROJ // 4
SHARD_PAD = 896


def _ranges_to_perm(o, n):
    out, p = [], 0
    for start, size in _PERM:
        a, b = max(o, start), min(o + n, start + size)
        if a < b:
            out.append((a, b - a, p + a - start))
        p += size
    return out


def _ranges_to_orig(p0, n):
    out, p = [], 0
    for start, size in _PERM:
        a, b = max(p0, p), min(p0 + n, p + size)
        if a < b:
            out.append((a, b - a, start + a - p))
        p += size
    return out


def _lane_window(load, lo, n, d, lane):
    a = 128 * (lo // 128)
    off = lo - a
    w = 128 if off + n <= 128 else 256
    chunk = load(a, w)
    shift = (d - off) % w
    if shift:
        chunk = pltpu.roll(chunk, shift, axis=1)
    return jnp.where((lane >= d) & (lane < d + n), chunk[:, 0:128], 0.0)


def _assemble_w_in(slabs, name, rb=256):
    def body(s_ref, wp_ref, wpt_ref):
        lane = _iota((1, 128), 1)
        for b in range(NP // 128):
            acc = jnp.zeros((rb, 128), F32)
            for p, n, o in _ranges_to_orig(128 * b, 128):
                while n > 0:
                    s, lo = o // SHARD, o % SHARD
                    cnt = min(n, SHARD - lo)
                    acc = acc + _lane_window(lambda a, w, s=s: s_ref[s, :, a:a + w].astype(F32), lo, cnt, p - 128 * b, lane)
                    o, p, n = o + cnt, p + cnt, n - cnt
            wp_ref[:, 128 * b:128 * b + 128] = acc.astype(BF16)
            wpt_ref[128 * b:128 * b + 128, :] = acc.T.astype(BF16)

    return pl.pallas_call(
        body, grid=(D // rb,), name=name,
        in_specs=[pl.BlockSpec((4, rb, SHARD_PAD), lambda i: (0, i, 0))],
        out_specs=[pl.BlockSpec((rb, NP), lambda i: (i, 0)), pl.BlockSpec((NP, rb), lambda i: (0, i))],
        out_shape=[jax.ShapeDtypeStruct((D, NP), BF16), jax.ShapeDtypeStruct((NP, D), BF16)],
        compiler_params=_cparams(("parallel",)))(slabs)


def _split_dw_in(dwp, name, rb=256):
    rows = dwp.shape[0]

    def body(g_ref, o_ref):
        lane = _iota((1, 128), 1)
        for s in range(4):
            for k in range(SHARD_PAD // 128):
                acc = jnp.zeros((rb, 128), F32)
                n_valid = min(128, SHARD - 128 * k)
                for o, n, p in _ranges_to_perm(SHARD * s + 128 * k, n_valid):
                    acc = acc + _lane_window(lambda a, w: g_ref[:, a:a + w].astype(F32), p, n, o - SHARD * s - 128 * k, lane)
                o_ref[s, :, 128 * k:128 * k + 128] = acc.astype(o_ref.dtype)

    return pl.pallas_call(
        body, grid=(rows // rb,), name=name,
        in_specs=[pl.BlockSpec((rb, NP), lambda i: (i, 0))],
        out_specs=pl.BlockSpec((4, rb, SHARD_PAD), lambda i: (0, i, 0)),
        out_shape=jax.ShapeDtypeStruct((4, rows, SHARD_PAD), dwp.dtype),
        compiler_params=_cparams(("parallel",)))(dwp)


def _half(c, n):
    return pl.ds(pl.multiple_of(c * (n // 2), n // 2), n // 2)


def _other_chips(x, y):
    return ((1 - x, y), (x, 1 - y), (1 - x, 1 - y))


def _remote(src, dst, send, recv, k, dev):
    return pltpu.make_async_remote_copy(src_ref=src, dst_ref=dst, send_sem=send.at[k], recv_sem=recv.at[k], device_id=dev,
                                        device_id_type=MESH)


def _sem(n):
    return pltpu.SemaphoreType.DMA((n,))


def _rider_gather_ici(shards):
    shards = tuple(shards)
    n = len(shards)

    def copies(rins, routs, sems, arrivals=True):
        send, recv = sems
        x, y, c = _place()
        me = 2 * x + y
        out, inc = [], []
        for j, (px, py) in enumerate(_other_chips(x, y)):
            for k in range(n):
                rows = _half(c, shards[k].shape[0])
                out.append(_remote(rins[k].at[rows], routs[k].at[me, rows], send, recv, n * j + k, (px, py, c)))
                if arrivals:
                    inc.append(_remote(rins[k].at[rows], routs[k].at[2 * px + py, rows], send, recv, n * j + k, (px, py, c)))
        return out, inc

    def start(rins, routs, sems):
        for cp in copies(rins, routs, sems, arrivals=False)[0]:
            cp.start()

    def finish(rins, routs, sems):
        out, inc = copies(rins, routs, sems)
        for cp in inc:
            cp.wait_recv()
        for cp in out:
            cp.wait_send()

    return _Rider(shards, [jax.ShapeDtypeStruct((4,) + a.shape, a.dtype) for a in shards], [_sem(3 * n), _sem(3 * n)],
                  start, finish)


def _gather_ici_two_hops(shards, extra):
    shards = tuple(shards)
    n = len(shards)

    def body(*refs):
        ins, e_in, outs, e_out = refs[:n], refs[n], refs[n + 1:2 * n + 1], refs[2 * n + 1]
        send, recv = refs[2 * n + 2:]
        x, y, c = _place()
        slab = lambda px, py: 2 * px + py
        xn, yn, dg = (1 - x, y), (x, 1 - y), (1 - x, 1 - y)

        def part(k, q):
            r = shards[k].shape[0] // 4
            return pl.ds(pl.multiple_of(c * 2 * r + q * r, r), r)

        def hop(k, q, src_chip, to, sem):
            rows = part(k, q)
            src = ins[k].at[rows] if src_chip is None else outs[k].at[slab(*src_chip), rows]
            own = (x, y) if src_chip is None else src_chip
            return _remote(src, outs[k].at[slab(*own), rows], send, recv, sem, (*to, c))

        small = [_remote(e_in, e_out.at[slab(x, y)], send, recv, 6 * n + j, (*to, c)) for j, to in enumerate((xn, yn, dg))]
        first = [hop(k, q, None, (xn, yn)[q], 2 * k + q) for k in range(n) for q in (0, 1)]
        for cp in small + first:
            cp.start()
        for k in range(n):
            for q in (0, 1):
                nb = (xn, yn)[q]
                _remote(ins[k].at[part(k, q)], outs[k].at[slab(*nb), part(k, q)], send, recv, 2 * k + q, (*nb, c)).wait_recv()
        second = []
        for k in range(n):
            for q in (0, 1):
                to, via = (yn, xn)[q], (xn, yn)[q]
                second.append(hop(k, q, None, to, 2 * n + 4 * k + 2 * q))
                second.append(hop(k, q, via, to, 2 * n + 4 * k + 2 * q + 1))
        for cp in second:
            cp.start()
        for k in range(n):
            for q in (0, 1):
                frm, rows = (yn, xn)[q], part(k, q)
                for j, origin in enumerate((frm, dg)):
                    _remote(ins[k].at[rows], outs[k].at[slab(*origin), rows], send, recv, 2 * n + 4 * k + 2 * q + j,
                            (*frm, c)).wait_recv()
        for j, frm in enumerate((xn, yn, dg)):
            _remote(e_in, e_out.at[slab(*frm)], send, recv, 6 * n + j, (*frm, c)).wait_recv()
        third, theirs = [], []
        for k in range(n):
            rows = shards[k].shape[0]
            for j, chip in enumerate((xn, yn, dg)):
                got, missing = outs[k].at[slab(*chip), _half(c, rows)], outs[k].at[slab(*chip), _half(1 - c, rows)]
                third.append(_remote(got, got, send, recv, 6 * n + 3 + 3 * k + j, (x, y, 1 - c)))
                theirs.append(_remote(missing, missing, send, recv, 6 * n + 3 + 3 * k + j, (x, y, 1 - c)))
        for cp in third:
            cp.start()
        for cp in theirs:
            cp.wait_recv()
        for cp in small + first + second + third:
            cp.wait_send()

    outs = pl.pallas_call(
        body, name="gather0", in_specs=[_ANY] * (n + 1), out_specs=[_ANY] * (n + 1),
        out_shape=[jax.ShapeDtypeStruct((4,) + a.shape, a.dtype) for a in shards + (extra,)],
        scratch_shapes=[_sem(9 * n + 3), _sem(9 * n + 3)])(*shards, extra)
    return list(outs)


def _rider_gather_d2d(slabs):
    slabs = tuple(slabs)
    n = len(slabs)

    def copies(routs, sems, arrivals=True):
        send, recv = sems
        x, y, c = _place()
        out, inc = [], []
        for j, (px, py) in enumerate(_other_chips(x, y)):
            for k in range(n):
                rows = slabs[k].shape[1]
                mine, theirs = routs[k].at[2 * px + py, _half(c, rows)], routs[k].at[2 * px + py, _half(1 - c, rows)]
                out.append(_remote(mine, mine, send, recv, n * j + k, (x, y, 1 - c)))
                if arrivals:
                    inc.append(_remote(theirs, theirs, send, recv, n * j + k, (x, y, 1 - c)))
        return out, inc

    def start(rins, routs, sems):
        for cp in copies(routs, sems, arrivals=False)[0]:
            cp.start()

    def finish(rins, routs, sems):
        out, inc = copies(routs, sems)
        for cp in inc:
            cp.wait_recv()
        for cp in out:
            cp.wait_send()

    return _Rider(slabs, [jax.ShapeDtypeStruct(a.shape, a.dtype) for a in slabs], [_sem(3 * n), _sem(3 * n)], start, finish,
                  aliases={k: k for k in range(n)})


def _rider_swap(parts):
    parts = tuple(parts)
    n = len(parts)

    def copies(rins, routs, sems):
        send, recv = sems
        x, y, c = _place()
        return [_remote(rins[k].at[:, _half(1 - c, parts[k].shape[1])], routs[k], send, recv, k, (x, y, 1 - c))
                for k in range(n)]

    def start(rins, routs, sems):
        for cp in copies(rins, routs, sems):
            cp.start()

    def finish(rins, routs, sems):
        for cp in copies(rins, routs, sems):
            cp.wait()

    return _Rider(parts, [jax.ShapeDtypeStruct((a.shape[0], a.shape[1] // 2, a.shape[2]), a.dtype) for a in parts],
                  [_sem(n), _sem(n)], start, finish)


def _rider_scatter(parts):
    parts = tuple(parts)
    n = len(parts)

    def copies(rins, routs, sems, arrivals=True):
        send, recv = sems
        x, y, c = _place()
        me = 2 * x + y
        out, inc = [], []
        for j, (px, py) in enumerate(_other_chips(x, y)):
            for k in range(n):
                out.append(_remote(rins[k].at[2 * px + py], routs[k].at[me], send, recv, n * j + k, (px, py, c)))
                if arrivals:
                    inc.append(_remote(rins[k].at[me], routs[k].at[2 * px + py], send, recv, n * j + k, (px, py, c)))
        return out, inc

    def start(rins, routs, sems):
        for cp in copies(rins, routs, sems, arrivals=False)[0]:
            cp.start()

    def finish(rins, routs, sems):
        out, inc = copies(rins, routs, sems)
        for cp in inc:
            cp.wait_recv()
        for cp in out:
            cp.wait_send()

    return _Rider(parts, [jax.ShapeDtypeStruct(a.shape, a.dtype) for a in parts], [_sem(3 * n), _sem(3 * n)], start, finish)


def _rider_share(fulls):
    fulls = tuple(fulls)
    n = len(fulls)

    def copies(routs, sems, arrivals=True):
        send, recv = sems
        x, y, c = _place()
        out, inc = [], []
        for k in range(n):
            mine, theirs = routs[k].at[_half(c, fulls[k].shape[0])], routs[k].at[_half(1 - c, fulls[k].shape[0])]
            out.append(_remote(mine, mine, send, recv, k, (x, y, 1 - c)))
            if arrivals:
                inc.append(_remote(theirs, theirs, send, recv, k, (x, y, 1 - c)))
        return out, inc

    def start(rins, routs, sems):
        for cp in copies(routs, sems, arrivals=False)[0]:
            cp.start()

    def finish(rins, routs, sems):
        out, inc = copies(routs, sems)
        for cp in inc:
            cp.wait_recv()
        for cp in out:
            cp.wait_send()

    return _Rider(fulls, [jax.ShapeDtypeStruct(a.shape, a.dtype) for a in fulls], [_sem(n), _sem(n)], start, finish,
                  aliases={k: k for k in range(n)})


def _pair_sum(core, full, recv, name, br=128):
    n, rows, cols = recv.shape

    def body(c_ref, a_ref, b_ref, o_ref):
        o_ref[...] = (a_ref[...] + b_ref[...]).astype(BF16)

    nb = rows // br
    return pl.pallas_call(
        body, name=name, out_shape=jax.ShapeDtypeStruct(recv.shape, BF16),
        grid_spec=pltpu.PrefetchScalarGridSpec(
            num_scalar_prefetch=1, grid=(n, nb),
            in_specs=[pl.BlockSpec((1, br, cols), lambda i, j, c: (i, c[0] * nb + j, 0)),
                      pl.BlockSpec((1, br, cols), lambda i, j, c: (i, j, 0))],
            out_specs=pl.BlockSpec((1, br, cols), lambda i, j, c: (i, j, 0))),
        compiler_params=_cparams(("parallel", "parallel")))(core, full, recv)


def _chip_sum(place, gathered, mine, name, br=128):
    _, r, c = gathered.shape
    nb = r // br

    def body(p_ref, g_ref, m_ref, o_ref):
        slab = lambda j: jnp.where(p_ref[1] == j, m_ref[j], g_ref[j]).astype(F32)
        o_ref[...] = ((slab(0) + slab(1)) + slab(2)) + slab(3)

    return pl.pallas_call(
        body, name=name, out_shape=jax.ShapeDtypeStruct((2 * r, c), F32),
        grid_spec=pltpu.PrefetchScalarGridSpec(
            num_scalar_prefetch=1, grid=(nb,),
            in_specs=[pl.BlockSpec((4, br, c), lambda i, p: (0, i, 0)), pl.BlockSpec((4, br, c), lambda i, p: (0, i, 0))],
            out_specs=pl.BlockSpec((br, c), lambda i, p: (p[0] * nb + i, 0))),
        compiler_params=_cparams(("parallel",)))(place, gathered, mine)


def _adamw(w, g, m, v, name, br):
    n, r, c = w.shape

    def body(w_ref, g_ref, m_ref, v_ref, d_ref, m2_ref, v2_ref):
        d_ref[...], m2_ref[...], v2_ref[...] = _adam_math(w_ref[...], g_ref[...], m_ref[...], v_ref[...])

    spec = pl.BlockSpec((1, br, c), lambda i, j: (i, j, 0))
    shp = jax.ShapeDtypeStruct(w.shape, F32)
    return pl.pallas_call(body, grid=(n, r // br), name=name, in_specs=[spec] * 4, out_specs=[spec] * 3,
                          out_shape=[shp] * 3, compiler_params=_cparams(("parallel", "parallel")))(w, g, m, v)


def _adamw_w_in(w, g, m, v, name, bc=93):
    cols = w.shape[2]
    lead = lambda a: jnp.transpose(a, (2, 0, 1))
    g = jnp.stack([a[:, 0:cols] for a in g])

    def body(w_ref, g_ref, m_ref, v_ref, go_ref, d_ref, m2_ref, v2_ref):
        for l in range(2):
            gv = g_ref[:, l, :]
            d_ref[:, l, :], m2_ref[:, l, :], v2_ref[:, l, :] = _adam_math(w_ref[:, l, :], gv, m_ref[:, l, :], v_ref[:, l, :])
            go_ref[:, l, :] = gv

    spec = pl.BlockSpec((bc, 2, D), lambda i: (i, 0, 0))
    outs = pl.pallas_call(body, grid=(cols // bc,), name=name, in_specs=[spec] * 4, out_specs=[spec] * 4,
                          out_shape=[jax.ShapeDtypeStruct((cols, 2, D), F32)] * 4,
                          compiler_params=_cparams(("parallel",)))(lead(w), lead(g), lead(m), lead(v))
    return [jnp.transpose(o, (1, 2, 0)) for o in outs]


_SMALL_NAMES = ("norm_w", "conv_a_w", "gla_gate_w", "gla_gate_b", "gla_norm_w", "pool_w", "pool_scale", "ssd_conv_w",
                "ssd_conv_b", "ssd_dt_bias", "ssd_a_log", "ssd_d", "ssd_norm_w", "final_norm_w")
SMALL_ROWS = 80


def _adam_math(w, g, m, v):
    m2 = ADAM_B1 * m + (1.0 - ADAM_B1) * g
    v2 = ADAM_B2 * v + (1.0 - ADAM_B2) * (g * g)
    m_hat = m2 / (1.0 - ADAM_B1 ** ADAM_STEP)
    v_hat = v2 / (1.0 - ADAM_B2 ** ADAM_STEP)
    return -ADAM_LR * (m_hat / (jnp.sqrt(v_hat) + ADAM_EPS) + ADAM_WD * w), m2, v2


def _small_slices(name, chip):
    if name == "conv_a_w":
        return [((), slice(R_CAW, R_CAW + 3), slice(64 * chip, 64 * chip + 64))]
    if name == "ssd_conv_w":
        return [((), slice(R_SCW, R_SCW + 4), slice(192 * chip, 192 * chip + 192))]
    if name == "gla_gate_w":
        return [((), slice(0, 16), slice(768, 896))]
    if name == "pool_w":
        return [((g, slice(16 * q, 16 * q + 16)), slice(16, 32), slice(256 * q + 64 * g, 256 * q + 64 * g + 64))
                for g in range(4) for q in range(4)]
    row, lanes = {"gla_gate_b": (R_GB, slice(0, 128)), "gla_norm_w": (R_GNW, slice(0, 64)),
                  "pool_scale": (R_PSC, slice(0, 256)), "ssd_conv_b": (R_SCB, slice(0, 768)),
                  "ssd_dt_bias": (R_DTB, slice(16, 20)), "ssd_a_log": (R_AE, slice(0, 4)), "ssd_d": (R_DE, slice(0, 4)),
                  "ssd_norm_w": (R_SNW, slice(0, 256))}[name]
    return [((), slice(row, row + 1), lanes)]


def _rider_exchange(block):
    def copies(rins, routs, sems):
        send, recv = sems
        x, y, c = _place()
        flip = lambda v, bit: 1 - v if bit else v
        return [_remote(rins[0], routs[0].at[k], send, recv, k - 1, (flip(x, k & 4), flip(y, k & 2), flip(c, k & 1)))
                for k in range(1, 8)]

    def start(rins, routs, sems):
        for cp in copies(rins, routs, sems):
            cp.start()

    def finish(rins, routs, sems):
        for cp in copies(rins, routs, sems):
            cp.wait()

    return _Rider((block,), [jax.ShapeDtypeStruct((8,) + block.shape, block.dtype)], [_sem(7), _sem(7)], start, finish)


def _join_riders(a, b):
    na, oa, sa = len(a.inputs), len(a.out_shapes), len(a.sems)

    def start(rins, routs, sems):
        a.start(rins[:na], routs[:oa], sems[:sa])
        b.start(rins[na:], routs[oa:], sems[sa:])

    def finish(rins, routs, sems):
        a.finish(rins[:na], routs[:oa], sems[:sa])
        b.finish(rins[na:], routs[oa:], sems[sa:])

    aliases = {**a.aliases, **{na + k: oa + v for k, v in b.aliases.items()}}
    return _Rider(a.inputs + b.inputs, a.out_shapes + b.out_shapes, a.sems + b.sems, start, finish, aliases)


def _small_adamw(blocks, w, m, v):
    n = len(_SMALL_NAMES)

    def body(*refs):
        (own, ex), (own0, ex0) = refs[0:2], refs[2:4]
        refs = refs[3:]
        w_refs, m_refs, v_refs = refs[1:1 + n], refs[1 + n:1 + 2 * n], refs[1 + 2 * n:1 + 3 * n]
        o = 1 + 3 * n
        g_out, d_out, m_out, v_out = refs[o:o + n], refs[o + n:o + 2 * n], refs[o + 2 * n:o + 3 * n], refs[o + 3 * n:o + 4 * n]
        loss_ref, acc, acc0 = refs[o + 4 * n:o + 4 * n + 3]
        chip = 2 * lax.axis_index("x") + lax.axis_index("y")
        me = 2 * chip + lax.axis_index("c")
        acc[...] = jnp.zeros_like(acc)
        acc0[...] = jnp.zeros_like(acc0)
        for src in range(8):
            @pl.when(me == src)
            def _():
                acc[...] += own[...]
                acc0[...] += own0[...]

            @pl.when(me != src)
            def _(src=src):
                acc[...] += ex[jnp.bitwise_xor(me, src)]
                acc0[...] += ex0[jnp.bitwise_xor(me, src)]

        loss_ref[...] = acc[73:74, 0:1]

        def update(i, idx, g):
            d, m2, v2 = _adam_math(w_refs[i][idx], g, m_refs[i][idx], v_refs[i][idx])
            g_out[i][idx], d_out[i][idx], m_out[i][idx], v_out[i][idx] = g, d, m2, v2

        for i, name in enumerate(_SMALL_NAMES):
            if name == "final_norm_w":
                update(i, (slice(0, 1), slice(None)), acc[72:73, :])
            elif name == "norm_w":
                update(i, (slice(0, 1), slice(None)), acc0[0:1, :])
                update(i, (slice(1, 2), slice(None)), acc[64:65, :])
            elif name in ("conv_a_w", "ssd_conv_w"):
                for s in range(4):
                    @pl.when(chip == s)
                    def _(i=i, name=name, s=s):
                        for l in range(2):
                            (_, rows, lanes), = _small_slices(name, s)
                            update(i, (l,), acc[rows.start + 32 * l:rows.stop + 32 * l, lanes])
            else:
                for l in range(2):
                    for idx, rows, lanes in _small_slices(name, 0):
                        g = acc[rows.start + 32 * l:rows.stop + 32 * l, lanes]
                        if w_refs[i].ndim == 2:
                            update(i, (slice(l, l + 1), slice(None)), g)
                        else:
                            update(i, (l,) + idx, g)

    args = [a for pair in blocks for a in pair] + [d[k] for d in (w, m, v) for k in _SMALL_NAMES]
    shapes = [jax.ShapeDtypeStruct(w[k].shape, F32) for k in _SMALL_NAMES]
    vmem = pl.BlockSpec(memory_space=pltpu.VMEM)
    outs = pl.pallas_call(body, name="small_adamw", in_specs=[vmem] * len(args), out_specs=[vmem] * (4 * n + 1),
                          out_shape=shapes * 4 + [jax.ShapeDtypeStruct((1, 1), F32)],
                          scratch_shapes=[pltpu.VMEM((SMALL_ROWS, D), F32), pltpu.VMEM((8, D), F32)])(*args)
    return outs[0:n], outs[n:2 * n], outs[2 * n:3 * n], outs[3 * n:4 * n], outs[4 * n]


def _mixer_consts(layer, conv_a_w, gla_gate_w, gla_gate_b, gla_norm_w, pool_w, pool_scale, ssd_conv_w, ssd_conv_b,
                  ssd_dt_bias, ssd_a_log, ssd_d, ssd_norm_w):
    def row(v):
        return jnp.pad(v.reshape(1, -1), ((0, 0), (0, 768 - v.size)))

    dtb = jnp.pad(ssd_dt_bias[layer], (16, 108))
    rows = [jnp.pad(conv_a_w[layer], ((0, 0), (0, 512))), row(gla_gate_b[layer]), row(jnp.tile(gla_norm_w[layer], 4)),
            row(pool_scale[layer]), row(ssd_conv_b[layer]), row(dtb), row(jnp.repeat(-jnp.exp(ssd_a_log[layer]), 64)),
            row(jnp.repeat(ssd_d[layer], 64)), row(ssd_norm_w[layer]), jnp.zeros((1, 768), F32), ssd_conv_w[layer]]
    prm = jnp.concatenate(rows, axis=0)
    gw = jnp.pad(gla_gate_w[layer], ((0, 112), (0, 0))).astype(BF16)
    on_diag = (_iota((256, 256), 0) >> 6) == (_iota((256, 256), 1) >> 6)
    pw = jnp.where(on_diag, jnp.tile(pool_w[layer].reshape(256, 64), (1, 4)), 0.0)
    return (prm, gw, pw.astype(BF16)) + _mixer_matrices()


def _grad_slabs(dwp, dwo):
    return dwp.reshape(1, D, NP), dwo.reshape(4, D // 4, D)


class _Comm:
    def __init__(self, w_in, w_out):
        self.w_in16 = jnp.pad(w_in.astype(BF16), ((0, 0), (0, 0), (0, SHARD_PAD - SHARD)))
        self.w_out16 = w_out.astype(BF16)
        self.core = lax.axis_index("c").astype(jnp.int32).reshape(1)
        self.chip = 2 * lax.axis_index("x") + lax.axis_index("y")
        self.place = jnp.stack([lax.axis_index("c"), self.chip]).astype(jnp.int32)

    def gather_ici(self, layer):
        return _rider_gather_ici((self.w_in16[layer], self.w_out16[layer]))

    def pair_sum(self, layer, slabs, received):
        d_in, d_out = [_pair_sum(self.core, a, b, name=f"reduce_pair_sum{layer}_{k}")
                       for k, (a, b) in enumerate(zip(slabs, received))]
        return [_split_dw_in(d_in[0], name=f"split_dw_in{layer}"), d_out]

    def chip_sum(self, layer, gathered, mine):
        return [_chip_sum(self.place, a, b, name=f"reduce_chip_sum{layer}_{k}") for k, (a, b) in enumerate(zip(gathered, mine))]

    def layer_weights(self, layer, s_in, s_out):
        own = lambda slabs, shard: jnp.stack([jnp.where(self.chip == s, shard, slabs[s]) for s in range(4)])
        wp, wpt = _assemble_w_in(own(s_in, self.w_in16[layer]), name=f"assemble_w_in{layer}")
        wo = own(s_out, self.w_out16[layer]).reshape(D, D)
        return wp, wpt, wo, wo.T


def _local_step(x, tgt, norm_w, final_norm_w, consts, wts0, wts1=None, comm=None):
    nw = [norm_w[l:l + 1] for l in range(2)]
    proj0, h0, slabs = _rmsproj(x, nw[0], wts0[0], name="rmsproj0", rider=comm and comm.gather_ici(1))
    (mix0, sg0, ss0, x1, *conv0), slabs = _mixer_fwd(proj0, x, wts0[2], *consts[0], name="mixer_fwd0",
                                                     rider=comm and _rider_gather_d2d(slabs))
    if comm:
        wts1 = comm.layer_weights(1, *slabs)
    proj1, h1, _ = _rmsproj(x1, nw[1], wts1[0], name="rmsproj1")
    (mix1, sg1, ss1, dx, *conv1, head), _ = _mixer_fwd(proj1, x1, wts1[2], *consts[1], name="mixer_fwd1",
                                                       head=(tgt, final_norm_w.reshape(1, D)))
    (dproj, mgr1, dwo1), _ = _mixer_bwd(proj1, dx, wts1[3], mix1, sg1, ss1, *conv1, *consts[1], name="mixer_bwd1")
    dwp1, _ = _dwin(h1, dproj, name="dwin1")
    slabs1 = comm and _grad_slabs(dwp1, dwo1)
    (dx, dnw1), recv = _dxin(dproj, wts1[1], x1, dx, nw[1], name="dxin1", rider=comm and _rider_swap(slabs1))
    pairs1 = comm and comm.pair_sum(1, slabs1, recv)
    (dproj, mgr0, dwo0), gathered = _mixer_bwd(proj0, dx, wts0[3], mix0, sg0, ss0, *conv0, *consts[0], name="mixer_bwd0",
                                               rider=comm and _rider_scatter(pairs1))
    if not comm:
        dwp0, _ = _dwin(h0, dproj, name="dwin0")
        (dx, dnw0), _ = _dxin(dproj, wts0[1], x, dx, nw[0], name="dxin0")
        return head, dx, ((dwp0, dwp1), (dwo0, dwo1)), (dnw0, dnw1), (mgr0, mgr1)
    dwo0 = dwo0.reshape(4, D // 4, D)
    small = jnp.concatenate([mgr0, mgr1, dnw1, head], axis=0)
    dwp0, (*big1, recv_out, got_small) = _dwin(h0, dproj, name="dwin0", rider=_join_riders(_join_riders(
        _rider_share(comm.chip_sum(1, gathered, pairs1)), _rider_swap((dwo0,))), _rider_exchange(small)))
    slabs0 = (dwp0.reshape(1, D, NP), dwo0)
    pairs0 = comm.pair_sum(0, slabs0, (_run_rider(_rider_swap(slabs0[0:1]), "reduce_swap0")[0], recv_out))
    (dx, dnw0), gathered = _dxin(dproj, wts0[1], x, dx, nw[0], name="dxin0", rider=_rider_scatter(pairs0))
    last = _run_rider(_join_riders(_rider_share(comm.chip_sum(0, gathered, pairs0)), _rider_exchange(dnw0)),
                      "reduce_share0")
    return dx, ((last[0], big1[0]), (last[1], big1[1])), ((small, got_small), (dnw0, last[2]))


def kernel(x, norm_w, w_in, conv_a_w, gla_gate_w, gla_gate_b, gla_norm_w, pool_w, pool_scale, ssd_conv_w, ssd_conv_b, ssd_dt_bias, ssd_a_log, ssd_d, ssd_norm_w, w_out, final_norm_w, loss_target, m_norm_w, m_w_in, m_conv_a_w, m_gla_gate_w, m_gla_gate_b, m_gla_norm_w, m_pool_w, m_pool_scale, m_ssd_conv_w, m_ssd_conv_b, m_ssd_dt_bias, m_ssd_a_log, m_ssd_d, m_ssd_norm_w, m_w_out, m_final_norm_w, v_norm_w, v_w_in, v_conv_a_w, v_gla_gate_w, v_gla_gate_b, v_gla_norm_w, v_pool_w, v_pool_scale, v_ssd_conv_w, v_ssd_conv_b, v_ssd_dt_bias, v_ssd_a_log, v_ssd_d, v_ssd_norm_w, v_w_out, v_final_norm_w):
    weights = dict(norm_w=norm_w, w_in=w_in, conv_a_w=conv_a_w, gla_gate_w=gla_gate_w, gla_gate_b=gla_gate_b,
                   gla_norm_w=gla_norm_w, pool_w=pool_w, pool_scale=pool_scale, ssd_conv_w=ssd_conv_w,
                   ssd_conv_b=ssd_conv_b, ssd_dt_bias=ssd_dt_bias, ssd_a_log=ssd_a_log, ssd_d=ssd_d,
                   ssd_norm_w=ssd_norm_w, w_out=w_out, final_norm_w=final_norm_w)
    m_in = dict(norm_w=m_norm_w, w_in=m_w_in, conv_a_w=m_conv_a_w, gla_gate_w=m_gla_gate_w, gla_gate_b=m_gla_gate_b,
                gla_norm_w=m_gla_norm_w, pool_w=m_pool_w, pool_scale=m_pool_scale, ssd_conv_w=m_ssd_conv_w,
                ssd_conv_b=m_ssd_conv_b, ssd_dt_bias=m_ssd_dt_bias, ssd_a_log=m_ssd_a_log, ssd_d=m_ssd_d,
                ssd_norm_w=m_ssd_norm_w, w_out=m_w_out, final_norm_w=m_final_norm_w)
    v_in = dict(norm_w=v_norm_w, w_in=v_w_in, conv_a_w=v_conv_a_w, gla_gate_w=v_gla_gate_w, gla_gate_b=v_gla_gate_b,
                gla_norm_w=v_gla_norm_w, pool_w=v_pool_w, pool_scale=v_pool_scale, ssd_conv_w=v_ssd_conv_w,
                ssd_conv_b=v_ssd_conv_b, ssd_dt_bias=v_ssd_dt_bias, ssd_a_log=v_ssd_a_log, ssd_d=v_ssd_d,
                ssd_norm_w=v_ssd_norm_w, w_out=v_w_out, final_norm_w=v_final_norm_w)
    order = ("norm_w", "w_in", "conv_a_w", "gla_gate_w", "gla_gate_b", "gla_norm_w", "pool_w", "pool_scale",
             "ssd_conv_w", "ssd_conv_b", "ssd_dt_bias", "ssd_a_log", "ssd_d", "ssd_norm_w", "w_out", "final_norm_w")
    t = x.shape[1]

    comm = _Comm(w_in, w_out)
    cshard = jnp.zeros((16, 256), F32)
    for l in range(2):
        cshard = cshard.at[8 * l:8 * l + 3, 0:64].set(conv_a_w[l]).at[8 * l + 3:8 * l + 7, 0:192].set(ssd_conv_w[l])
    s_in, s_out, g_c = _gather_ici_two_hops((comm.w_in16[0], comm.w_out16[0]), cshard)
    g_c = [jnp.where(comm.chip == s, cshard, g_c[s]) for s in range(4)]
    conv_a_full = jnp.stack([jnp.concatenate([g_c[s][8 * l:8 * l + 3, 0:64] for s in range(4)], axis=-1) for l in range(2)])
    ssd_conv_full = jnp.stack([jnp.concatenate([g_c[s][8 * l + 3:8 * l + 7, 0:192] for s in range(4)], axis=-1)
                               for l in range(2)])
    consts = [_mixer_consts(l, conv_a_full, gla_gate_w, gla_gate_b, gla_norm_w, pool_w, pool_scale, ssd_conv_full,
                            ssd_conv_b, ssd_dt_bias, ssd_a_log, ssd_d, ssd_norm_w) for l in range(2)]

    dx, big, blocks = _local_step(x.reshape(t, D), loss_target.reshape(t, D), norm_w, final_norm_w, consts,
                                  comm.layer_weights(0, s_in, s_out), comm=comm)

    as2d = lambda d: {k: (d[k].reshape(1, D) if k == "final_norm_w" else d[k]) for k in _SMALL_NAMES}
    small = _small_adamw(blocks, as2d(weights), as2d(m_in), as2d(v_in))
    grads, delta, new_m, new_v = ({k: (a.reshape(D) if k == "final_norm_w" else a) for k, a in zip(_SMALL_NAMES, part)}
                                  for part in small[0:4])
    loss = small[4].reshape(())

    grads["w_out"] = jnp.stack(big[1])

    grads["w_in"], delta["w_in"], new_m["w_in"], new_v["w_in"] = _adamw_w_in(w_in, big[0], m_w_in, v_w_in, name="adamw_w_in")
    delta["w_out"], new_m["w_out"], new_v["w_out"] = _adamw(w_out, grads["w_out"], m_w_out, v_w_out, name="adamw_w_out", br=256)

    return (loss, dx.reshape(1, t, D), *[grads[k] for k in order], *[delta[k] for k in order],
            *[new_m[k] for k in order], *[new_v[k] for k in order])
```

```python
import functools

import jax
import jax.numpy as jnp
from jax import lax
from jax.experimental import pallas as pl
from jax.experimental.pallas import tpu as pltpu

F32 = jnp.float32
BF16 = jnp.bfloat16
MESH = pl.DeviceIdType.MESH

D = 1024
CH = 64
EPS = 1e-6
NP = 3456
NPROJ = 3348
NPM = 3328
GLA_SCALE = 32.0 ** -0.5
INV_TAU = 1.0 / 16.0
TB = 512
NCH = TB // CH
assert TB % 256 == 0

C_AH, C_AB, C_AC, C_AZ, C_GQ, C_GK, C_GV = 0, 256, 512, 768, 1024, 1152, 1280
C_GZ, C_PU, C_PZ, C_SZ, C_SX, C_TL = 1536, 1792, 2048, 2304, 2560, 3328
_PERM = ((0, 1536), (1552, 1792), (1536, 16), (3344, 4))

R_CAW, R_GB, R_GNW, R_PSC, R_SCB, R_DTB, R_AE, R_DE, R_SNW, R_SCW = 0, 3, 4, 5, 6, 7, 8, 9, 10, 12

ADAM_LR, ADAM_B1, ADAM_B2, ADAM_EPS, ADAM_WD, ADAM_STEP = 0.001, 0.9, 0.999, 1e-08, 0.01, 10

VMEM_LIMIT = 56 * 1024 * 1024


def _cparams(sem, limit=VMEM_LIMIT):
    return pltpu.CompilerParams(dimension_semantics=sem, vmem_limit_bytes=limit)


_ANY = pl.BlockSpec(memory_space=pl.ANY)


def _place():
    return lax.axis_index("x"), lax.axis_index("y"), lax.axis_index("c")


class _Rider:
    def __init__(self, inputs, out_shapes, sems, start, finish, aliases=None):
        self.inputs, self.out_shapes, self.sems = tuple(inputs), tuple(out_shapes), tuple(sems)
        self.start, self.finish, self.aliases = start, finish, dict(aliases or {})


def _call(body, args, *, grid, in_specs, out_specs, out_shape, name, sem, scratch_shapes=(), rider=None):
    if rider is None:
        outs = pl.pallas_call(body, grid=grid, name=name, in_specs=list(in_specs), out_specs=list(out_specs),
                              out_shape=list(out_shape), scratch_shapes=list(scratch_shapes),
                              compiler_params=_cparams(sem))(*args)
        return list(outs), []
    ni, no, ns = len(args), len(out_shape), len(scratch_shapes)
    ri, ro = len(rider.inputs), len(rider.out_shapes)

    def full(*refs):
        ins, rins = refs[:ni], refs[ni:ni + ri]
        outs, routs = refs[ni + ri:ni + ri + no], refs[ni + ri + no:ni + ri + no + ro]
        scr, rsem = refs[ni + ri + no + ro:ni + ri + no + ro + ns], refs[ni + ri + no + ro + ns:]
        first = functools.reduce(jnp.logical_and, [pl.program_id(a) == 0 for a in range(len(grid))])
        last = functools.reduce(jnp.logical_and, [pl.program_id(a) == grid[a] - 1 for a in range(len(grid))])

        @pl.when(first)
        def _():
            rider.start(rins, routs, rsem)

        body(*ins, *outs, *scr)

        @pl.when(last)
        def _():
            rider.finish(rins, routs, rsem)

    outs = pl.pallas_call(
        full, grid=grid, name=name, in_specs=list(in_specs) + [_ANY] * ri, out_specs=list(out_specs) + [_ANY] * ro,
        out_shape=list(out_shape) + list(rider.out_shapes), scratch_shapes=list(scratch_shapes) + list(rider.sems),
        input_output_aliases={ni + k: no + v for k, v in rider.aliases.items()},
        compiler_params=_cparams(("arbitrary",) * len(grid)))(*args, *rider.inputs)
    return list(outs[:no]), list(outs[no:])


def _run_rider(rider, name):
    ri = len(rider.inputs)

    def body(*refs):
        rins, routs, rsem = refs[:ri], refs[ri:ri + len(rider.out_shapes)], refs[ri + len(rider.out_shapes):]
        rider.start(rins, routs, rsem)
        rider.finish(rins, routs, rsem)

    return list(pl.pallas_call(body, name=name, in_specs=[_ANY] * ri, out_specs=[_ANY] * len(rider.out_shapes),
                               out_shape=list(rider.out_shapes), scratch_shapes=list(rider.sems),
                               input_output_aliases=dict(rider.aliases))(*rider.inputs))


def _dot(a, b):
    return jnp.dot(a.astype(BF16), b.astype(BF16), preferred_element_type=F32)


def _dot_nt(a, b):
    return lax.dot_general(a.astype(BF16), b.astype(BF16), (((1,), (1,)), ((), ())), preferred_element_type=F32)


def _dot_tn(a, b):
    return lax.dot_general(a.astype(BF16), b.astype(BF16), (((0,), (0,)), ((), ())), preferred_element_type=F32)


def _split(a):
    hi = a.astype(BF16)
    lo = (a - hi.astype(F32)).astype(BF16)
    return hi, lo


def _dot2_l(a, b):
    hi, lo = _split(a)
    return _dot(hi, b) + _dot(lo, b)


def _dot2_r(a, b):
    hi, lo = _split(b)
    return _dot(a, hi) + _dot(a, lo)


def _dot3_l(a, b):
    hi, lo = _split(a)
    lo2 = ((a - hi.astype(F32)) - lo.astype(F32)).astype(BF16)
    return _dot(hi, b) + _dot(lo, b) + _dot(lo2, b)


def _dot2_nt(a, b):
    hi, lo = _split(a)
    return _dot_nt(hi, b) + _dot_nt(lo, b)


def _silu(z):
    return z * jax.nn.sigmoid(z)


def _lse1(x):
    return jnp.log(1.0 + jnp.exp(-jnp.abs(x)))


def _cs(a):
    return jnp.sum(a, axis=0, keepdims=True)


def _iota(shape, dim):
    return lax.broadcasted_iota(jnp.int32, shape, dim)


def _mixer_matrices():
    r, c = _iota((256, 256), 0), _iota((256, 256), 1)
    same_chunk = (r >> 6) == (c >> 6)
    mats = jnp.stack([jnp.where((c > r) & same_chunk, 1.0, 0.0), jnp.where((c < r) & same_chunk, 1.0, 0.0),
                      jnp.where(same_chunk, 1.0 / 64.0, 0.0), jnp.where((r < 128) & (r - 16 == (c >> 6)), 1.0, 0.0)])
    mask = jnp.where((_iota((256, 128), 0) >> 6) == (_iota((256, 128), 1) >> 5), 1.0, 0.0)
    return mats.astype(BF16), mask.astype(F32)


def _dn(ext, k, n, h):
    return pltpu.roll(ext, k, axis=0)[h:h + n]


def _up(ext, k, n):
    return pltpu.roll(ext, ext.shape[0] - k, axis=0)[:n]


def _pool_lane_select(lane, s2, s4, s8, s16):
    return jnp.where(lane < 64, s2, jnp.where(lane < 128, s4, jnp.where(lane < 192, s8, s16)))


def _winsum_dn(ext, lane):
    s2 = ext + pltpu.roll(ext, 1, axis=0)
    s4 = s2 + pltpu.roll(s2, 2, axis=0)
    s8 = s4 + pltpu.roll(s4, 4, axis=0)
    s16 = s8 + pltpu.roll(s8, 8, axis=0)
    return _pool_lane_select(lane, s2, s4, s8, s16)


def _winsum_up(ext, lane):
    m = ext.shape[0]
    s2 = ext + pltpu.roll(ext, m - 1, axis=0)
    s4 = s2 + pltpu.roll(s2, m - 2, axis=0)
    s8 = s4 + pltpu.roll(s4, m - 4, axis=0)
    s16 = s8 + pltpu.roll(s8, m - 8, axis=0)
    return _pool_lane_select(lane, s2, s4, s8, s16)


def _pool_inv_count(tile, n):
    lane = _iota((1, 256), 1)
    win = _pool_lane_select(lane, 2.0, 4.0, 8.0, 16.0).astype(F32)
    tpos = (tile * TB + _iota((n, 1), 0) + 1).astype(F32)
    return jnp.where(tpos >= win, 1.0 / win, 1.0 / tpos)


def _pool_scale(v, tile):
    win = _pool_lane_select(_iota((1, 256), 1), 2.0, 4.0, 8.0, 16.0).astype(F32)
    return jnp.concatenate([v[0:16] * _pool_inv_count(tile, 16), v[16:] * (1.0 / win)], axis=0)


def _silu_pair(z):
    s = jax.nn.sigmoid(z)
    return z * s, s * (1.0 + z * (1.0 - s))


def _chunks(a):
    return [a[c * CH:(c + 1) * CH] for c in range(a.shape[0] // CH)]


def _halves(fn, a, b):
    return jnp.concatenate([fn(a[:, 0:128], b[:, 0:128]), fn(a[:, 128:256], b[:, 128:256])], axis=1)


def _chunk_sums(tri, a):
    return jnp.concatenate([_dot2_r(tri, a[r:r + 256]) for r in range(0, a.shape[0], 256)], axis=0)


def _mixer_tile_prep(p_ref, t_ref, xc, prm_ref, gw_v, cm_ref, mk_ref):
    tail = t_ref[...]
    pre = _dot(tail, gw_v) + prm_ref[R_GB:R_GB + 1, 0:128]
    la = (jnp.minimum(pre, 0.0) - _lse1(pre)) * INV_TAU
    dtin = tail + prm_ref[R_DTB:R_DTB + 1, 0:128]
    dtf = jnp.maximum(dtin, 0.0) + _lse1(dtin)
    dte = _dot2_l(dtf, cm_ref[3, 0:128, :])
    da = dte * prm_ref[R_AE:R_AE + 1, 0:256]
    rev = _chunk_sums(cm_ref[0], jnp.concatenate([la, da], axis=1))
    dec = jnp.exp(rev[:, 0:128])
    kd = p_ref[:, C_GK:C_GK + 128].astype(F32) * dec
    wdec = jnp.exp(rev[:, 128:384])
    w = wdec * dte
    xw = xc[:, 0:256] * w
    d_s = [jnp.exp(_cs(a)) for a in _chunks(la)]
    et = [jnp.exp(_cs(a)) for a in _chunks(da)]
    mask_t = mk_ref[...]
    ut_g = [_dot_tn(v, k) * mask_t for v, k in zip(_chunks(p_ref[:, C_GV:C_GV + 256].astype(F32)), _chunks(kd))]
    ut_s = [_halves(_dot_tn, b, x) for b, x in zip(_chunks(xc[:, 256:512]), _chunks(xw))]
    return tail, pre, dtin, dte, dec, kd, wdec, w, xw, d_s, et, ut_g, ut_s


def _rmsproj(x, nw, wp, name, tm=512, rider=None):
    t = x.shape[0]

    def body(x_ref, nw_ref, w_ref, o_ref, t_ref, h_ref):
        xv = x_ref[...]
        rs = lax.rsqrt(jnp.mean(xv * xv, axis=-1, keepdims=True) + EPS)
        h = (xv * rs * nw_ref[...]).astype(BF16)
        h_ref[...] = h
        proj = jnp.dot(h, w_ref[...], preferred_element_type=F32)
        o_ref[...] = proj[:, 0:NPM].astype(BF16)
        t_ref[...] = proj[:, NPM:NP]

    (proj, tail, h), extra = _call(
        body, (x, nw, wp), grid=(t // tm,), name=name, sem=("parallel",), rider=rider,
        in_specs=[pl.BlockSpec((tm, D), lambda i: (i, 0)), pl.BlockSpec((1, D), lambda i: (0, 0)),
                  pl.BlockSpec((D, NP), lambda i: (0, 0))],
        out_specs=[pl.BlockSpec((tm, NPM), lambda i: (i, 0)), pl.BlockSpec((tm, NP - NPM), lambda i: (i, 0)),
                   pl.BlockSpec((tm, D), lambda i: (i, 0))],
        out_shape=[jax.ShapeDtypeStruct((t, NPM), BF16), jax.ShapeDtypeStruct((t, NP - NPM), F32),
                   jax.ShapeDtypeStruct((t, D), BF16)])
    return (proj, tail), h, extra


def _head_tile(xv, tgt, w):
    rs = lax.rsqrt(jnp.mean(xv * xv, axis=-1, keepdims=True) + EPS)
    xh = xv * rs
    err = xh * w - tgt
    dy = err * (1.0 / D)
    dxh = dy * w
    dx = rs * (dxh - xh * jnp.mean(dxh * xh, axis=-1, keepdims=True))
    return dx, _cs(dy * xh), (0.5 / D) * jnp.sum(err * err)


def _dxin(dp, wpt, x, dxn, nw, name, tm=512, rider=None):
    t = x.shape[0]

    def body(dp_ref, w_ref, x_ref, dxn_ref, nw_ref, dx_ref, dnw_ref):
        @pl.when(pl.program_id(0) == 0)
        def _():
            dnw_ref[...] = jnp.zeros_like(dnw_ref)

        dh = jnp.dot(dp_ref[...], w_ref[...], preferred_element_type=F32)
        xv = x_ref[...]
        rs = lax.rsqrt(jnp.mean(xv * xv, axis=-1, keepdims=True) + EPS)
        xh = xv * rs
        dnw_ref[0:1, :] += _cs(dh * xh)
        dxh = dh * nw_ref[...]
        dx_ref[...] = dxn_ref[...] + rs * (dxh - xh * jnp.mean(dxh * xh, axis=-1, keepdims=True))

    return _call(
        body, (dp, wpt, x, dxn, nw), grid=(t // tm,), name=name, sem=("arbitrary",), rider=rider,
        in_specs=[pl.BlockSpec((tm, NP), lambda i: (i, 0)), pl.BlockSpec((NP, D), lambda i: (0, 0)),
                  pl.BlockSpec((tm, D), lambda i: (i, 0)), pl.BlockSpec((tm, D), lambda i: (i, 0)),
                  pl.BlockSpec((1, D), lambda i: (0, 0))],
        out_specs=[pl.BlockSpec((tm, D), lambda i: (i, 0)), pl.BlockSpec((8, D), lambda i: (0, 0))],
        out_shape=[jax.ShapeDtypeStruct((t, D), F32), jax.ShapeDtypeStruct((8, D), F32)])


def _dwin(h, dp, name, tm=1024, rider=None):
    t = h.shape[0]

    def body(h_ref, dp_ref, o_ref):
        @pl.when(pl.program_id(0) == 0)
        def _():
            o_ref[...] = jnp.zeros_like(o_ref)

        o_ref[...] += _dot_tn(h_ref[...], dp_ref[...])

    (dwp,), extra = _call(
        body, (h, dp), grid=(t // tm,), name=name, sem=("arbitrary",), rider=rider,
        in_specs=[pl.BlockSpec((tm, D), lambda i: (i, 0)), pl.BlockSpec((tm, NP), lambda i: (i, 0))],
        out_specs=[pl.BlockSpec((D, NP), lambda i: (0, 0))], out_shape=[jax.ShapeDtypeStruct((D, NP), F32)])
    return dwp, extra


def _mixer_fwd(proj, x, wo, prm, gw, pw, cmat, mask, name, rider=None, head=None):
    proj, tail = proj
    t = proj.shape[0]
    nt, nc = t // TB, t // CH

    def body(p_ref, t_ref, x_ref, wo_ref, prm_ref, gw_ref, pw_ref, cm_ref, mk_ref, *rest):
        (tgt_ref, fw_ref), rest = (rest[:2], rest[2:]) if head else ((None, None), rest)
        mix_ref, sg_ref, ss_ref, xn_ref, xc_ref, dxc_ref, cv_ref, pool_ref = rest[:8]
        acc_ref = rest[8] if head else None
        sg_s, ss_s, h_ua, h_pu, h_sx = rest[-5:]
        i = pl.program_id(0)

        @pl.when(i == 0)
        def _():
            for r in (sg_s, ss_s, h_ua, h_pu, h_sx) + ((acc_ref,) if head else ()):
                r[...] = jnp.zeros_like(r)

        lane = _iota((1, 256), 1)
        u = p_ref[:, C_AC:C_AC + 256].astype(F32) * p_ref[:, C_AH:C_AH + 256].astype(F32)
        ext = jnp.concatenate([h_ua[...], u], axis=0)
        cv = (prm_ref[R_CAW + 2:R_CAW + 3, 0:256] * u + prm_ref[R_CAW + 1:R_CAW + 2, 0:256] * _dn(ext, 1, TB, 8)
              + prm_ref[R_CAW:R_CAW + 1, 0:256] * _dn(ext, 2, TB, 8))
        cv_ref[...] = cv.astype(BF16)
        mix_ref[:, 0:256] = (p_ref[:, C_AB:C_AB + 256].astype(F32) * cv * _silu(p_ref[:, C_AZ:C_AZ + 256].astype(F32))).astype(BF16)
        h_ua[...] = u[TB - 8:, :]
        pu = p_ref[:, C_PU:C_PU + 256].astype(F32)
        ext = jnp.concatenate([h_pu[...], pu], axis=0)
        pooled = (_pool_scale(_winsum_dn(ext, lane)[16:], i) - pu).astype(BF16)
        pool_ref[...] = pooled
        mixed = jnp.dot(pooled, pw_ref[...], preferred_element_type=F32)
        mix_ref[:, 512:768] = (prm_ref[R_PSC:R_PSC + 1, 0:256] * mixed * _silu(p_ref[:, C_PZ:C_PZ + 256].astype(F32))).astype(BF16)
        h_pu[...] = pu[TB - 16:, :]
        sx = p_ref[:, C_SX:C_SX + 768].astype(F32)
        ext = jnp.concatenate([h_sx[...], sx], axis=0)
        xc, dxc = _silu_pair(prm_ref[R_SCW + 3:R_SCW + 4, :] * sx + prm_ref[R_SCW + 2:R_SCW + 3, :] * _dn(ext, 1, TB, 8)
                             + prm_ref[R_SCW + 1:R_SCW + 2, :] * _dn(ext, 2, TB, 8)
                             + prm_ref[R_SCW:R_SCW + 1, :] * _dn(ext, 3, TB, 8) + prm_ref[R_SCB:R_SCB + 1, :])
        xc_ref[...] = xc.astype(BF16)
        dxc_ref[...] = dxc.astype(BF16)
        h_sx[...] = sx[TB - 8:, :]

        _, _, _, _, _, _, _, _, _, d_s, et, ut_g, ut_s = _mixer_tile_prep(p_ref, t_ref, xc, prm_ref, gw_ref[...], cm_ref, mk_ref)
        s_g, s_s = sg_s[...], ss_s[...]
        o, y = [], []
        qs = _chunks(p_ref[:, C_GQ:C_GQ + 128].astype(F32) * GLA_SCALE)
        cm = _chunks(xc[:, 512:768])
        for c in range(NCH):
            sg_ref[c] = s_g
            ss_ref[c] = s_s
            s_g = s_g * d_s[c] + ut_g[c]
            s_s = s_s * et[c] + ut_s[c]
            o.append(_dot_nt(qs[c], s_g))
            y.append(_halves(_dot, cm[c], s_s))
        sg_s[...] = s_g
        ss_s[...] = s_s
        o = jnp.concatenate(o, axis=0)
        on = o * lax.rsqrt(_dot2_l(o * o, cm_ref[2]) + EPS)
        mix_ref[:, 256:512] = (on * prm_ref[R_GNW:R_GNW + 1, 0:256] * _silu(p_ref[:, C_GZ:C_GZ + 256].astype(F32))).astype(BF16)
        y2 = ((jnp.concatenate(y, axis=0) + prm_ref[R_DE:R_DE + 1, 0:256] * xc[:, 0:256])
              * _silu(p_ref[:, C_SZ:C_SZ + 256].astype(F32)))
        mix_ref[:, 768:1024] = (y2 * lax.rsqrt(jnp.mean(y2 * y2, axis=-1, keepdims=True) + EPS)
                                * prm_ref[R_SNW:R_SNW + 1, 0:256]).astype(BF16)
        xn = x_ref[...] + jnp.dot(mix_ref[...], wo_ref[...], preferred_element_type=F32)
        if head:
            xn_ref[...], dfw, loss = _head_tile(xn, tgt_ref[...], fw_ref[...])
            acc_ref[0:1, :] += dfw
            acc_ref[1:2, :] += jnp.zeros((1, D), F32) + loss
        else:
            xn_ref[...] = xn

    row = pl.BlockSpec((TB, D), lambda i: (i, 0))
    return _call(
        body, (proj, tail, x, wo, prm, gw, pw, cmat, mask) + tuple(head or ()), grid=(nt,), name=name, sem=("arbitrary",),
        rider=rider,
        in_specs=[pl.BlockSpec((TB, NPM), lambda i: (i, 0)), pl.BlockSpec((TB, NP - NPM), lambda i: (i, 0)), row,
                  pl.BlockSpec((D, D), lambda i: (0, 0)), pl.BlockSpec((16, 768), lambda i: (0, 0)),
                  pl.BlockSpec((128, 128), lambda i: (0, 0)), pl.BlockSpec((256, 256), lambda i: (0, 0)),
                  pl.BlockSpec((4, 256, 256), lambda i: (0, 0, 0)), pl.BlockSpec((256, 128), lambda i: (0, 0))]
        + ([row, pl.BlockSpec((1, D), lambda i: (0, 0))] if head else []),
        out_specs=[row, pl.BlockSpec((NCH, 256, 128), lambda i: (i, 0, 0)),
                   pl.BlockSpec((NCH, 128, 256), lambda i: (i, 0, 0)), row] + [pl.BlockSpec((TB, 768), lambda i: (i, 0))] * 2
        + [pl.BlockSpec((TB, 256), lambda i: (i, 0))] * 2 + ([pl.BlockSpec((8, D), lambda i: (0, 0))] if head else []),
        out_shape=[jax.ShapeDtypeStruct((t, D), BF16), jax.ShapeDtypeStruct((nc, 256, 128), F32),
                   jax.ShapeDtypeStruct((nc, 128, 256), F32), jax.ShapeDtypeStruct((t, D), F32)]
        + [jax.ShapeDtypeStruct((t, 768), BF16)] * 2 + [jax.ShapeDtypeStruct((t, 256), BF16)] * 2
        + ([jax.ShapeDtypeStruct((8, D), F32)] if head else []),
        scratch_shapes=[pltpu.VMEM((256, 128), F32), pltpu.VMEM((128, 256), F32), pltpu.VMEM((8, 256), F32),
                        pltpu.VMEM((16, 256), F32), pltpu.VMEM((8, 768), F32)])


def _mixer_bwd(proj, dxn, wot, mix, sg, ss, xc16, dxc16, cv16, pool16, prm, gw, pw, cmat, mask, name, rider=None):
    proj, tail = proj
    t = proj.shape[0]
    nt = t // TB
    rev = lambda i: nt - 1 - i

    def body(p_ref, t_ref, dxn_ref, wot_ref, mix_ref, sg_ref, ss_ref, xc_ref, dxc_ref, cv_ref, pool_ref, prm_ref, gw_ref,
             pw_ref, cm_ref, mk_ref, dp_ref, sgc_ref, dwo_ref,
             gg_s, gs_s, h_dcv, h_dpl, h_dpre, gsm_ref, dgw_ref, dpw_ref, dm_ref):
        i = pl.program_id(0)
        tile = nt - 1 - i

        @pl.when(i == 0)
        def _():
            for r in (gg_s, gs_s, h_dcv, h_dpl, h_dpre, gsm_ref, dgw_ref, dpw_ref, dwo_ref):
                r[...] = jnp.zeros_like(r)

        dxn = dxn_ref[...].astype(BF16)
        dm_ref[...] = jnp.dot(dxn, wot_ref[...], preferred_element_type=F32)
        dwo_ref[...] += _dot_tn(mix_ref[...], dxn)

        lane = _iota((1, 256), 1)
        ah, ac = p_ref[:, C_AH:C_AH + 256].astype(F32), p_ref[:, C_AC:C_AC + 256].astype(F32)
        ab, az = p_ref[:, C_AB:C_AB + 256].astype(F32), p_ref[:, C_AZ:C_AZ + 256].astype(F32)
        w0, w1, w2 = (prm_ref[R_CAW + j:R_CAW + j + 1, 0:256] for j in range(3))
        u = ac * ah
        cv = cv_ref[...].astype(F32)
        g = dm_ref[:, 0:256]
        sz, dsz = _silu_pair(az)
        dp_ref[:, C_AB:C_AB + 256] = (g * cv * sz).astype(BF16)
        dp_ref[:, C_AZ:C_AZ + 256] = (g * ab * cv * dsz).astype(BF16)
        dcv = g * ab * sz
        dext = jnp.concatenate([dcv, h_dcv[...]], axis=0)
        dcv1, dcv2 = _up(dext, 1, TB), _up(dext, 2, TB)
        du = w2 * dcv + w1 * dcv1 + w0 * dcv2
        dp_ref[:, C_AC:C_AC + 256] = (du * ah).astype(BF16)
        dp_ref[:, C_AH:C_AH + 256] = (du * ac).astype(BF16)
        gsm_ref[R_CAW:R_CAW + 1, 0:256] += _cs(u * dcv2)
        gsm_ref[R_CAW + 1:R_CAW + 2, 0:256] += _cs(u * dcv1)
        gsm_ref[R_CAW + 2:R_CAW + 3, 0:256] += _cs(u * dcv)
        h_dcv[...] = dcv[0:8, :]
        pz = p_ref[:, C_PZ:C_PZ + 256].astype(F32)
        psc = prm_ref[R_PSC:R_PSC + 1, 0:256]
        pooled = pool_ref[...]
        pw_v = pw_ref[...]
        mixed = jnp.dot(pooled, pw_v, preferred_element_type=F32)
        g = dm_ref[:, 512:768]
        sz, dsz = _silu_pair(pz)
        gsm_ref[R_PSC:R_PSC + 1, 0:256] += _cs(g * mixed * sz)
        dp_ref[:, C_PZ:C_PZ + 256] = (g * psc * mixed * dsz).astype(BF16)
        dmixed = g * psc * sz
        dpw_ref[...] += _dot_tn(pooled, dmixed)
        dpooled = _dot_nt(dmixed, pw_v)
        qd = _pool_scale(dpooled, tile)
        dext = jnp.concatenate([qd, h_dpl[...]], axis=0)
        dp_ref[:, C_PU:C_PU + 256] = (_winsum_up(dext, lane)[:TB] - dpooled).astype(BF16)
        h_dpl[...] = qd[0:16, :]
        cw = [prm_ref[R_SCW + j:R_SCW + j + 1, :] for j in range(4)]
        xc = xc_ref[...].astype(F32)
        xs, bm, cm = xc[:, 0:256], xc[:, 256:512], xc[:, 512:768]

        gw_v = gw_ref[...]
        tail, pre, dtin, dte, dec, kd, wdec, w, xw, d_s, et, ut_g, ut_s = _mixer_tile_prep(p_ref, t_ref, xc, prm_ref,
                                                                                          gw_v, cm_ref, mk_ref)
        gmean = cm_ref[2]
        mask_t = mk_ref[...]
        gnw = prm_ref[R_GNW:R_GNW + 1, 0:256]
        a_e = prm_ref[R_AE:R_AE + 1, 0:256]
        d_e = prm_ref[R_DE:R_DE + 1, 0:256]
        snw = prm_ref[R_SNW:R_SNW + 1, 0:256]
        sg_in = [sg_ref[c] for c in range(NCH)]
        ss_in = [ss_ref[c] for c in range(NCH)]
        sg_n = [sg_in[c] * d_s[c] + ut_g[c] for c in range(NCH)]
        ss_n = [ss_in[c] * et[c] + ut_s[c] for c in range(NCH)]
        qs = _chunks(p_ref[:, C_GQ:C_GQ + 128].astype(F32) * GLA_SCALE)
        cm_c, bm_c, xw_c, kd_c = _chunks(cm), _chunks(bm), _chunks(xw), _chunks(kd)
        v_c = _chunks(p_ref[:, C_GV:C_GV + 256].astype(F32))
        o = jnp.concatenate([_dot_nt(qs[c], sg_n[c]) for c in range(NCH)], axis=0)
        y = jnp.concatenate([_halves(_dot, cm_c[c], ss_n[c]) for c in range(NCH)], axis=0) + d_e * xs
        gz = p_ref[:, C_GZ:C_GZ + 256].astype(F32)
        r = lax.rsqrt(_dot2_l(o * o, gmean) + EPS)
        on = o * r
        dyb = dm_ref[:, 256:512]
        sz, dsz = _silu_pair(gz)
        dp_ref[:, C_GZ:C_GZ + 256] = (dyb * on * gnw * dsz).astype(BF16)
        tg = dyb * sz
        gsm_ref[R_GNW:R_GNW + 1, 0:256] += _cs(tg * on)
        don = tg * gnw
        do_c = _chunks(r * (don - on * _dot2_l(don * on, gmean)))
        ssz = p_ref[:, C_SZ:C_SZ + 256].astype(F32)
        sil, dsil = _silu_pair(ssz)
        y2 = y * sil
        r = lax.rsqrt(jnp.mean(y2 * y2, axis=-1, keepdims=True) + EPS)
        yn = y2 * r
        dyd = dm_ref[:, 768:1024]
        gsm_ref[R_SNW:R_SNW + 1, 0:256] += _cs(dyd * yn)
        dn = dyd * snw
        dy2 = r * (dn - yn * jnp.mean(dn * yn, axis=-1, keepdims=True))
        dp_ref[:, C_SZ:C_SZ + 256] = (dy2 * y * dsil).astype(BF16)
        dy = dy2 * sil
        gsm_ref[R_DE:R_DE + 1, 0:256] += _cs(dy * xs)
        dy_c = _chunks(dy)
        dq = jnp.concatenate([_dot(do_c[c], sg_n[c]) for c in range(NCH)], axis=0)
        dp_ref[:, C_GQ:C_GQ + 128] = (dq * GLA_SCALE).astype(BF16)
        dcm = jnp.concatenate([_halves(_dot_nt, dy_c[c], ss_n[c]) for c in range(NCH)], axis=0)
        gg = [_dot_tn(do_c[c], qs[c]) * mask_t for c in range(NCH)]
        gs = [_halves(_dot_tn, cm_c[c], dy_c[c]) for c in range(NCH)]
        car_g, car_s = gg_s[...], gs_s[...]
        for c in reversed(range(NCH)):
            gg[c] = gg[c] + car_g
            gs[c] = gs[c] + car_s
            car_g = gg[c] * d_s[c]
            car_s = gs[c] * et[c]
        gg_s[...] = car_g
        gs_s[...] = car_s
        dkd = jnp.concatenate([_dot(v_c[c], gg[c]) for c in range(NCH)], axis=0)
        dp_ref[:, C_GV:C_GV + 256] = jnp.concatenate([_dot_nt(kd_c[c], gg[c]) for c in range(NCH)], axis=0).astype(BF16)
        dp_ref[:, C_GK:C_GK + 128] = (dkd * dec).astype(BF16)
        dbm = jnp.concatenate([_halves(_dot_nt, xw_c[c], gs[c]) for c in range(NCH)], axis=0)
        dxw = jnp.concatenate([_halves(_dot, bm_c[c], gs[c]) for c in range(NCH)], axis=0)
        dxs = dy * d_e + dxw * w
        dw = dxw * xs
        dsuf = _chunk_sums(cm_ref[1], jnp.concatenate([dkd * kd, dw * dte * wdec], axis=1))
        tot_g = jnp.concatenate([jnp.broadcast_to(_cs(gg[c] * sg_in[c]) * d_s[c], (CH, 128)) for c in range(NCH)], axis=0)
        tot_s = jnp.concatenate([jnp.broadcast_to(_cs(gs[c] * ss_in[c]) * et[c], (CH, 256)) for c in range(NCH)], axis=0)
        dpre = (dsuf[:, 0:128] + tot_g) * INV_TAU * jax.nn.sigmoid(-pre)
        dgw_ref[...] += _dot_tn(tail, dpre)
        gsm_ref[R_GB:R_GB + 1, 0:128] += _cs(dpre)
        dda = dsuf[:, 128:384] + tot_s
        gsm_ref[R_AE:R_AE + 1, 0:256] += _cs(dda * dte)
        dtail_s = _dot2_nt(dw * wdec + dda * a_e, cm_ref[3, 0:128, :]) * jax.nn.sigmoid(dtin)
        gsm_ref[R_DTB:R_DTB + 1, 0:128] += _cs(dtail_s)
        dp_ref[:, C_TL:C_TL + 128] = (_dot_nt(dpre, gw_v) + dtail_s).astype(BF16)
        dpre_c = jnp.concatenate([dxs, dbm, dcm], axis=1) * dxc_ref[...].astype(F32)
        dext = jnp.concatenate([dpre_c, h_dpre[...]], axis=0)
        ups = [dpre_c, _up(dext, 1, TB), _up(dext, 2, TB), _up(dext, 3, TB)]
        dp_ref[:, C_SX:C_SX + 768] = (cw[3] * ups[0] + cw[2] * ups[1] + cw[1] * ups[2] + cw[0] * ups[3]).astype(BF16)
        sx = p_ref[:, C_SX:C_SX + 768].astype(F32)
        for k in range(4):
            gsm_ref[R_SCW + k:R_SCW + k + 1, :] += _cs(sx * ups[3 - k])
        gsm_ref[R_SCB:R_SCB + 1, :] += _cs(dpre_c)
        h_dpre[...] = dpre_c[0:8, :]

        @pl.when(i == nt - 1)
        def _():
            ri, ci = _iota((256, 256), 0), _iota((256, 256), 1)
            per_head = jnp.where((ri >> 6) == ci, 1.0, 0.0).astype(BF16)
            per_dv = jnp.where((ri & 63) == ci, 1.0, 0.0).astype(BF16)
            row = _iota((8, 256), 0)
            top = gsm_ref[0:8, 0:256]
            sgc_ref[0:8, 0:256] = jnp.where(row == R_GNW, _dot3_l(top, per_dv), top)
            bot = gsm_ref[8:16, 0:256]
            fold = _dot3_l(jnp.where(row == R_AE - 8, bot * a_e, bot), per_head)
            sgc_ref[8:16, 0:256] = jnp.where((row == R_AE - 8) | (row == R_DE - 8), fold, bot)
            sgc_ref[0:16, 256:768] = gsm_ref[:, 256:768]
            sgc_ref[0:16, 768:896] = dgw_ref[0:16, :]
            sgc_ref[0:16, 896:1024] = jnp.zeros((16, 128), F32)
            diag = _pool_lane_select(lane, dpw_ref[0:64, :], dpw_ref[64:128, :], dpw_ref[128:192, :], dpw_ref[192:256, :])
            for q in range(4):
                sgc_ref[16:32, 256 * q:256 * q + 256] = diag[16 * q:16 * q + 16, :]

    return _call(
        body, (proj, tail, dxn, wot, mix, sg, ss, xc16, dxc16, cv16, pool16, prm, gw, pw, cmat, mask), grid=(nt,), name=name,
        sem=("arbitrary",), rider=rider,
        in_specs=[pl.BlockSpec((TB, NPM), lambda i: (rev(i), 0)),
                  pl.BlockSpec((TB, NP - NPM), lambda i: (rev(i), 0)),
                  pl.BlockSpec((TB, D), lambda i: (rev(i), 0)), pl.BlockSpec((D, D), lambda i: (0, 0)),
                  pl.BlockSpec((TB, D), lambda i: (rev(i), 0)),
                  pl.BlockSpec((NCH, 256, 128), lambda i: (rev(i), 0, 0)),
                  pl.BlockSpec((NCH, 128, 256), lambda i: (rev(i), 0, 0)),
                  pl.BlockSpec((TB, 768), lambda i: (rev(i), 0)), pl.BlockSpec((TB, 768), lambda i: (rev(i), 0)),
                  pl.BlockSpec((TB, 256), lambda i: (rev(i), 0)), pl.BlockSpec((TB, 256), lambda i: (rev(i), 0)),
                  pl.BlockSpec((16, 768), lambda i: (0, 0)), pl.BlockSpec((128, 128), lambda i: (0, 0)),
                  pl.BlockSpec((256, 256), lambda i: (0, 0)), pl.BlockSpec((4, 256, 256), lambda i: (0, 0, 0)),
                  pl.BlockSpec((256, 128), lambda i: (0, 0))],
        out_specs=[pl.BlockSpec((TB, NP), lambda i: (rev(i), 0)), pl.BlockSpec((32, 1024), lambda i: (0, 0)),
                   pl.BlockSpec((D, D), lambda i: (0, 0))],
        out_shape=[jax.ShapeDtypeStruct((t, NP), BF16), jax.ShapeDtypeStruct((32, 1024), F32),
                   jax.ShapeDtypeStruct((D, D), F32)],
        scratch_shapes=[pltpu.VMEM((256, 128), F32), pltpu.VMEM((128, 256), F32), pltpu.VMEM((8, 256), F32),
                        pltpu.VMEM((16, 256), F32), pltpu.VMEM((8, 768), F32), pltpu.VMEM((16, 768), F32),
                        pltpu.VMEM((128, 128), F32), pltpu.VMEM((256, 256), F32), pltpu.VMEM((TB, D), F32)])


SHARD = NPROJ // 4
SHARD_PAD = 896


def _ranges_to_perm(o, n):
    out, p = [], 0
    for start, size in _PERM:
        a, b = max(o, start), min(o + n, start + size)
        if a < b:
            out.append((a, b - a, p + a - start))
        p += size
    return out


def _ranges_to_orig(p0, n):
    out, p = [], 0
    for start, size in _PERM:
        a, b = max(p0, p), min(p0 + n, p + size)
        if a < b:
            out.append((a, b - a, start + a - p))
        p += size
    return out


def _lane_window(load, lo, n, d, lane):
    a = 128 * (lo // 128)
    off = lo - a
    w = 128 if off + n <= 128 else 256
    chunk = load(a, w)
    shift = (d - off) % w
    if shift:
        chunk = pltpu.roll(chunk, shift, axis=1)
    return jnp.where((lane >= d) & (lane < d + n), chunk[:, 0:128], 0.0)


def _assemble_w_in(slabs, name, rb=256):
    def body(s_ref, wp_ref, wpt_ref):
        lane = _iota((1, 128), 1)
        for b in range(NP // 128):
            acc = jnp.zeros((rb, 128), F32)
            for p, n, o in _ranges_to_orig(128 * b, 128):
                while n > 0:
                    s, lo = o // SHARD, o % SHARD
                    cnt = min(n, SHARD - lo)
                    acc = acc + _lane_window(lambda a, w, s=s: s_ref[s, :, a:a + w].astype(F32), lo, cnt, p - 128 * b, lane)
                    o, p, n = o + cnt, p + cnt, n - cnt
            wp_ref[:, 128 * b:128 * b + 128] = acc.astype(BF16)
            wpt_ref[128 * b:128 * b + 128, :] = acc.T.astype(BF16)

    return pl.pallas_call(
        body, grid=(D // rb,), name=name,
        in_specs=[pl.BlockSpec((4, rb, SHARD_PAD), lambda i: (0, i, 0))],
        out_specs=[pl.BlockSpec((rb, NP), lambda i: (i, 0)), pl.BlockSpec((NP, rb), lambda i: (0, i))],
        out_shape=[jax.ShapeDtypeStruct((D, NP), BF16), jax.ShapeDtypeStruct((NP, D), BF16)],
        compiler_params=_cparams(("parallel",)))(slabs)


def _split_dw_in(dwp, name, rb=256):
    rows = dwp.shape[0]

    def body(g_ref, o_ref):
        lane = _iota((1, 128), 1)
        for s in range(4):
            for k in range(SHARD_PAD // 128):
                acc = jnp.zeros((rb, 128), F32)
                n_valid = min(128, SHARD - 128 * k)
                for o, n, p in _ranges_to_perm(SHARD * s + 128 * k, n_valid):
                    acc = acc + _lane_window(lambda a, w: g_ref[:, a:a + w].astype(F32), p, n, o - SHARD * s - 128 * k, lane)
                o_ref[s, :, 128 * k:128 * k + 128] = acc.astype(o_ref.dtype)

    return pl.pallas_call(
        body, grid=(rows // rb,), name=name,
        in_specs=[pl.BlockSpec((rb, NP), lambda i: (i, 0))],
        out_specs=pl.BlockSpec((4, rb, SHARD_PAD), lambda i: (0, i, 0)),
        out_shape=jax.ShapeDtypeStruct((4, rows, SHARD_PAD), dwp.dtype),
        compiler_params=_cparams(("parallel",)))(dwp)


def _half(c, n):
    return pl.ds(pl.multiple_of(c * (n // 2), n // 2), n // 2)


def _other_chips(x, y):
    return ((1 - x, y), (x, 1 - y), (1 - x, 1 - y))


def _remote(src, dst, send, recv, k, dev):
    return pltpu.make_async_remote_copy(src_ref=src, dst_ref=dst, send_sem=send.at[k], recv_sem=recv.at[k], device_id=dev,
                                        device_id_type=MESH)


def _sem(n):
    return pltpu.SemaphoreType.DMA((n,))


def _rider_gather_ici(shards):
    shards = tuple(shards)
    n = len(shards)

    def copies(rins, routs, sems, arrivals=True):
        send, recv = sems
        x, y, c = _place()
        me = 2 * x + y
        out, inc = [], []
        for j, (px, py) in enumerate(_other_chips(x, y)):
            for k in range(n):
                rows = _half(c, shards[k].shape[0])
                out.append(_remote(rins[k].at[rows], routs[k].at[me, rows], send, recv, n * j + k, (px, py, c)))
                if arrivals:
                    inc.append(_remote(rins[k].at[rows], routs[k].at[2 * px + py, rows], send, recv, n * j + k, (px, py, c)))
        return out, inc

    def start(rins, routs, sems):
        for cp in copies(rins, routs, sems, arrivals=False)[0]:
            cp.start()

    def finish(rins, routs, sems):
        out, inc = copies(rins, routs, sems)
        for cp in inc:
            cp.wait_recv()
        for cp in out:
            cp.wait_send()

    return _Rider(shards, [jax.ShapeDtypeStruct((4,) + a.shape, a.dtype) for a in shards], [_sem(3 * n), _sem(3 * n)],
                  start, finish)


def _gather_ici_two_hops(shards, extra):
    shards = tuple(shards)
    n = len(shards)

    def body(*refs):
        ins, e_in, outs, e_out = refs[:n], refs[n], refs[n + 1:2 * n + 1], refs[2 * n + 1]
        send, recv = refs[2 * n + 2:]
        x, y, c = _place()
        slab = lambda px, py: 2 * px + py
        xn, yn, dg = (1 - x, y), (x, 1 - y), (1 - x, 1 - y)

        def part(k, q):
            r = shards[k].shape[0] // 4
            return pl.ds(pl.multiple_of(c * 2 * r + q * r, r), r)

        def hop(k, q, src_chip, to, sem):
            rows = part(k, q)
            src = ins[k].at[rows] if src_chip is None else outs[k].at[slab(*src_chip), rows]
            own = (x, y) if src_chip is None else src_chip
            return _remote(src, outs[k].at[slab(*own), rows], send, recv, sem, (*to, c))

        small = [_remote(e_in, e_out.at[slab(x, y)], send, recv, 6 * n + j, (*to, c)) for j, to in enumerate((xn, yn, dg))]
        first = [hop(k, q, None, (xn, yn)[q], 2 * k + q) for k in range(n) for q in (0, 1)]
        for cp in small + first:
            cp.start()
        for k in range(n):
            for q in (0, 1):
                nb = (xn, yn)[q]
                _remote(ins[k].at[part(k, q)], outs[k].at[slab(*nb), part(k, q)], send, recv, 2 * k + q, (*nb, c)).wait_recv()
        second = []
        for k in range(n):
            for q in (0, 1):
                to, via = (yn, xn)[q], (xn, yn)[q]
                second.append(hop(k, q, None, to, 2 * n + 4 * k + 2 * q))
                second.append(hop(k, q, via, to, 2 * n + 4 * k + 2 * q + 1))
        for cp in second:
            cp.start()
        for k in range(n):
            for q in (0, 1):
                frm, rows = (yn, xn)[q], part(k, q)
                for j, origin in enumerate((frm, dg)):
                    _remote(ins[k].at[rows], outs[k].at[slab(*origin), rows], send, recv, 2 * n + 4 * k + 2 * q + j,
                            (*frm, c)).wait_recv()
        for j, frm in enumerate((xn, yn, dg)):
            _remote(e_in, e_out.at[slab(*frm)], send, recv, 6 * n + j, (*frm, c)).wait_recv()
        third, theirs = [], []
        for k in range(n):
            rows = shards[k].shape[0]
            for j, chip in enumerate((xn, yn, dg)):
                got, missing = outs[k].at[slab(*chip), _half(c, rows)], outs[k].at[slab(*chip), _half(1 - c, rows)]
                third.append(_remote(got, got, send, recv, 6 * n + 3 + 3 * k + j, (x, y, 1 - c)))
                theirs.append(_remote(missing, missing, send, recv, 6 * n + 3 + 3 * k + j, (x, y, 1 - c)))
        for cp in third:
            cp.start()
        for cp in theirs:
            cp.wait_recv()
        for cp in small + first + second + third:
            cp.wait_send()

    outs = pl.pallas_call(
        body, name="gather0", in_specs=[_ANY] * (n + 1), out_specs=[_ANY] * (n + 1),
        out_shape=[jax.ShapeDtypeStruct((4,) + a.shape, a.dtype) for a in shards + (extra,)],
        scratch_shapes=[_sem(9 * n + 3), _sem(9 * n + 3)])(*shards, extra)
    return list(outs)


def _rider_gather_d2d(slabs):
    slabs = tuple(slabs)
    n = len(slabs)

    def copies(routs, sems, arrivals=True):
        send, recv = sems
        x, y, c = _place()
        out, inc = [], []
        for j, (px, py) in enumerate(_other_chips(x, y)):
            for k in range(n):
                rows = slabs[k].shape[1]
                mine, theirs = routs[k].at[2 * px + py, _half(c, rows)], routs[k].at[2 * px + py, _half(1 - c, rows)]
                out.append(_remote(mine, mine, send, recv, n * j + k, (x, y, 1 - c)))
                if arrivals:
                    inc.append(_remote(theirs, theirs, send, recv, n * j + k, (x, y, 1 - c)))
        return out, inc

    def start(rins, routs, sems):
        for cp in copies(routs, sems, arrivals=False)[0]:
            cp.start()

    def finish(rins, routs, sems):
        out, inc = copies(routs, sems)
        for cp in inc:
            cp.wait_recv()
        for cp in out:
            cp.wait_send()

    return _Rider(slabs, [jax.ShapeDtypeStruct(a.shape, a.dtype) for a in slabs], [_sem(3 * n), _sem(3 * n)], start, finish,
                  aliases={k: k for k in range(n)})


def _rider_swap(parts):
    parts = tuple(parts)
    n = len(parts)

    def copies(rins, routs, sems):
        send, recv = sems
        x, y, c = _place()
        return [_remote(rins[k].at[:, _half(1 - c, parts[k].shape[1])], routs[k], send, recv, k, (x, y, 1 - c))
                for k in range(n)]

    def start(rins, routs, sems):
        for cp in copies(rins, routs, sems):
            cp.start()

    def finish(rins, routs, sems):
        for cp in copies(rins, routs, sems):
            cp.wait()

    return _Rider(parts, [jax.ShapeDtypeStruct((a.shape[0], a.shape[1] // 2, a.shape[2]), a.dtype) for a in parts],
                  [_sem(n), _sem(n)], start, finish)


def _rider_scatter(parts):
    parts = tuple(parts)
    n = len(parts)

    def copies(rins, routs, sems, arrivals=True):
        send, recv = sems
        x, y, c = _place()
        me = 2 * x + y
        out, inc = [], []
        for j, (px, py) in enumerate(_other_chips(x, y)):
            for k in range(n):
                out.append(_remote(rins[k].at[2 * px + py], routs[k].at[me], send, recv, n * j + k, (px, py, c)))
                if arrivals:
                    inc.append(_remote(rins[k].at[me], routs[k].at[2 * px + py], send, recv, n * j + k, (px, py, c)))
        return out, inc

    def start(rins, routs, sems):
        for cp in copies(rins, routs, sems, arrivals=False)[0]:
            cp.start()

    def finish(rins, routs, sems):
        out, inc = copies(rins, routs, sems)
        for cp in inc:
            cp.wait_recv()
        for cp in out:
            cp.wait_send()

    return _Rider(parts, [jax.ShapeDtypeStruct(a.shape, a.dtype) for a in parts], [_sem(3 * n), _sem(3 * n)], start, finish)


def _rider_share(fulls):
    fulls = tuple(fulls)
    n = len(fulls)

    def copies(routs, sems, arrivals=True):
        send, recv = sems
        x, y, c = _place()
        out, inc = [], []
        for k in range(n):
            mine, theirs = routs[k].at[_half(c, fulls[k].shape[0])], routs[k].at[_half(1 - c, fulls[k].shape[0])]
            out.append(_remote(mine, mine, send, recv, k, (x, y, 1 - c)))
            if arrivals:
                inc.append(_remote(theirs, theirs, send, recv, k, (x, y, 1 - c)))
        return out, inc

    def start(rins, routs, sems):
        for cp in copies(routs, sems, arrivals=False)[0]:
            cp.start()

    def finish(rins, routs, sems):
        out, inc = copies(routs, sems)
        for cp in inc:
            cp.wait_recv()
        for cp in out:
            cp.wait_send()

    return _Rider(fulls, [jax.ShapeDtypeStruct(a.shape, a.dtype) for a in fulls], [_sem(n), _sem(n)], start, finish,
                  aliases={k: k for k in range(n)})


def _pair_sum(core, full, recv, name, br=128):
    n, rows, cols = recv.shape

    def body(c_ref, a_ref, b_ref, o_ref):
        o_ref[...] = (a_ref[...] + b_ref[...]).astype(BF16)

    nb = rows // br
    return pl.pallas_call(
        body, name=name, out_shape=jax.ShapeDtypeStruct(recv.shape, BF16),
        grid_spec=pltpu.PrefetchScalarGridSpec(
            num_scalar_prefetch=1, grid=(n, nb),
            in_specs=[pl.BlockSpec((1, br, cols), lambda i, j, c: (i, c[0] * nb + j, 0)),
                      pl.BlockSpec((1, br, cols), lambda i, j, c: (i, j, 0))],
            out_specs=pl.BlockSpec((1, br, cols), lambda i, j, c: (i, j, 0))),
        compiler_params=_cparams(("parallel", "parallel")))(core, full, recv)


def _chip_sum(place, gathered, mine, name, br=128):
    _, r, c = gathered.shape
    nb = r // br

    def body(p_ref, g_ref, m_ref, o_ref):
        slab = lambda j: jnp.where(p_ref[1] == j, m_ref[j], g_ref[j]).astype(F32)
        o_ref[...] = ((slab(0) + slab(1)) + slab(2)) + slab(3)

    return pl.pallas_call(
        body, name=name, out_shape=jax.ShapeDtypeStruct((2 * r, c), F32),
        grid_spec=pltpu.PrefetchScalarGridSpec(
            num_scalar_prefetch=1, grid=(nb,),
            in_specs=[pl.BlockSpec((4, br, c), lambda i, p: (0, i, 0)), pl.BlockSpec((4, br, c), lambda i, p: (0, i, 0))],
            out_specs=pl.BlockSpec((br, c), lambda i, p: (p[0] * nb + i, 0))),
        compiler_params=_cparams(("parallel",)))(place, gathered, mine)


def _adamw(w, g, m, v, name, br):
    n, r, c = w.shape

    def body(w_ref, g_ref, m_ref, v_ref, d_ref, m2_ref, v2_ref):
        d_ref[...], m2_ref[...], v2_ref[...] = _adam_math(w_ref[...], g_ref[...], m_ref[...], v_ref[...])

    spec = pl.BlockSpec((1, br, c), lambda i, j: (i, j, 0))
    shp = jax.ShapeDtypeStruct(w.shape, F32)
    return pl.pallas_call(body, grid=(n, r // br), name=name, in_specs=[spec] * 4, out_specs=[spec] * 3,
                          out_shape=[shp] * 3, compiler_params=_cparams(("parallel", "parallel")))(w, g, m, v)


def _adamw_w_in(w, g, m, v, name, bc=93):
    cols = w.shape[2]
    lead = lambda a: jnp.transpose(a, (2, 0, 1))
    g = jnp.stack([a[:, 0:cols] for a in g])

    def body(w_ref, g_ref, m_ref, v_ref, go_ref, d_ref, m2_ref, v2_ref):
        for l in range(2):
            gv = g_ref[:, l, :]
            d_ref[:, l, :], m2_ref[:, l, :], v2_ref[:, l, :] = _adam_math(w_ref[:, l, :], gv, m_ref[:, l, :], v_ref[:, l, :])
            go_ref[:, l, :] = gv

    spec = pl.BlockSpec((bc, 2, D), lambda i: (i, 0, 0))
    outs = pl.pallas_call(body, grid=(cols // bc,), name=name, in_specs=[spec] * 4, out_specs=[spec] * 4,
                          out_shape=[jax.ShapeDtypeStruct((cols, 2, D), F32)] * 4,
                          compiler_params=_cparams(("parallel",)))(lead(w), lead(g), lead(m), lead(v))
    return [jnp.transpose(o, (1, 2, 0)) for o in outs]


_SMALL_NAMES = ("norm_w", "conv_a_w", "gla_gate_w", "gla_gate_b", "gla_norm_w", "pool_w", "pool_scale", "ssd_conv_w",
                "ssd_conv_b", "ssd_dt_bias", "ssd_a_log", "ssd_d", "ssd_norm_w", "final_norm_w")
SMALL_ROWS = 80


def _adam_math(w, g, m, v):
    m2 = ADAM_B1 * m + (1.0 - ADAM_B1) * g
    v2 = ADAM_B2 * v + (1.0 - ADAM_B2) * (g * g)
    m_hat = m2 / (1.0 - ADAM_B1 ** ADAM_STEP)
    v_hat = v2 / (1.0 - ADAM_B2 ** ADAM_STEP)
    return -ADAM_LR * (m_hat / (jnp.sqrt(v_hat) + ADAM_EPS) + ADAM_WD * w), m2, v2


def _small_slices(name, chip):
    if name == "conv_a_w":
        return [((), slice(R_CAW, R_CAW + 3), slice(64 * chip, 64 * chip + 64))]
    if name == "ssd_conv_w":
        return [((), slice(R_SCW, R_SCW + 4), slice(192 * chip, 192 * chip + 192))]
    if name == "gla_gate_w":
        return [((), slice(0, 16), slice(768, 896))]
    if name == "pool_w":
        return [((g, slice(16 * q, 16 * q + 16)), slice(16, 32), slice(256 * q + 64 * g, 256 * q + 64 * g + 64))
                for g in range(4) for q in range(4)]
    row, lanes = {"gla_gate_b": (R_GB, slice(0, 128)), "gla_norm_w": (R_GNW, slice(0, 64)),
                  "pool_scale": (R_PSC, slice(0, 256)), "ssd_conv_b": (R_SCB, slice(0, 768)),
                  "ssd_dt_bias": (R_DTB, slice(16, 20)), "ssd_a_log": (R_AE, slice(0, 4)), "ssd_d": (R_DE, slice(0, 4)),
                  "ssd_norm_w": (R_SNW, slice(0, 256))}[name]
    return [((), slice(row, row + 1), lanes)]


def _rider_exchange(block):
    def copies(rins, routs, sems):
        send, recv = sems
        x, y, c = _place()
        flip = lambda v, bit: 1 - v if bit else v
        return [_remote(rins[0], routs[0].at[k], send, recv, k - 1, (flip(x, k & 4), flip(y, k & 2), flip(c, k & 1)))
                for k in range(1, 8)]

    def start(rins, routs, sems):
        for cp in copies(rins, routs, sems):
            cp.start()

    def finish(rins, routs, sems):
        for cp in copies(rins, routs, sems):
            cp.wait()

    return _Rider((block,), [jax.ShapeDtypeStruct((8,) + block.shape, block.dtype)], [_sem(7), _sem(7)], start, finish)


def _join_riders(a, b):
    na, oa, sa = len(a.inputs), len(a.out_shapes), len(a.sems)

    def start(rins, routs, sems):
        a.start(rins[:na], routs[:oa], sems[:sa])
        b.start(rins[na:], routs[oa:], sems[sa:])

    def finish(rins, routs, sems):
        a.finish(rins[:na], routs[:oa], sems[:sa])
        b.finish(rins[na:], routs[oa:], sems[sa:])

    aliases = {**a.aliases, **{na + k: oa + v for k, v in b.aliases.items()}}
    return _Rider(a.inputs + b.inputs, a.out_shapes + b.out_shapes, a.sems + b.sems, start, finish, aliases)


def _small_adamw(blocks, w, m, v):
    n = len(_SMALL_NAMES)

    def body(*refs):
        (own, ex), (own0, ex0) = refs[0:2], refs[2:4]
        refs = refs[3:]
        w_refs, m_refs, v_refs = refs[1:1 + n], refs[1 + n:1 + 2 * n], refs[1 + 2 * n:1 + 3 * n]
        o = 1 + 3 * n
        g_out, d_out, m_out, v_out = refs[o:o + n], refs[o + n:o + 2 * n], refs[o + 2 * n:o + 3 * n], refs[o + 3 * n:o + 4 * n]
        loss_ref, acc, acc0 = refs[o + 4 * n:o + 4 * n + 3]
        chip = 2 * lax.axis_index("x") + lax.axis_index("y")
        me = 2 * chip + lax.axis_index("c")
        acc[...] = jnp.zeros_like(acc)
        acc0[...] = jnp.zeros_like(acc0)
        for src in range(8):
            @pl.when(me == src)
            def _():
                acc[...] += own[...]
                acc0[...] += own0[...]

            @pl.when(me != src)
            def _(src=src):
                acc[...] += ex[jnp.bitwise_xor(me, src)]
                acc0[...] += ex0[jnp.bitwise_xor(me, src)]

        loss_ref[...] = acc[73:74, 0:1]

        def update(i, idx, g):
            d, m2, v2 = _adam_math(w_refs[i][idx], g, m_refs[i][idx], v_refs[i][idx])
            g_out[i][idx], d_out[i][idx], m_out[i][idx], v_out[i][idx] = g, d, m2, v2

        for i, name in enumerate(_SMALL_NAMES):
            if name == "final_norm_w":
                update(i, (slice(0, 1), slice(None)), acc[72:73, :])
            elif name == "norm_w":
                update(i, (slice(0, 1), slice(None)), acc0[0:1, :])
                update(i, (slice(1, 2), slice(None)), acc[64:65, :])
            elif name in ("conv_a_w", "ssd_conv_w"):
                for s in range(4):
                    @pl.when(chip == s)
                    def _(i=i, name=name, s=s):
                        for l in range(2):
                            (_, rows, lanes), = _small_slices(name, s)
                            update(i, (l,), acc[rows.start + 32 * l:rows.stop + 32 * l, lanes])
            else:
                for l in range(2):
                    for idx, rows, lanes in _small_slices(name, 0):
                        g = acc[rows.start + 32 * l:rows.stop + 32 * l, lanes]
                        if w_refs[i].ndim == 2:
                            update(i, (slice(l, l + 1), slice(None)), g)
                        else:
                            update(i, (l,) + idx, g)

    args = [a for pair in blocks for a in pair] + [d[k] for d in (w, m, v) for k in _SMALL_NAMES]
    shapes = [jax.ShapeDtypeStruct(w[k].shape, F32) for k in _SMALL_NAMES]
    vmem = pl.BlockSpec(memory_space=pltpu.VMEM)
    outs = pl.pallas_call(body, name="small_adamw", in_specs=[vmem] * len(args), out_specs=[vmem] * (4 * n + 1),
                          out_shape=shapes * 4 + [jax.ShapeDtypeStruct((1, 1), F32)],
                          scratch_shapes=[pltpu.VMEM((SMALL_ROWS, D), F32), pltpu.VMEM((8, D), F32)])(*args)
    return outs[0:n], outs[n:2 * n], outs[2 * n:3 * n], outs[3 * n:4 * n], outs[4 * n]


def _mixer_consts(layer, conv_a_w, gla_gate_w, gla_gate_b, gla_norm_w, pool_w, pool_scale, ssd_conv_w, ssd_conv_b,
                  ssd_dt_bias, ssd_a_log, ssd_d, ssd_norm_w):
    def row(v):
        return jnp.pad(v.reshape(1, -1), ((0, 0), (0, 768 - v.size)))

    dtb = jnp.pad(ssd_dt_bias[layer], (16, 108))
    rows = [jnp.pad(conv_a_w[layer], ((0, 0), (0, 512))), row(gla_gate_b[layer]), row(jnp.tile(gla_norm_w[layer], 4)),
            row(pool_scale[layer]), row(ssd_conv_b[layer]), row(dtb), row(jnp.repeat(-jnp.exp(ssd_a_log[layer]), 64)),
            row(jnp.repeat(ssd_d[layer], 64)), row(ssd_norm_w[layer]), jnp.zeros((1, 768), F32), ssd_conv_w[layer]]
    prm = jnp.concatenate(rows, axis=0)
    gw = jnp.pad(gla_gate_w[layer], ((0, 112), (0, 0))).astype(BF16)
    on_diag = (_iota((256, 256), 0) >> 6) == (_iota((256, 256), 1) >> 6)
    pw = jnp.where(on_diag, jnp.tile(pool_w[layer].reshape(256, 64), (1, 4)), 0.0)
    return (prm, gw, pw.astype(BF16)) + _mixer_matrices()


def _grad_slabs(dwp, dwo):
    return dwp.reshape(1, D, NP), dwo.reshape(4, D // 4, D)


class _Comm:
    def __init__(self, w_in, w_out):
        self.w_in16 = jnp.pad(w_in.astype(BF16), ((0, 0), (0, 0), (0, SHARD_PAD - SHARD)))
        self.w_out16 = w_out.astype(BF16)
        self.core = lax.axis_index("c").astype(jnp.int32).reshape(1)
        self.chip = 2 * lax.axis_index("x") + lax.axis_index("y")
        self.place = jnp.stack([lax.axis_index("c"), self.chip]).astype(jnp.int32)

    def gather_ici(self, layer):
        return _rider_gather_ici((self.w_in16[layer], self.w_out16[layer]))

    def pair_sum(self, layer, slabs, received):
        d_in, d_out = [_pair_sum(self.core, a, b, name=f"reduce_pair_sum{layer}_{k}")
                       for k, (a, b) in enumerate(zip(slabs, received))]
        return [_split_dw_in(d_in[0], name=f"split_dw_in{layer}"), d_out]

    def chip_sum(self, layer, gathered, mine):
        return [_chip_sum(self.place, a, b, name=f"reduce_chip_sum{layer}_{k}") for k, (a, b) in enumerate(zip(gathered, mine))]

    def layer_weights(self, layer, s_in, s_out):
        own = lambda slabs, shard: jnp.stack([jnp.where(self.chip == s, shard, slabs[s]) for s in range(4)])
        wp, wpt = _assemble_w_in(own(s_in, self.w_in16[layer]), name=f"assemble_w_in{layer}")
        wo = own(s_out, self.w_out16[layer]).reshape(D, D)
        return wp, wpt, wo, wo.T


def _local_step(x, tgt, norm_w, final_norm_w, consts, wts0, wts1=None, comm=None):
    nw = [norm_w[l:l + 1] for l in range(2)]
    proj0, h0, slabs = _rmsproj(x, nw[0], wts0[0], name="rmsproj0", rider=comm and comm.gather_ici(1))
    (mix0, sg0, ss0, x1, *conv0), slabs = _mixer_fwd(proj0, x, wts0[2], *consts[0], name="mixer_fwd0",
                                                     rider=comm and _rider_gather_d2d(slabs))
    if comm:
        wts1 = comm.layer_weights(1, *slabs)
    proj1, h1, _ = _rmsproj(x1, nw[1], wts1[0], name="rmsproj1")
    (mix1, sg1, ss1, dx, *conv1, head), _ = _mixer_fwd(proj1, x1, wts1[2], *consts[1], name="mixer_fwd1",
                                                       head=(tgt, final_norm_w.reshape(1, D)))
    (dproj, mgr1, dwo1), _ = _mixer_bwd(proj1, dx, wts1[3], mix1, sg1, ss1, *conv1, *consts[1], name="mixer_bwd1")
    dwp1, _ = _dwin(h1, dproj, name="dwin1")
    slabs1 = comm and _grad_slabs(dwp1, dwo1)
    (dx, dnw1), recv = _dxin(dproj, wts1[1], x1, dx, nw[1], name="dxin1", rider=comm and _rider_swap(slabs1))
    pairs1 = comm and comm.pair_sum(1, slabs1, recv)
    (dproj, mgr0, dwo0), gathered = _mixer_bwd(proj0, dx, wts0[3], mix0, sg0, ss0, *conv0, *consts[0], name="mixer_bwd0",
                                               rider=comm and _rider_scatter(pairs1))
    if not comm:
        dwp0, _ = _dwin(h0, dproj, name="dwin0")
        (dx, dnw0), _ = _dxin(dproj, wts0[1], x, dx, nw[0], name="dxin0")
        return head, dx, ((dwp0, dwp1), (dwo0, dwo1)), (dnw0, dnw1), (mgr0, mgr1)
    dwo0 = dwo0.reshape(4, D // 4, D)
    small = jnp.concatenate([mgr0, mgr1, dnw1, head], axis=0)
    dwp0, (*big1, recv_out, got_small) = _dwin(h0, dproj, name="dwin0", rider=_join_riders(_join_riders(
        _rider_share(comm.chip_sum(1, gathered, pairs1)), _rider_swap((dwo0,))), _rider_exchange(small)))
    slabs0 = (dwp0.reshape(1, D, NP), dwo0)
    pairs0 = comm.pair_sum(0, slabs0, (_run_rider(_rider_swap(slabs0[0:1]), "reduce_swap0")[0], recv_out))
    (dx, dnw0), gathered = _dxin(dproj, wts0[1], x, dx, nw[0], name="dxin0", rider=_rider_scatter(pairs0))
    last = _run_rider(_join_riders(_rider_share(comm.chip_sum(0, gathered, pairs0)), _rider_exchange(dnw0)),
                      "reduce_share0")
    return dx, ((last[0], big1[0]), (last[1], big1[1])), ((small, got_small), (dnw0, last[2]))


def kernel(x, norm_w, w_in, conv_a_w, gla_gate_w, gla_gate_b, gla_norm_w, pool_w, pool_scale, ssd_conv_w, ssd_conv_b, ssd_dt_bias, ssd_a_log, ssd_d, ssd_norm_w, w_out, final_norm_w, loss_target, m_norm_w, m_w_in, m_conv_a_w, m_gla_gate_w, m_gla_gate_b, m_gla_norm_w, m_pool_w, m_pool_scale, m_ssd_conv_w, m_ssd_conv_b, m_ssd_dt_bias, m_ssd_a_log, m_ssd_d, m_ssd_norm_w, m_w_out, m_final_norm_w, v_norm_w, v_w_in, v_conv_a_w, v_gla_gate_w, v_gla_gate_b, v_gla_norm_w, v_pool_w, v_pool_scale, v_ssd_conv_w, v_ssd_conv_b, v_ssd_dt_bias, v_ssd_a_log, v_ssd_d, v_ssd_norm_w, v_w_out, v_final_norm_w):
    weights = dict(norm_w=norm_w, w_in=w_in, conv_a_w=conv_a_w, gla_gate_w=gla_gate_w, gla_gate_b=gla_gate_b,
                   gla_norm_w=gla_norm_w, pool_w=pool_w, pool_scale=pool_scale, ssd_conv_w=ssd_conv_w,
                   ssd_conv_b=ssd_conv_b, ssd_dt_bias=ssd_dt_bias, ssd_a_log=ssd_a_log, ssd_d=ssd_d,
                   ssd_norm_w=ssd_norm_w, w_out=w_out, final_norm_w=final_norm_w)
    m_in = dict(norm_w=m_norm_w, w_in=m_w_in, conv_a_w=m_conv_a_w, gla_gate_w=m_gla_gate_w, gla_gate_b=m_gla_gate_b,
                gla_norm_w=m_gla_norm_w, pool_w=m_pool_w, pool_scale=m_pool_scale, ssd_conv_w=m_ssd_conv_w,
                ssd_conv_b=m_ssd_conv_b, ssd_dt_bias=m_ssd_dt_bias, ssd_a_log=m_ssd_a_log, ssd_d=m_ssd_d,
                ssd_norm_w=m_ssd_norm_w, w_out=m_w_out, final_norm_w=m_final_norm_w)
    v_in = dict(norm_w=v_norm_w, w_in=v_w_in, conv_a_w=v_conv_a_w, gla_gate_w=v_gla_gate_w, gla_gate_b=v_gla_gate_b,
                gla_norm_w=v_gla_norm_w, pool_w=v_pool_w, pool_scale=v_pool_scale, ssd_conv_w=v_ssd_conv_w,
                ssd_conv_b=v_ssd_conv_b, ssd_dt_bias=v_ssd_dt_bias, ssd_a_log=v_ssd_a_log, ssd_d=v_ssd_d,
                ssd_norm_w=v_ssd_norm_w, w_out=v_w_out, final_norm_w=v_final_norm_w)
    order = ("norm_w", "w_in", "conv_a_w", "gla_gate_w", "gla_gate_b", "gla_norm_w", "pool_w", "pool_scale",
             "ssd_conv_w", "ssd_conv_b", "ssd_dt_bias", "ssd_a_log", "ssd_d", "ssd_norm_w", "w_out", "final_norm_w")
    t = x.shape[1]

    comm = _Comm(w_in, w_out)
    cshard = jnp.zeros((16, 256), F32)
    for l in range(2):
        cshard = cshard.at[8 * l:8 * l + 3, 0:64].set(conv_a_w[l]).at[8 * l + 3:8 * l + 7, 0:192].set(ssd_conv_w[l])
    s_in, s_out, g_c = _gather_ici_two_hops((comm.w_in16[0], comm.w_out16[0]), cshard)
    g_c = [jnp.where(comm.chip == s, cshard, g_c[s]) for s in range(4)]
    conv_a_full = jnp.stack([jnp.concatenate([g_c[s][8 * l:8 * l + 3, 0:64] for s in range(4)], axis=-1) for l in range(2)])
    ssd_conv_full = jnp.stack([jnp.concatenate([g_c[s][8 * l + 3:8 * l + 7, 0:192] for s in range(4)], axis=-1)
                               for l in range(2)])
    consts = [_mixer_consts(l, conv_a_full, gla_gate_w, gla_gate_b, gla_norm_w, pool_w, pool_scale, ssd_conv_full,
                            ssd_conv_b, ssd_dt_bias, ssd_a_log, ssd_d, ssd_norm_w) for l in range(2)]

    dx, big, blocks = _local_step(x.reshape(t, D), loss_target.reshape(t, D), norm_w, final_norm_w, consts,
                                  comm.layer_weights(0, s_in, s_out), comm=comm)

    as2d = lambda d: {k: (d[k].reshape(1, D) if k == "final_norm_w" else d[k]) for k in _SMALL_NAMES}
    small = _small_adamw(blocks, as2d(weights), as2d(m_in), as2d(v_in))
    grads, delta, new_m, new_v = ({k: (a.reshape(D) if k == "final_norm_w" else a) for k, a in zip(_SMALL_NAMES, part)}
                                  for part in small[0:4])
    loss = small[4].reshape(())

    grads["w_out"] = jnp.stack(big[1])

    grads["w_in"], delta["w_in"], new_m["w_in"], new_v["w_in"] = _adamw_w_in(w_in, big[0], m_w_in, v_w_in, name="adamw_w_in")
    delta["w_out"], new_m["w_out"], new_v["w_out"] = _adamw(w_out, grads["w_out"], m_w_out, v_w_out, name="adamw_w_out", br=256)

    return (loss, dx.reshape(1, t, D), *[grads[k] for k in order], *[delta[k] for k in order],
            *[new_m[k] for k in order], *[new_v[k] for k in order])
```
